```python
import math
import jax, jax.numpy as jnp
from jax import lax
import numpy as np

D_MODEL = 1024
BATCH = 8
SEQ = 8192
DEPTH = 1

N_META = 16
MIX_WIDTH = D_MODEL
SSM_WIDTH = MIX_WIDTH // 2
POOL_WIDTH = MIX_WIDTH - SSM_WIDTH
SSM_GROUP_CH = 16
SSM_GROUPS = SSM_WIDTH // SSM_GROUP_CH
SSM_STATE = 64
DT_MIN = 1e-3
DT_MAX = 1e-1
POOL_WINDOWS = (2, 4, 8, 16)
POOL_GROUPS = len(POOL_WINDOWS)
POOL_GROUP = POOL_WIDTH // POOL_GROUPS
D_FF = ((8 * D_MODEL // 3 + 127) // 128) * 128
RMS_EPS = 1e-6

kernel_name = "hymba_s5_poolformer_macaron_layer"


def rms_norm(x, g):
    xf = x.astype(jnp.float32)
    y = xf * lax.rsqrt(jnp.mean(xf * xf, axis=-1, keepdims=True) + RMS_EPS)
    return (y * g.astype(jnp.float32)).astype(x.dtype)


def swiglu(h, w_gate, w_up, w_down):
    return (jax.nn.silu(h @ w_gate) * (h @ w_up)) @ w_down


def _complex_scan_combine(e1, e2):
    a1r, a1i, b1r, b1i = e1
    a2r, a2i, b2r, b2i = e2
    ar = a2r * a1r - a2i * a1i
    ai = a2r * a1i + a2i * a1r
    a2r_b = a2r[:, None]
    a2i_b = a2i[:, None]
    br = a2r_b * b1r - a2i_b * b1i + b2r
    bi = a2r_b * b1i + a2i_b * b1r + b2i
    return (ar, ai, br, bi)


def s5_mixer(u, lam_re, lam_im, log_dt, b_re, b_im, c_re, c_im, d_skip, w_glu):
    Bt, L, _ = u.shape
    ug = u.astype(jnp.float32).reshape(Bt, L, SSM_GROUPS, SSM_GROUP_CH)
    lr = lam_re.astype(jnp.float32)
    li = lam_im.astype(jnp.float32)
    dt = jnp.exp(log_dt.astype(jnp.float32))[:, None]
    decay = jnp.exp(lr * dt)
    ang = li * dt
    a_re = decay * jnp.cos(ang)
    a_im = decay * jnp.sin(ang)
    nr = a_re - 1.0
    den = lr * lr + li * li
    q_re = (nr * lr + a_im * li) / den
    q_im = (a_im * lr - nr * li) / den
    br = b_re.astype(jnp.float32)
    bi = b_im.astype(jnp.float32)
    bb_re = q_re[..., None] * br - q_im[..., None] * bi
    bb_im = q_re[..., None] * bi + q_im[..., None] * br
    bu_re = jnp.einsum('blgh,gnh->lbgn', ug, bb_re)
    bu_im = jnp.einsum('blgh,gnh->lbgn', ug, bb_im)
    A_re = jnp.broadcast_to(a_re, (L,) + a_re.shape)
    A_im = jnp.broadcast_to(a_im, (L,) + a_im.shape)
    _, _, x_re, x_im = lax.associative_scan(_complex_scan_combine, (A_re, A_im, bu_re, bu_im), axis=0)
    y = (jnp.einsum('lbgn,ghn->blgh', x_re, c_re.astype(jnp.float32))
         - jnp.einsum('lbgn,ghn->blgh', x_im, c_im.astype(jnp.float32)))
    y = y + d_skip.astype(jnp.float32).reshape(SSM_GROUPS, SSM_GROUP_CH) * ug
    z = jnp.einsum('blgh,ghk->blgk', jax.nn.gelu(y), w_glu.astype(jnp.float32))
    out = z[..., :SSM_GROUP_CH] * jax.nn.sigmoid(z[..., SSM_GROUP_CH:])
    return out.reshape(Bt, L, SSM_WIDTH).astype(u.dtype)


def pool_mixer(u, pool_w, pool_scale):
    Bt, L, _ = u.shape
    uf = u.astype(jnp.float32)
    cs = jnp.cumsum(uf, axis=1)
    cs = jnp.concatenate([jnp.zeros((Bt, 1, POOL_WIDTH), jnp.float32), cs], axis=1)
    t1 = jnp.arange(1, L + 1, dtype=jnp.float32)
    outs = []
    for gi, w in enumerate(POOL_WINDOWS):
        lo_c, hi_c = gi * POOL_GROUP, (gi + 1) * POOL_GROUP
        c = cs[:, :, lo_c:hi_c]
        hi = c[:, 1:]
        lo = jnp.concatenate([jnp.zeros((Bt, w - 1, POOL_GROUP), jnp.float32), c[:, :L - w + 1]], axis=1)
        cnt = jnp.minimum(t1, float(w))[None, :, None]
        outs.append((hi - lo) / cnt - uf[:, :, lo_c:hi_c])
    pooled = jnp.stack(outs, axis=2)
    mixed = jnp.einsum('blgc,gcd->blgd', pooled, pool_w.astype(jnp.float32))
    mixed = mixed * pool_scale.astype(jnp.float32).reshape(POOL_GROUPS, POOL_GROUP)
    return mixed.reshape(Bt, L, POOL_WIDTH).astype(u.dtype)


def _fwd_setup_inputs(seed: int = 0) -> dict:
    key = jax.random.key(seed)
    ks = jax.random.split(key, 40)
    f32 = jnp.float32

    def nrm(k, shape, scale):
        return jax.random.normal(k, shape, f32) * scale

    def gain(k, shape):
        return 1.0 + 0.02 * jax.random.normal(k, shape, f32)

    Dp = DEPTH
    G, N, H = SSM_GROUPS, SSM_STATE, SSM_GROUP_CH
    n_idx = jnp.arange(N, dtype=f32)
    return {
        "x": nrm(ks[0], (BATCH, SEQ, D_MODEL), 1.0),
        "meta_tokens": nrm(ks[1], (N_META, D_MODEL), 1.0),
        "ffn1_pre_norm": gain(ks[2], (Dp, D_MODEL)),
        "ffn1_post_norm": gain(ks[3], (Dp, D_MODEL)),
        "ffn1_w_gate": nrm(ks[4], (Dp, D_MODEL, D_FF), D_MODEL ** -0.5),
        "ffn1_w_up": nrm(ks[5], (Dp, D_MODEL, D_FF), D_MODEL ** -0.5),
        "ffn1_w_down": nrm(ks[6], (Dp, D_FF, D_MODEL), D_FF ** -0.5),
        "mix_pre_norm": gain(ks[7], (Dp, D_MODEL)),
        "mix_post_norm": gain(ks[8], (Dp, D_MODEL)),
        "w_in": nrm(ks[9], (Dp, D_MODEL, MIX_WIDTH), D_MODEL ** -0.5),
        "ssm_lambda_re": -0.5 + 0.01 * jax.random.normal(ks[10], (Dp, G, N), f32),
        "ssm_lambda_im": math.pi * n_idx + 0.01 * jax.random.normal(ks[11], (Dp, G, N), f32),
        "ssm_log_dt": jax.random.uniform(ks[12], (Dp, G), f32, math.log(DT_MIN), math.log(DT_MAX)),
        "ssm_b_re": nrm(ks[13], (Dp, G, N, H), (2.0 * H) ** -0.5),
        "ssm_b_im": nrm(ks[14], (Dp, G, N, H), (2.0 * H) ** -0.5),
        "ssm_c_re": nrm(ks[15], (Dp, G, H, N), (2.0 * N) ** -0.5),
        "ssm_c_im": nrm(ks[16], (Dp, G, H, N), (2.0 * N) ** -0.5),
        "ssm_d": nrm(ks[17], (Dp, SSM_WIDTH), 1.0),
        "ssm_w_glu": nrm(ks[18], (Dp, G, H, 2 * H), H ** -0.5),
        "pool_w": nrm(ks[19], (Dp, POOL_GROUPS, POOL_GROUP, POOL_GROUP), POOL_GROUP ** -0.5),
        "pool_scale": gain(ks[20], (Dp, POOL_WIDTH)),
        "ssm_out_norm": gain(ks[21], (Dp, SSM_WIDTH)),
        "pool_out_norm": gain(ks[22], (Dp, POOL_WIDTH)),
        "w_out": nrm(ks[23], (Dp, MIX_WIDTH, D_MODEL), MIX_WIDTH ** -0.5),
        "ffn2_pre_norm": gain(ks[24], (Dp, D_MODEL)),
        "ffn2_post_norm": gain(ks[25], (Dp, D_MODEL)),
        "ffn2_w_gate": nrm(ks[26], (Dp, D_MODEL, D_FF), D_MODEL ** -0.5),
        "ffn2_w_up": nrm(ks[27], (Dp, D_MODEL, D_FF), D_MODEL ** -0.5),
        "ffn2_w_down": nrm(ks[28], (Dp, D_FF, D_MODEL), D_FF ** -0.5),
    }


def _fwd_reference(x, meta_tokens, ffn1_pre_norm, ffn1_post_norm, ffn1_w_gate, ffn1_w_up, ffn1_w_down,
              mix_pre_norm, mix_post_norm, w_in, ssm_lambda_re, ssm_lambda_im, ssm_log_dt,
              ssm_b_re, ssm_b_im, ssm_c_re, ssm_c_im, ssm_d, ssm_w_glu, pool_w, pool_scale,
              ssm_out_norm, pool_out_norm, w_out, ffn2_pre_norm, ffn2_post_norm,
              ffn2_w_gate, ffn2_w_up, ffn2_w_down):
    Bt = x.shape[0]
    meta = jnp.broadcast_to(meta_tokens.astype(x.dtype)[None], (Bt, N_META, D_MODEL))
    h = jnp.concatenate([meta, x], axis=1)
    for i in range(DEPTH):
        f = swiglu(rms_norm(h, ffn1_pre_norm[i]), ffn1_w_gate[i], ffn1_w_up[i], ffn1_w_down[i])
        h = h + 0.5 * rms_norm(f, ffn1_post_norm[i])
        proj = rms_norm(h, mix_pre_norm[i]) @ w_in[i]
        y_ssm = s5_mixer(proj[..., :SSM_WIDTH], ssm_lambda_re[i], ssm_lambda_im[i], ssm_log_dt[i],
                         ssm_b_re[i], ssm_b_im[i], ssm_c_re[i], ssm_c_im[i], ssm_d[i], ssm_w_glu[i])
        y_pool = pool_mixer(proj[..., SSM_WIDTH:], pool_w[i], pool_scale[i])
        mixed = jnp.concatenate([rms_norm(y_ssm, ssm_out_norm[i]),
                                 rms_norm(y_pool, pool_out_norm[i])], axis=-1) @ w_out[i]
        h = h + rms_norm(mixed, mix_post_norm[i])
        f = swiglu(rms_norm(h, ffn2_pre_norm[i]), ffn2_w_gate[i], ffn2_w_up[i], ffn2_w_down[i])
        h = h + 0.5 * rms_norm(f, ffn2_post_norm[i])
    return h[:, N_META:]


import jax as _jax
import jax.numpy as _jnp

TWIN_FORMAT = 'train_step'
FWD_PARAMS = ['x', 'meta_tokens', 'ffn1_pre_norm', 'ffn1_post_norm', 'ffn1_w_gate', 'ffn1_w_up', 'ffn1_w_down', 'mix_pre_norm', 'mix_post_norm', 'w_in', 'ssm_lambda_re', 'ssm_lambda_im', 'ssm_log_dt', 'ssm_b_re', 'ssm_b_im', 'ssm_c_re', 'ssm_c_im', 'ssm_d', 'ssm_w_glu', 'pool_w', 'pool_scale', 'ssm_out_norm', 'pool_out_norm', 'w_out', 'ffn2_pre_norm', 'ffn2_post_norm', 'ffn2_w_gate', 'ffn2_w_up', 'ffn2_w_down']
TWIN_WEIGHTS = ['meta_tokens', 'ffn1_pre_norm', 'ffn1_post_norm', 'ffn1_w_gate', 'ffn1_w_up', 'ffn1_w_down', 'mix_pre_norm', 'mix_post_norm', 'w_in', 'ssm_lambda_re', 'ssm_lambda_im', 'ssm_log_dt', 'ssm_b_re', 'ssm_b_im', 'ssm_c_re', 'ssm_c_im', 'ssm_d', 'ssm_w_glu', 'pool_w', 'pool_scale', 'ssm_out_norm', 'pool_out_norm', 'w_out', 'ffn2_pre_norm', 'ffn2_post_norm', 'ffn2_w_gate', 'ffn2_w_up', 'ffn2_w_down']
TWIN_DIFF_INPUT = 'x'
TWIN_INPUTS = ['x', 'meta_tokens', 'ffn1_pre_norm', 'ffn1_post_norm', 'ffn1_w_gate', 'ffn1_w_up', 'ffn1_w_down', 'mix_pre_norm', 'mix_post_norm', 'w_in', 'ssm_lambda_re', 'ssm_lambda_im', 'ssm_log_dt', 'ssm_b_re', 'ssm_b_im', 'ssm_c_re', 'ssm_c_im', 'ssm_d', 'ssm_w_glu', 'pool_w', 'pool_scale', 'ssm_out_norm', 'pool_out_norm', 'w_out', 'ffn2_pre_norm', 'ffn2_post_norm', 'ffn2_w_gate', 'ffn2_w_up', 'ffn2_w_down', 'loss_target', 'm_meta_tokens', 'm_ffn1_pre_norm', 'm_ffn1_post_norm', 'm_ffn1_w_gate', 'm_ffn1_w_up', 'm_ffn1_w_down', 'm_mix_pre_norm', 'm_mix_post_norm', 'm_w_in', 'm_ssm_lambda_re', 'm_ssm_lambda_im', 'm_ssm_log_dt', 'm_ssm_b_re', 'm_ssm_b_im', 'm_ssm_c_re', 'm_ssm_c_im', 'm_ssm_d', 'm_ssm_w_glu', 'm_pool_w', 'm_pool_scale', 'm_ssm_out_norm', 'm_pool_out_norm', 'm_w_out', 'm_ffn2_pre_norm', 'm_ffn2_post_norm', 'm_ffn2_w_gate', 'm_ffn2_w_up', 'm_ffn2_w_down', 'v_meta_tokens', 'v_ffn1_pre_norm', 'v_ffn1_post_norm', 'v_ffn1_w_gate', 'v_ffn1_w_up', 'v_ffn1_w_down', 'v_mix_pre_norm', 'v_mix_post_norm', 'v_w_in', 'v_ssm_lambda_re', 'v_ssm_lambda_im', 'v_ssm_log_dt', 'v_ssm_b_re', 'v_ssm_b_im', 'v_ssm_c_re', 'v_ssm_c_im', 'v_ssm_d', 'v_ssm_w_glu', 'v_pool_w', 'v_pool_scale', 'v_ssm_out_norm', 'v_pool_out_norm', 'v_w_out', 'v_ffn2_pre_norm', 'v_ffn2_post_norm', 'v_ffn2_w_gate', 'v_ffn2_w_up', 'v_ffn2_w_down']
TWIN_OUTPUTS = ['loss', 'grad_x', 'grad_meta_tokens', 'grad_ffn1_pre_norm', 'grad_ffn1_post_norm', 'grad_ffn1_w_gate', 'grad_ffn1_w_up', 'grad_ffn1_w_down', 'grad_mix_pre_norm', 'grad_mix_post_norm', 'grad_w_in', 'grad_ssm_lambda_re', 'grad_ssm_lambda_im', 'grad_ssm_log_dt', 'grad_ssm_b_re', 'grad_ssm_b_im', 'grad_ssm_c_re', 'grad_ssm_c_im', 'grad_ssm_d', 'grad_ssm_w_glu', 'grad_pool_w', 'grad_pool_scale', 'grad_ssm_out_norm', 'grad_pool_out_norm', 'grad_w_out', 'grad_ffn2_pre_norm', 'grad_ffn2_post_norm', 'grad_ffn2_w_gate', 'grad_ffn2_w_up', 'grad_ffn2_w_down', 'delta_meta_tokens', 'delta_ffn1_pre_norm', 'delta_ffn1_post_norm', 'delta_ffn1_w_gate', 'delta_ffn1_w_up', 'delta_ffn1_w_down', 'delta_mix_pre_norm', 'delta_mix_post_norm', 'delta_w_in', 'delta_ssm_lambda_re', 'delta_ssm_lambda_im', 'delta_ssm_log_dt', 'delta_ssm_b_re', 'delta_ssm_b_im', 'delta_ssm_c_re', 'delta_ssm_c_im', 'delta_ssm_d', 'delta_ssm_w_glu', 'delta_pool_w', 'delta_pool_scale', 'delta_ssm_out_norm', 'delta_pool_out_norm', 'delta_w_out', 'delta_ffn2_pre_norm', 'delta_ffn2_post_norm', 'delta_ffn2_w_gate', 'delta_ffn2_w_up', 'delta_ffn2_w_down', 'new_m_meta_tokens', 'new_m_ffn1_pre_norm', 'new_m_ffn1_post_norm', 'new_m_ffn1_w_gate', 'new_m_ffn1_w_up', 'new_m_ffn1_w_down', 'new_m_mix_pre_norm', 'new_m_mix_post_norm', 'new_m_w_in', 'new_m_ssm_lambda_re', 'new_m_ssm_lambda_im', 'new_m_ssm_log_dt', 'new_m_ssm_b_re', 'new_m_ssm_b_im', 'new_m_ssm_c_re', 'new_m_ssm_c_im', 'new_m_ssm_d', 'new_m_ssm_w_glu', 'new_m_pool_w', 'new_m_pool_scale', 'new_m_ssm_out_norm', 'new_m_pool_out_norm', 'new_m_w_out', 'new_m_ffn2_pre_norm', 'new_m_ffn2_post_norm', 'new_m_ffn2_w_gate', 'new_m_ffn2_w_up', 'new_m_ffn2_w_down', 'new_v_meta_tokens', 'new_v_ffn1_pre_norm', 'new_v_ffn1_post_norm', 'new_v_ffn1_w_gate', 'new_v_ffn1_w_up', 'new_v_ffn1_w_down', 'new_v_mix_pre_norm', 'new_v_mix_post_norm', 'new_v_w_in', 'new_v_ssm_lambda_re', 'new_v_ssm_lambda_im', 'new_v_ssm_log_dt', 'new_v_ssm_b_re', 'new_v_ssm_b_im', 'new_v_ssm_c_re', 'new_v_ssm_c_im', 'new_v_ssm_d', 'new_v_ssm_w_glu', 'new_v_pool_w', 'new_v_pool_scale', 'new_v_ssm_out_norm', 'new_v_pool_out_norm', 'new_v_w_out', 'new_v_ffn2_pre_norm', 'new_v_ffn2_post_norm', 'new_v_ffn2_w_gate', 'new_v_ffn2_w_up', 'new_v_ffn2_w_down']
TWIN_LEAF_KINDS = {'loss': 'loss', 'grad_x': 'grad_x', 'grad_meta_tokens': 'grad_w', 'grad_ffn1_pre_norm': 'grad_w', 'grad_ffn1_post_norm': 'grad_w', 'grad_ffn1_w_gate': 'grad_w', 'grad_ffn1_w_up': 'grad_w', 'grad_ffn1_w_down': 'grad_w', 'grad_mix_pre_norm': 'grad_w', 'grad_mix_post_norm': 'grad_w', 'grad_w_in': 'grad_w', 'grad_ssm_lambda_re': 'grad_w', 'grad_ssm_lambda_im': 'grad_w', 'grad_ssm_log_dt': 'grad_w', 'grad_ssm_b_re': 'grad_w', 'grad_ssm_b_im': 'grad_w', 'grad_ssm_c_re': 'grad_w', 'grad_ssm_c_im': 'grad_w', 'grad_ssm_d': 'grad_w', 'grad_ssm_w_glu': 'grad_w', 'grad_pool_w': 'grad_w', 'grad_pool_scale': 'grad_w', 'grad_ssm_out_norm': 'grad_w', 'grad_pool_out_norm': 'grad_w', 'grad_w_out': 'grad_w', 'grad_ffn2_pre_norm': 'grad_w', 'grad_ffn2_post_norm': 'grad_w', 'grad_ffn2_w_gate': 'grad_w', 'grad_ffn2_w_up': 'grad_w', 'grad_ffn2_w_down': 'grad_w', 'delta_meta_tokens': 'delta_w', 'delta_ffn1_pre_norm': 'delta_w', 'delta_ffn1_post_norm': 'delta_w', 'delta_ffn1_w_gate': 'delta_w', 'delta_ffn1_w_up': 'delta_w', 'delta_ffn1_w_down': 'delta_w', 'delta_mix_pre_norm': 'delta_w', 'delta_mix_post_norm': 'delta_w', 'delta_w_in': 'delta_w', 'delta_ssm_lambda_re': 'delta_w', 'delta_ssm_lambda_im': 'delta_w', 'delta_ssm_log_dt': 'delta_w', 'delta_ssm_b_re': 'delta_w', 'delta_ssm_b_im': 'delta_w', 'delta_ssm_c_re': 'delta_w', 'delta_ssm_c_im': 'delta_w', 'delta_ssm_d': 'delta_w', 'delta_ssm_w_glu': 'delta_w', 'delta_pool_w': 'delta_w', 'delta_pool_scale': 'delta_w', 'delta_ssm_out_norm': 'delta_w', 'delta_pool_out_norm': 'delta_w', 'delta_w_out': 'delta_w', 'delta_ffn2_pre_norm': 'delta_w', 'delta_ffn2_post_norm': 'delta_w', 'delta_ffn2_w_gate': 'delta_w', 'delta_ffn2_w_up': 'delta_w', 'delta_ffn2_w_down': 'delta_w', 'new_m_meta_tokens': 'new_m', 'new_m_ffn1_pre_norm': 'new_m', 'new_m_ffn1_post_norm': 'new_m', 'new_m_ffn1_w_gate': 'new_m', 'new_m_ffn1_w_up': 'new_m', 'new_m_ffn1_w_down': 'new_m', 'new_m_mix_pre_norm': 'new_m', 'new_m_mix_post_norm': 'new_m', 'new_m_w_in': 'new_m', 'new_m_ssm_lambda_re': 'new_m', 'new_m_ssm_lambda_im': 'new_m', 'new_m_ssm_log_dt': 'new_m', 'new_m_ssm_b_re': 'new_m', 'new_m_ssm_b_im': 'new_m', 'new_m_ssm_c_re': 'new_m', 'new_m_ssm_c_im': 'new_m', 'new_m_ssm_d': 'new_m', 'new_m_ssm_w_glu': 'new_m', 'new_m_pool_w': 'new_m', 'new_m_pool_scale': 'new_m', 'new_m_ssm_out_norm': 'new_m', 'new_m_pool_out_norm': 'new_m', 'new_m_w_out': 'new_m', 'new_m_ffn2_pre_norm': 'new_m', 'new_m_ffn2_post_norm': 'new_m', 'new_m_ffn2_w_gate': 'new_m', 'new_m_ffn2_w_up': 'new_m', 'new_m_ffn2_w_down': 'new_m', 'new_v_meta_tokens': 'new_v', 'new_v_ffn1_pre_norm': 'new_v', 'new_v_ffn1_post_norm': 'new_v', 'new_v_ffn1_w_gate': 'new_v', 'new_v_ffn1_w_up': 'new_v', 'new_v_ffn1_w_down': 'new_v', 'new_v_mix_pre_norm': 'new_v', 'new_v_mix_post_norm': 'new_v', 'new_v_w_in': 'new_v', 'new_v_ssm_lambda_re': 'new_v', 'new_v_ssm_lambda_im': 'new_v', 'new_v_ssm_log_dt': 'new_v', 'new_v_ssm_b_re': 'new_v', 'new_v_ssm_b_im': 'new_v', 'new_v_ssm_c_re': 'new_v', 'new_v_ssm_c_im': 'new_v', 'new_v_ssm_d': 'new_v', 'new_v_ssm_w_glu': 'new_v', 'new_v_pool_w': 'new_v', 'new_v_pool_scale': 'new_v', 'new_v_ssm_out_norm': 'new_v', 'new_v_pool_out_norm': 'new_v', 'new_v_w_out': 'new_v', 'new_v_ffn2_pre_norm': 'new_v', 'new_v_ffn2_post_norm': 'new_v', 'new_v_ffn2_w_gate': 'new_v', 'new_v_ffn2_w_up': 'new_v', 'new_v_ffn2_w_down': 'new_v'}


def _forward(args):
    return _fwd_reference(*[args[k] for k in FWD_PARAMS])


def _output_shape():
    out = _jax.eval_shape(lambda: _forward(_fwd_setup_inputs(0)))
    return out.shape, out.dtype

N_MICROBATCH = 1
ADAM_LR = 0.001
ADAM_B1 = 0.9
ADAM_B2 = 0.999
ADAM_EPS = 1e-08
ADAM_WD = 0.01
ADAM_STEP = 10
PER_EXAMPLE_BATCH_AXIS = {'x': 0, 'loss_target': 0}
SHARED_INPUTS = []
_WEIGHT_DTYPES = {'meta_tokens': _jnp.float32, 'ffn1_pre_norm': _jnp.float32, 'ffn1_post_norm': _jnp.float32, 'ffn1_w_gate': _jnp.float32, 'ffn1_w_up': _jnp.float32, 'ffn1_w_down': _jnp.float32, 'mix_pre_norm': _jnp.float32, 'mix_post_norm': _jnp.float32, 'w_in': _jnp.float32, 'ssm_lambda_re': _jnp.float32, 'ssm_lambda_im': _jnp.float32, 'ssm_log_dt': _jnp.float32, 'ssm_b_re': _jnp.float32, 'ssm_b_im': _jnp.float32, 'ssm_c_re': _jnp.float32, 'ssm_c_im': _jnp.float32, 'ssm_d': _jnp.float32, 'ssm_w_glu': _jnp.float32, 'pool_w': _jnp.float32, 'pool_scale': _jnp.float32, 'ssm_out_norm': _jnp.float32, 'pool_out_norm': _jnp.float32, 'w_out': _jnp.float32, 'ffn2_pre_norm': _jnp.float32, 'ffn2_post_norm': _jnp.float32, 'ffn2_w_gate': _jnp.float32, 'ffn2_w_up': _jnp.float32, 'ffn2_w_down': _jnp.float32}
MOMENT_SCALE = {'meta_tokens': 4.014640e-03, 'ffn1_pre_norm': 5.481770e-01, 'ffn1_post_norm': 1.584899e+01, 'ffn1_w_gate': 2.152898e-01, 'ffn1_w_up': 2.475822e-01, 'ffn1_w_down': 4.392054e-01, 'mix_pre_norm': 1.306658e+00, 'mix_post_norm': 6.469365e+01, 'w_in': 1.113297e+00, 'ssm_lambda_re': 3.024574e-02, 'ssm_lambda_im': 3.416739e-02, 'ssm_log_dt': 3.250022e+01, 'ssm_b_re': 1.825531e-02, 'ssm_b_im': 1.792078e-02, 'ssm_c_re': 3.529293e-02, 'ssm_c_im': 3.621341e-02, 'ssm_d': 1.837725e+00, 'ssm_w_glu': 1.430559e+00, 'pool_w': 1.279605e+00, 'pool_scale': 1.327140e+00, 'ssm_out_norm': 2.188418e+00, 'pool_out_norm': 1.324723e+00, 'w_out': 1.672332e+00, 'ffn2_pre_norm': 1.167804e+00, 'ffn2_post_norm': 1.590742e+01, 'ffn2_w_gate': 3.032857e-01, 'ffn2_w_up': 5.395423e-01, 'ffn2_w_down': 9.090840e-01}


def _to_microbatches(a, axis):
    t = _jnp.moveaxis(a, axis, 0)
    t = t.reshape((N_MICROBATCH, t.shape[0] // N_MICROBATCH) + t.shape[1:])
    return _jnp.moveaxis(t, 1, axis + 1)


def setup_inputs(seed: int = 0) -> dict:
    inp = _fwd_setup_inputs(seed)
    key = _jax.random.fold_in(_jax.random.key(seed), 7919)
    shape, _ = _output_shape()
    out = dict(inp)
    out["loss_target"] = _jax.random.normal(_jax.random.fold_in(key, 0), shape, _jnp.float32)
    for i, name in enumerate(TWIN_WEIGHTS):
        w = inp[name].astype(_jnp.float32)
        if MOMENT_SCALE is None:
            s = _jnp.sqrt(_jnp.mean(_jnp.square(w)) + 1e-30)
        else:
            s = MOMENT_SCALE[name]
        km, kv = _jax.random.split(_jax.random.fold_in(key, i + 1))
        out[name] = w
        out["m_" + name] = s * _jax.random.normal(km, w.shape, _jnp.float32)
        out["v_" + name] = (s * s) * _jax.random.uniform(kv, w.shape, _jnp.float32, 0.5, 1.5)
    if N_MICROBATCH > 1:
        for name, axis in PER_EXAMPLE_BATCH_AXIS.items():
            out[name] = _to_microbatches(out[name], axis)
    return {'x': out['x'], 'meta_tokens': out['meta_tokens'], 'ffn1_pre_norm': out['ffn1_pre_norm'], 'ffn1_post_norm': out['ffn1_post_norm'], 'ffn1_w_gate': out['ffn1_w_gate'], 'ffn1_w_up': out['ffn1_w_up'], 'ffn1_w_down': out['ffn1_w_down'], 'mix_pre_norm': out['mix_pre_norm'], 'mix_post_norm': out['mix_post_norm'], 'w_in': out['w_in'], 'ssm_lambda_re': out['ssm_lambda_re'], 'ssm_lambda_im': out['ssm_lambda_im'], 'ssm_log_dt': out['ssm_log_dt'], 'ssm_b_re': out['ssm_b_re'], 'ssm_b_im': out['ssm_b_im'], 'ssm_c_re': out['ssm_c_re'], 'ssm_c_im': out['ssm_c_im'], 'ssm_d': out['ssm_d'], 'ssm_w_glu': out['ssm_w_glu'], 'pool_w': out['pool_w'], 'pool_scale': out['pool_scale'], 'ssm_out_norm': out['ssm_out_norm'], 'pool_out_norm': out['pool_out_norm'], 'w_out': out['w_out'], 'ffn2_pre_norm': out['ffn2_pre_norm'], 'ffn2_post_norm': out['ffn2_post_norm'], 'ffn2_w_gate': out['ffn2_w_gate'], 'ffn2_w_up': out['ffn2_w_up'], 'ffn2_w_down': out['ffn2_w_down'], 'loss_target': out['loss_target'], 'm_meta_tokens': out['m_meta_tokens'], 'm_ffn1_pre_norm': out['m_ffn1_pre_norm'], 'm_ffn1_post_norm': out['m_ffn1_post_norm'], 'm_ffn1_w_gate': out['m_ffn1_w_gate'], 'm_ffn1_w_up': out['m_ffn1_w_up'], 'm_ffn1_w_down': out['m_ffn1_w_down'], 'm_mix_pre_norm': out['m_mix_pre_norm'], 'm_mix_post_norm': out['m_mix_post_norm'], 'm_w_in': out['m_w_in'], 'm_ssm_lambda_re': out['m_ssm_lambda_re'], 'm_ssm_lambda_im': out['m_ssm_lambda_im'], 'm_ssm_log_dt': out['m_ssm_log_dt'], 'm_ssm_b_re': out['m_ssm_b_re'], 'm_ssm_b_im': out['m_ssm_b_im'], 'm_ssm_c_re': out['m_ssm_c_re'], 'm_ssm_c_im': out['m_ssm_c_im'], 'm_ssm_d': out['m_ssm_d'], 'm_ssm_w_glu': out['m_ssm_w_glu'], 'm_pool_w': out['m_pool_w'], 'm_pool_scale': out['m_pool_scale'], 'm_ssm_out_norm': out['m_ssm_out_norm'], 'm_pool_out_norm': out['m_pool_out_norm'], 'm_w_out': out['m_w_out'], 'm_ffn2_pre_norm': out['m_ffn2_pre_norm'], 'm_ffn2_post_norm': out['m_ffn2_post_norm'], 'm_ffn2_w_gate': out['m_ffn2_w_gate'], 'm_ffn2_w_up': out['m_ffn2_w_up'], 'm_ffn2_w_down': out['m_ffn2_w_down'], 'v_meta_tokens': out['v_meta_tokens'], 'v_ffn1_pre_norm': out['v_ffn1_pre_norm'], 'v_ffn1_post_norm': out['v_ffn1_post_norm'], 'v_ffn1_w_gate': out['v_ffn1_w_gate'], 'v_ffn1_w_up': out['v_ffn1_w_up'], 'v_ffn1_w_down': out['v_ffn1_w_down'], 'v_mix_pre_norm': out['v_mix_pre_norm'], 'v_mix_post_norm': out['v_mix_post_norm'], 'v_w_in': out['v_w_in'], 'v_ssm_lambda_re': out['v_ssm_lambda_re'], 'v_ssm_lambda_im': out['v_ssm_lambda_im'], 'v_ssm_log_dt': out['v_ssm_log_dt'], 'v_ssm_b_re': out['v_ssm_b_re'], 'v_ssm_b_im': out['v_ssm_b_im'], 'v_ssm_c_re': out['v_ssm_c_re'], 'v_ssm_c_im': out['v_ssm_c_im'], 'v_ssm_d': out['v_ssm_d'], 'v_ssm_w_glu': out['v_ssm_w_glu'], 'v_pool_w': out['v_pool_w'], 'v_pool_scale': out['v_pool_scale'], 'v_ssm_out_norm': out['v_ssm_out_norm'], 'v_pool_out_norm': out['v_pool_out_norm'], 'v_w_out': out['v_w_out'], 'v_ffn2_pre_norm': out['v_ffn2_pre_norm'], 'v_ffn2_post_norm': out['v_ffn2_post_norm'], 'v_ffn2_w_gate': out['v_ffn2_w_gate'], 'v_ffn2_w_up': out['v_ffn2_w_up'], 'v_ffn2_w_down': out['v_ffn2_w_down']}


def _loss(weights, diff, rest, loss_target):
    with _jax.named_scope("forward"):
        args = {**rest, TWIN_DIFF_INPUT: diff, **{k: w.astype(_WEIGHT_DTYPES[k]) for k, w in weights.items()}}
        y = _forward(args)
    with _jax.named_scope("loss_head"):
        err = _jnp.square(y.astype(_jnp.float32) - loss_target)
        return 0.5 * _jnp.sum(_jnp.mean(err, axis=-1)) if err.ndim else 0.5 * err


def _adamw(w, g, m, v):
    m = ADAM_B1 * m + (1.0 - ADAM_B1) * g
    v = ADAM_B2 * v + (1.0 - ADAM_B2) * _jnp.square(g)
    m_hat = m / (1.0 - ADAM_B1 ** ADAM_STEP)
    v_hat = v / (1.0 - ADAM_B2 ** ADAM_STEP)
    delta = -ADAM_LR * (m_hat / (_jnp.sqrt(v_hat) + ADAM_EPS) + ADAM_WD * w)
    return delta, m, v


def reference(x, meta_tokens, ffn1_pre_norm, ffn1_post_norm, ffn1_w_gate, ffn1_w_up, ffn1_w_down, mix_pre_norm, mix_post_norm, w_in, ssm_lambda_re, ssm_lambda_im, ssm_log_dt, ssm_b_re, ssm_b_im, ssm_c_re, ssm_c_im, ssm_d, ssm_w_glu, pool_w, pool_scale, ssm_out_norm, pool_out_norm, w_out, ffn2_pre_norm, ffn2_post_norm, ffn2_w_gate, ffn2_w_up, ffn2_w_down, loss_target, m_meta_tokens, m_ffn1_pre_norm, m_ffn1_post_norm, m_ffn1_w_gate, m_ffn1_w_up, m_ffn1_w_down, m_mix_pre_norm, m_mix_post_norm, m_w_in, m_ssm_lambda_re, m_ssm_lambda_im, m_ssm_log_dt, m_ssm_b_re, m_ssm_b_im, m_ssm_c_re, m_ssm_c_im, m_ssm_d, m_ssm_w_glu, m_pool_w, m_pool_scale, m_ssm_out_norm, m_pool_out_norm, m_w_out, m_ffn2_pre_norm, m_ffn2_post_norm, m_ffn2_w_gate, m_ffn2_w_up, m_ffn2_w_down, v_meta_tokens, v_ffn1_pre_norm, v_ffn1_post_norm, v_ffn1_w_gate, v_ffn1_w_up, v_ffn1_w_down, v_mix_pre_norm, v_mix_post_norm, v_w_in, v_ssm_lambda_re, v_ssm_lambda_im, v_ssm_log_dt, v_ssm_b_re, v_ssm_b_im, v_ssm_c_re, v_ssm_c_im, v_ssm_d, v_ssm_w_glu, v_pool_w, v_pool_scale, v_ssm_out_norm, v_pool_out_norm, v_w_out, v_ffn2_pre_norm, v_ffn2_post_norm, v_ffn2_w_gate, v_ffn2_w_up, v_ffn2_w_down):
    given = dict(x=x, meta_tokens=meta_tokens, ffn1_pre_norm=ffn1_pre_norm, ffn1_post_norm=ffn1_post_norm, ffn1_w_gate=ffn1_w_gate, ffn1_w_up=ffn1_w_up, ffn1_w_down=ffn1_w_down, mix_pre_norm=mix_pre_norm, mix_post_norm=mix_post_norm, w_in=w_in, ssm_lambda_re=ssm_lambda_re, ssm_lambda_im=ssm_lambda_im, ssm_log_dt=ssm_log_dt, ssm_b_re=ssm_b_re, ssm_b_im=ssm_b_im, ssm_c_re=ssm_c_re, ssm_c_im=ssm_c_im, ssm_d=ssm_d, ssm_w_glu=ssm_w_glu, pool_w=pool_w, pool_scale=pool_scale, ssm_out_norm=ssm_out_norm, pool_out_norm=pool_out_norm, w_out=w_out, ffn2_pre_norm=ffn2_pre_norm, ffn2_post_norm=ffn2_post_norm, ffn2_w_gate=ffn2_w_gate, ffn2_w_up=ffn2_w_up, ffn2_w_down=ffn2_w_down, loss_target=loss_target, m_meta_tokens=m_meta_tokens, m_ffn1_pre_norm=m_ffn1_pre_norm, m_ffn1_post_norm=m_ffn1_post_norm, m_ffn1_w_gate=m_ffn1_w_gate, m_ffn1_w_up=m_ffn1_w_up, m_ffn1_w_down=m_ffn1_w_down, m_mix_pre_norm=m_mix_pre_norm, m_mix_post_norm=m_mix_post_norm, m_w_in=m_w_in, m_ssm_lambda_re=m_ssm_lambda_re, m_ssm_lambda_im=m_ssm_lambda_im, m_ssm_log_dt=m_ssm_log_dt, m_ssm_b_re=m_ssm_b_re, m_ssm_b_im=m_ssm_b_im, m_ssm_c_re=m_ssm_c_re, m_ssm_c_im=m_ssm_c_im, m_ssm_d=m_ssm_d, m_ssm_w_glu=m_ssm_w_glu, m_pool_w=m_pool_w, m_pool_scale=m_pool_scale, m_ssm_out_norm=m_ssm_out_norm, m_pool_out_norm=m_pool_out_norm, m_w_out=m_w_out, m_ffn2_pre_norm=m_ffn2_pre_norm, m_ffn2_post_norm=m_ffn2_post_norm, m_ffn2_w_gate=m_ffn2_w_gate, m_ffn2_w_up=m_ffn2_w_up, m_ffn2_w_down=m_ffn2_w_down, v_meta_tokens=v_meta_tokens, v_ffn1_pre_norm=v_ffn1_pre_norm, v_ffn1_post_norm=v_ffn1_post_norm, v_ffn1_w_gate=v_ffn1_w_gate, v_ffn1_w_up=v_ffn1_w_up, v_ffn1_w_down=v_ffn1_w_down, v_mix_pre_norm=v_mix_pre_norm, v_mix_post_norm=v_mix_post_norm, v_w_in=v_w_in, v_ssm_lambda_re=v_ssm_lambda_re, v_ssm_lambda_im=v_ssm_lambda_im, v_ssm_log_dt=v_ssm_log_dt, v_ssm_b_re=v_ssm_b_re, v_ssm_b_im=v_ssm_b_im, v_ssm_c_re=v_ssm_c_re, v_ssm_c_im=v_ssm_c_im, v_ssm_d=v_ssm_d, v_ssm_w_glu=v_ssm_w_glu, v_pool_w=v_pool_w, v_pool_scale=v_pool_scale, v_ssm_out_norm=v_ssm_out_norm, v_pool_out_norm=v_pool_out_norm, v_w_out=v_w_out, v_ffn2_pre_norm=v_ffn2_pre_norm, v_ffn2_post_norm=v_ffn2_post_norm, v_ffn2_w_gate=v_ffn2_w_gate, v_ffn2_w_up=v_ffn2_w_up, v_ffn2_w_down=v_ffn2_w_down)
    weights = {n: given[n] for n in TWIN_WEIGHTS}
    shared = {n: given[n] for n in SHARED_INPUTS}
    per_example = {n: given[n] for n in ['x']}
    grad_fn = _jax.value_and_grad(_loss, argnums=(0, 1))

    def one_microbatch(ex, loss_target):
        ex = dict(ex)
        diff = ex.pop(TWIN_DIFF_INPUT)
        return grad_fn(weights, diff, {**shared, **ex}, loss_target)

    if N_MICROBATCH == 1:
        loss, (grad_w, grad_x) = one_microbatch(per_example, given["loss_target"])
    else:
        def body(carry, xs):
            loss_sum, grad_sum = carry
            l_k, (gw_k, gx_k) = one_microbatch(xs[0], xs[1])
            with _jax.named_scope("update"):
                return (loss_sum + l_k, _jax.tree.map(_jnp.add, grad_sum, gw_k)), gx_k

        init = (_jnp.zeros((), _jnp.float32), _jax.tree.map(_jnp.zeros_like, weights))
        (loss, grad_w), grad_x = _jax.lax.scan(body, init, (per_example, given["loss_target"]))
    with _jax.named_scope("update"):
        delta_w, new_m, new_v = {}, {}, {}
        for n in TWIN_WEIGHTS:
            delta_w[n], new_m[n], new_v[n] = _adamw(weights[n], grad_w[n], given["m_" + n], given["v_" + n])
    return (loss, grad_x, *[grad_w[n] for n in TWIN_WEIGHTS], *[delta_w[n] for n in TWIN_WEIGHTS],
            *[new_m[n] for n in TWIN_WEIGHTS], *[new_v[n] for n in TWIN_WEIGHTS])
```

```python
import functools
import math

import jax
import jax.numpy as jnp
from jax import lax
from jax.experimental import pallas as pl
from jax.experimental.pallas import tpu as pltpu

F32 = jnp.float32
MXU_DTYPE = jnp.bfloat16

RMS_EPS = 1e-6
N_META = 16
SSM_GROUP_CH = 16
SSM_STATE = 64
POOL_WINDOWS = (2, 4, 8, 16)
POOL_HALO = 16
ADAM_LR, ADAM_B1, ADAM_B2, ADAM_EPS, ADAM_WD, ADAM_STEP = 0.001, 0.9, 0.999, 1e-08, 0.01, 10

LANES = 128
SUBLANES = 8
VMEM_LIMIT = 60 * 1024 * 1024
FFN_TILE = 432
MIX_TILE = 216
SLAB_GROUP = 8
MESH = pl.DeviceIdType.MESH
ANY = pl.BlockSpec(memory_space=pl.ANY)


def _params(sem=None, vmem=VMEM_LIMIT):
    if sem is None:
        return pltpu.CompilerParams(vmem_limit_bytes=vmem)
    return pltpu.CompilerParams(dimension_semantics=sem, vmem_limit_bytes=vmem)


def _mm(a, b):
    return jnp.dot(a.astype(MXU_DTYPE), b.astype(MXU_DTYPE), preferred_element_type=F32)


def _mm_nt(a, b):
    return lax.dot_general(a.astype(MXU_DTYPE), b.astype(MXU_DTYPE), (((1,), (1,)), ((), ())), preferred_element_type=F32)


def _mm_tn(a, b):
    return lax.dot_general(a.astype(MXU_DTYPE), b.astype(MXU_DTYPE), (((0,), (0,)), ((), ())), preferred_element_type=F32)


def _rms_stat(x):
    return lax.rsqrt(jnp.mean(x * x, axis=-1, keepdims=True) + RMS_EPS)


def _rms_bwd(x, g, dy):
    r = _rms_stat(x)
    xh = x * r
    dg = jnp.sum(dy * xh, axis=0, keepdims=True)
    dxh = dy * g
    dx = r * (dxh - xh * jnp.mean(dxh * xh, axis=-1, keepdims=True))
    return dx, dg


def _sigmoid(x):
    return 1.0 / (1.0 + jnp.exp(-x))


GELU_C = math.sqrt(2.0 / math.pi)
GELU_K = 0.044715


def _gelu(y):
    return 0.5 * y * (1.0 + jnp.tanh(GELU_C * (y + GELU_K * y * y * y)))


def _gelu_grad(y):
    th = jnp.tanh(GELU_C * (y + GELU_K * y * y * y))
    return 0.5 * (1.0 + th) + 0.5 * y * (1.0 - th * th) * GELU_C * (1.0 + 3.0 * GELU_K * y * y)


def _row_spec(tile, cols, rev_n=None):
    if rev_n is None:
        return pl.BlockSpec((tile, cols), lambda i: (i, 0))
    return pl.BlockSpec((tile, cols), lambda i: (rev_n - 1 - i, 0))


def _full_spec(shape):
    zeros = (0,) * len(shape)
    return pl.BlockSpec(shape, lambda *_: zeros)


def _acc(ref, val, first):
    @pl.when(first)
    def _():
        ref[...] = val

    @pl.when(jnp.logical_not(first))
    def _():
        ref[...] += val


def _place():
    x, y, c = lax.axis_index("x"), lax.axis_index("y"), lax.axis_index("c")
    others = [(1 - x, y), (x, 1 - y), (1 - x, 1 - y)]
    return x, y, c, others


def _gather_xy(shards, name):
    n = len(shards)

    def body(*refs):
        ins, outs = refs[:n], refs[n : 2 * n]
        send_sems, recv_sems, local_sems = refs[2 * n :]
        x, y, c, others = _place()
        me = 2 * x + y
        sibling = (x, y, 1 - c)

        def copy(a, k, chip, to, src=None):
            dst = outs[a].at[2 * chip[0] + chip[1], c]
            return pltpu.make_async_remote_copy(
                src_ref=dst if src is None else src, dst_ref=dst, send_sem=send_sems.at[a * 6 + k], recv_sem=recv_sems.at[a * 6 + k], device_id=to, device_id_type=MESH
            )

        local = [pltpu.make_async_copy(ins[a], outs[a].at[me], local_sems.at[a]) for a in range(n)]
        for cp in local:
            cp.start()
        first = [copy(a, j, (x, y), (*chip, c), src=ins[a].at[c]) for a in range(n) for j, chip in enumerate(others)]
        for cp in first:
            cp.start()
        passed = []
        for a in range(n):
            for j, chip in enumerate(others):
                copy(a, j, chip, (x, y, c)).wait_recv()
                fwd = pltpu.make_async_remote_copy(
                    src_ref=outs[a].at[2 * chip[0] + chip[1], c],
                    dst_ref=outs[a].at[2 * chip[0] + chip[1], c],
                    send_sem=send_sems.at[a * 6 + 3 + j],
                    recv_sem=recv_sems.at[a * 6 + 3 + j],
                    device_id=sibling,
                    device_id_type=MESH,
                )
                fwd.start()
                passed.append(fwd)
        for a in range(n):
            for j, chip in enumerate(others):
                got = outs[a].at[2 * chip[0] + chip[1], 1 - c]
                pltpu.make_async_remote_copy(
                    src_ref=got, dst_ref=got, send_sem=send_sems.at[a * 6 + 3 + j], recv_sem=recv_sems.at[a * 6 + 3 + j], device_id=sibling, device_id_type=MESH
                ).wait_recv()
        for cp in first + passed:
            cp.wait_send()
        for cp in local:
            cp.wait()

    return pl.pallas_call(
        body,
        name=name,
        out_shape=[jax.ShapeDtypeStruct((4,) + s.shape, s.dtype) for s in shards],
        in_specs=[ANY] * n,
        out_specs=[ANY] * n,
        scratch_shapes=[pltpu.SemaphoreType.DMA((6 * n,)), pltpu.SemaphoreType.DMA((6 * n,)), pltpu.SemaphoreType.DMA((n,))],
    )(*shards)


def _sibling_scatter(views, name):
    n = len(views)

    def body(*refs):
        ins, outs = refs[:n], refs[n : 2 * n]
        send_sems, recv_sems = refs[2 * n :]
        x, y, c, _ = _place()
        cps = []
        for a in range(n):
            for k in range(4):
                cps.append(
                    pltpu.make_async_remote_copy(
                        src_ref=ins[a].at[k, 1 - c], dst_ref=outs[a].at[k], send_sem=send_sems.at[4 * a + k], recv_sem=recv_sems.at[4 * a + k], device_id=(x, y, 1 - c), device_id_type=MESH
                    )
                )
        for cp in cps:
            cp.start()
        for cp in cps:
            cp.wait_recv()
        for cp in cps:
            cp.wait_send()

    return pl.pallas_call(
        body,
        name=name,
        out_shape=[jax.ShapeDtypeStruct((4,) + v.shape[2:], v.dtype) for v in views],
        in_specs=[ANY] * n,
        out_specs=[ANY] * n,
        scratch_shapes=[pltpu.SemaphoreType.DMA((4 * n,)), pltpu.SemaphoreType.DMA((4 * n,))],
    )(*views)


def _chip_scatter(parts, name):
    n = len(parts)

    def body(*refs):
        ins, outs = refs[:n], refs[n : 2 * n]
        send_sems, recv_sems = refs[2 * n :]
        x, y, c, others = _place()
        cps = []
        for a in range(n):
            for j, chip in enumerate(others):
                cps.append(
                    pltpu.make_async_remote_copy(
                        src_ref=ins[a].at[2 * chip[0] + chip[1]], dst_ref=outs[a].at[j], send_sem=send_sems.at[3 * a + j], recv_sem=recv_sems.at[3 * a + j], device_id=(*chip, c), device_id_type=MESH
                    )
                )
        for cp in cps:
            cp.start()
        for cp in cps:
            cp.wait_recv()
        for cp in cps:
            cp.wait_send()

    return pl.pallas_call(
        body,
        name=name,
        out_shape=[jax.ShapeDtypeStruct((3,) + p.shape[1:], p.dtype) for p in parts],
        in_specs=[ANY] * n,
        out_specs=[ANY] * n,
        scratch_shapes=[pltpu.SemaphoreType.DMA((3 * n,)), pltpu.SemaphoreType.DMA((3 * n,))],
    )(*parts)


def _sibling_share(halves, name):
    n = len(halves)

    def body(*refs):
        ins, outs = refs[:n], refs[n : 2 * n]
        send_sems, recv_sems, local_sems = refs[2 * n :]
        x, y, c, _ = _place()
        local = [pltpu.make_async_copy(ins[a], outs[a].at[c], local_sems.at[a]) for a in range(n)]
        sends = [
            pltpu.make_async_remote_copy(src_ref=ins[a], dst_ref=outs[a].at[c], send_sem=send_sems.at[a], recv_sem=recv_sems.at[a], device_id=(x, y, 1 - c), device_id_type=MESH)
            for a in range(n)
        ]
        for cp in local + sends:
            cp.start()
        for a in range(n):
            got = outs[a].at[1 - c]
            pltpu.make_async_remote_copy(src_ref=got, dst_ref=got, send_sem=send_sems.at[a], recv_sem=recv_sems.at[a], device_id=(x, y, 1 - c), device_id_type=MESH).wait_recv()
        for cp in sends:
            cp.wait_send()
        for cp in local:
            cp.wait()

    return pl.pallas_call(
        body,
        name=name,
        out_shape=[jax.ShapeDtypeStruct((2,) + h.shape, h.dtype) for h in halves],
        in_specs=[ANY] * n,
        out_specs=[ANY] * n,
        scratch_shapes=[pltpu.SemaphoreType.DMA((n,)), pltpu.SemaphoreType.DMA((n,)), pltpu.SemaphoreType.DMA((n,))],
    )(*halves)


def _row_tile(rows):
    for t in (512, 352, 256, 176, 128, 112, 64, 32, 16, 8):
        if rows % t == 0:
            return t
    return rows


def _add_own_half(view, got, core, name):
    _, _, r, c = view.shape
    tr = _row_tile(r)

    def body(core_ref, v_ref, g_ref, o_ref):
        o_ref[...] = v_ref[...] + g_ref[...]

    return pl.pallas_call(
        body,
        name=name,
        out_shape=jax.ShapeDtypeStruct((4, r, c), F32),
        grid_spec=pltpu.PrefetchScalarGridSpec(
            num_scalar_prefetch=1,
            grid=(4, r // tr),
            in_specs=[pl.BlockSpec((None, None, tr, c), lambda k, i, cr: (k, cr[0], i, 0)), pl.BlockSpec((None, tr, c), lambda k, i, cr: (k, i, 0))],
            out_specs=pl.BlockSpec((None, tr, c), lambda k, i, cr: (k, i, 0)),
        ),
        compiler_params=_params(("arbitrary", "arbitrary")),
    )(core, view, got)


def _add_chips(part, got, chip, name):
    _, r, c = part.shape
    tr = _row_tile(r)

    def body(chip_ref, p_ref, g_ref, o_ref):
        o_ref[...] = ((p_ref[...] + g_ref[0]) + g_ref[1]) + g_ref[2]

    return pl.pallas_call(
        body,
        name=name,
        out_shape=jax.ShapeDtypeStruct((r, c), F32),
        grid_spec=pltpu.PrefetchScalarGridSpec(
            num_scalar_prefetch=1,
            grid=(r // tr,),
            in_specs=[pl.BlockSpec((None, tr, c), lambda i, ch: (ch[0], i, 0)), pl.BlockSpec((3, tr, c), lambda i, ch: (0, i, 0))],
            out_specs=pl.BlockSpec((tr, c), lambda i, ch: (i, 0)),
        ),
        compiler_params=_params(("arbitrary",)),
    )(chip, part, got)


def _adamw(w, g, m, v, name):
    r, c = w.shape
    tr = _row_tile(r)

    def body(w_ref, g_ref, m_ref, v_ref, d_ref, nm_ref, nv_ref):
        g = g_ref[...]
        nm = ADAM_B1 * m_ref[...] + (1.0 - ADAM_B1) * g
        nv = ADAM_B2 * v_ref[...] + (1.0 - ADAM_B2) * (g * g)
        m_hat = nm / (1.0 - ADAM_B1**ADAM_STEP)
        v_hat = nv / (1.0 - ADAM_B2**ADAM_STEP)
        d_ref[...] = -ADAM_LR * (m_hat / (jnp.sqrt(v_hat) + ADAM_EPS) + ADAM_WD * w_ref[...])
        nm_ref[...] = nm
        nv_ref[...] = nv

    spec = pl.BlockSpec((tr, c), lambda i: (i, 0))
    return pl.pallas_call(
        body, name=name, out_shape=[jax.ShapeDtypeStruct((r, c), F32)] * 3, grid=(r // tr,), in_specs=[spec] * 4, out_specs=[spec] * 3, compiler_params=_params(("arbitrary",))
    )(w, g, m, v)


def _load_weights(pairs, sems):
    @pl.when(pl.program_id(0) == 0)
    def _():
        cps = [pltpu.make_async_copy(src, dst, sems.at[k]) for k, (src, dst) in enumerate(pairs)]
        for cp in cps:
            cp.start()
        for cp in cps:
            cp.wait()


def _ffn_fwd(h, g_pre, g_post, wg, wu, wd, name):
    T, D = h.shape
    ns, _, fs = wg.shape
    tm = FFN_TILE

    def body(h_ref, gpre_ref, gpost_ref, wg_hbm, wu_hbm, wd_hbm, hout_ref, f_ref, a_ref, b_ref, s_ref, n_ref, wg_v, wu_v, wd_v, sems):
        _load_weights([(wg_hbm, wg_v), (wu_hbm, wu_v), (wd_hbm, wd_v)], sems)
        hh = h_ref[...]
        n = (hh * _rms_stat(hh) * gpre_ref[...]).astype(MXU_DTYPE)
        n_ref[...] = n.astype(n_ref.dtype)
        f = jnp.zeros((tm, D), F32)
        for k in range(ns):
            a = _mm(n, wg_v[k])
            b = _mm(n, wu_v[k])
            s = (a * _sigmoid(a) * b).astype(MXU_DTYPE)
            a_ref[:, k * fs : (k + 1) * fs] = a.astype(a_ref.dtype)
            b_ref[:, k * fs : (k + 1) * fs] = b.astype(b_ref.dtype)
            s_ref[:, k * fs : (k + 1) * fs] = s.astype(s_ref.dtype)
            f = f + _mm(s, wd_v[k])
        f_ref[...] = f
        hout_ref[...] = hh + 0.5 * (f * _rms_stat(f) * gpost_ref[...])

    act = jax.ShapeDtypeStruct((T, ns * fs), MXU_DTYPE)
    return pl.pallas_call(
        body,
        name=name,
        grid=(T // tm,),
        out_shape=[jax.ShapeDtypeStruct((T, D), F32), jax.ShapeDtypeStruct((T, D), F32), act, act, act, jax.ShapeDtypeStruct((T, D), MXU_DTYPE)],
        in_specs=[_row_spec(tm, D), _full_spec((1, D)), _full_spec((1, D)), ANY, ANY, ANY],
        out_specs=[_row_spec(tm, D), _row_spec(tm, D), _row_spec(tm, ns * fs), _row_spec(tm, ns * fs), _row_spec(tm, ns * fs), _row_spec(tm, D)],
        scratch_shapes=[pltpu.VMEM(wg.shape, wg.dtype), pltpu.VMEM(wu.shape, wu.dtype), pltpu.VMEM(wd.shape, wd.dtype), pltpu.SemaphoreType.DMA((3,))],
        compiler_params=_params(("arbitrary",)),
    )(h, g_pre, g_post, wg, wu, wd)


def _ffn_bwd_down(dh, f, a, b, g_post, wd, name):
    T, D = dh.shape
    ns, fs, _ = wd.shape
    tm = FFN_TILE

    def body(dh_ref, f_ref, a_ref, b_ref, gpost_ref, wd_hbm, da_ref, db_ref, df_ref, dg_ref, wd_v, sems):
        _load_weights([(wd_hbm, wd_v)], sems)
        df, dg = _rms_bwd(f_ref[...], gpost_ref[...], 0.5 * dh_ref[...])
        _acc(dg_ref, dg, pl.program_id(0) == 0)
        dfb = df.astype(MXU_DTYPE)
        df_ref[...] = dfb.astype(df_ref.dtype)
        for k in range(ns):
            ds = _mm_nt(dfb, wd_v[k])
            a = a_ref[:, k * fs : (k + 1) * fs].astype(F32)
            b = b_ref[:, k * fs : (k + 1) * fs].astype(F32)
            sg = _sigmoid(a)
            db_ref[:, k * fs : (k + 1) * fs] = (ds * (a * sg)).astype(db_ref.dtype)
            da_ref[:, k * fs : (k + 1) * fs] = (ds * b * (sg * (1.0 + a * (1.0 - sg)))).astype(da_ref.dtype)

    act = jax.ShapeDtypeStruct((T, ns * fs), MXU_DTYPE)
    return pl.pallas_call(
        body,
        name=name,
        grid=(T // tm,),
        out_shape=[act, act, jax.ShapeDtypeStruct((T, D), MXU_DTYPE), jax.ShapeDtypeStruct((1, D), F32)],
        in_specs=[_row_spec(tm, D), _row_spec(tm, D), _row_spec(tm, ns * fs), _row_spec(tm, ns * fs), _full_spec((1, D)), ANY],
        out_specs=[_row_spec(tm, ns * fs), _row_spec(tm, ns * fs), _row_spec(tm, D), _full_spec((1, D))],
        scratch_shapes=[pltpu.VMEM(wd.shape, wd.dtype), pltpu.SemaphoreType.DMA((1,))],
        compiler_params=_params(("arbitrary",)),
    )(dh, f, a, b, g_post, wd)


def _ffn_bwd_up(da, db, h, dh, g_pre, wg, wu, name):
    T, D = h.shape
    ns, _, fs = wg.shape
    tm = FFN_TILE

    def body(da_ref, db_ref, h_ref, dh_ref, gpre_ref, wg_hbm, wu_hbm, dhin_ref, dg_ref, wg_v, wu_v, sems):
        _load_weights([(wg_hbm, wg_v), (wu_hbm, wu_v)], sems)
        dn = jnp.zeros((tm, D), F32)
        for k in range(ns):
            dn = dn + _mm_nt(da_ref[:, k * fs : (k + 1) * fs], wg_v[k]) + _mm_nt(db_ref[:, k * fs : (k + 1) * fs], wu_v[k])
        dx, dg = _rms_bwd(h_ref[...], gpre_ref[...], dn)
        _acc(dg_ref, dg, pl.program_id(0) == 0)
        dhin_ref[...] = dh_ref[...] + dx

    return pl.pallas_call(
        body,
        name=name,
        grid=(T // tm,),
        out_shape=[jax.ShapeDtypeStruct((T, D), F32), jax.ShapeDtypeStruct((1, D), F32)],
        in_specs=[_row_spec(tm, ns * fs), _row_spec(tm, ns * fs), _row_spec(tm, D), _row_spec(tm, D), _full_spec((1, D)), ANY, ANY],
        out_specs=[_row_spec(tm, D), _full_spec((1, D))],
        scratch_shapes=[pltpu.VMEM(wg.shape, wg.dtype), pltpu.VMEM(wu.shape, wu.dtype), pltpu.SemaphoreType.DMA((2,))],
        compiler_params=_params(("arbitrary",)),
    )(da, db, h, dh, g_pre, wg, wu)


def _token_tile(T):
    for t in (912, 864, 432):
        if T % t == 0:
            return t
    raise ValueError(f"no token tile for {T} rows")


def _tn_matmul(xm, ym, ms, ns, name):
    T = xm.shape[0]
    mo, no = xm.shape[1] // ms, ym.shape[1] // ns
    tk = _token_tile(T)

    def body(x_ref, y_ref, o_ref):
        _acc(o_ref, _mm_tn(x_ref[...], y_ref[...]), pl.program_id(2) == 0)

    return pl.pallas_call(
        body,
        name=name,
        grid=(ms, ns, T // tk),
        out_shape=jax.ShapeDtypeStruct((ms, ns, mo, no), F32),
        in_specs=[pl.BlockSpec((tk, mo), lambda i, j, k: (k, i)), pl.BlockSpec((tk, no), lambda i, j, k: (k, j))],
        out_specs=pl.BlockSpec((None, None, mo, no), lambda i, j, k: (i, j, 0, 0)),
        compiler_params=_params(("arbitrary", "arbitrary", "arbitrary")),
    )(xm, ym)


TAB_A, TAB_AS1, TAB_AS2, TAB_AS4, TAB_JF, TAB_JB = 0, 2, 4, 6, 8, 10


def _scan_inplace(zr, zi, tabs, pows, car_r, car_i, seg, reverse):
    n_slabs = zr.shape[0]
    sgn = -1.0 if reverse else 1.0
    row = lax.broadcasted_iota(jnp.int32, (SUBLANES, LANES), 0)

    def cmul(pr, pi, xr, xi):
        return pr * xr - pi * xi, pr * xi + pi * xr

    for k0 in range(0, n_slabs, SLAB_GROUP):
        slabs = range(k0, min(k0 + SLAB_GROUP, n_slabs))
        ar = [tabs[TAB_A, k] for k in slabs]
        ai = [sgn * tabs[TAB_A + 1, k] for k in slabs]

        def first_pass(t, carry):
            r = (seg - 1 - t) if reverse else t
            out = []
            for q, k in enumerate(slabs):
                xr, xi = carry[2 * q], carry[2 * q + 1]
                pr, pi = cmul(ar[q], ai[q], xr, xi)
                nr = pr + zr[k, pl.ds(r, SUBLANES, stride=seg), :]
                ni = pi + zi[k, pl.ds(r, SUBLANES, stride=seg), :]
                zr[k, pl.ds(r, SUBLANES, stride=seg), :] = nr
                zi[k, pl.ds(r, SUBLANES, stride=seg), :] = ni
                out += [nr, ni]
            return tuple(out)

        ends = lax.fori_loop(0, seg, first_pass, tuple(jnp.zeros((SUBLANES, LANES), F32) for _ in range(2 * len(slabs))))

        incoming = []
        for q, k in enumerate(slabs):
            fr, fi = ends[2 * q], ends[2 * q + 1]
            for d, tab in ((1, TAB_AS1), (2, TAB_AS2), (4, TAB_AS4)):
                shift, keep = (SUBLANES - d, row < SUBLANES - d) if reverse else (d, row >= d)
                sr = jnp.where(keep, pltpu.roll(fr, shift, 0), 0.0)
                si = jnp.where(keep, pltpu.roll(fi, shift, 0), 0.0)
                pr, pi = cmul(tabs[tab, k], sgn * tabs[tab + 1, k], sr, si)
                fr, fi = fr + pr, fi + pi
            cr, ci = car_r[k], car_i[k]
            jtab = TAB_JB if reverse else TAB_JF
            pr, pi = cmul(tabs[jtab, k], sgn * tabs[jtab + 1, k], cr, ci)
            er, ei = fr + pr, fi + pi
            if reverse:
                inr = jnp.where(row < SUBLANES - 1, pltpu.roll(er, SUBLANES - 1, 0), cr)
                ini = jnp.where(row < SUBLANES - 1, pltpu.roll(ei, SUBLANES - 1, 0), ci)
                car_r[k] = jnp.broadcast_to(er[0:1, :], (SUBLANES, LANES))
                car_i[k] = jnp.broadcast_to(ei[0:1, :], (SUBLANES, LANES))
            else:
                inr = jnp.where(row >= 1, pltpu.roll(er, 1, 0), cr)
                ini = jnp.where(row >= 1, pltpu.roll(ei, 1, 0), ci)
                car_r[k] = jnp.broadcast_to(er[SUBLANES - 1 : SUBLANES, :], (SUBLANES, LANES))
                car_i[k] = jnp.broadcast_to(ei[SUBLANES - 1 : SUBLANES, :], (SUBLANES, LANES))
            incoming += [inr, ini]

        def second_pass(r, _):
            p = (seg - 1 - r) if reverse else r
            for q, k in enumerate(slabs):
                pr, pi = cmul(pows[0, k, p], sgn * pows[1, k, p], incoming[2 * q], incoming[2 * q + 1])
                zr[k, pl.ds(r, SUBLANES, stride=seg), :] = zr[k, pl.ds(r, SUBLANES, stride=seg), :] + pr
                zi[k, pl.ds(r, SUBLANES, stride=seg), :] = zi[k, pl.ds(r, SUBLANES, stride=seg), :] + pi
            return 0

        lax.fori_loop(0, seg, second_pass, 0)


def _slabs_to_cols(ref, k0, n):
    return jnp.concatenate([ref[k0 + q] for q in range(n)], axis=1)


def _window_sum(ext, doublings, forward):
    rows = ext.shape[0]
    s = ext
    for k in range(doublings):
        s = s + pltpu.roll(s, (1 << k) if forward else rows - (1 << k), 0)
    return s


def _mix_fwd(h1, g_pre, g_so, g_po, g_post, dskip, pscale, win, wout, bbre, bbim, ccre, ccim, wgv, wgg, pw, tabs, pows, name):
    T, D = h1.shape
    W = D // 2
    tm = MIX_TILE
    seg = tm // SUBLANES
    n_slabs = tabs.shape[1]
    nch, cch, sch = bbre.shape
    spc = sch // LANES
    pg = W // len(POOL_WINDOWS)

    def body(h_ref, gpre_ref, gso_ref, gpo_ref, gpost_ref, dskip_ref, pscale_ref, win_ref, wout_ref, bbre_ref, bbim_ref, ccre_ref, ccim_ref, wgv_ref, wgg_ref, pw_ref, tabs_ref, pows_ref,
             proj_ref, xr_ref, xi_ref, y_ref, pooled_ref, mixed_ref, h2_ref, n2_ref, cat_ref, car_r, car_i, halo):
        i = pl.program_id(0)

        @pl.when(i == 0)
        def _():
            car_r[...] = jnp.zeros_like(car_r)
            car_i[...] = jnp.zeros_like(car_i)
            halo[...] = jnp.zeros_like(halo)

        hh = h_ref[...]
        n2 = (hh * _rms_stat(hh) * gpre_ref[...]).astype(MXU_DTYPE)
        n2_ref[...] = n2.astype(n2_ref.dtype)
        proj = _mm(n2, win_ref[...])
        proj_ref[...] = proj
        us, up = proj[:, :W], proj[:, W:]

        for c in range(nch):
            uc = us[:, c * cch : (c + 1) * cch].astype(MXU_DTYPE)
            bur, bui = _mm(uc, bbre_ref[c]), _mm(uc, bbim_ref[c])
            for q in range(spc):
                xr_ref[c * spc + q] = bur[:, q * LANES : (q + 1) * LANES]
                xi_ref[c * spc + q] = bui[:, q * LANES : (q + 1) * LANES]
        _scan_inplace(xr_ref, xi_ref, tabs_ref, pows_ref, car_r, car_i, seg, reverse=False)
        ys = []
        for c in range(nch):
            ys.append(_mm(_slabs_to_cols(xr_ref, c * spc, spc), ccre_ref[c]) - _mm(_slabs_to_cols(xi_ref, c * spc, spc), ccim_ref[c]))
        y = jnp.concatenate(ys, axis=1) + dskip_ref[...] * us
        y_ref[...] = y
        ge = _gelu(y).astype(MXU_DTYPE)
        zv = jnp.concatenate([_mm(ge[:, c * cch : (c + 1) * cch], wgv_ref[c]) for c in range(nch)], axis=1)
        zg = jnp.concatenate([_mm(ge[:, c * cch : (c + 1) * cch], wgg_ref[c]) for c in range(nch)], axis=1)
        out = zv * _sigmoid(zg)
        cat_s = out * _rms_stat(out) * gso_ref[...]

        ext = jnp.concatenate([halo[...], up], axis=0)
        halo[...] = up[tm - POOL_HALO :, :]
        t1 = (i * tm + 1 + lax.broadcasted_iota(jnp.int32, (tm, pg), 0)).astype(F32)
        pooled, pms = [], []
        for g, w in enumerate(POOL_WINDOWS):
            col = ext[:, g * pg : (g + 1) * pg]
            win_sum = _window_sum(col, g + 1, True)[POOL_HALO:, :]
            pooled_g = win_sum / jnp.minimum(t1, float(w)) - up[:, g * pg : (g + 1) * pg]
            pooled.append(pooled_g)
            pms.append(_mm(pooled_g, pw_ref[g]))
        pooled_ref[...] = jnp.concatenate(pooled, axis=1)
        yp = jnp.concatenate(pms, axis=1) * pscale_ref[...]
        cat_p = yp * _rms_stat(yp) * gpo_ref[...]

        cat = jnp.concatenate([cat_s, cat_p], axis=1).astype(MXU_DTYPE)
        cat_ref[...] = cat.astype(cat_ref.dtype)
        mixed = _mm(cat, wout_ref[...])
        mixed_ref[...] = mixed
        h2_ref[...] = hh + mixed * _rms_stat(mixed) * gpost_ref[...]

    tok = lambda cols, dt=F32: jax.ShapeDtypeStruct((T, cols), dt)
    slab_spec = pl.BlockSpec((n_slabs, tm, LANES), lambda i: (0, i, 0))
    operands = (h1, g_pre, g_so, g_po, g_post, dskip, pscale, win, wout, bbre, bbim, ccre, ccim, wgv, wgg, pw, tabs, pows)
    return pl.pallas_call(
        body,
        name=name,
        grid=(T // tm,),
        out_shape=[tok(D), jax.ShapeDtypeStruct((n_slabs, T, LANES), F32), jax.ShapeDtypeStruct((n_slabs, T, LANES), F32), tok(W), tok(W), tok(D), tok(D), tok(D, MXU_DTYPE), tok(D, MXU_DTYPE)],
        in_specs=[_row_spec(tm, D)] + [_full_spec(o.shape) for o in operands[1:]],
        out_specs=[_row_spec(tm, D), slab_spec, slab_spec, _row_spec(tm, W), _row_spec(tm, W), _row_spec(tm, D), _row_spec(tm, D), _row_spec(tm, D), _row_spec(tm, D)],
        scratch_shapes=[pltpu.VMEM((n_slabs, SUBLANES, LANES), F32), pltpu.VMEM((n_slabs, SUBLANES, LANES), F32), pltpu.VMEM((POOL_HALO, W), F32)],
        compiler_params=_params(("arbitrary",)),
    )(*operands)


def _mix_bwd_heads(dh2, mixed, y, pooled, proj, g_so, g_po, g_post, dskip, pscale, wout, wgv, wgg, pw, name):
    T, D = dh2.shape
    W = D // 2
    tm = MIX_TILE
    nch, cch, _ = wgv.shape
    ng, pg, _ = pw.shape

    def body(dh2_ref, mixed_ref, y_ref, pooled_ref, us_ref, gso_ref, gpo_ref, gpost_ref, dskip_ref, pscale_ref, wout_ref, wgv_ref, wgg_ref, pw_ref,
             dy_ref, dpooled_ref, dmixed_ref, dgpost_ref, dgso_ref, dgpo_ref, dd_ref, dscale_ref, dwgv_ref, dwgg_ref, dpw_ref):
        first = pl.program_id(0) == 0
        dmixed, dgpost = _rms_bwd(mixed_ref[...], gpost_ref[...], dh2_ref[...])
        _acc(dgpost_ref, dgpost, first)
        dmb = dmixed.astype(MXU_DTYPE)
        dmixed_ref[...] = dmb.astype(dmixed_ref.dtype)
        dcat = _mm_nt(dmb, wout_ref[...])
        dcs, dcp = dcat[:, :W], dcat[:, W:]

        y = y_ref[...]
        ge = _gelu(y).astype(MXU_DTYPE)
        zv = jnp.concatenate([_mm(ge[:, c * cch : (c + 1) * cch], wgv_ref[c]) for c in range(nch)], axis=1)
        zg = jnp.concatenate([_mm(ge[:, c * cch : (c + 1) * cch], wgg_ref[c]) for c in range(nch)], axis=1)
        sg = _sigmoid(zg)
        dout, dgso = _rms_bwd(zv * sg, gso_ref[...], dcs)
        _acc(dgso_ref, dgso, first)
        dzv = (dout * sg).astype(MXU_DTYPE)
        dzg = (dout * zv * sg * (1.0 - sg)).astype(MXU_DTYPE)
        dges = []
        for c in range(nch):
            cs = slice(c * cch, (c + 1) * cch)
            dges.append(_mm_nt(dzv[:, cs], wgv_ref[c]) + _mm_nt(dzg[:, cs], wgg_ref[c]))
            _acc(dwgv_ref.at[c], _mm_tn(ge[:, cs], dzv[:, cs]), first)
            _acc(dwgg_ref.at[c], _mm_tn(ge[:, cs], dzg[:, cs]), first)
        dy = jnp.concatenate(dges, axis=1) * _gelu_grad(y)
        dy_ref[...] = dy
        _acc(dd_ref, jnp.sum(dy * us_ref[...], axis=0, keepdims=True), first)

        pooled_b = pooled_ref[...].astype(MXU_DTYPE)
        pm = jnp.concatenate([_mm(pooled_b[:, g * pg : (g + 1) * pg], pw_ref[g]) for g in range(ng)], axis=1)
        dyp, dgpo = _rms_bwd(pm * pscale_ref[...], gpo_ref[...], dcp)
        _acc(dgpo_ref, dgpo, first)
        _acc(dscale_ref, jnp.sum(dyp * pm, axis=0, keepdims=True), first)
        dpm = (dyp * pscale_ref[...]).astype(MXU_DTYPE)
        dps = []
        for g in range(ng):
            gs = slice(g * pg, (g + 1) * pg)
            dps.append(_mm_nt(dpm[:, gs], pw_ref[g]))
            _acc(dpw_ref.at[g], _mm_tn(pooled_b[:, gs], dpm[:, gs]), first)
        dpooled_ref[...] = jnp.concatenate(dps, axis=1)

    vec = lambda n: jax.ShapeDtypeStruct((1, n), F32)
    operands = (dh2, mixed, y, pooled, proj, g_so, g_po, g_post, dskip, pscale, wout, wgv, wgg, pw)
    return pl.pallas_call(
        body,
        name=name,
        grid=(T // tm,),
        out_shape=[jax.ShapeDtypeStruct((T, W), F32), jax.ShapeDtypeStruct((T, W), F32), jax.ShapeDtypeStruct((T, D), MXU_DTYPE), vec(D), vec(W), vec(W), vec(W), vec(W),
                   jax.ShapeDtypeStruct(wgv.shape, F32), jax.ShapeDtypeStruct(wgg.shape, F32), jax.ShapeDtypeStruct(pw.shape, F32)],
        in_specs=[_row_spec(tm, D), _row_spec(tm, D), _row_spec(tm, W), _row_spec(tm, W), _row_spec(tm, W)] + [_full_spec(o.shape) for o in operands[5:]],
        out_specs=[_row_spec(tm, W), _row_spec(tm, W), _row_spec(tm, D), _full_spec((1, D)), _full_spec((1, W)), _full_spec((1, W)), _full_spec((1, W)), _full_spec((1, W)),
                   _full_spec(wgv.shape), _full_spec(wgg.shape), _full_spec(pw.shape)],
        compiler_params=_params(("arbitrary",)),
    )(*operands)


def _mix_bwd_scan(dy, dpooled, xr, xi, proj, h1, dh2, g_pre, dskip, win, bbre, bbim, ccre, ccim, tabs, pows, name):
    T, D = h1.shape
    W = D // 2
    tm = MIX_TILE
    seg = tm // SUBLANES
    nt = T // tm
    n_slabs = tabs.shape[1]
    nch, cch, sch = bbre.shape
    spc = sch // LANES
    pg = W // len(POOL_WINDOWS)
    blocks_per_tile = tm // SUBLANES

    def body(dy_ref, dp_ref, xr_ref, xi_ref, xpr_ref, xpi_ref, proj_ref, h_ref, dh2_ref, gpre_ref, dskip_ref, win_ref, bbre_ref, bbim_ref, ccre_ref, ccim_ref, tabs_ref, pows_ref,
             dh1_ref, dproj_ref, dgpre_ref, dccre_ref, dccim_ref, dbbre_ref, dbbim_ref, dar_ref, dai_ref, lr, li, car_r, car_i, halo):
        i = pl.program_id(0)
        first = i == 0
        tile = nt - 1 - i
        row = lax.broadcasted_iota(jnp.int32, (SUBLANES, LANES), 0)

        @pl.when(first)
        def _():
            car_r[...] = jnp.zeros_like(car_r)
            car_i[...] = jnp.zeros_like(car_i)
            halo[...] = jnp.zeros_like(halo)
            dar_ref[...] = jnp.zeros_like(dar_ref)
            dai_ref[...] = jnp.zeros_like(dai_ref)

        dy = dy_ref[...]
        for c in range(nch):
            dyc = dy[:, c * cch : (c + 1) * cch].astype(MXU_DTYPE)
            gr, gi = _mm_nt(dyc, ccre_ref[c]), _mm_nt(dyc, ccim_ref[c])
            for q in range(spc):
                lr[c * spc + q] = gr[:, q * LANES : (q + 1) * LANES]
                li[c * spc + q] = -gi[:, q * LANES : (q + 1) * LANES]
            _acc(dccre_ref.at[c], _mm_tn(_slabs_to_cols(xr_ref, c * spc, spc), dyc), first)
            _acc(dccim_ref.at[c], -_mm_tn(_slabs_to_cols(xi_ref, c * spc, spc), dyc), first)
        _scan_inplace(lr, li, tabs_ref, pows_ref, car_r, car_i, seg, reverse=True)

        for k in range(n_slabs):
            prev_r = jnp.where(tile > 0, jnp.broadcast_to(xpr_ref[k, SUBLANES - 1 : SUBLANES, :], (SUBLANES, LANES)), 0.0)
            prev_i = jnp.where(tile > 0, jnp.broadcast_to(xpi_ref[k, SUBLANES - 1 : SUBLANES, :], (SUBLANES, LANES)), 0.0)
            x0r = jnp.where(row >= 1, pltpu.roll(xr_ref[k, pl.ds(seg - 1, SUBLANES, stride=seg), :], 1, 0), prev_r)
            x0i = jnp.where(row >= 1, pltpu.roll(xi_ref[k, pl.ds(seg - 1, SUBLANES, stride=seg), :], 1, 0), prev_i)
            l0r, l0i = lr[k, pl.ds(0, SUBLANES, stride=seg), :], li[k, pl.ds(0, SUBLANES, stride=seg), :]

            def step(r, acc, k=k):
                ar_, ai_ = acc
                pr_, pi_ = xr_ref[k, pl.ds(r - 1, SUBLANES, stride=seg), :], xi_ref[k, pl.ds(r - 1, SUBLANES, stride=seg), :]
                lr_, li_ = lr[k, pl.ds(r, SUBLANES, stride=seg), :], li[k, pl.ds(r, SUBLANES, stride=seg), :]
                return ar_ + lr_ * pr_ + li_ * pi_, ai_ + li_ * pr_ - lr_ * pi_

            ar_, ai_ = lax.fori_loop(1, seg, step, (l0r * x0r + l0i * x0i, l0i * x0r - l0r * x0i))
            dar_ref[k] += ar_
            dai_ref[k] += ai_

        us = proj_ref[:, :W]
        dus = []
        for c in range(nch):
            lrc, lic = _slabs_to_cols(lr, c * spc, spc).astype(MXU_DTYPE), _slabs_to_cols(li, c * spc, spc).astype(MXU_DTYPE)
            uc = us[:, c * cch : (c + 1) * cch].astype(MXU_DTYPE)
            _acc(dbbre_ref.at[c], _mm_tn(uc, lrc), first)
            _acc(dbbim_ref.at[c], _mm_tn(uc, lic), first)
            dus.append(_mm_nt(lrc, bbre_ref[c]) + _mm_nt(lic, bbim_ref[c]))
        du_s = jnp.concatenate(dus, axis=1) + dskip_ref[...] * dy

        dp = dp_ref[...]
        t1 = (tile * tm + 1 + lax.broadcasted_iota(jnp.int32, (tm, pg), 0)).astype(F32)
        dups, heads = [], []
        for g, w in enumerate(POOL_WINDOWS):
            dpg = dp[:, g * pg : (g + 1) * pg]
            qg = dpg / jnp.minimum(t1, float(w))
            ext = jnp.concatenate([qg, halo[:, g * pg : (g + 1) * pg]], axis=0)
            dups.append(_window_sum(ext, g + 1, False)[:tm, :] - dpg)
            heads.append(qg[:POOL_HALO, :])
        halo[...] = jnp.concatenate(heads, axis=1)
        dproj = jnp.concatenate([du_s] + dups, axis=1).astype(MXU_DTYPE)
        dproj_ref[...] = dproj.astype(dproj_ref.dtype)
        dx, dg = _rms_bwd(h_ref[...], gpre_ref[...], _mm_nt(dproj, win_ref[...]))
        _acc(dgpre_ref, dg, first)
        dh1_ref[...] = dh2_ref[...] + dx

    rev = lambda cols: _row_spec(tm, cols, rev_n=nt)
    slab_spec = pl.BlockSpec((n_slabs, tm, LANES), lambda i: (0, nt - 1 - i, 0))
    prev_spec = pl.BlockSpec((n_slabs, SUBLANES, LANES), lambda i: (0, jnp.maximum((nt - 1 - i) * blocks_per_tile - 1, 0), 0))
    consts = (g_pre, dskip, win, bbre, bbim, ccre, ccim, tabs, pows)
    return pl.pallas_call(
        body,
        name=name,
        grid=(nt,),
        out_shape=[jax.ShapeDtypeStruct((T, D), F32), jax.ShapeDtypeStruct((T, D), MXU_DTYPE), jax.ShapeDtypeStruct((1, D), F32),
                   jax.ShapeDtypeStruct(ccre.shape, F32), jax.ShapeDtypeStruct(ccim.shape, F32), jax.ShapeDtypeStruct(bbre.shape, F32), jax.ShapeDtypeStruct(bbim.shape, F32),
                   jax.ShapeDtypeStruct((n_slabs, SUBLANES, LANES), F32), jax.ShapeDtypeStruct((n_slabs, SUBLANES, LANES), F32)],
        in_specs=[rev(W), rev(W), slab_spec, slab_spec, prev_spec, prev_spec, rev(D), rev(D), rev(D)] + [_full_spec(o.shape) for o in consts],
        out_specs=[rev(D), rev(D), _full_spec((1, D)), _full_spec(ccre.shape), _full_spec(ccim.shape), _full_spec(bbre.shape), _full_spec(bbim.shape),
                   _full_spec((n_slabs, SUBLANES, LANES)), _full_spec((n_slabs, SUBLANES, LANES))],
        scratch_shapes=[pltpu.VMEM((n_slabs, tm, LANES), F32), pltpu.VMEM((n_slabs, tm, LANES), F32), pltpu.VMEM((n_slabs, SUBLANES, LANES), F32), pltpu.VMEM((n_slabs, SUBLANES, LANES), F32),
                        pltpu.VMEM((POOL_HALO, W), F32)],
        compiler_params=_params(("arbitrary",)),
    )(dy, dpooled, xr, xi, xr, xi, proj, h1, dh2, *consts)


def _loss_grad(h3, target, name):
    T, D = h3.shape
    tm = FFN_TILE

    def body(h_ref, t_ref, dy_ref, sq_ref):
        i = pl.program_id(0)
        rows = i * tm + lax.broadcasted_iota(jnp.int32, (tm, D), 0)
        err = jnp.where(rows >= N_META, h_ref[...] - t_ref[...], 0.0)
        dy_ref[...] = err * (1.0 / D)
        _acc(sq_ref, jnp.sum(err * err, axis=0, keepdims=True), i == 0)

    return pl.pallas_call(
        body,
        name=name,
        grid=(T // tm,),
        out_shape=[jax.ShapeDtypeStruct((T, D), F32), jax.ShapeDtypeStruct((1, D), F32)],
        in_specs=[_row_spec(tm, D), _row_spec(tm, D)],
        out_specs=[_row_spec(tm, D), _full_spec((1, D))],
        compiler_params=_params(("arbitrary",)),
    )(h3, target)


def _discretize(lam_re, lam_im, log_dt, b_re, b_im):
    dt = jnp.exp(log_dt)[:, None]
    decay = jnp.exp(lam_re * dt)
    ang = lam_im * dt
    a_re, a_im = decay * jnp.cos(ang), decay * jnp.sin(ang)
    nr = a_re - 1.0
    den = lam_re * lam_re + lam_im * lam_im
    q_re = (nr * lam_re + a_im * lam_im) / den
    q_im = (a_im * lam_re - nr * lam_im) / den
    bb_re = q_re[..., None] * b_re - q_im[..., None] * b_im
    bb_im = q_re[..., None] * b_im + q_im[..., None] * b_re
    return a_re, a_im, bb_re, bb_im


GROUPS_PER_CHUNK = 16


def _block_diag(w, rows_first):
    G = w.shape[0]
    nch = G // GROUPS_PER_CHUNK
    if not rows_first:
        w = jnp.swapaxes(w, 1, 2)
    p, q = w.shape[1], w.shape[2]
    eye = jnp.eye(GROUPS_PER_CHUNK, dtype=w.dtype)
    out = jnp.einsum("cgpq,gk->cgpkq", w.reshape(nch, GROUPS_PER_CHUNK, p, q), eye)
    return out.reshape(nch, GROUPS_PER_CHUNK * p, GROUPS_PER_CHUNK * q)


def _block_diag_extract(m, p, q, rows_first):
    nch = m.shape[0]
    eye = jnp.eye(GROUPS_PER_CHUNK, dtype=m.dtype)
    out = jnp.einsum("cgpkq,gk->cgpq", m.reshape(nch, GROUPS_PER_CHUNK, p, GROUPS_PER_CHUNK, q), eye).reshape(nch * GROUPS_PER_CHUNK, p, q)
    return out if rows_first else jnp.swapaxes(out, 1, 2)


def _cmul(ar, ai, br, bi):
    return ar * br - ai * bi, ar * bi + ai * br


def _scan_tables(a_re, a_im, seg):
    n = a_re.size
    ar, ai = a_re.reshape(n), a_im.reshape(n)
    pr, pi = [ar], [ai]
    for _ in range(seg - 1):
        nr, ni = _cmul(pr[-1], pi[-1], ar, ai)
        pr.append(nr)
        pi.append(ni)
    s1 = (pr[-1], pi[-1])
    s2 = _cmul(*s1, *s1)
    s4 = _cmul(*s2, *s2)
    jr, ji = [s1[0]], [s1[1]]
    for _ in range(SUBLANES - 1):
        nr, ni = _cmul(jr[-1], ji[-1], *s1)
        jr.append(nr)
        ji.append(ni)

    def bcast(v):
        return jnp.broadcast_to(v.reshape(n // LANES, 1, LANES), (n // LANES, SUBLANES, LANES))

    def per_sublane(vs):
        return jnp.stack([v.reshape(n // LANES, LANES) for v in vs], axis=1)

    tabs = jnp.stack([bcast(ar), bcast(ai), bcast(s1[0]), bcast(s1[1]), bcast(s2[0]), bcast(s2[1]), bcast(s4[0]), bcast(s4[1]),
                      per_sublane(jr), per_sublane(ji), per_sublane(jr[::-1]), per_sublane(ji[::-1])])
    pows = jnp.stack([jnp.stack([bcast(v) for v in pr], axis=1), jnp.stack([bcast(v) for v in pi], axis=1)])
    return tabs, pows


SMALL = ("ffn1_pre_norm", "ffn1_post_norm", "mix_pre_norm", "mix_post_norm", "ssm_lambda_re", "ssm_lambda_im", "ssm_log_dt", "ssm_b_re", "ssm_b_im", "ssm_c_re", "ssm_c_im",
         "ssm_d", "ssm_w_glu", "pool_w", "pool_scale", "ssm_out_norm", "pool_out_norm", "ffn2_pre_norm", "ffn2_post_norm")
BIG = ("ffn1_w_gate", "ffn1_w_up", "ffn1_w_down", "w_in", "w_out", "ffn2_w_gate", "ffn2_w_up", "ffn2_w_down")
ORDER = ("meta_tokens", "ffn1_pre_norm", "ffn1_post_norm", "ffn1_w_gate", "ffn1_w_up", "ffn1_w_down", "mix_pre_norm", "mix_post_norm", "w_in", "ssm_lambda_re", "ssm_lambda_im",
         "ssm_log_dt", "ssm_b_re", "ssm_b_im", "ssm_c_re", "ssm_c_im", "ssm_d", "ssm_w_glu", "pool_w", "pool_scale", "ssm_out_norm", "pool_out_norm", "w_out", "ffn2_pre_norm",
         "ffn2_post_norm", "ffn2_w_gate", "ffn2_w_up", "ffn2_w_down")
PACK_ROWS = SUBLANES * 8


FFN_CHUNKS = 2


def _shards_to_chunks(w):
    ns, d, fs = w.shape
    per = ns // FFN_CHUNKS
    return jnp.transpose(w.reshape(FFN_CHUNKS, per, d, fs), (0, 2, 1, 3)).reshape(FFN_CHUNKS, d, per * fs)


def _chunks_to_shards(w, ns):
    nc, d, fc = w.shape
    per = ns // nc
    return jnp.transpose(w.reshape(nc, d, per, fc // per), (0, 2, 1, 3)).reshape(ns, d, fc // per)


def _pack(arrays, rows):
    flat = jnp.concatenate([a.reshape(-1) for a in arrays])
    return jnp.pad(flat, (0, rows * LANES - flat.size)).reshape(rows, LANES)


def _unpack(packed, shapes):
    flat = packed.reshape(-1)
    out, off = [], 0
    for s in shapes:
        n = math.prod(s)
        out.append(flat[off : off + n].reshape(s))
        off += n
    return out


def _step(p, x, loss_target, m, v):
    D = x.shape[-1]
    xi_, yi_, ci_ = lax.axis_index("x"), lax.axis_index("y"), lax.axis_index("c")
    chip = (2 * xi_ + yi_).astype(jnp.int32).reshape(1)
    core = ci_.astype(jnp.int32).reshape(1)

    def halves(w):
        return w.reshape(2, w.shape[0] // 2, w.shape[1])

    shard = {n: p[n][0] for n in BIG}
    gathered = _gather_xy([halves(shard[n].astype(MXU_DTYPE)) for n in BIG] + [halves(p["meta_tokens"])], "gather_weights")
    full = {}
    for n, g in zip(BIG, gathered[:-1]):
        g = g.reshape(4, g.shape[1] * g.shape[2], g.shape[3])
        if n.endswith(("w_gate", "w_up")):
            g = _shards_to_chunks(g)
        elif n.endswith("w_down"):
            g = g.reshape(FFN_CHUNKS, -1, g.shape[2])
        full[n] = g
    meta = jnp.transpose(gathered[-1].reshape(4, N_META, -1), (1, 0, 2)).reshape(N_META, D)
    w_in = full["w_in"].reshape(D, D)
    w_out = full["w_out"].reshape(D, D)

    vec = lambda n: p[n].reshape(1, -1)
    G, N, H = p["ssm_b_re"].shape[1:]
    W = D // 2
    a_re, a_im, bb_re, bb_im = _discretize(p["ssm_lambda_re"][0], p["ssm_lambda_im"][0], p["ssm_log_dt"][0], p["ssm_b_re"][0], p["ssm_b_im"][0])
    tabs, pows = _scan_tables(a_re, a_im, MIX_TILE // SUBLANES)
    bf = lambda a: a.astype(MXU_DTYPE)
    bbre, bbim = bf(_block_diag(bb_re, False)), bf(_block_diag(bb_im, False))
    ccre, ccim = bf(_block_diag(p["ssm_c_re"][0], False)), bf(_block_diag(p["ssm_c_im"][0], False))
    wgv, wgg = bf(_block_diag(p["ssm_w_glu"][0][:, :, :H], True)), bf(_block_diag(p["ssm_w_glu"][0][:, :, H:], True))
    pw = bf(p["pool_w"][0])

    h0 = jnp.concatenate([meta, x[0]], axis=0)
    T = h0.shape[0]
    h1, f1, a1, b1, s1, n1 = _ffn_fwd(h0, vec("ffn1_pre_norm"), vec("ffn1_post_norm"), full["ffn1_w_gate"], full["ffn1_w_up"], full["ffn1_w_down"], "ffn1_fwd")
    proj, xr, xim, y, pooled, mixed, h2, n2, cat = _mix_fwd(
        h1, vec("mix_pre_norm"), vec("ssm_out_norm"), vec("pool_out_norm"), vec("mix_post_norm"), vec("ssm_d"), vec("pool_scale"), w_in, w_out, bbre, bbim, ccre, ccim, wgv, wgg, pw, tabs, pows, "mix_fwd"
    )
    h3, f2, a2, b2, s2, n3 = _ffn_fwd(h2, vec("ffn2_pre_norm"), vec("ffn2_post_norm"), full["ffn2_w_gate"], full["ffn2_w_up"], full["ffn2_w_down"], "ffn2_fwd")
    dh3, sq = _loss_grad(h3, jnp.pad(loss_target[0], ((N_META, 0), (0, 0))), "loss_grad")
    loss = lax.psum(0.5 * jnp.sum(sq) / D, ("x", "y", "c"))

    g = {}

    def ffn_bwd(tag, dh, h, f, a, b, s, n):
        da, db, df, g[tag + "_post_norm"] = _ffn_bwd_down(dh, f, a, b, vec(tag + "_post_norm"), full[tag + "_w_down"], tag + "_bwd_down")
        dh_in, g[tag + "_pre_norm"] = _ffn_bwd_up(da, db, h, dh, vec(tag + "_pre_norm"), full[tag + "_w_gate"], full[tag + "_w_up"], tag + "_bwd_up")
        g[tag + "_w_gate"] = _chunks_to_shards(_tn_matmul(n, da, 1, FFN_CHUNKS, tag + "_dw_gate")[0], 4)
        g[tag + "_w_up"] = _chunks_to_shards(_tn_matmul(n, db, 1, FFN_CHUNKS, tag + "_dw_up")[0], 4)
        g[tag + "_w_down"] = _tn_matmul(s, df, FFN_CHUNKS, 1, tag + "_dw_down")
        return dh_in

    dh2 = ffn_bwd("ffn2", dh3, h2, f2, a2, b2, s2, n3)
    dy, dpooled, dmixed, g["mix_post_norm"], g["ssm_out_norm"], g["pool_out_norm"], g["ssm_d"], g["pool_scale"], dwgv, dwgg, g["pool_w"] = _mix_bwd_heads(
        dh2, mixed, y, pooled, proj, vec("ssm_out_norm"), vec("pool_out_norm"), vec("mix_post_norm"), vec("ssm_d"), vec("pool_scale"), w_out, wgv, wgg, pw, "mix_bwd_heads"
    )
    dh1, dproj, g["mix_pre_norm"], dccre, dccim, dbbre, dbbim, dar, dai = _mix_bwd_scan(
        dy, dpooled, xr, xim, proj, h1, dh2, vec("mix_pre_norm"), vec("ssm_d"), w_in, bbre, bbim, ccre, ccim, tabs, pows, "mix_bwd_scan"
    )
    g["w_in"] = _tn_matmul(n2, dproj, 1, 1, "dw_in")
    g["w_out"] = _tn_matmul(cat, dmixed, 1, 1, "dw_out")
    dh0 = ffn_bwd("ffn1", dh1, h0, f1, a1, b1, s1, n1)
    grad_x = dh0[N_META:][None]

    g["ssm_c_re"] = _block_diag_extract(dccre, N, H, False)
    g["ssm_c_im"] = _block_diag_extract(dccim, N, H, False)
    g["ssm_w_glu"] = jnp.concatenate([_block_diag_extract(dwgv, H, H, True), _block_diag_extract(dwgg, H, H, True)], axis=-1)
    d_a_re, d_a_im = jnp.sum(dar, axis=1).reshape(G, N), jnp.sum(dai, axis=1).reshape(G, N)
    _, pull = jax.vjp(_discretize, p["ssm_lambda_re"][0], p["ssm_lambda_im"][0], p["ssm_log_dt"][0], p["ssm_b_re"][0], p["ssm_b_im"][0])
    g["ssm_lambda_re"], g["ssm_lambda_im"], g["ssm_log_dt"], g["ssm_b_re"], g["ssm_b_im"] = pull(
        (d_a_re, d_a_im, _block_diag_extract(dbbre, H, N, False), _block_diag_extract(dbbim, H, N, False))
    )

    small_shapes = [p[n].shape for n in SMALL] + [(N_META, D)]
    small_size = sum(math.prod(s) for s in small_shapes)
    rows = -(-small_size // (LANES * PACK_ROWS)) * PACK_ROWS
    packed = _pack([g[n] for n in SMALL] + [dh0[:N_META]], rows)
    views = [packed.reshape(4, 2, rows // 8, LANES)]
    for n in BIG:
        r, c = shard[n].shape
        views.append(g[n].reshape(4, 2, r // 2, c))
    got = _sibling_scatter(views, "reduce_sibling")
    parts = [_add_own_half(v, r_, core, f"reduce_add_sibling_{k}") for k, (v, r_) in enumerate(zip(views, got))]
    got = _chip_scatter(parts, "reduce_chips")
    sums = [_add_chips(p_, r_, chip, f"reduce_add_chips_{k}") for k, (p_, r_) in enumerate(zip(parts, got))]
    shared = _sibling_share(sums, "reduce_share")
    small_all = _gather_xy([shared[0]], "gather_small")[0].reshape(rows, LANES)
    grads = dict(zip(SMALL + ("meta_full",), _unpack(small_all, small_shapes)))
    grads["meta_tokens"] = lax.dynamic_slice_in_dim(grads.pop("meta_full"), chip[0] * (D // 4), D // 4, axis=1)
    for n, s in zip(BIG, shared[1:]):
        grads[n] = s.reshape(p[n].shape)

    delta, new_m, new_v = {}, {}, {}
    sm = [_pack([t[n] for n in SMALL], rows) for t in (p, grads, m, v)]
    for out, packed_out in zip((delta, new_m, new_v), _adamw(*sm, "adamw_small")):
        out.update(zip(SMALL, _unpack(packed_out, [p[n].shape for n in SMALL])))
    for n in BIG + ("meta_tokens",):
        shape = p[n].shape
        flat = lambda a: a.reshape(-1, shape[-1])
        d_, m_, v_ = _adamw(flat(p[n]), flat(grads[n]), flat(m[n]), flat(v[n]), "adamw_" + n)
        delta[n], new_m[n], new_v[n] = d_.reshape(shape), m_.reshape(shape), v_.reshape(shape)

    return (loss, grad_x, *[grads[n] for n in ORDER], *[delta[n] for n in ORDER], *[new_m[n] for n in ORDER], *[new_v[n] for n in ORDER])


def kernel(x, meta_tokens, ffn1_pre_norm, ffn1_post_norm, ffn1_w_gate, ffn1_w_up, ffn1_w_down, mix_pre_norm, mix_post_norm, w_in, ssm_lambda_re, ssm_lambda_im, ssm_log_dt, ssm_b_re, ssm_b_im, ssm_c_re, ssm_c_im, ssm_d, ssm_w_glu, pool_w, pool_scale, ssm_out_norm, pool_out_norm, w_out, ffn2_pre_norm, ffn2_post_norm, ffn2_w_gate, ffn2_w_up, ffn2_w_down, loss_target, m_meta_tokens, m_ffn1_pre_norm, m_ffn1_post_norm, m_ffn1_w_gate, m_ffn1_w_up, m_ffn1_w_down, m_mix_pre_norm, m_mix_post_norm, m_w_in, m_ssm_lambda_re, m_ssm_lambda_im, m_ssm_log_dt, m_ssm_b_re, m_ssm_b_im, m_ssm_c_re, m_ssm_c_im, m_ssm_d, m_ssm_w_glu, m_pool_w, m_pool_scale, m_ssm_out_norm, m_pool_out_norm, m_w_out, m_ffn2_pre_norm, m_ffn2_post_norm, m_ffn2_w_gate, m_ffn2_w_up, m_ffn2_w_down, v_meta_tokens, v_ffn1_pre_norm, v_ffn1_post_norm, v_ffn1_w_gate, v_ffn1_w_up, v_ffn1_w_down, v_mix_pre_norm, v_mix_post_norm, v_w_in, v_ssm_lambda_re, v_ssm_lambda_im, v_ssm_log_dt, v_ssm_b_re, v_ssm_b_im, v_ssm_c_re, v_ssm_c_im, v_ssm_d, v_ssm_w_glu, v_pool_w, v_pool_scale, v_ssm_out_norm, v_pool_out_norm, v_w_out, v_ffn2_pre_norm, v_ffn2_post_norm, v_ffn2_w_gate, v_ffn2_w_up, v_ffn2_w_down):
    args = locals()
    p = {n: args[n] for n in ORDER}
    m = {n: args["m_" + n] for n in ORDER}
    v = {n: args["v_" + n] for n in ORDER}
    return _step(p, x, loss_target, m, v)
```

```python
import math

import jax
import jax.numpy as jnp
from jax import lax
from jax.experimental import pallas as pl
from jax.experimental.pallas import tpu as pltpu

F32 = jnp.float32
MXU_DTYPE = jnp.bfloat16
WIRE_DTYPE = jnp.bfloat16

RMS_EPS = 1e-6
N_META = 16
POOL_WINDOWS = (2, 4, 8, 16)
POOL_HALO = 16
ADAM_LR, ADAM_B1, ADAM_B2, ADAM_EPS, ADAM_WD, ADAM_STEP = 0.001, 0.9, 0.999, 1e-08, 0.01, 10

LANES = 128
SUBLANES = 8
VMEM_LIMIT = 60 * 1024 * 1024
FFN_TILE = 432
MIX_TILE = 216
SLAB_GROUP = 8
MESH = pl.DeviceIdType.MESH
ANY = pl.BlockSpec(memory_space=pl.ANY)


def _mm(a, b):
    return jnp.dot(a.astype(MXU_DTYPE), b.astype(MXU_DTYPE), preferred_element_type=F32)


def _mm_nt(a, b):
    return lax.dot_general(a.astype(MXU_DTYPE), b.astype(MXU_DTYPE), (((1,), (1,)), ((), ())), preferred_element_type=F32)


def _mm_tn(a, b):
    return lax.dot_general(a.astype(MXU_DTYPE), b.astype(MXU_DTYPE), (((0,), (0,)), ((), ())), preferred_element_type=F32)


def _rms_stat(x):
    return lax.rsqrt(jnp.mean(x * x, axis=-1, keepdims=True) + RMS_EPS)


def _rms_bwd(x, g, dy):
    r = _rms_stat(x)
    xh = x * r
    dg = jnp.sum(dy * xh, axis=0, keepdims=True)
    dxh = dy * g
    dx = r * (dxh - xh * jnp.mean(dxh * xh, axis=-1, keepdims=True))
    return dx, dg


def _sigmoid(x):
    return 1.0 / (1.0 + jnp.exp(-x))


GELU_C = math.sqrt(2.0 / math.pi)
GELU_K = 0.044715


def _gelu(y):
    return 0.5 * y * (1.0 + jnp.tanh(GELU_C * (y + GELU_K * y * y * y)))


def _gelu_grad(y):
    th = jnp.tanh(GELU_C * (y + GELU_K * y * y * y))
    return 0.5 * (1.0 + th) + 0.5 * y * (1.0 - th * th) * GELU_C * (1.0 + 3.0 * GELU_K * y * y)


def _row_spec(tile, cols, rev_n=None):
    if rev_n is None:
        return pl.BlockSpec((tile, cols), lambda i: (i, 0))
    return pl.BlockSpec((tile, cols), lambda i: (rev_n - 1 - i, 0))


def _full_spec(shape):
    zeros = (0,) * len(shape)
    return pl.BlockSpec(shape, lambda *_: zeros)


def _acc(ref, val, first):
    @pl.when(first)
    def _():
        ref[...] = val

    @pl.when(jnp.logical_not(first))
    def _():
        ref[...] += val


def _place():
    x, y, c = lax.axis_index("x"), lax.axis_index("y"), lax.axis_index("c")
    others = [(1 - x, y), (x, 1 - y), (1 - x, 1 - y)]
    return x, y, c, others


class _Exchange:
    mid_step = None

    def __init__(self, ins, out_shapes, aliases, n_sems):
        self.ins, self.out_shapes, self.aliases, self.n_sems = list(ins), list(out_shapes), dict(aliases), n_sems

    def mid(self, ins, outs, send_sems, recv_sems):
        pass


class _SiblingScatter(_Exchange):
    def __init__(self, views):
        super().__init__(views, [jax.ShapeDtypeStruct((4,) + v.shape[2:], v.dtype) for v in views], {}, 4 * len(views))

    def _copies(self, ins, outs, send_sems, recv_sems):
        x, y, c, _ = _place()
        return [
            pltpu.make_async_remote_copy(src_ref=ins[a].at[k, 1 - c], dst_ref=outs[a].at[k], send_sem=send_sems.at[4 * a + k], recv_sem=recv_sems.at[4 * a + k], device_id=(x, y, 1 - c), device_id_type=MESH)
            for a in range(len(ins))
            for k in range(4)
        ]

    def start(self, *refs):
        for cp in self._copies(*refs):
            cp.start()

    def finish(self, *refs):
        cps = self._copies(*refs)
        for cp in cps:
            cp.wait_recv()
        for cp in cps:
            cp.wait_send()


class _ChipScatter(_Exchange):
    def __init__(self, parts):
        super().__init__(parts, [jax.ShapeDtypeStruct((3,) + p.shape[1:], p.dtype) for p in parts], {}, 3 * len(parts))

    def _copies(self, ins, outs, send_sems, recv_sems):
        x, y, c, others = _place()
        return [
            pltpu.make_async_remote_copy(src_ref=ins[a].at[2 * chip[0] + chip[1]], dst_ref=outs[a].at[j], send_sem=send_sems.at[3 * a + j], recv_sem=recv_sems.at[3 * a + j], device_id=(*chip, c), device_id_type=MESH)
            for a in range(len(ins))
            for j, chip in enumerate(others)
        ]

    start = _SiblingScatter.start
    finish = _SiblingScatter.finish


class _SiblingShare(_Exchange):
    def __init__(self, bufs):
        super().__init__(bufs, [jax.ShapeDtypeStruct(b.shape, b.dtype) for b in bufs], {a: a for a in range(len(bufs))}, len(bufs))

    def _copy(self, outs, send_sems, recv_sems, a, half):
        x, y, c, _ = _place()
        mine = outs[a].at[c if half == "mine" else 1 - c]
        return pltpu.make_async_remote_copy(src_ref=mine, dst_ref=mine, send_sem=send_sems.at[a], recv_sem=recv_sems.at[a], device_id=(x, y, 1 - c), device_id_type=MESH)

    def start(self, ins, outs, send_sems, recv_sems):
        for a in range(len(outs)):
            self._copy(outs, send_sems, recv_sems, a, "mine").start()

    def finish(self, ins, outs, send_sems, recv_sems):
        for a in range(len(outs)):
            self._copy(outs, send_sems, recv_sems, a, "theirs").wait_recv()
        for a in range(len(outs)):
            self._copy(outs, send_sems, recv_sems, a, "mine").wait_send()


class _Gather(_Exchange):
    def __init__(self, bufs, mid_step=None):
        super().__init__(bufs, [jax.ShapeDtypeStruct(b.shape, b.dtype) for b in bufs], {a: a for a in range(len(bufs))}, 6 * len(bufs))
        self.mid_step = mid_step

    def _copy(self, outs, send_sems, recv_sems, a, j, chip, half, to):
        blk = outs[a].at[2 * chip[0] + chip[1], half]
        return pltpu.make_async_remote_copy(src_ref=blk, dst_ref=blk, send_sem=send_sems.at[6 * a + j], recv_sem=recv_sems.at[6 * a + j], device_id=to, device_id_type=MESH)

    def start(self, ins, outs, send_sems, recv_sems):
        x, y, c, others = _place()
        for a in range(len(outs)):
            for j, chip in enumerate(others):
                self._copy(outs, send_sems, recv_sems, a, j, (x, y), c, (*chip, c)).start()

    def mid(self, ins, outs, send_sems, recv_sems):
        x, y, c, others = _place()
        for a in range(len(outs)):
            for j, chip in enumerate(others):
                self._copy(outs, send_sems, recv_sems, a, j, chip, c, (x, y, c)).wait_recv()
                self._copy(outs, send_sems, recv_sems, a, 3 + j, chip, c, (x, y, 1 - c)).start()

    def finish(self, ins, outs, send_sems, recv_sems):
        x, y, c, others = _place()
        for a in range(len(outs)):
            for j, chip in enumerate(others):
                self._copy(outs, send_sems, recv_sems, a, 3 + j, chip, 1 - c, (x, y, c)).wait_recv()
        for a in range(len(outs)):
            for j, chip in enumerate(others):
                self._copy(outs, send_sems, recv_sems, a, j, (x, y), c, (*chip, c)).wait_send()
                self._copy(outs, send_sems, recv_sems, a, 3 + j, chip, c, (x, y, 1 - c)).wait_send()


def _exchange_call(ex, name):
    n, m = len(ex.ins), len(ex.out_shapes)

    def body(*refs):
        parts = (refs[:n], refs[n : n + m], refs[n + m], refs[n + m + 1])
        ex.start(*parts)
        ex.mid(*parts)
        ex.finish(*parts)

    return pl.pallas_call(
        body,
        name=name,
        out_shape=ex.out_shapes,
        in_specs=[ANY] * n,
        out_specs=[ANY] * m,
        scratch_shapes=[pltpu.SemaphoreType.DMA((ex.n_sems,)), pltpu.SemaphoreType.DMA((ex.n_sems,))],
        input_output_aliases=ex.aliases,
    )(*ex.ins)


def _pallas(body, *, name, grid, in_specs, out_specs, out_shape, operands, scratch_shapes=(), exchange=None):
    params = pltpu.CompilerParams(dimension_semantics=("arbitrary",) * len(grid), vmem_limit_bytes=VMEM_LIMIT)
    if exchange is None:
        outs = pl.pallas_call(body, name=name, grid=grid, in_specs=in_specs, out_specs=out_specs, out_shape=out_shape, scratch_shapes=list(scratch_shapes), compiler_params=params)(*operands)
        return outs, []
    ex = exchange
    n_in, n_out, n_scr = len(in_specs), len(out_specs), len(scratch_shapes)
    x_in, x_out = len(ex.ins), len(ex.out_shapes)

    def hosted(*refs):
        ins, x_ins = refs[:n_in], refs[n_in : n_in + x_in]
        outs, x_outs = refs[n_in + x_in : n_in + x_in + n_out], refs[n_in + x_in + n_out : n_in + x_in + n_out + x_out]
        rest = refs[n_in + x_in + n_out + x_out :]
        parts = (x_ins, x_outs, rest[n_scr], rest[n_scr + 1])
        ids = [pl.program_id(d) for d in range(len(grid))]
        first = functools_reduce_and([i == 0 for i in ids])
        last = functools_reduce_and([i == g - 1 for i, g in zip(ids, grid)])

        @pl.when(first)
        def _():
            ex.start(*parts)

        body(*ins, *outs, *rest[:n_scr])

        if ex.mid_step is not None:

            @pl.when(ids[0] == ex.mid_step)
            def _():
                ex.mid(*parts)

        @pl.when(last)
        def _():
            ex.finish(*parts)

    outs = pl.pallas_call(
        hosted,
        name=name,
        grid=grid,
        in_specs=list(in_specs) + [ANY] * x_in,
        out_specs=list(out_specs) + [ANY] * x_out,
        out_shape=list(out_shape) + ex.out_shapes,
        scratch_shapes=list(scratch_shapes) + [pltpu.SemaphoreType.DMA((ex.n_sems,)), pltpu.SemaphoreType.DMA((ex.n_sems,))],
        input_output_aliases={n_in + i: n_out + o for i, o in ex.aliases.items()},
        compiler_params=params,
    )(*operands, *ex.ins)
    return outs[:n_out], outs[n_out:]


def functools_reduce_and(conds):
    out = conds[0]
    for c in conds[1:]:
        out = jnp.logical_and(out, c)
    return out


def _row_tile(rows):
    for t in (512, 352, 256, 176, 128, 112, 64, 32, 16, 8):
        if rows % t == 0:
            return t
    return rows


def _add_own_half(view, got, place, name):
    _, _, r, c = view.shape
    tr = _row_tile(r)

    def body(place_ref, v_ref, g_ref, o_ref, w_ref):
        s = v_ref[...] + g_ref[...]
        o_ref[...] = s
        w_ref[...] = s.astype(w_ref.dtype)

    blk = pl.BlockSpec((None, tr, c), lambda k, i, pr: (k, i, 0))
    return pl.pallas_call(
        body,
        name=name,
        out_shape=[jax.ShapeDtypeStruct((4, r, c), F32), jax.ShapeDtypeStruct((4, r, c), WIRE_DTYPE)],
        grid_spec=pltpu.PrefetchScalarGridSpec(
            num_scalar_prefetch=1, grid=(4, r // tr), in_specs=[pl.BlockSpec((None, None, tr, c), lambda k, i, pr: (k, pr[1], i, 0)), blk], out_specs=[blk, blk]
        ),
        compiler_params=pltpu.CompilerParams(dimension_semantics=("arbitrary", "arbitrary"), vmem_limit_bytes=VMEM_LIMIT),
    )(place, view, got)


def _add_chips(part, got, place, name):
    _, r, c = part.shape
    tr = _row_tile(r)

    def body(place_ref, p_ref, g_ref, o_ref):
        o_ref[...] = ((p_ref[...] + g_ref[0].astype(F32)) + g_ref[1].astype(F32)) + g_ref[2].astype(F32)

    return pl.pallas_call(
        body,
        name=name,
        out_shape=jax.ShapeDtypeStruct((2, r, c), F32),
        grid_spec=pltpu.PrefetchScalarGridSpec(
            num_scalar_prefetch=1,
            grid=(r // tr,),
            in_specs=[pl.BlockSpec((None, tr, c), lambda i, pr: (pr[0], i, 0)), pl.BlockSpec((3, tr, c), lambda i, pr: (0, i, 0))],
            out_specs=pl.BlockSpec((None, tr, c), lambda i, pr: (pr[1], i, 0)),
        ),
        compiler_params=pltpu.CompilerParams(dimension_semantics=("arbitrary",), vmem_limit_bytes=VMEM_LIMIT),
    )(place, part, got)


def _adamw(w, g, m, v, name):
    r, c = w.shape
    tr = _row_tile(r)

    def body(w_ref, g_ref, m_ref, v_ref, d_ref, nm_ref, nv_ref):
        g = g_ref[...]
        nm = ADAM_B1 * m_ref[...] + (1.0 - ADAM_B1) * g
        nv = ADAM_B2 * v_ref[...] + (1.0 - ADAM_B2) * (g * g)
        m_hat = nm / (1.0 - ADAM_B1**ADAM_STEP)
        v_hat = nv / (1.0 - ADAM_B2**ADAM_STEP)
        d_ref[...] = -ADAM_LR * (m_hat / (jnp.sqrt(v_hat) + ADAM_EPS) + ADAM_WD * w_ref[...])
        nm_ref[...] = nm
        nv_ref[...] = nv

    spec = pl.BlockSpec((tr, c), lambda i: (i, 0))
    outs, _ = _pallas(body, name=name, grid=(r // tr,), in_specs=[spec] * 4, out_specs=[spec] * 3, out_shape=[jax.ShapeDtypeStruct((r, c), F32)] * 3, operands=(w, g, m, v))
    return outs


def _load_weights(pairs, sems):
    @pl.when(pl.program_id(0) == 0)
    def _():
        cps = [pltpu.make_async_copy(src, dst, sems.at[k]) for k, (src, dst) in enumerate(pairs)]
        for cp in cps:
            cp.start()
        for cp in cps:
            cp.wait()


def _ffn_fwd(h, g_pre, g_post, wg, wu, wd, name, exchange=None):
    T, D = h.shape
    ns, _, fs = wg.shape
    tm = FFN_TILE

    def body(h_ref, gpre_ref, gpost_ref, wg_hbm, wu_hbm, wd_hbm, hout_ref, f_ref, a_ref, b_ref, s_ref, n_ref, wg_v, wu_v, wd_v, sems):
        _load_weights([(wg_hbm, wg_v), (wu_hbm, wu_v), (wd_hbm, wd_v)], sems)
        hh = h_ref[...]
        n = (hh * _rms_stat(hh) * gpre_ref[...]).astype(MXU_DTYPE)
        n_ref[...] = n.astype(n_ref.dtype)
        f = jnp.zeros((tm, D), F32)
        for k in range(ns):
            a = _mm(n, wg_v[k])
            b = _mm(n, wu_v[k])
            s = (a * _sigmoid(a) * b).astype(MXU_DTYPE)
            a_ref[:, k * fs : (k + 1) * fs] = a.astype(a_ref.dtype)
            b_ref[:, k * fs : (k + 1) * fs] = b.astype(b_ref.dtype)
            s_ref[:, k * fs : (k + 1) * fs] = s.astype(s_ref.dtype)
            f = f + _mm(s, wd_v[k])
        f_ref[...] = f
        hout_ref[...] = hh + 0.5 * (f * _rms_stat(f) * gpost_ref[...])

    act = jax.ShapeDtypeStruct((T, ns * fs), MXU_DTYPE)
    return _pallas(
        body,
        name=name,
        grid=(T // tm,),
        out_shape=[jax.ShapeDtypeStruct((T, D), F32), jax.ShapeDtypeStruct((T, D), F32), act, act, act, jax.ShapeDtypeStruct((T, D), MXU_DTYPE)],
        in_specs=[_row_spec(tm, D), _full_spec((1, D)), _full_spec((1, D)), ANY, ANY, ANY],
        out_specs=[_row_spec(tm, D), _row_spec(tm, D), _row_spec(tm, ns * fs), _row_spec(tm, ns * fs), _row_spec(tm, ns * fs), _row_spec(tm, D)],
        scratch_shapes=[pltpu.VMEM(wg.shape, wg.dtype), pltpu.VMEM(wu.shape, wu.dtype), pltpu.VMEM(wd.shape, wd.dtype), pltpu.SemaphoreType.DMA((3,))],
        operands=(h, g_pre, g_post, wg, wu, wd),
        exchange=exchange,
    )


def _ffn_bwd_down(dh, f, a, b, g_post, wd, name, exchange=None):
    T, D = dh.shape
    ns, fs, _ = wd.shape
    tm = FFN_TILE

    def body(dh_ref, f_ref, a_ref, b_ref, gpost_ref, wd_hbm, da_ref, db_ref, df_ref, dg_ref, wd_v, sems):
        _load_weights([(wd_hbm, wd_v)], sems)
        df, dg = _rms_bwd(f_ref[...], gpost_ref[...], 0.5 * dh_ref[...])
        _acc(dg_ref, dg, pl.program_id(0) == 0)
        dfb = df.astype(MXU_DTYPE)
        df_ref[...] = dfb.astype(df_ref.dtype)
        for k in range(ns):
            ds = _mm_nt(dfb, wd_v[k])
            a = a_ref[:, k * fs : (k + 1) * fs].astype(F32)
            b = b_ref[:, k * fs : (k + 1) * fs].astype(F32)
            sg = _sigmoid(a)
            db_ref[:, k * fs : (k + 1) * fs] = (ds * (a * sg)).astype(db_ref.dtype)
            da_ref[:, k * fs : (k + 1) * fs] = (ds * b * (sg * (1.0 + a * (1.0 - sg)))).astype(da_ref.dtype)

    act = jax.ShapeDtypeStruct((T, ns * fs), MXU_DTYPE)
    return _pallas(
        body,
        name=name,
        grid=(T // tm,),
        out_shape=[act, act, jax.ShapeDtypeStruct((T, D), MXU_DTYPE), jax.ShapeDtypeStruct((1, D), F32)],
        in_specs=[_row_spec(tm, D), _row_spec(tm, D), _row_spec(tm, ns * fs), _row_spec(tm, ns * fs), _full_spec((1, D)), ANY],
        out_specs=[_row_spec(tm, ns * fs), _row_spec(tm, ns * fs), _row_spec(tm, D), _full_spec((1, D))],
        scratch_shapes=[pltpu.VMEM(wd.shape, wd.dtype), pltpu.SemaphoreType.DMA((1,))],
        operands=(dh, f, a, b, g_post, wd),
        exchange=exchange,
    )


def _ffn_bwd_up(da, db, h, dh, g_pre, wg, wu, name, exchange=None):
    T, D = h.shape
    ns, _, fs = wg.shape
    tm = FFN_TILE

    def body(da_ref, db_ref, h_ref, dh_ref, gpre_ref, wg_hbm, wu_hbm, dhin_ref, dg_ref, wg_v, wu_v, sems):
        _load_weights([(wg_hbm, wg_v), (wu_hbm, wu_v)], sems)
        dn = jnp.zeros((tm, D), F32)
        for k in range(ns):
            dn = dn + _mm_nt(da_ref[:, k * fs : (k + 1) * fs], wg_v[k]) + _mm_nt(db_ref[:, k * fs : (k + 1) * fs], wu_v[k])
        dx, dg = _rms_bwd(h_ref[...], gpre_ref[...], dn)
        _acc(dg_ref, dg, pl.program_id(0) == 0)
        dhin_ref[...] = dh_ref[...] + dx

    return _pallas(
        body,
        name=name,
        grid=(T // tm,),
        out_shape=[jax.ShapeDtypeStruct((T, D), F32), jax.ShapeDtypeStruct((1, D), F32)],
        in_specs=[_row_spec(tm, ns * fs), _row_spec(tm, ns * fs), _row_spec(tm, D), _row_spec(tm, D), _full_spec((1, D)), ANY, ANY],
        out_specs=[_row_spec(tm, D), _full_spec((1, D))],
        scratch_shapes=[pltpu.VMEM(wg.shape, wg.dtype), pltpu.VMEM(wu.shape, wu.dtype), pltpu.SemaphoreType.DMA((2,))],
        operands=(da, db, h, dh, g_pre, wg, wu),
        exchange=exchange,
    )


def _token_tile(T):
    for t in (912, 864, 432):
        if T % t == 0:
            return t
    raise ValueError(f"no token tile for {T} rows")


def _tn_matmul(xm, ym, ms, ns, name, exchange=None):
    T = xm.shape[0]
    mo, no = xm.shape[1] // ms, ym.shape[1] // ns
    tk = _token_tile(T)

    def body(x_ref, y_ref, o_ref):
        _acc(o_ref, _mm_tn(x_ref[...], y_ref[...]), pl.program_id(2) == 0)

    (out,), x_outs = _pallas(
        body,
        name=name,
        grid=(ms, ns, T // tk),
        out_shape=[jax.ShapeDtypeStruct((ms, ns, mo, no), F32)],
        in_specs=[pl.BlockSpec((tk, mo), lambda i, j, k: (k, i)), pl.BlockSpec((tk, no), lambda i, j, k: (k, j))],
        out_specs=[pl.BlockSpec((None, None, mo, no), lambda i, j, k: (i, j, 0, 0))],
        operands=(xm, ym),
        exchange=exchange,
    )
    return out, x_outs


TAB_A, TAB_AS1, TAB_AS2, TAB_AS4, TAB_JF, TAB_JB = 0, 2, 4, 6, 8, 10


def _scan_inplace(zr, zi, tabs, pows, car_r, car_i, seg, reverse):
    n_slabs = zr.shape[0]
    sgn = -1.0 if reverse else 1.0
    row = lax.broadcasted_iota(jnp.int32, (SUBLANES, LANES), 0)

    def cmul(pr, pi, xr, xi):
        return pr * xr - pi * xi, pr * xi + pi * xr

    for k0 in range(0, n_slabs, SLAB_GROUP):
        slabs = range(k0, min(k0 + SLAB_GROUP, n_slabs))
        ar = [tabs[TAB_A, k] for k in slabs]
        ai = [sgn * tabs[TAB_A + 1, k] for k in slabs]

        def first_pass(t, carry):
            r = (seg - 1 - t) if reverse else t
            out = []
            for q, k in enumerate(slabs):
                xr, xi = carry[2 * q], carry[2 * q + 1]
                pr, pi = cmul(ar[q], ai[q], xr, xi)
                nr = pr + zr[k, pl.ds(r, SUBLANES, stride=seg), :]
                ni = pi + zi[k, pl.ds(r, SUBLANES, stride=seg), :]
                zr[k, pl.ds(r, SUBLANES, stride=seg), :] = nr
                zi[k, pl.ds(r, SUBLANES, stride=seg), :] = ni
                out += [nr, ni]
            return tuple(out)

        ends = lax.fori_loop(0, seg, first_pass, tuple(jnp.zeros((SUBLANES, LANES), F32) for _ in range(2 * len(slabs))))

        incoming = []
        for q, k in enumerate(slabs):
            fr, fi = ends[2 * q], ends[2 * q + 1]
            for d, tab in ((1, TAB_AS1), (2, TAB_AS2), (4, TAB_AS4)):
                shift, keep = (SUBLANES - d, row < SUBLANES - d) if reverse else (d, row >= d)
                sr = jnp.where(keep, pltpu.roll(fr, shift, 0), 0.0)
                si = jnp.where(keep, pltpu.roll(fi, shift, 0), 0.0)
                pr, pi = cmul(tabs[tab, k], sgn * tabs[tab + 1, k], sr, si)
                fr, fi = fr + pr, fi + pi
            cr, ci = car_r[k], car_i[k]
            jtab = TAB_JB if reverse else TAB_JF
            pr, pi = cmul(tabs[jtab, k], sgn * tabs[jtab + 1, k], cr, ci)
            er, ei = fr + pr, fi + pi
            if reverse:
                inr = jnp.where(row < SUBLANES - 1, pltpu.roll(er, SUBLANES - 1, 0), cr)
                ini = jnp.where(row < SUBLANES - 1, pltpu.roll(ei, SUBLANES - 1, 0), ci)
                car_r[k] = jnp.broadcast_to(er[0:1, :], (SUBLANES, LANES))
                car_i[k] = jnp.broadcast_to(ei[0:1, :], (SUBLANES, LANES))
            else:
                inr = jnp.where(row >= 1, pltpu.roll(er, 1, 0), cr)
                ini = jnp.where(row >= 1, pltpu.roll(ei, 1, 0), ci)
                car_r[k] = jnp.broadcast_to(er[SUBLANES - 1 : SUBLANES, :], (SUBLANES, LANES))
                car_i[k] = jnp.broadcast_to(ei[SUBLANES - 1 : SUBLANES, :], (SUBLANES, LANES))
            incoming += [inr, ini]

        def second_pass(r, _):
            p = (seg - 1 - r) if reverse else r
            for q, k in enumerate(slabs):
                pr, pi = cmul(pows[0, k, p], sgn * pows[1, k, p], incoming[2 * q], incoming[2 * q + 1])
                zr[k, pl.ds(r, SUBLANES, stride=seg), :] = zr[k, pl.ds(r, SUBLANES, stride=seg), :] + pr
                zi[k, pl.ds(r, SUBLANES, stride=seg), :] = zi[k, pl.ds(r, SUBLANES, stride=seg), :] + pi
            return 0

        lax.fori_loop(0, seg, second_pass, 0)


def _slabs_to_cols(ref, k0, n):
    return jnp.concatenate([ref[k0 + q] for q in range(n)], axis=1)


def _window_sum(ext, doublings, forward):
    rows = ext.shape[0]
    s = ext
    for k in range(doublings):
        s = s + pltpu.roll(s, (1 << k) if forward else rows - (1 << k), 0)
    return s


def _mix_fwd(h1, g_pre, g_so, g_po, g_post, dskip, pscale, win, wout, bbre, bbim, ccre, ccim, wgv, wgg, pw, tabs, pows, name):
    T, D = h1.shape
    W = D // 2
    tm = MIX_TILE
    seg = tm // SUBLANES
    n_slabs = tabs.shape[1]
    nch, cch, sch = bbre.shape
    spc = sch // LANES
    pg = W // len(POOL_WINDOWS)

    def body(h_ref, gpre_ref, gso_ref, gpo_ref, gpost_ref, dskip_ref, pscale_ref, win_ref, wout_ref, bbre_ref, bbim_ref, ccre_ref, ccim_ref, wgv_ref, wgg_ref, pw_ref, tabs_ref, pows_ref,
             proj_ref, xr_ref, xi_ref, y_ref, pooled_ref, mixed_ref, h2_ref, n2_ref, cat_ref, car_r, car_i, halo):
        i = pl.program_id(0)

        @pl.when(i == 0)
        def _():
            car_r[...] = jnp.zeros_like(car_r)
            car_i[...] = jnp.zeros_like(car_i)
            halo[...] = jnp.zeros_like(halo)

        hh = h_ref[...]
        n2 = (hh * _rms_stat(hh) * gpre_ref[...]).astype(MXU_DTYPE)
        n2_ref[...] = n2.astype(n2_ref.dtype)
        proj = _mm(n2, win_ref[...])
        proj_ref[...] = proj
        us, up = proj[:, :W], proj[:, W:]

        for c in range(nch):
            uc = us[:, c * cch : (c + 1) * cch].astype(MXU_DTYPE)
            bur, bui = _mm(uc, bbre_ref[c]), _mm(uc, bbim_ref[c])
            for q in range(spc):
                xr_ref[c * spc + q] = bur[:, q * LANES : (q + 1) * LANES]
                xi_ref[c * spc + q] = bui[:, q * LANES : (q + 1) * LANES]
        _scan_inplace(xr_ref, xi_ref, tabs_ref, pows_ref, car_r, car_i, seg, reverse=False)
        ys = []
        for c in range(nch):
            ys.append(_mm(_slabs_to_cols(xr_ref, c * spc, spc), ccre_ref[c]) - _mm(_slabs_to_cols(xi_ref, c * spc, spc), ccim_ref[c]))
        y = jnp.concatenate(ys, axis=1) + dskip_ref[...] * us
        y_ref[...] = y
        ge = _gelu(y).astype(MXU_DTYPE)
        zv = jnp.concatenate([_mm(ge[:, c * cch : (c + 1) * cch], wgv_ref[c]) for c in range(nch)], axis=1)
        zg = jnp.concatenate([_mm(ge[:, c * cch : (c + 1) * cch], wgg_ref[c]) for c in range(nch)], axis=1)
        out = zv * _sigmoid(zg)
        cat_s = out * _rms_stat(out) * gso_ref[...]

        ext = jnp.concatenate([halo[...], up], axis=0)
        halo[...] = up[tm - POOL_HALO :, :]
        t1 = (i * tm + 1 + lax.broadcasted_iota(jnp.int32, (tm, pg), 0)).astype(F32)
        pooled, pms = [], []
        for g, w in enumerate(POOL_WINDOWS):
            col = ext[:, g * pg : (g + 1) * pg]
            win_sum = _window_sum(col, g + 1, True)[POOL_HALO:, :]
            pooled_g = win_sum / jnp.minimum(t1, float(w)) - up[:, g * pg : (g + 1) * pg]
            pooled.append(pooled_g)
            pms.append(_mm(pooled_g, pw_ref[g]))
        pooled_ref[...] = jnp.concatenate(pooled, axis=1)
        yp = jnp.concatenate(pms, axis=1) * pscale_ref[...]
        cat_p = yp * _rms_stat(yp) * gpo_ref[...]

        cat = jnp.concatenate([cat_s, cat_p], axis=1).astype(MXU_DTYPE)
        cat_ref[...] = cat.astype(cat_ref.dtype)
        mixed = _mm(cat, wout_ref[...])
        mixed_ref[...] = mixed
        h2_ref[...] = hh + mixed * _rms_stat(mixed) * gpost_ref[...]

    tok = lambda cols, dt=F32: jax.ShapeDtypeStruct((T, cols), dt)
    slab_spec = pl.BlockSpec((n_slabs, tm, LANES), lambda i: (0, i, 0))
    operands = (h1, g_pre, g_so, g_po, g_post, dskip, pscale, win, wout, bbre, bbim, ccre, ccim, wgv, wgg, pw, tabs, pows)
    outs, _ = _pallas(
        body,
        name=name,
        grid=(T // tm,),
        out_shape=[tok(D), jax.ShapeDtypeStruct((n_slabs, T, LANES), F32), jax.ShapeDtypeStruct((n_slabs, T, LANES), F32), tok(W), tok(W), tok(D), tok(D), tok(D, MXU_DTYPE), tok(D, MXU_DTYPE)],
        in_specs=[_row_spec(tm, D)] + [_full_spec(o.shape) for o in operands[1:]],
        out_specs=[_row_spec(tm, D), slab_spec, slab_spec, _row_spec(tm, W), _row_spec(tm, W), _row_spec(tm, D), _row_spec(tm, D), _row_spec(tm, D), _row_spec(tm, D)],
        scratch_shapes=[pltpu.VMEM((n_slabs, SUBLANES, LANES), F32), pltpu.VMEM((n_slabs, SUBLANES, LANES), F32), pltpu.VMEM((POOL_HALO, W), F32)],
        operands=operands,
    )
    return outs


def _mix_bwd_heads(dh2, mixed, y, pooled, proj, g_so, g_po, g_post, pscale, wout, wgv, wgg, pw, name, exchange=None):
    T, D = dh2.shape
    W = D // 2
    tm = MIX_TILE
    nch, cch, _ = wgv.shape
    ng, pg, _ = pw.shape

    def body(dh2_ref, mixed_ref, y_ref, pooled_ref, us_ref, gso_ref, gpo_ref, gpost_ref, pscale_ref, wout_ref, wgv_ref, wgg_ref, pw_ref,
             dy_ref, dpooled_ref, dmixed_ref, dgpost_ref, dgso_ref, dgpo_ref, dd_ref, dscale_ref, dwgv_ref, dwgg_ref, dpw_ref):
        first = pl.program_id(0) == 0
        dmixed, dgpost = _rms_bwd(mixed_ref[...], gpost_ref[...], dh2_ref[...])
        _acc(dgpost_ref, dgpost, first)
        dmb = dmixed.astype(MXU_DTYPE)
        dmixed_ref[...] = dmb.astype(dmixed_ref.dtype)
        dcat = _mm_nt(dmb, wout_ref[...])
        dcs, dcp = dcat[:, :W], dcat[:, W:]

        y = y_ref[...]
        ge = _gelu(y).astype(MXU_DTYPE)
        zv = jnp.concatenate([_mm(ge[:, c * cch : (c + 1) * cch], wgv_ref[c]) for c in range(nch)], axis=1)
        zg = jnp.concatenate([_mm(ge[:, c * cch : (c + 1) * cch], wgg_ref[c]) for c in range(nch)], axis=1)
        sg = _sigmoid(zg)
        dout, dgso = _rms_bwd(zv * sg, gso_ref[...], dcs)
        _acc(dgso_ref, dgso, first)
        dzv = (dout * sg).astype(MXU_DTYPE)
        dzg = (dout * zv * sg * (1.0 - sg)).astype(MXU_DTYPE)
        dges = []
        for c in range(nch):
            cs = slice(c * cch, (c + 1) * cch)
            dges.append(_mm_nt(dzv[:, cs], wgv_ref[c]) + _mm_nt(dzg[:, cs], wgg_ref[c]))
            _acc(dwgv_ref.at[c], _mm_tn(ge[:, cs], dzv[:, cs]), first)
            _acc(dwgg_ref.at[c], _mm_tn(ge[:, cs], dzg[:, cs]), first)
        dy = jnp.concatenate(dges, axis=1) * _gelu_grad(y)
        dy_ref[...] = dy
        _acc(dd_ref, jnp.sum(dy * us_ref[...], axis=0, keepdims=True), first)

        pooled_b = pooled_ref[...].astype(MXU_DTYPE)
        pm = jnp.concatenate([_mm(pooled_b[:, g * pg : (g + 1) * pg], pw_ref[g]) for g in range(ng)], axis=1)
        dyp, dgpo = _rms_bwd(pm * pscale_ref[...], gpo_ref[...], dcp)
        _acc(dgpo_ref, dgpo, first)
        _acc(dscale_ref, jnp.sum(dyp * pm, axis=0, keepdims=True), first)
        dpm = (dyp * pscale_ref[...]).astype(MXU_DTYPE)
        dps = []
        for g in range(ng):
            gs = slice(g * pg, (g + 1) * pg)
            dps.append(_mm_nt(dpm[:, gs], pw_ref[g]))
            _acc(dpw_ref.at[g], _mm_tn(pooled_b[:, gs], dpm[:, gs]), first)
        dpooled_ref[...] = jnp.concatenate(dps, axis=1)

    vec = lambda n: jax.ShapeDtypeStruct((1, n), F32)
    operands = (dh2, mixed, y, pooled, proj, g_so, g_po, g_post, pscale, wout, wgv, wgg, pw)
    return _pallas(
        body,
        name=name,
        grid=(T // tm,),
        out_shape=[jax.ShapeDtypeStruct((T, W), F32), jax.ShapeDtypeStruct((T, W), F32), jax.ShapeDtypeStruct((T, D), MXU_DTYPE), vec(D), vec(W), vec(W), vec(W), vec(W),
                   jax.ShapeDtypeStruct(wgv.shape, F32), jax.ShapeDtypeStruct(wgg.shape, F32), jax.ShapeDtypeStruct(pw.shape, F32)],
        in_specs=[_row_spec(tm, D), _row_spec(tm, D), _row_spec(tm, W), _row_spec(tm, W), _row_spec(tm, W)] + [_full_spec(o.shape) for o in operands[5:]],
        out_specs=[_row_spec(tm, W), _row_spec(tm, W), _row_spec(tm, D), _full_spec((1, D)), _full_spec((1, W)), _full_spec((1, W)), _full_spec((1, W)), _full_spec((1, W)),
                   _full_spec(wgv.shape), _full_spec(wgg.shape), _full_spec(pw.shape)],
        operands=operands,
        exchange=exchange,
    )


def _mix_bwd_scan(dy, dpooled, xr, xi, proj, h1, dh2, g_pre, dskip, win, bbre, bbim, ccre, ccim, tabs, pows, name, exchange=None):
    T, D = h1.shape
    W = D // 2
    tm = MIX_TILE
    seg = tm // SUBLANES
    nt = T // tm
    n_slabs = tabs.shape[1]
    nch, cch, sch = bbre.shape
    spc = sch // LANES
    pg = W // len(POOL_WINDOWS)
    blocks_per_tile = tm // SUBLANES

    def body(dy_ref, dp_ref, xr_ref, xi_ref, xpr_ref, xpi_ref, proj_ref, h_ref, dh2_ref, gpre_ref, dskip_ref, win_ref, bbre_ref, bbim_ref, ccre_ref, ccim_ref, tabs_ref, pows_ref,
             dh1_ref, dproj_ref, dgpre_ref, dccre_ref, dccim_ref, dbbre_ref, dbbim_ref, dar_ref, dai_ref, lr, li, car_r, car_i, halo):
        i = pl.program_id(0)
        first = i == 0
        tile = nt - 1 - i
        row = lax.broadcasted_iota(jnp.int32, (SUBLANES, LANES), 0)

        @pl.when(first)
        def _():
            car_r[...] = jnp.zeros_like(car_r)
            car_i[...] = jnp.zeros_like(car_i)
            halo[...] = jnp.zeros_like(halo)
            dar_ref[...] = jnp.zeros_like(dar_ref)
            dai_ref[...] = jnp.zeros_like(dai_ref)

        dy = dy_ref[...]
        for c in range(nch):
            dyc = dy[:, c * cch : (c + 1) * cch].astype(MXU_DTYPE)
            gr, gi = _mm_nt(dyc, ccre_ref[c]), _mm_nt(dyc, ccim_ref[c])
            for q in range(spc):
                lr[c * spc + q] = gr[:, q * LANES : (q + 1) * LANES]
                li[c * spc + q] = -gi[:, q * LANES : (q + 1) * LANES]
            _acc(dccre_ref.at[c], _mm_tn(_slabs_to_cols(xr_ref, c * spc, spc), dyc), first)
            _acc(dccim_ref.at[c], -_mm_tn(_slabs_to_cols(xi_ref, c * spc, spc), dyc), first)
        _scan_inplace(lr, li, tabs_ref, pows_ref, car_r, car_i, seg, reverse=True)

        for k in range(n_slabs):
            prev_r = jnp.where(tile > 0, jnp.broadcast_to(xpr_ref[k, SUBLANES - 1 : SUBLANES, :], (SUBLANES, LANES)), 0.0)
            prev_i = jnp.where(tile > 0, jnp.broadcast_to(xpi_ref[k, SUBLANES - 1 : SUBLANES, :], (SUBLANES, LANES)), 0.0)
            x0r = jnp.where(row >= 1, pltpu.roll(xr_ref[k, pl.ds(seg - 1, SUBLANES, stride=seg), :], 1, 0), prev_r)
            x0i = jnp.where(row >= 1, pltpu.roll(xi_ref[k, pl.ds(seg - 1, SUBLANES, stride=seg), :], 1, 0), prev_i)
            l0r, l0i = lr[k, pl.ds(0, SUBLANES, stride=seg), :], li[k, pl.ds(0, SUBLANES, stride=seg), :]

            def step(r, acc, k=k):
                ar_, ai_ = acc
                pr_, pi_ = xr_ref[k, pl.ds(r - 1, SUBLANES, stride=seg), :], xi_ref[k, pl.ds(r - 1, SUBLANES, stride=seg), :]
                lr_, li_ = lr[k, pl.ds(r, SUBLANES, stride=seg), :], li[k, pl.ds(r, SUBLANES, stride=seg), :]
                return ar_ + lr_ * pr_ + li_ * pi_, ai_ + li_ * pr_ - lr_ * pi_

            ar_, ai_ = lax.fori_loop(1, seg, step, (l0r * x0r + l0i * x0i, l0i * x0r - l0r * x0i))
            dar_ref[k] += ar_
            dai_ref[k] += ai_

        us = proj_ref[:, :W]
        dus = []
        for c in range(nch):
            lrc, lic = _slabs_to_cols(lr, c * spc, spc).astype(MXU_DTYPE), _slabs_to_cols(li, c * spc, spc).astype(MXU_DTYPE)
            uc = us[:, c * cch : (c + 1) * cch].astype(MXU_DTYPE)
            _acc(dbbre_ref.at[c], _mm_tn(uc, lrc), first)
            _acc(dbbim_ref.at[c], _mm_tn(uc, lic), first)
            dus.append(_mm_nt(lrc, bbre_ref[c]) + _mm_nt(lic, bbim_ref[c]))
        du_s = jnp.concatenate(dus, axis=1) + dskip_ref[...] * dy

        dp = dp_ref[...]
        t1 = (tile * tm + 1 + lax.broadcasted_iota(jnp.int32, (tm, pg), 0)).astype(F32)
        dups, heads = [], []
        for g, w in enumerate(POOL_WINDOWS):
            dpg = dp[:, g * pg : (g + 1) * pg]
            qg = dpg / jnp.minimum(t1, float(w))
            ext = jnp.concatenate([qg, halo[:, g * pg : (g + 1) * pg]], axis=0)
            dups.append(_window_sum(ext, g + 1, False)[:tm, :] - dpg)
            heads.append(qg[:POOL_HALO, :])
        halo[...] = jnp.concatenate(heads, axis=1)
        dproj = jnp.concatenate([du_s] + dups, axis=1).astype(MXU_DTYPE)
        dproj_ref[...] = dproj.astype(dproj_ref.dtype)
        dx, dg = _rms_bwd(h_ref[...], gpre_ref[...], _mm_nt(dproj, win_ref[...]))
        _acc(dgpre_ref, dg, first)
        dh1_ref[...] = dh2_ref[...] + dx

    rev = lambda cols: _row_spec(tm, cols, rev_n=nt)
    slab_spec = pl.BlockSpec((n_slabs, tm, LANES), lambda i: (0, nt - 1 - i, 0))
    prev_spec = pl.BlockSpec((n_slabs, SUBLANES, LANES), lambda i: (0, jnp.maximum((nt - 1 - i) * blocks_per_tile - 1, 0), 0))
    consts = (g_pre, dskip, win, bbre, bbim, ccre, ccim, tabs, pows)
    return _pallas(
        body,
        name=name,
        grid=(nt,),
        out_shape=[jax.ShapeDtypeStruct((T, D), F32), jax.ShapeDtypeStruct((T, D), MXU_DTYPE), jax.ShapeDtypeStruct((1, D), F32),
                   jax.ShapeDtypeStruct(ccre.shape, F32), jax.ShapeDtypeStruct(ccim.shape, F32), jax.ShapeDtypeStruct(bbre.shape, F32), jax.ShapeDtypeStruct(bbim.shape, F32),
                   jax.ShapeDtypeStruct((n_slabs, SUBLANES, LANES), F32), jax.ShapeDtypeStruct((n_slabs, SUBLANES, LANES), F32)],
        in_specs=[rev(W), rev(W), slab_spec, slab_spec, prev_spec, prev_spec, rev(D), rev(D), rev(D)] + [_full_spec(o.shape) for o in consts],
        out_specs=[rev(D), rev(D), _full_spec((1, D)), _full_spec(ccre.shape), _full_spec(ccim.shape), _full_spec(bbre.shape), _full_spec(bbim.shape),
                   _full_spec((n_slabs, SUBLANES, LANES)), _full_spec((n_slabs, SUBLANES, LANES))],
        scratch_shapes=[pltpu.VMEM((n_slabs, tm, LANES), F32), pltpu.VMEM((n_slabs, tm, LANES), F32), pltpu.VMEM((n_slabs, SUBLANES, LANES), F32), pltpu.VMEM((n_slabs, SUBLANES, LANES), F32),
                        pltpu.VMEM((POOL_HALO, W), F32)],
        operands=(dy, dpooled, xr, xi, xr, xi, proj, h1, dh2, *consts),
        exchange=exchange,
    )


def _loss_grad(h3, target, name):
    T, D = h3.shape
    tm = FFN_TILE

    def body(h_ref, t_ref, dy_ref, sq_ref):
        i = pl.program_id(0)
        rows = i * tm + lax.broadcasted_iota(jnp.int32, (tm, D), 0)
        err = jnp.where(rows >= N_META, h_ref[...] - t_ref[...], 0.0)
        dy_ref[...] = err * (1.0 / D)
        _acc(sq_ref, jnp.sum(err * err, axis=0, keepdims=True), i == 0)

    outs, _ = _pallas(
        body,
        name=name,
        grid=(T // tm,),
        out_shape=[jax.ShapeDtypeStruct((T, D), F32), jax.ShapeDtypeStruct((1, D), F32)],
        in_specs=[_row_spec(tm, D), _row_spec(tm, D)],
        out_specs=[_row_spec(tm, D), _full_spec((1, D))],
        operands=(h3, target),
    )
    return outs


def _discretize(lam_re, lam_im, log_dt, b_re, b_im):
    dt = jnp.exp(log_dt)[:, None]
    decay = jnp.exp(lam_re * dt)
    ang = lam_im * dt
    a_re, a_im = decay * jnp.cos(ang), decay * jnp.sin(ang)
    nr = a_re - 1.0
    den = lam_re * lam_re + lam_im * lam_im
    q_re = (nr * lam_re + a_im * lam_im) / den
    q_im = (a_im * lam_re - nr * lam_im) / den
    bb_re = q_re[..., None] * b_re - q_im[..., None] * b_im
    bb_im = q_re[..., None] * b_im + q_im[..., None] * b_re
    return a_re, a_im, bb_re, bb_im


GROUPS_PER_CHUNK = 16


def _block_diag(w, rows_first):
    G = w.shape[0]
    nch = G // GROUPS_PER_CHUNK
    if not rows_first:
        w = jnp.swapaxes(w, 1, 2)
    p, q = w.shape[1], w.shape[2]
    eye = jnp.eye(GROUPS_PER_CHUNK, dtype=w.dtype)
    out = jnp.einsum("cgpq,gk->cgpkq", w.reshape(nch, GROUPS_PER_CHUNK, p, q), eye)
    return out.reshape(nch, GROUPS_PER_CHUNK * p, GROUPS_PER_CHUNK * q)


def _block_diag_extract(m, p, q, rows_first):
    nch = m.shape[0]
    eye = jnp.eye(GROUPS_PER_CHUNK, dtype=m.dtype)
    out = jnp.einsum("cgpkq,gk->cgpq", m.reshape(nch, GROUPS_PER_CHUNK, p, GROUPS_PER_CHUNK, q), eye).reshape(nch * GROUPS_PER_CHUNK, p, q)
    return out if rows_first else jnp.swapaxes(out, 1, 2)


def _cmul(ar, ai, br, bi):
    return ar * br - ai * bi, ar * bi + ai * br


def _powers(ar, ai, count):
    pr, pi = ar[None], ai[None]
    while pr.shape[0] < count:
        nr, ni = _cmul(pr, pi, pr[-1][None], pi[-1][None])
        pr, pi = jnp.concatenate([pr, nr]), jnp.concatenate([pi, ni])
    return pr[:count], pi[:count]


def _scan_tables(a_re, a_im, seg):
    n = a_re.size
    ns = n // LANES
    ar, ai = a_re.reshape(n), a_im.reshape(n)
    pr, pi = _powers(ar, ai, seg)
    jr, ji = _powers(pr[-1], pi[-1], SUBLANES)

    def bcast(v):
        return jnp.broadcast_to(v.reshape(ns, 1, LANES), (ns, SUBLANES, LANES))

    def per_sublane(vs):
        return jnp.transpose(vs.reshape(SUBLANES, ns, LANES), (1, 0, 2))

    tabs = jnp.stack([bcast(ar), bcast(ai), bcast(jr[0]), bcast(ji[0]), bcast(jr[1]), bcast(ji[1]), bcast(jr[3]), bcast(ji[3]),
                      per_sublane(jr), per_sublane(ji), per_sublane(jr[::-1]), per_sublane(ji[::-1])])

    def rows(vs):
        return jnp.broadcast_to(jnp.transpose(vs.reshape(seg, ns, 1, LANES), (1, 0, 2, 3)), (ns, seg, SUBLANES, LANES))

    return tabs, jnp.stack([rows(pr), rows(pi)])


SMALL = ("ffn1_pre_norm", "ffn1_post_norm", "mix_pre_norm", "mix_post_norm", "ssm_lambda_re", "ssm_lambda_im", "ssm_log_dt", "ssm_b_re", "ssm_b_im", "ssm_c_re", "ssm_c_im",
         "ssm_d", "ssm_w_glu", "pool_w", "pool_scale", "ssm_out_norm", "pool_out_norm", "ffn2_pre_norm", "ffn2_post_norm")
BIG = ("ffn1_w_gate", "ffn1_w_up", "ffn1_w_down", "w_in", "w_out", "ffn2_w_gate", "ffn2_w_up", "ffn2_w_down")
ORDER = ("meta_tokens", "ffn1_pre_norm", "ffn1_post_norm", "ffn1_w_gate", "ffn1_w_up", "ffn1_w_down", "mix_pre_norm", "mix_post_norm", "w_in", "ssm_lambda_re", "ssm_lambda_im",
         "ssm_log_dt", "ssm_b_re", "ssm_b_im", "ssm_c_re", "ssm_c_im", "ssm_d", "ssm_w_glu", "pool_w", "pool_scale", "ssm_out_norm", "pool_out_norm", "w_out", "ffn2_pre_norm",
         "ffn2_post_norm", "ffn2_w_gate", "ffn2_w_up", "ffn2_w_down")
PACK_ROWS = SUBLANES * 8
FFN_CHUNKS = 2


def _shards_to_chunks(w):
    ns, d, fs = w.shape
    per = ns // FFN_CHUNKS
    return jnp.transpose(w.reshape(FFN_CHUNKS, per, d, fs), (0, 2, 1, 3)).reshape(FFN_CHUNKS, d, per * fs)


def _chunks_to_shards(w, ns):
    nc, d, fc = w.shape
    per = ns // nc
    return jnp.transpose(w.reshape(nc, d, per, fc // per), (0, 2, 1, 3)).reshape(ns, d, fc // per)


def _pack(arrays, rows):
    flat = jnp.concatenate([a.reshape(-1) for a in arrays])
    return jnp.pad(flat, (0, rows * LANES - flat.size)).reshape(rows, LANES)


def _unpack(packed, shapes):
    flat = packed.reshape(-1)
    out, off = [], 0
    for s in shapes:
        n = math.prod(s)
        out.append(flat[off : off + n].reshape(s))
        off += n
    return out


def _step(p, x, loss_target, m, v):
    D = x.shape[-1]
    chip = (2 * lax.axis_index("x") + lax.axis_index("y")).astype(jnp.int32)
    place = jnp.stack([chip, lax.axis_index("c").astype(jnp.int32)])

    def gather_buffer(w):
        own = w.reshape(1, 2, w.shape[0] // 2, w.shape[1])
        return lax.dynamic_update_slice(lax.empty((4,) + own.shape[1:], own.dtype), own, (chip, 0, 0, 0))

    def gathered_weight(n, g):
        g = g.reshape(4, g.shape[1] * g.shape[2], g.shape[3])
        if n.endswith(("w_gate", "w_up")):
            return _shards_to_chunks(g)
        if n.endswith("w_down"):
            return g.reshape(FFN_CHUNKS, -1, g.shape[2])
        return g.reshape(-1, g.shape[2])

    def grad_view(n, g):
        r, c = p[n].shape[1:]
        return g.reshape(4, 2, r // 2, c)

    def reduce_sum(got_sibling, views, tag):
        sums = [_add_own_half(v_, g_, place, f"{tag}_add_sibling_{k}") for k, (v_, g_) in enumerate(zip(views, got_sibling))]
        return [s[0] for s in sums], [s[1] for s in sums]

    def reduce_halves(parts, got_chips, tag):
        return [_add_chips(p_, g_, place, f"{tag}_add_chips_{k}") for k, (p_, g_) in enumerate(zip(parts, got_chips))]

    first_names = ("ffn1_w_gate", "ffn1_w_up", "ffn1_w_down")
    later_names = ("w_in", "w_out", "ffn2_w_gate", "ffn2_w_up", "ffn2_w_down")
    bufs = {n: gather_buffer(p[n][0].astype(MXU_DTYPE)) for n in BIG}
    got = _exchange_call(_Gather([bufs[n] for n in first_names] + [gather_buffer(p["meta_tokens"])]), "gather_first")
    full = {n: gathered_weight(n, g_) for n, g_ in zip(first_names, got)}
    meta = jnp.transpose(got[-1].reshape(4, N_META, -1), (1, 0, 2)).reshape(N_META, D)

    vec = lambda n: p[n].reshape(1, -1)
    G, N, H = p["ssm_b_re"].shape[1:]
    a_re, a_im, bb_re, bb_im = _discretize(p["ssm_lambda_re"][0], p["ssm_lambda_im"][0], p["ssm_log_dt"][0], p["ssm_b_re"][0], p["ssm_b_im"][0])
    tabs, pows = _scan_tables(a_re, a_im, MIX_TILE // SUBLANES)
    bf = lambda a: a.astype(MXU_DTYPE)
    bbre, bbim = bf(_block_diag(bb_re, False)), bf(_block_diag(bb_im, False))
    ccre, ccim = bf(_block_diag(p["ssm_c_re"][0], False)), bf(_block_diag(p["ssm_c_im"][0], False))
    wgv, wgg = bf(_block_diag(p["ssm_w_glu"][0][:, :, :H], True)), bf(_block_diag(p["ssm_w_glu"][0][:, :, H:], True))
    pw = bf(p["pool_w"][0])

    h0 = jnp.concatenate([meta, x[0]], axis=0)
    n_ffn_steps = h0.shape[0] // FFN_TILE
    (h1, f1, a1, b1, s1, n1), got = _ffn_fwd(
        h0, vec("ffn1_pre_norm"), vec("ffn1_post_norm"), full["ffn1_w_gate"], full["ffn1_w_up"], full["ffn1_w_down"], "ffn1_fwd",
        exchange=_Gather([bufs[n] for n in later_names], mid_step=(3 * n_ffn_steps) // 4),
    )
    full.update({n: gathered_weight(n, g_) for n, g_ in zip(later_names, got)})
    proj, xr, xim, y, pooled, mixed, h2, n2, cat = _mix_fwd(
        h1, vec("mix_pre_norm"), vec("ssm_out_norm"), vec("pool_out_norm"), vec("mix_post_norm"), vec("ssm_d"), vec("pool_scale"), full["w_in"], full["w_out"],
        bbre, bbim, ccre, ccim, wgv, wgg, pw, tabs, pows, "mix_fwd",
    )
    (h3, f2, a2, b2, s2, n3), _ = _ffn_fwd(h2, vec("ffn2_pre_norm"), vec("ffn2_post_norm"), full["ffn2_w_gate"], full["ffn2_w_up"], full["ffn2_w_down"], "ffn2_fwd")
    dh3, sq = _loss_grad(h3, jnp.pad(loss_target[0], ((N_META, 0), (0, 0))), "loss_grad")
    loss = lax.psum(0.5 * jnp.sum(sq) / D, ("x", "y", "c"))

    g, shared = {}, {}
    ffn_names = lambda tag: (tag + "_w_gate", tag + "_w_up", tag + "_w_down")

    (da, db, df, g["ffn2_post_norm"]), _ = _ffn_bwd_down(dh3, f2, a2, b2, vec("ffn2_post_norm"), full["ffn2_w_down"], "ffn2_bwd_down")
    (dh2, g["ffn2_pre_norm"]), _ = _ffn_bwd_up(da, db, h2, dh3, vec("ffn2_pre_norm"), full["ffn2_w_gate"], full["ffn2_w_up"], "ffn2_bwd_up")
    views2 = [
        grad_view("ffn2_w_gate", _chunks_to_shards(_tn_matmul(n3, da, 1, FFN_CHUNKS, "ffn2_dw_gate")[0][0], 4)),
        grad_view("ffn2_w_up", _chunks_to_shards(_tn_matmul(n3, db, 1, FFN_CHUNKS, "ffn2_dw_up")[0][0], 4)),
        grad_view("ffn2_w_down", _tn_matmul(s2, df, FFN_CHUNKS, 1, "ffn2_dw_down")[0]),
    ]
    (dy, dpooled, dmixed, g["mix_post_norm"], g["ssm_out_norm"], g["pool_out_norm"], g["ssm_d"], g["pool_scale"], dwgv, dwgg, g["pool_w"]), got = _mix_bwd_heads(
        dh2, mixed, y, pooled, proj, vec("ssm_out_norm"), vec("pool_out_norm"), vec("mix_post_norm"), vec("pool_scale"), full["w_out"], wgv, wgg, pw, "mix_bwd_heads",
        exchange=_SiblingScatter(views2),
    )
    parts2, wire2 = reduce_sum(got, views2, "ffn2")
    (dh1, dproj, g["mix_pre_norm"], dccre, dccim, dbbre, dbbim, dar, dai), got = _mix_bwd_scan(
        dy, dpooled, xr, xim, proj, h1, dh2, vec("mix_pre_norm"), vec("ssm_d"), full["w_in"], bbre, bbim, ccre, ccim, tabs, pows, "mix_bwd_scan",
        exchange=_ChipScatter(wire2),
    )
    halves2 = reduce_halves(parts2, got, "ffn2")
    dw_in, got = _tn_matmul(n2, dproj, 1, 1, "dw_in", exchange=_SiblingShare(halves2))
    shared.update(zip(ffn_names("ffn2"), got))
    dw_out, _ = _tn_matmul(cat, dmixed, 1, 1, "dw_out")
    views_m = [grad_view("w_in", dw_in), grad_view("w_out", dw_out)]

    (da, db, df, g["ffn1_post_norm"]), got = _ffn_bwd_down(dh1, f1, a1, b1, vec("ffn1_post_norm"), full["ffn1_w_down"], "ffn1_bwd_down", exchange=_SiblingScatter(views_m))
    parts_m, wire_m = reduce_sum(got, views_m, "mix")
    (dh0, g["ffn1_pre_norm"]), got = _ffn_bwd_up(da, db, h0, dh1, vec("ffn1_pre_norm"), full["ffn1_w_gate"], full["ffn1_w_up"], "ffn1_bwd_up", exchange=_ChipScatter(wire_m))
    halves_m = reduce_halves(parts_m, got, "mix")
    dw_gate, got = _tn_matmul(n1, da, 1, FFN_CHUNKS, "ffn1_dw_gate", exchange=_SiblingShare(halves_m))
    shared.update(zip(("w_in", "w_out"), got))
    grad_x = dh0[N_META:][None]

    g["ssm_c_re"] = _block_diag_extract(dccre, N, H, False)
    g["ssm_c_im"] = _block_diag_extract(dccim, N, H, False)
    g["ssm_w_glu"] = jnp.concatenate([_block_diag_extract(dwgv, H, H, True), _block_diag_extract(dwgg, H, H, True)], axis=-1)
    d_a_re, d_a_im = jnp.sum(dar, axis=1).reshape(G, N), jnp.sum(dai, axis=1).reshape(G, N)
    _, pull = jax.vjp(_discretize, p["ssm_lambda_re"][0], p["ssm_lambda_im"][0], p["ssm_log_dt"][0], p["ssm_b_re"][0], p["ssm_b_im"][0])
    g["ssm_lambda_re"], g["ssm_lambda_im"], g["ssm_log_dt"], g["ssm_b_re"], g["ssm_b_im"] = pull(
        (d_a_re, d_a_im, _block_diag_extract(dbbre, H, N, False), _block_diag_extract(dbbim, H, N, False))
    )

    small_shapes = [p[n].shape for n in SMALL] + [(N_META, D)]
    small_size = sum(math.prod(s) for s in small_shapes)
    rows = -(-small_size // (LANES * PACK_ROWS)) * PACK_ROWS
    views1 = [
        _pack([g[n] for n in SMALL] + [dh0[:N_META]], rows).reshape(4, 2, rows // 8, LANES),
        grad_view("ffn1_w_gate", _chunks_to_shards(dw_gate[0], 4)),
        grad_view("ffn1_w_up", _chunks_to_shards(_tn_matmul(n1, db, 1, FFN_CHUNKS, "ffn1_dw_up")[0][0], 4)),
        grad_view("ffn1_w_down", _tn_matmul(s1, df, FFN_CHUNKS, 1, "ffn1_dw_down")[0]),
    ]
    parts1, wire1 = reduce_sum(_exchange_call(_SiblingScatter(views1), "ffn1_reduce_sibling"), views1, "ffn1")
    halves1 = reduce_halves(parts1, _exchange_call(_ChipScatter(wire1), "ffn1_reduce_chips"), "ffn1")
    got = _exchange_call(_SiblingShare(halves1), "ffn1_reduce_share")
    shared.update(zip(ffn_names("ffn1"), got[1:]))
    small_buf = lax.dynamic_update_slice(lax.empty((4,) + got[0].shape, F32), got[0][None], (chip, 0, 0, 0))
    small_all = _exchange_call(_Gather([small_buf]), "gather_small")[0].reshape(rows, LANES)
    grads = dict(zip(SMALL + ("meta_full",), _unpack(small_all, small_shapes)))
    grads["meta_tokens"] = lax.dynamic_slice_in_dim(grads.pop("meta_full"), chip * (D // 4), D // 4, axis=1)
    for n in BIG:
        grads[n] = shared[n].reshape(p[n].shape)

    delta, new_m, new_v = {}, {}, {}
    sm = [_pack([t[n] for n in SMALL], rows) for t in (p, grads, m, v)]
    for out, packed_out in zip((delta, new_m, new_v), _adamw(*sm, "adamw_small")):
        out.update(zip(SMALL, _unpack(packed_out, [p[n].shape for n in SMALL])))
    for n in BIG + ("meta_tokens",):
        shape = p[n].shape
        flat = lambda a: a.reshape(-1, shape[-1])
        d_, m_, v_ = _adamw(flat(p[n]), flat(grads[n]), flat(m[n]), flat(v[n]), "adamw_" + n)
        delta[n], new_m[n], new_v[n] = d_.reshape(shape), m_.reshape(shape), v_.reshape(shape)

    return (loss, grad_x, *[grads[n] for n in ORDER], *[delta[n] for n in ORDER], *[new_m[n] for n in ORDER], *[new_v[n] for n in ORDER])


def kernel(x, meta_tokens, ffn1_pre_norm, ffn1_post_norm, ffn1_w_gate, ffn1_w_up, ffn1_w_down, mix_pre_norm, mix_post_norm, w_in, ssm_lambda_re, ssm_lambda_im, ssm_log_dt, ssm_b_re, ssm_b_im, ssm_c_re, ssm_c_im, ssm_d, ssm_w_glu, pool_w, pool_scale, ssm_out_norm, pool_out_norm, w_out, ffn2_pre_norm, ffn2_post_norm, ffn2_w_gate, ffn2_w_up, ffn2_w_down, loss_target, m_meta_tokens, m_ffn1_pre_norm, m_ffn1_post_norm, m_ffn1_w_gate, m_ffn1_w_up, m_ffn1_w_down, m_mix_pre_norm, m_mix_post_norm, m_w_in, m_ssm_lambda_re, m_ssm_lambda_im, m_ssm_log_dt, m_ssm_b_re, m_ssm_b_im, m_ssm_c_re, m_ssm_c_im, m_ssm_d, m_ssm_w_glu, m_pool_w, m_pool_scale, m_ssm_out_norm, m_pool_out_norm, m_w_out, m_ffn2_pre_norm, m_ffn2_post_norm, m_ffn2_w_gate, m_ffn2_w_up, m_ffn2_w_down, v_meta_tokens, v_ffn1_pre_norm, v_ffn1_post_norm, v_ffn1_w_gate, v_ffn1_w_up, v_ffn1_w_down, v_mix_pre_norm, v_mix_post_norm, v_w_in, v_ssm_lambda_re, v_ssm_lambda_im, v_ssm_log_dt, v_ssm_b_re, v_ssm_b_im, v_ssm_c_re, v_ssm_c_im, v_ssm_d, v_ssm_w_glu, v_pool_w, v_pool_scale, v_ssm_out_norm, v_pool_out_norm, v_w_out, v_ffn2_pre_norm, v_ffn2_post_norm, v_ffn2_w_gate, v_ffn2_w_up, v_ffn2_w_down):
    args = locals()
    p = {n: args[n] for n in ORDER}
    m = {n: args["m_" + n] for n in ORDER}
    v = {n: args["v_" + n] for n in ORDER}
    return _step(p, x, loss_target, m, v)
```

```python
import math

import jax
import jax.numpy as jnp
from jax import lax
from jax.experimental import pallas as pl
from jax.experimental.pallas import tpu as pltpu

F32 = jnp.float32
MXU_DTYPE = jnp.bfloat16
WIRE_DTYPE = jnp.bfloat16

RMS_EPS = 1e-6
N_META = 16
POOL_WINDOWS = (2, 4, 8, 16)
POOL_HALO = 16
ADAM_LR, ADAM_B1, ADAM_B2, ADAM_EPS, ADAM_WD, ADAM_STEP = 0.001, 0.9, 0.999, 1e-08, 0.01, 10

LANES = 128
SUBLANES = 8
VMEM_LIMIT = 60 * 1024 * 1024
FFN_TILE = 432
FFN_CHUNK = 1024
MIX_TILE = 216
SLAB_GROUP = 8
MESH = pl.DeviceIdType.MESH
ANY = pl.BlockSpec(memory_space=pl.ANY)


def _mm(a, b):
    return jnp.dot(a.astype(MXU_DTYPE), b.astype(MXU_DTYPE), preferred_element_type=F32)


def _mm_nt(a, b):
    return lax.dot_general(a.astype(MXU_DTYPE), b.astype(MXU_DTYPE), (((1,), (1,)), ((), ())), preferred_element_type=F32)


def _mm_tn(a, b):
    return lax.dot_general(a.astype(MXU_DTYPE), b.astype(MXU_DTYPE), (((0,), (0,)), ((), ())), preferred_element_type=F32)


def _rms_stat(x):
    return lax.rsqrt(jnp.mean(x * x, axis=-1, keepdims=True) + RMS_EPS)


def _rms_bwd(x, g, dy):
    r = _rms_stat(x)
    xh = x * r
    dg = jnp.sum(dy * xh, axis=0, keepdims=True)
    dxh = dy * g
    dx = r * (dxh - xh * jnp.mean(dxh * xh, axis=-1, keepdims=True))
    return dx, dg


def _sigmoid(x):
    return 1.0 / (1.0 + jnp.exp(-x))


GELU_C = math.sqrt(2.0 / math.pi)
GELU_K = 0.044715


def _gelu(y):
    return 0.5 * y * (1.0 + jnp.tanh(GELU_C * (y + GELU_K * y * y * y)))


def _gelu_grad(y):
    th = jnp.tanh(GELU_C * (y + GELU_K * y * y * y))
    return 0.5 * (1.0 + th) + 0.5 * y * (1.0 - th * th) * GELU_C * (1.0 + 3.0 * GELU_K * y * y)


def _row_spec(tile, cols, rev_n=None):
    if rev_n is None:
        return pl.BlockSpec((tile, cols), lambda i: (i, 0))
    return pl.BlockSpec((tile, cols), lambda i: (rev_n - 1 - i, 0))


def _full_spec(shape):
    zeros = (0,) * len(shape)
    return pl.BlockSpec(shape, lambda *_: zeros)


def _acc(ref, val, first):
    @pl.when(first)
    def _():
        ref[...] = val

    @pl.when(jnp.logical_not(first))
    def _():
        ref[...] += val


def _place():
    x, y, c = lax.axis_index("x"), lax.axis_index("y"), lax.axis_index("c")
    others = [(1 - x, y), (x, 1 - y), (1 - x, 1 - y)]
    return x, y, c, others


class _Exchange:
    mid_step = None

    def __init__(self, ins, out_shapes, aliases, n_sems):
        self.ins, self.out_shapes, self.aliases, self.n_sems = list(ins), list(out_shapes), dict(aliases), n_sems

    def mid(self, ins, outs, send_sems, recv_sems):
        pass


class _SiblingScatter(_Exchange):
    def __init__(self, views):
        super().__init__(views, [jax.ShapeDtypeStruct((4,) + v.shape[2:], v.dtype) for v in views], {}, 4 * len(views))

    def _copies(self, ins, outs, send_sems, recv_sems):
        x, y, c, _ = _place()
        return [
            pltpu.make_async_remote_copy(src_ref=ins[a].at[k, 1 - c], dst_ref=outs[a].at[k], send_sem=send_sems.at[4 * a + k], recv_sem=recv_sems.at[4 * a + k], device_id=(x, y, 1 - c), device_id_type=MESH)
            for a in range(len(ins))
            for k in range(4)
        ]

    def start(self, *refs):
        for cp in self._copies(*refs):
            cp.start()

    def finish(self, *refs):
        cps = self._copies(*refs)
        for cp in cps:
            cp.wait_recv()
        for cp in cps:
            cp.wait_send()


class _ChipScatter(_Exchange):
    def __init__(self, parts):
        super().__init__(parts, [jax.ShapeDtypeStruct((3,) + p.shape[1:], p.dtype) for p in parts], {}, 3 * len(parts))

    def _copies(self, ins, outs, send_sems, recv_sems):
        x, y, c, others = _place()
        return [
            pltpu.make_async_remote_copy(src_ref=ins[a].at[2 * chip[0] + chip[1]], dst_ref=outs[a].at[j], send_sem=send_sems.at[3 * a + j], recv_sem=recv_sems.at[3 * a + j], device_id=(*chip, c), device_id_type=MESH)
            for a in range(len(ins))
            for j, chip in enumerate(others)
        ]

    start = _SiblingScatter.start
    finish = _SiblingScatter.finish


class _SiblingShare(_Exchange):
    def __init__(self, bufs):
        super().__init__(bufs, [jax.ShapeDtypeStruct(b.shape, b.dtype) for b in bufs], {a: a for a in range(len(bufs))}, len(bufs))

    def _copy(self, outs, send_sems, recv_sems, a, half):
        x, y, c, _ = _place()
        mine = outs[a].at[c if half == "mine" else 1 - c]
        return pltpu.make_async_remote_copy(src_ref=mine, dst_ref=mine, send_sem=send_sems.at[a], recv_sem=recv_sems.at[a], device_id=(x, y, 1 - c), device_id_type=MESH)

    def start(self, ins, outs, send_sems, recv_sems):
        for a in range(len(outs)):
            self._copy(outs, send_sems, recv_sems, a, "mine").start()

    def finish(self, ins, outs, send_sems, recv_sems):
        for a in range(len(outs)):
            self._copy(outs, send_sems, recv_sems, a, "theirs").wait_recv()
        for a in range(len(outs)):
            self._copy(outs, send_sems, recv_sems, a, "mine").wait_send()


class _Gather(_Exchange):
    def __init__(self, bufs, mid_step=None):
        super().__init__(bufs, [jax.ShapeDtypeStruct(b.shape, b.dtype) for b in bufs], {a: a for a in range(len(bufs))}, 6 * len(bufs))
        self.mid_step = mid_step

    def _copy(self, outs, send_sems, recv_sems, a, j, chip, half, to):
        blk = outs[a].at[2 * chip[0] + chip[1], half]
        return pltpu.make_async_remote_copy(src_ref=blk, dst_ref=blk, send_sem=send_sems.at[6 * a + j], recv_sem=recv_sems.at[6 * a + j], device_id=to, device_id_type=MESH)

    def start(self, ins, outs, send_sems, recv_sems):
        x, y, c, others = _place()
        for a in range(len(outs)):
            for j, chip in enumerate(others):
                self._copy(outs, send_sems, recv_sems, a, j, (x, y), c, (*chip, c)).start()

    def mid(self, ins, outs, send_sems, recv_sems):
        x, y, c, others = _place()
        for a in range(len(outs)):
            for j, chip in enumerate(others):
                self._copy(outs, send_sems, recv_sems, a, j, chip, c, (x, y, c)).wait_recv()
                self._copy(outs, send_sems, recv_sems, a, 3 + j, chip, c, (x, y, 1 - c)).start()

    def finish(self, ins, outs, send_sems, recv_sems):
        x, y, c, others = _place()
        for a in range(len(outs)):
            for j, chip in enumerate(others):
                self._copy(outs, send_sems, recv_sems, a, 3 + j, chip, 1 - c, (x, y, c)).wait_recv()
        for a in range(len(outs)):
            for j, chip in enumerate(others):
                self._copy(outs, send_sems, recv_sems, a, j, (x, y), c, (*chip, c)).wait_send()
                self._copy(outs, send_sems, recv_sems, a, 3 + j, chip, c, (x, y, 1 - c)).wait_send()


def _exchange_call(ex, name):
    n, m = len(ex.ins), len(ex.out_shapes)

    def body(*refs):
        parts = (refs[:n], refs[n : n + m], refs[n + m], refs[n + m + 1])
        ex.start(*parts)
        ex.mid(*parts)
        ex.finish(*parts)

    return pl.pallas_call(
        body,
        name=name,
        out_shape=ex.out_shapes,
        in_specs=[ANY] * n,
        out_specs=[ANY] * m,
        scratch_shapes=[pltpu.SemaphoreType.DMA((ex.n_sems,)), pltpu.SemaphoreType.DMA((ex.n_sems,))],
        input_output_aliases=ex.aliases,
    )(*ex.ins)


def _pallas(body, *, name, grid, in_specs, out_specs, out_shape, operands, scratch_shapes=(), exchange=None):
    params = pltpu.CompilerParams(dimension_semantics=("arbitrary",) * len(grid), vmem_limit_bytes=VMEM_LIMIT)
    if exchange is None:
        outs = pl.pallas_call(body, name=name, grid=grid, in_specs=in_specs, out_specs=out_specs, out_shape=out_shape, scratch_shapes=list(scratch_shapes), compiler_params=params)(*operands)
        return outs, []
    ex = exchange
    n_in, n_out, n_scr = len(in_specs), len(out_specs), len(scratch_shapes)
    x_in, x_out = len(ex.ins), len(ex.out_shapes)

    def hosted(*refs):
        ins, x_ins = refs[:n_in], refs[n_in : n_in + x_in]
        outs, x_outs = refs[n_in + x_in : n_in + x_in + n_out], refs[n_in + x_in + n_out : n_in + x_in + n_out + x_out]
        rest = refs[n_in + x_in + n_out + x_out :]
        parts = (x_ins, x_outs, rest[n_scr], rest[n_scr + 1])
        ids = [pl.program_id(d) for d in range(len(grid))]
        first = _all([i == 0 for i in ids])
        last = _all([i == g - 1 for i, g in zip(ids, grid)])

        @pl.when(first)
        def _():
            ex.start(*parts)

        body(*ins, *outs, *rest[:n_scr])

        if ex.mid_step is not None:

            @pl.when(ids[0] == ex.mid_step)
            def _():
                ex.mid(*parts)

        @pl.when(last)
        def _():
            ex.finish(*parts)

    outs = pl.pallas_call(
        hosted,
        name=name,
        grid=grid,
        in_specs=list(in_specs) + [ANY] * x_in,
        out_specs=list(out_specs) + [ANY] * x_out,
        out_shape=list(out_shape) + ex.out_shapes,
        scratch_shapes=list(scratch_shapes) + [pltpu.SemaphoreType.DMA((ex.n_sems,)), pltpu.SemaphoreType.DMA((ex.n_sems,))],
        input_output_aliases={n_in + i: n_out + o for i, o in ex.aliases.items()},
        compiler_params=params,
    )(*operands, *ex.ins)
    return outs[:n_out], outs[n_out:]


def _all(conds):
    out = conds[0]
    for c in conds[1:]:
        out = jnp.logical_and(out, c)
    return out


def _row_tile(rows):
    for t in (512, 352, 256, 176, 128, 112, 64, 32, 16, 8):
        if rows % t == 0:
            return t
    return rows


def _add_own_half(view, got, place, name):
    _, _, r, c = view.shape
    tr = _row_tile(r)

    def body(place_ref, v_ref, g_ref, o_ref, w_ref):
        s = v_ref[...] + g_ref[...]
        o_ref[...] = s
        w_ref[...] = s.astype(w_ref.dtype)

    blk = pl.BlockSpec((None, tr, c), lambda k, i, pr: (k, i, 0))
    return pl.pallas_call(
        body,
        name=name,
        out_shape=[jax.ShapeDtypeStruct((4, r, c), F32), jax.ShapeDtypeStruct((4, r, c), WIRE_DTYPE)],
        grid_spec=pltpu.PrefetchScalarGridSpec(
            num_scalar_prefetch=1, grid=(4, r // tr), in_specs=[pl.BlockSpec((None, None, tr, c), lambda k, i, pr: (k, pr[1], i, 0)), blk], out_specs=[blk, blk]
        ),
        compiler_params=pltpu.CompilerParams(dimension_semantics=("arbitrary", "arbitrary"), vmem_limit_bytes=VMEM_LIMIT),
    )(place, view, got)


def _add_chips(part, got, place, name):
    _, r, c = part.shape
    tr = _row_tile(r)

    def body(place_ref, p_ref, g_ref, o_ref):
        o_ref[...] = ((p_ref[...] + g_ref[0].astype(F32)) + g_ref[1].astype(F32)) + g_ref[2].astype(F32)

    return pl.pallas_call(
        body,
        name=name,
        out_shape=jax.ShapeDtypeStruct((2, r, c), F32),
        grid_spec=pltpu.PrefetchScalarGridSpec(
            num_scalar_prefetch=1,
            grid=(r // tr,),
            in_specs=[pl.BlockSpec((None, tr, c), lambda i, pr: (pr[0], i, 0)), pl.BlockSpec((3, tr, c), lambda i, pr: (0, i, 0))],
            out_specs=pl.BlockSpec((None, tr, c), lambda i, pr: (pr[1], i, 0)),
        ),
        compiler_params=pltpu.CompilerParams(dimension_semantics=("arbitrary",), vmem_limit_bytes=VMEM_LIMIT),
    )(place, part, got)


def _adamw(w, g, m, v, name):
    r, c = w.shape
    tr = _row_tile(r)

    def body(w_ref, g_ref, m_ref, v_ref, d_ref, nm_ref, nv_ref):
        g = g_ref[...]
        nm = ADAM_B1 * m_ref[...] + (1.0 - ADAM_B1) * g
        nv = ADAM_B2 * v_ref[...] + (1.0 - ADAM_B2) * (g * g)
        m_hat = nm / (1.0 - ADAM_B1**ADAM_STEP)
        v_hat = nv / (1.0 - ADAM_B2**ADAM_STEP)
        d_ref[...] = -ADAM_LR * (m_hat / (jnp.sqrt(v_hat) + ADAM_EPS) + ADAM_WD * w_ref[...])
        nm_ref[...] = nm
        nv_ref[...] = nv

    spec = pl.BlockSpec((tr, c), lambda i: (i, 0))
    outs, _ = _pallas(body, name=name, grid=(r // tr,), in_specs=[spec] * 4, out_specs=[spec] * 3, out_shape=[jax.ShapeDtypeStruct((r, c), F32)] * 3, operands=(w, g, m, v))
    return outs


def _load_weights(pairs, sems):
    @pl.when(pl.program_id(0) == 0)
    def _():
        cps = [pltpu.make_async_copy(src, dst, sems.at[k]) for k, (src, dst) in enumerate(pairs)]
        for cp in cps:
            cp.start()
        for cp in cps:
            cp.wait()


def _ffn_chunks(F):
    bounds = list(range(0, F, FFN_CHUNK)) + [F]
    return list(zip(bounds[:-1], bounds[1:]))


def _ffn_fwd(h, g_pre, g_post, wg, wu, wd, name, exchange=None):
    T, D = h.shape
    F = wg.shape[1]
    tm = FFN_TILE

    def body(h_ref, gpre_ref, gpost_ref, wg_hbm, wu_hbm, wd_hbm, hout_ref, f_ref, ga_ref, si_ref, s_ref, n_ref, wg_v, wu_v, wd_v, sems):
        _load_weights([(wg_hbm, wg_v), (wu_hbm, wu_v), (wd_hbm, wd_v)], sems)
        hh = h_ref[...]
        n = (hh * _rms_stat(hh) * gpre_ref[...]).astype(MXU_DTYPE)
        n_ref[...] = n.astype(n_ref.dtype)
        f = jnp.zeros((tm, D), F32)
        for lo, hi in _ffn_chunks(F):
            a = _mm(n, wg_v[:, lo:hi])
            b = _mm(n, wu_v[:, lo:hi])
            sg = _sigmoid(a)
            si = a * sg
            s = (si * b).astype(MXU_DTYPE)
            ga_ref[:, lo:hi] = (b * (sg * (1.0 + a * (1.0 - sg)))).astype(ga_ref.dtype)
            si_ref[:, lo:hi] = si.astype(si_ref.dtype)
            s_ref[:, lo:hi] = s.astype(s_ref.dtype)
            f = f + _mm(s, wd_v[lo:hi, :])
        f_ref[...] = f
        hout_ref[...] = hh + 0.5 * (f * _rms_stat(f) * gpost_ref[...])

    act = jax.ShapeDtypeStruct((T, F), MXU_DTYPE)
    return _pallas(
        body,
        name=name,
        grid=(T // tm,),
        out_shape=[jax.ShapeDtypeStruct((T, D), F32), jax.ShapeDtypeStruct((T, D), F32), act, act, act, jax.ShapeDtypeStruct((T, D), MXU_DTYPE)],
        in_specs=[_row_spec(tm, D), _full_spec((1, D)), _full_spec((1, D)), ANY, ANY, ANY],
        out_specs=[_row_spec(tm, D), _row_spec(tm, D), _row_spec(tm, F), _row_spec(tm, F), _row_spec(tm, F), _row_spec(tm, D)],
        scratch_shapes=[pltpu.VMEM(wg.shape, wg.dtype), pltpu.VMEM(wu.shape, wu.dtype), pltpu.VMEM(wd.shape, wd.dtype), pltpu.SemaphoreType.DMA((3,))],
        operands=(h, g_pre, g_post, wg, wu, wd),
        exchange=exchange,
    )


def _ffn_bwd_down(dh, f, ga, si, g_post, wd, name, exchange=None):
    T, D = dh.shape
    F = wd.shape[0]
    tm = FFN_TILE

    def body(dh_ref, f_ref, ga_ref, si_ref, gpost_ref, wd_hbm, da_ref, db_ref, df_ref, dg_ref, wd_v, sems):
        _load_weights([(wd_hbm, wd_v)], sems)
        df, dg = _rms_bwd(f_ref[...], gpost_ref[...], 0.5 * dh_ref[...])
        _acc(dg_ref, dg, pl.program_id(0) == 0)
        dfb = df.astype(MXU_DTYPE)
        df_ref[...] = dfb.astype(df_ref.dtype)
        for lo, hi in _ffn_chunks(F):
            ds = _mm_nt(dfb, wd_v[lo:hi, :])
            da_ref[:, lo:hi] = (ds * ga_ref[:, lo:hi].astype(F32)).astype(da_ref.dtype)
            db_ref[:, lo:hi] = (ds * si_ref[:, lo:hi].astype(F32)).astype(db_ref.dtype)

    act = jax.ShapeDtypeStruct((T, F), MXU_DTYPE)
    return _pallas(
        body,
        name=name,
        grid=(T // tm,),
        out_shape=[act, act, jax.ShapeDtypeStruct((T, D), MXU_DTYPE), jax.ShapeDtypeStruct((1, D), F32)],
        in_specs=[_row_spec(tm, D), _row_spec(tm, D), _row_spec(tm, F), _row_spec(tm, F), _full_spec((1, D)), ANY],
        out_specs=[_row_spec(tm, F), _row_spec(tm, F), _row_spec(tm, D), _full_spec((1, D))],
        scratch_shapes=[pltpu.VMEM(wd.shape, wd.dtype), pltpu.SemaphoreType.DMA((1,))],
        operands=(dh, f, ga, si, g_post, wd),
        exchange=exchange,
    )


def _ffn_bwd_up(da, db, h, dh, g_pre, wg, wu, name, exchange=None):
    T, D = h.shape
    F = wg.shape[1]
    tm = FFN_TILE

    def body(da_ref, db_ref, h_ref, dh_ref, gpre_ref, wg_hbm, wu_hbm, dhin_ref, dg_ref, wg_v, wu_v, sems):
        _load_weights([(wg_hbm, wg_v), (wu_hbm, wu_v)], sems)
        dn = jnp.zeros((tm, D), F32)
        for lo, hi in _ffn_chunks(F):
            dn = dn + _mm_nt(da_ref[:, lo:hi], wg_v[:, lo:hi]) + _mm_nt(db_ref[:, lo:hi], wu_v[:, lo:hi])
        dx, dg = _rms_bwd(h_ref[...], gpre_ref[...], dn)
        _acc(dg_ref, dg, pl.program_id(0) == 0)
        dhin_ref[...] = dh_ref[...] + dx

    return _pallas(
        body,
        name=name,
        grid=(T // tm,),
        out_shape=[jax.ShapeDtypeStruct((T, D), F32), jax.ShapeDtypeStruct((1, D), F32)],
        in_specs=[_row_spec(tm, F), _row_spec(tm, F), _row_spec(tm, D), _row_spec(tm, D), _full_spec((1, D)), ANY, ANY],
        out_specs=[_row_spec(tm, D), _full_spec((1, D))],
        scratch_shapes=[pltpu.VMEM(wg.shape, wg.dtype), pltpu.VMEM(wu.shape, wu.dtype), pltpu.SemaphoreType.DMA((2,))],
        operands=(da, db, h, dh, g_pre, wg, wu),
        exchange=exchange,
    )


def _token_tile(T):
    for t in (912, 864, 432):
        if T % t == 0:
            return t
    raise ValueError(f"no token tile for {T} rows")


def _tn_matmul(xm, ym, name, exchange=None):
    T, M = xm.shape
    N = ym.shape[1]
    tk = _token_tile(T)

    def body(x_ref, y_ref, o_ref):
        _acc(o_ref, _mm_tn(x_ref[...], y_ref[...]), pl.program_id(0) == 0)

    (out,), x_outs = _pallas(
        body,
        name=name,
        grid=(T // tk,),
        out_shape=[jax.ShapeDtypeStruct((M, N), F32)],
        in_specs=[pl.BlockSpec((tk, M), lambda k: (k, 0)), pl.BlockSpec((tk, N), lambda k: (k, 0))],
        out_specs=[_full_spec((M, N))],
        operands=(xm, ym),
        exchange=exchange,
    )
    return out, x_outs


TAB_A, TAB_AS1, TAB_AS2, TAB_AS4, TAB_JF, TAB_JB = 0, 2, 4, 6, 8, 10


def _scan_inplace(zr, zi, tabs, pows, car_r, car_i, seg, reverse):
    n_slabs = zr.shape[0]
    sgn = -1.0 if reverse else 1.0
    row = lax.broadcasted_iota(jnp.int32, (SUBLANES, LANES), 0)

    def cmul(pr, pi, xr, xi):
        return pr * xr - pi * xi, pr * xi + pi * xr

    for k0 in range(0, n_slabs, SLAB_GROUP):
        slabs = range(k0, min(k0 + SLAB_GROUP, n_slabs))
        ar = [tabs[TAB_A, k] for k in slabs]
        ai = [sgn * tabs[TAB_A + 1, k] for k in slabs]

        def first_pass(t, carry):
            r = (seg - 1 - t) if reverse else t
            out = []
            for q, k in enumerate(slabs):
                xr, xi = carry[2 * q], carry[2 * q + 1]
                pr, pi = cmul(ar[q], ai[q], xr, xi)
                nr = pr + zr[k, pl.ds(r, SUBLANES, stride=seg), :]
                ni = pi + zi[k, pl.ds(r, SUBLANES, stride=seg), :]
                zr[k, pl.ds(r, SUBLANES, stride=seg), :] = nr
                zi[k, pl.ds(r, SUBLANES, stride=seg), :] = ni
                out += [nr, ni]
            return tuple(out)

        ends = lax.fori_loop(0, seg, first_pass, tuple(jnp.zeros((SUBLANES, LANES), F32) for _ in range(2 * len(slabs))))

        incoming = []
        for q, k in enumerate(slabs):
            fr, fi = ends[2 * q], ends[2 * q + 1]
            for d, tab in ((1, TAB_AS1), (2, TAB_AS2), (4, TAB_AS4)):
                shift, keep = (SUBLANES - d, row < SUBLANES - d) if reverse else (d, row >= d)
                sr = jnp.where(keep, pltpu.roll(fr, shift, 0), 0.0)
                si = jnp.where(keep, pltpu.roll(fi, shift, 0), 0.0)
                pr, pi = cmul(tabs[tab, k], sgn * tabs[tab + 1, k], sr, si)
                fr, fi = fr + pr, fi + pi
            cr, ci = car_r[k], car_i[k]
            jtab = TAB_JB if reverse else TAB_JF
            pr, pi = cmul(tabs[jtab, k], sgn * tabs[jtab + 1, k], cr, ci)
            er, ei = fr + pr, fi + pi
            if reverse:
                inr = jnp.where(row < SUBLANES - 1, pltpu.roll(er, SUBLANES - 1, 0), cr)
                ini = jnp.where(row < SUBLANES - 1, pltpu.roll(ei, SUBLANES - 1, 0), ci)
                car_r[k] = jnp.broadcast_to(er[0:1, :], (SUBLANES, LANES))
                car_i[k] = jnp.broadcast_to(ei[0:1, :], (SUBLANES, LANES))
            else:
                inr = jnp.where(row >= 1, pltpu.roll(er, 1, 0), cr)
                ini = jnp.where(row >= 1, pltpu.roll(ei, 1, 0), ci)
                car_r[k] = jnp.broadcast_to(er[SUBLANES - 1 : SUBLANES, :], (SUBLANES, LANES))
                car_i[k] = jnp.broadcast_to(ei[SUBLANES - 1 : SUBLANES, :], (SUBLANES, LANES))
            incoming += [inr, ini]

        def second_pass(r, _):
            p = (seg - 1 - r) if reverse else r
            for q, k in enumerate(slabs):
                pr, pi = cmul(pows[0, k, p], sgn * pows[1, k, p], incoming[2 * q], incoming[2 * q + 1])
                zr[k, pl.ds(r, SUBLANES, stride=seg), :] = zr[k, pl.ds(r, SUBLANES, stride=seg), :] + pr
                zi[k, pl.ds(r, SUBLANES, stride=seg), :] = zi[k, pl.ds(r, SUBLANES, stride=seg), :] + pi
            return 0

        lax.fori_loop(0, seg, second_pass, 0)


def _slabs_to_cols(ref, k0, n):
    return jnp.concatenate([ref[k0 + q] for q in range(n)], axis=1)


def _window_sum(ext, doublings, forward):
    rows = ext.shape[0]
    s = ext
    for k in range(doublings):
        s = s + pltpu.roll(s, (1 << k) if forward else rows - (1 << k), 0)
    return s


def _mix_fwd(h1, g_pre, g_so, g_po, g_post, dskip, pscale, win, wout, bbre, bbim, ccre, ccim, wgv, wgg, pw, tabs, pows, name):
    T, D = h1.shape
    W = D // 2
    tm = MIX_TILE
    seg = tm // SUBLANES
    n_slabs = tabs.shape[1]
    nch, cch, sch = bbre.shape
    spc = sch // LANES
    pg = W // len(POOL_WINDOWS)

    def body(h_ref, gpre_ref, gso_ref, gpo_ref, gpost_ref, dskip_ref, pscale_ref, win_ref, wout_ref, bbre_ref, bbim_ref, ccre_ref, ccim_ref, wgv_ref, wgg_ref, pw_ref, tabs_ref, pows_ref,
             proj_ref, xr_ref, xi_ref, y_ref, pooled_ref, mixed_ref, h2_ref, n2_ref, cat_ref, car_r, car_i, halo):
        i = pl.program_id(0)

        @pl.when(i == 0)
        def _():
            car_r[...] = jnp.zeros_like(car_r)
            car_i[...] = jnp.zeros_like(car_i)
            halo[...] = jnp.zeros_like(halo)

        hh = h_ref[...]
        n2 = (hh * _rms_stat(hh) * gpre_ref[...]).astype(MXU_DTYPE)
        n2_ref[...] = n2.astype(n2_ref.dtype)
        proj = _mm(n2, win_ref[...])
        proj_ref[...] = proj
        us, up = proj[:, :W], proj[:, W:]

        for c in range(nch):
            uc = us[:, c * cch : (c + 1) * cch].astype(MXU_DTYPE)
            bur, bui = _mm(uc, bbre_ref[c]), _mm(uc, bbim_ref[c])
            for q in range(spc):
                xr_ref[c * spc + q] = bur[:, q * LANES : (q + 1) * LANES]
                xi_ref[c * spc + q] = bui[:, q * LANES : (q + 1) * LANES]
        _scan_inplace(xr_ref, xi_ref, tabs_ref, pows_ref, car_r, car_i, seg, reverse=False)
        ys = []
        for c in range(nch):
            ys.append(_mm(_slabs_to_cols(xr_ref, c * spc, spc), ccre_ref[c]) - _mm(_slabs_to_cols(xi_ref, c * spc, spc), ccim_ref[c]))
        y = jnp.concatenate(ys, axis=1) + dskip_ref[...] * us
        y_ref[...] = y
        ge = _gelu(y).astype(MXU_DTYPE)
        zv = jnp.concatenate([_mm(ge[:, c * cch : (c + 1) * cch], wgv_ref[c]) for c in range(nch)], axis=1)
        zg = jnp.concatenate([_mm(ge[:, c * cch : (c + 1) * cch], wgg_ref[c]) for c in range(nch)], axis=1)
        out = zv * _sigmoid(zg)
        cat_s = out * _rms_stat(out) * gso_ref[...]

        ext = jnp.concatenate([halo[...], up], axis=0)
        halo[...] = up[tm - POOL_HALO :, :]
        t1 = (i * tm + 1 + lax.broadcasted_iota(jnp.int32, (tm, pg), 0)).astype(F32)
        pooled, pms = [], []
        for g, w in enumerate(POOL_WINDOWS):
            col = ext[:, g * pg : (g + 1) * pg]
            win_sum = _window_sum(col, g + 1, True)[POOL_HALO:, :]
            pooled_g = win_sum / jnp.minimum(t1, float(w)) - up[:, g * pg : (g + 1) * pg]
            pooled.append(pooled_g)
            pms.append(_mm(pooled_g, pw_ref[g]))
        pooled_ref[...] = jnp.concatenate(pooled, axis=1)
        yp = jnp.concatenate(pms, axis=1) * pscale_ref[...]
        cat_p = yp * _rms_stat(yp) * gpo_ref[...]

        cat = jnp.concatenate([cat_s, cat_p], axis=1).astype(MXU_DTYPE)
        cat_ref[...] = cat.astype(cat_ref.dtype)
        mixed = _mm(cat, wout_ref[...])
        mixed_ref[...] = mixed
        h2_ref[...] = hh + mixed * _rms_stat(mixed) * gpost_ref[...]

    tok = lambda cols, dt=F32: jax.ShapeDtypeStruct((T, cols), dt)
    slab_spec = pl.BlockSpec((n_slabs, tm, LANES), lambda i: (0, i, 0))
    operands = (h1, g_pre, g_so, g_po, g_post, dskip, pscale, win, wout, bbre, bbim, ccre, ccim, wgv, wgg, pw, tabs, pows)
    outs, _ = _pallas(
        body,
        name=name,
        grid=(T // tm,),
        out_shape=[tok(D), jax.ShapeDtypeStruct((n_slabs, T, LANES), F32), jax.ShapeDtypeStruct((n_slabs, T, LANES), F32), tok(W), tok(W), tok(D), tok(D), tok(D, MXU_DTYPE), tok(D, MXU_DTYPE)],
        in_specs=[_row_spec(tm, D)] + [_full_spec(o.shape) for o in operands[1:]],
        out_specs=[_row_spec(tm, D), slab_spec, slab_spec, _row_spec(tm, W), _row_spec(tm, W), _row_spec(tm, D), _row_spec(tm, D), _row_spec(tm, D), _row_spec(tm, D)],
        scratch_shapes=[pltpu.VMEM((n_slabs, SUBLANES, LANES), F32), pltpu.VMEM((n_slabs, SUBLANES, LANES), F32), pltpu.VMEM((POOL_HALO, W), F32)],
        operands=operands,
    )
    return outs


def _mix_bwd_heads(dh2, mixed, y, pooled, proj, g_so, g_po, g_post, pscale, wout, wgv, wgg, pw, name, exchange=None):
    T, D = dh2.shape
    W = D // 2
    tm = MIX_TILE
    nch, cch, _ = wgv.shape
    ng, pg, _ = pw.shape

    def body(dh2_ref, mixed_ref, y_ref, pooled_ref, us_ref, gso_ref, gpo_ref, gpost_ref, pscale_ref, wout_ref, wgv_ref, wgg_ref, pw_ref,
             dy_ref, dpooled_ref, dmixed_ref, dgpost_ref, dgso_ref, dgpo_ref, dd_ref, dscale_ref, dwgv_ref, dwgg_ref, dpw_ref):
        first = pl.program_id(0) == 0
        dmixed, dgpost = _rms_bwd(mixed_ref[...], gpost_ref[...], dh2_ref[...])
        _acc(dgpost_ref, dgpost, first)
        dmb = dmixed.astype(MXU_DTYPE)
        dmixed_ref[...] = dmb.astype(dmixed_ref.dtype)
        dcat = _mm_nt(dmb, wout_ref[...])
        dcs, dcp = dcat[:, :W], dcat[:, W:]

        y = y_ref[...]
        ge = _gelu(y).astype(MXU_DTYPE)
        zv = jnp.concatenate([_mm(ge[:, c * cch : (c + 1) * cch], wgv_ref[c]) for c in range(nch)], axis=1)
        zg = jnp.concatenate([_mm(ge[:, c * cch : (c + 1) * cch], wgg_ref[c]) for c in range(nch)], axis=1)
        sg = _sigmoid(zg)
        dout, dgso = _rms_bwd(zv * sg, gso_ref[...], dcs)
        _acc(dgso_ref, dgso, first)
        dzv = (dout * sg).astype(MXU_DTYPE)
        dzg = (dout * zv * sg * (1.0 - sg)).astype(MXU_DTYPE)
        dges = []
        for c in range(nch):
            cs = slice(c * cch, (c + 1) * cch)
            dges.append(_mm_nt(dzv[:, cs], wgv_ref[c]) + _mm_nt(dzg[:, cs], wgg_ref[c]))
            _acc(dwgv_ref.at[c], _mm_tn(ge[:, cs], dzv[:, cs]), first)
            _acc(dwgg_ref.at[c], _mm_tn(ge[:, cs], dzg[:, cs]), first)
        dy = jnp.concatenate(dges, axis=1) * _gelu_grad(y)
        dy_ref[...] = dy
        _acc(dd_ref, jnp.sum(dy * us_ref[...], axis=0, keepdims=True), first)

        pooled_b = pooled_ref[...].astype(MXU_DTYPE)
        pm = jnp.concatenate([_mm(pooled_b[:, g * pg : (g + 1) * pg], pw_ref[g]) for g in range(ng)], axis=1)
        dyp, dgpo = _rms_bwd(pm * pscale_ref[...], gpo_ref[...], dcp)
        _acc(dgpo_ref, dgpo, first)
        _acc(dscale_ref, jnp.sum(dyp * pm, axis=0, keepdims=True), first)
        dpm = (dyp * pscale_ref[...]).astype(MXU_DTYPE)
        dps = []
        for g in range(ng):
            gs = slice(g * pg, (g + 1) * pg)
            dps.append(_mm_nt(dpm[:, gs], pw_ref[g]))
            _acc(dpw_ref.at[g], _mm_tn(pooled_b[:, gs], dpm[:, gs]), first)
        dpooled_ref[...] = jnp.concatenate(dps, axis=1)

    vec = lambda n: jax.ShapeDtypeStruct((1, n), F32)
    operands = (dh2, mixed, y, pooled, proj, g_so, g_po, g_post, pscale, wout, wgv, wgg, pw)
    return _pallas(
        body,
        name=name,
        grid=(T // tm,),
        out_shape=[jax.ShapeDtypeStruct((T, W), F32), jax.ShapeDtypeStruct((T, W), F32), jax.ShapeDtypeStruct((T, D), MXU_DTYPE), vec(D), vec(W), vec(W), vec(W), vec(W),
                   jax.ShapeDtypeStruct(wgv.shape, F32), jax.ShapeDtypeStruct(wgg.shape, F32), jax.ShapeDtypeStruct(pw.shape, F32)],
        in_specs=[_row_spec(tm, D), _row_spec(tm, D), _row_spec(tm, W), _row_spec(tm, W), _row_spec(tm, W)] + [_full_spec(o.shape) for o in operands[5:]],
        out_specs=[_row_spec(tm, W), _row_spec(tm, W), _row_spec(tm, D), _full_spec((1, D)), _full_spec((1, W)), _full_spec((1, W)), _full_spec((1, W)), _full_spec((1, W)),
                   _full_spec(wgv.shape), _full_spec(wgg.shape), _full_spec(pw.shape)],
        operands=operands,
        exchange=exchange,
    )


def _mix_bwd_scan(dy, dpooled, xr, xi, proj, h1, dh2, g_pre, dskip, win, bbre, bbim, ccre, ccim, tabs, pows, name, exchange=None):
    T, D = h1.shape
    W = D // 2
    tm = MIX_TILE
    seg = tm // SUBLANES
    nt = T // tm
    n_slabs = tabs.shape[1]
    nch, cch, sch = bbre.shape
    spc = sch // LANES
    pg = W // len(POOL_WINDOWS)
    blocks_per_tile = tm // SUBLANES

    def body(dy_ref, dp_ref, xr_ref, xi_ref, xpr_ref, xpi_ref, proj_ref, h_ref, dh2_ref, gpre_ref, dskip_ref, win_ref, bbre_ref, bbim_ref, ccre_ref, ccim_ref, tabs_ref, pows_ref,
             dh1_ref, dproj_ref, dgpre_ref, dccre_ref, dccim_ref, dbbre_ref, dbbim_ref, dar_ref, dai_ref, lr, li, car_r, car_i, halo):
        i = pl.program_id(0)
        first = i == 0
        tile = nt - 1 - i
        row = lax.broadcasted_iota(jnp.int32, (SUBLANES, LANES), 0)

        @pl.when(first)
        def _():
            car_r[...] = jnp.zeros_like(car_r)
            car_i[...] = jnp.zeros_like(car_i)
            halo[...] = jnp.zeros_like(halo)
            dar_ref[...] = jnp.zeros_like(dar_ref)
            dai_ref[...] = jnp.zeros_like(dai_ref)

        dy = dy_ref[...]
        for c in range(nch):
            dyc = dy[:, c * cch : (c + 1) * cch].astype(MXU_DTYPE)
            gr, gi = _mm_nt(dyc, ccre_ref[c]), _mm_nt(dyc, ccim_ref[c])
            for q in range(spc):
                lr[c * spc + q] = gr[:, q * LANES : (q + 1) * LANES]
                li[c * spc + q] = -gi[:, q * LANES : (q + 1) * LANES]
            _acc(dccre_ref.at[c], _mm_tn(_slabs_to_cols(xr_ref, c * spc, spc), dyc), first)
            _acc(dccim_ref.at[c], -_mm_tn(_slabs_to_cols(xi_ref, c * spc, spc), dyc), first)
        _scan_inplace(lr, li, tabs_ref, pows_ref, car_r, car_i, seg, reverse=True)

        for k0 in range(0, n_slabs, SLAB_GROUP):
            slabs = range(k0, min(k0 + SLAB_GROUP, n_slabs))
            init = []
            for k in slabs:
                prev_r = jnp.where(tile > 0, jnp.broadcast_to(xpr_ref[k, SUBLANES - 1 : SUBLANES, :], (SUBLANES, LANES)), 0.0)
                prev_i = jnp.where(tile > 0, jnp.broadcast_to(xpi_ref[k, SUBLANES - 1 : SUBLANES, :], (SUBLANES, LANES)), 0.0)
                x0r = jnp.where(row >= 1, pltpu.roll(xr_ref[k, pl.ds(seg - 1, SUBLANES, stride=seg), :], 1, 0), prev_r)
                x0i = jnp.where(row >= 1, pltpu.roll(xi_ref[k, pl.ds(seg - 1, SUBLANES, stride=seg), :], 1, 0), prev_i)
                l0r, l0i = lr[k, pl.ds(0, SUBLANES, stride=seg), :], li[k, pl.ds(0, SUBLANES, stride=seg), :]
                init += [l0r * x0r + l0i * x0i, l0i * x0r - l0r * x0i]

            def step(r, acc, slabs=slabs):
                out = []
                for q, k in enumerate(slabs):
                    pr_, pi_ = xr_ref[k, pl.ds(r - 1, SUBLANES, stride=seg), :], xi_ref[k, pl.ds(r - 1, SUBLANES, stride=seg), :]
                    lr_, li_ = lr[k, pl.ds(r, SUBLANES, stride=seg), :], li[k, pl.ds(r, SUBLANES, stride=seg), :]
                    out += [acc[2 * q] + lr_ * pr_ + li_ * pi_, acc[2 * q + 1] + li_ * pr_ - lr_ * pi_]
                return tuple(out)

            sums = lax.fori_loop(1, seg, step, tuple(init))
            for q, k in enumerate(slabs):
                dar_ref[k] += sums[2 * q]
                dai_ref[k] += sums[2 * q + 1]

        us = proj_ref[:, :W]
        dus = []
        for c in range(nch):
            lrc, lic = _slabs_to_cols(lr, c * spc, spc).astype(MXU_DTYPE), _slabs_to_cols(li, c * spc, spc).astype(MXU_DTYPE)
            uc = us[:, c * cch : (c + 1) * cch].astype(MXU_DTYPE)
            _acc(dbbre_ref.at[c], _mm_tn(uc, lrc), first)
            _acc(dbbim_ref.at[c], _mm_tn(uc, lic), first)
            dus.append(_mm_nt(lrc, bbre_ref[c]) + _mm_nt(lic, bbim_ref[c]))
        du_s = jnp.concatenate(dus, axis=1) + dskip_ref[...] * dy

        dp = dp_ref[...]
        t1 = (tile * tm + 1 + lax.broadcasted_iota(jnp.int32, (tm, pg), 0)).astype(F32)
        dups, heads = [], []
        for g, w in enumerate(POOL_WINDOWS):
            dpg = dp[:, g * pg : (g + 1) * pg]
            qg = dpg / jnp.minimum(t1, float(w))
            ext = jnp.concatenate([qg, halo[:, g * pg : (g + 1) * pg]], axis=0)
            dups.append(_window_sum(ext, g + 1, False)[:tm, :] - dpg)
            heads.append(qg[:POOL_HALO, :])
        halo[...] = jnp.concatenate(heads, axis=1)
        dproj = jnp.concatenate([du_s] + dups, axis=1).astype(MXU_DTYPE)
        dproj_ref[...] = dproj.astype(dproj_ref.dtype)
        dx, dg = _rms_bwd(h_ref[...], gpre_ref[...], _mm_nt(dproj, win_ref[...]))
        _acc(dgpre_ref, dg, first)
        dh1_ref[...] = dh2_ref[...] + dx

    rev = lambda cols: _row_spec(tm, cols, rev_n=nt)
    slab_spec = pl.BlockSpec((n_slabs, tm, LANES), lambda i: (0, nt - 1 - i, 0))
    prev_spec = pl.BlockSpec((n_slabs, SUBLANES, LANES), lambda i: (0, jnp.maximum((nt - 1 - i) * blocks_per_tile - 1, 0), 0))
    consts = (g_pre, dskip, win, bbre, bbim, ccre, ccim, tabs, pows)
    return _pallas(
        body,
        name=name,
        grid=(nt,),
        out_shape=[jax.ShapeDtypeStruct((T, D), F32), jax.ShapeDtypeStruct((T, D), MXU_DTYPE), jax.ShapeDtypeStruct((1, D), F32),
                   jax.ShapeDtypeStruct(ccre.shape, F32), jax.ShapeDtypeStruct(ccim.shape, F32), jax.ShapeDtypeStruct(bbre.shape, F32), jax.ShapeDtypeStruct(bbim.shape, F32),
                   jax.ShapeDtypeStruct((n_slabs, SUBLANES, LANES), F32), jax.ShapeDtypeStruct((n_slabs, SUBLANES, LANES), F32)],
        in_specs=[rev(W), rev(W), slab_spec, slab_spec, prev_spec, prev_spec, rev(D), rev(D), rev(D)] + [_full_spec(o.shape) for o in consts],
        out_specs=[rev(D), rev(D), _full_spec((1, D)), _full_spec(ccre.shape), _full_spec(ccim.shape), _full_spec(bbre.shape), _full_spec(bbim.shape),
                   _full_spec((n_slabs, SUBLANES, LANES)), _full_spec((n_slabs, SUBLANES, LANES))],
        scratch_shapes=[pltpu.VMEM((n_slabs, tm, LANES), F32), pltpu.VMEM((n_slabs, tm, LANES), F32), pltpu.VMEM((n_slabs, SUBLANES, LANES), F32), pltpu.VMEM((n_slabs, SUBLANES, LANES), F32),
                        pltpu.VMEM((POOL_HALO, W), F32)],
        operands=(dy, dpooled, xr, xi, xr, xi, proj, h1, dh2, *consts),
        exchange=exchange,
    )


def _loss_grad(h3, target, name):
    T, D = h3.shape
    tm = FFN_TILE

    def body(h_ref, t_ref, dy_ref, sq_ref):
        i = pl.program_id(0)
        rows = i * tm + lax.broadcasted_iota(jnp.int32, (tm, D), 0)
        err = jnp.where(rows >= N_META, h_ref[...] - t_ref[...], 0.0)
        dy_ref[...] = err * (1.0 / D)
        _acc(sq_ref, jnp.sum(err * err, axis=0, keepdims=True), i == 0)

    outs, _ = _pallas(
        body,
        name=name,
        grid=(T // tm,),
        out_shape=[jax.ShapeDtypeStruct((T, D), F32), jax.ShapeDtypeStruct((1, D), F32)],
        in_specs=[_row_spec(tm, D), _row_spec(tm, D)],
        out_specs=[_row_spec(tm, D), _full_spec((1, D))],
        operands=(h3, target),
    )
    return outs


def _discretize(lam_re, lam_im, log_dt, b_re, b_im):
    dt = jnp.exp(log_dt)[:, None]
    decay = jnp.exp(lam_re * dt)
    ang = lam_im * dt
    a_re, a_im = decay * jnp.cos(ang), decay * jnp.sin(ang)
    nr = a_re - 1.0
    den = lam_re * lam_re + lam_im * lam_im
    q_re = (nr * lam_re + a_im * lam_im) / den
    q_im = (a_im * lam_re - nr * lam_im) / den
    bb_re = q_re[..., None] * b_re - q_im[..., None] * b_im
    bb_im = q_re[..., None] * b_im + q_im[..., None] * b_re
    return a_re, a_im, bb_re, bb_im


GROUPS_PER_CHUNK = 16


def _block_diag(w, rows_first):
    G = w.shape[0]
    nch = G // GROUPS_PER_CHUNK
    if not rows_first:
        w = jnp.swapaxes(w, 1, 2)
    p, q = w.shape[1], w.shape[2]
    eye = jnp.eye(GROUPS_PER_CHUNK, dtype=w.dtype)
    out = jnp.einsum("cgpq,gk->cgpkq", w.reshape(nch, GROUPS_PER_CHUNK, p, q), eye)
    return out.reshape(nch, GROUPS_PER_CHUNK * p, GROUPS_PER_CHUNK * q)


def _block_diag_extract(m, p, q, rows_first):
    nch = m.shape[0]
    eye = jnp.eye(GROUPS_PER_CHUNK, dtype=m.dtype)
    out = jnp.einsum("cgpkq,gk->cgpq", m.reshape(nch, GROUPS_PER_CHUNK, p, GROUPS_PER_CHUNK, q), eye).reshape(nch * GROUPS_PER_CHUNK, p, q)
    return out if rows_first else jnp.swapaxes(out, 1, 2)


def _cmul(ar, ai, br, bi):
    return ar * br - ai * bi, ar * bi + ai * br


def _powers(ar, ai, count):
    pr, pi = ar[None], ai[None]
    while pr.shape[0] < count:
        nr, ni = _cmul(pr, pi, pr[-1][None], pi[-1][None])
        pr, pi = jnp.concatenate([pr, nr]), jnp.concatenate([pi, ni])
    return pr[:count], pi[:count]


def _scan_tables(a_re, a_im, seg):
    n = a_re.size
    ns = n // LANES
    ar, ai = a_re.reshape(n), a_im.reshape(n)
    pr, pi = _powers(ar, ai, seg)
    jr, ji = _powers(pr[-1], pi[-1], SUBLANES)

    def bcast(v):
        return jnp.broadcast_to(v.reshape(ns, 1, LANES), (ns, SUBLANES, LANES))

    def per_sublane(vs):
        return jnp.transpose(vs.reshape(SUBLANES, ns, LANES), (1, 0, 2))

    tabs = jnp.stack([bcast(ar), bcast(ai), bcast(jr[0]), bcast(ji[0]), bcast(jr[1]), bcast(ji[1]), bcast(jr[3]), bcast(ji[3]),
                      per_sublane(jr), per_sublane(ji), per_sublane(jr[::-1]), per_sublane(ji[::-1])])

    def rows(vs):
        return jnp.broadcast_to(jnp.transpose(vs.reshape(seg, ns, 1, LANES), (1, 0, 2, 3)), (ns, seg, SUBLANES, LANES))

    return tabs, jnp.stack([rows(pr), rows(pi)])


SMALL = ("ffn1_pre_norm", "ffn1_post_norm", "mix_pre_norm", "mix_post_norm", "ssm_lambda_re", "ssm_lambda_im", "ssm_log_dt", "ssm_b_re", "ssm_b_im", "ssm_c_re", "ssm_c_im",
         "ssm_d", "ssm_w_glu", "pool_w", "pool_scale", "ssm_out_norm", "pool_out_norm", "ffn2_pre_norm", "ffn2_post_norm")
BIG = ("ffn1_w_gate", "ffn1_w_up", "ffn1_w_down", "w_in", "w_out", "ffn2_w_gate", "ffn2_w_up", "ffn2_w_down")
ORDER = ("meta_tokens", "ffn1_pre_norm", "ffn1_post_norm", "ffn1_w_gate", "ffn1_w_up", "ffn1_w_down", "mix_pre_norm", "mix_post_norm", "w_in", "ssm_lambda_re", "ssm_lambda_im",
         "ssm_log_dt", "ssm_b_re", "ssm_b_im", "ssm_c_re", "ssm_c_im", "ssm_d", "ssm_w_glu", "pool_w", "pool_scale", "ssm_out_norm", "pool_out_norm", "w_out", "ffn2_pre_norm",
         "ffn2_post_norm", "ffn2_w_gate", "ffn2_w_up", "ffn2_w_down")
PACK_ROWS = SUBLANES * 8
def _shards_to_cols(w):
    ns, d, fs = w.shape
    return jnp.transpose(w, (1, 0, 2)).reshape(d, ns * fs)


def _cols_to_shards(w, ns):
    d, f = w.shape
    return jnp.transpose(w.reshape(d, ns, f // ns), (1, 0, 2))


def _pack(arrays, rows):
    flat = jnp.concatenate([a.reshape(-1) for a in arrays])
    return jnp.pad(flat, (0, rows * LANES - flat.size)).reshape(rows, LANES)


def _unpack(packed, shapes):
    flat = packed.reshape(-1)
    out, off = [], 0
    for s in shapes:
        n = math.prod(s)
        out.append(flat[off : off + n].reshape(s))
        off += n
    return out


def _step(p, x, loss_target, m, v):
    D = x.shape[-1]
    chip = (2 * lax.axis_index("x") + lax.axis_index("y")).astype(jnp.int32)
    place = jnp.stack([chip, lax.axis_index("c").astype(jnp.int32)])

    def gather_buffer(w):
        own = w.reshape(1, 2, w.shape[0] // 2, w.shape[1])
        return lax.dynamic_update_slice(lax.empty((4,) + own.shape[1:], own.dtype), own, (chip, 0, 0, 0))

    def gathered_weight(n, g):
        g = g.reshape(4, g.shape[1] * g.shape[2], g.shape[3])
        if n.endswith(("w_gate", "w_up")):
            return _shards_to_cols(g)
        return g.reshape(-1, g.shape[2])

    def grad_view(n, g):
        r, c = p[n].shape[1:]
        if n.endswith(("w_gate", "w_up")):
            g = _cols_to_shards(g, 4)
        return g.reshape(4, 2, r // 2, c)

    def reduce_sum(got_sibling, views, tag):
        sums = [_add_own_half(v_, g_, place, f"{tag}_add_sibling_{k}") for k, (v_, g_) in enumerate(zip(views, got_sibling))]
        return [s[0] for s in sums], [s[1] for s in sums]

    def reduce_halves(parts, got_chips, tag):
        return [_add_chips(p_, g_, place, f"{tag}_add_chips_{k}") for k, (p_, g_) in enumerate(zip(parts, got_chips))]

    first_names = ("ffn1_w_gate", "ffn1_w_up", "ffn1_w_down")
    later_names = ("w_in", "w_out", "ffn2_w_gate", "ffn2_w_up", "ffn2_w_down")
    bufs = {n: gather_buffer(p[n][0].astype(MXU_DTYPE)) for n in BIG}
    got = _exchange_call(_Gather([bufs[n] for n in first_names] + [gather_buffer(p["meta_tokens"])]), "gather_first")
    full = {n: gathered_weight(n, g_) for n, g_ in zip(first_names, got)}
    meta = jnp.transpose(got[-1].reshape(4, N_META, -1), (1, 0, 2)).reshape(N_META, D)

    vec = lambda n: p[n].reshape(1, -1)
    G, N, H = p["ssm_b_re"].shape[1:]
    a_re, a_im, bb_re, bb_im = _discretize(p["ssm_lambda_re"][0], p["ssm_lambda_im"][0], p["ssm_log_dt"][0], p["ssm_b_re"][0], p["ssm_b_im"][0])
    tabs, pows = _scan_tables(a_re, a_im, MIX_TILE // SUBLANES)
    bf = lambda a: a.astype(MXU_DTYPE)
    bbre, bbim = bf(_block_diag(bb_re, False)), bf(_block_diag(bb_im, False))
    ccre, ccim = bf(_block_diag(p["ssm_c_re"][0], False)), bf(_block_diag(p["ssm_c_im"][0], False))
    wgv, wgg = bf(_block_diag(p["ssm_w_glu"][0][:, :, :H], True)), bf(_block_diag(p["ssm_w_glu"][0][:, :, H:], True))
    pw = bf(p["pool_w"][0])

    h0 = jnp.concatenate([meta, x[0]], axis=0)
    n_ffn_steps = h0.shape[0] // FFN_TILE
    (h1, f1, ga1, si1, s1, n1), got = _ffn_fwd(
        h0, vec("ffn1_pre_norm"), vec("ffn1_post_norm"), full["ffn1_w_gate"], full["ffn1_w_up"], full["ffn1_w_down"], "ffn1_fwd",
        exchange=_Gather([bufs[n] for n in later_names], mid_step=(3 * n_ffn_steps) // 4),
    )
    full.update({n: gathered_weight(n, g_) for n, g_ in zip(later_names, got)})
    proj, xr, xim, y, pooled, mixed, h2, n2, cat = _mix_fwd(
        h1, vec("mix_pre_norm"), vec("ssm_out_norm"), vec("pool_out_norm"), vec("mix_post_norm"), vec("ssm_d"), vec("pool_scale"), full["w_in"], full["w_out"],
        bbre, bbim, ccre, ccim, wgv, wgg, pw, tabs, pows, "mix_fwd",
    )
    (h3, f2, ga2, si2, s2, n3), _ = _ffn_fwd(h2, vec("ffn2_pre_norm"), vec("ffn2_post_norm"), full["ffn2_w_gate"], full["ffn2_w_up"], full["ffn2_w_down"], "ffn2_fwd")
    dh3, sq = _loss_grad(h3, jnp.pad(loss_target[0], ((N_META, 0), (0, 0))), "loss_grad")
    loss = lax.psum(0.5 * jnp.sum(sq) / D, ("x", "y", "c"))

    g, shared = {}, {}
    ffn_names = lambda tag: (tag + "_w_gate", tag + "_w_up", tag + "_w_down")

    (da, db, df, g["ffn2_post_norm"]), _ = _ffn_bwd_down(dh3, f2, ga2, si2, vec("ffn2_post_norm"), full["ffn2_w_down"], "ffn2_bwd_down")
    (dh2, g["ffn2_pre_norm"]), _ = _ffn_bwd_up(da, db, h2, dh3, vec("ffn2_pre_norm"), full["ffn2_w_gate"], full["ffn2_w_up"], "ffn2_bwd_up")
    views2 = [
        grad_view("ffn2_w_gate", _tn_matmul(n3, da, "ffn2_dw_gate")[0]),
        grad_view("ffn2_w_up", _tn_matmul(n3, db, "ffn2_dw_up")[0]),
        grad_view("ffn2_w_down", _tn_matmul(s2, df, "ffn2_dw_down")[0]),
    ]
    (dy, dpooled, dmixed, g["mix_post_norm"], g["ssm_out_norm"], g["pool_out_norm"], g["ssm_d"], g["pool_scale"], dwgv, dwgg, g["pool_w"]), got = _mix_bwd_heads(
        dh2, mixed, y, pooled, proj, vec("ssm_out_norm"), vec("pool_out_norm"), vec("mix_post_norm"), vec("pool_scale"), full["w_out"], wgv, wgg, pw, "mix_bwd_heads",
        exchange=_SiblingScatter(views2),
    )
    parts2, wire2 = reduce_sum(got, views2, "ffn2")
    (dh1, dproj, g["mix_pre_norm"], dccre, dccim, dbbre, dbbim, dar, dai), got = _mix_bwd_scan(
        dy, dpooled, xr, xim, proj, h1, dh2, vec("mix_pre_norm"), vec("ssm_d"), full["w_in"], bbre, bbim, ccre, ccim, tabs, pows, "mix_bwd_scan",
        exchange=_ChipScatter(wire2),
    )
    halves2 = reduce_halves(parts2, got, "ffn2")
    dw_in, got = _tn_matmul(n2, dproj, "dw_in", exchange=_SiblingShare(halves2))
    shared.update(zip(ffn_names("ffn2"), got))
    dw_out, _ = _tn_matmul(cat, dmixed, "dw_out")
    views_m = [grad_view("w_in", dw_in), grad_view("w_out", dw_out)]

    (da, db, df, g["ffn1_post_norm"]), got = _ffn_bwd_down(dh1, f1, ga1, si1, vec("ffn1_post_norm"), full["ffn1_w_down"], "ffn1_bwd_down", exchange=_SiblingScatter(views_m))
    parts_m, wire_m = reduce_sum(got, views_m, "mix")
    (dh0, g["ffn1_pre_norm"]), got = _ffn_bwd_up(da, db, h0, dh1, vec("ffn1_pre_norm"), full["ffn1_w_gate"], full["ffn1_w_up"], "ffn1_bwd_up", exchange=_ChipScatter(wire_m))
    halves_m = reduce_halves(parts_m, got, "mix")
    dw_gate, got = _tn_matmul(n1, da, "ffn1_dw_gate", exchange=_SiblingShare(halves_m))
    shared.update(zip(("w_in", "w_out"), got))
    grad_x = dh0[N_META:][None]

    g["ssm_c_re"] = _block_diag_extract(dccre, N, H, False)
    g["ssm_c_im"] = _block_diag_extract(dccim, N, H, False)
    g["ssm_w_glu"] = jnp.concatenate([_block_diag_extract(dwgv, H, H, True), _block_diag_extract(dwgg, H, H, True)], axis=-1)
    d_a_re, d_a_im = jnp.sum(dar, axis=1).reshape(G, N), jnp.sum(dai, axis=1).reshape(G, N)
    _, pull = jax.vjp(_discretize, p["ssm_lambda_re"][0], p["ssm_lambda_im"][0], p["ssm_log_dt"][0], p["ssm_b_re"][0], p["ssm_b_im"][0])
    g["ssm_lambda_re"], g["ssm_lambda_im"], g["ssm_log_dt"], g["ssm_b_re"], g["ssm_b_im"] = pull(
        (d_a_re, d_a_im, _block_diag_extract(dbbre, H, N, False), _block_diag_extract(dbbim, H, N, False))
    )

    small_shapes = [p[n].shape for n in SMALL] + [(N_META, D)]
    small_size = sum(math.prod(s) for s in small_shapes)
    rows = -(-small_size // (LANES * PACK_ROWS)) * PACK_ROWS
    views1 = [
        _pack([g[n] for n in SMALL] + [dh0[:N_META]], rows).reshape(4, 2, rows // 8, LANES),
        grad_view("ffn1_w_gate", dw_gate),
        grad_view("ffn1_w_up", _tn_matmul(n1, db, "ffn1_dw_up")[0]),
        grad_view("ffn1_w_down", _tn_matmul(s1, df, "ffn1_dw_down")[0]),
    ]
    parts1, wire1 = reduce_sum(_exchange_call(_SiblingScatter(views1), "ffn1_reduce_sibling"), views1, "ffn1")
    halves1 = reduce_halves(parts1, _exchange_call(_ChipScatter(wire1), "ffn1_reduce_chips"), "ffn1")
    got = _exchange_call(_SiblingShare(halves1), "ffn1_reduce_share")
    shared.update(zip(ffn_names("ffn1"), got[1:]))
    small_buf = lax.dynamic_update_slice(lax.empty((4,) + got[0].shape, F32), got[0][None], (chip, 0, 0, 0))
    small_all = _exchange_call(_Gather([small_buf]), "gather_small")[0].reshape(rows, LANES)
    grads = dict(zip(SMALL + ("meta_full",), _unpack(small_all, small_shapes)))
    grads["meta_tokens"] = lax.dynamic_slice_in_dim(grads.pop("meta_full"), chip * (D // 4), D // 4, axis=1)
    for n in BIG:
        grads[n] = shared[n].reshape(p[n].shape)

    delta, new_m, new_v = {}, {}, {}
    sm = [_pack([t[n] for n in SMALL], rows) for t in (p, grads, m, v)]
    for out, packed_out in zip((delta, new_m, new_v), _adamw(*sm, "adamw_small")):
        out.update(zip(SMALL, _unpack(packed_out, [p[n].shape for n in SMALL])))
    for n in BIG + ("meta_tokens",):
        shape = p[n].shape
        flat = lambda a: a.reshape(-1, shape[-1])
        d_, m_, v_ = _adamw(flat(p[n]), flat(grads[n]), flat(m[n]), flat(v[n]), "adamw_" + n)
        delta[n], new_m[n], new_v[n] = d_.reshape(shape), m_.reshape(shape), v_.reshape(shape)

    return (loss, grad_x, *[grads[n] for n in ORDER], *[delta[n] for n in ORDER], *[new_m[n] for n in ORDER], *[new_v[n] for n in ORDER])


def kernel(x, meta_tokens, ffn1_pre_norm, ffn1_post_norm, ffn1_w_gate, ffn1_w_up, ffn1_w_down, mix_pre_norm, mix_post_norm, w_in, ssm_lambda_re, ssm_lambda_im, ssm_log_dt, ssm_b_re, ssm_b_im, ssm_c_re, ssm_c_im, ssm_d, ssm_w_glu, pool_w, pool_scale, ssm_out_norm, pool_out_norm, w_out, ffn2_pre_norm, ffn2_post_norm, ffn2_w_gate, ffn2_w_up, ffn2_w_down, loss_target, m_meta_tokens, m_ffn1_pre_norm, m_ffn1_post_norm, m_ffn1_w_gate, m_ffn1_w_up, m_ffn1_w_down, m_mix_pre_norm, m_mix_post_norm, m_w_in, m_ssm_lambda_re, m_ssm_lambda_im, m_ssm_log_dt, m_ssm_b_re, m_ssm_b_im, m_ssm_c_re, m_ssm_c_im, m_ssm_d, m_ssm_w_glu, m_pool_w, m_pool_scale, m_ssm_out_norm, m_pool_out_norm, m_w_out, m_ffn2_pre_norm, m_ffn2_post_norm, m_ffn2_w_gate, m_ffn2_w_up, m_ffn2_w_down, v_meta_tokens, v_ffn1_pre_norm, v_ffn1_post_norm, v_ffn1_w_gate, v_ffn1_w_up, v_ffn1_w_down, v_mix_pre_norm, v_mix_post_norm, v_w_in, v_ssm_lambda_re, v_ssm_lambda_im, v_ssm_log_dt, v_ssm_b_re, v_ssm_b_im, v_ssm_c_re, v_ssm_c_im, v_ssm_d, v_ssm_w_glu, v_pool_w, v_pool_scale, v_ssm_out_norm, v_pool_out_norm, v_w_out, v_ffn2_pre_norm, v_ffn2_post_norm, v_ffn2_w_gate, v_ffn2_w_up, v_ffn2_w_down):
    args = locals()
    p = {n: args[n] for n in ORDER}
    m = {n: args["m_" + n] for n in ORDER}
    v = {n: args["v_" + n] for n in ORDER}
    return _step(p, x, loss_target, m, v)
```

```python
import math

import jax
import jax.numpy as jnp
from jax import lax
from jax.experimental import pallas as pl
from jax.experimental.pallas import tpu as pltpu

F32 = jnp.float32
MXU_DTYPE = jnp.bfloat16
WIRE_DTYPE = jnp.bfloat16

RMS_EPS = 1e-6
N_META = 16
POOL_WINDOWS = (2, 4, 8, 16)
POOL_HALO = 16
ADAM_LR, ADAM_B1, ADAM_B2, ADAM_EPS, ADAM_WD, ADAM_STEP = 0.001, 0.9, 0.999, 1e-08, 0.01, 10

LANES = 128
SUBLANES = 8
VMEM_LIMIT = 60 * 1024 * 1024
FFN_TILE = 432
FFN_CHUNK = 1024
MIX_TILE = 216
SLAB_GROUP = 8
MESH = pl.DeviceIdType.MESH
ANY = pl.BlockSpec(memory_space=pl.ANY)


def _mm(a, b):
    return jnp.dot(a.astype(MXU_DTYPE), b.astype(MXU_DTYPE), preferred_element_type=F32)


def _mm_nt(a, b):
    return lax.dot_general(a.astype(MXU_DTYPE), b.astype(MXU_DTYPE), (((1,), (1,)), ((), ())), preferred_element_type=F32)


def _mm_tn(a, b):
    return lax.dot_general(a.astype(MXU_DTYPE), b.astype(MXU_DTYPE), (((0,), (0,)), ((), ())), preferred_element_type=F32)


def _rms_stat(x):
    return lax.rsqrt(jnp.mean(x * x, axis=-1, keepdims=True) + RMS_EPS)


def _rms_bwd(x, g, dy):
    r = _rms_stat(x)
    xh = x * r
    dg = jnp.sum(dy * xh, axis=0, keepdims=True)
    dxh = dy * g
    dx = r * (dxh - xh * jnp.mean(dxh * xh, axis=-1, keepdims=True))
    return dx, dg


def _sigmoid(x):
    return 1.0 / (1.0 + jnp.exp(-x))


GELU_C = math.sqrt(2.0 / math.pi)
GELU_K = 0.044715


def _gelu(y):
    return 0.5 * y * (1.0 + jnp.tanh(GELU_C * (y + GELU_K * y * y * y)))


def _gelu_grad(y):
    th = jnp.tanh(GELU_C * (y + GELU_K * y * y * y))
    return 0.5 * (1.0 + th) + 0.5 * y * (1.0 - th * th) * GELU_C * (1.0 + 3.0 * GELU_K * y * y)


def _row_spec(tile, cols, rev_n=None):
    if rev_n is None:
        return pl.BlockSpec((tile, cols), lambda i: (i, 0))
    return pl.BlockSpec((tile, cols), lambda i: (rev_n - 1 - i, 0))


def _full_spec(shape):
    zeros = (0,) * len(shape)
    return pl.BlockSpec(shape, lambda *_: zeros)


def _acc(ref, val, first):
    @pl.when(first)
    def _():
        ref[...] = val

    @pl.when(jnp.logical_not(first))
    def _():
        ref[...] += val


def _place():
    x, y, c = lax.axis_index("x"), lax.axis_index("y"), lax.axis_index("c")
    others = [(1 - x, y), (x, 1 - y), (1 - x, 1 - y)]
    return x, y, c, others


class _Exchange:
    mid_step = None

    def __init__(self, ins, out_shapes, aliases, n_sems):
        self.ins, self.out_shapes, self.aliases, self.n_sems = list(ins), list(out_shapes), dict(aliases), n_sems

    def mid(self, ins, outs, send_sems, recv_sems):
        pass


class _SiblingScatter(_Exchange):
    def __init__(self, views):
        super().__init__(views, [jax.ShapeDtypeStruct((4,) + v.shape[2:], v.dtype) for v in views], {}, 4 * len(views))

    def _copies(self, ins, outs, send_sems, recv_sems):
        x, y, c, _ = _place()
        return [
            pltpu.make_async_remote_copy(src_ref=ins[a].at[k, 1 - c], dst_ref=outs[a].at[k], send_sem=send_sems.at[4 * a + k], recv_sem=recv_sems.at[4 * a + k], device_id=(x, y, 1 - c), device_id_type=MESH)
            for a in range(len(ins))
            for k in range(4)
        ]

    def start(self, *refs):
        for cp in self._copies(*refs):
            cp.start()

    def finish(self, *refs):
        cps = self._copies(*refs)
        for cp in cps:
            cp.wait_recv()
        for cp in cps:
            cp.wait_send()


class _ChipScatter(_Exchange):
    def __init__(self, parts):
        super().__init__(parts, [jax.ShapeDtypeStruct((3,) + p.shape[1:], p.dtype) for p in parts], {}, 3 * len(parts))

    def _copies(self, ins, outs, send_sems, recv_sems):
        x, y, c, others = _place()
        return [
            pltpu.make_async_remote_copy(src_ref=ins[a].at[2 * chip[0] + chip[1]], dst_ref=outs[a].at[j], send_sem=send_sems.at[3 * a + j], recv_sem=recv_sems.at[3 * a + j], device_id=(*chip, c), device_id_type=MESH)
            for a in range(len(ins))
            for j, chip in enumerate(others)
        ]

    start = _SiblingScatter.start
    finish = _SiblingScatter.finish


class _SiblingShare(_Exchange):
    def __init__(self, bufs):
        super().__init__(bufs, [jax.ShapeDtypeStruct(b.shape, b.dtype) for b in bufs], {a: a for a in range(len(bufs))}, len(bufs))

    def _copy(self, outs, send_sems, recv_sems, a, half):
        x, y, c, _ = _place()
        mine = outs[a].at[c if half == "mine" else 1 - c]
        return pltpu.make_async_remote_copy(src_ref=mine, dst_ref=mine, send_sem=send_sems.at[a], recv_sem=recv_sems.at[a], device_id=(x, y, 1 - c), device_id_type=MESH)

    def start(self, ins, outs, send_sems, recv_sems):
        for a in range(len(outs)):
            self._copy(outs, send_sems, recv_sems, a, "mine").start()

    def finish(self, ins, outs, send_sems, recv_sems):
        for a in range(len(outs)):
            self._copy(outs, send_sems, recv_sems, a, "theirs").wait_recv()
        for a in range(len(outs)):
            self._copy(outs, send_sems, recv_sems, a, "mine").wait_send()


class _Gather(_Exchange):
    def __init__(self, bufs, mid_step=None):
        super().__init__(bufs, [jax.ShapeDtypeStruct(b.shape, b.dtype) for b in bufs], {a: a for a in range(len(bufs))}, 6 * len(bufs))
        self.mid_step = mid_step

    def _copy(self, outs, send_sems, recv_sems, a, j, chip, half, to):
        blk = outs[a].at[2 * chip[0] + chip[1], half]
        return pltpu.make_async_remote_copy(src_ref=blk, dst_ref=blk, send_sem=send_sems.at[6 * a + j], recv_sem=recv_sems.at[6 * a + j], device_id=to, device_id_type=MESH)

    def start(self, ins, outs, send_sems, recv_sems):
        x, y, c, others = _place()
        for a in range(len(outs)):
            for j, chip in enumerate(others):
                self._copy(outs, send_sems, recv_sems, a, j, (x, y), c, (*chip, c)).start()

    def mid(self, ins, outs, send_sems, recv_sems):
        x, y, c, others = _place()
        for a in range(len(outs)):
            for j, chip in enumerate(others):
                self._copy(outs, send_sems, recv_sems, a, j, chip, c, (x, y, c)).wait_recv()
                self._copy(outs, send_sems, recv_sems, a, 3 + j, chip, c, (x, y, 1 - c)).start()

    def finish(self, ins, outs, send_sems, recv_sems):
        x, y, c, others = _place()
        for a in range(len(outs)):
            for j, chip in enumerate(others):
                self._copy(outs, send_sems, recv_sems, a, 3 + j, chip, 1 - c, (x, y, c)).wait_recv()
        for a in range(len(outs)):
            for j, chip in enumerate(others):
                self._copy(outs, send_sems, recv_sems, a, j, (x, y), c, (*chip, c)).wait_send()
                self._copy(outs, send_sems, recv_sems, a, 3 + j, chip, c, (x, y, 1 - c)).wait_send()


def _exchange_call(ex, name):
    n, m = len(ex.ins), len(ex.out_shapes)

    def body(*refs):
        parts = (refs[:n], refs[n : n + m], refs[n + m], refs[n + m + 1])
        ex.start(*parts)
        ex.mid(*parts)
        ex.finish(*parts)

    return pl.pallas_call(
        body,
        name=name,
        out_shape=ex.out_shapes,
        in_specs=[ANY] * n,
        out_specs=[ANY] * m,
        scratch_shapes=[pltpu.SemaphoreType.DMA((ex.n_sems,)), pltpu.SemaphoreType.DMA((ex.n_sems,))],
        input_output_aliases=ex.aliases,
    )(*ex.ins)


def _pallas(body, *, name, grid, in_specs, out_specs, out_shape, operands, scratch_shapes=(), exchange=None):
    params = pltpu.CompilerParams(dimension_semantics=("arbitrary",) * len(grid), vmem_limit_bytes=VMEM_LIMIT)
    if exchange is None:
        outs = pl.pallas_call(body, name=name, grid=grid, in_specs=in_specs, out_specs=out_specs, out_shape=out_shape, scratch_shapes=list(scratch_shapes), compiler_params=params)(*operands)
        return outs, []
    ex = exchange
    n_in, n_out, n_scr = len(in_specs), len(out_specs), len(scratch_shapes)
    x_in, x_out = len(ex.ins), len(ex.out_shapes)

    def hosted(*refs):
        ins, x_ins = refs[:n_in], refs[n_in : n_in + x_in]
        outs, x_outs = refs[n_in + x_in : n_in + x_in + n_out], refs[n_in + x_in + n_out : n_in + x_in + n_out + x_out]
        rest = refs[n_in + x_in + n_out + x_out :]
        parts = (x_ins, x_outs, rest[n_scr], rest[n_scr + 1])
        ids = [pl.program_id(d) for d in range(len(grid))]
        first = _all([i == 0 for i in ids])
        last = _all([i == g - 1 for i, g in zip(ids, grid)])

        @pl.when(first)
        def _():
            ex.start(*parts)

        body(*ins, *outs, *rest[:n_scr])

        if ex.mid_step is not None:

            @pl.when(ids[0] == ex.mid_step)
            def _():
                ex.mid(*parts)

        @pl.when(last)
        def _():
            ex.finish(*parts)

    outs = pl.pallas_call(
        hosted,
        name=name,
        grid=grid,
        in_specs=list(in_specs) + [ANY] * x_in,
        out_specs=list(out_specs) + [ANY] * x_out,
        out_shape=list(out_shape) + ex.out_shapes,
        scratch_shapes=list(scratch_shapes) + [pltpu.SemaphoreType.DMA((ex.n_sems,)), pltpu.SemaphoreType.DMA((ex.n_sems,))],
        input_output_aliases={n_in + i: n_out + o for i, o in ex.aliases.items()},
        compiler_params=params,
    )(*operands, *ex.ins)
    return outs[:n_out], outs[n_out:]


def _all(conds):
    out = conds[0]
    for c in conds[1:]:
        out = jnp.logical_and(out, c)
    return out


def _row_tile(rows):
    for t in (512, 352, 256, 176, 128, 112, 64, 32, 16, 8):
        if rows % t == 0:
            return t
    return rows


def _add_own_half(view, got, place, name):
    _, _, r, c = view.shape
    tr = _row_tile(r)

    def body(place_ref, v_ref, g_ref, o_ref, w_ref):
        s = v_ref[...] + g_ref[...]
        o_ref[...] = s
        w_ref[...] = s.astype(w_ref.dtype)

    blk = pl.BlockSpec((None, tr, c), lambda k, i, pr: (k, i, 0))
    return pl.pallas_call(
        body,
        name=name,
        out_shape=[jax.ShapeDtypeStruct((4, r, c), F32), jax.ShapeDtypeStruct((4, r, c), WIRE_DTYPE)],
        grid_spec=pltpu.PrefetchScalarGridSpec(
            num_scalar_prefetch=1, grid=(4, r // tr), in_specs=[pl.BlockSpec((None, None, tr, c), lambda k, i, pr: (k, pr[1], i, 0)), blk], out_specs=[blk, blk]
        ),
        compiler_params=pltpu.CompilerParams(dimension_semantics=("arbitrary", "arbitrary"), vmem_limit_bytes=VMEM_LIMIT),
    )(place, view, got)


def _add_chips(part, got, place, name):
    _, r, c = part.shape
    tr = _row_tile(r)

    def body(place_ref, p_ref, g_ref, o_ref):
        o_ref[...] = ((p_ref[...] + g_ref[0].astype(F32)) + g_ref[1].astype(F32)) + g_ref[2].astype(F32)

    return pl.pallas_call(
        body,
        name=name,
        out_shape=jax.ShapeDtypeStruct((2, r, c), F32),
        grid_spec=pltpu.PrefetchScalarGridSpec(
            num_scalar_prefetch=1,
            grid=(r // tr,),
            in_specs=[pl.BlockSpec((None, tr, c), lambda i, pr: (pr[0], i, 0)), pl.BlockSpec((3, tr, c), lambda i, pr: (0, i, 0))],
            out_specs=pl.BlockSpec((None, tr, c), lambda i, pr: (pr[1], i, 0)),
        ),
        compiler_params=pltpu.CompilerParams(dimension_semantics=("arbitrary",), vmem_limit_bytes=VMEM_LIMIT),
    )(place, part, got)


def _adamw_update(w_ref, g_ref, m_ref, v_ref, d_ref, nm_ref, nv_ref):
    g = g_ref[...]
    nm = ADAM_B1 * m_ref[...] + (1.0 - ADAM_B1) * g
    nv = ADAM_B2 * v_ref[...] + (1.0 - ADAM_B2) * (g * g)
    m_hat = nm / (1.0 - ADAM_B1**ADAM_STEP)
    v_hat = nv / (1.0 - ADAM_B2**ADAM_STEP)
    d_ref[...] = -ADAM_LR * (m_hat / (jnp.sqrt(v_hat) + ADAM_EPS) + ADAM_WD * w_ref[...])
    nm_ref[...] = nm
    nv_ref[...] = nv


def _adamw(w, g, m, v, name):
    r, c = w.shape
    tr = _row_tile(r)
    spec = pl.BlockSpec((tr, c), lambda i: (i, 0))
    outs, _ = _pallas(_adamw_update, name=name, grid=(r // tr,), in_specs=[spec] * 4, out_specs=[spec] * 3, out_shape=[jax.ShapeDtypeStruct((r, c), F32)] * 3, operands=(w, g, m, v))
    return outs


def _adamw_many(ws, gs, ms, vs, name):
    n = len(ws)

    def body(*refs):
        for k in range(n):
            _adamw_update(*(refs[j * n + k] for j in range(7)))

    outs = pl.pallas_call(
        body,
        name=name,
        out_shape=[jax.ShapeDtypeStruct(w.shape, F32) for w in ws] * 3,
        in_specs=[pl.BlockSpec(memory_space=pltpu.VMEM)] * (4 * n),
        out_specs=[pl.BlockSpec(memory_space=pltpu.VMEM)] * (3 * n),
    )(*ws, *gs, *ms, *vs)
    return outs[:n], outs[n : 2 * n], outs[2 * n :]


def _load_weights(pairs, sems):
    @pl.when(pl.program_id(0) == 0)
    def _():
        cps = [pltpu.make_async_copy(src, dst, sems.at[k]) for k, (src, dst) in enumerate(pairs)]
        for cp in cps:
            cp.start()
        for cp in cps:
            cp.wait()


def _ffn_chunks(F):
    bounds = list(range(0, F, FFN_CHUNK)) + [F]
    return list(zip(bounds[:-1], bounds[1:]))


def _shifted_specs(tm, cols):
    per = tm // N_META
    return [_row_spec(tm, cols), pl.BlockSpec((N_META, cols), lambda i: (jnp.maximum(i * per - 1, 0), 0))]


def _shifted_tile(cur_ref, before_ref, tm):
    return jnp.concatenate([before_ref[...], cur_ref[0 : tm - N_META, :]], axis=0)


def _ffn_fwd(h, g_pre, g_post, wg, wu, wd, name, exchange=None, meta=None, target=None):
    D = h.shape[1]
    T = h.shape[0] + (0 if meta is None else N_META)
    F = wg.shape[0]
    tm = FFN_TILE
    n_src = 1 if meta is None else 3
    n_tgt = 0 if target is None else 2

    def body(*refs):
        src, refs = refs[:n_src], refs[n_src:]
        tgt, refs = refs[:n_tgt], refs[n_tgt:]
        gpre_ref, gpost_ref, wg_hbm, wu_hbm, wd_hbm, hout_ref, f_ref, ga_ref, si_ref, s_ref, n_ref = refs[:11]
        extra, (wg_v, wu_v, wd_v, sems) = refs[11:-4], refs[-4:]
        i = pl.program_id(0)
        _load_weights([(wg_hbm, wg_v), (wu_hbm, wu_v), (wd_hbm, wd_v)], sems)
        if meta is None:
            hh = src[0][...]
        else:
            hh = jnp.concatenate([jnp.where(i == 0, src[2][...], src[1][...]), src[0][0 : tm - N_META, :]], axis=0)
            extra[0][...] = hh
        n = (hh * _rms_stat(hh) * gpre_ref[...]).astype(MXU_DTYPE)
        n_ref[...] = n.astype(n_ref.dtype)
        f = jnp.zeros((tm, D), F32)
        for lo, hi in _ffn_chunks(F):
            a = _mm_nt(n, wg_v[lo:hi, :])
            b = _mm_nt(n, wu_v[lo:hi, :])
            sg = _sigmoid(a)
            si = a * sg
            s = (si * b).astype(MXU_DTYPE)
            ga_ref[:, lo:hi] = (b * (sg * (1.0 + a * (1.0 - sg)))).astype(ga_ref.dtype)
            si_ref[:, lo:hi] = si.astype(si_ref.dtype)
            s_ref[:, lo:hi] = s.astype(s_ref.dtype)
            f = f + _mm(s, wd_v[lo:hi, :])
        f_ref[...] = f
        out = hh + 0.5 * (f * _rms_stat(f) * gpost_ref[...])
        hout_ref[...] = out
        if target is not None:
            rows = i * tm + lax.broadcasted_iota(jnp.int32, (tm, D), 0)
            err = jnp.where(rows >= N_META, out - _shifted_tile(tgt[0], tgt[1], tm), 0.0)
            extra[-2][...] = err * (1.0 / D)
            _acc(extra[-1], jnp.sum(err * err, axis=0, keepdims=True), i == 0)

    tok = jax.ShapeDtypeStruct((T, D), F32)
    act = jax.ShapeDtypeStruct((T, F), MXU_DTYPE)
    src_specs = [_row_spec(tm, D)] if meta is None else _shifted_specs(tm, D) + [_full_spec((N_META, D))]
    src_ops = (h,) if meta is None else (h, h, meta)
    tgt_specs, tgt_ops = ([], ()) if target is None else (_shifted_specs(tm, D), (target, target))
    extra_shapes = ([] if meta is None else [tok]) + ([] if target is None else [tok, jax.ShapeDtypeStruct((1, D), F32)])
    extra_specs = ([] if meta is None else [_row_spec(tm, D)]) + ([] if target is None else [_row_spec(tm, D), _full_spec((1, D))])
    return _pallas(
        body,
        name=name,
        grid=(T // tm,),
        out_shape=[tok, tok, act, act, act, jax.ShapeDtypeStruct((T, D), MXU_DTYPE)] + extra_shapes,
        in_specs=src_specs + tgt_specs + [_full_spec((1, D)), _full_spec((1, D)), ANY, ANY, ANY],
        out_specs=[_row_spec(tm, D), _row_spec(tm, D), _row_spec(tm, F), _row_spec(tm, F), _row_spec(tm, F), _row_spec(tm, D)] + extra_specs,
        scratch_shapes=[pltpu.VMEM(wg.shape, wg.dtype), pltpu.VMEM(wu.shape, wu.dtype), pltpu.VMEM(wd.shape, wd.dtype), pltpu.SemaphoreType.DMA((3,))],
        operands=(*src_ops, *tgt_ops, g_pre, g_post, wg, wu, wd),
        exchange=exchange,
    )


def _ffn_bwd_down(dh, f, ga, si, g_post, wd, name, exchange=None):
    T, D = dh.shape
    F = wd.shape[0]
    tm = FFN_TILE

    def body(dh_ref, f_ref, ga_ref, si_ref, gpost_ref, wd_hbm, da_ref, db_ref, df_ref, dg_ref, wd_v, sems):
        _load_weights([(wd_hbm, wd_v)], sems)
        df, dg = _rms_bwd(f_ref[...], gpost_ref[...], 0.5 * dh_ref[...])
        _acc(dg_ref, dg, pl.program_id(0) == 0)
        dfb = df.astype(MXU_DTYPE)
        df_ref[...] = dfb.astype(df_ref.dtype)
        for lo, hi in _ffn_chunks(F):
            ds = _mm_nt(dfb, wd_v[lo:hi, :])
            da_ref[:, lo:hi] = (ds * ga_ref[:, lo:hi].astype(F32)).astype(da_ref.dtype)
            db_ref[:, lo:hi] = (ds * si_ref[:, lo:hi].astype(F32)).astype(db_ref.dtype)

    act = jax.ShapeDtypeStruct((T, F), MXU_DTYPE)
    return _pallas(
        body,
        name=name,
        grid=(T // tm,),
        out_shape=[act, act, jax.ShapeDtypeStruct((T, D), MXU_DTYPE), jax.ShapeDtypeStruct((1, D), F32)],
        in_specs=[_row_spec(tm, D), _row_spec(tm, D), _row_spec(tm, F), _row_spec(tm, F), _full_spec((1, D)), ANY],
        out_specs=[_row_spec(tm, F), _row_spec(tm, F), _row_spec(tm, D), _full_spec((1, D))],
        scratch_shapes=[pltpu.VMEM(wd.shape, wd.dtype), pltpu.SemaphoreType.DMA((1,))],
        operands=(dh, f, ga, si, g_post, wd),
        exchange=exchange,
    )


def _ffn_bwd_up(da, db, h, dh, g_pre, wg, wu, name, exchange=None):
    T, D = h.shape
    F = wg.shape[0]
    tm = FFN_TILE

    def body(da_ref, db_ref, h_ref, dh_ref, gpre_ref, wg_hbm, wu_hbm, dhin_ref, dg_ref, wg_v, wu_v, sems):
        _load_weights([(wg_hbm, wg_v), (wu_hbm, wu_v)], sems)
        dn = jnp.zeros((tm, D), F32)
        for lo, hi in _ffn_chunks(F):
            dn = dn + _mm(da_ref[:, lo:hi], wg_v[lo:hi, :]) + _mm(db_ref[:, lo:hi], wu_v[lo:hi, :])
        dx, dg = _rms_bwd(h_ref[...], gpre_ref[...], dn)
        _acc(dg_ref, dg, pl.program_id(0) == 0)
        dhin_ref[...] = dh_ref[...] + dx

    return _pallas(
        body,
        name=name,
        grid=(T // tm,),
        out_shape=[jax.ShapeDtypeStruct((T, D), F32), jax.ShapeDtypeStruct((1, D), F32)],
        in_specs=[_row_spec(tm, F), _row_spec(tm, F), _row_spec(tm, D), _row_spec(tm, D), _full_spec((1, D)), ANY, ANY],
        out_specs=[_row_spec(tm, D), _full_spec((1, D))],
        scratch_shapes=[pltpu.VMEM(wg.shape, wg.dtype), pltpu.VMEM(wu.shape, wu.dtype), pltpu.SemaphoreType.DMA((2,))],
        operands=(da, db, h, dh, g_pre, wg, wu),
        exchange=exchange,
    )


def _token_tile(T):
    for t in (912, 864, 432):
        if T % t == 0:
            return t
    raise ValueError(f"no token tile for {T} rows")


def _tn_matmul(xm, ym, name, exchange=None):
    T, M = xm.shape
    N = ym.shape[1]
    tk = _token_tile(T)

    def body(x_ref, y_ref, o_ref):
        _acc(o_ref, _mm_tn(x_ref[...], y_ref[...]), pl.program_id(0) == 0)

    (out,), x_outs = _pallas(
        body,
        name=name,
        grid=(T // tk,),
        out_shape=[jax.ShapeDtypeStruct((M, N), F32)],
        in_specs=[pl.BlockSpec((tk, M), lambda k: (k, 0)), pl.BlockSpec((tk, N), lambda k: (k, 0))],
        out_specs=[_full_spec((M, N))],
        operands=(xm, ym),
        exchange=exchange,
    )
    return out, x_outs


TAB_A, TAB_AS1, TAB_AS2, TAB_AS4, TAB_JF, TAB_JB = 0, 2, 4, 6, 8, 10


def _scan_inplace(zr, zi, tabs, pows, car_r, car_i, seg, reverse):
    n_slabs = zr.shape[0]
    sgn = -1.0 if reverse else 1.0
    row = lax.broadcasted_iota(jnp.int32, (SUBLANES, LANES), 0)

    def cmul(pr, pi, xr, xi):
        return pr * xr - pi * xi, pr * xi + pi * xr

    for k0 in range(0, n_slabs, SLAB_GROUP):
        slabs = range(k0, min(k0 + SLAB_GROUP, n_slabs))
        ar = [tabs[TAB_A, k] for k in slabs]
        ai = [sgn * tabs[TAB_A + 1, k] for k in slabs]

        def first_pass(t, carry):
            r = (seg - 1 - t) if reverse else t
            out = []
            for q, k in enumerate(slabs):
                xr, xi = carry[2 * q], carry[2 * q + 1]
                pr, pi = cmul(ar[q], ai[q], xr, xi)
                nr = pr + zr[k, pl.ds(r, SUBLANES, stride=seg), :]
                ni = pi + zi[k, pl.ds(r, SUBLANES, stride=seg), :]
                zr[k, pl.ds(r, SUBLANES, stride=seg), :] = nr
                zi[k, pl.ds(r, SUBLANES, stride=seg), :] = ni
                out += [nr, ni]
            return tuple(out)

        ends = lax.fori_loop(0, seg, first_pass, tuple(jnp.zeros((SUBLANES, LANES), F32) for _ in range(2 * len(slabs))))

        incoming = []
        for q, k in enumerate(slabs):
            fr, fi = ends[2 * q], ends[2 * q + 1]
            for d, tab in ((1, TAB_AS1), (2, TAB_AS2), (4, TAB_AS4)):
                shift, keep = (SUBLANES - d, row < SUBLANES - d) if reverse else (d, row >= d)
                sr = jnp.where(keep, pltpu.roll(fr, shift, 0), 0.0)
                si = jnp.where(keep, pltpu.roll(fi, shift, 0), 0.0)
                pr, pi = cmul(tabs[tab, k], sgn * tabs[tab + 1, k], sr, si)
                fr, fi = fr + pr, fi + pi
            cr, ci = car_r[k], car_i[k]
            jtab = TAB_JB if reverse else TAB_JF
            pr, pi = cmul(tabs[jtab, k], sgn * tabs[jtab + 1, k], cr, ci)
            er, ei = fr + pr, fi + pi
            if reverse:
                inr = jnp.where(row < SUBLANES - 1, pltpu.roll(er, SUBLANES - 1, 0), cr)
                ini = jnp.where(row < SUBLANES - 1, pltpu.roll(ei, SUBLANES - 1, 0), ci)
                car_r[k] = jnp.broadcast_to(er[0:1, :], (SUBLANES, LANES))
                car_i[k] = jnp.broadcast_to(ei[0:1, :], (SUBLANES, LANES))
            else:
                inr = jnp.where(row >= 1, pltpu.roll(er, 1, 0), cr)
                ini = jnp.where(row >= 1, pltpu.roll(ei, 1, 0), ci)
                car_r[k] = jnp.broadcast_to(er[SUBLANES - 1 : SUBLANES, :], (SUBLANES, LANES))
                car_i[k] = jnp.broadcast_to(ei[SUBLANES - 1 : SUBLANES, :], (SUBLANES, LANES))
            incoming += [inr, ini]

        def second_pass(r, _):
            p = (seg - 1 - r) if reverse else r
            for q, k in enumerate(slabs):
                pr, pi = cmul(pows[0, k, p], sgn * pows[1, k, p], incoming[2 * q], incoming[2 * q + 1])
                zr[k, pl.ds(r, SUBLANES, stride=seg), :] = zr[k, pl.ds(r, SUBLANES, stride=seg), :] + pr
                zi[k, pl.ds(r, SUBLANES, stride=seg), :] = zi[k, pl.ds(r, SUBLANES, stride=seg), :] + pi
            return 0

        lax.fori_loop(0, seg, second_pass, 0)


def _slabs_to_cols(ref, k0, n):
    return jnp.concatenate([ref[k0 + q] for q in range(n)], axis=1)


def _window_sum(ext, doublings, forward):
    rows = ext.shape[0]
    s = ext
    for k in range(doublings):
        s = s + pltpu.roll(s, (1 << k) if forward else rows - (1 << k), 0)
    return s


def _mix_fwd(h1, g_pre, g_so, g_po, g_post, dskip, pscale, win, wout, bbre, bbim, ccre, ccim, wgv, wgg, pw, tabs, pows, name):
    T, D = h1.shape
    W = D // 2
    tm = MIX_TILE
    seg = tm // SUBLANES
    n_slabs = tabs.shape[1]
    nch, cch, sch = bbre.shape
    spc = sch // LANES
    pg = W // len(POOL_WINDOWS)

    def body(h_ref, gpre_ref, gso_ref, gpo_ref, gpost_ref, dskip_ref, pscale_ref, win_ref, wout_ref, bbre_ref, bbim_ref, ccre_ref, ccim_ref, wgv_ref, wgg_ref, pw_ref, tabs_ref, pows_ref,
             proj_ref, xr_ref, xi_ref, y_ref, pooled_ref, mixed_ref, h2_ref, n2_ref, cat_ref, car_r, car_i, halo):
        i = pl.program_id(0)

        @pl.when(i == 0)
        def _():
            car_r[...] = jnp.zeros_like(car_r)
            car_i[...] = jnp.zeros_like(car_i)
            halo[...] = jnp.zeros_like(halo)

        hh = h_ref[...]
        n2 = (hh * _rms_stat(hh) * gpre_ref[...]).astype(MXU_DTYPE)
        n2_ref[...] = n2.astype(n2_ref.dtype)
        proj = _mm(n2, win_ref[...])
        proj_ref[...] = proj
        us, up = proj[:, :W], proj[:, W:]

        for c in range(nch):
            uc = us[:, c * cch : (c + 1) * cch].astype(MXU_DTYPE)
            bur, bui = _mm(uc, bbre_ref[c]), _mm(uc, bbim_ref[c])
            for q in range(spc):
                xr_ref[c * spc + q] = bur[:, q * LANES : (q + 1) * LANES]
                xi_ref[c * spc + q] = bui[:, q * LANES : (q + 1) * LANES]
        _scan_inplace(xr_ref, xi_ref, tabs_ref, pows_ref, car_r, car_i, seg, reverse=False)
        ys = []
        for c in range(nch):
            ys.append(_mm(_slabs_to_cols(xr_ref, c * spc, spc), ccre_ref[c]) - _mm(_slabs_to_cols(xi_ref, c * spc, spc), ccim_ref[c]))
        y = jnp.concatenate(ys, axis=1) + dskip_ref[...] * us
        y_ref[...] = y
        ge = _gelu(y).astype(MXU_DTYPE)
        zv = jnp.concatenate([_mm(ge[:, c * cch : (c + 1) * cch], wgv_ref[c]) for c in range(nch)], axis=1)
        zg = jnp.concatenate([_mm(ge[:, c * cch : (c + 1) * cch], wgg_ref[c]) for c in range(nch)], axis=1)
        out = zv * _sigmoid(zg)
        cat_s = out * _rms_stat(out) * gso_ref[...]

        ext = jnp.concatenate([halo[...], up], axis=0)
        halo[...] = up[tm - POOL_HALO :, :]
        t1 = (i * tm + 1 + lax.broadcasted_iota(jnp.int32, (tm, pg), 0)).astype(F32)
        pooled, pms = [], []
        for g, w in enumerate(POOL_WINDOWS):
            col = ext[:, g * pg : (g + 1) * pg]
            win_sum = _window_sum(col, g + 1, True)[POOL_HALO:, :]
            pooled_g = win_sum / jnp.minimum(t1, float(w)) - up[:, g * pg : (g + 1) * pg]
            pooled.append(pooled_g)
            pms.append(_mm(pooled_g, pw_ref[g]))
        pooled_ref[...] = jnp.concatenate(pooled, axis=1)
        yp = jnp.concatenate(pms, axis=1) * pscale_ref[...]
        cat_p = yp * _rms_stat(yp) * gpo_ref[...]

        cat = jnp.concatenate([cat_s, cat_p], axis=1).astype(MXU_DTYPE)
        cat_ref[...] = cat.astype(cat_ref.dtype)
        mixed = _mm(cat, wout_ref[...])
        mixed_ref[...] = mixed
        h2_ref[...] = hh + mixed * _rms_stat(mixed) * gpost_ref[...]

    tok = lambda cols, dt=F32: jax.ShapeDtypeStruct((T, cols), dt)
    slab_spec = pl.BlockSpec((n_slabs, tm, LANES), lambda i: (0, i, 0))
    operands = (h1, g_pre, g_so, g_po, g_post, dskip, pscale, win, wout, bbre, bbim, ccre, ccim, wgv, wgg, pw, tabs, pows)
    outs, _ = _pallas(
        body,
        name=name,
        grid=(T // tm,),
        out_shape=[tok(D), jax.ShapeDtypeStruct((n_slabs, T, LANES), F32), jax.ShapeDtypeStruct((n_slabs, T, LANES), F32), tok(W), tok(W), tok(D), tok(D), tok(D, MXU_DTYPE), tok(D, MXU_DTYPE)],
        in_specs=[_row_spec(tm, D)] + [_full_spec(o.shape) for o in operands[1:]],
        out_specs=[_row_spec(tm, D), slab_spec, slab_spec, _row_spec(tm, W), _row_spec(tm, W), _row_spec(tm, D), _row_spec(tm, D), _row_spec(tm, D), _row_spec(tm, D)],
        scratch_shapes=[pltpu.VMEM((n_slabs, SUBLANES, LANES), F32), pltpu.VMEM((n_slabs, SUBLANES, LANES), F32), pltpu.VMEM((POOL_HALO, W), F32)],
        operands=operands,
    )
    return outs


def _mix_bwd_heads(dh2, mixed, y, pooled, proj, g_so, g_po, g_post, pscale, wout, wgv, wgg, pw, name, exchange=None):
    T, D = dh2.shape
    W = D // 2
    tm = MIX_TILE
    nch, cch, _ = wgv.shape
    ng, pg, _ = pw.shape

    def body(dh2_ref, mixed_ref, y_ref, pooled_ref, us_ref, gso_ref, gpo_ref, gpost_ref, pscale_ref, wout_ref, wgv_ref, wgg_ref, pw_ref,
             dy_ref, dpooled_ref, dmixed_ref, dgpost_ref, dgso_ref, dgpo_ref, dd_ref, dscale_ref, dwgv_ref, dwgg_ref, dpw_ref):
        first = pl.program_id(0) == 0
        dmixed, dgpost = _rms_bwd(mixed_ref[...], gpost_ref[...], dh2_ref[...])
        _acc(dgpost_ref, dgpost, first)
        dmb = dmixed.astype(MXU_DTYPE)
        dmixed_ref[...] = dmb.astype(dmixed_ref.dtype)
        dcat = _mm_nt(dmb, wout_ref[...])
        dcs, dcp = dcat[:, :W], dcat[:, W:]

        y = y_ref[...]
        ge = _gelu(y).astype(MXU_DTYPE)
        zv = jnp.concatenate([_mm(ge[:, c * cch : (c + 1) * cch], wgv_ref[c]) for c in range(nch)], axis=1)
        zg = jnp.concatenate([_mm(ge[:, c * cch : (c + 1) * cch], wgg_ref[c]) for c in range(nch)], axis=1)
        sg = _sigmoid(zg)
        dout, dgso = _rms_bwd(zv * sg, gso_ref[...], dcs)
        _acc(dgso_ref, dgso, first)
        dzv = (dout * sg).astype(MXU_DTYPE)
        dzg = (dout * zv * sg * (1.0 - sg)).astype(MXU_DTYPE)
        dges = []
        for c in range(nch):
            cs = slice(c * cch, (c + 1) * cch)
            dges.append(_mm_nt(dzv[:, cs], wgv_ref[c]) + _mm_nt(dzg[:, cs], wgg_ref[c]))
            _acc(dwgv_ref.at[c], _mm_tn(ge[:, cs], dzv[:, cs]), first)
            _acc(dwgg_ref.at[c], _mm_tn(ge[:, cs], dzg[:, cs]), first)
        dy = jnp.concatenate(dges, axis=1) * _gelu_grad(y)
        dy_ref[...] = dy
        _acc(dd_ref, jnp.sum(dy * us_ref[...], axis=0, keepdims=True), first)

        pooled_b = pooled_ref[...].astype(MXU_DTYPE)
        pm = jnp.concatenate([_mm(pooled_b[:, g * pg : (g + 1) * pg], pw_ref[g]) for g in range(ng)], axis=1)
        dyp, dgpo = _rms_bwd(pm * pscale_ref[...], gpo_ref[...], dcp)
        _acc(dgpo_ref, dgpo, first)
        _acc(dscale_ref, jnp.sum(dyp * pm, axis=0, keepdims=True), first)
        dpm = (dyp * pscale_ref[...]).astype(MXU_DTYPE)
        dps = []
        for g in range(ng):
            gs = slice(g * pg, (g + 1) * pg)
            dps.append(_mm_nt(dpm[:, gs], pw_ref[g]))
            _acc(dpw_ref.at[g], _mm_tn(pooled_b[:, gs], dpm[:, gs]), first)
        dpooled_ref[...] = jnp.concatenate(dps, axis=1)

    vec = lambda n: jax.ShapeDtypeStruct((1, n), F32)
    operands = (dh2, mixed, y, pooled, proj, g_so, g_po, g_post, pscale, wout, wgv, wgg, pw)
    return _pallas(
        body,
        name=name,
        grid=(T // tm,),
        out_shape=[jax.ShapeDtypeStruct((T, W), F32), jax.ShapeDtypeStruct((T, W), F32), jax.ShapeDtypeStruct((T, D), MXU_DTYPE), vec(D), vec(W), vec(W), vec(W), vec(W),
                   jax.ShapeDtypeStruct(wgv.shape, F32), jax.ShapeDtypeStruct(wgg.shape, F32), jax.ShapeDtypeStruct(pw.shape, F32)],
        in_specs=[_row_spec(tm, D), _row_spec(tm, D), _row_spec(tm, W), _row_spec(tm, W), _row_spec(tm, W)] + [_full_spec(o.shape) for o in operands[5:]],
        out_specs=[_row_spec(tm, W), _row_spec(tm, W), _row_spec(tm, D), _full_spec((1, D)), _full_spec((1, W)), _full_spec((1, W)), _full_spec((1, W)), _full_spec((1, W)),
                   _full_spec(wgv.shape), _full_spec(wgg.shape), _full_spec(pw.shape)],
        operands=operands,
        exchange=exchange,
    )


def _mix_bwd_scan(dy, dpooled, xr, xi, proj, h1, dh2, g_pre, dskip, win, bbre, bbim, ccre, ccim, tabs, pows, name, exchange=None):
    T, D = h1.shape
    W = D // 2
    tm = MIX_TILE
    seg = tm // SUBLANES
    nt = T // tm
    n_slabs = tabs.shape[1]
    nch, cch, sch = bbre.shape
    spc = sch // LANES
    pg = W // len(POOL_WINDOWS)
    blocks_per_tile = tm // SUBLANES

    def body(dy_ref, dp_ref, xr_ref, xi_ref, xpr_ref, xpi_ref, proj_ref, h_ref, dh2_ref, gpre_ref, dskip_ref, win_ref, bbre_ref, bbim_ref, ccre_ref, ccim_ref, tabs_ref, pows_ref,
             dh1_ref, dproj_ref, dgpre_ref, dccre_ref, dccim_ref, dbbre_ref, dbbim_ref, dar_ref, dai_ref, lr, li, car_r, car_i, halo):
        i = pl.program_id(0)
        first = i == 0
        tile = nt - 1 - i
        row = lax.broadcasted_iota(jnp.int32, (SUBLANES, LANES), 0)

        @pl.when(first)
        def _():
            car_r[...] = jnp.zeros_like(car_r)
            car_i[...] = jnp.zeros_like(car_i)
            halo[...] = jnp.zeros_like(halo)
            dar_ref[...] = jnp.zeros_like(dar_ref)
            dai_ref[...] = jnp.zeros_like(dai_ref)

        dy = dy_ref[...]
        for c in range(nch):
            dyc = dy[:, c * cch : (c + 1) * cch].astype(MXU_DTYPE)
            gr, gi = _mm_nt(dyc, ccre_ref[c]), _mm_nt(dyc, ccim_ref[c])
            for q in range(spc):
                lr[c * spc + q] = gr[:, q * LANES : (q + 1) * LANES]
                li[c * spc + q] = -gi[:, q * LANES : (q + 1) * LANES]
            _acc(dccre_ref.at[c], _mm_tn(_slabs_to_cols(xr_ref, c * spc, spc), dyc), first)
            _acc(dccim_ref.at[c], -_mm_tn(_slabs_to_cols(xi_ref, c * spc, spc), dyc), first)
        _scan_inplace(lr, li, tabs_ref, pows_ref, car_r, car_i, seg, reverse=True)

        for k0 in range(0, n_slabs, SLAB_GROUP):
            slabs = range(k0, min(k0 + SLAB_GROUP, n_slabs))
            init = []
            for k in slabs:
                prev_r = jnp.where(tile > 0, jnp.broadcast_to(xpr_ref[k, SUBLANES - 1 : SUBLANES, :], (SUBLANES, LANES)), 0.0)
                prev_i = jnp.where(tile > 0, jnp.broadcast_to(xpi_ref[k, SUBLANES - 1 : SUBLANES, :], (SUBLANES, LANES)), 0.0)
                x0r = jnp.where(row >= 1, pltpu.roll(xr_ref[k, pl.ds(seg - 1, SUBLANES, stride=seg), :], 1, 0), prev_r)
                x0i = jnp.where(row >= 1, pltpu.roll(xi_ref[k, pl.ds(seg - 1, SUBLANES, stride=seg), :], 1, 0), prev_i)
                l0r, l0i = lr[k, pl.ds(0, SUBLANES, stride=seg), :], li[k, pl.ds(0, SUBLANES, stride=seg), :]
                init += [l0r * x0r + l0i * x0i, l0i * x0r - l0r * x0i]

            def step(r, acc, slabs=slabs):
                out = []
                for q, k in enumerate(slabs):
                    pr_, pi_ = xr_ref[k, pl.ds(r - 1, SUBLANES, stride=seg), :], xi_ref[k, pl.ds(r - 1, SUBLANES, stride=seg), :]
                    lr_, li_ = lr[k, pl.ds(r, SUBLANES, stride=seg), :], li[k, pl.ds(r, SUBLANES, stride=seg), :]
                    out += [acc[2 * q] + lr_ * pr_ + li_ * pi_, acc[2 * q + 1] + li_ * pr_ - lr_ * pi_]
                return tuple(out)

            sums = lax.fori_loop(1, seg, step, tuple(init))
            for q, k in enumerate(slabs):
                dar_ref[k] += sums[2 * q]
                dai_ref[k] += sums[2 * q + 1]

        us = proj_ref[:, :W]
        dus = []
        for c in range(nch):
            lrc, lic = _slabs_to_cols(lr, c * spc, spc).astype(MXU_DTYPE), _slabs_to_cols(li, c * spc, spc).astype(MXU_DTYPE)
            uc = us[:, c * cch : (c + 1) * cch].astype(MXU_DTYPE)
            _acc(dbbre_ref.at[c], _mm_tn(uc, lrc), first)
            _acc(dbbim_ref.at[c], _mm_tn(uc, lic), first)
            dus.append(_mm_nt(lrc, bbre_ref[c]) + _mm_nt(lic, bbim_ref[c]))
        du_s = jnp.concatenate(dus, axis=1) + dskip_ref[...] * dy

        dp = dp_ref[...]
        t1 = (tile * tm + 1 + lax.broadcasted_iota(jnp.int32, (tm, pg), 0)).astype(F32)
        dups, heads = [], []
        for g, w in enumerate(POOL_WINDOWS):
            dpg = dp[:, g * pg : (g + 1) * pg]
            qg = dpg / jnp.minimum(t1, float(w))
            ext = jnp.concatenate([qg, halo[:, g * pg : (g + 1) * pg]], axis=0)
            dups.append(_window_sum(ext, g + 1, False)[:tm, :] - dpg)
            heads.append(qg[:POOL_HALO, :])
        halo[...] = jnp.concatenate(heads, axis=1)
        dproj = jnp.concatenate([du_s] + dups, axis=1).astype(MXU_DTYPE)
        dproj_ref[...] = dproj.astype(dproj_ref.dtype)
        dx, dg = _rms_bwd(h_ref[...], gpre_ref[...], _mm_nt(dproj, win_ref[...]))
        _acc(dgpre_ref, dg, first)
        dh1_ref[...] = dh2_ref[...] + dx

    rev = lambda cols: _row_spec(tm, cols, rev_n=nt)
    slab_spec = pl.BlockSpec((n_slabs, tm, LANES), lambda i: (0, nt - 1 - i, 0))
    prev_spec = pl.BlockSpec((n_slabs, SUBLANES, LANES), lambda i: (0, jnp.maximum((nt - 1 - i) * blocks_per_tile - 1, 0), 0))
    consts = (g_pre, dskip, win, bbre, bbim, ccre, ccim, tabs, pows)
    return _pallas(
        body,
        name=name,
        grid=(nt,),
        out_shape=[jax.ShapeDtypeStruct((T, D), F32), jax.ShapeDtypeStruct((T, D), MXU_DTYPE), jax.ShapeDtypeStruct((1, D), F32),
                   jax.ShapeDtypeStruct(ccre.shape, F32), jax.ShapeDtypeStruct(ccim.shape, F32), jax.ShapeDtypeStruct(bbre.shape, F32), jax.ShapeDtypeStruct(bbim.shape, F32),
                   jax.ShapeDtypeStruct((n_slabs, SUBLANES, LANES), F32), jax.ShapeDtypeStruct((n_slabs, SUBLANES, LANES), F32)],
        in_specs=[rev(W), rev(W), slab_spec, slab_spec, prev_spec, prev_spec, rev(D), rev(D), rev(D)] + [_full_spec(o.shape) for o in consts],
        out_specs=[rev(D), rev(D), _full_spec((1, D)), _full_spec(ccre.shape), _full_spec(ccim.shape), _full_spec(bbre.shape), _full_spec(bbim.shape),
                   _full_spec((n_slabs, SUBLANES, LANES)), _full_spec((n_slabs, SUBLANES, LANES))],
        scratch_shapes=[pltpu.VMEM((n_slabs, tm, LANES), F32), pltpu.VMEM((n_slabs, tm, LANES), F32), pltpu.VMEM((n_slabs, SUBLANES, LANES), F32), pltpu.VMEM((n_slabs, SUBLANES, LANES), F32),
                        pltpu.VMEM((POOL_HALO, W), F32)],
        operands=(dy, dpooled, xr, xi, xr, xi, proj, h1, dh2, *consts),
        exchange=exchange,
    )


def _discretize(lam_re, lam_im, log_dt, b_re, b_im):
    dt = jnp.exp(log_dt)[:, None]
    decay = jnp.exp(lam_re * dt)
    ang = lam_im * dt
    a_re, a_im = decay * jnp.cos(ang), decay * jnp.sin(ang)
    nr = a_re - 1.0
    den = lam_re * lam_re + lam_im * lam_im
    q_re = (nr * lam_re + a_im * lam_im) / den
    q_im = (a_im * lam_re - nr * lam_im) / den
    bb_re = q_re[..., None] * b_re - q_im[..., None] * b_im
    bb_im = q_re[..., None] * b_im + q_im[..., None] * b_re
    return a_re, a_im, bb_re, bb_im


GROUPS_PER_CHUNK = 16


def _block_diag(w, rows_first):
    G = w.shape[0]
    nch = G // GROUPS_PER_CHUNK
    if not rows_first:
        w = jnp.swapaxes(w, 1, 2)
    p, q = w.shape[1], w.shape[2]
    eye = jnp.eye(GROUPS_PER_CHUNK, dtype=w.dtype)
    out = jnp.einsum("cgpq,gk->cgpkq", w.reshape(nch, GROUPS_PER_CHUNK, p, q), eye)
    return out.reshape(nch, GROUPS_PER_CHUNK * p, GROUPS_PER_CHUNK * q)


def _block_diag_extract(m, p, q, rows_first):
    nch = m.shape[0]
    eye = jnp.eye(GROUPS_PER_CHUNK, dtype=m.dtype)
    out = jnp.einsum("cgpkq,gk->cgpq", m.reshape(nch, GROUPS_PER_CHUNK, p, GROUPS_PER_CHUNK, q), eye).reshape(nch * GROUPS_PER_CHUNK, p, q)
    return out if rows_first else jnp.swapaxes(out, 1, 2)


def _cmul(ar, ai, br, bi):
    return ar * br - ai * bi, ar * bi + ai * br


def _powers(ar, ai, count):
    pr, pi = ar[None], ai[None]
    while pr.shape[0] < count:
        nr, ni = _cmul(pr, pi, pr[-1][None], pi[-1][None])
        pr, pi = jnp.concatenate([pr, nr]), jnp.concatenate([pi, ni])
    return pr[:count], pi[:count]


def _scan_tables(a_re, a_im, seg):
    n = a_re.size
    ns = n // LANES
    ar, ai = a_re.reshape(n), a_im.reshape(n)
    pr, pi = _powers(ar, ai, seg)
    jr, ji = _powers(pr[-1], pi[-1], SUBLANES)

    def bcast(v):
        return jnp.broadcast_to(v.reshape(ns, 1, LANES), (ns, SUBLANES, LANES))

    def per_sublane(vs):
        return jnp.transpose(vs.reshape(SUBLANES, ns, LANES), (1, 0, 2))

    tabs = jnp.stack([bcast(ar), bcast(ai), bcast(jr[0]), bcast(ji[0]), bcast(jr[1]), bcast(ji[1]), bcast(jr[3]), bcast(ji[3]),
                      per_sublane(jr), per_sublane(ji), per_sublane(jr[::-1]), per_sublane(ji[::-1])])

    def rows(vs):
        return jnp.broadcast_to(jnp.transpose(vs.reshape(seg, ns, 1, LANES), (1, 0, 2, 3)), (ns, seg, SUBLANES, LANES))

    return tabs, jnp.stack([rows(pr), rows(pi)])


SMALL = ("ffn1_pre_norm", "ffn1_post_norm", "mix_pre_norm", "mix_post_norm", "ssm_lambda_re", "ssm_lambda_im", "ssm_log_dt", "ssm_b_re", "ssm_b_im", "ssm_c_re", "ssm_c_im",
         "ssm_d", "ssm_w_glu", "pool_w", "pool_scale", "ssm_out_norm", "pool_out_norm", "ffn2_pre_norm", "ffn2_post_norm")
BIG = ("ffn1_w_gate", "ffn1_w_up", "ffn1_w_down", "w_in", "w_out", "ffn2_w_gate", "ffn2_w_up", "ffn2_w_down")
ORDER = ("meta_tokens", "ffn1_pre_norm", "ffn1_post_norm", "ffn1_w_gate", "ffn1_w_up", "ffn1_w_down", "mix_pre_norm", "mix_post_norm", "w_in", "ssm_lambda_re", "ssm_lambda_im",
         "ssm_log_dt", "ssm_b_re", "ssm_b_im", "ssm_c_re", "ssm_c_im", "ssm_d", "ssm_w_glu", "pool_w", "pool_scale", "ssm_out_norm", "pool_out_norm", "w_out", "ffn2_pre_norm",
         "ffn2_post_norm", "ffn2_w_gate", "ffn2_w_up", "ffn2_w_down")
PACK_ROWS = SUBLANES * 8
def _pack(arrays, rows):
    flat = jnp.concatenate([a.reshape(-1) for a in arrays])
    return jnp.pad(flat, (0, rows * LANES - flat.size)).reshape(rows, LANES)


def _unpack(packed, shapes):
    flat = packed.reshape(-1)
    out, off = [], 0
    for s in shapes:
        n = math.prod(s)
        out.append(flat[off : off + n].reshape(s))
        off += n
    return out


def _step(p, x, loss_target, m, v):
    D = x.shape[-1]
    chip = (2 * lax.axis_index("x") + lax.axis_index("y")).astype(jnp.int32)
    place = jnp.stack([chip, lax.axis_index("c").astype(jnp.int32)])

    def gather_buffer(w):
        own = w.reshape(1, 2, w.shape[0] // 2, w.shape[1])
        return lax.dynamic_update_slice(lax.empty((4,) + own.shape[1:], own.dtype), own, (chip, 0, 0, 0))

    def rows_of(n, a):
        return jnp.swapaxes(a[0], 0, 1) if n.endswith(("w_gate", "w_up")) else a[0]

    def rows_back(n, a):
        return (jnp.swapaxes(a, 0, 1) if n.endswith(("w_gate", "w_up")) else a)[None]

    def grad_view(g):
        return g.reshape(4, 2, g.shape[0] // 8, g.shape[1])

    def reduce_sum(got_sibling, views, tag):
        sums = [_add_own_half(v_, g_, place, f"{tag}_add_sibling_{k}") for k, (v_, g_) in enumerate(zip(views, got_sibling))]
        return [s[0] for s in sums], [s[1] for s in sums]

    def reduce_halves(parts, got_chips, tag):
        return [_add_chips(p_, g_, place, f"{tag}_add_chips_{k}") for k, (p_, g_) in enumerate(zip(parts, got_chips))]

    first_names = ("ffn1_w_gate", "ffn1_w_up", "ffn1_w_down")
    later_names = ("w_in", "w_out", "ffn2_w_gate", "ffn2_w_up", "ffn2_w_down")
    bufs = {n: gather_buffer(rows_of(n, p[n]).astype(MXU_DTYPE)) for n in BIG}
    gathered_weight = lambda g_: g_.reshape(-1, g_.shape[-1])
    got = _exchange_call(_Gather([bufs[n] for n in first_names] + [gather_buffer(p["meta_tokens"])]), "gather_first")
    full = {n: gathered_weight(g_) for n, g_ in zip(first_names, got)}
    meta = jnp.transpose(got[-1].reshape(4, N_META, -1), (1, 0, 2)).reshape(N_META, D)

    vec = lambda n: p[n].reshape(1, -1)
    G, N, H = p["ssm_b_re"].shape[1:]
    a_re, a_im, bb_re, bb_im = _discretize(p["ssm_lambda_re"][0], p["ssm_lambda_im"][0], p["ssm_log_dt"][0], p["ssm_b_re"][0], p["ssm_b_im"][0])
    tabs, pows = _scan_tables(a_re, a_im, MIX_TILE // SUBLANES)
    bf = lambda a: a.astype(MXU_DTYPE)
    bbre, bbim = bf(_block_diag(bb_re, False)), bf(_block_diag(bb_im, False))
    ccre, ccim = bf(_block_diag(p["ssm_c_re"][0], False)), bf(_block_diag(p["ssm_c_im"][0], False))
    wgv, wgg = bf(_block_diag(p["ssm_w_glu"][0][:, :, :H], True)), bf(_block_diag(p["ssm_w_glu"][0][:, :, H:], True))
    pw = bf(p["pool_w"][0])

    n_ffn_steps = (x.shape[1] + N_META) // FFN_TILE
    (h1, f1, ga1, si1, s1, n1, h0), got = _ffn_fwd(
        x[0], vec("ffn1_pre_norm"), vec("ffn1_post_norm"), full["ffn1_w_gate"], full["ffn1_w_up"], full["ffn1_w_down"], "ffn1_fwd",
        exchange=_Gather([bufs[n] for n in later_names], mid_step=(3 * n_ffn_steps) // 4), meta=meta,
    )
    full.update({n: gathered_weight(g_) for n, g_ in zip(later_names, got)})
    proj, xr, xim, y, pooled, mixed, h2, n2, cat = _mix_fwd(
        h1, vec("mix_pre_norm"), vec("ssm_out_norm"), vec("pool_out_norm"), vec("mix_post_norm"), vec("ssm_d"), vec("pool_scale"), full["w_in"], full["w_out"],
        bbre, bbim, ccre, ccim, wgv, wgg, pw, tabs, pows, "mix_fwd",
    )
    (_, f2, ga2, si2, s2, n3, dh3, sq), _ = _ffn_fwd(
        h2, vec("ffn2_pre_norm"), vec("ffn2_post_norm"), full["ffn2_w_gate"], full["ffn2_w_up"], full["ffn2_w_down"], "ffn2_fwd", target=loss_target[0]
    )
    loss = lax.psum(0.5 * jnp.sum(sq) / D, ("x", "y", "c"))

    g, shared = {}, {}
    ffn_names = lambda tag: (tag + "_w_gate", tag + "_w_up", tag + "_w_down")

    (da, db, df, g["ffn2_post_norm"]), _ = _ffn_bwd_down(dh3, f2, ga2, si2, vec("ffn2_post_norm"), full["ffn2_w_down"], "ffn2_bwd_down")
    (dh2, g["ffn2_pre_norm"]), _ = _ffn_bwd_up(da, db, h2, dh3, vec("ffn2_pre_norm"), full["ffn2_w_gate"], full["ffn2_w_up"], "ffn2_bwd_up")
    views2 = [
        grad_view(_tn_matmul(da, n3, "ffn2_dw_gate")[0]),
        grad_view(_tn_matmul(db, n3, "ffn2_dw_up")[0]),
        grad_view(_tn_matmul(s2, df, "ffn2_dw_down")[0]),
    ]
    (dy, dpooled, dmixed, g["mix_post_norm"], g["ssm_out_norm"], g["pool_out_norm"], g["ssm_d"], g["pool_scale"], dwgv, dwgg, g["pool_w"]), got = _mix_bwd_heads(
        dh2, mixed, y, pooled, proj, vec("ssm_out_norm"), vec("pool_out_norm"), vec("mix_post_norm"), vec("pool_scale"), full["w_out"], wgv, wgg, pw, "mix_bwd_heads",
        exchange=_SiblingScatter(views2),
    )
    parts2, wire2 = reduce_sum(got, views2, "ffn2")
    (dh1, dproj, g["mix_pre_norm"], dccre, dccim, dbbre, dbbim, dar, dai), got = _mix_bwd_scan(
        dy, dpooled, xr, xim, proj, h1, dh2, vec("mix_pre_norm"), vec("ssm_d"), full["w_in"], bbre, bbim, ccre, ccim, tabs, pows, "mix_bwd_scan",
        exchange=_ChipScatter(wire2),
    )
    halves2 = reduce_halves(parts2, got, "ffn2")
    dw_in, got = _tn_matmul(n2, dproj, "dw_in", exchange=_SiblingShare(halves2))
    shared.update(zip(ffn_names("ffn2"), got))
    dw_out, _ = _tn_matmul(cat, dmixed, "dw_out")
    views_m = [grad_view(dw_in), grad_view(dw_out)]

    (da, db, df, g["ffn1_post_norm"]), got = _ffn_bwd_down(dh1, f1, ga1, si1, vec("ffn1_post_norm"), full["ffn1_w_down"], "ffn1_bwd_down", exchange=_SiblingScatter(views_m))
    parts_m, wire_m = reduce_sum(got, views_m, "mix")
    (dh0, g["ffn1_pre_norm"]), got = _ffn_bwd_up(da, db, h0, dh1, vec("ffn1_pre_norm"), full["ffn1_w_gate"], full["ffn1_w_up"], "ffn1_bwd_up", exchange=_ChipScatter(wire_m))
    halves_m = reduce_halves(parts_m, got, "mix")
    dw_gate, got = _tn_matmul(da, n1, "ffn1_dw_gate", exchange=_SiblingShare(halves_m))
    shared.update(zip(("w_in", "w_out"), got))
    grad_x = dh0[N_META:][None]

    g["ssm_c_re"] = _block_diag_extract(dccre, N, H, False)
    g["ssm_c_im"] = _block_diag_extract(dccim, N, H, False)
    g["ssm_w_glu"] = jnp.concatenate([_block_diag_extract(dwgv, H, H, True), _block_diag_extract(dwgg, H, H, True)], axis=-1)
    d_a_re, d_a_im = jnp.sum(dar, axis=1).reshape(G, N), jnp.sum(dai, axis=1).reshape(G, N)
    _, pull = jax.vjp(_discretize, p["ssm_lambda_re"][0], p["ssm_lambda_im"][0], p["ssm_log_dt"][0], p["ssm_b_re"][0], p["ssm_b_im"][0])
    g["ssm_lambda_re"], g["ssm_lambda_im"], g["ssm_log_dt"], g["ssm_b_re"], g["ssm_b_im"] = pull(
        (d_a_re, d_a_im, _block_diag_extract(dbbre, H, N, False), _block_diag_extract(dbbim, H, N, False))
    )

    small_shapes = [p[n].shape for n in SMALL] + [(N_META, D)]
    small_size = sum(math.prod(s) for s in small_shapes)
    rows = -(-small_size // (LANES * PACK_ROWS)) * PACK_ROWS
    views1 = [
        _pack([g[n] for n in SMALL] + [dh0[:N_META]], rows).reshape(4, 2, rows // 8, LANES),
        grad_view(dw_gate),
        grad_view(_tn_matmul(db, n1, "ffn1_dw_up")[0]),
        grad_view(_tn_matmul(s1, df, "ffn1_dw_down")[0]),
    ]
    parts1, wire1 = reduce_sum(_exchange_call(_SiblingScatter(views1), "ffn1_reduce_sibling"), views1, "ffn1")
    halves1 = reduce_halves(parts1, _exchange_call(_ChipScatter(wire1), "ffn1_reduce_chips"), "ffn1")
    got = _exchange_call(_SiblingShare(halves1), "ffn1_reduce_share")
    shared.update(zip(ffn_names("ffn1"), got[1:]))
    small_buf = lax.dynamic_update_slice(lax.empty((4,) + got[0].shape, F32), got[0][None], (chip, 0, 0, 0))
    small_all = _exchange_call(_Gather([small_buf]), "gather_small")[0].reshape(rows, LANES)
    grads = dict(zip(SMALL + ("meta_full",), _unpack(small_all, small_shapes)))
    grads["meta_tokens"] = lax.dynamic_slice_in_dim(grads.pop("meta_full"), chip * (D // 4), D // 4, axis=1)
    delta, new_m, new_v = {}, {}, {}
    for n in BIG:
        g_rows = shared[n].reshape(-1, shared[n].shape[-1])
        outs = _adamw(rows_of(n, p[n]), g_rows, rows_of(n, m[n]), rows_of(n, v[n]), "adamw_" + n)
        grads[n], delta[n], new_m[n], new_v[n] = (rows_back(n, a) for a in (g_rows, *outs))
    delta["meta_tokens"], new_m["meta_tokens"], new_v["meta_tokens"] = _adamw(p["meta_tokens"], grads["meta_tokens"], m["meta_tokens"], v["meta_tokens"], "adamw_meta_tokens")

    def as_2d(n, a):
        a = a.reshape(p[n].shape)[0]
        if n in ("ssm_b_re", "ssm_b_im"):
            a = jnp.swapaxes(a, 1, 2)
        return a.reshape(-1, a.shape[-1])

    def from_2d(n, a):
        if n in ("ssm_b_re", "ssm_b_im"):
            g_, n_, h_ = p[n].shape[1:]
            return jnp.swapaxes(a.reshape(g_, h_, n_), 1, 2)[None]
        return a.reshape(p[n].shape)

    outs = _adamw_many(*[[as_2d(n, t[n]) for n in SMALL] for t in (p, grads, m, v)], "adamw_small")
    for out, arrays in zip((delta, new_m, new_v), outs):
        out.update({n: from_2d(n, a) for n, a in zip(SMALL, arrays)})

    return (loss, grad_x, *[grads[n] for n in ORDER], *[delta[n] for n in ORDER], *[new_m[n] for n in ORDER], *[new_v[n] for n in ORDER])


def kernel(x, meta_tokens, ffn1_pre_norm, ffn1_post_norm, ffn1_w_gate, ffn1_w_up, ffn1_w_down, mix_pre_norm, mix_post_norm, w_in, ssm_lambda_re, ssm_lambda_im, ssm_log_dt, ssm_b_re, ssm_b_im, ssm_c_re, ssm_c_im, ssm_d, ssm_w_glu, pool_w, pool_scale, ssm_out_norm, pool_out_norm, w_out, ffn2_pre_norm, ffn2_post_norm, ffn2_w_gate, ffn2_w_up, ffn2_w_down, loss_target, m_meta_tokens, m_ffn1_pre_norm, m_ffn1_post_norm, m_ffn1_w_gate, m_ffn1_w_up, m_ffn1_w_down, m_mix_pre_norm, m_mix_post_norm, m_w_in, m_ssm_lambda_re, m_ssm_lambda_im, m_ssm_log_dt, m_ssm_b_re, m_ssm_b_im, m_ssm_c_re, m_ssm_c_im, m_ssm_d, m_ssm_w_glu, m_pool_w, m_pool_scale, m_ssm_out_norm, m_pool_out_norm, m_w_out, m_ffn2_pre_norm, m_ffn2_post_norm, m_ffn2_w_gate, m_ffn2_w_up, m_ffn2_w_down, v_meta_tokens, v_ffn1_pre_norm, v_ffn1_post_norm, v_ffn1_w_gate, v_ffn1_w_up, v_ffn1_w_down, v_mix_pre_norm, v_mix_post_norm, v_w_in, v_ssm_lambda_re, v_ssm_lambda_im, v_ssm_log_dt, v_ssm_b_re, v_ssm_b_im, v_ssm_c_re, v_ssm_c_im, v_ssm_d, v_ssm_w_glu, v_pool_w, v_pool_scale, v_ssm_out_norm, v_pool_out_norm, v_w_out, v_ffn2_pre_norm, v_ffn2_post_norm, v_ffn2_w_gate, v_ffn2_w_up, v_ffn2_w_down):
    args = locals()
    p = {n: args[n] for n in ORDER}
    m = {n: args["m_" + n] for n in ORDER}
    v = {n: args["v_" + n] for n in ORDER}
    return _step(p, x, loss_target, m, v)
```

```python
import math

import jax
import jax.numpy as jnp
from jax import lax
from jax.experimental import pallas as pl
from jax.experimental.pallas import tpu as pltpu

F32 = jnp.float32
MXU_DTYPE = jnp.bfloat16
WIRE_DTYPE = jnp.bfloat16

RMS_EPS = 1e-6
N_META = 16
POOL_WINDOWS = (2, 4, 8, 16)
POOL_HALO = 16
ADAM_LR, ADAM_B1, ADAM_B2, ADAM_EPS, ADAM_WD, ADAM_STEP = 0.001, 0.9, 0.999, 1e-08, 0.01, 10

LANES = 128
SUBLANES = 8
VMEM_LIMIT = 60 * 1024 * 1024
FFN_TILE = 432
FFN_CHUNK = 1024
MIX_TILE = 216
SLAB_GROUP = 8
MESH = pl.DeviceIdType.MESH
ANY = pl.BlockSpec(memory_space=pl.ANY)


def _mm(a, b):
    return jnp.dot(a.astype(MXU_DTYPE), b.astype(MXU_DTYPE), preferred_element_type=F32)


def _mm_nt(a, b):
    return lax.dot_general(a.astype(MXU_DTYPE), b.astype(MXU_DTYPE), (((1,), (1,)), ((), ())), preferred_element_type=F32)


def _mm_tn(a, b):
    return lax.dot_general(a.astype(MXU_DTYPE), b.astype(MXU_DTYPE), (((0,), (0,)), ((), ())), preferred_element_type=F32)


def _rms_stat(x):
    return lax.rsqrt(jnp.mean(x * x, axis=-1, keepdims=True) + RMS_EPS)


def _rms_bwd(x, g, dy):
    r = _rms_stat(x)
    xh = x * r
    dg = jnp.sum(dy * xh, axis=0, keepdims=True)
    dxh = dy * g
    dx = r * (dxh - xh * jnp.mean(dxh * xh, axis=-1, keepdims=True))
    return dx, dg


def _sigmoid(x):
    return 1.0 / (1.0 + jnp.exp(-x))


GELU_C = math.sqrt(2.0 / math.pi)
GELU_K = 0.044715


def _gelu(y):
    return 0.5 * y * (1.0 + jnp.tanh(GELU_C * (y + GELU_K * y * y * y)))


def _gelu_grad(y):
    th = jnp.tanh(GELU_C * (y + GELU_K * y * y * y))
    return 0.5 * (1.0 + th) + 0.5 * y * (1.0 - th * th) * GELU_C * (1.0 + 3.0 * GELU_K * y * y)


def _row_spec(tile, cols, rev_n=None):
    if rev_n is None:
        return pl.BlockSpec((tile, cols), lambda i: (i, 0))
    return pl.BlockSpec((tile, cols), lambda i: (rev_n - 1 - i, 0))


def _full_spec(shape):
    zeros = (0,) * len(shape)
    return pl.BlockSpec(shape, lambda *_: zeros)


def _acc(ref, val, first):
    @pl.when(first)
    def _():
        ref[...] = val

    @pl.when(jnp.logical_not(first))
    def _():
        ref[...] += val


def _place():
    x, y, c = lax.axis_index("x"), lax.axis_index("y"), lax.axis_index("c")
    others = [(1 - x, y), (x, 1 - y), (1 - x, 1 - y)]
    return x, y, c, others


class _Exchange:
    mid_step = None

    def __init__(self, ins, out_shapes, aliases, n_sems):
        self.ins, self.out_shapes, self.aliases, self.n_sems = list(ins), list(out_shapes), dict(aliases), n_sems

    def mid(self, ins, outs, send_sems, recv_sems):
        pass


class _SiblingScatter(_Exchange):
    def __init__(self, views):
        super().__init__(views, [jax.ShapeDtypeStruct((4,) + v.shape[2:], v.dtype) for v in views], {}, 4 * len(views))

    def _copies(self, ins, outs, send_sems, recv_sems):
        x, y, c, _ = _place()
        return [
            pltpu.make_async_remote_copy(src_ref=ins[a].at[k, 1 - c], dst_ref=outs[a].at[k], send_sem=send_sems.at[4 * a + k], recv_sem=recv_sems.at[4 * a + k], device_id=(x, y, 1 - c), device_id_type=MESH)
            for a in range(len(ins))
            for k in range(4)
        ]

    def start(self, *refs):
        for cp in self._copies(*refs):
            cp.start()

    def finish(self, *refs):
        cps = self._copies(*refs)
        for cp in cps:
            cp.wait_recv()
        for cp in cps:
            cp.wait_send()


class _ChipScatter(_Exchange):
    def __init__(self, parts):
        super().__init__(parts, [jax.ShapeDtypeStruct((3,) + p.shape[1:], p.dtype) for p in parts], {}, 3 * len(parts))

    def _copies(self, ins, outs, send_sems, recv_sems):
        x, y, c, others = _place()
        return [
            pltpu.make_async_remote_copy(src_ref=ins[a].at[2 * chip[0] + chip[1]], dst_ref=outs[a].at[j], send_sem=send_sems.at[3 * a + j], recv_sem=recv_sems.at[3 * a + j], device_id=(*chip, c), device_id_type=MESH)
            for a in range(len(ins))
            for j, chip in enumerate(others)
        ]

    start = _SiblingScatter.start
    finish = _SiblingScatter.finish


class _SiblingShare(_Exchange):
    def __init__(self, bufs):
        super().__init__(bufs, [jax.ShapeDtypeStruct(b.shape, b.dtype) for b in bufs], {a: a for a in range(len(bufs))}, len(bufs))

    def _copy(self, outs, send_sems, recv_sems, a, half):
        x, y, c, _ = _place()
        mine = outs[a].at[c if half == "mine" else 1 - c]
        return pltpu.make_async_remote_copy(src_ref=mine, dst_ref=mine, send_sem=send_sems.at[a], recv_sem=recv_sems.at[a], device_id=(x, y, 1 - c), device_id_type=MESH)

    def start(self, ins, outs, send_sems, recv_sems):
        for a in range(len(outs)):
            self._copy(outs, send_sems, recv_sems, a, "mine").start()

    def finish(self, ins, outs, send_sems, recv_sems):
        for a in range(len(outs)):
            self._copy(outs, send_sems, recv_sems, a, "theirs").wait_recv()
        for a in range(len(outs)):
            self._copy(outs, send_sems, recv_sems, a, "mine").wait_send()


class _Gather(_Exchange):
    def __init__(self, bufs, mid_step=None):
        super().__init__(bufs, [jax.ShapeDtypeStruct(b.shape, b.dtype) for b in bufs], {a: a for a in range(len(bufs))}, 6 * len(bufs))
        self.mid_step = mid_step

    def _copy(self, outs, send_sems, recv_sems, a, j, chip, half, to):
        blk = outs[a].at[2 * chip[0] + chip[1], half]
        return pltpu.make_async_remote_copy(src_ref=blk, dst_ref=blk, send_sem=send_sems.at[6 * a + j], recv_sem=recv_sems.at[6 * a + j], device_id=to, device_id_type=MESH)

    def start(self, ins, outs, send_sems, recv_sems):
        x, y, c, others = _place()
        for a in range(len(outs)):
            for j, chip in enumerate(others):
                self._copy(outs, send_sems, recv_sems, a, j, (x, y), c, (*chip, c)).start()

    def mid(self, ins, outs, send_sems, recv_sems):
        x, y, c, others = _place()
        for a in range(len(outs)):
            for j, chip in enumerate(others):
                self._copy(outs, send_sems, recv_sems, a, j, chip, c, (x, y, c)).wait_recv()
                self._copy(outs, send_sems, recv_sems, a, 3 + j, chip, c, (x, y, 1 - c)).start()

    def finish(self, ins, outs, send_sems, recv_sems):
        x, y, c, others = _place()
        for a in range(len(outs)):
            for j, chip in enumerate(others):
                self._copy(outs, send_sems, recv_sems, a, 3 + j, chip, 1 - c, (x, y, c)).wait_recv()
        for a in range(len(outs)):
            for j, chip in enumerate(others):
                self._copy(outs, send_sems, recv_sems, a, j, (x, y), c, (*chip, c)).wait_send()
                self._copy(outs, send_sems, recv_sems, a, 3 + j, chip, c, (x, y, 1 - c)).wait_send()


class _SemSlice:
    def __init__(self, sems, off):
        self.sems, self.off = sems, off

    @property
    def at(self):
        return self

    def __getitem__(self, i):
        return self.sems.at[self.off + i]


class _Group(_Exchange):
    def __init__(self, exchanges):
        ins, outs, aliases, n_sems, self.spans = [], [], {}, 0, []
        for ex in exchanges:
            self.spans.append((len(ins), len(outs), n_sems))
            aliases.update({len(ins) + i: len(outs) + o for i, o in ex.aliases.items()})
            ins, outs, n_sems = ins + ex.ins, outs + ex.out_shapes, n_sems + ex.n_sems
        super().__init__(ins, outs, aliases, n_sems)
        self.exchanges = exchanges
        mids = {ex.mid_step for ex in exchanges if ex.mid_step is not None}
        self.mid_step = mids.pop() if mids else None

    def _each(self, method, ins, outs, send_sems, recv_sems):
        for ex, (i0, o0, s0) in zip(self.exchanges, self.spans):
            getattr(ex, method)(ins[i0 : i0 + len(ex.ins)], outs[o0 : o0 + len(ex.out_shapes)], _SemSlice(send_sems, s0), _SemSlice(recv_sems, s0))

    def start(self, *refs):
        self._each("start", *refs)

    def mid(self, *refs):
        self._each("mid", *refs)

    def finish(self, *refs):
        self._each("finish", *refs)

    def split(self, outs):
        return [outs[o0 : o0 + len(ex.out_shapes)] for ex, (_, o0, _) in zip(self.exchanges, self.spans)]


def _exchange_call(ex, name):
    n, m = len(ex.ins), len(ex.out_shapes)

    def body(*refs):
        parts = (refs[:n], refs[n : n + m], refs[n + m], refs[n + m + 1])
        ex.start(*parts)
        ex.mid(*parts)
        ex.finish(*parts)

    return pl.pallas_call(
        body,
        name=name,
        out_shape=ex.out_shapes,
        in_specs=[ANY] * n,
        out_specs=[ANY] * m,
        scratch_shapes=[pltpu.SemaphoreType.DMA((ex.n_sems,)), pltpu.SemaphoreType.DMA((ex.n_sems,))],
        input_output_aliases=ex.aliases,
    )(*ex.ins)


def _pallas(body, *, name, grid, in_specs, out_specs, out_shape, operands, scratch_shapes=(), exchange=None):
    params = pltpu.CompilerParams(dimension_semantics=("arbitrary",) * len(grid), vmem_limit_bytes=VMEM_LIMIT)
    if exchange is None:
        outs = pl.pallas_call(body, name=name, grid=grid, in_specs=in_specs, out_specs=out_specs, out_shape=out_shape, scratch_shapes=list(scratch_shapes), compiler_params=params)(*operands)
        return outs, []
    ex = exchange
    n_in, n_out, n_scr = len(in_specs), len(out_specs), len(scratch_shapes)
    x_in, x_out = len(ex.ins), len(ex.out_shapes)

    def hosted(*refs):
        ins, x_ins = refs[:n_in], refs[n_in : n_in + x_in]
        outs, x_outs = refs[n_in + x_in : n_in + x_in + n_out], refs[n_in + x_in + n_out : n_in + x_in + n_out + x_out]
        rest = refs[n_in + x_in + n_out + x_out :]
        parts = (x_ins, x_outs, rest[n_scr], rest[n_scr + 1])
        ids = [pl.program_id(d) for d in range(len(grid))]
        first = _all([i == 0 for i in ids])
        last = _all([i == g - 1 for i, g in zip(ids, grid)])

        @pl.when(first)
        def _():
            ex.start(*parts)

        body(*ins, *outs, *rest[:n_scr])

        if ex.mid_step is not None:

            @pl.when(ids[0] == ex.mid_step)
            def _():
                ex.mid(*parts)

        @pl.when(last)
        def _():
            ex.finish(*parts)

    outs = pl.pallas_call(
        hosted,
        name=name,
        grid=grid,
        in_specs=list(in_specs) + [ANY] * x_in,
        out_specs=list(out_specs) + [ANY] * x_out,
        out_shape=list(out_shape) + ex.out_shapes,
        scratch_shapes=list(scratch_shapes) + [pltpu.SemaphoreType.DMA((ex.n_sems,)), pltpu.SemaphoreType.DMA((ex.n_sems,))],
        input_output_aliases={n_in + i: n_out + o for i, o in ex.aliases.items()},
        compiler_params=params,
    )(*operands, *ex.ins)
    return outs[:n_out], outs[n_out:]


def _all(conds):
    out = conds[0]
    for c in conds[1:]:
        out = jnp.logical_and(out, c)
    return out


def _row_tile(rows):
    if rows <= 512:
        return rows
    for t in (512, 352, 256, 176, 128, 112, 64, 32, 16, 8):
        if rows % t == 0:
            return t
    return rows


def _add_own_half(view, got, place, name):
    _, _, r, c = view.shape
    tr = _row_tile(r)

    def body(place_ref, v_ref, g_ref, o_ref, w_ref):
        s = v_ref[...] + g_ref[...]
        o_ref[...] = s
        w_ref[...] = s.astype(w_ref.dtype)

    blk = pl.BlockSpec((None, tr, c), lambda k, i, pr: (k, i, 0))
    return pl.pallas_call(
        body,
        name=name,
        out_shape=[jax.ShapeDtypeStruct((4, r, c), F32), jax.ShapeDtypeStruct((4, r, c), WIRE_DTYPE)],
        grid_spec=pltpu.PrefetchScalarGridSpec(
            num_scalar_prefetch=1, grid=(4, r // tr), in_specs=[pl.BlockSpec((None, None, tr, c), lambda k, i, pr: (k, pr[1], i, 0)), blk], out_specs=[blk, blk]
        ),
        compiler_params=pltpu.CompilerParams(dimension_semantics=("arbitrary", "arbitrary"), vmem_limit_bytes=VMEM_LIMIT),
    )(place, view, got)


def _add_chips(part, got, place, name):
    _, r, c = part.shape
    tr = _row_tile(r)

    def body(place_ref, p_ref, g_ref, o_ref):
        o_ref[...] = ((p_ref[...] + g_ref[0].astype(F32)) + g_ref[1].astype(F32)) + g_ref[2].astype(F32)

    return pl.pallas_call(
        body,
        name=name,
        out_shape=jax.ShapeDtypeStruct((2, r, c), F32),
        grid_spec=pltpu.PrefetchScalarGridSpec(
            num_scalar_prefetch=1,
            grid=(r // tr,),
            in_specs=[pl.BlockSpec((None, tr, c), lambda i, pr: (pr[0], i, 0)), pl.BlockSpec((3, tr, c), lambda i, pr: (0, i, 0))],
            out_specs=pl.BlockSpec((None, tr, c), lambda i, pr: (pr[1], i, 0)),
        ),
        compiler_params=pltpu.CompilerParams(dimension_semantics=("arbitrary",), vmem_limit_bytes=VMEM_LIMIT),
    )(place, part, got)


def _adamw_update(w_ref, g_ref, m_ref, v_ref, d_ref, nm_ref, nv_ref):
    g = g_ref[...]
    nm = ADAM_B1 * m_ref[...] + (1.0 - ADAM_B1) * g
    nv = ADAM_B2 * v_ref[...] + (1.0 - ADAM_B2) * (g * g)
    m_hat = nm / (1.0 - ADAM_B1**ADAM_STEP)
    v_hat = nv / (1.0 - ADAM_B2**ADAM_STEP)
    d_ref[...] = -ADAM_LR * (m_hat / (jnp.sqrt(v_hat) + ADAM_EPS) + ADAM_WD * w_ref[...])
    nm_ref[...] = nm
    nv_ref[...] = nv


def _adamw(w, g, m, v, name):
    r, c = w.shape
    tr = _row_tile(r)
    spec = pl.BlockSpec((tr, c), lambda i: (i, 0))
    outs, _ = _pallas(_adamw_update, name=name, grid=(r // tr,), in_specs=[spec] * 4, out_specs=[spec] * 3, out_shape=[jax.ShapeDtypeStruct((r, c), F32)] * 3, operands=(w, g, m, v))
    return outs


def _adamw_many(ws, gs, ms, vs, name):
    n = len(ws)

    def body(*refs):
        for k in range(n):
            _adamw_update(*(refs[j * n + k] for j in range(7)))

    outs = pl.pallas_call(
        body,
        name=name,
        out_shape=[jax.ShapeDtypeStruct(w.shape, F32) for w in ws] * 3,
        in_specs=[pl.BlockSpec(memory_space=pltpu.VMEM)] * (4 * n),
        out_specs=[pl.BlockSpec(memory_space=pltpu.VMEM)] * (3 * n),
    )(*ws, *gs, *ms, *vs)
    return outs[:n], outs[n : 2 * n], outs[2 * n :]


def _load_weights(pairs, sems):
    @pl.when(pl.program_id(0) == 0)
    def _():
        cps = [pltpu.make_async_copy(src, dst, sems.at[k]) for k, (src, dst) in enumerate(pairs)]
        for cp in cps:
            cp.start()
        for cp in cps:
            cp.wait()


def _ffn_chunks(F):
    bounds = list(range(0, F, FFN_CHUNK)) + [F]
    return list(zip(bounds[:-1], bounds[1:]))


def _shifted_specs(tm, cols):
    per = tm // N_META
    return [_row_spec(tm, cols), pl.BlockSpec((N_META, cols), lambda i: (jnp.maximum(i * per - 1, 0), 0))]


def _shifted_tile(cur_ref, before_ref, tm):
    return jnp.concatenate([before_ref[...], cur_ref[0 : tm - N_META, :]], axis=0)


def _ffn_fwd(h, g_pre, g_post, wg, wu, wd, name, exchange=None, meta=None, target=None):
    D = h.shape[1]
    T = h.shape[0] + (0 if meta is None else N_META)
    F = wg.shape[0]
    tm = FFN_TILE
    n_src = 1 if meta is None else 3
    n_tgt = 0 if target is None else 2

    def body(*refs):
        src, refs = refs[:n_src], refs[n_src:]
        tgt, refs = refs[:n_tgt], refs[n_tgt:]
        gpre_ref, gpost_ref, wg_hbm, wu_hbm, wd_hbm, hout_ref, f_ref, ga_ref, si_ref, s_ref, n_ref = refs[:11]
        extra, (wg_v, wu_v, wd_v, sems) = refs[11:-4], refs[-4:]
        i = pl.program_id(0)
        _load_weights([(wg_hbm, wg_v), (wu_hbm, wu_v), (wd_hbm, wd_v)], sems)
        if meta is None:
            hh = src[0][...]
        else:
            hh = jnp.concatenate([jnp.where(i == 0, src[2][...], src[1][...]), src[0][0 : tm - N_META, :]], axis=0)
            extra[0][...] = hh
        n = (hh * _rms_stat(hh) * gpre_ref[...]).astype(MXU_DTYPE)
        n_ref[...] = n.astype(n_ref.dtype)
        f = jnp.zeros((tm, D), F32)
        for lo, hi in _ffn_chunks(F):
            a = _mm_nt(n, wg_v[lo:hi, :])
            b = _mm_nt(n, wu_v[lo:hi, :])
            sg = _sigmoid(a)
            si = a * sg
            s = (si * b).astype(MXU_DTYPE)
            ga_ref[:, lo:hi] = (b * (sg * (1.0 + a * (1.0 - sg)))).astype(ga_ref.dtype)
            si_ref[:, lo:hi] = si.astype(si_ref.dtype)
            s_ref[:, lo:hi] = s.astype(s_ref.dtype)
            f = f + _mm(s, wd_v[lo:hi, :])
        f_ref[...] = f
        out = hh + 0.5 * (f * _rms_stat(f) * gpost_ref[...])
        hout_ref[...] = out
        if target is not None:
            rows = i * tm + lax.broadcasted_iota(jnp.int32, (tm, D), 0)
            err = jnp.where(rows >= N_META, out - _shifted_tile(tgt[0], tgt[1], tm), 0.0)
            extra[-2][...] = err * (1.0 / D)
            _acc(extra[-1], jnp.sum(err * err, axis=0, keepdims=True), i == 0)

    tok = jax.ShapeDtypeStruct((T, D), F32)
    act = jax.ShapeDtypeStruct((T, F), MXU_DTYPE)
    src_specs = [_row_spec(tm, D)] if meta is None else _shifted_specs(tm, D) + [_full_spec((N_META, D))]
    src_ops = (h,) if meta is None else (h, h, meta)
    tgt_specs, tgt_ops = ([], ()) if target is None else (_shifted_specs(tm, D), (target, target))
    extra_shapes = ([] if meta is None else [tok]) + ([] if target is None else [tok, jax.ShapeDtypeStruct((1, D), F32)])
    extra_specs = ([] if meta is None else [_row_spec(tm, D)]) + ([] if target is None else [_row_spec(tm, D), _full_spec((1, D))])
    return _pallas(
        body,
        name=name,
        grid=(T // tm,),
        out_shape=[tok, tok, act, act, act, jax.ShapeDtypeStruct((T, D), MXU_DTYPE)] + extra_shapes,
        in_specs=src_specs + tgt_specs + [_full_spec((1, D)), _full_spec((1, D)), ANY, ANY, ANY],
        out_specs=[_row_spec(tm, D), _row_spec(tm, D), _row_spec(tm, F), _row_spec(tm, F), _row_spec(tm, F), _row_spec(tm, D)] + extra_specs,
        scratch_shapes=[pltpu.VMEM(wg.shape, wg.dtype), pltpu.VMEM(wu.shape, wu.dtype), pltpu.VMEM(wd.shape, wd.dtype), pltpu.SemaphoreType.DMA((3,))],
        operands=(*src_ops, *tgt_ops, g_pre, g_post, wg, wu, wd),
        exchange=exchange,
    )


def _ffn_bwd_down(dh, f, ga, si, g_post, wd, name, exchange=None):
    T, D = dh.shape
    F = wd.shape[0]
    tm = FFN_TILE

    def body(dh_ref, f_ref, ga_ref, si_ref, gpost_ref, wd_hbm, da_ref, db_ref, df_ref, dg_ref, wd_v, sems):
        _load_weights([(wd_hbm, wd_v)], sems)
        df, dg = _rms_bwd(f_ref[...], gpost_ref[...], 0.5 * dh_ref[...])
        _acc(dg_ref, dg, pl.program_id(0) == 0)
        dfb = df.astype(MXU_DTYPE)
        df_ref[...] = dfb.astype(df_ref.dtype)
        for lo, hi in _ffn_chunks(F):
            ds = _mm_nt(dfb, wd_v[lo:hi, :])
            da_ref[:, lo:hi] = (ds * ga_ref[:, lo:hi].astype(F32)).astype(da_ref.dtype)
            db_ref[:, lo:hi] = (ds * si_ref[:, lo:hi].astype(F32)).astype(db_ref.dtype)

    act = jax.ShapeDtypeStruct((T, F), MXU_DTYPE)
    return _pallas(
        body,
        name=name,
        grid=(T // tm,),
        out_shape=[act, act, jax.ShapeDtypeStruct((T, D), MXU_DTYPE), jax.ShapeDtypeStruct((1, D), F32)],
        in_specs=[_row_spec(tm, D), _row_spec(tm, D), _row_spec(tm, F), _row_spec(tm, F), _full_spec((1, D)), ANY],
        out_specs=[_row_spec(tm, F), _row_spec(tm, F), _row_spec(tm, D), _full_spec((1, D))],
        scratch_shapes=[pltpu.VMEM(wd.shape, wd.dtype), pltpu.SemaphoreType.DMA((1,))],
        operands=(dh, f, ga, si, g_post, wd),
        exchange=exchange,
    )


def _ffn_bwd_up(da, db, h, dh, g_pre, wg, wu, name, exchange=None):
    T, D = h.shape
    F = wg.shape[0]
    tm = FFN_TILE

    def body(da_ref, db_ref, h_ref, dh_ref, gpre_ref, wg_hbm, wu_hbm, dhin_ref, dg_ref, wg_v, wu_v, sems):
        _load_weights([(wg_hbm, wg_v), (wu_hbm, wu_v)], sems)
        dn = jnp.zeros((tm, D), F32)
        for lo, hi in _ffn_chunks(F):
            dn = dn + _mm(da_ref[:, lo:hi], wg_v[lo:hi, :]) + _mm(db_ref[:, lo:hi], wu_v[lo:hi, :])
        dx, dg = _rms_bwd(h_ref[...], gpre_ref[...], dn)
        _acc(dg_ref, dg, pl.program_id(0) == 0)
        dhin_ref[...] = dh_ref[...] + dx

    return _pallas(
        body,
        name=name,
        grid=(T // tm,),
        out_shape=[jax.ShapeDtypeStruct((T, D), F32), jax.ShapeDtypeStruct((1, D), F32)],
        in_specs=[_row_spec(tm, F), _row_spec(tm, F), _row_spec(tm, D), _row_spec(tm, D), _full_spec((1, D)), ANY, ANY],
        out_specs=[_row_spec(tm, D), _full_spec((1, D))],
        scratch_shapes=[pltpu.VMEM(wg.shape, wg.dtype), pltpu.VMEM(wu.shape, wu.dtype), pltpu.SemaphoreType.DMA((2,))],
        operands=(da, db, h, dh, g_pre, wg, wu),
        exchange=exchange,
    )


def _token_tile(T):
    for t in (912, 864, 432):
        if T % t == 0:
            return t
    raise ValueError(f"no token tile for {T} rows")


def _tn_matmul(xm, ym, name, exchange=None):
    T, M = xm.shape
    N = ym.shape[1]
    tk = _token_tile(T)

    def body(x_ref, y_ref, o_ref):
        _acc(o_ref, _mm_tn(x_ref[...], y_ref[...]), pl.program_id(0) == 0)

    (out,), x_outs = _pallas(
        body,
        name=name,
        grid=(T // tk,),
        out_shape=[jax.ShapeDtypeStruct((M, N), F32)],
        in_specs=[pl.BlockSpec((tk, M), lambda k: (k, 0)), pl.BlockSpec((tk, N), lambda k: (k, 0))],
        out_specs=[_full_spec((M, N))],
        operands=(xm, ym),
        exchange=exchange,
    )
    return out, x_outs


TAB_A, TAB_AS1, TAB_AS2, TAB_AS4, TAB_JF, TAB_JB = 0, 2, 4, 6, 8, 10


def _scan_inplace(zr, zi, tabs, pows, car_r, car_i, seg, reverse):
    n_slabs = zr.shape[0]
    sgn = -1.0 if reverse else 1.0
    row = lax.broadcasted_iota(jnp.int32, (SUBLANES, LANES), 0)

    def cmul(pr, pi, xr, xi):
        return pr * xr - pi * xi, pr * xi + pi * xr

    for k0 in range(0, n_slabs, SLAB_GROUP):
        slabs = range(k0, min(k0 + SLAB_GROUP, n_slabs))
        ar = [tabs[TAB_A, k] for k in slabs]
        ai = [sgn * tabs[TAB_A + 1, k] for k in slabs]

        def first_pass(t, carry):
            r = (seg - 1 - t) if reverse else t
            out = []
            for q, k in enumerate(slabs):
                xr, xi = carry[2 * q], carry[2 * q + 1]
                pr, pi = cmul(ar[q], ai[q], xr, xi)
                nr = pr + zr[k, pl.ds(r, SUBLANES, stride=seg), :]
                ni = pi + zi[k, pl.ds(r, SUBLANES, stride=seg), :]
                zr[k, pl.ds(r, SUBLANES, stride=seg), :] = nr
                zi[k, pl.ds(r, SUBLANES, stride=seg), :] = ni
                out += [nr, ni]
            return tuple(out)

        ends = lax.fori_loop(0, seg, first_pass, tuple(jnp.zeros((SUBLANES, LANES), F32) for _ in range(2 * len(slabs))))

        incoming = []
        for q, k in enumerate(slabs):
            fr, fi = ends[2 * q], ends[2 * q + 1]
            for d, tab in ((1, TAB_AS1), (2, TAB_AS2), (4, TAB_AS4)):
                shift, keep = (SUBLANES - d, row < SUBLANES - d) if reverse else (d, row >= d)
                sr = jnp.where(keep, pltpu.roll(fr, shift, 0), 0.0)
                si = jnp.where(keep, pltpu.roll(fi, shift, 0), 0.0)
                pr, pi = cmul(tabs[tab, k], sgn * tabs[tab + 1, k], sr, si)
                fr, fi = fr + pr, fi + pi
            cr, ci = car_r[k], car_i[k]
            jtab = TAB_JB if reverse else TAB_JF
            pr, pi = cmul(tabs[jtab, k], sgn * tabs[jtab + 1, k], cr, ci)
            er, ei = fr + pr, fi + pi
            if reverse:
                inr = jnp.where(row < SUBLANES - 1, pltpu.roll(er, SUBLANES - 1, 0), cr)
                ini = jnp.where(row < SUBLANES - 1, pltpu.roll(ei, SUBLANES - 1, 0), ci)
                car_r[k] = jnp.broadcast_to(er[0:1, :], (SUBLANES, LANES))
                car_i[k] = jnp.broadcast_to(ei[0:1, :], (SUBLANES, LANES))
            else:
                inr = jnp.where(row >= 1, pltpu.roll(er, 1, 0), cr)
                ini = jnp.where(row >= 1, pltpu.roll(ei, 1, 0), ci)
                car_r[k] = jnp.broadcast_to(er[SUBLANES - 1 : SUBLANES, :], (SUBLANES, LANES))
                car_i[k] = jnp.broadcast_to(ei[SUBLANES - 1 : SUBLANES, :], (SUBLANES, LANES))
            incoming += [inr, ini]

        def second_pass(r, _):
            p = (seg - 1 - r) if reverse else r
            for q, k in enumerate(slabs):
                pr, pi = cmul(pows[0, k, p], sgn * pows[1, k, p], incoming[2 * q], incoming[2 * q + 1])
                zr[k, pl.ds(r, SUBLANES, stride=seg), :] = zr[k, pl.ds(r, SUBLANES, stride=seg), :] + pr
                zi[k, pl.ds(r, SUBLANES, stride=seg), :] = zi[k, pl.ds(r, SUBLANES, stride=seg), :] + pi
            return 0

        lax.fori_loop(0, seg, second_pass, 0)


def _slabs_to_cols(ref, k0, n):
    return jnp.concatenate([ref[k0 + q] for q in range(n)], axis=1)


def _window_sum(ext, doublings, forward):
    rows = ext.shape[0]
    s = ext
    for k in range(doublings):
        s = s + pltpu.roll(s, (1 << k) if forward else rows - (1 << k), 0)
    return s


def _mix_fwd(h1, g_pre, g_so, g_po, g_post, dskip, pscale, win, wout, bbre, bbim, ccre, ccim, wgv, wgg, pw, tabs, pows, name):
    T, D = h1.shape
    W = D // 2
    tm = MIX_TILE
    seg = tm // SUBLANES
    n_slabs = tabs.shape[1]
    nch, cch, sch = bbre.shape
    spc = sch // LANES
    pg = W // len(POOL_WINDOWS)

    def body(h_ref, gpre_ref, gso_ref, gpo_ref, gpost_ref, dskip_ref, pscale_ref, win_ref, wout_ref, bbre_ref, bbim_ref, ccre_ref, ccim_ref, wgv_ref, wgg_ref, pw_ref, tabs_ref, pows_ref,
             proj_ref, xr_ref, xi_ref, y_ref, pooled_ref, mixed_ref, h2_ref, n2_ref, cat_ref, car_r, car_i, halo):
        i = pl.program_id(0)

        @pl.when(i == 0)
        def _():
            car_r[...] = jnp.zeros_like(car_r)
            car_i[...] = jnp.zeros_like(car_i)
            halo[...] = jnp.zeros_like(halo)

        hh = h_ref[...]
        n2 = (hh * _rms_stat(hh) * gpre_ref[...]).astype(MXU_DTYPE)
        n2_ref[...] = n2.astype(n2_ref.dtype)
        proj = _mm(n2, win_ref[...])
        proj_ref[...] = proj
        us, up = proj[:, :W], proj[:, W:]

        for c in range(nch):
            uc = us[:, c * cch : (c + 1) * cch].astype(MXU_DTYPE)
            bur, bui = _mm(uc, bbre_ref[c]), _mm(uc, bbim_ref[c])
            for q in range(spc):
                xr_ref[c * spc + q] = bur[:, q * LANES : (q + 1) * LANES]
                xi_ref[c * spc + q] = bui[:, q * LANES : (q + 1) * LANES]
        _scan_inplace(xr_ref, xi_ref, tabs_ref, pows_ref, car_r, car_i, seg, reverse=False)
        ys = []
        for c in range(nch):
            ys.append(_mm(_slabs_to_cols(xr_ref, c * spc, spc), ccre_ref[c]) - _mm(_slabs_to_cols(xi_ref, c * spc, spc), ccim_ref[c]))
        y = jnp.concatenate(ys, axis=1) + dskip_ref[...] * us
        y_ref[...] = y
        ge = _gelu(y).astype(MXU_DTYPE)
        zv = jnp.concatenate([_mm(ge[:, c * cch : (c + 1) * cch], wgv_ref[c]) for c in range(nch)], axis=1)
        zg = jnp.concatenate([_mm(ge[:, c * cch : (c + 1) * cch], wgg_ref[c]) for c in range(nch)], axis=1)
        out = zv * _sigmoid(zg)
        cat_s = out * _rms_stat(out) * gso_ref[...]

        ext = jnp.concatenate([halo[...], up], axis=0)
        halo[...] = up[tm - POOL_HALO :, :]
        t1 = (i * tm + 1 + lax.broadcasted_iota(jnp.int32, (tm, pg), 0)).astype(F32)
        pooled, pms = [], []
        for g, w in enumerate(POOL_WINDOWS):
            col = ext[:, g * pg : (g + 1) * pg]
            win_sum = _window_sum(col, g + 1, True)[POOL_HALO:, :]
            pooled_g = win_sum / jnp.minimum(t1, float(w)) - up[:, g * pg : (g + 1) * pg]
            pooled.append(pooled_g)
            pms.append(_mm(pooled_g, pw_ref[g]))
        pooled_ref[...] = jnp.concatenate(pooled, axis=1)
        yp = jnp.concatenate(pms, axis=1) * pscale_ref[...]
        cat_p = yp * _rms_stat(yp) * gpo_ref[...]

        cat = jnp.concatenate([cat_s, cat_p], axis=1).astype(MXU_DTYPE)
        cat_ref[...] = cat.astype(cat_ref.dtype)
        mixed = _mm(cat, wout_ref[...])
        mixed_ref[...] = mixed
        h2_ref[...] = hh + mixed * _rms_stat(mixed) * gpost_ref[...]

    tok = lambda cols, dt=F32: jax.ShapeDtypeStruct((T, cols), dt)
    slab_spec = pl.BlockSpec((n_slabs, tm, LANES), lambda i: (0, i, 0))
    operands = (h1, g_pre, g_so, g_po, g_post, dskip, pscale, win, wout, bbre, bbim, ccre, ccim, wgv, wgg, pw, tabs, pows)
    outs, _ = _pallas(
        body,
        name=name,
        grid=(T // tm,),
        out_shape=[tok(D), jax.ShapeDtypeStruct((n_slabs, T, LANES), F32), jax.ShapeDtypeStruct((n_slabs, T, LANES), F32), tok(W), tok(W), tok(D), tok(D), tok(D, MXU_DTYPE), tok(D, MXU_DTYPE)],
        in_specs=[_row_spec(tm, D)] + [_full_spec(o.shape) for o in operands[1:]],
        out_specs=[_row_spec(tm, D), slab_spec, slab_spec, _row_spec(tm, W), _row_spec(tm, W), _row_spec(tm, D), _row_spec(tm, D), _row_spec(tm, D), _row_spec(tm, D)],
        scratch_shapes=[pltpu.VMEM((n_slabs, SUBLANES, LANES), F32), pltpu.VMEM((n_slabs, SUBLANES, LANES), F32), pltpu.VMEM((POOL_HALO, W), F32)],
        operands=operands,
    )
    return outs


def _mix_bwd_heads(dh2, mixed, y, pooled, proj, g_so, g_po, g_post, pscale, wout, wgv, wgg, pw, name, exchange=None):
    T, D = dh2.shape
    W = D // 2
    tm = MIX_TILE
    nch, cch, _ = wgv.shape
    ng, pg, _ = pw.shape

    def body(dh2_ref, mixed_ref, y_ref, pooled_ref, us_ref, gso_ref, gpo_ref, gpost_ref, pscale_ref, wout_ref, wgv_ref, wgg_ref, pw_ref,
             dy_ref, dpooled_ref, dmixed_ref, dgpost_ref, dgso_ref, dgpo_ref, dd_ref, dscale_ref, dwgv_ref, dwgg_ref, dpw_ref):
        first = pl.program_id(0) == 0
        dmixed, dgpost = _rms_bwd(mixed_ref[...], gpost_ref[...], dh2_ref[...])
        _acc(dgpost_ref, dgpost, first)
        dmb = dmixed.astype(MXU_DTYPE)
        dmixed_ref[...] = dmb.astype(dmixed_ref.dtype)
        dcat = _mm_nt(dmb, wout_ref[...])
        dcs, dcp = dcat[:, :W], dcat[:, W:]

        y = y_ref[...]
        ge = _gelu(y).astype(MXU_DTYPE)
        zv = jnp.concatenate([_mm(ge[:, c * cch : (c + 1) * cch], wgv_ref[c]) for c in range(nch)], axis=1)
        zg = jnp.concatenate([_mm(ge[:, c * cch : (c + 1) * cch], wgg_ref[c]) for c in range(nch)], axis=1)
        sg = _sigmoid(zg)
        dout, dgso = _rms_bwd(zv * sg, gso_ref[...], dcs)
        _acc(dgso_ref, dgso, first)
        dzv = (dout * sg).astype(MXU_DTYPE)
        dzg = (dout * zv * sg * (1.0 - sg)).astype(MXU_DTYPE)
        dges = []
        for c in range(nch):
            cs = slice(c * cch, (c + 1) * cch)
            dges.append(_mm_nt(dzv[:, cs], wgv_ref[c]) + _mm_nt(dzg[:, cs], wgg_ref[c]))
            _acc(dwgv_ref.at[c], _mm_tn(ge[:, cs], dzv[:, cs]), first)
            _acc(dwgg_ref.at[c], _mm_tn(ge[:, cs], dzg[:, cs]), first)
        dy = jnp.concatenate(dges, axis=1) * _gelu_grad(y)
        dy_ref[...] = dy
        _acc(dd_ref, jnp.sum(dy * us_ref[...], axis=0, keepdims=True), first)

        pooled_b = pooled_ref[...].astype(MXU_DTYPE)
        pm = jnp.concatenate([_mm(pooled_b[:, g * pg : (g + 1) * pg], pw_ref[g]) for g in range(ng)], axis=1)
        dyp, dgpo = _rms_bwd(pm * pscale_ref[...], gpo_ref[...], dcp)
        _acc(dgpo_ref, dgpo, first)
        _acc(dscale_ref, jnp.sum(dyp * pm, axis=0, keepdims=True), first)
        dpm = (dyp * pscale_ref[...]).astype(MXU_DTYPE)
        dps = []
        for g in range(ng):
            gs = slice(g * pg, (g + 1) * pg)
            dps.append(_mm_nt(dpm[:, gs], pw_ref[g]))
            _acc(dpw_ref.at[g], _mm_tn(pooled_b[:, gs], dpm[:, gs]), first)
        dpooled_ref[...] = jnp.concatenate(dps, axis=1)

    vec = lambda n: jax.ShapeDtypeStruct((1, n), F32)
    operands = (dh2, mixed, y, pooled, proj, g_so, g_po, g_post, pscale, wout, wgv, wgg, pw)
    return _pallas(
        body,
        name=name,
        grid=(T // tm,),
        out_shape=[jax.ShapeDtypeStruct((T, W), F32), jax.ShapeDtypeStruct((T, W), F32), jax.ShapeDtypeStruct((T, D), MXU_DTYPE), vec(D), vec(W), vec(W), vec(W), vec(W),
                   jax.ShapeDtypeStruct(wgv.shape, F32), jax.ShapeDtypeStruct(wgg.shape, F32), jax.ShapeDtypeStruct(pw.shape, F32)],
        in_specs=[_row_spec(tm, D), _row_spec(tm, D), _row_spec(tm, W), _row_spec(tm, W), _row_spec(tm, W)] + [_full_spec(o.shape) for o in operands[5:]],
        out_specs=[_row_spec(tm, W), _row_spec(tm, W), _row_spec(tm, D), _full_spec((1, D)), _full_spec((1, W)), _full_spec((1, W)), _full_spec((1, W)), _full_spec((1, W)),
                   _full_spec(wgv.shape), _full_spec(wgg.shape), _full_spec(pw.shape)],
        operands=operands,
        exchange=exchange,
    )


def _mix_bwd_scan(dy, dpooled, xr, xi, proj, h1, dh2, g_pre, dskip, win, bbre, bbim, ccre, ccim, tabs, pows, name, exchange=None):
    T, D = h1.shape
    W = D // 2
    tm = MIX_TILE
    seg = tm // SUBLANES
    nt = T // tm
    n_slabs = tabs.shape[1]
    nch, cch, sch = bbre.shape
    spc = sch // LANES
    pg = W // len(POOL_WINDOWS)
    blocks_per_tile = tm // SUBLANES

    def body(dy_ref, dp_ref, xr_ref, xi_ref, xpr_ref, xpi_ref, proj_ref, h_ref, dh2_ref, gpre_ref, dskip_ref, win_ref, bbre_ref, bbim_ref, ccre_ref, ccim_ref, tabs_ref, pows_ref,
             dh1_ref, dproj_ref, dgpre_ref, dccre_ref, dccim_ref, dbbre_ref, dbbim_ref, dar_ref, dai_ref, lr, li, car_r, car_i, halo):
        i = pl.program_id(0)
        first = i == 0
        tile = nt - 1 - i
        row = lax.broadcasted_iota(jnp.int32, (SUBLANES, LANES), 0)

        @pl.when(first)
        def _():
            car_r[...] = jnp.zeros_like(car_r)
            car_i[...] = jnp.zeros_like(car_i)
            halo[...] = jnp.zeros_like(halo)
            dar_ref[...] = jnp.zeros_like(dar_ref)
            dai_ref[...] = jnp.zeros_like(dai_ref)

        dy = dy_ref[...]
        for c in range(nch):
            dyc = dy[:, c * cch : (c + 1) * cch].astype(MXU_DTYPE)
            gr, gi = _mm_nt(dyc, ccre_ref[c]), _mm_nt(dyc, ccim_ref[c])
            for q in range(spc):
                lr[c * spc + q] = gr[:, q * LANES : (q + 1) * LANES]
                li[c * spc + q] = -gi[:, q * LANES : (q + 1) * LANES]
            _acc(dccre_ref.at[c], _mm_tn(_slabs_to_cols(xr_ref, c * spc, spc), dyc), first)
            _acc(dccim_ref.at[c], -_mm_tn(_slabs_to_cols(xi_ref, c * spc, spc), dyc), first)
        _scan_inplace(lr, li, tabs_ref, pows_ref, car_r, car_i, seg, reverse=True)

        for k0 in range(0, n_slabs, SLAB_GROUP):
            slabs = range(k0, min(k0 + SLAB_GROUP, n_slabs))
            init = []
            for k in slabs:
                prev_r = jnp.where(tile > 0, jnp.broadcast_to(xpr_ref[k, SUBLANES - 1 : SUBLANES, :], (SUBLANES, LANES)), 0.0)
                prev_i = jnp.where(tile > 0, jnp.broadcast_to(xpi_ref[k, SUBLANES - 1 : SUBLANES, :], (SUBLANES, LANES)), 0.0)
                x0r = jnp.where(row >= 1, pltpu.roll(xr_ref[k, pl.ds(seg - 1, SUBLANES, stride=seg), :], 1, 0), prev_r)
                x0i = jnp.where(row >= 1, pltpu.roll(xi_ref[k, pl.ds(seg - 1, SUBLANES, stride=seg), :], 1, 0), prev_i)
                l0r, l0i = lr[k, pl.ds(0, SUBLANES, stride=seg), :], li[k, pl.ds(0, SUBLANES, stride=seg), :]
                init += [l0r * x0r + l0i * x0i, l0i * x0r - l0r * x0i]

            def step(r, acc, slabs=slabs):
                out = []
                for q, k in enumerate(slabs):
                    pr_, pi_ = xr_ref[k, pl.ds(r - 1, SUBLANES, stride=seg), :], xi_ref[k, pl.ds(r - 1, SUBLANES, stride=seg), :]
                    lr_, li_ = lr[k, pl.ds(r, SUBLANES, stride=seg), :], li[k, pl.ds(r, SUBLANES, stride=seg), :]
                    out += [acc[2 * q] + lr_ * pr_ + li_ * pi_, acc[2 * q + 1] + li_ * pr_ - lr_ * pi_]
                return tuple(out)

            sums = lax.fori_loop(1, seg, step, tuple(init))
            for q, k in enumerate(slabs):
                dar_ref[k] += sums[2 * q]
                dai_ref[k] += sums[2 * q + 1]

        us = proj_ref[:, :W]
        dus = []
        for c in range(nch):
            lrc, lic = _slabs_to_cols(lr, c * spc, spc).astype(MXU_DTYPE), _slabs_to_cols(li, c * spc, spc).astype(MXU_DTYPE)
            uc = us[:, c * cch : (c + 1) * cch].astype(MXU_DTYPE)
            _acc(dbbre_ref.at[c], _mm_tn(uc, lrc), first)
            _acc(dbbim_ref.at[c], _mm_tn(uc, lic), first)
            dus.append(_mm_nt(lrc, bbre_ref[c]) + _mm_nt(lic, bbim_ref[c]))
        du_s = jnp.concatenate(dus, axis=1) + dskip_ref[...] * dy

        dp = dp_ref[...]
        t1 = (tile * tm + 1 + lax.broadcasted_iota(jnp.int32, (tm, pg), 0)).astype(F32)
        dups, heads = [], []
        for g, w in enumerate(POOL_WINDOWS):
            dpg = dp[:, g * pg : (g + 1) * pg]
            qg = dpg / jnp.minimum(t1, float(w))
            ext = jnp.concatenate([qg, halo[:, g * pg : (g + 1) * pg]], axis=0)
            dups.append(_window_sum(ext, g + 1, False)[:tm, :] - dpg)
            heads.append(qg[:POOL_HALO, :])
        halo[...] = jnp.concatenate(heads, axis=1)
        dproj = jnp.concatenate([du_s] + dups, axis=1).astype(MXU_DTYPE)
        dproj_ref[...] = dproj.astype(dproj_ref.dtype)
        dx, dg = _rms_bwd(h_ref[...], gpre_ref[...], _mm_nt(dproj, win_ref[...]))
        _acc(dgpre_ref, dg, first)
        dh1_ref[...] = dh2_ref[...] + dx

    rev = lambda cols: _row_spec(tm, cols, rev_n=nt)
    slab_spec = pl.BlockSpec((n_slabs, tm, LANES), lambda i: (0, nt - 1 - i, 0))
    prev_spec = pl.BlockSpec((n_slabs, SUBLANES, LANES), lambda i: (0, jnp.maximum((nt - 1 - i) * blocks_per_tile - 1, 0), 0))
    consts = (g_pre, dskip, win, bbre, bbim, ccre, ccim, tabs, pows)
    return _pallas(
        body,
        name=name,
        grid=(nt,),
        out_shape=[jax.ShapeDtypeStruct((T, D), F32), jax.ShapeDtypeStruct((T, D), MXU_DTYPE), jax.ShapeDtypeStruct((1, D), F32),
                   jax.ShapeDtypeStruct(ccre.shape, F32), jax.ShapeDtypeStruct(ccim.shape, F32), jax.ShapeDtypeStruct(bbre.shape, F32), jax.ShapeDtypeStruct(bbim.shape, F32),
                   jax.ShapeDtypeStruct((n_slabs, SUBLANES, LANES), F32), jax.ShapeDtypeStruct((n_slabs, SUBLANES, LANES), F32)],
        in_specs=[rev(W), rev(W), slab_spec, slab_spec, prev_spec, prev_spec, rev(D), rev(D), rev(D)] + [_full_spec(o.shape) for o in consts],
        out_specs=[rev(D), rev(D), _full_spec((1, D)), _full_spec(ccre.shape), _full_spec(ccim.shape), _full_spec(bbre.shape), _full_spec(bbim.shape),
                   _full_spec((n_slabs, SUBLANES, LANES)), _full_spec((n_slabs, SUBLANES, LANES))],
        scratch_shapes=[pltpu.VMEM((n_slabs, tm, LANES), F32), pltpu.VMEM((n_slabs, tm, LANES), F32), pltpu.VMEM((n_slabs, SUBLANES, LANES), F32), pltpu.VMEM((n_slabs, SUBLANES, LANES), F32),
                        pltpu.VMEM((POOL_HALO, W), F32)],
        operands=(dy, dpooled, xr, xi, xr, xi, proj, h1, dh2, *consts),
        exchange=exchange,
    )


def _discretize(lam_re, lam_im, log_dt, b_re, b_im):
    dt = jnp.exp(log_dt)[:, None]
    decay = jnp.exp(lam_re * dt)
    ang = lam_im * dt
    a_re, a_im = decay * jnp.cos(ang), decay * jnp.sin(ang)
    nr = a_re - 1.0
    den = lam_re * lam_re + lam_im * lam_im
    q_re = (nr * lam_re + a_im * lam_im) / den
    q_im = (a_im * lam_re - nr * lam_im) / den
    bb_re = q_re[..., None] * b_re - q_im[..., None] * b_im
    bb_im = q_re[..., None] * b_im + q_im[..., None] * b_re
    return a_re, a_im, bb_re, bb_im


GROUPS_PER_CHUNK = 16


def _block_diag(w, rows_first):
    G = w.shape[0]
    nch = G // GROUPS_PER_CHUNK
    if not rows_first:
        w = jnp.swapaxes(w, 1, 2)
    p, q = w.shape[1], w.shape[2]
    eye = jnp.eye(GROUPS_PER_CHUNK, dtype=w.dtype)
    out = jnp.einsum("cgpq,gk->cgpkq", w.reshape(nch, GROUPS_PER_CHUNK, p, q), eye)
    return out.reshape(nch, GROUPS_PER_CHUNK * p, GROUPS_PER_CHUNK * q)


def _block_diag_extract(m, p, q, rows_first):
    nch = m.shape[0]
    eye = jnp.eye(GROUPS_PER_CHUNK, dtype=m.dtype)
    out = jnp.einsum("cgpkq,gk->cgpq", m.reshape(nch, GROUPS_PER_CHUNK, p, GROUPS_PER_CHUNK, q), eye).reshape(nch * GROUPS_PER_CHUNK, p, q)
    return out if rows_first else jnp.swapaxes(out, 1, 2)


def _cmul(ar, ai, br, bi):
    return ar * br - ai * bi, ar * bi + ai * br


def _powers(ar, ai, count):
    pr, pi = ar[None], ai[None]
    while pr.shape[0] < count:
        nr, ni = _cmul(pr, pi, pr[-1][None], pi[-1][None])
        pr, pi = jnp.concatenate([pr, nr]), jnp.concatenate([pi, ni])
    return pr[:count], pi[:count]


def _scan_tables(a_re, a_im, seg):
    n = a_re.size
    ns = n // LANES
    ar, ai = a_re.reshape(n), a_im.reshape(n)
    pr, pi = _powers(ar, ai, seg)
    jr, ji = _powers(pr[-1], pi[-1], SUBLANES)

    def bcast(v):
        return jnp.broadcast_to(v.reshape(ns, 1, LANES), (ns, SUBLANES, LANES))

    def per_sublane(vs):
        return jnp.transpose(vs.reshape(SUBLANES, ns, LANES), (1, 0, 2))

    tabs = jnp.stack([bcast(ar), bcast(ai), bcast(jr[0]), bcast(ji[0]), bcast(jr[1]), bcast(ji[1]), bcast(jr[3]), bcast(ji[3]),
                      per_sublane(jr), per_sublane(ji), per_sublane(jr[::-1]), per_sublane(ji[::-1])])

    def rows(vs):
        return jnp.broadcast_to(jnp.transpose(vs.reshape(seg, ns, 1, LANES), (1, 0, 2, 3)), (ns, seg, SUBLANES, LANES))

    return tabs, jnp.stack([rows(pr), rows(pi)])


SMALL = ("ffn1_pre_norm", "ffn1_post_norm", "mix_pre_norm", "mix_post_norm", "ssm_lambda_re", "ssm_lambda_im", "ssm_log_dt", "ssm_b_re", "ssm_b_im", "ssm_c_re", "ssm_c_im",
         "ssm_d", "ssm_w_glu", "pool_w", "pool_scale", "ssm_out_norm", "pool_out_norm", "ffn2_pre_norm", "ffn2_post_norm")
BIG = ("ffn1_w_gate", "ffn1_w_up", "ffn1_w_down", "w_in", "w_out", "ffn2_w_gate", "ffn2_w_up", "ffn2_w_down")
ORDER = ("meta_tokens", "ffn1_pre_norm", "ffn1_post_norm", "ffn1_w_gate", "ffn1_w_up", "ffn1_w_down", "mix_pre_norm", "mix_post_norm", "w_in", "ssm_lambda_re", "ssm_lambda_im",
         "ssm_log_dt", "ssm_b_re", "ssm_b_im", "ssm_c_re", "ssm_c_im", "ssm_d", "ssm_w_glu", "pool_w", "pool_scale", "ssm_out_norm", "pool_out_norm", "w_out", "ffn2_pre_norm",
         "ffn2_post_norm", "ffn2_w_gate", "ffn2_w_up", "ffn2_w_down")
PACK_ROWS = SUBLANES * 8
def _pack(arrays, rows):
    flat = jnp.concatenate([a.reshape(-1) for a in arrays])
    return jnp.pad(flat, (0, rows * LANES - flat.size)).reshape(rows, LANES)


def _unpack(packed, shapes):
    flat = packed.reshape(-1)
    out, off = [], 0
    for s in shapes:
        n = math.prod(s)
        out.append(flat[off : off + n].reshape(s))
        off += n
    return out


def _step(p, x, loss_target, m, v):
    D = x.shape[-1]
    chip = (2 * lax.axis_index("x") + lax.axis_index("y")).astype(jnp.int32)
    place = jnp.stack([chip, lax.axis_index("c").astype(jnp.int32)])

    def gather_buffer(w):
        own = w.reshape(1, 2, w.shape[0] // 2, w.shape[1])
        return lax.dynamic_update_slice(lax.empty((4,) + own.shape[1:], own.dtype), own, (chip, 0, 0, 0))

    def rows_of(n, a):
        return jnp.swapaxes(a[0], 0, 1) if n.endswith(("w_gate", "w_up")) else a[0]

    def rows_back(n, a):
        return (jnp.swapaxes(a, 0, 1) if n.endswith(("w_gate", "w_up")) else a)[None]

    def grad_view(g):
        return g.reshape(4, 2, g.shape[0] // 8, g.shape[1])

    def reduce_sum(got_sibling, views, tag):
        sums = [_add_own_half(v_, g_, place, f"{tag}_add_sibling_{k}") for k, (v_, g_) in enumerate(zip(views, got_sibling))]
        return [s[0] for s in sums], [s[1] for s in sums]

    def reduce_halves(parts, got_chips, tag):
        return [_add_chips(p_, g_, place, f"{tag}_add_chips_{k}") for k, (p_, g_) in enumerate(zip(parts, got_chips))]

    first_names = ("ffn1_w_gate", "ffn1_w_up", "ffn1_w_down")
    later_names = ("w_in", "w_out", "ffn2_w_gate", "ffn2_w_up", "ffn2_w_down")
    bufs = {n: gather_buffer(rows_of(n, p[n]).astype(MXU_DTYPE)) for n in BIG}
    gathered_weight = lambda g_: g_.reshape(-1, g_.shape[-1])
    got = _exchange_call(_Gather([bufs[n] for n in first_names] + [gather_buffer(p["meta_tokens"])]), "gather_first")
    full = {n: gathered_weight(g_) for n, g_ in zip(first_names, got)}
    meta = jnp.transpose(got[-1].reshape(4, N_META, -1), (1, 0, 2)).reshape(N_META, D)

    vec = lambda n: p[n].reshape(1, -1)
    G, N, H = p["ssm_b_re"].shape[1:]
    a_re, a_im, bb_re, bb_im = _discretize(p["ssm_lambda_re"][0], p["ssm_lambda_im"][0], p["ssm_log_dt"][0], p["ssm_b_re"][0], p["ssm_b_im"][0])
    tabs, pows = _scan_tables(a_re, a_im, MIX_TILE // SUBLANES)
    bf = lambda a: a.astype(MXU_DTYPE)
    bbre, bbim = bf(_block_diag(bb_re, False)), bf(_block_diag(bb_im, False))
    ccre, ccim = bf(_block_diag(p["ssm_c_re"][0], False)), bf(_block_diag(p["ssm_c_im"][0], False))
    wgv, wgg = bf(_block_diag(p["ssm_w_glu"][0][:, :, :H], True)), bf(_block_diag(p["ssm_w_glu"][0][:, :, H:], True))
    pw = bf(p["pool_w"][0])

    n_ffn_steps = (x.shape[1] + N_META) // FFN_TILE
    (h1, f1, ga1, si1, s1, n1, h0), got = _ffn_fwd(
        x[0], vec("ffn1_pre_norm"), vec("ffn1_post_norm"), full["ffn1_w_gate"], full["ffn1_w_up"], full["ffn1_w_down"], "ffn1_fwd",
        exchange=_Gather([bufs[n] for n in later_names], mid_step=(3 * n_ffn_steps) // 4), meta=meta,
    )
    full.update({n: gathered_weight(g_) for n, g_ in zip(later_names, got)})
    proj, xr, xim, y, pooled, mixed, h2, n2, cat = _mix_fwd(
        h1, vec("mix_pre_norm"), vec("ssm_out_norm"), vec("pool_out_norm"), vec("mix_post_norm"), vec("ssm_d"), vec("pool_scale"), full["w_in"], full["w_out"],
        bbre, bbim, ccre, ccim, wgv, wgg, pw, tabs, pows, "mix_fwd",
    )
    (_, f2, ga2, si2, s2, n3, dh3, sq), _ = _ffn_fwd(
        h2, vec("ffn2_pre_norm"), vec("ffn2_post_norm"), full["ffn2_w_gate"], full["ffn2_w_up"], full["ffn2_w_down"], "ffn2_fwd", target=loss_target[0]
    )
    loss = lax.psum(0.5 * jnp.sum(sq) / D, ("x", "y", "c"))

    g, shared = {}, {}
    ffn_names = lambda tag: (tag + "_w_gate", tag + "_w_up", tag + "_w_down")

    (da, db, df, g["ffn2_post_norm"]), _ = _ffn_bwd_down(dh3, f2, ga2, si2, vec("ffn2_post_norm"), full["ffn2_w_down"], "ffn2_bwd_down")
    (dh2, g["ffn2_pre_norm"]), _ = _ffn_bwd_up(da, db, h2, dh3, vec("ffn2_pre_norm"), full["ffn2_w_gate"], full["ffn2_w_up"], "ffn2_bwd_up")
    views2 = [
        grad_view(_tn_matmul(da, n3, "ffn2_dw_gate")[0]),
        grad_view(_tn_matmul(db, n3, "ffn2_dw_up")[0]),
        grad_view(_tn_matmul(s2, df, "ffn2_dw_down")[0]),
    ]
    (dy, dpooled, dmixed, g["mix_post_norm"], g["ssm_out_norm"], g["pool_out_norm"], g["ssm_d"], g["pool_scale"], dwgv, dwgg, g["pool_w"]), got = _mix_bwd_heads(
        dh2, mixed, y, pooled, proj, vec("ssm_out_norm"), vec("pool_out_norm"), vec("mix_post_norm"), vec("pool_scale"), full["w_out"], wgv, wgg, pw, "mix_bwd_heads",
        exchange=_SiblingScatter(views2),
    )
    parts2, wire2 = reduce_sum(got, views2, "ffn2")
    (dh1, dproj, g["mix_pre_norm"], dccre, dccim, dbbre, dbbim, dar, dai), got = _mix_bwd_scan(
        dy, dpooled, xr, xim, proj, h1, dh2, vec("mix_pre_norm"), vec("ssm_d"), full["w_in"], bbre, bbim, ccre, ccim, tabs, pows, "mix_bwd_scan",
        exchange=_ChipScatter(wire2),
    )
    halves2 = reduce_halves(parts2, got, "ffn2")
    dw_in, got = _tn_matmul(n2, dproj, "dw_in", exchange=_SiblingShare(halves2))
    shared.update(zip(ffn_names("ffn2"), got))
    dw_out, _ = _tn_matmul(cat, dmixed, "dw_out")
    views_m = [grad_view(dw_in), grad_view(dw_out)]

    (da, db, df, g["ffn1_post_norm"]), got = _ffn_bwd_down(dh1, f1, ga1, si1, vec("ffn1_post_norm"), full["ffn1_w_down"], "ffn1_bwd_down", exchange=_SiblingScatter(views_m))
    parts_m, wire_m = reduce_sum(got, views_m, "mix")
    dw_down, got = _tn_matmul(s1, df, "ffn1_dw_down", exchange=_ChipScatter(wire_m))
    halves_m = reduce_halves(parts_m, got, "mix")
    views_d = [grad_view(dw_down)]
    ex = _Group([_SiblingShare(halves_m), _SiblingScatter(views_d)])
    dw_gate, got = _tn_matmul(da, n1, "ffn1_dw_gate", exchange=ex)
    got_m, got_d = ex.split(got)
    shared.update(zip(("w_in", "w_out"), got_m))
    parts_d, wire_d = reduce_sum(got_d, views_d, "ffn1_down")
    views_g = [grad_view(dw_gate)]
    ex = _Group([_ChipScatter(wire_d), _SiblingScatter(views_g)])
    dw_up, got = _tn_matmul(db, n1, "ffn1_dw_up", exchange=ex)
    got_d, got_g = ex.split(got)
    halves_d = reduce_halves(parts_d, got_d, "ffn1_down")
    parts_g, wire_g = reduce_sum(got_g, views_g, "ffn1_gate")
    views_u = [grad_view(dw_up)]
    ex = _Group([_SiblingShare(halves_d), _ChipScatter(wire_g), _SiblingScatter(views_u)])
    (dh0, g["ffn1_pre_norm"]), got = _ffn_bwd_up(da, db, h0, dh1, vec("ffn1_pre_norm"), full["ffn1_w_gate"], full["ffn1_w_up"], "ffn1_bwd_up", exchange=ex)
    got_d, got_g, got_u = ex.split(got)
    shared["ffn1_w_down"] = got_d[0]
    halves_g = reduce_halves(parts_g, got_g, "ffn1_gate")
    parts_u, wire_u = reduce_sum(got_u, views_u, "ffn1_up")
    grad_x = dh0[N_META:][None]

    g["ssm_c_re"] = _block_diag_extract(dccre, N, H, False)
    g["ssm_c_im"] = _block_diag_extract(dccim, N, H, False)
    g["ssm_w_glu"] = jnp.concatenate([_block_diag_extract(dwgv, H, H, True), _block_diag_extract(dwgg, H, H, True)], axis=-1)
    d_a_re, d_a_im = jnp.sum(dar, axis=1).reshape(G, N), jnp.sum(dai, axis=1).reshape(G, N)
    _, pull = jax.vjp(_discretize, p["ssm_lambda_re"][0], p["ssm_lambda_im"][0], p["ssm_log_dt"][0], p["ssm_b_re"][0], p["ssm_b_im"][0])
    g["ssm_lambda_re"], g["ssm_lambda_im"], g["ssm_log_dt"], g["ssm_b_re"], g["ssm_b_im"] = pull(
        (d_a_re, d_a_im, _block_diag_extract(dbbre, H, N, False), _block_diag_extract(dbbim, H, N, False))
    )

    small_shapes = [p[n].shape for n in SMALL] + [(N_META, D)]
    small_size = sum(math.prod(s) for s in small_shapes)
    rows = -(-small_size // (LANES * PACK_ROWS)) * PACK_ROWS
    views_s = [_pack([g[n] for n in SMALL] + [dh0[:N_META]], rows).reshape(4, 2, rows // 8, LANES)]
    parts_s, wire_s = reduce_sum(_exchange_call(_SiblingScatter(views_s), "small_reduce_sibling"), views_s, "small")
    ex = _Group([_ChipScatter(wire_u + wire_s), _SiblingShare(halves_g)])
    got_c, got_g = ex.split(_exchange_call(ex, "tail_reduce_chips"))
    shared["ffn1_w_gate"] = got_g[0]
    got = _exchange_call(_SiblingShare(reduce_halves(parts_u + parts_s, got_c, "tail")), "tail_reduce_share")
    shared["ffn1_w_up"] = got[0]
    small_buf = lax.dynamic_update_slice(lax.empty((4,) + got[1].shape, F32), got[1][None], (chip, 0, 0, 0))
    small_all = _exchange_call(_Gather([small_buf]), "gather_small")[0].reshape(rows, LANES)
    grads = dict(zip(SMALL + ("meta_full",), _unpack(small_all, small_shapes)))
    grads["meta_tokens"] = lax.dynamic_slice_in_dim(grads.pop("meta_full"), chip * (D // 4), D // 4, axis=1)
    delta, new_m, new_v = {}, {}, {}
    for n in BIG:
        g_rows = shared[n].reshape(-1, shared[n].shape[-1])
        outs = _adamw(rows_of(n, p[n]), g_rows, rows_of(n, m[n]), rows_of(n, v[n]), "adamw_" + n)
        grads[n], delta[n], new_m[n], new_v[n] = (rows_back(n, a) for a in (g_rows, *outs))
    delta["meta_tokens"], new_m["meta_tokens"], new_v["meta_tokens"] = _adamw(p["meta_tokens"], grads["meta_tokens"], m["meta_tokens"], v["meta_tokens"], "adamw_meta_tokens")

    def as_2d(n, a):
        a = a.reshape(p[n].shape)[0]
        if n in ("ssm_b_re", "ssm_b_im"):
            a = jnp.swapaxes(a, 1, 2)
        return a.reshape(-1, a.shape[-1])

    def from_2d(n, a):
        if n in ("ssm_b_re", "ssm_b_im"):
            g_, n_, h_ = p[n].shape[1:]
            return jnp.swapaxes(a.reshape(g_, h_, n_), 1, 2)[None]
        return a.reshape(p[n].shape)

    outs = _adamw_many(*[[as_2d(n, t[n]) for n in SMALL] for t in (p, grads, m, v)], "adamw_small")
    for out, arrays in zip((delta, new_m, new_v), outs):
        out.update({n: from_2d(n, a) for n, a in zip(SMALL, arrays)})

    return (loss, grad_x, *[grads[n] for n in ORDER], *[delta[n] for n in ORDER], *[new_m[n] for n in ORDER], *[new_v[n] for n in ORDER])


def kernel(x, meta_tokens, ffn1_pre_norm, ffn1_post_norm, ffn1_w_gate, ffn1_w_up, ffn1_w_down, mix_pre_norm, mix_post_norm, w_in, ssm_lambda_re, ssm_lambda_im, ssm_log_dt, ssm_b_re, ssm_b_im, ssm_c_re, ssm_c_im, ssm_d, ssm_w_glu, pool_w, pool_scale, ssm_out_norm, pool_out_norm, w_out, ffn2_pre_norm, ffn2_post_norm, ffn2_w_gate, ffn2_w_up, ffn2_w_down, loss_target, m_meta_tokens, m_ffn1_pre_norm, m_ffn1_post_norm, m_ffn1_w_gate, m_ffn1_w_up, m_ffn1_w_down, m_mix_pre_norm, m_mix_post_norm, m_w_in, m_ssm_lambda_re, m_ssm_lambda_im, m_ssm_log_dt, m_ssm_b_re, m_ssm_b_im, m_ssm_c_re, m_ssm_c_im, m_ssm_d, m_ssm_w_glu, m_pool_w, m_pool_scale, m_ssm_out_norm, m_pool_out_norm, m_w_out, m_ffn2_pre_norm, m_ffn2_post_norm, m_ffn2_w_gate, m_ffn2_w_up, m_ffn2_w_down, v_meta_tokens, v_ffn1_pre_norm, v_ffn1_post_norm, v_ffn1_w_gate, v_ffn1_w_up, v_ffn1_w_down, v_mix_pre_norm, v_mix_post_norm, v_w_in, v_ssm_lambda_re, v_ssm_lambda_im, v_ssm_log_dt, v_ssm_b_re, v_ssm_b_im, v_ssm_c_re, v_ssm_c_im, v_ssm_d, v_ssm_w_glu, v_pool_w, v_pool_scale, v_ssm_out_norm, v_pool_out_norm, v_w_out, v_ffn2_pre_norm, v_ffn2_post_norm, v_ffn2_w_gate, v_ffn2_w_up, v_ffn2_w_down):
    args = locals()
    p = {n: args[n] for n in ORDER}
    m = {n: args["m_" + n] for n in ORDER}
    v = {n: args["v_" + n] for n in ORDER}
    return _step(p, x, loss_target, m, v)
```

```python
import math

import jax
import jax.numpy as jnp
from jax import lax
from jax.experimental import pallas as pl
from jax.experimental.pallas import tpu as pltpu

F32 = jnp.float32
MXU_DTYPE = jnp.bfloat16
WIRE_DTYPE = jnp.bfloat16

RMS_EPS = 1e-6
N_META = 16
POOL_WINDOWS = (2, 4, 8, 16)
POOL_HALO = 16
ADAM_LR, ADAM_B1, ADAM_B2, ADAM_EPS, ADAM_WD, ADAM_STEP = 0.001, 0.9, 0.999, 1e-08, 0.01, 10

LANES = 128
SUBLANES = 8
VMEM_LIMIT = 60 * 1024 * 1024
FFN_TILE = 432
FFN_CHUNK = 1024
MIX_TILE = 216
SLAB_GROUP = 8
MESH = pl.DeviceIdType.MESH
ANY = pl.BlockSpec(memory_space=pl.ANY)


def _mm(a, b):
    return jnp.dot(a.astype(MXU_DTYPE), b.astype(MXU_DTYPE), preferred_element_type=F32)


def _mm_nt(a, b):
    return lax.dot_general(a.astype(MXU_DTYPE), b.astype(MXU_DTYPE), (((1,), (1,)), ((), ())), preferred_element_type=F32)


def _mm_tn(a, b):
    return lax.dot_general(a.astype(MXU_DTYPE), b.astype(MXU_DTYPE), (((0,), (0,)), ((), ())), preferred_element_type=F32)


def _rms_stat(x):
    return lax.rsqrt(jnp.mean(x * x, axis=-1, keepdims=True) + RMS_EPS)


def _rms_bwd(x, g, dy):
    r = _rms_stat(x)
    xh = x * r
    dg = jnp.sum(dy * xh, axis=0, keepdims=True)
    dxh = dy * g
    dx = r * (dxh - xh * jnp.mean(dxh * xh, axis=-1, keepdims=True))
    return dx, dg


def _sigmoid(x):
    return 1.0 / (1.0 + jnp.exp(-x))


GELU_C = math.sqrt(2.0 / math.pi)
GELU_K = 0.044715


def _gelu(y):
    return 0.5 * y * (1.0 + jnp.tanh(GELU_C * (y + GELU_K * y * y * y)))


def _gelu_grad(y):
    th = jnp.tanh(GELU_C * (y + GELU_K * y * y * y))
    return 0.5 * (1.0 + th) + 0.5 * y * (1.0 - th * th) * GELU_C * (1.0 + 3.0 * GELU_K * y * y)


def _row_spec(tile, cols, rev_n=None):
    if rev_n is None:
        return pl.BlockSpec((tile, cols), lambda i: (i, 0))
    return pl.BlockSpec((tile, cols), lambda i: (rev_n - 1 - i, 0))


def _full_spec(shape):
    zeros = (0,) * len(shape)
    return pl.BlockSpec(shape, lambda *_: zeros)


def _acc(ref, val, first):
    @pl.when(first)
    def _():
        ref[...] = val

    @pl.when(jnp.logical_not(first))
    def _():
        ref[...] += val


def _place():
    x, y, c = lax.axis_index("x"), lax.axis_index("y"), lax.axis_index("c")
    others = [(1 - x, y), (x, 1 - y), (1 - x, 1 - y)]
    return x, y, c, others


class _Exchange:
    mid_step = None

    def __init__(self, ins, out_shapes, aliases, n_sems):
        self.ins, self.out_shapes, self.aliases, self.n_sems = list(ins), list(out_shapes), dict(aliases), n_sems

    def mid(self, ins, outs, send_sems, recv_sems):
        pass


class _SiblingScatter(_Exchange):
    def __init__(self, views):
        super().__init__(views, [jax.ShapeDtypeStruct((4,) + v.shape[2:], v.dtype) for v in views], {}, 4 * len(views))

    def _copies(self, ins, outs, send_sems, recv_sems):
        x, y, c, _ = _place()
        return [
            pltpu.make_async_remote_copy(src_ref=ins[a].at[k, 1 - c], dst_ref=outs[a].at[k], send_sem=send_sems.at[4 * a + k], recv_sem=recv_sems.at[4 * a + k], device_id=(x, y, 1 - c), device_id_type=MESH)
            for a in range(len(ins))
            for k in range(4)
        ]

    def start(self, *refs):
        for cp in self._copies(*refs):
            cp.start()

    def finish(self, *refs):
        cps = self._copies(*refs)
        for cp in cps:
            cp.wait_recv()
        for cp in cps:
            cp.wait_send()


class _ChipScatter(_Exchange):
    def __init__(self, parts):
        super().__init__(parts, [jax.ShapeDtypeStruct((3,) + p.shape[1:], p.dtype) for p in parts], {}, 3 * len(parts))

    def _copies(self, ins, outs, send_sems, recv_sems):
        x, y, c, others = _place()
        return [
            pltpu.make_async_remote_copy(src_ref=ins[a].at[2 * chip[0] + chip[1]], dst_ref=outs[a].at[j], send_sem=send_sems.at[3 * a + j], recv_sem=recv_sems.at[3 * a + j], device_id=(*chip, c), device_id_type=MESH)
            for a in range(len(ins))
            for j, chip in enumerate(others)
        ]

    start = _SiblingScatter.start
    finish = _SiblingScatter.finish


class _SiblingShare(_Exchange):
    def __init__(self, bufs):
        super().__init__(bufs, [jax.ShapeDtypeStruct(b.shape, b.dtype) for b in bufs], {a: a for a in range(len(bufs))}, len(bufs))

    def _copy(self, outs, send_sems, recv_sems, a, half):
        x, y, c, _ = _place()
        mine = outs[a].at[c if half == "mine" else 1 - c]
        return pltpu.make_async_remote_copy(src_ref=mine, dst_ref=mine, send_sem=send_sems.at[a], recv_sem=recv_sems.at[a], device_id=(x, y, 1 - c), device_id_type=MESH)

    def start(self, ins, outs, send_sems, recv_sems):
        for a in range(len(outs)):
            self._copy(outs, send_sems, recv_sems, a, "mine").start()

    def finish(self, ins, outs, send_sems, recv_sems):
        for a in range(len(outs)):
            self._copy(outs, send_sems, recv_sems, a, "theirs").wait_recv()
        for a in range(len(outs)):
            self._copy(outs, send_sems, recv_sems, a, "mine").wait_send()


class _Gather(_Exchange):
    def __init__(self, bufs, mid_step=None):
        super().__init__(bufs, [jax.ShapeDtypeStruct(b.shape, b.dtype) for b in bufs], {a: a for a in range(len(bufs))}, 6 * len(bufs))
        self.mid_step = mid_step

    def _copy(self, outs, send_sems, recv_sems, a, j, chip, half, to):
        blk = outs[a].at[2 * chip[0] + chip[1], half]
        return pltpu.make_async_remote_copy(src_ref=blk, dst_ref=blk, send_sem=send_sems.at[6 * a + j], recv_sem=recv_sems.at[6 * a + j], device_id=to, device_id_type=MESH)

    def start(self, ins, outs, send_sems, recv_sems):
        x, y, c, others = _place()
        for a in range(len(outs)):
            for j, chip in enumerate(others):
                self._copy(outs, send_sems, recv_sems, a, j, (x, y), c, (*chip, c)).start()

    def mid(self, ins, outs, send_sems, recv_sems):
        x, y, c, others = _place()
        for a in range(len(outs)):
            for j, chip in enumerate(others):
                self._copy(outs, send_sems, recv_sems, a, j, chip, c, (x, y, c)).wait_recv()
                self._copy(outs, send_sems, recv_sems, a, 3 + j, chip, c, (x, y, 1 - c)).start()

    def finish(self, ins, outs, send_sems, recv_sems):
        x, y, c, others = _place()
        for a in range(len(outs)):
            for j, chip in enumerate(others):
                self._copy(outs, send_sems, recv_sems, a, 3 + j, chip, 1 - c, (x, y, c)).wait_recv()
        for a in range(len(outs)):
            for j, chip in enumerate(others):
                self._copy(outs, send_sems, recv_sems, a, j, (x, y), c, (*chip, c)).wait_send()
                self._copy(outs, send_sems, recv_sems, a, 3 + j, chip, c, (x, y, 1 - c)).wait_send()


class _SemSlice:
    def __init__(self, sems, off):
        self.sems, self.off = sems, off

    @property
    def at(self):
        return self

    def __getitem__(self, i):
        return self.sems.at[self.off + i]


class _Group(_Exchange):
    def __init__(self, exchanges):
        ins, outs, aliases, n_sems, self.spans = [], [], {}, 0, []
        for ex in exchanges:
            self.spans.append((len(ins), len(outs), n_sems))
            aliases.update({len(ins) + i: len(outs) + o for i, o in ex.aliases.items()})
            ins, outs, n_sems = ins + ex.ins, outs + ex.out_shapes, n_sems + ex.n_sems
        super().__init__(ins, outs, aliases, n_sems)
        self.exchanges = exchanges
        mids = {ex.mid_step for ex in exchanges if ex.mid_step is not None}
        self.mid_step = mids.pop() if mids else None

    def _each(self, method, ins, outs, send_sems, recv_sems):
        for ex, (i0, o0, s0) in zip(self.exchanges, self.spans):
            getattr(ex, method)(ins[i0 : i0 + len(ex.ins)], outs[o0 : o0 + len(ex.out_shapes)], _SemSlice(send_sems, s0), _SemSlice(recv_sems, s0))

    def start(self, *refs):
        self._each("start", *refs)

    def mid(self, *refs):
        self._each("mid", *refs)

    def finish(self, *refs):
        self._each("finish", *refs)

    def split(self, outs):
        return [outs[o0 : o0 + len(ex.out_shapes)] for ex, (_, o0, _) in zip(self.exchanges, self.spans)]


def _exchange_call(ex, name):
    n, m = len(ex.ins), len(ex.out_shapes)

    def body(*refs):
        parts = (refs[:n], refs[n : n + m], refs[n + m], refs[n + m + 1])
        ex.start(*parts)
        ex.mid(*parts)
        ex.finish(*parts)

    return pl.pallas_call(
        body,
        name=name,
        out_shape=ex.out_shapes,
        in_specs=[ANY] * n,
        out_specs=[ANY] * m,
        scratch_shapes=[pltpu.SemaphoreType.DMA((ex.n_sems,)), pltpu.SemaphoreType.DMA((ex.n_sems,))],
        input_output_aliases=ex.aliases,
    )(*ex.ins)


def _pallas(body, *, name, grid, in_specs, out_specs, out_shape, operands, scratch_shapes=(), exchange=None):
    params = pltpu.CompilerParams(dimension_semantics=("arbitrary",) * len(grid), vmem_limit_bytes=VMEM_LIMIT)
    if exchange is None:
        outs = pl.pallas_call(body, name=name, grid=grid, in_specs=in_specs, out_specs=out_specs, out_shape=out_shape, scratch_shapes=list(scratch_shapes), compiler_params=params)(*operands)
        return outs, []
    ex = exchange
    n_in, n_out, n_scr = len(in_specs), len(out_specs), len(scratch_shapes)
    x_in, x_out = len(ex.ins), len(ex.out_shapes)

    def hosted(*refs):
        ins, x_ins = refs[:n_in], refs[n_in : n_in + x_in]
        outs, x_outs = refs[n_in + x_in : n_in + x_in + n_out], refs[n_in + x_in + n_out : n_in + x_in + n_out + x_out]
        rest = refs[n_in + x_in + n_out + x_out :]
        parts = (x_ins, x_outs, rest[n_scr], rest[n_scr + 1])
        ids = [pl.program_id(d) for d in range(len(grid))]
        first = _all([i == 0 for i in ids])
        last = _all([i == g - 1 for i, g in zip(ids, grid)])

        @pl.when(first)
        def _():
            ex.start(*parts)

        body(*ins, *outs, *rest[:n_scr])

        if ex.mid_step is not None:

            @pl.when(ids[0] == ex.mid_step)
            def _():
                ex.mid(*parts)

        @pl.when(last)
        def _():
            ex.finish(*parts)

    outs = pl.pallas_call(
        hosted,
        name=name,
        grid=grid,
        in_specs=list(in_specs) + [ANY] * x_in,
        out_specs=list(out_specs) + [ANY] * x_out,
        out_shape=list(out_shape) + ex.out_shapes,
        scratch_shapes=list(scratch_shapes) + [pltpu.SemaphoreType.DMA((ex.n_sems,)), pltpu.SemaphoreType.DMA((ex.n_sems,))],
        input_output_aliases={n_in + i: n_out + o for i, o in ex.aliases.items()},
        compiler_params=params,
    )(*operands, *ex.ins)
    return outs[:n_out], outs[n_out:]


def _all(conds):
    out = conds[0]
    for c in conds[1:]:
        out = jnp.logical_and(out, c)
    return out


def _row_tile(rows):
    if rows <= 512:
        return rows
    for t in (512, 352, 256, 176, 128, 112, 64, 32, 16, 8):
        if rows % t == 0:
            return t
    return rows


def _add_own_half(view, got, place, name):
    _, _, r, c = view.shape
    tr = _row_tile(r)

    def body(place_ref, v_ref, g_ref, o_ref, w_ref):
        s = v_ref[...] + g_ref[...]
        o_ref[...] = s
        w_ref[...] = s.astype(w_ref.dtype)

    blk = pl.BlockSpec((None, tr, c), lambda k, i, pr: (k, i, 0))
    return pl.pallas_call(
        body,
        name=name,
        out_shape=[jax.ShapeDtypeStruct((4, r, c), F32), jax.ShapeDtypeStruct((4, r, c), WIRE_DTYPE)],
        grid_spec=pltpu.PrefetchScalarGridSpec(
            num_scalar_prefetch=1, grid=(4, r // tr), in_specs=[pl.BlockSpec((None, None, tr, c), lambda k, i, pr: (k, pr[1], i, 0)), blk], out_specs=[blk, blk]
        ),
        compiler_params=pltpu.CompilerParams(dimension_semantics=("arbitrary", "arbitrary"), vmem_limit_bytes=VMEM_LIMIT),
    )(place, view, got)


def _add_chips(part, got, place, name):
    _, r, c = part.shape
    tr = _row_tile(r)

    def body(place_ref, p_ref, g_ref, o_ref):
        o_ref[...] = ((p_ref[...] + g_ref[0].astype(F32)) + g_ref[1].astype(F32)) + g_ref[2].astype(F32)

    return pl.pallas_call(
        body,
        name=name,
        out_shape=jax.ShapeDtypeStruct((2, r, c), F32),
        grid_spec=pltpu.PrefetchScalarGridSpec(
            num_scalar_prefetch=1,
            grid=(r // tr,),
            in_specs=[pl.BlockSpec((None, tr, c), lambda i, pr: (pr[0], i, 0)), pl.BlockSpec((3, tr, c), lambda i, pr: (0, i, 0))],
            out_specs=pl.BlockSpec((None, tr, c), lambda i, pr: (pr[1], i, 0)),
        ),
        compiler_params=pltpu.CompilerParams(dimension_semantics=("arbitrary",), vmem_limit_bytes=VMEM_LIMIT),
    )(place, part, got)


def _adamw_update(w_ref, g_ref, m_ref, v_ref, d_ref, nm_ref, nv_ref):
    g = g_ref[...]
    nm = ADAM_B1 * m_ref[...] + (1.0 - ADAM_B1) * g
    nv = ADAM_B2 * v_ref[...] + (1.0 - ADAM_B2) * (g * g)
    m_hat = nm / (1.0 - ADAM_B1**ADAM_STEP)
    v_hat = nv / (1.0 - ADAM_B2**ADAM_STEP)
    d_ref[...] = -ADAM_LR * (m_hat / (jnp.sqrt(v_hat) + ADAM_EPS) + ADAM_WD * w_ref[...])
    nm_ref[...] = nm
    nv_ref[...] = nv


def _adamw(w, g, m, v, name):
    r, c = w.shape
    tr = _row_tile(r)
    spec = pl.BlockSpec((tr, c), lambda i: (i, 0))
    outs, _ = _pallas(_adamw_update, name=name, grid=(r // tr,), in_specs=[spec] * 4, out_specs=[spec] * 3, out_shape=[jax.ShapeDtypeStruct((r, c), F32)] * 3, operands=(w, g, m, v))
    return outs


def _adamw_many(ws, gs, ms, vs, name):
    n = len(ws)

    def body(*refs):
        for k in range(n):
            _adamw_update(*(refs[j * n + k] for j in range(7)))

    outs = pl.pallas_call(
        body,
        name=name,
        out_shape=[jax.ShapeDtypeStruct(w.shape, F32) for w in ws] * 3,
        in_specs=[pl.BlockSpec(memory_space=pltpu.VMEM)] * (4 * n),
        out_specs=[pl.BlockSpec(memory_space=pltpu.VMEM)] * (3 * n),
    )(*ws, *gs, *ms, *vs)
    return outs[:n], outs[n : 2 * n], outs[2 * n :]


def _load_weights(pairs, sems):
    @pl.when(pl.program_id(0) == 0)
    def _():
        cps = [pltpu.make_async_copy(src, dst, sems.at[k]) for k, (src, dst) in enumerate(pairs)]
        for cp in cps:
            cp.start()
        for cp in cps:
            cp.wait()


def _ffn_chunks(F):
    bounds = list(range(0, F, FFN_CHUNK)) + [F]
    return list(zip(bounds[:-1], bounds[1:]))


def _shifted_specs(tm, cols):
    per = tm // N_META
    return [_row_spec(tm, cols), pl.BlockSpec((N_META, cols), lambda i: (jnp.maximum(i * per - 1, 0), 0))]


def _shifted_tile(cur_ref, before_ref, tm):
    return jnp.concatenate([before_ref[...], cur_ref[0 : tm - N_META, :]], axis=0)


def _ffn_fwd(h, g_pre, g_post, wg, wu, wd, name, exchange=None, meta=None, target=None):
    D = h.shape[1]
    T = h.shape[0] + (0 if meta is None else N_META)
    F = wg.shape[0]
    tm = FFN_TILE
    n_src = 1 if meta is None else 3
    n_tgt = 0 if target is None else 2

    def body(*refs):
        src, refs = refs[:n_src], refs[n_src:]
        tgt, refs = refs[:n_tgt], refs[n_tgt:]
        gpre_ref, gpost_ref, wg_hbm, wu_hbm, wd_hbm, hout_ref, f_ref, ga_ref, si_ref, s_ref, n_ref = refs[:11]
        extra, (wg_v, wu_v, wd_v, sems) = refs[11:-4], refs[-4:]
        i = pl.program_id(0)
        _load_weights([(wg_hbm, wg_v), (wu_hbm, wu_v), (wd_hbm, wd_v)], sems)
        if meta is None:
            hh = src[0][...]
        else:
            hh = jnp.concatenate([jnp.where(i == 0, src[2][...], src[1][...]), src[0][0 : tm - N_META, :]], axis=0)
            extra[0][...] = hh
        n = (hh * _rms_stat(hh) * gpre_ref[...]).astype(MXU_DTYPE)
        n_ref[...] = n.astype(n_ref.dtype)
        f = jnp.zeros((tm, D), F32)
        for lo, hi in _ffn_chunks(F):
            a = _mm_nt(n, wg_v[lo:hi, :])
            b = _mm_nt(n, wu_v[lo:hi, :])
            sg = _sigmoid(a)
            si = a * sg
            s = (si * b).astype(MXU_DTYPE)
            ga_ref[:, lo:hi] = (b * (sg * (1.0 + a * (1.0 - sg)))).astype(ga_ref.dtype)
            si_ref[:, lo:hi] = si.astype(si_ref.dtype)
            s_ref[:, lo:hi] = s.astype(s_ref.dtype)
            f = f + _mm(s, wd_v[lo:hi, :])
        f_ref[...] = f
        out = hh + 0.5 * (f * _rms_stat(f) * gpost_ref[...])
        hout_ref[...] = out
        if target is not None:
            rows = i * tm + lax.broadcasted_iota(jnp.int32, (tm, D), 0)
            err = jnp.where(rows >= N_META, out - _shifted_tile(tgt[0], tgt[1], tm), 0.0)
            extra[-2][...] = err * (1.0 / D)
            _acc(extra[-1], jnp.sum(err * err, axis=0, keepdims=True), i == 0)

    tok = jax.ShapeDtypeStruct((T, D), F32)
    act = jax.ShapeDtypeStruct((T, F), MXU_DTYPE)
    src_specs = [_row_spec(tm, D)] if meta is None else _shifted_specs(tm, D) + [_full_spec((N_META, D))]
    src_ops = (h,) if meta is None else (h, h, meta)
    tgt_specs, tgt_ops = ([], ()) if target is None else (_shifted_specs(tm, D), (target, target))
    extra_shapes = ([] if meta is None else [tok]) + ([] if target is None else [tok, jax.ShapeDtypeStruct((1, D), F32)])
    extra_specs = ([] if meta is None else [_row_spec(tm, D)]) + ([] if target is None else [_row_spec(tm, D), _full_spec((1, D))])
    return _pallas(
        body,
        name=name,
        grid=(T // tm,),
        out_shape=[tok, tok, act, act, act, jax.ShapeDtypeStruct((T, D), MXU_DTYPE)] + extra_shapes,
        in_specs=src_specs + tgt_specs + [_full_spec((1, D)), _full_spec((1, D)), ANY, ANY, ANY],
        out_specs=[_row_spec(tm, D), _row_spec(tm, D), _row_spec(tm, F), _row_spec(tm, F), _row_spec(tm, F), _row_spec(tm, D)] + extra_specs,
        scratch_shapes=[pltpu.VMEM(wg.shape, wg.dtype), pltpu.VMEM(wu.shape, wu.dtype), pltpu.VMEM(wd.shape, wd.dtype), pltpu.SemaphoreType.DMA((3,))],
        operands=(*src_ops, *tgt_ops, g_pre, g_post, wg, wu, wd),
        exchange=exchange,
    )


def _ffn_bwd_down(dh, f, ga, si, g_post, wd, name, exchange=None):
    T, D = dh.shape
    F = wd.shape[0]
    tm = FFN_TILE

    def body(dh_ref, f_ref, ga_ref, si_ref, gpost_ref, wd_hbm, da_ref, db_ref, df_ref, dg_ref, wd_v, sems):
        _load_weights([(wd_hbm, wd_v)], sems)
        df, dg = _rms_bwd(f_ref[...], gpost_ref[...], 0.5 * dh_ref[...])
        _acc(dg_ref, dg, pl.program_id(0) == 0)
        dfb = df.astype(MXU_DTYPE)
        df_ref[...] = dfb.astype(df_ref.dtype)
        for lo, hi in _ffn_chunks(F):
            ds = _mm_nt(dfb, wd_v[lo:hi, :])
            da_ref[:, lo:hi] = (ds * ga_ref[:, lo:hi].astype(F32)).astype(da_ref.dtype)
            db_ref[:, lo:hi] = (ds * si_ref[:, lo:hi].astype(F32)).astype(db_ref.dtype)

    act = jax.ShapeDtypeStruct((T, F), MXU_DTYPE)
    return _pallas(
        body,
        name=name,
        grid=(T // tm,),
        out_shape=[act, act, jax.ShapeDtypeStruct((T, D), MXU_DTYPE), jax.ShapeDtypeStruct((1, D), F32)],
        in_specs=[_row_spec(tm, D), _row_spec(tm, D), _row_spec(tm, F), _row_spec(tm, F), _full_spec((1, D)), ANY],
        out_specs=[_row_spec(tm, F), _row_spec(tm, F), _row_spec(tm, D), _full_spec((1, D))],
        scratch_shapes=[pltpu.VMEM(wd.shape, wd.dtype), pltpu.SemaphoreType.DMA((1,))],
        operands=(dh, f, ga, si, g_post, wd),
        exchange=exchange,
    )


def _ffn_bwd_up(da, db, h, dh, g_pre, wg, wu, name, exchange=None):
    T, D = h.shape
    F = wg.shape[0]
    tm = FFN_TILE

    def body(da_ref, db_ref, h_ref, dh_ref, gpre_ref, wg_hbm, wu_hbm, dhin_ref, dg_ref, wg_v, wu_v, sems):
        _load_weights([(wg_hbm, wg_v), (wu_hbm, wu_v)], sems)
        dn = jnp.zeros((tm, D), F32)
        for lo, hi in _ffn_chunks(F):
            dn = dn + _mm(da_ref[:, lo:hi], wg_v[lo:hi, :]) + _mm(db_ref[:, lo:hi], wu_v[lo:hi, :])
        dx, dg = _rms_bwd(h_ref[...], gpre_ref[...], dn)
        _acc(dg_ref, dg, pl.program_id(0) == 0)
        dhin_ref[...] = dh_ref[...] + dx

    return _pallas(
        body,
        name=name,
        grid=(T // tm,),
        out_shape=[jax.ShapeDtypeStruct((T, D), F32), jax.ShapeDtypeStruct((1, D), F32)],
        in_specs=[_row_spec(tm, F), _row_spec(tm, F), _row_spec(tm, D), _row_spec(tm, D), _full_spec((1, D)), ANY, ANY],
        out_specs=[_row_spec(tm, D), _full_spec((1, D))],
        scratch_shapes=[pltpu.VMEM(wg.shape, wg.dtype), pltpu.VMEM(wu.shape, wu.dtype), pltpu.SemaphoreType.DMA((2,))],
        operands=(da, db, h, dh, g_pre, wg, wu),
        exchange=exchange,
    )


def _ffn_bwd(dh, f, ga, si, h, g_post, g_pre, wg, wu, wd, name):
    T, D = dh.shape
    F = wd.shape[0]
    tm = FFN_TILE

    def body(dh_ref, f_ref, ga_ref, si_ref, h_ref, gpost_ref, gpre_ref, wg_hbm, wu_hbm, wd_hbm, da_ref, db_ref, df_ref, dhin_ref, dgpost_ref, dgpre_ref, wg_v, wu_v, wd_v, sems):
        first = pl.program_id(0) == 0
        _load_weights([(wg_hbm, wg_v), (wu_hbm, wu_v), (wd_hbm, wd_v)], sems)
        dh = dh_ref[...]
        df, dg = _rms_bwd(f_ref[...], gpost_ref[...], 0.5 * dh)
        _acc(dgpost_ref, dg, first)
        dfb = df.astype(MXU_DTYPE)
        df_ref[...] = dfb.astype(df_ref.dtype)
        dn = jnp.zeros((tm, D), F32)
        for lo, hi in _ffn_chunks(F):
            ds = _mm_nt(dfb, wd_v[lo:hi, :])
            da = (ds * ga_ref[:, lo:hi].astype(F32)).astype(MXU_DTYPE)
            db = (ds * si_ref[:, lo:hi].astype(F32)).astype(MXU_DTYPE)
            da_ref[:, lo:hi] = da.astype(da_ref.dtype)
            db_ref[:, lo:hi] = db.astype(db_ref.dtype)
            dn = dn + _mm(da, wg_v[lo:hi, :]) + _mm(db, wu_v[lo:hi, :])
        dx, dg = _rms_bwd(h_ref[...], gpre_ref[...], dn)
        _acc(dgpre_ref, dg, first)
        dhin_ref[...] = dh + dx

    act = jax.ShapeDtypeStruct((T, F), MXU_DTYPE)
    vec = jax.ShapeDtypeStruct((1, D), F32)
    outs, _ = _pallas(
        body,
        name=name,
        grid=(T // tm,),
        out_shape=[act, act, jax.ShapeDtypeStruct((T, D), MXU_DTYPE), jax.ShapeDtypeStruct((T, D), F32), vec, vec],
        in_specs=[_row_spec(tm, D), _row_spec(tm, D), _row_spec(tm, F), _row_spec(tm, F), _row_spec(tm, D), _full_spec((1, D)), _full_spec((1, D)), ANY, ANY, ANY],
        out_specs=[_row_spec(tm, F), _row_spec(tm, F), _row_spec(tm, D), _row_spec(tm, D), _full_spec((1, D)), _full_spec((1, D))],
        scratch_shapes=[pltpu.VMEM(wg.shape, wg.dtype), pltpu.VMEM(wu.shape, wu.dtype), pltpu.VMEM(wd.shape, wd.dtype), pltpu.SemaphoreType.DMA((3,))],
        operands=(dh, f, ga, si, h, g_post, g_pre, wg, wu, wd),
    )
    return outs


def _token_tile(T):
    for t in (912, 864, 432):
        if T % t == 0:
            return t
    raise ValueError(f"no token tile for {T} rows")


def _tn_matmul(xm, ym, name, exchange=None):
    T, M = xm.shape
    N = ym.shape[1]
    tk = _token_tile(T)

    def body(x_ref, y_ref, o_ref):
        _acc(o_ref, _mm_tn(x_ref[...], y_ref[...]), pl.program_id(0) == 0)

    (out,), x_outs = _pallas(
        body,
        name=name,
        grid=(T // tk,),
        out_shape=[jax.ShapeDtypeStruct((M, N), F32)],
        in_specs=[pl.BlockSpec((tk, M), lambda k: (k, 0)), pl.BlockSpec((tk, N), lambda k: (k, 0))],
        out_specs=[_full_spec((M, N))],
        operands=(xm, ym),
        exchange=exchange,
    )
    return out, x_outs


TAB_A, TAB_AS1, TAB_AS2, TAB_AS4, TAB_JF, TAB_JB = 0, 2, 4, 6, 8, 10


def _scan_inplace(zr, zi, tabs, pows, car_r, car_i, seg, reverse):
    n_slabs = zr.shape[0]
    sgn = -1.0 if reverse else 1.0
    row = lax.broadcasted_iota(jnp.int32, (SUBLANES, LANES), 0)

    def cmul(pr, pi, xr, xi):
        return pr * xr - pi * xi, pr * xi + pi * xr

    for k0 in range(0, n_slabs, SLAB_GROUP):
        slabs = range(k0, min(k0 + SLAB_GROUP, n_slabs))
        ar = [tabs[TAB_A, k] for k in slabs]
        ai = [sgn * tabs[TAB_A + 1, k] for k in slabs]

        def first_pass(t, carry):
            r = (seg - 1 - t) if reverse else t
            out = []
            for q, k in enumerate(slabs):
                xr, xi = carry[2 * q], carry[2 * q + 1]
                pr, pi = cmul(ar[q], ai[q], xr, xi)
                nr = pr + zr[k, pl.ds(r, SUBLANES, stride=seg), :]
                ni = pi + zi[k, pl.ds(r, SUBLANES, stride=seg), :]
                zr[k, pl.ds(r, SUBLANES, stride=seg), :] = nr
                zi[k, pl.ds(r, SUBLANES, stride=seg), :] = ni
                out += [nr, ni]
            return tuple(out)

        ends = lax.fori_loop(0, seg, first_pass, tuple(jnp.zeros((SUBLANES, LANES), F32) for _ in range(2 * len(slabs))))

        incoming = []
        for q, k in enumerate(slabs):
            fr, fi = ends[2 * q], ends[2 * q + 1]
            for d, tab in ((1, TAB_AS1), (2, TAB_AS2), (4, TAB_AS4)):
                shift, keep = (SUBLANES - d, row < SUBLANES - d) if reverse else (d, row >= d)
                sr = jnp.where(keep, pltpu.roll(fr, shift, 0), 0.0)
                si = jnp.where(keep, pltpu.roll(fi, shift, 0), 0.0)
                pr, pi = cmul(tabs[tab, k], sgn * tabs[tab + 1, k], sr, si)
                fr, fi = fr + pr, fi + pi
            cr, ci = car_r[k], car_i[k]
            jtab = TAB_JB if reverse else TAB_JF
            pr, pi = cmul(tabs[jtab, k], sgn * tabs[jtab + 1, k], cr, ci)
            er, ei = fr + pr, fi + pi
            if reverse:
                inr = jnp.where(row < SUBLANES - 1, pltpu.roll(er, SUBLANES - 1, 0), cr)
                ini = jnp.where(row < SUBLANES - 1, pltpu.roll(ei, SUBLANES - 1, 0), ci)
                car_r[k] = jnp.broadcast_to(er[0:1, :], (SUBLANES, LANES))
                car_i[k] = jnp.broadcast_to(ei[0:1, :], (SUBLANES, LANES))
            else:
                inr = jnp.where(row >= 1, pltpu.roll(er, 1, 0), cr)
                ini = jnp.where(row >= 1, pltpu.roll(ei, 1, 0), ci)
                car_r[k] = jnp.broadcast_to(er[SUBLANES - 1 : SUBLANES, :], (SUBLANES, LANES))
                car_i[k] = jnp.broadcast_to(ei[SUBLANES - 1 : SUBLANES, :], (SUBLANES, LANES))
            incoming += [inr, ini]

        def second_pass(r, _):
            p = (seg - 1 - r) if reverse else r
            for q, k in enumerate(slabs):
                pr, pi = cmul(pows[0, k, p], sgn * pows[1, k, p], incoming[2 * q], incoming[2 * q + 1])
                zr[k, pl.ds(r, SUBLANES, stride=seg), :] = zr[k, pl.ds(r, SUBLANES, stride=seg), :] + pr
                zi[k, pl.ds(r, SUBLANES, stride=seg), :] = zi[k, pl.ds(r, SUBLANES, stride=seg), :] + pi
            return 0

        lax.fori_loop(0, seg, second_pass, 0)


def _slabs_to_cols(ref, k0, n):
    return jnp.concatenate([ref[k0 + q] for q in range(n)], axis=1)


def _window_sum(ext, doublings, forward):
    rows = ext.shape[0]
    s = ext
    for k in range(doublings):
        s = s + pltpu.roll(s, (1 << k) if forward else rows - (1 << k), 0)
    return s


def _mix_fwd(h1, g_pre, g_so, g_po, g_post, dskip, pscale, win, wout, bbre, bbim, ccre, ccim, wgv, wgg, pw, tabs, pows, name):
    T, D = h1.shape
    W = D // 2
    tm = MIX_TILE
    seg = tm // SUBLANES
    n_slabs = tabs.shape[1]
    nch, cch, sch = bbre.shape
    spc = sch // LANES
    pg = W // len(POOL_WINDOWS)

    def body(h_ref, gpre_ref, gso_ref, gpo_ref, gpost_ref, dskip_ref, pscale_ref, win_ref, wout_ref, bbre_ref, bbim_ref, ccre_ref, ccim_ref, wgv_ref, wgg_ref, pw_ref, tabs_ref, pows_ref,
             proj_ref, xr_ref, xi_ref, y_ref, pooled_ref, mixed_ref, h2_ref, n2_ref, cat_ref, car_r, car_i, halo):
        i = pl.program_id(0)

        @pl.when(i == 0)
        def _():
            car_r[...] = jnp.zeros_like(car_r)
            car_i[...] = jnp.zeros_like(car_i)
            halo[...] = jnp.zeros_like(halo)

        hh = h_ref[...]
        n2 = (hh * _rms_stat(hh) * gpre_ref[...]).astype(MXU_DTYPE)
        n2_ref[...] = n2.astype(n2_ref.dtype)
        proj = _mm(n2, win_ref[...])
        proj_ref[...] = proj
        us, up = proj[:, :W], proj[:, W:]

        for c in range(nch):
            uc = us[:, c * cch : (c + 1) * cch].astype(MXU_DTYPE)
            bur, bui = _mm(uc, bbre_ref[c]), _mm(uc, bbim_ref[c])
            for q in range(spc):
                xr_ref[c * spc + q] = bur[:, q * LANES : (q + 1) * LANES]
                xi_ref[c * spc + q] = bui[:, q * LANES : (q + 1) * LANES]
        _scan_inplace(xr_ref, xi_ref, tabs_ref, pows_ref, car_r, car_i, seg, reverse=False)
        ys = []
        for c in range(nch):
            ys.append(_mm(_slabs_to_cols(xr_ref, c * spc, spc), ccre_ref[c]) - _mm(_slabs_to_cols(xi_ref, c * spc, spc), ccim_ref[c]))
        y = jnp.concatenate(ys, axis=1) + dskip_ref[...] * us
        y_ref[...] = y
        ge = _gelu(y).astype(MXU_DTYPE)
        zv = jnp.concatenate([_mm(ge[:, c * cch : (c + 1) * cch], wgv_ref[c]) for c in range(nch)], axis=1)
        zg = jnp.concatenate([_mm(ge[:, c * cch : (c + 1) * cch], wgg_ref[c]) for c in range(nch)], axis=1)
        out = zv * _sigmoid(zg)
        cat_s = out * _rms_stat(out) * gso_ref[...]

        ext = jnp.concatenate([halo[...], up], axis=0)
        halo[...] = up[tm - POOL_HALO :, :]
        t1 = (i * tm + 1 + lax.broadcasted_iota(jnp.int32, (tm, pg), 0)).astype(F32)
        pooled, pms = [], []
        for g, w in enumerate(POOL_WINDOWS):
            col = ext[:, g * pg : (g + 1) * pg]
            win_sum = _window_sum(col, g + 1, True)[POOL_HALO:, :]
            pooled_g = win_sum / jnp.minimum(t1, float(w)) - up[:, g * pg : (g + 1) * pg]
            pooled.append(pooled_g)
            pms.append(_mm(pooled_g, pw_ref[g]))
        pooled_ref[...] = jnp.concatenate(pooled, axis=1)
        yp = jnp.concatenate(pms, axis=1) * pscale_ref[...]
        cat_p = yp * _rms_stat(yp) * gpo_ref[...]

        cat = jnp.concatenate([cat_s, cat_p], axis=1).astype(MXU_DTYPE)
        cat_ref[...] = cat.astype(cat_ref.dtype)
        mixed = _mm(cat, wout_ref[...])
        mixed_ref[...] = mixed
        h2_ref[...] = hh + mixed * _rms_stat(mixed) * gpost_ref[...]

    tok = lambda cols, dt=F32: jax.ShapeDtypeStruct((T, cols), dt)
    slab_spec = pl.BlockSpec((n_slabs, tm, LANES), lambda i: (0, i, 0))
    operands = (h1, g_pre, g_so, g_po, g_post, dskip, pscale, win, wout, bbre, bbim, ccre, ccim, wgv, wgg, pw, tabs, pows)
    outs, _ = _pallas(
        body,
        name=name,
        grid=(T // tm,),
        out_shape=[tok(D), jax.ShapeDtypeStruct((n_slabs, T, LANES), F32), jax.ShapeDtypeStruct((n_slabs, T, LANES), F32), tok(W), tok(W), tok(D), tok(D), tok(D, MXU_DTYPE), tok(D, MXU_DTYPE)],
        in_specs=[_row_spec(tm, D)] + [_full_spec(o.shape) for o in operands[1:]],
        out_specs=[_row_spec(tm, D), slab_spec, slab_spec, _row_spec(tm, W), _row_spec(tm, W), _row_spec(tm, D), _row_spec(tm, D), _row_spec(tm, D), _row_spec(tm, D)],
        scratch_shapes=[pltpu.VMEM((n_slabs, SUBLANES, LANES), F32), pltpu.VMEM((n_slabs, SUBLANES, LANES), F32), pltpu.VMEM((POOL_HALO, W), F32)],
        operands=operands,
    )
    return outs


def _mix_bwd_heads(dh2, mixed, y, pooled, proj, g_so, g_po, g_post, pscale, wout, wgv, wgg, pw, name, exchange=None):
    T, D = dh2.shape
    W = D // 2
    tm = FFN_TILE
    nch, cch, _ = wgv.shape
    ng, pg, _ = pw.shape

    def body(dh2_ref, mixed_ref, y_ref, pooled_ref, us_ref, gso_ref, gpo_ref, gpost_ref, pscale_ref, wout_ref, wgv_ref, wgg_ref, pw_ref,
             dy_ref, dpooled_ref, dmixed_ref, dgpost_ref, dgso_ref, dgpo_ref, dd_ref, dscale_ref, dwgv_ref, dwgg_ref, dpw_ref):
        first = pl.program_id(0) == 0
        dmixed, dgpost = _rms_bwd(mixed_ref[...], gpost_ref[...], dh2_ref[...])
        _acc(dgpost_ref, dgpost, first)
        dmb = dmixed.astype(MXU_DTYPE)
        dmixed_ref[...] = dmb.astype(dmixed_ref.dtype)
        dcat = _mm_nt(dmb, wout_ref[...])
        dcs, dcp = dcat[:, :W], dcat[:, W:]

        y = y_ref[...]
        ge = _gelu(y).astype(MXU_DTYPE)
        zv = jnp.concatenate([_mm(ge[:, c * cch : (c + 1) * cch], wgv_ref[c]) for c in range(nch)], axis=1)
        zg = jnp.concatenate([_mm(ge[:, c * cch : (c + 1) * cch], wgg_ref[c]) for c in range(nch)], axis=1)
        sg = _sigmoid(zg)
        dout, dgso = _rms_bwd(zv * sg, gso_ref[...], dcs)
        _acc(dgso_ref, dgso, first)
        dzv = (dout * sg).astype(MXU_DTYPE)
        dzg = (dout * zv * sg * (1.0 - sg)).astype(MXU_DTYPE)
        dges = []
        for c in range(nch):
            cs = slice(c * cch, (c + 1) * cch)
            dges.append(_mm_nt(dzv[:, cs], wgv_ref[c]) + _mm_nt(dzg[:, cs], wgg_ref[c]))
            _acc(dwgv_ref.at[c], _mm_tn(ge[:, cs], dzv[:, cs]), first)
            _acc(dwgg_ref.at[c], _mm_tn(ge[:, cs], dzg[:, cs]), first)
        dy = jnp.concatenate(dges, axis=1) * _gelu_grad(y)
        dy_ref[...] = dy
        _acc(dd_ref, jnp.sum(dy * us_ref[...], axis=0, keepdims=True), first)

        pooled_b = pooled_ref[...].astype(MXU_DTYPE)
        pm = jnp.concatenate([_mm(pooled_b[:, g * pg : (g + 1) * pg], pw_ref[g]) for g in range(ng)], axis=1)
        dyp, dgpo = _rms_bwd(pm * pscale_ref[...], gpo_ref[...], dcp)
        _acc(dgpo_ref, dgpo, first)
        _acc(dscale_ref, jnp.sum(dyp * pm, axis=0, keepdims=True), first)
        dpm = (dyp * pscale_ref[...]).astype(MXU_DTYPE)
        dps = []
        for g in range(ng):
            gs = slice(g * pg, (g + 1) * pg)
            dps.append(_mm_nt(dpm[:, gs], pw_ref[g]))
            _acc(dpw_ref.at[g], _mm_tn(pooled_b[:, gs], dpm[:, gs]), first)
        dpooled_ref[...] = jnp.concatenate(dps, axis=1)

    vec = lambda n: jax.ShapeDtypeStruct((1, n), F32)
    operands = (dh2, mixed, y, pooled, proj, g_so, g_po, g_post, pscale, wout, wgv, wgg, pw)
    return _pallas(
        body,
        name=name,
        grid=(T // tm,),
        out_shape=[jax.ShapeDtypeStruct((T, W), F32), jax.ShapeDtypeStruct((T, W), F32), jax.ShapeDtypeStruct((T, D), MXU_DTYPE), vec(D), vec(W), vec(W), vec(W), vec(W),
                   jax.ShapeDtypeStruct(wgv.shape, F32), jax.ShapeDtypeStruct(wgg.shape, F32), jax.ShapeDtypeStruct(pw.shape, F32)],
        in_specs=[_row_spec(tm, D), _row_spec(tm, D), _row_spec(tm, W), _row_spec(tm, W), _row_spec(tm, W)] + [_full_spec(o.shape) for o in operands[5:]],
        out_specs=[_row_spec(tm, W), _row_spec(tm, W), _row_spec(tm, D), _full_spec((1, D)), _full_spec((1, W)), _full_spec((1, W)), _full_spec((1, W)), _full_spec((1, W)),
                   _full_spec(wgv.shape), _full_spec(wgg.shape), _full_spec(pw.shape)],
        operands=operands,
        exchange=exchange,
    )


def _mix_bwd_scan(dy, dpooled, xr, xi, proj, h1, dh2, g_pre, dskip, win, bbre, bbim, ccre, ccim, tabs, pows, name, exchange=None):
    T, D = h1.shape
    W = D // 2
    tm = MIX_TILE
    seg = tm // SUBLANES
    nt = T // tm
    n_slabs = tabs.shape[1]
    nch, cch, sch = bbre.shape
    spc = sch // LANES
    pg = W // len(POOL_WINDOWS)
    blocks_per_tile = tm // SUBLANES

    def body(dy_ref, dp_ref, xr_ref, xi_ref, xpr_ref, xpi_ref, proj_ref, h_ref, dh2_ref, gpre_ref, dskip_ref, win_ref, bbre_ref, bbim_ref, ccre_ref, ccim_ref, tabs_ref, pows_ref,
             dh1_ref, dproj_ref, dgpre_ref, dccre_ref, dccim_ref, dbbre_ref, dbbim_ref, dar_ref, dai_ref, lr, li, car_r, car_i, halo):
        i = pl.program_id(0)
        first = i == 0
        tile = nt - 1 - i
        row = lax.broadcasted_iota(jnp.int32, (SUBLANES, LANES), 0)

        @pl.when(first)
        def _():
            car_r[...] = jnp.zeros_like(car_r)
            car_i[...] = jnp.zeros_like(car_i)
            halo[...] = jnp.zeros_like(halo)
            dar_ref[...] = jnp.zeros_like(dar_ref)
            dai_ref[...] = jnp.zeros_like(dai_ref)

        dy = dy_ref[...]
        for c in range(nch):
            dyc = dy[:, c * cch : (c + 1) * cch].astype(MXU_DTYPE)
            gr, gi = _mm_nt(dyc, ccre_ref[c]), _mm_nt(dyc, ccim_ref[c])
            for q in range(spc):
                lr[c * spc + q] = gr[:, q * LANES : (q + 1) * LANES]
                li[c * spc + q] = -gi[:, q * LANES : (q + 1) * LANES]
            _acc(dccre_ref.at[c], _mm_tn(_slabs_to_cols(xr_ref, c * spc, spc), dyc), first)
            _acc(dccim_ref.at[c], -_mm_tn(_slabs_to_cols(xi_ref, c * spc, spc), dyc), first)
        _scan_inplace(lr, li, tabs_ref, pows_ref, car_r, car_i, seg, reverse=True)

        for k0 in range(0, n_slabs, SLAB_GROUP):
            slabs = range(k0, min(k0 + SLAB_GROUP, n_slabs))
            init = []
            for k in slabs:
                prev_r = jnp.where(tile > 0, jnp.broadcast_to(xpr_ref[k, SUBLANES - 1 : SUBLANES, :], (SUBLANES, LANES)), 0.0)
                prev_i = jnp.where(tile > 0, jnp.broadcast_to(xpi_ref[k, SUBLANES - 1 : SUBLANES, :], (SUBLANES, LANES)), 0.0)
                x0r = jnp.where(row >= 1, pltpu.roll(xr_ref[k, pl.ds(seg - 1, SUBLANES, stride=seg), :], 1, 0), prev_r)
                x0i = jnp.where(row >= 1, pltpu.roll(xi_ref[k, pl.ds(seg - 1, SUBLANES, stride=seg), :], 1, 0), prev_i)
                l0r, l0i = lr[k, pl.ds(0, SUBLANES, stride=seg), :], li[k, pl.ds(0, SUBLANES, stride=seg), :]
                init += [l0r * x0r + l0i * x0i, l0i * x0r - l0r * x0i]

            def step(r, acc, slabs=slabs):
                out = []
                for q, k in enumerate(slabs):
                    pr_, pi_ = xr_ref[k, pl.ds(r - 1, SUBLANES, stride=seg), :], xi_ref[k, pl.ds(r - 1, SUBLANES, stride=seg), :]
                    lr_, li_ = lr[k, pl.ds(r, SUBLANES, stride=seg), :], li[k, pl.ds(r, SUBLANES, stride=seg), :]
                    out += [acc[2 * q] + lr_ * pr_ + li_ * pi_, acc[2 * q + 1] + li_ * pr_ - lr_ * pi_]
                return tuple(out)

            sums = lax.fori_loop(1, seg, step, tuple(init))
            for q, k in enumerate(slabs):
                dar_ref[k] += sums[2 * q]
                dai_ref[k] += sums[2 * q + 1]

        us = proj_ref[:, :W]
        dus = []
        for c in range(nch):
            lrc, lic = _slabs_to_cols(lr, c * spc, spc).astype(MXU_DTYPE), _slabs_to_cols(li, c * spc, spc).astype(MXU_DTYPE)
            uc = us[:, c * cch : (c + 1) * cch].astype(MXU_DTYPE)
            _acc(dbbre_ref.at[c], _mm_tn(uc, lrc), first)
            _acc(dbbim_ref.at[c], _mm_tn(uc, lic), first)
            dus.append(_mm_nt(lrc, bbre_ref[c]) + _mm_nt(lic, bbim_ref[c]))
        du_s = jnp.concatenate(dus, axis=1) + dskip_ref[...] * dy

        dp = dp_ref[...]
        t1 = (tile * tm + 1 + lax.broadcasted_iota(jnp.int32, (tm, pg), 0)).astype(F32)
        dups, heads = [], []
        for g, w in enumerate(POOL_WINDOWS):
            dpg = dp[:, g * pg : (g + 1) * pg]
            qg = dpg / jnp.minimum(t1, float(w))
            ext = jnp.concatenate([qg, halo[:, g * pg : (g + 1) * pg]], axis=0)
            dups.append(_window_sum(ext, g + 1, False)[:tm, :] - dpg)
            heads.append(qg[:POOL_HALO, :])
        halo[...] = jnp.concatenate(heads, axis=1)
        dproj = jnp.concatenate([du_s] + dups, axis=1).astype(MXU_DTYPE)
        dproj_ref[...] = dproj.astype(dproj_ref.dtype)
        dx, dg = _rms_bwd(h_ref[...], gpre_ref[...], _mm_nt(dproj, win_ref[...]))
        _acc(dgpre_ref, dg, first)
        dh1_ref[...] = dh2_ref[...] + dx

    rev = lambda cols: _row_spec(tm, cols, rev_n=nt)
    slab_spec = pl.BlockSpec((n_slabs, tm, LANES), lambda i: (0, nt - 1 - i, 0))
    prev_spec = pl.BlockSpec((n_slabs, SUBLANES, LANES), lambda i: (0, jnp.maximum((nt - 1 - i) * blocks_per_tile - 1, 0), 0))
    consts = (g_pre, dskip, win, bbre, bbim, ccre, ccim, tabs, pows)
    return _pallas(
        body,
        name=name,
        grid=(nt,),
        out_shape=[jax.ShapeDtypeStruct((T, D), F32), jax.ShapeDtypeStruct((T, D), MXU_DTYPE), jax.ShapeDtypeStruct((1, D), F32),
                   jax.ShapeDtypeStruct(ccre.shape, F32), jax.ShapeDtypeStruct(ccim.shape, F32), jax.ShapeDtypeStruct(bbre.shape, F32), jax.ShapeDtypeStruct(bbim.shape, F32),
                   jax.ShapeDtypeStruct((n_slabs, SUBLANES, LANES), F32), jax.ShapeDtypeStruct((n_slabs, SUBLANES, LANES), F32)],
        in_specs=[rev(W), rev(W), slab_spec, slab_spec, prev_spec, prev_spec, rev(D), rev(D), rev(D)] + [_full_spec(o.shape) for o in consts],
        out_specs=[rev(D), rev(D), _full_spec((1, D)), _full_spec(ccre.shape), _full_spec(ccim.shape), _full_spec(bbre.shape), _full_spec(bbim.shape),
                   _full_spec((n_slabs, SUBLANES, LANES)), _full_spec((n_slabs, SUBLANES, LANES))],
        scratch_shapes=[pltpu.VMEM((n_slabs, tm, LANES), F32), pltpu.VMEM((n_slabs, tm, LANES), F32), pltpu.VMEM((n_slabs, SUBLANES, LANES), F32), pltpu.VMEM((n_slabs, SUBLANES, LANES), F32),
                        pltpu.VMEM((POOL_HALO, W), F32)],
        operands=(dy, dpooled, xr, xi, xr, xi, proj, h1, dh2, *consts),
        exchange=exchange,
    )


def _discretize(lam_re, lam_im, log_dt, b_re, b_im):
    dt = jnp.exp(log_dt)[:, None]
    decay = jnp.exp(lam_re * dt)
    ang = lam_im * dt
    a_re, a_im = decay * jnp.cos(ang), decay * jnp.sin(ang)
    nr = a_re - 1.0
    den = lam_re * lam_re + lam_im * lam_im
    q_re = (nr * lam_re + a_im * lam_im) / den
    q_im = (a_im * lam_re - nr * lam_im) / den
    bb_re = q_re[..., None] * b_re - q_im[..., None] * b_im
    bb_im = q_re[..., None] * b_im + q_im[..., None] * b_re
    return a_re, a_im, bb_re, bb_im


GROUPS_PER_CHUNK = 16


def _block_diag(w, rows_first):
    G = w.shape[0]
    nch = G // GROUPS_PER_CHUNK
    if not rows_first:
        w = jnp.swapaxes(w, 1, 2)
    p, q = w.shape[1], w.shape[2]
    eye = jnp.eye(GROUPS_PER_CHUNK, dtype=w.dtype)
    out = jnp.einsum("cgpq,gk->cgpkq", w.reshape(nch, GROUPS_PER_CHUNK, p, q), eye)
    return out.reshape(nch, GROUPS_PER_CHUNK * p, GROUPS_PER_CHUNK * q)


def _block_diag_extract(m, p, q, rows_first):
    nch = m.shape[0]
    eye = jnp.eye(GROUPS_PER_CHUNK, dtype=m.dtype)
    out = jnp.einsum("cgpkq,gk->cgpq", m.reshape(nch, GROUPS_PER_CHUNK, p, GROUPS_PER_CHUNK, q), eye).reshape(nch * GROUPS_PER_CHUNK, p, q)
    return out if rows_first else jnp.swapaxes(out, 1, 2)


def _cmul(ar, ai, br, bi):
    return ar * br - ai * bi, ar * bi + ai * br


def _powers(ar, ai, count):
    pr, pi = ar[None], ai[None]
    while pr.shape[0] < count:
        nr, ni = _cmul(pr, pi, pr[-1][None], pi[-1][None])
        pr, pi = jnp.concatenate([pr, nr]), jnp.concatenate([pi, ni])
    return pr[:count], pi[:count]


def _scan_tables(a_re, a_im, seg):
    n = a_re.size
    ns = n // LANES
    ar, ai = a_re.reshape(n), a_im.reshape(n)
    pr, pi = _powers(ar, ai, seg)
    jr, ji = _powers(pr[-1], pi[-1], SUBLANES)

    def bcast(v):
        return jnp.broadcast_to(v.reshape(ns, 1, LANES), (ns, SUBLANES, LANES))

    def per_sublane(vs):
        return jnp.transpose(vs.reshape(SUBLANES, ns, LANES), (1, 0, 2))

    tabs = jnp.stack([bcast(ar), bcast(ai), bcast(jr[0]), bcast(ji[0]), bcast(jr[1]), bcast(ji[1]), bcast(jr[3]), bcast(ji[3]),
                      per_sublane(jr), per_sublane(ji), per_sublane(jr[::-1]), per_sublane(ji[::-1])])

    def rows(vs):
        return jnp.broadcast_to(jnp.transpose(vs.reshape(seg, ns, 1, LANES), (1, 0, 2, 3)), (ns, seg, SUBLANES, LANES))

    return tabs, jnp.stack([rows(pr), rows(pi)])


SMALL = ("ffn1_pre_norm", "ffn1_post_norm", "mix_pre_norm", "mix_post_norm", "ssm_lambda_re", "ssm_lambda_im", "ssm_log_dt", "ssm_b_re", "ssm_b_im", "ssm_c_re", "ssm_c_im",
         "ssm_d", "ssm_w_glu", "pool_w", "pool_scale", "ssm_out_norm", "pool_out_norm", "ffn2_pre_norm", "ffn2_post_norm")
BIG = ("ffn1_w_gate", "ffn1_w_up", "ffn1_w_down", "w_in", "w_out", "ffn2_w_gate", "ffn2_w_up", "ffn2_w_down")
ORDER = ("meta_tokens", "ffn1_pre_norm", "ffn1_post_norm", "ffn1_w_gate", "ffn1_w_up", "ffn1_w_down", "mix_pre_norm", "mix_post_norm", "w_in", "ssm_lambda_re", "ssm_lambda_im",
         "ssm_log_dt", "ssm_b_re", "ssm_b_im", "ssm_c_re", "ssm_c_im", "ssm_d", "ssm_w_glu", "pool_w", "pool_scale", "ssm_out_norm", "pool_out_norm", "w_out", "ffn2_pre_norm",
         "ffn2_post_norm", "ffn2_w_gate", "ffn2_w_up", "ffn2_w_down")
PACK_ROWS = SUBLANES * 8
def _pack(arrays, rows):
    flat = jnp.concatenate([a.reshape(-1) for a in arrays])
    return jnp.pad(flat, (0, rows * LANES - flat.size)).reshape(rows, LANES)


def _unpack(packed, shapes):
    flat = packed.reshape(-1)
    out, off = [], 0
    for s in shapes:
        n = math.prod(s)
        out.append(flat[off : off + n].reshape(s))
        off += n
    return out


def _step(p, x, loss_target, m, v):
    D = x.shape[-1]
    chip = (2 * lax.axis_index("x") + lax.axis_index("y")).astype(jnp.int32)
    place = jnp.stack([chip, lax.axis_index("c").astype(jnp.int32)])

    def gather_buffer(w):
        own = w.reshape(1, 2, w.shape[0] // 2, w.shape[1])
        return lax.dynamic_update_slice(lax.empty((4,) + own.shape[1:], own.dtype), own, (chip, 0, 0, 0))

    def rows_of(n, a):
        return jnp.swapaxes(a[0], 0, 1) if n.endswith(("w_gate", "w_up")) else a[0]

    def rows_back(n, a):
        return (jnp.swapaxes(a, 0, 1) if n.endswith(("w_gate", "w_up")) else a)[None]

    def grad_view(g):
        return g.reshape(4, 2, g.shape[0] // 8, g.shape[1])

    def reduce_sum(got_sibling, views, tag):
        sums = [_add_own_half(v_, g_, place, f"{tag}_add_sibling_{k}") for k, (v_, g_) in enumerate(zip(views, got_sibling))]
        return [s[0] for s in sums], [s[1] for s in sums]

    def reduce_halves(parts, got_chips, tag):
        return [_add_chips(p_, g_, place, f"{tag}_add_chips_{k}") for k, (p_, g_) in enumerate(zip(parts, got_chips))]

    first_names = ("ffn1_w_gate", "ffn1_w_up", "ffn1_w_down")
    later_names = ("w_in", "w_out", "ffn2_w_gate", "ffn2_w_up", "ffn2_w_down")
    bufs = {n: gather_buffer(rows_of(n, p[n]).astype(MXU_DTYPE)) for n in BIG}
    gathered_weight = lambda g_: g_.reshape(-1, g_.shape[-1])
    got = _exchange_call(_Gather([bufs[n] for n in first_names] + [gather_buffer(p["meta_tokens"])]), "gather_first")
    full = {n: gathered_weight(g_) for n, g_ in zip(first_names, got)}
    meta = jnp.transpose(got[-1].reshape(4, N_META, -1), (1, 0, 2)).reshape(N_META, D)

    vec = lambda n: p[n].reshape(1, -1)
    G, N, H = p["ssm_b_re"].shape[1:]
    a_re, a_im, bb_re, bb_im = _discretize(p["ssm_lambda_re"][0], p["ssm_lambda_im"][0], p["ssm_log_dt"][0], p["ssm_b_re"][0], p["ssm_b_im"][0])
    tabs, pows = _scan_tables(a_re, a_im, MIX_TILE // SUBLANES)
    bf = lambda a: a.astype(MXU_DTYPE)
    bbre, bbim = bf(_block_diag(bb_re, False)), bf(_block_diag(bb_im, False))
    ccre, ccim = bf(_block_diag(p["ssm_c_re"][0], False)), bf(_block_diag(p["ssm_c_im"][0], False))
    wgv, wgg = bf(_block_diag(p["ssm_w_glu"][0][:, :, :H], True)), bf(_block_diag(p["ssm_w_glu"][0][:, :, H:], True))
    pw = bf(p["pool_w"][0])

    n_ffn_steps = (x.shape[1] + N_META) // FFN_TILE
    (h1, f1, ga1, si1, s1, n1, h0), got = _ffn_fwd(
        x[0], vec("ffn1_pre_norm"), vec("ffn1_post_norm"), full["ffn1_w_gate"], full["ffn1_w_up"], full["ffn1_w_down"], "ffn1_fwd",
        exchange=_Gather([bufs[n] for n in later_names], mid_step=(3 * n_ffn_steps) // 4), meta=meta,
    )
    full.update({n: gathered_weight(g_) for n, g_ in zip(later_names, got)})
    proj, xr, xim, y, pooled, mixed, h2, n2, cat = _mix_fwd(
        h1, vec("mix_pre_norm"), vec("ssm_out_norm"), vec("pool_out_norm"), vec("mix_post_norm"), vec("ssm_d"), vec("pool_scale"), full["w_in"], full["w_out"],
        bbre, bbim, ccre, ccim, wgv, wgg, pw, tabs, pows, "mix_fwd",
    )
    (_, f2, ga2, si2, s2, n3, dh3, sq), _ = _ffn_fwd(
        h2, vec("ffn2_pre_norm"), vec("ffn2_post_norm"), full["ffn2_w_gate"], full["ffn2_w_up"], full["ffn2_w_down"], "ffn2_fwd", target=loss_target[0]
    )
    loss = lax.psum(0.5 * jnp.sum(sq) / D, ("x", "y", "c"))

    g, shared = {}, {}
    ffn_names = lambda tag: (tag + "_w_gate", tag + "_w_up", tag + "_w_down")

    da, db, df, dh2, g["ffn2_post_norm"], g["ffn2_pre_norm"] = _ffn_bwd(
        dh3, f2, ga2, si2, h2, vec("ffn2_post_norm"), vec("ffn2_pre_norm"), full["ffn2_w_gate"], full["ffn2_w_up"], full["ffn2_w_down"], "ffn2_bwd"
    )
    views2 = [
        grad_view(_tn_matmul(da, n3, "ffn2_dw_gate")[0]),
        grad_view(_tn_matmul(db, n3, "ffn2_dw_up")[0]),
        grad_view(_tn_matmul(s2, df, "ffn2_dw_down")[0]),
    ]
    (dy, dpooled, dmixed, g["mix_post_norm"], g["ssm_out_norm"], g["pool_out_norm"], g["ssm_d"], g["pool_scale"], dwgv, dwgg, g["pool_w"]), got = _mix_bwd_heads(
        dh2, mixed, y, pooled, proj, vec("ssm_out_norm"), vec("pool_out_norm"), vec("mix_post_norm"), vec("pool_scale"), full["w_out"], wgv, wgg, pw, "mix_bwd_heads",
        exchange=_SiblingScatter(views2),
    )
    parts2, wire2 = reduce_sum(got, views2, "ffn2")
    (dh1, dproj, g["mix_pre_norm"], dccre, dccim, dbbre, dbbim, dar, dai), got = _mix_bwd_scan(
        dy, dpooled, xr, xim, proj, h1, dh2, vec("mix_pre_norm"), vec("ssm_d"), full["w_in"], bbre, bbim, ccre, ccim, tabs, pows, "mix_bwd_scan",
        exchange=_ChipScatter(wire2),
    )
    halves2 = reduce_halves(parts2, got, "ffn2")
    dw_in, got = _tn_matmul(n2, dproj, "dw_in", exchange=_SiblingShare(halves2))
    shared.update(zip(ffn_names("ffn2"), got))
    dw_out, _ = _tn_matmul(cat, dmixed, "dw_out")
    views_m = [grad_view(dw_in), grad_view(dw_out)]

    (da, db, df, g["ffn1_post_norm"]), got = _ffn_bwd_down(dh1, f1, ga1, si1, vec("ffn1_post_norm"), full["ffn1_w_down"], "ffn1_bwd_down", exchange=_SiblingScatter(views_m))
    parts_m, wire_m = reduce_sum(got, views_m, "mix")
    dw_down, got = _tn_matmul(s1, df, "ffn1_dw_down", exchange=_ChipScatter(wire_m))
    halves_m = reduce_halves(parts_m, got, "mix")
    views_d = [grad_view(dw_down)]
    ex = _Group([_SiblingShare(halves_m), _SiblingScatter(views_d)])
    dw_gate, got = _tn_matmul(da, n1, "ffn1_dw_gate", exchange=ex)
    got_m, got_d = ex.split(got)
    shared.update(zip(("w_in", "w_out"), got_m))
    parts_d, wire_d = reduce_sum(got_d, views_d, "ffn1_down")
    views_g = [grad_view(dw_gate)]
    ex = _Group([_ChipScatter(wire_d), _SiblingScatter(views_g)])
    dw_up, got = _tn_matmul(db, n1, "ffn1_dw_up", exchange=ex)
    got_d, got_g = ex.split(got)
    halves_d = reduce_halves(parts_d, got_d, "ffn1_down")
    parts_g, wire_g = reduce_sum(got_g, views_g, "ffn1_gate")
    views_u = [grad_view(dw_up)]
    ex = _Group([_SiblingShare(halves_d), _ChipScatter(wire_g), _SiblingScatter(views_u)])
    (dh0, g["ffn1_pre_norm"]), got = _ffn_bwd_up(da, db, h0, dh1, vec("ffn1_pre_norm"), full["ffn1_w_gate"], full["ffn1_w_up"], "ffn1_bwd_up", exchange=ex)
    got_d, got_g, got_u = ex.split(got)
    shared["ffn1_w_down"] = got_d[0]
    halves_g = reduce_halves(parts_g, got_g, "ffn1_gate")
    parts_u, wire_u = reduce_sum(got_u, views_u, "ffn1_up")
    grad_x = dh0[N_META:][None]

    g["ssm_c_re"] = _block_diag_extract(dccre, N, H, False)
    g["ssm_c_im"] = _block_diag_extract(dccim, N, H, False)
    g["ssm_w_glu"] = jnp.concatenate([_block_diag_extract(dwgv, H, H, True), _block_diag_extract(dwgg, H, H, True)], axis=-1)
    d_a_re, d_a_im = jnp.sum(dar, axis=1).reshape(G, N), jnp.sum(dai, axis=1).reshape(G, N)
    _, pull = jax.vjp(_discretize, p["ssm_lambda_re"][0], p["ssm_lambda_im"][0], p["ssm_log_dt"][0], p["ssm_b_re"][0], p["ssm_b_im"][0])
    g["ssm_lambda_re"], g["ssm_lambda_im"], g["ssm_log_dt"], g["ssm_b_re"], g["ssm_b_im"] = pull(
        (d_a_re, d_a_im, _block_diag_extract(dbbre, H, N, False), _block_diag_extract(dbbim, H, N, False))
    )

    small_shapes = [p[n].shape for n in SMALL] + [(N_META, D)]
    small_size = sum(math.prod(s) for s in small_shapes)
    rows = -(-small_size // (LANES * PACK_ROWS)) * PACK_ROWS
    views_s = [_pack([g[n] for n in SMALL] + [dh0[:N_META]], rows).reshape(4, 2, rows // 8, LANES)]
    parts_s, wire_s = reduce_sum(_exchange_call(_SiblingScatter(views_s), "small_reduce_sibling"), views_s, "small")
    ex = _Group([_ChipScatter(wire_u + wire_s), _SiblingShare(halves_g)])
    got_c, got_g = ex.split(_exchange_call(ex, "tail_reduce_chips"))
    shared["ffn1_w_gate"] = got_g[0]
    got = _exchange_call(_SiblingShare(reduce_halves(parts_u + parts_s, got_c, "tail")), "tail_reduce_share")
    shared["ffn1_w_up"] = got[0]
    small_buf = lax.dynamic_update_slice(lax.empty((4,) + got[1].shape, F32), got[1][None], (chip, 0, 0, 0))
    small_all = _exchange_call(_Gather([small_buf]), "gather_small")[0].reshape(rows, LANES)
    grads = dict(zip(SMALL + ("meta_full",), _unpack(small_all, small_shapes)))
    grads["meta_tokens"] = lax.dynamic_slice_in_dim(grads.pop("meta_full"), chip * (D // 4), D // 4, axis=1)
    delta, new_m, new_v = {}, {}, {}
    for n in BIG:
        g_rows = shared[n].reshape(-1, shared[n].shape[-1])
        outs = _adamw(rows_of(n, p[n]), g_rows, rows_of(n, m[n]), rows_of(n, v[n]), "adamw_" + n)
        grads[n], delta[n], new_m[n], new_v[n] = (rows_back(n, a) for a in (g_rows, *outs))
    delta["meta_tokens"], new_m["meta_tokens"], new_v["meta_tokens"] = _adamw(p["meta_tokens"], grads["meta_tokens"], m["meta_tokens"], v["meta_tokens"], "adamw_meta_tokens")

    def as_2d(n, a):
        a = a.reshape(p[n].shape)[0]
        if n in ("ssm_b_re", "ssm_b_im"):
            a = jnp.swapaxes(a, 1, 2)
        return a.reshape(-1, a.shape[-1])

    def from_2d(n, a):
        if n in ("ssm_b_re", "ssm_b_im"):
            g_, n_, h_ = p[n].shape[1:]
            return jnp.swapaxes(a.reshape(g_, h_, n_), 1, 2)[None]
        return a.reshape(p[n].shape)

    outs = _adamw_many(*[[as_2d(n, t[n]) for n in SMALL] for t in (p, grads, m, v)], "adamw_small")
    for out, arrays in zip((delta, new_m, new_v), outs):
        out.update({n: from_2d(n, a) for n, a in zip(SMALL, arrays)})

    return (loss, grad_x, *[grads[n] for n in ORDER], *[delta[n] for n in ORDER], *[new_m[n] for n in ORDER], *[new_v[n] for n in ORDER])


def kernel(x, meta_tokens, ffn1_pre_norm, ffn1_post_norm, ffn1_w_gate, ffn1_w_up, ffn1_w_down, mix_pre_norm, mix_post_norm, w_in, ssm_lambda_re, ssm_lambda_im, ssm_log_dt, ssm_b_re, ssm_b_im, ssm_c_re, ssm_c_im, ssm_d, ssm_w_glu, pool_w, pool_scale, ssm_out_norm, pool_out_norm, w_out, ffn2_pre_norm, ffn2_post_norm, ffn2_w_gate, ffn2_w_up, ffn2_w_down, loss_target, m_meta_tokens, m_ffn1_pre_norm, m_ffn1_post_norm, m_ffn1_w_gate, m_ffn1_w_up, m_ffn1_w_down, m_mix_pre_norm, m_mix_post_norm, m_w_in, m_ssm_lambda_re, m_ssm_lambda_im, m_ssm_log_dt, m_ssm_b_re, m_ssm_b_im, m_ssm_c_re, m_ssm_c_im, m_ssm_d, m_ssm_w_glu, m_pool_w, m_pool_scale, m_ssm_out_norm, m_pool_out_norm, m_w_out, m_ffn2_pre_norm, m_ffn2_post_norm, m_ffn2_w_gate, m_ffn2_w_up, m_ffn2_w_down, v_meta_tokens, v_ffn1_pre_norm, v_ffn1_post_norm, v_ffn1_w_gate, v_ffn1_w_up, v_ffn1_w_down, v_mix_pre_norm, v_mix_post_norm, v_w_in, v_ssm_lambda_re, v_ssm_lambda_im, v_ssm_log_dt, v_ssm_b_re, v_ssm_b_im, v_ssm_c_re, v_ssm_c_im, v_ssm_d, v_ssm_w_glu, v_pool_w, v_pool_scale, v_ssm_out_norm, v_pool_out_norm, v_w_out, v_ffn2_pre_norm, v_ffn2_post_norm, v_ffn2_w_gate, v_ffn2_w_up, v_ffn2_w_down):
    args = locals()
    p = {n: args[n] for n in ORDER}
    m = {n: args["m_" + n] for n in ORDER}
    v = {n: args["v_" + n] for n in ORDER}
    return _step(p, x, loss_target, m, v)
```

```python
import math

import jax
import jax.numpy as jnp
from jax import lax
from jax.experimental import pallas as pl
from jax.experimental.pallas import tpu as pltpu

F32 = jnp.float32
MXU_DTYPE = jnp.bfloat16
WIRE_DTYPE = jnp.bfloat16

RMS_EPS = 1e-6
N_META = 16
POOL_WINDOWS = (2, 4, 8, 16)
POOL_HALO = 16
ADAM_LR, ADAM_B1, ADAM_B2, ADAM_EPS, ADAM_WD, ADAM_STEP = 0.001, 0.9, 0.999, 1e-08, 0.01, 10

LANES = 128
SUBLANES = 8
VMEM_LIMIT = 60 * 1024 * 1024
FFN_TILE = 432
FFN_CHUNK = 1024
MIX_TILE = 216
MIX_FWD_SUBTILES = 2
SLAB_GROUP = 8
MESH = pl.DeviceIdType.MESH
ANY = pl.BlockSpec(memory_space=pl.ANY)


def _mm(a, b):
    return jnp.dot(a.astype(MXU_DTYPE), b.astype(MXU_DTYPE), preferred_element_type=F32)


def _mm_nt(a, b):
    return lax.dot_general(a.astype(MXU_DTYPE), b.astype(MXU_DTYPE), (((1,), (1,)), ((), ())), preferred_element_type=F32)


def _mm_tn(a, b):
    return lax.dot_general(a.astype(MXU_DTYPE), b.astype(MXU_DTYPE), (((0,), (0,)), ((), ())), preferred_element_type=F32)


def _rms_stat(x):
    return lax.rsqrt(jnp.mean(x * x, axis=-1, keepdims=True) + RMS_EPS)


def _rms_bwd(x, g, dy):
    r = _rms_stat(x)
    xh = x * r
    dg = jnp.sum(dy * xh, axis=0, keepdims=True)
    dxh = dy * g
    dx = r * (dxh - xh * jnp.mean(dxh * xh, axis=-1, keepdims=True))
    return dx, dg


def _sigmoid(x):
    return 1.0 / (1.0 + jnp.exp(-x))


GELU_C = math.sqrt(2.0 / math.pi)
GELU_K = 0.044715


def _gelu(y):
    return 0.5 * y * (1.0 + jnp.tanh(GELU_C * (y + GELU_K * y * y * y)))


def _gelu_grad(y):
    th = jnp.tanh(GELU_C * (y + GELU_K * y * y * y))
    return 0.5 * (1.0 + th) + 0.5 * y * (1.0 - th * th) * GELU_C * (1.0 + 3.0 * GELU_K * y * y)


def _row_spec(tile, cols, rev_n=None):
    if rev_n is None:
        return pl.BlockSpec((tile, cols), lambda i: (i, 0))
    return pl.BlockSpec((tile, cols), lambda i: (rev_n - 1 - i, 0))


def _full_spec(shape, single=False):
    zeros = (0,) * len(shape)
    if single:
        return pl.BlockSpec(shape, lambda *_: zeros, pipeline_mode=pl.Buffered(1))
    return pl.BlockSpec(shape, lambda *_: zeros)


def _acc(ref, val, first):
    @pl.when(first)
    def _():
        ref[...] = val

    @pl.when(jnp.logical_not(first))
    def _():
        ref[...] += val


def _place():
    x, y, c = lax.axis_index("x"), lax.axis_index("y"), lax.axis_index("c")
    others = [(1 - x, y), (x, 1 - y), (1 - x, 1 - y)]
    return x, y, c, others


class _Exchange:
    mid_step = None

    def __init__(self, ins, out_shapes, aliases, n_sems):
        self.ins, self.out_shapes, self.aliases, self.n_sems = list(ins), list(out_shapes), dict(aliases), n_sems

    def mid(self, ins, outs, send_sems, recv_sems):
        pass


class _SiblingScatter(_Exchange):
    def __init__(self, views):
        super().__init__(views, [jax.ShapeDtypeStruct((4,) + v.shape[2:], v.dtype) for v in views], {}, 4 * len(views))

    def _copies(self, ins, outs, send_sems, recv_sems):
        x, y, c, _ = _place()
        return [
            pltpu.make_async_remote_copy(src_ref=ins[a].at[k, 1 - c], dst_ref=outs[a].at[k], send_sem=send_sems.at[4 * a + k], recv_sem=recv_sems.at[4 * a + k], device_id=(x, y, 1 - c), device_id_type=MESH)
            for a in range(len(ins))
            for k in range(4)
        ]

    def start(self, *refs):
        for cp in self._copies(*refs):
            cp.start()

    def finish(self, *refs):
        cps = self._copies(*refs)
        for cp in cps:
            cp.wait_recv()
        for cp in cps:
            cp.wait_send()


class _ChipScatter(_Exchange):
    def __init__(self, parts):
        super().__init__(parts, [jax.ShapeDtypeStruct((3,) + p.shape[1:], p.dtype) for p in parts], {}, 3 * len(parts))

    def _copies(self, ins, outs, send_sems, recv_sems):
        x, y, c, others = _place()
        return [
            pltpu.make_async_remote_copy(src_ref=ins[a].at[2 * chip[0] + chip[1]], dst_ref=outs[a].at[j], send_sem=send_sems.at[3 * a + j], recv_sem=recv_sems.at[3 * a + j], device_id=(*chip, c), device_id_type=MESH)
            for a in range(len(ins))
            for j, chip in enumerate(others)
        ]

    start = _SiblingScatter.start
    finish = _SiblingScatter.finish


class _SiblingShare(_Exchange):
    def __init__(self, bufs):
        super().__init__(bufs, [jax.ShapeDtypeStruct(b.shape, b.dtype) for b in bufs], {a: a for a in range(len(bufs))}, len(bufs))

    def _copy(self, outs, send_sems, recv_sems, a, half):
        x, y, c, _ = _place()
        mine = outs[a].at[c if half == "mine" else 1 - c]
        return pltpu.make_async_remote_copy(src_ref=mine, dst_ref=mine, send_sem=send_sems.at[a], recv_sem=recv_sems.at[a], device_id=(x, y, 1 - c), device_id_type=MESH)

    def start(self, ins, outs, send_sems, recv_sems):
        for a in range(len(outs)):
            self._copy(outs, send_sems, recv_sems, a, "mine").start()

    def finish(self, ins, outs, send_sems, recv_sems):
        for a in range(len(outs)):
            self._copy(outs, send_sems, recv_sems, a, "theirs").wait_recv()
        for a in range(len(outs)):
            self._copy(outs, send_sems, recv_sems, a, "mine").wait_send()


class _Gather(_Exchange):
    def __init__(self, bufs, mid_step=None):
        super().__init__(bufs, [jax.ShapeDtypeStruct(b.shape, b.dtype) for b in bufs], {a: a for a in range(len(bufs))}, 6 * len(bufs))
        self.mid_step = mid_step

    def _copy(self, outs, send_sems, recv_sems, a, j, chip, half, to):
        blk = outs[a].at[2 * chip[0] + chip[1], half]
        return pltpu.make_async_remote_copy(src_ref=blk, dst_ref=blk, send_sem=send_sems.at[6 * a + j], recv_sem=recv_sems.at[6 * a + j], device_id=to, device_id_type=MESH)

    def start(self, ins, outs, send_sems, recv_sems):
        x, y, c, others = _place()
        for a in range(len(outs)):
            for j, chip in enumerate(others):
                self._copy(outs, send_sems, recv_sems, a, j, (x, y), c, (*chip, c)).start()

    def mid(self, ins, outs, send_sems, recv_sems):
        x, y, c, others = _place()
        for a in range(len(outs)):
            for j, chip in enumerate(others):
                self._copy(outs, send_sems, recv_sems, a, j, chip, c, (x, y, c)).wait_recv()
                self._copy(outs, send_sems, recv_sems, a, 3 + j, chip, c, (x, y, 1 - c)).start()

    def finish(self, ins, outs, send_sems, recv_sems):
        x, y, c, others = _place()
        for a in range(len(outs)):
            for j, chip in enumerate(others):
                self._copy(outs, send_sems, recv_sems, a, 3 + j, chip, 1 - c, (x, y, c)).wait_recv()
        for a in range(len(outs)):
            for j, chip in enumerate(others):
                self._copy(outs, send_sems, recv_sems, a, j, (x, y), c, (*chip, c)).wait_send()
                self._copy(outs, send_sems, recv_sems, a, 3 + j, chip, c, (x, y, 1 - c)).wait_send()


class _SemSlice:
    def __init__(self, sems, off):
        self.sems, self.off = sems, off

    @property
    def at(self):
        return self

    def __getitem__(self, i):
        return self.sems.at[self.off + i]


class _Group(_Exchange):
    def __init__(self, exchanges):
        ins, outs, aliases, n_sems, self.spans = [], [], {}, 0, []
        for ex in exchanges:
            self.spans.append((len(ins), len(outs), n_sems))
            aliases.update({len(ins) + i: len(outs) + o for i, o in ex.aliases.items()})
            ins, outs, n_sems = ins + ex.ins, outs + ex.out_shapes, n_sems + ex.n_sems
        super().__init__(ins, outs, aliases, n_sems)
        self.exchanges = exchanges
        mids = {ex.mid_step for ex in exchanges if ex.mid_step is not None}
        self.mid_step = mids.pop() if mids else None

    def _each(self, method, ins, outs, send_sems, recv_sems):
        for ex, (i0, o0, s0) in zip(self.exchanges, self.spans):
            getattr(ex, method)(ins[i0 : i0 + len(ex.ins)], outs[o0 : o0 + len(ex.out_shapes)], _SemSlice(send_sems, s0), _SemSlice(recv_sems, s0))

    def start(self, *refs):
        self._each("start", *refs)

    def mid(self, *refs):
        self._each("mid", *refs)

    def finish(self, *refs):
        self._each("finish", *refs)

    def split(self, outs):
        return [outs[o0 : o0 + len(ex.out_shapes)] for ex, (_, o0, _) in zip(self.exchanges, self.spans)]


def _exchange_call(ex, name):
    n, m = len(ex.ins), len(ex.out_shapes)

    def body(*refs):
        parts = (refs[:n], refs[n : n + m], refs[n + m], refs[n + m + 1])
        ex.start(*parts)
        ex.mid(*parts)
        ex.finish(*parts)

    return pl.pallas_call(
        body,
        name=name,
        out_shape=ex.out_shapes,
        in_specs=[ANY] * n,
        out_specs=[ANY] * m,
        scratch_shapes=[pltpu.SemaphoreType.DMA((ex.n_sems,)), pltpu.SemaphoreType.DMA((ex.n_sems,))],
        input_output_aliases=ex.aliases,
    )(*ex.ins)


def _pallas(body, *, name, grid, in_specs, out_specs, out_shape, operands, scratch_shapes=(), exchange=None):
    params = pltpu.CompilerParams(dimension_semantics=("arbitrary",) * len(grid), vmem_limit_bytes=VMEM_LIMIT)
    if exchange is None:
        outs = pl.pallas_call(body, name=name, grid=grid, in_specs=in_specs, out_specs=out_specs, out_shape=out_shape, scratch_shapes=list(scratch_shapes), compiler_params=params)(*operands)
        return outs, []
    ex = exchange
    n_in, n_out, n_scr = len(in_specs), len(out_specs), len(scratch_shapes)
    x_in, x_out = len(ex.ins), len(ex.out_shapes)

    def hosted(*refs):
        ins, x_ins = refs[:n_in], refs[n_in : n_in + x_in]
        outs, x_outs = refs[n_in + x_in : n_in + x_in + n_out], refs[n_in + x_in + n_out : n_in + x_in + n_out + x_out]
        rest = refs[n_in + x_in + n_out + x_out :]
        parts = (x_ins, x_outs, rest[n_scr], rest[n_scr + 1])
        ids = [pl.program_id(d) for d in range(len(grid))]
        first = _all([i == 0 for i in ids])
        last = _all([i == g - 1 for i, g in zip(ids, grid)])

        @pl.when(first)
        def _():
            ex.start(*parts)

        body(*ins, *outs, *rest[:n_scr])

        if ex.mid_step is not None:

            @pl.when(ids[0] == ex.mid_step)
            def _():
                ex.mid(*parts)

        @pl.when(last)
        def _():
            ex.finish(*parts)

    outs = pl.pallas_call(
        hosted,
        name=name,
        grid=grid,
        in_specs=list(in_specs) + [ANY] * x_in,
        out_specs=list(out_specs) + [ANY] * x_out,
        out_shape=list(out_shape) + ex.out_shapes,
        scratch_shapes=list(scratch_shapes) + [pltpu.SemaphoreType.DMA((ex.n_sems,)), pltpu.SemaphoreType.DMA((ex.n_sems,))],
        input_output_aliases={n_in + i: n_out + o for i, o in ex.aliases.items()},
        compiler_params=params,
    )(*operands, *ex.ins)
    return outs[:n_out], outs[n_out:]


def _all(conds):
    out = conds[0]
    for c in conds[1:]:
        out = jnp.logical_and(out, c)
    return out


def _row_tile(rows):
    if rows <= 512:
        return rows
    for t in (512, 352, 256, 176, 128, 112, 64, 32, 16, 8):
        if rows % t == 0:
            return t
    return rows


def _add_own_half(view, got, place, name):
    _, _, r, c = view.shape
    tr = _row_tile(r)

    def body(place_ref, v_ref, g_ref, o_ref, w_ref):
        s = v_ref[...] + g_ref[...]
        o_ref[...] = s
        w_ref[...] = s.astype(w_ref.dtype)

    blk = pl.BlockSpec((None, tr, c), lambda k, i, pr: (k, i, 0))
    return pl.pallas_call(
        body,
        name=name,
        out_shape=[jax.ShapeDtypeStruct((4, r, c), F32), jax.ShapeDtypeStruct((4, r, c), WIRE_DTYPE)],
        grid_spec=pltpu.PrefetchScalarGridSpec(
            num_scalar_prefetch=1, grid=(4, r // tr), in_specs=[pl.BlockSpec((None, None, tr, c), lambda k, i, pr: (k, pr[1], i, 0)), blk], out_specs=[blk, blk]
        ),
        compiler_params=pltpu.CompilerParams(dimension_semantics=("arbitrary", "arbitrary"), vmem_limit_bytes=VMEM_LIMIT),
    )(place, view, got)


def _add_chips(part, got, place, name):
    _, r, c = part.shape
    tr = _row_tile(r)

    def body(place_ref, p_ref, g_ref, o_ref):
        o_ref[...] = ((p_ref[...] + g_ref[0].astype(F32)) + g_ref[1].astype(F32)) + g_ref[2].astype(F32)

    return pl.pallas_call(
        body,
        name=name,
        out_shape=jax.ShapeDtypeStruct((2, r, c), F32),
        grid_spec=pltpu.PrefetchScalarGridSpec(
            num_scalar_prefetch=1,
            grid=(r // tr,),
            in_specs=[pl.BlockSpec((None, tr, c), lambda i, pr: (pr[0], i, 0)), pl.BlockSpec((3, tr, c), lambda i, pr: (0, i, 0))],
            out_specs=pl.BlockSpec((None, tr, c), lambda i, pr: (pr[1], i, 0)),
        ),
        compiler_params=pltpu.CompilerParams(dimension_semantics=("arbitrary",), vmem_limit_bytes=VMEM_LIMIT),
    )(place, part, got)


def _adamw_update(w_ref, g_ref, m_ref, v_ref, d_ref, nm_ref, nv_ref):
    g = g_ref[...]
    nm = ADAM_B1 * m_ref[...] + (1.0 - ADAM_B1) * g
    nv = ADAM_B2 * v_ref[...] + (1.0 - ADAM_B2) * (g * g)
    m_hat = nm / (1.0 - ADAM_B1**ADAM_STEP)
    v_hat = nv / (1.0 - ADAM_B2**ADAM_STEP)
    d_ref[...] = -ADAM_LR * (m_hat / (jnp.sqrt(v_hat) + ADAM_EPS) + ADAM_WD * w_ref[...])
    nm_ref[...] = nm
    nv_ref[...] = nv


def _adamw(w, g, m, v, name):
    r, c = w.shape
    tr = _row_tile(r)
    spec = pl.BlockSpec((tr, c), lambda i: (i, 0))
    outs, _ = _pallas(_adamw_update, name=name, grid=(r // tr,), in_specs=[spec] * 4, out_specs=[spec] * 3, out_shape=[jax.ShapeDtypeStruct((r, c), F32)] * 3, operands=(w, g, m, v))
    return outs


def _adamw_many(ws, gs, ms, vs, name):
    n = len(ws)

    def body(*refs):
        for k in range(n):
            _adamw_update(*(refs[j * n + k] for j in range(7)))

    outs = pl.pallas_call(
        body,
        name=name,
        out_shape=[jax.ShapeDtypeStruct(w.shape, F32) for w in ws] * 3,
        in_specs=[pl.BlockSpec(memory_space=pltpu.VMEM)] * (4 * n),
        out_specs=[pl.BlockSpec(memory_space=pltpu.VMEM)] * (3 * n),
    )(*ws, *gs, *ms, *vs)
    return outs[:n], outs[n : 2 * n], outs[2 * n :]


def _load_weights(pairs, sems):
    @pl.when(pl.program_id(0) == 0)
    def _():
        cps = [pltpu.make_async_copy(src, dst, sems.at[k]) for k, (src, dst) in enumerate(pairs)]
        for cp in cps:
            cp.start()
        for cp in cps:
            cp.wait()


def _ffn_chunks(F):
    bounds = list(range(0, F, FFN_CHUNK)) + [F]
    return list(zip(bounds[:-1], bounds[1:]))


def _shifted_specs(tm, cols):
    per = tm // N_META
    return [_row_spec(tm, cols), pl.BlockSpec((N_META, cols), lambda i: (jnp.maximum(i * per - 1, 0), 0))]


def _shifted_tile(cur_ref, before_ref, tm):
    return jnp.concatenate([before_ref[...], cur_ref[0 : tm - N_META, :]], axis=0)


def _ffn_fwd(h, g_pre, g_post, wg, wu, wd, name, exchange=None, meta=None, target=None):
    D = h.shape[1]
    T = h.shape[0] + (0 if meta is None else N_META)
    F = wg.shape[0]
    tm = FFN_TILE
    n_src = 1 if meta is None else 3
    n_tgt = 0 if target is None else 2

    def body(*refs):
        src, refs = refs[:n_src], refs[n_src:]
        tgt, refs = refs[:n_tgt], refs[n_tgt:]
        gpre_ref, gpost_ref, wg_hbm, wu_hbm, wd_hbm, hout_ref, f_ref, ga_ref, si_ref, s_ref, n_ref = refs[:11]
        extra, (wg_v, wu_v, wd_v, sems) = refs[11:-4], refs[-4:]
        i = pl.program_id(0)
        _load_weights([(wg_hbm, wg_v), (wu_hbm, wu_v), (wd_hbm, wd_v)], sems)
        if meta is None:
            hh = src[0][...]
        else:
            hh = jnp.concatenate([jnp.where(i == 0, src[2][...], src[1][...]), src[0][0 : tm - N_META, :]], axis=0)
            extra[0][...] = hh
        n = (hh * _rms_stat(hh) * gpre_ref[...]).astype(MXU_DTYPE)
        n_ref[...] = n.astype(n_ref.dtype)
        f = jnp.zeros((tm, D), F32)
        for lo, hi in _ffn_chunks(F):
            a = _mm_nt(n, wg_v[lo:hi, :])
            b = _mm_nt(n, wu_v[lo:hi, :])
            sg = _sigmoid(a)
            si = a * sg
            s = (si * b).astype(MXU_DTYPE)
            ga_ref[:, lo:hi] = (b * (sg * (1.0 + a * (1.0 - sg)))).astype(ga_ref.dtype)
            si_ref[:, lo:hi] = si.astype(si_ref.dtype)
            s_ref[:, lo:hi] = s.astype(s_ref.dtype)
            f = f + _mm(s, wd_v[lo:hi, :])
        f_ref[...] = f
        out = hh + 0.5 * (f * _rms_stat(f) * gpost_ref[...])
        hout_ref[...] = out
        if target is not None:
            rows = i * tm + lax.broadcasted_iota(jnp.int32, (tm, D), 0)
            err = jnp.where(rows >= N_META, out - _shifted_tile(tgt[0], tgt[1], tm), 0.0)
            extra[-2][...] = err * (1.0 / D)
            _acc(extra[-1], jnp.sum(err * err, axis=0, keepdims=True), i == 0)

    tok = jax.ShapeDtypeStruct((T, D), F32)
    act = jax.ShapeDtypeStruct((T, F), MXU_DTYPE)
    src_specs = [_row_spec(tm, D)] if meta is None else _shifted_specs(tm, D) + [_full_spec((N_META, D))]
    src_ops = (h,) if meta is None else (h, h, meta)
    tgt_specs, tgt_ops = ([], ()) if target is None else (_shifted_specs(tm, D), (target, target))
    extra_shapes = ([] if meta is None else [tok]) + ([] if target is None else [tok, jax.ShapeDtypeStruct((1, D), F32)])
    extra_specs = ([] if meta is None else [_row_spec(tm, D)]) + ([] if target is None else [_row_spec(tm, D), _full_spec((1, D))])
    return _pallas(
        body,
        name=name,
        grid=(T // tm,),
        out_shape=[tok, tok, act, act, act, jax.ShapeDtypeStruct((T, D), MXU_DTYPE)] + extra_shapes,
        in_specs=src_specs + tgt_specs + [_full_spec((1, D)), _full_spec((1, D)), ANY, ANY, ANY],
        out_specs=[_row_spec(tm, D), _row_spec(tm, D), _row_spec(tm, F), _row_spec(tm, F), _row_spec(tm, F), _row_spec(tm, D)] + extra_specs,
        scratch_shapes=[pltpu.VMEM(wg.shape, wg.dtype), pltpu.VMEM(wu.shape, wu.dtype), pltpu.VMEM(wd.shape, wd.dtype), pltpu.SemaphoreType.DMA((3,))],
        operands=(*src_ops, *tgt_ops, g_pre, g_post, wg, wu, wd),
        exchange=exchange,
    )


def _ffn_bwd_down(dh, f, ga, si, g_post, wd, name, exchange=None):
    T, D = dh.shape
    F = wd.shape[0]
    tm = FFN_TILE

    def body(dh_ref, f_ref, ga_ref, si_ref, gpost_ref, wd_hbm, da_ref, db_ref, df_ref, dg_ref, wd_v, sems):
        _load_weights([(wd_hbm, wd_v)], sems)
        df, dg = _rms_bwd(f_ref[...], gpost_ref[...], 0.5 * dh_ref[...])
        _acc(dg_ref, dg, pl.program_id(0) == 0)
        dfb = df.astype(MXU_DTYPE)
        df_ref[...] = dfb.astype(df_ref.dtype)
        for lo, hi in _ffn_chunks(F):
            ds = _mm_nt(dfb, wd_v[lo:hi, :])
            da_ref[:, lo:hi] = (ds * ga_ref[:, lo:hi].astype(F32)).astype(da_ref.dtype)
            db_ref[:, lo:hi] = (ds * si_ref[:, lo:hi].astype(F32)).astype(db_ref.dtype)

    act = jax.ShapeDtypeStruct((T, F), MXU_DTYPE)
    return _pallas(
        body,
        name=name,
        grid=(T // tm,),
        out_shape=[act, act, jax.ShapeDtypeStruct((T, D), MXU_DTYPE), jax.ShapeDtypeStruct((1, D), F32)],
        in_specs=[_row_spec(tm, D), _row_spec(tm, D), _row_spec(tm, F), _row_spec(tm, F), _full_spec((1, D)), ANY],
        out_specs=[_row_spec(tm, F), _row_spec(tm, F), _row_spec(tm, D), _full_spec((1, D))],
        scratch_shapes=[pltpu.VMEM(wd.shape, wd.dtype), pltpu.SemaphoreType.DMA((1,))],
        operands=(dh, f, ga, si, g_post, wd),
        exchange=exchange,
    )


def _ffn_bwd_up(da, db, h, dh, g_pre, wg, wu, name, exchange=None):
    T, D = h.shape
    F = wg.shape[0]
    tm = FFN_TILE

    def body(da_ref, db_ref, h_ref, dh_ref, gpre_ref, wg_hbm, wu_hbm, dhin_ref, dg_ref, wg_v, wu_v, sems):
        _load_weights([(wg_hbm, wg_v), (wu_hbm, wu_v)], sems)
        dn = jnp.zeros((tm, D), F32)
        for lo, hi in _ffn_chunks(F):
            dn = dn + _mm(da_ref[:, lo:hi], wg_v[lo:hi, :]) + _mm(db_ref[:, lo:hi], wu_v[lo:hi, :])
        dx, dg = _rms_bwd(h_ref[...], gpre_ref[...], dn)
        _acc(dg_ref, dg, pl.program_id(0) == 0)
        dhin_ref[...] = dh_ref[...] + dx

    return _pallas(
        body,
        name=name,
        grid=(T // tm,),
        out_shape=[jax.ShapeDtypeStruct((T, D), F32), jax.ShapeDtypeStruct((1, D), F32)],
        in_specs=[_row_spec(tm, F), _row_spec(tm, F), _row_spec(tm, D), _row_spec(tm, D), _full_spec((1, D)), ANY, ANY],
        out_specs=[_row_spec(tm, D), _full_spec((1, D))],
        scratch_shapes=[pltpu.VMEM(wg.shape, wg.dtype), pltpu.VMEM(wu.shape, wu.dtype), pltpu.SemaphoreType.DMA((2,))],
        operands=(da, db, h, dh, g_pre, wg, wu),
        exchange=exchange,
    )


def _ffn_bwd(dh, f, ga, si, h, g_post, g_pre, wg, wu, wd, name):
    T, D = dh.shape
    F = wd.shape[0]
    tm = FFN_TILE

    def body(dh_ref, f_ref, ga_ref, si_ref, h_ref, gpost_ref, gpre_ref, wg_hbm, wu_hbm, wd_hbm, da_ref, db_ref, df_ref, dhin_ref, dgpost_ref, dgpre_ref, wg_v, wu_v, wd_v, sems):
        first = pl.program_id(0) == 0
        _load_weights([(wg_hbm, wg_v), (wu_hbm, wu_v), (wd_hbm, wd_v)], sems)
        dh = dh_ref[...]
        df, dg = _rms_bwd(f_ref[...], gpost_ref[...], 0.5 * dh)
        _acc(dgpost_ref, dg, first)
        dfb = df.astype(MXU_DTYPE)
        df_ref[...] = dfb.astype(df_ref.dtype)
        dn = jnp.zeros((tm, D), F32)
        for lo, hi in _ffn_chunks(F):
            ds = _mm_nt(dfb, wd_v[lo:hi, :])
            da = (ds * ga_ref[:, lo:hi].astype(F32)).astype(MXU_DTYPE)
            db = (ds * si_ref[:, lo:hi].astype(F32)).astype(MXU_DTYPE)
            da_ref[:, lo:hi] = da.astype(da_ref.dtype)
            db_ref[:, lo:hi] = db.astype(db_ref.dtype)
            dn = dn + _mm(da, wg_v[lo:hi, :]) + _mm(db, wu_v[lo:hi, :])
        dx, dg = _rms_bwd(h_ref[...], gpre_ref[...], dn)
        _acc(dgpre_ref, dg, first)
        dhin_ref[...] = dh + dx

    act = jax.ShapeDtypeStruct((T, F), MXU_DTYPE)
    vec = jax.ShapeDtypeStruct((1, D), F32)
    outs, _ = _pallas(
        body,
        name=name,
        grid=(T // tm,),
        out_shape=[act, act, jax.ShapeDtypeStruct((T, D), MXU_DTYPE), jax.ShapeDtypeStruct((T, D), F32), vec, vec],
        in_specs=[_row_spec(tm, D), _row_spec(tm, D), _row_spec(tm, F), _row_spec(tm, F), _row_spec(tm, D), _full_spec((1, D)), _full_spec((1, D)), ANY, ANY, ANY],
        out_specs=[_row_spec(tm, F), _row_spec(tm, F), _row_spec(tm, D), _row_spec(tm, D), _full_spec((1, D)), _full_spec((1, D))],
        scratch_shapes=[pltpu.VMEM(wg.shape, wg.dtype), pltpu.VMEM(wu.shape, wu.dtype), pltpu.VMEM(wd.shape, wd.dtype), pltpu.SemaphoreType.DMA((3,))],
        operands=(dh, f, ga, si, h, g_post, g_pre, wg, wu, wd),
    )
    return outs


def _token_tile(T):
    for t in (912, 864, 432):
        if T % t == 0:
            return t
    raise ValueError(f"no token tile for {T} rows")


def _tn_matmul(xm, ym, name, exchange=None):
    T, M = xm.shape
    N = ym.shape[1]
    tk = _token_tile(T)

    def body(x_ref, y_ref, o_ref):
        _acc(o_ref, _mm_tn(x_ref[...], y_ref[...]), pl.program_id(0) == 0)

    (out,), x_outs = _pallas(
        body,
        name=name,
        grid=(T // tk,),
        out_shape=[jax.ShapeDtypeStruct((M, N), F32)],
        in_specs=[pl.BlockSpec((tk, M), lambda k: (k, 0)), pl.BlockSpec((tk, N), lambda k: (k, 0))],
        out_specs=[_full_spec((M, N))],
        operands=(xm, ym),
        exchange=exchange,
    )
    return out, x_outs


TAB_A, TAB_AS1, TAB_AS2, TAB_AS4, TAB_JF, TAB_JB = 0, 2, 4, 6, 8, 10


def _scan_inplace(zr, zi, tabs, pows, car_r, car_i, seg, reverse, base=0):
    n_slabs = zr.shape[0]
    sgn = -1.0 if reverse else 1.0
    row = lax.broadcasted_iota(jnp.int32, (SUBLANES, LANES), 0)

    def cmul(pr, pi, xr, xi):
        return pr * xr - pi * xi, pr * xi + pi * xr

    for k0 in range(0, n_slabs, SLAB_GROUP):
        slabs = range(k0, min(k0 + SLAB_GROUP, n_slabs))
        ar = [tabs[TAB_A, k] for k in slabs]
        ai = [sgn * tabs[TAB_A + 1, k] for k in slabs]

        def first_pass(t, carry):
            r = (seg - 1 - t) if reverse else t
            out = []
            for q, k in enumerate(slabs):
                xr, xi = carry[2 * q], carry[2 * q + 1]
                pr, pi = cmul(ar[q], ai[q], xr, xi)
                nr = pr + zr[k, pl.ds(base + r, SUBLANES, stride=seg), :]
                ni = pi + zi[k, pl.ds(base + r, SUBLANES, stride=seg), :]
                zr[k, pl.ds(base + r, SUBLANES, stride=seg), :] = nr
                zi[k, pl.ds(base + r, SUBLANES, stride=seg), :] = ni
                out += [nr, ni]
            return tuple(out)

        ends = lax.fori_loop(0, seg, first_pass, tuple(jnp.zeros((SUBLANES, LANES), F32) for _ in range(2 * len(slabs))))

        incoming = []
        for q, k in enumerate(slabs):
            fr, fi = ends[2 * q], ends[2 * q + 1]
            for d, tab in ((1, TAB_AS1), (2, TAB_AS2), (4, TAB_AS4)):
                shift, keep = (SUBLANES - d, row < SUBLANES - d) if reverse else (d, row >= d)
                sr = jnp.where(keep, pltpu.roll(fr, shift, 0), 0.0)
                si = jnp.where(keep, pltpu.roll(fi, shift, 0), 0.0)
                pr, pi = cmul(tabs[tab, k], sgn * tabs[tab + 1, k], sr, si)
                fr, fi = fr + pr, fi + pi
            cr, ci = car_r[k], car_i[k]
            jtab = TAB_JB if reverse else TAB_JF
            pr, pi = cmul(tabs[jtab, k], sgn * tabs[jtab + 1, k], cr, ci)
            er, ei = fr + pr, fi + pi
            if reverse:
                inr = jnp.where(row < SUBLANES - 1, pltpu.roll(er, SUBLANES - 1, 0), cr)
                ini = jnp.where(row < SUBLANES - 1, pltpu.roll(ei, SUBLANES - 1, 0), ci)
                car_r[k] = jnp.broadcast_to(er[0:1, :], (SUBLANES, LANES))
                car_i[k] = jnp.broadcast_to(ei[0:1, :], (SUBLANES, LANES))
            else:
                inr = jnp.where(row >= 1, pltpu.roll(er, 1, 0), cr)
                ini = jnp.where(row >= 1, pltpu.roll(ei, 1, 0), ci)
                car_r[k] = jnp.broadcast_to(er[SUBLANES - 1 : SUBLANES, :], (SUBLANES, LANES))
                car_i[k] = jnp.broadcast_to(ei[SUBLANES - 1 : SUBLANES, :], (SUBLANES, LANES))
            incoming += [inr, ini]

        def second_pass(r, _):
            p = (seg - 1 - r) if reverse else r
            for q, k in enumerate(slabs):
                pr, pi = cmul(pows[0, k, p], sgn * pows[1, k, p], incoming[2 * q], incoming[2 * q + 1])
                zr[k, pl.ds(base + r, SUBLANES, stride=seg), :] = zr[k, pl.ds(base + r, SUBLANES, stride=seg), :] + pr
                zi[k, pl.ds(base + r, SUBLANES, stride=seg), :] = zi[k, pl.ds(base + r, SUBLANES, stride=seg), :] + pi
            return 0

        lax.fori_loop(0, seg, second_pass, 0)


def _slabs_to_cols(ref, k0, n):
    return jnp.concatenate([ref[k0 + q] for q in range(n)], axis=1)


def _window_sum(ext, doublings, forward):
    rows = ext.shape[0]
    s = ext
    for k in range(doublings):
        s = s + pltpu.roll(s, (1 << k) if forward else rows - (1 << k), 0)
    return s


def _mix_fwd(h1, g_pre, g_so, g_po, g_post, dskip, pscale, win, wout, bbre, bbim, ccre, ccim, wgv, wgg, pw, tabs, pows, name):
    T, D = h1.shape
    W = D // 2
    tm = MIX_FWD_SUBTILES * MIX_TILE
    seg = MIX_TILE // SUBLANES
    n_slabs = tabs.shape[1]
    nch, cch, sch = bbre.shape
    spc = sch // LANES
    pg = W // len(POOL_WINDOWS)

    def body(h_ref, gpre_ref, gso_ref, gpo_ref, gpost_ref, dskip_ref, pscale_ref, win_ref, wout_ref, bbre_ref, bbim_ref, ccre_ref, ccim_ref, wgv_ref, wgg_ref, pw_ref, tabs_ref, pows_ref,
             proj_ref, xr_ref, xi_ref, y_ref, pooled_ref, mixed_ref, h2_ref, n2_ref, cat_ref, car_r, car_i, halo):
        i = pl.program_id(0)

        @pl.when(i == 0)
        def _():
            car_r[...] = jnp.zeros_like(car_r)
            car_i[...] = jnp.zeros_like(car_i)
            halo[...] = jnp.zeros_like(halo)

        hh = h_ref[...]
        n2 = (hh * _rms_stat(hh) * gpre_ref[...]).astype(MXU_DTYPE)
        n2_ref[...] = n2.astype(n2_ref.dtype)
        proj = _mm(n2, win_ref[...])
        proj_ref[...] = proj
        us, up = proj[:, :W], proj[:, W:]

        for c in range(nch):
            uc = us[:, c * cch : (c + 1) * cch].astype(MXU_DTYPE)
            bur, bui = _mm(uc, bbre_ref[c]), _mm(uc, bbim_ref[c])
            for q in range(spc):
                xr_ref[c * spc + q] = bur[:, q * LANES : (q + 1) * LANES]
                xi_ref[c * spc + q] = bui[:, q * LANES : (q + 1) * LANES]
        for sub in range(MIX_FWD_SUBTILES):
            _scan_inplace(xr_ref, xi_ref, tabs_ref, pows_ref, car_r, car_i, seg, reverse=False, base=sub * MIX_TILE)
        ys = []
        for c in range(nch):
            ys.append(_mm(_slabs_to_cols(xr_ref, c * spc, spc), ccre_ref[c]) - _mm(_slabs_to_cols(xi_ref, c * spc, spc), ccim_ref[c]))
        y = jnp.concatenate(ys, axis=1) + dskip_ref[...] * us
        y_ref[...] = y
        ge = _gelu(y).astype(MXU_DTYPE)
        zv = jnp.concatenate([_mm(ge[:, c * cch : (c + 1) * cch], wgv_ref[c]) for c in range(nch)], axis=1)
        zg = jnp.concatenate([_mm(ge[:, c * cch : (c + 1) * cch], wgg_ref[c]) for c in range(nch)], axis=1)
        out = zv * _sigmoid(zg)
        cat_s = out * _rms_stat(out) * gso_ref[...]

        ext = jnp.concatenate([halo[...], up], axis=0)
        halo[...] = up[tm - POOL_HALO :, :]
        t1 = (i * tm + 1 + lax.broadcasted_iota(jnp.int32, (tm, pg), 0)).astype(F32)
        pooled, pms = [], []
        for g, w in enumerate(POOL_WINDOWS):
            col = ext[:, g * pg : (g + 1) * pg]
            win_sum = _window_sum(col, g + 1, True)[POOL_HALO:, :]
            pooled_g = win_sum / jnp.minimum(t1, float(w)) - up[:, g * pg : (g + 1) * pg]
            pooled.append(pooled_g)
            pms.append(_mm(pooled_g, pw_ref[g]))
        pooled_ref[...] = jnp.concatenate(pooled, axis=1)
        yp = jnp.concatenate(pms, axis=1) * pscale_ref[...]
        cat_p = yp * _rms_stat(yp) * gpo_ref[...]

        cat = jnp.concatenate([cat_s, cat_p], axis=1).astype(MXU_DTYPE)
        cat_ref[...] = cat.astype(cat_ref.dtype)
        mixed = _mm(cat, wout_ref[...])
        mixed_ref[...] = mixed
        h2_ref[...] = hh + mixed * _rms_stat(mixed) * gpost_ref[...]

    tok = lambda cols, dt=F32: jax.ShapeDtypeStruct((T, cols), dt)
    slab_spec = pl.BlockSpec((n_slabs, tm, LANES), lambda i: (0, i, 0))
    operands = (h1, g_pre, g_so, g_po, g_post, dskip, pscale, win, wout, bbre, bbim, ccre, ccim, wgv, wgg, pw, tabs, pows)
    outs, _ = _pallas(
        body,
        name=name,
        grid=(T // tm,),
        out_shape=[tok(D), jax.ShapeDtypeStruct((n_slabs, T, LANES), F32), jax.ShapeDtypeStruct((n_slabs, T, LANES), F32), tok(W), tok(W), tok(D), tok(D), tok(D, MXU_DTYPE), tok(D, MXU_DTYPE)],
        in_specs=[_row_spec(tm, D)] + [_full_spec(o.shape, single=True) for o in operands[1:]],
        out_specs=[_row_spec(tm, D), slab_spec, slab_spec, _row_spec(tm, W), _row_spec(tm, W), _row_spec(tm, D), _row_spec(tm, D), _row_spec(tm, D), _row_spec(tm, D)],
        scratch_shapes=[pltpu.VMEM((n_slabs, SUBLANES, LANES), F32), pltpu.VMEM((n_slabs, SUBLANES, LANES), F32), pltpu.VMEM((POOL_HALO, W), F32)],
        operands=operands,
    )
    return outs


def _mix_bwd_heads(dh2, mixed, y, pooled, proj, g_so, g_po, g_post, pscale, wout, wgv, wgg, pw, name, exchange=None):
    T, D = dh2.shape
    W = D // 2
    tm = FFN_TILE
    nch, cch, _ = wgv.shape
    ng, pg, _ = pw.shape

    def body(dh2_ref, mixed_ref, y_ref, pooled_ref, us_ref, gso_ref, gpo_ref, gpost_ref, pscale_ref, wout_ref, wgv_ref, wgg_ref, pw_ref,
             dy_ref, dpooled_ref, dmixed_ref, dgpost_ref, dgso_ref, dgpo_ref, dd_ref, dscale_ref, dwgv_ref, dwgg_ref, dpw_ref):
        first = pl.program_id(0) == 0
        dmixed, dgpost = _rms_bwd(mixed_ref[...], gpost_ref[...], dh2_ref[...])
        _acc(dgpost_ref, dgpost, first)
        dmb = dmixed.astype(MXU_DTYPE)
        dmixed_ref[...] = dmb.astype(dmixed_ref.dtype)
        dcat = _mm_nt(dmb, wout_ref[...])
        dcs, dcp = dcat[:, :W], dcat[:, W:]

        y = y_ref[...]
        ge = _gelu(y).astype(MXU_DTYPE)
        zv = jnp.concatenate([_mm(ge[:, c * cch : (c + 1) * cch], wgv_ref[c]) for c in range(nch)], axis=1)
        zg = jnp.concatenate([_mm(ge[:, c * cch : (c + 1) * cch], wgg_ref[c]) for c in range(nch)], axis=1)
        sg = _sigmoid(zg)
        dout, dgso = _rms_bwd(zv * sg, gso_ref[...], dcs)
        _acc(dgso_ref, dgso, first)
        dzv = (dout * sg).astype(MXU_DTYPE)
        dzg = (dout * zv * sg * (1.0 - sg)).astype(MXU_DTYPE)
        dges = []
        for c in range(nch):
            cs = slice(c * cch, (c + 1) * cch)
            dges.append(_mm_nt(dzv[:, cs], wgv_ref[c]) + _mm_nt(dzg[:, cs], wgg_ref[c]))
            _acc(dwgv_ref.at[c], _mm_tn(ge[:, cs], dzv[:, cs]), first)
            _acc(dwgg_ref.at[c], _mm_tn(ge[:, cs], dzg[:, cs]), first)
        dy = jnp.concatenate(dges, axis=1) * _gelu_grad(y)
        dy_ref[...] = dy
        _acc(dd_ref, jnp.sum(dy * us_ref[...], axis=0, keepdims=True), first)

        pooled_b = pooled_ref[...].astype(MXU_DTYPE)
        pm = jnp.concatenate([_mm(pooled_b[:, g * pg : (g + 1) * pg], pw_ref[g]) for g in range(ng)], axis=1)
        dyp, dgpo = _rms_bwd(pm * pscale_ref[...], gpo_ref[...], dcp)
        _acc(dgpo_ref, dgpo, first)
        _acc(dscale_ref, jnp.sum(dyp * pm, axis=0, keepdims=True), first)
        dpm = (dyp * pscale_ref[...]).astype(MXU_DTYPE)
        dps = []
        for g in range(ng):
            gs = slice(g * pg, (g + 1) * pg)
            dps.append(_mm_nt(dpm[:, gs], pw_ref[g]))
            _acc(dpw_ref.at[g], _mm_tn(pooled_b[:, gs], dpm[:, gs]), first)
        dpooled_ref[...] = jnp.concatenate(dps, axis=1)

    vec = lambda n: jax.ShapeDtypeStruct((1, n), F32)
    operands = (dh2, mixed, y, pooled, proj, g_so, g_po, g_post, pscale, wout, wgv, wgg, pw)
    return _pallas(
        body,
        name=name,
        grid=(T // tm,),
        out_shape=[jax.ShapeDtypeStruct((T, W), F32), jax.ShapeDtypeStruct((T, W), F32), jax.ShapeDtypeStruct((T, D), MXU_DTYPE), vec(D), vec(W), vec(W), vec(W), vec(W),
                   jax.ShapeDtypeStruct(wgv.shape, F32), jax.ShapeDtypeStruct(wgg.shape, F32), jax.ShapeDtypeStruct(pw.shape, F32)],
        in_specs=[_row_spec(tm, D), _row_spec(tm, D), _row_spec(tm, W), _row_spec(tm, W), _row_spec(tm, W)] + [_full_spec(o.shape) for o in operands[5:]],
        out_specs=[_row_spec(tm, W), _row_spec(tm, W), _row_spec(tm, D), _full_spec((1, D)), _full_spec((1, W)), _full_spec((1, W)), _full_spec((1, W)), _full_spec((1, W)),
                   _full_spec(wgv.shape), _full_spec(wgg.shape), _full_spec(pw.shape)],
        operands=operands,
        exchange=exchange,
    )


def _mix_bwd_scan(dy, dpooled, xr, xi, proj, h1, dh2, g_pre, dskip, win, bbre, bbim, ccre, ccim, tabs, pows, name, exchange=None):
    T, D = h1.shape
    W = D // 2
    tm = MIX_TILE
    seg = tm // SUBLANES
    nt = T // tm
    n_slabs = tabs.shape[1]
    nch, cch, sch = bbre.shape
    spc = sch // LANES
    pg = W // len(POOL_WINDOWS)
    blocks_per_tile = tm // SUBLANES
    assert nt % 2 == 0, "the weight gradients are accumulated over pairs of tiles"

    def body(dy_ref, dp_ref, xr_ref, xi_ref, xpr_ref, xpi_ref, proj_ref, h_ref, dh2_ref, gpre_ref, dskip_ref, win_ref, bbre_ref, bbim_ref, ccre_ref, ccim_ref, tabs_ref, pows_ref,
             dh1_ref, dproj_ref, dgpre_ref, dccre_ref, dccim_ref, dbbre_ref, dbbim_ref, dar_ref, dai_ref, lr, li, car_r, car_i, halo, kept):
        i = pl.program_id(0)
        first = i == 0
        tile = nt - 1 - i
        row = lax.broadcasted_iota(jnp.int32, (SUBLANES, LANES), 0)
        second = (i % 2) == 1

        def pair_products(slot, operands, pairs):
            @pl.when(jnp.logical_not(second))
            def _():
                for j, a in enumerate(operands):
                    kept[slot + j, :, 0 : a.shape[1]] = a

            @pl.when(second)
            def _():
                both = [jnp.concatenate([kept[slot + j, :, 0 : a.shape[1]], a], axis=0).astype(MXU_DTYPE) for j, a in enumerate(operands)]
                for out_ref, left, right, negate in pairs:
                    prod = _mm_tn(both[left], both[right])
                    _acc(out_ref, -prod if negate else prod, i == 1)

        @pl.when(first)
        def _():
            car_r[...] = jnp.zeros_like(car_r)
            car_i[...] = jnp.zeros_like(car_i)
            halo[...] = jnp.zeros_like(halo)
            dar_ref[...] = jnp.zeros_like(dar_ref)
            dai_ref[...] = jnp.zeros_like(dai_ref)

        dy = dy_ref[...]
        for c in range(nch):
            dyc = dy[:, c * cch : (c + 1) * cch]
            gr, gi = _mm_nt(dyc, ccre_ref[c]), _mm_nt(dyc, ccim_ref[c])
            for q in range(spc):
                lr[c * spc + q] = gr[:, q * LANES : (q + 1) * LANES]
                li[c * spc + q] = -gi[:, q * LANES : (q + 1) * LANES]
            pair_products(3 * c, [_slabs_to_cols(xr_ref, c * spc, spc), _slabs_to_cols(xi_ref, c * spc, spc), dyc], [(dccre_ref.at[c], 0, 2, False), (dccim_ref.at[c], 1, 2, True)])
        _scan_inplace(lr, li, tabs_ref, pows_ref, car_r, car_i, seg, reverse=True)

        for k0 in range(0, n_slabs, SLAB_GROUP):
            slabs = range(k0, min(k0 + SLAB_GROUP, n_slabs))
            init = []
            for k in slabs:
                prev_r = jnp.where(tile > 0, jnp.broadcast_to(xpr_ref[k, SUBLANES - 1 : SUBLANES, :], (SUBLANES, LANES)), 0.0)
                prev_i = jnp.where(tile > 0, jnp.broadcast_to(xpi_ref[k, SUBLANES - 1 : SUBLANES, :], (SUBLANES, LANES)), 0.0)
                x0r = jnp.where(row >= 1, pltpu.roll(xr_ref[k, pl.ds(seg - 1, SUBLANES, stride=seg), :], 1, 0), prev_r)
                x0i = jnp.where(row >= 1, pltpu.roll(xi_ref[k, pl.ds(seg - 1, SUBLANES, stride=seg), :], 1, 0), prev_i)
                l0r, l0i = lr[k, pl.ds(0, SUBLANES, stride=seg), :], li[k, pl.ds(0, SUBLANES, stride=seg), :]
                init += [l0r * x0r + l0i * x0i, l0i * x0r - l0r * x0i]

            def step(r, acc, slabs=slabs):
                out = []
                for q, k in enumerate(slabs):
                    pr_, pi_ = xr_ref[k, pl.ds(r - 1, SUBLANES, stride=seg), :], xi_ref[k, pl.ds(r - 1, SUBLANES, stride=seg), :]
                    lr_, li_ = lr[k, pl.ds(r, SUBLANES, stride=seg), :], li[k, pl.ds(r, SUBLANES, stride=seg), :]
                    out += [acc[2 * q] + lr_ * pr_ + li_ * pi_, acc[2 * q + 1] + li_ * pr_ - lr_ * pi_]
                return tuple(out)

            sums = lax.fori_loop(1, seg, step, tuple(init))
            for q, k in enumerate(slabs):
                dar_ref[k] += sums[2 * q]
                dai_ref[k] += sums[2 * q + 1]

        us = proj_ref[:, :W]
        dus = []
        for c in range(nch):
            lrc, lic = _slabs_to_cols(lr, c * spc, spc), _slabs_to_cols(li, c * spc, spc)
            pair_products(3 * nch + 3 * c, [us[:, c * cch : (c + 1) * cch], lrc, lic], [(dbbre_ref.at[c], 0, 1, False), (dbbim_ref.at[c], 0, 2, False)])
            dus.append(_mm_nt(lrc, bbre_ref[c]) + _mm_nt(lic, bbim_ref[c]))
        du_s = jnp.concatenate(dus, axis=1) + dskip_ref[...] * dy

        dp = dp_ref[...]
        t1 = (tile * tm + 1 + lax.broadcasted_iota(jnp.int32, (tm, pg), 0)).astype(F32)
        dups, heads = [], []
        for g, w in enumerate(POOL_WINDOWS):
            dpg = dp[:, g * pg : (g + 1) * pg]
            qg = dpg / jnp.minimum(t1, float(w))
            ext = jnp.concatenate([qg, halo[:, g * pg : (g + 1) * pg]], axis=0)
            dups.append(_window_sum(ext, g + 1, False)[:tm, :] - dpg)
            heads.append(qg[:POOL_HALO, :])
        halo[...] = jnp.concatenate(heads, axis=1)
        dproj = jnp.concatenate([du_s] + dups, axis=1).astype(MXU_DTYPE)
        dproj_ref[...] = dproj.astype(dproj_ref.dtype)
        dx, dg = _rms_bwd(h_ref[...], gpre_ref[...], _mm_nt(dproj, win_ref[...]))
        _acc(dgpre_ref, dg, first)
        dh1_ref[...] = dh2_ref[...] + dx

    rev = lambda cols: _row_spec(tm, cols, rev_n=nt)
    slab_spec = pl.BlockSpec((n_slabs, tm, LANES), lambda i: (0, nt - 1 - i, 0))
    prev_spec = pl.BlockSpec((n_slabs, SUBLANES, LANES), lambda i: (0, jnp.maximum((nt - 1 - i) * blocks_per_tile - 1, 0), 0))
    consts = (g_pre, dskip, win, bbre, bbim, ccre, ccim, tabs, pows)
    return _pallas(
        body,
        name=name,
        grid=(nt,),
        out_shape=[jax.ShapeDtypeStruct((T, D), F32), jax.ShapeDtypeStruct((T, D), MXU_DTYPE), jax.ShapeDtypeStruct((1, D), F32),
                   jax.ShapeDtypeStruct(ccre.shape, F32), jax.ShapeDtypeStruct(ccim.shape, F32), jax.ShapeDtypeStruct(bbre.shape, F32), jax.ShapeDtypeStruct(bbim.shape, F32),
                   jax.ShapeDtypeStruct((n_slabs, SUBLANES, LANES), F32), jax.ShapeDtypeStruct((n_slabs, SUBLANES, LANES), F32)],
        in_specs=[rev(W), rev(W), slab_spec, slab_spec, prev_spec, prev_spec, rev(D), rev(D), rev(D)] + [_full_spec(o.shape) for o in consts],
        out_specs=[rev(D), rev(D), _full_spec((1, D)), _full_spec(ccre.shape), _full_spec(ccim.shape), _full_spec(bbre.shape), _full_spec(bbim.shape),
                   _full_spec((n_slabs, SUBLANES, LANES)), _full_spec((n_slabs, SUBLANES, LANES))],
        scratch_shapes=[pltpu.VMEM((n_slabs, tm, LANES), F32), pltpu.VMEM((n_slabs, tm, LANES), F32), pltpu.VMEM((n_slabs, SUBLANES, LANES), F32), pltpu.VMEM((n_slabs, SUBLANES, LANES), F32),
                        pltpu.VMEM((POOL_HALO, W), F32), pltpu.VMEM((6 * nch, tm, sch), F32)],
        operands=(dy, dpooled, xr, xi, xr, xi, proj, h1, dh2, *consts),
        exchange=exchange,
    )


def _discretize(lam_re, lam_im, log_dt, b_re, b_im):
    dt = jnp.exp(log_dt)[:, None]
    decay = jnp.exp(lam_re * dt)
    ang = lam_im * dt
    a_re, a_im = decay * jnp.cos(ang), decay * jnp.sin(ang)
    nr = a_re - 1.0
    den = lam_re * lam_re + lam_im * lam_im
    q_re = (nr * lam_re + a_im * lam_im) / den
    q_im = (a_im * lam_re - nr * lam_im) / den
    bb_re = q_re[..., None] * b_re - q_im[..., None] * b_im
    bb_im = q_re[..., None] * b_im + q_im[..., None] * b_re
    return a_re, a_im, bb_re, bb_im


GROUPS_PER_CHUNK = 16


def _block_diag(w, rows_first):
    G = w.shape[0]
    nch = G // GROUPS_PER_CHUNK
    if not rows_first:
        w = jnp.swapaxes(w, 1, 2)
    p, q = w.shape[1], w.shape[2]
    eye = jnp.eye(GROUPS_PER_CHUNK, dtype=w.dtype)
    out = jnp.einsum("cgpq,gk->cgpkq", w.reshape(nch, GROUPS_PER_CHUNK, p, q), eye)
    return out.reshape(nch, GROUPS_PER_CHUNK * p, GROUPS_PER_CHUNK * q)


def _block_diag_extract(m, p, q, rows_first):
    nch = m.shape[0]
    eye = jnp.eye(GROUPS_PER_CHUNK, dtype=m.dtype)
    out = jnp.einsum("cgpkq,gk->cgpq", m.reshape(nch, GROUPS_PER_CHUNK, p, GROUPS_PER_CHUNK, q), eye).reshape(nch * GROUPS_PER_CHUNK, p, q)
    return out if rows_first else jnp.swapaxes(out, 1, 2)


def _cmul(ar, ai, br, bi):
    return ar * br - ai * bi, ar * bi + ai * br


def _powers(ar, ai, count):
    pr, pi = ar[None], ai[None]
    while pr.shape[0] < count:
        nr, ni = _cmul(pr, pi, pr[-1][None], pi[-1][None])
        pr, pi = jnp.concatenate([pr, nr]), jnp.concatenate([pi, ni])
    return pr[:count], pi[:count]


def _scan_tables(a_re, a_im, seg):
    n = a_re.size
    ns = n // LANES
    ar, ai = a_re.reshape(n), a_im.reshape(n)
    pr, pi = _powers(ar, ai, seg)
    jr, ji = _powers(pr[-1], pi[-1], SUBLANES)

    def bcast(v):
        return jnp.broadcast_to(v.reshape(ns, 1, LANES), (ns, SUBLANES, LANES))

    def per_sublane(vs):
        return jnp.transpose(vs.reshape(SUBLANES, ns, LANES), (1, 0, 2))

    tabs = jnp.stack([bcast(ar), bcast(ai), bcast(jr[0]), bcast(ji[0]), bcast(jr[1]), bcast(ji[1]), bcast(jr[3]), bcast(ji[3]),
                      per_sublane(jr), per_sublane(ji), per_sublane(jr[::-1]), per_sublane(ji[::-1])])

    def rows(vs):
        return jnp.broadcast_to(jnp.transpose(vs.reshape(seg, ns, 1, LANES), (1, 0, 2, 3)), (ns, seg, SUBLANES, LANES))

    return tabs, jnp.stack([rows(pr), rows(pi)])


SMALL = ("ffn1_pre_norm", "ffn1_post_norm", "mix_pre_norm", "mix_post_norm", "ssm_lambda_re", "ssm_lambda_im", "ssm_log_dt", "ssm_b_re", "ssm_b_im", "ssm_c_re", "ssm_c_im",
         "ssm_d", "ssm_w_glu", "pool_w", "pool_scale", "ssm_out_norm", "pool_out_norm", "ffn2_pre_norm", "ffn2_post_norm")
BIG = ("ffn1_w_gate", "ffn1_w_up", "ffn1_w_down", "w_in", "w_out", "ffn2_w_gate", "ffn2_w_up", "ffn2_w_down")
ORDER = ("meta_tokens", "ffn1_pre_norm", "ffn1_post_norm", "ffn1_w_gate", "ffn1_w_up", "ffn1_w_down", "mix_pre_norm", "mix_post_norm", "w_in", "ssm_lambda_re", "ssm_lambda_im",
         "ssm_log_dt", "ssm_b_re", "ssm_b_im", "ssm_c_re", "ssm_c_im", "ssm_d", "ssm_w_glu", "pool_w", "pool_scale", "ssm_out_norm", "pool_out_norm", "w_out", "ffn2_pre_norm",
         "ffn2_post_norm", "ffn2_w_gate", "ffn2_w_up", "ffn2_w_down")
PACK_ROWS = SUBLANES * 8
def _pack(arrays, rows):
    flat = jnp.concatenate([a.reshape(-1) for a in arrays])
    return jnp.pad(flat, (0, rows * LANES - flat.size)).reshape(rows, LANES)


def _unpack(packed, shapes):
    flat = packed.reshape(-1)
    out, off = [], 0
    for s in shapes:
        n = math.prod(s)
        out.append(flat[off : off + n].reshape(s))
        off += n
    return out


def _step(p, x, loss_target, m, v):
    D = x.shape[-1]
    chip = (2 * lax.axis_index("x") + lax.axis_index("y")).astype(jnp.int32)
    place = jnp.stack([chip, lax.axis_index("c").astype(jnp.int32)])

    def gather_buffer(w):
        own = w.reshape(1, 2, w.shape[0] // 2, w.shape[1])
        return lax.dynamic_update_slice(lax.empty((4,) + own.shape[1:], own.dtype), own, (chip, 0, 0, 0))

    def rows_of(n, a):
        return jnp.swapaxes(a[0], 0, 1) if n.endswith(("w_gate", "w_up")) else a[0]

    def rows_back(n, a):
        return (jnp.swapaxes(a, 0, 1) if n.endswith(("w_gate", "w_up")) else a)[None]

    def grad_view(g):
        return g.reshape(4, 2, g.shape[0] // 8, g.shape[1])

    def reduce_sum(got_sibling, views, tag):
        sums = [_add_own_half(v_, g_, place, f"{tag}_add_sibling_{k}") for k, (v_, g_) in enumerate(zip(views, got_sibling))]
        return [s[0] for s in sums], [s[1] for s in sums]

    def reduce_halves(parts, got_chips, tag):
        return [_add_chips(p_, g_, place, f"{tag}_add_chips_{k}") for k, (p_, g_) in enumerate(zip(parts, got_chips))]

    first_names = ("ffn1_w_gate", "ffn1_w_up", "ffn1_w_down")
    later_names = ("w_in", "w_out", "ffn2_w_gate", "ffn2_w_up", "ffn2_w_down")
    bufs = {n: gather_buffer(rows_of(n, p[n]).astype(MXU_DTYPE)) for n in BIG}
    gathered_weight = lambda g_: g_.reshape(-1, g_.shape[-1])
    got = _exchange_call(_Gather([bufs[n] for n in first_names] + [gather_buffer(p["meta_tokens"])]), "gather_first")
    full = {n: gathered_weight(g_) for n, g_ in zip(first_names, got)}
    meta = jnp.transpose(got[-1].reshape(4, N_META, -1), (1, 0, 2)).reshape(N_META, D)

    vec = lambda n: p[n].reshape(1, -1)
    G, N, H = p["ssm_b_re"].shape[1:]
    a_re, a_im, bb_re, bb_im = _discretize(p["ssm_lambda_re"][0], p["ssm_lambda_im"][0], p["ssm_log_dt"][0], p["ssm_b_re"][0], p["ssm_b_im"][0])
    tabs, pows = _scan_tables(a_re, a_im, MIX_TILE // SUBLANES)
    bf = lambda a: a.astype(MXU_DTYPE)
    bbre, bbim = bf(_block_diag(bb_re, False)), bf(_block_diag(bb_im, False))
    ccre, ccim = bf(_block_diag(p["ssm_c_re"][0], False)), bf(_block_diag(p["ssm_c_im"][0], False))
    wgv, wgg = bf(_block_diag(p["ssm_w_glu"][0][:, :, :H], True)), bf(_block_diag(p["ssm_w_glu"][0][:, :, H:], True))
    pw = bf(p["pool_w"][0])

    n_ffn_steps = (x.shape[1] + N_META) // FFN_TILE
    (h1, f1, ga1, si1, s1, n1, h0), got = _ffn_fwd(
        x[0], vec("ffn1_pre_norm"), vec("ffn1_post_norm"), full["ffn1_w_gate"], full["ffn1_w_up"], full["ffn1_w_down"], "ffn1_fwd",
        exchange=_Gather([bufs[n] for n in later_names], mid_step=(3 * n_ffn_steps) // 4), meta=meta,
    )
    full.update({n: gathered_weight(g_) for n, g_ in zip(later_names, got)})
    proj, xr, xim, y, pooled, mixed, h2, n2, cat = _mix_fwd(
        h1, vec("mix_pre_norm"), vec("ssm_out_norm"), vec("pool_out_norm"), vec("mix_post_norm"), vec("ssm_d"), vec("pool_scale"), full["w_in"], full["w_out"],
        bbre, bbim, ccre, ccim, wgv, wgg, pw, tabs, pows, "mix_fwd",
    )
    (_, f2, ga2, si2, s2, n3, dh3, sq), _ = _ffn_fwd(
        h2, vec("ffn2_pre_norm"), vec("ffn2_post_norm"), full["ffn2_w_gate"], full["ffn2_w_up"], full["ffn2_w_down"], "ffn2_fwd", target=loss_target[0]
    )
    loss = lax.psum(0.5 * jnp.sum(sq) / D, ("x", "y", "c"))

    g, shared = {}, {}
    ffn_names = lambda tag: (tag + "_w_gate", tag + "_w_up", tag + "_w_down")

    da, db, df, dh2, g["ffn2_post_norm"], g["ffn2_pre_norm"] = _ffn_bwd(
        dh3, f2, ga2, si2, h2, vec("ffn2_post_norm"), vec("ffn2_pre_norm"), full["ffn2_w_gate"], full["ffn2_w_up"], full["ffn2_w_down"], "ffn2_bwd"
    )
    views2 = [
        grad_view(_tn_matmul(da, n3, "ffn2_dw_gate")[0]),
        grad_view(_tn_matmul(db, n3, "ffn2_dw_up")[0]),
        grad_view(_tn_matmul(s2, df, "ffn2_dw_down")[0]),
    ]
    (dy, dpooled, dmixed, g["mix_post_norm"], g["ssm_out_norm"], g["pool_out_norm"], g["ssm_d"], g["pool_scale"], dwgv, dwgg, g["pool_w"]), got = _mix_bwd_heads(
        dh2, mixed, y, pooled, proj, vec("ssm_out_norm"), vec("pool_out_norm"), vec("mix_post_norm"), vec("pool_scale"), full["w_out"], wgv, wgg, pw, "mix_bwd_heads",
        exchange=_SiblingScatter(views2),
    )
    parts2, wire2 = reduce_sum(got, views2, "ffn2")
    (dh1, dproj, g["mix_pre_norm"], dccre, dccim, dbbre, dbbim, dar, dai), got = _mix_bwd_scan(
        dy, dpooled, xr, xim, proj, h1, dh2, vec("mix_pre_norm"), vec("ssm_d"), full["w_in"], bbre, bbim, ccre, ccim, tabs, pows, "mix_bwd_scan",
        exchange=_ChipScatter(wire2),
    )
    halves2 = reduce_halves(parts2, got, "ffn2")
    dw_in, got = _tn_matmul(n2, dproj, "dw_in", exchange=_SiblingShare(halves2))
    shared.update(zip(ffn_names("ffn2"), got))
    dw_out, _ = _tn_matmul(cat, dmixed, "dw_out")
    views_m = [grad_view(dw_in), grad_view(dw_out)]

    (da, db, df, g["ffn1_post_norm"]), got = _ffn_bwd_down(dh1, f1, ga1, si1, vec("ffn1_post_norm"), full["ffn1_w_down"], "ffn1_bwd_down", exchange=_SiblingScatter(views_m))
    parts_m, wire_m = reduce_sum(got, views_m, "mix")
    dw_down, got = _tn_matmul(s1, df, "ffn1_dw_down", exchange=_ChipScatter(wire_m))
    halves_m = reduce_halves(parts_m, got, "mix")
    views_d = [grad_view(dw_down)]
    ex = _Group([_SiblingShare(halves_m), _SiblingScatter(views_d)])
    dw_gate, got = _tn_matmul(da, n1, "ffn1_dw_gate", exchange=ex)
    got_m, got_d = ex.split(got)
    shared.update(zip(("w_in", "w_out"), got_m))
    parts_d, wire_d = reduce_sum(got_d, views_d, "ffn1_down")
    views_g = [grad_view(dw_gate)]
    ex = _Group([_ChipScatter(wire_d), _SiblingScatter(views_g)])
    dw_up, got = _tn_matmul(db, n1, "ffn1_dw_up", exchange=ex)
    got_d, got_g = ex.split(got)
    halves_d = reduce_halves(parts_d, got_d, "ffn1_down")
    parts_g, wire_g = reduce_sum(got_g, views_g, "ffn1_gate")
    views_u = [grad_view(dw_up)]
    ex = _Group([_SiblingShare(halves_d), _ChipScatter(wire_g), _SiblingScatter(views_u)])
    (dh0, g["ffn1_pre_norm"]), got = _ffn_bwd_up(da, db, h0, dh1, vec("ffn1_pre_norm"), full["ffn1_w_gate"], full["ffn1_w_up"], "ffn1_bwd_up", exchange=ex)
    got_d, got_g, got_u = ex.split(got)
    shared["ffn1_w_down"] = got_d[0]
    halves_g = reduce_halves(parts_g, got_g, "ffn1_gate")
    parts_u, wire_u = reduce_sum(got_u, views_u, "ffn1_up")
    grad_x = dh0[N_META:][None]

    g["ssm_c_re"] = _block_diag_extract(dccre, N, H, False)
    g["ssm_c_im"] = _block_diag_extract(dccim, N, H, False)
    g["ssm_w_glu"] = jnp.concatenate([_block_diag_extract(dwgv, H, H, True), _block_diag_extract(dwgg, H, H, True)], axis=-1)
    d_a_re, d_a_im = jnp.sum(dar, axis=1).reshape(G, N), jnp.sum(dai, axis=1).reshape(G, N)
    _, pull = jax.vjp(_discretize, p["ssm_lambda_re"][0], p["ssm_lambda_im"][0], p["ssm_log_dt"][0], p["ssm_b_re"][0], p["ssm_b_im"][0])
    g["ssm_lambda_re"], g["ssm_lambda_im"], g["ssm_log_dt"], g["ssm_b_re"], g["ssm_b_im"] = pull(
        (d_a_re, d_a_im, _block_diag_extract(dbbre, H, N, False), _block_diag_extract(dbbim, H, N, False))
    )

    small_shapes = [p[n].shape for n in SMALL] + [(N_META, D)]
    small_size = sum(math.prod(s) for s in small_shapes)
    rows = -(-small_size // (LANES * PACK_ROWS)) * PACK_ROWS
    views_s = [_pack([g[n] for n in SMALL] + [dh0[:N_META]], rows).reshape(4, 2, rows // 8, LANES)]
    parts_s, wire_s = reduce_sum(_exchange_call(_SiblingScatter(views_s), "small_reduce_sibling"), views_s, "small")
    ex = _Group([_ChipScatter(wire_u + wire_s), _SiblingShare(halves_g)])
    got_c, got_g = ex.split(_exchange_call(ex, "tail_reduce_chips"))
    shared["ffn1_w_gate"] = got_g[0]
    got = _exchange_call(_SiblingShare(reduce_halves(parts_u + parts_s, got_c, "tail")), "tail_reduce_share")
    shared["ffn1_w_up"] = got[0]
    small_buf = lax.dynamic_update_slice(lax.empty((4,) + got[1].shape, F32), got[1][None], (chip, 0, 0, 0))
    small_all = _exchange_call(_Gather([small_buf]), "gather_small")[0].reshape(rows, LANES)
    grads = dict(zip(SMALL + ("meta_full",), _unpack(small_all, small_shapes)))
    grads["meta_tokens"] = lax.dynamic_slice_in_dim(grads.pop("meta_full"), chip * (D // 4), D // 4, axis=1)
    delta, new_m, new_v = {}, {}, {}
    for n in BIG:
        g_rows = shared[n].reshape(-1, shared[n].shape[-1])
        outs = _adamw(rows_of(n, p[n]), g_rows, rows_of(n, m[n]), rows_of(n, v[n]), "adamw_" + n)
        grads[n], delta[n], new_m[n], new_v[n] = (rows_back(n, a) for a in (g_rows, *outs))
    delta["meta_tokens"], new_m["meta_tokens"], new_v["meta_tokens"] = _adamw(p["meta_tokens"], grads["meta_tokens"], m["meta_tokens"], v["meta_tokens"], "adamw_meta_tokens")

    def as_2d(n, a):
        a = a.reshape(p[n].shape)[0]
        if n in ("ssm_b_re", "ssm_b_im"):
            a = jnp.swapaxes(a, 1, 2)
        return a.reshape(-1, a.shape[-1])

    def from_2d(n, a):
        if n in ("ssm_b_re", "ssm_b_im"):
            g_, n_, h_ = p[n].shape[1:]
            return jnp.swapaxes(a.reshape(g_, h_, n_), 1, 2)[None]
        return a.reshape(p[n].shape)

    outs = _adamw_many(*[[as_2d(n, t[n]) for n in SMALL] for t in (p, grads, m, v)], "adamw_small")
    for out, arrays in zip((delta, new_m, new_v), outs):
        out.update({n: from_2d(n, a) for n, a in zip(SMALL, arrays)})

    return (loss, grad_x, *[grads[n] for n in ORDER], *[delta[n] for n in ORDER], *[new_m[n] for n in ORDER], *[new_v[n] for n in ORDER])


def kernel(x, meta_tokens, ffn1_pre_norm, ffn1_post_norm, ffn1_w_gate, ffn1_w_up, ffn1_w_down, mix_pre_norm, mix_post_norm, w_in, ssm_lambda_re, ssm_lambda_im, ssm_log_dt, ssm_b_re, ssm_b_im, ssm_c_re, ssm_c_im, ssm_d, ssm_w_glu, pool_w, pool_scale, ssm_out_norm, pool_out_norm, w_out, ffn2_pre_norm, ffn2_post_norm, ffn2_w_gate, ffn2_w_up, ffn2_w_down, loss_target, m_meta_tokens, m_ffn1_pre_norm, m_ffn1_post_norm, m_ffn1_w_gate, m_ffn1_w_up, m_ffn1_w_down, m_mix_pre_norm, m_mix_post_norm, m_w_in, m_ssm_lambda_re, m_ssm_lambda_im, m_ssm_log_dt, m_ssm_b_re, m_ssm_b_im, m_ssm_c_re, m_ssm_c_im, m_ssm_d, m_ssm_w_glu, m_pool_w, m_pool_scale, m_ssm_out_norm, m_pool_out_norm, m_w_out, m_ffn2_pre_norm, m_ffn2_post_norm, m_ffn2_w_gate, m_ffn2_w_up, m_ffn2_w_down, v_meta_tokens, v_ffn1_pre_norm, v_ffn1_post_norm, v_ffn1_w_gate, v_ffn1_w_up, v_ffn1_w_down, v_mix_pre_norm, v_mix_post_norm, v_w_in, v_ssm_lambda_re, v_ssm_lambda_im, v_ssm_log_dt, v_ssm_b_re, v_ssm_b_im, v_ssm_c_re, v_ssm_c_im, v_ssm_d, v_ssm_w_glu, v_pool_w, v_pool_scale, v_ssm_out_norm, v_pool_out_norm, v_w_out, v_ffn2_pre_norm, v_ffn2_post_norm, v_ffn2_w_gate, v_ffn2_w_up, v_ffn2_w_down):
    args = locals()
    p = {n: args[n] for n in ORDER}
    m = {n: args["m_" + n] for n in ORDER}
    v = {n: args["v_" + n] for n in ORDER}
    return _step(p, x, loss_target, m, v)
```

```python
import math

import jax
import jax.numpy as jnp
from jax import lax
from jax.experimental import pallas as pl
from jax.experimental.pallas import tpu as pltpu

F32 = jnp.float32
MXU_DTYPE = jnp.bfloat16
WIRE_DTYPE = jnp.bfloat16

RMS_EPS = 1e-6
N_META = 16
POOL_WINDOWS = (2, 4, 8, 16)
POOL_HALO = 16
ADAM_LR, ADAM_B1, ADAM_B2, ADAM_EPS, ADAM_WD, ADAM_STEP = 0.001, 0.9, 0.999, 1e-08, 0.01, 10

LANES = 128
SUBLANES = 8
VMEM_LIMIT = 60 * 1024 * 1024
FFN_TILE = 432
FFN_CHUNK = 1024
MIX_TILE = 216
MIX_FWD_SUBTILES = 2
SLAB_GROUP = 8
MESH = pl.DeviceIdType.MESH
ANY = pl.BlockSpec(memory_space=pl.ANY)


def _mm(a, b):
    return jnp.dot(a.astype(MXU_DTYPE), b.astype(MXU_DTYPE), preferred_element_type=F32)


def _mm_nt(a, b):
    return lax.dot_general(a.astype(MXU_DTYPE), b.astype(MXU_DTYPE), (((1,), (1,)), ((), ())), preferred_element_type=F32)


def _mm_tn(a, b):
    return lax.dot_general(a.astype(MXU_DTYPE), b.astype(MXU_DTYPE), (((0,), (0,)), ((), ())), preferred_element_type=F32)


def _rms_stat(x):
    return lax.rsqrt(jnp.mean(x * x, axis=-1, keepdims=True) + RMS_EPS)


def _rms_bwd(x, g, dy):
    r = _rms_stat(x)
    xh = x * r
    dg = jnp.sum(dy * xh, axis=0, keepdims=True)
    dxh = dy * g
    dx = r * (dxh - xh * jnp.mean(dxh * xh, axis=-1, keepdims=True))
    return dx, dg


def _sigmoid(x):
    return 1.0 / (1.0 + jnp.exp(-x))


GELU_C = math.sqrt(2.0 / math.pi)
GELU_K = 0.044715


def _gelu(y):
    return 0.5 * y * (1.0 + jnp.tanh(GELU_C * (y + GELU_K * y * y * y)))


def _gelu_grad(y):
    th = jnp.tanh(GELU_C * (y + GELU_K * y * y * y))
    return 0.5 * (1.0 + th) + 0.5 * y * (1.0 - th * th) * GELU_C * (1.0 + 3.0 * GELU_K * y * y)


def _row_spec(tile, cols, rev_n=None):
    if rev_n is None:
        return pl.BlockSpec((tile, cols), lambda i: (i, 0))
    return pl.BlockSpec((tile, cols), lambda i: (rev_n - 1 - i, 0))


def _full_spec(shape, single=False):
    zeros = (0,) * len(shape)
    if single:
        return pl.BlockSpec(shape, lambda *_: zeros, pipeline_mode=pl.Buffered(1))
    return pl.BlockSpec(shape, lambda *_: zeros)


def _acc(ref, val, first):
    @pl.when(first)
    def _():
        ref[...] = val

    @pl.when(jnp.logical_not(first))
    def _():
        ref[...] += val


def _place():
    x, y, c = lax.axis_index("x"), lax.axis_index("y"), lax.axis_index("c")
    others = [(1 - x, y), (x, 1 - y), (1 - x, 1 - y)]
    return x, y, c, others


class _Exchange:
    mid_step = None

    def __init__(self, ins, out_shapes, aliases, n_sems):
        self.ins, self.out_shapes, self.aliases, self.n_sems = list(ins), list(out_shapes), dict(aliases), n_sems

    def mid(self, ins, outs, send_sems, recv_sems):
        pass


class _SiblingScatter(_Exchange):
    def __init__(self, views):
        super().__init__(views, [jax.ShapeDtypeStruct((4,) + v.shape[2:], v.dtype) for v in views], {}, 4 * len(views))

    def _copies(self, ins, outs, send_sems, recv_sems):
        x, y, c, _ = _place()
        return [
            pltpu.make_async_remote_copy(src_ref=ins[a].at[k, 1 - c], dst_ref=outs[a].at[k], send_sem=send_sems.at[4 * a + k], recv_sem=recv_sems.at[4 * a + k], device_id=(x, y, 1 - c), device_id_type=MESH)
            for a in range(len(ins))
            for k in range(4)
        ]

    def start(self, *refs):
        for cp in self._copies(*refs):
            cp.start()

    def finish(self, *refs):
        cps = self._copies(*refs)
        for cp in cps:
            cp.wait_recv()
        for cp in cps:
            cp.wait_send()


class _ChipScatter(_Exchange):
    def __init__(self, parts):
        super().__init__(parts, [jax.ShapeDtypeStruct((3,) + p.shape[1:], p.dtype) for p in parts], {}, 3 * len(parts))

    def _copies(self, ins, outs, send_sems, recv_sems):
        x, y, c, others = _place()
        return [
            pltpu.make_async_remote_copy(src_ref=ins[a].at[2 * chip[0] + chip[1]], dst_ref=outs[a].at[j], send_sem=send_sems.at[3 * a + j], recv_sem=recv_sems.at[3 * a + j], device_id=(*chip, c), device_id_type=MESH)
            for a in range(len(ins))
            for j, chip in enumerate(others)
        ]

    start = _SiblingScatter.start
    finish = _SiblingScatter.finish


class _SiblingShare(_Exchange):
    def __init__(self, bufs):
        super().__init__(bufs, [jax.ShapeDtypeStruct(b.shape, b.dtype) for b in bufs], {a: a for a in range(len(bufs))}, len(bufs))

    def _copy(self, outs, send_sems, recv_sems, a, half):
        x, y, c, _ = _place()
        mine = outs[a].at[c if half == "mine" else 1 - c]
        return pltpu.make_async_remote_copy(src_ref=mine, dst_ref=mine, send_sem=send_sems.at[a], recv_sem=recv_sems.at[a], device_id=(x, y, 1 - c), device_id_type=MESH)

    def start(self, ins, outs, send_sems, recv_sems):
        for a in range(len(outs)):
            self._copy(outs, send_sems, recv_sems, a, "mine").start()

    def finish(self, ins, outs, send_sems, recv_sems):
        for a in range(len(outs)):
            self._copy(outs, send_sems, recv_sems, a, "theirs").wait_recv()
        for a in range(len(outs)):
            self._copy(outs, send_sems, recv_sems, a, "mine").wait_send()


class _Gather(_Exchange):
    def __init__(self, bufs, mid_step=None):
        super().__init__(bufs, [jax.ShapeDtypeStruct(b.shape, b.dtype) for b in bufs], {a: a for a in range(len(bufs))}, 6 * len(bufs))
        self.mid_step = mid_step

    def _copy(self, outs, send_sems, recv_sems, a, j, chip, half, to):
        blk = outs[a].at[2 * chip[0] + chip[1], half]
        return pltpu.make_async_remote_copy(src_ref=blk, dst_ref=blk, send_sem=send_sems.at[6 * a + j], recv_sem=recv_sems.at[6 * a + j], device_id=to, device_id_type=MESH)

    def start(self, ins, outs, send_sems, recv_sems):
        x, y, c, others = _place()
        for a in range(len(outs)):
            for j, chip in enumerate(others):
                self._copy(outs, send_sems, recv_sems, a, j, (x, y), c, (*chip, c)).start()

    def mid(self, ins, outs, send_sems, recv_sems):
        x, y, c, others = _place()
        for a in range(len(outs)):
            for j, chip in enumerate(others):
                self._copy(outs, send_sems, recv_sems, a, j, chip, c, (x, y, c)).wait_recv()
                self._copy(outs, send_sems, recv_sems, a, 3 + j, chip, c, (x, y, 1 - c)).start()

    def finish(self, ins, outs, send_sems, recv_sems):
        x, y, c, others = _place()
        for a in range(len(outs)):
            for j, chip in enumerate(others):
                self._copy(outs, send_sems, recv_sems, a, 3 + j, chip, 1 - c, (x, y, c)).wait_recv()
        for a in range(len(outs)):
            for j, chip in enumerate(others):
                self._copy(outs, send_sems, recv_sems, a, j, (x, y), c, (*chip, c)).wait_send()
                self._copy(outs, send_sems, recv_sems, a, 3 + j, chip, c, (x, y, 1 - c)).wait_send()


class _SemSlice:
    def __init__(self, sems, off):
        self.sems, self.off = sems, off

    @property
    def at(self):
        return self

    def __getitem__(self, i):
        return self.sems.at[self.off + i]


class _Group(_Exchange):
    def __init__(self, exchanges):
        ins, outs, aliases, n_sems, self.spans = [], [], {}, 0, []
        for ex in exchanges:
            self.spans.append((len(ins), len(outs), n_sems))
            aliases.update({len(ins) + i: len(outs) + o for i, o in ex.aliases.items()})
            ins, outs, n_sems = ins + ex.ins, outs + ex.out_shapes, n_sems + ex.n_sems
        super().__init__(ins, outs, aliases, n_sems)
        self.exchanges = exchanges
        mids = {ex.mid_step for ex in exchanges if ex.mid_step is not None}
        self.mid_step = mids.pop() if mids else None

    def _each(self, method, ins, outs, send_sems, recv_sems):
        for ex, (i0, o0, s0) in zip(self.exchanges, self.spans):
            getattr(ex, method)(ins[i0 : i0 + len(ex.ins)], outs[o0 : o0 + len(ex.out_shapes)], _SemSlice(send_sems, s0), _SemSlice(recv_sems, s0))

    def start(self, *refs):
        self._each("start", *refs)

    def mid(self, *refs):
        self._each("mid", *refs)

    def finish(self, *refs):
        self._each("finish", *refs)

    def split(self, outs):
        return [outs[o0 : o0 + len(ex.out_shapes)] for ex, (_, o0, _) in zip(self.exchanges, self.spans)]


def _exchange_call(ex, name):
    n, m = len(ex.ins), len(ex.out_shapes)

    def body(*refs):
        parts = (refs[:n], refs[n : n + m], refs[n + m], refs[n + m + 1])
        ex.start(*parts)
        ex.mid(*parts)
        ex.finish(*parts)

    return pl.pallas_call(
        body,
        name=name,
        out_shape=ex.out_shapes,
        in_specs=[ANY] * n,
        out_specs=[ANY] * m,
        scratch_shapes=[pltpu.SemaphoreType.DMA((ex.n_sems,)), pltpu.SemaphoreType.DMA((ex.n_sems,))],
        input_output_aliases=ex.aliases,
    )(*ex.ins)


def _pallas(body, *, name, grid, in_specs, out_specs, out_shape, operands, scratch_shapes=(), exchange=None):
    params = pltpu.CompilerParams(dimension_semantics=("arbitrary",) * len(grid), vmem_limit_bytes=VMEM_LIMIT)
    if exchange is None:
        outs = pl.pallas_call(body, name=name, grid=grid, in_specs=in_specs, out_specs=out_specs, out_shape=out_shape, scratch_shapes=list(scratch_shapes), compiler_params=params)(*operands)
        return outs, []
    ex = exchange
    n_in, n_out, n_scr = len(in_specs), len(out_specs), len(scratch_shapes)
    x_in, x_out = len(ex.ins), len(ex.out_shapes)

    def hosted(*refs):
        ins, x_ins = refs[:n_in], refs[n_in : n_in + x_in]
        outs, x_outs = refs[n_in + x_in : n_in + x_in + n_out], refs[n_in + x_in + n_out : n_in + x_in + n_out + x_out]
        rest = refs[n_in + x_in + n_out + x_out :]
        parts = (x_ins, x_outs, rest[n_scr], rest[n_scr + 1])
        ids = [pl.program_id(d) for d in range(len(grid))]
        first = _all([i == 0 for i in ids])
        last = _all([i == g - 1 for i, g in zip(ids, grid)])

        @pl.when(first)
        def _():
            ex.start(*parts)

        body(*ins, *outs, *rest[:n_scr])

        if ex.mid_step is not None:

            @pl.when(ids[0] == ex.mid_step)
            def _():
                ex.mid(*parts)

        @pl.when(last)
        def _():
            ex.finish(*parts)

    outs = pl.pallas_call(
        hosted,
        name=name,
        grid=grid,
        in_specs=list(in_specs) + [ANY] * x_in,
        out_specs=list(out_specs) + [ANY] * x_out,
        out_shape=list(out_shape) + ex.out_shapes,
        scratch_shapes=list(scratch_shapes) + [pltpu.SemaphoreType.DMA((ex.n_sems,)), pltpu.SemaphoreType.DMA((ex.n_sems,))],
        input_output_aliases={n_in + i: n_out + o for i, o in ex.aliases.items()},
        compiler_params=params,
    )(*operands, *ex.ins)
    return outs[:n_out], outs[n_out:]


def _all(conds):
    out = conds[0]
    for c in conds[1:]:
        out = jnp.logical_and(out, c)
    return out


def _row_tile(rows):
    if rows <= 512:
        return rows
    for t in (512, 352, 256, 176, 128, 112, 64, 32, 16, 8):
        if rows % t == 0:
            return t
    return rows


def _add_own_half(view, got, place, name):
    _, _, r, c = view.shape
    tr = _row_tile(r)

    def body(place_ref, v_ref, g_ref, o_ref, w_ref):
        s = v_ref[...] + g_ref[...]
        o_ref[...] = s
        w_ref[...] = s.astype(w_ref.dtype)

    blk = pl.BlockSpec((None, tr, c), lambda k, i, pr: (k, i, 0))
    return pl.pallas_call(
        body,
        name=name,
        out_shape=[jax.ShapeDtypeStruct((4, r, c), F32), jax.ShapeDtypeStruct((4, r, c), WIRE_DTYPE)],
        grid_spec=pltpu.PrefetchScalarGridSpec(
            num_scalar_prefetch=1, grid=(4, r // tr), in_specs=[pl.BlockSpec((None, None, tr, c), lambda k, i, pr: (k, pr[1], i, 0)), blk], out_specs=[blk, blk]
        ),
        compiler_params=pltpu.CompilerParams(dimension_semantics=("arbitrary", "arbitrary"), vmem_limit_bytes=VMEM_LIMIT),
    )(place, view, got)


def _add_chips(part, got, place, name):
    _, r, c = part.shape
    tr = _row_tile(r)

    def body(place_ref, p_ref, g_ref, o_ref):
        o_ref[...] = ((p_ref[...] + g_ref[0].astype(F32)) + g_ref[1].astype(F32)) + g_ref[2].astype(F32)

    return pl.pallas_call(
        body,
        name=name,
        out_shape=jax.ShapeDtypeStruct((2, r, c), F32),
        grid_spec=pltpu.PrefetchScalarGridSpec(
            num_scalar_prefetch=1,
            grid=(r // tr,),
            in_specs=[pl.BlockSpec((None, tr, c), lambda i, pr: (pr[0], i, 0)), pl.BlockSpec((3, tr, c), lambda i, pr: (0, i, 0))],
            out_specs=pl.BlockSpec((None, tr, c), lambda i, pr: (pr[1], i, 0)),
        ),
        compiler_params=pltpu.CompilerParams(dimension_semantics=("arbitrary",), vmem_limit_bytes=VMEM_LIMIT),
    )(place, part, got)


def _adamw_update(w_ref, g_ref, m_ref, v_ref, d_ref, nm_ref, nv_ref):
    g = g_ref[...]
    nm = ADAM_B1 * m_ref[...] + (1.0 - ADAM_B1) * g
    nv = ADAM_B2 * v_ref[...] + (1.0 - ADAM_B2) * (g * g)
    m_hat = nm / (1.0 - ADAM_B1**ADAM_STEP)
    v_hat = nv / (1.0 - ADAM_B2**ADAM_STEP)
    d_ref[...] = -ADAM_LR * (m_hat / (jnp.sqrt(v_hat) + ADAM_EPS) + ADAM_WD * w_ref[...])
    nm_ref[...] = nm
    nv_ref[...] = nv


def _adamw(w, g, m, v, name):
    r, c = w.shape
    tr = _row_tile(r)
    spec = pl.BlockSpec((tr, c), lambda i: (i, 0))
    outs, _ = _pallas(_adamw_update, name=name, grid=(r // tr,), in_specs=[spec] * 4, out_specs=[spec] * 3, out_shape=[jax.ShapeDtypeStruct((r, c), F32)] * 3, operands=(w, g, m, v))
    return outs


def _adamw_many(ws, gs, ms, vs, name):
    n = len(ws)

    def body(*refs):
        for k in range(n):
            _adamw_update(*(refs[j * n + k] for j in range(7)))

    outs = pl.pallas_call(
        body,
        name=name,
        out_shape=[jax.ShapeDtypeStruct(w.shape, F32) for w in ws] * 3,
        in_specs=[pl.BlockSpec(memory_space=pltpu.VMEM)] * (4 * n),
        out_specs=[pl.BlockSpec(memory_space=pltpu.VMEM)] * (3 * n),
    )(*ws, *gs, *ms, *vs)
    return outs[:n], outs[n : 2 * n], outs[2 * n :]


def _load_weights(pairs, sems):
    @pl.when(pl.program_id(0) == 0)
    def _():
        cps = [pltpu.make_async_copy(src, dst, sems.at[k]) for k, (src, dst) in enumerate(pairs)]
        for cp in cps:
            cp.start()
        for cp in cps:
            cp.wait()


def _ffn_chunks(F):
    bounds = list(range(0, F, FFN_CHUNK)) + [F]
    return list(zip(bounds[:-1], bounds[1:]))


def _shifted_specs(tm, cols):
    per = tm // N_META
    return [_row_spec(tm, cols), pl.BlockSpec((N_META, cols), lambda i: (jnp.maximum(i * per - 1, 0), 0))]


def _shifted_tile(cur_ref, before_ref, tm):
    return jnp.concatenate([before_ref[...], cur_ref[0 : tm - N_META, :]], axis=0)


def _tokens_tile(cur_ref, before_ref, meta_ref, tm, tile_0):
    first = jnp.where(tile_0, meta_ref[...], before_ref[...])
    return jnp.concatenate([first, cur_ref[0 : tm - N_META, :]], axis=0)


def _ffn_fwd(h, g_pre, g_post, wg, wu, wd, name, exchange=None, meta=None, target=None):
    D = h.shape[1]
    T = h.shape[0] + (0 if meta is None else N_META)
    F = wg.shape[0]
    tm = FFN_TILE
    n_src = 1 if meta is None else 3
    n_tgt = 0 if target is None else 2
    n_head = 1 if target is None else 2

    def body(*refs):
        src, refs = refs[:n_src], refs[n_src:]
        tgt, refs = refs[:n_tgt], refs[n_tgt:]
        (gpre_ref, gpost_ref, wg_hbm, wu_hbm, wd_hbm), refs = refs[:5], refs[5:]
        head, (f_ref, ga_ref, si_ref, s_ref, n_ref, wg_v, wu_v, wd_v, sems) = refs[:n_head], refs[n_head:]
        i = pl.program_id(0)
        _load_weights([(wg_hbm, wg_v), (wu_hbm, wu_v), (wd_hbm, wd_v)], sems)
        hh = src[0][...] if meta is None else _tokens_tile(src[0], src[1], src[2], tm, i == 0)
        n = (hh * _rms_stat(hh) * gpre_ref[...]).astype(MXU_DTYPE)
        n_ref[...] = n.astype(n_ref.dtype)
        f = jnp.zeros((tm, D), F32)
        for lo, hi in _ffn_chunks(F):
            a = _mm_nt(n, wg_v[lo:hi, :])
            b = _mm_nt(n, wu_v[lo:hi, :])
            sg = _sigmoid(a)
            si = a * sg
            s = (si * b).astype(MXU_DTYPE)
            ga_ref[:, lo:hi] = (b * (sg * (1.0 + a * (1.0 - sg)))).astype(ga_ref.dtype)
            si_ref[:, lo:hi] = si.astype(si_ref.dtype)
            s_ref[:, lo:hi] = s.astype(s_ref.dtype)
            f = f + _mm(s, wd_v[lo:hi, :])
        f_ref[...] = f
        out = hh + 0.5 * (f * _rms_stat(f) * gpost_ref[...])
        if target is None:
            head[0][...] = out
        else:
            rows = i * tm + lax.broadcasted_iota(jnp.int32, (tm, D), 0)
            err = jnp.where(rows >= N_META, out - _shifted_tile(tgt[0], tgt[1], tm), 0.0)
            head[0][...] = err * (1.0 / D)
            _acc(head[1], jnp.sum(err * err, axis=0, keepdims=True), i == 0)

    tok = jax.ShapeDtypeStruct((T, D), F32)
    act = jax.ShapeDtypeStruct((T, F), MXU_DTYPE)
    src_specs = [_row_spec(tm, D)] if meta is None else _shifted_specs(tm, D) + [_full_spec((N_META, D))]
    src_ops = (h,) if meta is None else (h, h, meta)
    tgt_specs, tgt_ops = ([], ()) if target is None else (_shifted_specs(tm, D), (target, target))
    head_shapes = [tok] if target is None else [tok, jax.ShapeDtypeStruct((1, D), F32)]
    head_specs = [_row_spec(tm, D)] if target is None else [_row_spec(tm, D), _full_spec((1, D))]
    return _pallas(
        body,
        name=name,
        grid=(T // tm,),
        out_shape=head_shapes + [tok, act, act, act, jax.ShapeDtypeStruct((T, D), MXU_DTYPE)],
        in_specs=src_specs + tgt_specs + [_full_spec((1, D)), _full_spec((1, D)), ANY, ANY, ANY],
        out_specs=head_specs + [_row_spec(tm, D), _row_spec(tm, F), _row_spec(tm, F), _row_spec(tm, F), _row_spec(tm, D)],
        scratch_shapes=[pltpu.VMEM(wg.shape, wg.dtype), pltpu.VMEM(wu.shape, wu.dtype), pltpu.VMEM(wd.shape, wd.dtype), pltpu.SemaphoreType.DMA((3,))],
        operands=(*src_ops, *tgt_ops, g_pre, g_post, wg, wu, wd),
        exchange=exchange,
    )


def _ffn_bwd_down(dh, f, ga, si, g_post, wd, name, exchange=None):
    T, D = dh.shape
    F = wd.shape[0]
    tm = FFN_TILE

    def body(dh_ref, f_ref, ga_ref, si_ref, gpost_ref, wd_hbm, da_ref, db_ref, df_ref, dg_ref, wd_v, sems):
        _load_weights([(wd_hbm, wd_v)], sems)
        df, dg = _rms_bwd(f_ref[...], gpost_ref[...], 0.5 * dh_ref[...])
        _acc(dg_ref, dg, pl.program_id(0) == 0)
        dfb = df.astype(MXU_DTYPE)
        df_ref[...] = dfb.astype(df_ref.dtype)
        for lo, hi in _ffn_chunks(F):
            ds = _mm_nt(dfb, wd_v[lo:hi, :])
            da_ref[:, lo:hi] = (ds * ga_ref[:, lo:hi].astype(F32)).astype(da_ref.dtype)
            db_ref[:, lo:hi] = (ds * si_ref[:, lo:hi].astype(F32)).astype(db_ref.dtype)

    act = jax.ShapeDtypeStruct((T, F), MXU_DTYPE)
    return _pallas(
        body,
        name=name,
        grid=(T // tm,),
        out_shape=[act, act, jax.ShapeDtypeStruct((T, D), MXU_DTYPE), jax.ShapeDtypeStruct((1, D), F32)],
        in_specs=[_row_spec(tm, D), _row_spec(tm, D), _row_spec(tm, F), _row_spec(tm, F), _full_spec((1, D)), ANY],
        out_specs=[_row_spec(tm, F), _row_spec(tm, F), _row_spec(tm, D), _full_spec((1, D))],
        scratch_shapes=[pltpu.VMEM(wd.shape, wd.dtype), pltpu.SemaphoreType.DMA((1,))],
        operands=(dh, f, ga, si, g_post, wd),
        exchange=exchange,
    )


def _ffn_bwd_up(da, db, x, meta, dh, g_pre, wg, wu, name, exchange=None):
    D = x.shape[1]
    T = x.shape[0] + N_META
    F = wg.shape[0]
    tm = FFN_TILE
    nt = T // tm
    per = tm // N_META
    tile = lambda i: jnp.minimum(i, nt - 1)

    def body(da_ref, db_ref, x_ref, xb_ref, meta_ref, dh_ref, gpre_ref, wg_hbm, wu_hbm, dx_ref, dmeta_ref, dg_ref, wg_v, wu_v, sems, held):
        i = pl.program_id(0)
        _load_weights([(wg_hbm, wg_v), (wu_hbm, wu_v)], sems)

        @pl.when(i < nt)
        def _():
            dn = jnp.zeros((tm, D), F32)
            for lo, hi in _ffn_chunks(F):
                dn = dn + _mm(da_ref[:, lo:hi], wg_v[lo:hi, :]) + _mm(db_ref[:, lo:hi], wu_v[lo:hi, :])
            dx, dg = _rms_bwd(_tokens_tile(x_ref, xb_ref, meta_ref, tm, i == 0), gpre_ref[...], dn)
            _acc(dg_ref, dg, i == 0)
            dh_in = dh_ref[...] + dx

            @pl.when(i == 0)
            def _():
                dmeta_ref[...] = dh_in[0:N_META, :]

            @pl.when(i > 0)
            def _():
                dx_ref[...] = jnp.concatenate([held[...], dh_in[0:N_META, :]], axis=0)

            held[...] = dh_in[N_META:, :]

        @pl.when(i == nt)
        def _():
            dx_ref[0 : tm - N_META, :] = held[...]

    rows = lambda cols: pl.BlockSpec((tm, cols), lambda i: (tile(i), 0))
    return _pallas(
        body,
        name=name,
        grid=(nt + 1,),
        out_shape=[jax.ShapeDtypeStruct((T - N_META, D), F32), jax.ShapeDtypeStruct((N_META, D), F32), jax.ShapeDtypeStruct((1, D), F32)],
        in_specs=[rows(F), rows(F), rows(D), pl.BlockSpec((N_META, D), lambda i: (jnp.maximum(tile(i) * per - 1, 0), 0)), _full_spec((N_META, D)), rows(D), _full_spec((1, D)), ANY, ANY],
        out_specs=[pl.BlockSpec((tm, D), lambda i: (jnp.maximum(i - 1, 0), 0)), _full_spec((N_META, D)), _full_spec((1, D))],
        scratch_shapes=[pltpu.VMEM(wg.shape, wg.dtype), pltpu.VMEM(wu.shape, wu.dtype), pltpu.SemaphoreType.DMA((2,)), pltpu.VMEM((tm - N_META, D), F32)],
        operands=(da, db, x, x, meta, dh, g_pre, wg, wu),
        exchange=exchange,
    )


def _ffn_bwd(dh, f, ga, si, h, g_post, g_pre, wg, wu, wd, name):
    T, D = dh.shape
    F = wd.shape[0]
    tm = FFN_TILE

    def body(dh_ref, f_ref, ga_ref, si_ref, h_ref, gpost_ref, gpre_ref, wg_hbm, wu_hbm, wd_hbm, da_ref, db_ref, df_ref, dhin_ref, dgpost_ref, dgpre_ref, wg_v, wu_v, wd_v, sems):
        first = pl.program_id(0) == 0
        _load_weights([(wg_hbm, wg_v), (wu_hbm, wu_v), (wd_hbm, wd_v)], sems)
        dh = dh_ref[...]
        df, dg = _rms_bwd(f_ref[...], gpost_ref[...], 0.5 * dh)
        _acc(dgpost_ref, dg, first)
        dfb = df.astype(MXU_DTYPE)
        df_ref[...] = dfb.astype(df_ref.dtype)
        dn = jnp.zeros((tm, D), F32)
        for lo, hi in _ffn_chunks(F):
            ds = _mm_nt(dfb, wd_v[lo:hi, :])
            da = (ds * ga_ref[:, lo:hi].astype(F32)).astype(MXU_DTYPE)
            db = (ds * si_ref[:, lo:hi].astype(F32)).astype(MXU_DTYPE)
            da_ref[:, lo:hi] = da.astype(da_ref.dtype)
            db_ref[:, lo:hi] = db.astype(db_ref.dtype)
            dn = dn + _mm(da, wg_v[lo:hi, :]) + _mm(db, wu_v[lo:hi, :])
        dx, dg = _rms_bwd(h_ref[...], gpre_ref[...], dn)
        _acc(dgpre_ref, dg, first)
        dhin_ref[...] = dh + dx

    act = jax.ShapeDtypeStruct((T, F), MXU_DTYPE)
    vec = jax.ShapeDtypeStruct((1, D), F32)
    outs, _ = _pallas(
        body,
        name=name,
        grid=(T // tm,),
        out_shape=[act, act, jax.ShapeDtypeStruct((T, D), MXU_DTYPE), jax.ShapeDtypeStruct((T, D), F32), vec, vec],
        in_specs=[_row_spec(tm, D), _row_spec(tm, D), _row_spec(tm, F), _row_spec(tm, F), _row_spec(tm, D), _full_spec((1, D)), _full_spec((1, D)), ANY, ANY, ANY],
        out_specs=[_row_spec(tm, F), _row_spec(tm, F), _row_spec(tm, D), _row_spec(tm, D), _full_spec((1, D)), _full_spec((1, D))],
        scratch_shapes=[pltpu.VMEM(wg.shape, wg.dtype), pltpu.VMEM(wu.shape, wu.dtype), pltpu.VMEM(wd.shape, wd.dtype), pltpu.SemaphoreType.DMA((3,))],
        operands=(dh, f, ga, si, h, g_post, g_pre, wg, wu, wd),
    )
    return outs


def _token_tile(T):
    for t in (912, 864, 432):
        if T % t == 0:
            return t
    raise ValueError(f"no token tile for {T} rows")


def _tn_matmul(xm, ym, name, exchange=None):
    T, M = xm.shape
    N = ym.shape[1]
    tk = _token_tile(T)

    def body(x_ref, y_ref, o_ref):
        _acc(o_ref, _mm_tn(x_ref[...], y_ref[...]), pl.program_id(0) == 0)

    (out,), x_outs = _pallas(
        body,
        name=name,
        grid=(T // tk,),
        out_shape=[jax.ShapeDtypeStruct((M, N), F32)],
        in_specs=[pl.BlockSpec((tk, M), lambda k: (k, 0)), pl.BlockSpec((tk, N), lambda k: (k, 0))],
        out_specs=[_full_spec((M, N))],
        operands=(xm, ym),
        exchange=exchange,
    )
    return out, x_outs


TAB_A, TAB_AS1, TAB_AS2, TAB_AS4, TAB_JF, TAB_JB = 0, 2, 4, 6, 8, 10


def _scan_inplace(zr, zi, tabs, pows, car_r, car_i, seg, reverse, base=0):
    n_slabs = zr.shape[0]
    sgn = -1.0 if reverse else 1.0
    row = lax.broadcasted_iota(jnp.int32, (SUBLANES, LANES), 0)

    def cmul(pr, pi, xr, xi):
        return pr * xr - pi * xi, pr * xi + pi * xr

    for k0 in range(0, n_slabs, SLAB_GROUP):
        slabs = range(k0, min(k0 + SLAB_GROUP, n_slabs))
        ar = [tabs[TAB_A, k] for k in slabs]
        ai = [sgn * tabs[TAB_A + 1, k] for k in slabs]

        def first_pass(t, carry):
            r = (seg - 1 - t) if reverse else t
            out = []
            for q, k in enumerate(slabs):
                xr, xi = carry[2 * q], carry[2 * q + 1]
                pr, pi = cmul(ar[q], ai[q], xr, xi)
                nr = pr + zr[k, pl.ds(base + r, SUBLANES, stride=seg), :]
                ni = pi + zi[k, pl.ds(base + r, SUBLANES, stride=seg), :]
                zr[k, pl.ds(base + r, SUBLANES, stride=seg), :] = nr
                zi[k, pl.ds(base + r, SUBLANES, stride=seg), :] = ni
                out += [nr, ni]
            return tuple(out)

        ends = lax.fori_loop(0, seg, first_pass, tuple(jnp.zeros((SUBLANES, LANES), F32) for _ in range(2 * len(slabs))))

        incoming = []
        for q, k in enumerate(slabs):
            fr, fi = ends[2 * q], ends[2 * q + 1]
            for d, tab in ((1, TAB_AS1), (2, TAB_AS2), (4, TAB_AS4)):
                shift, keep = (SUBLANES - d, row < SUBLANES - d) if reverse else (d, row >= d)
                sr = jnp.where(keep, pltpu.roll(fr, shift, 0), 0.0)
                si = jnp.where(keep, pltpu.roll(fi, shift, 0), 0.0)
                pr, pi = cmul(tabs[tab, k], sgn * tabs[tab + 1, k], sr, si)
                fr, fi = fr + pr, fi + pi
            cr, ci = car_r[k], car_i[k]
            jtab = TAB_JB if reverse else TAB_JF
            pr, pi = cmul(tabs[jtab, k], sgn * tabs[jtab + 1, k], cr, ci)
            er, ei = fr + pr, fi + pi
            if reverse:
                inr = jnp.where(row < SUBLANES - 1, pltpu.roll(er, SUBLANES - 1, 0), cr)
                ini = jnp.where(row < SUBLANES - 1, pltpu.roll(ei, SUBLANES - 1, 0), ci)
                car_r[k] = jnp.broadcast_to(er[0:1, :], (SUBLANES, LANES))
                car_i[k] = jnp.broadcast_to(ei[0:1, :], (SUBLANES, LANES))
            else:
                inr = jnp.where(row >= 1, pltpu.roll(er, 1, 0), cr)
                ini = jnp.where(row >= 1, pltpu.roll(ei, 1, 0), ci)
                car_r[k] = jnp.broadcast_to(er[SUBLANES - 1 : SUBLANES, :], (SUBLANES, LANES))
                car_i[k] = jnp.broadcast_to(ei[SUBLANES - 1 : SUBLANES, :], (SUBLANES, LANES))
            incoming += [inr, ini]

        def second_pass(r, _):
            p = (seg - 1 - r) if reverse else r
            for q, k in enumerate(slabs):
                pr, pi = cmul(pows[0, k, p], sgn * pows[1, k, p], incoming[2 * q], incoming[2 * q + 1])
                zr[k, pl.ds(base + r, SUBLANES, stride=seg), :] = zr[k, pl.ds(base + r, SUBLANES, stride=seg), :] + pr
                zi[k, pl.ds(base + r, SUBLANES, stride=seg), :] = zi[k, pl.ds(base + r, SUBLANES, stride=seg), :] + pi
            return 0

        lax.fori_loop(0, seg, second_pass, 0)


def _slabs_to_cols(ref, k0, n):
    return jnp.concatenate([ref[k0 + q] for q in range(n)], axis=1)


def _window_sum(ext, doublings, forward):
    rows = ext.shape[0]
    s = ext
    for k in range(doublings):
        s = s + pltpu.roll(s, (1 << k) if forward else rows - (1 << k), 0)
    return s


def _mix_fwd(h1, g_pre, g_so, g_po, g_post, dskip, pscale, win, wout, bbre, bbim, ccre, ccim, wgv, wgg, pw, tabs, pows, name):
    T, D = h1.shape
    W = D // 2
    tm = MIX_FWD_SUBTILES * MIX_TILE
    seg = MIX_TILE // SUBLANES
    n_slabs = tabs.shape[1]
    nch, cch, sch = bbre.shape
    spc = sch // LANES
    pg = W // len(POOL_WINDOWS)

    def body(h_ref, gpre_ref, gso_ref, gpo_ref, gpost_ref, dskip_ref, pscale_ref, win_ref, wout_ref, bbre_ref, bbim_ref, ccre_ref, ccim_ref, wgv_ref, wgg_ref, pw_ref, tabs_ref, pows_ref,
             proj_ref, xr_ref, xi_ref, y_ref, pooled_ref, mixed_ref, h2_ref, n2_ref, cat_ref, car_r, car_i, halo):
        i = pl.program_id(0)

        @pl.when(i == 0)
        def _():
            car_r[...] = jnp.zeros_like(car_r)
            car_i[...] = jnp.zeros_like(car_i)
            halo[...] = jnp.zeros_like(halo)

        hh = h_ref[...]
        n2 = (hh * _rms_stat(hh) * gpre_ref[...]).astype(MXU_DTYPE)
        n2_ref[...] = n2.astype(n2_ref.dtype)
        proj = _mm(n2, win_ref[...])
        proj_ref[...] = proj
        us, up = proj[:, :W], proj[:, W:]

        for c in range(nch):
            uc = us[:, c * cch : (c + 1) * cch].astype(MXU_DTYPE)
            bur, bui = _mm(uc, bbre_ref[c]), _mm(uc, bbim_ref[c])
            for q in range(spc):
                xr_ref[c * spc + q] = bur[:, q * LANES : (q + 1) * LANES]
                xi_ref[c * spc + q] = bui[:, q * LANES : (q + 1) * LANES]
        for sub in range(MIX_FWD_SUBTILES):
            _scan_inplace(xr_ref, xi_ref, tabs_ref, pows_ref, car_r, car_i, seg, reverse=False, base=sub * MIX_TILE)
        ys = []
        for c in range(nch):
            ys.append(_mm(_slabs_to_cols(xr_ref, c * spc, spc), ccre_ref[c]) - _mm(_slabs_to_cols(xi_ref, c * spc, spc), ccim_ref[c]))
        y = jnp.concatenate(ys, axis=1) + dskip_ref[...] * us
        y_ref[...] = y
        ge = _gelu(y).astype(MXU_DTYPE)
        zv = jnp.concatenate([_mm(ge[:, c * cch : (c + 1) * cch], wgv_ref[c]) for c in range(nch)], axis=1)
        zg = jnp.concatenate([_mm(ge[:, c * cch : (c + 1) * cch], wgg_ref[c]) for c in range(nch)], axis=1)
        out = zv * _sigmoid(zg)
        cat_s = out * _rms_stat(out) * gso_ref[...]

        ext = jnp.concatenate([halo[...], up], axis=0)
        halo[...] = up[tm - POOL_HALO :, :]
        t1 = (i * tm + 1 + lax.broadcasted_iota(jnp.int32, (tm, pg), 0)).astype(F32)
        pooled, pms = [], []
        for g, w in enumerate(POOL_WINDOWS):
            col = ext[:, g * pg : (g + 1) * pg]
            win_sum = _window_sum(col, g + 1, True)[POOL_HALO:, :]
            pooled_g = win_sum / jnp.minimum(t1, float(w)) - up[:, g * pg : (g + 1) * pg]
            pooled.append(pooled_g)
            pms.append(_mm(pooled_g, pw_ref[g]))
        pooled_ref[...] = jnp.concatenate(pooled, axis=1)
        yp = jnp.concatenate(pms, axis=1) * pscale_ref[...]
        cat_p = yp * _rms_stat(yp) * gpo_ref[...]

        cat = jnp.concatenate([cat_s, cat_p], axis=1).astype(MXU_DTYPE)
        cat_ref[...] = cat.astype(cat_ref.dtype)
        mixed = _mm(cat, wout_ref[...])
        mixed_ref[...] = mixed
        h2_ref[...] = hh + mixed * _rms_stat(mixed) * gpost_ref[...]

    tok = lambda cols, dt=F32: jax.ShapeDtypeStruct((T, cols), dt)
    slab_spec = pl.BlockSpec((n_slabs, tm, LANES), lambda i: (0, i, 0))
    operands = (h1, g_pre, g_so, g_po, g_post, dskip, pscale, win, wout, bbre, bbim, ccre, ccim, wgv, wgg, pw, tabs, pows)
    outs, _ = _pallas(
        body,
        name=name,
        grid=(T // tm,),
        out_shape=[tok(D), jax.ShapeDtypeStruct((n_slabs, T, LANES), F32), jax.ShapeDtypeStruct((n_slabs, T, LANES), F32), tok(W), tok(W), tok(D), tok(D), tok(D, MXU_DTYPE), tok(D, MXU_DTYPE)],
        in_specs=[_row_spec(tm, D)] + [_full_spec(o.shape, single=True) for o in operands[1:]],
        out_specs=[_row_spec(tm, D), slab_spec, slab_spec, _row_spec(tm, W), _row_spec(tm, W), _row_spec(tm, D), _row_spec(tm, D), _row_spec(tm, D), _row_spec(tm, D)],
        scratch_shapes=[pltpu.VMEM((n_slabs, SUBLANES, LANES), F32), pltpu.VMEM((n_slabs, SUBLANES, LANES), F32), pltpu.VMEM((POOL_HALO, W), F32)],
        operands=operands,
    )
    return outs


def _mix_bwd_heads(dh2, mixed, y, pooled, proj, g_so, g_po, g_post, pscale, wout, wgv, wgg, pw, name, exchange=None):
    T, D = dh2.shape
    W = D // 2
    tm = FFN_TILE
    nch, cch, _ = wgv.shape
    ng, pg, _ = pw.shape

    def body(dh2_ref, mixed_ref, y_ref, pooled_ref, us_ref, gso_ref, gpo_ref, gpost_ref, pscale_ref, wout_ref, wgv_ref, wgg_ref, pw_ref,
             dy_ref, dpooled_ref, dmixed_ref, dgpost_ref, dgso_ref, dgpo_ref, dd_ref, dscale_ref, dwgv_ref, dwgg_ref, dpw_ref):
        first = pl.program_id(0) == 0
        dmixed, dgpost = _rms_bwd(mixed_ref[...], gpost_ref[...], dh2_ref[...])
        _acc(dgpost_ref, dgpost, first)
        dmb = dmixed.astype(MXU_DTYPE)
        dmixed_ref[...] = dmb.astype(dmixed_ref.dtype)
        dcat = _mm_nt(dmb, wout_ref[...])
        dcs, dcp = dcat[:, :W], dcat[:, W:]

        y = y_ref[...]
        ge = _gelu(y).astype(MXU_DTYPE)
        zv = jnp.concatenate([_mm(ge[:, c * cch : (c + 1) * cch], wgv_ref[c]) for c in range(nch)], axis=1)
        zg = jnp.concatenate([_mm(ge[:, c * cch : (c + 1) * cch], wgg_ref[c]) for c in range(nch)], axis=1)
        sg = _sigmoid(zg)
        dout, dgso = _rms_bwd(zv * sg, gso_ref[...], dcs)
        _acc(dgso_ref, dgso, first)
        dzv = (dout * sg).astype(MXU_DTYPE)
        dzg = (dout * zv * sg * (1.0 - sg)).astype(MXU_DTYPE)
        dges = []
        for c in range(nch):
            cs = slice(c * cch, (c + 1) * cch)
            dges.append(_mm_nt(dzv[:, cs], wgv_ref[c]) + _mm_nt(dzg[:, cs], wgg_ref[c]))
            _acc(dwgv_ref.at[c], _mm_tn(ge[:, cs], dzv[:, cs]), first)
            _acc(dwgg_ref.at[c], _mm_tn(ge[:, cs], dzg[:, cs]), first)
        dy = jnp.concatenate(dges, axis=1) * _gelu_grad(y)
        dy_ref[...] = dy
        _acc(dd_ref, jnp.sum(dy * us_ref[...], axis=0, keepdims=True), first)

        pooled_b = pooled_ref[...].astype(MXU_DTYPE)
        pm = jnp.concatenate([_mm(pooled_b[:, g * pg : (g + 1) * pg], pw_ref[g]) for g in range(ng)], axis=1)
        dyp, dgpo = _rms_bwd(pm * pscale_ref[...], gpo_ref[...], dcp)
        _acc(dgpo_ref, dgpo, first)
        _acc(dscale_ref, jnp.sum(dyp * pm, axis=0, keepdims=True), first)
        dpm = (dyp * pscale_ref[...]).astype(MXU_DTYPE)
        dps = []
        for g in range(ng):
            gs = slice(g * pg, (g + 1) * pg)
            dps.append(_mm_nt(dpm[:, gs], pw_ref[g]))
            _acc(dpw_ref.at[g], _mm_tn(pooled_b[:, gs], dpm[:, gs]), first)
        dpooled_ref[...] = jnp.concatenate(dps, axis=1)

    vec = lambda n: jax.ShapeDtypeStruct((1, n), F32)
    operands = (dh2, mixed, y, pooled, proj, g_so, g_po, g_post, pscale, wout, wgv, wgg, pw)
    return _pallas(
        body,
        name=name,
        grid=(T // tm,),
        out_shape=[jax.ShapeDtypeStruct((T, W), F32), jax.ShapeDtypeStruct((T, W), F32), jax.ShapeDtypeStruct((T, D), MXU_DTYPE), vec(D), vec(W), vec(W), vec(W), vec(W),
                   jax.ShapeDtypeStruct(wgv.shape, F32), jax.ShapeDtypeStruct(wgg.shape, F32), jax.ShapeDtypeStruct(pw.shape, F32)],
        in_specs=[_row_spec(tm, D), _row_spec(tm, D), _row_spec(tm, W), _row_spec(tm, W), _row_spec(tm, W)] + [_full_spec(o.shape) for o in operands[5:]],
        out_specs=[_row_spec(tm, W), _row_spec(tm, W), _row_spec(tm, D), _full_spec((1, D)), _full_spec((1, W)), _full_spec((1, W)), _full_spec((1, W)), _full_spec((1, W)),
                   _full_spec(wgv.shape), _full_spec(wgg.shape), _full_spec(pw.shape)],
        operands=operands,
        exchange=exchange,
    )


def _mix_bwd_scan(dy, dpooled, xr, xi, proj, h1, dh2, g_pre, dskip, win, bbre, bbim, ccre, ccim, tabs, pows, name, exchange=None):
    T, D = h1.shape
    W = D // 2
    tm = MIX_TILE
    seg = tm // SUBLANES
    nt = T // tm
    n_slabs = tabs.shape[1]
    nch, cch, sch = bbre.shape
    spc = sch // LANES
    pg = W // len(POOL_WINDOWS)
    blocks_per_tile = tm // SUBLANES
    assert nt % 2 == 0, "the weight gradients are accumulated over pairs of tiles"

    def body(dy_ref, dp_ref, xr_ref, xi_ref, xpr_ref, xpi_ref, proj_ref, h_ref, dh2_ref, gpre_ref, dskip_ref, win_ref, bbre_ref, bbim_ref, ccre_ref, ccim_ref, tabs_ref, pows_ref,
             dh1_ref, dproj_ref, dgpre_ref, dccre_ref, dccim_ref, dbbre_ref, dbbim_ref, dar_ref, dai_ref, lr, li, car_r, car_i, halo, kept):
        i = pl.program_id(0)
        first = i == 0
        tile = nt - 1 - i
        row = lax.broadcasted_iota(jnp.int32, (SUBLANES, LANES), 0)
        second = (i % 2) == 1

        def pair_products(slot, operands, pairs):
            @pl.when(jnp.logical_not(second))
            def _():
                for j, a in enumerate(operands):
                    kept[slot + j, :, 0 : a.shape[1]] = a

            @pl.when(second)
            def _():
                both = [jnp.concatenate([kept[slot + j, :, 0 : a.shape[1]], a], axis=0).astype(MXU_DTYPE) for j, a in enumerate(operands)]
                for out_ref, left, right, negate in pairs:
                    prod = _mm_tn(both[left], both[right])
                    _acc(out_ref, -prod if negate else prod, i == 1)

        @pl.when(first)
        def _():
            car_r[...] = jnp.zeros_like(car_r)
            car_i[...] = jnp.zeros_like(car_i)
            halo[...] = jnp.zeros_like(halo)
            dar_ref[...] = jnp.zeros_like(dar_ref)
            dai_ref[...] = jnp.zeros_like(dai_ref)

        dy = dy_ref[...]
        for c in range(nch):
            dyc = dy[:, c * cch : (c + 1) * cch]
            gr, gi = _mm_nt(dyc, ccre_ref[c]), _mm_nt(dyc, ccim_ref[c])
            for q in range(spc):
                lr[c * spc + q] = gr[:, q * LANES : (q + 1) * LANES]
                li[c * spc + q] = -gi[:, q * LANES : (q + 1) * LANES]
            pair_products(3 * c, [_slabs_to_cols(xr_ref, c * spc, spc), _slabs_to_cols(xi_ref, c * spc, spc), dyc], [(dccre_ref.at[c], 0, 2, False), (dccim_ref.at[c], 1, 2, True)])
        _scan_inplace(lr, li, tabs_ref, pows_ref, car_r, car_i, seg, reverse=True)

        for k0 in range(0, n_slabs, SLAB_GROUP):
            slabs = range(k0, min(k0 + SLAB_GROUP, n_slabs))
            init = []
            for k in slabs:
                prev_r = jnp.where(tile > 0, jnp.broadcast_to(xpr_ref[k, SUBLANES - 1 : SUBLANES, :], (SUBLANES, LANES)), 0.0)
                prev_i = jnp.where(tile > 0, jnp.broadcast_to(xpi_ref[k, SUBLANES - 1 : SUBLANES, :], (SUBLANES, LANES)), 0.0)
                x0r = jnp.where(row >= 1, pltpu.roll(xr_ref[k, pl.ds(seg - 1, SUBLANES, stride=seg), :], 1, 0), prev_r)
                x0i = jnp.where(row >= 1, pltpu.roll(xi_ref[k, pl.ds(seg - 1, SUBLANES, stride=seg), :], 1, 0), prev_i)
                l0r, l0i = lr[k, pl.ds(0, SUBLANES, stride=seg), :], li[k, pl.ds(0, SUBLANES, stride=seg), :]
                init += [l0r * x0r + l0i * x0i, l0i * x0r - l0r * x0i]

            def step(r, acc, slabs=slabs):
                out = []
                for q, k in enumerate(slabs):
                    pr_, pi_ = xr_ref[k, pl.ds(r - 1, SUBLANES, stride=seg), :], xi_ref[k, pl.ds(r - 1, SUBLANES, stride=seg), :]
                    lr_, li_ = lr[k, pl.ds(r, SUBLANES, stride=seg), :], li[k, pl.ds(r, SUBLANES, stride=seg), :]
                    out += [acc[2 * q] + lr_ * pr_ + li_ * pi_, acc[2 * q + 1] + li_ * pr_ - lr_ * pi_]
                return tuple(out)

            sums = lax.fori_loop(1, seg, step, tuple(init))
            for q, k in enumerate(slabs):
                dar_ref[k] += sums[2 * q]
                dai_ref[k] += sums[2 * q + 1]

        us = proj_ref[:, :W]
        dus = []
        for c in range(nch):
            lrc, lic = _slabs_to_cols(lr, c * spc, spc), _slabs_to_cols(li, c * spc, spc)
            pair_products(3 * nch + 3 * c, [us[:, c * cch : (c + 1) * cch], lrc, lic], [(dbbre_ref.at[c], 0, 1, False), (dbbim_ref.at[c], 0, 2, False)])
            dus.append(_mm_nt(lrc, bbre_ref[c]) + _mm_nt(lic, bbim_ref[c]))
        du_s = jnp.concatenate(dus, axis=1) + dskip_ref[...] * dy

        dp = dp_ref[...]
        t1 = (tile * tm + 1 + lax.broadcasted_iota(jnp.int32, (tm, pg), 0)).astype(F32)
        dups, heads = [], []
        for g, w in enumerate(POOL_WINDOWS):
            dpg = dp[:, g * pg : (g + 1) * pg]
            qg = dpg / jnp.minimum(t1, float(w))
            ext = jnp.concatenate([qg, halo[:, g * pg : (g + 1) * pg]], axis=0)
            dups.append(_window_sum(ext, g + 1, False)[:tm, :] - dpg)
            heads.append(qg[:POOL_HALO, :])
        halo[...] = jnp.concatenate(heads, axis=1)
        dproj = jnp.concatenate([du_s] + dups, axis=1).astype(MXU_DTYPE)
        dproj_ref[...] = dproj.astype(dproj_ref.dtype)
        dx, dg = _rms_bwd(h_ref[...], gpre_ref[...], _mm_nt(dproj, win_ref[...]))
        _acc(dgpre_ref, dg, first)
        dh1_ref[...] = dh2_ref[...] + dx

    rev = lambda cols: _row_spec(tm, cols, rev_n=nt)
    slab_spec = pl.BlockSpec((n_slabs, tm, LANES), lambda i: (0, nt - 1 - i, 0))
    prev_spec = pl.BlockSpec((n_slabs, SUBLANES, LANES), lambda i: (0, jnp.maximum((nt - 1 - i) * blocks_per_tile - 1, 0), 0))
    consts = (g_pre, dskip, win, bbre, bbim, ccre, ccim, tabs, pows)
    return _pallas(
        body,
        name=name,
        grid=(nt,),
        out_shape=[jax.ShapeDtypeStruct((T, D), F32), jax.ShapeDtypeStruct((T, D), MXU_DTYPE), jax.ShapeDtypeStruct((1, D), F32),
                   jax.ShapeDtypeStruct(ccre.shape, F32), jax.ShapeDtypeStruct(ccim.shape, F32), jax.ShapeDtypeStruct(bbre.shape, F32), jax.ShapeDtypeStruct(bbim.shape, F32),
                   jax.ShapeDtypeStruct((n_slabs, SUBLANES, LANES), F32), jax.ShapeDtypeStruct((n_slabs, SUBLANES, LANES), F32)],
        in_specs=[rev(W), rev(W), slab_spec, slab_spec, prev_spec, prev_spec, rev(D), rev(D), rev(D)] + [_full_spec(o.shape) for o in consts],
        out_specs=[rev(D), rev(D), _full_spec((1, D)), _full_spec(ccre.shape), _full_spec(ccim.shape), _full_spec(bbre.shape), _full_spec(bbim.shape),
                   _full_spec((n_slabs, SUBLANES, LANES)), _full_spec((n_slabs, SUBLANES, LANES))],
        scratch_shapes=[pltpu.VMEM((n_slabs, tm, LANES), F32), pltpu.VMEM((n_slabs, tm, LANES), F32), pltpu.VMEM((n_slabs, SUBLANES, LANES), F32), pltpu.VMEM((n_slabs, SUBLANES, LANES), F32),
                        pltpu.VMEM((POOL_HALO, W), F32), pltpu.VMEM((6 * nch, tm, sch), F32)],
        operands=(dy, dpooled, xr, xi, xr, xi, proj, h1, dh2, *consts),
        exchange=exchange,
    )


def _discretize(lam_re, lam_im, log_dt, b_re, b_im):
    dt = jnp.exp(log_dt)[:, None]
    decay = jnp.exp(lam_re * dt)
    ang = lam_im * dt
    a_re, a_im = decay * jnp.cos(ang), decay * jnp.sin(ang)
    nr = a_re - 1.0
    den = lam_re * lam_re + lam_im * lam_im
    q_re = (nr * lam_re + a_im * lam_im) / den
    q_im = (a_im * lam_re - nr * lam_im) / den
    bb_re = q_re[..., None] * b_re - q_im[..., None] * b_im
    bb_im = q_re[..., None] * b_im + q_im[..., None] * b_re
    return a_re, a_im, bb_re, bb_im


GROUPS_PER_CHUNK = 16


def _block_diag(w, rows_first):
    G = w.shape[0]
    nch = G // GROUPS_PER_CHUNK
    if not rows_first:
        w = jnp.swapaxes(w, 1, 2)
    p, q = w.shape[1], w.shape[2]
    eye = jnp.eye(GROUPS_PER_CHUNK, dtype=w.dtype)
    out = jnp.einsum("cgpq,gk->cgpkq", w.reshape(nch, GROUPS_PER_CHUNK, p, q), eye)
    return out.reshape(nch, GROUPS_PER_CHUNK * p, GROUPS_PER_CHUNK * q)


def _block_diag_extract(m, p, q, rows_first):
    nch = m.shape[0]
    eye = jnp.eye(GROUPS_PER_CHUNK, dtype=m.dtype)
    out = jnp.einsum("cgpkq,gk->cgpq", m.reshape(nch, GROUPS_PER_CHUNK, p, GROUPS_PER_CHUNK, q), eye).reshape(nch * GROUPS_PER_CHUNK, p, q)
    return out if rows_first else jnp.swapaxes(out, 1, 2)


def _cmul(ar, ai, br, bi):
    return ar * br - ai * bi, ar * bi + ai * br


def _powers(ar, ai, count):
    pr, pi = ar[None], ai[None]
    while pr.shape[0] < count:
        nr, ni = _cmul(pr, pi, pr[-1][None], pi[-1][None])
        pr, pi = jnp.concatenate([pr, nr]), jnp.concatenate([pi, ni])
    return pr[:count], pi[:count]


def _scan_tables(a_re, a_im, seg):
    n = a_re.size
    ns = n // LANES
    ar, ai = a_re.reshape(n), a_im.reshape(n)
    pr, pi = _powers(ar, ai, seg)
    jr, ji = _powers(pr[-1], pi[-1], SUBLANES)

    def bcast(v):
        return jnp.broadcast_to(v.reshape(ns, 1, LANES), (ns, SUBLANES, LANES))

    def per_sublane(vs):
        return jnp.transpose(vs.reshape(SUBLANES, ns, LANES), (1, 0, 2))

    tabs = jnp.stack([bcast(ar), bcast(ai), bcast(jr[0]), bcast(ji[0]), bcast(jr[1]), bcast(ji[1]), bcast(jr[3]), bcast(ji[3]),
                      per_sublane(jr), per_sublane(ji), per_sublane(jr[::-1]), per_sublane(ji[::-1])])

    def rows(vs):
        return jnp.broadcast_to(jnp.transpose(vs.reshape(seg, ns, 1, LANES), (1, 0, 2, 3)), (ns, seg, SUBLANES, LANES))

    return tabs, jnp.stack([rows(pr), rows(pi)])


SMALL = ("ffn1_pre_norm", "ffn1_post_norm", "mix_pre_norm", "mix_post_norm", "ssm_lambda_re", "ssm_lambda_im", "ssm_log_dt", "ssm_b_re", "ssm_b_im", "ssm_c_re", "ssm_c_im",
         "ssm_d", "ssm_w_glu", "pool_w", "pool_scale", "ssm_out_norm", "pool_out_norm", "ffn2_pre_norm", "ffn2_post_norm")
BIG = ("ffn1_w_gate", "ffn1_w_up", "ffn1_w_down", "w_in", "w_out", "ffn2_w_gate", "ffn2_w_up", "ffn2_w_down")
ORDER = ("meta_tokens", "ffn1_pre_norm", "ffn1_post_norm", "ffn1_w_gate", "ffn1_w_up", "ffn1_w_down", "mix_pre_norm", "mix_post_norm", "w_in", "ssm_lambda_re", "ssm_lambda_im",
         "ssm_log_dt", "ssm_b_re", "ssm_b_im", "ssm_c_re", "ssm_c_im", "ssm_d", "ssm_w_glu", "pool_w", "pool_scale", "ssm_out_norm", "pool_out_norm", "w_out", "ffn2_pre_norm",
         "ffn2_post_norm", "ffn2_w_gate", "ffn2_w_up", "ffn2_w_down")
PACK_ROWS = SUBLANES * 8
def _pack(arrays, rows):
    flat = jnp.concatenate([a.reshape(-1) for a in arrays])
    return jnp.pad(flat, (0, rows * LANES - flat.size)).reshape(rows, LANES)


def _unpack(packed, shapes):
    flat = packed.reshape(-1)
    out, off = [], 0
    for s in shapes:
        n = math.prod(s)
        out.append(flat[off : off + n].reshape(s))
        off += n
    return out


def _step(p, x, loss_target, m, v):
    D = x.shape[-1]
    chip = (2 * lax.axis_index("x") + lax.axis_index("y")).astype(jnp.int32)
    place = jnp.stack([chip, lax.axis_index("c").astype(jnp.int32)])

    def gather_buffer(w):
        own = w.reshape(1, 2, w.shape[0] // 2, w.shape[1])
        return lax.dynamic_update_slice(lax.empty((4,) + own.shape[1:], own.dtype), own, (chip, 0, 0, 0))

    def rows_of(n, a):
        return jnp.swapaxes(a[0], 0, 1) if n.endswith(("w_gate", "w_up")) else a[0]

    def rows_back(n, a):
        return (jnp.swapaxes(a, 0, 1) if n.endswith(("w_gate", "w_up")) else a)[None]

    def grad_view(g):
        return g.reshape(4, 2, g.shape[0] // 8, g.shape[1])

    def reduce_sum(got_sibling, views, tag):
        sums = [_add_own_half(v_, g_, place, f"{tag}_add_sibling_{k}") for k, (v_, g_) in enumerate(zip(views, got_sibling))]
        return [s[0] for s in sums], [s[1] for s in sums]

    def reduce_halves(parts, got_chips, tag):
        return [_add_chips(p_, g_, place, f"{tag}_add_chips_{k}") for k, (p_, g_) in enumerate(zip(parts, got_chips))]

    first_names = ("ffn1_w_gate", "ffn1_w_up", "ffn1_w_down")
    later_names = ("w_in", "w_out", "ffn2_w_gate", "ffn2_w_up", "ffn2_w_down")
    bufs = {n: gather_buffer(rows_of(n, p[n]).astype(MXU_DTYPE)) for n in BIG}
    gathered_weight = lambda g_: g_.reshape(-1, g_.shape[-1])
    got = _exchange_call(_Gather([bufs[n] for n in first_names] + [gather_buffer(p["meta_tokens"])]), "gather_first")
    full = {n: gathered_weight(g_) for n, g_ in zip(first_names, got)}
    meta = jnp.transpose(got[-1].reshape(4, N_META, -1), (1, 0, 2)).reshape(N_META, D)

    vec = lambda n: p[n].reshape(1, -1)
    G, N, H = p["ssm_b_re"].shape[1:]
    a_re, a_im, bb_re, bb_im = _discretize(p["ssm_lambda_re"][0], p["ssm_lambda_im"][0], p["ssm_log_dt"][0], p["ssm_b_re"][0], p["ssm_b_im"][0])
    tabs, pows = _scan_tables(a_re, a_im, MIX_TILE // SUBLANES)
    bf = lambda a: a.astype(MXU_DTYPE)
    bbre, bbim = bf(_block_diag(bb_re, False)), bf(_block_diag(bb_im, False))
    ccre, ccim = bf(_block_diag(p["ssm_c_re"][0], False)), bf(_block_diag(p["ssm_c_im"][0], False))
    wgv, wgg = bf(_block_diag(p["ssm_w_glu"][0][:, :, :H], True)), bf(_block_diag(p["ssm_w_glu"][0][:, :, H:], True))
    pw = bf(p["pool_w"][0])

    n_ffn_steps = (x.shape[1] + N_META) // FFN_TILE
    (h1, f1, ga1, si1, s1, n1), got = _ffn_fwd(
        x[0], vec("ffn1_pre_norm"), vec("ffn1_post_norm"), full["ffn1_w_gate"], full["ffn1_w_up"], full["ffn1_w_down"], "ffn1_fwd",
        exchange=_Gather([bufs[n] for n in later_names], mid_step=(3 * n_ffn_steps) // 4), meta=meta,
    )
    full.update({n: gathered_weight(g_) for n, g_ in zip(later_names, got)})
    proj, xr, xim, y, pooled, mixed, h2, n2, cat = _mix_fwd(
        h1, vec("mix_pre_norm"), vec("ssm_out_norm"), vec("pool_out_norm"), vec("mix_post_norm"), vec("ssm_d"), vec("pool_scale"), full["w_in"], full["w_out"],
        bbre, bbim, ccre, ccim, wgv, wgg, pw, tabs, pows, "mix_fwd",
    )
    (dh3, sq, f2, ga2, si2, s2, n3), _ = _ffn_fwd(
        h2, vec("ffn2_pre_norm"), vec("ffn2_post_norm"), full["ffn2_w_gate"], full["ffn2_w_up"], full["ffn2_w_down"], "ffn2_fwd", target=loss_target[0]
    )
    loss = lax.psum(0.5 * jnp.sum(sq) / D, ("x", "y", "c"))

    g, shared = {}, {}
    ffn_names = lambda tag: (tag + "_w_gate", tag + "_w_up", tag + "_w_down")

    da, db, df, dh2, g["ffn2_post_norm"], g["ffn2_pre_norm"] = _ffn_bwd(
        dh3, f2, ga2, si2, h2, vec("ffn2_post_norm"), vec("ffn2_pre_norm"), full["ffn2_w_gate"], full["ffn2_w_up"], full["ffn2_w_down"], "ffn2_bwd"
    )
    views2 = [
        grad_view(_tn_matmul(da, n3, "ffn2_dw_gate")[0]),
        grad_view(_tn_matmul(db, n3, "ffn2_dw_up")[0]),
        grad_view(_tn_matmul(s2, df, "ffn2_dw_down")[0]),
    ]
    (dy, dpooled, dmixed, g["mix_post_norm"], g["ssm_out_norm"], g["pool_out_norm"], g["ssm_d"], g["pool_scale"], dwgv, dwgg, g["pool_w"]), got = _mix_bwd_heads(
        dh2, mixed, y, pooled, proj, vec("ssm_out_norm"), vec("pool_out_norm"), vec("mix_post_norm"), vec("pool_scale"), full["w_out"], wgv, wgg, pw, "mix_bwd_heads",
        exchange=_SiblingScatter(views2),
    )
    parts2, wire2 = reduce_sum(got, views2, "ffn2")
    (dh1, dproj, g["mix_pre_norm"], dccre, dccim, dbbre, dbbim, dar, dai), got = _mix_bwd_scan(
        dy, dpooled, xr, xim, proj, h1, dh2, vec("mix_pre_norm"), vec("ssm_d"), full["w_in"], bbre, bbim, ccre, ccim, tabs, pows, "mix_bwd_scan",
        exchange=_ChipScatter(wire2),
    )
    halves2 = reduce_halves(parts2, got, "ffn2")
    dw_in, got = _tn_matmul(n2, dproj, "dw_in", exchange=_SiblingShare(halves2))
    shared.update(zip(ffn_names("ffn2"), got))
    dw_out, _ = _tn_matmul(cat, dmixed, "dw_out")
    views_m = [grad_view(dw_in), grad_view(dw_out)]

    (da, db, df, g["ffn1_post_norm"]), got = _ffn_bwd_down(dh1, f1, ga1, si1, vec("ffn1_post_norm"), full["ffn1_w_down"], "ffn1_bwd_down", exchange=_SiblingScatter(views_m))
    parts_m, wire_m = reduce_sum(got, views_m, "mix")
    dw_down, got = _tn_matmul(s1, df, "ffn1_dw_down", exchange=_ChipScatter(wire_m))
    halves_m = reduce_halves(parts_m, got, "mix")
    views_d = [grad_view(dw_down)]
    ex = _Group([_SiblingShare(halves_m), _SiblingScatter(views_d)])
    dw_gate, got = _tn_matmul(da, n1, "ffn1_dw_gate", exchange=ex)
    got_m, got_d = ex.split(got)
    shared.update(zip(("w_in", "w_out"), got_m))
    parts_d, wire_d = reduce_sum(got_d, views_d, "ffn1_down")
    views_g = [grad_view(dw_gate)]
    ex = _Group([_ChipScatter(wire_d), _SiblingScatter(views_g)])
    dw_up, got = _tn_matmul(db, n1, "ffn1_dw_up", exchange=ex)
    got_d, got_g = ex.split(got)
    halves_d = reduce_halves(parts_d, got_d, "ffn1_down")
    parts_g, wire_g = reduce_sum(got_g, views_g, "ffn1_gate")
    views_u = [grad_view(dw_up)]
    ex = _Group([_SiblingShare(halves_d), _ChipScatter(wire_g), _SiblingScatter(views_u)])
    (grad_x, d_meta, g["ffn1_pre_norm"]), got = _ffn_bwd_up(da, db, x[0], meta, dh1, vec("ffn1_pre_norm"), full["ffn1_w_gate"], full["ffn1_w_up"], "ffn1_bwd_up", exchange=ex)
    got_d, got_g, got_u = ex.split(got)
    shared["ffn1_w_down"] = got_d[0]
    halves_g = reduce_halves(parts_g, got_g, "ffn1_gate")
    parts_u, wire_u = reduce_sum(got_u, views_u, "ffn1_up")
    grad_x = grad_x[None]

    g["ssm_c_re"] = _block_diag_extract(dccre, N, H, False)
    g["ssm_c_im"] = _block_diag_extract(dccim, N, H, False)
    g["ssm_w_glu"] = jnp.concatenate([_block_diag_extract(dwgv, H, H, True), _block_diag_extract(dwgg, H, H, True)], axis=-1)
    d_a_re, d_a_im = jnp.sum(dar, axis=1).reshape(G, N), jnp.sum(dai, axis=1).reshape(G, N)
    _, pull = jax.vjp(_discretize, p["ssm_lambda_re"][0], p["ssm_lambda_im"][0], p["ssm_log_dt"][0], p["ssm_b_re"][0], p["ssm_b_im"][0])
    g["ssm_lambda_re"], g["ssm_lambda_im"], g["ssm_log_dt"], g["ssm_b_re"], g["ssm_b_im"] = pull(
        (d_a_re, d_a_im, _block_diag_extract(dbbre, H, N, False), _block_diag_extract(dbbim, H, N, False))
    )

    small_shapes = [p[n].shape for n in SMALL] + [(N_META, D)]
    small_size = sum(math.prod(s) for s in small_shapes)
    rows = -(-small_size // (LANES * PACK_ROWS)) * PACK_ROWS
    views_s = [_pack([g[n] for n in SMALL] + [d_meta], rows).reshape(4, 2, rows // 8, LANES)]
    parts_s, wire_s = reduce_sum(_exchange_call(_SiblingScatter(views_s), "small_reduce_sibling"), views_s, "small")
    ex = _Group([_ChipScatter(wire_u + wire_s), _SiblingShare(halves_g)])
    got_c, got_g = ex.split(_exchange_call(ex, "tail_reduce_chips"))
    shared["ffn1_w_gate"] = got_g[0]
    got = _exchange_call(_SiblingShare(reduce_halves(parts_u + parts_s, got_c, "tail")), "tail_reduce_share")
    shared["ffn1_w_up"] = got[0]
    small_buf = lax.dynamic_update_slice(lax.empty((4,) + got[1].shape, F32), got[1][None], (chip, 0, 0, 0))
    small_all = _exchange_call(_Gather([small_buf]), "gather_small")[0].reshape(rows, LANES)
    grads = dict(zip(SMALL + ("meta_full",), _unpack(small_all, small_shapes)))
    grads["meta_tokens"] = lax.dynamic_slice_in_dim(grads.pop("meta_full"), chip * (D // 4), D // 4, axis=1)
    delta, new_m, new_v = {}, {}, {}
    for n in BIG:
        g_rows = shared[n].reshape(-1, shared[n].shape[-1])
        outs = _adamw(rows_of(n, p[n]), g_rows, rows_of(n, m[n]), rows_of(n, v[n]), "adamw_" + n)
        grads[n], delta[n], new_m[n], new_v[n] = (rows_back(n, a) for a in (g_rows, *outs))
    delta["meta_tokens"], new_m["meta_tokens"], new_v["meta_tokens"] = _adamw(p["meta_tokens"], grads["meta_tokens"], m["meta_tokens"], v["meta_tokens"], "adamw_meta_tokens")

    def as_2d(n, a):
        a = a.reshape(p[n].shape)[0]
        if n in ("ssm_b_re", "ssm_b_im"):
            a = jnp.swapaxes(a, 1, 2)
        return a.reshape(-1, a.shape[-1])

    def from_2d(n, a):
        if n in ("ssm_b_re", "ssm_b_im"):
            g_, n_, h_ = p[n].shape[1:]
            return jnp.swapaxes(a.reshape(g_, h_, n_), 1, 2)[None]
        return a.reshape(p[n].shape)

    outs = _adamw_many(*[[as_2d(n, t[n]) for n in SMALL] for t in (p, grads, m, v)], "adamw_small")
    for out, arrays in zip((delta, new_m, new_v), outs):
        out.update({n: from_2d(n, a) for n, a in zip(SMALL, arrays)})

    return (loss, grad_x, *[grads[n] for n in ORDER], *[delta[n] for n in ORDER], *[new_m[n] for n in ORDER], *[new_v[n] for n in ORDER])


def kernel(x, meta_tokens, ffn1_pre_norm, ffn1_post_norm, ffn1_w_gate, ffn1_w_up, ffn1_w_down, mix_pre_norm, mix_post_norm, w_in, ssm_lambda_re, ssm_lambda_im, ssm_log_dt, ssm_b_re, ssm_b_im, ssm_c_re, ssm_c_im, ssm_d, ssm_w_glu, pool_w, pool_scale, ssm_out_norm, pool_out_norm, w_out, ffn2_pre_norm, ffn2_post_norm, ffn2_w_gate, ffn2_w_up, ffn2_w_down, loss_target, m_meta_tokens, m_ffn1_pre_norm, m_ffn1_post_norm, m_ffn1_w_gate, m_ffn1_w_up, m_ffn1_w_down, m_mix_pre_norm, m_mix_post_norm, m_w_in, m_ssm_lambda_re, m_ssm_lambda_im, m_ssm_log_dt, m_ssm_b_re, m_ssm_b_im, m_ssm_c_re, m_ssm_c_im, m_ssm_d, m_ssm_w_glu, m_pool_w, m_pool_scale, m_ssm_out_norm, m_pool_out_norm, m_w_out, m_ffn2_pre_norm, m_ffn2_post_norm, m_ffn2_w_gate, m_ffn2_w_up, m_ffn2_w_down, v_meta_tokens, v_ffn1_pre_norm, v_ffn1_post_norm, v_ffn1_w_gate, v_ffn1_w_up, v_ffn1_w_down, v_mix_pre_norm, v_mix_post_norm, v_w_in, v_ssm_lambda_re, v_ssm_lambda_im, v_ssm_log_dt, v_ssm_b_re, v_ssm_b_im, v_ssm_c_re, v_ssm_c_im, v_ssm_d, v_ssm_w_glu, v_pool_w, v_pool_scale, v_ssm_out_norm, v_pool_out_norm, v_w_out, v_ffn2_pre_norm, v_ffn2_post_norm, v_ffn2_w_gate, v_ffn2_w_up, v_ffn2_w_down):
    args = locals()
    p = {n: args[n] for n in ORDER}
    m = {n: args["m_" + n] for n in ORDER}
    v = {n: args["v_" + n] for n in ORDER}
    return _step(p, x, loss_target, m, v)
```

```python
import math

import jax
import jax.numpy as jnp
from jax import lax
from jax.experimental import pallas as pl
from jax.experimental.pallas import tpu as pltpu

F32 = jnp.float32
MXU_DTYPE = jnp.bfloat16
WIRE_DTYPE = jnp.bfloat16

RMS_EPS = 1e-6
N_META = 16
POOL_WINDOWS = (2, 4, 8, 16)
POOL_HALO = 16
ADAM_LR, ADAM_B1, ADAM_B2, ADAM_EPS, ADAM_WD, ADAM_STEP = 0.001, 0.9, 0.999, 1e-08, 0.01, 10

LANES = 128
SUBLANES = 8
VMEM_LIMIT = 60 * 1024 * 1024
FFN_TILE = 432
FFN_CHUNK = 1024
MIX_TILE = 216
MIX_SUBTILES = 2
SLAB_GROUP = 8
MESH = pl.DeviceIdType.MESH
ANY = pl.BlockSpec(memory_space=pl.ANY)


def _mm(a, b):
    return jnp.dot(a.astype(MXU_DTYPE), b.astype(MXU_DTYPE), preferred_element_type=F32)


def _mm_nt(a, b):
    return lax.dot_general(a.astype(MXU_DTYPE), b.astype(MXU_DTYPE), (((1,), (1,)), ((), ())), preferred_element_type=F32)


def _mm_tn(a, b):
    return lax.dot_general(a.astype(MXU_DTYPE), b.astype(MXU_DTYPE), (((0,), (0,)), ((), ())), preferred_element_type=F32)


def _rms_stat(x):
    return lax.rsqrt(jnp.mean(x * x, axis=-1, keepdims=True) + RMS_EPS)


def _rms_bwd(x, g, dy):
    r = _rms_stat(x)
    xh = x * r
    dg = jnp.sum(dy * xh, axis=0, keepdims=True)
    dxh = dy * g
    dx = r * (dxh - xh * jnp.mean(dxh * xh, axis=-1, keepdims=True))
    return dx, dg


def _sigmoid(x):
    return 1.0 / (1.0 + jnp.exp(-x))


GELU_C = math.sqrt(2.0 / math.pi)
GELU_K = 0.044715


def _gelu(y):
    return 0.5 * y * (1.0 + jnp.tanh(GELU_C * (y + GELU_K * y * y * y)))


def _gelu_grad(y):
    th = jnp.tanh(GELU_C * (y + GELU_K * y * y * y))
    return 0.5 * (1.0 + th) + 0.5 * y * (1.0 - th * th) * GELU_C * (1.0 + 3.0 * GELU_K * y * y)


def _row_spec(tile, cols, rev_n=None):
    if rev_n is None:
        return pl.BlockSpec((tile, cols), lambda i: (i, 0))
    return pl.BlockSpec((tile, cols), lambda i: (rev_n - 1 - i, 0))


def _full_spec(shape, single=False):
    zeros = (0,) * len(shape)
    if single:
        return pl.BlockSpec(shape, lambda *_: zeros, pipeline_mode=pl.Buffered(1))
    return pl.BlockSpec(shape, lambda *_: zeros)


def _acc(ref, val, first):
    @pl.when(first)
    def _():
        ref[...] = val

    @pl.when(jnp.logical_not(first))
    def _():
        ref[...] += val


def _place():
    x, y, c = lax.axis_index("x"), lax.axis_index("y"), lax.axis_index("c")
    others = [(1 - x, y), (x, 1 - y), (1 - x, 1 - y)]
    return x, y, c, others


class _Exchange:
    mid_step = None

    def __init__(self, ins, out_shapes, aliases, n_sems):
        self.ins, self.out_shapes, self.aliases, self.n_sems = list(ins), list(out_shapes), dict(aliases), n_sems

    def mid(self, ins, outs, send_sems, recv_sems):
        pass


class _SiblingScatter(_Exchange):
    def __init__(self, views):
        super().__init__(views, [jax.ShapeDtypeStruct((4,) + v.shape[2:], v.dtype) for v in views], {}, 4 * len(views))

    def _copies(self, ins, outs, send_sems, recv_sems):
        x, y, c, _ = _place()
        return [
            pltpu.make_async_remote_copy(src_ref=ins[a].at[k, 1 - c], dst_ref=outs[a].at[k], send_sem=send_sems.at[4 * a + k], recv_sem=recv_sems.at[4 * a + k], device_id=(x, y, 1 - c), device_id_type=MESH)
            for a in range(len(ins))
            for k in range(4)
        ]

    def start(self, *refs):
        for cp in self._copies(*refs):
            cp.start()

    def finish(self, *refs):
        cps = self._copies(*refs)
        for cp in cps:
            cp.wait_recv()
        for cp in cps:
            cp.wait_send()


class _ChipScatter(_Exchange):
    def __init__(self, parts):
        super().__init__(parts, [jax.ShapeDtypeStruct((3,) + p.shape[1:], p.dtype) for p in parts], {}, 3 * len(parts))

    def _copies(self, ins, outs, send_sems, recv_sems):
        x, y, c, others = _place()
        return [
            pltpu.make_async_remote_copy(src_ref=ins[a].at[2 * chip[0] + chip[1]], dst_ref=outs[a].at[j], send_sem=send_sems.at[3 * a + j], recv_sem=recv_sems.at[3 * a + j], device_id=(*chip, c), device_id_type=MESH)
            for a in range(len(ins))
            for j, chip in enumerate(others)
        ]

    start = _SiblingScatter.start
    finish = _SiblingScatter.finish


class _SiblingShare(_Exchange):
    def __init__(self, bufs):
        super().__init__(bufs, [jax.ShapeDtypeStruct(b.shape, b.dtype) for b in bufs], {a: a for a in range(len(bufs))}, len(bufs))

    def _copy(self, outs, send_sems, recv_sems, a, half):
        x, y, c, _ = _place()
        mine = outs[a].at[c if half == "mine" else 1 - c]
        return pltpu.make_async_remote_copy(src_ref=mine, dst_ref=mine, send_sem=send_sems.at[a], recv_sem=recv_sems.at[a], device_id=(x, y, 1 - c), device_id_type=MESH)

    def start(self, ins, outs, send_sems, recv_sems):
        for a in range(len(outs)):
            self._copy(outs, send_sems, recv_sems, a, "mine").start()

    def finish(self, ins, outs, send_sems, recv_sems):
        for a in range(len(outs)):
            self._copy(outs, send_sems, recv_sems, a, "theirs").wait_recv()
        for a in range(len(outs)):
            self._copy(outs, send_sems, recv_sems, a, "mine").wait_send()


class _Gather(_Exchange):
    def __init__(self, bufs, mid_step=None):
        super().__init__(bufs, [jax.ShapeDtypeStruct(b.shape, b.dtype) for b in bufs], {a: a for a in range(len(bufs))}, 6 * len(bufs))
        self.mid_step = mid_step

    def _copy(self, outs, send_sems, recv_sems, a, j, chip, half, to):
        blk = outs[a].at[2 * chip[0] + chip[1], half]
        return pltpu.make_async_remote_copy(src_ref=blk, dst_ref=blk, send_sem=send_sems.at[6 * a + j], recv_sem=recv_sems.at[6 * a + j], device_id=to, device_id_type=MESH)

    def start(self, ins, outs, send_sems, recv_sems):
        x, y, c, others = _place()
        for a in range(len(outs)):
            for j, chip in enumerate(others):
                self._copy(outs, send_sems, recv_sems, a, j, (x, y), c, (*chip, c)).start()

    def mid(self, ins, outs, send_sems, recv_sems):
        x, y, c, others = _place()
        for a in range(len(outs)):
            for j, chip in enumerate(others):
                self._copy(outs, send_sems, recv_sems, a, j, chip, c, (x, y, c)).wait_recv()
                self._copy(outs, send_sems, recv_sems, a, 3 + j, chip, c, (x, y, 1 - c)).start()

    def finish(self, ins, outs, send_sems, recv_sems):
        x, y, c, others = _place()
        for a in range(len(outs)):
            for j, chip in enumerate(others):
                self._copy(outs, send_sems, recv_sems, a, 3 + j, chip, 1 - c, (x, y, c)).wait_recv()
        for a in range(len(outs)):
            for j, chip in enumerate(others):
                self._copy(outs, send_sems, recv_sems, a, j, (x, y), c, (*chip, c)).wait_send()
                self._copy(outs, send_sems, recv_sems, a, 3 + j, chip, c, (x, y, 1 - c)).wait_send()


class _SemSlice:
    def __init__(self, sems, off):
        self.sems, self.off = sems, off

    @property
    def at(self):
        return self

    def __getitem__(self, i):
        return self.sems.at[self.off + i]


class _Group(_Exchange):
    def __init__(self, exchanges):
        ins, outs, aliases, n_sems, self.spans = [], [], {}, 0, []
        for ex in exchanges:
            self.spans.append((len(ins), len(outs), n_sems))
            aliases.update({len(ins) + i: len(outs) + o for i, o in ex.aliases.items()})
            ins, outs, n_sems = ins + ex.ins, outs + ex.out_shapes, n_sems + ex.n_sems
        super().__init__(ins, outs, aliases, n_sems)
        self.exchanges = exchanges
        mids = {ex.mid_step for ex in exchanges if ex.mid_step is not None}
        self.mid_step = mids.pop() if mids else None

    def _each(self, method, ins, outs, send_sems, recv_sems):
        for ex, (i0, o0, s0) in zip(self.exchanges, self.spans):
            getattr(ex, method)(ins[i0 : i0 + len(ex.ins)], outs[o0 : o0 + len(ex.out_shapes)], _SemSlice(send_sems, s0), _SemSlice(recv_sems, s0))

    def start(self, *refs):
        self._each("start", *refs)

    def mid(self, *refs):
        self._each("mid", *refs)

    def finish(self, *refs):
        self._each("finish", *refs)

    def split(self, outs):
        return [outs[o0 : o0 + len(ex.out_shapes)] for ex, (_, o0, _) in zip(self.exchanges, self.spans)]


def _exchange_call(ex, name):
    n, m = len(ex.ins), len(ex.out_shapes)

    def body(*refs):
        parts = (refs[:n], refs[n : n + m], refs[n + m], refs[n + m + 1])
        ex.start(*parts)
        ex.mid(*parts)
        ex.finish(*parts)

    return pl.pallas_call(
        body,
        name=name,
        out_shape=ex.out_shapes,
        in_specs=[ANY] * n,
        out_specs=[ANY] * m,
        scratch_shapes=[pltpu.SemaphoreType.DMA((ex.n_sems,)), pltpu.SemaphoreType.DMA((ex.n_sems,))],
        input_output_aliases=ex.aliases,
    )(*ex.ins)


def _pallas(body, *, name, grid, in_specs, out_specs, out_shape, operands, scratch_shapes=(), exchange=None):
    params = pltpu.CompilerParams(dimension_semantics=("arbitrary",) * len(grid), vmem_limit_bytes=VMEM_LIMIT)
    if exchange is None:
        outs = pl.pallas_call(body, name=name, grid=grid, in_specs=in_specs, out_specs=out_specs, out_shape=out_shape, scratch_shapes=list(scratch_shapes), compiler_params=params)(*operands)
        return outs, []
    ex = exchange
    n_in, n_out, n_scr = len(in_specs), len(out_specs), len(scratch_shapes)
    x_in, x_out = len(ex.ins), len(ex.out_shapes)

    def hosted(*refs):
        ins, x_ins = refs[:n_in], refs[n_in : n_in + x_in]
        outs, x_outs = refs[n_in + x_in : n_in + x_in + n_out], refs[n_in + x_in + n_out : n_in + x_in + n_out + x_out]
        rest = refs[n_in + x_in + n_out + x_out :]
        parts = (x_ins, x_outs, rest[n_scr], rest[n_scr + 1])
        ids = [pl.program_id(d) for d in range(len(grid))]
        first = _all([i == 0 for i in ids])
        last = _all([i == g - 1 for i, g in zip(ids, grid)])

        @pl.when(first)
        def _():
            ex.start(*parts)

        body(*ins, *outs, *rest[:n_scr])

        if ex.mid_step is not None:

            @pl.when(ids[0] == ex.mid_step)
            def _():
                ex.mid(*parts)

        @pl.when(last)
        def _():
            ex.finish(*parts)

    outs = pl.pallas_call(
        hosted,
        name=name,
        grid=grid,
        in_specs=list(in_specs) + [ANY] * x_in,
        out_specs=list(out_specs) + [ANY] * x_out,
        out_shape=list(out_shape) + ex.out_shapes,
        scratch_shapes=list(scratch_shapes) + [pltpu.SemaphoreType.DMA((ex.n_sems,)), pltpu.SemaphoreType.DMA((ex.n_sems,))],
        input_output_aliases={n_in + i: n_out + o for i, o in ex.aliases.items()},
        compiler_params=params,
    )(*operands, *ex.ins)
    return outs[:n_out], outs[n_out:]


def _all(conds):
    out = conds[0]
    for c in conds[1:]:
        out = jnp.logical_and(out, c)
    return out


def _row_tile(rows):
    if rows <= 512:
        return rows
    for t in (512, 352, 256, 176, 128, 112, 64, 32, 16, 8):
        if rows % t == 0:
            return t
    return rows


def _add_own_half(view, got, place, name):
    _, _, r, c = view.shape
    tr = _row_tile(r)

    def body(place_ref, v_ref, g_ref, o_ref, w_ref):
        s = v_ref[...] + g_ref[...]
        o_ref[...] = s
        w_ref[...] = s.astype(w_ref.dtype)

    blk = pl.BlockSpec((None, tr, c), lambda k, i, pr: (k, i, 0))
    return pl.pallas_call(
        body,
        name=name,
        out_shape=[jax.ShapeDtypeStruct((4, r, c), F32), jax.ShapeDtypeStruct((4, r, c), WIRE_DTYPE)],
        grid_spec=pltpu.PrefetchScalarGridSpec(
            num_scalar_prefetch=1, grid=(4, r // tr), in_specs=[pl.BlockSpec((None, None, tr, c), lambda k, i, pr: (k, pr[1], i, 0)), blk], out_specs=[blk, blk]
        ),
        compiler_params=pltpu.CompilerParams(dimension_semantics=("arbitrary", "arbitrary"), vmem_limit_bytes=VMEM_LIMIT),
    )(place, view, got)


def _add_chips(part, got, place, name):
    _, r, c = part.shape
    tr = _row_tile(r)

    def body(place_ref, p_ref, g_ref, o_ref):
        o_ref[...] = ((p_ref[...] + g_ref[0].astype(F32)) + g_ref[1].astype(F32)) + g_ref[2].astype(F32)

    return pl.pallas_call(
        body,
        name=name,
        out_shape=jax.ShapeDtypeStruct((2, r, c), F32),
        grid_spec=pltpu.PrefetchScalarGridSpec(
            num_scalar_prefetch=1,
            grid=(r // tr,),
            in_specs=[pl.BlockSpec((None, tr, c), lambda i, pr: (pr[0], i, 0)), pl.BlockSpec((3, tr, c), lambda i, pr: (0, i, 0))],
            out_specs=pl.BlockSpec((None, tr, c), lambda i, pr: (pr[1], i, 0)),
        ),
        compiler_params=pltpu.CompilerParams(dimension_semantics=("arbitrary",), vmem_limit_bytes=VMEM_LIMIT),
    )(place, part, got)


def _adamw_update(w_ref, g_ref, m_ref, v_ref, d_ref, nm_ref, nv_ref):
    g = g_ref[...]
    nm = ADAM_B1 * m_ref[...] + (1.0 - ADAM_B1) * g
    nv = ADAM_B2 * v_ref[...] + (1.0 - ADAM_B2) * (g * g)
    m_hat = nm / (1.0 - ADAM_B1**ADAM_STEP)
    v_hat = nv / (1.0 - ADAM_B2**ADAM_STEP)
    d_ref[...] = -ADAM_LR * (m_hat / (jnp.sqrt(v_hat) + ADAM_EPS) + ADAM_WD * w_ref[...])
    nm_ref[...] = nm
    nv_ref[...] = nv


def _adamw(w, g, m, v, name):
    r, c = w.shape
    tr = _row_tile(r)
    spec = pl.BlockSpec((tr, c), lambda i: (i, 0))
    outs, _ = _pallas(_adamw_update, name=name, grid=(r // tr,), in_specs=[spec] * 4, out_specs=[spec] * 3, out_shape=[jax.ShapeDtypeStruct((r, c), F32)] * 3, operands=(w, g, m, v))
    return outs


def _adamw_many(ws, gs, ms, vs, name):
    n = len(ws)

    def body(*refs):
        for k in range(n):
            _adamw_update(*(refs[j * n + k] for j in range(7)))

    outs = pl.pallas_call(
        body,
        name=name,
        out_shape=[jax.ShapeDtypeStruct(w.shape, F32) for w in ws] * 3,
        in_specs=[pl.BlockSpec(memory_space=pltpu.VMEM)] * (4 * n),
        out_specs=[pl.BlockSpec(memory_space=pltpu.VMEM)] * (3 * n),
    )(*ws, *gs, *ms, *vs)
    return outs[:n], outs[n : 2 * n], outs[2 * n :]


def _load_weights(pairs, sems):
    @pl.when(pl.program_id(0) == 0)
    def _():
        cps = [pltpu.make_async_copy(src, dst, sems.at[k]) for k, (src, dst) in enumerate(pairs)]
        for cp in cps:
            cp.start()
        for cp in cps:
            cp.wait()


def _ffn_chunks(F):
    bounds = list(range(0, F, FFN_CHUNK)) + [F]
    return list(zip(bounds[:-1], bounds[1:]))


def _shifted_specs(tm, cols):
    per = tm // N_META
    return [_row_spec(tm, cols), pl.BlockSpec((N_META, cols), lambda i: (jnp.maximum(i * per - 1, 0), 0))]


def _shifted_tile(cur_ref, before_ref, tm):
    return jnp.concatenate([before_ref[...], cur_ref[0 : tm - N_META, :]], axis=0)


def _tokens_tile(cur_ref, before_ref, meta_ref, tm, tile_0):
    first = jnp.where(tile_0, meta_ref[...], before_ref[...])
    return jnp.concatenate([first, cur_ref[0 : tm - N_META, :]], axis=0)


def _ffn_fwd(h, g_pre, g_post, wg, wu, wd, name, exchange=None, meta=None, target=None):
    D = h.shape[1]
    T = h.shape[0] + (0 if meta is None else N_META)
    F = wg.shape[0]
    tm = FFN_TILE
    n_src = 1 if meta is None else 3
    n_tgt = 0 if target is None else 2
    n_head = 1 if target is None else 2

    def body(*refs):
        src, refs = refs[:n_src], refs[n_src:]
        tgt, refs = refs[:n_tgt], refs[n_tgt:]
        (gpre_ref, gpost_ref, wg_hbm, wu_hbm, wd_hbm), refs = refs[:5], refs[5:]
        head, (f_ref, ga_ref, si_ref, s_ref, n_ref, wg_v, wu_v, wd_v, sems) = refs[:n_head], refs[n_head:]
        i = pl.program_id(0)
        _load_weights([(wg_hbm, wg_v), (wu_hbm, wu_v), (wd_hbm, wd_v)], sems)
        hh = src[0][...] if meta is None else _tokens_tile(src[0], src[1], src[2], tm, i == 0)
        n = (hh * _rms_stat(hh) * gpre_ref[...]).astype(MXU_DTYPE)
        n_ref[...] = n.astype(n_ref.dtype)
        f = jnp.zeros((tm, D), F32)
        for lo, hi in _ffn_chunks(F):
            a = _mm_nt(n, wg_v[lo:hi, :])
            b = _mm_nt(n, wu_v[lo:hi, :])
            sg = _sigmoid(a)
            si = a * sg
            s = (si * b).astype(MXU_DTYPE)
            ga_ref[:, lo:hi] = (b * (sg * (1.0 + a * (1.0 - sg)))).astype(ga_ref.dtype)
            si_ref[:, lo:hi] = si.astype(si_ref.dtype)
            s_ref[:, lo:hi] = s.astype(s_ref.dtype)
            f = f + _mm(s, wd_v[lo:hi, :])
        f_ref[...] = f
        out = hh + 0.5 * (f * _rms_stat(f) * gpost_ref[...])
        if target is None:
            head[0][...] = out
        else:
            rows = i * tm + lax.broadcasted_iota(jnp.int32, (tm, D), 0)
            err = jnp.where(rows >= N_META, out - _shifted_tile(tgt[0], tgt[1], tm), 0.0)
            head[0][...] = err * (1.0 / D)
            _acc(head[1], jnp.sum(err * err, axis=0, keepdims=True), i == 0)

    tok = jax.ShapeDtypeStruct((T, D), F32)
    act = jax.ShapeDtypeStruct((T, F), MXU_DTYPE)
    src_specs = [_row_spec(tm, D)] if meta is None else _shifted_specs(tm, D) + [_full_spec((N_META, D))]
    src_ops = (h,) if meta is None else (h, h, meta)
    tgt_specs, tgt_ops = ([], ()) if target is None else (_shifted_specs(tm, D), (target, target))
    head_shapes = [tok] if target is None else [tok, jax.ShapeDtypeStruct((1, D), F32)]
    head_specs = [_row_spec(tm, D)] if target is None else [_row_spec(tm, D), _full_spec((1, D))]
    return _pallas(
        body,
        name=name,
        grid=(T // tm,),
        out_shape=head_shapes + [tok, act, act, act, jax.ShapeDtypeStruct((T, D), MXU_DTYPE)],
        in_specs=src_specs + tgt_specs + [_full_spec((1, D)), _full_spec((1, D)), ANY, ANY, ANY],
        out_specs=head_specs + [_row_spec(tm, D), _row_spec(tm, F), _row_spec(tm, F), _row_spec(tm, F), _row_spec(tm, D)],
        scratch_shapes=[pltpu.VMEM(wg.shape, wg.dtype), pltpu.VMEM(wu.shape, wu.dtype), pltpu.VMEM(wd.shape, wd.dtype), pltpu.SemaphoreType.DMA((3,))],
        operands=(*src_ops, *tgt_ops, g_pre, g_post, wg, wu, wd),
        exchange=exchange,
    )


def _ffn_bwd_down(dh, f, ga, si, g_post, wd, name, exchange=None):
    T, D = dh.shape
    F = wd.shape[0]
    tm = FFN_TILE

    def body(dh_ref, f_ref, ga_ref, si_ref, gpost_ref, wd_hbm, da_ref, db_ref, df_ref, dg_ref, wd_v, sems):
        _load_weights([(wd_hbm, wd_v)], sems)
        df, dg = _rms_bwd(f_ref[...], gpost_ref[...], 0.5 * dh_ref[...])
        _acc(dg_ref, dg, pl.program_id(0) == 0)
        dfb = df.astype(MXU_DTYPE)
        df_ref[...] = dfb.astype(df_ref.dtype)
        for lo, hi in _ffn_chunks(F):
            ds = _mm_nt(dfb, wd_v[lo:hi, :])
            da_ref[:, lo:hi] = (ds * ga_ref[:, lo:hi].astype(F32)).astype(da_ref.dtype)
            db_ref[:, lo:hi] = (ds * si_ref[:, lo:hi].astype(F32)).astype(db_ref.dtype)

    act = jax.ShapeDtypeStruct((T, F), MXU_DTYPE)
    return _pallas(
        body,
        name=name,
        grid=(T // tm,),
        out_shape=[act, act, jax.ShapeDtypeStruct((T, D), MXU_DTYPE), jax.ShapeDtypeStruct((1, D), F32)],
        in_specs=[_row_spec(tm, D), _row_spec(tm, D), _row_spec(tm, F), _row_spec(tm, F), _full_spec((1, D)), ANY],
        out_specs=[_row_spec(tm, F), _row_spec(tm, F), _row_spec(tm, D), _full_spec((1, D))],
        scratch_shapes=[pltpu.VMEM(wd.shape, wd.dtype), pltpu.SemaphoreType.DMA((1,))],
        operands=(dh, f, ga, si, g_post, wd),
        exchange=exchange,
    )


def _ffn_bwd_up(da, db, x, meta, dh, g_pre, wg, wu, name, exchange=None):
    D = x.shape[1]
    T = x.shape[0] + N_META
    F = wg.shape[0]
    tm = FFN_TILE
    nt = T // tm
    per = tm // N_META
    tile = lambda i: jnp.minimum(i, nt - 1)

    def body(da_ref, db_ref, x_ref, xb_ref, meta_ref, dh_ref, gpre_ref, wg_hbm, wu_hbm, dx_ref, dmeta_ref, dg_ref, wg_v, wu_v, sems, held):
        i = pl.program_id(0)
        _load_weights([(wg_hbm, wg_v), (wu_hbm, wu_v)], sems)

        @pl.when(i < nt)
        def _():
            dn = jnp.zeros((tm, D), F32)
            for lo, hi in _ffn_chunks(F):
                dn = dn + _mm(da_ref[:, lo:hi], wg_v[lo:hi, :]) + _mm(db_ref[:, lo:hi], wu_v[lo:hi, :])
            dx, dg = _rms_bwd(_tokens_tile(x_ref, xb_ref, meta_ref, tm, i == 0), gpre_ref[...], dn)
            _acc(dg_ref, dg, i == 0)
            dh_in = dh_ref[...] + dx

            @pl.when(i == 0)
            def _():
                dmeta_ref[...] = dh_in[0:N_META, :]

            @pl.when(i > 0)
            def _():
                dx_ref[...] = jnp.concatenate([held[...], dh_in[0:N_META, :]], axis=0)

            held[...] = dh_in[N_META:, :]

        @pl.when(i == nt)
        def _():
            dx_ref[0 : tm - N_META, :] = held[...]

    rows = lambda cols: pl.BlockSpec((tm, cols), lambda i: (tile(i), 0))
    return _pallas(
        body,
        name=name,
        grid=(nt + 1,),
        out_shape=[jax.ShapeDtypeStruct((T - N_META, D), F32), jax.ShapeDtypeStruct((N_META, D), F32), jax.ShapeDtypeStruct((1, D), F32)],
        in_specs=[rows(F), rows(F), rows(D), pl.BlockSpec((N_META, D), lambda i: (jnp.maximum(tile(i) * per - 1, 0), 0)), _full_spec((N_META, D)), rows(D), _full_spec((1, D)), ANY, ANY],
        out_specs=[pl.BlockSpec((tm, D), lambda i: (jnp.maximum(i - 1, 0), 0)), _full_spec((N_META, D)), _full_spec((1, D))],
        scratch_shapes=[pltpu.VMEM(wg.shape, wg.dtype), pltpu.VMEM(wu.shape, wu.dtype), pltpu.SemaphoreType.DMA((2,)), pltpu.VMEM((tm - N_META, D), F32)],
        operands=(da, db, x, x, meta, dh, g_pre, wg, wu),
        exchange=exchange,
    )


def _ffn_bwd(dh, f, ga, si, h, g_post, g_pre, wg, wu, wd, name):
    T, D = dh.shape
    F = wd.shape[0]
    tm = FFN_TILE

    def body(dh_ref, f_ref, ga_ref, si_ref, h_ref, gpost_ref, gpre_ref, wg_hbm, wu_hbm, wd_hbm, da_ref, db_ref, df_ref, dhin_ref, dgpost_ref, dgpre_ref, wg_v, wu_v, wd_v, sems):
        first = pl.program_id(0) == 0
        _load_weights([(wg_hbm, wg_v), (wu_hbm, wu_v), (wd_hbm, wd_v)], sems)
        dh = dh_ref[...]
        df, dg = _rms_bwd(f_ref[...], gpost_ref[...], 0.5 * dh)
        _acc(dgpost_ref, dg, first)
        dfb = df.astype(MXU_DTYPE)
        df_ref[...] = dfb.astype(df_ref.dtype)
        dn = jnp.zeros((tm, D), F32)
        for lo, hi in _ffn_chunks(F):
            ds = _mm_nt(dfb, wd_v[lo:hi, :])
            da = (ds * ga_ref[:, lo:hi].astype(F32)).astype(MXU_DTYPE)
            db = (ds * si_ref[:, lo:hi].astype(F32)).astype(MXU_DTYPE)
            da_ref[:, lo:hi] = da.astype(da_ref.dtype)
            db_ref[:, lo:hi] = db.astype(db_ref.dtype)
            dn = dn + _mm(da, wg_v[lo:hi, :]) + _mm(db, wu_v[lo:hi, :])
        dx, dg = _rms_bwd(h_ref[...], gpre_ref[...], dn)
        _acc(dgpre_ref, dg, first)
        dhin_ref[...] = dh + dx

    act = jax.ShapeDtypeStruct((T, F), MXU_DTYPE)
    vec = jax.ShapeDtypeStruct((1, D), F32)
    outs, _ = _pallas(
        body,
        name=name,
        grid=(T // tm,),
        out_shape=[act, act, jax.ShapeDtypeStruct((T, D), MXU_DTYPE), jax.ShapeDtypeStruct((T, D), F32), vec, vec],
        in_specs=[_row_spec(tm, D), _row_spec(tm, D), _row_spec(tm, F), _row_spec(tm, F), _row_spec(tm, D), _full_spec((1, D)), _full_spec((1, D)), ANY, ANY, ANY],
        out_specs=[_row_spec(tm, F), _row_spec(tm, F), _row_spec(tm, D), _row_spec(tm, D), _full_spec((1, D)), _full_spec((1, D))],
        scratch_shapes=[pltpu.VMEM(wg.shape, wg.dtype), pltpu.VMEM(wu.shape, wu.dtype), pltpu.VMEM(wd.shape, wd.dtype), pltpu.SemaphoreType.DMA((3,))],
        operands=(dh, f, ga, si, h, g_post, g_pre, wg, wu, wd),
    )
    return outs


def _token_tile(T):
    for t in (912, 864, 432):
        if T % t == 0:
            return t
    raise ValueError(f"no token tile for {T} rows")


def _tn_matmul(xm, ym, name, exchange=None):
    T, M = xm.shape
    N = ym.shape[1]
    tk = _token_tile(T)

    def body(x_ref, y_ref, o_ref):
        _acc(o_ref, _mm_tn(x_ref[...], y_ref[...]), pl.program_id(0) == 0)

    (out,), x_outs = _pallas(
        body,
        name=name,
        grid=(T // tk,),
        out_shape=[jax.ShapeDtypeStruct((M, N), F32)],
        in_specs=[pl.BlockSpec((tk, M), lambda k: (k, 0)), pl.BlockSpec((tk, N), lambda k: (k, 0))],
        out_specs=[_full_spec((M, N))],
        operands=(xm, ym),
        exchange=exchange,
    )
    return out, x_outs


TAB_A, TAB_AS1, TAB_AS2, TAB_AS4, TAB_JF, TAB_JB = 0, 2, 4, 6, 8, 10


def _scan_inplace(zr, zi, tabs, pows, car_r, car_i, seg, reverse, base=0):
    n_slabs = zr.shape[0]
    sgn = -1.0 if reverse else 1.0
    row = lax.broadcasted_iota(jnp.int32, (SUBLANES, LANES), 0)

    def cmul(pr, pi, xr, xi):
        return pr * xr - pi * xi, pr * xi + pi * xr

    for k0 in range(0, n_slabs, SLAB_GROUP):
        slabs = range(k0, min(k0 + SLAB_GROUP, n_slabs))
        ar = [tabs[TAB_A, k] for k in slabs]
        ai = [sgn * tabs[TAB_A + 1, k] for k in slabs]

        def first_pass(t, carry):
            r = (seg - 1 - t) if reverse else t
            out = []
            for q, k in enumerate(slabs):
                xr, xi = carry[2 * q], carry[2 * q + 1]
                pr, pi = cmul(ar[q], ai[q], xr, xi)
                nr = pr + zr[k, pl.ds(base + r, SUBLANES, stride=seg), :]
                ni = pi + zi[k, pl.ds(base + r, SUBLANES, stride=seg), :]
                zr[k, pl.ds(base + r, SUBLANES, stride=seg), :] = nr
                zi[k, pl.ds(base + r, SUBLANES, stride=seg), :] = ni
                out += [nr, ni]
            return tuple(out)

        ends = lax.fori_loop(0, seg, first_pass, tuple(jnp.zeros((SUBLANES, LANES), F32) for _ in range(2 * len(slabs))))

        incoming = []
        for q, k in enumerate(slabs):
            fr, fi = ends[2 * q], ends[2 * q + 1]
            for d, tab in ((1, TAB_AS1), (2, TAB_AS2), (4, TAB_AS4)):
                shift, keep = (SUBLANES - d, row < SUBLANES - d) if reverse else (d, row >= d)
                sr = jnp.where(keep, pltpu.roll(fr, shift, 0), 0.0)
                si = jnp.where(keep, pltpu.roll(fi, shift, 0), 0.0)
                pr, pi = cmul(tabs[tab, k], sgn * tabs[tab + 1, k], sr, si)
                fr, fi = fr + pr, fi + pi
            cr, ci = car_r[k], car_i[k]
            jtab = TAB_JB if reverse else TAB_JF
            pr, pi = cmul(tabs[jtab, k], sgn * tabs[jtab + 1, k], cr, ci)
            er, ei = fr + pr, fi + pi
            if reverse:
                inr = jnp.where(row < SUBLANES - 1, pltpu.roll(er, SUBLANES - 1, 0), cr)
                ini = jnp.where(row < SUBLANES - 1, pltpu.roll(ei, SUBLANES - 1, 0), ci)
                car_r[k] = jnp.broadcast_to(er[0:1, :], (SUBLANES, LANES))
                car_i[k] = jnp.broadcast_to(ei[0:1, :], (SUBLANES, LANES))
            else:
                inr = jnp.where(row >= 1, pltpu.roll(er, 1, 0), cr)
                ini = jnp.where(row >= 1, pltpu.roll(ei, 1, 0), ci)
                car_r[k] = jnp.broadcast_to(er[SUBLANES - 1 : SUBLANES, :], (SUBLANES, LANES))
                car_i[k] = jnp.broadcast_to(ei[SUBLANES - 1 : SUBLANES, :], (SUBLANES, LANES))
            incoming += [inr, ini]

        def second_pass(r, _):
            p = (seg - 1 - r) if reverse else r
            for q, k in enumerate(slabs):
                pr, pi = cmul(pows[0, k, p], sgn * pows[1, k, p], incoming[2 * q], incoming[2 * q + 1])
                zr[k, pl.ds(base + r, SUBLANES, stride=seg), :] = zr[k, pl.ds(base + r, SUBLANES, stride=seg), :] + pr
                zi[k, pl.ds(base + r, SUBLANES, stride=seg), :] = zi[k, pl.ds(base + r, SUBLANES, stride=seg), :] + pi
            return 0

        lax.fori_loop(0, seg, second_pass, 0)


def _slabs_to_cols(ref, k0, n):
    return jnp.concatenate([ref[k0 + q] for q in range(n)], axis=1)


def _window_sum(ext, doublings, forward):
    rows = ext.shape[0]
    s = ext
    for k in range(doublings):
        s = s + pltpu.roll(s, (1 << k) if forward else rows - (1 << k), 0)
    return s


def _mix_fwd(h1, g_pre, g_so, g_po, g_post, dskip, pscale, win, wout, bbre, bbim, ccre, ccim, wgv, wgg, pw, tabs, pows, name):
    T, D = h1.shape
    W = D // 2
    tm = MIX_SUBTILES * MIX_TILE
    seg = MIX_TILE // SUBLANES
    n_slabs = tabs.shape[1]
    nch, cch, sch = bbre.shape
    spc = sch // LANES
    pg = W // len(POOL_WINDOWS)

    def body(h_ref, gpre_ref, gso_ref, gpo_ref, gpost_ref, dskip_ref, pscale_ref, win_ref, wout_ref, bbre_ref, bbim_ref, ccre_ref, ccim_ref, wgv_ref, wgg_ref, pw_ref, tabs_ref, pows_ref,
             proj_ref, xr_ref, xi_ref, y_ref, pooled_ref, mixed_ref, h2_ref, n2_ref, cat_ref, car_r, car_i, halo):
        i = pl.program_id(0)

        @pl.when(i == 0)
        def _():
            car_r[...] = jnp.zeros_like(car_r)
            car_i[...] = jnp.zeros_like(car_i)
            halo[...] = jnp.zeros_like(halo)

        hh = h_ref[...]
        n2 = (hh * _rms_stat(hh) * gpre_ref[...]).astype(MXU_DTYPE)
        n2_ref[...] = n2.astype(n2_ref.dtype)
        proj = _mm(n2, win_ref[...])
        proj_ref[...] = proj
        us, up = proj[:, :W], proj[:, W:]

        for c in range(nch):
            uc = us[:, c * cch : (c + 1) * cch].astype(MXU_DTYPE)
            bur, bui = _mm(uc, bbre_ref[c]), _mm(uc, bbim_ref[c])
            for q in range(spc):
                xr_ref[c * spc + q] = bur[:, q * LANES : (q + 1) * LANES]
                xi_ref[c * spc + q] = bui[:, q * LANES : (q + 1) * LANES]
        for sub in range(MIX_SUBTILES):
            _scan_inplace(xr_ref, xi_ref, tabs_ref, pows_ref, car_r, car_i, seg, reverse=False, base=sub * MIX_TILE)
        ys = []
        for c in range(nch):
            ys.append(_mm(_slabs_to_cols(xr_ref, c * spc, spc), ccre_ref[c]) - _mm(_slabs_to_cols(xi_ref, c * spc, spc), ccim_ref[c]))
        y = jnp.concatenate(ys, axis=1) + dskip_ref[...] * us
        y_ref[...] = y
        ge = _gelu(y).astype(MXU_DTYPE)
        zv = jnp.concatenate([_mm(ge[:, c * cch : (c + 1) * cch], wgv_ref[c]) for c in range(nch)], axis=1)
        zg = jnp.concatenate([_mm(ge[:, c * cch : (c + 1) * cch], wgg_ref[c]) for c in range(nch)], axis=1)
        out = zv * _sigmoid(zg)
        cat_s = out * _rms_stat(out) * gso_ref[...]

        ext = jnp.concatenate([halo[...], up], axis=0)
        halo[...] = up[tm - POOL_HALO :, :]
        t1 = (i * tm + 1 + lax.broadcasted_iota(jnp.int32, (tm, pg), 0)).astype(F32)
        pooled, pms = [], []
        for g, w in enumerate(POOL_WINDOWS):
            col = ext[:, g * pg : (g + 1) * pg]
            win_sum = _window_sum(col, g + 1, True)[POOL_HALO:, :]
            pooled_g = win_sum / jnp.minimum(t1, float(w)) - up[:, g * pg : (g + 1) * pg]
            pooled.append(pooled_g)
            pms.append(_mm(pooled_g, pw_ref[g]))
        pooled_ref[...] = jnp.concatenate(pooled, axis=1)
        yp = jnp.concatenate(pms, axis=1) * pscale_ref[...]
        cat_p = yp * _rms_stat(yp) * gpo_ref[...]

        cat = jnp.concatenate([cat_s, cat_p], axis=1).astype(MXU_DTYPE)
        cat_ref[...] = cat.astype(cat_ref.dtype)
        mixed = _mm(cat, wout_ref[...])
        mixed_ref[...] = mixed
        h2_ref[...] = hh + mixed * _rms_stat(mixed) * gpost_ref[...]

    tok = lambda cols, dt=F32: jax.ShapeDtypeStruct((T, cols), dt)
    slab_spec = pl.BlockSpec((n_slabs, tm, LANES), lambda i: (0, i, 0))
    operands = (h1, g_pre, g_so, g_po, g_post, dskip, pscale, win, wout, bbre, bbim, ccre, ccim, wgv, wgg, pw, tabs, pows)
    outs, _ = _pallas(
        body,
        name=name,
        grid=(T // tm,),
        out_shape=[tok(D), jax.ShapeDtypeStruct((n_slabs, T, LANES), F32), jax.ShapeDtypeStruct((n_slabs, T, LANES), F32), tok(W), tok(W), tok(D), tok(D), tok(D, MXU_DTYPE), tok(D, MXU_DTYPE)],
        in_specs=[_row_spec(tm, D)] + [_full_spec(o.shape, single=True) for o in operands[1:]],
        out_specs=[_row_spec(tm, D), slab_spec, slab_spec, _row_spec(tm, W), _row_spec(tm, W), _row_spec(tm, D), _row_spec(tm, D), _row_spec(tm, D), _row_spec(tm, D)],
        scratch_shapes=[pltpu.VMEM((n_slabs, SUBLANES, LANES), F32), pltpu.VMEM((n_slabs, SUBLANES, LANES), F32), pltpu.VMEM((POOL_HALO, W), F32)],
        operands=operands,
    )
    return outs


def _mix_bwd_heads(dh2, mixed, y, pooled, proj, g_so, g_po, g_post, pscale, wout, wgv, wgg, pw, name, exchange=None):
    T, D = dh2.shape
    W = D // 2
    tm = FFN_TILE
    nch, cch, _ = wgv.shape
    ng, pg, _ = pw.shape

    def body(dh2_ref, mixed_ref, y_ref, pooled_ref, us_ref, gso_ref, gpo_ref, gpost_ref, pscale_ref, wout_ref, wgv_ref, wgg_ref, pw_ref,
             dy_ref, dpooled_ref, dmixed_ref, dgpost_ref, dgso_ref, dgpo_ref, dd_ref, dscale_ref, dwgv_ref, dwgg_ref, dpw_ref):
        first = pl.program_id(0) == 0
        dmixed, dgpost = _rms_bwd(mixed_ref[...], gpost_ref[...], dh2_ref[...])
        _acc(dgpost_ref, dgpost, first)
        dmb = dmixed.astype(MXU_DTYPE)
        dmixed_ref[...] = dmb.astype(dmixed_ref.dtype)
        dcat = _mm_nt(dmb, wout_ref[...])
        dcs, dcp = dcat[:, :W], dcat[:, W:]

        y = y_ref[...]
        ge = _gelu(y).astype(MXU_DTYPE)
        zv = jnp.concatenate([_mm(ge[:, c * cch : (c + 1) * cch], wgv_ref[c]) for c in range(nch)], axis=1)
        zg = jnp.concatenate([_mm(ge[:, c * cch : (c + 1) * cch], wgg_ref[c]) for c in range(nch)], axis=1)
        sg = _sigmoid(zg)
        dout, dgso = _rms_bwd(zv * sg, gso_ref[...], dcs)
        _acc(dgso_ref, dgso, first)
        dzv = (dout * sg).astype(MXU_DTYPE)
        dzg = (dout * zv * sg * (1.0 - sg)).astype(MXU_DTYPE)
        dges = []
        for c in range(nch):
            cs = slice(c * cch, (c + 1) * cch)
            dges.append(_mm_nt(dzv[:, cs], wgv_ref[c]) + _mm_nt(dzg[:, cs], wgg_ref[c]))
            _acc(dwgv_ref.at[c], _mm_tn(ge[:, cs], dzv[:, cs]), first)
            _acc(dwgg_ref.at[c], _mm_tn(ge[:, cs], dzg[:, cs]), first)
        dy = jnp.concatenate(dges, axis=1) * _gelu_grad(y)
        dy_ref[...] = dy
        _acc(dd_ref, jnp.sum(dy * us_ref[...], axis=0, keepdims=True), first)

        pooled_b = pooled_ref[...].astype(MXU_DTYPE)
        pm = jnp.concatenate([_mm(pooled_b[:, g * pg : (g + 1) * pg], pw_ref[g]) for g in range(ng)], axis=1)
        dyp, dgpo = _rms_bwd(pm * pscale_ref[...], gpo_ref[...], dcp)
        _acc(dgpo_ref, dgpo, first)
        _acc(dscale_ref, jnp.sum(dyp * pm, axis=0, keepdims=True), first)
        dpm = (dyp * pscale_ref[...]).astype(MXU_DTYPE)
        dps = []
        for g in range(ng):
            gs = slice(g * pg, (g + 1) * pg)
            dps.append(_mm_nt(dpm[:, gs], pw_ref[g]))
            _acc(dpw_ref.at[g], _mm_tn(pooled_b[:, gs], dpm[:, gs]), first)
        dpooled_ref[...] = jnp.concatenate(dps, axis=1)

    vec = lambda n: jax.ShapeDtypeStruct((1, n), F32)
    operands = (dh2, mixed, y, pooled, proj, g_so, g_po, g_post, pscale, wout, wgv, wgg, pw)
    return _pallas(
        body,
        name=name,
        grid=(T // tm,),
        out_shape=[jax.ShapeDtypeStruct((T, W), F32), jax.ShapeDtypeStruct((T, W), F32), jax.ShapeDtypeStruct((T, D), MXU_DTYPE), vec(D), vec(W), vec(W), vec(W), vec(W),
                   jax.ShapeDtypeStruct(wgv.shape, F32), jax.ShapeDtypeStruct(wgg.shape, F32), jax.ShapeDtypeStruct(pw.shape, F32)],
        in_specs=[_row_spec(tm, D), _row_spec(tm, D), _row_spec(tm, W), _row_spec(tm, W), _row_spec(tm, W)] + [_full_spec(o.shape) for o in operands[5:]],
        out_specs=[_row_spec(tm, W), _row_spec(tm, W), _row_spec(tm, D), _full_spec((1, D)), _full_spec((1, W)), _full_spec((1, W)), _full_spec((1, W)), _full_spec((1, W)),
                   _full_spec(wgv.shape), _full_spec(wgg.shape), _full_spec(pw.shape)],
        operands=operands,
        exchange=exchange,
    )


def _mix_bwd_scan(dy, dpooled, xr, xi, proj, dskip, bbre, bbim, ccre, ccim, tabs, pows, name, exchange=None):
    T, W = dy.shape
    D = 2 * W
    tm = MIX_SUBTILES * MIX_TILE
    seg = MIX_TILE // SUBLANES
    nt = T // tm
    n_slabs = tabs.shape[1]
    nch, cch, sch = bbre.shape
    spc = sch // LANES
    pg = W // len(POOL_WINDOWS)
    blocks_per_tile = tm // SUBLANES

    def body(dy_ref, dp_ref, xr_ref, xi_ref, xpr_ref, xpi_ref, us_ref, dskip_ref, bbre_ref, bbim_ref, ccre_ref, ccim_ref, tabs_ref, pows_ref,
             dproj_ref, dccre_ref, dccim_ref, dbbre_ref, dbbim_ref, dar_ref, dai_ref, lr, li, car_r, car_i, halo):
        i = pl.program_id(0)
        first = i == 0
        tile = nt - 1 - i
        row = lax.broadcasted_iota(jnp.int32, (SUBLANES, LANES), 0)

        @pl.when(first)
        def _():
            car_r[...] = jnp.zeros_like(car_r)
            car_i[...] = jnp.zeros_like(car_i)
            halo[...] = jnp.zeros_like(halo)
            dar_ref[...] = jnp.zeros_like(dar_ref)
            dai_ref[...] = jnp.zeros_like(dai_ref)

        dy = dy_ref[...]
        for c in range(nch):
            dyc = dy[:, c * cch : (c + 1) * cch]
            gr, gi = _mm_nt(dyc, ccre_ref[c]), _mm_nt(dyc, ccim_ref[c])
            for q in range(spc):
                lr[c * spc + q] = gr[:, q * LANES : (q + 1) * LANES]
                li[c * spc + q] = -gi[:, q * LANES : (q + 1) * LANES]
            _acc(dccre_ref.at[c], _mm_tn(_slabs_to_cols(xr_ref, c * spc, spc), dyc), first)
            _acc(dccim_ref.at[c], -_mm_tn(_slabs_to_cols(xi_ref, c * spc, spc), dyc), first)
        for sub in reversed(range(MIX_SUBTILES)):
            _scan_inplace(lr, li, tabs_ref, pows_ref, car_r, car_i, seg, reverse=True, base=sub * MIX_TILE)

        for sub in range(MIX_SUBTILES):
            base = sub * MIX_TILE
            for k0 in range(0, n_slabs, SLAB_GROUP):
                slabs = range(k0, min(k0 + SLAB_GROUP, n_slabs))
                init = []
                for k in slabs:
                    if sub == 0:
                        prev_r = jnp.where(tile > 0, jnp.broadcast_to(xpr_ref[k, SUBLANES - 1 : SUBLANES, :], (SUBLANES, LANES)), 0.0)
                        prev_i = jnp.where(tile > 0, jnp.broadcast_to(xpi_ref[k, SUBLANES - 1 : SUBLANES, :], (SUBLANES, LANES)), 0.0)
                    else:
                        prev_r = jnp.broadcast_to(xr_ref[k, base - 1 : base, :], (SUBLANES, LANES))
                        prev_i = jnp.broadcast_to(xi_ref[k, base - 1 : base, :], (SUBLANES, LANES))
                    x0r = jnp.where(row >= 1, pltpu.roll(xr_ref[k, pl.ds(base + seg - 1, SUBLANES, stride=seg), :], 1, 0), prev_r)
                    x0i = jnp.where(row >= 1, pltpu.roll(xi_ref[k, pl.ds(base + seg - 1, SUBLANES, stride=seg), :], 1, 0), prev_i)
                    l0r, l0i = lr[k, pl.ds(base, SUBLANES, stride=seg), :], li[k, pl.ds(base, SUBLANES, stride=seg), :]
                    init += [l0r * x0r + l0i * x0i, l0i * x0r - l0r * x0i]

                def step(r, acc, slabs=slabs, base=base):
                    out = []
                    for q, k in enumerate(slabs):
                        pr_, pi_ = xr_ref[k, pl.ds(base + r - 1, SUBLANES, stride=seg), :], xi_ref[k, pl.ds(base + r - 1, SUBLANES, stride=seg), :]
                        lr_, li_ = lr[k, pl.ds(base + r, SUBLANES, stride=seg), :], li[k, pl.ds(base + r, SUBLANES, stride=seg), :]
                        out += [acc[2 * q] + lr_ * pr_ + li_ * pi_, acc[2 * q + 1] + li_ * pr_ - lr_ * pi_]
                    return tuple(out)

                sums = lax.fori_loop(1, seg, step, tuple(init))
                for q, k in enumerate(slabs):
                    dar_ref[k] += sums[2 * q]
                    dai_ref[k] += sums[2 * q + 1]

        us = us_ref[...]
        dus = []
        for c in range(nch):
            lrc, lic = _slabs_to_cols(lr, c * spc, spc).astype(MXU_DTYPE), _slabs_to_cols(li, c * spc, spc).astype(MXU_DTYPE)
            uc = us[:, c * cch : (c + 1) * cch]
            _acc(dbbre_ref.at[c], _mm_tn(uc, lrc), first)
            _acc(dbbim_ref.at[c], _mm_tn(uc, lic), first)
            dus.append(_mm_nt(lrc, bbre_ref[c]) + _mm_nt(lic, bbim_ref[c]))
        du_s = jnp.concatenate(dus, axis=1) + dskip_ref[...] * dy

        dp = dp_ref[...]
        t1 = (tile * tm + 1 + lax.broadcasted_iota(jnp.int32, (tm, pg), 0)).astype(F32)
        dups, heads = [], []
        for g, w in enumerate(POOL_WINDOWS):
            dpg = dp[:, g * pg : (g + 1) * pg]
            qg = dpg / jnp.minimum(t1, float(w))
            ext = jnp.concatenate([qg, halo[:, g * pg : (g + 1) * pg]], axis=0)
            dups.append(_window_sum(ext, g + 1, False)[:tm, :] - dpg)
            heads.append(qg[:POOL_HALO, :])
        halo[...] = jnp.concatenate(heads, axis=1)
        dproj_ref[...] = jnp.concatenate([du_s] + dups, axis=1).astype(dproj_ref.dtype)

    rev = lambda cols: _row_spec(tm, cols, rev_n=nt)
    slab_spec = pl.BlockSpec((n_slabs, tm, LANES), lambda i: (0, nt - 1 - i, 0))
    prev_spec = pl.BlockSpec((n_slabs, SUBLANES, LANES), lambda i: (0, jnp.maximum((nt - 1 - i) * blocks_per_tile - 1, 0), 0))
    consts = (dskip, bbre, bbim, ccre, ccim, tabs, pows)
    return _pallas(
        body,
        name=name,
        grid=(nt,),
        out_shape=[jax.ShapeDtypeStruct((T, D), MXU_DTYPE), jax.ShapeDtypeStruct(ccre.shape, F32), jax.ShapeDtypeStruct(ccim.shape, F32), jax.ShapeDtypeStruct(bbre.shape, F32),
                   jax.ShapeDtypeStruct(bbim.shape, F32), jax.ShapeDtypeStruct((n_slabs, SUBLANES, LANES), F32), jax.ShapeDtypeStruct((n_slabs, SUBLANES, LANES), F32)],
        in_specs=[rev(W), rev(W), slab_spec, slab_spec, prev_spec, prev_spec, rev(W)] + [_full_spec(o.shape, single=True) for o in consts],
        out_specs=[rev(D), _full_spec(ccre.shape), _full_spec(ccim.shape), _full_spec(bbre.shape), _full_spec(bbim.shape),
                   _full_spec((n_slabs, SUBLANES, LANES)), _full_spec((n_slabs, SUBLANES, LANES))],
        scratch_shapes=[pltpu.VMEM((n_slabs, tm, LANES), F32), pltpu.VMEM((n_slabs, tm, LANES), F32), pltpu.VMEM((n_slabs, SUBLANES, LANES), F32), pltpu.VMEM((n_slabs, SUBLANES, LANES), F32),
                        pltpu.VMEM((POOL_HALO, W), F32)],
        operands=(dy, dpooled, xr, xi, xr, xi, proj, *consts),
        exchange=exchange,
    )


def _mix_bwd_in(dproj, h1, dh2, g_pre, win, name):
    T, D = h1.shape
    tm = _token_tile(T)

    def body(dproj_ref, h_ref, dh2_ref, gpre_ref, win_ref, dh1_ref, dg_ref):
        dx, dg = _rms_bwd(h_ref[...], gpre_ref[...], _mm_nt(dproj_ref[...], win_ref[...]))
        _acc(dg_ref, dg, pl.program_id(0) == 0)
        dh1_ref[...] = dh2_ref[...] + dx

    outs, _ = _pallas(
        body,
        name=name,
        grid=(T // tm,),
        out_shape=[jax.ShapeDtypeStruct((T, D), F32), jax.ShapeDtypeStruct((1, D), F32)],
        in_specs=[_row_spec(tm, D), _row_spec(tm, D), _row_spec(tm, D), _full_spec((1, D)), _full_spec(win.shape)],
        out_specs=[_row_spec(tm, D), _full_spec((1, D))],
        operands=(dproj, h1, dh2, g_pre, win),
    )
    return outs


def _discretize(lam_re, lam_im, log_dt, b_re, b_im):
    dt = jnp.exp(log_dt)[:, None]
    decay = jnp.exp(lam_re * dt)
    ang = lam_im * dt
    a_re, a_im = decay * jnp.cos(ang), decay * jnp.sin(ang)
    nr = a_re - 1.0
    den = lam_re * lam_re + lam_im * lam_im
    q_re = (nr * lam_re + a_im * lam_im) / den
    q_im = (a_im * lam_re - nr * lam_im) / den
    bb_re = q_re[..., None] * b_re - q_im[..., None] * b_im
    bb_im = q_re[..., None] * b_im + q_im[..., None] * b_re
    return a_re, a_im, bb_re, bb_im


GROUPS_PER_CHUNK = 16


def _block_diag(w, rows_first):
    G = w.shape[0]
    nch = G // GROUPS_PER_CHUNK
    if not rows_first:
        w = jnp.swapaxes(w, 1, 2)
    p, q = w.shape[1], w.shape[2]
    eye = jnp.eye(GROUPS_PER_CHUNK, dtype=w.dtype)
    out = jnp.einsum("cgpq,gk->cgpkq", w.reshape(nch, GROUPS_PER_CHUNK, p, q), eye)
    return out.reshape(nch, GROUPS_PER_CHUNK * p, GROUPS_PER_CHUNK * q)


def _block_diag_extract(m, p, q, rows_first):
    nch = m.shape[0]
    eye = jnp.eye(GROUPS_PER_CHUNK, dtype=m.dtype)
    out = jnp.einsum("cgpkq,gk->cgpq", m.reshape(nch, GROUPS_PER_CHUNK, p, GROUPS_PER_CHUNK, q), eye).reshape(nch * GROUPS_PER_CHUNK, p, q)
    return out if rows_first else jnp.swapaxes(out, 1, 2)


def _cmul(ar, ai, br, bi):
    return ar * br - ai * bi, ar * bi + ai * br


def _powers(ar, ai, count):
    pr, pi = ar[None], ai[None]
    while pr.shape[0] < count:
        nr, ni = _cmul(pr, pi, pr[-1][None], pi[-1][None])
        pr, pi = jnp.concatenate([pr, nr]), jnp.concatenate([pi, ni])
    return pr[:count], pi[:count]


def _scan_tables(a_re, a_im, seg):
    n = a_re.size
    ns = n // LANES
    ar, ai = a_re.reshape(n), a_im.reshape(n)
    pr, pi = _powers(ar, ai, seg)
    jr, ji = _powers(pr[-1], pi[-1], SUBLANES)

    def bcast(v):
        return jnp.broadcast_to(v.reshape(ns, 1, LANES), (ns, SUBLANES, LANES))

    def per_sublane(vs):
        return jnp.transpose(vs.reshape(SUBLANES, ns, LANES), (1, 0, 2))

    tabs = jnp.stack([bcast(ar), bcast(ai), bcast(jr[0]), bcast(ji[0]), bcast(jr[1]), bcast(ji[1]), bcast(jr[3]), bcast(ji[3]),
                      per_sublane(jr), per_sublane(ji), per_sublane(jr[::-1]), per_sublane(ji[::-1])])

    def rows(vs):
        return jnp.broadcast_to(jnp.transpose(vs.reshape(seg, ns, 1, LANES), (1, 0, 2, 3)), (ns, seg, SUBLANES, LANES))

    return tabs, jnp.stack([rows(pr), rows(pi)])


SMALL = ("ffn1_pre_norm", "ffn1_post_norm", "mix_pre_norm", "mix_post_norm", "ssm_lambda_re", "ssm_lambda_im", "ssm_log_dt", "ssm_b_re", "ssm_b_im", "ssm_c_re", "ssm_c_im",
         "ssm_d", "ssm_w_glu", "pool_w", "pool_scale", "ssm_out_norm", "pool_out_norm", "ffn2_pre_norm", "ffn2_post_norm")
BIG = ("ffn1_w_gate", "ffn1_w_up", "ffn1_w_down", "w_in", "w_out", "ffn2_w_gate", "ffn2_w_up", "ffn2_w_down")
ORDER = ("meta_tokens", "ffn1_pre_norm", "ffn1_post_norm", "ffn1_w_gate", "ffn1_w_up", "ffn1_w_down", "mix_pre_norm", "mix_post_norm", "w_in", "ssm_lambda_re", "ssm_lambda_im",
         "ssm_log_dt", "ssm_b_re", "ssm_b_im", "ssm_c_re", "ssm_c_im", "ssm_d", "ssm_w_glu", "pool_w", "pool_scale", "ssm_out_norm", "pool_out_norm", "w_out", "ffn2_pre_norm",
         "ffn2_post_norm", "ffn2_w_gate", "ffn2_w_up", "ffn2_w_down")
PACK_ROWS = SUBLANES * 8
def _pack(arrays, rows):
    flat = jnp.concatenate([a.reshape(-1) for a in arrays])
    return jnp.pad(flat, (0, rows * LANES - flat.size)).reshape(rows, LANES)


def _unpack(packed, shapes):
    flat = packed.reshape(-1)
    out, off = [], 0
    for s in shapes:
        n = math.prod(s)
        out.append(flat[off : off + n].reshape(s))
        off += n
    return out


def _step(p, x, loss_target, m, v):
    D = x.shape[-1]
    chip = (2 * lax.axis_index("x") + lax.axis_index("y")).astype(jnp.int32)
    place = jnp.stack([chip, lax.axis_index("c").astype(jnp.int32)])

    def gather_buffer(w):
        own = w.reshape(1, 2, w.shape[0] // 2, w.shape[1])
        return lax.dynamic_update_slice(lax.empty((4,) + own.shape[1:], own.dtype), own, (chip, 0, 0, 0))

    def rows_of(n, a):
        return jnp.swapaxes(a[0], 0, 1) if n.endswith(("w_gate", "w_up")) else a[0]

    def rows_back(n, a):
        return (jnp.swapaxes(a, 0, 1) if n.endswith(("w_gate", "w_up")) else a)[None]

    def grad_view(g):
        return g.reshape(4, 2, g.shape[0] // 8, g.shape[1])

    def reduce_sum(got_sibling, views, tag):
        sums = [_add_own_half(v_, g_, place, f"{tag}_add_sibling_{k}") for k, (v_, g_) in enumerate(zip(views, got_sibling))]
        return [s[0] for s in sums], [s[1] for s in sums]

    def reduce_halves(parts, got_chips, tag):
        return [_add_chips(p_, g_, place, f"{tag}_add_chips_{k}") for k, (p_, g_) in enumerate(zip(parts, got_chips))]

    first_names = ("ffn1_w_gate", "ffn1_w_up", "ffn1_w_down")
    later_names = ("w_in", "w_out", "ffn2_w_gate", "ffn2_w_up", "ffn2_w_down")
    bufs = {n: gather_buffer(rows_of(n, p[n]).astype(MXU_DTYPE)) for n in BIG}
    gathered_weight = lambda g_: g_.reshape(-1, g_.shape[-1])
    got = _exchange_call(_Gather([bufs[n] for n in first_names] + [gather_buffer(p["meta_tokens"])]), "gather_first")
    full = {n: gathered_weight(g_) for n, g_ in zip(first_names, got)}
    meta = jnp.transpose(got[-1].reshape(4, N_META, -1), (1, 0, 2)).reshape(N_META, D)

    vec = lambda n: p[n].reshape(1, -1)
    G, N, H = p["ssm_b_re"].shape[1:]
    a_re, a_im, bb_re, bb_im = _discretize(p["ssm_lambda_re"][0], p["ssm_lambda_im"][0], p["ssm_log_dt"][0], p["ssm_b_re"][0], p["ssm_b_im"][0])
    tabs, pows = _scan_tables(a_re, a_im, MIX_TILE // SUBLANES)
    bf = lambda a: a.astype(MXU_DTYPE)
    bbre, bbim = bf(_block_diag(bb_re, False)), bf(_block_diag(bb_im, False))
    ccre, ccim = bf(_block_diag(p["ssm_c_re"][0], False)), bf(_block_diag(p["ssm_c_im"][0], False))
    wgv, wgg = bf(_block_diag(p["ssm_w_glu"][0][:, :, :H], True)), bf(_block_diag(p["ssm_w_glu"][0][:, :, H:], True))
    pw = bf(p["pool_w"][0])

    n_ffn_steps = (x.shape[1] + N_META) // FFN_TILE
    (h1, f1, ga1, si1, s1, n1), got = _ffn_fwd(
        x[0], vec("ffn1_pre_norm"), vec("ffn1_post_norm"), full["ffn1_w_gate"], full["ffn1_w_up"], full["ffn1_w_down"], "ffn1_fwd",
        exchange=_Gather([bufs[n] for n in later_names], mid_step=(3 * n_ffn_steps) // 4), meta=meta,
    )
    full.update({n: gathered_weight(g_) for n, g_ in zip(later_names, got)})
    proj, xr, xim, y, pooled, mixed, h2, n2, cat = _mix_fwd(
        h1, vec("mix_pre_norm"), vec("ssm_out_norm"), vec("pool_out_norm"), vec("mix_post_norm"), vec("ssm_d"), vec("pool_scale"), full["w_in"], full["w_out"],
        bbre, bbim, ccre, ccim, wgv, wgg, pw, tabs, pows, "mix_fwd",
    )
    (dh3, sq, f2, ga2, si2, s2, n3), _ = _ffn_fwd(
        h2, vec("ffn2_pre_norm"), vec("ffn2_post_norm"), full["ffn2_w_gate"], full["ffn2_w_up"], full["ffn2_w_down"], "ffn2_fwd", target=loss_target[0]
    )
    loss = lax.psum(0.5 * jnp.sum(sq) / D, ("x", "y", "c"))

    g, shared = {}, {}
    ffn_names = lambda tag: (tag + "_w_gate", tag + "_w_up", tag + "_w_down")

    da, db, df, dh2, g["ffn2_post_norm"], g["ffn2_pre_norm"] = _ffn_bwd(
        dh3, f2, ga2, si2, h2, vec("ffn2_post_norm"), vec("ffn2_pre_norm"), full["ffn2_w_gate"], full["ffn2_w_up"], full["ffn2_w_down"], "ffn2_bwd"
    )
    views2 = [
        grad_view(_tn_matmul(da, n3, "ffn2_dw_gate")[0]),
        grad_view(_tn_matmul(db, n3, "ffn2_dw_up")[0]),
        grad_view(_tn_matmul(s2, df, "ffn2_dw_down")[0]),
    ]
    (dy, dpooled, dmixed, g["mix_post_norm"], g["ssm_out_norm"], g["pool_out_norm"], g["ssm_d"], g["pool_scale"], dwgv, dwgg, g["pool_w"]), got = _mix_bwd_heads(
        dh2, mixed, y, pooled, proj, vec("ssm_out_norm"), vec("pool_out_norm"), vec("mix_post_norm"), vec("pool_scale"), full["w_out"], wgv, wgg, pw, "mix_bwd_heads",
        exchange=_SiblingScatter(views2),
    )
    parts2, wire2 = reduce_sum(got, views2, "ffn2")
    (dproj, dccre, dccim, dbbre, dbbim, dar, dai), got = _mix_bwd_scan(
        dy, dpooled, xr, xim, proj, vec("ssm_d"), bbre, bbim, ccre, ccim, tabs, pows, "mix_bwd_scan", exchange=_ChipScatter(wire2)
    )
    dh1, g["mix_pre_norm"] = _mix_bwd_in(dproj, h1, dh2, vec("mix_pre_norm"), full["w_in"], "mix_bwd_in")
    halves2 = reduce_halves(parts2, got, "ffn2")
    dw_in, got = _tn_matmul(n2, dproj, "dw_in", exchange=_SiblingShare(halves2))
    shared.update(zip(ffn_names("ffn2"), got))
    dw_out, _ = _tn_matmul(cat, dmixed, "dw_out")
    views_m = [grad_view(dw_in), grad_view(dw_out)]

    (da, db, df, g["ffn1_post_norm"]), got = _ffn_bwd_down(dh1, f1, ga1, si1, vec("ffn1_post_norm"), full["ffn1_w_down"], "ffn1_bwd_down", exchange=_SiblingScatter(views_m))
    parts_m, wire_m = reduce_sum(got, views_m, "mix")
    dw_down, got = _tn_matmul(s1, df, "ffn1_dw_down", exchange=_ChipScatter(wire_m))
    halves_m = reduce_halves(parts_m, got, "mix")
    views_d = [grad_view(dw_down)]
    ex = _Group([_SiblingShare(halves_m), _SiblingScatter(views_d)])
    dw_gate, got = _tn_matmul(da, n1, "ffn1_dw_gate", exchange=ex)
    got_m, got_d = ex.split(got)
    shared.update(zip(("w_in", "w_out"), got_m))
    parts_d, wire_d = reduce_sum(got_d, views_d, "ffn1_down")
    views_g = [grad_view(dw_gate)]
    ex = _Group([_ChipScatter(wire_d), _SiblingScatter(views_g)])
    dw_up, got = _tn_matmul(db, n1, "ffn1_dw_up", exchange=ex)
    got_d, got_g = ex.split(got)
    halves_d = reduce_halves(parts_d, got_d, "ffn1_down")
    parts_g, wire_g = reduce_sum(got_g, views_g, "ffn1_gate")
    views_u = [grad_view(dw_up)]
    ex = _Group([_SiblingShare(halves_d), _ChipScatter(wire_g), _SiblingScatter(views_u)])
    (grad_x, d_meta, g["ffn1_pre_norm"]), got = _ffn_bwd_up(da, db, x[0], meta, dh1, vec("ffn1_pre_norm"), full["ffn1_w_gate"], full["ffn1_w_up"], "ffn1_bwd_up", exchange=ex)
    got_d, got_g, got_u = ex.split(got)
    shared["ffn1_w_down"] = got_d[0]
    halves_g = reduce_halves(parts_g, got_g, "ffn1_gate")
    parts_u, wire_u = reduce_sum(got_u, views_u, "ffn1_up")
    grad_x = grad_x[None]

    g["ssm_c_re"] = _block_diag_extract(dccre, N, H, False)
    g["ssm_c_im"] = _block_diag_extract(dccim, N, H, False)
    g["ssm_w_glu"] = jnp.concatenate([_block_diag_extract(dwgv, H, H, True), _block_diag_extract(dwgg, H, H, True)], axis=-1)
    d_a_re, d_a_im = jnp.sum(dar, axis=1).reshape(G, N), jnp.sum(dai, axis=1).reshape(G, N)
    _, pull = jax.vjp(_discretize, p["ssm_lambda_re"][0], p["ssm_lambda_im"][0], p["ssm_log_dt"][0], p["ssm_b_re"][0], p["ssm_b_im"][0])
    g["ssm_lambda_re"], g["ssm_lambda_im"], g["ssm_log_dt"], g["ssm_b_re"], g["ssm_b_im"] = pull(
        (d_a_re, d_a_im, _block_diag_extract(dbbre, H, N, False), _block_diag_extract(dbbim, H, N, False))
    )

    small_shapes = [p[n].shape for n in SMALL] + [(N_META, D)]
    small_size = sum(math.prod(s) for s in small_shapes)
    rows = -(-small_size // (LANES * PACK_ROWS)) * PACK_ROWS
    views_s = [_pack([g[n] for n in SMALL] + [d_meta], rows).reshape(4, 2, rows // 8, LANES)]
    parts_s, wire_s = reduce_sum(_exchange_call(_SiblingScatter(views_s), "small_reduce_sibling"), views_s, "small")
    ex = _Group([_ChipScatter(wire_u + wire_s), _SiblingShare(halves_g)])
    got_c, got_g = ex.split(_exchange_call(ex, "tail_reduce_chips"))
    shared["ffn1_w_gate"] = got_g[0]
    got = _exchange_call(_SiblingShare(reduce_halves(parts_u + parts_s, got_c, "tail")), "tail_reduce_share")
    shared["ffn1_w_up"] = got[0]
    small_buf = lax.dynamic_update_slice(lax.empty((4,) + got[1].shape, F32), got[1][None], (chip, 0, 0, 0))
    small_all = _exchange_call(_Gather([small_buf]), "gather_small")[0].reshape(rows, LANES)
    grads = dict(zip(SMALL + ("meta_full",), _unpack(small_all, small_shapes)))
    grads["meta_tokens"] = lax.dynamic_slice_in_dim(grads.pop("meta_full"), chip * (D // 4), D // 4, axis=1)
    delta, new_m, new_v = {}, {}, {}
    for n in BIG:
        g_rows = shared[n].reshape(-1, shared[n].shape[-1])
        outs = _adamw(rows_of(n, p[n]), g_rows, rows_of(n, m[n]), rows_of(n, v[n]), "adamw_" + n)
        grads[n], delta[n], new_m[n], new_v[n] = (rows_back(n, a) for a in (g_rows, *outs))
    delta["meta_tokens"], new_m["meta_tokens"], new_v["meta_tokens"] = _adamw(p["meta_tokens"], grads["meta_tokens"], m["meta_tokens"], v["meta_tokens"], "adamw_meta_tokens")

    def as_2d(n, a):
        a = a.reshape(p[n].shape)[0]
        if n in ("ssm_b_re", "ssm_b_im"):
            a = jnp.swapaxes(a, 1, 2)
        return a.reshape(-1, a.shape[-1])

    def from_2d(n, a):
        if n in ("ssm_b_re", "ssm_b_im"):
            g_, n_, h_ = p[n].shape[1:]
            return jnp.swapaxes(a.reshape(g_, h_, n_), 1, 2)[None]
        return a.reshape(p[n].shape)

    outs = _adamw_many(*[[as_2d(n, t[n]) for n in SMALL] for t in (p, grads, m, v)], "adamw_small")
    for out, arrays in zip((delta, new_m, new_v), outs):
        out.update({n: from_2d(n, a) for n, a in zip(SMALL, arrays)})

    return (loss, grad_x, *[grads[n] for n in ORDER], *[delta[n] for n in ORDER], *[new_m[n] for n in ORDER], *[new_v[n] for n in ORDER])


def kernel(x, meta_tokens, ffn1_pre_norm, ffn1_post_norm, ffn1_w_gate, ffn1_w_up, ffn1_w_down, mix_pre_norm, mix_post_norm, w_in, ssm_lambda_re, ssm_lambda_im, ssm_log_dt, ssm_b_re, ssm_b_im, ssm_c_re, ssm_c_im, ssm_d, ssm_w_glu, pool_w, pool_scale, ssm_out_norm, pool_out_norm, w_out, ffn2_pre_norm, ffn2_post_norm, ffn2_w_gate, ffn2_w_up, ffn2_w_down, loss_target, m_meta_tokens, m_ffn1_pre_norm, m_ffn1_post_norm, m_ffn1_w_gate, m_ffn1_w_up, m_ffn1_w_down, m_mix_pre_norm, m_mix_post_norm, m_w_in, m_ssm_lambda_re, m_ssm_lambda_im, m_ssm_log_dt, m_ssm_b_re, m_ssm_b_im, m_ssm_c_re, m_ssm_c_im, m_ssm_d, m_ssm_w_glu, m_pool_w, m_pool_scale, m_ssm_out_norm, m_pool_out_norm, m_w_out, m_ffn2_pre_norm, m_ffn2_post_norm, m_ffn2_w_gate, m_ffn2_w_up, m_ffn2_w_down, v_meta_tokens, v_ffn1_pre_norm, v_ffn1_post_norm, v_ffn1_w_gate, v_ffn1_w_up, v_ffn1_w_down, v_mix_pre_norm, v_mix_post_norm, v_w_in, v_ssm_lambda_re, v_ssm_lambda_im, v_ssm_log_dt, v_ssm_b_re, v_ssm_b_im, v_ssm_c_re, v_ssm_c_im, v_ssm_d, v_ssm_w_glu, v_pool_w, v_pool_scale, v_ssm_out_norm, v_pool_out_norm, v_w_out, v_ffn2_pre_norm, v_ffn2_post_norm, v_ffn2_w_gate, v_ffn2_w_up, v_ffn2_w_down):
    args = locals()
    p = {n: args[n] for n in ORDER}
    m = {n: args["m_" + n] for n in ORDER}
    v = {n: args["v_" + n] for n in ORDER}
    return _step(p, x, loss_target, m, v)
```

```python
import math

import jax
import jax.numpy as jnp
from jax import lax
from jax.experimental import pallas as pl
from jax.experimental.pallas import tpu as pltpu

F32 = jnp.float32
MXU_DTYPE = jnp.bfloat16
WIRE_DTYPE = jnp.bfloat16

RMS_EPS = 1e-6
N_META = 16
POOL_WINDOWS = (2, 4, 8, 16)
POOL_HALO = 16
ADAM_LR, ADAM_B1, ADAM_B2, ADAM_EPS, ADAM_WD, ADAM_STEP = 0.001, 0.9, 0.999, 1e-08, 0.01, 10

LANES = 128
SUBLANES = 8
VMEM_LIMIT = 60 * 1024 * 1024
FFN_TILE = 432
FFN_CHUNK = 1024
TN_TILE = 1024
MIX_TILE = 216
MIX_SUBTILES = 2
SLAB_GROUP = 8
MESH = pl.DeviceIdType.MESH
ANY = pl.BlockSpec(memory_space=pl.ANY)


def _mm(a, b):
    return jnp.dot(a.astype(MXU_DTYPE), b.astype(MXU_DTYPE), preferred_element_type=F32)


def _mm_nt(a, b):
    return lax.dot_general(a.astype(MXU_DTYPE), b.astype(MXU_DTYPE), (((1,), (1,)), ((), ())), preferred_element_type=F32)


def _mm_tn(a, b):
    return lax.dot_general(a.astype(MXU_DTYPE), b.astype(MXU_DTYPE), (((0,), (0,)), ((), ())), preferred_element_type=F32)


def _rms_stat(x):
    return lax.rsqrt(jnp.mean(x * x, axis=-1, keepdims=True) + RMS_EPS)


def _rms_bwd(x, g, dy):
    r = _rms_stat(x)
    xh = x * r
    dg = jnp.sum(dy * xh, axis=0, keepdims=True)
    dxh = dy * g
    dx = r * (dxh - xh * jnp.mean(dxh * xh, axis=-1, keepdims=True))
    return dx, dg


def _sigmoid(x):
    return 1.0 / (1.0 + jnp.exp(-x))


GELU_C = math.sqrt(2.0 / math.pi)
GELU_K = 0.044715


def _gelu(y):
    return 0.5 * y * (1.0 + jnp.tanh(GELU_C * (y + GELU_K * y * y * y)))


def _gelu_grad(y):
    th = jnp.tanh(GELU_C * (y + GELU_K * y * y * y))
    return 0.5 * (1.0 + th) + 0.5 * y * (1.0 - th * th) * GELU_C * (1.0 + 3.0 * GELU_K * y * y)


def _row_spec(tile, cols, rev_n=None):
    if rev_n is None:
        return pl.BlockSpec((tile, cols), lambda i: (i, 0))
    return pl.BlockSpec((tile, cols), lambda i: (rev_n - 1 - i, 0))


def _full_spec(shape, single=False):
    zeros = (0,) * len(shape)
    if single:
        return pl.BlockSpec(shape, lambda *_: zeros, pipeline_mode=pl.Buffered(1))
    return pl.BlockSpec(shape, lambda *_: zeros)


def _acc(ref, val, first):
    @pl.when(first)
    def _():
        ref[...] = val

    @pl.when(jnp.logical_not(first))
    def _():
        ref[...] += val


def _place():
    x, y, c = lax.axis_index("x"), lax.axis_index("y"), lax.axis_index("c")
    others = [(1 - x, y), (x, 1 - y), (1 - x, 1 - y)]
    return x, y, c, others


class _Exchange:
    mid_step = None

    def __init__(self, ins, out_shapes, aliases, n_sems):
        self.ins, self.out_shapes, self.aliases, self.n_sems = list(ins), list(out_shapes), dict(aliases), n_sems

    def mid(self, ins, outs, send_sems, recv_sems):
        pass


class _SiblingScatter(_Exchange):
    def __init__(self, views):
        super().__init__(views, [jax.ShapeDtypeStruct((4,) + v.shape[2:], v.dtype) for v in views], {}, 4 * len(views))

    def _copies(self, ins, outs, send_sems, recv_sems):
        x, y, c, _ = _place()
        return [
            pltpu.make_async_remote_copy(src_ref=ins[a].at[k, 1 - c], dst_ref=outs[a].at[k], send_sem=send_sems.at[4 * a + k], recv_sem=recv_sems.at[4 * a + k], device_id=(x, y, 1 - c), device_id_type=MESH)
            for a in range(len(ins))
            for k in range(4)
        ]

    def start(self, *refs):
        for cp in self._copies(*refs):
            cp.start()

    def finish(self, *refs):
        cps = self._copies(*refs)
        for cp in cps:
            cp.wait_recv()
        for cp in cps:
            cp.wait_send()


class _ChipScatter(_Exchange):
    def __init__(self, parts):
        super().__init__(parts, [jax.ShapeDtypeStruct((3,) + p.shape[1:], p.dtype) for p in parts], {}, 3 * len(parts))

    def _copies(self, ins, outs, send_sems, recv_sems):
        x, y, c, others = _place()
        return [
            pltpu.make_async_remote_copy(src_ref=ins[a].at[2 * chip[0] + chip[1]], dst_ref=outs[a].at[j], send_sem=send_sems.at[3 * a + j], recv_sem=recv_sems.at[3 * a + j], device_id=(*chip, c), device_id_type=MESH)
            for a in range(len(ins))
            for j, chip in enumerate(others)
        ]

    start = _SiblingScatter.start
    finish = _SiblingScatter.finish


class _SiblingShare(_Exchange):
    def __init__(self, bufs):
        super().__init__(bufs, [jax.ShapeDtypeStruct(b.shape, b.dtype) for b in bufs], {a: a for a in range(len(bufs))}, len(bufs))

    def _copy(self, outs, send_sems, recv_sems, a, half):
        x, y, c, _ = _place()
        mine = outs[a].at[c if half == "mine" else 1 - c]
        return pltpu.make_async_remote_copy(src_ref=mine, dst_ref=mine, send_sem=send_sems.at[a], recv_sem=recv_sems.at[a], device_id=(x, y, 1 - c), device_id_type=MESH)

    def start(self, ins, outs, send_sems, recv_sems):
        for a in range(len(outs)):
            self._copy(outs, send_sems, recv_sems, a, "mine").start()

    def finish(self, ins, outs, send_sems, recv_sems):
        for a in range(len(outs)):
            self._copy(outs, send_sems, recv_sems, a, "theirs").wait_recv()
        for a in range(len(outs)):
            self._copy(outs, send_sems, recv_sems, a, "mine").wait_send()


class _Gather(_Exchange):
    def __init__(self, bufs, mid_step=None):
        super().__init__(bufs, [jax.ShapeDtypeStruct(b.shape, b.dtype) for b in bufs], {a: a for a in range(len(bufs))}, 6 * len(bufs))
        self.mid_step = mid_step

    def _copy(self, outs, send_sems, recv_sems, a, j, chip, half, to):
        blk = outs[a].at[2 * chip[0] + chip[1], half]
        return pltpu.make_async_remote_copy(src_ref=blk, dst_ref=blk, send_sem=send_sems.at[6 * a + j], recv_sem=recv_sems.at[6 * a + j], device_id=to, device_id_type=MESH)

    def start(self, ins, outs, send_sems, recv_sems):
        x, y, c, others = _place()
        for a in range(len(outs)):
            for j, chip in enumerate(others):
                self._copy(outs, send_sems, recv_sems, a, j, (x, y), c, (*chip, c)).start()

    def mid(self, ins, outs, send_sems, recv_sems):
        x, y, c, others = _place()
        for a in range(len(outs)):
            for j, chip in enumerate(others):
                self._copy(outs, send_sems, recv_sems, a, j, chip, c, (x, y, c)).wait_recv()
                self._copy(outs, send_sems, recv_sems, a, 3 + j, chip, c, (x, y, 1 - c)).start()

    def finish(self, ins, outs, send_sems, recv_sems):
        x, y, c, others = _place()
        for a in range(len(outs)):
            for j, chip in enumerate(others):
                self._copy(outs, send_sems, recv_sems, a, 3 + j, chip, 1 - c, (x, y, c)).wait_recv()
        for a in range(len(outs)):
            for j, chip in enumerate(others):
                self._copy(outs, send_sems, recv_sems, a, j, (x, y), c, (*chip, c)).wait_send()
                self._copy(outs, send_sems, recv_sems, a, 3 + j, chip, c, (x, y, 1 - c)).wait_send()


class _SemSlice:
    def __init__(self, sems, off):
        self.sems, self.off = sems, off

    @property
    def at(self):
        return self

    def __getitem__(self, i):
        return self.sems.at[self.off + i]


class _Group(_Exchange):
    def __init__(self, exchanges):
        ins, outs, aliases, n_sems, self.spans = [], [], {}, 0, []
        for ex in exchanges:
            self.spans.append((len(ins), len(outs), n_sems))
            aliases.update({len(ins) + i: len(outs) + o for i, o in ex.aliases.items()})
            ins, outs, n_sems = ins + ex.ins, outs + ex.out_shapes, n_sems + ex.n_sems
        super().__init__(ins, outs, aliases, n_sems)
        self.exchanges = exchanges
        mids = {ex.mid_step for ex in exchanges if ex.mid_step is not None}
        self.mid_step = mids.pop() if mids else None

    def _each(self, method, ins, outs, send_sems, recv_sems):
        for ex, (i0, o0, s0) in zip(self.exchanges, self.spans):
            getattr(ex, method)(ins[i0 : i0 + len(ex.ins)], outs[o0 : o0 + len(ex.out_shapes)], _SemSlice(send_sems, s0), _SemSlice(recv_sems, s0))

    def start(self, *refs):
        self._each("start", *refs)

    def mid(self, *refs):
        self._each("mid", *refs)

    def finish(self, *refs):
        self._each("finish", *refs)

    def split(self, outs):
        return [outs[o0 : o0 + len(ex.out_shapes)] for ex, (_, o0, _) in zip(self.exchanges, self.spans)]


def _exchange_call(ex, name):
    n, m = len(ex.ins), len(ex.out_shapes)

    def body(*refs):
        parts = (refs[:n], refs[n : n + m], refs[n + m], refs[n + m + 1])
        ex.start(*parts)
        ex.mid(*parts)
        ex.finish(*parts)

    return pl.pallas_call(
        body,
        name=name,
        out_shape=ex.out_shapes,
        in_specs=[ANY] * n,
        out_specs=[ANY] * m,
        scratch_shapes=[pltpu.SemaphoreType.DMA((ex.n_sems,)), pltpu.SemaphoreType.DMA((ex.n_sems,))],
        input_output_aliases=ex.aliases,
    )(*ex.ins)


def _pallas(body, *, name, grid, in_specs, out_specs, out_shape, operands, scratch_shapes=(), exchange=None):
    params = pltpu.CompilerParams(dimension_semantics=("arbitrary",) * len(grid), vmem_limit_bytes=VMEM_LIMIT)
    if exchange is None:
        outs = pl.pallas_call(body, name=name, grid=grid, in_specs=in_specs, out_specs=out_specs, out_shape=out_shape, scratch_shapes=list(scratch_shapes), compiler_params=params)(*operands)
        return outs, []
    ex = exchange
    n_in, n_out, n_scr = len(in_specs), len(out_specs), len(scratch_shapes)
    x_in, x_out = len(ex.ins), len(ex.out_shapes)

    def hosted(*refs):
        ins, x_ins = refs[:n_in], refs[n_in : n_in + x_in]
        outs, x_outs = refs[n_in + x_in : n_in + x_in + n_out], refs[n_in + x_in + n_out : n_in + x_in + n_out + x_out]
        rest = refs[n_in + x_in + n_out + x_out :]
        parts = (x_ins, x_outs, rest[n_scr], rest[n_scr + 1])
        ids = [pl.program_id(d) for d in range(len(grid))]
        first = _all([i == 0 for i in ids])
        last = _all([i == g - 1 for i, g in zip(ids, grid)])

        @pl.when(first)
        def _():
            ex.start(*parts)

        body(*ins, *outs, *rest[:n_scr])

        if ex.mid_step is not None:

            @pl.when(ids[0] == ex.mid_step)
            def _():
                ex.mid(*parts)

        @pl.when(last)
        def _():
            ex.finish(*parts)

    outs = pl.pallas_call(
        hosted,
        name=name,
        grid=grid,
        in_specs=list(in_specs) + [ANY] * x_in,
        out_specs=list(out_specs) + [ANY] * x_out,
        out_shape=list(out_shape) + ex.out_shapes,
        scratch_shapes=list(scratch_shapes) + [pltpu.SemaphoreType.DMA((ex.n_sems,)), pltpu.SemaphoreType.DMA((ex.n_sems,))],
        input_output_aliases={n_in + i: n_out + o for i, o in ex.aliases.items()},
        compiler_params=params,
    )(*operands, *ex.ins)
    return outs[:n_out], outs[n_out:]


def _all(conds):
    out = conds[0]
    for c in conds[1:]:
        out = jnp.logical_and(out, c)
    return out


def _row_tile(rows):
    if rows <= 512:
        return rows
    for t in (512, 352, 256, 176, 128, 112, 64, 32, 16, 8):
        if rows % t == 0:
            return t
    return rows


def _add_own_half(view, got, place, name):
    _, _, r, c = view.shape
    tr = _row_tile(r)

    def body(place_ref, v_ref, g_ref, o_ref, w_ref):
        s = v_ref[...] + g_ref[...]
        o_ref[...] = s
        w_ref[...] = s.astype(w_ref.dtype)

    blk = pl.BlockSpec((None, tr, c), lambda k, i, pr: (k, i, 0))
    return pl.pallas_call(
        body,
        name=name,
        out_shape=[jax.ShapeDtypeStruct((4, r, c), F32), jax.ShapeDtypeStruct((4, r, c), WIRE_DTYPE)],
        grid_spec=pltpu.PrefetchScalarGridSpec(
            num_scalar_prefetch=1, grid=(4, r // tr), in_specs=[pl.BlockSpec((None, None, tr, c), lambda k, i, pr: (k, pr[1], i, 0)), blk], out_specs=[blk, blk]
        ),
        compiler_params=pltpu.CompilerParams(dimension_semantics=("arbitrary", "arbitrary"), vmem_limit_bytes=VMEM_LIMIT),
    )(place, view, got)


def _add_chips(part, got, place, name):
    _, r, c = part.shape
    tr = _row_tile(r)

    def body(place_ref, p_ref, g_ref, o_ref):
        o_ref[...] = ((p_ref[...] + g_ref[0].astype(F32)) + g_ref[1].astype(F32)) + g_ref[2].astype(F32)

    return pl.pallas_call(
        body,
        name=name,
        out_shape=jax.ShapeDtypeStruct((2, r, c), F32),
        grid_spec=pltpu.PrefetchScalarGridSpec(
            num_scalar_prefetch=1,
            grid=(r // tr,),
            in_specs=[pl.BlockSpec((None, tr, c), lambda i, pr: (pr[0], i, 0)), pl.BlockSpec((3, tr, c), lambda i, pr: (0, i, 0))],
            out_specs=pl.BlockSpec((None, tr, c), lambda i, pr: (pr[1], i, 0)),
        ),
        compiler_params=pltpu.CompilerParams(dimension_semantics=("arbitrary",), vmem_limit_bytes=VMEM_LIMIT),
    )(place, part, got)


def _adamw_update(w_ref, g_ref, m_ref, v_ref, d_ref, nm_ref, nv_ref):
    g = g_ref[...]
    nm = ADAM_B1 * m_ref[...] + (1.0 - ADAM_B1) * g
    nv = ADAM_B2 * v_ref[...] + (1.0 - ADAM_B2) * (g * g)
    m_hat = nm / (1.0 - ADAM_B1**ADAM_STEP)
    v_hat = nv / (1.0 - ADAM_B2**ADAM_STEP)
    d_ref[...] = -ADAM_LR * (m_hat / (jnp.sqrt(v_hat) + ADAM_EPS) + ADAM_WD * w_ref[...])
    nm_ref[...] = nm
    nv_ref[...] = nv


def _adamw(w, g, m, v, name):
    r, c = w.shape
    tr = _row_tile(r)
    spec = pl.BlockSpec((tr, c), lambda i: (i, 0))
    outs, _ = _pallas(_adamw_update, name=name, grid=(r // tr,), in_specs=[spec] * 4, out_specs=[spec] * 3, out_shape=[jax.ShapeDtypeStruct((r, c), F32)] * 3, operands=(w, g, m, v))
    return outs


def _adamw_many(ws, gs, ms, vs, name):
    n = len(ws)

    def body(*refs):
        for k in range(n):
            _adamw_update(*(refs[j * n + k] for j in range(7)))

    outs = pl.pallas_call(
        body,
        name=name,
        out_shape=[jax.ShapeDtypeStruct(w.shape, F32) for w in ws] * 3,
        in_specs=[pl.BlockSpec(memory_space=pltpu.VMEM)] * (4 * n),
        out_specs=[pl.BlockSpec(memory_space=pltpu.VMEM)] * (3 * n),
    )(*ws, *gs, *ms, *vs)
    return outs[:n], outs[n : 2 * n], outs[2 * n :]


def _load_weights(pairs, sems):
    @pl.when(pl.program_id(0) == 0)
    def _():
        cps = [pltpu.make_async_copy(src, dst, sems.at[k]) for k, (src, dst) in enumerate(pairs)]
        for cp in cps:
            cp.start()
        for cp in cps:
            cp.wait()


def _ffn_chunks(F):
    bounds = list(range(0, F, FFN_CHUNK)) + [F]
    return list(zip(bounds[:-1], bounds[1:]))


def _shifted_specs(tm, cols):
    per = tm // N_META
    return [_row_spec(tm, cols), pl.BlockSpec((N_META, cols), lambda i: (jnp.maximum(i * per - 1, 0), 0))]


def _shifted_tile(cur_ref, before_ref, tm):
    return jnp.concatenate([before_ref[...], cur_ref[0 : tm - N_META, :]], axis=0)


def _tokens_tile(cur_ref, before_ref, meta_ref, tm, tile_0):
    first = jnp.where(tile_0, meta_ref[...], before_ref[...])
    return jnp.concatenate([first, cur_ref[0 : tm - N_META, :]], axis=0)


def _ffn_fwd(h, g_pre, g_post, wg, wu, wd, name, exchange=None, meta=None, target=None):
    D = h.shape[1]
    T = h.shape[0] + (0 if meta is None else N_META)
    F = wg.shape[0]
    tm = FFN_TILE
    n_src = 1 if meta is None else 3
    n_tgt = 0 if target is None else 2
    n_head = 1 if target is None else 2

    def body(*refs):
        src, refs = refs[:n_src], refs[n_src:]
        tgt, refs = refs[:n_tgt], refs[n_tgt:]
        (gpre_ref, gpost_ref, wg_hbm, wu_hbm, wd_hbm), refs = refs[:5], refs[5:]
        head, (f_ref, ga_ref, si_ref, s_ref, n_ref, wg_v, wu_v, wd_v, sems) = refs[:n_head], refs[n_head:]
        i = pl.program_id(0)
        _load_weights([(wg_hbm, wg_v), (wu_hbm, wu_v), (wd_hbm, wd_v)], sems)
        hh = src[0][...] if meta is None else _tokens_tile(src[0], src[1], src[2], tm, i == 0)
        n = (hh * _rms_stat(hh) * gpre_ref[...]).astype(MXU_DTYPE)
        n_ref[...] = n.astype(n_ref.dtype)
        f = jnp.zeros((tm, D), F32)
        for lo, hi in _ffn_chunks(F):
            a = _mm_nt(n, wg_v[lo:hi, :])
            b = _mm_nt(n, wu_v[lo:hi, :])
            sg = _sigmoid(a)
            si = a * sg
            s = (si * b).astype(MXU_DTYPE)
            ga_ref[:, lo:hi] = (b * (sg * (1.0 + a * (1.0 - sg)))).astype(ga_ref.dtype)
            si_ref[:, lo:hi] = si.astype(si_ref.dtype)
            s_ref[:, lo:hi] = s.astype(s_ref.dtype)
            f = f + _mm(s, wd_v[lo:hi, :])
        f_ref[...] = f
        out = hh + 0.5 * (f * _rms_stat(f) * gpost_ref[...])
        if target is None:
            head[0][...] = out
        else:
            rows = i * tm + lax.broadcasted_iota(jnp.int32, (tm, D), 0)
            err = jnp.where(rows >= N_META, out - _shifted_tile(tgt[0], tgt[1], tm), 0.0)
            head[0][...] = err * (1.0 / D)
            _acc(head[1], jnp.sum(err * err, axis=0, keepdims=True), i == 0)

    tok = jax.ShapeDtypeStruct((T, D), F32)
    act = jax.ShapeDtypeStruct((T, F), MXU_DTYPE)
    src_specs = [_row_spec(tm, D)] if meta is None else _shifted_specs(tm, D) + [_full_spec((N_META, D))]
    src_ops = (h,) if meta is None else (h, h, meta)
    tgt_specs, tgt_ops = ([], ()) if target is None else (_shifted_specs(tm, D), (target, target))
    head_shapes = [tok] if target is None else [tok, jax.ShapeDtypeStruct((1, D), F32)]
    head_specs = [_row_spec(tm, D)] if target is None else [_row_spec(tm, D), _full_spec((1, D))]
    return _pallas(
        body,
        name=name,
        grid=(T // tm,),
        out_shape=head_shapes + [tok, act, act, act, jax.ShapeDtypeStruct((T, D), MXU_DTYPE)],
        in_specs=src_specs + tgt_specs + [_full_spec((1, D)), _full_spec((1, D)), ANY, ANY, ANY],
        out_specs=head_specs + [_row_spec(tm, D), _row_spec(tm, F), _row_spec(tm, F), _row_spec(tm, F), _row_spec(tm, D)],
        scratch_shapes=[pltpu.VMEM(wg.shape, wg.dtype), pltpu.VMEM(wu.shape, wu.dtype), pltpu.VMEM(wd.shape, wd.dtype), pltpu.SemaphoreType.DMA((3,))],
        operands=(*src_ops, *tgt_ops, g_pre, g_post, wg, wu, wd),
        exchange=exchange,
    )


def _ffn_bwd_down(dh, f, ga, si, g_post, wd, name, exchange=None):
    T, D = dh.shape
    F = wd.shape[0]
    tm = FFN_TILE

    def body(dh_ref, f_ref, ga_ref, si_ref, gpost_ref, wd_hbm, da_ref, db_ref, df_ref, dg_ref, wd_v, sems):
        _load_weights([(wd_hbm, wd_v)], sems)
        df, dg = _rms_bwd(f_ref[...], gpost_ref[...], 0.5 * dh_ref[...])
        _acc(dg_ref, dg, pl.program_id(0) == 0)
        dfb = df.astype(MXU_DTYPE)
        df_ref[...] = dfb.astype(df_ref.dtype)
        for lo, hi in _ffn_chunks(F):
            ds = _mm_nt(dfb, wd_v[lo:hi, :])
            da_ref[:, lo:hi] = (ds * ga_ref[:, lo:hi].astype(F32)).astype(da_ref.dtype)
            db_ref[:, lo:hi] = (ds * si_ref[:, lo:hi].astype(F32)).astype(db_ref.dtype)

    act = jax.ShapeDtypeStruct((T, F), MXU_DTYPE)
    return _pallas(
        body,
        name=name,
        grid=(T // tm,),
        out_shape=[act, act, jax.ShapeDtypeStruct((T, D), MXU_DTYPE), jax.ShapeDtypeStruct((1, D), F32)],
        in_specs=[_row_spec(tm, D), _row_spec(tm, D), _row_spec(tm, F), _row_spec(tm, F), _full_spec((1, D)), ANY],
        out_specs=[_row_spec(tm, F), _row_spec(tm, F), _row_spec(tm, D), _full_spec((1, D))],
        scratch_shapes=[pltpu.VMEM(wd.shape, wd.dtype), pltpu.SemaphoreType.DMA((1,))],
        operands=(dh, f, ga, si, g_post, wd),
        exchange=exchange,
    )


def _ffn_bwd_up(da, db, x, meta, dh, g_pre, wg, wu, name, exchange=None):
    D = x.shape[1]
    T = x.shape[0] + N_META
    F = wg.shape[0]
    tm = FFN_TILE
    nt = T // tm
    per = tm // N_META
    tile = lambda i: jnp.minimum(i, nt - 1)

    def body(da_ref, db_ref, x_ref, xb_ref, meta_ref, dh_ref, gpre_ref, wg_hbm, wu_hbm, dx_ref, dmeta_ref, dg_ref, wg_v, wu_v, sems, held):
        i = pl.program_id(0)
        _load_weights([(wg_hbm, wg_v), (wu_hbm, wu_v)], sems)

        @pl.when(i < nt)
        def _():
            dn = jnp.zeros((tm, D), F32)
            for lo, hi in _ffn_chunks(F):
                dn = dn + _mm(da_ref[:, lo:hi], wg_v[lo:hi, :]) + _mm(db_ref[:, lo:hi], wu_v[lo:hi, :])
            dx, dg = _rms_bwd(_tokens_tile(x_ref, xb_ref, meta_ref, tm, i == 0), gpre_ref[...], dn)
            _acc(dg_ref, dg, i == 0)
            dh_in = dh_ref[...] + dx

            @pl.when(i == 0)
            def _():
                dmeta_ref[...] = dh_in[0:N_META, :]

            @pl.when(i > 0)
            def _():
                dx_ref[...] = jnp.concatenate([held[...], dh_in[0:N_META, :]], axis=0)

            held[...] = dh_in[N_META:, :]

        @pl.when(i == nt)
        def _():
            dx_ref[0 : tm - N_META, :] = held[...]

    rows = lambda cols: pl.BlockSpec((tm, cols), lambda i: (tile(i), 0))
    return _pallas(
        body,
        name=name,
        grid=(nt + 1,),
        out_shape=[jax.ShapeDtypeStruct((T - N_META, D), F32), jax.ShapeDtypeStruct((N_META, D), F32), jax.ShapeDtypeStruct((1, D), F32)],
        in_specs=[rows(F), rows(F), rows(D), pl.BlockSpec((N_META, D), lambda i: (jnp.maximum(tile(i) * per - 1, 0), 0)), _full_spec((N_META, D)), rows(D), _full_spec((1, D)), ANY, ANY],
        out_specs=[pl.BlockSpec((tm, D), lambda i: (jnp.maximum(i - 1, 0), 0)), _full_spec((N_META, D)), _full_spec((1, D))],
        scratch_shapes=[pltpu.VMEM(wg.shape, wg.dtype), pltpu.VMEM(wu.shape, wu.dtype), pltpu.SemaphoreType.DMA((2,)), pltpu.VMEM((tm - N_META, D), F32)],
        operands=(da, db, x, x, meta, dh, g_pre, wg, wu),
        exchange=exchange,
    )


def _ffn_bwd(dh, f, ga, si, h, g_post, g_pre, wg, wu, wd, name):
    T, D = dh.shape
    F = wd.shape[0]
    tm = FFN_TILE

    def body(dh_ref, f_ref, ga_ref, si_ref, h_ref, gpost_ref, gpre_ref, wg_hbm, wu_hbm, wd_hbm, da_ref, db_ref, df_ref, dhin_ref, dgpost_ref, dgpre_ref, wg_v, wu_v, wd_v, sems):
        first = pl.program_id(0) == 0
        _load_weights([(wg_hbm, wg_v), (wu_hbm, wu_v), (wd_hbm, wd_v)], sems)
        dh = dh_ref[...]
        df, dg = _rms_bwd(f_ref[...], gpost_ref[...], 0.5 * dh)
        _acc(dgpost_ref, dg, first)
        dfb = df.astype(MXU_DTYPE)
        df_ref[...] = dfb.astype(df_ref.dtype)
        dn = jnp.zeros((tm, D), F32)
        for lo, hi in _ffn_chunks(F):
            ds = _mm_nt(dfb, wd_v[lo:hi, :])
            da = (ds * ga_ref[:, lo:hi].astype(F32)).astype(MXU_DTYPE)
            db = (ds * si_ref[:, lo:hi].astype(F32)).astype(MXU_DTYPE)
            da_ref[:, lo:hi] = da.astype(da_ref.dtype)
            db_ref[:, lo:hi] = db.astype(db_ref.dtype)
            dn = dn + _mm(da, wg_v[lo:hi, :]) + _mm(db, wu_v[lo:hi, :])
        dx, dg = _rms_bwd(h_ref[...], gpre_ref[...], dn)
        _acc(dgpre_ref, dg, first)
        dhin_ref[...] = dh + dx

    act = jax.ShapeDtypeStruct((T, F), MXU_DTYPE)
    vec = jax.ShapeDtypeStruct((1, D), F32)
    outs, _ = _pallas(
        body,
        name=name,
        grid=(T // tm,),
        out_shape=[act, act, jax.ShapeDtypeStruct((T, D), MXU_DTYPE), jax.ShapeDtypeStruct((T, D), F32), vec, vec],
        in_specs=[_row_spec(tm, D), _row_spec(tm, D), _row_spec(tm, F), _row_spec(tm, F), _row_spec(tm, D), _full_spec((1, D)), _full_spec((1, D)), ANY, ANY, ANY],
        out_specs=[_row_spec(tm, F), _row_spec(tm, F), _row_spec(tm, D), _row_spec(tm, D), _full_spec((1, D)), _full_spec((1, D))],
        scratch_shapes=[pltpu.VMEM(wg.shape, wg.dtype), pltpu.VMEM(wu.shape, wu.dtype), pltpu.VMEM(wd.shape, wd.dtype), pltpu.SemaphoreType.DMA((3,))],
        operands=(dh, f, ga, si, h, g_post, g_pre, wg, wu, wd),
    )
    return outs


def _token_tile(T):
    for t in (912, 864, 432):
        if T % t == 0:
            return t
    raise ValueError(f"no token tile for {T} rows")


def _tn_matmul(xm, ym, name, exchange=None):
    T, M = xm.shape
    N = ym.shape[1]
    if (T - N_META) % TN_TILE:
        tk = _token_tile(T)

        def body(x_ref, y_ref, o_ref):
            _acc(o_ref, _mm_tn(x_ref[...], y_ref[...]), pl.program_id(0) == 0)

        grid, operands = (T // tk,), (xm, ym)
        in_specs = [pl.BlockSpec((tk, M), lambda k: (k, 0)), pl.BlockSpec((tk, N), lambda k: (k, 0))]
    else:
        tk = TN_TILE

        def body(x_ref, y_ref, xh_ref, yh_ref, o_ref):
            prod = _mm_tn(x_ref[...], y_ref[...])

            @pl.when(pl.program_id(0) == 0)
            def _():
                o_ref[...] = prod + _mm_tn(xh_ref[...], yh_ref[...])

            @pl.when(pl.program_id(0) > 0)
            def _():
                o_ref[...] += prod

        grid, operands = ((T - N_META) // tk,), (xm, ym, xm, ym)
        start = lambda k: (pl.multiple_of(N_META + k * tk, N_META), 0)
        in_specs = [pl.BlockSpec((pl.Element(tk), pl.Element(M)), start), pl.BlockSpec((pl.Element(tk), pl.Element(N)), start),
                    pl.BlockSpec((N_META, M), lambda k: (0, 0)), pl.BlockSpec((N_META, N), lambda k: (0, 0))]

    (out,), x_outs = _pallas(
        body,
        name=name,
        grid=grid,
        out_shape=[jax.ShapeDtypeStruct((M, N), F32)],
        in_specs=in_specs,
        out_specs=[_full_spec((M, N))],
        operands=operands,
        exchange=exchange,
    )
    return out, x_outs


TAB_A, TAB_AS1, TAB_AS2, TAB_AS4, TAB_JF, TAB_JB = 0, 2, 4, 6, 8, 10


def _scan_inplace(zr, zi, tabs, pows, car_r, car_i, seg, reverse, base=0):
    n_slabs = zr.shape[0]
    sgn = -1.0 if reverse else 1.0
    row = lax.broadcasted_iota(jnp.int32, (SUBLANES, LANES), 0)

    def cmul(pr, pi, xr, xi):
        return pr * xr - pi * xi, pr * xi + pi * xr

    for k0 in range(0, n_slabs, SLAB_GROUP):
        slabs = range(k0, min(k0 + SLAB_GROUP, n_slabs))
        ar = [tabs[TAB_A, k] for k in slabs]
        ai = [sgn * tabs[TAB_A + 1, k] for k in slabs]

        def first_pass(t, carry):
            r = (seg - 1 - t) if reverse else t
            out = []
            for q, k in enumerate(slabs):
                xr, xi = carry[2 * q], carry[2 * q + 1]
                pr, pi = cmul(ar[q], ai[q], xr, xi)
                nr = pr + zr[k, pl.ds(base + r, SUBLANES, stride=seg), :]
                ni = pi + zi[k, pl.ds(base + r, SUBLANES, stride=seg), :]
                zr[k, pl.ds(base + r, SUBLANES, stride=seg), :] = nr
                zi[k, pl.ds(base + r, SUBLANES, stride=seg), :] = ni
                out += [nr, ni]
            return tuple(out)

        ends = lax.fori_loop(0, seg, first_pass, tuple(jnp.zeros((SUBLANES, LANES), F32) for _ in range(2 * len(slabs))))

        incoming = []
        for q, k in enumerate(slabs):
            fr, fi = ends[2 * q], ends[2 * q + 1]
            for d, tab in ((1, TAB_AS1), (2, TAB_AS2), (4, TAB_AS4)):
                shift, keep = (SUBLANES - d, row < SUBLANES - d) if reverse else (d, row >= d)
                sr = jnp.where(keep, pltpu.roll(fr, shift, 0), 0.0)
                si = jnp.where(keep, pltpu.roll(fi, shift, 0), 0.0)
                pr, pi = cmul(tabs[tab, k], sgn * tabs[tab + 1, k], sr, si)
                fr, fi = fr + pr, fi + pi
            cr, ci = car_r[k], car_i[k]
            jtab = TAB_JB if reverse else TAB_JF
            pr, pi = cmul(tabs[jtab, k], sgn * tabs[jtab + 1, k], cr, ci)
            er, ei = fr + pr, fi + pi
            if reverse:
                inr = jnp.where(row < SUBLANES - 1, pltpu.roll(er, SUBLANES - 1, 0), cr)
                ini = jnp.where(row < SUBLANES - 1, pltpu.roll(ei, SUBLANES - 1, 0), ci)
                car_r[k] = jnp.broadcast_to(er[0:1, :], (SUBLANES, LANES))
                car_i[k] = jnp.broadcast_to(ei[0:1, :], (SUBLANES, LANES))
            else:
                inr = jnp.where(row >= 1, pltpu.roll(er, 1, 0), cr)
                ini = jnp.where(row >= 1, pltpu.roll(ei, 1, 0), ci)
                car_r[k] = jnp.broadcast_to(er[SUBLANES - 1 : SUBLANES, :], (SUBLANES, LANES))
                car_i[k] = jnp.broadcast_to(ei[SUBLANES - 1 : SUBLANES, :], (SUBLANES, LANES))
            incoming += [inr, ini]

        def second_pass(r, _):
            p = (seg - 1 - r) if reverse else r
            for q, k in enumerate(slabs):
                pr, pi = cmul(pows[0, k, p], sgn * pows[1, k, p], incoming[2 * q], incoming[2 * q + 1])
                zr[k, pl.ds(base + r, SUBLANES, stride=seg), :] = zr[k, pl.ds(base + r, SUBLANES, stride=seg), :] + pr
                zi[k, pl.ds(base + r, SUBLANES, stride=seg), :] = zi[k, pl.ds(base + r, SUBLANES, stride=seg), :] + pi
            return 0

        lax.fori_loop(0, seg, second_pass, 0)


def _slabs_to_cols(ref, k0, n):
    return jnp.concatenate([ref[k0 + q] for q in range(n)], axis=1)


def _window_sum(ext, doublings, forward):
    rows = ext.shape[0]
    s = ext
    for k in range(doublings):
        s = s + pltpu.roll(s, (1 << k) if forward else rows - (1 << k), 0)
    return s


def _mix_fwd(h1, g_pre, g_so, g_po, g_post, dskip, pscale, win, wout, bbre, bbim, ccre, ccim, wgv, wgg, pw, tabs, pows, name):
    T, D = h1.shape
    W = D // 2
    tm = MIX_SUBTILES * MIX_TILE
    seg = MIX_TILE // SUBLANES
    n_slabs = tabs.shape[1]
    nch, cch, sch = bbre.shape
    spc = sch // LANES
    pg = W // len(POOL_WINDOWS)

    def body(h_ref, gpre_ref, gso_ref, gpo_ref, gpost_ref, dskip_ref, pscale_ref, win_ref, wout_ref, bbre_ref, bbim_ref, ccre_ref, ccim_ref, wgv_ref, wgg_ref, pw_ref, tabs_ref, pows_ref,
             proj_ref, xr_ref, xi_ref, y_ref, pooled_ref, mixed_ref, h2_ref, n2_ref, cat_ref, car_r, car_i, halo):
        i = pl.program_id(0)

        @pl.when(i == 0)
        def _():
            car_r[...] = jnp.zeros_like(car_r)
            car_i[...] = jnp.zeros_like(car_i)
            halo[...] = jnp.zeros_like(halo)

        hh = h_ref[...]
        n2 = (hh * _rms_stat(hh) * gpre_ref[...]).astype(MXU_DTYPE)
        n2_ref[...] = n2.astype(n2_ref.dtype)
        proj = _mm(n2, win_ref[...])
        proj_ref[...] = proj
        us, up = proj[:, :W], proj[:, W:]

        for c in range(nch):
            uc = us[:, c * cch : (c + 1) * cch].astype(MXU_DTYPE)
            bur, bui = _mm(uc, bbre_ref[c]), _mm(uc, bbim_ref[c])
            for q in range(spc):
                xr_ref[c * spc + q] = bur[:, q * LANES : (q + 1) * LANES]
                xi_ref[c * spc + q] = bui[:, q * LANES : (q + 1) * LANES]
        for sub in range(MIX_SUBTILES):
            _scan_inplace(xr_ref, xi_ref, tabs_ref, pows_ref, car_r, car_i, seg, reverse=False, base=sub * MIX_TILE)
        ys = []
        for c in range(nch):
            ys.append(_mm(_slabs_to_cols(xr_ref, c * spc, spc), ccre_ref[c]) - _mm(_slabs_to_cols(xi_ref, c * spc, spc), ccim_ref[c]))
        y = jnp.concatenate(ys, axis=1) + dskip_ref[...] * us
        y_ref[...] = y
        ge = _gelu(y).astype(MXU_DTYPE)
        zv = jnp.concatenate([_mm(ge[:, c * cch : (c + 1) * cch], wgv_ref[c]) for c in range(nch)], axis=1)
        zg = jnp.concatenate([_mm(ge[:, c * cch : (c + 1) * cch], wgg_ref[c]) for c in range(nch)], axis=1)
        out = zv * _sigmoid(zg)
        cat_s = out * _rms_stat(out) * gso_ref[...]

        ext = jnp.concatenate([halo[...], up], axis=0)
        halo[...] = up[tm - POOL_HALO :, :]
        t1 = (i * tm + 1 + lax.broadcasted_iota(jnp.int32, (tm, pg), 0)).astype(F32)
        pooled, pms = [], []
        for g, w in enumerate(POOL_WINDOWS):
            col = ext[:, g * pg : (g + 1) * pg]
            win_sum = _window_sum(col, g + 1, True)[POOL_HALO:, :]
            pooled_g = win_sum / jnp.minimum(t1, float(w)) - up[:, g * pg : (g + 1) * pg]
            pooled.append(pooled_g)
            pms.append(_mm(pooled_g, pw_ref[g]))
        pooled_ref[...] = jnp.concatenate(pooled, axis=1)
        yp = jnp.concatenate(pms, axis=1) * pscale_ref[...]
        cat_p = yp * _rms_stat(yp) * gpo_ref[...]

        cat = jnp.concatenate([cat_s, cat_p], axis=1).astype(MXU_DTYPE)
        cat_ref[...] = cat.astype(cat_ref.dtype)
        mixed = _mm(cat, wout_ref[...])
        mixed_ref[...] = mixed
        h2_ref[...] = hh + mixed * _rms_stat(mixed) * gpost_ref[...]

    tok = lambda cols, dt=F32: jax.ShapeDtypeStruct((T, cols), dt)
    slab_spec = pl.BlockSpec((n_slabs, tm, LANES), lambda i: (0, i, 0))
    operands = (h1, g_pre, g_so, g_po, g_post, dskip, pscale, win, wout, bbre, bbim, ccre, ccim, wgv, wgg, pw, tabs, pows)
    outs, _ = _pallas(
        body,
        name=name,
        grid=(T // tm,),
        out_shape=[tok(D), jax.ShapeDtypeStruct((n_slabs, T, LANES), F32), jax.ShapeDtypeStruct((n_slabs, T, LANES), F32), tok(W), tok(W), tok(D), tok(D), tok(D, MXU_DTYPE), tok(D, MXU_DTYPE)],
        in_specs=[_row_spec(tm, D)] + [_full_spec(o.shape, single=True) for o in operands[1:]],
        out_specs=[_row_spec(tm, D), slab_spec, slab_spec, _row_spec(tm, W), _row_spec(tm, W), _row_spec(tm, D), _row_spec(tm, D), _row_spec(tm, D), _row_spec(tm, D)],
        scratch_shapes=[pltpu.VMEM((n_slabs, SUBLANES, LANES), F32), pltpu.VMEM((n_slabs, SUBLANES, LANES), F32), pltpu.VMEM((POOL_HALO, W), F32)],
        operands=operands,
    )
    return outs


def _mix_bwd_heads(dh2, mixed, y, pooled, proj, g_so, g_po, g_post, pscale, wout, wgv, wgg, pw, name, exchange=None):
    T, D = dh2.shape
    W = D // 2
    tm = FFN_TILE
    nch, cch, _ = wgv.shape
    ng, pg, _ = pw.shape

    def body(dh2_ref, mixed_ref, y_ref, pooled_ref, us_ref, gso_ref, gpo_ref, gpost_ref, pscale_ref, wout_ref, wgv_ref, wgg_ref, pw_ref,
             dy_ref, dpooled_ref, dmixed_ref, dgpost_ref, dgso_ref, dgpo_ref, dd_ref, dscale_ref, dwgv_ref, dwgg_ref, dpw_ref):
        first = pl.program_id(0) == 0
        dmixed, dgpost = _rms_bwd(mixed_ref[...], gpost_ref[...], dh2_ref[...])
        _acc(dgpost_ref, dgpost, first)
        dmb = dmixed.astype(MXU_DTYPE)
        dmixed_ref[...] = dmb.astype(dmixed_ref.dtype)
        dcat = _mm_nt(dmb, wout_ref[...])
        dcs, dcp = dcat[:, :W], dcat[:, W:]

        y = y_ref[...]
        ge = _gelu(y).astype(MXU_DTYPE)
        zv = jnp.concatenate([_mm(ge[:, c * cch : (c + 1) * cch], wgv_ref[c]) for c in range(nch)], axis=1)
        zg = jnp.concatenate([_mm(ge[:, c * cch : (c + 1) * cch], wgg_ref[c]) for c in range(nch)], axis=1)
        sg = _sigmoid(zg)
        dout, dgso = _rms_bwd(zv * sg, gso_ref[...], dcs)
        _acc(dgso_ref, dgso, first)
        dzv = (dout * sg).astype(MXU_DTYPE)
        dzg = (dout * zv * sg * (1.0 - sg)).astype(MXU_DTYPE)
        dges = []
        for c in range(nch):
            cs = slice(c * cch, (c + 1) * cch)
            dges.append(_mm_nt(dzv[:, cs], wgv_ref[c]) + _mm_nt(dzg[:, cs], wgg_ref[c]))
            _acc(dwgv_ref.at[c], _mm_tn(ge[:, cs], dzv[:, cs]), first)
            _acc(dwgg_ref.at[c], _mm_tn(ge[:, cs], dzg[:, cs]), first)
        dy = jnp.concatenate(dges, axis=1) * _gelu_grad(y)
        dy_ref[...] = dy
        _acc(dd_ref, jnp.sum(dy * us_ref[...], axis=0, keepdims=True), first)

        pooled_b = pooled_ref[...].astype(MXU_DTYPE)
        pm = jnp.concatenate([_mm(pooled_b[:, g * pg : (g + 1) * pg], pw_ref[g]) for g in range(ng)], axis=1)
        dyp, dgpo = _rms_bwd(pm * pscale_ref[...], gpo_ref[...], dcp)
        _acc(dgpo_ref, dgpo, first)
        _acc(dscale_ref, jnp.sum(dyp * pm, axis=0, keepdims=True), first)
        dpm = (dyp * pscale_ref[...]).astype(MXU_DTYPE)
        dps = []
        for g in range(ng):
            gs = slice(g * pg, (g + 1) * pg)
            dps.append(_mm_nt(dpm[:, gs], pw_ref[g]))
            _acc(dpw_ref.at[g], _mm_tn(pooled_b[:, gs], dpm[:, gs]), first)
        dpooled_ref[...] = jnp.concatenate(dps, axis=1)

    vec = lambda n: jax.ShapeDtypeStruct((1, n), F32)
    operands = (dh2, mixed, y, pooled, proj, g_so, g_po, g_post, pscale, wout, wgv, wgg, pw)
    return _pallas(
        body,
        name=name,
        grid=(T // tm,),
        out_shape=[jax.ShapeDtypeStruct((T, W), F32), jax.ShapeDtypeStruct((T, W), F32), jax.ShapeDtypeStruct((T, D), MXU_DTYPE), vec(D), vec(W), vec(W), vec(W), vec(W),
                   jax.ShapeDtypeStruct(wgv.shape, F32), jax.ShapeDtypeStruct(wgg.shape, F32), jax.ShapeDtypeStruct(pw.shape, F32)],
        in_specs=[_row_spec(tm, D), _row_spec(tm, D), _row_spec(tm, W), _row_spec(tm, W), _row_spec(tm, W)] + [_full_spec(o.shape) for o in operands[5:]],
        out_specs=[_row_spec(tm, W), _row_spec(tm, W), _row_spec(tm, D), _full_spec((1, D)), _full_spec((1, W)), _full_spec((1, W)), _full_spec((1, W)), _full_spec((1, W)),
                   _full_spec(wgv.shape), _full_spec(wgg.shape), _full_spec(pw.shape)],
        operands=operands,
        exchange=exchange,
    )


def _mix_bwd_scan(dy, dpooled, xr, xi, proj, dskip, bbre, bbim, ccre, ccim, tabs, pows, name, exchange=None):
    T, W = dy.shape
    D = 2 * W
    tm = MIX_SUBTILES * MIX_TILE
    seg = MIX_TILE // SUBLANES
    nt = T // tm
    n_slabs = tabs.shape[1]
    nch, cch, sch = bbre.shape
    spc = sch // LANES
    pg = W // len(POOL_WINDOWS)
    blocks_per_tile = tm // SUBLANES

    def body(dy_ref, dp_ref, xr_ref, xi_ref, xpr_ref, xpi_ref, us_ref, dskip_ref, bbre_ref, bbim_ref, ccre_ref, ccim_ref, tabs_ref, pows_ref,
             dproj_ref, dccre_ref, dccim_ref, dbbre_ref, dbbim_ref, dar_ref, dai_ref, lr, li, car_r, car_i, halo):
        i = pl.program_id(0)
        first = i == 0
        tile = nt - 1 - i
        row = lax.broadcasted_iota(jnp.int32, (SUBLANES, LANES), 0)

        @pl.when(first)
        def _():
            car_r[...] = jnp.zeros_like(car_r)
            car_i[...] = jnp.zeros_like(car_i)
            halo[...] = jnp.zeros_like(halo)
            dar_ref[...] = jnp.zeros_like(dar_ref)
            dai_ref[...] = jnp.zeros_like(dai_ref)

        dy = dy_ref[...]
        for c in range(nch):
            dyc = dy[:, c * cch : (c + 1) * cch]
            gr, gi = _mm_nt(dyc, ccre_ref[c]), _mm_nt(dyc, ccim_ref[c])
            for q in range(spc):
                lr[c * spc + q] = gr[:, q * LANES : (q + 1) * LANES]
                li[c * spc + q] = -gi[:, q * LANES : (q + 1) * LANES]
            _acc(dccre_ref.at[c], _mm_tn(_slabs_to_cols(xr_ref, c * spc, spc), dyc), first)
            _acc(dccim_ref.at[c], -_mm_tn(_slabs_to_cols(xi_ref, c * spc, spc), dyc), first)
        for sub in reversed(range(MIX_SUBTILES)):
            _scan_inplace(lr, li, tabs_ref, pows_ref, car_r, car_i, seg, reverse=True, base=sub * MIX_TILE)

        for sub in range(MIX_SUBTILES):
            base = sub * MIX_TILE
            for k0 in range(0, n_slabs, SLAB_GROUP):
                slabs = range(k0, min(k0 + SLAB_GROUP, n_slabs))
                init = []
                for k in slabs:
                    if sub == 0:
                        prev_r = jnp.where(tile > 0, jnp.broadcast_to(xpr_ref[k, SUBLANES - 1 : SUBLANES, :], (SUBLANES, LANES)), 0.0)
                        prev_i = jnp.where(tile > 0, jnp.broadcast_to(xpi_ref[k, SUBLANES - 1 : SUBLANES, :], (SUBLANES, LANES)), 0.0)
                    else:
                        prev_r = jnp.broadcast_to(xr_ref[k, base - 1 : base, :], (SUBLANES, LANES))
                        prev_i = jnp.broadcast_to(xi_ref[k, base - 1 : base, :], (SUBLANES, LANES))
                    x0r = jnp.where(row >= 1, pltpu.roll(xr_ref[k, pl.ds(base + seg - 1, SUBLANES, stride=seg), :], 1, 0), prev_r)
                    x0i = jnp.where(row >= 1, pltpu.roll(xi_ref[k, pl.ds(base + seg - 1, SUBLANES, stride=seg), :], 1, 0), prev_i)
                    l0r, l0i = lr[k, pl.ds(base, SUBLANES, stride=seg), :], li[k, pl.ds(base, SUBLANES, stride=seg), :]
                    init += [l0r * x0r + l0i * x0i, l0i * x0r - l0r * x0i]

                def step(r, acc, slabs=slabs, base=base):
                    out = []
                    for q, k in enumerate(slabs):
                        pr_, pi_ = xr_ref[k, pl.ds(base + r - 1, SUBLANES, stride=seg), :], xi_ref[k, pl.ds(base + r - 1, SUBLANES, stride=seg), :]
                        lr_, li_ = lr[k, pl.ds(base + r, SUBLANES, stride=seg), :], li[k, pl.ds(base + r, SUBLANES, stride=seg), :]
                        out += [acc[2 * q] + lr_ * pr_ + li_ * pi_, acc[2 * q + 1] + li_ * pr_ - lr_ * pi_]
                    return tuple(out)

                sums = lax.fori_loop(1, seg, step, tuple(init))
                for q, k in enumerate(slabs):
                    dar_ref[k] += sums[2 * q]
                    dai_ref[k] += sums[2 * q + 1]

        us = us_ref[...]
        dus = []
        for c in range(nch):
            lrc, lic = _slabs_to_cols(lr, c * spc, spc).astype(MXU_DTYPE), _slabs_to_cols(li, c * spc, spc).astype(MXU_DTYPE)
            uc = us[:, c * cch : (c + 1) * cch]
            _acc(dbbre_ref.at[c], _mm_tn(uc, lrc), first)
            _acc(dbbim_ref.at[c], _mm_tn(uc, lic), first)
            dus.append(_mm_nt(lrc, bbre_ref[c]) + _mm_nt(lic, bbim_ref[c]))
        du_s = jnp.concatenate(dus, axis=1) + dskip_ref[...] * dy

        dp = dp_ref[...]
        t1 = (tile * tm + 1 + lax.broadcasted_iota(jnp.int32, (tm, pg), 0)).astype(F32)
        dups, heads = [], []
        for g, w in enumerate(POOL_WINDOWS):
            dpg = dp[:, g * pg : (g + 1) * pg]
            qg = dpg / jnp.minimum(t1, float(w))
            ext = jnp.concatenate([qg, halo[:, g * pg : (g + 1) * pg]], axis=0)
            dups.append(_window_sum(ext, g + 1, False)[:tm, :] - dpg)
            heads.append(qg[:POOL_HALO, :])
        halo[...] = jnp.concatenate(heads, axis=1)
        dproj_ref[...] = jnp.concatenate([du_s] + dups, axis=1).astype(dproj_ref.dtype)

    rev = lambda cols: _row_spec(tm, cols, rev_n=nt)
    slab_spec = pl.BlockSpec((n_slabs, tm, LANES), lambda i: (0, nt - 1 - i, 0))
    prev_spec = pl.BlockSpec((n_slabs, SUBLANES, LANES), lambda i: (0, jnp.maximum((nt - 1 - i) * blocks_per_tile - 1, 0), 0))
    consts = (dskip, bbre, bbim, ccre, ccim, tabs, pows)
    return _pallas(
        body,
        name=name,
        grid=(nt,),
        out_shape=[jax.ShapeDtypeStruct((T, D), MXU_DTYPE), jax.ShapeDtypeStruct(ccre.shape, F32), jax.ShapeDtypeStruct(ccim.shape, F32), jax.ShapeDtypeStruct(bbre.shape, F32),
                   jax.ShapeDtypeStruct(bbim.shape, F32), jax.ShapeDtypeStruct((n_slabs, SUBLANES, LANES), F32), jax.ShapeDtypeStruct((n_slabs, SUBLANES, LANES), F32)],
        in_specs=[rev(W), rev(W), slab_spec, slab_spec, prev_spec, prev_spec, rev(W)] + [_full_spec(o.shape, single=True) for o in consts],
        out_specs=[rev(D), _full_spec(ccre.shape), _full_spec(ccim.shape), _full_spec(bbre.shape), _full_spec(bbim.shape),
                   _full_spec((n_slabs, SUBLANES, LANES)), _full_spec((n_slabs, SUBLANES, LANES))],
        scratch_shapes=[pltpu.VMEM((n_slabs, tm, LANES), F32), pltpu.VMEM((n_slabs, tm, LANES), F32), pltpu.VMEM((n_slabs, SUBLANES, LANES), F32), pltpu.VMEM((n_slabs, SUBLANES, LANES), F32),
                        pltpu.VMEM((POOL_HALO, W), F32)],
        operands=(dy, dpooled, xr, xi, xr, xi, proj, *consts),
        exchange=exchange,
    )


def _mix_bwd_in(dproj, h1, dh2, g_pre, win, name):
    T, D = h1.shape
    tm = _token_tile(T)

    def body(dproj_ref, h_ref, dh2_ref, gpre_ref, win_ref, dh1_ref, dg_ref):
        dx, dg = _rms_bwd(h_ref[...], gpre_ref[...], _mm_nt(dproj_ref[...], win_ref[...]))
        _acc(dg_ref, dg, pl.program_id(0) == 0)
        dh1_ref[...] = dh2_ref[...] + dx

    outs, _ = _pallas(
        body,
        name=name,
        grid=(T // tm,),
        out_shape=[jax.ShapeDtypeStruct((T, D), F32), jax.ShapeDtypeStruct((1, D), F32)],
        in_specs=[_row_spec(tm, D), _row_spec(tm, D), _row_spec(tm, D), _full_spec((1, D)), _full_spec(win.shape)],
        out_specs=[_row_spec(tm, D), _full_spec((1, D))],
        operands=(dproj, h1, dh2, g_pre, win),
    )
    return outs


def _discretize(lam_re, lam_im, log_dt, b_re, b_im):
    dt = jnp.exp(log_dt)[:, None]
    decay = jnp.exp(lam_re * dt)
    ang = lam_im * dt
    a_re, a_im = decay * jnp.cos(ang), decay * jnp.sin(ang)
    nr = a_re - 1.0
    den = lam_re * lam_re + lam_im * lam_im
    q_re = (nr * lam_re + a_im * lam_im) / den
    q_im = (a_im * lam_re - nr * lam_im) / den
    bb_re = q_re[..., None] * b_re - q_im[..., None] * b_im
    bb_im = q_re[..., None] * b_im + q_im[..., None] * b_re
    return a_re, a_im, bb_re, bb_im


GROUPS_PER_CHUNK = 16


def _block_diag(w, rows_first):
    G = w.shape[0]
    nch = G // GROUPS_PER_CHUNK
    if not rows_first:
        w = jnp.swapaxes(w, 1, 2)
    p, q = w.shape[1], w.shape[2]
    eye = jnp.eye(GROUPS_PER_CHUNK, dtype=w.dtype)
    out = jnp.einsum("cgpq,gk->cgpkq", w.reshape(nch, GROUPS_PER_CHUNK, p, q), eye)
    return out.reshape(nch, GROUPS_PER_CHUNK * p, GROUPS_PER_CHUNK * q)


def _block_diag_extract(m, p, q, rows_first):
    nch = m.shape[0]
    eye = jnp.eye(GROUPS_PER_CHUNK, dtype=m.dtype)
    out = jnp.einsum("cgpkq,gk->cgpq", m.reshape(nch, GROUPS_PER_CHUNK, p, GROUPS_PER_CHUNK, q), eye).reshape(nch * GROUPS_PER_CHUNK, p, q)
    return out if rows_first else jnp.swapaxes(out, 1, 2)


def _cmul(ar, ai, br, bi):
    return ar * br - ai * bi, ar * bi + ai * br


def _powers(ar, ai, count):
    pr, pi = ar[None], ai[None]
    while pr.shape[0] < count:
        nr, ni = _cmul(pr, pi, pr[-1][None], pi[-1][None])
        pr, pi = jnp.concatenate([pr, nr]), jnp.concatenate([pi, ni])
    return pr[:count], pi[:count]


def _scan_tables(a_re, a_im, seg):
    n = a_re.size
    ns = n // LANES
    ar, ai = a_re.reshape(n), a_im.reshape(n)
    pr, pi = _powers(ar, ai, seg)
    jr, ji = _powers(pr[-1], pi[-1], SUBLANES)

    def bcast(v):
        return jnp.broadcast_to(v.reshape(ns, 1, LANES), (ns, SUBLANES, LANES))

    def per_sublane(vs):
        return jnp.transpose(vs.reshape(SUBLANES, ns, LANES), (1, 0, 2))

    tabs = jnp.stack([bcast(ar), bcast(ai), bcast(jr[0]), bcast(ji[0]), bcast(jr[1]), bcast(ji[1]), bcast(jr[3]), bcast(ji[3]),
                      per_sublane(jr), per_sublane(ji), per_sublane(jr[::-1]), per_sublane(ji[::-1])])

    def rows(vs):
        return jnp.broadcast_to(jnp.transpose(vs.reshape(seg, ns, 1, LANES), (1, 0, 2, 3)), (ns, seg, SUBLANES, LANES))

    return tabs, jnp.stack([rows(pr), rows(pi)])


SMALL = ("ffn1_pre_norm", "ffn1_post_norm", "mix_pre_norm", "mix_post_norm", "ssm_lambda_re", "ssm_lambda_im", "ssm_log_dt", "ssm_b_re", "ssm_b_im", "ssm_c_re", "ssm_c_im",
         "ssm_d", "ssm_w_glu", "pool_w", "pool_scale", "ssm_out_norm", "pool_out_norm", "ffn2_pre_norm", "ffn2_post_norm")
BIG = ("ffn1_w_gate", "ffn1_w_up", "ffn1_w_down", "w_in", "w_out", "ffn2_w_gate", "ffn2_w_up", "ffn2_w_down")
ORDER = ("meta_tokens", "ffn1_pre_norm", "ffn1_post_norm", "ffn1_w_gate", "ffn1_w_up", "ffn1_w_down", "mix_pre_norm", "mix_post_norm", "w_in", "ssm_lambda_re", "ssm_lambda_im",
         "ssm_log_dt", "ssm_b_re", "ssm_b_im", "ssm_c_re", "ssm_c_im", "ssm_d", "ssm_w_glu", "pool_w", "pool_scale", "ssm_out_norm", "pool_out_norm", "w_out", "ffn2_pre_norm",
         "ffn2_post_norm", "ffn2_w_gate", "ffn2_w_up", "ffn2_w_down")
PACK_ROWS = SUBLANES * 8
def _pack(arrays, rows):
    flat = jnp.concatenate([a.reshape(-1) for a in arrays])
    return jnp.pad(flat, (0, rows * LANES - flat.size)).reshape(rows, LANES)


def _unpack(packed, shapes):
    flat = packed.reshape(-1)
    out, off = [], 0
    for s in shapes:
        n = math.prod(s)
        out.append(flat[off : off + n].reshape(s))
        off += n
    return out


def _step(p, x, loss_target, m, v):
    D = x.shape[-1]
    chip = (2 * lax.axis_index("x") + lax.axis_index("y")).astype(jnp.int32)
    place = jnp.stack([chip, lax.axis_index("c").astype(jnp.int32)])

    def gather_buffer(w):
        own = w.reshape(1, 2, w.shape[0] // 2, w.shape[1])
        return lax.dynamic_update_slice(lax.empty((4,) + own.shape[1:], own.dtype), own, (chip, 0, 0, 0))

    def rows_of(n, a):
        return jnp.swapaxes(a[0], 0, 1) if n.endswith(("w_gate", "w_up")) else a[0]

    def rows_back(n, a):
        return (jnp.swapaxes(a, 0, 1) if n.endswith(("w_gate", "w_up")) else a)[None]

    def grad_view(g):
        return g.reshape(4, 2, g.shape[0] // 8, g.shape[1])

    def reduce_sum(got_sibling, views, tag):
        sums = [_add_own_half(v_, g_, place, f"{tag}_add_sibling_{k}") for k, (v_, g_) in enumerate(zip(views, got_sibling))]
        return [s[0] for s in sums], [s[1] for s in sums]

    def reduce_halves(parts, got_chips, tag):
        return [_add_chips(p_, g_, place, f"{tag}_add_chips_{k}") for k, (p_, g_) in enumerate(zip(parts, got_chips))]

    first_names = ("ffn1_w_gate", "ffn1_w_up", "ffn1_w_down")
    later_names = ("w_in", "w_out", "ffn2_w_gate", "ffn2_w_up", "ffn2_w_down")
    bufs = {n: gather_buffer(rows_of(n, p[n]).astype(MXU_DTYPE)) for n in BIG}
    gathered_weight = lambda g_: g_.reshape(-1, g_.shape[-1])
    got = _exchange_call(_Gather([bufs[n] for n in first_names] + [gather_buffer(p["meta_tokens"])]), "gather_first")
    full = {n: gathered_weight(g_) for n, g_ in zip(first_names, got)}
    meta = jnp.transpose(got[-1].reshape(4, N_META, -1), (1, 0, 2)).reshape(N_META, D)

    vec = lambda n: p[n].reshape(1, -1)
    G, N, H = p["ssm_b_re"].shape[1:]
    a_re, a_im, bb_re, bb_im = _discretize(p["ssm_lambda_re"][0], p["ssm_lambda_im"][0], p["ssm_log_dt"][0], p["ssm_b_re"][0], p["ssm_b_im"][0])
    tabs, pows = _scan_tables(a_re, a_im, MIX_TILE // SUBLANES)
    bf = lambda a: a.astype(MXU_DTYPE)
    bbre, bbim = bf(_block_diag(bb_re, False)), bf(_block_diag(bb_im, False))
    ccre, ccim = bf(_block_diag(p["ssm_c_re"][0], False)), bf(_block_diag(p["ssm_c_im"][0], False))
    wgv, wgg = bf(_block_diag(p["ssm_w_glu"][0][:, :, :H], True)), bf(_block_diag(p["ssm_w_glu"][0][:, :, H:], True))
    pw = bf(p["pool_w"][0])

    n_ffn_steps = (x.shape[1] + N_META) // FFN_TILE
    (h1, f1, ga1, si1, s1, n1), got = _ffn_fwd(
        x[0], vec("ffn1_pre_norm"), vec("ffn1_post_norm"), full["ffn1_w_gate"], full["ffn1_w_up"], full["ffn1_w_down"], "ffn1_fwd",
        exchange=_Gather([bufs[n] for n in later_names], mid_step=(3 * n_ffn_steps) // 4), meta=meta,
    )
    full.update({n: gathered_weight(g_) for n, g_ in zip(later_names, got)})
    proj, xr, xim, y, pooled, mixed, h2, n2, cat = _mix_fwd(
        h1, vec("mix_pre_norm"), vec("ssm_out_norm"), vec("pool_out_norm"), vec("mix_post_norm"), vec("ssm_d"), vec("pool_scale"), full["w_in"], full["w_out"],
        bbre, bbim, ccre, ccim, wgv, wgg, pw, tabs, pows, "mix_fwd",
    )
    (dh3, sq, f2, ga2, si2, s2, n3), _ = _ffn_fwd(
        h2, vec("ffn2_pre_norm"), vec("ffn2_post_norm"), full["ffn2_w_gate"], full["ffn2_w_up"], full["ffn2_w_down"], "ffn2_fwd", target=loss_target[0]
    )
    loss = lax.psum(0.5 * jnp.sum(sq) / D, ("x", "y", "c"))

    g, shared = {}, {}
    ffn_names = lambda tag: (tag + "_w_gate", tag + "_w_up", tag + "_w_down")

    da, db, df, dh2, g["ffn2_post_norm"], g["ffn2_pre_norm"] = _ffn_bwd(
        dh3, f2, ga2, si2, h2, vec("ffn2_post_norm"), vec("ffn2_pre_norm"), full["ffn2_w_gate"], full["ffn2_w_up"], full["ffn2_w_down"], "ffn2_bwd"
    )
    views2 = [
        grad_view(_tn_matmul(da, n3, "ffn2_dw_gate")[0]),
        grad_view(_tn_matmul(db, n3, "ffn2_dw_up")[0]),
        grad_view(_tn_matmul(s2, df, "ffn2_dw_down")[0]),
    ]
    (dy, dpooled, dmixed, g["mix_post_norm"], g["ssm_out_norm"], g["pool_out_norm"], g["ssm_d"], g["pool_scale"], dwgv, dwgg, g["pool_w"]), got = _mix_bwd_heads(
        dh2, mixed, y, pooled, proj, vec("ssm_out_norm"), vec("pool_out_norm"), vec("mix_post_norm"), vec("pool_scale"), full["w_out"], wgv, wgg, pw, "mix_bwd_heads",
        exchange=_SiblingScatter(views2),
    )
    parts2, wire2 = reduce_sum(got, views2, "ffn2")
    (dproj, dccre, dccim, dbbre, dbbim, dar, dai), got = _mix_bwd_scan(
        dy, dpooled, xr, xim, proj, vec("ssm_d"), bbre, bbim, ccre, ccim, tabs, pows, "mix_bwd_scan", exchange=_ChipScatter(wire2)
    )
    dh1, g["mix_pre_norm"] = _mix_bwd_in(dproj, h1, dh2, vec("mix_pre_norm"), full["w_in"], "mix_bwd_in")
    halves2 = reduce_halves(parts2, got, "ffn2")
    dw_in, got = _tn_matmul(n2, dproj, "dw_in", exchange=_SiblingShare(halves2))
    shared.update(zip(ffn_names("ffn2"), got))
    dw_out, _ = _tn_matmul(cat, dmixed, "dw_out")
    views_m = [grad_view(dw_in), grad_view(dw_out)]

    (da, db, df, g["ffn1_post_norm"]), got = _ffn_bwd_down(dh1, f1, ga1, si1, vec("ffn1_post_norm"), full["ffn1_w_down"], "ffn1_bwd_down", exchange=_SiblingScatter(views_m))
    parts_m, wire_m = reduce_sum(got, views_m, "mix")
    dw_down, got = _tn_matmul(s1, df, "ffn1_dw_down", exchange=_ChipScatter(wire_m))
    halves_m = reduce_halves(parts_m, got, "mix")
    views_d = [grad_view(dw_down)]
    ex = _Group([_SiblingShare(halves_m), _SiblingScatter(views_d)])
    dw_gate, got = _tn_matmul(da, n1, "ffn1_dw_gate", exchange=ex)
    got_m, got_d = ex.split(got)
    shared.update(zip(("w_in", "w_out"), got_m))
    parts_d, wire_d = reduce_sum(got_d, views_d, "ffn1_down")
    views_g = [grad_view(dw_gate)]
    ex = _Group([_ChipScatter(wire_d), _SiblingScatter(views_g)])
    dw_up, got = _tn_matmul(db, n1, "ffn1_dw_up", exchange=ex)
    got_d, got_g = ex.split(got)
    halves_d = reduce_halves(parts_d, got_d, "ffn1_down")
    parts_g, wire_g = reduce_sum(got_g, views_g, "ffn1_gate")
    views_u = [grad_view(dw_up)]
    ex = _Group([_SiblingShare(halves_d), _ChipScatter(wire_g), _SiblingScatter(views_u)])
    (grad_x, d_meta, g["ffn1_pre_norm"]), got = _ffn_bwd_up(da, db, x[0], meta, dh1, vec("ffn1_pre_norm"), full["ffn1_w_gate"], full["ffn1_w_up"], "ffn1_bwd_up", exchange=ex)
    got_d, got_g, got_u = ex.split(got)
    shared["ffn1_w_down"] = got_d[0]
    halves_g = reduce_halves(parts_g, got_g, "ffn1_gate")
    parts_u, wire_u = reduce_sum(got_u, views_u, "ffn1_up")
    grad_x = grad_x[None]

    g["ssm_c_re"] = _block_diag_extract(dccre, N, H, False)
    g["ssm_c_im"] = _block_diag_extract(dccim, N, H, False)
    g["ssm_w_glu"] = jnp.concatenate([_block_diag_extract(dwgv, H, H, True), _block_diag_extract(dwgg, H, H, True)], axis=-1)
    d_a_re, d_a_im = jnp.sum(dar, axis=1).reshape(G, N), jnp.sum(dai, axis=1).reshape(G, N)
    _, pull = jax.vjp(_discretize, p["ssm_lambda_re"][0], p["ssm_lambda_im"][0], p["ssm_log_dt"][0], p["ssm_b_re"][0], p["ssm_b_im"][0])
    g["ssm_lambda_re"], g["ssm_lambda_im"], g["ssm_log_dt"], g["ssm_b_re"], g["ssm_b_im"] = pull(
        (d_a_re, d_a_im, _block_diag_extract(dbbre, H, N, False), _block_diag_extract(dbbim, H, N, False))
    )

    small_shapes = [p[n].shape for n in SMALL] + [(N_META, D)]
    small_size = sum(math.prod(s) for s in small_shapes)
    rows = -(-small_size // (LANES * PACK_ROWS)) * PACK_ROWS
    views_s = [_pack([g[n] for n in SMALL] + [d_meta], rows).reshape(4, 2, rows // 8, LANES)]
    parts_s, wire_s = reduce_sum(_exchange_call(_SiblingScatter(views_s), "small_reduce_sibling"), views_s, "small")
    ex = _Group([_ChipScatter(wire_u + wire_s), _SiblingShare(halves_g)])
    got_c, got_g = ex.split(_exchange_call(ex, "tail_reduce_chips"))
    shared["ffn1_w_gate"] = got_g[0]
    got = _exchange_call(_SiblingShare(reduce_halves(parts_u + parts_s, got_c, "tail")), "tail_reduce_share")
    shared["ffn1_w_up"] = got[0]
    small_buf = lax.dynamic_update_slice(lax.empty((4,) + got[1].shape, F32), got[1][None], (chip, 0, 0, 0))
    small_all = _exchange_call(_Gather([small_buf]), "gather_small")[0].reshape(rows, LANES)
    grads = dict(zip(SMALL + ("meta_full",), _unpack(small_all, small_shapes)))
    grads["meta_tokens"] = lax.dynamic_slice_in_dim(grads.pop("meta_full"), chip * (D // 4), D // 4, axis=1)
    delta, new_m, new_v = {}, {}, {}
    for n in BIG:
        g_rows = shared[n].reshape(-1, shared[n].shape[-1])
        outs = _adamw(rows_of(n, p[n]), g_rows, rows_of(n, m[n]), rows_of(n, v[n]), "adamw_" + n)
        grads[n], delta[n], new_m[n], new_v[n] = (rows_back(n, a) for a in (g_rows, *outs))
    delta["meta_tokens"], new_m["meta_tokens"], new_v["meta_tokens"] = _adamw(p["meta_tokens"], grads["meta_tokens"], m["meta_tokens"], v["meta_tokens"], "adamw_meta_tokens")

    def as_2d(n, a):
        a = a.reshape(p[n].shape)[0]
        if n in ("ssm_b_re", "ssm_b_im"):
            a = jnp.swapaxes(a, 1, 2)
        return a.reshape(-1, a.shape[-1])

    def from_2d(n, a):
        if n in ("ssm_b_re", "ssm_b_im"):
            g_, n_, h_ = p[n].shape[1:]
            return jnp.swapaxes(a.reshape(g_, h_, n_), 1, 2)[None]
        return a.reshape(p[n].shape)

    outs = _adamw_many(*[[as_2d(n, t[n]) for n in SMALL] for t in (p, grads, m, v)], "adamw_small")
    for out, arrays in zip((delta, new_m, new_v), outs):
        out.update({n: from_2d(n, a) for n, a in zip(SMALL, arrays)})

    return (loss, grad_x, *[grads[n] for n in ORDER], *[delta[n] for n in ORDER], *[new_m[n] for n in ORDER], *[new_v[n] for n in ORDER])


def kernel(x, meta_tokens, ffn1_pre_norm, ffn1_post_norm, ffn1_w_gate, ffn1_w_up, ffn1_w_down, mix_pre_norm, mix_post_norm, w_in, ssm_lambda_re, ssm_lambda_im, ssm_log_dt, ssm_b_re, ssm_b_im, ssm_c_re, ssm_c_im, ssm_d, ssm_w_glu, pool_w, pool_scale, ssm_out_norm, pool_out_norm, w_out, ffn2_pre_norm, ffn2_post_norm, ffn2_w_gate, ffn2_w_up, ffn2_w_down, loss_target, m_meta_tokens, m_ffn1_pre_norm, m_ffn1_post_norm, m_ffn1_w_gate, m_ffn1_w_up, m_ffn1_w_down, m_mix_pre_norm, m_mix_post_norm, m_w_in, m_ssm_lambda_re, m_ssm_lambda_im, m_ssm_log_dt, m_ssm_b_re, m_ssm_b_im, m_ssm_c_re, m_ssm_c_im, m_ssm_d, m_ssm_w_glu, m_pool_w, m_pool_scale, m_ssm_out_norm, m_pool_out_norm, m_w_out, m_ffn2_pre_norm, m_ffn2_post_norm, m_ffn2_w_gate, m_ffn2_w_up, m_ffn2_w_down, v_meta_tokens, v_ffn1_pre_norm, v_ffn1_post_norm, v_ffn1_w_gate, v_ffn1_w_up, v_ffn1_w_down, v_mix_pre_norm, v_mix_post_norm, v_w_in, v_ssm_lambda_re, v_ssm_lambda_im, v_ssm_log_dt, v_ssm_b_re, v_ssm_b_im, v_ssm_c_re, v_ssm_c_im, v_ssm_d, v_ssm_w_glu, v_pool_w, v_pool_scale, v_ssm_out_norm, v_pool_out_norm, v_w_out, v_ffn2_pre_norm, v_ffn2_post_norm, v_ffn2_w_gate, v_ffn2_w_up, v_ffn2_w_down):
    args = locals()
    p = {n: args[n] for n in ORDER}
    m = {n: args["m_" + n] for n in ORDER}
    v = {n: args["v_" + n] for n in ORDER}
    return _step(p, x, loss_target, m, v)
```

```python
import math

import jax
import jax.numpy as jnp
from jax import lax
from jax.experimental import pallas as pl
from jax.experimental.pallas import tpu as pltpu

F32 = jnp.float32
MXU_DTYPE = jnp.bfloat16
WIRE_DTYPE = jnp.bfloat16

RMS_EPS = 1e-6
N_META = 16
POOL_WINDOWS = (2, 4, 8, 16)
POOL_HALO = 16
ADAM_LR, ADAM_B1, ADAM_B2, ADAM_EPS, ADAM_WD, ADAM_STEP = 0.001, 0.9, 0.999, 1e-08, 0.01, 10

LANES = 128
SUBLANES = 8
VMEM_LIMIT = 60 * 1024 * 1024
FFN_TILE = 432
FFN_CHUNK = 1024
TN_TILE = 1024
MIX_TILE = 216
MIX_SUBTILES = 2
SLAB_GROUP = 8
MESH = pl.DeviceIdType.MESH
ANY = pl.BlockSpec(memory_space=pl.ANY)


def _mm(a, b):
    return jnp.dot(a.astype(MXU_DTYPE), b.astype(MXU_DTYPE), preferred_element_type=F32)


def _mm_nt(a, b):
    return lax.dot_general(a.astype(MXU_DTYPE), b.astype(MXU_DTYPE), (((1,), (1,)), ((), ())), preferred_element_type=F32)


def _mm_tn(a, b):
    return lax.dot_general(a.astype(MXU_DTYPE), b.astype(MXU_DTYPE), (((0,), (0,)), ((), ())), preferred_element_type=F32)


def _rms_stat(x):
    return lax.rsqrt(jnp.mean(x * x, axis=-1, keepdims=True) + RMS_EPS)


def _rms_bwd(x, g, dy):
    r = _rms_stat(x)
    xh = x * r
    dg = jnp.sum(dy * xh, axis=0, keepdims=True)
    dxh = dy * g
    dx = r * (dxh - xh * jnp.mean(dxh * xh, axis=-1, keepdims=True))
    return dx, dg


def _sigmoid(x):
    return 1.0 / (1.0 + jnp.exp(-x))


GELU_C = math.sqrt(2.0 / math.pi)
GELU_K = 0.044715


def _gelu(y):
    return 0.5 * y * (1.0 + jnp.tanh(GELU_C * (y + GELU_K * y * y * y)))


def _gelu_grad(y):
    th = jnp.tanh(GELU_C * (y + GELU_K * y * y * y))
    return 0.5 * (1.0 + th) + 0.5 * y * (1.0 - th * th) * GELU_C * (1.0 + 3.0 * GELU_K * y * y)


def _row_spec(tile, cols, rev_n=None):
    if rev_n is None:
        return pl.BlockSpec((tile, cols), lambda i: (i, 0))
    return pl.BlockSpec((tile, cols), lambda i: (rev_n - 1 - i, 0))


def _full_spec(shape, single=False):
    zeros = (0,) * len(shape)
    if single:
        return pl.BlockSpec(shape, lambda *_: zeros, pipeline_mode=pl.Buffered(1))
    return pl.BlockSpec(shape, lambda *_: zeros)


def _acc(ref, val, first):
    @pl.when(first)
    def _():
        ref[...] = val

    @pl.when(jnp.logical_not(first))
    def _():
        ref[...] += val


def _place():
    x, y, c = lax.axis_index("x"), lax.axis_index("y"), lax.axis_index("c")
    others = [(1 - x, y), (x, 1 - y), (1 - x, 1 - y)]
    return x, y, c, others


class _Exchange:
    mid_step = None

    def __init__(self, ins, out_shapes, aliases, n_sems):
        self.ins, self.out_shapes, self.aliases, self.n_sems = list(ins), list(out_shapes), dict(aliases), n_sems

    def mid(self, ins, outs, send_sems, recv_sems):
        pass


class _SiblingScatter(_Exchange):
    def __init__(self, views):
        super().__init__(views, [jax.ShapeDtypeStruct((4,) + v.shape[2:], v.dtype) for v in views], {}, 4 * len(views))

    def _copies(self, ins, outs, send_sems, recv_sems):
        x, y, c, _ = _place()
        return [
            pltpu.make_async_remote_copy(src_ref=ins[a].at[k, 1 - c], dst_ref=outs[a].at[k], send_sem=send_sems.at[4 * a + k], recv_sem=recv_sems.at[4 * a + k], device_id=(x, y, 1 - c), device_id_type=MESH)
            for a in range(len(ins))
            for k in range(4)
        ]

    def start(self, *refs):
        for cp in self._copies(*refs):
            cp.start()

    def finish(self, *refs):
        cps = self._copies(*refs)
        for cp in cps:
            cp.wait_recv()
        for cp in cps:
            cp.wait_send()


class _ChipScatter(_Exchange):
    def __init__(self, parts):
        super().__init__(parts, [jax.ShapeDtypeStruct((3,) + p.shape[1:], p.dtype) for p in parts], {}, 3 * len(parts))

    def _copies(self, ins, outs, send_sems, recv_sems):
        x, y, c, others = _place()
        return [
            pltpu.make_async_remote_copy(src_ref=ins[a].at[2 * chip[0] + chip[1]], dst_ref=outs[a].at[j], send_sem=send_sems.at[3 * a + j], recv_sem=recv_sems.at[3 * a + j], device_id=(*chip, c), device_id_type=MESH)
            for a in range(len(ins))
            for j, chip in enumerate(others)
        ]

    start = _SiblingScatter.start
    finish = _SiblingScatter.finish


class _SiblingShare(_Exchange):
    def __init__(self, bufs):
        super().__init__(bufs, [jax.ShapeDtypeStruct(b.shape, b.dtype) for b in bufs], {a: a for a in range(len(bufs))}, len(bufs))

    def _copy(self, outs, send_sems, recv_sems, a, half):
        x, y, c, _ = _place()
        mine = outs[a].at[c if half == "mine" else 1 - c]
        return pltpu.make_async_remote_copy(src_ref=mine, dst_ref=mine, send_sem=send_sems.at[a], recv_sem=recv_sems.at[a], device_id=(x, y, 1 - c), device_id_type=MESH)

    def start(self, ins, outs, send_sems, recv_sems):
        for a in range(len(outs)):
            self._copy(outs, send_sems, recv_sems, a, "mine").start()

    def finish(self, ins, outs, send_sems, recv_sems):
        for a in range(len(outs)):
            self._copy(outs, send_sems, recv_sems, a, "theirs").wait_recv()
        for a in range(len(outs)):
            self._copy(outs, send_sems, recv_sems, a, "mine").wait_send()


class _Gather(_Exchange):
    def __init__(self, bufs, mid_step=None):
        super().__init__(bufs, [jax.ShapeDtypeStruct(b.shape, b.dtype) for b in bufs], {a: a for a in range(len(bufs))}, 6 * len(bufs))
        self.mid_step = mid_step

    def _copy(self, outs, send_sems, recv_sems, a, j, chip, half, to):
        blk = outs[a].at[2 * chip[0] + chip[1], half]
        return pltpu.make_async_remote_copy(src_ref=blk, dst_ref=blk, send_sem=send_sems.at[6 * a + j], recv_sem=recv_sems.at[6 * a + j], device_id=to, device_id_type=MESH)

    def start(self, ins, outs, send_sems, recv_sems):
        x, y, c, others = _place()
        for a in range(len(outs)):
            for j, chip in enumerate(others):
                self._copy(outs, send_sems, recv_sems, a, j, (x, y), c, (*chip, c)).start()

    def mid(self, ins, outs, send_sems, recv_sems):
        x, y, c, others = _place()
        for a in range(len(outs)):
            for j, chip in enumerate(others):
                self._copy(outs, send_sems, recv_sems, a, j, chip, c, (x, y, c)).wait_recv()
                self._copy(outs, send_sems, recv_sems, a, 3 + j, chip, c, (x, y, 1 - c)).start()

    def finish(self, ins, outs, send_sems, recv_sems):
        x, y, c, others = _place()
        for a in range(len(outs)):
            for j, chip in enumerate(others):
                self._copy(outs, send_sems, recv_sems, a, 3 + j, chip, 1 - c, (x, y, c)).wait_recv()
        for a in range(len(outs)):
            for j, chip in enumerate(others):
                self._copy(outs, send_sems, recv_sems, a, j, (x, y), c, (*chip, c)).wait_send()
                self._copy(outs, send_sems, recv_sems, a, 3 + j, chip, c, (x, y, 1 - c)).wait_send()


class _SemSlice:
    def __init__(self, sems, off):
        self.sems, self.off = sems, off

    @property
    def at(self):
        return self

    def __getitem__(self, i):
        return self.sems.at[self.off + i]


class _Group(_Exchange):
    def __init__(self, exchanges):
        ins, outs, aliases, n_sems, self.spans = [], [], {}, 0, []
        for ex in exchanges:
            self.spans.append((len(ins), len(outs), n_sems))
            aliases.update({len(ins) + i: len(outs) + o for i, o in ex.aliases.items()})
            ins, outs, n_sems = ins + ex.ins, outs + ex.out_shapes, n_sems + ex.n_sems
        super().__init__(ins, outs, aliases, n_sems)
        self.exchanges = exchanges
        mids = {ex.mid_step for ex in exchanges if ex.mid_step is not None}
        self.mid_step = mids.pop() if mids else None

    def _each(self, method, ins, outs, send_sems, recv_sems):
        for ex, (i0, o0, s0) in zip(self.exchanges, self.spans):
            getattr(ex, method)(ins[i0 : i0 + len(ex.ins)], outs[o0 : o0 + len(ex.out_shapes)], _SemSlice(send_sems, s0), _SemSlice(recv_sems, s0))

    def start(self, *refs):
        self._each("start", *refs)

    def mid(self, *refs):
        self._each("mid", *refs)

    def finish(self, *refs):
        self._each("finish", *refs)

    def split(self, outs):
        return [outs[o0 : o0 + len(ex.out_shapes)] for ex, (_, o0, _) in zip(self.exchanges, self.spans)]


def _exchange_call(ex, name):
    n, m = len(ex.ins), len(ex.out_shapes)

    def body(*refs):
        parts = (refs[:n], refs[n : n + m], refs[n + m], refs[n + m + 1])
        ex.start(*parts)
        ex.mid(*parts)
        ex.finish(*parts)

    return pl.pallas_call(
        body,
        name=name,
        out_shape=ex.out_shapes,
        in_specs=[ANY] * n,
        out_specs=[ANY] * m,
        scratch_shapes=[pltpu.SemaphoreType.DMA((ex.n_sems,)), pltpu.SemaphoreType.DMA((ex.n_sems,))],
        input_output_aliases=ex.aliases,
    )(*ex.ins)


def _pallas(body, *, name, grid, in_specs, out_specs, out_shape, operands, scratch_shapes=(), exchange=None):
    params = pltpu.CompilerParams(dimension_semantics=("arbitrary",) * len(grid), vmem_limit_bytes=VMEM_LIMIT)
    if exchange is None:
        outs = pl.pallas_call(body, name=name, grid=grid, in_specs=in_specs, out_specs=out_specs, out_shape=out_shape, scratch_shapes=list(scratch_shapes), compiler_params=params)(*operands)
        return outs, []
    ex = exchange
    n_in, n_out, n_scr = len(in_specs), len(out_specs), len(scratch_shapes)
    x_in, x_out = len(ex.ins), len(ex.out_shapes)

    def hosted(*refs):
        ins, x_ins = refs[:n_in], refs[n_in : n_in + x_in]
        outs, x_outs = refs[n_in + x_in : n_in + x_in + n_out], refs[n_in + x_in + n_out : n_in + x_in + n_out + x_out]
        rest = refs[n_in + x_in + n_out + x_out :]
        parts = (x_ins, x_outs, rest[n_scr], rest[n_scr + 1])
        ids = [pl.program_id(d) for d in range(len(grid))]
        first = _all([i == 0 for i in ids])
        last = _all([i == g - 1 for i, g in zip(ids, grid)])

        @pl.when(first)
        def _():
            ex.start(*parts)

        body(*ins, *outs, *rest[:n_scr])

        if ex.mid_step is not None:

            @pl.when(ids[0] == ex.mid_step)
            def _():
                ex.mid(*parts)

        @pl.when(last)
        def _():
            ex.finish(*parts)

    outs = pl.pallas_call(
        hosted,
        name=name,
        grid=grid,
        in_specs=list(in_specs) + [ANY] * x_in,
        out_specs=list(out_specs) + [ANY] * x_out,
        out_shape=list(out_shape) + ex.out_shapes,
        scratch_shapes=list(scratch_shapes) + [pltpu.SemaphoreType.DMA((ex.n_sems,)), pltpu.SemaphoreType.DMA((ex.n_sems,))],
        input_output_aliases={n_in + i: n_out + o for i, o in ex.aliases.items()},
        compiler_params=params,
    )(*operands, *ex.ins)
    return outs[:n_out], outs[n_out:]


def _all(conds):
    out = conds[0]
    for c in conds[1:]:
        out = jnp.logical_and(out, c)
    return out


def _row_tile(rows):
    if rows <= 512:
        return rows
    for t in (512, 352, 256, 176, 128, 112, 64, 32, 16, 8):
        if rows % t == 0:
            return t
    return rows


def _add_own_half(view, got, place, name):
    _, _, r, c = view.shape
    tr = _row_tile(r)

    def body(place_ref, v_ref, g_ref, o_ref, w_ref):
        s = v_ref[...] + g_ref[...]
        w_ref[...] = s.astype(w_ref.dtype)

        @pl.when(pl.program_id(1) == place_ref[0])
        def _():
            o_ref[...] = s

    blk = pl.BlockSpec((None, tr, c), lambda i, k, pr: (k, i, 0))
    return pl.pallas_call(
        body,
        name=name,
        out_shape=[jax.ShapeDtypeStruct((r, c), F32), jax.ShapeDtypeStruct((4, r, c), WIRE_DTYPE)],
        grid_spec=pltpu.PrefetchScalarGridSpec(
            num_scalar_prefetch=1,
            grid=(r // tr, 4),
            in_specs=[pl.BlockSpec((None, None, tr, c), lambda i, k, pr: (k, pr[1], i, 0)), blk],
            out_specs=[pl.BlockSpec((tr, c), lambda i, k, pr: (i, 0)), blk],
        ),
        compiler_params=pltpu.CompilerParams(dimension_semantics=("arbitrary", "arbitrary"), vmem_limit_bytes=VMEM_LIMIT),
    )(place, view, got)


def _add_chips(part, got, place, name):
    r, c = part.shape
    tr = _row_tile(r)

    def body(place_ref, p_ref, g_ref, o_ref):
        o_ref[...] = ((p_ref[...] + g_ref[0].astype(F32)) + g_ref[1].astype(F32)) + g_ref[2].astype(F32)

    return pl.pallas_call(
        body,
        name=name,
        out_shape=jax.ShapeDtypeStruct((2, r, c), F32),
        grid_spec=pltpu.PrefetchScalarGridSpec(
            num_scalar_prefetch=1,
            grid=(r // tr,),
            in_specs=[pl.BlockSpec((tr, c), lambda i, pr: (i, 0)), pl.BlockSpec((3, tr, c), lambda i, pr: (0, i, 0))],
            out_specs=pl.BlockSpec((None, tr, c), lambda i, pr: (pr[1], i, 0)),
        ),
        compiler_params=pltpu.CompilerParams(dimension_semantics=("arbitrary",), vmem_limit_bytes=VMEM_LIMIT),
    )(place, part, got)


def _adamw_update(w_ref, g_ref, m_ref, v_ref, d_ref, nm_ref, nv_ref):
    g = g_ref[...]
    nm = ADAM_B1 * m_ref[...] + (1.0 - ADAM_B1) * g
    nv = ADAM_B2 * v_ref[...] + (1.0 - ADAM_B2) * (g * g)
    m_hat = nm / (1.0 - ADAM_B1**ADAM_STEP)
    v_hat = nv / (1.0 - ADAM_B2**ADAM_STEP)
    d_ref[...] = -ADAM_LR * (m_hat / (jnp.sqrt(v_hat) + ADAM_EPS) + ADAM_WD * w_ref[...])
    nm_ref[...] = nm
    nv_ref[...] = nv


def _adamw(w, g, m, v, name):
    r, c = w.shape
    tr = _row_tile(r)
    spec = pl.BlockSpec((tr, c), lambda i: (i, 0))
    outs, _ = _pallas(_adamw_update, name=name, grid=(r // tr,), in_specs=[spec] * 4, out_specs=[spec] * 3, out_shape=[jax.ShapeDtypeStruct((r, c), F32)] * 3, operands=(w, g, m, v))
    return outs


ADAM_GROUP_ROWS = 64


def _adamw_group(items, name, exchange=None):
    tr = ADAM_GROUP_ROWS
    steps = [w.shape[0] // tr for w, _, _, _ in items]
    starts = [sum(steps[:a]) for a in range(len(items))]

    def body(*refs):
        i = pl.program_id(0)
        for a in range(len(items)):
            ins, outs = refs[4 * a : 4 * a + 4], refs[4 * len(items) + 3 * a : 4 * len(items) + 3 * a + 3]

            @pl.when(jnp.logical_and(i >= starts[a], i < starts[a] + steps[a]))
            def _(ins=ins, outs=outs):
                _adamw_update(*ins, *outs)

    def spec(a, cols):
        return pl.BlockSpec((tr, cols), lambda i: (jnp.clip(i - starts[a], 0, steps[a] - 1), 0))

    outs, x_outs = _pallas(
        body,
        name=name,
        grid=(sum(steps),),
        in_specs=[spec(a, w.shape[1]) for a, (w, _, _, _) in enumerate(items) for _ in range(4)],
        out_specs=[spec(a, w.shape[1]) for a, (w, _, _, _) in enumerate(items) for _ in range(3)],
        out_shape=[jax.ShapeDtypeStruct(w.shape, F32) for w, _, _, _ in items for _ in range(3)],
        operands=[t for item in items for t in item],
        exchange=exchange,
    )
    return [tuple(outs[3 * a : 3 * a + 3]) for a in range(len(items))], x_outs


def _adamw_many(ws, gs, ms, vs, name):
    n = len(ws)

    def body(*refs):
        for k in range(n):
            _adamw_update(*(refs[j * n + k] for j in range(7)))

    outs = pl.pallas_call(
        body,
        name=name,
        out_shape=[jax.ShapeDtypeStruct(w.shape, F32) for w in ws] * 3,
        in_specs=[pl.BlockSpec(memory_space=pltpu.VMEM)] * (4 * n),
        out_specs=[pl.BlockSpec(memory_space=pltpu.VMEM)] * (3 * n),
    )(*ws, *gs, *ms, *vs)
    return outs[:n], outs[n : 2 * n], outs[2 * n :]


def _load_weights(pairs, sems):
    @pl.when(pl.program_id(0) == 0)
    def _():
        cps = [pltpu.make_async_copy(src, dst, sems.at[k]) for k, (src, dst) in enumerate(pairs)]
        for cp in cps:
            cp.start()
        for cp in cps:
            cp.wait()


def _ffn_chunks(F):
    bounds = list(range(0, F, FFN_CHUNK)) + [F]
    return list(zip(bounds[:-1], bounds[1:]))


def _shifted_specs(tm, cols):
    per = tm // N_META
    return [_row_spec(tm, cols), pl.BlockSpec((N_META, cols), lambda i: (jnp.maximum(i * per - 1, 0), 0))]


def _shifted_tile(cur_ref, before_ref, tm):
    return jnp.concatenate([before_ref[...], cur_ref[0 : tm - N_META, :]], axis=0)


def _tokens_tile(cur_ref, before_ref, meta_ref, tm, tile_0):
    first = jnp.where(tile_0, meta_ref[...], before_ref[...])
    return jnp.concatenate([first, cur_ref[0 : tm - N_META, :]], axis=0)


def _ffn_fwd(h, g_pre, g_post, wg, wu, wd, name, exchange=None, meta=None, target=None):
    D = h.shape[1]
    T = h.shape[0] + (0 if meta is None else N_META)
    F = wg.shape[0]
    tm = FFN_TILE
    n_src = 1 if meta is None else 3
    n_tgt = 0 if target is None else 2
    n_head = 1 if target is None else 2

    def body(*refs):
        src, refs = refs[:n_src], refs[n_src:]
        tgt, refs = refs[:n_tgt], refs[n_tgt:]
        (gpre_ref, gpost_ref, wg_hbm, wu_hbm, wd_hbm), refs = refs[:5], refs[5:]
        head, (f_ref, ga_ref, si_ref, s_ref, n_ref, wg_v, wu_v, wd_v, sems) = refs[:n_head], refs[n_head:]
        i = pl.program_id(0)
        _load_weights([(wg_hbm, wg_v), (wu_hbm, wu_v), (wd_hbm, wd_v)], sems)
        hh = src[0][...] if meta is None else _tokens_tile(src[0], src[1], src[2], tm, i == 0)
        n = (hh * _rms_stat(hh) * gpre_ref[...]).astype(MXU_DTYPE)
        n_ref[...] = n.astype(n_ref.dtype)
        f = jnp.zeros((tm, D), F32)
        for lo, hi in _ffn_chunks(F):
            a = _mm_nt(n, wg_v[lo:hi, :])
            b = _mm_nt(n, wu_v[lo:hi, :])
            sg = _sigmoid(a)
            si = a * sg
            s = (si * b).astype(MXU_DTYPE)
            ga_ref[:, lo:hi] = (b * (sg * (1.0 + a * (1.0 - sg)))).astype(ga_ref.dtype)
            si_ref[:, lo:hi] = si.astype(si_ref.dtype)
            s_ref[:, lo:hi] = s.astype(s_ref.dtype)
            f = f + _mm(s, wd_v[lo:hi, :])
        f_ref[...] = f
        out = hh + 0.5 * (f * _rms_stat(f) * gpost_ref[...])
        if target is None:
            head[0][...] = out
        else:
            rows = i * tm + lax.broadcasted_iota(jnp.int32, (tm, D), 0)
            err = jnp.where(rows >= N_META, out - _shifted_tile(tgt[0], tgt[1], tm), 0.0)
            head[0][...] = err * (1.0 / D)
            _acc(head[1], jnp.sum(err * err, axis=0, keepdims=True), i == 0)

    tok = jax.ShapeDtypeStruct((T, D), F32)
    act = jax.ShapeDtypeStruct((T, F), MXU_DTYPE)
    src_specs = [_row_spec(tm, D)] if meta is None else _shifted_specs(tm, D) + [_full_spec((N_META, D))]
    src_ops = (h,) if meta is None else (h, h, meta)
    tgt_specs, tgt_ops = ([], ()) if target is None else (_shifted_specs(tm, D), (target, target))
    head_shapes = [tok] if target is None else [tok, jax.ShapeDtypeStruct((1, D), F32)]
    head_specs = [_row_spec(tm, D)] if target is None else [_row_spec(tm, D), _full_spec((1, D))]
    return _pallas(
        body,
        name=name,
        grid=(T // tm,),
        out_shape=head_shapes + [tok, act, act, act, jax.ShapeDtypeStruct((T, D), MXU_DTYPE)],
        in_specs=src_specs + tgt_specs + [_full_spec((1, D)), _full_spec((1, D)), ANY, ANY, ANY],
        out_specs=head_specs + [_row_spec(tm, D), _row_spec(tm, F), _row_spec(tm, F), _row_spec(tm, F), _row_spec(tm, D)],
        scratch_shapes=[pltpu.VMEM(wg.shape, wg.dtype), pltpu.VMEM(wu.shape, wu.dtype), pltpu.VMEM(wd.shape, wd.dtype), pltpu.SemaphoreType.DMA((3,))],
        operands=(*src_ops, *tgt_ops, g_pre, g_post, wg, wu, wd),
        exchange=exchange,
    )


def _ffn_bwd_down(dh, f, ga, si, g_post, wd, name, exchange=None):
    T, D = dh.shape
    F = wd.shape[0]
    tm = FFN_TILE

    def body(dh_ref, f_ref, ga_ref, si_ref, gpost_ref, wd_hbm, da_ref, db_ref, df_ref, dg_ref, wd_v, sems):
        _load_weights([(wd_hbm, wd_v)], sems)
        df, dg = _rms_bwd(f_ref[...], gpost_ref[...], 0.5 * dh_ref[...])
        _acc(dg_ref, dg, pl.program_id(0) == 0)
        dfb = df.astype(MXU_DTYPE)
        df_ref[...] = dfb.astype(df_ref.dtype)
        for lo, hi in _ffn_chunks(F):
            ds = _mm_nt(dfb, wd_v[lo:hi, :])
            da_ref[:, lo:hi] = (ds * ga_ref[:, lo:hi].astype(F32)).astype(da_ref.dtype)
            db_ref[:, lo:hi] = (ds * si_ref[:, lo:hi].astype(F32)).astype(db_ref.dtype)

    act = jax.ShapeDtypeStruct((T, F), MXU_DTYPE)
    return _pallas(
        body,
        name=name,
        grid=(T // tm,),
        out_shape=[act, act, jax.ShapeDtypeStruct((T, D), MXU_DTYPE), jax.ShapeDtypeStruct((1, D), F32)],
        in_specs=[_row_spec(tm, D), _row_spec(tm, D), _row_spec(tm, F), _row_spec(tm, F), _full_spec((1, D)), ANY],
        out_specs=[_row_spec(tm, F), _row_spec(tm, F), _row_spec(tm, D), _full_spec((1, D))],
        scratch_shapes=[pltpu.VMEM(wd.shape, wd.dtype), pltpu.SemaphoreType.DMA((1,))],
        operands=(dh, f, ga, si, g_post, wd),
        exchange=exchange,
    )


def _ffn_bwd_up(da, db, x, meta, dh, g_pre, wg, wu, name, exchange=None):
    D = x.shape[1]
    T = x.shape[0] + N_META
    F = wg.shape[0]
    tm = FFN_TILE
    nt = T // tm
    per = tm // N_META
    tile = lambda i: jnp.minimum(i, nt - 1)

    def body(da_ref, db_ref, x_ref, xb_ref, meta_ref, dh_ref, gpre_ref, wg_hbm, wu_hbm, dx_ref, dmeta_ref, dg_ref, wg_v, wu_v, sems, held):
        i = pl.program_id(0)
        _load_weights([(wg_hbm, wg_v), (wu_hbm, wu_v)], sems)

        @pl.when(i < nt)
        def _():
            dn = jnp.zeros((tm, D), F32)
            for lo, hi in _ffn_chunks(F):
                dn = dn + _mm(da_ref[:, lo:hi], wg_v[lo:hi, :]) + _mm(db_ref[:, lo:hi], wu_v[lo:hi, :])
            dx, dg = _rms_bwd(_tokens_tile(x_ref, xb_ref, meta_ref, tm, i == 0), gpre_ref[...], dn)
            _acc(dg_ref, dg, i == 0)
            dh_in = dh_ref[...] + dx

            @pl.when(i == 0)
            def _():
                dmeta_ref[...] = dh_in[0:N_META, :]

            @pl.when(i > 0)
            def _():
                dx_ref[...] = jnp.concatenate([held[...], dh_in[0:N_META, :]], axis=0)

            held[...] = dh_in[N_META:, :]

        @pl.when(i == nt)
        def _():
            dx_ref[0 : tm - N_META, :] = held[...]

    rows = lambda cols: pl.BlockSpec((tm, cols), lambda i: (tile(i), 0))
    return _pallas(
        body,
        name=name,
        grid=(nt + 1,),
        out_shape=[jax.ShapeDtypeStruct((T - N_META, D), F32), jax.ShapeDtypeStruct((N_META, D), F32), jax.ShapeDtypeStruct((1, D), F32)],
        in_specs=[rows(F), rows(F), rows(D), pl.BlockSpec((N_META, D), lambda i: (jnp.maximum(tile(i) * per - 1, 0), 0)), _full_spec((N_META, D)), rows(D), _full_spec((1, D)), ANY, ANY],
        out_specs=[pl.BlockSpec((tm, D), lambda i: (jnp.maximum(i - 1, 0), 0)), _full_spec((N_META, D)), _full_spec((1, D))],
        scratch_shapes=[pltpu.VMEM(wg.shape, wg.dtype), pltpu.VMEM(wu.shape, wu.dtype), pltpu.SemaphoreType.DMA((2,)), pltpu.VMEM((tm - N_META, D), F32)],
        operands=(da, db, x, x, meta, dh, g_pre, wg, wu),
        exchange=exchange,
    )


def _ffn_bwd(dh, f, ga, si, h, g_post, g_pre, wg, wu, wd, name):
    T, D = dh.shape
    F = wd.shape[0]
    tm = FFN_TILE

    def body(dh_ref, f_ref, ga_ref, si_ref, h_ref, gpost_ref, gpre_ref, wg_hbm, wu_hbm, wd_hbm, da_ref, db_ref, df_ref, dhin_ref, dgpost_ref, dgpre_ref, wg_v, wu_v, wd_v, sems):
        first = pl.program_id(0) == 0
        _load_weights([(wg_hbm, wg_v), (wu_hbm, wu_v), (wd_hbm, wd_v)], sems)
        dh = dh_ref[...]
        df, dg = _rms_bwd(f_ref[...], gpost_ref[...], 0.5 * dh)
        _acc(dgpost_ref, dg, first)
        dfb = df.astype(MXU_DTYPE)
        df_ref[...] = dfb.astype(df_ref.dtype)
        dn = jnp.zeros((tm, D), F32)
        for lo, hi in _ffn_chunks(F):
            ds = _mm_nt(dfb, wd_v[lo:hi, :])
            da = (ds * ga_ref[:, lo:hi].astype(F32)).astype(MXU_DTYPE)
            db = (ds * si_ref[:, lo:hi].astype(F32)).astype(MXU_DTYPE)
            da_ref[:, lo:hi] = da.astype(da_ref.dtype)
            db_ref[:, lo:hi] = db.astype(db_ref.dtype)
            dn = dn + _mm(da, wg_v[lo:hi, :]) + _mm(db, wu_v[lo:hi, :])
        dx, dg = _rms_bwd(h_ref[...], gpre_ref[...], dn)
        _acc(dgpre_ref, dg, first)
        dhin_ref[...] = dh + dx

    act = jax.ShapeDtypeStruct((T, F), MXU_DTYPE)
    vec = jax.ShapeDtypeStruct((1, D), F32)
    outs, _ = _pallas(
        body,
        name=name,
        grid=(T // tm,),
        out_shape=[act, act, jax.ShapeDtypeStruct((T, D), MXU_DTYPE), jax.ShapeDtypeStruct((T, D), F32), vec, vec],
        in_specs=[_row_spec(tm, D), _row_spec(tm, D), _row_spec(tm, F), _row_spec(tm, F), _row_spec(tm, D), _full_spec((1, D)), _full_spec((1, D)), ANY, ANY, ANY],
        out_specs=[_row_spec(tm, F), _row_spec(tm, F), _row_spec(tm, D), _row_spec(tm, D), _full_spec((1, D)), _full_spec((1, D))],
        scratch_shapes=[pltpu.VMEM(wg.shape, wg.dtype), pltpu.VMEM(wu.shape, wu.dtype), pltpu.VMEM(wd.shape, wd.dtype), pltpu.SemaphoreType.DMA((3,))],
        operands=(dh, f, ga, si, h, g_post, g_pre, wg, wu, wd),
    )
    return outs


def _token_tile(T):
    for t in (912, 864, 432):
        if T % t == 0:
            return t
    raise ValueError(f"no token tile for {T} rows")


def _tn_matmul(xm, ym, name, exchange=None):
    T, M = xm.shape
    N = ym.shape[1]
    if (T - N_META) % TN_TILE:
        tk = _token_tile(T)

        def body(x_ref, y_ref, o_ref):
            _acc(o_ref, _mm_tn(x_ref[...], y_ref[...]), pl.program_id(0) == 0)

        grid, operands = (T // tk,), (xm, ym)
        in_specs = [pl.BlockSpec((tk, M), lambda k: (k, 0)), pl.BlockSpec((tk, N), lambda k: (k, 0))]
    else:
        tk = TN_TILE

        def body(x_ref, y_ref, xh_ref, yh_ref, o_ref):
            prod = _mm_tn(x_ref[...], y_ref[...])

            @pl.when(pl.program_id(0) == 0)
            def _():
                o_ref[...] = prod + _mm_tn(xh_ref[...], yh_ref[...])

            @pl.when(pl.program_id(0) > 0)
            def _():
                o_ref[...] += prod

        grid, operands = ((T - N_META) // tk,), (xm, ym, xm, ym)
        start = lambda k: (pl.multiple_of(N_META + k * tk, N_META), 0)
        in_specs = [pl.BlockSpec((pl.Element(tk), pl.Element(M)), start), pl.BlockSpec((pl.Element(tk), pl.Element(N)), start),
                    pl.BlockSpec((N_META, M), lambda k: (0, 0)), pl.BlockSpec((N_META, N), lambda k: (0, 0))]

    (out,), x_outs = _pallas(
        body,
        name=name,
        grid=grid,
        out_shape=[jax.ShapeDtypeStruct((M, N), F32)],
        in_specs=in_specs,
        out_specs=[_full_spec((M, N))],
        operands=operands,
        exchange=exchange,
    )
    return out, x_outs


TAB_A, TAB_AS1, TAB_AS2, TAB_AS4, TAB_JF, TAB_JB = 0, 2, 4, 6, 8, 10


def _scan_inplace(zr, zi, tabs, pows, car_r, car_i, seg, reverse, base=0):
    n_slabs = zr.shape[0]
    sgn = -1.0 if reverse else 1.0
    row = lax.broadcasted_iota(jnp.int32, (SUBLANES, LANES), 0)

    def cmul(pr, pi, xr, xi):
        return pr * xr - pi * xi, pr * xi + pi * xr

    for k0 in range(0, n_slabs, SLAB_GROUP):
        slabs = range(k0, min(k0 + SLAB_GROUP, n_slabs))
        ar = [tabs[TAB_A, k] for k in slabs]
        ai = [sgn * tabs[TAB_A + 1, k] for k in slabs]

        def first_pass(t, carry):
            r = (seg - 1 - t) if reverse else t
            out = []
            for q, k in enumerate(slabs):
                xr, xi = carry[2 * q], carry[2 * q + 1]
                pr, pi = cmul(ar[q], ai[q], xr, xi)
                nr = pr + zr[k, pl.ds(base + r, SUBLANES, stride=seg), :]
                ni = pi + zi[k, pl.ds(base + r, SUBLANES, stride=seg), :]
                zr[k, pl.ds(base + r, SUBLANES, stride=seg), :] = nr
                zi[k, pl.ds(base + r, SUBLANES, stride=seg), :] = ni
                out += [nr, ni]
            return tuple(out)

        ends = lax.fori_loop(0, seg, first_pass, tuple(jnp.zeros((SUBLANES, LANES), F32) for _ in range(2 * len(slabs))))

        incoming = []
        for q, k in enumerate(slabs):
            fr, fi = ends[2 * q], ends[2 * q + 1]
            for d, tab in ((1, TAB_AS1), (2, TAB_AS2), (4, TAB_AS4)):
                shift, keep = (SUBLANES - d, row < SUBLANES - d) if reverse else (d, row >= d)
                sr = jnp.where(keep, pltpu.roll(fr, shift, 0), 0.0)
                si = jnp.where(keep, pltpu.roll(fi, shift, 0), 0.0)
                pr, pi = cmul(tabs[tab, k], sgn * tabs[tab + 1, k], sr, si)
                fr, fi = fr + pr, fi + pi
            cr, ci = car_r[k], car_i[k]
            jtab = TAB_JB if reverse else TAB_JF
            pr, pi = cmul(tabs[jtab, k], sgn * tabs[jtab + 1, k], cr, ci)
            er, ei = fr + pr, fi + pi
            if reverse:
                inr = jnp.where(row < SUBLANES - 1, pltpu.roll(er, SUBLANES - 1, 0), cr)
                ini = jnp.where(row < SUBLANES - 1, pltpu.roll(ei, SUBLANES - 1, 0), ci)
                car_r[k] = jnp.broadcast_to(er[0:1, :], (SUBLANES, LANES))
                car_i[k] = jnp.broadcast_to(ei[0:1, :], (SUBLANES, LANES))
            else:
                inr = jnp.where(row >= 1, pltpu.roll(er, 1, 0), cr)
                ini = jnp.where(row >= 1, pltpu.roll(ei, 1, 0), ci)
                car_r[k] = jnp.broadcast_to(er[SUBLANES - 1 : SUBLANES, :], (SUBLANES, LANES))
                car_i[k] = jnp.broadcast_to(ei[SUBLANES - 1 : SUBLANES, :], (SUBLANES, LANES))
            incoming += [inr, ini]

        def second_pass(r, _):
            p = (seg - 1 - r) if reverse else r
            for q, k in enumerate(slabs):
                pr, pi = cmul(pows[0, k, p], sgn * pows[1, k, p], incoming[2 * q], incoming[2 * q + 1])
                zr[k, pl.ds(base + r, SUBLANES, stride=seg), :] = zr[k, pl.ds(base + r, SUBLANES, stride=seg), :] + pr
                zi[k, pl.ds(base + r, SUBLANES, stride=seg), :] = zi[k, pl.ds(base + r, SUBLANES, stride=seg), :] + pi
            return 0

        lax.fori_loop(0, seg, second_pass, 0)


def _slabs_to_cols(ref, k0, n):
    return jnp.concatenate([ref[k0 + q] for q in range(n)], axis=1)


def _window_sum(ext, doublings, forward):
    rows = ext.shape[0]
    s = ext
    for k in range(doublings):
        s = s + pltpu.roll(s, (1 << k) if forward else rows - (1 << k), 0)
    return s


def _mix_fwd(h1, g_pre, g_so, g_po, g_post, dskip, pscale, win, wout, bbre, bbim, ccre, ccim, wgv, wgg, pw, tabs, pows, name):
    T, D = h1.shape
    W = D // 2
    tm = MIX_SUBTILES * MIX_TILE
    seg = MIX_TILE // SUBLANES
    n_slabs = tabs.shape[1]
    nch, cch, sch = bbre.shape
    spc = sch // LANES
    pg = W // len(POOL_WINDOWS)

    def body(h_ref, gpre_ref, gso_ref, gpo_ref, gpost_ref, dskip_ref, pscale_ref, win_ref, wout_ref, bbre_ref, bbim_ref, ccre_ref, ccim_ref, wgv_ref, wgg_ref, pw_ref, tabs_ref, pows_ref,
             proj_ref, xr_ref, xi_ref, y_ref, pooled_ref, mixed_ref, h2_ref, n2_ref, cat_ref, car_r, car_i, halo):
        i = pl.program_id(0)

        @pl.when(i == 0)
        def _():
            car_r[...] = jnp.zeros_like(car_r)
            car_i[...] = jnp.zeros_like(car_i)
            halo[...] = jnp.zeros_like(halo)

        hh = h_ref[...]
        n2 = (hh * _rms_stat(hh) * gpre_ref[...]).astype(MXU_DTYPE)
        n2_ref[...] = n2.astype(n2_ref.dtype)
        proj = _mm(n2, win_ref[...])
        proj_ref[...] = proj
        us, up = proj[:, :W], proj[:, W:]

        for c in range(nch):
            uc = us[:, c * cch : (c + 1) * cch].astype(MXU_DTYPE)
            bur, bui = _mm(uc, bbre_ref[c]), _mm(uc, bbim_ref[c])
            for q in range(spc):
                xr_ref[c * spc + q] = bur[:, q * LANES : (q + 1) * LANES]
                xi_ref[c * spc + q] = bui[:, q * LANES : (q + 1) * LANES]
        for sub in range(MIX_SUBTILES):
            _scan_inplace(xr_ref, xi_ref, tabs_ref, pows_ref, car_r, car_i, seg, reverse=False, base=sub * MIX_TILE)
        ys = []
        for c in range(nch):
            ys.append(_mm(_slabs_to_cols(xr_ref, c * spc, spc), ccre_ref[c]) - _mm(_slabs_to_cols(xi_ref, c * spc, spc), ccim_ref[c]))
        y = jnp.concatenate(ys, axis=1) + dskip_ref[...] * us
        y_ref[...] = y
        ge = _gelu(y).astype(MXU_DTYPE)
        zv = jnp.concatenate([_mm(ge[:, c * cch : (c + 1) * cch], wgv_ref[c]) for c in range(nch)], axis=1)
        zg = jnp.concatenate([_mm(ge[:, c * cch : (c + 1) * cch], wgg_ref[c]) for c in range(nch)], axis=1)
        out = zv * _sigmoid(zg)
        cat_s = out * _rms_stat(out) * gso_ref[...]

        ext = jnp.concatenate([halo[...], up], axis=0)
        halo[...] = up[tm - POOL_HALO :, :]
        t1 = (i * tm + 1 + lax.broadcasted_iota(jnp.int32, (tm, pg), 0)).astype(F32)
        pooled, pms = [], []
        for g, w in enumerate(POOL_WINDOWS):
            col = ext[:, g * pg : (g + 1) * pg]
            win_sum = _window_sum(col, g + 1, True)[POOL_HALO:, :]
            pooled_g = win_sum / jnp.minimum(t1, float(w)) - up[:, g * pg : (g + 1) * pg]
            pooled.append(pooled_g)
            pms.append(_mm(pooled_g, pw_ref[g]))
        pooled_ref[...] = jnp.concatenate(pooled, axis=1)
        yp = jnp.concatenate(pms, axis=1) * pscale_ref[...]
        cat_p = yp * _rms_stat(yp) * gpo_ref[...]

        cat = jnp.concatenate([cat_s, cat_p], axis=1).astype(MXU_DTYPE)
        cat_ref[...] = cat.astype(cat_ref.dtype)
        mixed = _mm(cat, wout_ref[...])
        mixed_ref[...] = mixed
        h2_ref[...] = hh + mixed * _rms_stat(mixed) * gpost_ref[...]

    tok = lambda cols, dt=F32: jax.ShapeDtypeStruct((T, cols), dt)
    slab_spec = pl.BlockSpec((n_slabs, tm, LANES), lambda i: (0, i, 0))
    operands = (h1, g_pre, g_so, g_po, g_post, dskip, pscale, win, wout, bbre, bbim, ccre, ccim, wgv, wgg, pw, tabs, pows)
    outs, _ = _pallas(
        body,
        name=name,
        grid=(T // tm,),
        out_shape=[tok(D), jax.ShapeDtypeStruct((n_slabs, T, LANES), F32), jax.ShapeDtypeStruct((n_slabs, T, LANES), F32), tok(W), tok(W), tok(D), tok(D), tok(D, MXU_DTYPE), tok(D, MXU_DTYPE)],
        in_specs=[_row_spec(tm, D)] + [_full_spec(o.shape, single=True) for o in operands[1:]],
        out_specs=[_row_spec(tm, D), slab_spec, slab_spec, _row_spec(tm, W), _row_spec(tm, W), _row_spec(tm, D), _row_spec(tm, D), _row_spec(tm, D), _row_spec(tm, D)],
        scratch_shapes=[pltpu.VMEM((n_slabs, SUBLANES, LANES), F32), pltpu.VMEM((n_slabs, SUBLANES, LANES), F32), pltpu.VMEM((POOL_HALO, W), F32)],
        operands=operands,
    )
    return outs


def _mix_bwd_heads(dh2, mixed, y, pooled, proj, g_so, g_po, g_post, pscale, wout, wgv, wgg, pw, name, exchange=None):
    T, D = dh2.shape
    W = D // 2
    tm = FFN_TILE
    nch, cch, _ = wgv.shape
    ng, pg, _ = pw.shape

    def body(dh2_ref, mixed_ref, y_ref, pooled_ref, us_ref, gso_ref, gpo_ref, gpost_ref, pscale_ref, wout_ref, wgv_ref, wgg_ref, pw_ref,
             dy_ref, dpooled_ref, dmixed_ref, dgpost_ref, dgso_ref, dgpo_ref, dd_ref, dscale_ref, dwgv_ref, dwgg_ref, dpw_ref):
        first = pl.program_id(0) == 0
        dmixed, dgpost = _rms_bwd(mixed_ref[...], gpost_ref[...], dh2_ref[...])
        _acc(dgpost_ref, dgpost, first)
        dmb = dmixed.astype(MXU_DTYPE)
        dmixed_ref[...] = dmb.astype(dmixed_ref.dtype)
        dcat = _mm_nt(dmb, wout_ref[...])
        dcs, dcp = dcat[:, :W], dcat[:, W:]

        y = y_ref[...]
        ge = _gelu(y).astype(MXU_DTYPE)
        zv = jnp.concatenate([_mm(ge[:, c * cch : (c + 1) * cch], wgv_ref[c]) for c in range(nch)], axis=1)
        zg = jnp.concatenate([_mm(ge[:, c * cch : (c + 1) * cch], wgg_ref[c]) for c in range(nch)], axis=1)
        sg = _sigmoid(zg)
        dout, dgso = _rms_bwd(zv * sg, gso_ref[...], dcs)
        _acc(dgso_ref, dgso, first)
        dzv = (dout * sg).astype(MXU_DTYPE)
        dzg = (dout * zv * sg * (1.0 - sg)).astype(MXU_DTYPE)
        dges = []
        for c in range(nch):
            cs = slice(c * cch, (c + 1) * cch)
            dges.append(_mm_nt(dzv[:, cs], wgv_ref[c]) + _mm_nt(dzg[:, cs], wgg_ref[c]))
            _acc(dwgv_ref.at[c], _mm_tn(ge[:, cs], dzv[:, cs]), first)
            _acc(dwgg_ref.at[c], _mm_tn(ge[:, cs], dzg[:, cs]), first)
        dy = jnp.concatenate(dges, axis=1) * _gelu_grad(y)
        dy_ref[...] = dy
        _acc(dd_ref, jnp.sum(dy * us_ref[...], axis=0, keepdims=True), first)

        pooled_b = pooled_ref[...].astype(MXU_DTYPE)
        pm = jnp.concatenate([_mm(pooled_b[:, g * pg : (g + 1) * pg], pw_ref[g]) for g in range(ng)], axis=1)
        dyp, dgpo = _rms_bwd(pm * pscale_ref[...], gpo_ref[...], dcp)
        _acc(dgpo_ref, dgpo, first)
        _acc(dscale_ref, jnp.sum(dyp * pm, axis=0, keepdims=True), first)
        dpm = (dyp * pscale_ref[...]).astype(MXU_DTYPE)
        dps = []
        for g in range(ng):
            gs = slice(g * pg, (g + 1) * pg)
            dps.append(_mm_nt(dpm[:, gs], pw_ref[g]))
            _acc(dpw_ref.at[g], _mm_tn(pooled_b[:, gs], dpm[:, gs]), first)
        dpooled_ref[...] = jnp.concatenate(dps, axis=1)

    vec = lambda n: jax.ShapeDtypeStruct((1, n), F32)
    operands = (dh2, mixed, y, pooled, proj, g_so, g_po, g_post, pscale, wout, wgv, wgg, pw)
    return _pallas(
        body,
        name=name,
        grid=(T // tm,),
        out_shape=[jax.ShapeDtypeStruct((T, W), F32), jax.ShapeDtypeStruct((T, W), F32), jax.ShapeDtypeStruct((T, D), MXU_DTYPE), vec(D), vec(W), vec(W), vec(W), vec(W),
                   jax.ShapeDtypeStruct(wgv.shape, F32), jax.ShapeDtypeStruct(wgg.shape, F32), jax.ShapeDtypeStruct(pw.shape, F32)],
        in_specs=[_row_spec(tm, D), _row_spec(tm, D), _row_spec(tm, W), _row_spec(tm, W), _row_spec(tm, W)] + [_full_spec(o.shape) for o in operands[5:]],
        out_specs=[_row_spec(tm, W), _row_spec(tm, W), _row_spec(tm, D), _full_spec((1, D)), _full_spec((1, W)), _full_spec((1, W)), _full_spec((1, W)), _full_spec((1, W)),
                   _full_spec(wgv.shape), _full_spec(wgg.shape), _full_spec(pw.shape)],
        operands=operands,
        exchange=exchange,
    )


def _mix_bwd_scan(dy, dpooled, xr, xi, proj, dskip, bbre, bbim, ccre, ccim, tabs, pows, name, exchange=None):
    T, W = dy.shape
    D = 2 * W
    tm = MIX_SUBTILES * MIX_TILE
    seg = MIX_TILE // SUBLANES
    nt = T // tm
    n_slabs = tabs.shape[1]
    nch, cch, sch = bbre.shape
    spc = sch // LANES
    pg = W // len(POOL_WINDOWS)
    blocks_per_tile = tm // SUBLANES

    def body(dy_ref, dp_ref, xr_ref, xi_ref, xpr_ref, xpi_ref, us_ref, dskip_ref, bbre_ref, bbim_ref, ccre_ref, ccim_ref, tabs_ref, pows_ref,
             dproj_ref, dccre_ref, dccim_ref, dbbre_ref, dbbim_ref, dar_ref, dai_ref, lr, li, car_r, car_i, halo):
        i = pl.program_id(0)
        first = i == 0
        tile = nt - 1 - i
        row = lax.broadcasted_iota(jnp.int32, (SUBLANES, LANES), 0)

        @pl.when(first)
        def _():
            car_r[...] = jnp.zeros_like(car_r)
            car_i[...] = jnp.zeros_like(car_i)
            halo[...] = jnp.zeros_like(halo)
            dar_ref[...] = jnp.zeros_like(dar_ref)
            dai_ref[...] = jnp.zeros_like(dai_ref)

        dy = dy_ref[...]
        for c in range(nch):
            dyc = dy[:, c * cch : (c + 1) * cch]
            gr, gi = _mm_nt(dyc, ccre_ref[c]), _mm_nt(dyc, ccim_ref[c])
            for q in range(spc):
                lr[c * spc + q] = gr[:, q * LANES : (q + 1) * LANES]
                li[c * spc + q] = -gi[:, q * LANES : (q + 1) * LANES]
            _acc(dccre_ref.at[c], _mm_tn(_slabs_to_cols(xr_ref, c * spc, spc), dyc), first)
            _acc(dccim_ref.at[c], -_mm_tn(_slabs_to_cols(xi_ref, c * spc, spc), dyc), first)
        for sub in reversed(range(MIX_SUBTILES)):
            _scan_inplace(lr, li, tabs_ref, pows_ref, car_r, car_i, seg, reverse=True, base=sub * MIX_TILE)

        for sub in range(MIX_SUBTILES):
            base = sub * MIX_TILE
            for k0 in range(0, n_slabs, SLAB_GROUP):
                slabs = range(k0, min(k0 + SLAB_GROUP, n_slabs))
                init = []
                for k in slabs:
                    if sub == 0:
                        prev_r = jnp.where(tile > 0, jnp.broadcast_to(xpr_ref[k, SUBLANES - 1 : SUBLANES, :], (SUBLANES, LANES)), 0.0)
                        prev_i = jnp.where(tile > 0, jnp.broadcast_to(xpi_ref[k, SUBLANES - 1 : SUBLANES, :], (SUBLANES, LANES)), 0.0)
                    else:
                        prev_r = jnp.broadcast_to(xr_ref[k, base - 1 : base, :], (SUBLANES, LANES))
                        prev_i = jnp.broadcast_to(xi_ref[k, base - 1 : base, :], (SUBLANES, LANES))
                    x0r = jnp.where(row >= 1, pltpu.roll(xr_ref[k, pl.ds(base + seg - 1, SUBLANES, stride=seg), :], 1, 0), prev_r)
                    x0i = jnp.where(row >= 1, pltpu.roll(xi_ref[k, pl.ds(base + seg - 1, SUBLANES, stride=seg), :], 1, 0), prev_i)
                    l0r, l0i = lr[k, pl.ds(base, SUBLANES, stride=seg), :], li[k, pl.ds(base, SUBLANES, stride=seg), :]
                    init += [l0r * x0r + l0i * x0i, l0i * x0r - l0r * x0i]

                def step(r, acc, slabs=slabs, base=base):
                    out = []
                    for q, k in enumerate(slabs):
                        pr_, pi_ = xr_ref[k, pl.ds(base + r - 1, SUBLANES, stride=seg), :], xi_ref[k, pl.ds(base + r - 1, SUBLANES, stride=seg), :]
                        lr_, li_ = lr[k, pl.ds(base + r, SUBLANES, stride=seg), :], li[k, pl.ds(base + r, SUBLANES, stride=seg), :]
                        out += [acc[2 * q] + lr_ * pr_ + li_ * pi_, acc[2 * q + 1] + li_ * pr_ - lr_ * pi_]
                    return tuple(out)

                sums = lax.fori_loop(1, seg, step, tuple(init))
                for q, k in enumerate(slabs):
                    dar_ref[k] += sums[2 * q]
                    dai_ref[k] += sums[2 * q + 1]

        us = us_ref[...]
        dus = []
        for c in range(nch):
            lrc, lic = _slabs_to_cols(lr, c * spc, spc).astype(MXU_DTYPE), _slabs_to_cols(li, c * spc, spc).astype(MXU_DTYPE)
            uc = us[:, c * cch : (c + 1) * cch]
            _acc(dbbre_ref.at[c], _mm_tn(uc, lrc), first)
            _acc(dbbim_ref.at[c], _mm_tn(uc, lic), first)
            dus.append(_mm_nt(lrc, bbre_ref[c]) + _mm_nt(lic, bbim_ref[c]))
        du_s = jnp.concatenate(dus, axis=1) + dskip_ref[...] * dy

        dp = dp_ref[...]
        t1 = (tile * tm + 1 + lax.broadcasted_iota(jnp.int32, (tm, pg), 0)).astype(F32)
        dups, heads = [], []
        for g, w in enumerate(POOL_WINDOWS):
            dpg = dp[:, g * pg : (g + 1) * pg]
            qg = dpg / jnp.minimum(t1, float(w))
            ext = jnp.concatenate([qg, halo[:, g * pg : (g + 1) * pg]], axis=0)
            dups.append(_window_sum(ext, g + 1, False)[:tm, :] - dpg)
            heads.append(qg[:POOL_HALO, :])
        halo[...] = jnp.concatenate(heads, axis=1)
        dproj_ref[...] = jnp.concatenate([du_s] + dups, axis=1).astype(dproj_ref.dtype)

    rev = lambda cols: _row_spec(tm, cols, rev_n=nt)
    slab_spec = pl.BlockSpec((n_slabs, tm, LANES), lambda i: (0, nt - 1 - i, 0))
    prev_spec = pl.BlockSpec((n_slabs, SUBLANES, LANES), lambda i: (0, jnp.maximum((nt - 1 - i) * blocks_per_tile - 1, 0), 0))
    consts = (dskip, bbre, bbim, ccre, ccim, tabs, pows)
    return _pallas(
        body,
        name=name,
        grid=(nt,),
        out_shape=[jax.ShapeDtypeStruct((T, D), MXU_DTYPE), jax.ShapeDtypeStruct(ccre.shape, F32), jax.ShapeDtypeStruct(ccim.shape, F32), jax.ShapeDtypeStruct(bbre.shape, F32),
                   jax.ShapeDtypeStruct(bbim.shape, F32), jax.ShapeDtypeStruct((n_slabs, SUBLANES, LANES), F32), jax.ShapeDtypeStruct((n_slabs, SUBLANES, LANES), F32)],
        in_specs=[rev(W), rev(W), slab_spec, slab_spec, prev_spec, prev_spec, rev(W)] + [_full_spec(o.shape, single=True) for o in consts],
        out_specs=[rev(D), _full_spec(ccre.shape), _full_spec(ccim.shape), _full_spec(bbre.shape), _full_spec(bbim.shape),
                   _full_spec((n_slabs, SUBLANES, LANES)), _full_spec((n_slabs, SUBLANES, LANES))],
        scratch_shapes=[pltpu.VMEM((n_slabs, tm, LANES), F32), pltpu.VMEM((n_slabs, tm, LANES), F32), pltpu.VMEM((n_slabs, SUBLANES, LANES), F32), pltpu.VMEM((n_slabs, SUBLANES, LANES), F32),
                        pltpu.VMEM((POOL_HALO, W), F32)],
        operands=(dy, dpooled, xr, xi, xr, xi, proj, *consts),
        exchange=exchange,
    )


def _mix_bwd_in(dproj, h1, dh2, g_pre, win, name):
    T, D = h1.shape
    tm = _token_tile(T)

    def body(dproj_ref, h_ref, dh2_ref, gpre_ref, win_ref, dh1_ref, dg_ref):
        dx, dg = _rms_bwd(h_ref[...], gpre_ref[...], _mm_nt(dproj_ref[...], win_ref[...]))
        _acc(dg_ref, dg, pl.program_id(0) == 0)
        dh1_ref[...] = dh2_ref[...] + dx

    outs, _ = _pallas(
        body,
        name=name,
        grid=(T // tm,),
        out_shape=[jax.ShapeDtypeStruct((T, D), F32), jax.ShapeDtypeStruct((1, D), F32)],
        in_specs=[_row_spec(tm, D), _row_spec(tm, D), _row_spec(tm, D), _full_spec((1, D)), _full_spec(win.shape)],
        out_specs=[_row_spec(tm, D), _full_spec((1, D))],
        operands=(dproj, h1, dh2, g_pre, win),
    )
    return outs


def _discretize(lam_re, lam_im, log_dt, b_re, b_im):
    dt = jnp.exp(log_dt)[:, None]
    decay = jnp.exp(lam_re * dt)
    ang = lam_im * dt
    a_re, a_im = decay * jnp.cos(ang), decay * jnp.sin(ang)
    nr = a_re - 1.0
    den = lam_re * lam_re + lam_im * lam_im
    q_re = (nr * lam_re + a_im * lam_im) / den
    q_im = (a_im * lam_re - nr * lam_im) / den
    bb_re = q_re[..., None] * b_re - q_im[..., None] * b_im
    bb_im = q_re[..., None] * b_im + q_im[..., None] * b_re
    return a_re, a_im, bb_re, bb_im


GROUPS_PER_CHUNK = 16


def _block_diag(w, rows_first):
    G = w.shape[0]
    nch = G // GROUPS_PER_CHUNK
    if not rows_first:
        w = jnp.swapaxes(w, 1, 2)
    p, q = w.shape[1], w.shape[2]
    eye = jnp.eye(GROUPS_PER_CHUNK, dtype=w.dtype)
    out = jnp.einsum("cgpq,gk->cgpkq", w.reshape(nch, GROUPS_PER_CHUNK, p, q), eye)
    return out.reshape(nch, GROUPS_PER_CHUNK * p, GROUPS_PER_CHUNK * q)


def _block_diag_extract(m, p, q, rows_first):
    nch = m.shape[0]
    eye = jnp.eye(GROUPS_PER_CHUNK, dtype=m.dtype)
    out = jnp.einsum("cgpkq,gk->cgpq", m.reshape(nch, GROUPS_PER_CHUNK, p, GROUPS_PER_CHUNK, q), eye).reshape(nch * GROUPS_PER_CHUNK, p, q)
    return out if rows_first else jnp.swapaxes(out, 1, 2)


def _cmul(ar, ai, br, bi):
    return ar * br - ai * bi, ar * bi + ai * br


def _powers(ar, ai, count):
    pr, pi = ar[None], ai[None]
    while pr.shape[0] < count:
        nr, ni = _cmul(pr, pi, pr[-1][None], pi[-1][None])
        pr, pi = jnp.concatenate([pr, nr]), jnp.concatenate([pi, ni])
    return pr[:count], pi[:count]


def _scan_tables(a_re, a_im, seg):
    n = a_re.size
    ns = n // LANES
    ar, ai = a_re.reshape(n), a_im.reshape(n)
    pr, pi = _powers(ar, ai, seg)
    jr, ji = _powers(pr[-1], pi[-1], SUBLANES)

    def bcast(v):
        return jnp.broadcast_to(v.reshape(ns, 1, LANES), (ns, SUBLANES, LANES))

    def per_sublane(vs):
        return jnp.transpose(vs.reshape(SUBLANES, ns, LANES), (1, 0, 2))

    tabs = jnp.stack([bcast(ar), bcast(ai), bcast(jr[0]), bcast(ji[0]), bcast(jr[1]), bcast(ji[1]), bcast(jr[3]), bcast(ji[3]),
                      per_sublane(jr), per_sublane(ji), per_sublane(jr[::-1]), per_sublane(ji[::-1])])

    def rows(vs):
        return jnp.broadcast_to(jnp.transpose(vs.reshape(seg, ns, 1, LANES), (1, 0, 2, 3)), (ns, seg, SUBLANES, LANES))

    return tabs, jnp.stack([rows(pr), rows(pi)])


SMALL = ("ffn1_pre_norm", "ffn1_post_norm", "mix_pre_norm", "mix_post_norm", "ssm_lambda_re", "ssm_lambda_im", "ssm_log_dt", "ssm_b_re", "ssm_b_im", "ssm_c_re", "ssm_c_im",
         "ssm_d", "ssm_w_glu", "pool_w", "pool_scale", "ssm_out_norm", "pool_out_norm", "ffn2_pre_norm", "ffn2_post_norm")
BIG = ("ffn1_w_gate", "ffn1_w_up", "ffn1_w_down", "w_in", "w_out", "ffn2_w_gate", "ffn2_w_up", "ffn2_w_down")
ORDER = ("meta_tokens", "ffn1_pre_norm", "ffn1_post_norm", "ffn1_w_gate", "ffn1_w_up", "ffn1_w_down", "mix_pre_norm", "mix_post_norm", "w_in", "ssm_lambda_re", "ssm_lambda_im",
         "ssm_log_dt", "ssm_b_re", "ssm_b_im", "ssm_c_re", "ssm_c_im", "ssm_d", "ssm_w_glu", "pool_w", "pool_scale", "ssm_out_norm", "pool_out_norm", "w_out", "ffn2_pre_norm",
         "ffn2_post_norm", "ffn2_w_gate", "ffn2_w_up", "ffn2_w_down")
PACK_ROWS = SUBLANES * 8
def _pack(arrays, rows):
    flat = jnp.concatenate([a.reshape(-1) for a in arrays])
    return jnp.pad(flat, (0, rows * LANES - flat.size)).reshape(rows, LANES)


def _unpack(packed, shapes):
    flat = packed.reshape(-1)
    out, off = [], 0
    for s in shapes:
        n = math.prod(s)
        out.append(flat[off : off + n].reshape(s))
        off += n
    return out


def _step(p, x, loss_target, m, v):
    D = x.shape[-1]
    chip = (2 * lax.axis_index("x") + lax.axis_index("y")).astype(jnp.int32)
    place = jnp.stack([chip, lax.axis_index("c").astype(jnp.int32)])

    def gather_buffer(w):
        own = w.reshape(1, 2, w.shape[0] // 2, w.shape[1])
        return lax.dynamic_update_slice(lax.empty((4,) + own.shape[1:], own.dtype), own, (chip, 0, 0, 0))

    def rows_of(n, a):
        return jnp.swapaxes(a[0], 0, 1) if n.endswith(("w_gate", "w_up")) else a[0]

    def rows_back(n, a):
        return (jnp.swapaxes(a, 0, 1) if n.endswith(("w_gate", "w_up")) else a)[None]

    def grad_view(g):
        return g.reshape(4, 2, g.shape[0] // 8, g.shape[1])

    def reduce_sum(got_sibling, views, tag):
        sums = [_add_own_half(v_, g_, place, f"{tag}_add_sibling_{k}") for k, (v_, g_) in enumerate(zip(views, got_sibling))]
        return [s[0] for s in sums], [s[1] for s in sums]

    def reduce_halves(parts, got_chips, tag):
        return [_add_chips(p_, g_, place, f"{tag}_add_chips_{k}") for k, (p_, g_) in enumerate(zip(parts, got_chips))]

    first_names = ("ffn1_w_gate", "ffn1_w_up", "ffn1_w_down")
    later_names = ("w_in", "w_out", "ffn2_w_gate", "ffn2_w_up", "ffn2_w_down")
    bufs = {n: gather_buffer(rows_of(n, p[n]).astype(MXU_DTYPE)) for n in BIG}
    gathered_weight = lambda g_: g_.reshape(-1, g_.shape[-1])
    got = _exchange_call(_Gather([bufs[n] for n in first_names] + [gather_buffer(p["meta_tokens"])]), "gather_first")
    full = {n: gathered_weight(g_) for n, g_ in zip(first_names, got)}
    meta = jnp.transpose(got[-1].reshape(4, N_META, -1), (1, 0, 2)).reshape(N_META, D)

    vec = lambda n: p[n].reshape(1, -1)
    G, N, H = p["ssm_b_re"].shape[1:]
    a_re, a_im, bb_re, bb_im = _discretize(p["ssm_lambda_re"][0], p["ssm_lambda_im"][0], p["ssm_log_dt"][0], p["ssm_b_re"][0], p["ssm_b_im"][0])
    tabs, pows = _scan_tables(a_re, a_im, MIX_TILE // SUBLANES)
    bf = lambda a: a.astype(MXU_DTYPE)
    bbre, bbim = bf(_block_diag(bb_re, False)), bf(_block_diag(bb_im, False))
    ccre, ccim = bf(_block_diag(p["ssm_c_re"][0], False)), bf(_block_diag(p["ssm_c_im"][0], False))
    wgv, wgg = bf(_block_diag(p["ssm_w_glu"][0][:, :, :H], True)), bf(_block_diag(p["ssm_w_glu"][0][:, :, H:], True))
    pw = bf(p["pool_w"][0])

    n_ffn_steps = (x.shape[1] + N_META) // FFN_TILE
    (h1, f1, ga1, si1, s1, n1), got = _ffn_fwd(
        x[0], vec("ffn1_pre_norm"), vec("ffn1_post_norm"), full["ffn1_w_gate"], full["ffn1_w_up"], full["ffn1_w_down"], "ffn1_fwd",
        exchange=_Gather([bufs[n] for n in later_names], mid_step=(3 * n_ffn_steps) // 4), meta=meta,
    )
    full.update({n: gathered_weight(g_) for n, g_ in zip(later_names, got)})
    proj, xr, xim, y, pooled, mixed, h2, n2, cat = _mix_fwd(
        h1, vec("mix_pre_norm"), vec("ssm_out_norm"), vec("pool_out_norm"), vec("mix_post_norm"), vec("ssm_d"), vec("pool_scale"), full["w_in"], full["w_out"],
        bbre, bbim, ccre, ccim, wgv, wgg, pw, tabs, pows, "mix_fwd",
    )
    (dh3, sq, f2, ga2, si2, s2, n3), _ = _ffn_fwd(
        h2, vec("ffn2_pre_norm"), vec("ffn2_post_norm"), full["ffn2_w_gate"], full["ffn2_w_up"], full["ffn2_w_down"], "ffn2_fwd", target=loss_target[0]
    )
    loss = lax.psum(0.5 * jnp.sum(sq) / D, ("x", "y", "c"))

    g, shared = {}, {}
    ffn_names = lambda tag: (tag + "_w_gate", tag + "_w_up", tag + "_w_down")

    da, db, df, dh2, g["ffn2_post_norm"], g["ffn2_pre_norm"] = _ffn_bwd(
        dh3, f2, ga2, si2, h2, vec("ffn2_post_norm"), vec("ffn2_pre_norm"), full["ffn2_w_gate"], full["ffn2_w_up"], full["ffn2_w_down"], "ffn2_bwd"
    )
    views2 = [
        grad_view(_tn_matmul(da, n3, "ffn2_dw_gate")[0]),
        grad_view(_tn_matmul(db, n3, "ffn2_dw_up")[0]),
        grad_view(_tn_matmul(s2, df, "ffn2_dw_down")[0]),
    ]
    (dy, dpooled, dmixed, g["mix_post_norm"], g["ssm_out_norm"], g["pool_out_norm"], g["ssm_d"], g["pool_scale"], dwgv, dwgg, g["pool_w"]), got = _mix_bwd_heads(
        dh2, mixed, y, pooled, proj, vec("ssm_out_norm"), vec("pool_out_norm"), vec("mix_post_norm"), vec("pool_scale"), full["w_out"], wgv, wgg, pw, "mix_bwd_heads",
        exchange=_SiblingScatter(views2),
    )
    parts2, wire2 = reduce_sum(got, views2, "ffn2")
    (dproj, dccre, dccim, dbbre, dbbim, dar, dai), got = _mix_bwd_scan(
        dy, dpooled, xr, xim, proj, vec("ssm_d"), bbre, bbim, ccre, ccim, tabs, pows, "mix_bwd_scan", exchange=_ChipScatter(wire2)
    )
    dh1, g["mix_pre_norm"] = _mix_bwd_in(dproj, h1, dh2, vec("mix_pre_norm"), full["w_in"], "mix_bwd_in")
    halves2 = reduce_halves(parts2, got, "ffn2")
    dw_in, got = _tn_matmul(n2, dproj, "dw_in", exchange=_SiblingShare(halves2))
    shared.update(zip(ffn_names("ffn2"), got))
    dw_out, _ = _tn_matmul(cat, dmixed, "dw_out")
    views_m = [grad_view(dw_in), grad_view(dw_out)]

    (da, db, df, g["ffn1_post_norm"]), got = _ffn_bwd_down(dh1, f1, ga1, si1, vec("ffn1_post_norm"), full["ffn1_w_down"], "ffn1_bwd_down", exchange=_SiblingScatter(views_m))
    parts_m, wire_m = reduce_sum(got, views_m, "mix")
    dw_down, got = _tn_matmul(s1, df, "ffn1_dw_down", exchange=_ChipScatter(wire_m))
    halves_m = reduce_halves(parts_m, got, "mix")
    views_d = [grad_view(dw_down)]
    ex = _Group([_SiblingShare(halves_m), _SiblingScatter(views_d)])
    dw_gate, got = _tn_matmul(da, n1, "ffn1_dw_gate", exchange=ex)
    got_m, got_d = ex.split(got)
    shared.update(zip(("w_in", "w_out"), got_m))
    parts_d, wire_d = reduce_sum(got_d, views_d, "ffn1_down")
    views_g = [grad_view(dw_gate)]
    ex = _Group([_ChipScatter(wire_d), _SiblingScatter(views_g)])
    dw_up, got = _tn_matmul(db, n1, "ffn1_dw_up", exchange=ex)
    got_d, got_g = ex.split(got)
    halves_d = reduce_halves(parts_d, got_d, "ffn1_down")
    parts_g, wire_g = reduce_sum(got_g, views_g, "ffn1_gate")
    views_u = [grad_view(dw_up)]
    ex = _Group([_SiblingShare(halves_d), _ChipScatter(wire_g), _SiblingScatter(views_u)])
    (grad_x, d_meta, g["ffn1_pre_norm"]), got = _ffn_bwd_up(da, db, x[0], meta, dh1, vec("ffn1_pre_norm"), full["ffn1_w_gate"], full["ffn1_w_up"], "ffn1_bwd_up", exchange=ex)
    got_d, got_g, got_u = ex.split(got)
    shared["ffn1_w_down"] = got_d[0]
    halves_g = reduce_halves(parts_g, got_g, "ffn1_gate")
    parts_u, wire_u = reduce_sum(got_u, views_u, "ffn1_up")
    grad_x = grad_x[None]

    g["ssm_c_re"] = _block_diag_extract(dccre, N, H, False)
    g["ssm_c_im"] = _block_diag_extract(dccim, N, H, False)
    g["ssm_w_glu"] = jnp.concatenate([_block_diag_extract(dwgv, H, H, True), _block_diag_extract(dwgg, H, H, True)], axis=-1)
    d_a_re, d_a_im = jnp.sum(dar, axis=1).reshape(G, N), jnp.sum(dai, axis=1).reshape(G, N)
    _, pull = jax.vjp(_discretize, p["ssm_lambda_re"][0], p["ssm_lambda_im"][0], p["ssm_log_dt"][0], p["ssm_b_re"][0], p["ssm_b_im"][0])
    g["ssm_lambda_re"], g["ssm_lambda_im"], g["ssm_log_dt"], g["ssm_b_re"], g["ssm_b_im"] = pull(
        (d_a_re, d_a_im, _block_diag_extract(dbbre, H, N, False), _block_diag_extract(dbbim, H, N, False))
    )

    small_shapes = [p[n].shape for n in SMALL] + [(N_META, D)]
    small_size = sum(math.prod(s) for s in small_shapes)
    rows = -(-small_size // (LANES * PACK_ROWS)) * PACK_ROWS
    views_s = [_pack([g[n] for n in SMALL] + [d_meta], rows).reshape(4, 2, rows // 8, LANES)]
    parts_s, wire_s = reduce_sum(_exchange_call(_SiblingScatter(views_s), "small_reduce_sibling"), views_s, "small")
    grads, delta, new_m, new_v = {}, {}, {}, {}

    def adam_operands(n):
        return rows_of(n, p[n]), shared[n].reshape(-1, shared[n].shape[-1]), rows_of(n, m[n]), rows_of(n, v[n])

    def adam_results(n, g_rows, outs):
        grads[n], delta[n], new_m[n], new_v[n] = (rows_back(n, a) for a in (g_rows, *outs))

    ready = ("ffn2_w_gate", "ffn2_w_up", "ffn2_w_down", "w_in", "w_out", "ffn1_w_down")
    ex = _Group([_ChipScatter(wire_u + wire_s), _SiblingShare(halves_g)])
    items = [adam_operands(n) for n in ready]
    outs, got = _adamw_group(items, "adamw_ready", exchange=ex)
    for n, item, out in zip(ready, items, outs):
        adam_results(n, item[1], out)
    got_c, got_g = ex.split(got)
    shared["ffn1_w_gate"] = got_g[0]
    got = _exchange_call(_SiblingShare(reduce_halves(parts_u + parts_s, got_c, "tail")), "tail_reduce_share")
    shared["ffn1_w_up"] = got[0]
    small_buf = lax.dynamic_update_slice(lax.empty((4,) + got[1].shape, F32), got[1][None], (chip, 0, 0, 0))
    small_all = _exchange_call(_Gather([small_buf]), "gather_small")[0].reshape(rows, LANES)
    grads.update(zip(SMALL + ("meta_full",), _unpack(small_all, small_shapes)))
    grads["meta_tokens"] = lax.dynamic_slice_in_dim(grads.pop("meta_full"), chip * (D // 4), D // 4, axis=1)
    for n in ("ffn1_w_gate", "ffn1_w_up"):
        item = adam_operands(n)
        adam_results(n, item[1], _adamw(*item, "adamw_" + n))
    delta["meta_tokens"], new_m["meta_tokens"], new_v["meta_tokens"] = _adamw(p["meta_tokens"], grads["meta_tokens"], m["meta_tokens"], v["meta_tokens"], "adamw_meta_tokens")

    def as_2d(n, a):
        a = a.reshape(p[n].shape)[0]
        if n in ("ssm_b_re", "ssm_b_im"):
            a = jnp.swapaxes(a, 1, 2)
        return a.reshape(-1, a.shape[-1])

    def from_2d(n, a):
        if n in ("ssm_b_re", "ssm_b_im"):
            g_, n_, h_ = p[n].shape[1:]
            return jnp.swapaxes(a.reshape(g_, h_, n_), 1, 2)[None]
        return a.reshape(p[n].shape)

    outs = _adamw_many(*[[as_2d(n, t[n]) for n in SMALL] for t in (p, grads, m, v)], "adamw_small")
    for out, arrays in zip((delta, new_m, new_v), outs):
        out.update({n: from_2d(n, a) for n, a in zip(SMALL, arrays)})

    return (loss, grad_x, *[grads[n] for n in ORDER], *[delta[n] for n in ORDER], *[new_m[n] for n in ORDER], *[new_v[n] for n in ORDER])


def kernel(x, meta_tokens, ffn1_pre_norm, ffn1_post_norm, ffn1_w_gate, ffn1_w_up, ffn1_w_down, mix_pre_norm, mix_post_norm, w_in, ssm_lambda_re, ssm_lambda_im, ssm_log_dt, ssm_b_re, ssm_b_im, ssm_c_re, ssm_c_im, ssm_d, ssm_w_glu, pool_w, pool_scale, ssm_out_norm, pool_out_norm, w_out, ffn2_pre_norm, ffn2_post_norm, ffn2_w_gate, ffn2_w_up, ffn2_w_down, loss_target, m_meta_tokens, m_ffn1_pre_norm, m_ffn1_post_norm, m_ffn1_w_gate, m_ffn1_w_up, m_ffn1_w_down, m_mix_pre_norm, m_mix_post_norm, m_w_in, m_ssm_lambda_re, m_ssm_lambda_im, m_ssm_log_dt, m_ssm_b_re, m_ssm_b_im, m_ssm_c_re, m_ssm_c_im, m_ssm_d, m_ssm_w_glu, m_pool_w, m_pool_scale, m_ssm_out_norm, m_pool_out_norm, m_w_out, m_ffn2_pre_norm, m_ffn2_post_norm, m_ffn2_w_gate, m_ffn2_w_up, m_ffn2_w_down, v_meta_tokens, v_ffn1_pre_norm, v_ffn1_post_norm, v_ffn1_w_gate, v_ffn1_w_up, v_ffn1_w_down, v_mix_pre_norm, v_mix_post_norm, v_w_in, v_ssm_lambda_re, v_ssm_lambda_im, v_ssm_log_dt, v_ssm_b_re, v_ssm_b_im, v_ssm_c_re, v_ssm_c_im, v_ssm_d, v_ssm_w_glu, v_pool_w, v_pool_scale, v_ssm_out_norm, v_pool_out_norm, v_w_out, v_ffn2_pre_norm, v_ffn2_post_norm, v_ffn2_w_gate, v_ffn2_w_up, v_ffn2_w_down):
    args = locals()
    p = {n: args[n] for n in ORDER}
    m = {n: args["m_" + n] for n in ORDER}
    v = {n: args["v_" + n] for n in ORDER}
    return _step(p, x, loss_target, m, v)
```

```python
import math

import jax
import jax.numpy as jnp
from jax import lax
from jax.experimental import pallas as pl
from jax.experimental.pallas import tpu as pltpu

F32 = jnp.float32
MXU_DTYPE = jnp.bfloat16
WIRE_DTYPE = jnp.bfloat16

RMS_EPS = 1e-6
N_META = 16
POOL_WINDOWS = (2, 4, 8, 16)
POOL_HALO = 16
ADAM_LR, ADAM_B1, ADAM_B2, ADAM_EPS, ADAM_WD, ADAM_STEP = 0.001, 0.9, 0.999, 1e-08, 0.01, 10

LANES = 128
SUBLANES = 8
VMEM_LIMIT = 60 * 1024 * 1024
FFN_TILE = 432
FFN_CHUNK = 1024
TN_TILE = 1024
MIX_TILE = 216
MIX_SUBTILES = 2
SLAB_GROUP = 8
MESH = pl.DeviceIdType.MESH
ANY = pl.BlockSpec(memory_space=pl.ANY)


def _mm(a, b):
    return jnp.dot(a.astype(MXU_DTYPE), b.astype(MXU_DTYPE), preferred_element_type=F32)


def _mm_nt(a, b):
    return lax.dot_general(a.astype(MXU_DTYPE), b.astype(MXU_DTYPE), (((1,), (1,)), ((), ())), preferred_element_type=F32)


def _mm_tn(a, b):
    return lax.dot_general(a.astype(MXU_DTYPE), b.astype(MXU_DTYPE), (((0,), (0,)), ((), ())), preferred_element_type=F32)


def _rms_stat(x):
    return lax.rsqrt(jnp.mean(x * x, axis=-1, keepdims=True) + RMS_EPS)


def _rms_bwd(x, g, dy):
    r = _rms_stat(x)
    xh = x * r
    dg = jnp.sum(dy * xh, axis=0, keepdims=True)
    dxh = dy * g
    dx = r * (dxh - xh * jnp.mean(dxh * xh, axis=-1, keepdims=True))
    return dx, dg


def _sigmoid(x):
    return 1.0 / (1.0 + jnp.exp(-x))


GELU_C = math.sqrt(2.0 / math.pi)
GELU_K = 0.044715


def _gelu(y):
    return 0.5 * y * (1.0 + jnp.tanh(GELU_C * (y + GELU_K * y * y * y)))


def _gelu_grad(y):
    th = jnp.tanh(GELU_C * (y + GELU_K * y * y * y))
    return 0.5 * (1.0 + th) + 0.5 * y * (1.0 - th * th) * GELU_C * (1.0 + 3.0 * GELU_K * y * y)


def _row_spec(tile, cols, rev_n=None):
    if rev_n is None:
        return pl.BlockSpec((tile, cols), lambda i: (i, 0))
    return pl.BlockSpec((tile, cols), lambda i: (rev_n - 1 - i, 0))


def _full_spec(shape, single=False):
    zeros = (0,) * len(shape)
    if single:
        return pl.BlockSpec(shape, lambda *_: zeros, pipeline_mode=pl.Buffered(1))
    return pl.BlockSpec(shape, lambda *_: zeros)


def _acc(ref, val, first):
    @pl.when(first)
    def _():
        ref[...] = val

    @pl.when(jnp.logical_not(first))
    def _():
        ref[...] += val


def _place():
    x, y, c = lax.axis_index("x"), lax.axis_index("y"), lax.axis_index("c")
    others = [(1 - x, y), (x, 1 - y), (1 - x, 1 - y)]
    return x, y, c, others


class _Exchange:
    mid_step = None

    def __init__(self, ins, out_shapes, aliases, n_sems):
        self.ins, self.out_shapes, self.aliases, self.n_sems = list(ins), list(out_shapes), dict(aliases), n_sems

    def mid(self, ins, outs, send_sems, recv_sems):
        pass


class _SiblingScatter(_Exchange):
    def __init__(self, views):
        super().__init__(views, [jax.ShapeDtypeStruct((4,) + v.shape[2:], v.dtype) for v in views], {}, 4 * len(views))

    def _copies(self, ins, outs, send_sems, recv_sems):
        x, y, c, _ = _place()
        return [
            pltpu.make_async_remote_copy(src_ref=ins[a].at[k, 1 - c], dst_ref=outs[a].at[k], send_sem=send_sems.at[4 * a + k], recv_sem=recv_sems.at[4 * a + k], device_id=(x, y, 1 - c), device_id_type=MESH)
            for a in range(len(ins))
            for k in range(4)
        ]

    def start(self, *refs):
        for cp in self._copies(*refs):
            cp.start()

    def finish(self, *refs):
        cps = self._copies(*refs)
        for cp in cps:
            cp.wait_recv()
        for cp in cps:
            cp.wait_send()


class _ChipScatter(_Exchange):
    def __init__(self, parts):
        super().__init__(parts, [jax.ShapeDtypeStruct((3,) + p.shape[1:], p.dtype) for p in parts], {}, 3 * len(parts))

    def _copies(self, ins, outs, send_sems, recv_sems):
        x, y, c, others = _place()
        return [
            pltpu.make_async_remote_copy(src_ref=ins[a].at[2 * chip[0] + chip[1]], dst_ref=outs[a].at[j], send_sem=send_sems.at[3 * a + j], recv_sem=recv_sems.at[3 * a + j], device_id=(*chip, c), device_id_type=MESH)
            for a in range(len(ins))
            for j, chip in enumerate(others)
        ]

    start = _SiblingScatter.start
    finish = _SiblingScatter.finish


class _SiblingShare(_Exchange):
    def __init__(self, bufs):
        super().__init__(bufs, [jax.ShapeDtypeStruct(b.shape, b.dtype) for b in bufs], {a: a for a in range(len(bufs))}, len(bufs))

    def _copy(self, outs, send_sems, recv_sems, a, half):
        x, y, c, _ = _place()
        mine = outs[a].at[c if half == "mine" else 1 - c]
        return pltpu.make_async_remote_copy(src_ref=mine, dst_ref=mine, send_sem=send_sems.at[a], recv_sem=recv_sems.at[a], device_id=(x, y, 1 - c), device_id_type=MESH)

    def start(self, ins, outs, send_sems, recv_sems):
        for a in range(len(outs)):
            self._copy(outs, send_sems, recv_sems, a, "mine").start()

    def finish(self, ins, outs, send_sems, recv_sems):
        for a in range(len(outs)):
            self._copy(outs, send_sems, recv_sems, a, "theirs").wait_recv()
        for a in range(len(outs)):
            self._copy(outs, send_sems, recv_sems, a, "mine").wait_send()


class _Gather(_Exchange):
    def __init__(self, bufs, mid_step=None):
        super().__init__(bufs, [jax.ShapeDtypeStruct(b.shape, b.dtype) for b in bufs], {a: a for a in range(len(bufs))}, 6 * len(bufs))
        self.mid_step = mid_step

    def _copy(self, outs, send_sems, recv_sems, a, j, chip, half, to):
        blk = outs[a].at[2 * chip[0] + chip[1], half]
        return pltpu.make_async_remote_copy(src_ref=blk, dst_ref=blk, send_sem=send_sems.at[6 * a + j], recv_sem=recv_sems.at[6 * a + j], device_id=to, device_id_type=MESH)

    def start(self, ins, outs, send_sems, recv_sems):
        x, y, c, others = _place()
        for a in range(len(outs)):
            for j, chip in enumerate(others):
                self._copy(outs, send_sems, recv_sems, a, j, (x, y), c, (*chip, c)).start()

    def mid(self, ins, outs, send_sems, recv_sems):
        x, y, c, others = _place()
        for a in range(len(outs)):
            for j, chip in enumerate(others):
                self._copy(outs, send_sems, recv_sems, a, j, chip, c, (x, y, c)).wait_recv()
                self._copy(outs, send_sems, recv_sems, a, 3 + j, chip, c, (x, y, 1 - c)).start()

    def finish(self, ins, outs, send_sems, recv_sems):
        x, y, c, others = _place()
        for a in range(len(outs)):
            for j, chip in enumerate(others):
                self._copy(outs, send_sems, recv_sems, a, 3 + j, chip, 1 - c, (x, y, c)).wait_recv()
        for a in range(len(outs)):
            for j, chip in enumerate(others):
                self._copy(outs, send_sems, recv_sems, a, j, (x, y), c, (*chip, c)).wait_send()
                self._copy(outs, send_sems, recv_sems, a, 3 + j, chip, c, (x, y, 1 - c)).wait_send()


class _SemSlice:
    def __init__(self, sems, off):
        self.sems, self.off = sems, off

    @property
    def at(self):
        return self

    def __getitem__(self, i):
        return self.sems.at[self.off + i]


class _Group(_Exchange):
    def __init__(self, exchanges):
        ins, outs, aliases, n_sems, self.spans = [], [], {}, 0, []
        for ex in exchanges:
            self.spans.append((len(ins), len(outs), n_sems))
            aliases.update({len(ins) + i: len(outs) + o for i, o in ex.aliases.items()})
            ins, outs, n_sems = ins + ex.ins, outs + ex.out_shapes, n_sems + ex.n_sems
        super().__init__(ins, outs, aliases, n_sems)
        self.exchanges = exchanges
        mids = {ex.mid_step for ex in exchanges if ex.mid_step is not None}
        self.mid_step = mids.pop() if mids else None

    def _each(self, method, ins, outs, send_sems, recv_sems):
        for ex, (i0, o0, s0) in zip(self.exchanges, self.spans):
            getattr(ex, method)(ins[i0 : i0 + len(ex.ins)], outs[o0 : o0 + len(ex.out_shapes)], _SemSlice(send_sems, s0), _SemSlice(recv_sems, s0))

    def start(self, *refs):
        self._each("start", *refs)

    def mid(self, *refs):
        self._each("mid", *refs)

    def finish(self, *refs):
        self._each("finish", *refs)

    def split(self, outs):
        return [outs[o0 : o0 + len(ex.out_shapes)] for ex, (_, o0, _) in zip(self.exchanges, self.spans)]


def _exchange_call(ex, name):
    n, m = len(ex.ins), len(ex.out_shapes)

    def body(*refs):
        parts = (refs[:n], refs[n : n + m], refs[n + m], refs[n + m + 1])
        ex.start(*parts)
        ex.mid(*parts)
        ex.finish(*parts)

    return pl.pallas_call(
        body,
        name=name,
        out_shape=ex.out_shapes,
        in_specs=[ANY] * n,
        out_specs=[ANY] * m,
        scratch_shapes=[pltpu.SemaphoreType.DMA((ex.n_sems,)), pltpu.SemaphoreType.DMA((ex.n_sems,))],
        input_output_aliases=ex.aliases,
    )(*ex.ins)


def _pallas(body, *, name, grid, in_specs, out_specs, out_shape, operands, scratch_shapes=(), exchange=None):
    params = pltpu.CompilerParams(dimension_semantics=("arbitrary",) * len(grid), vmem_limit_bytes=VMEM_LIMIT)
    if exchange is None:
        outs = pl.pallas_call(body, name=name, grid=grid, in_specs=in_specs, out_specs=out_specs, out_shape=out_shape, scratch_shapes=list(scratch_shapes), compiler_params=params)(*operands)
        return outs, []
    ex = exchange
    n_in, n_out, n_scr = len(in_specs), len(out_specs), len(scratch_shapes)
    x_in, x_out = len(ex.ins), len(ex.out_shapes)

    def hosted(*refs):
        ins, x_ins = refs[:n_in], refs[n_in : n_in + x_in]
        outs, x_outs = refs[n_in + x_in : n_in + x_in + n_out], refs[n_in + x_in + n_out : n_in + x_in + n_out + x_out]
        rest = refs[n_in + x_in + n_out + x_out :]
        parts = (x_ins, x_outs, rest[n_scr], rest[n_scr + 1])
        ids = [pl.program_id(d) for d in range(len(grid))]
        first = _all([i == 0 for i in ids])
        last = _all([i == g - 1 for i, g in zip(ids, grid)])

        @pl.when(first)
        def _():
            ex.start(*parts)

        body(*ins, *outs, *rest[:n_scr])

        if ex.mid_step is not None:

            @pl.when(ids[0] == ex.mid_step)
            def _():
                ex.mid(*parts)

        @pl.when(last)
        def _():
            ex.finish(*parts)

    outs = pl.pallas_call(
        hosted,
        name=name,
        grid=grid,
        in_specs=list(in_specs) + [ANY] * x_in,
        out_specs=list(out_specs) + [ANY] * x_out,
        out_shape=list(out_shape) + ex.out_shapes,
        scratch_shapes=list(scratch_shapes) + [pltpu.SemaphoreType.DMA((ex.n_sems,)), pltpu.SemaphoreType.DMA((ex.n_sems,))],
        input_output_aliases={n_in + i: n_out + o for i, o in ex.aliases.items()},
        compiler_params=params,
    )(*operands, *ex.ins)
    return outs[:n_out], outs[n_out:]


def _all(conds):
    out = conds[0]
    for c in conds[1:]:
        out = jnp.logical_and(out, c)
    return out


def _row_tile(rows):
    if rows <= 512:
        return rows
    for t in (512, 352, 256, 176, 128, 112, 64, 32, 16, 8):
        if rows % t == 0:
            return t
    return rows


def _add_own_half(view, got, place, name):
    _, _, r, c = view.shape
    tr = _row_tile(r)

    def body(place_ref, v_ref, g_ref, o_ref, w_ref):
        s = v_ref[...] + g_ref[...]
        w_ref[...] = s.astype(w_ref.dtype)

        @pl.when(pl.program_id(1) == place_ref[0])
        def _():
            o_ref[...] = s

    blk = pl.BlockSpec((None, tr, c), lambda i, k, pr: (k, i, 0))
    return pl.pallas_call(
        body,
        name=name,
        out_shape=[jax.ShapeDtypeStruct((r, c), F32), jax.ShapeDtypeStruct((4, r, c), WIRE_DTYPE)],
        grid_spec=pltpu.PrefetchScalarGridSpec(
            num_scalar_prefetch=1,
            grid=(r // tr, 4),
            in_specs=[pl.BlockSpec((None, None, tr, c), lambda i, k, pr: (k, pr[1], i, 0)), blk],
            out_specs=[pl.BlockSpec((tr, c), lambda i, k, pr: (i, 0)), blk],
        ),
        compiler_params=pltpu.CompilerParams(dimension_semantics=("arbitrary", "arbitrary"), vmem_limit_bytes=VMEM_LIMIT),
    )(place, view, got)


def _add_chips(part, got, place, name):
    r, c = part.shape
    tr = _row_tile(r)

    def body(place_ref, p_ref, g_ref, o_ref):
        o_ref[...] = ((p_ref[...] + g_ref[0].astype(F32)) + g_ref[1].astype(F32)) + g_ref[2].astype(F32)

    return pl.pallas_call(
        body,
        name=name,
        out_shape=jax.ShapeDtypeStruct((2, r, c), F32),
        grid_spec=pltpu.PrefetchScalarGridSpec(
            num_scalar_prefetch=1,
            grid=(r // tr,),
            in_specs=[pl.BlockSpec((tr, c), lambda i, pr: (i, 0)), pl.BlockSpec((3, tr, c), lambda i, pr: (0, i, 0))],
            out_specs=pl.BlockSpec((None, tr, c), lambda i, pr: (pr[1], i, 0)),
        ),
        compiler_params=pltpu.CompilerParams(dimension_semantics=("arbitrary",), vmem_limit_bytes=VMEM_LIMIT),
    )(place, part, got)


def _adamw_update(w_ref, g_ref, m_ref, v_ref, d_ref, nm_ref, nv_ref):
    g = g_ref[...]
    nm = ADAM_B1 * m_ref[...] + (1.0 - ADAM_B1) * g
    nv = ADAM_B2 * v_ref[...] + (1.0 - ADAM_B2) * (g * g)
    m_hat = nm / (1.0 - ADAM_B1**ADAM_STEP)
    v_hat = nv / (1.0 - ADAM_B2**ADAM_STEP)
    d_ref[...] = -ADAM_LR * (m_hat / (jnp.sqrt(v_hat) + ADAM_EPS) + ADAM_WD * w_ref[...])
    nm_ref[...] = nm
    nv_ref[...] = nv


def _adamw(w, g, m, v, name):
    r, c = w.shape
    tr = _row_tile(r)
    spec = pl.BlockSpec((tr, c), lambda i: (i, 0))
    outs, _ = _pallas(_adamw_update, name=name, grid=(r // tr,), in_specs=[spec] * 4, out_specs=[spec] * 3, out_shape=[jax.ShapeDtypeStruct((r, c), F32)] * 3, operands=(w, g, m, v))
    return outs


def _adamw_many(ws, gs, ms, vs, name):
    n = len(ws)

    def body(*refs):
        for k in range(n):
            _adamw_update(*(refs[j * n + k] for j in range(7)))

    outs = pl.pallas_call(
        body,
        name=name,
        out_shape=[jax.ShapeDtypeStruct(w.shape, F32) for w in ws] * 3,
        in_specs=[pl.BlockSpec(memory_space=pltpu.VMEM)] * (4 * n),
        out_specs=[pl.BlockSpec(memory_space=pltpu.VMEM)] * (3 * n),
    )(*ws, *gs, *ms, *vs)
    return outs[:n], outs[n : 2 * n], outs[2 * n :]


def _load_weights(pairs, sems):
    @pl.when(pl.program_id(0) == 0)
    def _():
        cps = [pltpu.make_async_copy(src, dst, sems.at[k]) for k, (src, dst) in enumerate(pairs)]
        for cp in cps:
            cp.start()
        for cp in cps:
            cp.wait()


def _ffn_chunks(F):
    bounds = list(range(0, F, FFN_CHUNK)) + [F]
    return list(zip(bounds[:-1], bounds[1:]))


def _shifted_specs(tm, cols):
    per = tm // N_META
    return [_row_spec(tm, cols), pl.BlockSpec((N_META, cols), lambda i: (jnp.maximum(i * per - 1, 0), 0))]


def _shifted_tile(cur_ref, before_ref, tm):
    return jnp.concatenate([before_ref[...], cur_ref[0 : tm - N_META, :]], axis=0)


def _tokens_tile(cur_ref, before_ref, meta_ref, tm, tile_0):
    first = jnp.where(tile_0, meta_ref[...], before_ref[...])
    return jnp.concatenate([first, cur_ref[0 : tm - N_META, :]], axis=0)


def _ffn_fwd(h, g_pre, g_post, wg, wu, wd, name, exchange=None, meta=None, target=None):
    D = h.shape[1]
    T = h.shape[0] + (0 if meta is None else N_META)
    F = wg.shape[0]
    tm = FFN_TILE
    n_src = 1 if meta is None else 3
    n_tgt = 0 if target is None else 2
    n_head = 1 if target is None else 2

    def body(*refs):
        src, refs = refs[:n_src], refs[n_src:]
        tgt, refs = refs[:n_tgt], refs[n_tgt:]
        (gpre_ref, gpost_ref, wg_hbm, wu_hbm, wd_hbm), refs = refs[:5], refs[5:]
        head, (f_ref, ga_ref, si_ref, s_ref, n_ref, wg_v, wu_v, wd_v, sems) = refs[:n_head], refs[n_head:]
        i = pl.program_id(0)
        _load_weights([(wg_hbm, wg_v), (wu_hbm, wu_v), (wd_hbm, wd_v)], sems)
        hh = src[0][...] if meta is None else _tokens_tile(src[0], src[1], src[2], tm, i == 0)
        n = (hh * _rms_stat(hh) * gpre_ref[...]).astype(MXU_DTYPE)
        n_ref[...] = n.astype(n_ref.dtype)
        f = jnp.zeros((tm, D), F32)
        for lo, hi in _ffn_chunks(F):
            a = _mm_nt(n, wg_v[lo:hi, :])
            b = _mm_nt(n, wu_v[lo:hi, :])
            sg = _sigmoid(a)
            si = a * sg
            s = (si * b).astype(MXU_DTYPE)
            ga_ref[:, lo:hi] = (b * (sg * (1.0 + a * (1.0 - sg)))).astype(ga_ref.dtype)
            si_ref[:, lo:hi] = si.astype(si_ref.dtype)
            s_ref[:, lo:hi] = s.astype(s_ref.dtype)
            f = f + _mm(s, wd_v[lo:hi, :])
        f_ref[...] = f
        out = hh + 0.5 * (f * _rms_stat(f) * gpost_ref[...])
        if target is None:
            head[0][...] = out
        else:
            rows = i * tm + lax.broadcasted_iota(jnp.int32, (tm, D), 0)
            err = jnp.where(rows >= N_META, out - _shifted_tile(tgt[0], tgt[1], tm), 0.0)
            head[0][...] = err * (1.0 / D)
            _acc(head[1], jnp.sum(err * err, axis=0, keepdims=True), i == 0)

    tok = jax.ShapeDtypeStruct((T, D), F32)
    act = jax.ShapeDtypeStruct((T, F), MXU_DTYPE)
    src_specs = [_row_spec(tm, D)] if meta is None else _shifted_specs(tm, D) + [_full_spec((N_META, D))]
    src_ops = (h,) if meta is None else (h, h, meta)
    tgt_specs, tgt_ops = ([], ()) if target is None else (_shifted_specs(tm, D), (target, target))
    head_shapes = [tok] if target is None else [tok, jax.ShapeDtypeStruct((1, D), F32)]
    head_specs = [_row_spec(tm, D)] if target is None else [_row_spec(tm, D), _full_spec((1, D))]
    return _pallas(
        body,
        name=name,
        grid=(T // tm,),
        out_shape=head_shapes + [tok, act, act, act, jax.ShapeDtypeStruct((T, D), MXU_DTYPE)],
        in_specs=src_specs + tgt_specs + [_full_spec((1, D)), _full_spec((1, D)), ANY, ANY, ANY],
        out_specs=head_specs + [_row_spec(tm, D), _row_spec(tm, F), _row_spec(tm, F), _row_spec(tm, F), _row_spec(tm, D)],
        scratch_shapes=[pltpu.VMEM(wg.shape, wg.dtype), pltpu.VMEM(wu.shape, wu.dtype), pltpu.VMEM(wd.shape, wd.dtype), pltpu.SemaphoreType.DMA((3,))],
        operands=(*src_ops, *tgt_ops, g_pre, g_post, wg, wu, wd),
        exchange=exchange,
    )


def _ffn_bwd_down(dh, f, ga, si, g_post, wd, name, exchange=None):
    T, D = dh.shape
    F = wd.shape[0]
    tm = FFN_TILE

    def body(dh_ref, f_ref, ga_ref, si_ref, gpost_ref, wd_hbm, da_ref, db_ref, df_ref, dg_ref, wd_v, sems):
        _load_weights([(wd_hbm, wd_v)], sems)
        df, dg = _rms_bwd(f_ref[...], gpost_ref[...], 0.5 * dh_ref[...])
        _acc(dg_ref, dg, pl.program_id(0) == 0)
        dfb = df.astype(MXU_DTYPE)
        df_ref[...] = dfb.astype(df_ref.dtype)
        for lo, hi in _ffn_chunks(F):
            ds = _mm_nt(dfb, wd_v[lo:hi, :])
            da_ref[:, lo:hi] = (ds * ga_ref[:, lo:hi].astype(F32)).astype(da_ref.dtype)
            db_ref[:, lo:hi] = (ds * si_ref[:, lo:hi].astype(F32)).astype(db_ref.dtype)

    act = jax.ShapeDtypeStruct((T, F), MXU_DTYPE)
    return _pallas(
        body,
        name=name,
        grid=(T // tm,),
        out_shape=[act, act, jax.ShapeDtypeStruct((T, D), MXU_DTYPE), jax.ShapeDtypeStruct((1, D), F32)],
        in_specs=[_row_spec(tm, D), _row_spec(tm, D), _row_spec(tm, F), _row_spec(tm, F), _full_spec((1, D)), ANY],
        out_specs=[_row_spec(tm, F), _row_spec(tm, F), _row_spec(tm, D), _full_spec((1, D))],
        scratch_shapes=[pltpu.VMEM(wd.shape, wd.dtype), pltpu.SemaphoreType.DMA((1,))],
        operands=(dh, f, ga, si, g_post, wd),
        exchange=exchange,
    )


def _ffn_bwd_up(da, db, x, meta, dh, g_pre, wg, wu, name, exchange=None):
    D = x.shape[1]
    T = x.shape[0] + N_META
    F = wg.shape[0]
    tm = FFN_TILE
    nt = T // tm
    per = tm // N_META
    tile = lambda i: jnp.minimum(i, nt - 1)

    def body(da_ref, db_ref, x_ref, xb_ref, meta_ref, dh_ref, gpre_ref, wg_hbm, wu_hbm, dx_ref, dmeta_ref, dg_ref, wg_v, wu_v, sems, held):
        i = pl.program_id(0)
        _load_weights([(wg_hbm, wg_v), (wu_hbm, wu_v)], sems)

        @pl.when(i < nt)
        def _():
            dn = jnp.zeros((tm, D), F32)
            for lo, hi in _ffn_chunks(F):
                dn = dn + _mm(da_ref[:, lo:hi], wg_v[lo:hi, :]) + _mm(db_ref[:, lo:hi], wu_v[lo:hi, :])
            dx, dg = _rms_bwd(_tokens_tile(x_ref, xb_ref, meta_ref, tm, i == 0), gpre_ref[...], dn)
            _acc(dg_ref, dg, i == 0)
            dh_in = dh_ref[...] + dx

            @pl.when(i == 0)
            def _():
                dmeta_ref[...] = dh_in[0:N_META, :]

            @pl.when(i > 0)
            def _():
                dx_ref[...] = jnp.concatenate([held[...], dh_in[0:N_META, :]], axis=0)

            held[...] = dh_in[N_META:, :]

        @pl.when(i == nt)
        def _():
            dx_ref[0 : tm - N_META, :] = held[...]

    rows = lambda cols: pl.BlockSpec((tm, cols), lambda i: (tile(i), 0))
    return _pallas(
        body,
        name=name,
        grid=(nt + 1,),
        out_shape=[jax.ShapeDtypeStruct((T - N_META, D), F32), jax.ShapeDtypeStruct((N_META, D), F32), jax.ShapeDtypeStruct((1, D), F32)],
        in_specs=[rows(F), rows(F), rows(D), pl.BlockSpec((N_META, D), lambda i: (jnp.maximum(tile(i) * per - 1, 0), 0)), _full_spec((N_META, D)), rows(D), _full_spec((1, D)), ANY, ANY],
        out_specs=[pl.BlockSpec((tm, D), lambda i: (jnp.maximum(i - 1, 0), 0)), _full_spec((N_META, D)), _full_spec((1, D))],
        scratch_shapes=[pltpu.VMEM(wg.shape, wg.dtype), pltpu.VMEM(wu.shape, wu.dtype), pltpu.SemaphoreType.DMA((2,)), pltpu.VMEM((tm - N_META, D), F32)],
        operands=(da, db, x, x, meta, dh, g_pre, wg, wu),
        exchange=exchange,
    )


def _ffn_bwd(dh, f, ga, si, h, g_post, g_pre, wg, wu, wd, name):
    T, D = dh.shape
    F = wd.shape[0]
    tm = FFN_TILE

    def body(dh_ref, f_ref, ga_ref, si_ref, h_ref, gpost_ref, gpre_ref, wg_hbm, wu_hbm, wd_hbm, da_ref, db_ref, df_ref, dhin_ref, dgpost_ref, dgpre_ref, wg_v, wu_v, wd_v, sems):
        first = pl.program_id(0) == 0
        _load_weights([(wg_hbm, wg_v), (wu_hbm, wu_v), (wd_hbm, wd_v)], sems)
        dh = dh_ref[...]
        df, dg = _rms_bwd(f_ref[...], gpost_ref[...], 0.5 * dh)
        _acc(dgpost_ref, dg, first)
        dfb = df.astype(MXU_DTYPE)
        df_ref[...] = dfb.astype(df_ref.dtype)
        dn = jnp.zeros((tm, D), F32)
        for lo, hi in _ffn_chunks(F):
            ds = _mm_nt(dfb, wd_v[lo:hi, :])
            da = (ds * ga_ref[:, lo:hi].astype(F32)).astype(MXU_DTYPE)
            db = (ds * si_ref[:, lo:hi].astype(F32)).astype(MXU_DTYPE)
            da_ref[:, lo:hi] = da.astype(da_ref.dtype)
            db_ref[:, lo:hi] = db.astype(db_ref.dtype)
            dn = dn + _mm(da, wg_v[lo:hi, :]) + _mm(db, wu_v[lo:hi, :])
        dx, dg = _rms_bwd(h_ref[...], gpre_ref[...], dn)
        _acc(dgpre_ref, dg, first)
        dhin_ref[...] = dh + dx

    act = jax.ShapeDtypeStruct((T, F), MXU_DTYPE)
    vec = jax.ShapeDtypeStruct((1, D), F32)
    outs, _ = _pallas(
        body,
        name=name,
        grid=(T // tm,),
        out_shape=[act, act, jax.ShapeDtypeStruct((T, D), MXU_DTYPE), jax.ShapeDtypeStruct((T, D), F32), vec, vec],
        in_specs=[_row_spec(tm, D), _row_spec(tm, D), _row_spec(tm, F), _row_spec(tm, F), _row_spec(tm, D), _full_spec((1, D)), _full_spec((1, D)), ANY, ANY, ANY],
        out_specs=[_row_spec(tm, F), _row_spec(tm, F), _row_spec(tm, D), _row_spec(tm, D), _full_spec((1, D)), _full_spec((1, D))],
        scratch_shapes=[pltpu.VMEM(wg.shape, wg.dtype), pltpu.VMEM(wu.shape, wu.dtype), pltpu.VMEM(wd.shape, wd.dtype), pltpu.SemaphoreType.DMA((3,))],
        operands=(dh, f, ga, si, h, g_post, g_pre, wg, wu, wd),
    )
    return outs


def _token_tile(T):
    for t in (912, 864, 432):
        if T % t == 0:
            return t
    raise ValueError(f"no token tile for {T} rows")


def _tn_matmul(xm, ym, name, exchange=None):
    T, M = xm.shape
    N = ym.shape[1]
    if (T - N_META) % TN_TILE:
        tk = _token_tile(T)

        def body(x_ref, y_ref, o_ref):
            _acc(o_ref, _mm_tn(x_ref[...], y_ref[...]), pl.program_id(0) == 0)

        grid, operands = (T // tk,), (xm, ym)
        in_specs = [pl.BlockSpec((tk, M), lambda k: (k, 0)), pl.BlockSpec((tk, N), lambda k: (k, 0))]
    else:
        tk = TN_TILE

        def body(x_ref, y_ref, xh_ref, yh_ref, o_ref):
            prod = _mm_tn(x_ref[...], y_ref[...])

            @pl.when(pl.program_id(0) == 0)
            def _():
                o_ref[...] = prod + _mm_tn(xh_ref[...], yh_ref[...])

            @pl.when(pl.program_id(0) > 0)
            def _():
                o_ref[...] += prod

        grid, operands = ((T - N_META) // tk,), (xm, ym, xm, ym)
        start = lambda k: (pl.multiple_of(N_META + k * tk, N_META), 0)
        in_specs = [pl.BlockSpec((pl.Element(tk), pl.Element(M)), start), pl.BlockSpec((pl.Element(tk), pl.Element(N)), start),
                    pl.BlockSpec((N_META, M), lambda k: (0, 0)), pl.BlockSpec((N_META, N), lambda k: (0, 0))]

    (out,), x_outs = _pallas(
        body,
        name=name,
        grid=grid,
        out_shape=[jax.ShapeDtypeStruct((M, N), F32)],
        in_specs=in_specs,
        out_specs=[_full_spec((M, N))],
        operands=operands,
        exchange=exchange,
    )
    return out, x_outs


TAB_A, TAB_AS1, TAB_AS2, TAB_AS4, TAB_JF, TAB_JB = 0, 2, 4, 6, 8, 10


def _scan_inplace(zr, zi, tabs, pows, car_r, car_i, seg, reverse, base=0):
    n_slabs = zr.shape[0]
    sgn = -1.0 if reverse else 1.0
    row = lax.broadcasted_iota(jnp.int32, (SUBLANES, LANES), 0)

    def cmul(pr, pi, xr, xi):
        return pr * xr - pi * xi, pr * xi + pi * xr

    for k0 in range(0, n_slabs, SLAB_GROUP):
        slabs = range(k0, min(k0 + SLAB_GROUP, n_slabs))
        ar = [tabs[TAB_A, k] for k in slabs]
        ai = [sgn * tabs[TAB_A + 1, k] for k in slabs]

        def first_pass(t, carry):
            r = (seg - 1 - t) if reverse else t
            out = []
            for q, k in enumerate(slabs):
                xr, xi = carry[2 * q], carry[2 * q + 1]
                pr, pi = cmul(ar[q], ai[q], xr, xi)
                nr = pr + zr[k, pl.ds(base + r, SUBLANES, stride=seg), :]
                ni = pi + zi[k, pl.ds(base + r, SUBLANES, stride=seg), :]
                zr[k, pl.ds(base + r, SUBLANES, stride=seg), :] = nr
                zi[k, pl.ds(base + r, SUBLANES, stride=seg), :] = ni
                out += [nr, ni]
            return tuple(out)

        ends = lax.fori_loop(0, seg, first_pass, tuple(jnp.zeros((SUBLANES, LANES), F32) for _ in range(2 * len(slabs))))

        incoming = []
        for q, k in enumerate(slabs):
            fr, fi = ends[2 * q], ends[2 * q + 1]
            for d, tab in ((1, TAB_AS1), (2, TAB_AS2), (4, TAB_AS4)):
                shift, keep = (SUBLANES - d, row < SUBLANES - d) if reverse else (d, row >= d)
                sr = jnp.where(keep, pltpu.roll(fr, shift, 0), 0.0)
                si = jnp.where(keep, pltpu.roll(fi, shift, 0), 0.0)
                pr, pi = cmul(tabs[tab, k], sgn * tabs[tab + 1, k], sr, si)
                fr, fi = fr + pr, fi + pi
            cr, ci = car_r[k], car_i[k]
            jtab = TAB_JB if reverse else TAB_JF
            pr, pi = cmul(tabs[jtab, k], sgn * tabs[jtab + 1, k], cr, ci)
            er, ei = fr + pr, fi + pi
            if reverse:
                inr = jnp.where(row < SUBLANES - 1, pltpu.roll(er, SUBLANES - 1, 0), cr)
                ini = jnp.where(row < SUBLANES - 1, pltpu.roll(ei, SUBLANES - 1, 0), ci)
                car_r[k] = jnp.broadcast_to(er[0:1, :], (SUBLANES, LANES))
                car_i[k] = jnp.broadcast_to(ei[0:1, :], (SUBLANES, LANES))
            else:
                inr = jnp.where(row >= 1, pltpu.roll(er, 1, 0), cr)
                ini = jnp.where(row >= 1, pltpu.roll(ei, 1, 0), ci)
                car_r[k] = jnp.broadcast_to(er[SUBLANES - 1 : SUBLANES, :], (SUBLANES, LANES))
                car_i[k] = jnp.broadcast_to(ei[SUBLANES - 1 : SUBLANES, :], (SUBLANES, LANES))
            incoming += [inr, ini]

        def second_pass(r, _):
            p = (seg - 1 - r) if reverse else r
            for q, k in enumerate(slabs):
                pr, pi = cmul(pows[0, k, p], sgn * pows[1, k, p], incoming[2 * q], incoming[2 * q + 1])
                zr[k, pl.ds(base + r, SUBLANES, stride=seg), :] = zr[k, pl.ds(base + r, SUBLANES, stride=seg), :] + pr
                zi[k, pl.ds(base + r, SUBLANES, stride=seg), :] = zi[k, pl.ds(base + r, SUBLANES, stride=seg), :] + pi
            return 0

        lax.fori_loop(0, seg, second_pass, 0)


def _slabs_to_cols(ref, k0, n):
    return jnp.concatenate([ref[k0 + q] for q in range(n)], axis=1)


def _window_sum(ext, doublings, forward):
    rows = ext.shape[0]
    s = ext
    for k in range(doublings):
        s = s + pltpu.roll(s, (1 << k) if forward else rows - (1 << k), 0)
    return s


def _mix_fwd(h1, g_pre, g_so, g_po, g_post, dskip, pscale, win, wout, bbre, bbim, ccre, ccim, wgv, wgg, pw, tabs, pows, name):
    T, D = h1.shape
    W = D // 2
    tm = MIX_SUBTILES * MIX_TILE
    seg = MIX_TILE // SUBLANES
    n_slabs = tabs.shape[1]
    nch, cch, sch = bbre.shape
    spc = sch // LANES
    pg = W // len(POOL_WINDOWS)

    def body(h_ref, gpre_ref, gso_ref, gpo_ref, gpost_ref, dskip_ref, pscale_ref, win_ref, wout_ref, bbre_ref, bbim_ref, ccre_ref, ccim_ref, wgv_ref, wgg_ref, pw_ref, tabs_ref, pows_ref,
             proj_ref, xr_ref, xi_ref, y_ref, pooled_ref, mixed_ref, h2_ref, n2_ref, cat_ref, car_r, car_i, halo):
        i = pl.program_id(0)

        @pl.when(i == 0)
        def _():
            car_r[...] = jnp.zeros_like(car_r)
            car_i[...] = jnp.zeros_like(car_i)
            halo[...] = jnp.zeros_like(halo)

        hh = h_ref[...]
        n2 = (hh * _rms_stat(hh) * gpre_ref[...]).astype(MXU_DTYPE)
        n2_ref[...] = n2.astype(n2_ref.dtype)
        proj = _mm(n2, win_ref[...])
        proj_ref[...] = proj
        us, up = proj[:, :W], proj[:, W:]

        for c in range(nch):
            uc = us[:, c * cch : (c + 1) * cch].astype(MXU_DTYPE)
            bur, bui = _mm(uc, bbre_ref[c]), _mm(uc, bbim_ref[c])
            for q in range(spc):
                xr_ref[c * spc + q] = bur[:, q * LANES : (q + 1) * LANES]
                xi_ref[c * spc + q] = bui[:, q * LANES : (q + 1) * LANES]
        for sub in range(MIX_SUBTILES):
            _scan_inplace(xr_ref, xi_ref, tabs_ref, pows_ref, car_r, car_i, seg, reverse=False, base=sub * MIX_TILE)
        ys = []
        for c in range(nch):
            ys.append(_mm(_slabs_to_cols(xr_ref, c * spc, spc), ccre_ref[c]) - _mm(_slabs_to_cols(xi_ref, c * spc, spc), ccim_ref[c]))
        y = jnp.concatenate(ys, axis=1) + dskip_ref[...] * us
        y_ref[...] = y
        ge = _gelu(y).astype(MXU_DTYPE)
        zv = jnp.concatenate([_mm(ge[:, c * cch : (c + 1) * cch], wgv_ref[c]) for c in range(nch)], axis=1)
        zg = jnp.concatenate([_mm(ge[:, c * cch : (c + 1) * cch], wgg_ref[c]) for c in range(nch)], axis=1)
        out = zv * _sigmoid(zg)
        cat_s = out * _rms_stat(out) * gso_ref[...]

        ext = jnp.concatenate([halo[...], up], axis=0)
        halo[...] = up[tm - POOL_HALO :, :]
        t1 = (i * tm + 1 + lax.broadcasted_iota(jnp.int32, (tm, pg), 0)).astype(F32)
        pooled, pms = [], []
        for g, w in enumerate(POOL_WINDOWS):
            col = ext[:, g * pg : (g + 1) * pg]
            win_sum = _window_sum(col, g + 1, True)[POOL_HALO:, :]
            pooled_g = win_sum / jnp.minimum(t1, float(w)) - up[:, g * pg : (g + 1) * pg]
            pooled.append(pooled_g)
            pms.append(_mm(pooled_g, pw_ref[g]))
        pooled_ref[...] = jnp.concatenate(pooled, axis=1)
        yp = jnp.concatenate(pms, axis=1) * pscale_ref[...]
        cat_p = yp * _rms_stat(yp) * gpo_ref[...]

        cat = jnp.concatenate([cat_s, cat_p], axis=1).astype(MXU_DTYPE)
        cat_ref[...] = cat.astype(cat_ref.dtype)
        mixed = _mm(cat, wout_ref[...])
        mixed_ref[...] = mixed
        h2_ref[...] = hh + mixed * _rms_stat(mixed) * gpost_ref[...]

    tok = lambda cols, dt=F32: jax.ShapeDtypeStruct((T, cols), dt)
    slab_spec = pl.BlockSpec((n_slabs, tm, LANES), lambda i: (0, i, 0))
    operands = (h1, g_pre, g_so, g_po, g_post, dskip, pscale, win, wout, bbre, bbim, ccre, ccim, wgv, wgg, pw, tabs, pows)
    outs, _ = _pallas(
        body,
        name=name,
        grid=(T // tm,),
        out_shape=[tok(D), jax.ShapeDtypeStruct((n_slabs, T, LANES), F32), jax.ShapeDtypeStruct((n_slabs, T, LANES), F32), tok(W), tok(W), tok(D), tok(D), tok(D, MXU_DTYPE), tok(D, MXU_DTYPE)],
        in_specs=[_row_spec(tm, D)] + [_full_spec(o.shape, single=True) for o in operands[1:]],
        out_specs=[_row_spec(tm, D), slab_spec, slab_spec, _row_spec(tm, W), _row_spec(tm, W), _row_spec(tm, D), _row_spec(tm, D), _row_spec(tm, D), _row_spec(tm, D)],
        scratch_shapes=[pltpu.VMEM((n_slabs, SUBLANES, LANES), F32), pltpu.VMEM((n_slabs, SUBLANES, LANES), F32), pltpu.VMEM((POOL_HALO, W), F32)],
        operands=operands,
    )
    return outs


def _mix_bwd_heads(dh2, mixed, y, pooled, proj, g_so, g_po, g_post, pscale, wout, wgv, wgg, pw, name, exchange=None):
    T, D = dh2.shape
    W = D // 2
    tm = FFN_TILE
    nch, cch, _ = wgv.shape
    ng, pg, _ = pw.shape

    def body(dh2_ref, mixed_ref, y_ref, pooled_ref, us_ref, gso_ref, gpo_ref, gpost_ref, pscale_ref, wout_ref, wgv_ref, wgg_ref, pw_ref,
             dy_ref, dpooled_ref, dmixed_ref, dgpost_ref, dgso_ref, dgpo_ref, dd_ref, dscale_ref, dwgv_ref, dwgg_ref, dpw_ref):
        first = pl.program_id(0) == 0
        dmixed, dgpost = _rms_bwd(mixed_ref[...], gpost_ref[...], dh2_ref[...])
        _acc(dgpost_ref, dgpost, first)
        dmb = dmixed.astype(MXU_DTYPE)
        dmixed_ref[...] = dmb.astype(dmixed_ref.dtype)
        dcat = _mm_nt(dmb, wout_ref[...])
        dcs, dcp = dcat[:, :W], dcat[:, W:]

        y = y_ref[...]
        ge = _gelu(y).astype(MXU_DTYPE)
        zv = jnp.concatenate([_mm(ge[:, c * cch : (c + 1) * cch], wgv_ref[c]) for c in range(nch)], axis=1)
        zg = jnp.concatenate([_mm(ge[:, c * cch : (c + 1) * cch], wgg_ref[c]) for c in range(nch)], axis=1)
        sg = _sigmoid(zg)
        dout, dgso = _rms_bwd(zv * sg, gso_ref[...], dcs)
        _acc(dgso_ref, dgso, first)
        dzv = (dout * sg).astype(MXU_DTYPE)
        dzg = (dout * zv * sg * (1.0 - sg)).astype(MXU_DTYPE)
        dges = []
        for c in range(nch):
            cs = slice(c * cch, (c + 1) * cch)
            dges.append(_mm_nt(dzv[:, cs], wgv_ref[c]) + _mm_nt(dzg[:, cs], wgg_ref[c]))
            _acc(dwgv_ref.at[c], _mm_tn(ge[:, cs], dzv[:, cs]), first)
            _acc(dwgg_ref.at[c], _mm_tn(ge[:, cs], dzg[:, cs]), first)
        dy = jnp.concatenate(dges, axis=1) * _gelu_grad(y)
        dy_ref[...] = dy
        _acc(dd_ref, jnp.sum(dy * us_ref[...], axis=0, keepdims=True), first)

        pooled_b = pooled_ref[...].astype(MXU_DTYPE)
        pm = jnp.concatenate([_mm(pooled_b[:, g * pg : (g + 1) * pg], pw_ref[g]) for g in range(ng)], axis=1)
        dyp, dgpo = _rms_bwd(pm * pscale_ref[...], gpo_ref[...], dcp)
        _acc(dgpo_ref, dgpo, first)
        _acc(dscale_ref, jnp.sum(dyp * pm, axis=0, keepdims=True), first)
        dpm = (dyp * pscale_ref[...]).astype(MXU_DTYPE)
        dps = []
        for g in range(ng):
            gs = slice(g * pg, (g + 1) * pg)
            dps.append(_mm_nt(dpm[:, gs], pw_ref[g]))
            _acc(dpw_ref.at[g], _mm_tn(pooled_b[:, gs], dpm[:, gs]), first)
        dpooled_ref[...] = jnp.concatenate(dps, axis=1)

    vec = lambda n: jax.ShapeDtypeStruct((1, n), F32)
    operands = (dh2, mixed, y, pooled, proj, g_so, g_po, g_post, pscale, wout, wgv, wgg, pw)
    return _pallas(
        body,
        name=name,
        grid=(T // tm,),
        out_shape=[jax.ShapeDtypeStruct((T, W), F32), jax.ShapeDtypeStruct((T, W), F32), jax.ShapeDtypeStruct((T, D), MXU_DTYPE), vec(D), vec(W), vec(W), vec(W), vec(W),
                   jax.ShapeDtypeStruct(wgv.shape, F32), jax.ShapeDtypeStruct(wgg.shape, F32), jax.ShapeDtypeStruct(pw.shape, F32)],
        in_specs=[_row_spec(tm, D), _row_spec(tm, D), _row_spec(tm, W), _row_spec(tm, W), _row_spec(tm, W)] + [_full_spec(o.shape) for o in operands[5:]],
        out_specs=[_row_spec(tm, W), _row_spec(tm, W), _row_spec(tm, D), _full_spec((1, D)), _full_spec((1, W)), _full_spec((1, W)), _full_spec((1, W)), _full_spec((1, W)),
                   _full_spec(wgv.shape), _full_spec(wgg.shape), _full_spec(pw.shape)],
        operands=operands,
        exchange=exchange,
    )


def _mix_bwd_scan(dy, dpooled, xr, xi, proj, dskip, bbre, bbim, ccre, ccim, tabs, pows, name, exchange=None):
    T, W = dy.shape
    D = 2 * W
    tm = MIX_SUBTILES * MIX_TILE
    seg = MIX_TILE // SUBLANES
    nt = T // tm
    n_slabs = tabs.shape[1]
    nch, cch, sch = bbre.shape
    spc = sch // LANES
    pg = W // len(POOL_WINDOWS)
    blocks_per_tile = tm // SUBLANES

    def body(dy_ref, dp_ref, xr_ref, xi_ref, xpr_ref, xpi_ref, us_ref, dskip_ref, bbre_ref, bbim_ref, ccre_ref, ccim_ref, tabs_ref, pows_ref,
             dproj_ref, dccre_ref, dccim_ref, dbbre_ref, dbbim_ref, dar_ref, dai_ref, lr, li, car_r, car_i, halo):
        i = pl.program_id(0)
        first = i == 0
        tile = nt - 1 - i
        row = lax.broadcasted_iota(jnp.int32, (SUBLANES, LANES), 0)

        @pl.when(first)
        def _():
            car_r[...] = jnp.zeros_like(car_r)
            car_i[...] = jnp.zeros_like(car_i)
            halo[...] = jnp.zeros_like(halo)
            dar_ref[...] = jnp.zeros_like(dar_ref)
            dai_ref[...] = jnp.zeros_like(dai_ref)

        dy = dy_ref[...]
        for c in range(nch):
            dyc = dy[:, c * cch : (c + 1) * cch]
            gr, gi = _mm_nt(dyc, ccre_ref[c]), _mm_nt(dyc, ccim_ref[c])
            for q in range(spc):
                lr[c * spc + q] = gr[:, q * LANES : (q + 1) * LANES]
                li[c * spc + q] = -gi[:, q * LANES : (q + 1) * LANES]
            _acc(dccre_ref.at[c], _mm_tn(_slabs_to_cols(xr_ref, c * spc, spc), dyc), first)
            _acc(dccim_ref.at[c], -_mm_tn(_slabs_to_cols(xi_ref, c * spc, spc), dyc), first)
        for sub in reversed(range(MIX_SUBTILES)):
            _scan_inplace(lr, li, tabs_ref, pows_ref, car_r, car_i, seg, reverse=True, base=sub * MIX_TILE)

        for sub in range(MIX_SUBTILES):
            base = sub * MIX_TILE
            for k0 in range(0, n_slabs, SLAB_GROUP):
                slabs = range(k0, min(k0 + SLAB_GROUP, n_slabs))
                init = []
                for k in slabs:
                    if sub == 0:
                        prev_r = jnp.where(tile > 0, jnp.broadcast_to(xpr_ref[k, SUBLANES - 1 : SUBLANES, :], (SUBLANES, LANES)), 0.0)
                        prev_i = jnp.where(tile > 0, jnp.broadcast_to(xpi_ref[k, SUBLANES - 1 : SUBLANES, :], (SUBLANES, LANES)), 0.0)
                    else:
                        prev_r = jnp.broadcast_to(xr_ref[k, base - 1 : base, :], (SUBLANES, LANES))
                        prev_i = jnp.broadcast_to(xi_ref[k, base - 1 : base, :], (SUBLANES, LANES))
                    x0r = jnp.where(row >= 1, pltpu.roll(xr_ref[k, pl.ds(base + seg - 1, SUBLANES, stride=seg), :], 1, 0), prev_r)
                    x0i = jnp.where(row >= 1, pltpu.roll(xi_ref[k, pl.ds(base + seg - 1, SUBLANES, stride=seg), :], 1, 0), prev_i)
                    l0r, l0i = lr[k, pl.ds(base, SUBLANES, stride=seg), :], li[k, pl.ds(base, SUBLANES, stride=seg), :]
                    init += [l0r * x0r + l0i * x0i, l0i * x0r - l0r * x0i]

                def step(r, acc, slabs=slabs, base=base):
                    out = []
                    for q, k in enumerate(slabs):
                        pr_, pi_ = xr_ref[k, pl.ds(base + r - 1, SUBLANES, stride=seg), :], xi_ref[k, pl.ds(base + r - 1, SUBLANES, stride=seg), :]
                        lr_, li_ = lr[k, pl.ds(base + r, SUBLANES, stride=seg), :], li[k, pl.ds(base + r, SUBLANES, stride=seg), :]
                        out += [acc[2 * q] + lr_ * pr_ + li_ * pi_, acc[2 * q + 1] + li_ * pr_ - lr_ * pi_]
                    return tuple(out)

                sums = lax.fori_loop(1, seg, step, tuple(init))
                for q, k in enumerate(slabs):
                    dar_ref[k] += sums[2 * q]
                    dai_ref[k] += sums[2 * q + 1]

        us = us_ref[...]
        dus = []
        for c in range(nch):
            lrc, lic = _slabs_to_cols(lr, c * spc, spc).astype(MXU_DTYPE), _slabs_to_cols(li, c * spc, spc).astype(MXU_DTYPE)
            uc = us[:, c * cch : (c + 1) * cch]
            _acc(dbbre_ref.at[c], _mm_tn(uc, lrc), first)
            _acc(dbbim_ref.at[c], _mm_tn(uc, lic), first)
            dus.append(_mm_nt(lrc, bbre_ref[c]) + _mm_nt(lic, bbim_ref[c]))
        du_s = jnp.concatenate(dus, axis=1) + dskip_ref[...] * dy

        dp = dp_ref[...]
        t1 = (tile * tm + 1 + lax.broadcasted_iota(jnp.int32, (tm, pg), 0)).astype(F32)
        dups, heads = [], []
        for g, w in enumerate(POOL_WINDOWS):
            dpg = dp[:, g * pg : (g + 1) * pg]
            qg = dpg / jnp.minimum(t1, float(w))
            ext = jnp.concatenate([qg, halo[:, g * pg : (g + 1) * pg]], axis=0)
            dups.append(_window_sum(ext, g + 1, False)[:tm, :] - dpg)
            heads.append(qg[:POOL_HALO, :])
        halo[...] = jnp.concatenate(heads, axis=1)
        dproj_ref[...] = jnp.concatenate([du_s] + dups, axis=1).astype(dproj_ref.dtype)

    rev = lambda cols: _row_spec(tm, cols, rev_n=nt)
    slab_spec = pl.BlockSpec((n_slabs, tm, LANES), lambda i: (0, nt - 1 - i, 0))
    prev_spec = pl.BlockSpec((n_slabs, SUBLANES, LANES), lambda i: (0, jnp.maximum((nt - 1 - i) * blocks_per_tile - 1, 0), 0))
    consts = (dskip, bbre, bbim, ccre, ccim, tabs, pows)
    return _pallas(
        body,
        name=name,
        grid=(nt,),
        out_shape=[jax.ShapeDtypeStruct((T, D), MXU_DTYPE), jax.ShapeDtypeStruct(ccre.shape, F32), jax.ShapeDtypeStruct(ccim.shape, F32), jax.ShapeDtypeStruct(bbre.shape, F32),
                   jax.ShapeDtypeStruct(bbim.shape, F32), jax.ShapeDtypeStruct((n_slabs, SUBLANES, LANES), F32), jax.ShapeDtypeStruct((n_slabs, SUBLANES, LANES), F32)],
        in_specs=[rev(W), rev(W), slab_spec, slab_spec, prev_spec, prev_spec, rev(W)] + [_full_spec(o.shape, single=True) for o in consts],
        out_specs=[rev(D), _full_spec(ccre.shape), _full_spec(ccim.shape), _full_spec(bbre.shape), _full_spec(bbim.shape),
                   _full_spec((n_slabs, SUBLANES, LANES)), _full_spec((n_slabs, SUBLANES, LANES))],
        scratch_shapes=[pltpu.VMEM((n_slabs, tm, LANES), F32), pltpu.VMEM((n_slabs, tm, LANES), F32), pltpu.VMEM((n_slabs, SUBLANES, LANES), F32), pltpu.VMEM((n_slabs, SUBLANES, LANES), F32),
                        pltpu.VMEM((POOL_HALO, W), F32)],
        operands=(dy, dpooled, xr, xi, xr, xi, proj, *consts),
        exchange=exchange,
    )


def _mix_bwd_in(dproj, h1, dh2, g_pre, win, name):
    T, D = h1.shape
    tm = _token_tile(T)

    def body(dproj_ref, h_ref, dh2_ref, gpre_ref, win_ref, dh1_ref, dg_ref):
        dx, dg = _rms_bwd(h_ref[...], gpre_ref[...], _mm_nt(dproj_ref[...], win_ref[...]))
        _acc(dg_ref, dg, pl.program_id(0) == 0)
        dh1_ref[...] = dh2_ref[...] + dx

    outs, _ = _pallas(
        body,
        name=name,
        grid=(T // tm,),
        out_shape=[jax.ShapeDtypeStruct((T, D), F32), jax.ShapeDtypeStruct((1, D), F32)],
        in_specs=[_row_spec(tm, D), _row_spec(tm, D), _row_spec(tm, D), _full_spec((1, D)), _full_spec(win.shape)],
        out_specs=[_row_spec(tm, D), _full_spec((1, D))],
        operands=(dproj, h1, dh2, g_pre, win),
    )
    return outs


def _discretize(lam_re, lam_im, log_dt, b_re, b_im):
    dt = jnp.exp(log_dt)[:, None]
    decay = jnp.exp(lam_re * dt)
    ang = lam_im * dt
    a_re, a_im = decay * jnp.cos(ang), decay * jnp.sin(ang)
    nr = a_re - 1.0
    den = lam_re * lam_re + lam_im * lam_im
    q_re = (nr * lam_re + a_im * lam_im) / den
    q_im = (a_im * lam_re - nr * lam_im) / den
    bb_re = q_re[..., None] * b_re - q_im[..., None] * b_im
    bb_im = q_re[..., None] * b_im + q_im[..., None] * b_re
    return a_re, a_im, bb_re, bb_im


GROUPS_PER_CHUNK = 16


def _block_diag(w, rows_first):
    G = w.shape[0]
    nch = G // GROUPS_PER_CHUNK
    if not rows_first:
        w = jnp.swapaxes(w, 1, 2)
    p, q = w.shape[1], w.shape[2]
    eye = jnp.eye(GROUPS_PER_CHUNK, dtype=w.dtype)
    out = jnp.einsum("cgpq,gk->cgpkq", w.reshape(nch, GROUPS_PER_CHUNK, p, q), eye)
    return out.reshape(nch, GROUPS_PER_CHUNK * p, GROUPS_PER_CHUNK * q)


def _block_diag_extract(m, p, q, rows_first):
    nch = m.shape[0]
    eye = jnp.eye(GROUPS_PER_CHUNK, dtype=m.dtype)
    out = jnp.einsum("cgpkq,gk->cgpq", m.reshape(nch, GROUPS_PER_CHUNK, p, GROUPS_PER_CHUNK, q), eye).reshape(nch * GROUPS_PER_CHUNK, p, q)
    return out if rows_first else jnp.swapaxes(out, 1, 2)


def _cmul(ar, ai, br, bi):
    return ar * br - ai * bi, ar * bi + ai * br


def _powers(ar, ai, count):
    pr, pi = ar[None], ai[None]
    while pr.shape[0] < count:
        nr, ni = _cmul(pr, pi, pr[-1][None], pi[-1][None])
        pr, pi = jnp.concatenate([pr, nr]), jnp.concatenate([pi, ni])
    return pr[:count], pi[:count]


def _scan_tables(a_re, a_im, seg):
    n = a_re.size
    ns = n // LANES
    ar, ai = a_re.reshape(n), a_im.reshape(n)
    pr, pi = _powers(ar, ai, seg)
    jr, ji = _powers(pr[-1], pi[-1], SUBLANES)

    def bcast(v):
        return jnp.broadcast_to(v.reshape(ns, 1, LANES), (ns, SUBLANES, LANES))

    def per_sublane(vs):
        return jnp.transpose(vs.reshape(SUBLANES, ns, LANES), (1, 0, 2))

    tabs = jnp.stack([bcast(ar), bcast(ai), bcast(jr[0]), bcast(ji[0]), bcast(jr[1]), bcast(ji[1]), bcast(jr[3]), bcast(ji[3]),
                      per_sublane(jr), per_sublane(ji), per_sublane(jr[::-1]), per_sublane(ji[::-1])])

    def rows(vs):
        return jnp.broadcast_to(jnp.transpose(vs.reshape(seg, ns, 1, LANES), (1, 0, 2, 3)), (ns, seg, SUBLANES, LANES))

    return tabs, jnp.stack([rows(pr), rows(pi)])


SMALL = ("ffn1_pre_norm", "ffn1_post_norm", "mix_pre_norm", "mix_post_norm", "ssm_lambda_re", "ssm_lambda_im", "ssm_log_dt", "ssm_b_re", "ssm_b_im", "ssm_c_re", "ssm_c_im",
         "ssm_d", "ssm_w_glu", "pool_w", "pool_scale", "ssm_out_norm", "pool_out_norm", "ffn2_pre_norm", "ffn2_post_norm")
BIG = ("ffn1_w_gate", "ffn1_w_up", "ffn1_w_down", "w_in", "w_out", "ffn2_w_gate", "ffn2_w_up", "ffn2_w_down")
ORDER = ("meta_tokens", "ffn1_pre_norm", "ffn1_post_norm", "ffn1_w_gate", "ffn1_w_up", "ffn1_w_down", "mix_pre_norm", "mix_post_norm", "w_in", "ssm_lambda_re", "ssm_lambda_im",
         "ssm_log_dt", "ssm_b_re", "ssm_b_im", "ssm_c_re", "ssm_c_im", "ssm_d", "ssm_w_glu", "pool_w", "pool_scale", "ssm_out_norm", "pool_out_norm", "w_out", "ffn2_pre_norm",
         "ffn2_post_norm", "ffn2_w_gate", "ffn2_w_up", "ffn2_w_down")
PACK_ROWS = SUBLANES * 8
def _pack(arrays, rows):
    flat = jnp.concatenate([a.reshape(-1) for a in arrays])
    return jnp.pad(flat, (0, rows * LANES - flat.size)).reshape(rows, LANES)


def _unpack(packed, shapes):
    flat = packed.reshape(-1)
    out, off = [], 0
    for s in shapes:
        n = math.prod(s)
        out.append(flat[off : off + n].reshape(s))
        off += n
    return out


def _step(p, x, loss_target, m, v):
    D = x.shape[-1]
    chip = (2 * lax.axis_index("x") + lax.axis_index("y")).astype(jnp.int32)
    place = jnp.stack([chip, lax.axis_index("c").astype(jnp.int32)])

    def gather_buffer(w):
        own = w.reshape(1, 2, w.shape[0] // 2, w.shape[1])
        return lax.dynamic_update_slice(lax.empty((4,) + own.shape[1:], own.dtype), own, (chip, 0, 0, 0))

    def rows_of(n, a):
        return jnp.swapaxes(a[0], 0, 1) if n.endswith(("w_gate", "w_up")) else a[0]

    def rows_back(n, a):
        return (jnp.swapaxes(a, 0, 1) if n.endswith(("w_gate", "w_up")) else a)[None]

    def grad_view(g):
        return g.reshape(4, 2, g.shape[0] // 8, g.shape[1])

    def reduce_sum(got_sibling, views, tag):
        sums = [_add_own_half(v_, g_, place, f"{tag}_add_sibling_{k}") for k, (v_, g_) in enumerate(zip(views, got_sibling))]
        return [s[0] for s in sums], [s[1] for s in sums]

    def reduce_halves(parts, got_chips, tag):
        return [_add_chips(p_, g_, place, f"{tag}_add_chips_{k}") for k, (p_, g_) in enumerate(zip(parts, got_chips))]

    first_names = ("ffn1_w_gate", "ffn1_w_up", "ffn1_w_down")
    later_names = ("w_in", "w_out", "ffn2_w_gate", "ffn2_w_up", "ffn2_w_down")
    bufs = {n: gather_buffer(rows_of(n, p[n]).astype(MXU_DTYPE)) for n in BIG}
    gathered_weight = lambda g_: g_.reshape(-1, g_.shape[-1])
    got = _exchange_call(_Gather([bufs[n] for n in first_names] + [gather_buffer(p["meta_tokens"])]), "gather_first")
    full = {n: gathered_weight(g_) for n, g_ in zip(first_names, got)}
    meta = jnp.transpose(got[-1].reshape(4, N_META, -1), (1, 0, 2)).reshape(N_META, D)

    vec = lambda n: p[n].reshape(1, -1)
    G, N, H = p["ssm_b_re"].shape[1:]
    a_re, a_im, bb_re, bb_im = _discretize(p["ssm_lambda_re"][0], p["ssm_lambda_im"][0], p["ssm_log_dt"][0], p["ssm_b_re"][0], p["ssm_b_im"][0])
    tabs, pows = _scan_tables(a_re, a_im, MIX_TILE // SUBLANES)
    bf = lambda a: a.astype(MXU_DTYPE)
    bbre, bbim = bf(_block_diag(bb_re, False)), bf(_block_diag(bb_im, False))
    ccre, ccim = bf(_block_diag(p["ssm_c_re"][0], False)), bf(_block_diag(p["ssm_c_im"][0], False))
    wgv, wgg = bf(_block_diag(p["ssm_w_glu"][0][:, :, :H], True)), bf(_block_diag(p["ssm_w_glu"][0][:, :, H:], True))
    pw = bf(p["pool_w"][0])

    n_ffn_steps = (x.shape[1] + N_META) // FFN_TILE
    (h1, f1, ga1, si1, s1, n1), got = _ffn_fwd(
        x[0], vec("ffn1_pre_norm"), vec("ffn1_post_norm"), full["ffn1_w_gate"], full["ffn1_w_up"], full["ffn1_w_down"], "ffn1_fwd",
        exchange=_Gather([bufs[n] for n in later_names], mid_step=(3 * n_ffn_steps) // 4), meta=meta,
    )
    full.update({n: gathered_weight(g_) for n, g_ in zip(later_names, got)})
    proj, xr, xim, y, pooled, mixed, h2, n2, cat = _mix_fwd(
        h1, vec("mix_pre_norm"), vec("ssm_out_norm"), vec("pool_out_norm"), vec("mix_post_norm"), vec("ssm_d"), vec("pool_scale"), full["w_in"], full["w_out"],
        bbre, bbim, ccre, ccim, wgv, wgg, pw, tabs, pows, "mix_fwd",
    )
    (dh3, sq, f2, ga2, si2, s2, n3), _ = _ffn_fwd(
        h2, vec("ffn2_pre_norm"), vec("ffn2_post_norm"), full["ffn2_w_gate"], full["ffn2_w_up"], full["ffn2_w_down"], "ffn2_fwd", target=loss_target[0]
    )
    loss = lax.psum(0.5 * jnp.sum(sq) / D, ("x", "y", "c"))

    g, shared = {}, {}
    ffn_names = lambda tag: (tag + "_w_gate", tag + "_w_up", tag + "_w_down")

    da, db, df, dh2, g["ffn2_post_norm"], g["ffn2_pre_norm"] = _ffn_bwd(
        dh3, f2, ga2, si2, h2, vec("ffn2_post_norm"), vec("ffn2_pre_norm"), full["ffn2_w_gate"], full["ffn2_w_up"], full["ffn2_w_down"], "ffn2_bwd"
    )
    views2 = [
        grad_view(_tn_matmul(da, n3, "ffn2_dw_gate")[0]),
        grad_view(_tn_matmul(db, n3, "ffn2_dw_up")[0]),
        grad_view(_tn_matmul(s2, df, "ffn2_dw_down")[0]),
    ]
    (dy, dpooled, dmixed, g["mix_post_norm"], g["ssm_out_norm"], g["pool_out_norm"], g["ssm_d"], g["pool_scale"], dwgv, dwgg, g["pool_w"]), got = _mix_bwd_heads(
        dh2, mixed, y, pooled, proj, vec("ssm_out_norm"), vec("pool_out_norm"), vec("mix_post_norm"), vec("pool_scale"), full["w_out"], wgv, wgg, pw, "mix_bwd_heads",
        exchange=_SiblingScatter(views2),
    )
    parts2, wire2 = reduce_sum(got, views2, "ffn2")
    (dproj, dccre, dccim, dbbre, dbbim, dar, dai), got = _mix_bwd_scan(
        dy, dpooled, xr, xim, proj, vec("ssm_d"), bbre, bbim, ccre, ccim, tabs, pows, "mix_bwd_scan", exchange=_ChipScatter(wire2)
    )
    dh1, g["mix_pre_norm"] = _mix_bwd_in(dproj, h1, dh2, vec("mix_pre_norm"), full["w_in"], "mix_bwd_in")
    halves2 = reduce_halves(parts2, got, "ffn2")
    dw_in, got = _tn_matmul(n2, dproj, "dw_in", exchange=_SiblingShare(halves2))
    shared.update(zip(ffn_names("ffn2"), got))
    dw_out, _ = _tn_matmul(cat, dmixed, "dw_out")
    views_m = [grad_view(dw_in), grad_view(dw_out)]

    (da, db, df, g["ffn1_post_norm"]), got = _ffn_bwd_down(dh1, f1, ga1, si1, vec("ffn1_post_norm"), full["ffn1_w_down"], "ffn1_bwd_down", exchange=_SiblingScatter(views_m))
    parts_m, wire_m = reduce_sum(got, views_m, "mix")
    dw_down, got = _tn_matmul(s1, df, "ffn1_dw_down", exchange=_ChipScatter(wire_m))
    halves_m = reduce_halves(parts_m, got, "mix")
    views_d = [grad_view(dw_down)]
    ex = _Group([_SiblingShare(halves_m), _SiblingScatter(views_d)])
    dw_gate, got = _tn_matmul(da, n1, "ffn1_dw_gate", exchange=ex)
    got_m, got_d = ex.split(got)
    shared.update(zip(("w_in", "w_out"), got_m))
    parts_d, wire_d = reduce_sum(got_d, views_d, "ffn1_down")
    views_g = [grad_view(dw_gate)]
    ex = _Group([_ChipScatter(wire_d), _SiblingScatter(views_g)])
    dw_up, got = _tn_matmul(db, n1, "ffn1_dw_up", exchange=ex)
    got_d, got_g = ex.split(got)
    halves_d = reduce_halves(parts_d, got_d, "ffn1_down")
    parts_g, wire_g = reduce_sum(got_g, views_g, "ffn1_gate")
    views_u = [grad_view(dw_up)]
    ex = _Group([_SiblingShare(halves_d), _ChipScatter(wire_g), _SiblingScatter(views_u)])
    (grad_x, d_meta, g["ffn1_pre_norm"]), got = _ffn_bwd_up(da, db, x[0], meta, dh1, vec("ffn1_pre_norm"), full["ffn1_w_gate"], full["ffn1_w_up"], "ffn1_bwd_up", exchange=ex)
    got_d, got_g, got_u = ex.split(got)
    shared["ffn1_w_down"] = got_d[0]
    halves_g = reduce_halves(parts_g, got_g, "ffn1_gate")
    parts_u, wire_u = reduce_sum(got_u, views_u, "ffn1_up")
    grad_x = grad_x[None]

    g["ssm_c_re"] = _block_diag_extract(dccre, N, H, False)
    g["ssm_c_im"] = _block_diag_extract(dccim, N, H, False)
    g["ssm_w_glu"] = jnp.concatenate([_block_diag_extract(dwgv, H, H, True), _block_diag_extract(dwgg, H, H, True)], axis=-1)
    d_a_re, d_a_im = jnp.sum(dar, axis=1).reshape(G, N), jnp.sum(dai, axis=1).reshape(G, N)
    _, pull = jax.vjp(_discretize, p["ssm_lambda_re"][0], p["ssm_lambda_im"][0], p["ssm_log_dt"][0], p["ssm_b_re"][0], p["ssm_b_im"][0])
    g["ssm_lambda_re"], g["ssm_lambda_im"], g["ssm_log_dt"], g["ssm_b_re"], g["ssm_b_im"] = pull(
        (d_a_re, d_a_im, _block_diag_extract(dbbre, H, N, False), _block_diag_extract(dbbim, H, N, False))
    )

    small_shapes = [p[n].shape for n in SMALL] + [(N_META, D)]
    small_size = sum(math.prod(s) for s in small_shapes)
    rows = -(-small_size // (LANES * PACK_ROWS)) * PACK_ROWS
    views_s = [_pack([g[n] for n in SMALL] + [d_meta], rows).reshape(4, 2, rows // 8, LANES)]
    parts_s, wire_s = reduce_sum(_exchange_call(_SiblingScatter(views_s), "small_reduce_sibling"), views_s, "small")
    ex = _Group([_ChipScatter(wire_u + wire_s), _SiblingShare(halves_g)])
    got_c, got_g = ex.split(_exchange_call(ex, "tail_reduce_chips"))
    shared["ffn1_w_gate"] = got_g[0]
    got = _exchange_call(_SiblingShare(reduce_halves(parts_u + parts_s, got_c, "tail")), "tail_reduce_share")
    shared["ffn1_w_up"] = got[0]
    small_buf = lax.dynamic_update_slice(lax.empty((4,) + got[1].shape, F32), got[1][None], (chip, 0, 0, 0))
    small_all = _exchange_call(_Gather([small_buf]), "gather_small")[0].reshape(rows, LANES)
    grads = dict(zip(SMALL + ("meta_full",), _unpack(small_all, small_shapes)))
    grads["meta_tokens"] = lax.dynamic_slice_in_dim(grads.pop("meta_full"), chip * (D // 4), D // 4, axis=1)
    delta, new_m, new_v = {}, {}, {}
    for n in BIG:
        g_rows = shared[n].reshape(-1, shared[n].shape[-1])
        outs = _adamw(rows_of(n, p[n]), g_rows, rows_of(n, m[n]), rows_of(n, v[n]), "adamw_" + n)
        grads[n], delta[n], new_m[n], new_v[n] = (rows_back(n, a) for a in (g_rows, *outs))
    delta["meta_tokens"], new_m["meta_tokens"], new_v["meta_tokens"] = _adamw(p["meta_tokens"], grads["meta_tokens"], m["meta_tokens"], v["meta_tokens"], "adamw_meta_tokens")

    def as_2d(n, a):
        a = a.reshape(p[n].shape)[0]
        if n in ("ssm_b_re", "ssm_b_im"):
            a = jnp.swapaxes(a, 1, 2)
        return a.reshape(-1, a.shape[-1])

    def from_2d(n, a):
        if n in ("ssm_b_re", "ssm_b_im"):
            g_, n_, h_ = p[n].shape[1:]
            return jnp.swapaxes(a.reshape(g_, h_, n_), 1, 2)[None]
        return a.reshape(p[n].shape)

    outs = _adamw_many(*[[as_2d(n, t[n]) for n in SMALL] for t in (p, grads, m, v)], "adamw_small")
    for out, arrays in zip((delta, new_m, new_v), outs):
        out.update({n: from_2d(n, a) for n, a in zip(SMALL, arrays)})

    return (loss, grad_x, *[grads[n] for n in ORDER], *[delta[n] for n in ORDER], *[new_m[n] for n in ORDER], *[new_v[n] for n in ORDER])


def kernel(x, meta_tokens, ffn1_pre_norm, ffn1_post_norm, ffn1_w_gate, ffn1_w_up, ffn1_w_down, mix_pre_norm, mix_post_norm, w_in, ssm_lambda_re, ssm_lambda_im, ssm_log_dt, ssm_b_re, ssm_b_im, ssm_c_re, ssm_c_im, ssm_d, ssm_w_glu, pool_w, pool_scale, ssm_out_norm, pool_out_norm, w_out, ffn2_pre_norm, ffn2_post_norm, ffn2_w_gate, ffn2_w_up, ffn2_w_down, loss_target, m_meta_tokens, m_ffn1_pre_norm, m_ffn1_post_norm, m_ffn1_w_gate, m_ffn1_w_up, m_ffn1_w_down, m_mix_pre_norm, m_mix_post_norm, m_w_in, m_ssm_lambda_re, m_ssm_lambda_im, m_ssm_log_dt, m_ssm_b_re, m_ssm_b_im, m_ssm_c_re, m_ssm_c_im, m_ssm_d, m_ssm_w_glu, m_pool_w, m_pool_scale, m_ssm_out_norm, m_pool_out_norm, m_w_out, m_ffn2_pre_norm, m_ffn2_post_norm, m_ffn2_w_gate, m_ffn2_w_up, m_ffn2_w_down, v_meta_tokens, v_ffn1_pre_norm, v_ffn1_post_norm, v_ffn1_w_gate, v_ffn1_w_up, v_ffn1_w_down, v_mix_pre_norm, v_mix_post_norm, v_w_in, v_ssm_lambda_re, v_ssm_lambda_im, v_ssm_log_dt, v_ssm_b_re, v_ssm_b_im, v_ssm_c_re, v_ssm_c_im, v_ssm_d, v_ssm_w_glu, v_pool_w, v_pool_scale, v_ssm_out_norm, v_pool_out_norm, v_w_out, v_ffn2_pre_norm, v_ffn2_post_norm, v_ffn2_w_gate, v_ffn2_w_up, v_ffn2_w_down):
    args = locals()
    p = {n: args[n] for n in ORDER}
    m = {n: args["m_" + n] for n in ORDER}
    v = {n: args["v_" + n] for n in ORDER}
    return _step(p, x, loss_target, m, v)
```

```python
import math

import jax
import jax.numpy as jnp
from jax import lax
from jax.experimental import pallas as pl
from jax.experimental.pallas import tpu as pltpu

F32 = jnp.float32
MXU_DTYPE = jnp.bfloat16
WIRE_DTYPE = jnp.bfloat16

RMS_EPS = 1e-6
N_META = 16
POOL_WINDOWS = (2, 4, 8, 16)
POOL_HALO = 16
ADAM_LR, ADAM_B1, ADAM_B2, ADAM_EPS, ADAM_WD, ADAM_STEP = 0.001, 0.9, 0.999, 1e-08, 0.01, 10

LANES = 128
SUBLANES = 8
VMEM_LIMIT = 60 * 1024 * 1024
FFN_TILE = 432
FFN_CHUNK = 1024
TN_TILE = 1024
MIX_TILE = 216
MIX_SUBTILES = 2
SLAB_GROUP = 8
MESH = pl.DeviceIdType.MESH
ANY = pl.BlockSpec(memory_space=pl.ANY)


def _mm(a, b):
    return jnp.dot(a.astype(MXU_DTYPE), b.astype(MXU_DTYPE), preferred_element_type=F32)


def _mm_nt(a, b):
    return lax.dot_general(a.astype(MXU_DTYPE), b.astype(MXU_DTYPE), (((1,), (1,)), ((), ())), preferred_element_type=F32)


def _mm_tn(a, b):
    return lax.dot_general(a.astype(MXU_DTYPE), b.astype(MXU_DTYPE), (((0,), (0,)), ((), ())), preferred_element_type=F32)


def _rms_stat(x):
    return lax.rsqrt(jnp.mean(x * x, axis=-1, keepdims=True) + RMS_EPS)


def _rms_bwd(x, g, dy):
    r = _rms_stat(x)
    xh = x * r
    dg = jnp.sum(dy * xh, axis=0, keepdims=True)
    dxh = dy * g
    dx = r * (dxh - xh * jnp.mean(dxh * xh, axis=-1, keepdims=True))
    return dx, dg


def _sigmoid(x):
    return 1.0 / (1.0 + jnp.exp(-x))


GELU_C = math.sqrt(2.0 / math.pi)
GELU_K = 0.044715


def _gelu(y):
    return 0.5 * y * (1.0 + jnp.tanh(GELU_C * (y + GELU_K * y * y * y)))


def _gelu_grad(y):
    th = jnp.tanh(GELU_C * (y + GELU_K * y * y * y))
    return 0.5 * (1.0 + th) + 0.5 * y * (1.0 - th * th) * GELU_C * (1.0 + 3.0 * GELU_K * y * y)


def _row_spec(tile, cols, rev_n=None):
    if rev_n is None:
        return pl.BlockSpec((tile, cols), lambda i: (i, 0))
    return pl.BlockSpec((tile, cols), lambda i: (rev_n - 1 - i, 0))


def _full_spec(shape, single=False):
    zeros = (0,) * len(shape)
    if single:
        return pl.BlockSpec(shape, lambda *_: zeros, pipeline_mode=pl.Buffered(1))
    return pl.BlockSpec(shape, lambda *_: zeros)


def _acc(ref, val, first):
    @pl.when(first)
    def _():
        ref[...] = val

    @pl.when(jnp.logical_not(first))
    def _():
        ref[...] += val


def _place():
    x, y, c = lax.axis_index("x"), lax.axis_index("y"), lax.axis_index("c")
    others = [(1 - x, y), (x, 1 - y), (1 - x, 1 - y)]
    return x, y, c, others


class _Exchange:
    mid_step = None

    def __init__(self, ins, out_shapes, aliases, n_sems):
        self.ins, self.out_shapes, self.aliases, self.n_sems = list(ins), list(out_shapes), dict(aliases), n_sems

    def mid(self, ins, outs, send_sems, recv_sems):
        pass


class _SiblingScatter(_Exchange):
    def __init__(self, views):
        super().__init__(views, [jax.ShapeDtypeStruct((4,) + v.shape[2:], v.dtype) for v in views], {}, 4 * len(views))

    def _copies(self, ins, outs, send_sems, recv_sems):
        x, y, c, _ = _place()
        return [
            pltpu.make_async_remote_copy(src_ref=ins[a].at[k, 1 - c], dst_ref=outs[a].at[k], send_sem=send_sems.at[4 * a + k], recv_sem=recv_sems.at[4 * a + k], device_id=(x, y, 1 - c), device_id_type=MESH)
            for a in range(len(ins))
            for k in range(4)
        ]

    def start(self, *refs):
        for cp in self._copies(*refs):
            cp.start()

    def finish(self, *refs):
        cps = self._copies(*refs)
        for cp in cps:
            cp.wait_recv()
        for cp in cps:
            cp.wait_send()


class _ChipScatter(_Exchange):
    def __init__(self, parts):
        super().__init__(parts, [jax.ShapeDtypeStruct((3,) + p.shape[1:], p.dtype) for p in parts], {}, 3 * len(parts))

    def _copies(self, ins, outs, send_sems, recv_sems):
        x, y, c, others = _place()
        return [
            pltpu.make_async_remote_copy(src_ref=ins[a].at[2 * chip[0] + chip[1]], dst_ref=outs[a].at[j], send_sem=send_sems.at[3 * a + j], recv_sem=recv_sems.at[3 * a + j], device_id=(*chip, c), device_id_type=MESH)
            for a in range(len(ins))
            for j, chip in enumerate(others)
        ]

    start = _SiblingScatter.start
    finish = _SiblingScatter.finish


class _SiblingShare(_Exchange):
    def __init__(self, bufs):
        super().__init__(bufs, [jax.ShapeDtypeStruct(b.shape, b.dtype) for b in bufs], {a: a for a in range(len(bufs))}, len(bufs))

    def _copy(self, outs, send_sems, recv_sems, a, half):
        x, y, c, _ = _place()
        mine = outs[a].at[c if half == "mine" else 1 - c]
        return pltpu.make_async_remote_copy(src_ref=mine, dst_ref=mine, send_sem=send_sems.at[a], recv_sem=recv_sems.at[a], device_id=(x, y, 1 - c), device_id_type=MESH)

    def start(self, ins, outs, send_sems, recv_sems):
        for a in range(len(outs)):
            self._copy(outs, send_sems, recv_sems, a, "mine").start()

    def finish(self, ins, outs, send_sems, recv_sems):
        for a in range(len(outs)):
            self._copy(outs, send_sems, recv_sems, a, "theirs").wait_recv()
        for a in range(len(outs)):
            self._copy(outs, send_sems, recv_sems, a, "mine").wait_send()


class _Gather(_Exchange):
    def __init__(self, bufs, mid_step=None):
        super().__init__(bufs, [jax.ShapeDtypeStruct(b.shape, b.dtype) for b in bufs], {a: a for a in range(len(bufs))}, 6 * len(bufs))
        self.mid_step = mid_step

    def _copy(self, outs, send_sems, recv_sems, a, j, chip, half, to):
        blk = outs[a].at[2 * chip[0] + chip[1], half]
        return pltpu.make_async_remote_copy(src_ref=blk, dst_ref=blk, send_sem=send_sems.at[6 * a + j], recv_sem=recv_sems.at[6 * a + j], device_id=to, device_id_type=MESH)

    def start(self, ins, outs, send_sems, recv_sems):
        x, y, c, others = _place()
        for a in range(len(outs)):
            for j, chip in enumerate(others):
                self._copy(outs, send_sems, recv_sems, a, j, (x, y), c, (*chip, c)).start()

    def mid(self, ins, outs, send_sems, recv_sems):
        x, y, c, others = _place()
        for a in range(len(outs)):
            for j, chip in enumerate(others):
                self._copy(outs, send_sems, recv_sems, a, j, chip, c, (x, y, c)).wait_recv()
                self._copy(outs, send_sems, recv_sems, a, 3 + j, chip, c, (x, y, 1 - c)).start()

    def finish(self, ins, outs, send_sems, recv_sems):
        x, y, c, others = _place()
        for a in range(len(outs)):
            for j, chip in enumerate(others):
                self._copy(outs, send_sems, recv_sems, a, 3 + j, chip, 1 - c, (x, y, c)).wait_recv()
        for a in range(len(outs)):
            for j, chip in enumerate(others):
                self._copy(outs, send_sems, recv_sems, a, j, (x, y), c, (*chip, c)).wait_send()
                self._copy(outs, send_sems, recv_sems, a, 3 + j, chip, c, (x, y, 1 - c)).wait_send()


class _SemSlice:
    def __init__(self, sems, off):
        self.sems, self.off = sems, off

    @property
    def at(self):
        return self

    def __getitem__(self, i):
        return self.sems.at[self.off + i]


class _Group(_Exchange):
    def __init__(self, exchanges):
        ins, outs, aliases, n_sems, self.spans = [], [], {}, 0, []
        for ex in exchanges:
            self.spans.append((len(ins), len(outs), n_sems))
            aliases.update({len(ins) + i: len(outs) + o for i, o in ex.aliases.items()})
            ins, outs, n_sems = ins + ex.ins, outs + ex.out_shapes, n_sems + ex.n_sems
        super().__init__(ins, outs, aliases, n_sems)
        self.exchanges = exchanges
        mids = {ex.mid_step for ex in exchanges if ex.mid_step is not None}
        self.mid_step = mids.pop() if mids else None

    def _each(self, method, ins, outs, send_sems, recv_sems):
        for ex, (i0, o0, s0) in zip(self.exchanges, self.spans):
            getattr(ex, method)(ins[i0 : i0 + len(ex.ins)], outs[o0 : o0 + len(ex.out_shapes)], _SemSlice(send_sems, s0), _SemSlice(recv_sems, s0))

    def start(self, *refs):
        self._each("start", *refs)

    def mid(self, *refs):
        self._each("mid", *refs)

    def finish(self, *refs):
        self._each("finish", *refs)

    def split(self, outs):
        return [outs[o0 : o0 + len(ex.out_shapes)] for ex, (_, o0, _) in zip(self.exchanges, self.spans)]


def _exchange_call(ex, name):
    n, m = len(ex.ins), len(ex.out_shapes)

    def body(*refs):
        parts = (refs[:n], refs[n : n + m], refs[n + m], refs[n + m + 1])
        ex.start(*parts)
        ex.mid(*parts)
        ex.finish(*parts)

    return pl.pallas_call(
        body,
        name=name,
        out_shape=ex.out_shapes,
        in_specs=[ANY] * n,
        out_specs=[ANY] * m,
        scratch_shapes=[pltpu.SemaphoreType.DMA((ex.n_sems,)), pltpu.SemaphoreType.DMA((ex.n_sems,))],
        input_output_aliases=ex.aliases,
    )(*ex.ins)


def _pallas(body, *, name, grid, in_specs, out_specs, out_shape, operands, scratch_shapes=(), exchange=None):
    params = pltpu.CompilerParams(dimension_semantics=("arbitrary",) * len(grid), vmem_limit_bytes=VMEM_LIMIT)
    if exchange is None:
        outs = pl.pallas_call(body, name=name, grid=grid, in_specs=in_specs, out_specs=out_specs, out_shape=out_shape, scratch_shapes=list(scratch_shapes), compiler_params=params)(*operands)
        return outs, []
    ex = exchange
    n_in, n_out, n_scr = len(in_specs), len(out_specs), len(scratch_shapes)
    x_in, x_out = len(ex.ins), len(ex.out_shapes)

    def hosted(*refs):
        ins, x_ins = refs[:n_in], refs[n_in : n_in + x_in]
        outs, x_outs = refs[n_in + x_in : n_in + x_in + n_out], refs[n_in + x_in + n_out : n_in + x_in + n_out + x_out]
        rest = refs[n_in + x_in + n_out + x_out :]
        parts = (x_ins, x_outs, rest[n_scr], rest[n_scr + 1])
        ids = [pl.program_id(d) for d in range(len(grid))]
        first = _all([i == 0 for i in ids])
        last = _all([i == g - 1 for i, g in zip(ids, grid)])

        @pl.when(first)
        def _():
            ex.start(*parts)

        body(*ins, *outs, *rest[:n_scr])

        if ex.mid_step is not None:

            @pl.when(ids[0] == ex.mid_step)
            def _():
                ex.mid(*parts)

        @pl.when(last)
        def _():
            ex.finish(*parts)

    outs = pl.pallas_call(
        hosted,
        name=name,
        grid=grid,
        in_specs=list(in_specs) + [ANY] * x_in,
        out_specs=list(out_specs) + [ANY] * x_out,
        out_shape=list(out_shape) + ex.out_shapes,
        scratch_shapes=list(scratch_shapes) + [pltpu.SemaphoreType.DMA((ex.n_sems,)), pltpu.SemaphoreType.DMA((ex.n_sems,))],
        input_output_aliases={n_in + i: n_out + o for i, o in ex.aliases.items()},
        compiler_params=params,
    )(*operands, *ex.ins)
    return outs[:n_out], outs[n_out:]


def _all(conds):
    out = conds[0]
    for c in conds[1:]:
        out = jnp.logical_and(out, c)
    return out


def _row_tile(rows):
    if rows <= 512:
        return rows
    for t in (512, 352, 256, 176, 128, 112, 64, 32, 16, 8):
        if rows % t == 0:
            return t
    return rows


def _add_own_half(view, got, place, name):
    _, _, r, c = view.shape
    tr = _row_tile(r)

    def body(place_ref, v_ref, g_ref, o_ref, w_ref):
        s = v_ref[...] + g_ref[...]
        w_ref[...] = s.astype(w_ref.dtype)

        @pl.when(pl.program_id(1) == place_ref[0])
        def _():
            o_ref[...] = s

    blk = pl.BlockSpec((None, tr, c), lambda i, k, pr: (k, i, 0))
    return pl.pallas_call(
        body,
        name=name,
        out_shape=[jax.ShapeDtypeStruct((r, c), F32), jax.ShapeDtypeStruct((4, r, c), WIRE_DTYPE)],
        grid_spec=pltpu.PrefetchScalarGridSpec(
            num_scalar_prefetch=1,
            grid=(r // tr, 4),
            in_specs=[pl.BlockSpec((None, None, tr, c), lambda i, k, pr: (k, pr[1], i, 0)), blk],
            out_specs=[pl.BlockSpec((tr, c), lambda i, k, pr: (i, 0)), blk],
        ),
        compiler_params=pltpu.CompilerParams(dimension_semantics=("arbitrary", "arbitrary"), vmem_limit_bytes=VMEM_LIMIT),
    )(place, view, got)


def _add_chips(part, got, place, name):
    r, c = part.shape
    tr = _row_tile(r)

    def body(place_ref, p_ref, g_ref, o_ref):
        o_ref[...] = ((p_ref[...] + g_ref[0].astype(F32)) + g_ref[1].astype(F32)) + g_ref[2].astype(F32)

    return pl.pallas_call(
        body,
        name=name,
        out_shape=jax.ShapeDtypeStruct((2, r, c), F32),
        grid_spec=pltpu.PrefetchScalarGridSpec(
            num_scalar_prefetch=1,
            grid=(r // tr,),
            in_specs=[pl.BlockSpec((tr, c), lambda i, pr: (i, 0)), pl.BlockSpec((3, tr, c), lambda i, pr: (0, i, 0))],
            out_specs=pl.BlockSpec((None, tr, c), lambda i, pr: (pr[1], i, 0)),
        ),
        compiler_params=pltpu.CompilerParams(dimension_semantics=("arbitrary",), vmem_limit_bytes=VMEM_LIMIT),
    )(place, part, got)


def _adamw_update(w_ref, g_ref, m_ref, v_ref, d_ref, nm_ref, nv_ref):
    g = g_ref[...]
    nm = ADAM_B1 * m_ref[...] + (1.0 - ADAM_B1) * g
    nv = ADAM_B2 * v_ref[...] + (1.0 - ADAM_B2) * (g * g)
    m_hat = nm / (1.0 - ADAM_B1**ADAM_STEP)
    v_hat = nv / (1.0 - ADAM_B2**ADAM_STEP)
    d_ref[...] = -ADAM_LR * (m_hat / (jnp.sqrt(v_hat) + ADAM_EPS) + ADAM_WD * w_ref[...])
    nm_ref[...] = nm
    nv_ref[...] = nv


def _adamw(w, g, m, v, name):
    r, c = w.shape
    tr = _row_tile(r)
    spec = pl.BlockSpec((tr, c), lambda i: (i, 0))
    outs, _ = _pallas(_adamw_update, name=name, grid=(r // tr,), in_specs=[spec] * 4, out_specs=[spec] * 3, out_shape=[jax.ShapeDtypeStruct((r, c), F32)] * 3, operands=(w, g, m, v))
    return outs


def _adamw_many(ws, gs, ms, vs, name):
    n = len(ws)

    def body(*refs):
        for k in range(n):
            _adamw_update(*(refs[j * n + k] for j in range(7)))

    outs = pl.pallas_call(
        body,
        name=name,
        out_shape=[jax.ShapeDtypeStruct(w.shape, F32) for w in ws] * 3,
        in_specs=[pl.BlockSpec(memory_space=pltpu.VMEM)] * (4 * n),
        out_specs=[pl.BlockSpec(memory_space=pltpu.VMEM)] * (3 * n),
    )(*ws, *gs, *ms, *vs)
    return outs[:n], outs[n : 2 * n], outs[2 * n :]


def _load_weights(pairs, sems):
    @pl.when(pl.program_id(0) == 0)
    def _():
        cps = [pltpu.make_async_copy(src, dst, sems.at[k]) for k, (src, dst) in enumerate(pairs)]
        for cp in cps:
            cp.start()
        for cp in cps:
            cp.wait()


def _ffn_chunks(F):
    bounds = list(range(0, F, FFN_CHUNK)) + [F]
    return list(zip(bounds[:-1], bounds[1:]))


def _shifted_specs(tm, cols):
    per = tm // N_META
    return [_row_spec(tm, cols), pl.BlockSpec((N_META, cols), lambda i: (jnp.maximum(i * per - 1, 0), 0))]


def _shifted_tile(cur_ref, before_ref, tm):
    return jnp.concatenate([before_ref[...], cur_ref[0 : tm - N_META, :]], axis=0)


def _tokens_tile(cur_ref, before_ref, meta_ref, tm, tile_0):
    first = jnp.where(tile_0, meta_ref[...], before_ref[...])
    return jnp.concatenate([first, cur_ref[0 : tm - N_META, :]], axis=0)


def _ffn_fwd_loss(h, target, g_pre, g_post, wg, wu, wd, name):
    T, D = h.shape
    F = wg.shape[0]
    tm = FFN_TILE

    def body(h_ref, t_ref, tb_ref, gpre_ref, gpost_ref, wg_hbm, wu_hbm, wd_hbm, dy_ref, sq_ref, f_ref, ga_ref, si_ref, s_ref, n_ref, wg_v, wu_v, wd_v, sems):
        i = pl.program_id(0)
        _load_weights([(wg_hbm, wg_v), (wu_hbm, wu_v), (wd_hbm, wd_v)], sems)
        hh = h_ref[...]
        n = (hh * _rms_stat(hh) * gpre_ref[...]).astype(MXU_DTYPE)
        n_ref[...] = n.astype(n_ref.dtype)
        f = jnp.zeros((tm, D), F32)
        for lo, hi in _ffn_chunks(F):
            a = _mm_nt(n, wg_v[lo:hi, :])
            b = _mm_nt(n, wu_v[lo:hi, :])
            sg = _sigmoid(a)
            si = a * sg
            s = (si * b).astype(MXU_DTYPE)
            ga_ref[:, lo:hi] = (b * (sg * (1.0 + a * (1.0 - sg)))).astype(ga_ref.dtype)
            si_ref[:, lo:hi] = si.astype(si_ref.dtype)
            s_ref[:, lo:hi] = s.astype(s_ref.dtype)
            f = f + _mm(s, wd_v[lo:hi, :])
        f_ref[...] = f
        y = hh + 0.5 * (f * _rms_stat(f) * gpost_ref[...])
        rows = i * tm + lax.broadcasted_iota(jnp.int32, (tm, D), 0)
        err = jnp.where(rows >= N_META, y - _shifted_tile(t_ref, tb_ref, tm), 0.0)
        dy_ref[...] = err * (1.0 / D)
        _acc(sq_ref, jnp.sum(err * err, axis=0, keepdims=True), i == 0)

    tok = jax.ShapeDtypeStruct((T, D), F32)
    act = jax.ShapeDtypeStruct((T, F), MXU_DTYPE)
    outs, _ = _pallas(
        body,
        name=name,
        grid=(T // tm,),
        out_shape=[tok, jax.ShapeDtypeStruct((1, D), F32), tok, act, act, act, jax.ShapeDtypeStruct((T, D), MXU_DTYPE)],
        in_specs=[_row_spec(tm, D)] + _shifted_specs(tm, D) + [_full_spec((1, D)), _full_spec((1, D)), ANY, ANY, ANY],
        out_specs=[_row_spec(tm, D), _full_spec((1, D)), _row_spec(tm, D), _row_spec(tm, F), _row_spec(tm, F), _row_spec(tm, F), _row_spec(tm, D)],
        scratch_shapes=[pltpu.VMEM(wg.shape, wg.dtype), pltpu.VMEM(wu.shape, wu.dtype), pltpu.VMEM(wd.shape, wd.dtype), pltpu.SemaphoreType.DMA((3,))],
        operands=(h, target, target, g_pre, g_post, wg, wu, wd),
    )
    return outs


def _ffn_gate_up(x, meta, g_pre, wg, wu, name, exchange=None):
    D = x.shape[1]
    T = x.shape[0] + N_META
    F = wg.shape[0]
    tm = FFN_TILE

    def body(x_ref, xb_ref, meta_ref, gpre_ref, wg_hbm, wu_hbm, ga_ref, si_ref, s_ref, n_ref, wg_v, wu_v, sems):
        _load_weights([(wg_hbm, wg_v), (wu_hbm, wu_v)], sems)
        hh = _tokens_tile(x_ref, xb_ref, meta_ref, tm, pl.program_id(0) == 0)
        n = (hh * _rms_stat(hh) * gpre_ref[...]).astype(MXU_DTYPE)
        n_ref[...] = n.astype(n_ref.dtype)
        for lo, hi in _ffn_chunks(F):
            a = _mm_nt(n, wg_v[lo:hi, :])
            b = _mm_nt(n, wu_v[lo:hi, :])
            sg = _sigmoid(a)
            si = a * sg
            ga_ref[:, lo:hi] = (b * (sg * (1.0 + a * (1.0 - sg)))).astype(ga_ref.dtype)
            si_ref[:, lo:hi] = si.astype(si_ref.dtype)
            s_ref[:, lo:hi] = (si * b).astype(s_ref.dtype)

    act = jax.ShapeDtypeStruct((T, F), MXU_DTYPE)
    return _pallas(
        body,
        name=name,
        grid=(T // tm,),
        out_shape=[act, act, act, jax.ShapeDtypeStruct((T, D), MXU_DTYPE)],
        in_specs=_shifted_specs(tm, D) + [_full_spec((N_META, D)), _full_spec((1, D)), ANY, ANY],
        out_specs=[_row_spec(tm, F), _row_spec(tm, F), _row_spec(tm, F), _row_spec(tm, D)],
        scratch_shapes=[pltpu.VMEM(wg.shape, wg.dtype), pltpu.VMEM(wu.shape, wu.dtype), pltpu.SemaphoreType.DMA((2,))],
        operands=(x, x, meta, g_pre, wg, wu),
        exchange=exchange,
    )


def _ffn_down(x, meta, s, g_post, wd, name, exchange=None):
    D = x.shape[1]
    T = x.shape[0] + N_META
    F = wd.shape[0]
    tm = FFN_TILE

    def body(x_ref, xb_ref, meta_ref, s_ref, gpost_ref, wd_hbm, hout_ref, f_ref, wd_v, sems):
        _load_weights([(wd_hbm, wd_v)], sems)
        f = _mm(s_ref[...], wd_v[...])
        f_ref[...] = f
        hout_ref[...] = _tokens_tile(x_ref, xb_ref, meta_ref, tm, pl.program_id(0) == 0) + 0.5 * (f * _rms_stat(f) * gpost_ref[...])

    tok = jax.ShapeDtypeStruct((T, D), F32)
    return _pallas(
        body,
        name=name,
        grid=(T // tm,),
        out_shape=[tok, tok],
        in_specs=_shifted_specs(tm, D) + [_full_spec((N_META, D)), _row_spec(tm, F), _full_spec((1, D)), ANY],
        out_specs=[_row_spec(tm, D), _row_spec(tm, D)],
        scratch_shapes=[pltpu.VMEM(wd.shape, wd.dtype), pltpu.SemaphoreType.DMA((1,))],
        operands=(x, x, meta, s, g_post, wd),
        exchange=exchange,
    )


def _ffn_bwd_down(dh, f, ga, si, g_post, wd, name, exchange=None):
    T, D = dh.shape
    F = wd.shape[0]
    tm = FFN_TILE

    def body(dh_ref, f_ref, ga_ref, si_ref, gpost_ref, wd_hbm, da_ref, db_ref, df_ref, dg_ref, wd_v, sems):
        _load_weights([(wd_hbm, wd_v)], sems)
        df, dg = _rms_bwd(f_ref[...], gpost_ref[...], 0.5 * dh_ref[...])
        _acc(dg_ref, dg, pl.program_id(0) == 0)
        dfb = df.astype(MXU_DTYPE)
        df_ref[...] = dfb.astype(df_ref.dtype)
        for lo, hi in _ffn_chunks(F):
            ds = _mm_nt(dfb, wd_v[lo:hi, :])
            da_ref[:, lo:hi] = (ds * ga_ref[:, lo:hi].astype(F32)).astype(da_ref.dtype)
            db_ref[:, lo:hi] = (ds * si_ref[:, lo:hi].astype(F32)).astype(db_ref.dtype)

    act = jax.ShapeDtypeStruct((T, F), MXU_DTYPE)
    return _pallas(
        body,
        name=name,
        grid=(T // tm,),
        out_shape=[act, act, jax.ShapeDtypeStruct((T, D), MXU_DTYPE), jax.ShapeDtypeStruct((1, D), F32)],
        in_specs=[_row_spec(tm, D), _row_spec(tm, D), _row_spec(tm, F), _row_spec(tm, F), _full_spec((1, D)), ANY],
        out_specs=[_row_spec(tm, F), _row_spec(tm, F), _row_spec(tm, D), _full_spec((1, D))],
        scratch_shapes=[pltpu.VMEM(wd.shape, wd.dtype), pltpu.SemaphoreType.DMA((1,))],
        operands=(dh, f, ga, si, g_post, wd),
        exchange=exchange,
    )


def _ffn_bwd_up(da, db, x, meta, dh, g_pre, wg, wu, name, exchange=None):
    D = x.shape[1]
    T = x.shape[0] + N_META
    F = wg.shape[0]
    tm = FFN_TILE
    nt = T // tm
    per = tm // N_META
    tile = lambda i: jnp.minimum(i, nt - 1)

    def body(da_ref, db_ref, x_ref, xb_ref, meta_ref, dh_ref, gpre_ref, wg_hbm, wu_hbm, dx_ref, dmeta_ref, dg_ref, wg_v, wu_v, sems, held):
        i = pl.program_id(0)
        _load_weights([(wg_hbm, wg_v), (wu_hbm, wu_v)], sems)

        @pl.when(i < nt)
        def _():
            dn = jnp.zeros((tm, D), F32)
            for lo, hi in _ffn_chunks(F):
                dn = dn + _mm(da_ref[:, lo:hi], wg_v[lo:hi, :]) + _mm(db_ref[:, lo:hi], wu_v[lo:hi, :])
            dx, dg = _rms_bwd(_tokens_tile(x_ref, xb_ref, meta_ref, tm, i == 0), gpre_ref[...], dn)
            _acc(dg_ref, dg, i == 0)
            dh_in = dh_ref[...] + dx

            @pl.when(i == 0)
            def _():
                dmeta_ref[...] = dh_in[0:N_META, :]

            @pl.when(i > 0)
            def _():
                dx_ref[...] = jnp.concatenate([held[...], dh_in[0:N_META, :]], axis=0)

            held[...] = dh_in[N_META:, :]

        @pl.when(i == nt)
        def _():
            dx_ref[0 : tm - N_META, :] = held[...]

    rows = lambda cols: pl.BlockSpec((tm, cols), lambda i: (tile(i), 0))
    return _pallas(
        body,
        name=name,
        grid=(nt + 1,),
        out_shape=[jax.ShapeDtypeStruct((T - N_META, D), F32), jax.ShapeDtypeStruct((N_META, D), F32), jax.ShapeDtypeStruct((1, D), F32)],
        in_specs=[rows(F), rows(F), rows(D), pl.BlockSpec((N_META, D), lambda i: (jnp.maximum(tile(i) * per - 1, 0), 0)), _full_spec((N_META, D)), rows(D), _full_spec((1, D)), ANY, ANY],
        out_specs=[pl.BlockSpec((tm, D), lambda i: (jnp.maximum(i - 1, 0), 0)), _full_spec((N_META, D)), _full_spec((1, D))],
        scratch_shapes=[pltpu.VMEM(wg.shape, wg.dtype), pltpu.VMEM(wu.shape, wu.dtype), pltpu.SemaphoreType.DMA((2,)), pltpu.VMEM((tm - N_META, D), F32)],
        operands=(da, db, x, x, meta, dh, g_pre, wg, wu),
        exchange=exchange,
    )


def _ffn_bwd(dh, f, ga, si, h, g_post, g_pre, wg, wu, wd, name):
    T, D = dh.shape
    F = wd.shape[0]
    tm = FFN_TILE

    def body(dh_ref, f_ref, ga_ref, si_ref, h_ref, gpost_ref, gpre_ref, wg_hbm, wu_hbm, wd_hbm, da_ref, db_ref, df_ref, dhin_ref, dgpost_ref, dgpre_ref, wg_v, wu_v, wd_v, sems):
        first = pl.program_id(0) == 0
        _load_weights([(wg_hbm, wg_v), (wu_hbm, wu_v), (wd_hbm, wd_v)], sems)
        dh = dh_ref[...]
        df, dg = _rms_bwd(f_ref[...], gpost_ref[...], 0.5 * dh)
        _acc(dgpost_ref, dg, first)
        dfb = df.astype(MXU_DTYPE)
        df_ref[...] = dfb.astype(df_ref.dtype)
        dn = jnp.zeros((tm, D), F32)
        for lo, hi in _ffn_chunks(F):
            ds = _mm_nt(dfb, wd_v[lo:hi, :])
            da = (ds * ga_ref[:, lo:hi].astype(F32)).astype(MXU_DTYPE)
            db = (ds * si_ref[:, lo:hi].astype(F32)).astype(MXU_DTYPE)
            da_ref[:, lo:hi] = da.astype(da_ref.dtype)
            db_ref[:, lo:hi] = db.astype(db_ref.dtype)
            dn = dn + _mm(da, wg_v[lo:hi, :]) + _mm(db, wu_v[lo:hi, :])
        dx, dg = _rms_bwd(h_ref[...], gpre_ref[...], dn)
        _acc(dgpre_ref, dg, first)
        dhin_ref[...] = dh + dx

    act = jax.ShapeDtypeStruct((T, F), MXU_DTYPE)
    vec = jax.ShapeDtypeStruct((1, D), F32)
    outs, _ = _pallas(
        body,
        name=name,
        grid=(T // tm,),
        out_shape=[act, act, jax.ShapeDtypeStruct((T, D), MXU_DTYPE), jax.ShapeDtypeStruct((T, D), F32), vec, vec],
        in_specs=[_row_spec(tm, D), _row_spec(tm, D), _row_spec(tm, F), _row_spec(tm, F), _row_spec(tm, D), _full_spec((1, D)), _full_spec((1, D)), ANY, ANY, ANY],
        out_specs=[_row_spec(tm, F), _row_spec(tm, F), _row_spec(tm, D), _row_spec(tm, D), _full_spec((1, D)), _full_spec((1, D))],
        scratch_shapes=[pltpu.VMEM(wg.shape, wg.dtype), pltpu.VMEM(wu.shape, wu.dtype), pltpu.VMEM(wd.shape, wd.dtype), pltpu.SemaphoreType.DMA((3,))],
        operands=(dh, f, ga, si, h, g_post, g_pre, wg, wu, wd),
    )
    return outs


def _token_tile(T):
    for t in (912, 864, 432):
        if T % t == 0:
            return t
    raise ValueError(f"no token tile for {T} rows")


def _tn_matmul(xm, ym, name, exchange=None):
    T, M = xm.shape
    N = ym.shape[1]
    if (T - N_META) % TN_TILE:
        tk = _token_tile(T)

        def body(x_ref, y_ref, o_ref):
            _acc(o_ref, _mm_tn(x_ref[...], y_ref[...]), pl.program_id(0) == 0)

        grid, operands = (T // tk,), (xm, ym)
        in_specs = [pl.BlockSpec((tk, M), lambda k: (k, 0)), pl.BlockSpec((tk, N), lambda k: (k, 0))]
    else:
        tk = TN_TILE

        def body(x_ref, y_ref, xh_ref, yh_ref, o_ref):
            prod = _mm_tn(x_ref[...], y_ref[...])

            @pl.when(pl.program_id(0) == 0)
            def _():
                o_ref[...] = prod + _mm_tn(xh_ref[...], yh_ref[...])

            @pl.when(pl.program_id(0) > 0)
            def _():
                o_ref[...] += prod

        grid, operands = ((T - N_META) // tk,), (xm, ym, xm, ym)
        start = lambda k: (pl.multiple_of(N_META + k * tk, N_META), 0)
        in_specs = [pl.BlockSpec((pl.Element(tk), pl.Element(M)), start), pl.BlockSpec((pl.Element(tk), pl.Element(N)), start),
                    pl.BlockSpec((N_META, M), lambda k: (0, 0)), pl.BlockSpec((N_META, N), lambda k: (0, 0))]

    (out,), x_outs = _pallas(
        body,
        name=name,
        grid=grid,
        out_shape=[jax.ShapeDtypeStruct((M, N), F32)],
        in_specs=in_specs,
        out_specs=[_full_spec((M, N))],
        operands=operands,
        exchange=exchange,
    )
    return out, x_outs


TAB_A, TAB_AS1, TAB_AS2, TAB_AS4, TAB_JF, TAB_JB = 0, 2, 4, 6, 8, 10


def _scan_inplace(zr, zi, tabs, pows, car_r, car_i, seg, reverse, base=0):
    n_slabs = zr.shape[0]
    sgn = -1.0 if reverse else 1.0
    row = lax.broadcasted_iota(jnp.int32, (SUBLANES, LANES), 0)

    def cmul(pr, pi, xr, xi):
        return pr * xr - pi * xi, pr * xi + pi * xr

    for k0 in range(0, n_slabs, SLAB_GROUP):
        slabs = range(k0, min(k0 + SLAB_GROUP, n_slabs))
        ar = [tabs[TAB_A, k] for k in slabs]
        ai = [sgn * tabs[TAB_A + 1, k] for k in slabs]

        def first_pass(t, carry):
            r = (seg - 1 - t) if reverse else t
            out = []
            for q, k in enumerate(slabs):
                xr, xi = carry[2 * q], carry[2 * q + 1]
                pr, pi = cmul(ar[q], ai[q], xr, xi)
                nr = pr + zr[k, pl.ds(base + r, SUBLANES, stride=seg), :]
                ni = pi + zi[k, pl.ds(base + r, SUBLANES, stride=seg), :]
                zr[k, pl.ds(base + r, SUBLANES, stride=seg), :] = nr
                zi[k, pl.ds(base + r, SUBLANES, stride=seg), :] = ni
                out += [nr, ni]
            return tuple(out)

        ends = lax.fori_loop(0, seg, first_pass, tuple(jnp.zeros((SUBLANES, LANES), F32) for _ in range(2 * len(slabs))))

        incoming = []
        for q, k in enumerate(slabs):
            fr, fi = ends[2 * q], ends[2 * q + 1]
            for d, tab in ((1, TAB_AS1), (2, TAB_AS2), (4, TAB_AS4)):
                shift, keep = (SUBLANES - d, row < SUBLANES - d) if reverse else (d, row >= d)
                sr = jnp.where(keep, pltpu.roll(fr, shift, 0), 0.0)
                si = jnp.where(keep, pltpu.roll(fi, shift, 0), 0.0)
                pr, pi = cmul(tabs[tab, k], sgn * tabs[tab + 1, k], sr, si)
                fr, fi = fr + pr, fi + pi
            cr, ci = car_r[k], car_i[k]
            jtab = TAB_JB if reverse else TAB_JF
            pr, pi = cmul(tabs[jtab, k], sgn * tabs[jtab + 1, k], cr, ci)
            er, ei = fr + pr, fi + pi
            if reverse:
                inr = jnp.where(row < SUBLANES - 1, pltpu.roll(er, SUBLANES - 1, 0), cr)
                ini = jnp.where(row < SUBLANES - 1, pltpu.roll(ei, SUBLANES - 1, 0), ci)
                car_r[k] = jnp.broadcast_to(er[0:1, :], (SUBLANES, LANES))
                car_i[k] = jnp.broadcast_to(ei[0:1, :], (SUBLANES, LANES))
            else:
                inr = jnp.where(row >= 1, pltpu.roll(er, 1, 0), cr)
                ini = jnp.where(row >= 1, pltpu.roll(ei, 1, 0), ci)
                car_r[k] = jnp.broadcast_to(er[SUBLANES - 1 : SUBLANES, :], (SUBLANES, LANES))
                car_i[k] = jnp.broadcast_to(ei[SUBLANES - 1 : SUBLANES, :], (SUBLANES, LANES))
            incoming += [inr, ini]

        def second_pass(r, _):
            p = (seg - 1 - r) if reverse else r
            for q, k in enumerate(slabs):
                pr, pi = cmul(pows[0, k, p], sgn * pows[1, k, p], incoming[2 * q], incoming[2 * q + 1])
                zr[k, pl.ds(base + r, SUBLANES, stride=seg), :] = zr[k, pl.ds(base + r, SUBLANES, stride=seg), :] + pr
                zi[k, pl.ds(base + r, SUBLANES, stride=seg), :] = zi[k, pl.ds(base + r, SUBLANES, stride=seg), :] + pi
            return 0

        lax.fori_loop(0, seg, second_pass, 0)


def _slabs_to_cols(ref, k0, n):
    return jnp.concatenate([ref[k0 + q] for q in range(n)], axis=1)


def _window_sum(ext, doublings, forward):
    rows = ext.shape[0]
    s = ext
    for k in range(doublings):
        s = s + pltpu.roll(s, (1 << k) if forward else rows - (1 << k), 0)
    return s


def _mix_fwd(h1, g_pre, g_so, g_po, g_post, dskip, pscale, win, wout, bbre, bbim, ccre, ccim, wgv, wgg, pw, tabs, pows, name, exchange=None):
    T, D = h1.shape
    W = D // 2
    tm = MIX_SUBTILES * MIX_TILE
    seg = MIX_TILE // SUBLANES
    n_slabs = tabs.shape[1]
    nch, cch, sch = bbre.shape
    spc = sch // LANES
    pg = W // len(POOL_WINDOWS)

    def body(h_ref, gpre_ref, gso_ref, gpo_ref, gpost_ref, dskip_ref, pscale_ref, win_ref, wout_ref, bbre_ref, bbim_ref, ccre_ref, ccim_ref, wgv_ref, wgg_ref, pw_ref, tabs_ref, pows_ref,
             proj_ref, xr_ref, xi_ref, y_ref, pooled_ref, mixed_ref, h2_ref, n2_ref, cat_ref, car_r, car_i, halo):
        i = pl.program_id(0)

        @pl.when(i == 0)
        def _():
            car_r[...] = jnp.zeros_like(car_r)
            car_i[...] = jnp.zeros_like(car_i)
            halo[...] = jnp.zeros_like(halo)

        hh = h_ref[...]
        n2 = (hh * _rms_stat(hh) * gpre_ref[...]).astype(MXU_DTYPE)
        n2_ref[...] = n2.astype(n2_ref.dtype)
        proj = _mm(n2, win_ref[...])
        proj_ref[...] = proj
        us, up = proj[:, :W], proj[:, W:]

        for c in range(nch):
            uc = us[:, c * cch : (c + 1) * cch].astype(MXU_DTYPE)
            bur, bui = _mm(uc, bbre_ref[c]), _mm(uc, bbim_ref[c])
            for q in range(spc):
                xr_ref[c * spc + q] = bur[:, q * LANES : (q + 1) * LANES]
                xi_ref[c * spc + q] = bui[:, q * LANES : (q + 1) * LANES]
        for sub in range(MIX_SUBTILES):
            _scan_inplace(xr_ref, xi_ref, tabs_ref, pows_ref, car_r, car_i, seg, reverse=False, base=sub * MIX_TILE)
        ys = []
        for c in range(nch):
            ys.append(_mm(_slabs_to_cols(xr_ref, c * spc, spc), ccre_ref[c]) - _mm(_slabs_to_cols(xi_ref, c * spc, spc), ccim_ref[c]))
        y = jnp.concatenate(ys, axis=1) + dskip_ref[...] * us
        y_ref[...] = y
        ge = _gelu(y).astype(MXU_DTYPE)
        zv = jnp.concatenate([_mm(ge[:, c * cch : (c + 1) * cch], wgv_ref[c]) for c in range(nch)], axis=1)
        zg = jnp.concatenate([_mm(ge[:, c * cch : (c + 1) * cch], wgg_ref[c]) for c in range(nch)], axis=1)
        out = zv * _sigmoid(zg)
        cat_s = out * _rms_stat(out) * gso_ref[...]

        ext = jnp.concatenate([halo[...], up], axis=0)
        halo[...] = up[tm - POOL_HALO :, :]
        t1 = (i * tm + 1 + lax.broadcasted_iota(jnp.int32, (tm, pg), 0)).astype(F32)
        pooled, pms = [], []
        for g, w in enumerate(POOL_WINDOWS):
            col = ext[:, g * pg : (g + 1) * pg]
            win_sum = _window_sum(col, g + 1, True)[POOL_HALO:, :]
            pooled_g = win_sum / jnp.minimum(t1, float(w)) - up[:, g * pg : (g + 1) * pg]
            pooled.append(pooled_g)
            pms.append(_mm(pooled_g, pw_ref[g]))
        pooled_ref[...] = jnp.concatenate(pooled, axis=1)
        yp = jnp.concatenate(pms, axis=1) * pscale_ref[...]
        cat_p = yp * _rms_stat(yp) * gpo_ref[...]

        cat = jnp.concatenate([cat_s, cat_p], axis=1).astype(MXU_DTYPE)
        cat_ref[...] = cat.astype(cat_ref.dtype)
        mixed = _mm(cat, wout_ref[...])
        mixed_ref[...] = mixed
        h2_ref[...] = hh + mixed * _rms_stat(mixed) * gpost_ref[...]

    tok = lambda cols, dt=F32: jax.ShapeDtypeStruct((T, cols), dt)
    slab_spec = pl.BlockSpec((n_slabs, tm, LANES), lambda i: (0, i, 0))
    operands = (h1, g_pre, g_so, g_po, g_post, dskip, pscale, win, wout, bbre, bbim, ccre, ccim, wgv, wgg, pw, tabs, pows)
    return _pallas(
        body,
        name=name,
        grid=(T // tm,),
        out_shape=[tok(D), jax.ShapeDtypeStruct((n_slabs, T, LANES), F32), jax.ShapeDtypeStruct((n_slabs, T, LANES), F32), tok(W), tok(W), tok(D), tok(D), tok(D, MXU_DTYPE), tok(D, MXU_DTYPE)],
        in_specs=[_row_spec(tm, D)] + [_full_spec(o.shape, single=True) for o in operands[1:]],
        out_specs=[_row_spec(tm, D), slab_spec, slab_spec, _row_spec(tm, W), _row_spec(tm, W), _row_spec(tm, D), _row_spec(tm, D), _row_spec(tm, D), _row_spec(tm, D)],
        scratch_shapes=[pltpu.VMEM((n_slabs, SUBLANES, LANES), F32), pltpu.VMEM((n_slabs, SUBLANES, LANES), F32), pltpu.VMEM((POOL_HALO, W), F32)],
        operands=operands,
        exchange=exchange,
    )


def _mix_bwd_heads(dh2, mixed, y, pooled, proj, g_so, g_po, g_post, pscale, wout, wgv, wgg, pw, name, exchange=None):
    T, D = dh2.shape
    W = D // 2
    tm = _token_tile(T)
    nch, cch, _ = wgv.shape
    ng, pg, _ = pw.shape

    def body(dh2_ref, mixed_ref, y_ref, pooled_ref, us_ref, gso_ref, gpo_ref, gpost_ref, pscale_ref, wout_ref, wgv_ref, wgg_ref, pw_ref,
             dy_ref, dpooled_ref, dmixed_ref, dgpost_ref, dgso_ref, dgpo_ref, dd_ref, dscale_ref, dwgv_ref, dwgg_ref, dpw_ref):
        first = pl.program_id(0) == 0
        dmixed, dgpost = _rms_bwd(mixed_ref[...], gpost_ref[...], dh2_ref[...])
        _acc(dgpost_ref, dgpost, first)
        dmb = dmixed.astype(MXU_DTYPE)
        dmixed_ref[...] = dmb.astype(dmixed_ref.dtype)
        dcat = _mm_nt(dmb, wout_ref[...])
        dcs, dcp = dcat[:, :W], dcat[:, W:]

        y = y_ref[...]
        ge = _gelu(y).astype(MXU_DTYPE)
        zv = jnp.concatenate([_mm(ge[:, c * cch : (c + 1) * cch], wgv_ref[c]) for c in range(nch)], axis=1)
        zg = jnp.concatenate([_mm(ge[:, c * cch : (c + 1) * cch], wgg_ref[c]) for c in range(nch)], axis=1)
        sg = _sigmoid(zg)
        dout, dgso = _rms_bwd(zv * sg, gso_ref[...], dcs)
        _acc(dgso_ref, dgso, first)
        dzv = (dout * sg).astype(MXU_DTYPE)
        dzg = (dout * zv * sg * (1.0 - sg)).astype(MXU_DTYPE)
        dges = []
        for c in range(nch):
            cs = slice(c * cch, (c + 1) * cch)
            dges.append(_mm_nt(dzv[:, cs], wgv_ref[c]) + _mm_nt(dzg[:, cs], wgg_ref[c]))
            _acc(dwgv_ref.at[c], _mm_tn(ge[:, cs], dzv[:, cs]), first)
            _acc(dwgg_ref.at[c], _mm_tn(ge[:, cs], dzg[:, cs]), first)
        dy = jnp.concatenate(dges, axis=1) * _gelu_grad(y)
        dy_ref[...] = dy
        _acc(dd_ref, jnp.sum(dy * us_ref[...], axis=0, keepdims=True), first)

        pooled_b = pooled_ref[...].astype(MXU_DTYPE)
        pm = jnp.concatenate([_mm(pooled_b[:, g * pg : (g + 1) * pg], pw_ref[g]) for g in range(ng)], axis=1)
        dyp, dgpo = _rms_bwd(pm * pscale_ref[...], gpo_ref[...], dcp)
        _acc(dgpo_ref, dgpo, first)
        _acc(dscale_ref, jnp.sum(dyp * pm, axis=0, keepdims=True), first)
        dpm = (dyp * pscale_ref[...]).astype(MXU_DTYPE)
        dps = []
        for g in range(ng):
            gs = slice(g * pg, (g + 1) * pg)
            dps.append(_mm_nt(dpm[:, gs], pw_ref[g]))
            _acc(dpw_ref.at[g], _mm_tn(pooled_b[:, gs], dpm[:, gs]), first)
        dpooled_ref[...] = jnp.concatenate(dps, axis=1)

    vec = lambda n: jax.ShapeDtypeStruct((1, n), F32)
    operands = (dh2, mixed, y, pooled, proj, g_so, g_po, g_post, pscale, wout, wgv, wgg, pw)
    return _pallas(
        body,
        name=name,
        grid=(T // tm,),
        out_shape=[jax.ShapeDtypeStruct((T, W), F32), jax.ShapeDtypeStruct((T, W), F32), jax.ShapeDtypeStruct((T, D), MXU_DTYPE), vec(D), vec(W), vec(W), vec(W), vec(W),
                   jax.ShapeDtypeStruct(wgv.shape, F32), jax.ShapeDtypeStruct(wgg.shape, F32), jax.ShapeDtypeStruct(pw.shape, F32)],
        in_specs=[_row_spec(tm, D), _row_spec(tm, D), _row_spec(tm, W), _row_spec(tm, W), _row_spec(tm, W)] + [_full_spec(o.shape) for o in operands[5:]],
        out_specs=[_row_spec(tm, W), _row_spec(tm, W), _row_spec(tm, D), _full_spec((1, D)), _full_spec((1, W)), _full_spec((1, W)), _full_spec((1, W)), _full_spec((1, W)),
                   _full_spec(wgv.shape), _full_spec(wgg.shape), _full_spec(pw.shape)],
        operands=operands,
        exchange=exchange,
    )


def _mix_bwd_scan(dy, dpooled, xr, xi, proj, dskip, bbre, bbim, ccre, ccim, tabs, pows, name, exchange=None):
    T, W = dy.shape
    D = 2 * W
    tm = MIX_SUBTILES * MIX_TILE
    seg = MIX_TILE // SUBLANES
    nt = T // tm
    n_slabs = tabs.shape[1]
    nch, cch, sch = bbre.shape
    spc = sch // LANES
    pg = W // len(POOL_WINDOWS)
    blocks_per_tile = tm // SUBLANES

    def body(dy_ref, dp_ref, xr_ref, xi_ref, xpr_ref, xpi_ref, us_ref, dskip_ref, bbre_ref, bbim_ref, ccre_ref, ccim_ref, tabs_ref, pows_ref,
             dproj_ref, dccre_ref, dccim_ref, dbbre_ref, dbbim_ref, dar_ref, dai_ref, lr, li, car_r, car_i, halo):
        i = pl.program_id(0)
        first = i == 0
        tile = nt - 1 - i
        row = lax.broadcasted_iota(jnp.int32, (SUBLANES, LANES), 0)

        @pl.when(first)
        def _():
            car_r[...] = jnp.zeros_like(car_r)
            car_i[...] = jnp.zeros_like(car_i)
            halo[...] = jnp.zeros_like(halo)
            dar_ref[...] = jnp.zeros_like(dar_ref)
            dai_ref[...] = jnp.zeros_like(dai_ref)

        dy = dy_ref[...]
        for c in range(nch):
            dyc = dy[:, c * cch : (c + 1) * cch]
            gr, gi = _mm_nt(dyc, ccre_ref[c]), _mm_nt(dyc, ccim_ref[c])
            for q in range(spc):
                lr[c * spc + q] = gr[:, q * LANES : (q + 1) * LANES]
                li[c * spc + q] = -gi[:, q * LANES : (q + 1) * LANES]
            _acc(dccre_ref.at[c], _mm_tn(_slabs_to_cols(xr_ref, c * spc, spc), dyc), first)
            _acc(dccim_ref.at[c], -_mm_tn(_slabs_to_cols(xi_ref, c * spc, spc), dyc), first)
        for sub in reversed(range(MIX_SUBTILES)):
            _scan_inplace(lr, li, tabs_ref, pows_ref, car_r, car_i, seg, reverse=True, base=sub * MIX_TILE)

        for sub in range(MIX_SUBTILES):
            base = sub * MIX_TILE
            for k0 in range(0, n_slabs, SLAB_GROUP):
                slabs = range(k0, min(k0 + SLAB_GROUP, n_slabs))
                init = []
                for k in slabs:
                    if sub == 0:
                        prev_r = jnp.where(tile > 0, jnp.broadcast_to(xpr_ref[k, SUBLANES - 1 : SUBLANES, :], (SUBLANES, LANES)), 0.0)
                        prev_i = jnp.where(tile > 0, jnp.broadcast_to(xpi_ref[k, SUBLANES - 1 : SUBLANES, :], (SUBLANES, LANES)), 0.0)
                    else:
                        prev_r = jnp.broadcast_to(xr_ref[k, base - 1 : base, :], (SUBLANES, LANES))
                        prev_i = jnp.broadcast_to(xi_ref[k, base - 1 : base, :], (SUBLANES, LANES))
                    x0r = jnp.where(row >= 1, pltpu.roll(xr_ref[k, pl.ds(base + seg - 1, SUBLANES, stride=seg), :], 1, 0), prev_r)
                    x0i = jnp.where(row >= 1, pltpu.roll(xi_ref[k, pl.ds(base + seg - 1, SUBLANES, stride=seg), :], 1, 0), prev_i)
                    l0r, l0i = lr[k, pl.ds(base, SUBLANES, stride=seg), :], li[k, pl.ds(base, SUBLANES, stride=seg), :]
                    init += [l0r * x0r + l0i * x0i, l0i * x0r - l0r * x0i]

                def step(r, acc, slabs=slabs, base=base):
                    out = []
                    for q, k in enumerate(slabs):
                        pr_, pi_ = xr_ref[k, pl.ds(base + r - 1, SUBLANES, stride=seg), :], xi_ref[k, pl.ds(base + r - 1, SUBLANES, stride=seg), :]
                        lr_, li_ = lr[k, pl.ds(base + r, SUBLANES, stride=seg), :], li[k, pl.ds(base + r, SUBLANES, stride=seg), :]
                        out += [acc[2 * q] + lr_ * pr_ + li_ * pi_, acc[2 * q + 1] + li_ * pr_ - lr_ * pi_]
                    return tuple(out)

                sums = lax.fori_loop(1, seg, step, tuple(init))
                for q, k in enumerate(slabs):
                    dar_ref[k] += sums[2 * q]
                    dai_ref[k] += sums[2 * q + 1]

        us = us_ref[...]
        dus = []
        for c in range(nch):
            lrc, lic = _slabs_to_cols(lr, c * spc, spc).astype(MXU_DTYPE), _slabs_to_cols(li, c * spc, spc).astype(MXU_DTYPE)
            uc = us[:, c * cch : (c + 1) * cch]
            _acc(dbbre_ref.at[c], _mm_tn(uc, lrc), first)
            _acc(dbbim_ref.at[c], _mm_tn(uc, lic), first)
            dus.append(_mm_nt(lrc, bbre_ref[c]) + _mm_nt(lic, bbim_ref[c]))
        du_s = jnp.concatenate(dus, axis=1) + dskip_ref[...] * dy

        dp = dp_ref[...]
        t1 = (tile * tm + 1 + lax.broadcasted_iota(jnp.int32, (tm, pg), 0)).astype(F32)
        dups, heads = [], []
        for g, w in enumerate(POOL_WINDOWS):
            dpg = dp[:, g * pg : (g + 1) * pg]
            qg = dpg / jnp.minimum(t1, float(w))
            ext = jnp.concatenate([qg, halo[:, g * pg : (g + 1) * pg]], axis=0)
            dups.append(_window_sum(ext, g + 1, False)[:tm, :] - dpg)
            heads.append(qg[:POOL_HALO, :])
        halo[...] = jnp.concatenate(heads, axis=1)
        dproj_ref[...] = jnp.concatenate([du_s] + dups, axis=1).astype(dproj_ref.dtype)

    rev = lambda cols: _row_spec(tm, cols, rev_n=nt)
    slab_spec = pl.BlockSpec((n_slabs, tm, LANES), lambda i: (0, nt - 1 - i, 0))
    prev_spec = pl.BlockSpec((n_slabs, SUBLANES, LANES), lambda i: (0, jnp.maximum((nt - 1 - i) * blocks_per_tile - 1, 0), 0))
    consts = (dskip, bbre, bbim, ccre, ccim, tabs, pows)
    return _pallas(
        body,
        name=name,
        grid=(nt,),
        out_shape=[jax.ShapeDtypeStruct((T, D), MXU_DTYPE), jax.ShapeDtypeStruct(ccre.shape, F32), jax.ShapeDtypeStruct(ccim.shape, F32), jax.ShapeDtypeStruct(bbre.shape, F32),
                   jax.ShapeDtypeStruct(bbim.shape, F32), jax.ShapeDtypeStruct((n_slabs, SUBLANES, LANES), F32), jax.ShapeDtypeStruct((n_slabs, SUBLANES, LANES), F32)],
        in_specs=[rev(W), rev(W), slab_spec, slab_spec, prev_spec, prev_spec, rev(W)] + [_full_spec(o.shape, single=True) for o in consts],
        out_specs=[rev(D), _full_spec(ccre.shape), _full_spec(ccim.shape), _full_spec(bbre.shape), _full_spec(bbim.shape),
                   _full_spec((n_slabs, SUBLANES, LANES)), _full_spec((n_slabs, SUBLANES, LANES))],
        scratch_shapes=[pltpu.VMEM((n_slabs, tm, LANES), F32), pltpu.VMEM((n_slabs, tm, LANES), F32), pltpu.VMEM((n_slabs, SUBLANES, LANES), F32), pltpu.VMEM((n_slabs, SUBLANES, LANES), F32),
                        pltpu.VMEM((POOL_HALO, W), F32)],
        operands=(dy, dpooled, xr, xi, xr, xi, proj, *consts),
        exchange=exchange,
    )


def _mix_bwd_in(dproj, h1, dh2, g_pre, win, name):
    T, D = h1.shape
    tm = _token_tile(T)

    def body(dproj_ref, h_ref, dh2_ref, gpre_ref, win_ref, dh1_ref, dg_ref):
        dx, dg = _rms_bwd(h_ref[...], gpre_ref[...], _mm_nt(dproj_ref[...], win_ref[...]))
        _acc(dg_ref, dg, pl.program_id(0) == 0)
        dh1_ref[...] = dh2_ref[...] + dx

    outs, _ = _pallas(
        body,
        name=name,
        grid=(T // tm,),
        out_shape=[jax.ShapeDtypeStruct((T, D), F32), jax.ShapeDtypeStruct((1, D), F32)],
        in_specs=[_row_spec(tm, D), _row_spec(tm, D), _row_spec(tm, D), _full_spec((1, D)), _full_spec(win.shape)],
        out_specs=[_row_spec(tm, D), _full_spec((1, D))],
        operands=(dproj, h1, dh2, g_pre, win),
    )
    return outs


def _discretize(lam_re, lam_im, log_dt, b_re, b_im):
    dt = jnp.exp(log_dt)[:, None]
    decay = jnp.exp(lam_re * dt)
    ang = lam_im * dt
    a_re, a_im = decay * jnp.cos(ang), decay * jnp.sin(ang)
    nr = a_re - 1.0
    den = lam_re * lam_re + lam_im * lam_im
    q_re = (nr * lam_re + a_im * lam_im) / den
    q_im = (a_im * lam_re - nr * lam_im) / den
    bb_re = q_re[..., None] * b_re - q_im[..., None] * b_im
    bb_im = q_re[..., None] * b_im + q_im[..., None] * b_re
    return a_re, a_im, bb_re, bb_im


GROUPS_PER_CHUNK = 16


def _block_diag(w, rows_first):
    G = w.shape[0]
    nch = G // GROUPS_PER_CHUNK
    if not rows_first:
        w = jnp.swapaxes(w, 1, 2)
    p, q = w.shape[1], w.shape[2]
    eye = jnp.eye(GROUPS_PER_CHUNK, dtype=w.dtype)
    out = jnp.einsum("cgpq,gk->cgpkq", w.reshape(nch, GROUPS_PER_CHUNK, p, q), eye)
    return out.reshape(nch, GROUPS_PER_CHUNK * p, GROUPS_PER_CHUNK * q)


def _block_diag_extract(m, p, q, rows_first):
    nch = m.shape[0]
    eye = jnp.eye(GROUPS_PER_CHUNK, dtype=m.dtype)
    out = jnp.einsum("cgpkq,gk->cgpq", m.reshape(nch, GROUPS_PER_CHUNK, p, GROUPS_PER_CHUNK, q), eye).reshape(nch * GROUPS_PER_CHUNK, p, q)
    return out if rows_first else jnp.swapaxes(out, 1, 2)


def _cmul(ar, ai, br, bi):
    return ar * br - ai * bi, ar * bi + ai * br


def _powers(ar, ai, count):
    pr, pi = ar[None], ai[None]
    while pr.shape[0] < count:
        nr, ni = _cmul(pr, pi, pr[-1][None], pi[-1][None])
        pr, pi = jnp.concatenate([pr, nr]), jnp.concatenate([pi, ni])
    return pr[:count], pi[:count]


def _scan_tables(a_re, a_im, seg):
    n = a_re.size
    ns = n // LANES
    ar, ai = a_re.reshape(n), a_im.reshape(n)
    pr, pi = _powers(ar, ai, seg)
    jr, ji = _powers(pr[-1], pi[-1], SUBLANES)

    def bcast(v):
        return jnp.broadcast_to(v.reshape(ns, 1, LANES), (ns, SUBLANES, LANES))

    def per_sublane(vs):
        return jnp.transpose(vs.reshape(SUBLANES, ns, LANES), (1, 0, 2))

    tabs = jnp.stack([bcast(ar), bcast(ai), bcast(jr[0]), bcast(ji[0]), bcast(jr[1]), bcast(ji[1]), bcast(jr[3]), bcast(ji[3]),
                      per_sublane(jr), per_sublane(ji), per_sublane(jr[::-1]), per_sublane(ji[::-1])])

    def rows(vs):
        return jnp.broadcast_to(jnp.transpose(vs.reshape(seg, ns, 1, LANES), (1, 0, 2, 3)), (ns, seg, SUBLANES, LANES))

    return tabs, jnp.stack([rows(pr), rows(pi)])


SMALL = ("ffn1_pre_norm", "ffn1_post_norm", "mix_pre_norm", "mix_post_norm", "ssm_lambda_re", "ssm_lambda_im", "ssm_log_dt", "ssm_b_re", "ssm_b_im", "ssm_c_re", "ssm_c_im",
         "ssm_d", "ssm_w_glu", "pool_w", "pool_scale", "ssm_out_norm", "pool_out_norm", "ffn2_pre_norm", "ffn2_post_norm")
BIG = ("ffn1_w_gate", "ffn1_w_up", "ffn1_w_down", "w_in", "w_out", "ffn2_w_gate", "ffn2_w_up", "ffn2_w_down")
ORDER = ("meta_tokens", "ffn1_pre_norm", "ffn1_post_norm", "ffn1_w_gate", "ffn1_w_up", "ffn1_w_down", "mix_pre_norm", "mix_post_norm", "w_in", "ssm_lambda_re", "ssm_lambda_im",
         "ssm_log_dt", "ssm_b_re", "ssm_b_im", "ssm_c_re", "ssm_c_im", "ssm_d", "ssm_w_glu", "pool_w", "pool_scale", "ssm_out_norm", "pool_out_norm", "w_out", "ffn2_pre_norm",
         "ffn2_post_norm", "ffn2_w_gate", "ffn2_w_up", "ffn2_w_down")
PACK_ROWS = SUBLANES * 8
def _pack(arrays, rows):
    flat = jnp.concatenate([a.reshape(-1) for a in arrays])
    return jnp.pad(flat, (0, rows * LANES - flat.size)).reshape(rows, LANES)


def _unpack(packed, shapes):
    flat = packed.reshape(-1)
    out, off = [], 0
    for s in shapes:
        n = math.prod(s)
        out.append(flat[off : off + n].reshape(s))
        off += n
    return out


def _step(p, x, loss_target, m, v):
    D = x.shape[-1]
    chip = (2 * lax.axis_index("x") + lax.axis_index("y")).astype(jnp.int32)
    place = jnp.stack([chip, lax.axis_index("c").astype(jnp.int32)])

    def gather_buffer(w):
        own = w.reshape(1, 2, w.shape[0] // 2, w.shape[1])
        return lax.dynamic_update_slice(lax.empty((4,) + own.shape[1:], own.dtype), own, (chip, 0, 0, 0))

    def rows_of(n, a):
        return jnp.swapaxes(a[0], 0, 1) if n.endswith(("w_gate", "w_up")) else a[0]

    def rows_back(n, a):
        return (jnp.swapaxes(a, 0, 1) if n.endswith(("w_gate", "w_up")) else a)[None]

    def grad_view(g):
        return g.reshape(4, 2, g.shape[0] // 8, g.shape[1])

    def reduce_sum(got_sibling, views, tag):
        sums = [_add_own_half(v_, g_, place, f"{tag}_add_sibling_{k}") for k, (v_, g_) in enumerate(zip(views, got_sibling))]
        return [s[0] for s in sums], [s[1] for s in sums]

    def reduce_halves(parts, got_chips, tag):
        return [_add_chips(p_, g_, place, f"{tag}_add_chips_{k}") for k, (p_, g_) in enumerate(zip(parts, got_chips))]

    bufs = {n: gather_buffer(rows_of(n, p[n]).astype(MXU_DTYPE)) for n in BIG}
    full = {}

    def gathered(names, got):
        full.update({n: g_.reshape(-1, g_.shape[-1]) for n, g_ in zip(names, got)})

    def gather_of(names, n_steps):
        return _Gather([bufs[n] for n in names], mid_step=(3 * n_steps) // 4)

    first_names = ("ffn1_w_gate", "ffn1_w_up")
    got = _exchange_call(_Gather([bufs[n] for n in first_names] + [gather_buffer(p["meta_tokens"])]), "gather_first")
    gathered(first_names, got)
    meta = jnp.transpose(got[-1].reshape(4, N_META, -1), (1, 0, 2)).reshape(N_META, D)

    vec = lambda n: p[n].reshape(1, -1)
    G, N, H = p["ssm_b_re"].shape[1:]
    a_re, a_im, bb_re, bb_im = _discretize(p["ssm_lambda_re"][0], p["ssm_lambda_im"][0], p["ssm_log_dt"][0], p["ssm_b_re"][0], p["ssm_b_im"][0])
    tabs, pows = _scan_tables(a_re, a_im, MIX_TILE // SUBLANES)
    bf = lambda a: a.astype(MXU_DTYPE)
    bbre, bbim = bf(_block_diag(bb_re, False)), bf(_block_diag(bb_im, False))
    ccre, ccim = bf(_block_diag(p["ssm_c_re"][0], False)), bf(_block_diag(p["ssm_c_im"][0], False))
    wgv, wgg = bf(_block_diag(p["ssm_w_glu"][0][:, :, :H], True)), bf(_block_diag(p["ssm_w_glu"][0][:, :, H:], True))
    pw = bf(p["pool_w"][0])

    T = x.shape[1] + N_META
    names = ("ffn1_w_down", "w_in", "w_out", "ffn2_w_gate")
    (ga1, si1, s1, n1), got = _ffn_gate_up(
        x[0], meta, vec("ffn1_pre_norm"), full["ffn1_w_gate"], full["ffn1_w_up"], "ffn1_gate_up", exchange=gather_of(names, T // FFN_TILE)
    )
    gathered(names, got)
    (h1, f1), got = _ffn_down(x[0], meta, s1, vec("ffn1_post_norm"), full["ffn1_w_down"], "ffn1_down", exchange=gather_of(("ffn2_w_up",), T // FFN_TILE))
    gathered(("ffn2_w_up",), got)
    (proj, xr, xim, y, pooled, mixed, h2, n2, cat), got = _mix_fwd(
        h1, vec("mix_pre_norm"), vec("ssm_out_norm"), vec("pool_out_norm"), vec("mix_post_norm"), vec("ssm_d"), vec("pool_scale"), full["w_in"], full["w_out"],
        bbre, bbim, ccre, ccim, wgv, wgg, pw, tabs, pows, "mix_fwd", exchange=gather_of(("ffn2_w_down",), T // (MIX_SUBTILES * MIX_TILE)),
    )
    gathered(("ffn2_w_down",), got)
    dh3, sq, f2, ga2, si2, s2, n3 = _ffn_fwd_loss(
        h2, loss_target[0], vec("ffn2_pre_norm"), vec("ffn2_post_norm"), full["ffn2_w_gate"], full["ffn2_w_up"], full["ffn2_w_down"], "ffn2_fwd"
    )
    loss = lax.psum(0.5 * jnp.sum(sq) / D, ("x", "y", "c"))

    g, shared = {}, {}
    ffn_names = lambda tag: (tag + "_w_gate", tag + "_w_up", tag + "_w_down")

    da, db, df, dh2, g["ffn2_post_norm"], g["ffn2_pre_norm"] = _ffn_bwd(
        dh3, f2, ga2, si2, h2, vec("ffn2_post_norm"), vec("ffn2_pre_norm"), full["ffn2_w_gate"], full["ffn2_w_up"], full["ffn2_w_down"], "ffn2_bwd"
    )
    views2 = [
        grad_view(_tn_matmul(da, n3, "ffn2_dw_gate")[0]),
        grad_view(_tn_matmul(db, n3, "ffn2_dw_up")[0]),
        grad_view(_tn_matmul(s2, df, "ffn2_dw_down")[0]),
    ]
    (dy, dpooled, dmixed, g["mix_post_norm"], g["ssm_out_norm"], g["pool_out_norm"], g["ssm_d"], g["pool_scale"], dwgv, dwgg, g["pool_w"]), got = _mix_bwd_heads(
        dh2, mixed, y, pooled, proj, vec("ssm_out_norm"), vec("pool_out_norm"), vec("mix_post_norm"), vec("pool_scale"), full["w_out"], wgv, wgg, pw, "mix_bwd_heads",
        exchange=_SiblingScatter(views2),
    )
    parts2, wire2 = reduce_sum(got, views2, "ffn2")
    (dproj, dccre, dccim, dbbre, dbbim, dar, dai), got = _mix_bwd_scan(
        dy, dpooled, xr, xim, proj, vec("ssm_d"), bbre, bbim, ccre, ccim, tabs, pows, "mix_bwd_scan", exchange=_ChipScatter(wire2)
    )
    dh1, g["mix_pre_norm"] = _mix_bwd_in(dproj, h1, dh2, vec("mix_pre_norm"), full["w_in"], "mix_bwd_in")
    halves2 = reduce_halves(parts2, got, "ffn2")
    dw_in, got = _tn_matmul(n2, dproj, "dw_in", exchange=_SiblingShare(halves2))
    shared.update(zip(ffn_names("ffn2"), got))
    dw_out, _ = _tn_matmul(cat, dmixed, "dw_out")
    views_m = [grad_view(dw_in), grad_view(dw_out)]

    (da, db, df, g["ffn1_post_norm"]), got = _ffn_bwd_down(dh1, f1, ga1, si1, vec("ffn1_post_norm"), full["ffn1_w_down"], "ffn1_bwd_down", exchange=_SiblingScatter(views_m))
    parts_m, wire_m = reduce_sum(got, views_m, "mix")
    dw_down, got = _tn_matmul(s1, df, "ffn1_dw_down", exchange=_ChipScatter(wire_m))
    halves_m = reduce_halves(parts_m, got, "mix")
    views_d = [grad_view(dw_down)]
    ex = _Group([_SiblingShare(halves_m), _SiblingScatter(views_d)])
    dw_gate, got = _tn_matmul(da, n1, "ffn1_dw_gate", exchange=ex)
    got_m, got_d = ex.split(got)
    shared.update(zip(("w_in", "w_out"), got_m))
    parts_d, wire_d = reduce_sum(got_d, views_d, "ffn1_down")
    views_g = [grad_view(dw_gate)]
    ex = _Group([_ChipScatter(wire_d), _SiblingScatter(views_g)])
    dw_up, got = _tn_matmul(db, n1, "ffn1_dw_up", exchange=ex)
    got_d, got_g = ex.split(got)
    halves_d = reduce_halves(parts_d, got_d, "ffn1_down")
    parts_g, wire_g = reduce_sum(got_g, views_g, "ffn1_gate")
    views_u = [grad_view(dw_up)]
    ex = _Group([_SiblingShare(halves_d), _ChipScatter(wire_g), _SiblingScatter(views_u)])
    (grad_x, d_meta, g["ffn1_pre_norm"]), got = _ffn_bwd_up(da, db, x[0], meta, dh1, vec("ffn1_pre_norm"), full["ffn1_w_gate"], full["ffn1_w_up"], "ffn1_bwd_up", exchange=ex)
    got_d, got_g, got_u = ex.split(got)
    shared["ffn1_w_down"] = got_d[0]
    halves_g = reduce_halves(parts_g, got_g, "ffn1_gate")
    parts_u, wire_u = reduce_sum(got_u, views_u, "ffn1_up")
    grad_x = grad_x[None]

    g["ssm_c_re"] = _block_diag_extract(dccre, N, H, False)
    g["ssm_c_im"] = _block_diag_extract(dccim, N, H, False)
    g["ssm_w_glu"] = jnp.concatenate([_block_diag_extract(dwgv, H, H, True), _block_diag_extract(dwgg, H, H, True)], axis=-1)
    d_a_re, d_a_im = jnp.sum(dar, axis=1).reshape(G, N), jnp.sum(dai, axis=1).reshape(G, N)
    _, pull = jax.vjp(_discretize, p["ssm_lambda_re"][0], p["ssm_lambda_im"][0], p["ssm_log_dt"][0], p["ssm_b_re"][0], p["ssm_b_im"][0])
    g["ssm_lambda_re"], g["ssm_lambda_im"], g["ssm_log_dt"], g["ssm_b_re"], g["ssm_b_im"] = pull(
        (d_a_re, d_a_im, _block_diag_extract(dbbre, H, N, False), _block_diag_extract(dbbim, H, N, False))
    )

    small_shapes = [p[n].shape for n in SMALL] + [(N_META, D)]
    small_size = sum(math.prod(s) for s in small_shapes)
    rows = -(-small_size // (LANES * PACK_ROWS)) * PACK_ROWS
    views_s = [_pack([g[n] for n in SMALL] + [d_meta], rows).reshape(4, 2, rows // 8, LANES)]
    parts_s, wire_s = reduce_sum(_exchange_call(_SiblingScatter(views_s), "small_reduce_sibling"), views_s, "small")
    ex = _Group([_ChipScatter(wire_u + wire_s), _SiblingShare(halves_g)])
    got_c, got_g = ex.split(_exchange_call(ex, "tail_reduce_chips"))
    shared["ffn1_w_gate"] = got_g[0]
    got = _exchange_call(_SiblingShare(reduce_halves(parts_u + parts_s, got_c, "tail")), "tail_reduce_share")
    shared["ffn1_w_up"] = got[0]
    small_buf = lax.dynamic_update_slice(lax.empty((4,) + got[1].shape, F32), got[1][None], (chip, 0, 0, 0))
    small_all = _exchange_call(_Gather([small_buf]), "gather_small")[0].reshape(rows, LANES)
    grads = dict(zip(SMALL + ("meta_full",), _unpack(small_all, small_shapes)))
    grads["meta_tokens"] = lax.dynamic_slice_in_dim(grads.pop("meta_full"), chip * (D // 4), D // 4, axis=1)
    delta, new_m, new_v = {}, {}, {}
    for n in BIG:
        g_rows = shared[n].reshape(-1, shared[n].shape[-1])
        outs = _adamw(rows_of(n, p[n]), g_rows, rows_of(n, m[n]), rows_of(n, v[n]), "adamw_" + n)
        grads[n], delta[n], new_m[n], new_v[n] = (rows_back(n, a) for a in (g_rows, *outs))
    delta["meta_tokens"], new_m["meta_tokens"], new_v["meta_tokens"] = _adamw(p["meta_tokens"], grads["meta_tokens"], m["meta_tokens"], v["meta_tokens"], "adamw_meta_tokens")

    def as_2d(n, a):
        a = a.reshape(p[n].shape)[0]
        if n in ("ssm_b_re", "ssm_b_im"):
            a = jnp.swapaxes(a, 1, 2)
        return a.reshape(-1, a.shape[-1])

    def from_2d(n, a):
        if n in ("ssm_b_re", "ssm_b_im"):
            g_, n_, h_ = p[n].shape[1:]
            return jnp.swapaxes(a.reshape(g_, h_, n_), 1, 2)[None]
        return a.reshape(p[n].shape)

    outs = _adamw_many(*[[as_2d(n, t[n]) for n in SMALL] for t in (p, grads, m, v)], "adamw_small")
    for out, arrays in zip((delta, new_m, new_v), outs):
        out.update({n: from_2d(n, a) for n, a in zip(SMALL, arrays)})

    return (loss, grad_x, *[grads[n] for n in ORDER], *[delta[n] for n in ORDER], *[new_m[n] for n in ORDER], *[new_v[n] for n in ORDER])


def kernel(x, meta_tokens, ffn1_pre_norm, ffn1_post_norm, ffn1_w_gate, ffn1_w_up, ffn1_w_down, mix_pre_norm, mix_post_norm, w_in, ssm_lambda_re, ssm_lambda_im, ssm_log_dt, ssm_b_re, ssm_b_im, ssm_c_re, ssm_c_im, ssm_d, ssm_w_glu, pool_w, pool_scale, ssm_out_norm, pool_out_norm, w_out, ffn2_pre_norm, ffn2_post_norm, ffn2_w_gate, ffn2_w_up, ffn2_w_down, loss_target, m_meta_tokens, m_ffn1_pre_norm, m_ffn1_post_norm, m_ffn1_w_gate, m_ffn1_w_up, m_ffn1_w_down, m_mix_pre_norm, m_mix_post_norm, m_w_in, m_ssm_lambda_re, m_ssm_lambda_im, m_ssm_log_dt, m_ssm_b_re, m_ssm_b_im, m_ssm_c_re, m_ssm_c_im, m_ssm_d, m_ssm_w_glu, m_pool_w, m_pool_scale, m_ssm_out_norm, m_pool_out_norm, m_w_out, m_ffn2_pre_norm, m_ffn2_post_norm, m_ffn2_w_gate, m_ffn2_w_up, m_ffn2_w_down, v_meta_tokens, v_ffn1_pre_norm, v_ffn1_post_norm, v_ffn1_w_gate, v_ffn1_w_up, v_ffn1_w_down, v_mix_pre_norm, v_mix_post_norm, v_w_in, v_ssm_lambda_re, v_ssm_lambda_im, v_ssm_log_dt, v_ssm_b_re, v_ssm_b_im, v_ssm_c_re, v_ssm_c_im, v_ssm_d, v_ssm_w_glu, v_pool_w, v_pool_scale, v_ssm_out_norm, v_pool_out_norm, v_w_out, v_ffn2_pre_norm, v_ffn2_post_norm, v_ffn2_w_gate, v_ffn2_w_up, v_ffn2_w_down):
    args = locals()
    p = {n: args[n] for n in ORDER}
    m = {n: args["m_" + n] for n in ORDER}
    v = {n: args["v_" + n] for n in ORDER}
    return _step(p, x, loss_target, m, v)
```

```python
import math

import jax
import jax.numpy as jnp
from jax import lax
from jax.experimental import pallas as pl
from jax.experimental.pallas import tpu as pltpu

F32 = jnp.float32
MXU_DTYPE = jnp.bfloat16
WIRE_DTYPE = jnp.bfloat16

RMS_EPS = 1e-6
N_META = 16
POOL_WINDOWS = (2, 4, 8, 16)
POOL_HALO = 16
ADAM_LR, ADAM_B1, ADAM_B2, ADAM_EPS, ADAM_WD, ADAM_STEP = 0.001, 0.9, 0.999, 1e-08, 0.01, 10

LANES = 128
SUBLANES = 8
VMEM_LIMIT = 60 * 1024 * 1024
FFN_TILE = 432
FFN_CHUNK = 1024
TN_TILE = 1024
MIX_TILE = 216
MIX_SUBTILES = 2
SLAB_GROUP = 8
MESH = pl.DeviceIdType.MESH
ANY = pl.BlockSpec(memory_space=pl.ANY)


def _mm(a, b):
    return jnp.dot(a.astype(MXU_DTYPE), b.astype(MXU_DTYPE), preferred_element_type=F32)


def _mm_nt(a, b):
    return lax.dot_general(a.astype(MXU_DTYPE), b.astype(MXU_DTYPE), (((1,), (1,)), ((), ())), preferred_element_type=F32)


def _mm_tn(a, b):
    return lax.dot_general(a.astype(MXU_DTYPE), b.astype(MXU_DTYPE), (((0,), (0,)), ((), ())), preferred_element_type=F32)


def _rms_stat(x):
    return lax.rsqrt(jnp.mean(x * x, axis=-1, keepdims=True) + RMS_EPS)


def _rms_bwd(x, g, dy):
    r = _rms_stat(x)
    xh = x * r
    dg = jnp.sum(dy * xh, axis=0, keepdims=True)
    dxh = dy * g
    dx = r * (dxh - xh * jnp.mean(dxh * xh, axis=-1, keepdims=True))
    return dx, dg


def _sigmoid(x):
    return 1.0 / (1.0 + jnp.exp(-x))


GELU_C = math.sqrt(2.0 / math.pi)
GELU_K = 0.044715


def _gelu(y):
    return 0.5 * y * (1.0 + jnp.tanh(GELU_C * (y + GELU_K * y * y * y)))


def _gelu_grad(y):
    th = jnp.tanh(GELU_C * (y + GELU_K * y * y * y))
    return 0.5 * (1.0 + th) + 0.5 * y * (1.0 - th * th) * GELU_C * (1.0 + 3.0 * GELU_K * y * y)


def _row_spec(tile, cols, rev_n=None):
    if rev_n is None:
        return pl.BlockSpec((tile, cols), lambda i: (i, 0))
    return pl.BlockSpec((tile, cols), lambda i: (rev_n - 1 - i, 0))


def _full_spec(shape, single=False):
    zeros = (0,) * len(shape)
    if single:
        return pl.BlockSpec(shape, lambda *_: zeros, pipeline_mode=pl.Buffered(1))
    return pl.BlockSpec(shape, lambda *_: zeros)


def _acc(ref, val, first):
    @pl.when(first)
    def _():
        ref[...] = val

    @pl.when(jnp.logical_not(first))
    def _():
        ref[...] += val


def _place():
    x, y, c = lax.axis_index("x"), lax.axis_index("y"), lax.axis_index("c")
    others = [(1 - x, y), (x, 1 - y), (1 - x, 1 - y)]
    return x, y, c, others


class _Exchange:
    mid_step = None

    def __init__(self, ins, out_shapes, aliases, n_sems):
        self.ins, self.out_shapes, self.aliases, self.n_sems = list(ins), list(out_shapes), dict(aliases), n_sems

    def mid(self, ins, outs, send_sems, recv_sems):
        pass


class _SiblingScatter(_Exchange):
    def __init__(self, views):
        super().__init__(views, [jax.ShapeDtypeStruct((4,) + v.shape[2:], v.dtype) for v in views], {}, 4 * len(views))

    def _copies(self, ins, outs, send_sems, recv_sems):
        x, y, c, _ = _place()
        return [
            pltpu.make_async_remote_copy(src_ref=ins[a].at[k, 1 - c], dst_ref=outs[a].at[k], send_sem=send_sems.at[4 * a + k], recv_sem=recv_sems.at[4 * a + k], device_id=(x, y, 1 - c), device_id_type=MESH)
            for a in range(len(ins))
            for k in range(4)
        ]

    def start(self, *refs):
        for cp in self._copies(*refs):
            cp.start()

    def finish(self, *refs):
        cps = self._copies(*refs)
        for cp in cps:
            cp.wait_recv()
        for cp in cps:
            cp.wait_send()


class _ChipScatter(_Exchange):
    def __init__(self, parts):
        super().__init__(parts, [jax.ShapeDtypeStruct((3,) + p.shape[1:], p.dtype) for p in parts], {}, 3 * len(parts))

    def _copies(self, ins, outs, send_sems, recv_sems):
        x, y, c, others = _place()
        return [
            pltpu.make_async_remote_copy(src_ref=ins[a].at[2 * chip[0] + chip[1]], dst_ref=outs[a].at[j], send_sem=send_sems.at[3 * a + j], recv_sem=recv_sems.at[3 * a + j], device_id=(*chip, c), device_id_type=MESH)
            for a in range(len(ins))
            for j, chip in enumerate(others)
        ]

    start = _SiblingScatter.start
    finish = _SiblingScatter.finish


class _SiblingShare(_Exchange):
    def __init__(self, bufs):
        super().__init__(bufs, [jax.ShapeDtypeStruct(b.shape, b.dtype) for b in bufs], {a: a for a in range(len(bufs))}, len(bufs))

    def _copy(self, outs, send_sems, recv_sems, a, half):
        x, y, c, _ = _place()
        mine = outs[a].at[c if half == "mine" else 1 - c]
        return pltpu.make_async_remote_copy(src_ref=mine, dst_ref=mine, send_sem=send_sems.at[a], recv_sem=recv_sems.at[a], device_id=(x, y, 1 - c), device_id_type=MESH)

    def start(self, ins, outs, send_sems, recv_sems):
        for a in range(len(outs)):
            self._copy(outs, send_sems, recv_sems, a, "mine").start()

    def finish(self, ins, outs, send_sems, recv_sems):
        for a in range(len(outs)):
            self._copy(outs, send_sems, recv_sems, a, "theirs").wait_recv()
        for a in range(len(outs)):
            self._copy(outs, send_sems, recv_sems, a, "mine").wait_send()


class _Gather(_Exchange):
    def __init__(self, bufs, mid_step=None):
        super().__init__(bufs, [jax.ShapeDtypeStruct(b.shape, b.dtype) for b in bufs], {a: a for a in range(len(bufs))}, 6 * len(bufs))
        self.mid_step = mid_step

    def _copy(self, outs, send_sems, recv_sems, a, j, chip, half, to):
        blk = outs[a].at[2 * chip[0] + chip[1], half]
        return pltpu.make_async_remote_copy(src_ref=blk, dst_ref=blk, send_sem=send_sems.at[6 * a + j], recv_sem=recv_sems.at[6 * a + j], device_id=to, device_id_type=MESH)

    def start(self, ins, outs, send_sems, recv_sems):
        x, y, c, others = _place()
        for a in range(len(outs)):
            for j, chip in enumerate(others):
                self._copy(outs, send_sems, recv_sems, a, j, (x, y), c, (*chip, c)).start()

    def mid(self, ins, outs, send_sems, recv_sems):
        x, y, c, others = _place()
        for a in range(len(outs)):
            for j, chip in enumerate(others):
                self._copy(outs, send_sems, recv_sems, a, j, chip, c, (x, y, c)).wait_recv()
                self._copy(outs, send_sems, recv_sems, a, 3 + j, chip, c, (x, y, 1 - c)).start()

    def finish(self, ins, outs, send_sems, recv_sems):
        x, y, c, others = _place()
        for a in range(len(outs)):
            for j, chip in enumerate(others):
                self._copy(outs, send_sems, recv_sems, a, 3 + j, chip, 1 - c, (x, y, c)).wait_recv()
        for a in range(len(outs)):
            for j, chip in enumerate(others):
                self._copy(outs, send_sems, recv_sems, a, j, (x, y), c, (*chip, c)).wait_send()
                self._copy(outs, send_sems, recv_sems, a, 3 + j, chip, c, (x, y, 1 - c)).wait_send()


class _SemSlice:
    def __init__(self, sems, off):
        self.sems, self.off = sems, off

    @property
    def at(self):
        return self

    def __getitem__(self, i):
        return self.sems.at[self.off + i]


class _Group(_Exchange):
    def __init__(self, exchanges):
        ins, outs, aliases, n_sems, self.spans = [], [], {}, 0, []
        for ex in exchanges:
            self.spans.append((len(ins), len(outs), n_sems))
            aliases.update({len(ins) + i: len(outs) + o for i, o in ex.aliases.items()})
            ins, outs, n_sems = ins + ex.ins, outs + ex.out_shapes, n_sems + ex.n_sems
        super().__init__(ins, outs, aliases, n_sems)
        self.exchanges = exchanges
        mids = {ex.mid_step for ex in exchanges if ex.mid_step is not None}
        self.mid_step = mids.pop() if mids else None

    def _each(self, method, ins, outs, send_sems, recv_sems):
        for ex, (i0, o0, s0) in zip(self.exchanges, self.spans):
            getattr(ex, method)(ins[i0 : i0 + len(ex.ins)], outs[o0 : o0 + len(ex.out_shapes)], _SemSlice(send_sems, s0), _SemSlice(recv_sems, s0))

    def start(self, *refs):
        self._each("start", *refs)

    def mid(self, *refs):
        self._each("mid", *refs)

    def finish(self, *refs):
        self._each("finish", *refs)

    def split(self, outs):
        return [outs[o0 : o0 + len(ex.out_shapes)] for ex, (_, o0, _) in zip(self.exchanges, self.spans)]


def _exchange_call(ex, name):
    n, m = len(ex.ins), len(ex.out_shapes)

    def body(*refs):
        parts = (refs[:n], refs[n : n + m], refs[n + m], refs[n + m + 1])
        ex.start(*parts)
        ex.mid(*parts)
        ex.finish(*parts)

    return pl.pallas_call(
        body,
        name=name,
        out_shape=ex.out_shapes,
        in_specs=[ANY] * n,
        out_specs=[ANY] * m,
        scratch_shapes=[pltpu.SemaphoreType.DMA((ex.n_sems,)), pltpu.SemaphoreType.DMA((ex.n_sems,))],
        input_output_aliases=ex.aliases,
    )(*ex.ins)


def _pallas(body, *, name, grid, in_specs, out_specs, out_shape, operands, scratch_shapes=(), exchange=None):
    params = pltpu.CompilerParams(dimension_semantics=("arbitrary",) * len(grid), vmem_limit_bytes=VMEM_LIMIT)
    if exchange is None:
        outs = pl.pallas_call(body, name=name, grid=grid, in_specs=in_specs, out_specs=out_specs, out_shape=out_shape, scratch_shapes=list(scratch_shapes), compiler_params=params)(*operands)
        return outs, []
    ex = exchange
    n_in, n_out, n_scr = len(in_specs), len(out_specs), len(scratch_shapes)
    x_in, x_out = len(ex.ins), len(ex.out_shapes)

    def hosted(*refs):
        ins, x_ins = refs[:n_in], refs[n_in : n_in + x_in]
        outs, x_outs = refs[n_in + x_in : n_in + x_in + n_out], refs[n_in + x_in + n_out : n_in + x_in + n_out + x_out]
        rest = refs[n_in + x_in + n_out + x_out :]
        parts = (x_ins, x_outs, rest[n_scr], rest[n_scr + 1])
        ids = [pl.program_id(d) for d in range(len(grid))]
        first = _all([i == 0 for i in ids])
        last = _all([i == g - 1 for i, g in zip(ids, grid)])

        @pl.when(first)
        def _():
            ex.start(*parts)

        body(*ins, *outs, *rest[:n_scr])

        if ex.mid_step is not None:

            @pl.when(ids[0] == ex.mid_step)
            def _():
                ex.mid(*parts)

        @pl.when(last)
        def _():
            ex.finish(*parts)

    outs = pl.pallas_call(
        hosted,
        name=name,
        grid=grid,
        in_specs=list(in_specs) + [ANY] * x_in,
        out_specs=list(out_specs) + [ANY] * x_out,
        out_shape=list(out_shape) + ex.out_shapes,
        scratch_shapes=list(scratch_shapes) + [pltpu.SemaphoreType.DMA((ex.n_sems,)), pltpu.SemaphoreType.DMA((ex.n_sems,))],
        input_output_aliases={n_in + i: n_out + o for i, o in ex.aliases.items()},
        compiler_params=params,
    )(*operands, *ex.ins)
    return outs[:n_out], outs[n_out:]


def _all(conds):
    out = conds[0]
    for c in conds[1:]:
        out = jnp.logical_and(out, c)
    return out


def _row_tile(rows):
    if rows <= 512:
        return rows
    for t in (512, 352, 256, 176, 128, 112, 64, 32, 16, 8):
        if rows % t == 0:
            return t
    return rows


def _add_own_half(view, got, place, name):
    _, _, r, c = view.shape
    tr = _row_tile(r)

    def body(place_ref, v_ref, g_ref, o_ref, w_ref):
        s = v_ref[...] + g_ref[...]
        w_ref[...] = s.astype(w_ref.dtype)

        @pl.when(pl.program_id(1) == place_ref[0])
        def _():
            o_ref[...] = s

    blk = pl.BlockSpec((None, tr, c), lambda i, k, pr: (k, i, 0))
    return pl.pallas_call(
        body,
        name=name,
        out_shape=[jax.ShapeDtypeStruct((r, c), F32), jax.ShapeDtypeStruct((4, r, c), WIRE_DTYPE)],
        grid_spec=pltpu.PrefetchScalarGridSpec(
            num_scalar_prefetch=1,
            grid=(r // tr, 4),
            in_specs=[pl.BlockSpec((None, None, tr, c), lambda i, k, pr: (k, pr[1], i, 0)), blk],
            out_specs=[pl.BlockSpec((tr, c), lambda i, k, pr: (i, 0)), blk],
        ),
        compiler_params=pltpu.CompilerParams(dimension_semantics=("arbitrary", "arbitrary"), vmem_limit_bytes=VMEM_LIMIT),
    )(place, view, got)


def _add_chips(part, got, place, name):
    r, c = part.shape
    tr = _row_tile(r)

    def body(place_ref, p_ref, g_ref, o_ref):
        o_ref[...] = ((p_ref[...] + g_ref[0].astype(F32)) + g_ref[1].astype(F32)) + g_ref[2].astype(F32)

    return pl.pallas_call(
        body,
        name=name,
        out_shape=jax.ShapeDtypeStruct((2, r, c), F32),
        grid_spec=pltpu.PrefetchScalarGridSpec(
            num_scalar_prefetch=1,
            grid=(r // tr,),
            in_specs=[pl.BlockSpec((tr, c), lambda i, pr: (i, 0)), pl.BlockSpec((3, tr, c), lambda i, pr: (0, i, 0))],
            out_specs=pl.BlockSpec((None, tr, c), lambda i, pr: (pr[1], i, 0)),
        ),
        compiler_params=pltpu.CompilerParams(dimension_semantics=("arbitrary",), vmem_limit_bytes=VMEM_LIMIT),
    )(place, part, got)


def _adamw_update(w_ref, g_ref, m_ref, v_ref, d_ref, nm_ref, nv_ref):
    g = g_ref[...]
    nm = ADAM_B1 * m_ref[...] + (1.0 - ADAM_B1) * g
    nv = ADAM_B2 * v_ref[...] + (1.0 - ADAM_B2) * (g * g)
    m_hat = nm / (1.0 - ADAM_B1**ADAM_STEP)
    v_hat = nv / (1.0 - ADAM_B2**ADAM_STEP)
    d_ref[...] = -ADAM_LR * (m_hat / (jnp.sqrt(v_hat) + ADAM_EPS) + ADAM_WD * w_ref[...])
    nm_ref[...] = nm
    nv_ref[...] = nv


def _adamw(w, g, m, v, name):
    r, c = w.shape
    tr = _row_tile(r)
    spec = pl.BlockSpec((tr, c), lambda i: (i, 0))
    outs, _ = _pallas(_adamw_update, name=name, grid=(r // tr,), in_specs=[spec] * 4, out_specs=[spec] * 3, out_shape=[jax.ShapeDtypeStruct((r, c), F32)] * 3, operands=(w, g, m, v))
    return outs


def _adamw_many(ws, gs, ms, vs, name):
    n = len(ws)

    def body(*refs):
        for k in range(n):
            _adamw_update(*(refs[j * n + k] for j in range(7)))

    outs = pl.pallas_call(
        body,
        name=name,
        out_shape=[jax.ShapeDtypeStruct(w.shape, F32) for w in ws] * 3,
        in_specs=[pl.BlockSpec(memory_space=pltpu.VMEM)] * (4 * n),
        out_specs=[pl.BlockSpec(memory_space=pltpu.VMEM)] * (3 * n),
    )(*ws, *gs, *ms, *vs)
    return outs[:n], outs[n : 2 * n], outs[2 * n :]


def _load_weights(pairs, sems):
    @pl.when(pl.program_id(0) == 0)
    def _():
        cps = [pltpu.make_async_copy(src, dst, sems.at[k]) for k, (src, dst) in enumerate(pairs)]
        for cp in cps:
            cp.start()
        for cp in cps:
            cp.wait()


def _ffn_chunks(F):
    bounds = list(range(0, F, FFN_CHUNK)) + [F]
    return list(zip(bounds[:-1], bounds[1:]))


def _shifted_specs(tm, cols):
    per = tm // N_META
    return [_row_spec(tm, cols), pl.BlockSpec((N_META, cols), lambda i: (jnp.maximum(i * per - 1, 0), 0))]


def _shifted_tile(cur_ref, before_ref, tm):
    return jnp.concatenate([before_ref[...], cur_ref[0 : tm - N_META, :]], axis=0)


def _tokens_tile(cur_ref, before_ref, meta_ref, tm, tile_0):
    first = jnp.where(tile_0, meta_ref[...], before_ref[...])
    return jnp.concatenate([first, cur_ref[0 : tm - N_META, :]], axis=0)


def _ffn_fwd_loss(h, n, target, g_post, wg, wu, wd, name):
    T, D = h.shape
    F = wg.shape[0]
    tm = FFN_TILE

    def body(h_ref, n_ref, t_ref, tb_ref, gpost_ref, wg_hbm, wu_hbm, wd_hbm, dy_ref, sq_ref, f_ref, ga_ref, si_ref, s_ref, wg_v, wu_v, wd_v, sems):
        i = pl.program_id(0)
        _load_weights([(wg_hbm, wg_v), (wu_hbm, wu_v), (wd_hbm, wd_v)], sems)
        hh = h_ref[...]
        n = n_ref[...]
        f = jnp.zeros((tm, D), F32)
        for lo, hi in _ffn_chunks(F):
            a = _mm_nt(n, wg_v[lo:hi, :])
            b = _mm_nt(n, wu_v[lo:hi, :])
            sg = _sigmoid(a)
            si = a * sg
            s = (si * b).astype(MXU_DTYPE)
            ga_ref[:, lo:hi] = (b * (sg * (1.0 + a * (1.0 - sg)))).astype(ga_ref.dtype)
            si_ref[:, lo:hi] = si.astype(si_ref.dtype)
            s_ref[:, lo:hi] = s.astype(s_ref.dtype)
            f = f + _mm(s, wd_v[lo:hi, :])
        f_ref[...] = f
        y = hh + 0.5 * (f * _rms_stat(f) * gpost_ref[...])
        rows = i * tm + lax.broadcasted_iota(jnp.int32, (tm, D), 0)
        err = jnp.where(rows >= N_META, y - _shifted_tile(t_ref, tb_ref, tm), 0.0)
        dy_ref[...] = err * (1.0 / D)
        _acc(sq_ref, jnp.sum(err * err, axis=0, keepdims=True), i == 0)

    tok = jax.ShapeDtypeStruct((T, D), F32)
    act = jax.ShapeDtypeStruct((T, F), MXU_DTYPE)
    outs, _ = _pallas(
        body,
        name=name,
        grid=(T // tm,),
        out_shape=[tok, jax.ShapeDtypeStruct((1, D), F32), tok, act, act, act],
        in_specs=[_row_spec(tm, D), _row_spec(tm, D)] + _shifted_specs(tm, D) + [_full_spec((1, D)), ANY, ANY, ANY],
        out_specs=[_row_spec(tm, D), _full_spec((1, D)), _row_spec(tm, D), _row_spec(tm, F), _row_spec(tm, F), _row_spec(tm, F)],
        scratch_shapes=[pltpu.VMEM(wg.shape, wg.dtype), pltpu.VMEM(wu.shape, wu.dtype), pltpu.VMEM(wd.shape, wd.dtype), pltpu.SemaphoreType.DMA((3,))],
        operands=(h, n, target, target, g_post, wg, wu, wd),
    )
    return outs


def _ffn_gate_up(x, meta, g_pre, wg, wu, name, exchange=None):
    D = x.shape[1]
    T = x.shape[0] + N_META
    F = wg.shape[0]
    tm = FFN_TILE

    def body(x_ref, xb_ref, meta_ref, gpre_ref, wg_hbm, wu_hbm, ga_ref, si_ref, s_ref, n_ref, wg_v, wu_v, sems):
        _load_weights([(wg_hbm, wg_v), (wu_hbm, wu_v)], sems)
        hh = _tokens_tile(x_ref, xb_ref, meta_ref, tm, pl.program_id(0) == 0)
        n = (hh * _rms_stat(hh) * gpre_ref[...]).astype(MXU_DTYPE)
        n_ref[...] = n.astype(n_ref.dtype)
        for lo, hi in _ffn_chunks(F):
            a = _mm_nt(n, wg_v[lo:hi, :])
            b = _mm_nt(n, wu_v[lo:hi, :])
            sg = _sigmoid(a)
            si = a * sg
            ga_ref[:, lo:hi] = (b * (sg * (1.0 + a * (1.0 - sg)))).astype(ga_ref.dtype)
            si_ref[:, lo:hi] = si.astype(si_ref.dtype)
            s_ref[:, lo:hi] = (si * b).astype(s_ref.dtype)

    act = jax.ShapeDtypeStruct((T, F), MXU_DTYPE)
    return _pallas(
        body,
        name=name,
        grid=(T // tm,),
        out_shape=[act, act, act, jax.ShapeDtypeStruct((T, D), MXU_DTYPE)],
        in_specs=_shifted_specs(tm, D) + [_full_spec((N_META, D)), _full_spec((1, D)), ANY, ANY],
        out_specs=[_row_spec(tm, F), _row_spec(tm, F), _row_spec(tm, F), _row_spec(tm, D)],
        scratch_shapes=[pltpu.VMEM(wg.shape, wg.dtype), pltpu.VMEM(wu.shape, wu.dtype), pltpu.SemaphoreType.DMA((2,))],
        operands=(x, x, meta, g_pre, wg, wu),
        exchange=exchange,
    )


def _ffn_down(x, meta, s, g_post, wd, name, exchange=None):
    D = x.shape[1]
    T = x.shape[0] + N_META
    F = wd.shape[0]
    tm = _token_tile(T)

    def body(x_ref, xb_ref, meta_ref, s_ref, gpost_ref, wd_hbm, hout_ref, f_ref, wd_v, sems):
        _load_weights([(wd_hbm, wd_v)], sems)
        f = _mm(s_ref[...], wd_v[...])
        f_ref[...] = f
        hout_ref[...] = _tokens_tile(x_ref, xb_ref, meta_ref, tm, pl.program_id(0) == 0) + 0.5 * (f * _rms_stat(f) * gpost_ref[...])

    tok = jax.ShapeDtypeStruct((T, D), F32)
    return _pallas(
        body,
        name=name,
        grid=(T // tm,),
        out_shape=[tok, tok],
        in_specs=_shifted_specs(tm, D) + [_full_spec((N_META, D)), _row_spec(tm, F), _full_spec((1, D)), ANY],
        out_specs=[_row_spec(tm, D), _row_spec(tm, D)],
        scratch_shapes=[pltpu.VMEM(wd.shape, wd.dtype), pltpu.SemaphoreType.DMA((1,))],
        operands=(x, x, meta, s, g_post, wd),
        exchange=exchange,
    )


def _ffn_bwd_down(dh, f, ga, si, g_post, wd, name, exchange=None):
    T, D = dh.shape
    F = wd.shape[0]
    tm = FFN_TILE

    def body(dh_ref, f_ref, ga_ref, si_ref, gpost_ref, wd_hbm, da_ref, db_ref, df_ref, dg_ref, wd_v, sems):
        _load_weights([(wd_hbm, wd_v)], sems)
        df, dg = _rms_bwd(f_ref[...], gpost_ref[...], 0.5 * dh_ref[...])
        _acc(dg_ref, dg, pl.program_id(0) == 0)
        dfb = df.astype(MXU_DTYPE)
        df_ref[...] = dfb.astype(df_ref.dtype)
        for lo, hi in _ffn_chunks(F):
            ds = _mm_nt(dfb, wd_v[lo:hi, :])
            da_ref[:, lo:hi] = (ds * ga_ref[:, lo:hi].astype(F32)).astype(da_ref.dtype)
            db_ref[:, lo:hi] = (ds * si_ref[:, lo:hi].astype(F32)).astype(db_ref.dtype)

    act = jax.ShapeDtypeStruct((T, F), MXU_DTYPE)
    return _pallas(
        body,
        name=name,
        grid=(T // tm,),
        out_shape=[act, act, jax.ShapeDtypeStruct((T, D), MXU_DTYPE), jax.ShapeDtypeStruct((1, D), F32)],
        in_specs=[_row_spec(tm, D), _row_spec(tm, D), _row_spec(tm, F), _row_spec(tm, F), _full_spec((1, D)), ANY],
        out_specs=[_row_spec(tm, F), _row_spec(tm, F), _row_spec(tm, D), _full_spec((1, D))],
        scratch_shapes=[pltpu.VMEM(wd.shape, wd.dtype), pltpu.SemaphoreType.DMA((1,))],
        operands=(dh, f, ga, si, g_post, wd),
        exchange=exchange,
    )


def _ffn_bwd_up(da, db, x, meta, dh, g_pre, wg, wu, name, exchange=None):
    D = x.shape[1]
    T = x.shape[0] + N_META
    F = wg.shape[0]
    tm = FFN_TILE
    nt = T // tm
    per = tm // N_META
    tile = lambda i: jnp.minimum(i, nt - 1)

    def body(da_ref, db_ref, x_ref, xb_ref, meta_ref, dh_ref, gpre_ref, wg_hbm, wu_hbm, dx_ref, dmeta_ref, dg_ref, wg_v, wu_v, sems, held):
        i = pl.program_id(0)
        _load_weights([(wg_hbm, wg_v), (wu_hbm, wu_v)], sems)

        @pl.when(i < nt)
        def _():
            dn = jnp.zeros((tm, D), F32)
            for lo, hi in _ffn_chunks(F):
                dn = dn + _mm(da_ref[:, lo:hi], wg_v[lo:hi, :]) + _mm(db_ref[:, lo:hi], wu_v[lo:hi, :])
            dx, dg = _rms_bwd(_tokens_tile(x_ref, xb_ref, meta_ref, tm, i == 0), gpre_ref[...], dn)
            _acc(dg_ref, dg, i == 0)
            dh_in = dh_ref[...] + dx

            @pl.when(i == 0)
            def _():
                dmeta_ref[...] = dh_in[0:N_META, :]

            @pl.when(i > 0)
            def _():
                dx_ref[...] = jnp.concatenate([held[...], dh_in[0:N_META, :]], axis=0)

            held[...] = dh_in[N_META:, :]

        @pl.when(i == nt)
        def _():
            dx_ref[0 : tm - N_META, :] = held[...]

    rows = lambda cols: pl.BlockSpec((tm, cols), lambda i: (tile(i), 0))
    return _pallas(
        body,
        name=name,
        grid=(nt + 1,),
        out_shape=[jax.ShapeDtypeStruct((T - N_META, D), F32), jax.ShapeDtypeStruct((N_META, D), F32), jax.ShapeDtypeStruct((1, D), F32)],
        in_specs=[rows(F), rows(F), rows(D), pl.BlockSpec((N_META, D), lambda i: (jnp.maximum(tile(i) * per - 1, 0), 0)), _full_spec((N_META, D)), rows(D), _full_spec((1, D)), ANY, ANY],
        out_specs=[pl.BlockSpec((tm, D), lambda i: (jnp.maximum(i - 1, 0), 0)), _full_spec((N_META, D)), _full_spec((1, D))],
        scratch_shapes=[pltpu.VMEM(wg.shape, wg.dtype), pltpu.VMEM(wu.shape, wu.dtype), pltpu.SemaphoreType.DMA((2,)), pltpu.VMEM((tm - N_META, D), F32)],
        operands=(da, db, x, x, meta, dh, g_pre, wg, wu),
        exchange=exchange,
    )


def _ffn_bwd(dh, f, ga, si, h, g_post, g_pre, wg, wu, wd, name):
    T, D = dh.shape
    F = wd.shape[0]
    tm = FFN_TILE

    def body(dh_ref, f_ref, ga_ref, si_ref, h_ref, gpost_ref, gpre_ref, wg_hbm, wu_hbm, wd_hbm, da_ref, db_ref, df_ref, dhin_ref, dgpost_ref, dgpre_ref, wg_v, wu_v, wd_v, sems):
        first = pl.program_id(0) == 0
        _load_weights([(wg_hbm, wg_v), (wu_hbm, wu_v), (wd_hbm, wd_v)], sems)
        dh = dh_ref[...]
        df, dg = _rms_bwd(f_ref[...], gpost_ref[...], 0.5 * dh)
        _acc(dgpost_ref, dg, first)
        dfb = df.astype(MXU_DTYPE)
        df_ref[...] = dfb.astype(df_ref.dtype)
        dn = jnp.zeros((tm, D), F32)
        for lo, hi in _ffn_chunks(F):
            ds = _mm_nt(dfb, wd_v[lo:hi, :])
            da = (ds * ga_ref[:, lo:hi].astype(F32)).astype(MXU_DTYPE)
            db = (ds * si_ref[:, lo:hi].astype(F32)).astype(MXU_DTYPE)
            da_ref[:, lo:hi] = da.astype(da_ref.dtype)
            db_ref[:, lo:hi] = db.astype(db_ref.dtype)
            dn = dn + _mm(da, wg_v[lo:hi, :]) + _mm(db, wu_v[lo:hi, :])
        dx, dg = _rms_bwd(h_ref[...], gpre_ref[...], dn)
        _acc(dgpre_ref, dg, first)
        dhin_ref[...] = dh + dx

    act = jax.ShapeDtypeStruct((T, F), MXU_DTYPE)
    vec = jax.ShapeDtypeStruct((1, D), F32)
    outs, _ = _pallas(
        body,
        name=name,
        grid=(T // tm,),
        out_shape=[act, act, jax.ShapeDtypeStruct((T, D), MXU_DTYPE), jax.ShapeDtypeStruct((T, D), F32), vec, vec],
        in_specs=[_row_spec(tm, D), _row_spec(tm, D), _row_spec(tm, F), _row_spec(tm, F), _row_spec(tm, D), _full_spec((1, D)), _full_spec((1, D)), ANY, ANY, ANY],
        out_specs=[_row_spec(tm, F), _row_spec(tm, F), _row_spec(tm, D), _row_spec(tm, D), _full_spec((1, D)), _full_spec((1, D))],
        scratch_shapes=[pltpu.VMEM(wg.shape, wg.dtype), pltpu.VMEM(wu.shape, wu.dtype), pltpu.VMEM(wd.shape, wd.dtype), pltpu.SemaphoreType.DMA((3,))],
        operands=(dh, f, ga, si, h, g_post, g_pre, wg, wu, wd),
    )
    return outs


def _token_tile(T):
    for t in (912, 864, 432):
        if T % t == 0:
            return t
    raise ValueError(f"no token tile for {T} rows")


def _tn_matmul(xm, ym, name, exchange=None):
    T, M = xm.shape
    N = ym.shape[1]
    if (T - N_META) % TN_TILE:
        tk = _token_tile(T)

        def body(x_ref, y_ref, o_ref):
            _acc(o_ref, _mm_tn(x_ref[...], y_ref[...]), pl.program_id(0) == 0)

        grid, operands = (T // tk,), (xm, ym)
        in_specs = [pl.BlockSpec((tk, M), lambda k: (k, 0)), pl.BlockSpec((tk, N), lambda k: (k, 0))]
    else:
        tk = TN_TILE

        def body(x_ref, y_ref, xh_ref, yh_ref, o_ref):
            prod = _mm_tn(x_ref[...], y_ref[...])

            @pl.when(pl.program_id(0) == 0)
            def _():
                o_ref[...] = prod + _mm_tn(xh_ref[...], yh_ref[...])

            @pl.when(pl.program_id(0) > 0)
            def _():
                o_ref[...] += prod

        grid, operands = ((T - N_META) // tk,), (xm, ym, xm, ym)
        start = lambda k: (pl.multiple_of(N_META + k * tk, N_META), 0)
        in_specs = [pl.BlockSpec((pl.Element(tk), pl.Element(M)), start), pl.BlockSpec((pl.Element(tk), pl.Element(N)), start),
                    pl.BlockSpec((N_META, M), lambda k: (0, 0)), pl.BlockSpec((N_META, N), lambda k: (0, 0))]

    (out,), x_outs = _pallas(
        body,
        name=name,
        grid=grid,
        out_shape=[jax.ShapeDtypeStruct((M, N), F32)],
        in_specs=in_specs,
        out_specs=[_full_spec((M, N))],
        operands=operands,
        exchange=exchange,
    )
    return out, x_outs


TAB_A, TAB_AS1, TAB_AS2, TAB_AS4, TAB_JF, TAB_JB = 0, 2, 4, 6, 8, 10


def _scan_inplace(zr, zi, tabs, pows, car_r, car_i, seg, reverse, base=0):
    n_slabs = zr.shape[0]
    sgn = -1.0 if reverse else 1.0
    row = lax.broadcasted_iota(jnp.int32, (SUBLANES, LANES), 0)

    def cmul(pr, pi, xr, xi):
        return pr * xr - pi * xi, pr * xi + pi * xr

    for k0 in range(0, n_slabs, SLAB_GROUP):
        slabs = range(k0, min(k0 + SLAB_GROUP, n_slabs))
        ar = [tabs[TAB_A, k] for k in slabs]
        ai = [sgn * tabs[TAB_A + 1, k] for k in slabs]

        def first_pass(t, carry):
            r = (seg - 1 - t) if reverse else t
            out = []
            for q, k in enumerate(slabs):
                xr, xi = carry[2 * q], carry[2 * q + 1]
                pr, pi = cmul(ar[q], ai[q], xr, xi)
                nr = pr + zr[k, pl.ds(base + r, SUBLANES, stride=seg), :]
                ni = pi + zi[k, pl.ds(base + r, SUBLANES, stride=seg), :]
                zr[k, pl.ds(base + r, SUBLANES, stride=seg), :] = nr
                zi[k, pl.ds(base + r, SUBLANES, stride=seg), :] = ni
                out += [nr, ni]
            return tuple(out)

        ends = lax.fori_loop(0, seg, first_pass, tuple(jnp.zeros((SUBLANES, LANES), F32) for _ in range(2 * len(slabs))))

        incoming = []
        for q, k in enumerate(slabs):
            fr, fi = ends[2 * q], ends[2 * q + 1]
            for d, tab in ((1, TAB_AS1), (2, TAB_AS2), (4, TAB_AS4)):
                shift, keep = (SUBLANES - d, row < SUBLANES - d) if reverse else (d, row >= d)
                sr = jnp.where(keep, pltpu.roll(fr, shift, 0), 0.0)
                si = jnp.where(keep, pltpu.roll(fi, shift, 0), 0.0)
                pr, pi = cmul(tabs[tab, k], sgn * tabs[tab + 1, k], sr, si)
                fr, fi = fr + pr, fi + pi
            cr, ci = car_r[k], car_i[k]
            jtab = TAB_JB if reverse else TAB_JF
            pr, pi = cmul(tabs[jtab, k], sgn * tabs[jtab + 1, k], cr, ci)
            er, ei = fr + pr, fi + pi
            if reverse:
                inr = jnp.where(row < SUBLANES - 1, pltpu.roll(er, SUBLANES - 1, 0), cr)
                ini = jnp.where(row < SUBLANES - 1, pltpu.roll(ei, SUBLANES - 1, 0), ci)
                car_r[k] = jnp.broadcast_to(er[0:1, :], (SUBLANES, LANES))
                car_i[k] = jnp.broadcast_to(ei[0:1, :], (SUBLANES, LANES))
            else:
                inr = jnp.where(row >= 1, pltpu.roll(er, 1, 0), cr)
                ini = jnp.where(row >= 1, pltpu.roll(ei, 1, 0), ci)
                car_r[k] = jnp.broadcast_to(er[SUBLANES - 1 : SUBLANES, :], (SUBLANES, LANES))
                car_i[k] = jnp.broadcast_to(ei[SUBLANES - 1 : SUBLANES, :], (SUBLANES, LANES))
            incoming += [inr, ini]

        def second_pass(r, _):
            p = (seg - 1 - r) if reverse else r
            for q, k in enumerate(slabs):
                pr, pi = cmul(pows[0, k, p], sgn * pows[1, k, p], incoming[2 * q], incoming[2 * q + 1])
                zr[k, pl.ds(base + r, SUBLANES, stride=seg), :] = zr[k, pl.ds(base + r, SUBLANES, stride=seg), :] + pr
                zi[k, pl.ds(base + r, SUBLANES, stride=seg), :] = zi[k, pl.ds(base + r, SUBLANES, stride=seg), :] + pi
            return 0

        lax.fori_loop(0, seg, second_pass, 0)


def _slabs_to_cols(ref, k0, n):
    return jnp.concatenate([ref[k0 + q] for q in range(n)], axis=1)


def _window_sum(ext, doublings, forward):
    rows = ext.shape[0]
    s = ext
    for k in range(doublings):
        s = s + pltpu.roll(s, (1 << k) if forward else rows - (1 << k), 0)
    return s


def _mix_fwd(h1, g_pre, g_so, g_po, g_post, g_next, dskip, pscale, win, wout, bbre, bbim, ccre, ccim, wgv, wgg, pw, tabs, pows, name, exchange=None):
    T, D = h1.shape
    W = D // 2
    tm = MIX_SUBTILES * MIX_TILE
    seg = MIX_TILE // SUBLANES
    n_slabs = tabs.shape[1]
    nch, cch, sch = bbre.shape
    spc = sch // LANES
    pg = W // len(POOL_WINDOWS)

    def body(h_ref, gpre_ref, gso_ref, gpo_ref, gpost_ref, gnext_ref, dskip_ref, pscale_ref, win_ref, wout_ref, bbre_ref, bbim_ref, ccre_ref, ccim_ref, wgv_ref, wgg_ref, pw_ref, tabs_ref, pows_ref,
             proj_ref, xr_ref, xi_ref, y_ref, pooled_ref, mixed_ref, h2_ref, n2_ref, cat_ref, n3_ref, car_r, car_i, halo):
        i = pl.program_id(0)

        @pl.when(i == 0)
        def _():
            car_r[...] = jnp.zeros_like(car_r)
            car_i[...] = jnp.zeros_like(car_i)
            halo[...] = jnp.zeros_like(halo)

        hh = h_ref[...]
        n2 = (hh * _rms_stat(hh) * gpre_ref[...]).astype(MXU_DTYPE)
        n2_ref[...] = n2.astype(n2_ref.dtype)
        proj = _mm(n2, win_ref[...])
        proj_ref[...] = proj
        us, up = proj[:, :W], proj[:, W:]

        for c in range(nch):
            uc = us[:, c * cch : (c + 1) * cch].astype(MXU_DTYPE)
            bur, bui = _mm(uc, bbre_ref[c]), _mm(uc, bbim_ref[c])
            for q in range(spc):
                xr_ref[c * spc + q] = bur[:, q * LANES : (q + 1) * LANES]
                xi_ref[c * spc + q] = bui[:, q * LANES : (q + 1) * LANES]
        for sub in range(MIX_SUBTILES):
            _scan_inplace(xr_ref, xi_ref, tabs_ref, pows_ref, car_r, car_i, seg, reverse=False, base=sub * MIX_TILE)
        ys = []
        for c in range(nch):
            ys.append(_mm(_slabs_to_cols(xr_ref, c * spc, spc), ccre_ref[c]) - _mm(_slabs_to_cols(xi_ref, c * spc, spc), ccim_ref[c]))
        y = jnp.concatenate(ys, axis=1) + dskip_ref[...] * us
        y_ref[...] = y
        ge = _gelu(y).astype(MXU_DTYPE)
        zv = jnp.concatenate([_mm(ge[:, c * cch : (c + 1) * cch], wgv_ref[c]) for c in range(nch)], axis=1)
        zg = jnp.concatenate([_mm(ge[:, c * cch : (c + 1) * cch], wgg_ref[c]) for c in range(nch)], axis=1)
        out = zv * _sigmoid(zg)
        cat_s = out * _rms_stat(out) * gso_ref[...]

        ext = jnp.concatenate([halo[...], up], axis=0)
        halo[...] = up[tm - POOL_HALO :, :]
        t1 = (i * tm + 1 + lax.broadcasted_iota(jnp.int32, (tm, pg), 0)).astype(F32)
        pooled, pms = [], []
        for g, w in enumerate(POOL_WINDOWS):
            col = ext[:, g * pg : (g + 1) * pg]
            win_sum = _window_sum(col, g + 1, True)[POOL_HALO:, :]
            pooled_g = win_sum / jnp.minimum(t1, float(w)) - up[:, g * pg : (g + 1) * pg]
            pooled.append(pooled_g)
            pms.append(_mm(pooled_g, pw_ref[g]))
        pooled_ref[...] = jnp.concatenate(pooled, axis=1)
        yp = jnp.concatenate(pms, axis=1) * pscale_ref[...]
        cat_p = yp * _rms_stat(yp) * gpo_ref[...]

        cat = jnp.concatenate([cat_s, cat_p], axis=1).astype(MXU_DTYPE)
        cat_ref[...] = cat.astype(cat_ref.dtype)
        mixed = _mm(cat, wout_ref[...])
        mixed_ref[...] = mixed
        h2 = hh + mixed * _rms_stat(mixed) * gpost_ref[...]
        h2_ref[...] = h2
        n3_ref[...] = (h2 * _rms_stat(h2) * gnext_ref[...]).astype(n3_ref.dtype)

    tok = lambda cols, dt=F32: jax.ShapeDtypeStruct((T, cols), dt)
    slab_spec = pl.BlockSpec((n_slabs, tm, LANES), lambda i: (0, i, 0))
    operands = (h1, g_pre, g_so, g_po, g_post, g_next, dskip, pscale, win, wout, bbre, bbim, ccre, ccim, wgv, wgg, pw, tabs, pows)
    return _pallas(
        body,
        name=name,
        grid=(T // tm,),
        out_shape=[tok(D), jax.ShapeDtypeStruct((n_slabs, T, LANES), F32), jax.ShapeDtypeStruct((n_slabs, T, LANES), F32), tok(W), tok(W), tok(D), tok(D), tok(D, MXU_DTYPE), tok(D, MXU_DTYPE),
                   tok(D, MXU_DTYPE)],
        in_specs=[_row_spec(tm, D)] + [_full_spec(o.shape, single=True) for o in operands[1:]],
        out_specs=[_row_spec(tm, D), slab_spec, slab_spec, _row_spec(tm, W), _row_spec(tm, W), _row_spec(tm, D), _row_spec(tm, D), _row_spec(tm, D), _row_spec(tm, D), _row_spec(tm, D)],
        scratch_shapes=[pltpu.VMEM((n_slabs, SUBLANES, LANES), F32), pltpu.VMEM((n_slabs, SUBLANES, LANES), F32), pltpu.VMEM((POOL_HALO, W), F32)],
        operands=operands,
        exchange=exchange,
    )


def _mix_bwd_heads(dh2, mixed, y, pooled, proj, g_so, g_po, g_post, pscale, wout, wgv, wgg, pw, name, exchange=None):
    T, D = dh2.shape
    W = D // 2
    tm = _token_tile(T)
    nch, cch, _ = wgv.shape
    ng, pg, _ = pw.shape

    def body(dh2_ref, mixed_ref, y_ref, pooled_ref, us_ref, gso_ref, gpo_ref, gpost_ref, pscale_ref, wout_ref, wgv_ref, wgg_ref, pw_ref,
             dy_ref, dpooled_ref, dmixed_ref, dgpost_ref, dgso_ref, dgpo_ref, dd_ref, dscale_ref, dwgv_ref, dwgg_ref, dpw_ref):
        first = pl.program_id(0) == 0
        dmixed, dgpost = _rms_bwd(mixed_ref[...], gpost_ref[...], dh2_ref[...])
        _acc(dgpost_ref, dgpost, first)
        dmb = dmixed.astype(MXU_DTYPE)
        dmixed_ref[...] = dmb.astype(dmixed_ref.dtype)
        dcat = _mm_nt(dmb, wout_ref[...])
        dcs, dcp = dcat[:, :W], dcat[:, W:]

        y = y_ref[...]
        ge = _gelu(y).astype(MXU_DTYPE)
        zv = jnp.concatenate([_mm(ge[:, c * cch : (c + 1) * cch], wgv_ref[c]) for c in range(nch)], axis=1)
        zg = jnp.concatenate([_mm(ge[:, c * cch : (c + 1) * cch], wgg_ref[c]) for c in range(nch)], axis=1)
        sg = _sigmoid(zg)
        dout, dgso = _rms_bwd(zv * sg, gso_ref[...], dcs)
        _acc(dgso_ref, dgso, first)
        dzv = (dout * sg).astype(MXU_DTYPE)
        dzg = (dout * zv * sg * (1.0 - sg)).astype(MXU_DTYPE)
        dges = []
        for c in range(nch):
            cs = slice(c * cch, (c + 1) * cch)
            dges.append(_mm_nt(dzv[:, cs], wgv_ref[c]) + _mm_nt(dzg[:, cs], wgg_ref[c]))
            _acc(dwgv_ref.at[c], _mm_tn(ge[:, cs], dzv[:, cs]), first)
            _acc(dwgg_ref.at[c], _mm_tn(ge[:, cs], dzg[:, cs]), first)
        dy = jnp.concatenate(dges, axis=1) * _gelu_grad(y)
        dy_ref[...] = dy
        _acc(dd_ref, jnp.sum(dy * us_ref[...], axis=0, keepdims=True), first)

        pooled_b = pooled_ref[...].astype(MXU_DTYPE)
        pm = jnp.concatenate([_mm(pooled_b[:, g * pg : (g + 1) * pg], pw_ref[g]) for g in range(ng)], axis=1)
        dyp, dgpo = _rms_bwd(pm * pscale_ref[...], gpo_ref[...], dcp)
        _acc(dgpo_ref, dgpo, first)
        _acc(dscale_ref, jnp.sum(dyp * pm, axis=0, keepdims=True), first)
        dpm = (dyp * pscale_ref[...]).astype(MXU_DTYPE)
        dps = []
        for g in range(ng):
            gs = slice(g * pg, (g + 1) * pg)
            dps.append(_mm_nt(dpm[:, gs], pw_ref[g]))
            _acc(dpw_ref.at[g], _mm_tn(pooled_b[:, gs], dpm[:, gs]), first)
        dpooled_ref[...] = jnp.concatenate(dps, axis=1)

    vec = lambda n: jax.ShapeDtypeStruct((1, n), F32)
    operands = (dh2, mixed, y, pooled, proj, g_so, g_po, g_post, pscale, wout, wgv, wgg, pw)
    return _pallas(
        body,
        name=name,
        grid=(T // tm,),
        out_shape=[jax.ShapeDtypeStruct((T, W), F32), jax.ShapeDtypeStruct((T, W), F32), jax.ShapeDtypeStruct((T, D), MXU_DTYPE), vec(D), vec(W), vec(W), vec(W), vec(W),
                   jax.ShapeDtypeStruct(wgv.shape, F32), jax.ShapeDtypeStruct(wgg.shape, F32), jax.ShapeDtypeStruct(pw.shape, F32)],
        in_specs=[_row_spec(tm, D), _row_spec(tm, D), _row_spec(tm, W), _row_spec(tm, W), _row_spec(tm, W)] + [_full_spec(o.shape) for o in operands[5:]],
        out_specs=[_row_spec(tm, W), _row_spec(tm, W), _row_spec(tm, D), _full_spec((1, D)), _full_spec((1, W)), _full_spec((1, W)), _full_spec((1, W)), _full_spec((1, W)),
                   _full_spec(wgv.shape), _full_spec(wgg.shape), _full_spec(pw.shape)],
        operands=operands,
        exchange=exchange,
    )


def _mix_bwd_scan(dy, dpooled, xr, xi, proj, dskip, bbre, bbim, ccre, ccim, tabs, pows, name, exchange=None):
    T, W = dy.shape
    D = 2 * W
    tm = MIX_SUBTILES * MIX_TILE
    seg = MIX_TILE // SUBLANES
    nt = T // tm
    n_slabs = tabs.shape[1]
    nch, cch, sch = bbre.shape
    spc = sch // LANES
    pg = W // len(POOL_WINDOWS)
    blocks_per_tile = tm // SUBLANES

    def body(dy_ref, dp_ref, xr_ref, xi_ref, xpr_ref, xpi_ref, us_ref, dskip_ref, bbre_ref, bbim_ref, ccre_ref, ccim_ref, tabs_ref, pows_ref,
             dproj_ref, dccre_ref, dccim_ref, dbbre_ref, dbbim_ref, dar_ref, dai_ref, lr, li, car_r, car_i, halo):
        i = pl.program_id(0)
        first = i == 0
        tile = nt - 1 - i
        row = lax.broadcasted_iota(jnp.int32, (SUBLANES, LANES), 0)

        @pl.when(first)
        def _():
            car_r[...] = jnp.zeros_like(car_r)
            car_i[...] = jnp.zeros_like(car_i)
            halo[...] = jnp.zeros_like(halo)
            dar_ref[...] = jnp.zeros_like(dar_ref)
            dai_ref[...] = jnp.zeros_like(dai_ref)

        dy = dy_ref[...]
        for c in range(nch):
            dyc = dy[:, c * cch : (c + 1) * cch]
            gr, gi = _mm_nt(dyc, ccre_ref[c]), _mm_nt(dyc, ccim_ref[c])
            for q in range(spc):
                lr[c * spc + q] = gr[:, q * LANES : (q + 1) * LANES]
                li[c * spc + q] = -gi[:, q * LANES : (q + 1) * LANES]
            _acc(dccre_ref.at[c], _mm_tn(_slabs_to_cols(xr_ref, c * spc, spc), dyc), first)
            _acc(dccim_ref.at[c], -_mm_tn(_slabs_to_cols(xi_ref, c * spc, spc), dyc), first)
        for sub in reversed(range(MIX_SUBTILES)):
            _scan_inplace(lr, li, tabs_ref, pows_ref, car_r, car_i, seg, reverse=True, base=sub * MIX_TILE)

        for sub in range(MIX_SUBTILES):
            base = sub * MIX_TILE
            for k0 in range(0, n_slabs, SLAB_GROUP):
                slabs = range(k0, min(k0 + SLAB_GROUP, n_slabs))
                init = []
                for k in slabs:
                    if sub == 0:
                        prev_r = jnp.where(tile > 0, jnp.broadcast_to(xpr_ref[k, SUBLANES - 1 : SUBLANES, :], (SUBLANES, LANES)), 0.0)
                        prev_i = jnp.where(tile > 0, jnp.broadcast_to(xpi_ref[k, SUBLANES - 1 : SUBLANES, :], (SUBLANES, LANES)), 0.0)
                    else:
                        prev_r = jnp.broadcast_to(xr_ref[k, base - 1 : base, :], (SUBLANES, LANES))
                        prev_i = jnp.broadcast_to(xi_ref[k, base - 1 : base, :], (SUBLANES, LANES))
                    x0r = jnp.where(row >= 1, pltpu.roll(xr_ref[k, pl.ds(base + seg - 1, SUBLANES, stride=seg), :], 1, 0), prev_r)
                    x0i = jnp.where(row >= 1, pltpu.roll(xi_ref[k, pl.ds(base + seg - 1, SUBLANES, stride=seg), :], 1, 0), prev_i)
                    l0r, l0i = lr[k, pl.ds(base, SUBLANES, stride=seg), :], li[k, pl.ds(base, SUBLANES, stride=seg), :]
                    init += [l0r * x0r + l0i * x0i, l0i * x0r - l0r * x0i]

                def step(r, acc, slabs=slabs, base=base):
                    out = []
                    for q, k in enumerate(slabs):
                        pr_, pi_ = xr_ref[k, pl.ds(base + r - 1, SUBLANES, stride=seg), :], xi_ref[k, pl.ds(base + r - 1, SUBLANES, stride=seg), :]
                        lr_, li_ = lr[k, pl.ds(base + r, SUBLANES, stride=seg), :], li[k, pl.ds(base + r, SUBLANES, stride=seg), :]
                        out += [acc[2 * q] + lr_ * pr_ + li_ * pi_, acc[2 * q + 1] + li_ * pr_ - lr_ * pi_]
                    return tuple(out)

                sums = lax.fori_loop(1, seg, step, tuple(init))
                for q, k in enumerate(slabs):
                    dar_ref[k] += sums[2 * q]
                    dai_ref[k] += sums[2 * q + 1]

        us = us_ref[...]
        dus = []
        for c in range(nch):
            lrc, lic = _slabs_to_cols(lr, c * spc, spc).astype(MXU_DTYPE), _slabs_to_cols(li, c * spc, spc).astype(MXU_DTYPE)
            uc = us[:, c * cch : (c + 1) * cch]
            _acc(dbbre_ref.at[c], _mm_tn(uc, lrc), first)
            _acc(dbbim_ref.at[c], _mm_tn(uc, lic), first)
            dus.append(_mm_nt(lrc, bbre_ref[c]) + _mm_nt(lic, bbim_ref[c]))
        du_s = jnp.concatenate(dus, axis=1) + dskip_ref[...] * dy

        dp = dp_ref[...]
        t1 = (tile * tm + 1 + lax.broadcasted_iota(jnp.int32, (tm, pg), 0)).astype(F32)
        dups, heads = [], []
        for g, w in enumerate(POOL_WINDOWS):
            dpg = dp[:, g * pg : (g + 1) * pg]
            qg = dpg / jnp.minimum(t1, float(w))
            ext = jnp.concatenate([qg, halo[:, g * pg : (g + 1) * pg]], axis=0)
            dups.append(_window_sum(ext, g + 1, False)[:tm, :] - dpg)
            heads.append(qg[:POOL_HALO, :])
        halo[...] = jnp.concatenate(heads, axis=1)
        dproj_ref[...] = jnp.concatenate([du_s] + dups, axis=1).astype(dproj_ref.dtype)

    rev = lambda cols: _row_spec(tm, cols, rev_n=nt)
    slab_spec = pl.BlockSpec((n_slabs, tm, LANES), lambda i: (0, nt - 1 - i, 0))
    prev_spec = pl.BlockSpec((n_slabs, SUBLANES, LANES), lambda i: (0, jnp.maximum((nt - 1 - i) * blocks_per_tile - 1, 0), 0))
    consts = (dskip, bbre, bbim, ccre, ccim, tabs, pows)
    return _pallas(
        body,
        name=name,
        grid=(nt,),
        out_shape=[jax.ShapeDtypeStruct((T, D), MXU_DTYPE), jax.ShapeDtypeStruct(ccre.shape, F32), jax.ShapeDtypeStruct(ccim.shape, F32), jax.ShapeDtypeStruct(bbre.shape, F32),
                   jax.ShapeDtypeStruct(bbim.shape, F32), jax.ShapeDtypeStruct((n_slabs, SUBLANES, LANES), F32), jax.ShapeDtypeStruct((n_slabs, SUBLANES, LANES), F32)],
        in_specs=[rev(W), rev(W), slab_spec, slab_spec, prev_spec, prev_spec, rev(W)] + [_full_spec(o.shape, single=True) for o in consts],
        out_specs=[rev(D), _full_spec(ccre.shape), _full_spec(ccim.shape), _full_spec(bbre.shape), _full_spec(bbim.shape),
                   _full_spec((n_slabs, SUBLANES, LANES)), _full_spec((n_slabs, SUBLANES, LANES))],
        scratch_shapes=[pltpu.VMEM((n_slabs, tm, LANES), F32), pltpu.VMEM((n_slabs, tm, LANES), F32), pltpu.VMEM((n_slabs, SUBLANES, LANES), F32), pltpu.VMEM((n_slabs, SUBLANES, LANES), F32),
                        pltpu.VMEM((POOL_HALO, W), F32)],
        operands=(dy, dpooled, xr, xi, xr, xi, proj, *consts),
        exchange=exchange,
    )


def _mix_bwd_in(dproj, h1, dh2, g_pre, win, name):
    T, D = h1.shape
    tm = _token_tile(T)

    def body(dproj_ref, h_ref, dh2_ref, gpre_ref, win_ref, dh1_ref, dg_ref):
        dx, dg = _rms_bwd(h_ref[...], gpre_ref[...], _mm_nt(dproj_ref[...], win_ref[...]))
        _acc(dg_ref, dg, pl.program_id(0) == 0)
        dh1_ref[...] = dh2_ref[...] + dx

    outs, _ = _pallas(
        body,
        name=name,
        grid=(T // tm,),
        out_shape=[jax.ShapeDtypeStruct((T, D), F32), jax.ShapeDtypeStruct((1, D), F32)],
        in_specs=[_row_spec(tm, D), _row_spec(tm, D), _row_spec(tm, D), _full_spec((1, D)), _full_spec(win.shape)],
        out_specs=[_row_spec(tm, D), _full_spec((1, D))],
        operands=(dproj, h1, dh2, g_pre, win),
    )
    return outs


def _discretize(lam_re, lam_im, log_dt, b_re, b_im):
    dt = jnp.exp(log_dt)[:, None]
    decay = jnp.exp(lam_re * dt)
    ang = lam_im * dt
    a_re, a_im = decay * jnp.cos(ang), decay * jnp.sin(ang)
    nr = a_re - 1.0
    den = lam_re * lam_re + lam_im * lam_im
    q_re = (nr * lam_re + a_im * lam_im) / den
    q_im = (a_im * lam_re - nr * lam_im) / den
    bb_re = q_re[..., None] * b_re - q_im[..., None] * b_im
    bb_im = q_re[..., None] * b_im + q_im[..., None] * b_re
    return a_re, a_im, bb_re, bb_im


GROUPS_PER_CHUNK = 16


def _block_diag(w, rows_first):
    G = w.shape[0]
    nch = G // GROUPS_PER_CHUNK
    if not rows_first:
        w = jnp.swapaxes(w, 1, 2)
    p, q = w.shape[1], w.shape[2]
    eye = jnp.eye(GROUPS_PER_CHUNK, dtype=w.dtype)
    out = jnp.einsum("cgpq,gk->cgpkq", w.reshape(nch, GROUPS_PER_CHUNK, p, q), eye)
    return out.reshape(nch, GROUPS_PER_CHUNK * p, GROUPS_PER_CHUNK * q)


def _block_diag_extract(m, p, q, rows_first):
    nch = m.shape[0]
    eye = jnp.eye(GROUPS_PER_CHUNK, dtype=m.dtype)
    out = jnp.einsum("cgpkq,gk->cgpq", m.reshape(nch, GROUPS_PER_CHUNK, p, GROUPS_PER_CHUNK, q), eye).reshape(nch * GROUPS_PER_CHUNK, p, q)
    return out if rows_first else jnp.swapaxes(out, 1, 2)


def _cmul(ar, ai, br, bi):
    return ar * br - ai * bi, ar * bi + ai * br


def _powers(ar, ai, count):
    pr, pi = ar[None], ai[None]
    while pr.shape[0] < count:
        nr, ni = _cmul(pr, pi, pr[-1][None], pi[-1][None])
        pr, pi = jnp.concatenate([pr, nr]), jnp.concatenate([pi, ni])
    return pr[:count], pi[:count]


def _scan_tables(a_re, a_im, seg):
    n = a_re.size
    ns = n // LANES
    ar, ai = a_re.reshape(n), a_im.reshape(n)
    pr, pi = _powers(ar, ai, seg)
    jr, ji = _powers(pr[-1], pi[-1], SUBLANES)

    def bcast(v):
        return jnp.broadcast_to(v.reshape(ns, 1, LANES), (ns, SUBLANES, LANES))

    def per_sublane(vs):
        return jnp.transpose(vs.reshape(SUBLANES, ns, LANES), (1, 0, 2))

    tabs = jnp.stack([bcast(ar), bcast(ai), bcast(jr[0]), bcast(ji[0]), bcast(jr[1]), bcast(ji[1]), bcast(jr[3]), bcast(ji[3]),
                      per_sublane(jr), per_sublane(ji), per_sublane(jr[::-1]), per_sublane(ji[::-1])])

    def rows(vs):
        return jnp.broadcast_to(jnp.transpose(vs.reshape(seg, ns, 1, LANES), (1, 0, 2, 3)), (ns, seg, SUBLANES, LANES))

    return tabs, jnp.stack([rows(pr), rows(pi)])


SMALL = ("ffn1_pre_norm", "ffn1_post_norm", "mix_pre_norm", "mix_post_norm", "ssm_lambda_re", "ssm_lambda_im", "ssm_log_dt", "ssm_b_re", "ssm_b_im", "ssm_c_re", "ssm_c_im",
         "ssm_d", "ssm_w_glu", "pool_w", "pool_scale", "ssm_out_norm", "pool_out_norm", "ffn2_pre_norm", "ffn2_post_norm")
BIG = ("ffn1_w_gate", "ffn1_w_up", "ffn1_w_down", "w_in", "w_out", "ffn2_w_gate", "ffn2_w_up", "ffn2_w_down")
ORDER = ("meta_tokens", "ffn1_pre_norm", "ffn1_post_norm", "ffn1_w_gate", "ffn1_w_up", "ffn1_w_down", "mix_pre_norm", "mix_post_norm", "w_in", "ssm_lambda_re", "ssm_lambda_im",
         "ssm_log_dt", "ssm_b_re", "ssm_b_im", "ssm_c_re", "ssm_c_im", "ssm_d", "ssm_w_glu", "pool_w", "pool_scale", "ssm_out_norm", "pool_out_norm", "w_out", "ffn2_pre_norm",
         "ffn2_post_norm", "ffn2_w_gate", "ffn2_w_up", "ffn2_w_down")
PACK_ROWS = SUBLANES * 8
def _pack(arrays, rows):
    flat = jnp.concatenate([a.reshape(-1) for a in arrays])
    return jnp.pad(flat, (0, rows * LANES - flat.size)).reshape(rows, LANES)


def _unpack(packed, shapes):
    flat = packed.reshape(-1)
    out, off = [], 0
    for s in shapes:
        n = math.prod(s)
        out.append(flat[off : off + n].reshape(s))
        off += n
    return out


def _step(p, x, loss_target, m, v):
    D = x.shape[-1]
    chip = (2 * lax.axis_index("x") + lax.axis_index("y")).astype(jnp.int32)
    place = jnp.stack([chip, lax.axis_index("c").astype(jnp.int32)])

    def gather_buffer(w):
        own = w.reshape(1, 2, w.shape[0] // 2, w.shape[1])
        return lax.dynamic_update_slice(lax.empty((4,) + own.shape[1:], own.dtype), own, (chip, 0, 0, 0))

    def rows_of(n, a):
        return jnp.swapaxes(a[0], 0, 1) if n.endswith(("w_gate", "w_up")) else a[0]

    def rows_back(n, a):
        return (jnp.swapaxes(a, 0, 1) if n.endswith(("w_gate", "w_up")) else a)[None]

    def grad_view(g):
        return g.reshape(4, 2, g.shape[0] // 8, g.shape[1])

    def reduce_sum(got_sibling, views, tag):
        sums = [_add_own_half(v_, g_, place, f"{tag}_add_sibling_{k}") for k, (v_, g_) in enumerate(zip(views, got_sibling))]
        return [s[0] for s in sums], [s[1] for s in sums]

    def reduce_halves(parts, got_chips, tag):
        return [_add_chips(p_, g_, place, f"{tag}_add_chips_{k}") for k, (p_, g_) in enumerate(zip(parts, got_chips))]

    bufs = {n: gather_buffer(rows_of(n, p[n]).astype(MXU_DTYPE)) for n in BIG}
    full = {}

    def gathered(names, got):
        full.update({n: g_.reshape(-1, g_.shape[-1]) for n, g_ in zip(names, got)})

    def gather_of(names, n_steps):
        return _Gather([bufs[n] for n in names], mid_step=(3 * n_steps) // 4)

    first_names = ("ffn1_w_gate", "ffn1_w_up")
    got = _exchange_call(_Gather([bufs[n] for n in first_names] + [gather_buffer(p["meta_tokens"])]), "gather_first")
    gathered(first_names, got)
    meta = jnp.transpose(got[-1].reshape(4, N_META, -1), (1, 0, 2)).reshape(N_META, D)

    vec = lambda n: p[n].reshape(1, -1)
    G, N, H = p["ssm_b_re"].shape[1:]
    a_re, a_im, bb_re, bb_im = _discretize(p["ssm_lambda_re"][0], p["ssm_lambda_im"][0], p["ssm_log_dt"][0], p["ssm_b_re"][0], p["ssm_b_im"][0])
    tabs, pows = _scan_tables(a_re, a_im, MIX_TILE // SUBLANES)
    bf = lambda a: a.astype(MXU_DTYPE)
    bbre, bbim = bf(_block_diag(bb_re, False)), bf(_block_diag(bb_im, False))
    ccre, ccim = bf(_block_diag(p["ssm_c_re"][0], False)), bf(_block_diag(p["ssm_c_im"][0], False))
    wgv, wgg = bf(_block_diag(p["ssm_w_glu"][0][:, :, :H], True)), bf(_block_diag(p["ssm_w_glu"][0][:, :, H:], True))
    pw = bf(p["pool_w"][0])

    T = x.shape[1] + N_META
    names = ("ffn1_w_down", "w_in", "w_out", "ffn2_w_gate")
    (ga1, si1, s1, n1), got = _ffn_gate_up(
        x[0], meta, vec("ffn1_pre_norm"), full["ffn1_w_gate"], full["ffn1_w_up"], "ffn1_gate_up", exchange=gather_of(names, T // FFN_TILE)
    )
    gathered(names, got)
    names = ("ffn2_w_up", "ffn2_w_down")
    (h1, f1), got = _ffn_down(x[0], meta, s1, vec("ffn1_post_norm"), full["ffn1_w_down"], "ffn1_down", exchange=gather_of(names, T // _token_tile(T)))
    gathered(names, got)
    (proj, xr, xim, y, pooled, mixed, h2, n2, cat, n3), _ = _mix_fwd(
        h1, vec("mix_pre_norm"), vec("ssm_out_norm"), vec("pool_out_norm"), vec("mix_post_norm"), vec("ffn2_pre_norm"), vec("ssm_d"), vec("pool_scale"),
        full["w_in"], full["w_out"], bbre, bbim, ccre, ccim, wgv, wgg, pw, tabs, pows, "mix_fwd",
    )
    dh3, sq, f2, ga2, si2, s2 = _ffn_fwd_loss(h2, n3, loss_target[0], vec("ffn2_post_norm"), full["ffn2_w_gate"], full["ffn2_w_up"], full["ffn2_w_down"], "ffn2_fwd")
    loss = lax.psum(0.5 * jnp.sum(sq) / D, ("x", "y", "c"))

    g, shared = {}, {}
    ffn_names = lambda tag: (tag + "_w_gate", tag + "_w_up", tag + "_w_down")

    da, db, df, dh2, g["ffn2_post_norm"], g["ffn2_pre_norm"] = _ffn_bwd(
        dh3, f2, ga2, si2, h2, vec("ffn2_post_norm"), vec("ffn2_pre_norm"), full["ffn2_w_gate"], full["ffn2_w_up"], full["ffn2_w_down"], "ffn2_bwd"
    )
    views2 = [
        grad_view(_tn_matmul(da, n3, "ffn2_dw_gate")[0]),
        grad_view(_tn_matmul(db, n3, "ffn2_dw_up")[0]),
        grad_view(_tn_matmul(s2, df, "ffn2_dw_down")[0]),
    ]
    (dy, dpooled, dmixed, g["mix_post_norm"], g["ssm_out_norm"], g["pool_out_norm"], g["ssm_d"], g["pool_scale"], dwgv, dwgg, g["pool_w"]), got = _mix_bwd_heads(
        dh2, mixed, y, pooled, proj, vec("ssm_out_norm"), vec("pool_out_norm"), vec("mix_post_norm"), vec("pool_scale"), full["w_out"], wgv, wgg, pw, "mix_bwd_heads",
        exchange=_SiblingScatter(views2),
    )
    parts2, wire2 = reduce_sum(got, views2, "ffn2")
    (dproj, dccre, dccim, dbbre, dbbim, dar, dai), got = _mix_bwd_scan(
        dy, dpooled, xr, xim, proj, vec("ssm_d"), bbre, bbim, ccre, ccim, tabs, pows, "mix_bwd_scan", exchange=_ChipScatter(wire2)
    )
    dh1, g["mix_pre_norm"] = _mix_bwd_in(dproj, h1, dh2, vec("mix_pre_norm"), full["w_in"], "mix_bwd_in")
    halves2 = reduce_halves(parts2, got, "ffn2")
    dw_in, got = _tn_matmul(n2, dproj, "dw_in", exchange=_SiblingShare(halves2))
    shared.update(zip(ffn_names("ffn2"), got))
    dw_out, _ = _tn_matmul(cat, dmixed, "dw_out")
    views_m = [grad_view(dw_in), grad_view(dw_out)]

    (da, db, df, g["ffn1_post_norm"]), got = _ffn_bwd_down(dh1, f1, ga1, si1, vec("ffn1_post_norm"), full["ffn1_w_down"], "ffn1_bwd_down", exchange=_SiblingScatter(views_m))
    parts_m, wire_m = reduce_sum(got, views_m, "mix")
    dw_down, got = _tn_matmul(s1, df, "ffn1_dw_down", exchange=_ChipScatter(wire_m))
    halves_m = reduce_halves(parts_m, got, "mix")
    views_d = [grad_view(dw_down)]
    ex = _Group([_SiblingShare(halves_m), _SiblingScatter(views_d)])
    dw_gate, got = _tn_matmul(da, n1, "ffn1_dw_gate", exchange=ex)
    got_m, got_d = ex.split(got)
    shared.update(zip(("w_in", "w_out"), got_m))
    parts_d, wire_d = reduce_sum(got_d, views_d, "ffn1_down")
    views_g = [grad_view(dw_gate)]
    ex = _Group([_ChipScatter(wire_d), _SiblingScatter(views_g)])
    dw_up, got = _tn_matmul(db, n1, "ffn1_dw_up", exchange=ex)
    got_d, got_g = ex.split(got)
    halves_d = reduce_halves(parts_d, got_d, "ffn1_down")
    parts_g, wire_g = reduce_sum(got_g, views_g, "ffn1_gate")
    views_u = [grad_view(dw_up)]
    ex = _Group([_SiblingShare(halves_d), _ChipScatter(wire_g), _SiblingScatter(views_u)])
    (grad_x, d_meta, g["ffn1_pre_norm"]), got = _ffn_bwd_up(da, db, x[0], meta, dh1, vec("ffn1_pre_norm"), full["ffn1_w_gate"], full["ffn1_w_up"], "ffn1_bwd_up", exchange=ex)
    got_d, got_g, got_u = ex.split(got)
    shared["ffn1_w_down"] = got_d[0]
    halves_g = reduce_halves(parts_g, got_g, "ffn1_gate")
    parts_u, wire_u = reduce_sum(got_u, views_u, "ffn1_up")
    grad_x = grad_x[None]

    g["ssm_c_re"] = _block_diag_extract(dccre, N, H, False)
    g["ssm_c_im"] = _block_diag_extract(dccim, N, H, False)
    g["ssm_w_glu"] = jnp.concatenate([_block_diag_extract(dwgv, H, H, True), _block_diag_extract(dwgg, H, H, True)], axis=-1)
    d_a_re, d_a_im = jnp.sum(dar, axis=1).reshape(G, N), jnp.sum(dai, axis=1).reshape(G, N)
    _, pull = jax.vjp(_discretize, p["ssm_lambda_re"][0], p["ssm_lambda_im"][0], p["ssm_log_dt"][0], p["ssm_b_re"][0], p["ssm_b_im"][0])
    g["ssm_lambda_re"], g["ssm_lambda_im"], g["ssm_log_dt"], g["ssm_b_re"], g["ssm_b_im"] = pull(
        (d_a_re, d_a_im, _block_diag_extract(dbbre, H, N, False), _block_diag_extract(dbbim, H, N, False))
    )

    small_shapes = [p[n].shape for n in SMALL] + [(N_META, D)]
    small_size = sum(math.prod(s) for s in small_shapes)
    rows = -(-small_size // (LANES * PACK_ROWS)) * PACK_ROWS
    views_s = [_pack([g[n] for n in SMALL] + [d_meta], rows).reshape(4, 2, rows // 8, LANES)]
    parts_s, wire_s = reduce_sum(_exchange_call(_SiblingScatter(views_s), "small_reduce_sibling"), views_s, "small")
    ex = _Group([_ChipScatter(wire_u + wire_s), _SiblingShare(halves_g)])
    got_c, got_g = ex.split(_exchange_call(ex, "tail_reduce_chips"))
    shared["ffn1_w_gate"] = got_g[0]
    got = _exchange_call(_SiblingShare(reduce_halves(parts_u + parts_s, got_c, "tail")), "tail_reduce_share")
    shared["ffn1_w_up"] = got[0]
    small_buf = lax.dynamic_update_slice(lax.empty((4,) + got[1].shape, F32), got[1][None], (chip, 0, 0, 0))
    small_all = _exchange_call(_Gather([small_buf]), "gather_small")[0].reshape(rows, LANES)
    grads = dict(zip(SMALL + ("meta_full",), _unpack(small_all, small_shapes)))
    grads["meta_tokens"] = lax.dynamic_slice_in_dim(grads.pop("meta_full"), chip * (D // 4), D // 4, axis=1)
    delta, new_m, new_v = {}, {}, {}
    for n in BIG:
        g_rows = shared[n].reshape(-1, shared[n].shape[-1])
        outs = _adamw(rows_of(n, p[n]), g_rows, rows_of(n, m[n]), rows_of(n, v[n]), "adamw_" + n)
        grads[n], delta[n], new_m[n], new_v[n] = (rows_back(n, a) for a in (g_rows, *outs))
    delta["meta_tokens"], new_m["meta_tokens"], new_v["meta_tokens"] = _adamw(p["meta_tokens"], grads["meta_tokens"], m["meta_tokens"], v["meta_tokens"], "adamw_meta_tokens")

    def as_2d(n, a):
        a = a.reshape(p[n].shape)[0]
        if n in ("ssm_b_re", "ssm_b_im"):
            a = jnp.swapaxes(a, 1, 2)
        return a.reshape(-1, a.shape[-1])

    def from_2d(n, a):
        if n in ("ssm_b_re", "ssm_b_im"):
            g_, n_, h_ = p[n].shape[1:]
            return jnp.swapaxes(a.reshape(g_, h_, n_), 1, 2)[None]
        return a.reshape(p[n].shape)

    outs = _adamw_many(*[[as_2d(n, t[n]) for n in SMALL] for t in (p, grads, m, v)], "adamw_small")
    for out, arrays in zip((delta, new_m, new_v), outs):
        out.update({n: from_2d(n, a) for n, a in zip(SMALL, arrays)})

    return (loss, grad_x, *[grads[n] for n in ORDER], *[delta[n] for n in ORDER], *[new_m[n] for n in ORDER], *[new_v[n] for n in ORDER])


def kernel(x, meta_tokens, ffn1_pre_norm, ffn1_post_norm, ffn1_w_gate, ffn1_w_up, ffn1_w_down, mix_pre_norm, mix_post_norm, w_in, ssm_lambda_re, ssm_lambda_im, ssm_log_dt, ssm_b_re, ssm_b_im, ssm_c_re, ssm_c_im, ssm_d, ssm_w_glu, pool_w, pool_scale, ssm_out_norm, pool_out_norm, w_out, ffn2_pre_norm, ffn2_post_norm, ffn2_w_gate, ffn2_w_up, ffn2_w_down, loss_target, m_meta_tokens, m_ffn1_pre_norm, m_ffn1_post_norm, m_ffn1_w_gate, m_ffn1_w_up, m_ffn1_w_down, m_mix_pre_norm, m_mix_post_norm, m_w_in, m_ssm_lambda_re, m_ssm_lambda_im, m_ssm_log_dt, m_ssm_b_re, m_ssm_b_im, m_ssm_c_re, m_ssm_c_im, m_ssm_d, m_ssm_w_glu, m_pool_w, m_pool_scale, m_ssm_out_norm, m_pool_out_norm, m_w_out, m_ffn2_pre_norm, m_ffn2_post_norm, m_ffn2_w_gate, m_ffn2_w_up, m_ffn2_w_down, v_meta_tokens, v_ffn1_pre_norm, v_ffn1_post_norm, v_ffn1_w_gate, v_ffn1_w_up, v_ffn1_w_down, v_mix_pre_norm, v_mix_post_norm, v_w_in, v_ssm_lambda_re, v_ssm_lambda_im, v_ssm_log_dt, v_ssm_b_re, v_ssm_b_im, v_ssm_c_re, v_ssm_c_im, v_ssm_d, v_ssm_w_glu, v_pool_w, v_pool_scale, v_ssm_out_norm, v_pool_out_norm, v_w_out, v_ffn2_pre_norm, v_ffn2_post_norm, v_ffn2_w_gate, v_ffn2_w_up, v_ffn2_w_down):
    args = locals()
    p = {n: args[n] for n in ORDER}
    m = {n: args["m_" + n] for n in ORDER}
    v = {n: args["v_" + n] for n in ORDER}
    return _step(p, x, loss_target, m, v)
```

```python
import math

import jax
import jax.numpy as jnp
from jax import lax
from jax.experimental import pallas as pl
from jax.experimental.pallas import tpu as pltpu

F32 = jnp.float32
MXU_DTYPE = jnp.bfloat16
WIRE_DTYPE = jnp.bfloat16

RMS_EPS = 1e-6
N_META = 16
POOL_WINDOWS = (2, 4, 8, 16)
POOL_HALO = 16
ADAM_LR, ADAM_B1, ADAM_B2, ADAM_EPS, ADAM_WD, ADAM_STEP = 0.001, 0.9, 0.999, 1e-08, 0.01, 10

LANES = 128
SUBLANES = 8
VMEM_LIMIT = 60 * 1024 * 1024
FFN_TILE = 432
FFN_CHUNK = 1024
TN_TILE = 1024
MIX_TILE = 216
MIX_SUBTILES = 2
SLAB_GROUP = 8
MESH = pl.DeviceIdType.MESH
ANY = pl.BlockSpec(memory_space=pl.ANY)


def _mm(a, b):
    return jnp.dot(a.astype(MXU_DTYPE), b.astype(MXU_DTYPE), preferred_element_type=F32)


def _mm_nt(a, b):
    return lax.dot_general(a.astype(MXU_DTYPE), b.astype(MXU_DTYPE), (((1,), (1,)), ((), ())), preferred_element_type=F32)


def _mm_tn(a, b):
    return lax.dot_general(a.astype(MXU_DTYPE), b.astype(MXU_DTYPE), (((0,), (0,)), ((), ())), preferred_element_type=F32)


def _rms_stat(x):
    return lax.rsqrt(jnp.mean(x * x, axis=-1, keepdims=True) + RMS_EPS)


def _rms_bwd(x, g, dy):
    r = _rms_stat(x)
    xh = x * r
    dg = jnp.sum(dy * xh, axis=0, keepdims=True)
    dxh = dy * g
    dx = r * (dxh - xh * jnp.mean(dxh * xh, axis=-1, keepdims=True))
    return dx, dg


def _sigmoid(x):
    return 1.0 / (1.0 + jnp.exp(-x))


GELU_C = math.sqrt(2.0 / math.pi)
GELU_K = 0.044715


def _gelu(y):
    return 0.5 * y * (1.0 + jnp.tanh(GELU_C * (y + GELU_K * y * y * y)))


def _gelu_grad(y):
    th = jnp.tanh(GELU_C * (y + GELU_K * y * y * y))
    return 0.5 * (1.0 + th) + 0.5 * y * (1.0 - th * th) * GELU_C * (1.0 + 3.0 * GELU_K * y * y)


def _row_spec(tile, cols, rev_n=None):
    if rev_n is None:
        return pl.BlockSpec((tile, cols), lambda i: (i, 0))
    return pl.BlockSpec((tile, cols), lambda i: (rev_n - 1 - i, 0))


def _full_spec(shape, single=False):
    zeros = (0,) * len(shape)
    if single:
        return pl.BlockSpec(shape, lambda *_: zeros, pipeline_mode=pl.Buffered(1))
    return pl.BlockSpec(shape, lambda *_: zeros)


def _acc(ref, val, first):
    @pl.when(first)
    def _():
        ref[...] = val

    @pl.when(jnp.logical_not(first))
    def _():
        ref[...] += val


def _place():
    x, y, c = lax.axis_index("x"), lax.axis_index("y"), lax.axis_index("c")
    others = [(1 - x, y), (x, 1 - y), (1 - x, 1 - y)]
    return x, y, c, others


class _Exchange:
    mid_step = None

    def __init__(self, ins, out_shapes, aliases, n_sems):
        self.ins, self.out_shapes, self.aliases, self.n_sems = list(ins), list(out_shapes), dict(aliases), n_sems

    def mid(self, ins, outs, send_sems, recv_sems):
        pass


class _SiblingScatter(_Exchange):
    def __init__(self, views):
        super().__init__(views, [jax.ShapeDtypeStruct((4,) + v.shape[2:], v.dtype) for v in views], {}, 4 * len(views))

    def _copies(self, ins, outs, send_sems, recv_sems):
        x, y, c, _ = _place()
        return [
            pltpu.make_async_remote_copy(src_ref=ins[a].at[k, 1 - c], dst_ref=outs[a].at[k], send_sem=send_sems.at[4 * a + k], recv_sem=recv_sems.at[4 * a + k], device_id=(x, y, 1 - c), device_id_type=MESH)
            for a in range(len(ins))
            for k in range(4)
        ]

    def start(self, *refs):
        for cp in self._copies(*refs):
            cp.start()

    def finish(self, *refs):
        cps = self._copies(*refs)
        for cp in cps:
            cp.wait_recv()
        for cp in cps:
            cp.wait_send()


class _ChipScatter(_Exchange):
    def __init__(self, parts):
        super().__init__(parts, [jax.ShapeDtypeStruct((3,) + p.shape[1:], p.dtype) for p in parts], {}, 3 * len(parts))

    def _copies(self, ins, outs, send_sems, recv_sems):
        x, y, c, others = _place()
        return [
            pltpu.make_async_remote_copy(src_ref=ins[a].at[2 * chip[0] + chip[1]], dst_ref=outs[a].at[j], send_sem=send_sems.at[3 * a + j], recv_sem=recv_sems.at[3 * a + j], device_id=(*chip, c), device_id_type=MESH)
            for a in range(len(ins))
            for j, chip in enumerate(others)
        ]

    start = _SiblingScatter.start
    finish = _SiblingScatter.finish


class _SiblingShare(_Exchange):
    def __init__(self, bufs):
        super().__init__(bufs, [jax.ShapeDtypeStruct(b.shape, b.dtype) for b in bufs], {a: a for a in range(len(bufs))}, len(bufs))

    def _copy(self, outs, send_sems, recv_sems, a, half):
        x, y, c, _ = _place()
        mine = outs[a].at[c if half == "mine" else 1 - c]
        return pltpu.make_async_remote_copy(src_ref=mine, dst_ref=mine, send_sem=send_sems.at[a], recv_sem=recv_sems.at[a], device_id=(x, y, 1 - c), device_id_type=MESH)

    def start(self, ins, outs, send_sems, recv_sems):
        for a in range(len(outs)):
            self._copy(outs, send_sems, recv_sems, a, "mine").start()

    def finish(self, ins, outs, send_sems, recv_sems):
        for a in range(len(outs)):
            self._copy(outs, send_sems, recv_sems, a, "theirs").wait_recv()
        for a in range(len(outs)):
            self._copy(outs, send_sems, recv_sems, a, "mine").wait_send()


class _Gather(_Exchange):
    def __init__(self, bufs, mid_step=None):
        super().__init__(bufs, [jax.ShapeDtypeStruct(b.shape, b.dtype) for b in bufs], {a: a for a in range(len(bufs))}, 6 * len(bufs))
        self.mid_step = mid_step

    def _copy(self, outs, send_sems, recv_sems, a, j, chip, half, to):
        blk = outs[a].at[2 * chip[0] + chip[1], half]
        return pltpu.make_async_remote_copy(src_ref=blk, dst_ref=blk, send_sem=send_sems.at[6 * a + j], recv_sem=recv_sems.at[6 * a + j], device_id=to, device_id_type=MESH)

    def start(self, ins, outs, send_sems, recv_sems):
        x, y, c, others = _place()
        for a in range(len(outs)):
            for j, chip in enumerate(others):
                self._copy(outs, send_sems, recv_sems, a, j, (x, y), c, (*chip, c)).start()

    def mid(self, ins, outs, send_sems, recv_sems):
        x, y, c, others = _place()
        for a in range(len(outs)):
            for j, chip in enumerate(others):
                self._copy(outs, send_sems, recv_sems, a, j, chip, c, (x, y, c)).wait_recv()
                self._copy(outs, send_sems, recv_sems, a, 3 + j, chip, c, (x, y, 1 - c)).start()

    def finish(self, ins, outs, send_sems, recv_sems):
        x, y, c, others = _place()
        for a in range(len(outs)):
            for j, chip in enumerate(others):
                self._copy(outs, send_sems, recv_sems, a, 3 + j, chip, 1 - c, (x, y, c)).wait_recv()
        for a in range(len(outs)):
            for j, chip in enumerate(others):
                self._copy(outs, send_sems, recv_sems, a, j, (x, y), c, (*chip, c)).wait_send()
                self._copy(outs, send_sems, recv_sems, a, 3 + j, chip, c, (x, y, 1 - c)).wait_send()


class _SemSlice:
    def __init__(self, sems, off):
        self.sems, self.off = sems, off

    @property
    def at(self):
        return self

    def __getitem__(self, i):
        return self.sems.at[self.off + i]


class _Group(_Exchange):
    def __init__(self, exchanges):
        ins, outs, aliases, n_sems, self.spans = [], [], {}, 0, []
        for ex in exchanges:
            self.spans.append((len(ins), len(outs), n_sems))
            aliases.update({len(ins) + i: len(outs) + o for i, o in ex.aliases.items()})
            ins, outs, n_sems = ins + ex.ins, outs + ex.out_shapes, n_sems + ex.n_sems
        super().__init__(ins, outs, aliases, n_sems)
        self.exchanges = exchanges
        mids = {ex.mid_step for ex in exchanges if ex.mid_step is not None}
        self.mid_step = mids.pop() if mids else None

    def _each(self, method, ins, outs, send_sems, recv_sems):
        for ex, (i0, o0, s0) in zip(self.exchanges, self.spans):
            getattr(ex, method)(ins[i0 : i0 + len(ex.ins)], outs[o0 : o0 + len(ex.out_shapes)], _SemSlice(send_sems, s0), _SemSlice(recv_sems, s0))

    def start(self, *refs):
        self._each("start", *refs)

    def mid(self, *refs):
        self._each("mid", *refs)

    def finish(self, *refs):
        self._each("finish", *refs)

    def split(self, outs):
        return [outs[o0 : o0 + len(ex.out_shapes)] for ex, (_, o0, _) in zip(self.exchanges, self.spans)]


def _exchange_call(ex, name):
    n, m = len(ex.ins), len(ex.out_shapes)

    def body(*refs):
        parts = (refs[:n], refs[n : n + m], refs[n + m], refs[n + m + 1])
        ex.start(*parts)
        ex.mid(*parts)
        ex.finish(*parts)

    return pl.pallas_call(
        body,
        name=name,
        out_shape=ex.out_shapes,
        in_specs=[ANY] * n,
        out_specs=[ANY] * m,
        scratch_shapes=[pltpu.SemaphoreType.DMA((ex.n_sems,)), pltpu.SemaphoreType.DMA((ex.n_sems,))],
        input_output_aliases=ex.aliases,
    )(*ex.ins)


def _pallas(body, *, name, grid, in_specs, out_specs, out_shape, operands, scratch_shapes=(), exchange=None):
    params = pltpu.CompilerParams(dimension_semantics=("arbitrary",) * len(grid), vmem_limit_bytes=VMEM_LIMIT)
    if exchange is None:
        outs = pl.pallas_call(body, name=name, grid=grid, in_specs=in_specs, out_specs=out_specs, out_shape=out_shape, scratch_shapes=list(scratch_shapes), compiler_params=params)(*operands)
        return outs, []
    ex = exchange
    n_in, n_out, n_scr = len(in_specs), len(out_specs), len(scratch_shapes)
    x_in, x_out = len(ex.ins), len(ex.out_shapes)

    def hosted(*refs):
        ins, x_ins = refs[:n_in], refs[n_in : n_in + x_in]
        outs, x_outs = refs[n_in + x_in : n_in + x_in + n_out], refs[n_in + x_in + n_out : n_in + x_in + n_out + x_out]
        rest = refs[n_in + x_in + n_out + x_out :]
        parts = (x_ins, x_outs, rest[n_scr], rest[n_scr + 1])
        ids = [pl.program_id(d) for d in range(len(grid))]
        first = _all([i == 0 for i in ids])
        last = _all([i == g - 1 for i, g in zip(ids, grid)])

        @pl.when(first)
        def _():
            ex.start(*parts)

        body(*ins, *outs, *rest[:n_scr])

        if ex.mid_step is not None:

            @pl.when(ids[0] == ex.mid_step)
            def _():
                ex.mid(*parts)

        @pl.when(last)
        def _():
            ex.finish(*parts)

    outs = pl.pallas_call(
        hosted,
        name=name,
        grid=grid,
        in_specs=list(in_specs) + [ANY] * x_in,
        out_specs=list(out_specs) + [ANY] * x_out,
        out_shape=list(out_shape) + ex.out_shapes,
        scratch_shapes=list(scratch_shapes) + [pltpu.SemaphoreType.DMA((ex.n_sems,)), pltpu.SemaphoreType.DMA((ex.n_sems,))],
        input_output_aliases={n_in + i: n_out + o for i, o in ex.aliases.items()},
        compiler_params=params,
    )(*operands, *ex.ins)
    return outs[:n_out], outs[n_out:]


def _all(conds):
    out = conds[0]
    for c in conds[1:]:
        out = jnp.logical_and(out, c)
    return out


def _row_tile(rows):
    if rows <= 512:
        return rows
    for t in (512, 352, 256, 176, 128, 112, 64, 32, 16, 8):
        if rows % t == 0:
            return t
    return rows


def _add_own_half(view, got, place, name):
    _, _, r, c = view.shape
    tr = _row_tile(r)

    def body(place_ref, v_ref, g_ref, o_ref, w_ref):
        s = v_ref[...] + g_ref[...]
        w_ref[...] = s.astype(w_ref.dtype)

        @pl.when(pl.program_id(1) == place_ref[0])
        def _():
            o_ref[...] = s

    blk = pl.BlockSpec((None, tr, c), lambda i, k, pr: (k, i, 0))
    return pl.pallas_call(
        body,
        name=name,
        out_shape=[jax.ShapeDtypeStruct((r, c), F32), jax.ShapeDtypeStruct((4, r, c), WIRE_DTYPE)],
        grid_spec=pltpu.PrefetchScalarGridSpec(
            num_scalar_prefetch=1,
            grid=(r // tr, 4),
            in_specs=[pl.BlockSpec((None, None, tr, c), lambda i, k, pr: (k, pr[1], i, 0)), blk],
            out_specs=[pl.BlockSpec((tr, c), lambda i, k, pr: (i, 0)), blk],
        ),
        compiler_params=pltpu.CompilerParams(dimension_semantics=("arbitrary", "arbitrary"), vmem_limit_bytes=VMEM_LIMIT),
    )(place, view, got)


def _add_chips(part, got, place, name):
    r, c = part.shape
    tr = _row_tile(r)

    def body(place_ref, p_ref, g_ref, o_ref):
        o_ref[...] = ((p_ref[...] + g_ref[0].astype(F32)) + g_ref[1].astype(F32)) + g_ref[2].astype(F32)

    return pl.pallas_call(
        body,
        name=name,
        out_shape=jax.ShapeDtypeStruct((2, r, c), F32),
        grid_spec=pltpu.PrefetchScalarGridSpec(
            num_scalar_prefetch=1,
            grid=(r // tr,),
            in_specs=[pl.BlockSpec((tr, c), lambda i, pr: (i, 0)), pl.BlockSpec((3, tr, c), lambda i, pr: (0, i, 0))],
            out_specs=pl.BlockSpec((None, tr, c), lambda i, pr: (pr[1], i, 0)),
        ),
        compiler_params=pltpu.CompilerParams(dimension_semantics=("arbitrary",), vmem_limit_bytes=VMEM_LIMIT),
    )(place, part, got)


def _adamw_update(w_ref, g_ref, m_ref, v_ref, d_ref, nm_ref, nv_ref):
    g = g_ref[...]
    nm = ADAM_B1 * m_ref[...] + (1.0 - ADAM_B1) * g
    nv = ADAM_B2 * v_ref[...] + (1.0 - ADAM_B2) * (g * g)
    m_hat = nm / (1.0 - ADAM_B1**ADAM_STEP)
    v_hat = nv / (1.0 - ADAM_B2**ADAM_STEP)
    d_ref[...] = -ADAM_LR * (m_hat / (jnp.sqrt(v_hat) + ADAM_EPS) + ADAM_WD * w_ref[...])
    nm_ref[...] = nm
    nv_ref[...] = nv


def _adamw(w, g, m, v, name):
    r, c = w.shape
    tr = _row_tile(r)
    spec = pl.BlockSpec((tr, c), lambda i: (i, 0))
    outs, _ = _pallas(_adamw_update, name=name, grid=(r // tr,), in_specs=[spec] * 4, out_specs=[spec] * 3, out_shape=[jax.ShapeDtypeStruct((r, c), F32)] * 3, operands=(w, g, m, v))
    return outs


def _adamw_many(ws, gs, ms, vs, name):
    n = len(ws)

    def body(*refs):
        for k in range(n):
            _adamw_update(*(refs[j * n + k] for j in range(7)))

    outs = pl.pallas_call(
        body,
        name=name,
        out_shape=[jax.ShapeDtypeStruct(w.shape, F32) for w in ws] * 3,
        in_specs=[pl.BlockSpec(memory_space=pltpu.VMEM)] * (4 * n),
        out_specs=[pl.BlockSpec(memory_space=pltpu.VMEM)] * (3 * n),
    )(*ws, *gs, *ms, *vs)
    return outs[:n], outs[n : 2 * n], outs[2 * n :]


def _load_weights(pairs, sems):
    @pl.when(pl.program_id(0) == 0)
    def _():
        cps = [pltpu.make_async_copy(src, dst, sems.at[k]) for k, (src, dst) in enumerate(pairs)]
        for cp in cps:
            cp.start()
        for cp in cps:
            cp.wait()


def _ffn_chunks(F):
    bounds = list(range(0, F, FFN_CHUNK)) + [F]
    return list(zip(bounds[:-1], bounds[1:]))


def _shifted_specs(tm, cols):
    per = tm // N_META
    return [_row_spec(tm, cols), pl.BlockSpec((N_META, cols), lambda i: (jnp.maximum(i * per - 1, 0), 0))]


def _shifted_tile(cur_ref, before_ref, tm):
    return jnp.concatenate([before_ref[...], cur_ref[0 : tm - N_META, :]], axis=0)


def _tokens_tile(cur_ref, before_ref, meta_ref, tm, tile_0):
    first = jnp.where(tile_0, meta_ref[...], before_ref[...])
    return jnp.concatenate([first, cur_ref[0 : tm - N_META, :]], axis=0)


def _ffn_fwd_loss(h, target, g_pre, g_post, wg, wu, wd, name):
    T, D = h.shape
    F = wg.shape[0]
    tm = FFN_TILE

    def body(h_ref, t_ref, tb_ref, gpre_ref, gpost_ref, wg_hbm, wu_hbm, wd_hbm, dy_ref, sq_ref, f_ref, ga_ref, si_ref, s_ref, n_ref, wg_v, wu_v, wd_v, sems):
        i = pl.program_id(0)
        _load_weights([(wg_hbm, wg_v), (wu_hbm, wu_v), (wd_hbm, wd_v)], sems)
        hh = h_ref[...]
        n = (hh * _rms_stat(hh) * gpre_ref[...]).astype(MXU_DTYPE)
        n_ref[...] = n.astype(n_ref.dtype)
        f = jnp.zeros((tm, D), F32)
        for lo, hi in _ffn_chunks(F):
            a = _mm_nt(n, wg_v[lo:hi, :])
            b = _mm_nt(n, wu_v[lo:hi, :])
            sg = _sigmoid(a)
            si = a * sg
            s = (si * b).astype(MXU_DTYPE)
            ga_ref[:, lo:hi] = (b * (sg * (1.0 + a * (1.0 - sg)))).astype(ga_ref.dtype)
            si_ref[:, lo:hi] = si.astype(si_ref.dtype)
            s_ref[:, lo:hi] = s.astype(s_ref.dtype)
            f = f + _mm(s, wd_v[lo:hi, :])
        f_ref[...] = f
        y = hh + 0.5 * (f * _rms_stat(f) * gpost_ref[...])
        rows = i * tm + lax.broadcasted_iota(jnp.int32, (tm, D), 0)
        err = jnp.where(rows >= N_META, y - _shifted_tile(t_ref, tb_ref, tm), 0.0)
        dy_ref[...] = err * (1.0 / D)
        _acc(sq_ref, jnp.sum(err * err, axis=0, keepdims=True), i == 0)

    tok = jax.ShapeDtypeStruct((T, D), F32)
    act = jax.ShapeDtypeStruct((T, F), MXU_DTYPE)
    outs, _ = _pallas(
        body,
        name=name,
        grid=(T // tm,),
        out_shape=[tok, jax.ShapeDtypeStruct((1, D), F32), tok, act, act, act, jax.ShapeDtypeStruct((T, D), MXU_DTYPE)],
        in_specs=[_row_spec(tm, D)] + _shifted_specs(tm, D) + [_full_spec((1, D)), _full_spec((1, D)), ANY, ANY, ANY],
        out_specs=[_row_spec(tm, D), _full_spec((1, D)), _row_spec(tm, D), _row_spec(tm, F), _row_spec(tm, F), _row_spec(tm, F), _row_spec(tm, D)],
        scratch_shapes=[pltpu.VMEM(wg.shape, wg.dtype), pltpu.VMEM(wu.shape, wu.dtype), pltpu.VMEM(wd.shape, wd.dtype), pltpu.SemaphoreType.DMA((3,))],
        operands=(h, target, target, g_pre, g_post, wg, wu, wd),
    )
    return outs


def _ffn_gate_up(x, meta, g_pre, wg, wu, name, exchange=None):
    D = x.shape[1]
    T = x.shape[0] + N_META
    F = wg.shape[0]
    tm = FFN_TILE

    def body(x_ref, xb_ref, meta_ref, gpre_ref, wg_hbm, wu_hbm, ga_ref, si_ref, s_ref, n_ref, wg_v, wu_v, sems):
        _load_weights([(wg_hbm, wg_v), (wu_hbm, wu_v)], sems)
        hh = _tokens_tile(x_ref, xb_ref, meta_ref, tm, pl.program_id(0) == 0)
        n = (hh * _rms_stat(hh) * gpre_ref[...]).astype(MXU_DTYPE)
        n_ref[...] = n.astype(n_ref.dtype)
        for lo, hi in _ffn_chunks(F):
            a = _mm_nt(n, wg_v[lo:hi, :])
            b = _mm_nt(n, wu_v[lo:hi, :])
            sg = _sigmoid(a)
            si = a * sg
            ga_ref[:, lo:hi] = (b * (sg * (1.0 + a * (1.0 - sg)))).astype(ga_ref.dtype)
            si_ref[:, lo:hi] = si.astype(si_ref.dtype)
            s_ref[:, lo:hi] = (si * b).astype(s_ref.dtype)

    act = jax.ShapeDtypeStruct((T, F), MXU_DTYPE)
    return _pallas(
        body,
        name=name,
        grid=(T // tm,),
        out_shape=[act, act, act, jax.ShapeDtypeStruct((T, D), MXU_DTYPE)],
        in_specs=_shifted_specs(tm, D) + [_full_spec((N_META, D)), _full_spec((1, D)), ANY, ANY],
        out_specs=[_row_spec(tm, F), _row_spec(tm, F), _row_spec(tm, F), _row_spec(tm, D)],
        scratch_shapes=[pltpu.VMEM(wg.shape, wg.dtype), pltpu.VMEM(wu.shape, wu.dtype), pltpu.SemaphoreType.DMA((2,))],
        operands=(x, x, meta, g_pre, wg, wu),
        exchange=exchange,
    )


def _ffn_down(x, meta, s, g_post, wd, name, exchange=None):
    D = x.shape[1]
    T = x.shape[0] + N_META
    F = wd.shape[0]
    tm = FFN_TILE

    def body(x_ref, xb_ref, meta_ref, s_ref, gpost_ref, wd_hbm, hout_ref, f_ref, wd_v, sems):
        _load_weights([(wd_hbm, wd_v)], sems)
        f = _mm(s_ref[...], wd_v[...])
        f_ref[...] = f
        hout_ref[...] = _tokens_tile(x_ref, xb_ref, meta_ref, tm, pl.program_id(0) == 0) + 0.5 * (f * _rms_stat(f) * gpost_ref[...])

    tok = jax.ShapeDtypeStruct((T, D), F32)
    return _pallas(
        body,
        name=name,
        grid=(T // tm,),
        out_shape=[tok, tok],
        in_specs=_shifted_specs(tm, D) + [_full_spec((N_META, D)), _row_spec(tm, F), _full_spec((1, D)), ANY],
        out_specs=[_row_spec(tm, D), _row_spec(tm, D)],
        scratch_shapes=[pltpu.VMEM(wd.shape, wd.dtype), pltpu.SemaphoreType.DMA((1,))],
        operands=(x, x, meta, s, g_post, wd),
        exchange=exchange,
    )


def _ffn_bwd_down(dh, f, ga, si, g_post, wd, name, exchange=None):
    T, D = dh.shape
    F = wd.shape[0]
    tm = FFN_TILE

    def body(dh_ref, f_ref, ga_ref, si_ref, gpost_ref, wd_hbm, da_ref, db_ref, df_ref, dg_ref, wd_v, sems):
        _load_weights([(wd_hbm, wd_v)], sems)
        df, dg = _rms_bwd(f_ref[...], gpost_ref[...], 0.5 * dh_ref[...])
        _acc(dg_ref, dg, pl.program_id(0) == 0)
        dfb = df.astype(MXU_DTYPE)
        df_ref[...] = dfb.astype(df_ref.dtype)
        for lo, hi in _ffn_chunks(F):
            ds = _mm_nt(dfb, wd_v[lo:hi, :])
            da_ref[:, lo:hi] = (ds * ga_ref[:, lo:hi].astype(F32)).astype(da_ref.dtype)
            db_ref[:, lo:hi] = (ds * si_ref[:, lo:hi].astype(F32)).astype(db_ref.dtype)

    act = jax.ShapeDtypeStruct((T, F), MXU_DTYPE)
    return _pallas(
        body,
        name=name,
        grid=(T // tm,),
        out_shape=[act, act, jax.ShapeDtypeStruct((T, D), MXU_DTYPE), jax.ShapeDtypeStruct((1, D), F32)],
        in_specs=[_row_spec(tm, D), _row_spec(tm, D), _row_spec(tm, F), _row_spec(tm, F), _full_spec((1, D)), ANY],
        out_specs=[_row_spec(tm, F), _row_spec(tm, F), _row_spec(tm, D), _full_spec((1, D))],
        scratch_shapes=[pltpu.VMEM(wd.shape, wd.dtype), pltpu.SemaphoreType.DMA((1,))],
        operands=(dh, f, ga, si, g_post, wd),
        exchange=exchange,
    )


def _ffn_bwd_up(da, db, x, meta, dh, g_pre, wg, wu, name, exchange=None):
    D = x.shape[1]
    T = x.shape[0] + N_META
    F = wg.shape[0]
    tm = FFN_TILE
    nt = T // tm
    per = tm // N_META
    tile = lambda i: jnp.minimum(i, nt - 1)

    def body(da_ref, db_ref, x_ref, xb_ref, meta_ref, dh_ref, gpre_ref, wg_hbm, wu_hbm, dx_ref, dmeta_ref, dg_ref, wg_v, wu_v, sems, held):
        i = pl.program_id(0)
        _load_weights([(wg_hbm, wg_v), (wu_hbm, wu_v)], sems)

        @pl.when(i < nt)
        def _():
            dn = jnp.zeros((tm, D), F32)
            for lo, hi in _ffn_chunks(F):
                dn = dn + _mm(da_ref[:, lo:hi], wg_v[lo:hi, :]) + _mm(db_ref[:, lo:hi], wu_v[lo:hi, :])
            dx, dg = _rms_bwd(_tokens_tile(x_ref, xb_ref, meta_ref, tm, i == 0), gpre_ref[...], dn)
            _acc(dg_ref, dg, i == 0)
            dh_in = dh_ref[...] + dx

            @pl.when(i == 0)
            def _():
                dmeta_ref[...] = dh_in[0:N_META, :]

            @pl.when(i > 0)
            def _():
                dx_ref[...] = jnp.concatenate([held[...], dh_in[0:N_META, :]], axis=0)

            held[...] = dh_in[N_META:, :]

        @pl.when(i == nt)
        def _():
            dx_ref[0 : tm - N_META, :] = held[...]

    rows = lambda cols: pl.BlockSpec((tm, cols), lambda i: (tile(i), 0))
    return _pallas(
        body,
        name=name,
        grid=(nt + 1,),
        out_shape=[jax.ShapeDtypeStruct((T - N_META, D), F32), jax.ShapeDtypeStruct((N_META, D), F32), jax.ShapeDtypeStruct((1, D), F32)],
        in_specs=[rows(F), rows(F), rows(D), pl.BlockSpec((N_META, D), lambda i: (jnp.maximum(tile(i) * per - 1, 0), 0)), _full_spec((N_META, D)), rows(D), _full_spec((1, D)), ANY, ANY],
        out_specs=[pl.BlockSpec((tm, D), lambda i: (jnp.maximum(i - 1, 0), 0)), _full_spec((N_META, D)), _full_spec((1, D))],
        scratch_shapes=[pltpu.VMEM(wg.shape, wg.dtype), pltpu.VMEM(wu.shape, wu.dtype), pltpu.SemaphoreType.DMA((2,)), pltpu.VMEM((tm - N_META, D), F32)],
        operands=(da, db, x, x, meta, dh, g_pre, wg, wu),
        exchange=exchange,
    )


def _ffn_bwd(dh, f, ga, si, h, g_post, g_pre, wg, wu, wd, name):
    T, D = dh.shape
    F = wd.shape[0]
    tm = FFN_TILE

    def body(dh_ref, f_ref, ga_ref, si_ref, h_ref, gpost_ref, gpre_ref, wg_hbm, wu_hbm, wd_hbm, da_ref, db_ref, df_ref, dhin_ref, dgpost_ref, dgpre_ref, wg_v, wu_v, wd_v, sems):
        first = pl.program_id(0) == 0
        _load_weights([(wg_hbm, wg_v), (wu_hbm, wu_v), (wd_hbm, wd_v)], sems)
        dh = dh_ref[...]
        df, dg = _rms_bwd(f_ref[...], gpost_ref[...], 0.5 * dh)
        _acc(dgpost_ref, dg, first)
        dfb = df.astype(MXU_DTYPE)
        df_ref[...] = dfb.astype(df_ref.dtype)
        dn = jnp.zeros((tm, D), F32)
        for lo, hi in _ffn_chunks(F):
            ds = _mm_nt(dfb, wd_v[lo:hi, :])
            da = (ds * ga_ref[:, lo:hi].astype(F32)).astype(MXU_DTYPE)
            db = (ds * si_ref[:, lo:hi].astype(F32)).astype(MXU_DTYPE)
            da_ref[:, lo:hi] = da.astype(da_ref.dtype)
            db_ref[:, lo:hi] = db.astype(db_ref.dtype)
            dn = dn + _mm(da, wg_v[lo:hi, :]) + _mm(db, wu_v[lo:hi, :])
        dx, dg = _rms_bwd(h_ref[...], gpre_ref[...], dn)
        _acc(dgpre_ref, dg, first)
        dhin_ref[...] = dh + dx

    act = jax.ShapeDtypeStruct((T, F), MXU_DTYPE)
    vec = jax.ShapeDtypeStruct((1, D), F32)
    outs, _ = _pallas(
        body,
        name=name,
        grid=(T // tm,),
        out_shape=[act, act, jax.ShapeDtypeStruct((T, D), MXU_DTYPE), jax.ShapeDtypeStruct((T, D), F32), vec, vec],
        in_specs=[_row_spec(tm, D), _row_spec(tm, D), _row_spec(tm, F), _row_spec(tm, F), _row_spec(tm, D), _full_spec((1, D)), _full_spec((1, D)), ANY, ANY, ANY],
        out_specs=[_row_spec(tm, F), _row_spec(tm, F), _row_spec(tm, D), _row_spec(tm, D), _full_spec((1, D)), _full_spec((1, D))],
        scratch_shapes=[pltpu.VMEM(wg.shape, wg.dtype), pltpu.VMEM(wu.shape, wu.dtype), pltpu.VMEM(wd.shape, wd.dtype), pltpu.SemaphoreType.DMA((3,))],
        operands=(dh, f, ga, si, h, g_post, g_pre, wg, wu, wd),
    )
    return outs


def _token_tile(T):
    for t in (912, 864, 432):
        if T % t == 0:
            return t
    raise ValueError(f"no token tile for {T} rows")


def _tn_matmul(xm, ym, name, exchange=None):
    T, M = xm.shape
    N = ym.shape[1]
    if (T - N_META) % TN_TILE:
        tk = _token_tile(T)

        def body(x_ref, y_ref, o_ref):
            _acc(o_ref, _mm_tn(x_ref[...], y_ref[...]), pl.program_id(0) == 0)

        grid, operands = (T // tk,), (xm, ym)
        in_specs = [pl.BlockSpec((tk, M), lambda k: (k, 0)), pl.BlockSpec((tk, N), lambda k: (k, 0))]
    else:
        tk = TN_TILE

        def body(x_ref, y_ref, xh_ref, yh_ref, o_ref):
            prod = _mm_tn(x_ref[...], y_ref[...])

            @pl.when(pl.program_id(0) == 0)
            def _():
                o_ref[...] = prod + _mm_tn(xh_ref[...], yh_ref[...])

            @pl.when(pl.program_id(0) > 0)
            def _():
                o_ref[...] += prod

        grid, operands = ((T - N_META) // tk,), (xm, ym, xm, ym)
        start = lambda k: (pl.multiple_of(N_META + k * tk, N_META), 0)
        in_specs = [pl.BlockSpec((pl.Element(tk), pl.Element(M)), start), pl.BlockSpec((pl.Element(tk), pl.Element(N)), start),
                    pl.BlockSpec((N_META, M), lambda k: (0, 0)), pl.BlockSpec((N_META, N), lambda k: (0, 0))]

    (out,), x_outs = _pallas(
        body,
        name=name,
        grid=grid,
        out_shape=[jax.ShapeDtypeStruct((M, N), F32)],
        in_specs=in_specs,
        out_specs=[_full_spec((M, N))],
        operands=operands,
        exchange=exchange,
    )
    return out, x_outs


TAB_A, TAB_AS1, TAB_AS2, TAB_AS4, TAB_JF, TAB_JB = 0, 2, 4, 6, 8, 10


def _scan_inplace(zr, zi, tabs, pows, car_r, car_i, seg, reverse, base=0):
    n_slabs = zr.shape[0]
    sgn = -1.0 if reverse else 1.0
    row = lax.broadcasted_iota(jnp.int32, (SUBLANES, LANES), 0)

    def cmul(pr, pi, xr, xi):
        return pr * xr - pi * xi, pr * xi + pi * xr

    for k0 in range(0, n_slabs, SLAB_GROUP):
        slabs = range(k0, min(k0 + SLAB_GROUP, n_slabs))
        ar = [tabs[TAB_A, k] for k in slabs]
        ai = [sgn * tabs[TAB_A + 1, k] for k in slabs]

        def first_pass(t, carry):
            r = (seg - 1 - t) if reverse else t
            out = []
            for q, k in enumerate(slabs):
                xr, xi = carry[2 * q], carry[2 * q + 1]
                pr, pi = cmul(ar[q], ai[q], xr, xi)
                nr = pr + zr[k, pl.ds(base + r, SUBLANES, stride=seg), :]
                ni = pi + zi[k, pl.ds(base + r, SUBLANES, stride=seg), :]
                zr[k, pl.ds(base + r, SUBLANES, stride=seg), :] = nr
                zi[k, pl.ds(base + r, SUBLANES, stride=seg), :] = ni
                out += [nr, ni]
            return tuple(out)

        ends = lax.fori_loop(0, seg, first_pass, tuple(jnp.zeros((SUBLANES, LANES), F32) for _ in range(2 * len(slabs))))

        incoming = []
        for q, k in enumerate(slabs):
            fr, fi = ends[2 * q], ends[2 * q + 1]
            for d, tab in ((1, TAB_AS1), (2, TAB_AS2), (4, TAB_AS4)):
                shift, keep = (SUBLANES - d, row < SUBLANES - d) if reverse else (d, row >= d)
                sr = jnp.where(keep, pltpu.roll(fr, shift, 0), 0.0)
                si = jnp.where(keep, pltpu.roll(fi, shift, 0), 0.0)
                pr, pi = cmul(tabs[tab, k], sgn * tabs[tab + 1, k], sr, si)
                fr, fi = fr + pr, fi + pi
            cr, ci = car_r[k], car_i[k]
            jtab = TAB_JB if reverse else TAB_JF
            pr, pi = cmul(tabs[jtab, k], sgn * tabs[jtab + 1, k], cr, ci)
            er, ei = fr + pr, fi + pi
            if reverse:
                inr = jnp.where(row < SUBLANES - 1, pltpu.roll(er, SUBLANES - 1, 0), cr)
                ini = jnp.where(row < SUBLANES - 1, pltpu.roll(ei, SUBLANES - 1, 0), ci)
                car_r[k] = jnp.broadcast_to(er[0:1, :], (SUBLANES, LANES))
                car_i[k] = jnp.broadcast_to(ei[0:1, :], (SUBLANES, LANES))
            else:
                inr = jnp.where(row >= 1, pltpu.roll(er, 1, 0), cr)
                ini = jnp.where(row >= 1, pltpu.roll(ei, 1, 0), ci)
                car_r[k] = jnp.broadcast_to(er[SUBLANES - 1 : SUBLANES, :], (SUBLANES, LANES))
                car_i[k] = jnp.broadcast_to(ei[SUBLANES - 1 : SUBLANES, :], (SUBLANES, LANES))
            incoming += [inr, ini]

        def second_pass(r, _):
            p = (seg - 1 - r) if reverse else r
            for q, k in enumerate(slabs):
                pr, pi = cmul(pows[0, k, p], sgn * pows[1, k, p], incoming[2 * q], incoming[2 * q + 1])
                zr[k, pl.ds(base + r, SUBLANES, stride=seg), :] = zr[k, pl.ds(base + r, SUBLANES, stride=seg), :] + pr
                zi[k, pl.ds(base + r, SUBLANES, stride=seg), :] = zi[k, pl.ds(base + r, SUBLANES, stride=seg), :] + pi
            return 0

        lax.fori_loop(0, seg, second_pass, 0)


def _slabs_to_cols(ref, k0, n):
    return jnp.concatenate([ref[k0 + q] for q in range(n)], axis=1)


def _window_sum(ext, doublings, forward):
    rows = ext.shape[0]
    s = ext
    for k in range(doublings):
        s = s + pltpu.roll(s, (1 << k) if forward else rows - (1 << k), 0)
    return s


def _mix_fwd(h1, g_pre, g_so, g_po, g_post, dskip, pscale, win, wout, bbre, bbim, ccre, ccim, wgv, wgg, pw, tabs, pows, name, exchange=None):
    T, D = h1.shape
    W = D // 2
    tm = MIX_SUBTILES * MIX_TILE
    seg = MIX_TILE // SUBLANES
    n_slabs = tabs.shape[1]
    nch, cch, sch = bbre.shape
    spc = sch // LANES
    pg = W // len(POOL_WINDOWS)

    def body(h_ref, gpre_ref, gso_ref, gpo_ref, gpost_ref, dskip_ref, pscale_ref, win_ref, wout_ref, bbre_ref, bbim_ref, ccre_ref, ccim_ref, wgv_ref, wgg_ref, pw_ref, tabs_ref, pows_ref,
             proj_ref, xr_ref, xi_ref, y_ref, pooled_ref, mixed_ref, h2_ref, n2_ref, cat_ref, car_r, car_i, halo):
        i = pl.program_id(0)

        @pl.when(i == 0)
        def _():
            car_r[...] = jnp.zeros_like(car_r)
            car_i[...] = jnp.zeros_like(car_i)
            halo[...] = jnp.zeros_like(halo)

        hh = h_ref[...]
        n2 = (hh * _rms_stat(hh) * gpre_ref[...]).astype(MXU_DTYPE)
        n2_ref[...] = n2.astype(n2_ref.dtype)
        proj = _mm(n2, win_ref[...])
        proj_ref[...] = proj
        us, up = proj[:, :W], proj[:, W:]

        for c in range(nch):
            uc = us[:, c * cch : (c + 1) * cch].astype(MXU_DTYPE)
            bur, bui = _mm(uc, bbre_ref[c]), _mm(uc, bbim_ref[c])
            for q in range(spc):
                xr_ref[c * spc + q] = bur[:, q * LANES : (q + 1) * LANES]
                xi_ref[c * spc + q] = bui[:, q * LANES : (q + 1) * LANES]
        for sub in range(MIX_SUBTILES):
            _scan_inplace(xr_ref, xi_ref, tabs_ref, pows_ref, car_r, car_i, seg, reverse=False, base=sub * MIX_TILE)
        ys = []
        for c in range(nch):
            ys.append(_mm(_slabs_to_cols(xr_ref, c * spc, spc), ccre_ref[c]) - _mm(_slabs_to_cols(xi_ref, c * spc, spc), ccim_ref[c]))
        y = jnp.concatenate(ys, axis=1) + dskip_ref[...] * us
        y_ref[...] = y
        ge = _gelu(y).astype(MXU_DTYPE)
        zv = jnp.concatenate([_mm(ge[:, c * cch : (c + 1) * cch], wgv_ref[c]) for c in range(nch)], axis=1)
        zg = jnp.concatenate([_mm(ge[:, c * cch : (c + 1) * cch], wgg_ref[c]) for c in range(nch)], axis=1)
        out = zv * _sigmoid(zg)
        cat_s = out * _rms_stat(out) * gso_ref[...]

        ext = jnp.concatenate([halo[...], up], axis=0)
        halo[...] = up[tm - POOL_HALO :, :]
        t1 = (i * tm + 1 + lax.broadcasted_iota(jnp.int32, (tm, pg), 0)).astype(F32)
        pooled, pms = [], []
        for g, w in enumerate(POOL_WINDOWS):
            col = ext[:, g * pg : (g + 1) * pg]
            win_sum = _window_sum(col, g + 1, True)[POOL_HALO:, :]
            pooled_g = win_sum / jnp.minimum(t1, float(w)) - up[:, g * pg : (g + 1) * pg]
            pooled.append(pooled_g)
            pms.append(_mm(pooled_g, pw_ref[g]))
        pooled_ref[...] = jnp.concatenate(pooled, axis=1)
        yp = jnp.concatenate(pms, axis=1) * pscale_ref[...]
        cat_p = yp * _rms_stat(yp) * gpo_ref[...]

        cat = jnp.concatenate([cat_s, cat_p], axis=1).astype(MXU_DTYPE)
        cat_ref[...] = cat.astype(cat_ref.dtype)
        mixed = _mm(cat, wout_ref[...])
        mixed_ref[...] = mixed
        h2_ref[...] = hh + mixed * _rms_stat(mixed) * gpost_ref[...]

    tok = lambda cols, dt=F32: jax.ShapeDtypeStruct((T, cols), dt)
    slab_spec = pl.BlockSpec((n_slabs, tm, LANES), lambda i: (0, i, 0))
    operands = (h1, g_pre, g_so, g_po, g_post, dskip, pscale, win, wout, bbre, bbim, ccre, ccim, wgv, wgg, pw, tabs, pows)
    return _pallas(
        body,
        name=name,
        grid=(T // tm,),
        out_shape=[tok(D), jax.ShapeDtypeStruct((n_slabs, T, LANES), F32), jax.ShapeDtypeStruct((n_slabs, T, LANES), F32), tok(W), tok(W), tok(D), tok(D), tok(D, MXU_DTYPE), tok(D, MXU_DTYPE)],
        in_specs=[_row_spec(tm, D)] + [_full_spec(o.shape, single=True) for o in operands[1:]],
        out_specs=[_row_spec(tm, D), slab_spec, slab_spec, _row_spec(tm, W), _row_spec(tm, W), _row_spec(tm, D), _row_spec(tm, D), _row_spec(tm, D), _row_spec(tm, D)],
        scratch_shapes=[pltpu.VMEM((n_slabs, SUBLANES, LANES), F32), pltpu.VMEM((n_slabs, SUBLANES, LANES), F32), pltpu.VMEM((POOL_HALO, W), F32)],
        operands=operands,
        exchange=exchange,
    )


def _mix_bwd_heads(dh2, mixed, y, pooled, proj, g_so, g_po, g_post, pscale, wout, wgv, wgg, pw, name, exchange=None):
    T, D = dh2.shape
    W = D // 2
    tm = _token_tile(T)
    nch, cch, _ = wgv.shape
    ng, pg, _ = pw.shape

    def body(dh2_ref, mixed_ref, y_ref, pooled_ref, us_ref, gso_ref, gpo_ref, gpost_ref, pscale_ref, wout_ref, wgv_ref, wgg_ref, pw_ref,
             dy_ref, dpooled_ref, dmixed_ref, dgpost_ref, dgso_ref, dgpo_ref, dd_ref, dscale_ref, dwgv_ref, dwgg_ref, dpw_ref):
        first = pl.program_id(0) == 0
        dmixed, dgpost = _rms_bwd(mixed_ref[...], gpost_ref[...], dh2_ref[...])
        _acc(dgpost_ref, dgpost, first)
        dmb = dmixed.astype(MXU_DTYPE)
        dmixed_ref[...] = dmb.astype(dmixed_ref.dtype)
        dcat = _mm_nt(dmb, wout_ref[...])
        dcs, dcp = dcat[:, :W], dcat[:, W:]

        y = y_ref[...]
        ge = _gelu(y).astype(MXU_DTYPE)
        zv = jnp.concatenate([_mm(ge[:, c * cch : (c + 1) * cch], wgv_ref[c]) for c in range(nch)], axis=1)
        zg = jnp.concatenate([_mm(ge[:, c * cch : (c + 1) * cch], wgg_ref[c]) for c in range(nch)], axis=1)
        sg = _sigmoid(zg)
        dout, dgso = _rms_bwd(zv * sg, gso_ref[...], dcs)
        _acc(dgso_ref, dgso, first)
        dzv = (dout * sg).astype(MXU_DTYPE)
        dzg = (dout * zv * sg * (1.0 - sg)).astype(MXU_DTYPE)
        dges = []
        for c in range(nch):
            cs = slice(c * cch, (c + 1) * cch)
            dges.append(_mm_nt(dzv[:, cs], wgv_ref[c]) + _mm_nt(dzg[:, cs], wgg_ref[c]))
            _acc(dwgv_ref.at[c], _mm_tn(ge[:, cs], dzv[:, cs]), first)
            _acc(dwgg_ref.at[c], _mm_tn(ge[:, cs], dzg[:, cs]), first)
        dy = jnp.concatenate(dges, axis=1) * _gelu_grad(y)
        dy_ref[...] = dy
        _acc(dd_ref, jnp.sum(dy * us_ref[...], axis=0, keepdims=True), first)

        pooled_b = pooled_ref[...].astype(MXU_DTYPE)
        pm = jnp.concatenate([_mm(pooled_b[:, g * pg : (g + 1) * pg], pw_ref[g]) for g in range(ng)], axis=1)
        dyp, dgpo = _rms_bwd(pm * pscale_ref[...], gpo_ref[...], dcp)
        _acc(dgpo_ref, dgpo, first)
        _acc(dscale_ref, jnp.sum(dyp * pm, axis=0, keepdims=True), first)
        dpm = (dyp * pscale_ref[...]).astype(MXU_DTYPE)
        dps = []
        for g in range(ng):
            gs = slice(g * pg, (g + 1) * pg)
            dps.append(_mm_nt(dpm[:, gs], pw_ref[g]))
            _acc(dpw_ref.at[g], _mm_tn(pooled_b[:, gs], dpm[:, gs]), first)
        dpooled_ref[...] = jnp.concatenate(dps, axis=1)

    vec = lambda n: jax.ShapeDtypeStruct((1, n), F32)
    operands = (dh2, mixed, y, pooled, proj, g_so, g_po, g_post, pscale, wout, wgv, wgg, pw)
    return _pallas(
        body,
        name=name,
        grid=(T // tm,),
        out_shape=[jax.ShapeDtypeStruct((T, W), F32), jax.ShapeDtypeStruct((T, W), F32), jax.ShapeDtypeStruct((T, D), MXU_DTYPE), vec(D), vec(W), vec(W), vec(W), vec(W),
                   jax.ShapeDtypeStruct(wgv.shape, F32), jax.ShapeDtypeStruct(wgg.shape, F32), jax.ShapeDtypeStruct(pw.shape, F32)],
        in_specs=[_row_spec(tm, D), _row_spec(tm, D), _row_spec(tm, W), _row_spec(tm, W), _row_spec(tm, W)] + [_full_spec(o.shape) for o in operands[5:]],
        out_specs=[_row_spec(tm, W), _row_spec(tm, W), _row_spec(tm, D), _full_spec((1, D)), _full_spec((1, W)), _full_spec((1, W)), _full_spec((1, W)), _full_spec((1, W)),
                   _full_spec(wgv.shape), _full_spec(wgg.shape), _full_spec(pw.shape)],
        operands=operands,
        exchange=exchange,
    )


def _mix_bwd_scan(dy, dpooled, xr, xi, proj, dskip, bbre, bbim, ccre, ccim, tabs, pows, name, exchange=None):
    T, W = dy.shape
    D = 2 * W
    tm = MIX_SUBTILES * MIX_TILE
    seg = MIX_TILE // SUBLANES
    nt = T // tm
    n_slabs = tabs.shape[1]
    nch, cch, sch = bbre.shape
    spc = sch // LANES
    pg = W // len(POOL_WINDOWS)
    blocks_per_tile = tm // SUBLANES

    def body(dy_ref, dp_ref, xr_ref, xi_ref, xpr_ref, xpi_ref, us_ref, dskip_ref, bbre_ref, bbim_ref, ccre_ref, ccim_ref, tabs_ref, pows_ref,
             dproj_ref, dccre_ref, dccim_ref, dbbre_ref, dbbim_ref, dar_ref, dai_ref, lr, li, car_r, car_i, halo):
        i = pl.program_id(0)
        first = i == 0
        tile = nt - 1 - i
        row = lax.broadcasted_iota(jnp.int32, (SUBLANES, LANES), 0)

        @pl.when(first)
        def _():
            car_r[...] = jnp.zeros_like(car_r)
            car_i[...] = jnp.zeros_like(car_i)
            halo[...] = jnp.zeros_like(halo)
            dar_ref[...] = jnp.zeros_like(dar_ref)
            dai_ref[...] = jnp.zeros_like(dai_ref)

        dy = dy_ref[...]
        for c in range(nch):
            dyc = dy[:, c * cch : (c + 1) * cch]
            gr, gi = _mm_nt(dyc, ccre_ref[c]), _mm_nt(dyc, ccim_ref[c])
            for q in range(spc):
                lr[c * spc + q] = gr[:, q * LANES : (q + 1) * LANES]
                li[c * spc + q] = -gi[:, q * LANES : (q + 1) * LANES]
            _acc(dccre_ref.at[c], _mm_tn(_slabs_to_cols(xr_ref, c * spc, spc), dyc), first)
            _acc(dccim_ref.at[c], -_mm_tn(_slabs_to_cols(xi_ref, c * spc, spc), dyc), first)
        for sub in reversed(range(MIX_SUBTILES)):
            _scan_inplace(lr, li, tabs_ref, pows_ref, car_r, car_i, seg, reverse=True, base=sub * MIX_TILE)

        for sub in range(MIX_SUBTILES):
            base = sub * MIX_TILE
            for k0 in range(0, n_slabs, SLAB_GROUP):
                slabs = range(k0, min(k0 + SLAB_GROUP, n_slabs))
                init = []
                for k in slabs:
                    if sub == 0:
                        prev_r = jnp.where(tile > 0, jnp.broadcast_to(xpr_ref[k, SUBLANES - 1 : SUBLANES, :], (SUBLANES, LANES)), 0.0)
                        prev_i = jnp.where(tile > 0, jnp.broadcast_to(xpi_ref[k, SUBLANES - 1 : SUBLANES, :], (SUBLANES, LANES)), 0.0)
                    else:
                        prev_r = jnp.broadcast_to(xr_ref[k, base - 1 : base, :], (SUBLANES, LANES))
                        prev_i = jnp.broadcast_to(xi_ref[k, base - 1 : base, :], (SUBLANES, LANES))
                    x0r = jnp.where(row >= 1, pltpu.roll(xr_ref[k, pl.ds(base + seg - 1, SUBLANES, stride=seg), :], 1, 0), prev_r)
                    x0i = jnp.where(row >= 1, pltpu.roll(xi_ref[k, pl.ds(base + seg - 1, SUBLANES, stride=seg), :], 1, 0), prev_i)
                    l0r, l0i = lr[k, pl.ds(base, SUBLANES, stride=seg), :], li[k, pl.ds(base, SUBLANES, stride=seg), :]
                    init += [l0r * x0r + l0i * x0i, l0i * x0r - l0r * x0i]

                def step(r, acc, slabs=slabs, base=base):
                    out = []
                    for q, k in enumerate(slabs):
                        pr_, pi_ = xr_ref[k, pl.ds(base + r - 1, SUBLANES, stride=seg), :], xi_ref[k, pl.ds(base + r - 1, SUBLANES, stride=seg), :]
                        lr_, li_ = lr[k, pl.ds(base + r, SUBLANES, stride=seg), :], li[k, pl.ds(base + r, SUBLANES, stride=seg), :]
                        out += [acc[2 * q] + lr_ * pr_ + li_ * pi_, acc[2 * q + 1] + li_ * pr_ - lr_ * pi_]
                    return tuple(out)

                sums = lax.fori_loop(1, seg, step, tuple(init))
                for q, k in enumerate(slabs):
                    dar_ref[k] += sums[2 * q]
                    dai_ref[k] += sums[2 * q + 1]

        us = us_ref[...]
        dus = []
        for c in range(nch):
            lrc, lic = _slabs_to_cols(lr, c * spc, spc).astype(MXU_DTYPE), _slabs_to_cols(li, c * spc, spc).astype(MXU_DTYPE)
            uc = us[:, c * cch : (c + 1) * cch]
            _acc(dbbre_ref.at[c], _mm_tn(uc, lrc), first)
            _acc(dbbim_ref.at[c], _mm_tn(uc, lic), first)
            dus.append(_mm_nt(lrc, bbre_ref[c]) + _mm_nt(lic, bbim_ref[c]))
        du_s = jnp.concatenate(dus, axis=1) + dskip_ref[...] * dy

        dp = dp_ref[...]
        t1 = (tile * tm + 1 + lax.broadcasted_iota(jnp.int32, (tm, pg), 0)).astype(F32)
        dups, heads = [], []
        for g, w in enumerate(POOL_WINDOWS):
            dpg = dp[:, g * pg : (g + 1) * pg]
            qg = dpg / jnp.minimum(t1, float(w))
            ext = jnp.concatenate([qg, halo[:, g * pg : (g + 1) * pg]], axis=0)
            dups.append(_window_sum(ext, g + 1, False)[:tm, :] - dpg)
            heads.append(qg[:POOL_HALO, :])
        halo[...] = jnp.concatenate(heads, axis=1)
        dproj_ref[...] = jnp.concatenate([du_s] + dups, axis=1).astype(dproj_ref.dtype)

    rev = lambda cols: _row_spec(tm, cols, rev_n=nt)
    slab_spec = pl.BlockSpec((n_slabs, tm, LANES), lambda i: (0, nt - 1 - i, 0))
    prev_spec = pl.BlockSpec((n_slabs, SUBLANES, LANES), lambda i: (0, jnp.maximum((nt - 1 - i) * blocks_per_tile - 1, 0), 0))
    consts = (dskip, bbre, bbim, ccre, ccim, tabs, pows)
    return _pallas(
        body,
        name=name,
        grid=(nt,),
        out_shape=[jax.ShapeDtypeStruct((T, D), MXU_DTYPE), jax.ShapeDtypeStruct(ccre.shape, F32), jax.ShapeDtypeStruct(ccim.shape, F32), jax.ShapeDtypeStruct(bbre.shape, F32),
                   jax.ShapeDtypeStruct(bbim.shape, F32), jax.ShapeDtypeStruct((n_slabs, SUBLANES, LANES), F32), jax.ShapeDtypeStruct((n_slabs, SUBLANES, LANES), F32)],
        in_specs=[rev(W), rev(W), slab_spec, slab_spec, prev_spec, prev_spec, rev(W)] + [_full_spec(o.shape, single=True) for o in consts],
        out_specs=[rev(D), _full_spec(ccre.shape), _full_spec(ccim.shape), _full_spec(bbre.shape), _full_spec(bbim.shape),
                   _full_spec((n_slabs, SUBLANES, LANES)), _full_spec((n_slabs, SUBLANES, LANES))],
        scratch_shapes=[pltpu.VMEM((n_slabs, tm, LANES), F32), pltpu.VMEM((n_slabs, tm, LANES), F32), pltpu.VMEM((n_slabs, SUBLANES, LANES), F32), pltpu.VMEM((n_slabs, SUBLANES, LANES), F32),
                        pltpu.VMEM((POOL_HALO, W), F32)],
        operands=(dy, dpooled, xr, xi, xr, xi, proj, *consts),
        exchange=exchange,
    )


def _mix_bwd_in(dproj, h1, dh2, g_pre, win, name, exchange=None):
    T, D = h1.shape
    tm = _token_tile(T)

    def body(dproj_ref, h_ref, dh2_ref, gpre_ref, win_ref, dh1_ref, dg_ref):
        dx, dg = _rms_bwd(h_ref[...], gpre_ref[...], _mm_nt(dproj_ref[...], win_ref[...]))
        _acc(dg_ref, dg, pl.program_id(0) == 0)
        dh1_ref[...] = dh2_ref[...] + dx

    return _pallas(
        body,
        name=name,
        grid=(T // tm,),
        out_shape=[jax.ShapeDtypeStruct((T, D), F32), jax.ShapeDtypeStruct((1, D), F32)],
        in_specs=[_row_spec(tm, D), _row_spec(tm, D), _row_spec(tm, D), _full_spec((1, D)), _full_spec(win.shape)],
        out_specs=[_row_spec(tm, D), _full_spec((1, D))],
        operands=(dproj, h1, dh2, g_pre, win),
        exchange=exchange,
    )


def _discretize(lam_re, lam_im, log_dt, b_re, b_im):
    dt = jnp.exp(log_dt)[:, None]
    decay = jnp.exp(lam_re * dt)
    ang = lam_im * dt
    a_re, a_im = decay * jnp.cos(ang), decay * jnp.sin(ang)
    nr = a_re - 1.0
    den = lam_re * lam_re + lam_im * lam_im
    q_re = (nr * lam_re + a_im * lam_im) / den
    q_im = (a_im * lam_re - nr * lam_im) / den
    bb_re = q_re[..., None] * b_re - q_im[..., None] * b_im
    bb_im = q_re[..., None] * b_im + q_im[..., None] * b_re
    return a_re, a_im, bb_re, bb_im


GROUPS_PER_CHUNK = 16


def _block_diag(w, rows_first):
    G = w.shape[0]
    nch = G // GROUPS_PER_CHUNK
    if not rows_first:
        w = jnp.swapaxes(w, 1, 2)
    p, q = w.shape[1], w.shape[2]
    eye = jnp.eye(GROUPS_PER_CHUNK, dtype=w.dtype)
    out = jnp.einsum("cgpq,gk->cgpkq", w.reshape(nch, GROUPS_PER_CHUNK, p, q), eye)
    return out.reshape(nch, GROUPS_PER_CHUNK * p, GROUPS_PER_CHUNK * q)


def _block_diag_extract(m, p, q, rows_first):
    nch = m.shape[0]
    eye = jnp.eye(GROUPS_PER_CHUNK, dtype=m.dtype)
    out = jnp.einsum("cgpkq,gk->cgpq", m.reshape(nch, GROUPS_PER_CHUNK, p, GROUPS_PER_CHUNK, q), eye).reshape(nch * GROUPS_PER_CHUNK, p, q)
    return out if rows_first else jnp.swapaxes(out, 1, 2)


def _cmul(ar, ai, br, bi):
    return ar * br - ai * bi, ar * bi + ai * br


def _powers(ar, ai, count):
    pr, pi = ar[None], ai[None]
    while pr.shape[0] < count:
        nr, ni = _cmul(pr, pi, pr[-1][None], pi[-1][None])
        pr, pi = jnp.concatenate([pr, nr]), jnp.concatenate([pi, ni])
    return pr[:count], pi[:count]


def _scan_tables(a_re, a_im, seg):
    n = a_re.size
    ns = n // LANES
    ar, ai = a_re.reshape(n), a_im.reshape(n)
    pr, pi = _powers(ar, ai, seg)
    jr, ji = _powers(pr[-1], pi[-1], SUBLANES)

    def bcast(v):
        return jnp.broadcast_to(v.reshape(ns, 1, LANES), (ns, SUBLANES, LANES))

    def per_sublane(vs):
        return jnp.transpose(vs.reshape(SUBLANES, ns, LANES), (1, 0, 2))

    tabs = jnp.stack([bcast(ar), bcast(ai), bcast(jr[0]), bcast(ji[0]), bcast(jr[1]), bcast(ji[1]), bcast(jr[3]), bcast(ji[3]),
                      per_sublane(jr), per_sublane(ji), per_sublane(jr[::-1]), per_sublane(ji[::-1])])

    def rows(vs):
        return jnp.broadcast_to(jnp.transpose(vs.reshape(seg, ns, 1, LANES), (1, 0, 2, 3)), (ns, seg, SUBLANES, LANES))

    return tabs, jnp.stack([rows(pr), rows(pi)])


SMALL = ("ffn1_pre_norm", "ffn1_post_norm", "mix_pre_norm", "mix_post_norm", "ssm_lambda_re", "ssm_lambda_im", "ssm_log_dt", "ssm_b_re", "ssm_b_im", "ssm_c_re", "ssm_c_im",
         "ssm_d", "ssm_w_glu", "pool_w", "pool_scale", "ssm_out_norm", "pool_out_norm", "ffn2_pre_norm", "ffn2_post_norm")
BIG = ("ffn1_w_gate", "ffn1_w_up", "ffn1_w_down", "w_in", "w_out", "ffn2_w_gate", "ffn2_w_up", "ffn2_w_down")
ORDER = ("meta_tokens", "ffn1_pre_norm", "ffn1_post_norm", "ffn1_w_gate", "ffn1_w_up", "ffn1_w_down", "mix_pre_norm", "mix_post_norm", "w_in", "ssm_lambda_re", "ssm_lambda_im",
         "ssm_log_dt", "ssm_b_re", "ssm_b_im", "ssm_c_re", "ssm_c_im", "ssm_d", "ssm_w_glu", "pool_w", "pool_scale", "ssm_out_norm", "pool_out_norm", "w_out", "ffn2_pre_norm",
         "ffn2_post_norm", "ffn2_w_gate", "ffn2_w_up", "ffn2_w_down")
PACK_ROWS = SUBLANES * 8
def _pack(arrays, rows):
    flat = jnp.concatenate([a.reshape(-1) for a in arrays])
    return jnp.pad(flat, (0, rows * LANES - flat.size)).reshape(rows, LANES)


def _unpack(packed, shapes):
    flat = packed.reshape(-1)
    out, off = [], 0
    for s in shapes:
        n = math.prod(s)
        out.append(flat[off : off + n].reshape(s))
        off += n
    return out


def _step(p, x, loss_target, m, v):
    D = x.shape[-1]
    chip = (2 * lax.axis_index("x") + lax.axis_index("y")).astype(jnp.int32)
    place = jnp.stack([chip, lax.axis_index("c").astype(jnp.int32)])

    def gather_buffer(w):
        own = w.reshape(1, 2, w.shape[0] // 2, w.shape[1])
        return lax.dynamic_update_slice(lax.empty((4,) + own.shape[1:], own.dtype), own, (chip, 0, 0, 0))

    def rows_of(n, a):
        return jnp.swapaxes(a[0], 0, 1) if n.endswith(("w_gate", "w_up")) else a[0]

    def rows_back(n, a):
        return (jnp.swapaxes(a, 0, 1) if n.endswith(("w_gate", "w_up")) else a)[None]

    def grad_view(g):
        return g.reshape(4, 2, g.shape[0] // 8, g.shape[1])

    def reduce_sum(got_sibling, views, tag):
        sums = [_add_own_half(v_, g_, place, f"{tag}_add_sibling_{k}") for k, (v_, g_) in enumerate(zip(views, got_sibling))]
        return [s[0] for s in sums], [s[1] for s in sums]

    def reduce_halves(parts, got_chips, tag):
        return [_add_chips(p_, g_, place, f"{tag}_add_chips_{k}") for k, (p_, g_) in enumerate(zip(parts, got_chips))]

    bufs = {n: gather_buffer(rows_of(n, p[n]).astype(MXU_DTYPE)) for n in BIG}
    full = {}

    def gathered(names, got):
        full.update({n: g_.reshape(-1, g_.shape[-1]) for n, g_ in zip(names, got)})

    def gather_of(names, n_steps):
        return _Gather([bufs[n] for n in names], mid_step=(3 * n_steps) // 4)

    first_names = ("ffn1_w_gate", "ffn1_w_up")
    got = _exchange_call(_Gather([bufs[n] for n in first_names] + [gather_buffer(p["meta_tokens"])]), "gather_first")
    gathered(first_names, got)
    meta = jnp.transpose(got[-1].reshape(4, N_META, -1), (1, 0, 2)).reshape(N_META, D)

    vec = lambda n: p[n].reshape(1, -1)
    G, N, H = p["ssm_b_re"].shape[1:]
    a_re, a_im, bb_re, bb_im = _discretize(p["ssm_lambda_re"][0], p["ssm_lambda_im"][0], p["ssm_log_dt"][0], p["ssm_b_re"][0], p["ssm_b_im"][0])
    tabs, pows = _scan_tables(a_re, a_im, MIX_TILE // SUBLANES)
    bf = lambda a: a.astype(MXU_DTYPE)
    bbre, bbim = bf(_block_diag(bb_re, False)), bf(_block_diag(bb_im, False))
    ccre, ccim = bf(_block_diag(p["ssm_c_re"][0], False)), bf(_block_diag(p["ssm_c_im"][0], False))
    wgv, wgg = bf(_block_diag(p["ssm_w_glu"][0][:, :, :H], True)), bf(_block_diag(p["ssm_w_glu"][0][:, :, H:], True))
    pw = bf(p["pool_w"][0])

    T = x.shape[1] + N_META
    names = ("ffn1_w_down", "w_in", "w_out", "ffn2_w_gate")
    (ga1, si1, s1, n1), got = _ffn_gate_up(
        x[0], meta, vec("ffn1_pre_norm"), full["ffn1_w_gate"], full["ffn1_w_up"], "ffn1_gate_up", exchange=gather_of(names, T // FFN_TILE)
    )
    gathered(names, got)
    (h1, f1), got = _ffn_down(x[0], meta, s1, vec("ffn1_post_norm"), full["ffn1_w_down"], "ffn1_down", exchange=gather_of(("ffn2_w_up",), T // FFN_TILE))
    gathered(("ffn2_w_up",), got)
    (proj, xr, xim, y, pooled, mixed, h2, n2, cat), got = _mix_fwd(
        h1, vec("mix_pre_norm"), vec("ssm_out_norm"), vec("pool_out_norm"), vec("mix_post_norm"), vec("ssm_d"), vec("pool_scale"), full["w_in"], full["w_out"],
        bbre, bbim, ccre, ccim, wgv, wgg, pw, tabs, pows, "mix_fwd", exchange=gather_of(("ffn2_w_down",), T // (MIX_SUBTILES * MIX_TILE)),
    )
    gathered(("ffn2_w_down",), got)
    dh3, sq, f2, ga2, si2, s2, n3 = _ffn_fwd_loss(
        h2, loss_target[0], vec("ffn2_pre_norm"), vec("ffn2_post_norm"), full["ffn2_w_gate"], full["ffn2_w_up"], full["ffn2_w_down"], "ffn2_fwd"
    )
    loss = lax.psum(0.5 * jnp.sum(sq) / D, ("x", "y", "c"))

    g, shared = {}, {}
    ffn_names = lambda tag: (tag + "_w_gate", tag + "_w_up", tag + "_w_down")

    da, db, df, dh2, g["ffn2_post_norm"], g["ffn2_pre_norm"] = _ffn_bwd(
        dh3, f2, ga2, si2, h2, vec("ffn2_post_norm"), vec("ffn2_pre_norm"), full["ffn2_w_gate"], full["ffn2_w_up"], full["ffn2_w_down"], "ffn2_bwd"
    )
    views2 = [
        grad_view(_tn_matmul(da, n3, "ffn2_dw_gate")[0]),
        grad_view(_tn_matmul(db, n3, "ffn2_dw_up")[0]),
        grad_view(_tn_matmul(s2, df, "ffn2_dw_down")[0]),
    ]
    (dy, dpooled, dmixed, g["mix_post_norm"], g["ssm_out_norm"], g["pool_out_norm"], g["ssm_d"], g["pool_scale"], dwgv, dwgg, g["pool_w"]), got = _mix_bwd_heads(
        dh2, mixed, y, pooled, proj, vec("ssm_out_norm"), vec("pool_out_norm"), vec("mix_post_norm"), vec("pool_scale"), full["w_out"], wgv, wgg, pw, "mix_bwd_heads",
        exchange=_SiblingScatter(views2),
    )
    parts2, wire2 = reduce_sum(got, views2, "ffn2")
    (dproj, dccre, dccim, dbbre, dbbim, dar, dai), got = _mix_bwd_scan(
        dy, dpooled, xr, xim, proj, vec("ssm_d"), bbre, bbim, ccre, ccim, tabs, pows, "mix_bwd_scan", exchange=_ChipScatter(wire2)
    )
    halves2 = reduce_halves(parts2, got, "ffn2")
    (dh1, g["mix_pre_norm"]), got = _mix_bwd_in(dproj, h1, dh2, vec("mix_pre_norm"), full["w_in"], "mix_bwd_in", exchange=_SiblingShare(halves2))
    shared.update(zip(ffn_names("ffn2"), got))

    (da, db, df, g["ffn1_post_norm"]), _ = _ffn_bwd_down(dh1, f1, ga1, si1, vec("ffn1_post_norm"), full["ffn1_w_down"], "ffn1_bwd_down")
    (grad_x, d_meta, g["ffn1_pre_norm"]), _ = _ffn_bwd_up(da, db, x[0], meta, dh1, vec("ffn1_pre_norm"), full["ffn1_w_gate"], full["ffn1_w_up"], "ffn1_bwd_up")
    grad_x = grad_x[None]
    dw_down, _ = _tn_matmul(s1, df, "ffn1_dw_down")
    views_d = [grad_view(dw_down)]
    dw_gate, got = _tn_matmul(da, n1, "ffn1_dw_gate", exchange=_SiblingScatter(views_d))
    parts_d, wire_d = reduce_sum(got, views_d, "ffn1_down")
    views_g = [grad_view(dw_gate)]
    ex = _Group([_ChipScatter(wire_d), _SiblingScatter(views_g)])
    dw_up, got = _tn_matmul(db, n1, "ffn1_dw_up", exchange=ex)
    got_d, got_g = ex.split(got)
    halves_d = reduce_halves(parts_d, got_d, "ffn1_down")
    parts_g, wire_g = reduce_sum(got_g, views_g, "ffn1_gate")
    views_u = [grad_view(dw_up)]
    ex = _Group([_SiblingShare(halves_d), _ChipScatter(wire_g), _SiblingScatter(views_u)])
    dw_in, got = _tn_matmul(n2, dproj, "dw_in", exchange=ex)
    got_d, got_g, got_u = ex.split(got)
    shared["ffn1_w_down"] = got_d[0]
    halves_g = reduce_halves(parts_g, got_g, "ffn1_gate")
    parts_u, wire_u = reduce_sum(got_u, views_u, "ffn1_up")
    views_i = [grad_view(dw_in)]
    ex = _Group([_SiblingShare(halves_g), _ChipScatter(wire_u), _SiblingScatter(views_i)])
    dw_out, got = _tn_matmul(cat, dmixed, "dw_out", exchange=ex)
    got_g, got_u, got_i = ex.split(got)
    shared["ffn1_w_gate"] = got_g[0]
    halves_u = reduce_halves(parts_u, got_u, "ffn1_up")
    parts_i, wire_i = reduce_sum(got_i, views_i, "w_in")

    g["ssm_c_re"] = _block_diag_extract(dccre, N, H, False)
    g["ssm_c_im"] = _block_diag_extract(dccim, N, H, False)
    g["ssm_w_glu"] = jnp.concatenate([_block_diag_extract(dwgv, H, H, True), _block_diag_extract(dwgg, H, H, True)], axis=-1)
    d_a_re, d_a_im = jnp.sum(dar, axis=1).reshape(G, N), jnp.sum(dai, axis=1).reshape(G, N)
    _, pull = jax.vjp(_discretize, p["ssm_lambda_re"][0], p["ssm_lambda_im"][0], p["ssm_log_dt"][0], p["ssm_b_re"][0], p["ssm_b_im"][0])
    g["ssm_lambda_re"], g["ssm_lambda_im"], g["ssm_log_dt"], g["ssm_b_re"], g["ssm_b_im"] = pull(
        (d_a_re, d_a_im, _block_diag_extract(dbbre, H, N, False), _block_diag_extract(dbbim, H, N, False))
    )

    small_shapes = [p[n].shape for n in SMALL] + [(N_META, D)]
    small_size = sum(math.prod(s) for s in small_shapes)
    rows = -(-small_size // (LANES * PACK_ROWS)) * PACK_ROWS
    views_t = [grad_view(dw_out), _pack([g[n] for n in SMALL] + [d_meta], rows).reshape(4, 2, rows // 8, LANES)]
    ex = _Group([_SiblingScatter(views_t), _SiblingShare(halves_u)])
    got_t, got_u = ex.split(_exchange_call(ex, "tail_reduce_sibling"))
    shared["ffn1_w_up"] = got_u[0]
    parts_t, wire_t = reduce_sum(got_t, views_t, "tail")
    got = _exchange_call(_ChipScatter(wire_i + wire_t), "tail_reduce_chips")
    got = _exchange_call(_SiblingShare(reduce_halves(parts_i + parts_t, got, "tail")), "tail_reduce_share")
    shared["w_in"], shared["w_out"] = got[0], got[1]
    small_buf = lax.dynamic_update_slice(lax.empty((4,) + got[2].shape, F32), got[2][None], (chip, 0, 0, 0))
    small_all = _exchange_call(_Gather([small_buf]), "gather_small")[0].reshape(rows, LANES)
    grads = dict(zip(SMALL + ("meta_full",), _unpack(small_all, small_shapes)))
    grads["meta_tokens"] = lax.dynamic_slice_in_dim(grads.pop("meta_full"), chip * (D // 4), D // 4, axis=1)
    delta, new_m, new_v = {}, {}, {}
    for n in BIG:
        g_rows = shared[n].reshape(-1, shared[n].shape[-1])
        outs = _adamw(rows_of(n, p[n]), g_rows, rows_of(n, m[n]), rows_of(n, v[n]), "adamw_" + n)
        grads[n], delta[n], new_m[n], new_v[n] = (rows_back(n, a) for a in (g_rows, *outs))
    delta["meta_tokens"], new_m["meta_tokens"], new_v["meta_tokens"] = _adamw(p["meta_tokens"], grads["meta_tokens"], m["meta_tokens"], v["meta_tokens"], "adamw_meta_tokens")

    def as_2d(n, a):
        a = a.reshape(p[n].shape)[0]
        if n in ("ssm_b_re", "ssm_b_im"):
            a = jnp.swapaxes(a, 1, 2)
        return a.reshape(-1, a.shape[-1])

    def from_2d(n, a):
        if n in ("ssm_b_re", "ssm_b_im"):
            g_, n_, h_ = p[n].shape[1:]
            return jnp.swapaxes(a.reshape(g_, h_, n_), 1, 2)[None]
        return a.reshape(p[n].shape)

    outs = _adamw_many(*[[as_2d(n, t[n]) for n in SMALL] for t in (p, grads, m, v)], "adamw_small")
    for out, arrays in zip((delta, new_m, new_v), outs):
        out.update({n: from_2d(n, a) for n, a in zip(SMALL, arrays)})

    return (loss, grad_x, *[grads[n] for n in ORDER], *[delta[n] for n in ORDER], *[new_m[n] for n in ORDER], *[new_v[n] for n in ORDER])


def kernel(x, meta_tokens, ffn1_pre_norm, ffn1_post_norm, ffn1_w_gate, ffn1_w_up, ffn1_w_down, mix_pre_norm, mix_post_norm, w_in, ssm_lambda_re, ssm_lambda_im, ssm_log_dt, ssm_b_re, ssm_b_im, ssm_c_re, ssm_c_im, ssm_d, ssm_w_glu, pool_w, pool_scale, ssm_out_norm, pool_out_norm, w_out, ffn2_pre_norm, ffn2_post_norm, ffn2_w_gate, ffn2_w_up, ffn2_w_down, loss_target, m_meta_tokens, m_ffn1_pre_norm, m_ffn1_post_norm, m_ffn1_w_gate, m_ffn1_w_up, m_ffn1_w_down, m_mix_pre_norm, m_mix_post_norm, m_w_in, m_ssm_lambda_re, m_ssm_lambda_im, m_ssm_log_dt, m_ssm_b_re, m_ssm_b_im, m_ssm_c_re, m_ssm_c_im, m_ssm_d, m_ssm_w_glu, m_pool_w, m_pool_scale, m_ssm_out_norm, m_pool_out_norm, m_w_out, m_ffn2_pre_norm, m_ffn2_post_norm, m_ffn2_w_gate, m_ffn2_w_up, m_ffn2_w_down, v_meta_tokens, v_ffn1_pre_norm, v_ffn1_post_norm, v_ffn1_w_gate, v_ffn1_w_up, v_ffn1_w_down, v_mix_pre_norm, v_mix_post_norm, v_w_in, v_ssm_lambda_re, v_ssm_lambda_im, v_ssm_log_dt, v_ssm_b_re, v_ssm_b_im, v_ssm_c_re, v_ssm_c_im, v_ssm_d, v_ssm_w_glu, v_pool_w, v_pool_scale, v_ssm_out_norm, v_pool_out_norm, v_w_out, v_ffn2_pre_norm, v_ffn2_post_norm, v_ffn2_w_gate, v_ffn2_w_up, v_ffn2_w_down):
    args = locals()
    p = {n: args[n] for n in ORDER}
    m = {n: args["m_" + n] for n in ORDER}
    v = {n: args["v_" + n] for n in ORDER}
    return _step(p, x, loss_target, m, v)
```

```python
import math

import jax
import jax.numpy as jnp
from jax import lax
from jax.experimental import pallas as pl
from jax.experimental.pallas import tpu as pltpu

F32 = jnp.float32
MXU_DTYPE = jnp.bfloat16
WIRE_DTYPE = jnp.bfloat16

RMS_EPS = 1e-6
N_META = 16
POOL_WINDOWS = (2, 4, 8, 16)
POOL_HALO = 16
ADAM_LR, ADAM_B1, ADAM_B2, ADAM_EPS, ADAM_WD, ADAM_STEP = 0.001, 0.9, 0.999, 1e-08, 0.01, 10

LANES = 128
SUBLANES = 8
VMEM_LIMIT = 60 * 1024 * 1024
FFN_TILE = 432
FFN_CHUNK = 1024
TN_TILE = 1024
MIX_TILE = 216
MIX_SUBTILES = 2
SLAB_GROUP = 8
MESH = pl.DeviceIdType.MESH
ANY = pl.BlockSpec(memory_space=pl.ANY)


def _mm(a, b):
    return jnp.dot(a.astype(MXU_DTYPE), b.astype(MXU_DTYPE), preferred_element_type=F32)


def _mm_nt(a, b):
    return lax.dot_general(a.astype(MXU_DTYPE), b.astype(MXU_DTYPE), (((1,), (1,)), ((), ())), preferred_element_type=F32)


def _mm_tn(a, b):
    return lax.dot_general(a.astype(MXU_DTYPE), b.astype(MXU_DTYPE), (((0,), (0,)), ((), ())), preferred_element_type=F32)


def _rms_stat(x):
    return lax.rsqrt(jnp.mean(x * x, axis=-1, keepdims=True) + RMS_EPS)


def _rms_bwd(x, g, dy):
    r = _rms_stat(x)
    xh = x * r
    dg = jnp.sum(dy * xh, axis=0, keepdims=True)
    dxh = dy * g
    dx = r * (dxh - xh * jnp.mean(dxh * xh, axis=-1, keepdims=True))
    return dx, dg


def _sigmoid(x):
    return 1.0 / (1.0 + jnp.exp(-x))


GELU_C = math.sqrt(2.0 / math.pi)
GELU_K = 0.044715


def _gelu(y):
    return 0.5 * y * (1.0 + jnp.tanh(GELU_C * (y + GELU_K * y * y * y)))


def _gelu_grad(y):
    th = jnp.tanh(GELU_C * (y + GELU_K * y * y * y))
    return 0.5 * (1.0 + th) + 0.5 * y * (1.0 - th * th) * GELU_C * (1.0 + 3.0 * GELU_K * y * y)


def _row_spec(tile, cols, rev_n=None):
    if rev_n is None:
        return pl.BlockSpec((tile, cols), lambda i: (i, 0))
    return pl.BlockSpec((tile, cols), lambda i: (rev_n - 1 - i, 0))


def _full_spec(shape, single=False):
    zeros = (0,) * len(shape)
    if single:
        return pl.BlockSpec(shape, lambda *_: zeros, pipeline_mode=pl.Buffered(1))
    return pl.BlockSpec(shape, lambda *_: zeros)


def _acc(ref, val, first):
    @pl.when(first)
    def _():
        ref[...] = val

    @pl.when(jnp.logical_not(first))
    def _():
        ref[...] += val


def _place():
    x, y, c = lax.axis_index("x"), lax.axis_index("y"), lax.axis_index("c")
    others = [(1 - x, y), (x, 1 - y), (1 - x, 1 - y)]
    return x, y, c, others


class _Exchange:
    mid_step = None

    def __init__(self, ins, out_shapes, aliases, n_sems):
        self.ins, self.out_shapes, self.aliases, self.n_sems = list(ins), list(out_shapes), dict(aliases), n_sems

    def mid(self, ins, outs, send_sems, recv_sems):
        pass


class _SiblingScatter(_Exchange):
    def __init__(self, views):
        super().__init__(views, [jax.ShapeDtypeStruct((4,) + v.shape[2:], v.dtype) for v in views], {}, 4 * len(views))

    def _copies(self, ins, outs, send_sems, recv_sems):
        x, y, c, _ = _place()
        return [
            pltpu.make_async_remote_copy(src_ref=ins[a].at[k, 1 - c], dst_ref=outs[a].at[k], send_sem=send_sems.at[4 * a + k], recv_sem=recv_sems.at[4 * a + k], device_id=(x, y, 1 - c), device_id_type=MESH)
            for a in range(len(ins))
            for k in range(4)
        ]

    def start(self, *refs):
        for cp in self._copies(*refs):
            cp.start()

    def finish(self, *refs):
        cps = self._copies(*refs)
        for cp in cps:
            cp.wait_recv()
        for cp in cps:
            cp.wait_send()


class _ChipScatter(_Exchange):
    def __init__(self, parts):
        super().__init__(parts, [jax.ShapeDtypeStruct((3,) + p.shape[1:], p.dtype) for p in parts], {}, 3 * len(parts))

    def _copies(self, ins, outs, send_sems, recv_sems):
        x, y, c, others = _place()
        return [
            pltpu.make_async_remote_copy(src_ref=ins[a].at[2 * chip[0] + chip[1]], dst_ref=outs[a].at[j], send_sem=send_sems.at[3 * a + j], recv_sem=recv_sems.at[3 * a + j], device_id=(*chip, c), device_id_type=MESH)
            for a in range(len(ins))
            for j, chip in enumerate(others)
        ]

    start = _SiblingScatter.start
    finish = _SiblingScatter.finish


class _SiblingShare(_Exchange):
    def __init__(self, bufs):
        super().__init__(bufs, [jax.ShapeDtypeStruct(b.shape, b.dtype) for b in bufs], {a: a for a in range(len(bufs))}, len(bufs))

    def _copy(self, outs, send_sems, recv_sems, a, half):
        x, y, c, _ = _place()
        mine = outs[a].at[c if half == "mine" else 1 - c]
        return pltpu.make_async_remote_copy(src_ref=mine, dst_ref=mine, send_sem=send_sems.at[a], recv_sem=recv_sems.at[a], device_id=(x, y, 1 - c), device_id_type=MESH)

    def start(self, ins, outs, send_sems, recv_sems):
        for a in range(len(outs)):
            self._copy(outs, send_sems, recv_sems, a, "mine").start()

    def finish(self, ins, outs, send_sems, recv_sems):
        for a in range(len(outs)):
            self._copy(outs, send_sems, recv_sems, a, "theirs").wait_recv()
        for a in range(len(outs)):
            self._copy(outs, send_sems, recv_sems, a, "mine").wait_send()


class _Gather(_Exchange):
    def __init__(self, bufs, mid_step=None):
        super().__init__(bufs, [jax.ShapeDtypeStruct(b.shape, b.dtype) for b in bufs], {a: a for a in range(len(bufs))}, 6 * len(bufs))
        self.mid_step = mid_step

    def _copy(self, outs, send_sems, recv_sems, a, j, chip, half, to):
        blk = outs[a].at[2 * chip[0] + chip[1], half]
        return pltpu.make_async_remote_copy(src_ref=blk, dst_ref=blk, send_sem=send_sems.at[6 * a + j], recv_sem=recv_sems.at[6 * a + j], device_id=to, device_id_type=MESH)

    def start(self, ins, outs, send_sems, recv_sems):
        x, y, c, others = _place()
        for a in range(len(outs)):
            for j, chip in enumerate(others):
                self._copy(outs, send_sems, recv_sems, a, j, (x, y), c, (*chip, c)).start()

    def mid(self, ins, outs, send_sems, recv_sems):
        x, y, c, others = _place()
        for a in range(len(outs)):
            for j, chip in enumerate(others):
                self._copy(outs, send_sems, recv_sems, a, j, chip, c, (x, y, c)).wait_recv()
                self._copy(outs, send_sems, recv_sems, a, 3 + j, chip, c, (x, y, 1 - c)).start()

    def finish(self, ins, outs, send_sems, recv_sems):
        x, y, c, others = _place()
        for a in range(len(outs)):
            for j, chip in enumerate(others):
                self._copy(outs, send_sems, recv_sems, a, 3 + j, chip, 1 - c, (x, y, c)).wait_recv()
        for a in range(len(outs)):
            for j, chip in enumerate(others):
                self._copy(outs, send_sems, recv_sems, a, j, (x, y), c, (*chip, c)).wait_send()
                self._copy(outs, send_sems, recv_sems, a, 3 + j, chip, c, (x, y, 1 - c)).wait_send()


class _SemSlice:
    def __init__(self, sems, off):
        self.sems, self.off = sems, off

    @property
    def at(self):
        return self

    def __getitem__(self, i):
        return self.sems.at[self.off + i]


class _Group(_Exchange):
    def __init__(self, exchanges):
        ins, outs, aliases, n_sems, self.spans = [], [], {}, 0, []
        for ex in exchanges:
            self.spans.append((len(ins), len(outs), n_sems))
            aliases.update({len(ins) + i: len(outs) + o for i, o in ex.aliases.items()})
            ins, outs, n_sems = ins + ex.ins, outs + ex.out_shapes, n_sems + ex.n_sems
        super().__init__(ins, outs, aliases, n_sems)
        self.exchanges = exchanges
        mids = {ex.mid_step for ex in exchanges if ex.mid_step is not None}
        self.mid_step = mids.pop() if mids else None

    def _each(self, method, ins, outs, send_sems, recv_sems):
        for ex, (i0, o0, s0) in zip(self.exchanges, self.spans):
            getattr(ex, method)(ins[i0 : i0 + len(ex.ins)], outs[o0 : o0 + len(ex.out_shapes)], _SemSlice(send_sems, s0), _SemSlice(recv_sems, s0))

    def start(self, *refs):
        self._each("start", *refs)

    def mid(self, *refs):
        self._each("mid", *refs)

    def finish(self, *refs):
        self._each("finish", *refs)

    def split(self, outs):
        return [outs[o0 : o0 + len(ex.out_shapes)] for ex, (_, o0, _) in zip(self.exchanges, self.spans)]


def _exchange_call(ex, name):
    n, m = len(ex.ins), len(ex.out_shapes)

    def body(*refs):
        parts = (refs[:n], refs[n : n + m], refs[n + m], refs[n + m + 1])
        ex.start(*parts)
        ex.mid(*parts)
        ex.finish(*parts)

    return pl.pallas_call(
        body,
        name=name,
        out_shape=ex.out_shapes,
        in_specs=[ANY] * n,
        out_specs=[ANY] * m,
        scratch_shapes=[pltpu.SemaphoreType.DMA((ex.n_sems,)), pltpu.SemaphoreType.DMA((ex.n_sems,))],
        input_output_aliases=ex.aliases,
    )(*ex.ins)


def _pallas(body, *, name, grid, in_specs, out_specs, out_shape, operands, scratch_shapes=(), exchange=None):
    params = pltpu.CompilerParams(dimension_semantics=("arbitrary",) * len(grid), vmem_limit_bytes=VMEM_LIMIT)
    if exchange is None:
        outs = pl.pallas_call(body, name=name, grid=grid, in_specs=in_specs, out_specs=out_specs, out_shape=out_shape, scratch_shapes=list(scratch_shapes), compiler_params=params)(*operands)
        return outs, []
    ex = exchange
    n_in, n_out, n_scr = len(in_specs), len(out_specs), len(scratch_shapes)
    x_in, x_out = len(ex.ins), len(ex.out_shapes)

    def hosted(*refs):
        ins, x_ins = refs[:n_in], refs[n_in : n_in + x_in]
        outs, x_outs = refs[n_in + x_in : n_in + x_in + n_out], refs[n_in + x_in + n_out : n_in + x_in + n_out + x_out]
        rest = refs[n_in + x_in + n_out + x_out :]
        parts = (x_ins, x_outs, rest[n_scr], rest[n_scr + 1])
        ids = [pl.program_id(d) for d in range(len(grid))]
        first = _all([i == 0 for i in ids])
        last = _all([i == g - 1 for i, g in zip(ids, grid)])

        @pl.when(first)
        def _():
            ex.start(*parts)

        body(*ins, *outs, *rest[:n_scr])

        if ex.mid_step is not None:

            @pl.when(ids[0] == ex.mid_step)
            def _():
                ex.mid(*parts)

        @pl.when(last)
        def _():
            ex.finish(*parts)

    outs = pl.pallas_call(
        hosted,
        name=name,
        grid=grid,
        in_specs=list(in_specs) + [ANY] * x_in,
        out_specs=list(out_specs) + [ANY] * x_out,
        out_shape=list(out_shape) + ex.out_shapes,
        scratch_shapes=list(scratch_shapes) + [pltpu.SemaphoreType.DMA((ex.n_sems,)), pltpu.SemaphoreType.DMA((ex.n_sems,))],
        input_output_aliases={n_in + i: n_out + o for i, o in ex.aliases.items()},
        compiler_params=params,
    )(*operands, *ex.ins)
    return outs[:n_out], outs[n_out:]


def _all(conds):
    out = conds[0]
    for c in conds[1:]:
        out = jnp.logical_and(out, c)
    return out


def _row_tile(rows):
    if rows <= 512:
        return rows
    for t in (512, 352, 256, 176, 128, 112, 64, 32, 16, 8):
        if rows % t == 0:
            return t
    return rows


def _add_own_half(view, got, place, name):
    _, _, r, c = view.shape
    tr = _row_tile(r)

    def body(place_ref, v_ref, g_ref, o_ref, w_ref):
        s = v_ref[...] + g_ref[...]
        w_ref[...] = s.astype(w_ref.dtype)

        @pl.when(pl.program_id(1) == place_ref[0])
        def _():
            o_ref[...] = s

    blk = pl.BlockSpec((None, tr, c), lambda i, k, pr: (k, i, 0))
    return pl.pallas_call(
        body,
        name=name,
        out_shape=[jax.ShapeDtypeStruct((r, c), F32), jax.ShapeDtypeStruct((4, r, c), WIRE_DTYPE)],
        grid_spec=pltpu.PrefetchScalarGridSpec(
            num_scalar_prefetch=1,
            grid=(r // tr, 4),
            in_specs=[pl.BlockSpec((None, None, tr, c), lambda i, k, pr: (k, pr[1], i, 0)), blk],
            out_specs=[pl.BlockSpec((tr, c), lambda i, k, pr: (i, 0)), blk],
        ),
        compiler_params=pltpu.CompilerParams(dimension_semantics=("arbitrary", "arbitrary"), vmem_limit_bytes=VMEM_LIMIT),
    )(place, view, got)


def _add_chips(part, got, place, name):
    r, c = part.shape
    tr = _row_tile(r)

    def body(place_ref, p_ref, g_ref, o_ref):
        o_ref[...] = ((p_ref[...] + g_ref[0].astype(F32)) + g_ref[1].astype(F32)) + g_ref[2].astype(F32)

    return pl.pallas_call(
        body,
        name=name,
        out_shape=jax.ShapeDtypeStruct((2, r, c), F32),
        grid_spec=pltpu.PrefetchScalarGridSpec(
            num_scalar_prefetch=1,
            grid=(r // tr,),
            in_specs=[pl.BlockSpec((tr, c), lambda i, pr: (i, 0)), pl.BlockSpec((3, tr, c), lambda i, pr: (0, i, 0))],
            out_specs=pl.BlockSpec((None, tr, c), lambda i, pr: (pr[1], i, 0)),
        ),
        compiler_params=pltpu.CompilerParams(dimension_semantics=("arbitrary",), vmem_limit_bytes=VMEM_LIMIT),
    )(place, part, got)


def _adamw_update(w_ref, g_ref, m_ref, v_ref, d_ref, nm_ref, nv_ref):
    g = g_ref[...]
    nm = ADAM_B1 * m_ref[...] + (1.0 - ADAM_B1) * g
    nv = ADAM_B2 * v_ref[...] + (1.0 - ADAM_B2) * (g * g)
    m_hat = nm / (1.0 - ADAM_B1**ADAM_STEP)
    v_hat = nv / (1.0 - ADAM_B2**ADAM_STEP)
    d_ref[...] = -ADAM_LR * (m_hat / (jnp.sqrt(v_hat) + ADAM_EPS) + ADAM_WD * w_ref[...])
    nm_ref[...] = nm
    nv_ref[...] = nv


def _adamw(w, g, m, v, name):
    r, c = w.shape
    tr = _row_tile(r)
    spec = pl.BlockSpec((tr, c), lambda i: (i, 0))
    outs, _ = _pallas(_adamw_update, name=name, grid=(r // tr,), in_specs=[spec] * 4, out_specs=[spec] * 3, out_shape=[jax.ShapeDtypeStruct((r, c), F32)] * 3, operands=(w, g, m, v))
    return outs


def _adamw_many(ws, gs, ms, vs, name):
    n = len(ws)

    def body(*refs):
        for k in range(n):
            _adamw_update(*(refs[j * n + k] for j in range(7)))

    outs = pl.pallas_call(
        body,
        name=name,
        out_shape=[jax.ShapeDtypeStruct(w.shape, F32) for w in ws] * 3,
        in_specs=[pl.BlockSpec(memory_space=pltpu.VMEM)] * (4 * n),
        out_specs=[pl.BlockSpec(memory_space=pltpu.VMEM)] * (3 * n),
    )(*ws, *gs, *ms, *vs)
    return outs[:n], outs[n : 2 * n], outs[2 * n :]


def _load_weights(pairs, sems):
    @pl.when(pl.program_id(0) == 0)
    def _():
        cps = [pltpu.make_async_copy(src, dst, sems.at[k]) for k, (src, dst) in enumerate(pairs)]
        for cp in cps:
            cp.start()
        for cp in cps:
            cp.wait()


def _ffn_chunks(F):
    bounds = list(range(0, F, FFN_CHUNK)) + [F]
    return list(zip(bounds[:-1], bounds[1:]))


def _shifted_specs(tm, cols):
    per = tm // N_META
    return [_row_spec(tm, cols), pl.BlockSpec((N_META, cols), lambda i: (jnp.maximum(i * per - 1, 0), 0))]


def _shifted_tile(cur_ref, before_ref, tm):
    return jnp.concatenate([before_ref[...], cur_ref[0 : tm - N_META, :]], axis=0)


def _tokens_tile(cur_ref, before_ref, meta_ref, tm, tile_0):
    first = jnp.where(tile_0, meta_ref[...], before_ref[...])
    return jnp.concatenate([first, cur_ref[0 : tm - N_META, :]], axis=0)


def _ffn_fwd_loss(h, target, g_pre, g_post, wg, wu, wd, name):
    T, D = h.shape
    F = wg.shape[0]
    tm = FFN_TILE

    def body(h_ref, t_ref, tb_ref, gpre_ref, gpost_ref, wg_hbm, wu_hbm, wd_hbm, dy_ref, sq_ref, f_ref, ga_ref, si_ref, s_ref, n_ref, wg_v, wu_v, wd_v, sems):
        i = pl.program_id(0)
        _load_weights([(wg_hbm, wg_v), (wu_hbm, wu_v), (wd_hbm, wd_v)], sems)
        hh = h_ref[...]
        n = (hh * _rms_stat(hh) * gpre_ref[...]).astype(MXU_DTYPE)
        n_ref[...] = n.astype(n_ref.dtype)
        f = jnp.zeros((tm, D), F32)
        for lo, hi in _ffn_chunks(F):
            a = _mm_nt(n, wg_v[lo:hi, :])
            b = _mm_nt(n, wu_v[lo:hi, :])
            sg = _sigmoid(a)
            si = a * sg
            s = (si * b).astype(MXU_DTYPE)
            ga_ref[:, lo:hi] = (b * (sg * (1.0 + a * (1.0 - sg)))).astype(ga_ref.dtype)
            si_ref[:, lo:hi] = si.astype(si_ref.dtype)
            s_ref[:, lo:hi] = s.astype(s_ref.dtype)
            f = f + _mm(s, wd_v[lo:hi, :])
        f_ref[...] = f
        y = hh + 0.5 * (f * _rms_stat(f) * gpost_ref[...])
        rows = i * tm + lax.broadcasted_iota(jnp.int32, (tm, D), 0)
        err = jnp.where(rows >= N_META, y - _shifted_tile(t_ref, tb_ref, tm), 0.0)
        dy_ref[...] = err * (1.0 / D)
        _acc(sq_ref, jnp.sum(err * err, axis=0, keepdims=True), i == 0)

    tok = jax.ShapeDtypeStruct((T, D), F32)
    act = jax.ShapeDtypeStruct((T, F), MXU_DTYPE)
    outs, _ = _pallas(
        body,
        name=name,
        grid=(T // tm,),
        out_shape=[tok, jax.ShapeDtypeStruct((1, D), F32), tok, act, act, act, jax.ShapeDtypeStruct((T, D), MXU_DTYPE)],
        in_specs=[_row_spec(tm, D)] + _shifted_specs(tm, D) + [_full_spec((1, D)), _full_spec((1, D)), ANY, ANY, ANY],
        out_specs=[_row_spec(tm, D), _full_spec((1, D)), _row_spec(tm, D), _row_spec(tm, F), _row_spec(tm, F), _row_spec(tm, F), _row_spec(tm, D)],
        scratch_shapes=[pltpu.VMEM(wg.shape, wg.dtype), pltpu.VMEM(wu.shape, wu.dtype), pltpu.VMEM(wd.shape, wd.dtype), pltpu.SemaphoreType.DMA((3,))],
        operands=(h, target, target, g_pre, g_post, wg, wu, wd),
    )
    return outs


def _ffn_gate_up(x, meta, g_pre, wg, wu, name, exchange=None):
    D = x.shape[1]
    T = x.shape[0] + N_META
    F = wg.shape[0]
    tm = FFN_TILE

    def body(x_ref, xb_ref, meta_ref, gpre_ref, wg_hbm, wu_hbm, ga_ref, si_ref, s_ref, n_ref, wg_v, wu_v, sems):
        _load_weights([(wg_hbm, wg_v), (wu_hbm, wu_v)], sems)
        hh = _tokens_tile(x_ref, xb_ref, meta_ref, tm, pl.program_id(0) == 0)
        n = (hh * _rms_stat(hh) * gpre_ref[...]).astype(MXU_DTYPE)
        n_ref[...] = n.astype(n_ref.dtype)
        for lo, hi in _ffn_chunks(F):
            a = _mm_nt(n, wg_v[lo:hi, :])
            b = _mm_nt(n, wu_v[lo:hi, :])
            sg = _sigmoid(a)
            si = a * sg
            ga_ref[:, lo:hi] = (b * (sg * (1.0 + a * (1.0 - sg)))).astype(ga_ref.dtype)
            si_ref[:, lo:hi] = si.astype(si_ref.dtype)
            s_ref[:, lo:hi] = (si * b).astype(s_ref.dtype)

    act = jax.ShapeDtypeStruct((T, F), MXU_DTYPE)
    return _pallas(
        body,
        name=name,
        grid=(T // tm,),
        out_shape=[act, act, act, jax.ShapeDtypeStruct((T, D), MXU_DTYPE)],
        in_specs=_shifted_specs(tm, D) + [_full_spec((N_META, D)), _full_spec((1, D)), ANY, ANY],
        out_specs=[_row_spec(tm, F), _row_spec(tm, F), _row_spec(tm, F), _row_spec(tm, D)],
        scratch_shapes=[pltpu.VMEM(wg.shape, wg.dtype), pltpu.VMEM(wu.shape, wu.dtype), pltpu.SemaphoreType.DMA((2,))],
        operands=(x, x, meta, g_pre, wg, wu),
        exchange=exchange,
    )


def _ffn_down(x, meta, s, g_post, wd, name, exchange=None):
    D = x.shape[1]
    T = x.shape[0] + N_META
    F = wd.shape[0]
    tm = FFN_TILE

    def body(x_ref, xb_ref, meta_ref, s_ref, gpost_ref, wd_hbm, hout_ref, f_ref, wd_v, sems):
        _load_weights([(wd_hbm, wd_v)], sems)
        f = _mm(s_ref[...], wd_v[...])
        f_ref[...] = f
        hout_ref[...] = _tokens_tile(x_ref, xb_ref, meta_ref, tm, pl.program_id(0) == 0) + 0.5 * (f * _rms_stat(f) * gpost_ref[...])

    tok = jax.ShapeDtypeStruct((T, D), F32)
    return _pallas(
        body,
        name=name,
        grid=(T // tm,),
        out_shape=[tok, tok],
        in_specs=_shifted_specs(tm, D) + [_full_spec((N_META, D)), _row_spec(tm, F), _full_spec((1, D)), ANY],
        out_specs=[_row_spec(tm, D), _row_spec(tm, D)],
        scratch_shapes=[pltpu.VMEM(wd.shape, wd.dtype), pltpu.SemaphoreType.DMA((1,))],
        operands=(x, x, meta, s, g_post, wd),
        exchange=exchange,
    )


def _ffn_bwd(dh, f, ga, si, h, g_post, g_pre, wg, wu, wd, name):
    T, D = dh.shape
    F = wd.shape[0]
    tm = FFN_TILE

    def body(dh_ref, f_ref, ga_ref, si_ref, h_ref, gpost_ref, gpre_ref, wg_hbm, wu_hbm, wd_hbm, da_ref, db_ref, df_ref, dhin_ref, dgpost_ref, dgpre_ref, wg_v, wu_v, wd_v, sems):
        first = pl.program_id(0) == 0
        _load_weights([(wg_hbm, wg_v), (wu_hbm, wu_v), (wd_hbm, wd_v)], sems)
        dh = dh_ref[...]
        df, dg = _rms_bwd(f_ref[...], gpost_ref[...], 0.5 * dh)
        _acc(dgpost_ref, dg, first)
        dfb = df.astype(MXU_DTYPE)
        df_ref[...] = dfb.astype(df_ref.dtype)
        dn = jnp.zeros((tm, D), F32)
        for lo, hi in _ffn_chunks(F):
            ds = _mm_nt(dfb, wd_v[lo:hi, :])
            da = (ds * ga_ref[:, lo:hi].astype(F32)).astype(MXU_DTYPE)
            db = (ds * si_ref[:, lo:hi].astype(F32)).astype(MXU_DTYPE)
            da_ref[:, lo:hi] = da.astype(da_ref.dtype)
            db_ref[:, lo:hi] = db.astype(db_ref.dtype)
            dn = dn + _mm(da, wg_v[lo:hi, :]) + _mm(db, wu_v[lo:hi, :])
        dx, dg = _rms_bwd(h_ref[...], gpre_ref[...], dn)
        _acc(dgpre_ref, dg, first)
        dhin_ref[...] = dh + dx

    act = jax.ShapeDtypeStruct((T, F), MXU_DTYPE)
    vec = jax.ShapeDtypeStruct((1, D), F32)
    outs, _ = _pallas(
        body,
        name=name,
        grid=(T // tm,),
        out_shape=[act, act, jax.ShapeDtypeStruct((T, D), MXU_DTYPE), jax.ShapeDtypeStruct((T, D), F32), vec, vec],
        in_specs=[_row_spec(tm, D), _row_spec(tm, D), _row_spec(tm, F), _row_spec(tm, F), _row_spec(tm, D), _full_spec((1, D)), _full_spec((1, D)), ANY, ANY, ANY],
        out_specs=[_row_spec(tm, F), _row_spec(tm, F), _row_spec(tm, D), _row_spec(tm, D), _full_spec((1, D)), _full_spec((1, D))],
        scratch_shapes=[pltpu.VMEM(wg.shape, wg.dtype), pltpu.VMEM(wu.shape, wu.dtype), pltpu.VMEM(wd.shape, wd.dtype), pltpu.SemaphoreType.DMA((3,))],
        operands=(dh, f, ga, si, h, g_post, g_pre, wg, wu, wd),
    )
    return outs


def _ffn_bwd_first(dh, f, ga, si, x, meta, g_post, g_pre, wg, wu, wd, name):
    D = x.shape[1]
    T = x.shape[0] + N_META
    F = wd.shape[0]
    tm = FFN_TILE
    nt = T // tm
    per = tm // N_META
    tile = lambda i: jnp.minimum(i, nt - 1)

    def body(dh_ref, f_ref, ga_ref, si_ref, x_ref, xb_ref, meta_ref, gpost_ref, gpre_ref, wg_hbm, wu_hbm, wd_hbm,
             da_ref, db_ref, df_ref, dx_ref, dmeta_ref, dgpost_ref, dgpre_ref, wg_v, wu_v, wd_v, sems, held):
        i = pl.program_id(0)
        _load_weights([(wg_hbm, wg_v), (wu_hbm, wu_v), (wd_hbm, wd_v)], sems)

        @pl.when(i < nt)
        def _():
            dh = dh_ref[...]
            df, dg = _rms_bwd(f_ref[...], gpost_ref[...], 0.5 * dh)
            _acc(dgpost_ref, dg, i == 0)
            dfb = df.astype(MXU_DTYPE)
            df_ref[...] = dfb.astype(df_ref.dtype)
            dn = jnp.zeros((tm, D), F32)
            for lo, hi in _ffn_chunks(F):
                ds = _mm_nt(dfb, wd_v[lo:hi, :])
                da = (ds * ga_ref[:, lo:hi].astype(F32)).astype(MXU_DTYPE)
                db = (ds * si_ref[:, lo:hi].astype(F32)).astype(MXU_DTYPE)
                da_ref[:, lo:hi] = da.astype(da_ref.dtype)
                db_ref[:, lo:hi] = db.astype(db_ref.dtype)
                dn = dn + _mm(da, wg_v[lo:hi, :]) + _mm(db, wu_v[lo:hi, :])
            dx, dg = _rms_bwd(_tokens_tile(x_ref, xb_ref, meta_ref, tm, i == 0), gpre_ref[...], dn)
            _acc(dgpre_ref, dg, i == 0)
            dh_in = dh + dx

            @pl.when(i == 0)
            def _():
                dmeta_ref[...] = dh_in[0:N_META, :]

            @pl.when(i > 0)
            def _():
                dx_ref[...] = jnp.concatenate([held[...], dh_in[0:N_META, :]], axis=0)

            held[...] = dh_in[N_META:, :]

        @pl.when(i == nt)
        def _():
            dx_ref[0 : tm - N_META, :] = held[...]

    rows = lambda cols: pl.BlockSpec((tm, cols), lambda i: (tile(i), 0))
    act = jax.ShapeDtypeStruct((T, F), MXU_DTYPE)
    vec = jax.ShapeDtypeStruct((1, D), F32)
    outs, _ = _pallas(
        body,
        name=name,
        grid=(nt + 1,),
        out_shape=[act, act, jax.ShapeDtypeStruct((T, D), MXU_DTYPE), jax.ShapeDtypeStruct((T - N_META, D), F32), jax.ShapeDtypeStruct((N_META, D), F32), vec, vec],
        in_specs=[rows(D), rows(D), rows(F), rows(F), rows(D), pl.BlockSpec((N_META, D), lambda i: (jnp.maximum(tile(i) * per - 1, 0), 0)), _full_spec((N_META, D)),
                  _full_spec((1, D)), _full_spec((1, D)), ANY, ANY, ANY],
        out_specs=[rows(F), rows(F), rows(D), pl.BlockSpec((tm, D), lambda i: (jnp.maximum(i - 1, 0), 0)), _full_spec((N_META, D)), _full_spec((1, D)), _full_spec((1, D))],
        scratch_shapes=[pltpu.VMEM(wg.shape, wg.dtype), pltpu.VMEM(wu.shape, wu.dtype), pltpu.VMEM(wd.shape, wd.dtype), pltpu.SemaphoreType.DMA((3,)), pltpu.VMEM((tm - N_META, D), F32)],
        operands=(dh, f, ga, si, x, x, meta, g_post, g_pre, wg, wu, wd),
    )
    return outs


def _token_tile(T):
    for t in (912, 864, 432):
        if T % t == 0:
            return t
    raise ValueError(f"no token tile for {T} rows")


def _tn_matmul(xm, ym, name, exchange=None):
    T, M = xm.shape
    N = ym.shape[1]
    if (T - N_META) % TN_TILE:
        tk = _token_tile(T)

        def body(x_ref, y_ref, o_ref):
            _acc(o_ref, _mm_tn(x_ref[...], y_ref[...]), pl.program_id(0) == 0)

        grid, operands = (T // tk,), (xm, ym)
        in_specs = [pl.BlockSpec((tk, M), lambda k: (k, 0)), pl.BlockSpec((tk, N), lambda k: (k, 0))]
    else:
        tk = TN_TILE

        def body(x_ref, y_ref, xh_ref, yh_ref, o_ref):
            prod = _mm_tn(x_ref[...], y_ref[...])

            @pl.when(pl.program_id(0) == 0)
            def _():
                o_ref[...] = prod + _mm_tn(xh_ref[...], yh_ref[...])

            @pl.when(pl.program_id(0) > 0)
            def _():
                o_ref[...] += prod

        grid, operands = ((T - N_META) // tk,), (xm, ym, xm, ym)
        start = lambda k: (pl.multiple_of(N_META + k * tk, N_META), 0)
        in_specs = [pl.BlockSpec((pl.Element(tk), pl.Element(M)), start), pl.BlockSpec((pl.Element(tk), pl.Element(N)), start),
                    pl.BlockSpec((N_META, M), lambda k: (0, 0)), pl.BlockSpec((N_META, N), lambda k: (0, 0))]

    (out,), x_outs = _pallas(
        body,
        name=name,
        grid=grid,
        out_shape=[jax.ShapeDtypeStruct((M, N), F32)],
        in_specs=in_specs,
        out_specs=[_full_spec((M, N))],
        operands=operands,
        exchange=exchange,
    )
    return out, x_outs


TAB_A, TAB_AS1, TAB_AS2, TAB_AS4, TAB_JF, TAB_JB = 0, 2, 4, 6, 8, 10


def _scan_inplace(zr, zi, tabs, pows, car_r, car_i, seg, reverse, base=0):
    n_slabs = zr.shape[0]
    sgn = -1.0 if reverse else 1.0
    row = lax.broadcasted_iota(jnp.int32, (SUBLANES, LANES), 0)

    def cmul(pr, pi, xr, xi):
        return pr * xr - pi * xi, pr * xi + pi * xr

    for k0 in range(0, n_slabs, SLAB_GROUP):
        slabs = range(k0, min(k0 + SLAB_GROUP, n_slabs))
        ar = [tabs[TAB_A, k] for k in slabs]
        ai = [sgn * tabs[TAB_A + 1, k] for k in slabs]

        def first_pass(t, carry):
            r = (seg - 1 - t) if reverse else t
            out = []
            for q, k in enumerate(slabs):
                xr, xi = carry[2 * q], carry[2 * q + 1]
                pr, pi = cmul(ar[q], ai[q], xr, xi)
                nr = pr + zr[k, pl.ds(base + r, SUBLANES, stride=seg), :]
                ni = pi + zi[k, pl.ds(base + r, SUBLANES, stride=seg), :]
                zr[k, pl.ds(base + r, SUBLANES, stride=seg), :] = nr
                zi[k, pl.ds(base + r, SUBLANES, stride=seg), :] = ni
                out += [nr, ni]
            return tuple(out)

        ends = lax.fori_loop(0, seg, first_pass, tuple(jnp.zeros((SUBLANES, LANES), F32) for _ in range(2 * len(slabs))))

        incoming = []
        for q, k in enumerate(slabs):
            fr, fi = ends[2 * q], ends[2 * q + 1]
            for d, tab in ((1, TAB_AS1), (2, TAB_AS2), (4, TAB_AS4)):
                shift, keep = (SUBLANES - d, row < SUBLANES - d) if reverse else (d, row >= d)
                sr = jnp.where(keep, pltpu.roll(fr, shift, 0), 0.0)
                si = jnp.where(keep, pltpu.roll(fi, shift, 0), 0.0)
                pr, pi = cmul(tabs[tab, k], sgn * tabs[tab + 1, k], sr, si)
                fr, fi = fr + pr, fi + pi
            cr, ci = car_r[k], car_i[k]
            jtab = TAB_JB if reverse else TAB_JF
            pr, pi = cmul(tabs[jtab, k], sgn * tabs[jtab + 1, k], cr, ci)
            er, ei = fr + pr, fi + pi
            if reverse:
                inr = jnp.where(row < SUBLANES - 1, pltpu.roll(er, SUBLANES - 1, 0), cr)
                ini = jnp.where(row < SUBLANES - 1, pltpu.roll(ei, SUBLANES - 1, 0), ci)
                car_r[k] = jnp.broadcast_to(er[0:1, :], (SUBLANES, LANES))
                car_i[k] = jnp.broadcast_to(ei[0:1, :], (SUBLANES, LANES))
            else:
                inr = jnp.where(row >= 1, pltpu.roll(er, 1, 0), cr)
                ini = jnp.where(row >= 1, pltpu.roll(ei, 1, 0), ci)
                car_r[k] = jnp.broadcast_to(er[SUBLANES - 1 : SUBLANES, :], (SUBLANES, LANES))
                car_i[k] = jnp.broadcast_to(ei[SUBLANES - 1 : SUBLANES, :], (SUBLANES, LANES))
            incoming += [inr, ini]

        def second_pass(r, _):
            p = (seg - 1 - r) if reverse else r
            for q, k in enumerate(slabs):
                pr, pi = cmul(pows[0, k, p], sgn * pows[1, k, p], incoming[2 * q], incoming[2 * q + 1])
                zr[k, pl.ds(base + r, SUBLANES, stride=seg), :] = zr[k, pl.ds(base + r, SUBLANES, stride=seg), :] + pr
                zi[k, pl.ds(base + r, SUBLANES, stride=seg), :] = zi[k, pl.ds(base + r, SUBLANES, stride=seg), :] + pi
            return 0

        lax.fori_loop(0, seg, second_pass, 0)


def _slabs_to_cols(ref, k0, n):
    return jnp.concatenate([ref[k0 + q] for q in range(n)], axis=1)


def _window_sum(ext, doublings, forward):
    rows = ext.shape[0]
    s = ext
    for k in range(doublings):
        s = s + pltpu.roll(s, (1 << k) if forward else rows - (1 << k), 0)
    return s


def _mix_fwd(h1, g_pre, g_so, g_po, g_post, dskip, pscale, win, wout, bbre, bbim, ccre, ccim, wgv, wgg, pw, tabs, pows, name, exchange=None):
    T, D = h1.shape
    W = D // 2
    tm = MIX_SUBTILES * MIX_TILE
    seg = MIX_TILE // SUBLANES
    n_slabs = tabs.shape[1]
    nch, cch, sch = bbre.shape
    spc = sch // LANES
    pg = W // len(POOL_WINDOWS)

    def body(h_ref, gpre_ref, gso_ref, gpo_ref, gpost_ref, dskip_ref, pscale_ref, win_ref, wout_ref, bbre_ref, bbim_ref, ccre_ref, ccim_ref, wgv_ref, wgg_ref, pw_ref, tabs_ref, pows_ref,
             proj_ref, xr_ref, xi_ref, y_ref, pooled_ref, mixed_ref, h2_ref, n2_ref, cat_ref, car_r, car_i, halo):
        i = pl.program_id(0)

        @pl.when(i == 0)
        def _():
            car_r[...] = jnp.zeros_like(car_r)
            car_i[...] = jnp.zeros_like(car_i)
            halo[...] = jnp.zeros_like(halo)

        hh = h_ref[...]
        n2 = (hh * _rms_stat(hh) * gpre_ref[...]).astype(MXU_DTYPE)
        n2_ref[...] = n2.astype(n2_ref.dtype)
        proj = _mm(n2, win_ref[...])
        proj_ref[...] = proj
        us, up = proj[:, :W], proj[:, W:]

        for c in range(nch):
            uc = us[:, c * cch : (c + 1) * cch].astype(MXU_DTYPE)
            bur, bui = _mm(uc, bbre_ref[c]), _mm(uc, bbim_ref[c])
            for q in range(spc):
                xr_ref[c * spc + q] = bur[:, q * LANES : (q + 1) * LANES]
                xi_ref[c * spc + q] = bui[:, q * LANES : (q + 1) * LANES]
        for sub in range(MIX_SUBTILES):
            _scan_inplace(xr_ref, xi_ref, tabs_ref, pows_ref, car_r, car_i, seg, reverse=False, base=sub * MIX_TILE)
        ys = []
        for c in range(nch):
            ys.append(_mm(_slabs_to_cols(xr_ref, c * spc, spc), ccre_ref[c]) - _mm(_slabs_to_cols(xi_ref, c * spc, spc), ccim_ref[c]))
        y = jnp.concatenate(ys, axis=1) + dskip_ref[...] * us
        y_ref[...] = y
        ge = _gelu(y).astype(MXU_DTYPE)
        zv = jnp.concatenate([_mm(ge[:, c * cch : (c + 1) * cch], wgv_ref[c]) for c in range(nch)], axis=1)
        zg = jnp.concatenate([_mm(ge[:, c * cch : (c + 1) * cch], wgg_ref[c]) for c in range(nch)], axis=1)
        out = zv * _sigmoid(zg)
        cat_s = out * _rms_stat(out) * gso_ref[...]

        ext = jnp.concatenate([halo[...], up], axis=0)
        halo[...] = up[tm - POOL_HALO :, :]
        t1 = (i * tm + 1 + lax.broadcasted_iota(jnp.int32, (tm, pg), 0)).astype(F32)
        pooled, pms = [], []
        for g, w in enumerate(POOL_WINDOWS):
            col = ext[:, g * pg : (g + 1) * pg]
            win_sum = _window_sum(col, g + 1, True)[POOL_HALO:, :]
            pooled_g = win_sum / jnp.minimum(t1, float(w)) - up[:, g * pg : (g + 1) * pg]
            pooled.append(pooled_g)
            pms.append(_mm(pooled_g, pw_ref[g]))
        pooled_ref[...] = jnp.concatenate(pooled, axis=1)
        yp = jnp.concatenate(pms, axis=1) * pscale_ref[...]
        cat_p = yp * _rms_stat(yp) * gpo_ref[...]

        cat = jnp.concatenate([cat_s, cat_p], axis=1).astype(MXU_DTYPE)
        cat_ref[...] = cat.astype(cat_ref.dtype)
        mixed = _mm(cat, wout_ref[...])
        mixed_ref[...] = mixed
        h2_ref[...] = hh + mixed * _rms_stat(mixed) * gpost_ref[...]

    tok = lambda cols, dt=F32: jax.ShapeDtypeStruct((T, cols), dt)
    slab_spec = pl.BlockSpec((n_slabs, tm, LANES), lambda i: (0, i, 0))
    operands = (h1, g_pre, g_so, g_po, g_post, dskip, pscale, win, wout, bbre, bbim, ccre, ccim, wgv, wgg, pw, tabs, pows)
    return _pallas(
        body,
        name=name,
        grid=(T // tm,),
        out_shape=[tok(D), jax.ShapeDtypeStruct((n_slabs, T, LANES), F32), jax.ShapeDtypeStruct((n_slabs, T, LANES), F32), tok(W), tok(W), tok(D), tok(D), tok(D, MXU_DTYPE), tok(D, MXU_DTYPE)],
        in_specs=[_row_spec(tm, D)] + [_full_spec(o.shape, single=True) for o in operands[1:]],
        out_specs=[_row_spec(tm, D), slab_spec, slab_spec, _row_spec(tm, W), _row_spec(tm, W), _row_spec(tm, D), _row_spec(tm, D), _row_spec(tm, D), _row_spec(tm, D)],
        scratch_shapes=[pltpu.VMEM((n_slabs, SUBLANES, LANES), F32), pltpu.VMEM((n_slabs, SUBLANES, LANES), F32), pltpu.VMEM((POOL_HALO, W), F32)],
        operands=operands,
        exchange=exchange,
    )


def _mix_bwd_heads(dh2, mixed, y, pooled, proj, g_so, g_po, g_post, pscale, wout, wgv, wgg, pw, name, exchange=None):
    T, D = dh2.shape
    W = D // 2
    tm = _token_tile(T)
    nch, cch, _ = wgv.shape
    ng, pg, _ = pw.shape

    def body(dh2_ref, mixed_ref, y_ref, pooled_ref, us_ref, gso_ref, gpo_ref, gpost_ref, pscale_ref, wout_ref, wgv_ref, wgg_ref, pw_ref,
             dy_ref, dpooled_ref, dmixed_ref, dgpost_ref, dgso_ref, dgpo_ref, dd_ref, dscale_ref, dwgv_ref, dwgg_ref, dpw_ref):
        first = pl.program_id(0) == 0
        dmixed, dgpost = _rms_bwd(mixed_ref[...], gpost_ref[...], dh2_ref[...])
        _acc(dgpost_ref, dgpost, first)
        dmb = dmixed.astype(MXU_DTYPE)
        dmixed_ref[...] = dmb.astype(dmixed_ref.dtype)
        dcat = _mm_nt(dmb, wout_ref[...])
        dcs, dcp = dcat[:, :W], dcat[:, W:]

        y = y_ref[...]
        ge = _gelu(y).astype(MXU_DTYPE)
        zv = jnp.concatenate([_mm(ge[:, c * cch : (c + 1) * cch], wgv_ref[c]) for c in range(nch)], axis=1)
        zg = jnp.concatenate([_mm(ge[:, c * cch : (c + 1) * cch], wgg_ref[c]) for c in range(nch)], axis=1)
        sg = _sigmoid(zg)
        dout, dgso = _rms_bwd(zv * sg, gso_ref[...], dcs)
        _acc(dgso_ref, dgso, first)
        dzv = (dout * sg).astype(MXU_DTYPE)
        dzg = (dout * zv * sg * (1.0 - sg)).astype(MXU_DTYPE)
        dges = []
        for c in range(nch):
            cs = slice(c * cch, (c + 1) * cch)
            dges.append(_mm_nt(dzv[:, cs], wgv_ref[c]) + _mm_nt(dzg[:, cs], wgg_ref[c]))
            _acc(dwgv_ref.at[c], _mm_tn(ge[:, cs], dzv[:, cs]), first)
            _acc(dwgg_ref.at[c], _mm_tn(ge[:, cs], dzg[:, cs]), first)
        dy = jnp.concatenate(dges, axis=1) * _gelu_grad(y)
        dy_ref[...] = dy
        _acc(dd_ref, jnp.sum(dy * us_ref[...], axis=0, keepdims=True), first)

        pooled_b = pooled_ref[...].astype(MXU_DTYPE)
        pm = jnp.concatenate([_mm(pooled_b[:, g * pg : (g + 1) * pg], pw_ref[g]) for g in range(ng)], axis=1)
        dyp, dgpo = _rms_bwd(pm * pscale_ref[...], gpo_ref[...], dcp)
        _acc(dgpo_ref, dgpo, first)
        _acc(dscale_ref, jnp.sum(dyp * pm, axis=0, keepdims=True), first)
        dpm = (dyp * pscale_ref[...]).astype(MXU_DTYPE)
        dps = []
        for g in range(ng):
            gs = slice(g * pg, (g + 1) * pg)
            dps.append(_mm_nt(dpm[:, gs], pw_ref[g]))
            _acc(dpw_ref.at[g], _mm_tn(pooled_b[:, gs], dpm[:, gs]), first)
        dpooled_ref[...] = jnp.concatenate(dps, axis=1)

    vec = lambda n: jax.ShapeDtypeStruct((1, n), F32)
    operands = (dh2, mixed, y, pooled, proj, g_so, g_po, g_post, pscale, wout, wgv, wgg, pw)
    return _pallas(
        body,
        name=name,
        grid=(T // tm,),
        out_shape=[jax.ShapeDtypeStruct((T, W), F32), jax.ShapeDtypeStruct((T, W), F32), jax.ShapeDtypeStruct((T, D), MXU_DTYPE), vec(D), vec(W), vec(W), vec(W), vec(W),
                   jax.ShapeDtypeStruct(wgv.shape, F32), jax.ShapeDtypeStruct(wgg.shape, F32), jax.ShapeDtypeStruct(pw.shape, F32)],
        in_specs=[_row_spec(tm, D), _row_spec(tm, D), _row_spec(tm, W), _row_spec(tm, W), _row_spec(tm, W)] + [_full_spec(o.shape) for o in operands[5:]],
        out_specs=[_row_spec(tm, W), _row_spec(tm, W), _row_spec(tm, D), _full_spec((1, D)), _full_spec((1, W)), _full_spec((1, W)), _full_spec((1, W)), _full_spec((1, W)),
                   _full_spec(wgv.shape), _full_spec(wgg.shape), _full_spec(pw.shape)],
        operands=operands,
        exchange=exchange,
    )


def _mix_bwd_scan(dy, dpooled, xr, xi, proj, dskip, bbre, bbim, ccre, ccim, tabs, pows, name, exchange=None):
    T, W = dy.shape
    D = 2 * W
    tm = MIX_SUBTILES * MIX_TILE
    seg = MIX_TILE // SUBLANES
    nt = T // tm
    n_slabs = tabs.shape[1]
    nch, cch, sch = bbre.shape
    spc = sch // LANES
    pg = W // len(POOL_WINDOWS)
    blocks_per_tile = tm // SUBLANES

    def body(dy_ref, dp_ref, xr_ref, xi_ref, xpr_ref, xpi_ref, us_ref, dskip_ref, bbre_ref, bbim_ref, ccre_ref, ccim_ref, tabs_ref, pows_ref,
             dproj_ref, dccre_ref, dccim_ref, dbbre_ref, dbbim_ref, dar_ref, dai_ref, lr, li, car_r, car_i, halo):
        i = pl.program_id(0)
        first = i == 0
        tile = nt - 1 - i
        row = lax.broadcasted_iota(jnp.int32, (SUBLANES, LANES), 0)

        @pl.when(first)
        def _():
            car_r[...] = jnp.zeros_like(car_r)
            car_i[...] = jnp.zeros_like(car_i)
            halo[...] = jnp.zeros_like(halo)
            dar_ref[...] = jnp.zeros_like(dar_ref)
            dai_ref[...] = jnp.zeros_like(dai_ref)

        dy = dy_ref[...]
        for c in range(nch):
            dyc = dy[:, c * cch : (c + 1) * cch]
            gr, gi = _mm_nt(dyc, ccre_ref[c]), _mm_nt(dyc, ccim_ref[c])
            for q in range(spc):
                lr[c * spc + q] = gr[:, q * LANES : (q + 1) * LANES]
                li[c * spc + q] = -gi[:, q * LANES : (q + 1) * LANES]
            _acc(dccre_ref.at[c], _mm_tn(_slabs_to_cols(xr_ref, c * spc, spc), dyc), first)
            _acc(dccim_ref.at[c], -_mm_tn(_slabs_to_cols(xi_ref, c * spc, spc), dyc), first)
        for sub in reversed(range(MIX_SUBTILES)):
            _scan_inplace(lr, li, tabs_ref, pows_ref, car_r, car_i, seg, reverse=True, base=sub * MIX_TILE)

        for sub in range(MIX_SUBTILES):
            base = sub * MIX_TILE
            for k0 in range(0, n_slabs, SLAB_GROUP):
                slabs = range(k0, min(k0 + SLAB_GROUP, n_slabs))
                init = []
                for k in slabs:
                    if sub == 0:
                        prev_r = jnp.where(tile > 0, jnp.broadcast_to(xpr_ref[k, SUBLANES - 1 : SUBLANES, :], (SUBLANES, LANES)), 0.0)
                        prev_i = jnp.where(tile > 0, jnp.broadcast_to(xpi_ref[k, SUBLANES - 1 : SUBLANES, :], (SUBLANES, LANES)), 0.0)
                    else:
                        prev_r = jnp.broadcast_to(xr_ref[k, base - 1 : base, :], (SUBLANES, LANES))
                        prev_i = jnp.broadcast_to(xi_ref[k, base - 1 : base, :], (SUBLANES, LANES))
                    x0r = jnp.where(row >= 1, pltpu.roll(xr_ref[k, pl.ds(base + seg - 1, SUBLANES, stride=seg), :], 1, 0), prev_r)
                    x0i = jnp.where(row >= 1, pltpu.roll(xi_ref[k, pl.ds(base + seg - 1, SUBLANES, stride=seg), :], 1, 0), prev_i)
                    l0r, l0i = lr[k, pl.ds(base, SUBLANES, stride=seg), :], li[k, pl.ds(base, SUBLANES, stride=seg), :]
                    init += [l0r * x0r + l0i * x0i, l0i * x0r - l0r * x0i]

                def step(r, acc, slabs=slabs, base=base):
                    out = []
                    for q, k in enumerate(slabs):
                        pr_, pi_ = xr_ref[k, pl.ds(base + r - 1, SUBLANES, stride=seg), :], xi_ref[k, pl.ds(base + r - 1, SUBLANES, stride=seg), :]
                        lr_, li_ = lr[k, pl.ds(base + r, SUBLANES, stride=seg), :], li[k, pl.ds(base + r, SUBLANES, stride=seg), :]
                        out += [acc[2 * q] + lr_ * pr_ + li_ * pi_, acc[2 * q + 1] + li_ * pr_ - lr_ * pi_]
                    return tuple(out)

                sums = lax.fori_loop(1, seg, step, tuple(init))
                for q, k in enumerate(slabs):
                    dar_ref[k] += sums[2 * q]
                    dai_ref[k] += sums[2 * q + 1]

        us = us_ref[...]
        dus = []
        for c in range(nch):
            lrc, lic = _slabs_to_cols(lr, c * spc, spc).astype(MXU_DTYPE), _slabs_to_cols(li, c * spc, spc).astype(MXU_DTYPE)
            uc = us[:, c * cch : (c + 1) * cch]
            _acc(dbbre_ref.at[c], _mm_tn(uc, lrc), first)
            _acc(dbbim_ref.at[c], _mm_tn(uc, lic), first)
            dus.append(_mm_nt(lrc, bbre_ref[c]) + _mm_nt(lic, bbim_ref[c]))
        du_s = jnp.concatenate(dus, axis=1) + dskip_ref[...] * dy

        dp = dp_ref[...]
        t1 = (tile * tm + 1 + lax.broadcasted_iota(jnp.int32, (tm, pg), 0)).astype(F32)
        dups, heads = [], []
        for g, w in enumerate(POOL_WINDOWS):
            dpg = dp[:, g * pg : (g + 1) * pg]
            qg = dpg / jnp.minimum(t1, float(w))
            ext = jnp.concatenate([qg, halo[:, g * pg : (g + 1) * pg]], axis=0)
            dups.append(_window_sum(ext, g + 1, False)[:tm, :] - dpg)
            heads.append(qg[:POOL_HALO, :])
        halo[...] = jnp.concatenate(heads, axis=1)
        dproj_ref[...] = jnp.concatenate([du_s] + dups, axis=1).astype(dproj_ref.dtype)

    rev = lambda cols: _row_spec(tm, cols, rev_n=nt)
    slab_spec = pl.BlockSpec((n_slabs, tm, LANES), lambda i: (0, nt - 1 - i, 0))
    prev_spec = pl.BlockSpec((n_slabs, SUBLANES, LANES), lambda i: (0, jnp.maximum((nt - 1 - i) * blocks_per_tile - 1, 0), 0))
    consts = (dskip, bbre, bbim, ccre, ccim, tabs, pows)
    return _pallas(
        body,
        name=name,
        grid=(nt,),
        out_shape=[jax.ShapeDtypeStruct((T, D), MXU_DTYPE), jax.ShapeDtypeStruct(ccre.shape, F32), jax.ShapeDtypeStruct(ccim.shape, F32), jax.ShapeDtypeStruct(bbre.shape, F32),
                   jax.ShapeDtypeStruct(bbim.shape, F32), jax.ShapeDtypeStruct((n_slabs, SUBLANES, LANES), F32), jax.ShapeDtypeStruct((n_slabs, SUBLANES, LANES), F32)],
        in_specs=[rev(W), rev(W), slab_spec, slab_spec, prev_spec, prev_spec, rev(W)] + [_full_spec(o.shape, single=True) for o in consts],
        out_specs=[rev(D), _full_spec(ccre.shape), _full_spec(ccim.shape), _full_spec(bbre.shape), _full_spec(bbim.shape),
                   _full_spec((n_slabs, SUBLANES, LANES)), _full_spec((n_slabs, SUBLANES, LANES))],
        scratch_shapes=[pltpu.VMEM((n_slabs, tm, LANES), F32), pltpu.VMEM((n_slabs, tm, LANES), F32), pltpu.VMEM((n_slabs, SUBLANES, LANES), F32), pltpu.VMEM((n_slabs, SUBLANES, LANES), F32),
                        pltpu.VMEM((POOL_HALO, W), F32)],
        operands=(dy, dpooled, xr, xi, xr, xi, proj, *consts),
        exchange=exchange,
    )


def _mix_bwd_in(dproj, h1, dh2, g_pre, win, name, exchange=None):
    T, D = h1.shape
    tm = _token_tile(T)

    def body(dproj_ref, h_ref, dh2_ref, gpre_ref, win_ref, dh1_ref, dg_ref):
        dx, dg = _rms_bwd(h_ref[...], gpre_ref[...], _mm_nt(dproj_ref[...], win_ref[...]))
        _acc(dg_ref, dg, pl.program_id(0) == 0)
        dh1_ref[...] = dh2_ref[...] + dx

    return _pallas(
        body,
        name=name,
        grid=(T // tm,),
        out_shape=[jax.ShapeDtypeStruct((T, D), F32), jax.ShapeDtypeStruct((1, D), F32)],
        in_specs=[_row_spec(tm, D), _row_spec(tm, D), _row_spec(tm, D), _full_spec((1, D)), _full_spec(win.shape)],
        out_specs=[_row_spec(tm, D), _full_spec((1, D))],
        operands=(dproj, h1, dh2, g_pre, win),
        exchange=exchange,
    )


def _discretize(lam_re, lam_im, log_dt, b_re, b_im):
    dt = jnp.exp(log_dt)[:, None]
    decay = jnp.exp(lam_re * dt)
    ang = lam_im * dt
    a_re, a_im = decay * jnp.cos(ang), decay * jnp.sin(ang)
    nr = a_re - 1.0
    den = lam_re * lam_re + lam_im * lam_im
    q_re = (nr * lam_re + a_im * lam_im) / den
    q_im = (a_im * lam_re - nr * lam_im) / den
    bb_re = q_re[..., None] * b_re - q_im[..., None] * b_im
    bb_im = q_re[..., None] * b_im + q_im[..., None] * b_re
    return a_re, a_im, bb_re, bb_im


GROUPS_PER_CHUNK = 16


def _block_diag(w, rows_first):
    G = w.shape[0]
    nch = G // GROUPS_PER_CHUNK
    if not rows_first:
        w = jnp.swapaxes(w, 1, 2)
    p, q = w.shape[1], w.shape[2]
    eye = jnp.eye(GROUPS_PER_CHUNK, dtype=w.dtype)
    out = jnp.einsum("cgpq,gk->cgpkq", w.reshape(nch, GROUPS_PER_CHUNK, p, q), eye)
    return out.reshape(nch, GROUPS_PER_CHUNK * p, GROUPS_PER_CHUNK * q)


def _block_diag_extract(m, p, q, rows_first):
    nch = m.shape[0]
    eye = jnp.eye(GROUPS_PER_CHUNK, dtype=m.dtype)
    out = jnp.einsum("cgpkq,gk->cgpq", m.reshape(nch, GROUPS_PER_CHUNK, p, GROUPS_PER_CHUNK, q), eye).reshape(nch * GROUPS_PER_CHUNK, p, q)
    return out if rows_first else jnp.swapaxes(out, 1, 2)


def _cmul(ar, ai, br, bi):
    return ar * br - ai * bi, ar * bi + ai * br


def _powers(ar, ai, count):
    pr, pi = ar[None], ai[None]
    while pr.shape[0] < count:
        nr, ni = _cmul(pr, pi, pr[-1][None], pi[-1][None])
        pr, pi = jnp.concatenate([pr, nr]), jnp.concatenate([pi, ni])
    return pr[:count], pi[:count]


def _scan_tables(a_re, a_im, seg):
    n = a_re.size
    ns = n // LANES
    ar, ai = a_re.reshape(n), a_im.reshape(n)
    pr, pi = _powers(ar, ai, seg)
    jr, ji = _powers(pr[-1], pi[-1], SUBLANES)

    def bcast(v):
        return jnp.broadcast_to(v.reshape(ns, 1, LANES), (ns, SUBLANES, LANES))

    def per_sublane(vs):
        return jnp.transpose(vs.reshape(SUBLANES, ns, LANES), (1, 0, 2))

    tabs = jnp.stack([bcast(ar), bcast(ai), bcast(jr[0]), bcast(ji[0]), bcast(jr[1]), bcast(ji[1]), bcast(jr[3]), bcast(ji[3]),
                      per_sublane(jr), per_sublane(ji), per_sublane(jr[::-1]), per_sublane(ji[::-1])])

    def rows(vs):
        return jnp.broadcast_to(jnp.transpose(vs.reshape(seg, ns, 1, LANES), (1, 0, 2, 3)), (ns, seg, SUBLANES, LANES))

    return tabs, jnp.stack([rows(pr), rows(pi)])


SMALL = ("ffn1_pre_norm", "ffn1_post_norm", "mix_pre_norm", "mix_post_norm", "ssm_lambda_re", "ssm_lambda_im", "ssm_log_dt", "ssm_b_re", "ssm_b_im", "ssm_c_re", "ssm_c_im",
         "ssm_d", "ssm_w_glu", "pool_w", "pool_scale", "ssm_out_norm", "pool_out_norm", "ffn2_pre_norm", "ffn2_post_norm")
BIG = ("ffn1_w_gate", "ffn1_w_up", "ffn1_w_down", "w_in", "w_out", "ffn2_w_gate", "ffn2_w_up", "ffn2_w_down")
ORDER = ("meta_tokens", "ffn1_pre_norm", "ffn1_post_norm", "ffn1_w_gate", "ffn1_w_up", "ffn1_w_down", "mix_pre_norm", "mix_post_norm", "w_in", "ssm_lambda_re", "ssm_lambda_im",
         "ssm_log_dt", "ssm_b_re", "ssm_b_im", "ssm_c_re", "ssm_c_im", "ssm_d", "ssm_w_glu", "pool_w", "pool_scale", "ssm_out_norm", "pool_out_norm", "w_out", "ffn2_pre_norm",
         "ffn2_post_norm", "ffn2_w_gate", "ffn2_w_up", "ffn2_w_down")
PACK_ROWS = SUBLANES * 8
def _pack(arrays, rows):
    flat = jnp.concatenate([a.reshape(-1) for a in arrays])
    return jnp.pad(flat, (0, rows * LANES - flat.size)).reshape(rows, LANES)


def _unpack(packed, shapes):
    flat = packed.reshape(-1)
    out, off = [], 0
    for s in shapes:
        n = math.prod(s)
        out.append(flat[off : off + n].reshape(s))
        off += n
    return out


def _step(p, x, loss_target, m, v):
    D = x.shape[-1]
    chip = (2 * lax.axis_index("x") + lax.axis_index("y")).astype(jnp.int32)
    place = jnp.stack([chip, lax.axis_index("c").astype(jnp.int32)])

    def gather_buffer(w):
        own = w.reshape(1, 2, w.shape[0] // 2, w.shape[1])
        return lax.dynamic_update_slice(lax.empty((4,) + own.shape[1:], own.dtype), own, (chip, 0, 0, 0))

    def rows_of(n, a):
        return jnp.swapaxes(a[0], 0, 1) if n.endswith(("w_gate", "w_up")) else a[0]

    def rows_back(n, a):
        return (jnp.swapaxes(a, 0, 1) if n.endswith(("w_gate", "w_up")) else a)[None]

    def grad_view(g):
        return g.reshape(4, 2, g.shape[0] // 8, g.shape[1])

    def reduce_sum(got_sibling, views, tag):
        sums = [_add_own_half(v_, g_, place, f"{tag}_add_sibling_{k}") for k, (v_, g_) in enumerate(zip(views, got_sibling))]
        return [s[0] for s in sums], [s[1] for s in sums]

    def reduce_halves(parts, got_chips, tag):
        return [_add_chips(p_, g_, place, f"{tag}_add_chips_{k}") for k, (p_, g_) in enumerate(zip(parts, got_chips))]

    bufs = {n: gather_buffer(rows_of(n, p[n]).astype(MXU_DTYPE)) for n in BIG}
    full = {}

    def gathered(names, got):
        full.update({n: g_.reshape(-1, g_.shape[-1]) for n, g_ in zip(names, got)})

    def gather_of(names, n_steps):
        return _Gather([bufs[n] for n in names], mid_step=(3 * n_steps) // 4)

    first_names = ("ffn1_w_gate", "ffn1_w_up")
    got = _exchange_call(_Gather([bufs[n] for n in first_names] + [gather_buffer(p["meta_tokens"])]), "gather_first")
    gathered(first_names, got)
    meta = jnp.transpose(got[-1].reshape(4, N_META, -1), (1, 0, 2)).reshape(N_META, D)

    vec = lambda n: p[n].reshape(1, -1)
    G, N, H = p["ssm_b_re"].shape[1:]
    a_re, a_im, bb_re, bb_im = _discretize(p["ssm_lambda_re"][0], p["ssm_lambda_im"][0], p["ssm_log_dt"][0], p["ssm_b_re"][0], p["ssm_b_im"][0])
    tabs, pows = _scan_tables(a_re, a_im, MIX_TILE // SUBLANES)
    bf = lambda a: a.astype(MXU_DTYPE)
    bbre, bbim = bf(_block_diag(bb_re, False)), bf(_block_diag(bb_im, False))
    ccre, ccim = bf(_block_diag(p["ssm_c_re"][0], False)), bf(_block_diag(p["ssm_c_im"][0], False))
    wgv, wgg = bf(_block_diag(p["ssm_w_glu"][0][:, :, :H], True)), bf(_block_diag(p["ssm_w_glu"][0][:, :, H:], True))
    pw = bf(p["pool_w"][0])

    T = x.shape[1] + N_META
    names = ("ffn1_w_down", "w_in", "w_out", "ffn2_w_gate")
    (ga1, si1, s1, n1), got = _ffn_gate_up(
        x[0], meta, vec("ffn1_pre_norm"), full["ffn1_w_gate"], full["ffn1_w_up"], "ffn1_gate_up", exchange=gather_of(names, T // FFN_TILE)
    )
    gathered(names, got)
    (h1, f1), got = _ffn_down(x[0], meta, s1, vec("ffn1_post_norm"), full["ffn1_w_down"], "ffn1_down", exchange=gather_of(("ffn2_w_up",), T // FFN_TILE))
    gathered(("ffn2_w_up",), got)
    (proj, xr, xim, y, pooled, mixed, h2, n2, cat), got = _mix_fwd(
        h1, vec("mix_pre_norm"), vec("ssm_out_norm"), vec("pool_out_norm"), vec("mix_post_norm"), vec("ssm_d"), vec("pool_scale"), full["w_in"], full["w_out"],
        bbre, bbim, ccre, ccim, wgv, wgg, pw, tabs, pows, "mix_fwd", exchange=gather_of(("ffn2_w_down",), T // (MIX_SUBTILES * MIX_TILE)),
    )
    gathered(("ffn2_w_down",), got)
    dh3, sq, f2, ga2, si2, s2, n3 = _ffn_fwd_loss(
        h2, loss_target[0], vec("ffn2_pre_norm"), vec("ffn2_post_norm"), full["ffn2_w_gate"], full["ffn2_w_up"], full["ffn2_w_down"], "ffn2_fwd"
    )
    loss = lax.psum(0.5 * jnp.sum(sq) / D, ("x", "y", "c"))

    g, shared = {}, {}
    ffn_names = lambda tag: (tag + "_w_gate", tag + "_w_up", tag + "_w_down")

    da, db, df, dh2, g["ffn2_post_norm"], g["ffn2_pre_norm"] = _ffn_bwd(
        dh3, f2, ga2, si2, h2, vec("ffn2_post_norm"), vec("ffn2_pre_norm"), full["ffn2_w_gate"], full["ffn2_w_up"], full["ffn2_w_down"], "ffn2_bwd"
    )
    views2 = [
        grad_view(_tn_matmul(da, n3, "ffn2_dw_gate")[0]),
        grad_view(_tn_matmul(db, n3, "ffn2_dw_up")[0]),
        grad_view(_tn_matmul(s2, df, "ffn2_dw_down")[0]),
    ]
    (dy, dpooled, dmixed, g["mix_post_norm"], g["ssm_out_norm"], g["pool_out_norm"], g["ssm_d"], g["pool_scale"], dwgv, dwgg, g["pool_w"]), got = _mix_bwd_heads(
        dh2, mixed, y, pooled, proj, vec("ssm_out_norm"), vec("pool_out_norm"), vec("mix_post_norm"), vec("pool_scale"), full["w_out"], wgv, wgg, pw, "mix_bwd_heads",
        exchange=_SiblingScatter(views2),
    )
    parts2, wire2 = reduce_sum(got, views2, "ffn2")
    (dproj, dccre, dccim, dbbre, dbbim, dar, dai), got = _mix_bwd_scan(
        dy, dpooled, xr, xim, proj, vec("ssm_d"), bbre, bbim, ccre, ccim, tabs, pows, "mix_bwd_scan", exchange=_ChipScatter(wire2)
    )
    halves2 = reduce_halves(parts2, got, "ffn2")
    (dh1, g["mix_pre_norm"]), got = _mix_bwd_in(dproj, h1, dh2, vec("mix_pre_norm"), full["w_in"], "mix_bwd_in", exchange=_SiblingShare(halves2))
    shared.update(zip(ffn_names("ffn2"), got))

    da, db, df, grad_x, d_meta, g["ffn1_post_norm"], g["ffn1_pre_norm"] = _ffn_bwd_first(
        dh1, f1, ga1, si1, x[0], meta, vec("ffn1_post_norm"), vec("ffn1_pre_norm"), full["ffn1_w_gate"], full["ffn1_w_up"], full["ffn1_w_down"], "ffn1_bwd"
    )
    grad_x = grad_x[None]
    dw_down, _ = _tn_matmul(s1, df, "ffn1_dw_down")
    views_d = [grad_view(dw_down)]
    dw_gate, got = _tn_matmul(da, n1, "ffn1_dw_gate", exchange=_SiblingScatter(views_d))
    parts_d, wire_d = reduce_sum(got, views_d, "ffn1_down")
    views_g = [grad_view(dw_gate)]
    ex = _Group([_ChipScatter(wire_d), _SiblingScatter(views_g)])
    dw_up, got = _tn_matmul(db, n1, "ffn1_dw_up", exchange=ex)
    got_d, got_g = ex.split(got)
    halves_d = reduce_halves(parts_d, got_d, "ffn1_down")
    parts_g, wire_g = reduce_sum(got_g, views_g, "ffn1_gate")
    views_u = [grad_view(dw_up)]
    ex = _Group([_SiblingShare(halves_d), _ChipScatter(wire_g), _SiblingScatter(views_u)])
    dw_in, got = _tn_matmul(n2, dproj, "dw_in", exchange=ex)
    got_d, got_g, got_u = ex.split(got)
    shared["ffn1_w_down"] = got_d[0]
    halves_g = reduce_halves(parts_g, got_g, "ffn1_gate")
    parts_u, wire_u = reduce_sum(got_u, views_u, "ffn1_up")
    views_i = [grad_view(dw_in)]
    ex = _Group([_SiblingShare(halves_g), _ChipScatter(wire_u), _SiblingScatter(views_i)])
    dw_out, got = _tn_matmul(cat, dmixed, "dw_out", exchange=ex)
    got_g, got_u, got_i = ex.split(got)
    shared["ffn1_w_gate"] = got_g[0]
    halves_u = reduce_halves(parts_u, got_u, "ffn1_up")
    parts_i, wire_i = reduce_sum(got_i, views_i, "w_in")

    g["ssm_c_re"] = _block_diag_extract(dccre, N, H, False)
    g["ssm_c_im"] = _block_diag_extract(dccim, N, H, False)
    g["ssm_w_glu"] = jnp.concatenate([_block_diag_extract(dwgv, H, H, True), _block_diag_extract(dwgg, H, H, True)], axis=-1)
    d_a_re, d_a_im = jnp.sum(dar, axis=1).reshape(G, N), jnp.sum(dai, axis=1).reshape(G, N)
    _, pull = jax.vjp(_discretize, p["ssm_lambda_re"][0], p["ssm_lambda_im"][0], p["ssm_log_dt"][0], p["ssm_b_re"][0], p["ssm_b_im"][0])
    g["ssm_lambda_re"], g["ssm_lambda_im"], g["ssm_log_dt"], g["ssm_b_re"], g["ssm_b_im"] = pull(
        (d_a_re, d_a_im, _block_diag_extract(dbbre, H, N, False), _block_diag_extract(dbbim, H, N, False))
    )

    small_shapes = [p[n].shape for n in SMALL] + [(N_META, D)]
    small_size = sum(math.prod(s) for s in small_shapes)
    rows = -(-small_size // (LANES * PACK_ROWS)) * PACK_ROWS
    views_t = [grad_view(dw_out), _pack([g[n] for n in SMALL] + [d_meta], rows).reshape(4, 2, rows // 8, LANES)]
    ex = _Group([_SiblingScatter(views_t), _SiblingShare(halves_u)])
    got_t, got_u = ex.split(_exchange_call(ex, "tail_reduce_sibling"))
    shared["ffn1_w_up"] = got_u[0]
    parts_t, wire_t = reduce_sum(got_t, views_t, "tail")
    got = _exchange_call(_ChipScatter(wire_i + wire_t), "tail_reduce_chips")
    got = _exchange_call(_SiblingShare(reduce_halves(parts_i + parts_t, got, "tail")), "tail_reduce_share")
    shared["w_in"], shared["w_out"] = got[0], got[1]
    small_buf = lax.dynamic_update_slice(lax.empty((4,) + got[2].shape, F32), got[2][None], (chip, 0, 0, 0))
    small_all = _exchange_call(_Gather([small_buf]), "gather_small")[0].reshape(rows, LANES)
    grads = dict(zip(SMALL + ("meta_full",), _unpack(small_all, small_shapes)))
    grads["meta_tokens"] = lax.dynamic_slice_in_dim(grads.pop("meta_full"), chip * (D // 4), D // 4, axis=1)
    delta, new_m, new_v = {}, {}, {}
    for n in BIG:
        g_rows = shared[n].reshape(-1, shared[n].shape[-1])
        outs = _adamw(rows_of(n, p[n]), g_rows, rows_of(n, m[n]), rows_of(n, v[n]), "adamw_" + n)
        grads[n], delta[n], new_m[n], new_v[n] = (rows_back(n, a) for a in (g_rows, *outs))
    delta["meta_tokens"], new_m["meta_tokens"], new_v["meta_tokens"] = _adamw(p["meta_tokens"], grads["meta_tokens"], m["meta_tokens"], v["meta_tokens"], "adamw_meta_tokens")

    def as_2d(n, a):
        a = a.reshape(p[n].shape)[0]
        if n in ("ssm_b_re", "ssm_b_im"):
            a = jnp.swapaxes(a, 1, 2)
        return a.reshape(-1, a.shape[-1])

    def from_2d(n, a):
        if n in ("ssm_b_re", "ssm_b_im"):
            g_, n_, h_ = p[n].shape[1:]
            return jnp.swapaxes(a.reshape(g_, h_, n_), 1, 2)[None]
        return a.reshape(p[n].shape)

    outs = _adamw_many(*[[as_2d(n, t[n]) for n in SMALL] for t in (p, grads, m, v)], "adamw_small")
    for out, arrays in zip((delta, new_m, new_v), outs):
        out.update({n: from_2d(n, a) for n, a in zip(SMALL, arrays)})

    return (loss, grad_x, *[grads[n] for n in ORDER], *[delta[n] for n in ORDER], *[new_m[n] for n in ORDER], *[new_v[n] for n in ORDER])


def kernel(x, meta_tokens, ffn1_pre_norm, ffn1_post_norm, ffn1_w_gate, ffn1_w_up, ffn1_w_down, mix_pre_norm, mix_post_norm, w_in, ssm_lambda_re, ssm_lambda_im, ssm_log_dt, ssm_b_re, ssm_b_im, ssm_c_re, ssm_c_im, ssm_d, ssm_w_glu, pool_w, pool_scale, ssm_out_norm, pool_out_norm, w_out, ffn2_pre_norm, ffn2_post_norm, ffn2_w_gate, ffn2_w_up, ffn2_w_down, loss_target, m_meta_tokens, m_ffn1_pre_norm, m_ffn1_post_norm, m_ffn1_w_gate, m_ffn1_w_up, m_ffn1_w_down, m_mix_pre_norm, m_mix_post_norm, m_w_in, m_ssm_lambda_re, m_ssm_lambda_im, m_ssm_log_dt, m_ssm_b_re, m_ssm_b_im, m_ssm_c_re, m_ssm_c_im, m_ssm_d, m_ssm_w_glu, m_pool_w, m_pool_scale, m_ssm_out_norm, m_pool_out_norm, m_w_out, m_ffn2_pre_norm, m_ffn2_post_norm, m_ffn2_w_gate, m_ffn2_w_up, m_ffn2_w_down, v_meta_tokens, v_ffn1_pre_norm, v_ffn1_post_norm, v_ffn1_w_gate, v_ffn1_w_up, v_ffn1_w_down, v_mix_pre_norm, v_mix_post_norm, v_w_in, v_ssm_lambda_re, v_ssm_lambda_im, v_ssm_log_dt, v_ssm_b_re, v_ssm_b_im, v_ssm_c_re, v_ssm_c_im, v_ssm_d, v_ssm_w_glu, v_pool_w, v_pool_scale, v_ssm_out_norm, v_pool_out_norm, v_w_out, v_ffn2_pre_norm, v_ffn2_post_norm, v_ffn2_w_gate, v_ffn2_w_up, v_ffn2_w_down):
    args = locals()
    p = {n: args[n] for n in ORDER}
    m = {n: args["m_" + n] for n in ORDER}
    v = {n: args["v_" + n] for n in ORDER}
    return _step(p, x, loss_target, m, v)
```

```python
import math

import jax
import jax.numpy as jnp
from jax import lax
from jax.experimental import pallas as pl
from jax.experimental.pallas import tpu as pltpu

F32 = jnp.float32
MXU_DTYPE = jnp.bfloat16
WIRE_DTYPE = jnp.bfloat16

RMS_EPS = 1e-6
N_META = 16
POOL_WINDOWS = (2, 4, 8, 16)
POOL_HALO = 16
ADAM_LR, ADAM_B1, ADAM_B2, ADAM_EPS, ADAM_WD, ADAM_STEP = 0.001, 0.9, 0.999, 1e-08, 0.01, 10

LANES = 128
SUBLANES = 8
VMEM_LIMIT = 60 * 1024 * 1024
FFN_TILE = 432
FFN_CHUNK = 1024
TN_TILE = 1024
MIX_TILE = 216
MIX_SUBTILES = 2
SLAB_GROUP = 8
MESH = pl.DeviceIdType.MESH
ANY = pl.BlockSpec(memory_space=pl.ANY)


def _mm(a, b):
    return jnp.dot(a.astype(MXU_DTYPE), b.astype(MXU_DTYPE), preferred_element_type=F32)


def _mm_nt(a, b):
    return lax.dot_general(a.astype(MXU_DTYPE), b.astype(MXU_DTYPE), (((1,), (1,)), ((), ())), preferred_element_type=F32)


def _mm_tn(a, b):
    return lax.dot_general(a.astype(MXU_DTYPE), b.astype(MXU_DTYPE), (((0,), (0,)), ((), ())), preferred_element_type=F32)


def _rms_stat(x):
    return lax.rsqrt(jnp.mean(x * x, axis=-1, keepdims=True) + RMS_EPS)


def _rms_bwd(x, g, dy):
    r = _rms_stat(x)
    xh = x * r
    dg = jnp.sum(dy * xh, axis=0, keepdims=True)
    dxh = dy * g
    dx = r * (dxh - xh * jnp.mean(dxh * xh, axis=-1, keepdims=True))
    return dx, dg


def _sigmoid(x):
    return 1.0 / (1.0 + jnp.exp(-x))


GELU_C = math.sqrt(2.0 / math.pi)
GELU_K = 0.044715


def _gelu(y):
    return 0.5 * y * (1.0 + jnp.tanh(GELU_C * (y + GELU_K * y * y * y)))


def _gelu_grad(y):
    th = jnp.tanh(GELU_C * (y + GELU_K * y * y * y))
    return 0.5 * (1.0 + th) + 0.5 * y * (1.0 - th * th) * GELU_C * (1.0 + 3.0 * GELU_K * y * y)


def _row_spec(tile, cols, rev_n=None):
    if rev_n is None:
        return pl.BlockSpec((tile, cols), lambda i: (i, 0))
    return pl.BlockSpec((tile, cols), lambda i: (rev_n - 1 - i, 0))


def _full_spec(shape, single=False):
    zeros = (0,) * len(shape)
    if single:
        return pl.BlockSpec(shape, lambda *_: zeros, pipeline_mode=pl.Buffered(1))
    return pl.BlockSpec(shape, lambda *_: zeros)


def _acc(ref, val, first):
    @pl.when(first)
    def _():
        ref[...] = val

    @pl.when(jnp.logical_not(first))
    def _():
        ref[...] += val


def _place():
    x, y, c = lax.axis_index("x"), lax.axis_index("y"), lax.axis_index("c")
    others = [(1 - x, y), (x, 1 - y), (1 - x, 1 - y)]
    return x, y, c, others


class _Exchange:
    mid_step = None

    def __init__(self, ins, out_shapes, aliases, n_sems):
        self.ins, self.out_shapes, self.aliases, self.n_sems = list(ins), list(out_shapes), dict(aliases), n_sems

    def mid(self, ins, outs, send_sems, recv_sems):
        pass


class _SiblingScatter(_Exchange):
    def __init__(self, views):
        super().__init__(views, [jax.ShapeDtypeStruct((4,) + v.shape[2:], v.dtype) for v in views], {}, 4 * len(views))

    def _copies(self, ins, outs, send_sems, recv_sems):
        x, y, c, _ = _place()
        return [
            pltpu.make_async_remote_copy(src_ref=ins[a].at[k, 1 - c], dst_ref=outs[a].at[k], send_sem=send_sems.at[4 * a + k], recv_sem=recv_sems.at[4 * a + k], device_id=(x, y, 1 - c), device_id_type=MESH)
            for a in range(len(ins))
            for k in range(4)
        ]

    def start(self, *refs):
        for cp in self._copies(*refs):
            cp.start()

    def finish(self, *refs):
        cps = self._copies(*refs)
        for cp in cps:
            cp.wait_recv()
        for cp in cps:
            cp.wait_send()


class _ChipScatter(_Exchange):
    def __init__(self, parts):
        super().__init__(parts, [jax.ShapeDtypeStruct((3,) + p.shape[1:], p.dtype) for p in parts], {}, 3 * len(parts))

    def _copies(self, ins, outs, send_sems, recv_sems):
        x, y, c, others = _place()
        return [
            pltpu.make_async_remote_copy(src_ref=ins[a].at[2 * chip[0] + chip[1]], dst_ref=outs[a].at[j], send_sem=send_sems.at[3 * a + j], recv_sem=recv_sems.at[3 * a + j], device_id=(*chip, c), device_id_type=MESH)
            for a in range(len(ins))
            for j, chip in enumerate(others)
        ]

    start = _SiblingScatter.start
    finish = _SiblingScatter.finish


class _SiblingShare(_Exchange):
    def __init__(self, bufs):
        super().__init__(bufs, [jax.ShapeDtypeStruct(b.shape, b.dtype) for b in bufs], {a: a for a in range(len(bufs))}, len(bufs))

    def _copy(self, outs, send_sems, recv_sems, a, half):
        x, y, c, _ = _place()
        mine = outs[a].at[c if half == "mine" else 1 - c]
        return pltpu.make_async_remote_copy(src_ref=mine, dst_ref=mine, send_sem=send_sems.at[a], recv_sem=recv_sems.at[a], device_id=(x, y, 1 - c), device_id_type=MESH)

    def start(self, ins, outs, send_sems, recv_sems):
        for a in range(len(outs)):
            self._copy(outs, send_sems, recv_sems, a, "mine").start()

    def finish(self, ins, outs, send_sems, recv_sems):
        for a in range(len(outs)):
            self._copy(outs, send_sems, recv_sems, a, "theirs").wait_recv()
        for a in range(len(outs)):
            self._copy(outs, send_sems, recv_sems, a, "mine").wait_send()


class _Gather(_Exchange):
    def __init__(self, bufs, mid_step=None):
        super().__init__(bufs, [jax.ShapeDtypeStruct(b.shape, b.dtype) for b in bufs], {a: a for a in range(len(bufs))}, 6 * len(bufs))
        self.mid_step = mid_step

    def _copy(self, outs, send_sems, recv_sems, a, j, chip, half, to):
        blk = outs[a].at[2 * chip[0] + chip[1], half]
        return pltpu.make_async_remote_copy(src_ref=blk, dst_ref=blk, send_sem=send_sems.at[6 * a + j], recv_sem=recv_sems.at[6 * a + j], device_id=to, device_id_type=MESH)

    def start(self, ins, outs, send_sems, recv_sems):
        x, y, c, others = _place()
        for a in range(len(outs)):
            for j, chip in enumerate(others):
                self._copy(outs, send_sems, recv_sems, a, j, (x, y), c, (*chip, c)).start()

    def mid(self, ins, outs, send_sems, recv_sems):
        x, y, c, others = _place()
        for a in range(len(outs)):
            for j, chip in enumerate(others):
                self._copy(outs, send_sems, recv_sems, a, j, chip, c, (x, y, c)).wait_recv()
                self._copy(outs, send_sems, recv_sems, a, 3 + j, chip, c, (x, y, 1 - c)).start()

    def finish(self, ins, outs, send_sems, recv_sems):
        x, y, c, others = _place()
        for a in range(len(outs)):
            for j, chip in enumerate(others):
                self._copy(outs, send_sems, recv_sems, a, 3 + j, chip, 1 - c, (x, y, c)).wait_recv()
        for a in range(len(outs)):
            for j, chip in enumerate(others):
                self._copy(outs, send_sems, recv_sems, a, j, (x, y), c, (*chip, c)).wait_send()
                self._copy(outs, send_sems, recv_sems, a, 3 + j, chip, c, (x, y, 1 - c)).wait_send()


class _SemSlice:
    def __init__(self, sems, off):
        self.sems, self.off = sems, off

    @property
    def at(self):
        return self

    def __getitem__(self, i):
        return self.sems.at[self.off + i]


class _Group(_Exchange):
    def __init__(self, exchanges):
        ins, outs, aliases, n_sems, self.spans = [], [], {}, 0, []
        for ex in exchanges:
            self.spans.append((len(ins), len(outs), n_sems))
            aliases.update({len(ins) + i: len(outs) + o for i, o in ex.aliases.items()})
            ins, outs, n_sems = ins + ex.ins, outs + ex.out_shapes, n_sems + ex.n_sems
        super().__init__(ins, outs, aliases, n_sems)
        self.exchanges = exchanges
        mids = {ex.mid_step for ex in exchanges if ex.mid_step is not None}
        self.mid_step = mids.pop() if mids else None

    def _each(self, method, ins, outs, send_sems, recv_sems):
        for ex, (i0, o0, s0) in zip(self.exchanges, self.spans):
            getattr(ex, method)(ins[i0 : i0 + len(ex.ins)], outs[o0 : o0 + len(ex.out_shapes)], _SemSlice(send_sems, s0), _SemSlice(recv_sems, s0))

    def start(self, *refs):
        self._each("start", *refs)

    def mid(self, *refs):
        self._each("mid", *refs)

    def finish(self, *refs):
        self._each("finish", *refs)

    def split(self, outs):
        return [outs[o0 : o0 + len(ex.out_shapes)] for ex, (_, o0, _) in zip(self.exchanges, self.spans)]


def _exchange_call(ex, name):
    n, m = len(ex.ins), len(ex.out_shapes)

    def body(*refs):
        parts = (refs[:n], refs[n : n + m], refs[n + m], refs[n + m + 1])
        ex.start(*parts)
        ex.mid(*parts)
        ex.finish(*parts)

    return pl.pallas_call(
        body,
        name=name,
        out_shape=ex.out_shapes,
        in_specs=[ANY] * n,
        out_specs=[ANY] * m,
        scratch_shapes=[pltpu.SemaphoreType.DMA((ex.n_sems,)), pltpu.SemaphoreType.DMA((ex.n_sems,))],
        input_output_aliases=ex.aliases,
    )(*ex.ins)


def _pallas(body, *, name, grid, in_specs, out_specs, out_shape, operands, scratch_shapes=(), exchange=None):
    params = pltpu.CompilerParams(dimension_semantics=("arbitrary",) * len(grid), vmem_limit_bytes=VMEM_LIMIT)
    if exchange is None:
        outs = pl.pallas_call(body, name=name, grid=grid, in_specs=in_specs, out_specs=out_specs, out_shape=out_shape, scratch_shapes=list(scratch_shapes), compiler_params=params)(*operands)
        return outs, []
    ex = exchange
    n_in, n_out, n_scr = len(in_specs), len(out_specs), len(scratch_shapes)
    x_in, x_out = len(ex.ins), len(ex.out_shapes)

    def hosted(*refs):
        ins, x_ins = refs[:n_in], refs[n_in : n_in + x_in]
        outs, x_outs = refs[n_in + x_in : n_in + x_in + n_out], refs[n_in + x_in + n_out : n_in + x_in + n_out + x_out]
        rest = refs[n_in + x_in + n_out + x_out :]
        parts = (x_ins, x_outs, rest[n_scr], rest[n_scr + 1])
        ids = [pl.program_id(d) for d in range(len(grid))]
        first = _all([i == 0 for i in ids])
        last = _all([i == g - 1 for i, g in zip(ids, grid)])

        @pl.when(first)
        def _():
            ex.start(*parts)

        body(*ins, *outs, *rest[:n_scr])

        if ex.mid_step is not None:

            @pl.when(ids[0] == ex.mid_step)
            def _():
                ex.mid(*parts)

        @pl.when(last)
        def _():
            ex.finish(*parts)

    outs = pl.pallas_call(
        hosted,
        name=name,
        grid=grid,
        in_specs=list(in_specs) + [ANY] * x_in,
        out_specs=list(out_specs) + [ANY] * x_out,
        out_shape=list(out_shape) + ex.out_shapes,
        scratch_shapes=list(scratch_shapes) + [pltpu.SemaphoreType.DMA((ex.n_sems,)), pltpu.SemaphoreType.DMA((ex.n_sems,))],
        input_output_aliases={n_in + i: n_out + o for i, o in ex.aliases.items()},
        compiler_params=params,
    )(*operands, *ex.ins)
    return outs[:n_out], outs[n_out:]


def _all(conds):
    out = conds[0]
    for c in conds[1:]:
        out = jnp.logical_and(out, c)
    return out


def _row_tile(rows):
    if rows <= 512:
        return rows
    for t in (512, 352, 256, 176, 128, 112, 64, 32, 16, 8):
        if rows % t == 0:
            return t
    return rows


def _add_own_half(view, got, place, name, wire=WIRE_DTYPE):
    _, _, r, c = view.shape
    tr = _row_tile(r)

    def body(place_ref, v_ref, g_ref, o_ref, w_ref):
        s = v_ref[...] + g_ref[...]
        w_ref[...] = s.astype(w_ref.dtype)

        @pl.when(pl.program_id(1) == place_ref[0])
        def _():
            o_ref[...] = s

    blk = pl.BlockSpec((None, tr, c), lambda i, k, pr: (k, i, 0))
    return pl.pallas_call(
        body,
        name=name,
        out_shape=[jax.ShapeDtypeStruct((r, c), F32), jax.ShapeDtypeStruct((4, r, c), wire)],
        grid_spec=pltpu.PrefetchScalarGridSpec(
            num_scalar_prefetch=1,
            grid=(r // tr, 4),
            in_specs=[pl.BlockSpec((None, None, tr, c), lambda i, k, pr: (k, pr[1], i, 0)), blk],
            out_specs=[pl.BlockSpec((tr, c), lambda i, k, pr: (i, 0)), blk],
        ),
        compiler_params=pltpu.CompilerParams(dimension_semantics=("arbitrary", "arbitrary"), vmem_limit_bytes=VMEM_LIMIT),
    )(place, view, got)


def _add_chips(part, got, place, name):
    r, c = part.shape
    tr = _row_tile(r)

    def body(place_ref, p_ref, g_ref, o_ref):
        o_ref[...] = ((p_ref[...] + g_ref[0].astype(F32)) + g_ref[1].astype(F32)) + g_ref[2].astype(F32)

    return pl.pallas_call(
        body,
        name=name,
        out_shape=jax.ShapeDtypeStruct((2, r, c), F32),
        grid_spec=pltpu.PrefetchScalarGridSpec(
            num_scalar_prefetch=1,
            grid=(r // tr,),
            in_specs=[pl.BlockSpec((tr, c), lambda i, pr: (i, 0)), pl.BlockSpec((3, tr, c), lambda i, pr: (0, i, 0))],
            out_specs=pl.BlockSpec((None, tr, c), lambda i, pr: (pr[1], i, 0)),
        ),
        compiler_params=pltpu.CompilerParams(dimension_semantics=("arbitrary",), vmem_limit_bytes=VMEM_LIMIT),
    )(place, part, got)


def _adamw_update(w_ref, g_ref, m_ref, v_ref, d_ref, nm_ref, nv_ref):
    g = g_ref[...]
    nm = ADAM_B1 * m_ref[...] + (1.0 - ADAM_B1) * g
    nv = ADAM_B2 * v_ref[...] + (1.0 - ADAM_B2) * (g * g)
    m_hat = nm / (1.0 - ADAM_B1**ADAM_STEP)
    v_hat = nv / (1.0 - ADAM_B2**ADAM_STEP)
    d_ref[...] = -ADAM_LR * (m_hat / (jnp.sqrt(v_hat) + ADAM_EPS) + ADAM_WD * w_ref[...])
    nm_ref[...] = nm
    nv_ref[...] = nv


def _adamw(w, g, m, v, name):
    r, c = w.shape
    tr = _row_tile(r)
    spec = pl.BlockSpec((tr, c), lambda i: (i, 0))
    outs, _ = _pallas(_adamw_update, name=name, grid=(r // tr,), in_specs=[spec] * 4, out_specs=[spec] * 3, out_shape=[jax.ShapeDtypeStruct((r, c), F32)] * 3, operands=(w, g, m, v))
    return outs


def _adamw_many(ws, gs, ms, vs, name):
    n = len(ws)

    def body(*refs):
        for k in range(n):
            _adamw_update(*(refs[j * n + k] for j in range(7)))

    outs = pl.pallas_call(
        body,
        name=name,
        out_shape=[jax.ShapeDtypeStruct(w.shape, F32) for w in ws] * 3,
        in_specs=[pl.BlockSpec(memory_space=pltpu.VMEM)] * (4 * n),
        out_specs=[pl.BlockSpec(memory_space=pltpu.VMEM)] * (3 * n),
    )(*ws, *gs, *ms, *vs)
    return outs[:n], outs[n : 2 * n], outs[2 * n :]


def _load_weights(pairs, sems):
    @pl.when(pl.program_id(0) == 0)
    def _():
        cps = [pltpu.make_async_copy(src, dst, sems.at[k]) for k, (src, dst) in enumerate(pairs)]
        for cp in cps:
            cp.start()
        for cp in cps:
            cp.wait()


def _ffn_chunks(F):
    bounds = list(range(0, F, FFN_CHUNK)) + [F]
    return list(zip(bounds[:-1], bounds[1:]))


def _shifted_specs(tm, cols):
    per = tm // N_META
    return [_row_spec(tm, cols), pl.BlockSpec((N_META, cols), lambda i: (jnp.maximum(i * per - 1, 0), 0))]


def _shifted_tile(cur_ref, before_ref, tm):
    return jnp.concatenate([before_ref[...], cur_ref[0 : tm - N_META, :]], axis=0)


def _tokens_tile(cur_ref, before_ref, meta_ref, tm, tile_0):
    first = jnp.where(tile_0, meta_ref[...], before_ref[...])
    return jnp.concatenate([first, cur_ref[0 : tm - N_META, :]], axis=0)


def _ffn_fwd_loss(h, target, g_pre, g_post, wg, wu, wd, name):
    T, D = h.shape
    F = wg.shape[0]
    tm = FFN_TILE

    def body(h_ref, t_ref, tb_ref, gpre_ref, gpost_ref, wg_hbm, wu_hbm, wd_hbm, dy_ref, sq_ref, f_ref, ga_ref, si_ref, s_ref, n_ref, wg_v, wu_v, wd_v, sems):
        i = pl.program_id(0)
        _load_weights([(wg_hbm, wg_v), (wu_hbm, wu_v), (wd_hbm, wd_v)], sems)
        hh = h_ref[...]
        n = (hh * _rms_stat(hh) * gpre_ref[...]).astype(MXU_DTYPE)
        n_ref[...] = n.astype(n_ref.dtype)
        f = jnp.zeros((tm, D), F32)
        for lo, hi in _ffn_chunks(F):
            a = _mm_nt(n, wg_v[lo:hi, :])
            b = _mm_nt(n, wu_v[lo:hi, :])
            sg = _sigmoid(a)
            si = a * sg
            s = (si * b).astype(MXU_DTYPE)
            ga_ref[:, lo:hi] = (b * (sg * (1.0 + a * (1.0 - sg)))).astype(ga_ref.dtype)
            si_ref[:, lo:hi] = si.astype(si_ref.dtype)
            s_ref[:, lo:hi] = s.astype(s_ref.dtype)
            f = f + _mm(s, wd_v[lo:hi, :])
        f_ref[...] = f
        y = hh + 0.5 * (f * _rms_stat(f) * gpost_ref[...])
        rows = i * tm + lax.broadcasted_iota(jnp.int32, (tm, D), 0)
        err = jnp.where(rows >= N_META, y - _shifted_tile(t_ref, tb_ref, tm), 0.0)
        dy_ref[...] = err * (1.0 / D)
        _acc(sq_ref, jnp.sum(err * err, axis=0, keepdims=True), i == 0)

    tok = jax.ShapeDtypeStruct((T, D), F32)
    act = jax.ShapeDtypeStruct((T, F), MXU_DTYPE)
    outs, _ = _pallas(
        body,
        name=name,
        grid=(T // tm,),
        out_shape=[tok, jax.ShapeDtypeStruct((1, D), F32), tok, act, act, act, jax.ShapeDtypeStruct((T, D), MXU_DTYPE)],
        in_specs=[_row_spec(tm, D)] + _shifted_specs(tm, D) + [_full_spec((1, D)), _full_spec((1, D)), ANY, ANY, ANY],
        out_specs=[_row_spec(tm, D), _full_spec((1, D)), _row_spec(tm, D), _row_spec(tm, F), _row_spec(tm, F), _row_spec(tm, F), _row_spec(tm, D)],
        scratch_shapes=[pltpu.VMEM(wg.shape, wg.dtype), pltpu.VMEM(wu.shape, wu.dtype), pltpu.VMEM(wd.shape, wd.dtype), pltpu.SemaphoreType.DMA((3,))],
        operands=(h, target, target, g_pre, g_post, wg, wu, wd),
    )
    return outs


def _ffn_gate_up(x, meta, g_pre, wg, wu, name, exchange=None):
    D = x.shape[1]
    T = x.shape[0] + N_META
    F = wg.shape[0]
    tm = FFN_TILE

    def body(x_ref, xb_ref, meta_ref, gpre_ref, wg_hbm, wu_hbm, ga_ref, si_ref, s_ref, n_ref, wg_v, wu_v, sems):
        _load_weights([(wg_hbm, wg_v), (wu_hbm, wu_v)], sems)
        hh = _tokens_tile(x_ref, xb_ref, meta_ref, tm, pl.program_id(0) == 0)
        n = (hh * _rms_stat(hh) * gpre_ref[...]).astype(MXU_DTYPE)
        n_ref[...] = n.astype(n_ref.dtype)
        for lo, hi in _ffn_chunks(F):
            a = _mm_nt(n, wg_v[lo:hi, :])
            b = _mm_nt(n, wu_v[lo:hi, :])
            sg = _sigmoid(a)
            si = a * sg
            ga_ref[:, lo:hi] = (b * (sg * (1.0 + a * (1.0 - sg)))).astype(ga_ref.dtype)
            si_ref[:, lo:hi] = si.astype(si_ref.dtype)
            s_ref[:, lo:hi] = (si * b).astype(s_ref.dtype)

    act = jax.ShapeDtypeStruct((T, F), MXU_DTYPE)
    return _pallas(
        body,
        name=name,
        grid=(T // tm,),
        out_shape=[act, act, act, jax.ShapeDtypeStruct((T, D), MXU_DTYPE)],
        in_specs=_shifted_specs(tm, D) + [_full_spec((N_META, D)), _full_spec((1, D)), ANY, ANY],
        out_specs=[_row_spec(tm, F), _row_spec(tm, F), _row_spec(tm, F), _row_spec(tm, D)],
        scratch_shapes=[pltpu.VMEM(wg.shape, wg.dtype), pltpu.VMEM(wu.shape, wu.dtype), pltpu.SemaphoreType.DMA((2,))],
        operands=(x, x, meta, g_pre, wg, wu),
        exchange=exchange,
    )


def _ffn_down(x, meta, s, g_post, wd, name, exchange=None):
    D = x.shape[1]
    T = x.shape[0] + N_META
    F = wd.shape[0]
    tm = FFN_TILE

    def body(x_ref, xb_ref, meta_ref, s_ref, gpost_ref, wd_hbm, hout_ref, f_ref, wd_v, sems):
        _load_weights([(wd_hbm, wd_v)], sems)
        f = _mm(s_ref[...], wd_v[...])
        f_ref[...] = f
        hout_ref[...] = _tokens_tile(x_ref, xb_ref, meta_ref, tm, pl.program_id(0) == 0) + 0.5 * (f * _rms_stat(f) * gpost_ref[...])

    tok = jax.ShapeDtypeStruct((T, D), F32)
    return _pallas(
        body,
        name=name,
        grid=(T // tm,),
        out_shape=[tok, tok],
        in_specs=_shifted_specs(tm, D) + [_full_spec((N_META, D)), _row_spec(tm, F), _full_spec((1, D)), ANY],
        out_specs=[_row_spec(tm, D), _row_spec(tm, D)],
        scratch_shapes=[pltpu.VMEM(wd.shape, wd.dtype), pltpu.SemaphoreType.DMA((1,))],
        operands=(x, x, meta, s, g_post, wd),
        exchange=exchange,
    )


def _ffn_bwd(dh, f, ga, si, h, g_post, g_pre, wg, wu, wd, name):
    T, D = dh.shape
    F = wd.shape[0]
    tm = FFN_TILE

    def body(dh_ref, f_ref, ga_ref, si_ref, h_ref, gpost_ref, gpre_ref, wg_hbm, wu_hbm, wd_hbm, da_ref, db_ref, df_ref, dhin_ref, dgpost_ref, dgpre_ref, wg_v, wu_v, wd_v, sems):
        first = pl.program_id(0) == 0
        _load_weights([(wg_hbm, wg_v), (wu_hbm, wu_v), (wd_hbm, wd_v)], sems)
        dh = dh_ref[...]
        df, dg = _rms_bwd(f_ref[...], gpost_ref[...], 0.5 * dh)
        _acc(dgpost_ref, dg, first)
        dfb = df.astype(MXU_DTYPE)
        df_ref[...] = dfb.astype(df_ref.dtype)
        dn = jnp.zeros((tm, D), F32)
        for lo, hi in _ffn_chunks(F):
            ds = _mm_nt(dfb, wd_v[lo:hi, :])
            da = (ds * ga_ref[:, lo:hi].astype(F32)).astype(MXU_DTYPE)
            db = (ds * si_ref[:, lo:hi].astype(F32)).astype(MXU_DTYPE)
            da_ref[:, lo:hi] = da.astype(da_ref.dtype)
            db_ref[:, lo:hi] = db.astype(db_ref.dtype)
            dn = dn + _mm(da, wg_v[lo:hi, :]) + _mm(db, wu_v[lo:hi, :])
        dx, dg = _rms_bwd(h_ref[...], gpre_ref[...], dn)
        _acc(dgpre_ref, dg, first)
        dhin_ref[...] = dh + dx

    act = jax.ShapeDtypeStruct((T, F), MXU_DTYPE)
    vec = jax.ShapeDtypeStruct((1, D), F32)
    outs, _ = _pallas(
        body,
        name=name,
        grid=(T // tm,),
        out_shape=[act, act, jax.ShapeDtypeStruct((T, D), MXU_DTYPE), jax.ShapeDtypeStruct((T, D), F32), vec, vec],
        in_specs=[_row_spec(tm, D), _row_spec(tm, D), _row_spec(tm, F), _row_spec(tm, F), _row_spec(tm, D), _full_spec((1, D)), _full_spec((1, D)), ANY, ANY, ANY],
        out_specs=[_row_spec(tm, F), _row_spec(tm, F), _row_spec(tm, D), _row_spec(tm, D), _full_spec((1, D)), _full_spec((1, D))],
        scratch_shapes=[pltpu.VMEM(wg.shape, wg.dtype), pltpu.VMEM(wu.shape, wu.dtype), pltpu.VMEM(wd.shape, wd.dtype), pltpu.SemaphoreType.DMA((3,))],
        operands=(dh, f, ga, si, h, g_post, g_pre, wg, wu, wd),
    )
    return outs


def _ffn_bwd_first(dh, f, ga, si, x, meta, g_post, g_pre, wg, wu, wd, name):
    D = x.shape[1]
    T = x.shape[0] + N_META
    F = wd.shape[0]
    tm = FFN_TILE
    nt = T // tm
    per = tm // N_META
    tile = lambda i: jnp.minimum(i, nt - 1)

    def body(dh_ref, f_ref, ga_ref, si_ref, x_ref, xb_ref, meta_ref, gpost_ref, gpre_ref, wg_hbm, wu_hbm, wd_hbm,
             da_ref, db_ref, df_ref, dx_ref, dmeta_ref, dgpost_ref, dgpre_ref, wg_v, wu_v, wd_v, sems, held):
        i = pl.program_id(0)
        _load_weights([(wg_hbm, wg_v), (wu_hbm, wu_v), (wd_hbm, wd_v)], sems)

        @pl.when(i < nt)
        def _():
            dh = dh_ref[...]
            df, dg = _rms_bwd(f_ref[...], gpost_ref[...], 0.5 * dh)
            _acc(dgpost_ref, dg, i == 0)
            dfb = df.astype(MXU_DTYPE)
            df_ref[...] = dfb.astype(df_ref.dtype)
            dn = jnp.zeros((tm, D), F32)
            for lo, hi in _ffn_chunks(F):
                ds = _mm_nt(dfb, wd_v[lo:hi, :])
                da = (ds * ga_ref[:, lo:hi].astype(F32)).astype(MXU_DTYPE)
                db = (ds * si_ref[:, lo:hi].astype(F32)).astype(MXU_DTYPE)
                da_ref[:, lo:hi] = da.astype(da_ref.dtype)
                db_ref[:, lo:hi] = db.astype(db_ref.dtype)
                dn = dn + _mm(da, wg_v[lo:hi, :]) + _mm(db, wu_v[lo:hi, :])
            dx, dg = _rms_bwd(_tokens_tile(x_ref, xb_ref, meta_ref, tm, i == 0), gpre_ref[...], dn)
            _acc(dgpre_ref, dg, i == 0)
            dh_in = dh + dx

            @pl.when(i == 0)
            def _():
                dmeta_ref[...] = dh_in[0:N_META, :]

            @pl.when(i > 0)
            def _():
                dx_ref[...] = jnp.concatenate([held[...], dh_in[0:N_META, :]], axis=0)

            held[...] = dh_in[N_META:, :]

        @pl.when(i == nt)
        def _():
            dx_ref[0 : tm - N_META, :] = held[...]

    rows = lambda cols: pl.BlockSpec((tm, cols), lambda i: (tile(i), 0))
    act = jax.ShapeDtypeStruct((T, F), MXU_DTYPE)
    vec = jax.ShapeDtypeStruct((1, D), F32)
    outs, _ = _pallas(
        body,
        name=name,
        grid=(nt + 1,),
        out_shape=[act, act, jax.ShapeDtypeStruct((T, D), MXU_DTYPE), jax.ShapeDtypeStruct((T - N_META, D), F32), jax.ShapeDtypeStruct((N_META, D), F32), vec, vec],
        in_specs=[rows(D), rows(D), rows(F), rows(F), rows(D), pl.BlockSpec((N_META, D), lambda i: (jnp.maximum(tile(i) * per - 1, 0), 0)), _full_spec((N_META, D)),
                  _full_spec((1, D)), _full_spec((1, D)), ANY, ANY, ANY],
        out_specs=[rows(F), rows(F), rows(D), pl.BlockSpec((tm, D), lambda i: (jnp.maximum(i - 1, 0), 0)), _full_spec((N_META, D)), _full_spec((1, D)), _full_spec((1, D))],
        scratch_shapes=[pltpu.VMEM(wg.shape, wg.dtype), pltpu.VMEM(wu.shape, wu.dtype), pltpu.VMEM(wd.shape, wd.dtype), pltpu.SemaphoreType.DMA((3,)), pltpu.VMEM((tm - N_META, D), F32)],
        operands=(dh, f, ga, si, x, x, meta, g_post, g_pre, wg, wu, wd),
    )
    return outs


def _token_tile(T):
    for t in (912, 864, 432):
        if T % t == 0:
            return t
    raise ValueError(f"no token tile for {T} rows")


def _tn_matmul(xm, ym, name, exchange=None):
    T, M = xm.shape
    N = ym.shape[1]
    if (T - N_META) % TN_TILE:
        tk = _token_tile(T)

        def body(x_ref, y_ref, o_ref):
            _acc(o_ref, _mm_tn(x_ref[...], y_ref[...]), pl.program_id(0) == 0)

        grid, operands = (T // tk,), (xm, ym)
        in_specs = [pl.BlockSpec((tk, M), lambda k: (k, 0)), pl.BlockSpec((tk, N), lambda k: (k, 0))]
    else:
        tk = TN_TILE

        def body(x_ref, y_ref, xh_ref, yh_ref, o_ref):
            prod = _mm_tn(x_ref[...], y_ref[...])

            @pl.when(pl.program_id(0) == 0)
            def _():
                o_ref[...] = prod + _mm_tn(xh_ref[...], yh_ref[...])

            @pl.when(pl.program_id(0) > 0)
            def _():
                o_ref[...] += prod

        grid, operands = ((T - N_META) // tk,), (xm, ym, xm, ym)
        start = lambda k: (pl.multiple_of(N_META + k * tk, N_META), 0)
        in_specs = [pl.BlockSpec((pl.Element(tk), pl.Element(M)), start), pl.BlockSpec((pl.Element(tk), pl.Element(N)), start),
                    pl.BlockSpec((N_META, M), lambda k: (0, 0)), pl.BlockSpec((N_META, N), lambda k: (0, 0))]

    (out,), x_outs = _pallas(
        body,
        name=name,
        grid=grid,
        out_shape=[jax.ShapeDtypeStruct((M, N), F32)],
        in_specs=in_specs,
        out_specs=[_full_spec((M, N))],
        operands=operands,
        exchange=exchange,
    )
    return out, x_outs


TAB_A, TAB_AS1, TAB_AS2, TAB_AS4, TAB_JF, TAB_JB = 0, 2, 4, 6, 8, 10


def _scan_inplace(zr, zi, tabs, pows, car_r, car_i, seg, reverse, base=0):
    n_slabs = zr.shape[0]
    sgn = -1.0 if reverse else 1.0
    row = lax.broadcasted_iota(jnp.int32, (SUBLANES, LANES), 0)

    def cmul(pr, pi, xr, xi):
        return pr * xr - pi * xi, pr * xi + pi * xr

    for k0 in range(0, n_slabs, SLAB_GROUP):
        slabs = range(k0, min(k0 + SLAB_GROUP, n_slabs))
        ar = [tabs[TAB_A, k] for k in slabs]
        ai = [sgn * tabs[TAB_A + 1, k] for k in slabs]

        def first_pass(t, carry):
            r = (seg - 1 - t) if reverse else t
            out = []
            for q, k in enumerate(slabs):
                xr, xi = carry[2 * q], carry[2 * q + 1]
                pr, pi = cmul(ar[q], ai[q], xr, xi)
                nr = pr + zr[k, pl.ds(base + r, SUBLANES, stride=seg), :]
                ni = pi + zi[k, pl.ds(base + r, SUBLANES, stride=seg), :]
                zr[k, pl.ds(base + r, SUBLANES, stride=seg), :] = nr
                zi[k, pl.ds(base + r, SUBLANES, stride=seg), :] = ni
                out += [nr, ni]
            return tuple(out)

        ends = lax.fori_loop(0, seg, first_pass, tuple(jnp.zeros((SUBLANES, LANES), F32) for _ in range(2 * len(slabs))))

        incoming = []
        for q, k in enumerate(slabs):
            fr, fi = ends[2 * q], ends[2 * q + 1]
            for d, tab in ((1, TAB_AS1), (2, TAB_AS2), (4, TAB_AS4)):
                shift, keep = (SUBLANES - d, row < SUBLANES - d) if reverse else (d, row >= d)
                sr = jnp.where(keep, pltpu.roll(fr, shift, 0), 0.0)
                si = jnp.where(keep, pltpu.roll(fi, shift, 0), 0.0)
                pr, pi = cmul(tabs[tab, k], sgn * tabs[tab + 1, k], sr, si)
                fr, fi = fr + pr, fi + pi
            cr, ci = car_r[k], car_i[k]
            jtab = TAB_JB if reverse else TAB_JF
            pr, pi = cmul(tabs[jtab, k], sgn * tabs[jtab + 1, k], cr, ci)
            er, ei = fr + pr, fi + pi
            if reverse:
                inr = jnp.where(row < SUBLANES - 1, pltpu.roll(er, SUBLANES - 1, 0), cr)
                ini = jnp.where(row < SUBLANES - 1, pltpu.roll(ei, SUBLANES - 1, 0), ci)
                car_r[k] = jnp.broadcast_to(er[0:1, :], (SUBLANES, LANES))
                car_i[k] = jnp.broadcast_to(ei[0:1, :], (SUBLANES, LANES))
            else:
                inr = jnp.where(row >= 1, pltpu.roll(er, 1, 0), cr)
                ini = jnp.where(row >= 1, pltpu.roll(ei, 1, 0), ci)
                car_r[k] = jnp.broadcast_to(er[SUBLANES - 1 : SUBLANES, :], (SUBLANES, LANES))
                car_i[k] = jnp.broadcast_to(ei[SUBLANES - 1 : SUBLANES, :], (SUBLANES, LANES))
            incoming += [inr, ini]

        def second_pass(r, _):
            p = (seg - 1 - r) if reverse else r
            for q, k in enumerate(slabs):
                pr, pi = cmul(pows[0, k, p], sgn * pows[1, k, p], incoming[2 * q], incoming[2 * q + 1])
                zr[k, pl.ds(base + r, SUBLANES, stride=seg), :] = zr[k, pl.ds(base + r, SUBLANES, stride=seg), :] + pr
                zi[k, pl.ds(base + r, SUBLANES, stride=seg), :] = zi[k, pl.ds(base + r, SUBLANES, stride=seg), :] + pi
            return 0

        lax.fori_loop(0, seg, second_pass, 0)


def _slabs_to_cols(ref, k0, n):
    return jnp.concatenate([ref[k0 + q] for q in range(n)], axis=1)


def _window_sum(ext, doublings, forward):
    rows = ext.shape[0]
    s = ext
    for k in range(doublings):
        s = s + pltpu.roll(s, (1 << k) if forward else rows - (1 << k), 0)
    return s


def _mix_fwd(h1, g_pre, g_so, g_po, g_post, dskip, pscale, win, wout, bbre, bbim, ccre, ccim, wgv, wgg, pw, tabs, pows, name, exchange=None):
    T, D = h1.shape
    W = D // 2
    tm = MIX_SUBTILES * MIX_TILE
    seg = MIX_TILE // SUBLANES
    n_slabs = tabs.shape[1]
    nch, cch, sch = bbre.shape
    spc = sch // LANES
    pg = W // len(POOL_WINDOWS)

    def body(h_ref, gpre_ref, gso_ref, gpo_ref, gpost_ref, dskip_ref, pscale_ref, win_ref, wout_ref, bbre_ref, bbim_ref, ccre_ref, ccim_ref, wgv_ref, wgg_ref, pw_ref, tabs_ref, pows_ref,
             proj_ref, xr_ref, xi_ref, y_ref, pooled_ref, mixed_ref, h2_ref, n2_ref, cat_ref, car_r, car_i, halo):
        i = pl.program_id(0)

        @pl.when(i == 0)
        def _():
            car_r[...] = jnp.zeros_like(car_r)
            car_i[...] = jnp.zeros_like(car_i)
            halo[...] = jnp.zeros_like(halo)

        hh = h_ref[...]
        n2 = (hh * _rms_stat(hh) * gpre_ref[...]).astype(MXU_DTYPE)
        n2_ref[...] = n2.astype(n2_ref.dtype)
        proj = _mm(n2, win_ref[...])
        proj_ref[...] = proj
        us, up = proj[:, :W], proj[:, W:]

        for c in range(nch):
            uc = us[:, c * cch : (c + 1) * cch].astype(MXU_DTYPE)
            bur, bui = _mm(uc, bbre_ref[c]), _mm(uc, bbim_ref[c])
            for q in range(spc):
                xr_ref[c * spc + q] = bur[:, q * LANES : (q + 1) * LANES]
                xi_ref[c * spc + q] = bui[:, q * LANES : (q + 1) * LANES]
        for sub in range(MIX_SUBTILES):
            _scan_inplace(xr_ref, xi_ref, tabs_ref, pows_ref, car_r, car_i, seg, reverse=False, base=sub * MIX_TILE)
        ys = []
        for c in range(nch):
            ys.append(_mm(_slabs_to_cols(xr_ref, c * spc, spc), ccre_ref[c]) - _mm(_slabs_to_cols(xi_ref, c * spc, spc), ccim_ref[c]))
        y = jnp.concatenate(ys, axis=1) + dskip_ref[...] * us
        y_ref[...] = y
        ge = _gelu(y).astype(MXU_DTYPE)
        zv = jnp.concatenate([_mm(ge[:, c * cch : (c + 1) * cch], wgv_ref[c]) for c in range(nch)], axis=1)
        zg = jnp.concatenate([_mm(ge[:, c * cch : (c + 1) * cch], wgg_ref[c]) for c in range(nch)], axis=1)
        out = zv * _sigmoid(zg)
        cat_s = out * _rms_stat(out) * gso_ref[...]

        ext = jnp.concatenate([halo[...], up], axis=0)
        halo[...] = up[tm - POOL_HALO :, :]
        t1 = (i * tm + 1 + lax.broadcasted_iota(jnp.int32, (tm, pg), 0)).astype(F32)
        pooled, pms = [], []
        for g, w in enumerate(POOL_WINDOWS):
            col = ext[:, g * pg : (g + 1) * pg]
            win_sum = _window_sum(col, g + 1, True)[POOL_HALO:, :]
            pooled_g = win_sum / jnp.minimum(t1, float(w)) - up[:, g * pg : (g + 1) * pg]
            pooled.append(pooled_g)
            pms.append(_mm(pooled_g, pw_ref[g]))
        pooled_ref[...] = jnp.concatenate(pooled, axis=1)
        yp = jnp.concatenate(pms, axis=1) * pscale_ref[...]
        cat_p = yp * _rms_stat(yp) * gpo_ref[...]

        cat = jnp.concatenate([cat_s, cat_p], axis=1).astype(MXU_DTYPE)
        cat_ref[...] = cat.astype(cat_ref.dtype)
        mixed = _mm(cat, wout_ref[...])
        mixed_ref[...] = mixed
        h2_ref[...] = hh + mixed * _rms_stat(mixed) * gpost_ref[...]

    tok = lambda cols, dt=F32: jax.ShapeDtypeStruct((T, cols), dt)
    slab_spec = pl.BlockSpec((n_slabs, tm, LANES), lambda i: (0, i, 0))
    operands = (h1, g_pre, g_so, g_po, g_post, dskip, pscale, win, wout, bbre, bbim, ccre, ccim, wgv, wgg, pw, tabs, pows)
    return _pallas(
        body,
        name=name,
        grid=(T // tm,),
        out_shape=[tok(D), jax.ShapeDtypeStruct((n_slabs, T, LANES), F32), jax.ShapeDtypeStruct((n_slabs, T, LANES), F32), tok(W), tok(W), tok(D), tok(D), tok(D, MXU_DTYPE), tok(D, MXU_DTYPE)],
        in_specs=[_row_spec(tm, D)] + [_full_spec(o.shape, single=True) for o in operands[1:]],
        out_specs=[_row_spec(tm, D), slab_spec, slab_spec, _row_spec(tm, W), _row_spec(tm, W), _row_spec(tm, D), _row_spec(tm, D), _row_spec(tm, D), _row_spec(tm, D)],
        scratch_shapes=[pltpu.VMEM((n_slabs, SUBLANES, LANES), F32), pltpu.VMEM((n_slabs, SUBLANES, LANES), F32), pltpu.VMEM((POOL_HALO, W), F32)],
        operands=operands,
        exchange=exchange,
    )


def _mix_bwd_heads(dh2, mixed, y, pooled, proj, g_so, g_po, g_post, pscale, wout, wgv, wgg, pw, name, exchange=None):
    T, D = dh2.shape
    W = D // 2
    tm = _token_tile(T)
    nch, cch, _ = wgv.shape
    ng, pg, _ = pw.shape

    def body(dh2_ref, mixed_ref, y_ref, pooled_ref, us_ref, gso_ref, gpo_ref, gpost_ref, pscale_ref, wout_ref, wgv_ref, wgg_ref, pw_ref,
             dy_ref, dpooled_ref, dmixed_ref, dgpost_ref, dgso_ref, dgpo_ref, dd_ref, dscale_ref, dwgv_ref, dwgg_ref, dpw_ref):
        first = pl.program_id(0) == 0
        dmixed, dgpost = _rms_bwd(mixed_ref[...], gpost_ref[...], dh2_ref[...])
        _acc(dgpost_ref, dgpost, first)
        dmb = dmixed.astype(MXU_DTYPE)
        dmixed_ref[...] = dmb.astype(dmixed_ref.dtype)
        dcat = _mm_nt(dmb, wout_ref[...])
        dcs, dcp = dcat[:, :W], dcat[:, W:]

        y = y_ref[...]
        ge = _gelu(y).astype(MXU_DTYPE)
        zv = jnp.concatenate([_mm(ge[:, c * cch : (c + 1) * cch], wgv_ref[c]) for c in range(nch)], axis=1)
        zg = jnp.concatenate([_mm(ge[:, c * cch : (c + 1) * cch], wgg_ref[c]) for c in range(nch)], axis=1)
        sg = _sigmoid(zg)
        dout, dgso = _rms_bwd(zv * sg, gso_ref[...], dcs)
        _acc(dgso_ref, dgso, first)
        dzv = (dout * sg).astype(MXU_DTYPE)
        dzg = (dout * zv * sg * (1.0 - sg)).astype(MXU_DTYPE)
        dges = []
        for c in range(nch):
            cs = slice(c * cch, (c + 1) * cch)
            dges.append(_mm_nt(dzv[:, cs], wgv_ref[c]) + _mm_nt(dzg[:, cs], wgg_ref[c]))
            _acc(dwgv_ref.at[c], _mm_tn(ge[:, cs], dzv[:, cs]), first)
            _acc(dwgg_ref.at[c], _mm_tn(ge[:, cs], dzg[:, cs]), first)
        dy = jnp.concatenate(dges, axis=1) * _gelu_grad(y)
        dy_ref[...] = dy
        _acc(dd_ref, jnp.sum(dy * us_ref[...], axis=0, keepdims=True), first)

        pooled_b = pooled_ref[...].astype(MXU_DTYPE)
        pm = jnp.concatenate([_mm(pooled_b[:, g * pg : (g + 1) * pg], pw_ref[g]) for g in range(ng)], axis=1)
        dyp, dgpo = _rms_bwd(pm * pscale_ref[...], gpo_ref[...], dcp)
        _acc(dgpo_ref, dgpo, first)
        _acc(dscale_ref, jnp.sum(dyp * pm, axis=0, keepdims=True), first)
        dpm = (dyp * pscale_ref[...]).astype(MXU_DTYPE)
        dps = []
        for g in range(ng):
            gs = slice(g * pg, (g + 1) * pg)
            dps.append(_mm_nt(dpm[:, gs], pw_ref[g]))
            _acc(dpw_ref.at[g], _mm_tn(pooled_b[:, gs], dpm[:, gs]), first)
        dpooled_ref[...] = jnp.concatenate(dps, axis=1)

    vec = lambda n: jax.ShapeDtypeStruct((1, n), F32)
    operands = (dh2, mixed, y, pooled, proj, g_so, g_po, g_post, pscale, wout, wgv, wgg, pw)
    return _pallas(
        body,
        name=name,
        grid=(T // tm,),
        out_shape=[jax.ShapeDtypeStruct((T, W), F32), jax.ShapeDtypeStruct((T, W), F32), jax.ShapeDtypeStruct((T, D), MXU_DTYPE), vec(D), vec(W), vec(W), vec(W), vec(W),
                   jax.ShapeDtypeStruct(wgv.shape, F32), jax.ShapeDtypeStruct(wgg.shape, F32), jax.ShapeDtypeStruct(pw.shape, F32)],
        in_specs=[_row_spec(tm, D), _row_spec(tm, D), _row_spec(tm, W), _row_spec(tm, W), _row_spec(tm, W)] + [_full_spec(o.shape) for o in operands[5:]],
        out_specs=[_row_spec(tm, W), _row_spec(tm, W), _row_spec(tm, D), _full_spec((1, D)), _full_spec((1, W)), _full_spec((1, W)), _full_spec((1, W)), _full_spec((1, W)),
                   _full_spec(wgv.shape), _full_spec(wgg.shape), _full_spec(pw.shape)],
        operands=operands,
        exchange=exchange,
    )


def _mix_bwd_scan(dy, dpooled, xr, xi, proj, dskip, bbre, bbim, ccre, ccim, tabs, pows, name, exchange=None):
    T, W = dy.shape
    D = 2 * W
    tm = MIX_SUBTILES * MIX_TILE
    seg = MIX_TILE // SUBLANES
    nt = T // tm
    n_slabs = tabs.shape[1]
    nch, cch, sch = bbre.shape
    spc = sch // LANES
    pg = W // len(POOL_WINDOWS)
    blocks_per_tile = tm // SUBLANES

    def body(dy_ref, dp_ref, xr_ref, xi_ref, xpr_ref, xpi_ref, us_ref, dskip_ref, bbre_ref, bbim_ref, ccre_ref, ccim_ref, tabs_ref, pows_ref,
             dproj_ref, dccre_ref, dccim_ref, dbbre_ref, dbbim_ref, dar_ref, dai_ref, lr, li, car_r, car_i, halo):
        i = pl.program_id(0)
        first = i == 0
        tile = nt - 1 - i
        row = lax.broadcasted_iota(jnp.int32, (SUBLANES, LANES), 0)

        @pl.when(first)
        def _():
            car_r[...] = jnp.zeros_like(car_r)
            car_i[...] = jnp.zeros_like(car_i)
            halo[...] = jnp.zeros_like(halo)
            dar_ref[...] = jnp.zeros_like(dar_ref)
            dai_ref[...] = jnp.zeros_like(dai_ref)

        dy = dy_ref[...]
        for c in range(nch):
            dyc = dy[:, c * cch : (c + 1) * cch]
            gr, gi = _mm_nt(dyc, ccre_ref[c]), _mm_nt(dyc, ccim_ref[c])
            for q in range(spc):
                lr[c * spc + q] = gr[:, q * LANES : (q + 1) * LANES]
                li[c * spc + q] = -gi[:, q * LANES : (q + 1) * LANES]
            _acc(dccre_ref.at[c], _mm_tn(_slabs_to_cols(xr_ref, c * spc, spc), dyc), first)
            _acc(dccim_ref.at[c], -_mm_tn(_slabs_to_cols(xi_ref, c * spc, spc), dyc), first)
        for sub in reversed(range(MIX_SUBTILES)):
            _scan_inplace(lr, li, tabs_ref, pows_ref, car_r, car_i, seg, reverse=True, base=sub * MIX_TILE)

        for sub in range(MIX_SUBTILES):
            base = sub * MIX_TILE
            for k0 in range(0, n_slabs, SLAB_GROUP):
                slabs = range(k0, min(k0 + SLAB_GROUP, n_slabs))
                init = []
                for k in slabs:
                    if sub == 0:
                        prev_r = jnp.where(tile > 0, jnp.broadcast_to(xpr_ref[k, SUBLANES - 1 : SUBLANES, :], (SUBLANES, LANES)), 0.0)
                        prev_i = jnp.where(tile > 0, jnp.broadcast_to(xpi_ref[k, SUBLANES - 1 : SUBLANES, :], (SUBLANES, LANES)), 0.0)
                    else:
                        prev_r = jnp.broadcast_to(xr_ref[k, base - 1 : base, :], (SUBLANES, LANES))
                        prev_i = jnp.broadcast_to(xi_ref[k, base - 1 : base, :], (SUBLANES, LANES))
                    x0r = jnp.where(row >= 1, pltpu.roll(xr_ref[k, pl.ds(base + seg - 1, SUBLANES, stride=seg), :], 1, 0), prev_r)
                    x0i = jnp.where(row >= 1, pltpu.roll(xi_ref[k, pl.ds(base + seg - 1, SUBLANES, stride=seg), :], 1, 0), prev_i)
                    l0r, l0i = lr[k, pl.ds(base, SUBLANES, stride=seg), :], li[k, pl.ds(base, SUBLANES, stride=seg), :]
                    init += [l0r * x0r + l0i * x0i, l0i * x0r - l0r * x0i]

                def step(r, acc, slabs=slabs, base=base):
                    out = []
                    for q, k in enumerate(slabs):
                        pr_, pi_ = xr_ref[k, pl.ds(base + r - 1, SUBLANES, stride=seg), :], xi_ref[k, pl.ds(base + r - 1, SUBLANES, stride=seg), :]
                        lr_, li_ = lr[k, pl.ds(base + r, SUBLANES, stride=seg), :], li[k, pl.ds(base + r, SUBLANES, stride=seg), :]
                        out += [acc[2 * q] + lr_ * pr_ + li_ * pi_, acc[2 * q + 1] + li_ * pr_ - lr_ * pi_]
                    return tuple(out)

                sums = lax.fori_loop(1, seg, step, tuple(init))
                for q, k in enumerate(slabs):
                    dar_ref[k] += sums[2 * q]
                    dai_ref[k] += sums[2 * q + 1]

        us = us_ref[...]
        dus = []
        for c in range(nch):
            lrc, lic = _slabs_to_cols(lr, c * spc, spc).astype(MXU_DTYPE), _slabs_to_cols(li, c * spc, spc).astype(MXU_DTYPE)
            uc = us[:, c * cch : (c + 1) * cch]
            _acc(dbbre_ref.at[c], _mm_tn(uc, lrc), first)
            _acc(dbbim_ref.at[c], _mm_tn(uc, lic), first)
            dus.append(_mm_nt(lrc, bbre_ref[c]) + _mm_nt(lic, bbim_ref[c]))
        du_s = jnp.concatenate(dus, axis=1) + dskip_ref[...] * dy

        dp = dp_ref[...]
        t1 = (tile * tm + 1 + lax.broadcasted_iota(jnp.int32, (tm, pg), 0)).astype(F32)
        dups, heads = [], []
        for g, w in enumerate(POOL_WINDOWS):
            dpg = dp[:, g * pg : (g + 1) * pg]
            qg = dpg / jnp.minimum(t1, float(w))
            ext = jnp.concatenate([qg, halo[:, g * pg : (g + 1) * pg]], axis=0)
            dups.append(_window_sum(ext, g + 1, False)[:tm, :] - dpg)
            heads.append(qg[:POOL_HALO, :])
        halo[...] = jnp.concatenate(heads, axis=1)
        dproj_ref[...] = jnp.concatenate([du_s] + dups, axis=1).astype(dproj_ref.dtype)

    rev = lambda cols: _row_spec(tm, cols, rev_n=nt)
    slab_spec = pl.BlockSpec((n_slabs, tm, LANES), lambda i: (0, nt - 1 - i, 0))
    prev_spec = pl.BlockSpec((n_slabs, SUBLANES, LANES), lambda i: (0, jnp.maximum((nt - 1 - i) * blocks_per_tile - 1, 0), 0))
    consts = (dskip, bbre, bbim, ccre, ccim, tabs, pows)
    return _pallas(
        body,
        name=name,
        grid=(nt,),
        out_shape=[jax.ShapeDtypeStruct((T, D), MXU_DTYPE), jax.ShapeDtypeStruct(ccre.shape, F32), jax.ShapeDtypeStruct(ccim.shape, F32), jax.ShapeDtypeStruct(bbre.shape, F32),
                   jax.ShapeDtypeStruct(bbim.shape, F32), jax.ShapeDtypeStruct((n_slabs, SUBLANES, LANES), F32), jax.ShapeDtypeStruct((n_slabs, SUBLANES, LANES), F32)],
        in_specs=[rev(W), rev(W), slab_spec, slab_spec, prev_spec, prev_spec, rev(W)] + [_full_spec(o.shape, single=True) for o in consts],
        out_specs=[rev(D), _full_spec(ccre.shape), _full_spec(ccim.shape), _full_spec(bbre.shape), _full_spec(bbim.shape),
                   _full_spec((n_slabs, SUBLANES, LANES)), _full_spec((n_slabs, SUBLANES, LANES))],
        scratch_shapes=[pltpu.VMEM((n_slabs, tm, LANES), F32), pltpu.VMEM((n_slabs, tm, LANES), F32), pltpu.VMEM((n_slabs, SUBLANES, LANES), F32), pltpu.VMEM((n_slabs, SUBLANES, LANES), F32),
                        pltpu.VMEM((POOL_HALO, W), F32)],
        operands=(dy, dpooled, xr, xi, xr, xi, proj, *consts),
        exchange=exchange,
    )


def _mix_bwd_in(dproj, h1, dh2, g_pre, win, name, exchange=None):
    T, D = h1.shape
    tm = _token_tile(T)

    def body(dproj_ref, h_ref, dh2_ref, gpre_ref, win_ref, dh1_ref, dg_ref):
        dx, dg = _rms_bwd(h_ref[...], gpre_ref[...], _mm_nt(dproj_ref[...], win_ref[...]))
        _acc(dg_ref, dg, pl.program_id(0) == 0)
        dh1_ref[...] = dh2_ref[...] + dx

    return _pallas(
        body,
        name=name,
        grid=(T // tm,),
        out_shape=[jax.ShapeDtypeStruct((T, D), F32), jax.ShapeDtypeStruct((1, D), F32)],
        in_specs=[_row_spec(tm, D), _row_spec(tm, D), _row_spec(tm, D), _full_spec((1, D)), _full_spec(win.shape)],
        out_specs=[_row_spec(tm, D), _full_spec((1, D))],
        operands=(dproj, h1, dh2, g_pre, win),
        exchange=exchange,
    )


def _discretize(lam_re, lam_im, log_dt, b_re, b_im):
    dt = jnp.exp(log_dt)[:, None]
    decay = jnp.exp(lam_re * dt)
    ang = lam_im * dt
    a_re, a_im = decay * jnp.cos(ang), decay * jnp.sin(ang)
    nr = a_re - 1.0
    den = lam_re * lam_re + lam_im * lam_im
    q_re = (nr * lam_re + a_im * lam_im) / den
    q_im = (a_im * lam_re - nr * lam_im) / den
    bb_re = q_re[..., None] * b_re - q_im[..., None] * b_im
    bb_im = q_re[..., None] * b_im + q_im[..., None] * b_re
    return a_re, a_im, bb_re, bb_im


GROUPS_PER_CHUNK = 16


def _block_diag(w, rows_first):
    G = w.shape[0]
    nch = G // GROUPS_PER_CHUNK
    if not rows_first:
        w = jnp.swapaxes(w, 1, 2)
    p, q = w.shape[1], w.shape[2]
    eye = jnp.eye(GROUPS_PER_CHUNK, dtype=w.dtype)
    out = jnp.einsum("cgpq,gk->cgpkq", w.reshape(nch, GROUPS_PER_CHUNK, p, q), eye)
    return out.reshape(nch, GROUPS_PER_CHUNK * p, GROUPS_PER_CHUNK * q)


def _block_diag_extract(m, p, q, rows_first):
    nch = m.shape[0]
    eye = jnp.eye(GROUPS_PER_CHUNK, dtype=m.dtype)
    out = jnp.einsum("cgpkq,gk->cgpq", m.reshape(nch, GROUPS_PER_CHUNK, p, GROUPS_PER_CHUNK, q), eye).reshape(nch * GROUPS_PER_CHUNK, p, q)
    return out if rows_first else jnp.swapaxes(out, 1, 2)


def _cmul(ar, ai, br, bi):
    return ar * br - ai * bi, ar * bi + ai * br


def _powers(ar, ai, count):
    pr, pi = ar[None], ai[None]
    while pr.shape[0] < count:
        nr, ni = _cmul(pr, pi, pr[-1][None], pi[-1][None])
        pr, pi = jnp.concatenate([pr, nr]), jnp.concatenate([pi, ni])
    return pr[:count], pi[:count]


def _scan_tables(a_re, a_im, seg):
    n = a_re.size
    ns = n // LANES
    ar, ai = a_re.reshape(n), a_im.reshape(n)
    pr, pi = _powers(ar, ai, seg)
    jr, ji = _powers(pr[-1], pi[-1], SUBLANES)

    def bcast(v):
        return jnp.broadcast_to(v.reshape(ns, 1, LANES), (ns, SUBLANES, LANES))

    def per_sublane(vs):
        return jnp.transpose(vs.reshape(SUBLANES, ns, LANES), (1, 0, 2))

    tabs = jnp.stack([bcast(ar), bcast(ai), bcast(jr[0]), bcast(ji[0]), bcast(jr[1]), bcast(ji[1]), bcast(jr[3]), bcast(ji[3]),
                      per_sublane(jr), per_sublane(ji), per_sublane(jr[::-1]), per_sublane(ji[::-1])])

    def rows(vs):
        return jnp.broadcast_to(jnp.transpose(vs.reshape(seg, ns, 1, LANES), (1, 0, 2, 3)), (ns, seg, SUBLANES, LANES))

    return tabs, jnp.stack([rows(pr), rows(pi)])


SMALL = ("ffn1_pre_norm", "ffn1_post_norm", "mix_pre_norm", "mix_post_norm", "ssm_lambda_re", "ssm_lambda_im", "ssm_log_dt", "ssm_b_re", "ssm_b_im", "ssm_c_re", "ssm_c_im",
         "ssm_d", "ssm_w_glu", "pool_w", "pool_scale", "ssm_out_norm", "pool_out_norm", "ffn2_pre_norm", "ffn2_post_norm")
BIG = ("ffn1_w_gate", "ffn1_w_up", "ffn1_w_down", "w_in", "w_out", "ffn2_w_gate", "ffn2_w_up", "ffn2_w_down")
ORDER = ("meta_tokens", "ffn1_pre_norm", "ffn1_post_norm", "ffn1_w_gate", "ffn1_w_up", "ffn1_w_down", "mix_pre_norm", "mix_post_norm", "w_in", "ssm_lambda_re", "ssm_lambda_im",
         "ssm_log_dt", "ssm_b_re", "ssm_b_im", "ssm_c_re", "ssm_c_im", "ssm_d", "ssm_w_glu", "pool_w", "pool_scale", "ssm_out_norm", "pool_out_norm", "w_out", "ffn2_pre_norm",
         "ffn2_post_norm", "ffn2_w_gate", "ffn2_w_up", "ffn2_w_down")
PACK_ROWS = SUBLANES * 8
def _pack(arrays, rows):
    flat = jnp.concatenate([a.reshape(-1) for a in arrays])
    return jnp.pad(flat, (0, rows * LANES - flat.size)).reshape(rows, LANES)


def _unpack(packed, shapes):
    flat = packed.reshape(-1)
    out, off = [], 0
    for s in shapes:
        n = math.prod(s)
        out.append(flat[off : off + n].reshape(s))
        off += n
    return out


def _step(p, x, loss_target, m, v):
    D = x.shape[-1]
    chip = (2 * lax.axis_index("x") + lax.axis_index("y")).astype(jnp.int32)
    place = jnp.stack([chip, lax.axis_index("c").astype(jnp.int32)])

    def gather_buffer(w):
        own = w.reshape(1, 2, w.shape[0] // 2, w.shape[1])
        return lax.dynamic_update_slice(lax.empty((4,) + own.shape[1:], own.dtype), own, (chip, 0, 0, 0))

    def rows_of(n, a):
        return jnp.swapaxes(a[0], 0, 1) if n.endswith(("w_gate", "w_up")) else a[0]

    def rows_back(n, a):
        return (jnp.swapaxes(a, 0, 1) if n.endswith(("w_gate", "w_up")) else a)[None]

    def grad_view(g):
        return g.reshape(4, 2, g.shape[0] // 8, g.shape[1])

    def reduce_sum(got_sibling, views, tag, wires=None):
        wires = wires or [WIRE_DTYPE] * len(views)
        sums = [_add_own_half(v_, g_, place, f"{tag}_add_sibling_{k}", w_) for k, (v_, g_, w_) in enumerate(zip(views, got_sibling, wires))]
        return [s[0] for s in sums], [s[1] for s in sums]

    def reduce_halves(parts, got_chips, tag):
        return [_add_chips(p_, g_, place, f"{tag}_add_chips_{k}") for k, (p_, g_) in enumerate(zip(parts, got_chips))]

    bufs = {n: gather_buffer(rows_of(n, p[n]).astype(MXU_DTYPE)) for n in BIG}
    full = {}

    def gathered(names, got):
        full.update({n: g_.reshape(-1, g_.shape[-1]) for n, g_ in zip(names, got)})

    def gather_of(names, n_steps):
        return _Gather([bufs[n] for n in names], mid_step=(3 * n_steps) // 4)

    first_names = ("ffn1_w_gate", "ffn1_w_up")
    got = _exchange_call(_Gather([bufs[n] for n in first_names] + [gather_buffer(p["meta_tokens"])]), "gather_first")
    gathered(first_names, got)
    meta = jnp.transpose(got[-1].reshape(4, N_META, -1), (1, 0, 2)).reshape(N_META, D)

    vec = lambda n: p[n].reshape(1, -1)
    G, N, H = p["ssm_b_re"].shape[1:]
    a_re, a_im, bb_re, bb_im = _discretize(p["ssm_lambda_re"][0], p["ssm_lambda_im"][0], p["ssm_log_dt"][0], p["ssm_b_re"][0], p["ssm_b_im"][0])
    tabs, pows = _scan_tables(a_re, a_im, MIX_TILE // SUBLANES)
    bf = lambda a: a.astype(MXU_DTYPE)
    bbre, bbim = bf(_block_diag(bb_re, False)), bf(_block_diag(bb_im, False))
    ccre, ccim = bf(_block_diag(p["ssm_c_re"][0], False)), bf(_block_diag(p["ssm_c_im"][0], False))
    wgv, wgg = bf(_block_diag(p["ssm_w_glu"][0][:, :, :H], True)), bf(_block_diag(p["ssm_w_glu"][0][:, :, H:], True))
    pw = bf(p["pool_w"][0])

    T = x.shape[1] + N_META
    names = ("ffn1_w_down", "w_in", "w_out", "ffn2_w_gate", "ffn2_w_down")
    (ga1, si1, s1, n1), got = _ffn_gate_up(
        x[0], meta, vec("ffn1_pre_norm"), full["ffn1_w_gate"], full["ffn1_w_up"], "ffn1_gate_up", exchange=gather_of(names, T // FFN_TILE)
    )
    gathered(names, got)
    (h1, f1), got = _ffn_down(x[0], meta, s1, vec("ffn1_post_norm"), full["ffn1_w_down"], "ffn1_down", exchange=gather_of(("ffn2_w_up",), T // FFN_TILE))
    gathered(("ffn2_w_up",), got)
    (proj, xr, xim, y, pooled, mixed, h2, n2, cat), _ = _mix_fwd(
        h1, vec("mix_pre_norm"), vec("ssm_out_norm"), vec("pool_out_norm"), vec("mix_post_norm"), vec("ssm_d"), vec("pool_scale"), full["w_in"], full["w_out"],
        bbre, bbim, ccre, ccim, wgv, wgg, pw, tabs, pows, "mix_fwd",
    )
    dh3, sq, f2, ga2, si2, s2, n3 = _ffn_fwd_loss(
        h2, loss_target[0], vec("ffn2_pre_norm"), vec("ffn2_post_norm"), full["ffn2_w_gate"], full["ffn2_w_up"], full["ffn2_w_down"], "ffn2_fwd"
    )

    g, shared = {}, {}
    ffn_names = lambda tag: (tag + "_w_gate", tag + "_w_up", tag + "_w_down")

    da, db, df, dh2, g["ffn2_post_norm"], g["ffn2_pre_norm"] = _ffn_bwd(
        dh3, f2, ga2, si2, h2, vec("ffn2_post_norm"), vec("ffn2_pre_norm"), full["ffn2_w_gate"], full["ffn2_w_up"], full["ffn2_w_down"], "ffn2_bwd"
    )
    views2 = [
        grad_view(_tn_matmul(da, n3, "ffn2_dw_gate")[0]),
        grad_view(_tn_matmul(db, n3, "ffn2_dw_up")[0]),
        grad_view(_tn_matmul(s2, df, "ffn2_dw_down")[0]),
    ]
    (dy, dpooled, dmixed, g["mix_post_norm"], g["ssm_out_norm"], g["pool_out_norm"], g["ssm_d"], g["pool_scale"], dwgv, dwgg, g["pool_w"]), got = _mix_bwd_heads(
        dh2, mixed, y, pooled, proj, vec("ssm_out_norm"), vec("pool_out_norm"), vec("mix_post_norm"), vec("pool_scale"), full["w_out"], wgv, wgg, pw, "mix_bwd_heads",
        exchange=_SiblingScatter(views2),
    )
    parts2, wire2 = reduce_sum(got, views2, "ffn2")
    (dproj, dccre, dccim, dbbre, dbbim, dar, dai), got = _mix_bwd_scan(
        dy, dpooled, xr, xim, proj, vec("ssm_d"), bbre, bbim, ccre, ccim, tabs, pows, "mix_bwd_scan", exchange=_ChipScatter(wire2)
    )
    halves2 = reduce_halves(parts2, got, "ffn2")
    (dh1, g["mix_pre_norm"]), got = _mix_bwd_in(dproj, h1, dh2, vec("mix_pre_norm"), full["w_in"], "mix_bwd_in", exchange=_SiblingShare(halves2))
    shared.update(zip(ffn_names("ffn2"), got))

    da, db, df, grad_x, d_meta, g["ffn1_post_norm"], g["ffn1_pre_norm"] = _ffn_bwd_first(
        dh1, f1, ga1, si1, x[0], meta, vec("ffn1_post_norm"), vec("ffn1_pre_norm"), full["ffn1_w_gate"], full["ffn1_w_up"], full["ffn1_w_down"], "ffn1_bwd"
    )
    grad_x = grad_x[None]
    dw_down, _ = _tn_matmul(s1, df, "ffn1_dw_down")
    views_d = [grad_view(dw_down)]
    dw_gate, got = _tn_matmul(da, n1, "ffn1_dw_gate", exchange=_SiblingScatter(views_d))
    parts_d, wire_d = reduce_sum(got, views_d, "ffn1_down")
    views_g = [grad_view(dw_gate)]
    ex = _Group([_ChipScatter(wire_d), _SiblingScatter(views_g)])
    dw_up, got = _tn_matmul(db, n1, "ffn1_dw_up", exchange=ex)
    got_d, got_g = ex.split(got)
    halves_d = reduce_halves(parts_d, got_d, "ffn1_down")
    parts_g, wire_g = reduce_sum(got_g, views_g, "ffn1_gate")
    views_u = [grad_view(dw_up)]
    ex = _Group([_SiblingShare(halves_d), _ChipScatter(wire_g), _SiblingScatter(views_u)])
    dw_in, got = _tn_matmul(n2, dproj, "dw_in", exchange=ex)
    got_d, got_g, got_u = ex.split(got)
    shared["ffn1_w_down"] = got_d[0]
    halves_g = reduce_halves(parts_g, got_g, "ffn1_gate")
    parts_u, wire_u = reduce_sum(got_u, views_u, "ffn1_up")
    views_i = [grad_view(dw_in)]
    ex = _Group([_SiblingShare(halves_g), _ChipScatter(wire_u), _SiblingScatter(views_i)])
    dw_out, got = _tn_matmul(cat, dmixed, "dw_out", exchange=ex)
    got_g, got_u, got_i = ex.split(got)
    shared["ffn1_w_gate"] = got_g[0]
    halves_u = reduce_halves(parts_u, got_u, "ffn1_up")
    parts_i, wire_i = reduce_sum(got_i, views_i, "w_in")

    g["ssm_c_re"] = _block_diag_extract(dccre, N, H, False)
    g["ssm_c_im"] = _block_diag_extract(dccim, N, H, False)
    g["ssm_w_glu"] = jnp.concatenate([_block_diag_extract(dwgv, H, H, True), _block_diag_extract(dwgg, H, H, True)], axis=-1)
    d_a_re, d_a_im = jnp.sum(dar, axis=1).reshape(G, N), jnp.sum(dai, axis=1).reshape(G, N)
    _, pull = jax.vjp(_discretize, p["ssm_lambda_re"][0], p["ssm_lambda_im"][0], p["ssm_log_dt"][0], p["ssm_b_re"][0], p["ssm_b_im"][0])
    g["ssm_lambda_re"], g["ssm_lambda_im"], g["ssm_log_dt"], g["ssm_b_re"], g["ssm_b_im"] = pull(
        (d_a_re, d_a_im, _block_diag_extract(dbbre, H, N, False), _block_diag_extract(dbbim, H, N, False))
    )

    small_shapes = [p[n].shape for n in SMALL] + [(N_META, D), (1,)]
    small_size = sum(math.prod(s) for s in small_shapes)
    rows = -(-small_size // (LANES * PACK_ROWS)) * PACK_ROWS
    views_t = [grad_view(dw_out), _pack([g[n] for n in SMALL] + [d_meta, jnp.sum(sq).reshape(1)], rows).reshape(4, 2, rows // 8, LANES)]
    ex = _Group([_SiblingScatter(views_t), _SiblingShare(halves_u)])
    got_t, got_u = ex.split(_exchange_call(ex, "tail_reduce_sibling"))
    shared["ffn1_w_up"] = got_u[0]
    parts_t, wire_t = reduce_sum(got_t, views_t, "tail", wires=[WIRE_DTYPE, F32])
    got = _exchange_call(_ChipScatter(wire_i + wire_t), "tail_reduce_chips")
    got = _exchange_call(_SiblingShare(reduce_halves(parts_i + parts_t, got, "tail")), "tail_reduce_share")
    shared["w_in"], shared["w_out"] = got[0], got[1]
    small_buf = lax.dynamic_update_slice(lax.empty((4,) + got[2].shape, F32), got[2][None], (chip, 0, 0, 0))
    small_all = _exchange_call(_Gather([small_buf]), "gather_small")[0].reshape(rows, LANES)
    grads = dict(zip(SMALL + ("meta_full", "sq_sum"), _unpack(small_all, small_shapes)))
    grads["meta_tokens"] = lax.dynamic_slice_in_dim(grads.pop("meta_full"), chip * (D // 4), D // 4, axis=1)
    loss = (0.5 / D) * grads.pop("sq_sum")[0]
    delta, new_m, new_v = {}, {}, {}
    for n in BIG:
        g_rows = shared[n].reshape(-1, shared[n].shape[-1])
        outs = _adamw(rows_of(n, p[n]), g_rows, rows_of(n, m[n]), rows_of(n, v[n]), "adamw_" + n)
        grads[n], delta[n], new_m[n], new_v[n] = (rows_back(n, a) for a in (g_rows, *outs))
    delta["meta_tokens"], new_m["meta_tokens"], new_v["meta_tokens"] = _adamw(p["meta_tokens"], grads["meta_tokens"], m["meta_tokens"], v["meta_tokens"], "adamw_meta_tokens")

    def as_2d(n, a):
        a = a.reshape(p[n].shape)[0]
        if n in ("ssm_b_re", "ssm_b_im"):
            a = jnp.swapaxes(a, 1, 2)
        return a.reshape(-1, a.shape[-1])

    def from_2d(n, a):
        if n in ("ssm_b_re", "ssm_b_im"):
            g_, n_, h_ = p[n].shape[1:]
            return jnp.swapaxes(a.reshape(g_, h_, n_), 1, 2)[None]
        return a.reshape(p[n].shape)

    outs = _adamw_many(*[[as_2d(n, t[n]) for n in SMALL] for t in (p, grads, m, v)], "adamw_small")
    for out, arrays in zip((delta, new_m, new_v), outs):
        out.update({n: from_2d(n, a) for n, a in zip(SMALL, arrays)})

    return (loss, grad_x, *[grads[n] for n in ORDER], *[delta[n] for n in ORDER], *[new_m[n] for n in ORDER], *[new_v[n] for n in ORDER])


def kernel(x, meta_tokens, ffn1_pre_norm, ffn1_post_norm, ffn1_w_gate, ffn1_w_up, ffn1_w_down, mix_pre_norm, mix_post_norm, w_in, ssm_lambda_re, ssm_lambda_im, ssm_log_dt, ssm_b_re, ssm_b_im, ssm_c_re, ssm_c_im, ssm_d, ssm_w_glu, pool_w, pool_scale, ssm_out_norm, pool_out_norm, w_out, ffn2_pre_norm, ffn2_post_norm, ffn2_w_gate, ffn2_w_up, ffn2_w_down, loss_target, m_meta_tokens, m_ffn1_pre_norm, m_ffn1_post_norm, m_ffn1_w_gate, m_ffn1_w_up, m_ffn1_w_down, m_mix_pre_norm, m_mix_post_norm, m_w_in, m_ssm_lambda_re, m_ssm_lambda_im, m_ssm_log_dt, m_ssm_b_re, m_ssm_b_im, m_ssm_c_re, m_ssm_c_im, m_ssm_d, m_ssm_w_glu, m_pool_w, m_pool_scale, m_ssm_out_norm, m_pool_out_norm, m_w_out, m_ffn2_pre_norm, m_ffn2_post_norm, m_ffn2_w_gate, m_ffn2_w_up, m_ffn2_w_down, v_meta_tokens, v_ffn1_pre_norm, v_ffn1_post_norm, v_ffn1_w_gate, v_ffn1_w_up, v_ffn1_w_down, v_mix_pre_norm, v_mix_post_norm, v_w_in, v_ssm_lambda_re, v_ssm_lambda_im, v_ssm_log_dt, v_ssm_b_re, v_ssm_b_im, v_ssm_c_re, v_ssm_c_im, v_ssm_d, v_ssm_w_glu, v_pool_w, v_pool_scale, v_ssm_out_norm, v_pool_out_norm, v_w_out, v_ffn2_pre_norm, v_ffn2_post_norm, v_ffn2_w_gate, v_ffn2_w_up, v_ffn2_w_down):
    args = locals()
    p = {n: args[n] for n in ORDER}
    m = {n: args["m_" + n] for n in ORDER}
    v = {n: args["v_" + n] for n in ORDER}
    return _step(p, x, loss_target, m, v)
```

```python
import math

import jax
import jax.numpy as jnp
from jax import lax
from jax.experimental import pallas as pl
from jax.experimental.pallas import tpu as pltpu

F32 = jnp.float32
MXU_DTYPE = jnp.bfloat16
WIRE_DTYPE = jnp.bfloat16

RMS_EPS = 1e-6
N_META = 16
POOL_WINDOWS = (2, 4, 8, 16)
POOL_HALO = 16
ADAM_LR, ADAM_B1, ADAM_B2, ADAM_EPS, ADAM_WD, ADAM_STEP = 0.001, 0.9, 0.999, 1e-08, 0.01, 10

LANES = 128
SUBLANES = 8
VMEM_LIMIT = 60 * 1024 * 1024
FFN_TILE = 432
FFN_CHUNK = 1024
TN_TILE = 1024
MIX_TILE = 216
MIX_SUBTILES = 2
SLAB_GROUP = 8
MESH = pl.DeviceIdType.MESH
ANY = pl.BlockSpec(memory_space=pl.ANY)


def _mm(a, b):
    return jnp.dot(a.astype(MXU_DTYPE), b.astype(MXU_DTYPE), preferred_element_type=F32)


def _mm_nt(a, b):
    return lax.dot_general(a.astype(MXU_DTYPE), b.astype(MXU_DTYPE), (((1,), (1,)), ((), ())), preferred_element_type=F32)


def _mm_tn(a, b):
    return lax.dot_general(a.astype(MXU_DTYPE), b.astype(MXU_DTYPE), (((0,), (0,)), ((), ())), preferred_element_type=F32)


def _rms_stat(x):
    return lax.rsqrt(jnp.mean(x * x, axis=-1, keepdims=True) + RMS_EPS)


def _rms_bwd(x, g, dy):
    r = _rms_stat(x)
    xh = x * r
    dg = jnp.sum(dy * xh, axis=0, keepdims=True)
    dxh = dy * g
    dx = r * (dxh - xh * jnp.mean(dxh * xh, axis=-1, keepdims=True))
    return dx, dg


def _sigmoid(x):
    return 1.0 / (1.0 + jnp.exp(-x))


GELU_C = math.sqrt(2.0 / math.pi)
GELU_K = 0.044715


def _gelu(y):
    return 0.5 * y * (1.0 + jnp.tanh(GELU_C * (y + GELU_K * y * y * y)))


def _gelu_grad(y):
    th = jnp.tanh(GELU_C * (y + GELU_K * y * y * y))
    return 0.5 * (1.0 + th) + 0.5 * y * (1.0 - th * th) * GELU_C * (1.0 + 3.0 * GELU_K * y * y)


def _row_spec(tile, cols, rev_n=None):
    if rev_n is None:
        return pl.BlockSpec((tile, cols), lambda i: (i, 0))
    return pl.BlockSpec((tile, cols), lambda i: (rev_n - 1 - i, 0))


def _full_spec(shape, single=False):
    zeros = (0,) * len(shape)
    if single:
        return pl.BlockSpec(shape, lambda *_: zeros, pipeline_mode=pl.Buffered(1))
    return pl.BlockSpec(shape, lambda *_: zeros)


def _acc(ref, val, first):
    @pl.when(first)
    def _():
        ref[...] = val

    @pl.when(jnp.logical_not(first))
    def _():
        ref[...] += val


def _place():
    x, y, c = lax.axis_index("x"), lax.axis_index("y"), lax.axis_index("c")
    others = [(1 - x, y), (x, 1 - y), (1 - x, 1 - y)]
    return x, y, c, others


class _Exchange:
    mid_step = None

    def __init__(self, ins, out_shapes, aliases, n_sems):
        self.ins, self.out_shapes, self.aliases, self.n_sems = list(ins), list(out_shapes), dict(aliases), n_sems

    def mid(self, ins, outs, send_sems, recv_sems):
        pass


class _SiblingScatter(_Exchange):
    def __init__(self, views):
        super().__init__(views, [jax.ShapeDtypeStruct((4,) + v.shape[2:], v.dtype) for v in views], {}, 4 * len(views))

    def _copies(self, ins, outs, send_sems, recv_sems):
        x, y, c, _ = _place()
        return [
            pltpu.make_async_remote_copy(src_ref=ins[a].at[k, 1 - c], dst_ref=outs[a].at[k], send_sem=send_sems.at[4 * a + k], recv_sem=recv_sems.at[4 * a + k], device_id=(x, y, 1 - c), device_id_type=MESH)
            for a in range(len(ins))
            for k in range(4)
        ]

    def start(self, *refs):
        for cp in self._copies(*refs):
            cp.start()

    def finish(self, *refs):
        cps = self._copies(*refs)
        for cp in cps:
            cp.wait_recv()
        for cp in cps:
            cp.wait_send()


class _ChipScatter(_Exchange):
    def __init__(self, parts):
        super().__init__(parts, [jax.ShapeDtypeStruct((3,) + p.shape[1:], p.dtype) for p in parts], {}, 3 * len(parts))

    def _copies(self, ins, outs, send_sems, recv_sems):
        x, y, c, others = _place()
        return [
            pltpu.make_async_remote_copy(src_ref=ins[a].at[2 * chip[0] + chip[1]], dst_ref=outs[a].at[j], send_sem=send_sems.at[3 * a + j], recv_sem=recv_sems.at[3 * a + j], device_id=(*chip, c), device_id_type=MESH)
            for a in range(len(ins))
            for j, chip in enumerate(others)
        ]

    start = _SiblingScatter.start
    finish = _SiblingScatter.finish


class _SiblingShare(_Exchange):
    def __init__(self, bufs):
        super().__init__(bufs, [jax.ShapeDtypeStruct(b.shape, b.dtype) for b in bufs], {a: a for a in range(len(bufs))}, len(bufs))

    def _copy(self, outs, send_sems, recv_sems, a, half):
        x, y, c, _ = _place()
        mine = outs[a].at[c if half == "mine" else 1 - c]
        return pltpu.make_async_remote_copy(src_ref=mine, dst_ref=mine, send_sem=send_sems.at[a], recv_sem=recv_sems.at[a], device_id=(x, y, 1 - c), device_id_type=MESH)

    def start(self, ins, outs, send_sems, recv_sems):
        for a in range(len(outs)):
            self._copy(outs, send_sems, recv_sems, a, "mine").start()

    def finish(self, ins, outs, send_sems, recv_sems):
        for a in range(len(outs)):
            self._copy(outs, send_sems, recv_sems, a, "theirs").wait_recv()
        for a in range(len(outs)):
            self._copy(outs, send_sems, recv_sems, a, "mine").wait_send()


class _Gather(_Exchange):
    def __init__(self, bufs, mid_step=None):
        super().__init__(bufs, [jax.ShapeDtypeStruct(b.shape, b.dtype) for b in bufs], {a: a for a in range(len(bufs))}, 6 * len(bufs))
        self.mid_step = mid_step

    def _copy(self, outs, send_sems, recv_sems, a, j, chip, half, to):
        blk = outs[a].at[2 * chip[0] + chip[1], half]
        return pltpu.make_async_remote_copy(src_ref=blk, dst_ref=blk, send_sem=send_sems.at[6 * a + j], recv_sem=recv_sems.at[6 * a + j], device_id=to, device_id_type=MESH)

    def start(self, ins, outs, send_sems, recv_sems):
        x, y, c, others = _place()
        for a in range(len(outs)):
            for j, chip in enumerate(others):
                self._copy(outs, send_sems, recv_sems, a, j, (x, y), c, (*chip, c)).start()

    def mid(self, ins, outs, send_sems, recv_sems):
        x, y, c, others = _place()
        for a in range(len(outs)):
            for j, chip in enumerate(others):
                self._copy(outs, send_sems, recv_sems, a, j, chip, c, (x, y, c)).wait_recv()
                self._copy(outs, send_sems, recv_sems, a, 3 + j, chip, c, (x, y, 1 - c)).start()

    def finish(self, ins, outs, send_sems, recv_sems):
        x, y, c, others = _place()
        for a in range(len(outs)):
            for j, chip in enumerate(others):
                self._copy(outs, send_sems, recv_sems, a, 3 + j, chip, 1 - c, (x, y, c)).wait_recv()
        for a in range(len(outs)):
            for j, chip in enumerate(others):
                self._copy(outs, send_sems, recv_sems, a, j, (x, y), c, (*chip, c)).wait_send()
                self._copy(outs, send_sems, recv_sems, a, 3 + j, chip, c, (x, y, 1 - c)).wait_send()


class _SemSlice:
    def __init__(self, sems, off):
        self.sems, self.off = sems, off

    @property
    def at(self):
        return self

    def __getitem__(self, i):
        return self.sems.at[self.off + i]


class _Group(_Exchange):
    def __init__(self, exchanges):
        ins, outs, aliases, n_sems, self.spans = [], [], {}, 0, []
        for ex in exchanges:
            self.spans.append((len(ins), len(outs), n_sems))
            aliases.update({len(ins) + i: len(outs) + o for i, o in ex.aliases.items()})
            ins, outs, n_sems = ins + ex.ins, outs + ex.out_shapes, n_sems + ex.n_sems
        super().__init__(ins, outs, aliases, n_sems)
        self.exchanges = exchanges
        mids = {ex.mid_step for ex in exchanges if ex.mid_step is not None}
        self.mid_step = mids.pop() if mids else None

    def _each(self, method, ins, outs, send_sems, recv_sems):
        for ex, (i0, o0, s0) in zip(self.exchanges, self.spans):
            getattr(ex, method)(ins[i0 : i0 + len(ex.ins)], outs[o0 : o0 + len(ex.out_shapes)], _SemSlice(send_sems, s0), _SemSlice(recv_sems, s0))

    def start(self, *refs):
        self._each("start", *refs)

    def mid(self, *refs):
        self._each("mid", *refs)

    def finish(self, *refs):
        self._each("finish", *refs)

    def split(self, outs):
        return [outs[o0 : o0 + len(ex.out_shapes)] for ex, (_, o0, _) in zip(self.exchanges, self.spans)]


def _exchange_call(ex, name):
    n, m = len(ex.ins), len(ex.out_shapes)

    def body(*refs):
        parts = (refs[:n], refs[n : n + m], refs[n + m], refs[n + m + 1])
        ex.start(*parts)
        ex.mid(*parts)
        ex.finish(*parts)

    return pl.pallas_call(
        body,
        name=name,
        out_shape=ex.out_shapes,
        in_specs=[ANY] * n,
        out_specs=[ANY] * m,
        scratch_shapes=[pltpu.SemaphoreType.DMA((ex.n_sems,)), pltpu.SemaphoreType.DMA((ex.n_sems,))],
        input_output_aliases=ex.aliases,
    )(*ex.ins)


def _pallas(body, *, name, grid, in_specs, out_specs, out_shape, operands, scratch_shapes=(), exchange=None):
    params = pltpu.CompilerParams(dimension_semantics=("arbitrary",) * len(grid), vmem_limit_bytes=VMEM_LIMIT)
    if exchange is None:
        outs = pl.pallas_call(body, name=name, grid=grid, in_specs=in_specs, out_specs=out_specs, out_shape=out_shape, scratch_shapes=list(scratch_shapes), compiler_params=params)(*operands)
        return outs, []
    ex = exchange
    n_in, n_out, n_scr = len(in_specs), len(out_specs), len(scratch_shapes)
    x_in, x_out = len(ex.ins), len(ex.out_shapes)

    def hosted(*refs):
        ins, x_ins = refs[:n_in], refs[n_in : n_in + x_in]
        outs, x_outs = refs[n_in + x_in : n_in + x_in + n_out], refs[n_in + x_in + n_out : n_in + x_in + n_out + x_out]
        rest = refs[n_in + x_in + n_out + x_out :]
        parts = (x_ins, x_outs, rest[n_scr], rest[n_scr + 1])
        ids = [pl.program_id(d) for d in range(len(grid))]
        first = _all([i == 0 for i in ids])
        last = _all([i == g - 1 for i, g in zip(ids, grid)])

        @pl.when(first)
        def _():
            ex.start(*parts)

        body(*ins, *outs, *rest[:n_scr])

        if ex.mid_step is not None:

            @pl.when(ids[0] == ex.mid_step)
            def _():
                ex.mid(*parts)

        @pl.when(last)
        def _():
            ex.finish(*parts)

    outs = pl.pallas_call(
        hosted,
        name=name,
        grid=grid,
        in_specs=list(in_specs) + [ANY] * x_in,
        out_specs=list(out_specs) + [ANY] * x_out,
        out_shape=list(out_shape) + ex.out_shapes,
        scratch_shapes=list(scratch_shapes) + [pltpu.SemaphoreType.DMA((ex.n_sems,)), pltpu.SemaphoreType.DMA((ex.n_sems,))],
        input_output_aliases={n_in + i: n_out + o for i, o in ex.aliases.items()},
        compiler_params=params,
    )(*operands, *ex.ins)
    return outs[:n_out], outs[n_out:]


def _all(conds):
    out = conds[0]
    for c in conds[1:]:
        out = jnp.logical_and(out, c)
    return out


def _row_tile(rows):
    if rows <= 512:
        return rows
    for t in (512, 352, 256, 176, 128, 112, 64, 32, 16, 8):
        if rows % t == 0:
            return t
    return rows


def _add_own_half(view, got, place, name, wire=WIRE_DTYPE):
    _, _, r, c = view.shape
    tr = _row_tile(r)

    def body(place_ref, v_ref, g_ref, o_ref, w_ref):
        s = v_ref[...] + g_ref[...]
        w_ref[...] = s.astype(w_ref.dtype)

        @pl.when(pl.program_id(1) == place_ref[0])
        def _():
            o_ref[...] = s

    blk = pl.BlockSpec((None, tr, c), lambda i, k, pr: (k, i, 0))
    return pl.pallas_call(
        body,
        name=name,
        out_shape=[jax.ShapeDtypeStruct((r, c), F32), jax.ShapeDtypeStruct((4, r, c), wire)],
        grid_spec=pltpu.PrefetchScalarGridSpec(
            num_scalar_prefetch=1,
            grid=(r // tr, 4),
            in_specs=[pl.BlockSpec((None, None, tr, c), lambda i, k, pr: (k, pr[1], i, 0)), blk],
            out_specs=[pl.BlockSpec((tr, c), lambda i, k, pr: (i, 0)), blk],
        ),
        compiler_params=pltpu.CompilerParams(dimension_semantics=("arbitrary", "arbitrary"), vmem_limit_bytes=VMEM_LIMIT),
    )(place, view, got)


def _add_chips(part, got, place, name):
    r, c = part.shape
    tr = _row_tile(r)

    def body(place_ref, p_ref, g_ref, o_ref):
        o_ref[...] = ((p_ref[...] + g_ref[0].astype(F32)) + g_ref[1].astype(F32)) + g_ref[2].astype(F32)

    return pl.pallas_call(
        body,
        name=name,
        out_shape=jax.ShapeDtypeStruct((2, r, c), F32),
        grid_spec=pltpu.PrefetchScalarGridSpec(
            num_scalar_prefetch=1,
            grid=(r // tr,),
            in_specs=[pl.BlockSpec((tr, c), lambda i, pr: (i, 0)), pl.BlockSpec((3, tr, c), lambda i, pr: (0, i, 0))],
            out_specs=pl.BlockSpec((None, tr, c), lambda i, pr: (pr[1], i, 0)),
        ),
        compiler_params=pltpu.CompilerParams(dimension_semantics=("arbitrary",), vmem_limit_bytes=VMEM_LIMIT),
    )(place, part, got)


def _adamw_update(w_ref, g_ref, m_ref, v_ref, d_ref, nm_ref, nv_ref):
    g = g_ref[...]
    nm = ADAM_B1 * m_ref[...] + (1.0 - ADAM_B1) * g
    nv = ADAM_B2 * v_ref[...] + (1.0 - ADAM_B2) * (g * g)
    m_hat = nm / (1.0 - ADAM_B1**ADAM_STEP)
    v_hat = nv / (1.0 - ADAM_B2**ADAM_STEP)
    d_ref[...] = -ADAM_LR * (m_hat / (jnp.sqrt(v_hat) + ADAM_EPS) + ADAM_WD * w_ref[...])
    nm_ref[...] = nm
    nv_ref[...] = nv


def _adamw(w, g, m, v, name):
    r, c = w.shape
    tr = _row_tile(r)
    spec = pl.BlockSpec((tr, c), lambda i: (i, 0))
    outs, _ = _pallas(_adamw_update, name=name, grid=(r // tr,), in_specs=[spec] * 4, out_specs=[spec] * 3, out_shape=[jax.ShapeDtypeStruct((r, c), F32)] * 3, operands=(w, g, m, v))
    return outs


def _adamw_many(ws, gs, ms, vs, name):
    n = len(ws)

    def body(*refs):
        for k in range(n):
            _adamw_update(*(refs[j * n + k] for j in range(7)))

    outs = pl.pallas_call(
        body,
        name=name,
        out_shape=[jax.ShapeDtypeStruct(w.shape, F32) for w in ws] * 3,
        in_specs=[pl.BlockSpec(memory_space=pltpu.VMEM)] * (4 * n),
        out_specs=[pl.BlockSpec(memory_space=pltpu.VMEM)] * (3 * n),
    )(*ws, *gs, *ms, *vs)
    return outs[:n], outs[n : 2 * n], outs[2 * n :]


def _load_weights(pairs, sems):
    @pl.when(pl.program_id(0) == 0)
    def _():
        cps = [pltpu.make_async_copy(src, dst, sems.at[k]) for k, (src, dst) in enumerate(pairs)]
        for cp in cps:
            cp.start()
        for cp in cps:
            cp.wait()


def _ffn_chunks(F):
    bounds = list(range(0, F, FFN_CHUNK)) + [F]
    return list(zip(bounds[:-1], bounds[1:]))


def _shifted_specs(tm, cols):
    per = tm // N_META
    return [_row_spec(tm, cols), pl.BlockSpec((N_META, cols), lambda i: (jnp.maximum(i * per - 1, 0), 0))]


def _shifted_tile(cur_ref, before_ref, tm):
    return jnp.concatenate([before_ref[...], cur_ref[0 : tm - N_META, :]], axis=0)


def _tokens_tile(cur_ref, before_ref, meta_ref, tm, tile_0):
    first = jnp.where(tile_0, meta_ref[...], before_ref[...])
    return jnp.concatenate([first, cur_ref[0 : tm - N_META, :]], axis=0)


def _ffn_fwd_loss(h, target, g_pre, g_post, wg, wu, wd, name):
    T, D = h.shape
    F = wg.shape[0]
    tm = FFN_TILE

    def body(h_ref, t_ref, tb_ref, gpre_ref, gpost_ref, wg_hbm, wu_hbm, wd_hbm, dy_ref, sq_ref, f_ref, ga_ref, si_ref, s_ref, n_ref, wg_v, wu_v, wd_v, sems):
        i = pl.program_id(0)
        _load_weights([(wg_hbm, wg_v), (wu_hbm, wu_v), (wd_hbm, wd_v)], sems)
        hh = h_ref[...]
        n = (hh * _rms_stat(hh) * gpre_ref[...]).astype(MXU_DTYPE)
        n_ref[...] = n.astype(n_ref.dtype)
        f = jnp.zeros((tm, D), F32)
        for lo, hi in _ffn_chunks(F):
            a = _mm_nt(n, wg_v[lo:hi, :])
            b = _mm_nt(n, wu_v[lo:hi, :])
            sg = _sigmoid(a)
            si = a * sg
            s = (si * b).astype(MXU_DTYPE)
            ga_ref[:, lo:hi] = (b * (sg * (1.0 + a * (1.0 - sg)))).astype(ga_ref.dtype)
            si_ref[:, lo:hi] = si.astype(si_ref.dtype)
            s_ref[:, lo:hi] = s.astype(s_ref.dtype)
            f = f + _mm(s, wd_v[lo:hi, :])
        f_ref[...] = f
        y = hh + 0.5 * (f * _rms_stat(f) * gpost_ref[...])
        rows = i * tm + lax.broadcasted_iota(jnp.int32, (tm, D), 0)
        err = jnp.where(rows >= N_META, y - _shifted_tile(t_ref, tb_ref, tm), 0.0)
        dy_ref[...] = err * (1.0 / D)
        _acc(sq_ref, jnp.sum(err * err, axis=0, keepdims=True), i == 0)

    tok = jax.ShapeDtypeStruct((T, D), F32)
    act = jax.ShapeDtypeStruct((T, F), MXU_DTYPE)
    outs, _ = _pallas(
        body,
        name=name,
        grid=(T // tm,),
        out_shape=[tok, jax.ShapeDtypeStruct((1, D), F32), tok, act, act, act, jax.ShapeDtypeStruct((T, D), MXU_DTYPE)],
        in_specs=[_row_spec(tm, D)] + _shifted_specs(tm, D) + [_full_spec((1, D)), _full_spec((1, D)), ANY, ANY, ANY],
        out_specs=[_row_spec(tm, D), _full_spec((1, D)), _row_spec(tm, D), _row_spec(tm, F), _row_spec(tm, F), _row_spec(tm, F), _row_spec(tm, D)],
        scratch_shapes=[pltpu.VMEM(wg.shape, wg.dtype), pltpu.VMEM(wu.shape, wu.dtype), pltpu.VMEM(wd.shape, wd.dtype), pltpu.SemaphoreType.DMA((3,))],
        operands=(h, target, target, g_pre, g_post, wg, wu, wd),
    )
    return outs


def _ffn_gate_up(x, meta, g_pre, wg, wu, name, exchange=None):
    D = x.shape[1]
    T = x.shape[0] + N_META
    F = wg.shape[0]
    tm = FFN_TILE

    def body(x_ref, xb_ref, meta_ref, gpre_ref, wg_hbm, wu_hbm, ga_ref, si_ref, s_ref, n_ref, wg_v, wu_v, sems):
        _load_weights([(wg_hbm, wg_v), (wu_hbm, wu_v)], sems)
        hh = _tokens_tile(x_ref, xb_ref, meta_ref, tm, pl.program_id(0) == 0)
        n = (hh * _rms_stat(hh) * gpre_ref[...]).astype(MXU_DTYPE)
        n_ref[...] = n.astype(n_ref.dtype)
        for lo, hi in _ffn_chunks(F):
            a = _mm_nt(n, wg_v[lo:hi, :])
            b = _mm_nt(n, wu_v[lo:hi, :])
            sg = _sigmoid(a)
            si = a * sg
            ga_ref[:, lo:hi] = (b * (sg * (1.0 + a * (1.0 - sg)))).astype(ga_ref.dtype)
            si_ref[:, lo:hi] = si.astype(si_ref.dtype)
            s_ref[:, lo:hi] = (si * b).astype(s_ref.dtype)

    act = jax.ShapeDtypeStruct((T, F), MXU_DTYPE)
    return _pallas(
        body,
        name=name,
        grid=(T // tm,),
        out_shape=[act, act, act, jax.ShapeDtypeStruct((T, D), MXU_DTYPE)],
        in_specs=_shifted_specs(tm, D) + [_full_spec((N_META, D)), _full_spec((1, D)), ANY, ANY],
        out_specs=[_row_spec(tm, F), _row_spec(tm, F), _row_spec(tm, F), _row_spec(tm, D)],
        scratch_shapes=[pltpu.VMEM(wg.shape, wg.dtype), pltpu.VMEM(wu.shape, wu.dtype), pltpu.SemaphoreType.DMA((2,))],
        operands=(x, x, meta, g_pre, wg, wu),
        exchange=exchange,
    )


def _ffn_down(x, meta, s, g_post, wd, name, exchange=None):
    D = x.shape[1]
    T = x.shape[0] + N_META
    F = wd.shape[0]
    tm = FFN_TILE

    def body(x_ref, xb_ref, meta_ref, s_ref, gpost_ref, wd_hbm, hout_ref, f_ref, wd_v, sems):
        _load_weights([(wd_hbm, wd_v)], sems)
        f = _mm(s_ref[...], wd_v[...])
        f_ref[...] = f
        hout_ref[...] = _tokens_tile(x_ref, xb_ref, meta_ref, tm, pl.program_id(0) == 0) + 0.5 * (f * _rms_stat(f) * gpost_ref[...])

    tok = jax.ShapeDtypeStruct((T, D), F32)
    return _pallas(
        body,
        name=name,
        grid=(T // tm,),
        out_shape=[tok, tok],
        in_specs=_shifted_specs(tm, D) + [_full_spec((N_META, D)), _row_spec(tm, F), _full_spec((1, D)), ANY],
        out_specs=[_row_spec(tm, D), _row_spec(tm, D)],
        scratch_shapes=[pltpu.VMEM(wd.shape, wd.dtype), pltpu.SemaphoreType.DMA((1,))],
        operands=(x, x, meta, s, g_post, wd),
        exchange=exchange,
    )


def _ffn_bwd(dh, f, ga, si, h, g_post, g_pre, wg, wu, wd, name):
    T, D = dh.shape
    F = wd.shape[0]
    tm = FFN_TILE

    def body(dh_ref, f_ref, ga_ref, si_ref, h_ref, gpost_ref, gpre_ref, wg_hbm, wu_hbm, wd_hbm, da_ref, db_ref, df_ref, dhin_ref, dgpost_ref, dgpre_ref, wg_v, wu_v, wd_v, sems):
        first = pl.program_id(0) == 0
        _load_weights([(wg_hbm, wg_v), (wu_hbm, wu_v), (wd_hbm, wd_v)], sems)
        dh = dh_ref[...]
        df, dg = _rms_bwd(f_ref[...], gpost_ref[...], 0.5 * dh)
        _acc(dgpost_ref, dg, first)
        dfb = df.astype(MXU_DTYPE)
        df_ref[...] = dfb.astype(df_ref.dtype)
        dn = jnp.zeros((tm, D), F32)
        for lo, hi in _ffn_chunks(F):
            ds = _mm_nt(dfb, wd_v[lo:hi, :])
            da = (ds * ga_ref[:, lo:hi].astype(F32)).astype(MXU_DTYPE)
            db = (ds * si_ref[:, lo:hi].astype(F32)).astype(MXU_DTYPE)
            da_ref[:, lo:hi] = da.astype(da_ref.dtype)
            db_ref[:, lo:hi] = db.astype(db_ref.dtype)
            dn = dn + _mm(da, wg_v[lo:hi, :]) + _mm(db, wu_v[lo:hi, :])
        dx, dg = _rms_bwd(h_ref[...], gpre_ref[...], dn)
        _acc(dgpre_ref, dg, first)
        dhin_ref[...] = dh + dx

    act = jax.ShapeDtypeStruct((T, F), MXU_DTYPE)
    vec = jax.ShapeDtypeStruct((1, D), F32)
    outs, _ = _pallas(
        body,
        name=name,
        grid=(T // tm,),
        out_shape=[act, act, jax.ShapeDtypeStruct((T, D), MXU_DTYPE), jax.ShapeDtypeStruct((T, D), F32), vec, vec],
        in_specs=[_row_spec(tm, D), _row_spec(tm, D), _row_spec(tm, F), _row_spec(tm, F), _row_spec(tm, D), _full_spec((1, D)), _full_spec((1, D)), ANY, ANY, ANY],
        out_specs=[_row_spec(tm, F), _row_spec(tm, F), _row_spec(tm, D), _row_spec(tm, D), _full_spec((1, D)), _full_spec((1, D))],
        scratch_shapes=[pltpu.VMEM(wg.shape, wg.dtype), pltpu.VMEM(wu.shape, wu.dtype), pltpu.VMEM(wd.shape, wd.dtype), pltpu.SemaphoreType.DMA((3,))],
        operands=(dh, f, ga, si, h, g_post, g_pre, wg, wu, wd),
    )
    return outs


def _ffn_bwd_first(dh, f, ga, si, x, meta, g_post, g_pre, wg, wu, wd, name):
    D = x.shape[1]
    T = x.shape[0] + N_META
    F = wd.shape[0]
    tm = FFN_TILE
    nt = T // tm
    per = tm // N_META
    tile = lambda i: jnp.minimum(i, nt - 1)

    def body(dh_ref, f_ref, ga_ref, si_ref, x_ref, xb_ref, meta_ref, gpost_ref, gpre_ref, wg_hbm, wu_hbm, wd_hbm,
             da_ref, db_ref, df_ref, dx_ref, dmeta_ref, dgpost_ref, dgpre_ref, wg_v, wu_v, wd_v, sems, held):
        i = pl.program_id(0)
        _load_weights([(wg_hbm, wg_v), (wu_hbm, wu_v), (wd_hbm, wd_v)], sems)

        @pl.when(i < nt)
        def _():
            dh = dh_ref[...]
            df, dg = _rms_bwd(f_ref[...], gpost_ref[...], 0.5 * dh)
            _acc(dgpost_ref, dg, i == 0)
            dfb = df.astype(MXU_DTYPE)
            df_ref[...] = dfb.astype(df_ref.dtype)
            dn = jnp.zeros((tm, D), F32)
            for lo, hi in _ffn_chunks(F):
                ds = _mm_nt(dfb, wd_v[lo:hi, :])
                da = (ds * ga_ref[:, lo:hi].astype(F32)).astype(MXU_DTYPE)
                db = (ds * si_ref[:, lo:hi].astype(F32)).astype(MXU_DTYPE)
                da_ref[:, lo:hi] = da.astype(da_ref.dtype)
                db_ref[:, lo:hi] = db.astype(db_ref.dtype)
                dn = dn + _mm(da, wg_v[lo:hi, :]) + _mm(db, wu_v[lo:hi, :])
            dx, dg = _rms_bwd(_tokens_tile(x_ref, xb_ref, meta_ref, tm, i == 0), gpre_ref[...], dn)
            _acc(dgpre_ref, dg, i == 0)
            dh_in = dh + dx

            @pl.when(i == 0)
            def _():
                dmeta_ref[...] = dh_in[0:N_META, :]

            @pl.when(i > 0)
            def _():
                dx_ref[...] = jnp.concatenate([held[...], dh_in[0:N_META, :]], axis=0)

            held[...] = dh_in[N_META:, :]

        @pl.when(i == nt)
        def _():
            dx_ref[0 : tm - N_META, :] = held[...]

    rows = lambda cols: pl.BlockSpec((tm, cols), lambda i: (tile(i), 0))
    act = jax.ShapeDtypeStruct((T, F), MXU_DTYPE)
    vec = jax.ShapeDtypeStruct((1, D), F32)
    outs, _ = _pallas(
        body,
        name=name,
        grid=(nt + 1,),
        out_shape=[act, act, jax.ShapeDtypeStruct((T, D), MXU_DTYPE), jax.ShapeDtypeStruct((T - N_META, D), F32), jax.ShapeDtypeStruct((N_META, D), F32), vec, vec],
        in_specs=[rows(D), rows(D), rows(F), rows(F), rows(D), pl.BlockSpec((N_META, D), lambda i: (jnp.maximum(tile(i) * per - 1, 0), 0)), _full_spec((N_META, D)),
                  _full_spec((1, D)), _full_spec((1, D)), ANY, ANY, ANY],
        out_specs=[rows(F), rows(F), rows(D), pl.BlockSpec((tm, D), lambda i: (jnp.maximum(i - 1, 0), 0)), _full_spec((N_META, D)), _full_spec((1, D)), _full_spec((1, D))],
        scratch_shapes=[pltpu.VMEM(wg.shape, wg.dtype), pltpu.VMEM(wu.shape, wu.dtype), pltpu.VMEM(wd.shape, wd.dtype), pltpu.SemaphoreType.DMA((3,)), pltpu.VMEM((tm - N_META, D), F32)],
        operands=(dh, f, ga, si, x, x, meta, g_post, g_pre, wg, wu, wd),
    )
    return outs


def _token_tile(T):
    for t in (912, 864, 432):
        if T % t == 0:
            return t
    raise ValueError(f"no token tile for {T} rows")


def _tn_matmul(xm, ym, name, exchange=None):
    T, M = xm.shape
    N = ym.shape[1]
    if (T - N_META) % TN_TILE:
        tk = _token_tile(T)

        def body(x_ref, y_ref, o_ref):
            _acc(o_ref, _mm_tn(x_ref[...], y_ref[...]), pl.program_id(0) == 0)

        grid, operands = (T // tk,), (xm, ym)
        in_specs = [pl.BlockSpec((tk, M), lambda k: (k, 0)), pl.BlockSpec((tk, N), lambda k: (k, 0))]
    else:
        tk = TN_TILE

        def body(x_ref, y_ref, xh_ref, yh_ref, o_ref):
            prod = _mm_tn(x_ref[...], y_ref[...])

            @pl.when(pl.program_id(0) == 0)
            def _():
                o_ref[...] = prod + _mm_tn(xh_ref[...], yh_ref[...])

            @pl.when(pl.program_id(0) > 0)
            def _():
                o_ref[...] += prod

        grid, operands = ((T - N_META) // tk,), (xm, ym, xm, ym)
        start = lambda k: (pl.multiple_of(N_META + k * tk, N_META), 0)
        in_specs = [pl.BlockSpec((pl.Element(tk), pl.Element(M)), start), pl.BlockSpec((pl.Element(tk), pl.Element(N)), start),
                    pl.BlockSpec((N_META, M), lambda k: (0, 0)), pl.BlockSpec((N_META, N), lambda k: (0, 0))]

    (out,), x_outs = _pallas(
        body,
        name=name,
        grid=grid,
        out_shape=[jax.ShapeDtypeStruct((M, N), F32)],
        in_specs=in_specs,
        out_specs=[_full_spec((M, N))],
        operands=operands,
        exchange=exchange,
    )
    return out, x_outs


TAB_A, TAB_AS1, TAB_AS2, TAB_AS4, TAB_JF, TAB_JB = 0, 2, 4, 6, 8, 10


def _scan_inplace(zr, zi, tabs, pows, car_r, car_i, seg, reverse, base=0):
    n_slabs = zr.shape[0]
    sgn = -1.0 if reverse else 1.0
    row = lax.broadcasted_iota(jnp.int32, (SUBLANES, LANES), 0)

    def cmul(pr, pi, xr, xi):
        return pr * xr - pi * xi, pr * xi + pi * xr

    for k0 in range(0, n_slabs, SLAB_GROUP):
        slabs = range(k0, min(k0 + SLAB_GROUP, n_slabs))
        ar = [tabs[TAB_A, k] for k in slabs]
        ai = [sgn * tabs[TAB_A + 1, k] for k in slabs]

        def first_pass(t, carry):
            r = (seg - 1 - t) if reverse else t
            out = []
            for q, k in enumerate(slabs):
                xr, xi = carry[2 * q], carry[2 * q + 1]
                pr, pi = cmul(ar[q], ai[q], xr, xi)
                nr = pr + zr[k, pl.ds(base + r, SUBLANES, stride=seg), :]
                ni = pi + zi[k, pl.ds(base + r, SUBLANES, stride=seg), :]
                zr[k, pl.ds(base + r, SUBLANES, stride=seg), :] = nr
                zi[k, pl.ds(base + r, SUBLANES, stride=seg), :] = ni
                out += [nr, ni]
            return tuple(out)

        ends = lax.fori_loop(0, seg, first_pass, tuple(jnp.zeros((SUBLANES, LANES), F32) for _ in range(2 * len(slabs))))

        incoming = []
        for q, k in enumerate(slabs):
            fr, fi = ends[2 * q], ends[2 * q + 1]
            for d, tab in ((1, TAB_AS1), (2, TAB_AS2), (4, TAB_AS4)):
                shift, keep = (SUBLANES - d, row < SUBLANES - d) if reverse else (d, row >= d)
                sr = jnp.where(keep, pltpu.roll(fr, shift, 0), 0.0)
                si = jnp.where(keep, pltpu.roll(fi, shift, 0), 0.0)
                pr, pi = cmul(tabs[tab, k], sgn * tabs[tab + 1, k], sr, si)
                fr, fi = fr + pr, fi + pi
            cr, ci = car_r[k], car_i[k]
            jtab = TAB_JB if reverse else TAB_JF
            pr, pi = cmul(tabs[jtab, k], sgn * tabs[jtab + 1, k], cr, ci)
            er, ei = fr + pr, fi + pi
            if reverse:
                inr = jnp.where(row < SUBLANES - 1, pltpu.roll(er, SUBLANES - 1, 0), cr)
                ini = jnp.where(row < SUBLANES - 1, pltpu.roll(ei, SUBLANES - 1, 0), ci)
                car_r[k] = jnp.broadcast_to(er[0:1, :], (SUBLANES, LANES))
                car_i[k] = jnp.broadcast_to(ei[0:1, :], (SUBLANES, LANES))
            else:
                inr = jnp.where(row >= 1, pltpu.roll(er, 1, 0), cr)
                ini = jnp.where(row >= 1, pltpu.roll(ei, 1, 0), ci)
                car_r[k] = jnp.broadcast_to(er[SUBLANES - 1 : SUBLANES, :], (SUBLANES, LANES))
                car_i[k] = jnp.broadcast_to(ei[SUBLANES - 1 : SUBLANES, :], (SUBLANES, LANES))
            incoming += [inr, ini]

        def second_pass(r, _):
            p = (seg - 1 - r) if reverse else r
            for q, k in enumerate(slabs):
                pr, pi = cmul(pows[0, k, p], sgn * pows[1, k, p], incoming[2 * q], incoming[2 * q + 1])
                zr[k, pl.ds(base + r, SUBLANES, stride=seg), :] = zr[k, pl.ds(base + r, SUBLANES, stride=seg), :] + pr
                zi[k, pl.ds(base + r, SUBLANES, stride=seg), :] = zi[k, pl.ds(base + r, SUBLANES, stride=seg), :] + pi
            return 0

        lax.fori_loop(0, seg, second_pass, 0)


def _slabs_to_cols(ref, k0, n):
    return jnp.concatenate([ref[k0 + q] for q in range(n)], axis=1)


def _window_sum(ext, doublings, forward):
    rows = ext.shape[0]
    s = ext
    for k in range(doublings):
        s = s + pltpu.roll(s, (1 << k) if forward else rows - (1 << k), 0)
    return s


def _mix_fwd(h1, g_pre, g_so, g_po, g_post, dskip, pscale, win, wout, bbre, bbim, ccre, ccim, wgv, wgg, pw, tabs, pows, name, exchange=None):
    T, D = h1.shape
    W = D // 2
    tm = MIX_SUBTILES * MIX_TILE
    seg = MIX_TILE // SUBLANES
    n_slabs = tabs.shape[1]
    nch, cch, sch = bbre.shape
    spc = sch // LANES
    pg = W // len(POOL_WINDOWS)

    def body(h_ref, gpre_ref, gso_ref, gpo_ref, gpost_ref, dskip_ref, pscale_ref, win_ref, wout_ref, bbre_ref, bbim_ref, ccre_ref, ccim_ref, wgv_ref, wgg_ref, pw_ref, tabs_ref, pows_ref,
             proj_ref, xr_ref, xi_ref, y_ref, pooled_ref, mixed_ref, h2_ref, n2_ref, cat_ref, car_r, car_i, halo):
        i = pl.program_id(0)

        @pl.when(i == 0)
        def _():
            car_r[...] = jnp.zeros_like(car_r)
            car_i[...] = jnp.zeros_like(car_i)
            halo[...] = jnp.zeros_like(halo)

        hh = h_ref[...]
        n2 = (hh * _rms_stat(hh) * gpre_ref[...]).astype(MXU_DTYPE)
        n2_ref[...] = n2.astype(n2_ref.dtype)
        proj = _mm(n2, win_ref[...])
        proj_ref[...] = proj
        us, up = proj[:, :W], proj[:, W:]

        for c in range(nch):
            uc = us[:, c * cch : (c + 1) * cch].astype(MXU_DTYPE)
            bur, bui = _mm(uc, bbre_ref[c]), _mm(uc, bbim_ref[c])
            for q in range(spc):
                xr_ref[c * spc + q] = bur[:, q * LANES : (q + 1) * LANES]
                xi_ref[c * spc + q] = bui[:, q * LANES : (q + 1) * LANES]
        for sub in range(MIX_SUBTILES):
            _scan_inplace(xr_ref, xi_ref, tabs_ref, pows_ref, car_r, car_i, seg, reverse=False, base=sub * MIX_TILE)
        ys = []
        for c in range(nch):
            ys.append(_mm(_slabs_to_cols(xr_ref, c * spc, spc), ccre_ref[c]) - _mm(_slabs_to_cols(xi_ref, c * spc, spc), ccim_ref[c]))
        y = jnp.concatenate(ys, axis=1) + dskip_ref[...] * us
        y_ref[...] = y
        ge = _gelu(y).astype(MXU_DTYPE)
        zv = jnp.concatenate([_mm(ge[:, c * cch : (c + 1) * cch], wgv_ref[c]) for c in range(nch)], axis=1)
        zg = jnp.concatenate([_mm(ge[:, c * cch : (c + 1) * cch], wgg_ref[c]) for c in range(nch)], axis=1)
        out = zv * _sigmoid(zg)
        cat_s = out * _rms_stat(out) * gso_ref[...]

        ext = jnp.concatenate([halo[...], up], axis=0)
        halo[...] = up[tm - POOL_HALO :, :]
        t1 = (i * tm + 1 + lax.broadcasted_iota(jnp.int32, (tm, pg), 0)).astype(F32)
        pooled, pms = [], []
        for g, w in enumerate(POOL_WINDOWS):
            col = ext[:, g * pg : (g + 1) * pg]
            win_sum = _window_sum(col, g + 1, True)[POOL_HALO:, :]
            pooled_g = win_sum / jnp.minimum(t1, float(w)) - up[:, g * pg : (g + 1) * pg]
            pooled.append(pooled_g)
            pms.append(_mm(pooled_g, pw_ref[g]))
        pooled_ref[...] = jnp.concatenate(pooled, axis=1)
        yp = jnp.concatenate(pms, axis=1) * pscale_ref[...]
        cat_p = yp * _rms_stat(yp) * gpo_ref[...]

        cat = jnp.concatenate([cat_s, cat_p], axis=1).astype(MXU_DTYPE)
        cat_ref[...] = cat.astype(cat_ref.dtype)
        mixed = _mm(cat, wout_ref[...])
        mixed_ref[...] = mixed
        h2_ref[...] = hh + mixed * _rms_stat(mixed) * gpost_ref[...]

    tok = lambda cols, dt=F32: jax.ShapeDtypeStruct((T, cols), dt)
    slab_spec = pl.BlockSpec((n_slabs, tm, LANES), lambda i: (0, i, 0))
    operands = (h1, g_pre, g_so, g_po, g_post, dskip, pscale, win, wout, bbre, bbim, ccre, ccim, wgv, wgg, pw, tabs, pows)
    return _pallas(
        body,
        name=name,
        grid=(T // tm,),
        out_shape=[tok(D), jax.ShapeDtypeStruct((n_slabs, T, LANES), F32), jax.ShapeDtypeStruct((n_slabs, T, LANES), F32), tok(W), tok(W), tok(D), tok(D), tok(D, MXU_DTYPE), tok(D, MXU_DTYPE)],
        in_specs=[_row_spec(tm, D)] + [_full_spec(o.shape, single=True) for o in operands[1:]],
        out_specs=[_row_spec(tm, D), slab_spec, slab_spec, _row_spec(tm, W), _row_spec(tm, W), _row_spec(tm, D), _row_spec(tm, D), _row_spec(tm, D), _row_spec(tm, D)],
        scratch_shapes=[pltpu.VMEM((n_slabs, SUBLANES, LANES), F32), pltpu.VMEM((n_slabs, SUBLANES, LANES), F32), pltpu.VMEM((POOL_HALO, W), F32)],
        operands=operands,
        exchange=exchange,
    )


def _mix_bwd_heads(dh2, mixed, y, pooled, proj, g_so, g_po, g_post, pscale, wout, wgv, wgg, pw, name, exchange=None):
    T, D = dh2.shape
    W = D // 2
    tm = _token_tile(T)
    nch, cch, _ = wgv.shape
    ng, pg, _ = pw.shape

    def body(dh2_ref, mixed_ref, y_ref, pooled_ref, us_ref, gso_ref, gpo_ref, gpost_ref, pscale_ref, wout_ref, wgv_ref, wgg_ref, pw_ref,
             dy_ref, dpooled_ref, dmixed_ref, dgpost_ref, dgso_ref, dgpo_ref, dd_ref, dscale_ref, dwgv_ref, dwgg_ref, dpw_ref):
        first = pl.program_id(0) == 0
        dmixed, dgpost = _rms_bwd(mixed_ref[...], gpost_ref[...], dh2_ref[...])
        _acc(dgpost_ref, dgpost, first)
        dmb = dmixed.astype(MXU_DTYPE)
        dmixed_ref[...] = dmb.astype(dmixed_ref.dtype)
        dcat = _mm_nt(dmb, wout_ref[...])
        dcs, dcp = dcat[:, :W], dcat[:, W:]

        y = y_ref[...]
        ge = _gelu(y).astype(MXU_DTYPE)
        zv = jnp.concatenate([_mm(ge[:, c * cch : (c + 1) * cch], wgv_ref[c]) for c in range(nch)], axis=1)
        zg = jnp.concatenate([_mm(ge[:, c * cch : (c + 1) * cch], wgg_ref[c]) for c in range(nch)], axis=1)
        sg = _sigmoid(zg)
        dout, dgso = _rms_bwd(zv * sg, gso_ref[...], dcs)
        _acc(dgso_ref, dgso, first)
        dzv = (dout * sg).astype(MXU_DTYPE)
        dzg = (dout * zv * sg * (1.0 - sg)).astype(MXU_DTYPE)
        dges = []
        for c in range(nch):
            cs = slice(c * cch, (c + 1) * cch)
            dges.append(_mm_nt(dzv[:, cs], wgv_ref[c]) + _mm_nt(dzg[:, cs], wgg_ref[c]))
            _acc(dwgv_ref.at[c], _mm_tn(ge[:, cs], dzv[:, cs]), first)
            _acc(dwgg_ref.at[c], _mm_tn(ge[:, cs], dzg[:, cs]), first)
        dy = jnp.concatenate(dges, axis=1) * _gelu_grad(y)
        dy_ref[...] = dy
        _acc(dd_ref, jnp.sum(dy * us_ref[...], axis=0, keepdims=True), first)

        pooled_b = pooled_ref[...].astype(MXU_DTYPE)
        pm = jnp.concatenate([_mm(pooled_b[:, g * pg : (g + 1) * pg], pw_ref[g]) for g in range(ng)], axis=1)
        dyp, dgpo = _rms_bwd(pm * pscale_ref[...], gpo_ref[...], dcp)
        _acc(dgpo_ref, dgpo, first)
        _acc(dscale_ref, jnp.sum(dyp * pm, axis=0, keepdims=True), first)
        dpm = (dyp * pscale_ref[...]).astype(MXU_DTYPE)
        dps = []
        for g in range(ng):
            gs = slice(g * pg, (g + 1) * pg)
            dps.append(_mm_nt(dpm[:, gs], pw_ref[g]))
            _acc(dpw_ref.at[g], _mm_tn(pooled_b[:, gs], dpm[:, gs]), first)
        dpooled_ref[...] = jnp.concatenate(dps, axis=1)

    vec = lambda n: jax.ShapeDtypeStruct((1, n), F32)
    operands = (dh2, mixed, y, pooled, proj, g_so, g_po, g_post, pscale, wout, wgv, wgg, pw)
    return _pallas(
        body,
        name=name,
        grid=(T // tm,),
        out_shape=[jax.ShapeDtypeStruct((T, W), F32), jax.ShapeDtypeStruct((T, W), F32), jax.ShapeDtypeStruct((T, D), MXU_DTYPE), vec(D), vec(W), vec(W), vec(W), vec(W),
                   jax.ShapeDtypeStruct(wgv.shape, F32), jax.ShapeDtypeStruct(wgg.shape, F32), jax.ShapeDtypeStruct(pw.shape, F32)],
        in_specs=[_row_spec(tm, D), _row_spec(tm, D), _row_spec(tm, W), _row_spec(tm, W), _row_spec(tm, W)] + [_full_spec(o.shape) for o in operands[5:]],
        out_specs=[_row_spec(tm, W), _row_spec(tm, W), _row_spec(tm, D), _full_spec((1, D)), _full_spec((1, W)), _full_spec((1, W)), _full_spec((1, W)), _full_spec((1, W)),
                   _full_spec(wgv.shape), _full_spec(wgg.shape), _full_spec(pw.shape)],
        operands=operands,
        exchange=exchange,
    )


def _mix_bwd_scan(dy, dpooled, xr, xi, proj, dskip, bbre, bbim, ccre, ccim, tabs, pows, name, exchange=None):
    T, W = dy.shape
    D = 2 * W
    tm = MIX_SUBTILES * MIX_TILE
    seg = MIX_TILE // SUBLANES
    nt = T // tm
    n_slabs = tabs.shape[1]
    nch, cch, sch = bbre.shape
    spc = sch // LANES
    pg = W // len(POOL_WINDOWS)
    blocks_per_tile = tm // SUBLANES

    def body(dy_ref, dp_ref, xr_ref, xi_ref, xpr_ref, xpi_ref, us_ref, dskip_ref, bbre_ref, bbim_ref, ccre_ref, ccim_ref, tabs_ref, pows_ref,
             dproj_ref, dccre_ref, dccim_ref, dbbre_ref, dbbim_ref, dar_ref, dai_ref, lr, li, car_r, car_i, halo):
        i = pl.program_id(0)
        first = i == 0
        tile = nt - 1 - i
        row = lax.broadcasted_iota(jnp.int32, (SUBLANES, LANES), 0)

        @pl.when(first)
        def _():
            car_r[...] = jnp.zeros_like(car_r)
            car_i[...] = jnp.zeros_like(car_i)
            halo[...] = jnp.zeros_like(halo)
            dar_ref[...] = jnp.zeros_like(dar_ref)
            dai_ref[...] = jnp.zeros_like(dai_ref)

        dy = dy_ref[...]
        for c in range(nch):
            dyc = dy[:, c * cch : (c + 1) * cch]
            gr, gi = _mm_nt(dyc, ccre_ref[c]), _mm_nt(dyc, ccim_ref[c])
            for q in range(spc):
                lr[c * spc + q] = gr[:, q * LANES : (q + 1) * LANES]
                li[c * spc + q] = -gi[:, q * LANES : (q + 1) * LANES]
            _acc(dccre_ref.at[c], _mm_tn(_slabs_to_cols(xr_ref, c * spc, spc), dyc), first)
            _acc(dccim_ref.at[c], -_mm_tn(_slabs_to_cols(xi_ref, c * spc, spc), dyc), first)
        for sub in reversed(range(MIX_SUBTILES)):
            _scan_inplace(lr, li, tabs_ref, pows_ref, car_r, car_i, seg, reverse=True, base=sub * MIX_TILE)

        for sub in range(MIX_SUBTILES):
            base = sub * MIX_TILE
            for k0 in range(0, n_slabs, SLAB_GROUP):
                slabs = range(k0, min(k0 + SLAB_GROUP, n_slabs))
                init = []
                for k in slabs:
                    if sub == 0:
                        prev_r = jnp.where(tile > 0, jnp.broadcast_to(xpr_ref[k, SUBLANES - 1 : SUBLANES, :], (SUBLANES, LANES)), 0.0)
                        prev_i = jnp.where(tile > 0, jnp.broadcast_to(xpi_ref[k, SUBLANES - 1 : SUBLANES, :], (SUBLANES, LANES)), 0.0)
                    else:
                        prev_r = jnp.broadcast_to(xr_ref[k, base - 1 : base, :], (SUBLANES, LANES))
                        prev_i = jnp.broadcast_to(xi_ref[k, base - 1 : base, :], (SUBLANES, LANES))
                    x0r = jnp.where(row >= 1, pltpu.roll(xr_ref[k, pl.ds(base + seg - 1, SUBLANES, stride=seg), :], 1, 0), prev_r)
                    x0i = jnp.where(row >= 1, pltpu.roll(xi_ref[k, pl.ds(base + seg - 1, SUBLANES, stride=seg), :], 1, 0), prev_i)
                    l0r, l0i = lr[k, pl.ds(base, SUBLANES, stride=seg), :], li[k, pl.ds(base, SUBLANES, stride=seg), :]
                    init += [l0r * x0r + l0i * x0i, l0i * x0r - l0r * x0i]

                def step(r, acc, slabs=slabs, base=base):
                    out = []
                    for q, k in enumerate(slabs):
                        pr_, pi_ = xr_ref[k, pl.ds(base + r - 1, SUBLANES, stride=seg), :], xi_ref[k, pl.ds(base + r - 1, SUBLANES, stride=seg), :]
                        lr_, li_ = lr[k, pl.ds(base + r, SUBLANES, stride=seg), :], li[k, pl.ds(base + r, SUBLANES, stride=seg), :]
                        out += [acc[2 * q] + lr_ * pr_ + li_ * pi_, acc[2 * q + 1] + li_ * pr_ - lr_ * pi_]
                    return tuple(out)

                sums = lax.fori_loop(1, seg, step, tuple(init))
                for q, k in enumerate(slabs):
                    dar_ref[k] += sums[2 * q]
                    dai_ref[k] += sums[2 * q + 1]

        us = us_ref[...]
        dus = []
        for c in range(nch):
            lrc, lic = _slabs_to_cols(lr, c * spc, spc).astype(MXU_DTYPE), _slabs_to_cols(li, c * spc, spc).astype(MXU_DTYPE)
            uc = us[:, c * cch : (c + 1) * cch]
            _acc(dbbre_ref.at[c], _mm_tn(uc, lrc), first)
            _acc(dbbim_ref.at[c], _mm_tn(uc, lic), first)
            dus.append(_mm_nt(lrc, bbre_ref[c]) + _mm_nt(lic, bbim_ref[c]))
        du_s = jnp.concatenate(dus, axis=1) + dskip_ref[...] * dy

        dp = dp_ref[...]
        t1 = (tile * tm + 1 + lax.broadcasted_iota(jnp.int32, (tm, pg), 0)).astype(F32)
        dups, heads = [], []
        for g, w in enumerate(POOL_WINDOWS):
            dpg = dp[:, g * pg : (g + 1) * pg]
            qg = dpg / jnp.minimum(t1, float(w))
            ext = jnp.concatenate([qg, halo[:, g * pg : (g + 1) * pg]], axis=0)
            dups.append(_window_sum(ext, g + 1, False)[:tm, :] - dpg)
            heads.append(qg[:POOL_HALO, :])
        halo[...] = jnp.concatenate(heads, axis=1)
        dproj_ref[...] = jnp.concatenate([du_s] + dups, axis=1).astype(dproj_ref.dtype)

    rev = lambda cols: _row_spec(tm, cols, rev_n=nt)
    slab_spec = pl.BlockSpec((n_slabs, tm, LANES), lambda i: (0, nt - 1 - i, 0))
    prev_spec = pl.BlockSpec((n_slabs, SUBLANES, LANES), lambda i: (0, jnp.maximum((nt - 1 - i) * blocks_per_tile - 1, 0), 0))
    consts = (dskip, bbre, bbim, ccre, ccim, tabs, pows)
    return _pallas(
        body,
        name=name,
        grid=(nt,),
        out_shape=[jax.ShapeDtypeStruct((T, D), MXU_DTYPE), jax.ShapeDtypeStruct(ccre.shape, F32), jax.ShapeDtypeStruct(ccim.shape, F32), jax.ShapeDtypeStruct(bbre.shape, F32),
                   jax.ShapeDtypeStruct(bbim.shape, F32), jax.ShapeDtypeStruct((n_slabs, SUBLANES, LANES), F32), jax.ShapeDtypeStruct((n_slabs, SUBLANES, LANES), F32)],
        in_specs=[rev(W), rev(W), slab_spec, slab_spec, prev_spec, prev_spec, rev(W)] + [_full_spec(o.shape, single=True) for o in consts],
        out_specs=[rev(D), _full_spec(ccre.shape), _full_spec(ccim.shape), _full_spec(bbre.shape), _full_spec(bbim.shape),
                   _full_spec((n_slabs, SUBLANES, LANES)), _full_spec((n_slabs, SUBLANES, LANES))],
        scratch_shapes=[pltpu.VMEM((n_slabs, tm, LANES), F32), pltpu.VMEM((n_slabs, tm, LANES), F32), pltpu.VMEM((n_slabs, SUBLANES, LANES), F32), pltpu.VMEM((n_slabs, SUBLANES, LANES), F32),
                        pltpu.VMEM((POOL_HALO, W), F32)],
        operands=(dy, dpooled, xr, xi, xr, xi, proj, *consts),
        exchange=exchange,
    )


def _mix_bwd_in(dproj, h1, dh2, g_pre, win, name, exchange=None):
    T, D = h1.shape
    tm = _token_tile(T)

    def body(dproj_ref, h_ref, dh2_ref, gpre_ref, win_ref, dh1_ref, dg_ref):
        dx, dg = _rms_bwd(h_ref[...], gpre_ref[...], _mm_nt(dproj_ref[...], win_ref[...]))
        _acc(dg_ref, dg, pl.program_id(0) == 0)
        dh1_ref[...] = dh2_ref[...] + dx

    return _pallas(
        body,
        name=name,
        grid=(T // tm,),
        out_shape=[jax.ShapeDtypeStruct((T, D), F32), jax.ShapeDtypeStruct((1, D), F32)],
        in_specs=[_row_spec(tm, D), _row_spec(tm, D), _row_spec(tm, D), _full_spec((1, D)), _full_spec(win.shape)],
        out_specs=[_row_spec(tm, D), _full_spec((1, D))],
        operands=(dproj, h1, dh2, g_pre, win),
        exchange=exchange,
    )


def _discretize(lam_re, lam_im, log_dt, b_re, b_im):
    dt = jnp.exp(log_dt)[:, None]
    decay = jnp.exp(lam_re * dt)
    ang = lam_im * dt
    a_re, a_im = decay * jnp.cos(ang), decay * jnp.sin(ang)
    nr = a_re - 1.0
    den = lam_re * lam_re + lam_im * lam_im
    q_re = (nr * lam_re + a_im * lam_im) / den
    q_im = (a_im * lam_re - nr * lam_im) / den
    bb_re = q_re[..., None] * b_re - q_im[..., None] * b_im
    bb_im = q_re[..., None] * b_im + q_im[..., None] * b_re
    return a_re, a_im, bb_re, bb_im


GROUPS_PER_CHUNK = 16


def _block_diag(w, rows_first):
    G = w.shape[0]
    nch = G // GROUPS_PER_CHUNK
    if not rows_first:
        w = jnp.swapaxes(w, 1, 2)
    p, q = w.shape[1], w.shape[2]
    eye = jnp.eye(GROUPS_PER_CHUNK, dtype=w.dtype)
    out = jnp.einsum("cgpq,gk->cgpkq", w.reshape(nch, GROUPS_PER_CHUNK, p, q), eye)
    return out.reshape(nch, GROUPS_PER_CHUNK * p, GROUPS_PER_CHUNK * q)


def _block_diag_extract(m, p, q, rows_first):
    nch = m.shape[0]
    eye = jnp.eye(GROUPS_PER_CHUNK, dtype=m.dtype)
    out = jnp.einsum("cgpkq,gk->cgpq", m.reshape(nch, GROUPS_PER_CHUNK, p, GROUPS_PER_CHUNK, q), eye).reshape(nch * GROUPS_PER_CHUNK, p, q)
    return out if rows_first else jnp.swapaxes(out, 1, 2)


def _cmul(ar, ai, br, bi):
    return ar * br - ai * bi, ar * bi + ai * br


def _powers(ar, ai, count):
    pr, pi = ar[None], ai[None]
    while pr.shape[0] < count:
        nr, ni = _cmul(pr, pi, pr[-1][None], pi[-1][None])
        pr, pi = jnp.concatenate([pr, nr]), jnp.concatenate([pi, ni])
    return pr[:count], pi[:count]


def _scan_tables(a_re, a_im, seg):
    n = a_re.size
    ns = n // LANES
    ar, ai = a_re.reshape(n), a_im.reshape(n)
    pr, pi = _powers(ar, ai, seg)
    jr, ji = _powers(pr[-1], pi[-1], SUBLANES)

    def bcast(v):
        return jnp.broadcast_to(v.reshape(ns, 1, LANES), (ns, SUBLANES, LANES))

    def per_sublane(vs):
        return jnp.transpose(vs.reshape(SUBLANES, ns, LANES), (1, 0, 2))

    tabs = jnp.stack([bcast(ar), bcast(ai), bcast(jr[0]), bcast(ji[0]), bcast(jr[1]), bcast(ji[1]), bcast(jr[3]), bcast(ji[3]),
                      per_sublane(jr), per_sublane(ji), per_sublane(jr[::-1]), per_sublane(ji[::-1])])

    def rows(vs):
        return jnp.broadcast_to(jnp.transpose(vs.reshape(seg, ns, 1, LANES), (1, 0, 2, 3)), (ns, seg, SUBLANES, LANES))

    return tabs, jnp.stack([rows(pr), rows(pi)])


SMALL = ("ffn1_pre_norm", "ffn1_post_norm", "mix_pre_norm", "mix_post_norm", "ssm_lambda_re", "ssm_lambda_im", "ssm_log_dt", "ssm_b_re", "ssm_b_im", "ssm_c_re", "ssm_c_im",
         "ssm_d", "ssm_w_glu", "pool_w", "pool_scale", "ssm_out_norm", "pool_out_norm", "ffn2_pre_norm", "ffn2_post_norm")
BIG = ("ffn1_w_gate", "ffn1_w_up", "ffn1_w_down", "w_in", "w_out", "ffn2_w_gate", "ffn2_w_up", "ffn2_w_down")
ORDER = ("meta_tokens", "ffn1_pre_norm", "ffn1_post_norm", "ffn1_w_gate", "ffn1_w_up", "ffn1_w_down", "mix_pre_norm", "mix_post_norm", "w_in", "ssm_lambda_re", "ssm_lambda_im",
         "ssm_log_dt", "ssm_b_re", "ssm_b_im", "ssm_c_re", "ssm_c_im", "ssm_d", "ssm_w_glu", "pool_w", "pool_scale", "ssm_out_norm", "pool_out_norm", "w_out", "ffn2_pre_norm",
         "ffn2_post_norm", "ffn2_w_gate", "ffn2_w_up", "ffn2_w_down")
PACK_ROWS = SUBLANES * 8
def _pack(arrays, rows):
    flat = jnp.concatenate([a.reshape(-1) for a in arrays])
    return jnp.pad(flat, (0, rows * LANES - flat.size)).reshape(rows, LANES)


def _unpack(packed, shapes):
    flat = packed.reshape(-1)
    out, off = [], 0
    for s in shapes:
        n = math.prod(s)
        out.append(flat[off : off + n].reshape(s))
        off += n
    return out


def _step(p, x, loss_target, m, v):
    D = x.shape[-1]
    chip = (2 * lax.axis_index("x") + lax.axis_index("y")).astype(jnp.int32)
    place = jnp.stack([chip, lax.axis_index("c").astype(jnp.int32)])

    def gather_buffer(w):
        own = w.reshape(1, 2, w.shape[0] // 2, w.shape[1])
        return lax.dynamic_update_slice(lax.empty((4,) + own.shape[1:], own.dtype), own, (chip, 0, 0, 0))

    def rows_of(n, a):
        return jnp.swapaxes(a[0], 0, 1) if n.endswith(("w_gate", "w_up")) else a[0]

    def rows_back(n, a):
        return (jnp.swapaxes(a, 0, 1) if n.endswith(("w_gate", "w_up")) else a)[None]

    def grad_view(g):
        return g.reshape(4, 2, g.shape[0] // 8, g.shape[1])

    def reduce_sum(got_sibling, views, tag, wires=None):
        wires = wires or [WIRE_DTYPE] * len(views)
        sums = [_add_own_half(v_, g_, place, f"{tag}_add_sibling_{k}", w_) for k, (v_, g_, w_) in enumerate(zip(views, got_sibling, wires))]
        return [s[0] for s in sums], [s[1] for s in sums]

    def reduce_halves(parts, got_chips, tag):
        return [_add_chips(p_, g_, place, f"{tag}_add_chips_{k}") for k, (p_, g_) in enumerate(zip(parts, got_chips))]

    bufs = {n: gather_buffer(rows_of(n, p[n]).astype(MXU_DTYPE)) for n in BIG}
    full = {}

    def gathered(names, got):
        full.update({n: g_.reshape(-1, g_.shape[-1]) for n, g_ in zip(names, got)})

    def gather_of(names, n_steps):
        return _Gather([bufs[n] for n in names], mid_step=(3 * n_steps) // 4)

    first_names = ("ffn1_w_gate", "ffn1_w_up")
    got = _exchange_call(_Gather([bufs[n] for n in first_names] + [gather_buffer(p["meta_tokens"])]), "gather_first")
    gathered(first_names, got)
    meta = jnp.transpose(got[-1].reshape(4, N_META, -1), (1, 0, 2)).reshape(N_META, D)

    vec = lambda n: p[n].reshape(1, -1)
    G, N, H = p["ssm_b_re"].shape[1:]
    a_re, a_im, bb_re, bb_im = _discretize(p["ssm_lambda_re"][0], p["ssm_lambda_im"][0], p["ssm_log_dt"][0], p["ssm_b_re"][0], p["ssm_b_im"][0])
    tabs, pows = _scan_tables(a_re, a_im, MIX_TILE // SUBLANES)
    bf = lambda a: a.astype(MXU_DTYPE)
    bbre, bbim = bf(_block_diag(bb_re, False)), bf(_block_diag(bb_im, False))
    ccre, ccim = bf(_block_diag(p["ssm_c_re"][0], False)), bf(_block_diag(p["ssm_c_im"][0], False))
    wgv, wgg = bf(_block_diag(p["ssm_w_glu"][0][:, :, :H], True)), bf(_block_diag(p["ssm_w_glu"][0][:, :, H:], True))
    pw = bf(p["pool_w"][0])

    T = x.shape[1] + N_META
    names = ("ffn1_w_down", "w_in", "w_out", "ffn2_w_gate", "ffn2_w_down")
    (ga1, si1, s1, n1), got = _ffn_gate_up(
        x[0], meta, vec("ffn1_pre_norm"), full["ffn1_w_gate"], full["ffn1_w_up"], "ffn1_gate_up", exchange=gather_of(names, T // FFN_TILE)
    )
    gathered(names, got)
    (h1, f1), got = _ffn_down(x[0], meta, s1, vec("ffn1_post_norm"), full["ffn1_w_down"], "ffn1_down", exchange=gather_of(("ffn2_w_up",), T // FFN_TILE))
    gathered(("ffn2_w_up",), got)
    (proj, xr, xim, y, pooled, mixed, h2, n2, cat), _ = _mix_fwd(
        h1, vec("mix_pre_norm"), vec("ssm_out_norm"), vec("pool_out_norm"), vec("mix_post_norm"), vec("ssm_d"), vec("pool_scale"), full["w_in"], full["w_out"],
        bbre, bbim, ccre, ccim, wgv, wgg, pw, tabs, pows, "mix_fwd",
    )
    dh3, sq, f2, ga2, si2, s2, n3 = _ffn_fwd_loss(
        h2, loss_target[0], vec("ffn2_pre_norm"), vec("ffn2_post_norm"), full["ffn2_w_gate"], full["ffn2_w_up"], full["ffn2_w_down"], "ffn2_fwd"
    )

    g, shared = {}, {}
    ffn_names = lambda tag: (tag + "_w_gate", tag + "_w_up", tag + "_w_down")

    da, db, df, dh2, g["ffn2_post_norm"], g["ffn2_pre_norm"] = _ffn_bwd(
        dh3, f2, ga2, si2, h2, vec("ffn2_post_norm"), vec("ffn2_pre_norm"), full["ffn2_w_gate"], full["ffn2_w_up"], full["ffn2_w_down"], "ffn2_bwd"
    )
    views2 = [
        grad_view(_tn_matmul(da, n3, "ffn2_dw_gate")[0]),
        grad_view(_tn_matmul(db, n3, "ffn2_dw_up")[0]),
        grad_view(_tn_matmul(s2, df, "ffn2_dw_down")[0]),
    ]
    (dy, dpooled, dmixed, g["mix_post_norm"], g["ssm_out_norm"], g["pool_out_norm"], g["ssm_d"], g["pool_scale"], dwgv, dwgg, g["pool_w"]), got = _mix_bwd_heads(
        dh2, mixed, y, pooled, proj, vec("ssm_out_norm"), vec("pool_out_norm"), vec("mix_post_norm"), vec("pool_scale"), full["w_out"], wgv, wgg, pw, "mix_bwd_heads",
        exchange=_SiblingScatter(views2),
    )
    parts2, wire2 = reduce_sum(got, views2, "ffn2")
    (dproj, dccre, dccim, dbbre, dbbim, dar, dai), got = _mix_bwd_scan(
        dy, dpooled, xr, xim, proj, vec("ssm_d"), bbre, bbim, ccre, ccim, tabs, pows, "mix_bwd_scan", exchange=_ChipScatter(wire2)
    )
    halves2 = reduce_halves(parts2, got, "ffn2")
    (dh1, g["mix_pre_norm"]), got = _mix_bwd_in(dproj, h1, dh2, vec("mix_pre_norm"), full["w_in"], "mix_bwd_in", exchange=_SiblingShare(halves2))
    shared.update(zip(ffn_names("ffn2"), got))

    da, db, df, grad_x, d_meta, g["ffn1_post_norm"], g["ffn1_pre_norm"] = _ffn_bwd_first(
        dh1, f1, ga1, si1, x[0], meta, vec("ffn1_post_norm"), vec("ffn1_pre_norm"), full["ffn1_w_gate"], full["ffn1_w_up"], full["ffn1_w_down"], "ffn1_bwd"
    )
    grad_x = grad_x[None]

    g["ssm_c_re"] = _block_diag_extract(dccre, N, H, False)
    g["ssm_c_im"] = _block_diag_extract(dccim, N, H, False)
    g["ssm_w_glu"] = jnp.concatenate([_block_diag_extract(dwgv, H, H, True), _block_diag_extract(dwgg, H, H, True)], axis=-1)
    d_a_re, d_a_im = jnp.sum(dar, axis=1).reshape(G, N), jnp.sum(dai, axis=1).reshape(G, N)
    _, pull = jax.vjp(_discretize, p["ssm_lambda_re"][0], p["ssm_lambda_im"][0], p["ssm_log_dt"][0], p["ssm_b_re"][0], p["ssm_b_im"][0])
    g["ssm_lambda_re"], g["ssm_lambda_im"], g["ssm_log_dt"], g["ssm_b_re"], g["ssm_b_im"] = pull(
        (d_a_re, d_a_im, _block_diag_extract(dbbre, H, N, False), _block_diag_extract(dbbim, H, N, False))
    )
    small_shapes = [p[n].shape for n in SMALL] + [(N_META, D), (1,)]
    small_size = sum(math.prod(s) for s in small_shapes)
    rows = -(-small_size // (LANES * PACK_ROWS)) * PACK_ROWS
    views_s = [_pack([g[n] for n in SMALL] + [d_meta, jnp.sum(sq).reshape(1)], rows).reshape(4, 2, rows // 8, LANES)]

    dw_down, got = _tn_matmul(s1, df, "ffn1_dw_down", exchange=_SiblingScatter(views_s))
    parts_s, wire_s = reduce_sum(got, views_s, "small", wires=[F32])
    views_d = [grad_view(dw_down)]
    ex = _Group([_SiblingScatter(views_d), _ChipScatter(wire_s)])
    dw_gate, got = _tn_matmul(da, n1, "ffn1_dw_gate", exchange=ex)
    got_d, got_s = ex.split(got)
    parts_d, wire_d = reduce_sum(got_d, views_d, "ffn1_down")
    halves_s = reduce_halves(parts_s, got_s, "small")
    views_g = [grad_view(dw_gate)]
    ex = _Group([_ChipScatter(wire_d), _SiblingScatter(views_g), _SiblingShare(halves_s)])
    dw_up, got = _tn_matmul(db, n1, "ffn1_dw_up", exchange=ex)
    got_d, got_g, got_s = ex.split(got)
    small_buf = lax.dynamic_update_slice(lax.empty((4,) + got_s[0].shape, F32), got_s[0][None], (chip, 0, 0, 0))
    halves_d = reduce_halves(parts_d, got_d, "ffn1_down")
    parts_g, wire_g = reduce_sum(got_g, views_g, "ffn1_gate")
    views_u = [grad_view(dw_up)]
    tn_steps = (T - N_META) // TN_TILE if (T - N_META) % TN_TILE == 0 else T // _token_tile(T)
    ex = _Group([_SiblingShare(halves_d), _ChipScatter(wire_g), _SiblingScatter(views_u), _Gather([small_buf], mid_step=(3 * tn_steps) // 4)])
    dw_in, got = _tn_matmul(n2, dproj, "dw_in", exchange=ex)
    got_d, got_g, got_u, got_s = ex.split(got)
    grads = dict(zip(SMALL + ("meta_full", "sq_sum"), _unpack(got_s[0].reshape(rows, LANES), small_shapes)))
    grads["meta_tokens"] = lax.dynamic_slice_in_dim(grads.pop("meta_full"), chip * (D // 4), D // 4, axis=1)
    loss = (0.5 / D) * grads.pop("sq_sum")[0]
    shared["ffn1_w_down"] = got_d[0]
    halves_g = reduce_halves(parts_g, got_g, "ffn1_gate")
    parts_u, wire_u = reduce_sum(got_u, views_u, "ffn1_up")
    views_i = [grad_view(dw_in)]
    ex = _Group([_SiblingShare(halves_g), _ChipScatter(wire_u), _SiblingScatter(views_i)])
    dw_out, got = _tn_matmul(cat, dmixed, "dw_out", exchange=ex)
    got_g, got_u, got_i = ex.split(got)
    shared["ffn1_w_gate"] = got_g[0]
    halves_u = reduce_halves(parts_u, got_u, "ffn1_up")
    parts_i, wire_i = reduce_sum(got_i, views_i, "w_in")

    views_t = [grad_view(dw_out)]
    ex = _Group([_SiblingScatter(views_t), _SiblingShare(halves_u)])
    got_t, got_u = ex.split(_exchange_call(ex, "tail_reduce_sibling"))
    shared["ffn1_w_up"] = got_u[0]
    parts_t, wire_t = reduce_sum(got_t, views_t, "tail")
    got = _exchange_call(_ChipScatter(wire_i + wire_t), "tail_reduce_chips")
    got = _exchange_call(_SiblingShare(reduce_halves(parts_i + parts_t, got, "tail")), "tail_reduce_share")
    shared["w_in"], shared["w_out"] = got[0], got[1]
    delta, new_m, new_v = {}, {}, {}
    for n in BIG:
        g_rows = shared[n].reshape(-1, shared[n].shape[-1])
        outs = _adamw(rows_of(n, p[n]), g_rows, rows_of(n, m[n]), rows_of(n, v[n]), "adamw_" + n)
        grads[n], delta[n], new_m[n], new_v[n] = (rows_back(n, a) for a in (g_rows, *outs))
    delta["meta_tokens"], new_m["meta_tokens"], new_v["meta_tokens"] = _adamw(p["meta_tokens"], grads["meta_tokens"], m["meta_tokens"], v["meta_tokens"], "adamw_meta_tokens")

    def as_2d(n, a):
        a = a.reshape(p[n].shape)[0]
        if n in ("ssm_b_re", "ssm_b_im"):
            a = jnp.swapaxes(a, 1, 2)
        return a.reshape(-1, a.shape[-1])

    def from_2d(n, a):
        if n in ("ssm_b_re", "ssm_b_im"):
            g_, n_, h_ = p[n].shape[1:]
            return jnp.swapaxes(a.reshape(g_, h_, n_), 1, 2)[None]
        return a.reshape(p[n].shape)

    outs = _adamw_many(*[[as_2d(n, t[n]) for n in SMALL] for t in (p, grads, m, v)], "adamw_small")
    for out, arrays in zip((delta, new_m, new_v), outs):
        out.update({n: from_2d(n, a) for n, a in zip(SMALL, arrays)})

    return (loss, grad_x, *[grads[n] for n in ORDER], *[delta[n] for n in ORDER], *[new_m[n] for n in ORDER], *[new_v[n] for n in ORDER])


def kernel(x, meta_tokens, ffn1_pre_norm, ffn1_post_norm, ffn1_w_gate, ffn1_w_up, ffn1_w_down, mix_pre_norm, mix_post_norm, w_in, ssm_lambda_re, ssm_lambda_im, ssm_log_dt, ssm_b_re, ssm_b_im, ssm_c_re, ssm_c_im, ssm_d, ssm_w_glu, pool_w, pool_scale, ssm_out_norm, pool_out_norm, w_out, ffn2_pre_norm, ffn2_post_norm, ffn2_w_gate, ffn2_w_up, ffn2_w_down, loss_target, m_meta_tokens, m_ffn1_pre_norm, m_ffn1_post_norm, m_ffn1_w_gate, m_ffn1_w_up, m_ffn1_w_down, m_mix_pre_norm, m_mix_post_norm, m_w_in, m_ssm_lambda_re, m_ssm_lambda_im, m_ssm_log_dt, m_ssm_b_re, m_ssm_b_im, m_ssm_c_re, m_ssm_c_im, m_ssm_d, m_ssm_w_glu, m_pool_w, m_pool_scale, m_ssm_out_norm, m_pool_out_norm, m_w_out, m_ffn2_pre_norm, m_ffn2_post_norm, m_ffn2_w_gate, m_ffn2_w_up, m_ffn2_w_down, v_meta_tokens, v_ffn1_pre_norm, v_ffn1_post_norm, v_ffn1_w_gate, v_ffn1_w_up, v_ffn1_w_down, v_mix_pre_norm, v_mix_post_norm, v_w_in, v_ssm_lambda_re, v_ssm_lambda_im, v_ssm_log_dt, v_ssm_b_re, v_ssm_b_im, v_ssm_c_re, v_ssm_c_im, v_ssm_d, v_ssm_w_glu, v_pool_w, v_pool_scale, v_ssm_out_norm, v_pool_out_norm, v_w_out, v_ffn2_pre_norm, v_ffn2_post_norm, v_ffn2_w_gate, v_ffn2_w_up, v_ffn2_w_down):
    args = locals()
    p = {n: args[n] for n in ORDER}
    m = {n: args["m_" + n] for n in ORDER}
    v = {n: args["v_" + n] for n in ORDER}
    return _step(p, x, loss_target, m, v)
```

```python
import math

import jax
import jax.numpy as jnp
from jax import lax
from jax.experimental import pallas as pl
from jax.experimental.pallas import tpu as pltpu

F32 = jnp.float32
MXU_DTYPE = jnp.bfloat16
WIRE_DTYPE = jnp.bfloat16

RMS_EPS = 1e-6
N_META = 16
POOL_WINDOWS = (2, 4, 8, 16)
POOL_HALO = 16
ADAM_LR, ADAM_B1, ADAM_B2, ADAM_EPS, ADAM_WD, ADAM_STEP = 0.001, 0.9, 0.999, 1e-08, 0.01, 10

LANES = 128
SUBLANES = 8
VMEM_LIMIT = 60 * 1024 * 1024
FFN_TILE = 432
FFN_CHUNK = 1024
TN_TILE = 1024
MIX_TILE = 216
MIX_SUBTILES = 2
SLAB_GROUP = 8
MESH = pl.DeviceIdType.MESH
ANY = pl.BlockSpec(memory_space=pl.ANY)


def _mm(a, b):
    return jnp.dot(a.astype(MXU_DTYPE), b.astype(MXU_DTYPE), preferred_element_type=F32)


def _mm_nt(a, b):
    return lax.dot_general(a.astype(MXU_DTYPE), b.astype(MXU_DTYPE), (((1,), (1,)), ((), ())), preferred_element_type=F32)


def _mm_tn(a, b):
    return lax.dot_general(a.astype(MXU_DTYPE), b.astype(MXU_DTYPE), (((0,), (0,)), ((), ())), preferred_element_type=F32)


def _rms_stat(x):
    return lax.rsqrt(jnp.mean(x * x, axis=-1, keepdims=True) + RMS_EPS)


def _rms_bwd(x, g, dy):
    r = _rms_stat(x)
    xh = x * r
    dg = jnp.sum(dy * xh, axis=0, keepdims=True)
    dxh = dy * g
    dx = r * (dxh - xh * jnp.mean(dxh * xh, axis=-1, keepdims=True))
    return dx, dg


def _sigmoid(x):
    return 1.0 / (1.0 + jnp.exp(-x))


GELU_C = math.sqrt(2.0 / math.pi)
GELU_K = 0.044715


def _gelu(y):
    return 0.5 * y * (1.0 + jnp.tanh(GELU_C * (y + GELU_K * y * y * y)))


def _gelu_grad(y):
    th = jnp.tanh(GELU_C * (y + GELU_K * y * y * y))
    return 0.5 * (1.0 + th) + 0.5 * y * (1.0 - th * th) * GELU_C * (1.0 + 3.0 * GELU_K * y * y)


def _row_spec(tile, cols, rev_n=None):
    if rev_n is None:
        return pl.BlockSpec((tile, cols), lambda i: (i, 0))
    return pl.BlockSpec((tile, cols), lambda i: (rev_n - 1 - i, 0))


def _full_spec(shape, single=False):
    zeros = (0,) * len(shape)
    if single:
        return pl.BlockSpec(shape, lambda *_: zeros, pipeline_mode=pl.Buffered(1))
    return pl.BlockSpec(shape, lambda *_: zeros)


def _acc(ref, val, first):
    @pl.when(first)
    def _():
        ref[...] = val

    @pl.when(jnp.logical_not(first))
    def _():
        ref[...] += val


def _place():
    x, y, c = lax.axis_index("x"), lax.axis_index("y"), lax.axis_index("c")
    others = [(1 - x, y), (x, 1 - y), (1 - x, 1 - y)]
    return x, y, c, others


class _Exchange:
    def __init__(self, ins, out_shapes, aliases, n_sems):
        self.ins, self.out_shapes, self.aliases, self.n_sems = list(ins), list(out_shapes), dict(aliases), n_sems

    def mid_steps(self):
        return []

    def mid(self, ins, outs, send_sems, recv_sems, step=None):
        pass


class _SiblingScatter(_Exchange):
    def __init__(self, views):
        super().__init__(views, [jax.ShapeDtypeStruct((4,) + v.shape[2:], v.dtype) for v in views], {}, 4 * len(views))

    def _copies(self, ins, outs, send_sems, recv_sems):
        x, y, c, _ = _place()
        return [
            pltpu.make_async_remote_copy(src_ref=ins[a].at[k, 1 - c], dst_ref=outs[a].at[k], send_sem=send_sems.at[4 * a + k], recv_sem=recv_sems.at[4 * a + k], device_id=(x, y, 1 - c), device_id_type=MESH)
            for a in range(len(ins))
            for k in range(4)
        ]

    def start(self, *refs):
        for cp in self._copies(*refs):
            cp.start()

    def finish(self, *refs):
        cps = self._copies(*refs)
        for cp in cps:
            cp.wait_recv()
        for cp in cps:
            cp.wait_send()


class _ChipScatter(_Exchange):
    def __init__(self, parts):
        super().__init__(parts, [jax.ShapeDtypeStruct((3,) + p.shape[1:], p.dtype) for p in parts], {}, 3 * len(parts))

    def _copies(self, ins, outs, send_sems, recv_sems):
        x, y, c, others = _place()
        return [
            pltpu.make_async_remote_copy(src_ref=ins[a].at[2 * chip[0] + chip[1]], dst_ref=outs[a].at[j], send_sem=send_sems.at[3 * a + j], recv_sem=recv_sems.at[3 * a + j], device_id=(*chip, c), device_id_type=MESH)
            for a in range(len(ins))
            for j, chip in enumerate(others)
        ]

    start = _SiblingScatter.start
    finish = _SiblingScatter.finish


class _SiblingShare(_Exchange):
    def __init__(self, bufs):
        super().__init__(bufs, [jax.ShapeDtypeStruct(b.shape, b.dtype) for b in bufs], {a: a for a in range(len(bufs))}, len(bufs))

    def _copy(self, outs, send_sems, recv_sems, a, half):
        x, y, c, _ = _place()
        mine = outs[a].at[c if half == "mine" else 1 - c]
        return pltpu.make_async_remote_copy(src_ref=mine, dst_ref=mine, send_sem=send_sems.at[a], recv_sem=recv_sems.at[a], device_id=(x, y, 1 - c), device_id_type=MESH)

    def start(self, ins, outs, send_sems, recv_sems):
        for a in range(len(outs)):
            self._copy(outs, send_sems, recv_sems, a, "mine").start()

    def finish(self, ins, outs, send_sems, recv_sems):
        for a in range(len(outs)):
            self._copy(outs, send_sems, recv_sems, a, "theirs").wait_recv()
        for a in range(len(outs)):
            self._copy(outs, send_sems, recv_sems, a, "mine").wait_send()


class _Gather(_Exchange):
    def __init__(self, bufs, steps=None):
        super().__init__(bufs, [jax.ShapeDtypeStruct(b.shape, b.dtype) for b in bufs], {a: a for a in range(len(bufs))}, 6 * len(bufs))
        self.steps = steps

    def mid_steps(self):
        return sorted(set(self.steps or []))

    def _copy(self, outs, send_sems, recv_sems, a, j, chip, half, to):
        blk = outs[a].at[2 * chip[0] + chip[1], half]
        return pltpu.make_async_remote_copy(src_ref=blk, dst_ref=blk, send_sem=send_sems.at[6 * a + j], recv_sem=recv_sems.at[6 * a + j], device_id=to, device_id_type=MESH)

    def start(self, ins, outs, send_sems, recv_sems):
        x, y, c, others = _place()
        for a in range(len(outs)):
            for j, chip in enumerate(others):
                self._copy(outs, send_sems, recv_sems, a, j, (x, y), c, (*chip, c)).start()

    def mid(self, ins, outs, send_sems, recv_sems, step=None):
        x, y, c, others = _place()
        for a in range(len(outs)):
            if step is not None and self.steps[a] != step:
                continue
            for j, chip in enumerate(others):
                self._copy(outs, send_sems, recv_sems, a, j, chip, c, (x, y, c)).wait_recv()
                self._copy(outs, send_sems, recv_sems, a, 3 + j, chip, c, (x, y, 1 - c)).start()

    def finish(self, ins, outs, send_sems, recv_sems):
        x, y, c, others = _place()
        for a in range(len(outs)):
            for j, chip in enumerate(others):
                self._copy(outs, send_sems, recv_sems, a, 3 + j, chip, 1 - c, (x, y, c)).wait_recv()
        for a in range(len(outs)):
            for j, chip in enumerate(others):
                self._copy(outs, send_sems, recv_sems, a, j, (x, y), c, (*chip, c)).wait_send()
                self._copy(outs, send_sems, recv_sems, a, 3 + j, chip, c, (x, y, 1 - c)).wait_send()


class _SemSlice:
    def __init__(self, sems, off):
        self.sems, self.off = sems, off

    @property
    def at(self):
        return self

    def __getitem__(self, i):
        return self.sems.at[self.off + i]


class _Group(_Exchange):
    def __init__(self, exchanges):
        ins, outs, aliases, n_sems, self.spans = [], [], {}, 0, []
        for ex in exchanges:
            self.spans.append((len(ins), len(outs), n_sems))
            aliases.update({len(ins) + i: len(outs) + o for i, o in ex.aliases.items()})
            ins, outs, n_sems = ins + ex.ins, outs + ex.out_shapes, n_sems + ex.n_sems
        super().__init__(ins, outs, aliases, n_sems)
        self.exchanges = exchanges

    def mid_steps(self):
        return sorted({s for ex in self.exchanges for s in ex.mid_steps()})

    def _each(self, method, ins, outs, send_sems, recv_sems, **kw):
        for ex, (i0, o0, s0) in zip(self.exchanges, self.spans):
            getattr(ex, method)(ins[i0 : i0 + len(ex.ins)], outs[o0 : o0 + len(ex.out_shapes)], _SemSlice(send_sems, s0), _SemSlice(recv_sems, s0), **kw)

    def start(self, *refs):
        self._each("start", *refs)

    def mid(self, *refs, step=None):
        self._each("mid", *refs, step=step)

    def finish(self, *refs):
        self._each("finish", *refs)

    def split(self, outs):
        return [outs[o0 : o0 + len(ex.out_shapes)] for ex, (_, o0, _) in zip(self.exchanges, self.spans)]


def _exchange_call(ex, name):
    n, m = len(ex.ins), len(ex.out_shapes)

    def body(*refs):
        parts = (refs[:n], refs[n : n + m], refs[n + m], refs[n + m + 1])
        ex.start(*parts)
        ex.mid(*parts)
        ex.finish(*parts)

    return pl.pallas_call(
        body,
        name=name,
        out_shape=ex.out_shapes,
        in_specs=[ANY] * n,
        out_specs=[ANY] * m,
        scratch_shapes=[pltpu.SemaphoreType.DMA((ex.n_sems,)), pltpu.SemaphoreType.DMA((ex.n_sems,))],
        input_output_aliases=ex.aliases,
    )(*ex.ins)


def _pallas(body, *, name, grid, in_specs, out_specs, out_shape, operands, scratch_shapes=(), exchange=None):
    params = pltpu.CompilerParams(dimension_semantics=("arbitrary",) * len(grid), vmem_limit_bytes=VMEM_LIMIT)
    if exchange is None:
        outs = pl.pallas_call(body, name=name, grid=grid, in_specs=in_specs, out_specs=out_specs, out_shape=out_shape, scratch_shapes=list(scratch_shapes), compiler_params=params)(*operands)
        return outs, []
    ex = exchange
    n_in, n_out, n_scr = len(in_specs), len(out_specs), len(scratch_shapes)
    x_in, x_out = len(ex.ins), len(ex.out_shapes)

    def hosted(*refs):
        ins, x_ins = refs[:n_in], refs[n_in : n_in + x_in]
        outs, x_outs = refs[n_in + x_in : n_in + x_in + n_out], refs[n_in + x_in + n_out : n_in + x_in + n_out + x_out]
        rest = refs[n_in + x_in + n_out + x_out :]
        parts = (x_ins, x_outs, rest[n_scr], rest[n_scr + 1])
        ids = [pl.program_id(d) for d in range(len(grid))]
        first = _all([i == 0 for i in ids])
        last = _all([i == g - 1 for i, g in zip(ids, grid)])

        @pl.when(first)
        def _():
            ex.start(*parts)

        body(*ins, *outs, *rest[:n_scr])

        for step in ex.mid_steps():

            @pl.when(ids[0] == step)
            def _(step=step):
                ex.mid(*parts, step=step)

        @pl.when(last)
        def _():
            ex.finish(*parts)

    outs = pl.pallas_call(
        hosted,
        name=name,
        grid=grid,
        in_specs=list(in_specs) + [ANY] * x_in,
        out_specs=list(out_specs) + [ANY] * x_out,
        out_shape=list(out_shape) + ex.out_shapes,
        scratch_shapes=list(scratch_shapes) + [pltpu.SemaphoreType.DMA((ex.n_sems,)), pltpu.SemaphoreType.DMA((ex.n_sems,))],
        input_output_aliases={n_in + i: n_out + o for i, o in ex.aliases.items()},
        compiler_params=params,
    )(*operands, *ex.ins)
    return outs[:n_out], outs[n_out:]


def _all(conds):
    out = conds[0]
    for c in conds[1:]:
        out = jnp.logical_and(out, c)
    return out


def _row_tile(rows):
    if rows <= 512:
        return rows
    for t in (512, 352, 256, 176, 128, 112, 64, 32, 16, 8):
        if rows % t == 0:
            return t
    return rows


def _add_own_half(view, got, place, name, wire=WIRE_DTYPE):
    _, _, r, c = view.shape
    tr = _row_tile(r)

    def body(place_ref, v_ref, g_ref, o_ref, w_ref):
        s = v_ref[...] + g_ref[...]
        w_ref[...] = s.astype(w_ref.dtype)

        @pl.when(pl.program_id(1) == place_ref[0])
        def _():
            o_ref[...] = s

    blk = pl.BlockSpec((None, tr, c), lambda i, k, pr: (k, i, 0))
    return pl.pallas_call(
        body,
        name=name,
        out_shape=[jax.ShapeDtypeStruct((r, c), F32), jax.ShapeDtypeStruct((4, r, c), wire)],
        grid_spec=pltpu.PrefetchScalarGridSpec(
            num_scalar_prefetch=1,
            grid=(r // tr, 4),
            in_specs=[pl.BlockSpec((None, None, tr, c), lambda i, k, pr: (k, pr[1], i, 0)), blk],
            out_specs=[pl.BlockSpec((tr, c), lambda i, k, pr: (i, 0)), blk],
        ),
        compiler_params=pltpu.CompilerParams(dimension_semantics=("arbitrary", "arbitrary"), vmem_limit_bytes=VMEM_LIMIT),
    )(place, view, got)


def _add_chips(part, got, place, name):
    r, c = part.shape
    tr = _row_tile(r)

    def body(place_ref, p_ref, g_ref, o_ref):
        o_ref[...] = ((p_ref[...] + g_ref[0].astype(F32)) + g_ref[1].astype(F32)) + g_ref[2].astype(F32)

    return pl.pallas_call(
        body,
        name=name,
        out_shape=jax.ShapeDtypeStruct((2, r, c), F32),
        grid_spec=pltpu.PrefetchScalarGridSpec(
            num_scalar_prefetch=1,
            grid=(r // tr,),
            in_specs=[pl.BlockSpec((tr, c), lambda i, pr: (i, 0)), pl.BlockSpec((3, tr, c), lambda i, pr: (0, i, 0))],
            out_specs=pl.BlockSpec((None, tr, c), lambda i, pr: (pr[1], i, 0)),
        ),
        compiler_params=pltpu.CompilerParams(dimension_semantics=("arbitrary",), vmem_limit_bytes=VMEM_LIMIT),
    )(place, part, got)


def _adamw_update(w_ref, g_ref, m_ref, v_ref, d_ref, nm_ref, nv_ref):
    g = g_ref[...]
    nm = ADAM_B1 * m_ref[...] + (1.0 - ADAM_B1) * g
    nv = ADAM_B2 * v_ref[...] + (1.0 - ADAM_B2) * (g * g)
    m_hat = nm / (1.0 - ADAM_B1**ADAM_STEP)
    v_hat = nv / (1.0 - ADAM_B2**ADAM_STEP)
    d_ref[...] = -ADAM_LR * (m_hat / (jnp.sqrt(v_hat) + ADAM_EPS) + ADAM_WD * w_ref[...])
    nm_ref[...] = nm
    nv_ref[...] = nv


def _adamw(w, g, m, v, name):
    r, c = w.shape
    tr = _row_tile(r)
    spec = pl.BlockSpec((tr, c), lambda i: (i, 0))
    outs, _ = _pallas(_adamw_update, name=name, grid=(r // tr,), in_specs=[spec] * 4, out_specs=[spec] * 3, out_shape=[jax.ShapeDtypeStruct((r, c), F32)] * 3, operands=(w, g, m, v))
    return outs


def _adamw_many(ws, gs, ms, vs, name):
    n = len(ws)

    def body(*refs):
        for k in range(n):
            _adamw_update(*(refs[j * n + k] for j in range(7)))

    outs = pl.pallas_call(
        body,
        name=name,
        out_shape=[jax.ShapeDtypeStruct(w.shape, F32) for w in ws] * 3,
        in_specs=[pl.BlockSpec(memory_space=pltpu.VMEM)] * (4 * n),
        out_specs=[pl.BlockSpec(memory_space=pltpu.VMEM)] * (3 * n),
    )(*ws, *gs, *ms, *vs)
    return outs[:n], outs[n : 2 * n], outs[2 * n :]


def _load_weights(pairs, sems):
    @pl.when(pl.program_id(0) == 0)
    def _():
        cps = [pltpu.make_async_copy(src, dst, sems.at[k]) for k, (src, dst) in enumerate(pairs)]
        for cp in cps:
            cp.start()
        for cp in cps:
            cp.wait()


def _ffn_chunks(F):
    bounds = list(range(0, F, FFN_CHUNK)) + [F]
    return list(zip(bounds[:-1], bounds[1:]))


def _shifted_specs(tm, cols):
    per = tm // N_META
    return [_row_spec(tm, cols), pl.BlockSpec((N_META, cols), lambda i: (jnp.maximum(i * per - 1, 0), 0))]


def _shifted_tile(cur_ref, before_ref, tm):
    return jnp.concatenate([before_ref[...], cur_ref[0 : tm - N_META, :]], axis=0)


def _tokens_tile(cur_ref, before_ref, meta_ref, tm, tile_0):
    first = jnp.where(tile_0, meta_ref[...], before_ref[...])
    return jnp.concatenate([first, cur_ref[0 : tm - N_META, :]], axis=0)


def _ffn_fwd_loss(h, target, g_pre, g_post, wg, wu, wd, name):
    T, D = h.shape
    F = wg.shape[0]
    tm = FFN_TILE

    def body(h_ref, t_ref, tb_ref, gpre_ref, gpost_ref, wg_hbm, wu_hbm, wd_hbm, dy_ref, sq_ref, f_ref, ga_ref, si_ref, s_ref, n_ref, wg_v, wu_v, wd_v, sems):
        i = pl.program_id(0)
        _load_weights([(wg_hbm, wg_v), (wu_hbm, wu_v), (wd_hbm, wd_v)], sems)
        hh = h_ref[...]
        n = (hh * _rms_stat(hh) * gpre_ref[...]).astype(MXU_DTYPE)
        n_ref[...] = n.astype(n_ref.dtype)
        f = jnp.zeros((tm, D), F32)
        for lo, hi in _ffn_chunks(F):
            a = _mm_nt(n, wg_v[lo:hi, :])
            b = _mm_nt(n, wu_v[lo:hi, :])
            sg = _sigmoid(a)
            si = a * sg
            s = (si * b).astype(MXU_DTYPE)
            ga_ref[:, lo:hi] = (b * (sg * (1.0 + a * (1.0 - sg)))).astype(ga_ref.dtype)
            si_ref[:, lo:hi] = si.astype(si_ref.dtype)
            s_ref[:, lo:hi] = s.astype(s_ref.dtype)
            f = f + _mm(s, wd_v[lo:hi, :])
        f_ref[...] = f
        y = hh + 0.5 * (f * _rms_stat(f) * gpost_ref[...])
        rows = i * tm + lax.broadcasted_iota(jnp.int32, (tm, D), 0)
        err = jnp.where(rows >= N_META, y - _shifted_tile(t_ref, tb_ref, tm), 0.0)
        dy_ref[...] = err * (1.0 / D)
        _acc(sq_ref, jnp.sum(err * err, axis=0, keepdims=True), i == 0)

    tok = jax.ShapeDtypeStruct((T, D), F32)
    act = jax.ShapeDtypeStruct((T, F), MXU_DTYPE)
    outs, _ = _pallas(
        body,
        name=name,
        grid=(T // tm,),
        out_shape=[tok, jax.ShapeDtypeStruct((1, D), F32), tok, act, act, act, jax.ShapeDtypeStruct((T, D), MXU_DTYPE)],
        in_specs=[_row_spec(tm, D)] + _shifted_specs(tm, D) + [_full_spec((1, D)), _full_spec((1, D)), ANY, ANY, ANY],
        out_specs=[_row_spec(tm, D), _full_spec((1, D)), _row_spec(tm, D), _row_spec(tm, F), _row_spec(tm, F), _row_spec(tm, F), _row_spec(tm, D)],
        scratch_shapes=[pltpu.VMEM(wg.shape, wg.dtype), pltpu.VMEM(wu.shape, wu.dtype), pltpu.VMEM(wd.shape, wd.dtype), pltpu.SemaphoreType.DMA((3,))],
        operands=(h, target, target, g_pre, g_post, wg, wu, wd),
    )
    return outs


def _ffn_gate_up(x, meta, g_pre, wg, wu, name, exchange=None):
    D = x.shape[1]
    T = x.shape[0] + N_META
    F = wg.shape[0]
    tm = FFN_TILE

    def body(x_ref, xb_ref, meta_ref, gpre_ref, wg_hbm, wu_hbm, ga_ref, si_ref, s_ref, n_ref, wg_v, wu_v, sems):
        _load_weights([(wg_hbm, wg_v), (wu_hbm, wu_v)], sems)
        hh = _tokens_tile(x_ref, xb_ref, meta_ref, tm, pl.program_id(0) == 0)
        n = (hh * _rms_stat(hh) * gpre_ref[...]).astype(MXU_DTYPE)
        n_ref[...] = n.astype(n_ref.dtype)
        for lo, hi in _ffn_chunks(F):
            a = _mm_nt(n, wg_v[lo:hi, :])
            b = _mm_nt(n, wu_v[lo:hi, :])
            sg = _sigmoid(a)
            si = a * sg
            ga_ref[:, lo:hi] = (b * (sg * (1.0 + a * (1.0 - sg)))).astype(ga_ref.dtype)
            si_ref[:, lo:hi] = si.astype(si_ref.dtype)
            s_ref[:, lo:hi] = (si * b).astype(s_ref.dtype)

    act = jax.ShapeDtypeStruct((T, F), MXU_DTYPE)
    return _pallas(
        body,
        name=name,
        grid=(T // tm,),
        out_shape=[act, act, act, jax.ShapeDtypeStruct((T, D), MXU_DTYPE)],
        in_specs=_shifted_specs(tm, D) + [_full_spec((N_META, D)), _full_spec((1, D)), ANY, ANY],
        out_specs=[_row_spec(tm, F), _row_spec(tm, F), _row_spec(tm, F), _row_spec(tm, D)],
        scratch_shapes=[pltpu.VMEM(wg.shape, wg.dtype), pltpu.VMEM(wu.shape, wu.dtype), pltpu.SemaphoreType.DMA((2,))],
        operands=(x, x, meta, g_pre, wg, wu),
        exchange=exchange,
    )


def _ffn_down(x, meta, s, g_post, wd, name, exchange=None):
    D = x.shape[1]
    T = x.shape[0] + N_META
    F = wd.shape[0]
    tm = _token_tile(T)

    def body(x_ref, xb_ref, meta_ref, s_ref, gpost_ref, wd_hbm, hout_ref, f_ref, wd_v, sems):
        _load_weights([(wd_hbm, wd_v)], sems)
        f = _mm(s_ref[...], wd_v[...])
        f_ref[...] = f
        hout_ref[...] = _tokens_tile(x_ref, xb_ref, meta_ref, tm, pl.program_id(0) == 0) + 0.5 * (f * _rms_stat(f) * gpost_ref[...])

    tok = jax.ShapeDtypeStruct((T, D), F32)
    return _pallas(
        body,
        name=name,
        grid=(T // tm,),
        out_shape=[tok, tok],
        in_specs=_shifted_specs(tm, D) + [_full_spec((N_META, D)), _row_spec(tm, F), _full_spec((1, D)), ANY],
        out_specs=[_row_spec(tm, D), _row_spec(tm, D)],
        scratch_shapes=[pltpu.VMEM(wd.shape, wd.dtype), pltpu.SemaphoreType.DMA((1,))],
        operands=(x, x, meta, s, g_post, wd),
        exchange=exchange,
    )


def _ffn_bwd(dh, f, ga, si, h, g_post, g_pre, wg, wu, wd, name):
    T, D = dh.shape
    F = wd.shape[0]
    tm = FFN_TILE

    def body(dh_ref, f_ref, ga_ref, si_ref, h_ref, gpost_ref, gpre_ref, wg_hbm, wu_hbm, wd_hbm, da_ref, db_ref, df_ref, dhin_ref, dgpost_ref, dgpre_ref, wg_v, wu_v, wd_v, sems):
        first = pl.program_id(0) == 0
        _load_weights([(wg_hbm, wg_v), (wu_hbm, wu_v), (wd_hbm, wd_v)], sems)
        dh = dh_ref[...]
        df, dg = _rms_bwd(f_ref[...], gpost_ref[...], 0.5 * dh)
        _acc(dgpost_ref, dg, first)
        dfb = df.astype(MXU_DTYPE)
        df_ref[...] = dfb.astype(df_ref.dtype)
        dn = jnp.zeros((tm, D), F32)
        for lo, hi in _ffn_chunks(F):
            ds = _mm_nt(dfb, wd_v[lo:hi, :])
            da = (ds * ga_ref[:, lo:hi].astype(F32)).astype(MXU_DTYPE)
            db = (ds * si_ref[:, lo:hi].astype(F32)).astype(MXU_DTYPE)
            da_ref[:, lo:hi] = da.astype(da_ref.dtype)
            db_ref[:, lo:hi] = db.astype(db_ref.dtype)
            dn = dn + _mm(da, wg_v[lo:hi, :]) + _mm(db, wu_v[lo:hi, :])
        dx, dg = _rms_bwd(h_ref[...], gpre_ref[...], dn)
        _acc(dgpre_ref, dg, first)
        dhin_ref[...] = dh + dx

    act = jax.ShapeDtypeStruct((T, F), MXU_DTYPE)
    vec = jax.ShapeDtypeStruct((1, D), F32)
    outs, _ = _pallas(
        body,
        name=name,
        grid=(T // tm,),
        out_shape=[act, act, jax.ShapeDtypeStruct((T, D), MXU_DTYPE), jax.ShapeDtypeStruct((T, D), F32), vec, vec],
        in_specs=[_row_spec(tm, D), _row_spec(tm, D), _row_spec(tm, F), _row_spec(tm, F), _row_spec(tm, D), _full_spec((1, D)), _full_spec((1, D)), ANY, ANY, ANY],
        out_specs=[_row_spec(tm, F), _row_spec(tm, F), _row_spec(tm, D), _row_spec(tm, D), _full_spec((1, D)), _full_spec((1, D))],
        scratch_shapes=[pltpu.VMEM(wg.shape, wg.dtype), pltpu.VMEM(wu.shape, wu.dtype), pltpu.VMEM(wd.shape, wd.dtype), pltpu.SemaphoreType.DMA((3,))],
        operands=(dh, f, ga, si, h, g_post, g_pre, wg, wu, wd),
    )
    return outs


def _ffn_bwd_first(dh, f, ga, si, x, meta, g_post, g_pre, wg, wu, wd, name):
    D = x.shape[1]
    T = x.shape[0] + N_META
    F = wd.shape[0]
    tm = FFN_TILE
    nt = T // tm
    per = tm // N_META
    tile = lambda i: jnp.minimum(i, nt - 1)

    def body(dh_ref, f_ref, ga_ref, si_ref, x_ref, xb_ref, meta_ref, gpost_ref, gpre_ref, wg_hbm, wu_hbm, wd_hbm,
             da_ref, db_ref, df_ref, dx_ref, dmeta_ref, dgpost_ref, dgpre_ref, wg_v, wu_v, wd_v, sems, held):
        i = pl.program_id(0)
        _load_weights([(wg_hbm, wg_v), (wu_hbm, wu_v), (wd_hbm, wd_v)], sems)

        @pl.when(i < nt)
        def _():
            dh = dh_ref[...]
            df, dg = _rms_bwd(f_ref[...], gpost_ref[...], 0.5 * dh)
            _acc(dgpost_ref, dg, i == 0)
            dfb = df.astype(MXU_DTYPE)
            df_ref[...] = dfb.astype(df_ref.dtype)
            dn = jnp.zeros((tm, D), F32)
            for lo, hi in _ffn_chunks(F):
                ds = _mm_nt(dfb, wd_v[lo:hi, :])
                da = (ds * ga_ref[:, lo:hi].astype(F32)).astype(MXU_DTYPE)
                db = (ds * si_ref[:, lo:hi].astype(F32)).astype(MXU_DTYPE)
                da_ref[:, lo:hi] = da.astype(da_ref.dtype)
                db_ref[:, lo:hi] = db.astype(db_ref.dtype)
                dn = dn + _mm(da, wg_v[lo:hi, :]) + _mm(db, wu_v[lo:hi, :])
            dx, dg = _rms_bwd(_tokens_tile(x_ref, xb_ref, meta_ref, tm, i == 0), gpre_ref[...], dn)
            _acc(dgpre_ref, dg, i == 0)
            dh_in = dh + dx

            @pl.when(i == 0)
            def _():
                dmeta_ref[...] = dh_in[0:N_META, :]

            @pl.when(i > 0)
            def _():
                dx_ref[...] = jnp.concatenate([held[...], dh_in[0:N_META, :]], axis=0)

            held[...] = dh_in[N_META:, :]

        @pl.when(i == nt)
        def _():
            dx_ref[0 : tm - N_META, :] = held[...]

    rows = lambda cols: pl.BlockSpec((tm, cols), lambda i: (tile(i), 0))
    act = jax.ShapeDtypeStruct((T, F), MXU_DTYPE)
    vec = jax.ShapeDtypeStruct((1, D), F32)
    outs, _ = _pallas(
        body,
        name=name,
        grid=(nt + 1,),
        out_shape=[act, act, jax.ShapeDtypeStruct((T, D), MXU_DTYPE), jax.ShapeDtypeStruct((T - N_META, D), F32), jax.ShapeDtypeStruct((N_META, D), F32), vec, vec],
        in_specs=[rows(D), rows(D), rows(F), rows(F), rows(D), pl.BlockSpec((N_META, D), lambda i: (jnp.maximum(tile(i) * per - 1, 0), 0)), _full_spec((N_META, D)),
                  _full_spec((1, D)), _full_spec((1, D)), ANY, ANY, ANY],
        out_specs=[rows(F), rows(F), rows(D), pl.BlockSpec((tm, D), lambda i: (jnp.maximum(i - 1, 0), 0)), _full_spec((N_META, D)), _full_spec((1, D)), _full_spec((1, D))],
        scratch_shapes=[pltpu.VMEM(wg.shape, wg.dtype), pltpu.VMEM(wu.shape, wu.dtype), pltpu.VMEM(wd.shape, wd.dtype), pltpu.SemaphoreType.DMA((3,)), pltpu.VMEM((tm - N_META, D), F32)],
        operands=(dh, f, ga, si, x, x, meta, g_post, g_pre, wg, wu, wd),
    )
    return outs


def _token_tile(T):
    for t in (912, 864, 432):
        if T % t == 0:
            return t
    raise ValueError(f"no token tile for {T} rows")


def _tn_matmul(xm, ym, name, exchange=None):
    T, M = xm.shape
    N = ym.shape[1]
    if (T - N_META) % TN_TILE:
        tk = _token_tile(T)

        def body(x_ref, y_ref, o_ref):
            _acc(o_ref, _mm_tn(x_ref[...], y_ref[...]), pl.program_id(0) == 0)

        grid, operands = (T // tk,), (xm, ym)
        in_specs = [pl.BlockSpec((tk, M), lambda k: (k, 0)), pl.BlockSpec((tk, N), lambda k: (k, 0))]
    else:
        tk = TN_TILE

        def body(x_ref, y_ref, xh_ref, yh_ref, o_ref):
            prod = _mm_tn(x_ref[...], y_ref[...])

            @pl.when(pl.program_id(0) == 0)
            def _():
                o_ref[...] = prod + _mm_tn(xh_ref[...], yh_ref[...])

            @pl.when(pl.program_id(0) > 0)
            def _():
                o_ref[...] += prod

        grid, operands = ((T - N_META) // tk,), (xm, ym, xm, ym)
        start = lambda k: (pl.multiple_of(N_META + k * tk, N_META), 0)
        in_specs = [pl.BlockSpec((pl.Element(tk), pl.Element(M)), start), pl.BlockSpec((pl.Element(tk), pl.Element(N)), start),
                    pl.BlockSpec((N_META, M), lambda k: (0, 0)), pl.BlockSpec((N_META, N), lambda k: (0, 0))]

    (out,), x_outs = _pallas(
        body,
        name=name,
        grid=grid,
        out_shape=[jax.ShapeDtypeStruct((M, N), F32)],
        in_specs=in_specs,
        out_specs=[_full_spec((M, N))],
        operands=operands,
        exchange=exchange,
    )
    return out, x_outs


TAB_A, TAB_AS1, TAB_AS2, TAB_AS4, TAB_JF, TAB_JB = 0, 2, 4, 6, 8, 10


def _scan_inplace(zr, zi, tabs, pows, car_r, car_i, seg, reverse, base=0):
    n_slabs = zr.shape[0]
    sgn = -1.0 if reverse else 1.0
    row = lax.broadcasted_iota(jnp.int32, (SUBLANES, LANES), 0)

    def cmul(pr, pi, xr, xi):
        return pr * xr - pi * xi, pr * xi + pi * xr

    for k0 in range(0, n_slabs, SLAB_GROUP):
        slabs = range(k0, min(k0 + SLAB_GROUP, n_slabs))
        ar = [tabs[TAB_A, k] for k in slabs]
        ai = [sgn * tabs[TAB_A + 1, k] for k in slabs]

        def first_pass(t, carry):
            r = (seg - 1 - t) if reverse else t
            out = []
            for q, k in enumerate(slabs):
                xr, xi = carry[2 * q], carry[2 * q + 1]
                pr, pi = cmul(ar[q], ai[q], xr, xi)
                nr = pr + zr[k, pl.ds(base + r, SUBLANES, stride=seg), :]
                ni = pi + zi[k, pl.ds(base + r, SUBLANES, stride=seg), :]
                zr[k, pl.ds(base + r, SUBLANES, stride=seg), :] = nr
                zi[k, pl.ds(base + r, SUBLANES, stride=seg), :] = ni
                out += [nr, ni]
            return tuple(out)

        ends = lax.fori_loop(0, seg, first_pass, tuple(jnp.zeros((SUBLANES, LANES), F32) for _ in range(2 * len(slabs))))

        incoming = []
        for q, k in enumerate(slabs):
            fr, fi = ends[2 * q], ends[2 * q + 1]
            for d, tab in ((1, TAB_AS1), (2, TAB_AS2), (4, TAB_AS4)):
                shift, keep = (SUBLANES - d, row < SUBLANES - d) if reverse else (d, row >= d)
                sr = jnp.where(keep, pltpu.roll(fr, shift, 0), 0.0)
                si = jnp.where(keep, pltpu.roll(fi, shift, 0), 0.0)
                pr, pi = cmul(tabs[tab, k], sgn * tabs[tab + 1, k], sr, si)
                fr, fi = fr + pr, fi + pi
            cr, ci = car_r[k], car_i[k]
            jtab = TAB_JB if reverse else TAB_JF
            pr, pi = cmul(tabs[jtab, k], sgn * tabs[jtab + 1, k], cr, ci)
            er, ei = fr + pr, fi + pi
            if reverse:
                inr = jnp.where(row < SUBLANES - 1, pltpu.roll(er, SUBLANES - 1, 0), cr)
                ini = jnp.where(row < SUBLANES - 1, pltpu.roll(ei, SUBLANES - 1, 0), ci)
                car_r[k] = jnp.broadcast_to(er[0:1, :], (SUBLANES, LANES))
                car_i[k] = jnp.broadcast_to(ei[0:1, :], (SUBLANES, LANES))
            else:
                inr = jnp.where(row >= 1, pltpu.roll(er, 1, 0), cr)
                ini = jnp.where(row >= 1, pltpu.roll(ei, 1, 0), ci)
                car_r[k] = jnp.broadcast_to(er[SUBLANES - 1 : SUBLANES, :], (SUBLANES, LANES))
                car_i[k] = jnp.broadcast_to(ei[SUBLANES - 1 : SUBLANES, :], (SUBLANES, LANES))
            incoming += [inr, ini]

        def second_pass(r, _):
            p = (seg - 1 - r) if reverse else r
            for q, k in enumerate(slabs):
                pr, pi = cmul(pows[0, k, p], sgn * pows[1, k, p], incoming[2 * q], incoming[2 * q + 1])
                zr[k, pl.ds(base + r, SUBLANES, stride=seg), :] = zr[k, pl.ds(base + r, SUBLANES, stride=seg), :] + pr
                zi[k, pl.ds(base + r, SUBLANES, stride=seg), :] = zi[k, pl.ds(base + r, SUBLANES, stride=seg), :] + pi
            return 0

        lax.fori_loop(0, seg, second_pass, 0)


def _slabs_to_cols(ref, k0, n):
    return jnp.concatenate([ref[k0 + q] for q in range(n)], axis=1)


def _window_sum(ext, doublings, forward):
    rows = ext.shape[0]
    s = ext
    for k in range(doublings):
        s = s + pltpu.roll(s, (1 << k) if forward else rows - (1 << k), 0)
    return s


def _mix_fwd(h1, g_pre, g_so, g_po, g_post, dskip, pscale, win, wout, bbre, bbim, ccre, ccim, wgv, wgg, pw, tabs, pows, name, exchange=None):
    T, D = h1.shape
    W = D // 2
    tm = MIX_SUBTILES * MIX_TILE
    seg = MIX_TILE // SUBLANES
    n_slabs = tabs.shape[1]
    nch, cch, sch = bbre.shape
    spc = sch // LANES
    pg = W // len(POOL_WINDOWS)

    def body(h_ref, gpre_ref, gso_ref, gpo_ref, gpost_ref, dskip_ref, pscale_ref, win_ref, wout_ref, bbre_ref, bbim_ref, ccre_ref, ccim_ref, wgv_ref, wgg_ref, pw_ref, tabs_ref, pows_ref,
             proj_ref, xr_ref, xi_ref, y_ref, pooled_ref, mixed_ref, h2_ref, n2_ref, cat_ref, car_r, car_i, halo):
        i = pl.program_id(0)

        @pl.when(i == 0)
        def _():
            car_r[...] = jnp.zeros_like(car_r)
            car_i[...] = jnp.zeros_like(car_i)
            halo[...] = jnp.zeros_like(halo)

        hh = h_ref[...]
        n2 = (hh * _rms_stat(hh) * gpre_ref[...]).astype(MXU_DTYPE)
        n2_ref[...] = n2.astype(n2_ref.dtype)
        proj = _mm(n2, win_ref[...])
        proj_ref[...] = proj
        us, up = proj[:, :W], proj[:, W:]

        for c in range(nch):
            uc = us[:, c * cch : (c + 1) * cch].astype(MXU_DTYPE)
            bur, bui = _mm(uc, bbre_ref[c]), _mm(uc, bbim_ref[c])
            for q in range(spc):
                xr_ref[c * spc + q] = bur[:, q * LANES : (q + 1) * LANES]
                xi_ref[c * spc + q] = bui[:, q * LANES : (q + 1) * LANES]
        for sub in range(MIX_SUBTILES):
            _scan_inplace(xr_ref, xi_ref, tabs_ref, pows_ref, car_r, car_i, seg, reverse=False, base=sub * MIX_TILE)
        ys = []
        for c in range(nch):
            ys.append(_mm(_slabs_to_cols(xr_ref, c * spc, spc), ccre_ref[c]) - _mm(_slabs_to_cols(xi_ref, c * spc, spc), ccim_ref[c]))
        y = jnp.concatenate(ys, axis=1) + dskip_ref[...] * us
        y_ref[...] = y
        ge = _gelu(y).astype(MXU_DTYPE)
        zv = jnp.concatenate([_mm(ge[:, c * cch : (c + 1) * cch], wgv_ref[c]) for c in range(nch)], axis=1)
        zg = jnp.concatenate([_mm(ge[:, c * cch : (c + 1) * cch], wgg_ref[c]) for c in range(nch)], axis=1)
        out = zv * _sigmoid(zg)
        cat_s = out * _rms_stat(out) * gso_ref[...]

        ext = jnp.concatenate([halo[...], up], axis=0)
        halo[...] = up[tm - POOL_HALO :, :]
        t1 = (i * tm + 1 + lax.broadcasted_iota(jnp.int32, (tm, pg), 0)).astype(F32)
        pooled, pms = [], []
        for g, w in enumerate(POOL_WINDOWS):
            col = ext[:, g * pg : (g + 1) * pg]
            win_sum = _window_sum(col, g + 1, True)[POOL_HALO:, :]
            pooled_g = win_sum / jnp.minimum(t1, float(w)) - up[:, g * pg : (g + 1) * pg]
            pooled.append(pooled_g)
            pms.append(_mm(pooled_g, pw_ref[g]))
        pooled_ref[...] = jnp.concatenate(pooled, axis=1)
        yp = jnp.concatenate(pms, axis=1) * pscale_ref[...]
        cat_p = yp * _rms_stat(yp) * gpo_ref[...]

        cat = jnp.concatenate([cat_s, cat_p], axis=1).astype(MXU_DTYPE)
        cat_ref[...] = cat.astype(cat_ref.dtype)
        mixed = _mm(cat, wout_ref[...])
        mixed_ref[...] = mixed
        h2_ref[...] = hh + mixed * _rms_stat(mixed) * gpost_ref[...]

    tok = lambda cols, dt=F32: jax.ShapeDtypeStruct((T, cols), dt)
    slab_spec = pl.BlockSpec((n_slabs, tm, LANES), lambda i: (0, i, 0))
    operands = (h1, g_pre, g_so, g_po, g_post, dskip, pscale, win, wout, bbre, bbim, ccre, ccim, wgv, wgg, pw, tabs, pows)
    return _pallas(
        body,
        name=name,
        grid=(T // tm,),
        out_shape=[tok(D), jax.ShapeDtypeStruct((n_slabs, T, LANES), F32), jax.ShapeDtypeStruct((n_slabs, T, LANES), F32), tok(W), tok(W), tok(D), tok(D), tok(D, MXU_DTYPE), tok(D, MXU_DTYPE)],
        in_specs=[_row_spec(tm, D)] + [_full_spec(o.shape, single=True) for o in operands[1:]],
        out_specs=[_row_spec(tm, D), slab_spec, slab_spec, _row_spec(tm, W), _row_spec(tm, W), _row_spec(tm, D), _row_spec(tm, D), _row_spec(tm, D), _row_spec(tm, D)],
        scratch_shapes=[pltpu.VMEM((n_slabs, SUBLANES, LANES), F32), pltpu.VMEM((n_slabs, SUBLANES, LANES), F32), pltpu.VMEM((POOL_HALO, W), F32)],
        operands=operands,
        exchange=exchange,
    )


def _mix_bwd_heads(dh2, mixed, y, pooled, proj, g_so, g_po, g_post, pscale, wout, wgv, wgg, pw, name, exchange=None):
    T, D = dh2.shape
    W = D // 2
    tm = _token_tile(T)
    nch, cch, _ = wgv.shape
    ng, pg, _ = pw.shape

    def body(dh2_ref, mixed_ref, y_ref, pooled_ref, us_ref, gso_ref, gpo_ref, gpost_ref, pscale_ref, wout_ref, wgv_ref, wgg_ref, pw_ref,
             dy_ref, dpooled_ref, dmixed_ref, dgpost_ref, dgso_ref, dgpo_ref, dd_ref, dscale_ref, dwgv_ref, dwgg_ref, dpw_ref):
        first = pl.program_id(0) == 0
        dmixed, dgpost = _rms_bwd(mixed_ref[...], gpost_ref[...], dh2_ref[...])
        _acc(dgpost_ref, dgpost, first)
        dmb = dmixed.astype(MXU_DTYPE)
        dmixed_ref[...] = dmb.astype(dmixed_ref.dtype)
        dcat = _mm_nt(dmb, wout_ref[...])
        dcs, dcp = dcat[:, :W], dcat[:, W:]

        y = y_ref[...]
        ge = _gelu(y).astype(MXU_DTYPE)
        zv = jnp.concatenate([_mm(ge[:, c * cch : (c + 1) * cch], wgv_ref[c]) for c in range(nch)], axis=1)
        zg = jnp.concatenate([_mm(ge[:, c * cch : (c + 1) * cch], wgg_ref[c]) for c in range(nch)], axis=1)
        sg = _sigmoid(zg)
        dout, dgso = _rms_bwd(zv * sg, gso_ref[...], dcs)
        _acc(dgso_ref, dgso, first)
        dzv = (dout * sg).astype(MXU_DTYPE)
        dzg = (dout * zv * sg * (1.0 - sg)).astype(MXU_DTYPE)
        dges = []
        for c in range(nch):
            cs = slice(c * cch, (c + 1) * cch)
            dges.append(_mm_nt(dzv[:, cs], wgv_ref[c]) + _mm_nt(dzg[:, cs], wgg_ref[c]))
            _acc(dwgv_ref.at[c], _mm_tn(ge[:, cs], dzv[:, cs]), first)
            _acc(dwgg_ref.at[c], _mm_tn(ge[:, cs], dzg[:, cs]), first)
        dy = jnp.concatenate(dges, axis=1) * _gelu_grad(y)
        dy_ref[...] = dy
        _acc(dd_ref, jnp.sum(dy * us_ref[...], axis=0, keepdims=True), first)

        pooled_b = pooled_ref[...].astype(MXU_DTYPE)
        pm = jnp.concatenate([_mm(pooled_b[:, g * pg : (g + 1) * pg], pw_ref[g]) for g in range(ng)], axis=1)
        dyp, dgpo = _rms_bwd(pm * pscale_ref[...], gpo_ref[...], dcp)
        _acc(dgpo_ref, dgpo, first)
        _acc(dscale_ref, jnp.sum(dyp * pm, axis=0, keepdims=True), first)
        dpm = (dyp * pscale_ref[...]).astype(MXU_DTYPE)
        dps = []
        for g in range(ng):
            gs = slice(g * pg, (g + 1) * pg)
            dps.append(_mm_nt(dpm[:, gs], pw_ref[g]))
            _acc(dpw_ref.at[g], _mm_tn(pooled_b[:, gs], dpm[:, gs]), first)
        dpooled_ref[...] = jnp.concatenate(dps, axis=1)

    vec = lambda n: jax.ShapeDtypeStruct((1, n), F32)
    operands = (dh2, mixed, y, pooled, proj, g_so, g_po, g_post, pscale, wout, wgv, wgg, pw)
    return _pallas(
        body,
        name=name,
        grid=(T // tm,),
        out_shape=[jax.ShapeDtypeStruct((T, W), F32), jax.ShapeDtypeStruct((T, W), F32), jax.ShapeDtypeStruct((T, D), MXU_DTYPE), vec(D), vec(W), vec(W), vec(W), vec(W),
                   jax.ShapeDtypeStruct(wgv.shape, F32), jax.ShapeDtypeStruct(wgg.shape, F32), jax.ShapeDtypeStruct(pw.shape, F32)],
        in_specs=[_row_spec(tm, D), _row_spec(tm, D), _row_spec(tm, W), _row_spec(tm, W), _row_spec(tm, W)] + [_full_spec(o.shape) for o in operands[5:]],
        out_specs=[_row_spec(tm, W), _row_spec(tm, W), _row_spec(tm, D), _full_spec((1, D)), _full_spec((1, W)), _full_spec((1, W)), _full_spec((1, W)), _full_spec((1, W)),
                   _full_spec(wgv.shape), _full_spec(wgg.shape), _full_spec(pw.shape)],
        operands=operands,
        exchange=exchange,
    )


def _mix_bwd_scan(dy, dpooled, xr, xi, proj, dskip, bbre, bbim, ccre, ccim, tabs, pows, name, exchange=None):
    T, W = dy.shape
    D = 2 * W
    tm = MIX_SUBTILES * MIX_TILE
    seg = MIX_TILE // SUBLANES
    nt = T // tm
    n_slabs = tabs.shape[1]
    nch, cch, sch = bbre.shape
    spc = sch // LANES
    pg = W // len(POOL_WINDOWS)
    blocks_per_tile = tm // SUBLANES

    def body(dy_ref, dp_ref, xr_ref, xi_ref, xpr_ref, xpi_ref, us_ref, dskip_ref, bbre_ref, bbim_ref, ccre_ref, ccim_ref, tabs_ref, pows_ref,
             dproj_ref, dccre_ref, dccim_ref, dbbre_ref, dbbim_ref, dar_ref, dai_ref, lr, li, car_r, car_i, halo):
        i = pl.program_id(0)
        first = i == 0
        tile = nt - 1 - i
        row = lax.broadcasted_iota(jnp.int32, (SUBLANES, LANES), 0)

        @pl.when(first)
        def _():
            car_r[...] = jnp.zeros_like(car_r)
            car_i[...] = jnp.zeros_like(car_i)
            halo[...] = jnp.zeros_like(halo)
            dar_ref[...] = jnp.zeros_like(dar_ref)
            dai_ref[...] = jnp.zeros_like(dai_ref)

        dy = dy_ref[...]
        for c in range(nch):
            dyc = dy[:, c * cch : (c + 1) * cch]
            gr, gi = _mm_nt(dyc, ccre_ref[c]), _mm_nt(dyc, ccim_ref[c])
            for q in range(spc):
                lr[c * spc + q] = gr[:, q * LANES : (q + 1) * LANES]
                li[c * spc + q] = -gi[:, q * LANES : (q + 1) * LANES]
            _acc(dccre_ref.at[c], _mm_tn(_slabs_to_cols(xr_ref, c * spc, spc), dyc), first)
            _acc(dccim_ref.at[c], -_mm_tn(_slabs_to_cols(xi_ref, c * spc, spc), dyc), first)
        for sub in reversed(range(MIX_SUBTILES)):
            _scan_inplace(lr, li, tabs_ref, pows_ref, car_r, car_i, seg, reverse=True, base=sub * MIX_TILE)

        for sub in range(MIX_SUBTILES):
            base = sub * MIX_TILE
            for k0 in range(0, n_slabs, SLAB_GROUP):
                slabs = range(k0, min(k0 + SLAB_GROUP, n_slabs))
                init = []
                for k in slabs:
                    if sub == 0:
                        prev_r = jnp.where(tile > 0, jnp.broadcast_to(xpr_ref[k, SUBLANES - 1 : SUBLANES, :], (SUBLANES, LANES)), 0.0)
                        prev_i = jnp.where(tile > 0, jnp.broadcast_to(xpi_ref[k, SUBLANES - 1 : SUBLANES, :], (SUBLANES, LANES)), 0.0)
                    else:
                        prev_r = jnp.broadcast_to(xr_ref[k, base - 1 : base, :], (SUBLANES, LANES))
                        prev_i = jnp.broadcast_to(xi_ref[k, base - 1 : base, :], (SUBLANES, LANES))
                    x0r = jnp.where(row >= 1, pltpu.roll(xr_ref[k, pl.ds(base + seg - 1, SUBLANES, stride=seg), :], 1, 0), prev_r)
                    x0i = jnp.where(row >= 1, pltpu.roll(xi_ref[k, pl.ds(base + seg - 1, SUBLANES, stride=seg), :], 1, 0), prev_i)
                    l0r, l0i = lr[k, pl.ds(base, SUBLANES, stride=seg), :], li[k, pl.ds(base, SUBLANES, stride=seg), :]
                    init += [l0r * x0r + l0i * x0i, l0i * x0r - l0r * x0i]

                def step(r, acc, slabs=slabs, base=base):
                    out = []
                    for q, k in enumerate(slabs):
                        pr_, pi_ = xr_ref[k, pl.ds(base + r - 1, SUBLANES, stride=seg), :], xi_ref[k, pl.ds(base + r - 1, SUBLANES, stride=seg), :]
                        lr_, li_ = lr[k, pl.ds(base + r, SUBLANES, stride=seg), :], li[k, pl.ds(base + r, SUBLANES, stride=seg), :]
                        out += [acc[2 * q] + lr_ * pr_ + li_ * pi_, acc[2 * q + 1] + li_ * pr_ - lr_ * pi_]
                    return tuple(out)

                sums = lax.fori_loop(1, seg, step, tuple(init))
                for q, k in enumerate(slabs):
                    dar_ref[k] += sums[2 * q]
                    dai_ref[k] += sums[2 * q + 1]

        us = us_ref[...]
        dus = []
        for c in range(nch):
            lrc, lic = _slabs_to_cols(lr, c * spc, spc).astype(MXU_DTYPE), _slabs_to_cols(li, c * spc, spc).astype(MXU_DTYPE)
            uc = us[:, c * cch : (c + 1) * cch]
            _acc(dbbre_ref.at[c], _mm_tn(uc, lrc), first)
            _acc(dbbim_ref.at[c], _mm_tn(uc, lic), first)
            dus.append(_mm_nt(lrc, bbre_ref[c]) + _mm_nt(lic, bbim_ref[c]))
        du_s = jnp.concatenate(dus, axis=1) + dskip_ref[...] * dy

        dp = dp_ref[...]
        t1 = (tile * tm + 1 + lax.broadcasted_iota(jnp.int32, (tm, pg), 0)).astype(F32)
        dups, heads = [], []
        for g, w in enumerate(POOL_WINDOWS):
            dpg = dp[:, g * pg : (g + 1) * pg]
            qg = dpg / jnp.minimum(t1, float(w))
            ext = jnp.concatenate([qg, halo[:, g * pg : (g + 1) * pg]], axis=0)
            dups.append(_window_sum(ext, g + 1, False)[:tm, :] - dpg)
            heads.append(qg[:POOL_HALO, :])
        halo[...] = jnp.concatenate(heads, axis=1)
        dproj_ref[...] = jnp.concatenate([du_s] + dups, axis=1).astype(dproj_ref.dtype)

    rev = lambda cols: _row_spec(tm, cols, rev_n=nt)
    slab_spec = pl.BlockSpec((n_slabs, tm, LANES), lambda i: (0, nt - 1 - i, 0))
    prev_spec = pl.BlockSpec((n_slabs, SUBLANES, LANES), lambda i: (0, jnp.maximum((nt - 1 - i) * blocks_per_tile - 1, 0), 0))
    consts = (dskip, bbre, bbim, ccre, ccim, tabs, pows)
    return _pallas(
        body,
        name=name,
        grid=(nt,),
        out_shape=[jax.ShapeDtypeStruct((T, D), MXU_DTYPE), jax.ShapeDtypeStruct(ccre.shape, F32), jax.ShapeDtypeStruct(ccim.shape, F32), jax.ShapeDtypeStruct(bbre.shape, F32),
                   jax.ShapeDtypeStruct(bbim.shape, F32), jax.ShapeDtypeStruct((n_slabs, SUBLANES, LANES), F32), jax.ShapeDtypeStruct((n_slabs, SUBLANES, LANES), F32)],
        in_specs=[rev(W), rev(W), slab_spec, slab_spec, prev_spec, prev_spec, rev(W)] + [_full_spec(o.shape, single=True) for o in consts],
        out_specs=[rev(D), _full_spec(ccre.shape), _full_spec(ccim.shape), _full_spec(bbre.shape), _full_spec(bbim.shape),
                   _full_spec((n_slabs, SUBLANES, LANES)), _full_spec((n_slabs, SUBLANES, LANES))],
        scratch_shapes=[pltpu.VMEM((n_slabs, tm, LANES), F32), pltpu.VMEM((n_slabs, tm, LANES), F32), pltpu.VMEM((n_slabs, SUBLANES, LANES), F32), pltpu.VMEM((n_slabs, SUBLANES, LANES), F32),
                        pltpu.VMEM((POOL_HALO, W), F32)],
        operands=(dy, dpooled, xr, xi, xr, xi, proj, *consts),
        exchange=exchange,
    )


def _mix_bwd_in(dproj, h1, dh2, g_pre, win, name, exchange=None):
    T, D = h1.shape
    tm = _token_tile(T)

    def body(dproj_ref, h_ref, dh2_ref, gpre_ref, win_ref, dh1_ref, dg_ref):
        dx, dg = _rms_bwd(h_ref[...], gpre_ref[...], _mm_nt(dproj_ref[...], win_ref[...]))
        _acc(dg_ref, dg, pl.program_id(0) == 0)
        dh1_ref[...] = dh2_ref[...] + dx

    return _pallas(
        body,
        name=name,
        grid=(T // tm,),
        out_shape=[jax.ShapeDtypeStruct((T, D), F32), jax.ShapeDtypeStruct((1, D), F32)],
        in_specs=[_row_spec(tm, D), _row_spec(tm, D), _row_spec(tm, D), _full_spec((1, D)), _full_spec(win.shape)],
        out_specs=[_row_spec(tm, D), _full_spec((1, D))],
        operands=(dproj, h1, dh2, g_pre, win),
        exchange=exchange,
    )


def _discretize(lam_re, lam_im, log_dt, b_re, b_im):
    dt = jnp.exp(log_dt)[:, None]
    decay = jnp.exp(lam_re * dt)
    ang = lam_im * dt
    a_re, a_im = decay * jnp.cos(ang), decay * jnp.sin(ang)
    nr = a_re - 1.0
    den = lam_re * lam_re + lam_im * lam_im
    q_re = (nr * lam_re + a_im * lam_im) / den
    q_im = (a_im * lam_re - nr * lam_im) / den
    bb_re = q_re[..., None] * b_re - q_im[..., None] * b_im
    bb_im = q_re[..., None] * b_im + q_im[..., None] * b_re
    return a_re, a_im, bb_re, bb_im


GROUPS_PER_CHUNK = 16


def _block_diag(w, rows_first):
    G = w.shape[0]
    nch = G // GROUPS_PER_CHUNK
    if not rows_first:
        w = jnp.swapaxes(w, 1, 2)
    p, q = w.shape[1], w.shape[2]
    eye = jnp.eye(GROUPS_PER_CHUNK, dtype=w.dtype)
    out = jnp.einsum("cgpq,gk->cgpkq", w.reshape(nch, GROUPS_PER_CHUNK, p, q), eye)
    return out.reshape(nch, GROUPS_PER_CHUNK * p, GROUPS_PER_CHUNK * q)


def _block_diag_extract(m, p, q, rows_first):
    nch = m.shape[0]
    eye = jnp.eye(GROUPS_PER_CHUNK, dtype=m.dtype)
    out = jnp.einsum("cgpkq,gk->cgpq", m.reshape(nch, GROUPS_PER_CHUNK, p, GROUPS_PER_CHUNK, q), eye).reshape(nch * GROUPS_PER_CHUNK, p, q)
    return out if rows_first else jnp.swapaxes(out, 1, 2)


def _cmul(ar, ai, br, bi):
    return ar * br - ai * bi, ar * bi + ai * br


def _powers(ar, ai, count):
    pr, pi = ar[None], ai[None]
    while pr.shape[0] < count:
        nr, ni = _cmul(pr, pi, pr[-1][None], pi[-1][None])
        pr, pi = jnp.concatenate([pr, nr]), jnp.concatenate([pi, ni])
    return pr[:count], pi[:count]


def _scan_tables(a_re, a_im, seg):
    n = a_re.size
    ns = n // LANES
    ar, ai = a_re.reshape(n), a_im.reshape(n)
    pr, pi = _powers(ar, ai, seg)
    jr, ji = _powers(pr[-1], pi[-1], SUBLANES)

    def bcast(v):
        return jnp.broadcast_to(v.reshape(ns, 1, LANES), (ns, SUBLANES, LANES))

    def per_sublane(vs):
        return jnp.transpose(vs.reshape(SUBLANES, ns, LANES), (1, 0, 2))

    tabs = jnp.stack([bcast(ar), bcast(ai), bcast(jr[0]), bcast(ji[0]), bcast(jr[1]), bcast(ji[1]), bcast(jr[3]), bcast(ji[3]),
                      per_sublane(jr), per_sublane(ji), per_sublane(jr[::-1]), per_sublane(ji[::-1])])

    def rows(vs):
        return jnp.broadcast_to(jnp.transpose(vs.reshape(seg, ns, 1, LANES), (1, 0, 2, 3)), (ns, seg, SUBLANES, LANES))

    return tabs, jnp.stack([rows(pr), rows(pi)])


SMALL = ("ffn1_pre_norm", "ffn1_post_norm", "mix_pre_norm", "mix_post_norm", "ssm_lambda_re", "ssm_lambda_im", "ssm_log_dt", "ssm_b_re", "ssm_b_im", "ssm_c_re", "ssm_c_im",
         "ssm_d", "ssm_w_glu", "pool_w", "pool_scale", "ssm_out_norm", "pool_out_norm", "ffn2_pre_norm", "ffn2_post_norm")
BIG = ("ffn1_w_gate", "ffn1_w_up", "ffn1_w_down", "w_in", "w_out", "ffn2_w_gate", "ffn2_w_up", "ffn2_w_down")
ORDER = ("meta_tokens", "ffn1_pre_norm", "ffn1_post_norm", "ffn1_w_gate", "ffn1_w_up", "ffn1_w_down", "mix_pre_norm", "mix_post_norm", "w_in", "ssm_lambda_re", "ssm_lambda_im",
         "ssm_log_dt", "ssm_b_re", "ssm_b_im", "ssm_c_re", "ssm_c_im", "ssm_d", "ssm_w_glu", "pool_w", "pool_scale", "ssm_out_norm", "pool_out_norm", "w_out", "ffn2_pre_norm",
         "ffn2_post_norm", "ffn2_w_gate", "ffn2_w_up", "ffn2_w_down")
PACK_ROWS = SUBLANES * 8
def _pack(arrays, rows):
    flat = jnp.concatenate([a.reshape(-1) for a in arrays])
    return jnp.pad(flat, (0, rows * LANES - flat.size)).reshape(rows, LANES)


def _unpack(packed, shapes):
    flat = packed.reshape(-1)
    out, off = [], 0
    for s in shapes:
        n = math.prod(s)
        out.append(flat[off : off + n].reshape(s))
        off += n
    return out


def _step(p, x, loss_target, m, v):
    D = x.shape[-1]
    chip = (2 * lax.axis_index("x") + lax.axis_index("y")).astype(jnp.int32)
    place = jnp.stack([chip, lax.axis_index("c").astype(jnp.int32)])

    def gather_buffer(w):
        own = w.reshape(1, 2, w.shape[0] // 2, w.shape[1])
        return lax.dynamic_update_slice(lax.empty((4,) + own.shape[1:], own.dtype), own, (chip, 0, 0, 0))

    def rows_of(n, a):
        return jnp.swapaxes(a[0], 0, 1) if n.endswith(("w_gate", "w_up")) else a[0]

    def rows_back(n, a):
        return (jnp.swapaxes(a, 0, 1) if n.endswith(("w_gate", "w_up")) else a)[None]

    def grad_view(g):
        return g.reshape(4, 2, g.shape[0] // 8, g.shape[1])

    def reduce_sum(got_sibling, views, tag, wires=None):
        wires = wires or [WIRE_DTYPE] * len(views)
        sums = [_add_own_half(v_, g_, place, f"{tag}_add_sibling_{k}", w_) for k, (v_, g_, w_) in enumerate(zip(views, got_sibling, wires))]
        return [s[0] for s in sums], [s[1] for s in sums]

    def reduce_halves(parts, got_chips, tag):
        return [_add_chips(p_, g_, place, f"{tag}_add_chips_{k}") for k, (p_, g_) in enumerate(zip(parts, got_chips))]

    bufs = {n: gather_buffer(rows_of(n, p[n]).astype(MXU_DTYPE)) for n in BIG}
    full = {}

    def gathered(names, got):
        full.update({n: g_.reshape(-1, g_.shape[-1]) for n, g_ in zip(names, got)})

    def gather_of(names, n_steps, tenths):
        sizes = [bufs[n].size * bufs[n].dtype.itemsize for n in names]
        steps = [min(n_steps - 1, max(1, -(-tenths * n_steps * sum(sizes[: a + 1]) // (10 * sum(sizes))))) for a in range(len(names))]
        return _Gather([bufs[n] for n in names], steps=steps)

    first_names = ("ffn1_w_gate", "ffn1_w_up")
    got = _exchange_call(_Gather([bufs[n] for n in first_names] + [gather_buffer(p["meta_tokens"])]), "gather_first")
    gathered(first_names, got)
    meta = jnp.transpose(got[-1].reshape(4, N_META, -1), (1, 0, 2)).reshape(N_META, D)

    vec = lambda n: p[n].reshape(1, -1)
    G, N, H = p["ssm_b_re"].shape[1:]
    a_re, a_im, bb_re, bb_im = _discretize(p["ssm_lambda_re"][0], p["ssm_lambda_im"][0], p["ssm_log_dt"][0], p["ssm_b_re"][0], p["ssm_b_im"][0])
    tabs, pows = _scan_tables(a_re, a_im, MIX_TILE // SUBLANES)
    bf = lambda a: a.astype(MXU_DTYPE)
    bbre, bbim = bf(_block_diag(bb_re, False)), bf(_block_diag(bb_im, False))
    ccre, ccim = bf(_block_diag(p["ssm_c_re"][0], False)), bf(_block_diag(p["ssm_c_im"][0], False))
    wgv, wgg = bf(_block_diag(p["ssm_w_glu"][0][:, :, :H], True)), bf(_block_diag(p["ssm_w_glu"][0][:, :, H:], True))
    pw = bf(p["pool_w"][0])

    T = x.shape[1] + N_META
    names = ("ffn1_w_down", "w_in", "w_out", "ffn2_w_gate", "ffn2_w_down")
    (ga1, si1, s1, n1), got = _ffn_gate_up(
        x[0], meta, vec("ffn1_pre_norm"), full["ffn1_w_gate"], full["ffn1_w_up"], "ffn1_gate_up", exchange=gather_of(names, T // FFN_TILE, 9)
    )
    gathered(names, got)
    (h1, f1), got = _ffn_down(x[0], meta, s1, vec("ffn1_post_norm"), full["ffn1_w_down"], "ffn1_down", exchange=gather_of(("ffn2_w_up",), T // _token_tile(T), 6))
    gathered(("ffn2_w_up",), got)
    (proj, xr, xim, y, pooled, mixed, h2, n2, cat), _ = _mix_fwd(
        h1, vec("mix_pre_norm"), vec("ssm_out_norm"), vec("pool_out_norm"), vec("mix_post_norm"), vec("ssm_d"), vec("pool_scale"), full["w_in"], full["w_out"],
        bbre, bbim, ccre, ccim, wgv, wgg, pw, tabs, pows, "mix_fwd",
    )
    dh3, sq, f2, ga2, si2, s2, n3 = _ffn_fwd_loss(
        h2, loss_target[0], vec("ffn2_pre_norm"), vec("ffn2_post_norm"), full["ffn2_w_gate"], full["ffn2_w_up"], full["ffn2_w_down"], "ffn2_fwd"
    )

    g, shared = {}, {}
    ffn_names = lambda tag: (tag + "_w_gate", tag + "_w_up", tag + "_w_down")

    da, db, df, dh2, g["ffn2_post_norm"], g["ffn2_pre_norm"] = _ffn_bwd(
        dh3, f2, ga2, si2, h2, vec("ffn2_post_norm"), vec("ffn2_pre_norm"), full["ffn2_w_gate"], full["ffn2_w_up"], full["ffn2_w_down"], "ffn2_bwd"
    )
    views2 = [
        grad_view(_tn_matmul(da, n3, "ffn2_dw_gate")[0]),
        grad_view(_tn_matmul(db, n3, "ffn2_dw_up")[0]),
        grad_view(_tn_matmul(s2, df, "ffn2_dw_down")[0]),
    ]
    (dy, dpooled, dmixed, g["mix_post_norm"], g["ssm_out_norm"], g["pool_out_norm"], g["ssm_d"], g["pool_scale"], dwgv, dwgg, g["pool_w"]), got = _mix_bwd_heads(
        dh2, mixed, y, pooled, proj, vec("ssm_out_norm"), vec("pool_out_norm"), vec("mix_post_norm"), vec("pool_scale"), full["w_out"], wgv, wgg, pw, "mix_bwd_heads",
        exchange=_SiblingScatter(views2),
    )
    parts2, wire2 = reduce_sum(got, views2, "ffn2")
    (dproj, dccre, dccim, dbbre, dbbim, dar, dai), got = _mix_bwd_scan(
        dy, dpooled, xr, xim, proj, vec("ssm_d"), bbre, bbim, ccre, ccim, tabs, pows, "mix_bwd_scan", exchange=_ChipScatter(wire2)
    )
    halves2 = reduce_halves(parts2, got, "ffn2")
    (dh1, g["mix_pre_norm"]), got = _mix_bwd_in(dproj, h1, dh2, vec("mix_pre_norm"), full["w_in"], "mix_bwd_in", exchange=_SiblingShare(halves2))
    shared.update(zip(ffn_names("ffn2"), got))

    da, db, df, grad_x, d_meta, g["ffn1_post_norm"], g["ffn1_pre_norm"] = _ffn_bwd_first(
        dh1, f1, ga1, si1, x[0], meta, vec("ffn1_post_norm"), vec("ffn1_pre_norm"), full["ffn1_w_gate"], full["ffn1_w_up"], full["ffn1_w_down"], "ffn1_bwd"
    )
    grad_x = grad_x[None]

    g["ssm_c_re"] = _block_diag_extract(dccre, N, H, False)
    g["ssm_c_im"] = _block_diag_extract(dccim, N, H, False)
    g["ssm_w_glu"] = jnp.concatenate([_block_diag_extract(dwgv, H, H, True), _block_diag_extract(dwgg, H, H, True)], axis=-1)
    d_a_re, d_a_im = jnp.sum(dar, axis=1).reshape(G, N), jnp.sum(dai, axis=1).reshape(G, N)
    _, pull = jax.vjp(_discretize, p["ssm_lambda_re"][0], p["ssm_lambda_im"][0], p["ssm_log_dt"][0], p["ssm_b_re"][0], p["ssm_b_im"][0])
    g["ssm_lambda_re"], g["ssm_lambda_im"], g["ssm_log_dt"], g["ssm_b_re"], g["ssm_b_im"] = pull(
        (d_a_re, d_a_im, _block_diag_extract(dbbre, H, N, False), _block_diag_extract(dbbim, H, N, False))
    )
    small_shapes = [p[n].shape for n in SMALL] + [(N_META, D), (1,)]
    small_size = sum(math.prod(s) for s in small_shapes)
    rows = -(-small_size // (LANES * PACK_ROWS)) * PACK_ROWS
    views_s = [_pack([g[n] for n in SMALL] + [d_meta, jnp.sum(sq).reshape(1)], rows).reshape(4, 2, rows // 8, LANES)]

    dw_down, got = _tn_matmul(s1, df, "ffn1_dw_down", exchange=_SiblingScatter(views_s))
    parts_s, wire_s = reduce_sum(got, views_s, "small", wires=[F32])
    views_d = [grad_view(dw_down)]
    ex = _Group([_SiblingScatter(views_d), _ChipScatter(wire_s)])
    dw_gate, got = _tn_matmul(da, n1, "ffn1_dw_gate", exchange=ex)
    got_d, got_s = ex.split(got)
    parts_d, wire_d = reduce_sum(got_d, views_d, "ffn1_down")
    halves_s = reduce_halves(parts_s, got_s, "small")
    views_g = [grad_view(dw_gate)]
    ex = _Group([_ChipScatter(wire_d), _SiblingScatter(views_g), _SiblingShare(halves_s)])
    dw_up, got = _tn_matmul(db, n1, "ffn1_dw_up", exchange=ex)
    got_d, got_g, got_s = ex.split(got)
    small_buf = lax.dynamic_update_slice(lax.empty((4,) + got_s[0].shape, F32), got_s[0][None], (chip, 0, 0, 0))
    halves_d = reduce_halves(parts_d, got_d, "ffn1_down")
    parts_g, wire_g = reduce_sum(got_g, views_g, "ffn1_gate")
    views_u = [grad_view(dw_up)]
    tn_steps = (T - N_META) // TN_TILE if (T - N_META) % TN_TILE == 0 else T // _token_tile(T)
    ex = _Group([_SiblingShare(halves_d), _ChipScatter(wire_g), _SiblingScatter(views_u), _Gather([small_buf], steps=[(3 * tn_steps) // 4])])
    dw_in, got = _tn_matmul(n2, dproj, "dw_in", exchange=ex)
    got_d, got_g, got_u, got_s = ex.split(got)
    grads = dict(zip(SMALL + ("meta_full", "sq_sum"), _unpack(got_s[0].reshape(rows, LANES), small_shapes)))
    grads["meta_tokens"] = lax.dynamic_slice_in_dim(grads.pop("meta_full"), chip * (D // 4), D // 4, axis=1)
    loss = (0.5 / D) * grads.pop("sq_sum")[0]
    shared["ffn1_w_down"] = got_d[0]
    halves_g = reduce_halves(parts_g, got_g, "ffn1_gate")
    parts_u, wire_u = reduce_sum(got_u, views_u, "ffn1_up")
    views_i = [grad_view(dw_in)]
    ex = _Group([_SiblingShare(halves_g), _ChipScatter(wire_u), _SiblingScatter(views_i)])
    dw_out, got = _tn_matmul(cat, dmixed, "dw_out", exchange=ex)
    got_g, got_u, got_i = ex.split(got)
    shared["ffn1_w_gate"] = got_g[0]
    halves_u = reduce_halves(parts_u, got_u, "ffn1_up")
    parts_i, wire_i = reduce_sum(got_i, views_i, "w_in")

    views_t = [grad_view(dw_out)]
    ex = _Group([_SiblingScatter(views_t), _SiblingShare(halves_u)])
    got_t, got_u = ex.split(_exchange_call(ex, "tail_reduce_sibling"))
    shared["ffn1_w_up"] = got_u[0]
    parts_t, wire_t = reduce_sum(got_t, views_t, "tail")
    got = _exchange_call(_ChipScatter(wire_i + wire_t), "tail_reduce_chips")
    got = _exchange_call(_SiblingShare(reduce_halves(parts_i + parts_t, got, "tail")), "tail_reduce_share")
    shared["w_in"], shared["w_out"] = got[0], got[1]
    delta, new_m, new_v = {}, {}, {}
    for n in BIG:
        g_rows = shared[n].reshape(-1, shared[n].shape[-1])
        outs = _adamw(rows_of(n, p[n]), g_rows, rows_of(n, m[n]), rows_of(n, v[n]), "adamw_" + n)
        grads[n], delta[n], new_m[n], new_v[n] = (rows_back(n, a) for a in (g_rows, *outs))
    delta["meta_tokens"], new_m["meta_tokens"], new_v["meta_tokens"] = _adamw(p["meta_tokens"], grads["meta_tokens"], m["meta_tokens"], v["meta_tokens"], "adamw_meta_tokens")

    def as_2d(n, a):
        a = a.reshape(p[n].shape)[0]
        if n in ("ssm_b_re", "ssm_b_im"):
            a = jnp.swapaxes(a, 1, 2)
        return a.reshape(-1, a.shape[-1])

    def from_2d(n, a):
        if n in ("ssm_b_re", "ssm_b_im"):
            g_, n_, h_ = p[n].shape[1:]
            return jnp.swapaxes(a.reshape(g_, h_, n_), 1, 2)[None]
        return a.reshape(p[n].shape)

    outs = _adamw_many(*[[as_2d(n, t[n]) for n in SMALL] for t in (p, grads, m, v)], "adamw_small")
    for out, arrays in zip((delta, new_m, new_v), outs):
        out.update({n: from_2d(n, a) for n, a in zip(SMALL, arrays)})

    return (loss, grad_x, *[grads[n] for n in ORDER], *[delta[n] for n in ORDER], *[new_m[n] for n in ORDER], *[new_v[n] for n in ORDER])


def kernel(x, meta_tokens, ffn1_pre_norm, ffn1_post_norm, ffn1_w_gate, ffn1_w_up, ffn1_w_down, mix_pre_norm, mix_post_norm, w_in, ssm_lambda_re, ssm_lambda_im, ssm_log_dt, ssm_b_re, ssm_b_im, ssm_c_re, ssm_c_im, ssm_d, ssm_w_glu, pool_w, pool_scale, ssm_out_norm, pool_out_norm, w_out, ffn2_pre_norm, ffn2_post_norm, ffn2_w_gate, ffn2_w_up, ffn2_w_down, loss_target, m_meta_tokens, m_ffn1_pre_norm, m_ffn1_post_norm, m_ffn1_w_gate, m_ffn1_w_up, m_ffn1_w_down, m_mix_pre_norm, m_mix_post_norm, m_w_in, m_ssm_lambda_re, m_ssm_lambda_im, m_ssm_log_dt, m_ssm_b_re, m_ssm_b_im, m_ssm_c_re, m_ssm_c_im, m_ssm_d, m_ssm_w_glu, m_pool_w, m_pool_scale, m_ssm_out_norm, m_pool_out_norm, m_w_out, m_ffn2_pre_norm, m_ffn2_post_norm, m_ffn2_w_gate, m_ffn2_w_up, m_ffn2_w_down, v_meta_tokens, v_ffn1_pre_norm, v_ffn1_post_norm, v_ffn1_w_gate, v_ffn1_w_up, v_ffn1_w_down, v_mix_pre_norm, v_mix_post_norm, v_w_in, v_ssm_lambda_re, v_ssm_lambda_im, v_ssm_log_dt, v_ssm_b_re, v_ssm_b_im, v_ssm_c_re, v_ssm_c_im, v_ssm_d, v_ssm_w_glu, v_pool_w, v_pool_scale, v_ssm_out_norm, v_pool_out_norm, v_w_out, v_ffn2_pre_norm, v_ffn2_post_norm, v_ffn2_w_gate, v_ffn2_w_up, v_ffn2_w_down):
    args = locals()
    p = {n: args[n] for n in ORDER}
    m = {n: args["m_" + n] for n in ORDER}
    v = {n: args["v_" + n] for n in ORDER}
    return _step(p, x, loss_target, m, v)
```

```python
import math

import jax
import jax.numpy as jnp
from jax import lax
from jax.experimental import pallas as pl
from jax.experimental.pallas import tpu as pltpu

F32 = jnp.float32
MXU_DTYPE = jnp.bfloat16
WIRE_DTYPE = jnp.bfloat16

RMS_EPS = 1e-6
N_META = 16
POOL_WINDOWS = (2, 4, 8, 16)
POOL_HALO = 16
ADAM_LR, ADAM_B1, ADAM_B2, ADAM_EPS, ADAM_WD, ADAM_STEP = 0.001, 0.9, 0.999, 1e-08, 0.01, 10

LANES = 128
SUBLANES = 8
VMEM_LIMIT = 60 * 1024 * 1024
FFN_TILE = 432
FFN_CHUNK = 1024
TN_TILE = 1024
MIX_TILE = 216
MIX_SUBTILES = 2
SLAB_GROUP = 8
MESH = pl.DeviceIdType.MESH
ANY = pl.BlockSpec(memory_space=pl.ANY)


def _mm(a, b):
    return jnp.dot(a.astype(MXU_DTYPE), b.astype(MXU_DTYPE), preferred_element_type=F32)


def _mm_nt(a, b):
    return lax.dot_general(a.astype(MXU_DTYPE), b.astype(MXU_DTYPE), (((1,), (1,)), ((), ())), preferred_element_type=F32)


def _mm_tn(a, b):
    return lax.dot_general(a.astype(MXU_DTYPE), b.astype(MXU_DTYPE), (((0,), (0,)), ((), ())), preferred_element_type=F32)


def _rms_stat(x):
    return lax.rsqrt(jnp.mean(x * x, axis=-1, keepdims=True) + RMS_EPS)


def _rms_bwd(x, g, dy):
    r = _rms_stat(x)
    xh = x * r
    dg = jnp.sum(dy * xh, axis=0, keepdims=True)
    dxh = dy * g
    dx = r * (dxh - xh * jnp.mean(dxh * xh, axis=-1, keepdims=True))
    return dx, dg


def _sigmoid(x):
    return 1.0 / (1.0 + jnp.exp(-x))


GELU_C = math.sqrt(2.0 / math.pi)
GELU_K = 0.044715


def _gelu(y):
    return 0.5 * y * (1.0 + jnp.tanh(GELU_C * (y + GELU_K * y * y * y)))


def _gelu_grad(y):
    th = jnp.tanh(GELU_C * (y + GELU_K * y * y * y))
    return 0.5 * (1.0 + th) + 0.5 * y * (1.0 - th * th) * GELU_C * (1.0 + 3.0 * GELU_K * y * y)


def _row_spec(tile, cols, rev_n=None):
    if rev_n is None:
        return pl.BlockSpec((tile, cols), lambda i: (i, 0))
    return pl.BlockSpec((tile, cols), lambda i: (rev_n - 1 - i, 0))


def _full_spec(shape, single=False):
    zeros = (0,) * len(shape)
    if single:
        return pl.BlockSpec(shape, lambda *_: zeros, pipeline_mode=pl.Buffered(1))
    return pl.BlockSpec(shape, lambda *_: zeros)


def _acc(ref, val, first):
    @pl.when(first)
    def _():
        ref[...] = val

    @pl.when(jnp.logical_not(first))
    def _():
        ref[...] += val


def _place():
    x, y, c = lax.axis_index("x"), lax.axis_index("y"), lax.axis_index("c")
    others = [(1 - x, y), (x, 1 - y), (1 - x, 1 - y)]
    return x, y, c, others


class _Exchange:
    def __init__(self, ins, out_shapes, aliases, n_sems):
        self.ins, self.out_shapes, self.aliases, self.n_sems = list(ins), list(out_shapes), dict(aliases), n_sems

    def mid_steps(self):
        return []

    def mid(self, ins, outs, send_sems, recv_sems, step=None):
        pass


class _SiblingScatter(_Exchange):
    def __init__(self, views):
        super().__init__(views, [jax.ShapeDtypeStruct((4,) + v.shape[2:], v.dtype) for v in views], {}, 4 * len(views))

    def _copies(self, ins, outs, send_sems, recv_sems):
        x, y, c, _ = _place()
        return [
            pltpu.make_async_remote_copy(src_ref=ins[a].at[k, 1 - c], dst_ref=outs[a].at[k], send_sem=send_sems.at[4 * a + k], recv_sem=recv_sems.at[4 * a + k], device_id=(x, y, 1 - c), device_id_type=MESH)
            for a in range(len(ins))
            for k in range(4)
        ]

    def start(self, *refs):
        for cp in self._copies(*refs):
            cp.start()

    def finish(self, *refs):
        cps = self._copies(*refs)
        for cp in cps:
            cp.wait_recv()
        for cp in cps:
            cp.wait_send()


class _ChipScatter(_Exchange):
    def __init__(self, parts):
        super().__init__(parts, [jax.ShapeDtypeStruct((3,) + p.shape[1:], p.dtype) for p in parts], {}, 3 * len(parts))

    def _copies(self, ins, outs, send_sems, recv_sems):
        x, y, c, others = _place()
        return [
            pltpu.make_async_remote_copy(src_ref=ins[a].at[2 * chip[0] + chip[1]], dst_ref=outs[a].at[j], send_sem=send_sems.at[3 * a + j], recv_sem=recv_sems.at[3 * a + j], device_id=(*chip, c), device_id_type=MESH)
            for a in range(len(ins))
            for j, chip in enumerate(others)
        ]

    start = _SiblingScatter.start
    finish = _SiblingScatter.finish


class _SiblingShare(_Exchange):
    def __init__(self, bufs):
        super().__init__(bufs, [jax.ShapeDtypeStruct(b.shape, b.dtype) for b in bufs], {a: a for a in range(len(bufs))}, len(bufs))

    def _copy(self, outs, send_sems, recv_sems, a, half):
        x, y, c, _ = _place()
        mine = outs[a].at[c if half == "mine" else 1 - c]
        return pltpu.make_async_remote_copy(src_ref=mine, dst_ref=mine, send_sem=send_sems.at[a], recv_sem=recv_sems.at[a], device_id=(x, y, 1 - c), device_id_type=MESH)

    def start(self, ins, outs, send_sems, recv_sems):
        for a in range(len(outs)):
            self._copy(outs, send_sems, recv_sems, a, "mine").start()

    def finish(self, ins, outs, send_sems, recv_sems):
        for a in range(len(outs)):
            self._copy(outs, send_sems, recv_sems, a, "theirs").wait_recv()
        for a in range(len(outs)):
            self._copy(outs, send_sems, recv_sems, a, "mine").wait_send()


class _Gather(_Exchange):
    def __init__(self, bufs, steps=None):
        super().__init__(bufs, [jax.ShapeDtypeStruct(b.shape, b.dtype) for b in bufs], {a: a for a in range(len(bufs))}, 6 * len(bufs))
        self.steps = steps

    def mid_steps(self):
        return sorted(set(self.steps or []))

    def _copy(self, outs, send_sems, recv_sems, a, j, chip, half, to):
        blk = outs[a].at[2 * chip[0] + chip[1], half]
        return pltpu.make_async_remote_copy(src_ref=blk, dst_ref=blk, send_sem=send_sems.at[6 * a + j], recv_sem=recv_sems.at[6 * a + j], device_id=to, device_id_type=MESH)

    def start(self, ins, outs, send_sems, recv_sems):
        x, y, c, others = _place()
        for a in range(len(outs)):
            for j, chip in enumerate(others):
                self._copy(outs, send_sems, recv_sems, a, j, (x, y), c, (*chip, c)).start()

    def mid(self, ins, outs, send_sems, recv_sems, step=None):
        x, y, c, others = _place()
        for a in range(len(outs)):
            if step is not None and self.steps[a] != step:
                continue
            for j, chip in enumerate(others):
                self._copy(outs, send_sems, recv_sems, a, j, chip, c, (x, y, c)).wait_recv()
                self._copy(outs, send_sems, recv_sems, a, 3 + j, chip, c, (x, y, 1 - c)).start()

    def finish(self, ins, outs, send_sems, recv_sems):
        x, y, c, others = _place()
        for a in range(len(outs)):
            for j, chip in enumerate(others):
                self._copy(outs, send_sems, recv_sems, a, 3 + j, chip, 1 - c, (x, y, c)).wait_recv()
        for a in range(len(outs)):
            for j, chip in enumerate(others):
                self._copy(outs, send_sems, recv_sems, a, j, (x, y), c, (*chip, c)).wait_send()
                self._copy(outs, send_sems, recv_sems, a, 3 + j, chip, c, (x, y, 1 - c)).wait_send()


class _SemSlice:
    def __init__(self, sems, off):
        self.sems, self.off = sems, off

    @property
    def at(self):
        return self

    def __getitem__(self, i):
        return self.sems.at[self.off + i]


class _Group(_Exchange):
    def __init__(self, exchanges):
        ins, outs, aliases, n_sems, self.spans = [], [], {}, 0, []
        for ex in exchanges:
            self.spans.append((len(ins), len(outs), n_sems))
            aliases.update({len(ins) + i: len(outs) + o for i, o in ex.aliases.items()})
            ins, outs, n_sems = ins + ex.ins, outs + ex.out_shapes, n_sems + ex.n_sems
        super().__init__(ins, outs, aliases, n_sems)
        self.exchanges = exchanges

    def mid_steps(self):
        return sorted({s for ex in self.exchanges for s in ex.mid_steps()})

    def _each(self, method, ins, outs, send_sems, recv_sems, **kw):
        for ex, (i0, o0, s0) in zip(self.exchanges, self.spans):
            getattr(ex, method)(ins[i0 : i0 + len(ex.ins)], outs[o0 : o0 + len(ex.out_shapes)], _SemSlice(send_sems, s0), _SemSlice(recv_sems, s0), **kw)

    def start(self, *refs):
        self._each("start", *refs)

    def mid(self, *refs, step=None):
        self._each("mid", *refs, step=step)

    def finish(self, *refs):
        self._each("finish", *refs)

    def split(self, outs):
        return [outs[o0 : o0 + len(ex.out_shapes)] for ex, (_, o0, _) in zip(self.exchanges, self.spans)]


def _exchange_call(ex, name):
    n, m = len(ex.ins), len(ex.out_shapes)

    def body(*refs):
        parts = (refs[:n], refs[n : n + m], refs[n + m], refs[n + m + 1])
        ex.start(*parts)
        ex.mid(*parts)
        ex.finish(*parts)

    return pl.pallas_call(
        body,
        name=name,
        out_shape=ex.out_shapes,
        in_specs=[ANY] * n,
        out_specs=[ANY] * m,
        scratch_shapes=[pltpu.SemaphoreType.DMA((ex.n_sems,)), pltpu.SemaphoreType.DMA((ex.n_sems,))],
        input_output_aliases=ex.aliases,
    )(*ex.ins)


def _pallas(body, *, name, grid, in_specs, out_specs, out_shape, operands, scratch_shapes=(), exchange=None):
    params = pltpu.CompilerParams(dimension_semantics=("arbitrary",) * len(grid), vmem_limit_bytes=VMEM_LIMIT)
    if exchange is None:
        outs = pl.pallas_call(body, name=name, grid=grid, in_specs=in_specs, out_specs=out_specs, out_shape=out_shape, scratch_shapes=list(scratch_shapes), compiler_params=params)(*operands)
        return outs, []
    ex = exchange
    n_in, n_out, n_scr = len(in_specs), len(out_specs), len(scratch_shapes)
    x_in, x_out = len(ex.ins), len(ex.out_shapes)

    def hosted(*refs):
        ins, x_ins = refs[:n_in], refs[n_in : n_in + x_in]
        outs, x_outs = refs[n_in + x_in : n_in + x_in + n_out], refs[n_in + x_in + n_out : n_in + x_in + n_out + x_out]
        rest = refs[n_in + x_in + n_out + x_out :]
        parts = (x_ins, x_outs, rest[n_scr], rest[n_scr + 1])
        ids = [pl.program_id(d) for d in range(len(grid))]
        first = _all([i == 0 for i in ids])
        last = _all([i == g - 1 for i, g in zip(ids, grid)])

        @pl.when(first)
        def _():
            ex.start(*parts)

        body(*ins, *outs, *rest[:n_scr])

        for step in ex.mid_steps():

            @pl.when(ids[0] == step)
            def _(step=step):
                ex.mid(*parts, step=step)

        @pl.when(last)
        def _():
            ex.finish(*parts)

    outs = pl.pallas_call(
        hosted,
        name=name,
        grid=grid,
        in_specs=list(in_specs) + [ANY] * x_in,
        out_specs=list(out_specs) + [ANY] * x_out,
        out_shape=list(out_shape) + ex.out_shapes,
        scratch_shapes=list(scratch_shapes) + [pltpu.SemaphoreType.DMA((ex.n_sems,)), pltpu.SemaphoreType.DMA((ex.n_sems,))],
        input_output_aliases={n_in + i: n_out + o for i, o in ex.aliases.items()},
        compiler_params=params,
    )(*operands, *ex.ins)
    return outs[:n_out], outs[n_out:]


def _all(conds):
    out = conds[0]
    for c in conds[1:]:
        out = jnp.logical_and(out, c)
    return out


def _row_tile(rows):
    if rows <= 512:
        return rows
    for t in (512, 352, 256, 176, 128, 112, 64, 32, 16, 8):
        if rows % t == 0:
            return t
    return rows


def _add_own_half(view, got, place, name, wire=WIRE_DTYPE):
    _, _, r, c = view.shape
    tr = _row_tile(r)

    def body(place_ref, v_ref, g_ref, o_ref, w_ref):
        s = v_ref[...] + g_ref[...]
        w_ref[...] = s.astype(w_ref.dtype)

        @pl.when(pl.program_id(1) == place_ref[0])
        def _():
            o_ref[...] = s

    blk = pl.BlockSpec((None, tr, c), lambda i, k, pr: (k, i, 0))
    return pl.pallas_call(
        body,
        name=name,
        out_shape=[jax.ShapeDtypeStruct((r, c), F32), jax.ShapeDtypeStruct((4, r, c), wire)],
        grid_spec=pltpu.PrefetchScalarGridSpec(
            num_scalar_prefetch=1,
            grid=(r // tr, 4),
            in_specs=[pl.BlockSpec((None, None, tr, c), lambda i, k, pr: (k, pr[1], i, 0)), blk],
            out_specs=[pl.BlockSpec((tr, c), lambda i, k, pr: (i, 0)), blk],
        ),
        compiler_params=pltpu.CompilerParams(dimension_semantics=("arbitrary", "arbitrary"), vmem_limit_bytes=VMEM_LIMIT),
    )(place, view, got)


def _add_chips(part, got, place, name):
    r, c = part.shape
    tr = _row_tile(r)

    def body(place_ref, p_ref, g_ref, o_ref):
        o_ref[...] = ((p_ref[...] + g_ref[0].astype(F32)) + g_ref[1].astype(F32)) + g_ref[2].astype(F32)

    return pl.pallas_call(
        body,
        name=name,
        out_shape=jax.ShapeDtypeStruct((2, r, c), F32),
        grid_spec=pltpu.PrefetchScalarGridSpec(
            num_scalar_prefetch=1,
            grid=(r // tr,),
            in_specs=[pl.BlockSpec((tr, c), lambda i, pr: (i, 0)), pl.BlockSpec((3, tr, c), lambda i, pr: (0, i, 0))],
            out_specs=pl.BlockSpec((None, tr, c), lambda i, pr: (pr[1], i, 0)),
        ),
        compiler_params=pltpu.CompilerParams(dimension_semantics=("arbitrary",), vmem_limit_bytes=VMEM_LIMIT),
    )(place, part, got)


def _adamw_update(w_ref, g_ref, m_ref, v_ref, d_ref, nm_ref, nv_ref):
    g = g_ref[...]
    nm = ADAM_B1 * m_ref[...] + (1.0 - ADAM_B1) * g
    nv = ADAM_B2 * v_ref[...] + (1.0 - ADAM_B2) * (g * g)
    m_hat = nm / (1.0 - ADAM_B1**ADAM_STEP)
    v_hat = nv / (1.0 - ADAM_B2**ADAM_STEP)
    d_ref[...] = -ADAM_LR * (m_hat / (jnp.sqrt(v_hat) + ADAM_EPS) + ADAM_WD * w_ref[...])
    nm_ref[...] = nm
    nv_ref[...] = nv


def _adamw(w, g, m, v, name):
    r, c = w.shape
    tr = _row_tile(r)
    spec = pl.BlockSpec((tr, c), lambda i: (i, 0))
    outs, _ = _pallas(_adamw_update, name=name, grid=(r // tr,), in_specs=[spec] * 4, out_specs=[spec] * 3, out_shape=[jax.ShapeDtypeStruct((r, c), F32)] * 3, operands=(w, g, m, v))
    return outs


def _adamw_many(ws, gs, ms, vs, name):
    n = len(ws)

    def body(*refs):
        for k in range(n):
            _adamw_update(*(refs[j * n + k] for j in range(7)))

    outs = pl.pallas_call(
        body,
        name=name,
        out_shape=[jax.ShapeDtypeStruct(w.shape, F32) for w in ws] * 3,
        in_specs=[pl.BlockSpec(memory_space=pltpu.VMEM)] * (4 * n),
        out_specs=[pl.BlockSpec(memory_space=pltpu.VMEM)] * (3 * n),
    )(*ws, *gs, *ms, *vs)
    return outs[:n], outs[n : 2 * n], outs[2 * n :]


def _load_weights(pairs, sems):
    @pl.when(pl.program_id(0) == 0)
    def _():
        cps = [pltpu.make_async_copy(src, dst, sems.at[k]) for k, (src, dst) in enumerate(pairs)]
        for cp in cps:
            cp.start()
        for cp in cps:
            cp.wait()


def _ffn_chunks(F):
    bounds = list(range(0, F, FFN_CHUNK)) + [F]
    return list(zip(bounds[:-1], bounds[1:]))


def _shifted_specs(tm, cols):
    per = tm // N_META
    return [_row_spec(tm, cols), pl.BlockSpec((N_META, cols), lambda i: (jnp.maximum(i * per - 1, 0), 0))]


def _shifted_tile(cur_ref, before_ref, tm):
    return jnp.concatenate([before_ref[...], cur_ref[0 : tm - N_META, :]], axis=0)


def _tokens_tile(cur_ref, before_ref, meta_ref, tm, tile_0):
    first = jnp.where(tile_0, meta_ref[...], before_ref[...])
    return jnp.concatenate([first, cur_ref[0 : tm - N_META, :]], axis=0)


def _ffn_fwd_loss(h, target, g_pre, g_post, wg, wu, wd, name):
    T, D = h.shape
    F = wg.shape[0]
    tm = FFN_TILE

    def body(h_ref, t_ref, tb_ref, gpre_ref, gpost_ref, wg_hbm, wu_hbm, wd_hbm, dy_ref, sq_ref, f_ref, ga_ref, si_ref, s_ref, n_ref, wg_v, wu_v, wd_v, sems):
        i = pl.program_id(0)
        _load_weights([(wg_hbm, wg_v), (wu_hbm, wu_v), (wd_hbm, wd_v)], sems)
        hh = h_ref[...]
        n = (hh * _rms_stat(hh) * gpre_ref[...]).astype(MXU_DTYPE)
        n_ref[...] = n.astype(n_ref.dtype)
        f = jnp.zeros((tm, D), F32)
        for lo, hi in _ffn_chunks(F):
            a = _mm_nt(n, wg_v[lo:hi, :])
            b = _mm_nt(n, wu_v[lo:hi, :])
            sg = _sigmoid(a)
            si = a * sg
            s = (si * b).astype(MXU_DTYPE)
            ga_ref[:, lo:hi] = (b * (sg * (1.0 + a * (1.0 - sg)))).astype(ga_ref.dtype)
            si_ref[:, lo:hi] = si.astype(si_ref.dtype)
            s_ref[:, lo:hi] = s.astype(s_ref.dtype)
            f = f + _mm(s, wd_v[lo:hi, :])
        f_ref[...] = f
        y = hh + 0.5 * (f * _rms_stat(f) * gpost_ref[...])
        rows = i * tm + lax.broadcasted_iota(jnp.int32, (tm, D), 0)
        err = jnp.where(rows >= N_META, y - _shifted_tile(t_ref, tb_ref, tm), 0.0)
        dy_ref[...] = err * (1.0 / D)
        _acc(sq_ref, jnp.sum(err * err, axis=0, keepdims=True), i == 0)

    tok = jax.ShapeDtypeStruct((T, D), F32)
    act = jax.ShapeDtypeStruct((T, F), MXU_DTYPE)
    outs, _ = _pallas(
        body,
        name=name,
        grid=(T // tm,),
        out_shape=[tok, jax.ShapeDtypeStruct((1, D), F32), tok, act, act, act, jax.ShapeDtypeStruct((T, D), MXU_DTYPE)],
        in_specs=[_row_spec(tm, D)] + _shifted_specs(tm, D) + [_full_spec((1, D)), _full_spec((1, D)), ANY, ANY, ANY],
        out_specs=[_row_spec(tm, D), _full_spec((1, D)), _row_spec(tm, D), _row_spec(tm, F), _row_spec(tm, F), _row_spec(tm, F), _row_spec(tm, D)],
        scratch_shapes=[pltpu.VMEM(wg.shape, wg.dtype), pltpu.VMEM(wu.shape, wu.dtype), pltpu.VMEM(wd.shape, wd.dtype), pltpu.SemaphoreType.DMA((3,))],
        operands=(h, target, target, g_pre, g_post, wg, wu, wd),
    )
    return outs


def _ffn_gate_up(x, meta, g_pre, wg, wu, name, exchange=None):
    D = x.shape[1]
    T = x.shape[0] + N_META
    F = wg.shape[0]
    tm = FFN_TILE

    def body(x_ref, xb_ref, meta_ref, gpre_ref, wg_hbm, wu_hbm, ga_ref, si_ref, s_ref, n_ref, wg_v, wu_v, sems):
        _load_weights([(wg_hbm, wg_v), (wu_hbm, wu_v)], sems)
        hh = _tokens_tile(x_ref, xb_ref, meta_ref, tm, pl.program_id(0) == 0)
        n = (hh * _rms_stat(hh) * gpre_ref[...]).astype(MXU_DTYPE)
        n_ref[...] = n.astype(n_ref.dtype)
        for lo, hi in _ffn_chunks(F):
            a = _mm_nt(n, wg_v[lo:hi, :])
            b = _mm_nt(n, wu_v[lo:hi, :])
            sg = _sigmoid(a)
            si = a * sg
            ga_ref[:, lo:hi] = (b * (sg * (1.0 + a * (1.0 - sg)))).astype(ga_ref.dtype)
            si_ref[:, lo:hi] = si.astype(si_ref.dtype)
            s_ref[:, lo:hi] = (si * b).astype(s_ref.dtype)

    act = jax.ShapeDtypeStruct((T, F), MXU_DTYPE)
    return _pallas(
        body,
        name=name,
        grid=(T // tm,),
        out_shape=[act, act, act, jax.ShapeDtypeStruct((T, D), MXU_DTYPE)],
        in_specs=_shifted_specs(tm, D) + [_full_spec((N_META, D)), _full_spec((1, D)), ANY, ANY],
        out_specs=[_row_spec(tm, F), _row_spec(tm, F), _row_spec(tm, F), _row_spec(tm, D)],
        scratch_shapes=[pltpu.VMEM(wg.shape, wg.dtype), pltpu.VMEM(wu.shape, wu.dtype), pltpu.SemaphoreType.DMA((2,))],
        operands=(x, x, meta, g_pre, wg, wu),
        exchange=exchange,
    )


def _ffn_down(x, meta, s, g_post, wd, name, exchange=None):
    D = x.shape[1]
    T = x.shape[0] + N_META
    F = wd.shape[0]
    tm = _token_tile(T)

    def body(x_ref, xb_ref, meta_ref, s_ref, gpost_ref, wd_hbm, hout_ref, f_ref, wd_v, sems):
        _load_weights([(wd_hbm, wd_v)], sems)
        f = _mm(s_ref[...], wd_v[...])
        f_ref[...] = f
        hout_ref[...] = _tokens_tile(x_ref, xb_ref, meta_ref, tm, pl.program_id(0) == 0) + 0.5 * (f * _rms_stat(f) * gpost_ref[...])

    tok = jax.ShapeDtypeStruct((T, D), F32)
    return _pallas(
        body,
        name=name,
        grid=(T // tm,),
        out_shape=[tok, tok],
        in_specs=_shifted_specs(tm, D) + [_full_spec((N_META, D)), _row_spec(tm, F), _full_spec((1, D)), ANY],
        out_specs=[_row_spec(tm, D), _row_spec(tm, D)],
        scratch_shapes=[pltpu.VMEM(wd.shape, wd.dtype), pltpu.SemaphoreType.DMA((1,))],
        operands=(x, x, meta, s, g_post, wd),
        exchange=exchange,
    )


def _ffn_bwd(dh, f, ga, si, h, g_post, g_pre, wg, wu, wd, name):
    T, D = dh.shape
    F = wd.shape[0]
    tm = FFN_TILE

    def body(dh_ref, f_ref, ga_ref, si_ref, h_ref, gpost_ref, gpre_ref, wg_hbm, wu_hbm, wd_hbm, da_ref, db_ref, df_ref, dhin_ref, dgpost_ref, dgpre_ref, wg_v, wu_v, wd_v, sems):
        first = pl.program_id(0) == 0
        _load_weights([(wg_hbm, wg_v), (wu_hbm, wu_v), (wd_hbm, wd_v)], sems)
        dh = dh_ref[...]
        df, dg = _rms_bwd(f_ref[...], gpost_ref[...], 0.5 * dh)
        _acc(dgpost_ref, dg, first)
        dfb = df.astype(MXU_DTYPE)
        df_ref[...] = dfb.astype(df_ref.dtype)
        dn = jnp.zeros((tm, D), F32)
        for lo, hi in _ffn_chunks(F):
            ds = _mm_nt(dfb, wd_v[lo:hi, :])
            da = (ds * ga_ref[:, lo:hi].astype(F32)).astype(MXU_DTYPE)
            db = (ds * si_ref[:, lo:hi].astype(F32)).astype(MXU_DTYPE)
            da_ref[:, lo:hi] = da.astype(da_ref.dtype)
            db_ref[:, lo:hi] = db.astype(db_ref.dtype)
            dn = dn + _mm(da, wg_v[lo:hi, :]) + _mm(db, wu_v[lo:hi, :])
        dx, dg = _rms_bwd(h_ref[...], gpre_ref[...], dn)
        _acc(dgpre_ref, dg, first)
        dhin_ref[...] = dh + dx

    act = jax.ShapeDtypeStruct((T, F), MXU_DTYPE)
    vec = jax.ShapeDtypeStruct((1, D), F32)
    outs, _ = _pallas(
        body,
        name=name,
        grid=(T // tm,),
        out_shape=[act, act, jax.ShapeDtypeStruct((T, D), MXU_DTYPE), jax.ShapeDtypeStruct((T, D), F32), vec, vec],
        in_specs=[_row_spec(tm, D), _row_spec(tm, D), _row_spec(tm, F), _row_spec(tm, F), _row_spec(tm, D), _full_spec((1, D)), _full_spec((1, D)), ANY, ANY, ANY],
        out_specs=[_row_spec(tm, F), _row_spec(tm, F), _row_spec(tm, D), _row_spec(tm, D), _full_spec((1, D)), _full_spec((1, D))],
        scratch_shapes=[pltpu.VMEM(wg.shape, wg.dtype), pltpu.VMEM(wu.shape, wu.dtype), pltpu.VMEM(wd.shape, wd.dtype), pltpu.SemaphoreType.DMA((3,))],
        operands=(dh, f, ga, si, h, g_post, g_pre, wg, wu, wd),
    )
    return outs


def _ffn_bwd_first(dh, f, ga, si, x, meta, g_post, g_pre, wg, wu, wd, name):
    D = x.shape[1]
    T = x.shape[0] + N_META
    F = wd.shape[0]
    tm = FFN_TILE
    nt = T // tm
    per = tm // N_META
    tile = lambda i: jnp.minimum(i, nt - 1)

    def body(dh_ref, f_ref, ga_ref, si_ref, x_ref, xb_ref, meta_ref, gpost_ref, gpre_ref, wg_hbm, wu_hbm, wd_hbm,
             da_ref, db_ref, df_ref, dx_ref, dmeta_ref, dgpost_ref, dgpre_ref, wg_v, wu_v, wd_v, sems, held):
        i = pl.program_id(0)
        _load_weights([(wg_hbm, wg_v), (wu_hbm, wu_v), (wd_hbm, wd_v)], sems)

        @pl.when(i < nt)
        def _():
            dh = dh_ref[...]
            df, dg = _rms_bwd(f_ref[...], gpost_ref[...], 0.5 * dh)
            _acc(dgpost_ref, dg, i == 0)
            dfb = df.astype(MXU_DTYPE)
            df_ref[...] = dfb.astype(df_ref.dtype)
            dn = jnp.zeros((tm, D), F32)
            for lo, hi in _ffn_chunks(F):
                ds = _mm_nt(dfb, wd_v[lo:hi, :])
                da = (ds * ga_ref[:, lo:hi].astype(F32)).astype(MXU_DTYPE)
                db = (ds * si_ref[:, lo:hi].astype(F32)).astype(MXU_DTYPE)
                da_ref[:, lo:hi] = da.astype(da_ref.dtype)
                db_ref[:, lo:hi] = db.astype(db_ref.dtype)
                dn = dn + _mm(da, wg_v[lo:hi, :]) + _mm(db, wu_v[lo:hi, :])
            dx, dg = _rms_bwd(_tokens_tile(x_ref, xb_ref, meta_ref, tm, i == 0), gpre_ref[...], dn)
            _acc(dgpre_ref, dg, i == 0)
            dh_in = dh + dx

            @pl.when(i == 0)
            def _():
                dmeta_ref[...] = dh_in[0:N_META, :]

            @pl.when(i > 0)
            def _():
                dx_ref[...] = jnp.concatenate([held[...], dh_in[0:N_META, :]], axis=0)

            held[...] = dh_in[N_META:, :]

        @pl.when(i == nt)
        def _():
            dx_ref[0 : tm - N_META, :] = held[...]

    rows = lambda cols: pl.BlockSpec((tm, cols), lambda i: (tile(i), 0))
    act = jax.ShapeDtypeStruct((T, F), MXU_DTYPE)
    vec = jax.ShapeDtypeStruct((1, D), F32)
    outs, _ = _pallas(
        body,
        name=name,
        grid=(nt + 1,),
        out_shape=[act, act, jax.ShapeDtypeStruct((T, D), MXU_DTYPE), jax.ShapeDtypeStruct((T - N_META, D), F32), jax.ShapeDtypeStruct((N_META, D), F32), vec, vec],
        in_specs=[rows(D), rows(D), rows(F), rows(F), rows(D), pl.BlockSpec((N_META, D), lambda i: (jnp.maximum(tile(i) * per - 1, 0), 0)), _full_spec((N_META, D)),
                  _full_spec((1, D)), _full_spec((1, D)), ANY, ANY, ANY],
        out_specs=[rows(F), rows(F), rows(D), pl.BlockSpec((tm, D), lambda i: (jnp.maximum(i - 1, 0), 0)), _full_spec((N_META, D)), _full_spec((1, D)), _full_spec((1, D))],
        scratch_shapes=[pltpu.VMEM(wg.shape, wg.dtype), pltpu.VMEM(wu.shape, wu.dtype), pltpu.VMEM(wd.shape, wd.dtype), pltpu.SemaphoreType.DMA((3,)), pltpu.VMEM((tm - N_META, D), F32)],
        operands=(dh, f, ga, si, x, x, meta, g_post, g_pre, wg, wu, wd),
    )
    return outs


def _token_tile(T):
    for t in (912, 864, 432):
        if T % t == 0:
            return t
    raise ValueError(f"no token tile for {T} rows")


def _tn_matmul(xm, ym, name, exchange=None):
    T, M = xm.shape
    N = ym.shape[1]
    if (T - N_META) % TN_TILE:
        tk = _token_tile(T)

        def body(x_ref, y_ref, o_ref):
            _acc(o_ref, _mm_tn(x_ref[...], y_ref[...]), pl.program_id(0) == 0)

        grid, operands = (T // tk,), (xm, ym)
        in_specs = [pl.BlockSpec((tk, M), lambda k: (k, 0)), pl.BlockSpec((tk, N), lambda k: (k, 0))]
    else:
        tk = TN_TILE

        def body(x_ref, y_ref, xh_ref, yh_ref, o_ref):
            prod = _mm_tn(x_ref[...], y_ref[...])

            @pl.when(pl.program_id(0) == 0)
            def _():
                o_ref[...] = prod + _mm_tn(xh_ref[...], yh_ref[...])

            @pl.when(pl.program_id(0) > 0)
            def _():
                o_ref[...] += prod

        grid, operands = ((T - N_META) // tk,), (xm, ym, xm, ym)
        start = lambda k: (pl.multiple_of(N_META + k * tk, N_META), 0)
        in_specs = [pl.BlockSpec((pl.Element(tk), pl.Element(M)), start), pl.BlockSpec((pl.Element(tk), pl.Element(N)), start),
                    pl.BlockSpec((N_META, M), lambda k: (0, 0)), pl.BlockSpec((N_META, N), lambda k: (0, 0))]

    (out,), x_outs = _pallas(
        body,
        name=name,
        grid=grid,
        out_shape=[jax.ShapeDtypeStruct((M, N), F32)],
        in_specs=in_specs,
        out_specs=[_full_spec((M, N))],
        operands=operands,
        exchange=exchange,
    )
    return out, x_outs


TAB_A, TAB_AS1, TAB_AS2, TAB_AS4, TAB_JF, TAB_JB = 0, 2, 4, 6, 8, 10


def _scan_inplace(zr, zi, tabs, pows, car_r, car_i, seg, reverse, base=0):
    n_slabs = zr.shape[0]
    sgn = -1.0 if reverse else 1.0
    row = lax.broadcasted_iota(jnp.int32, (SUBLANES, LANES), 0)

    def cmul(pr, pi, xr, xi):
        return pr * xr - pi * xi, pr * xi + pi * xr

    for k0 in range(0, n_slabs, SLAB_GROUP):
        slabs = range(k0, min(k0 + SLAB_GROUP, n_slabs))
        ar = [tabs[TAB_A, k] for k in slabs]
        ai = [sgn * tabs[TAB_A + 1, k] for k in slabs]

        def first_pass(t, carry):
            r = (seg - 1 - t) if reverse else t
            out = []
            for q, k in enumerate(slabs):
                xr, xi = carry[2 * q], carry[2 * q + 1]
                pr, pi = cmul(ar[q], ai[q], xr, xi)
                nr = pr + zr[k, pl.ds(base + r, SUBLANES, stride=seg), :]
                ni = pi + zi[k, pl.ds(base + r, SUBLANES, stride=seg), :]
                zr[k, pl.ds(base + r, SUBLANES, stride=seg), :] = nr
                zi[k, pl.ds(base + r, SUBLANES, stride=seg), :] = ni
                out += [nr, ni]
            return tuple(out)

        ends = lax.fori_loop(0, seg, first_pass, tuple(jnp.zeros((SUBLANES, LANES), F32) for _ in range(2 * len(slabs))))

        incoming = []
        for q, k in enumerate(slabs):
            fr, fi = ends[2 * q], ends[2 * q + 1]
            for d, tab in ((1, TAB_AS1), (2, TAB_AS2), (4, TAB_AS4)):
                shift, keep = (SUBLANES - d, row < SUBLANES - d) if reverse else (d, row >= d)
                sr = jnp.where(keep, pltpu.roll(fr, shift, 0), 0.0)
                si = jnp.where(keep, pltpu.roll(fi, shift, 0), 0.0)
                pr, pi = cmul(tabs[tab, k], sgn * tabs[tab + 1, k], sr, si)
                fr, fi = fr + pr, fi + pi
            cr, ci = car_r[k], car_i[k]
            jtab = TAB_JB if reverse else TAB_JF
            pr, pi = cmul(tabs[jtab, k], sgn * tabs[jtab + 1, k], cr, ci)
            er, ei = fr + pr, fi + pi
            if reverse:
                inr = jnp.where(row < SUBLANES - 1, pltpu.roll(er, SUBLANES - 1, 0), cr)
                ini = jnp.where(row < SUBLANES - 1, pltpu.roll(ei, SUBLANES - 1, 0), ci)
                car_r[k] = jnp.broadcast_to(er[0:1, :], (SUBLANES, LANES))
                car_i[k] = jnp.broadcast_to(ei[0:1, :], (SUBLANES, LANES))
            else:
                inr = jnp.where(row >= 1, pltpu.roll(er, 1, 0), cr)
                ini = jnp.where(row >= 1, pltpu.roll(ei, 1, 0), ci)
                car_r[k] = jnp.broadcast_to(er[SUBLANES - 1 : SUBLANES, :], (SUBLANES, LANES))
                car_i[k] = jnp.broadcast_to(ei[SUBLANES - 1 : SUBLANES, :], (SUBLANES, LANES))
            incoming += [inr, ini]

        def second_pass(r, _):
            p = (seg - 1 - r) if reverse else r
            for q, k in enumerate(slabs):
                pr, pi = cmul(pows[0, k, p], sgn * pows[1, k, p], incoming[2 * q], incoming[2 * q + 1])
                zr[k, pl.ds(base + r, SUBLANES, stride=seg), :] = zr[k, pl.ds(base + r, SUBLANES, stride=seg), :] + pr
                zi[k, pl.ds(base + r, SUBLANES, stride=seg), :] = zi[k, pl.ds(base + r, SUBLANES, stride=seg), :] + pi
            return 0

        lax.fori_loop(0, seg, second_pass, 0)


def _slabs_to_cols(ref, k0, n):
    return jnp.concatenate([ref[k0 + q] for q in range(n)], axis=1)


def _window_sum(ext, doublings, forward):
    rows = ext.shape[0]
    s = ext
    for k in range(doublings):
        s = s + pltpu.roll(s, (1 << k) if forward else rows - (1 << k), 0)
    return s


def _mix_fwd(h1, g_pre, g_so, g_po, g_post, dskip, pscale, win, wout, bbre, bbim, ccre, ccim, wgv, wgg, pw, tabs, pows, name, exchange=None):
    T, D = h1.shape
    W = D // 2
    tm = MIX_SUBTILES * MIX_TILE
    seg = MIX_TILE // SUBLANES
    n_slabs = tabs.shape[1]
    nch, cch, sch = bbre.shape
    spc = sch // LANES
    pg = W // len(POOL_WINDOWS)

    def body(h_ref, gpre_ref, gso_ref, gpo_ref, gpost_ref, dskip_ref, pscale_ref, win_ref, wout_ref, bbre_ref, bbim_ref, ccre_ref, ccim_ref, wgv_ref, wgg_ref, pw_ref, tabs_ref, pows_ref,
             proj_ref, xr_ref, xi_ref, y_ref, pooled_ref, mixed_ref, h2_ref, n2_ref, cat_ref, car_r, car_i, halo):
        i = pl.program_id(0)

        @pl.when(i == 0)
        def _():
            car_r[...] = jnp.zeros_like(car_r)
            car_i[...] = jnp.zeros_like(car_i)
            halo[...] = jnp.zeros_like(halo)

        hh = h_ref[...]
        n2 = (hh * _rms_stat(hh) * gpre_ref[...]).astype(MXU_DTYPE)
        n2_ref[...] = n2.astype(n2_ref.dtype)
        proj = _mm(n2, win_ref[...])
        proj_ref[...] = proj
        us, up = proj[:, :W], proj[:, W:]

        for c in range(nch):
            uc = us[:, c * cch : (c + 1) * cch].astype(MXU_DTYPE)
            bur, bui = _mm(uc, bbre_ref[c]), _mm(uc, bbim_ref[c])
            for q in range(spc):
                xr_ref[c * spc + q] = bur[:, q * LANES : (q + 1) * LANES]
                xi_ref[c * spc + q] = bui[:, q * LANES : (q + 1) * LANES]
        for sub in range(MIX_SUBTILES):
            _scan_inplace(xr_ref, xi_ref, tabs_ref, pows_ref, car_r, car_i, seg, reverse=False, base=sub * MIX_TILE)
        ys = []
        for c in range(nch):
            ys.append(_mm(_slabs_to_cols(xr_ref, c * spc, spc), ccre_ref[c]) - _mm(_slabs_to_cols(xi_ref, c * spc, spc), ccim_ref[c]))
        y = jnp.concatenate(ys, axis=1) + dskip_ref[...] * us
        y_ref[...] = y
        ge = _gelu(y).astype(MXU_DTYPE)
        zv = jnp.concatenate([_mm(ge[:, c * cch : (c + 1) * cch], wgv_ref[c]) for c in range(nch)], axis=1)
        zg = jnp.concatenate([_mm(ge[:, c * cch : (c + 1) * cch], wgg_ref[c]) for c in range(nch)], axis=1)
        out = zv * _sigmoid(zg)
        cat_s = out * _rms_stat(out) * gso_ref[...]

        ext = jnp.concatenate([halo[...], up], axis=0)
        halo[...] = up[tm - POOL_HALO :, :]
        t1 = (i * tm + 1 + lax.broadcasted_iota(jnp.int32, (tm, pg), 0)).astype(F32)
        pooled, pms = [], []
        for g, w in enumerate(POOL_WINDOWS):
            col = ext[:, g * pg : (g + 1) * pg]
            win_sum = _window_sum(col, g + 1, True)[POOL_HALO:, :]
            pooled_g = win_sum / jnp.minimum(t1, float(w)) - up[:, g * pg : (g + 1) * pg]
            pooled.append(pooled_g)
            pms.append(_mm(pooled_g, pw_ref[g]))
        pooled_ref[...] = jnp.concatenate(pooled, axis=1)
        yp = jnp.concatenate(pms, axis=1) * pscale_ref[...]
        cat_p = yp * _rms_stat(yp) * gpo_ref[...]

        cat = jnp.concatenate([cat_s, cat_p], axis=1).astype(MXU_DTYPE)
        cat_ref[...] = cat.astype(cat_ref.dtype)
        mixed = _mm(cat, wout_ref[...])
        mixed_ref[...] = mixed
        h2_ref[...] = hh + mixed * _rms_stat(mixed) * gpost_ref[...]

    tok = lambda cols, dt=F32: jax.ShapeDtypeStruct((T, cols), dt)
    slab_spec = pl.BlockSpec((n_slabs, tm, LANES), lambda i: (0, i, 0))
    operands = (h1, g_pre, g_so, g_po, g_post, dskip, pscale, win, wout, bbre, bbim, ccre, ccim, wgv, wgg, pw, tabs, pows)
    return _pallas(
        body,
        name=name,
        grid=(T // tm,),
        out_shape=[tok(D), jax.ShapeDtypeStruct((n_slabs, T, LANES), F32), jax.ShapeDtypeStruct((n_slabs, T, LANES), F32), tok(W), tok(W), tok(D), tok(D), tok(D, MXU_DTYPE), tok(D, MXU_DTYPE)],
        in_specs=[_row_spec(tm, D)] + [_full_spec(o.shape, single=True) for o in operands[1:]],
        out_specs=[_row_spec(tm, D), slab_spec, slab_spec, _row_spec(tm, W), _row_spec(tm, W), _row_spec(tm, D), _row_spec(tm, D), _row_spec(tm, D), _row_spec(tm, D)],
        scratch_shapes=[pltpu.VMEM((n_slabs, SUBLANES, LANES), F32), pltpu.VMEM((n_slabs, SUBLANES, LANES), F32), pltpu.VMEM((POOL_HALO, W), F32)],
        operands=operands,
        exchange=exchange,
    )


def _mix_bwd_heads(dh2, mixed, y, pooled, proj, g_so, g_po, g_post, pscale, wout, wgv, wgg, pw, name, exchange=None):
    T, D = dh2.shape
    W = D // 2
    tm = _token_tile(T)
    nch, cch, _ = wgv.shape
    ng, pg, _ = pw.shape

    def body(dh2_ref, mixed_ref, y_ref, pooled_ref, us_ref, gso_ref, gpo_ref, gpost_ref, pscale_ref, wout_ref, wgv_ref, wgg_ref, pw_ref,
             dy_ref, dpooled_ref, dmixed_ref, dgpost_ref, dgso_ref, dgpo_ref, dd_ref, dscale_ref, dwgv_ref, dwgg_ref, dpw_ref):
        first = pl.program_id(0) == 0
        dmixed, dgpost = _rms_bwd(mixed_ref[...], gpost_ref[...], dh2_ref[...])
        _acc(dgpost_ref, dgpost, first)
        dmb = dmixed.astype(MXU_DTYPE)
        dmixed_ref[...] = dmb.astype(dmixed_ref.dtype)
        dcat = _mm_nt(dmb, wout_ref[...])
        dcs, dcp = dcat[:, :W], dcat[:, W:]

        y = y_ref[...]
        ge = _gelu(y).astype(MXU_DTYPE)
        zv = jnp.concatenate([_mm(ge[:, c * cch : (c + 1) * cch], wgv_ref[c]) for c in range(nch)], axis=1)
        zg = jnp.concatenate([_mm(ge[:, c * cch : (c + 1) * cch], wgg_ref[c]) for c in range(nch)], axis=1)
        sg = _sigmoid(zg)
        dout, dgso = _rms_bwd(zv * sg, gso_ref[...], dcs)
        _acc(dgso_ref, dgso, first)
        dzv = (dout * sg).astype(MXU_DTYPE)
        dzg = (dout * zv * sg * (1.0 - sg)).astype(MXU_DTYPE)
        dges = []
        for c in range(nch):
            cs = slice(c * cch, (c + 1) * cch)
            dges.append(_mm_nt(dzv[:, cs], wgv_ref[c]) + _mm_nt(dzg[:, cs], wgg_ref[c]))
            _acc(dwgv_ref.at[c], _mm_tn(ge[:, cs], dzv[:, cs]), first)
            _acc(dwgg_ref.at[c], _mm_tn(ge[:, cs], dzg[:, cs]), first)
        dy = jnp.concatenate(dges, axis=1) * _gelu_grad(y)
        dy_ref[...] = dy
        _acc(dd_ref, jnp.sum(dy * us_ref[...], axis=0, keepdims=True), first)

        pooled_b = pooled_ref[...].astype(MXU_DTYPE)
        pm = jnp.concatenate([_mm(pooled_b[:, g * pg : (g + 1) * pg], pw_ref[g]) for g in range(ng)], axis=1)
        dyp, dgpo = _rms_bwd(pm * pscale_ref[...], gpo_ref[...], dcp)
        _acc(dgpo_ref, dgpo, first)
        _acc(dscale_ref, jnp.sum(dyp * pm, axis=0, keepdims=True), first)
        dpm = (dyp * pscale_ref[...]).astype(MXU_DTYPE)
        dps = []
        for g in range(ng):
            gs = slice(g * pg, (g + 1) * pg)
            dps.append(_mm_nt(dpm[:, gs], pw_ref[g]))
            _acc(dpw_ref.at[g], _mm_tn(pooled_b[:, gs], dpm[:, gs]), first)
        dpooled_ref[...] = jnp.concatenate(dps, axis=1)

    vec = lambda n: jax.ShapeDtypeStruct((1, n), F32)
    operands = (dh2, mixed, y, pooled, proj, g_so, g_po, g_post, pscale, wout, wgv, wgg, pw)
    return _pallas(
        body,
        name=name,
        grid=(T // tm,),
        out_shape=[jax.ShapeDtypeStruct((T, W), F32), jax.ShapeDtypeStruct((T, W), F32), jax.ShapeDtypeStruct((T, D), MXU_DTYPE), vec(D), vec(W), vec(W), vec(W), vec(W),
                   jax.ShapeDtypeStruct(wgv.shape, F32), jax.ShapeDtypeStruct(wgg.shape, F32), jax.ShapeDtypeStruct(pw.shape, F32)],
        in_specs=[_row_spec(tm, D), _row_spec(tm, D), _row_spec(tm, W), _row_spec(tm, W), _row_spec(tm, W)] + [_full_spec(o.shape) for o in operands[5:]],
        out_specs=[_row_spec(tm, W), _row_spec(tm, W), _row_spec(tm, D), _full_spec((1, D)), _full_spec((1, W)), _full_spec((1, W)), _full_spec((1, W)), _full_spec((1, W)),
                   _full_spec(wgv.shape), _full_spec(wgg.shape), _full_spec(pw.shape)],
        operands=operands,
        exchange=exchange,
    )


def _mix_bwd_scan(dy, dpooled, xr, xi, proj, dskip, bbre, bbim, ccre, ccim, tabs, pows, name, exchange=None):
    T, W = dy.shape
    D = 2 * W
    tm = MIX_SUBTILES * MIX_TILE
    seg = MIX_TILE // SUBLANES
    nt = T // tm
    n_slabs = tabs.shape[1]
    nch, cch, sch = bbre.shape
    spc = sch // LANES
    pg = W // len(POOL_WINDOWS)
    blocks_per_tile = tm // SUBLANES

    def body(dy_ref, dp_ref, xr_ref, xi_ref, xpr_ref, xpi_ref, us_ref, dskip_ref, bbre_ref, bbim_ref, ccre_ref, ccim_ref, tabs_ref, pows_ref,
             dproj_ref, dccre_ref, dccim_ref, dbbre_ref, dbbim_ref, dar_ref, dai_ref, lr, li, car_r, car_i, halo):
        i = pl.program_id(0)
        first = i == 0
        tile = nt - 1 - i
        row = lax.broadcasted_iota(jnp.int32, (SUBLANES, LANES), 0)

        @pl.when(first)
        def _():
            car_r[...] = jnp.zeros_like(car_r)
            car_i[...] = jnp.zeros_like(car_i)
            halo[...] = jnp.zeros_like(halo)
            dar_ref[...] = jnp.zeros_like(dar_ref)
            dai_ref[...] = jnp.zeros_like(dai_ref)

        dy = dy_ref[...]
        for c in range(nch):
            dyc = dy[:, c * cch : (c + 1) * cch]
            gr, gi = _mm_nt(dyc, ccre_ref[c]), _mm_nt(dyc, ccim_ref[c])
            for q in range(spc):
                lr[c * spc + q] = gr[:, q * LANES : (q + 1) * LANES]
                li[c * spc + q] = -gi[:, q * LANES : (q + 1) * LANES]
            _acc(dccre_ref.at[c], _mm_tn(_slabs_to_cols(xr_ref, c * spc, spc), dyc), first)
            _acc(dccim_ref.at[c], -_mm_tn(_slabs_to_cols(xi_ref, c * spc, spc), dyc), first)
        for sub in reversed(range(MIX_SUBTILES)):
            _scan_inplace(lr, li, tabs_ref, pows_ref, car_r, car_i, seg, reverse=True, base=sub * MIX_TILE)

        for sub in range(MIX_SUBTILES):
            base = sub * MIX_TILE
            for k0 in range(0, n_slabs, SLAB_GROUP):
                slabs = range(k0, min(k0 + SLAB_GROUP, n_slabs))
                init = []
                for k in slabs:
                    if sub == 0:
                        prev_r = jnp.where(tile > 0, jnp.broadcast_to(xpr_ref[k, SUBLANES - 1 : SUBLANES, :], (SUBLANES, LANES)), 0.0)
                        prev_i = jnp.where(tile > 0, jnp.broadcast_to(xpi_ref[k, SUBLANES - 1 : SUBLANES, :], (SUBLANES, LANES)), 0.0)
                    else:
                        prev_r = jnp.broadcast_to(xr_ref[k, base - 1 : base, :], (SUBLANES, LANES))
                        prev_i = jnp.broadcast_to(xi_ref[k, base - 1 : base, :], (SUBLANES, LANES))
                    x0r = jnp.where(row >= 1, pltpu.roll(xr_ref[k, pl.ds(base + seg - 1, SUBLANES, stride=seg), :], 1, 0), prev_r)
                    x0i = jnp.where(row >= 1, pltpu.roll(xi_ref[k, pl.ds(base + seg - 1, SUBLANES, stride=seg), :], 1, 0), prev_i)
                    l0r, l0i = lr[k, pl.ds(base, SUBLANES, stride=seg), :], li[k, pl.ds(base, SUBLANES, stride=seg), :]
                    init += [l0r * x0r + l0i * x0i, l0i * x0r - l0r * x0i]

                def step(r, acc, slabs=slabs, base=base):
                    out = []
                    for q, k in enumerate(slabs):
                        pr_, pi_ = xr_ref[k, pl.ds(base + r - 1, SUBLANES, stride=seg), :], xi_ref[k, pl.ds(base + r - 1, SUBLANES, stride=seg), :]
                        lr_, li_ = lr[k, pl.ds(base + r, SUBLANES, stride=seg), :], li[k, pl.ds(base + r, SUBLANES, stride=seg), :]
                        out += [acc[2 * q] + lr_ * pr_ + li_ * pi_, acc[2 * q + 1] + li_ * pr_ - lr_ * pi_]
                    return tuple(out)

                sums = lax.fori_loop(1, seg, step, tuple(init))
                for q, k in enumerate(slabs):
                    dar_ref[k] += sums[2 * q]
                    dai_ref[k] += sums[2 * q + 1]

        us = us_ref[...]
        dus = []
        for c in range(nch):
            lrc, lic = _slabs_to_cols(lr, c * spc, spc).astype(MXU_DTYPE), _slabs_to_cols(li, c * spc, spc).astype(MXU_DTYPE)
            uc = us[:, c * cch : (c + 1) * cch]
            _acc(dbbre_ref.at[c], _mm_tn(uc, lrc), first)
            _acc(dbbim_ref.at[c], _mm_tn(uc, lic), first)
            dus.append(_mm_nt(lrc, bbre_ref[c]) + _mm_nt(lic, bbim_ref[c]))
        du_s = jnp.concatenate(dus, axis=1) + dskip_ref[...] * dy

        dp = dp_ref[...]
        t1 = (tile * tm + 1 + lax.broadcasted_iota(jnp.int32, (tm, pg), 0)).astype(F32)
        dups, heads = [], []
        for g, w in enumerate(POOL_WINDOWS):
            dpg = dp[:, g * pg : (g + 1) * pg]
            qg = dpg / jnp.minimum(t1, float(w))
            ext = jnp.concatenate([qg, halo[:, g * pg : (g + 1) * pg]], axis=0)
            dups.append(_window_sum(ext, g + 1, False)[:tm, :] - dpg)
            heads.append(qg[:POOL_HALO, :])
        halo[...] = jnp.concatenate(heads, axis=1)
        dproj_ref[...] = jnp.concatenate([du_s] + dups, axis=1).astype(dproj_ref.dtype)

    rev = lambda cols: _row_spec(tm, cols, rev_n=nt)
    slab_spec = pl.BlockSpec((n_slabs, tm, LANES), lambda i: (0, nt - 1 - i, 0))
    prev_spec = pl.BlockSpec((n_slabs, SUBLANES, LANES), lambda i: (0, jnp.maximum((nt - 1 - i) * blocks_per_tile - 1, 0), 0))
    consts = (dskip, bbre, bbim, ccre, ccim, tabs, pows)
    return _pallas(
        body,
        name=name,
        grid=(nt,),
        out_shape=[jax.ShapeDtypeStruct((T, D), MXU_DTYPE), jax.ShapeDtypeStruct(ccre.shape, F32), jax.ShapeDtypeStruct(ccim.shape, F32), jax.ShapeDtypeStruct(bbre.shape, F32),
                   jax.ShapeDtypeStruct(bbim.shape, F32), jax.ShapeDtypeStruct((n_slabs, SUBLANES, LANES), F32), jax.ShapeDtypeStruct((n_slabs, SUBLANES, LANES), F32)],
        in_specs=[rev(W), rev(W), slab_spec, slab_spec, prev_spec, prev_spec, rev(W)] + [_full_spec(o.shape, single=True) for o in consts],
        out_specs=[rev(D), _full_spec(ccre.shape), _full_spec(ccim.shape), _full_spec(bbre.shape), _full_spec(bbim.shape),
                   _full_spec((n_slabs, SUBLANES, LANES)), _full_spec((n_slabs, SUBLANES, LANES))],
        scratch_shapes=[pltpu.VMEM((n_slabs, tm, LANES), F32), pltpu.VMEM((n_slabs, tm, LANES), F32), pltpu.VMEM((n_slabs, SUBLANES, LANES), F32), pltpu.VMEM((n_slabs, SUBLANES, LANES), F32),
                        pltpu.VMEM((POOL_HALO, W), F32)],
        operands=(dy, dpooled, xr, xi, xr, xi, proj, *consts),
        exchange=exchange,
    )


def _mix_bwd_in(dproj, h1, dh2, g_pre, win, name, exchange=None):
    T, D = h1.shape
    tm = _token_tile(T)

    def body(dproj_ref, h_ref, dh2_ref, gpre_ref, win_ref, dh1_ref, dg_ref):
        dx, dg = _rms_bwd(h_ref[...], gpre_ref[...], _mm_nt(dproj_ref[...], win_ref[...]))
        _acc(dg_ref, dg, pl.program_id(0) == 0)
        dh1_ref[...] = dh2_ref[...] + dx

    return _pallas(
        body,
        name=name,
        grid=(T // tm,),
        out_shape=[jax.ShapeDtypeStruct((T, D), F32), jax.ShapeDtypeStruct((1, D), F32)],
        in_specs=[_row_spec(tm, D), _row_spec(tm, D), _row_spec(tm, D), _full_spec((1, D)), _full_spec(win.shape)],
        out_specs=[_row_spec(tm, D), _full_spec((1, D))],
        operands=(dproj, h1, dh2, g_pre, win),
        exchange=exchange,
    )


def _discretize(lam_re, lam_im, log_dt, b_re, b_im):
    dt = jnp.exp(log_dt)[:, None]
    decay = jnp.exp(lam_re * dt)
    ang = lam_im * dt
    a_re, a_im = decay * jnp.cos(ang), decay * jnp.sin(ang)
    nr = a_re - 1.0
    den = lam_re * lam_re + lam_im * lam_im
    q_re = (nr * lam_re + a_im * lam_im) / den
    q_im = (a_im * lam_re - nr * lam_im) / den
    bb_re = q_re[..., None] * b_re - q_im[..., None] * b_im
    bb_im = q_re[..., None] * b_im + q_im[..., None] * b_re
    return a_re, a_im, bb_re, bb_im


GROUPS_PER_CHUNK = 8


def _block_diag(w, rows_first):
    G = w.shape[0]
    nch = G // GROUPS_PER_CHUNK
    if not rows_first:
        w = jnp.swapaxes(w, 1, 2)
    p, q = w.shape[1], w.shape[2]
    eye = jnp.eye(GROUPS_PER_CHUNK, dtype=w.dtype)
    out = jnp.einsum("cgpq,gk->cgpkq", w.reshape(nch, GROUPS_PER_CHUNK, p, q), eye)
    return out.reshape(nch, GROUPS_PER_CHUNK * p, GROUPS_PER_CHUNK * q)


def _block_diag_extract(m, p, q, rows_first):
    nch = m.shape[0]
    eye = jnp.eye(GROUPS_PER_CHUNK, dtype=m.dtype)
    out = jnp.einsum("cgpkq,gk->cgpq", m.reshape(nch, GROUPS_PER_CHUNK, p, GROUPS_PER_CHUNK, q), eye).reshape(nch * GROUPS_PER_CHUNK, p, q)
    return out if rows_first else jnp.swapaxes(out, 1, 2)


def _cmul(ar, ai, br, bi):
    return ar * br - ai * bi, ar * bi + ai * br


def _powers(ar, ai, count):
    pr, pi = ar[None], ai[None]
    while pr.shape[0] < count:
        nr, ni = _cmul(pr, pi, pr[-1][None], pi[-1][None])
        pr, pi = jnp.concatenate([pr, nr]), jnp.concatenate([pi, ni])
    return pr[:count], pi[:count]


def _scan_tables(a_re, a_im, seg):
    n = a_re.size
    ns = n // LANES
    ar, ai = a_re.reshape(n), a_im.reshape(n)
    pr, pi = _powers(ar, ai, seg)
    jr, ji = _powers(pr[-1], pi[-1], SUBLANES)

    def bcast(v):
        return jnp.broadcast_to(v.reshape(ns, 1, LANES), (ns, SUBLANES, LANES))

    def per_sublane(vs):
        return jnp.transpose(vs.reshape(SUBLANES, ns, LANES), (1, 0, 2))

    tabs = jnp.stack([bcast(ar), bcast(ai), bcast(jr[0]), bcast(ji[0]), bcast(jr[1]), bcast(ji[1]), bcast(jr[3]), bcast(ji[3]),
                      per_sublane(jr), per_sublane(ji), per_sublane(jr[::-1]), per_sublane(ji[::-1])])

    def rows(vs):
        return jnp.broadcast_to(jnp.transpose(vs.reshape(seg, ns, 1, LANES), (1, 0, 2, 3)), (ns, seg, SUBLANES, LANES))

    return tabs, jnp.stack([rows(pr), rows(pi)])


SMALL = ("ffn1_pre_norm", "ffn1_post_norm", "mix_pre_norm", "mix_post_norm", "ssm_lambda_re", "ssm_lambda_im", "ssm_log_dt", "ssm_b_re", "ssm_b_im", "ssm_c_re", "ssm_c_im",
         "ssm_d", "ssm_w_glu", "pool_w", "pool_scale", "ssm_out_norm", "pool_out_norm", "ffn2_pre_norm", "ffn2_post_norm")
BIG = ("ffn1_w_gate", "ffn1_w_up", "ffn1_w_down", "w_in", "w_out", "ffn2_w_gate", "ffn2_w_up", "ffn2_w_down")
ORDER = ("meta_tokens", "ffn1_pre_norm", "ffn1_post_norm", "ffn1_w_gate", "ffn1_w_up", "ffn1_w_down", "mix_pre_norm", "mix_post_norm", "w_in", "ssm_lambda_re", "ssm_lambda_im",
         "ssm_log_dt", "ssm_b_re", "ssm_b_im", "ssm_c_re", "ssm_c_im", "ssm_d", "ssm_w_glu", "pool_w", "pool_scale", "ssm_out_norm", "pool_out_norm", "w_out", "ffn2_pre_norm",
         "ffn2_post_norm", "ffn2_w_gate", "ffn2_w_up", "ffn2_w_down")
PACK_ROWS = SUBLANES * 8
def _pack(arrays, rows):
    flat = jnp.concatenate([a.reshape(-1) for a in arrays])
    return jnp.pad(flat, (0, rows * LANES - flat.size)).reshape(rows, LANES)


def _unpack(packed, shapes):
    flat = packed.reshape(-1)
    out, off = [], 0
    for s in shapes:
        n = math.prod(s)
        out.append(flat[off : off + n].reshape(s))
        off += n
    return out


def _step(p, x, loss_target, m, v):
    D = x.shape[-1]
    chip = (2 * lax.axis_index("x") + lax.axis_index("y")).astype(jnp.int32)
    place = jnp.stack([chip, lax.axis_index("c").astype(jnp.int32)])

    def gather_buffer(w):
        own = w.reshape(1, 2, w.shape[0] // 2, w.shape[1])
        return lax.dynamic_update_slice(lax.empty((4,) + own.shape[1:], own.dtype), own, (chip, 0, 0, 0))

    def rows_of(n, a):
        return jnp.swapaxes(a[0], 0, 1) if n.endswith(("w_gate", "w_up")) else a[0]

    def rows_back(n, a):
        return (jnp.swapaxes(a, 0, 1) if n.endswith(("w_gate", "w_up")) else a)[None]

    def grad_view(g):
        return g.reshape(4, 2, g.shape[0] // 8, g.shape[1])

    def reduce_sum(got_sibling, views, tag, wires=None):
        wires = wires or [WIRE_DTYPE] * len(views)
        sums = [_add_own_half(v_, g_, place, f"{tag}_add_sibling_{k}", w_) for k, (v_, g_, w_) in enumerate(zip(views, got_sibling, wires))]
        return [s[0] for s in sums], [s[1] for s in sums]

    def reduce_halves(parts, got_chips, tag):
        return [_add_chips(p_, g_, place, f"{tag}_add_chips_{k}") for k, (p_, g_) in enumerate(zip(parts, got_chips))]

    bufs = {n: gather_buffer(rows_of(n, p[n]).astype(MXU_DTYPE)) for n in BIG}
    full = {}

    def gathered(names, got):
        full.update({n: g_.reshape(-1, g_.shape[-1]) for n, g_ in zip(names, got)})

    def gather_of(names, n_steps, tenths):
        sizes = [bufs[n].size * bufs[n].dtype.itemsize for n in names]
        steps = [min(n_steps - 1, max(1, -(-tenths * n_steps * sum(sizes[: a + 1]) // (10 * sum(sizes))))) for a in range(len(names))]
        return _Gather([bufs[n] for n in names], steps=steps)

    first_names = ("ffn1_w_gate", "ffn1_w_up")
    got = _exchange_call(_Gather([bufs[n] for n in first_names] + [gather_buffer(p["meta_tokens"])]), "gather_first")
    gathered(first_names, got)
    meta = jnp.transpose(got[-1].reshape(4, N_META, -1), (1, 0, 2)).reshape(N_META, D)

    vec = lambda n: p[n].reshape(1, -1)
    G, N, H = p["ssm_b_re"].shape[1:]
    a_re, a_im, bb_re, bb_im = _discretize(p["ssm_lambda_re"][0], p["ssm_lambda_im"][0], p["ssm_log_dt"][0], p["ssm_b_re"][0], p["ssm_b_im"][0])
    tabs, pows = _scan_tables(a_re, a_im, MIX_TILE // SUBLANES)
    bf = lambda a: a.astype(MXU_DTYPE)
    bbre, bbim = bf(_block_diag(bb_re, False)), bf(_block_diag(bb_im, False))
    ccre, ccim = bf(_block_diag(p["ssm_c_re"][0], False)), bf(_block_diag(p["ssm_c_im"][0], False))
    wgv, wgg = bf(_block_diag(p["ssm_w_glu"][0][:, :, :H], True)), bf(_block_diag(p["ssm_w_glu"][0][:, :, H:], True))
    pw = bf(p["pool_w"][0])

    T = x.shape[1] + N_META
    names = ("ffn1_w_down", "w_in", "w_out", "ffn2_w_gate", "ffn2_w_down")
    (ga1, si1, s1, n1), got = _ffn_gate_up(
        x[0], meta, vec("ffn1_pre_norm"), full["ffn1_w_gate"], full["ffn1_w_up"], "ffn1_gate_up", exchange=gather_of(names, T // FFN_TILE, 9)
    )
    gathered(names, got)
    (h1, f1), got = _ffn_down(x[0], meta, s1, vec("ffn1_post_norm"), full["ffn1_w_down"], "ffn1_down", exchange=gather_of(("ffn2_w_up",), T // _token_tile(T), 6))
    gathered(("ffn2_w_up",), got)
    (proj, xr, xim, y, pooled, mixed, h2, n2, cat), _ = _mix_fwd(
        h1, vec("mix_pre_norm"), vec("ssm_out_norm"), vec("pool_out_norm"), vec("mix_post_norm"), vec("ssm_d"), vec("pool_scale"), full["w_in"], full["w_out"],
        bbre, bbim, ccre, ccim, wgv, wgg, pw, tabs, pows, "mix_fwd",
    )
    dh3, sq, f2, ga2, si2, s2, n3 = _ffn_fwd_loss(
        h2, loss_target[0], vec("ffn2_pre_norm"), vec("ffn2_post_norm"), full["ffn2_w_gate"], full["ffn2_w_up"], full["ffn2_w_down"], "ffn2_fwd"
    )

    g, shared = {}, {}
    ffn_names = lambda tag: (tag + "_w_gate", tag + "_w_up", tag + "_w_down")

    da, db, df, dh2, g["ffn2_post_norm"], g["ffn2_pre_norm"] = _ffn_bwd(
        dh3, f2, ga2, si2, h2, vec("ffn2_post_norm"), vec("ffn2_pre_norm"), full["ffn2_w_gate"], full["ffn2_w_up"], full["ffn2_w_down"], "ffn2_bwd"
    )
    views2 = [
        grad_view(_tn_matmul(da, n3, "ffn2_dw_gate")[0]),
        grad_view(_tn_matmul(db, n3, "ffn2_dw_up")[0]),
        grad_view(_tn_matmul(s2, df, "ffn2_dw_down")[0]),
    ]
    (dy, dpooled, dmixed, g["mix_post_norm"], g["ssm_out_norm"], g["pool_out_norm"], g["ssm_d"], g["pool_scale"], dwgv, dwgg, g["pool_w"]), got = _mix_bwd_heads(
        dh2, mixed, y, pooled, proj, vec("ssm_out_norm"), vec("pool_out_norm"), vec("mix_post_norm"), vec("pool_scale"), full["w_out"], wgv, wgg, pw, "mix_bwd_heads",
        exchange=_SiblingScatter(views2),
    )
    parts2, wire2 = reduce_sum(got, views2, "ffn2")
    (dproj, dccre, dccim, dbbre, dbbim, dar, dai), got = _mix_bwd_scan(
        dy, dpooled, xr, xim, proj, vec("ssm_d"), bbre, bbim, ccre, ccim, tabs, pows, "mix_bwd_scan", exchange=_ChipScatter(wire2)
    )
    halves2 = reduce_halves(parts2, got, "ffn2")
    (dh1, g["mix_pre_norm"]), got = _mix_bwd_in(dproj, h1, dh2, vec("mix_pre_norm"), full["w_in"], "mix_bwd_in", exchange=_SiblingShare(halves2))
    shared.update(zip(ffn_names("ffn2"), got))

    da, db, df, grad_x, d_meta, g["ffn1_post_norm"], g["ffn1_pre_norm"] = _ffn_bwd_first(
        dh1, f1, ga1, si1, x[0], meta, vec("ffn1_post_norm"), vec("ffn1_pre_norm"), full["ffn1_w_gate"], full["ffn1_w_up"], full["ffn1_w_down"], "ffn1_bwd"
    )
    grad_x = grad_x[None]

    g["ssm_c_re"] = _block_diag_extract(dccre, N, H, False)
    g["ssm_c_im"] = _block_diag_extract(dccim, N, H, False)
    g["ssm_w_glu"] = jnp.concatenate([_block_diag_extract(dwgv, H, H, True), _block_diag_extract(dwgg, H, H, True)], axis=-1)
    d_a_re, d_a_im = jnp.sum(dar, axis=1).reshape(G, N), jnp.sum(dai, axis=1).reshape(G, N)
    _, pull = jax.vjp(_discretize, p["ssm_lambda_re"][0], p["ssm_lambda_im"][0], p["ssm_log_dt"][0], p["ssm_b_re"][0], p["ssm_b_im"][0])
    g["ssm_lambda_re"], g["ssm_lambda_im"], g["ssm_log_dt"], g["ssm_b_re"], g["ssm_b_im"] = pull(
        (d_a_re, d_a_im, _block_diag_extract(dbbre, H, N, False), _block_diag_extract(dbbim, H, N, False))
    )
    small_shapes = [p[n].shape for n in SMALL] + [(N_META, D), (1,)]
    small_size = sum(math.prod(s) for s in small_shapes)
    rows = -(-small_size // (LANES * PACK_ROWS)) * PACK_ROWS
    views_s = [_pack([g[n] for n in SMALL] + [d_meta, jnp.sum(sq).reshape(1)], rows).reshape(4, 2, rows // 8, LANES)]

    dw_down, got = _tn_matmul(s1, df, "ffn1_dw_down", exchange=_SiblingScatter(views_s))
    parts_s, wire_s = reduce_sum(got, views_s, "small", wires=[F32])
    views_d = [grad_view(dw_down)]
    ex = _Group([_SiblingScatter(views_d), _ChipScatter(wire_s)])
    dw_gate, got = _tn_matmul(da, n1, "ffn1_dw_gate", exchange=ex)
    got_d, got_s = ex.split(got)
    parts_d, wire_d = reduce_sum(got_d, views_d, "ffn1_down")
    halves_s = reduce_halves(parts_s, got_s, "small")
    views_g = [grad_view(dw_gate)]
    ex = _Group([_ChipScatter(wire_d), _SiblingScatter(views_g), _SiblingShare(halves_s)])
    dw_up, got = _tn_matmul(db, n1, "ffn1_dw_up", exchange=ex)
    got_d, got_g, got_s = ex.split(got)
    small_buf = lax.dynamic_update_slice(lax.empty((4,) + got_s[0].shape, F32), got_s[0][None], (chip, 0, 0, 0))
    halves_d = reduce_halves(parts_d, got_d, "ffn1_down")
    parts_g, wire_g = reduce_sum(got_g, views_g, "ffn1_gate")
    views_u = [grad_view(dw_up)]
    tn_steps = (T - N_META) // TN_TILE if (T - N_META) % TN_TILE == 0 else T // _token_tile(T)
    ex = _Group([_SiblingShare(halves_d), _ChipScatter(wire_g), _SiblingScatter(views_u), _Gather([small_buf], steps=[(3 * tn_steps) // 4])])
    dw_in, got = _tn_matmul(n2, dproj, "dw_in", exchange=ex)
    got_d, got_g, got_u, got_s = ex.split(got)
    grads = dict(zip(SMALL + ("meta_full", "sq_sum"), _unpack(got_s[0].reshape(rows, LANES), small_shapes)))
    grads["meta_tokens"] = lax.dynamic_slice_in_dim(grads.pop("meta_full"), chip * (D // 4), D // 4, axis=1)
    loss = (0.5 / D) * grads.pop("sq_sum")[0]
    shared["ffn1_w_down"] = got_d[0]
    halves_g = reduce_halves(parts_g, got_g, "ffn1_gate")
    parts_u, wire_u = reduce_sum(got_u, views_u, "ffn1_up")
    views_i = [grad_view(dw_in)]
    ex = _Group([_SiblingShare(halves_g), _ChipScatter(wire_u), _SiblingScatter(views_i)])
    dw_out, got = _tn_matmul(cat, dmixed, "dw_out", exchange=ex)
    got_g, got_u, got_i = ex.split(got)
    shared["ffn1_w_gate"] = got_g[0]
    halves_u = reduce_halves(parts_u, got_u, "ffn1_up")
    parts_i, wire_i = reduce_sum(got_i, views_i, "w_in")

    views_t = [grad_view(dw_out)]
    ex = _Group([_SiblingScatter(views_t), _SiblingShare(halves_u)])
    got_t, got_u = ex.split(_exchange_call(ex, "tail_reduce_sibling"))
    shared["ffn1_w_up"] = got_u[0]
    parts_t, wire_t = reduce_sum(got_t, views_t, "tail")
    got = _exchange_call(_ChipScatter(wire_i + wire_t), "tail_reduce_chips")
    got = _exchange_call(_SiblingShare(reduce_halves(parts_i + parts_t, got, "tail")), "tail_reduce_share")
    shared["w_in"], shared["w_out"] = got[0], got[1]
    delta, new_m, new_v = {}, {}, {}
    for n in BIG:
        g_rows = shared[n].reshape(-1, shared[n].shape[-1])
        outs = _adamw(rows_of(n, p[n]), g_rows, rows_of(n, m[n]), rows_of(n, v[n]), "adamw_" + n)
        grads[n], delta[n], new_m[n], new_v[n] = (rows_back(n, a) for a in (g_rows, *outs))
    delta["meta_tokens"], new_m["meta_tokens"], new_v["meta_tokens"] = _adamw(p["meta_tokens"], grads["meta_tokens"], m["meta_tokens"], v["meta_tokens"], "adamw_meta_tokens")

    def as_2d(n, a):
        a = a.reshape(p[n].shape)[0]
        if n in ("ssm_b_re", "ssm_b_im"):
            a = jnp.swapaxes(a, 1, 2)
        return a.reshape(-1, a.shape[-1])

    def from_2d(n, a):
        if n in ("ssm_b_re", "ssm_b_im"):
            g_, n_, h_ = p[n].shape[1:]
            return jnp.swapaxes(a.reshape(g_, h_, n_), 1, 2)[None]
        return a.reshape(p[n].shape)

    outs = _adamw_many(*[[as_2d(n, t[n]) for n in SMALL] for t in (p, grads, m, v)], "adamw_small")
    for out, arrays in zip((delta, new_m, new_v), outs):
        out.update({n: from_2d(n, a) for n, a in zip(SMALL, arrays)})

    return (loss, grad_x, *[grads[n] for n in ORDER], *[delta[n] for n in ORDER], *[new_m[n] for n in ORDER], *[new_v[n] for n in ORDER])


def kernel(x, meta_tokens, ffn1_pre_norm, ffn1_post_norm, ffn1_w_gate, ffn1_w_up, ffn1_w_down, mix_pre_norm, mix_post_norm, w_in, ssm_lambda_re, ssm_lambda_im, ssm_log_dt, ssm_b_re, ssm_b_im, ssm_c_re, ssm_c_im, ssm_d, ssm_w_glu, pool_w, pool_scale, ssm_out_norm, pool_out_norm, w_out, ffn2_pre_norm, ffn2_post_norm, ffn2_w_gate, ffn2_w_up, ffn2_w_down, loss_target, m_meta_tokens, m_ffn1_pre_norm, m_ffn1_post_norm, m_ffn1_w_gate, m_ffn1_w_up, m_ffn1_w_down, m_mix_pre_norm, m_mix_post_norm, m_w_in, m_ssm_lambda_re, m_ssm_lambda_im, m_ssm_log_dt, m_ssm_b_re, m_ssm_b_im, m_ssm_c_re, m_ssm_c_im, m_ssm_d, m_ssm_w_glu, m_pool_w, m_pool_scale, m_ssm_out_norm, m_pool_out_norm, m_w_out, m_ffn2_pre_norm, m_ffn2_post_norm, m_ffn2_w_gate, m_ffn2_w_up, m_ffn2_w_down, v_meta_tokens, v_ffn1_pre_norm, v_ffn1_post_norm, v_ffn1_w_gate, v_ffn1_w_up, v_ffn1_w_down, v_mix_pre_norm, v_mix_post_norm, v_w_in, v_ssm_lambda_re, v_ssm_lambda_im, v_ssm_log_dt, v_ssm_b_re, v_ssm_b_im, v_ssm_c_re, v_ssm_c_im, v_ssm_d, v_ssm_w_glu, v_pool_w, v_pool_scale, v_ssm_out_norm, v_pool_out_norm, v_w_out, v_ffn2_pre_norm, v_ffn2_post_norm, v_ffn2_w_gate, v_ffn2_w_up, v_ffn2_w_down):
    args = locals()
    p = {n: args[n] for n in ORDER}
    m = {n: args["m_" + n] for n in ORDER}
    v = {n: args["v_" + n] for n in ORDER}
    return _step(p, x, loss_target, m, v)
```

```python
import math

import jax
import jax.numpy as jnp
from jax import lax
from jax.experimental import pallas as pl
from jax.experimental.pallas import tpu as pltpu

F32 = jnp.float32
MXU_DTYPE = jnp.bfloat16
WIRE_DTYPE = jnp.bfloat16

RMS_EPS = 1e-6
N_META = 16
POOL_WINDOWS = (2, 4, 8, 16)
POOL_HALO = 16
ADAM_LR, ADAM_B1, ADAM_B2, ADAM_EPS, ADAM_WD, ADAM_STEP = 0.001, 0.9, 0.999, 1e-08, 0.01, 10

LANES = 128
SUBLANES = 8
VMEM_LIMIT = 60 * 1024 * 1024
FFN_TILE = 432
FFN_CHUNK = 1024
TN_TILE = 1024
MIX_TILE = 216
MIX_SUBTILES = 2
SLAB_GROUP = 8
MESH = pl.DeviceIdType.MESH
ANY = pl.BlockSpec(memory_space=pl.ANY)


def _mm(a, b):
    return jnp.dot(a.astype(MXU_DTYPE), b.astype(MXU_DTYPE), preferred_element_type=F32)


def _mm_nt(a, b):
    return lax.dot_general(a.astype(MXU_DTYPE), b.astype(MXU_DTYPE), (((1,), (1,)), ((), ())), preferred_element_type=F32)


def _mm_tn(a, b):
    return lax.dot_general(a.astype(MXU_DTYPE), b.astype(MXU_DTYPE), (((0,), (0,)), ((), ())), preferred_element_type=F32)


def _rms_stat(x):
    return lax.rsqrt(jnp.mean(x * x, axis=-1, keepdims=True) + RMS_EPS)


def _rms_bwd(x, g, dy):
    r = _rms_stat(x)
    xh = x * r
    dg = jnp.sum(dy * xh, axis=0, keepdims=True)
    dxh = dy * g
    dx = r * (dxh - xh * jnp.mean(dxh * xh, axis=-1, keepdims=True))
    return dx, dg


def _sigmoid(x):
    return 1.0 / (1.0 + jnp.exp(-x))


GELU_C = math.sqrt(2.0 / math.pi)
GELU_K = 0.044715


def _gelu(y):
    return 0.5 * y * (1.0 + jnp.tanh(GELU_C * (y + GELU_K * y * y * y)))


def _gelu_grad(y):
    th = jnp.tanh(GELU_C * (y + GELU_K * y * y * y))
    return 0.5 * (1.0 + th) + 0.5 * y * (1.0 - th * th) * GELU_C * (1.0 + 3.0 * GELU_K * y * y)


def _row_spec(tile, cols, rev_n=None):
    if rev_n is None:
        return pl.BlockSpec((tile, cols), lambda i: (i, 0))
    return pl.BlockSpec((tile, cols), lambda i: (rev_n - 1 - i, 0))


def _full_spec(shape, single=False):
    zeros = (0,) * len(shape)
    if single:
        return pl.BlockSpec(shape, lambda *_: zeros, pipeline_mode=pl.Buffered(1))
    return pl.BlockSpec(shape, lambda *_: zeros)


def _acc(ref, val, first):
    @pl.when(first)
    def _():
        ref[...] = val

    @pl.when(jnp.logical_not(first))
    def _():
        ref[...] += val


def _place():
    x, y, c = lax.axis_index("x"), lax.axis_index("y"), lax.axis_index("c")
    others = [(1 - x, y), (x, 1 - y), (1 - x, 1 - y)]
    return x, y, c, others


class _Exchange:
    def __init__(self, ins, out_shapes, aliases, n_sems):
        self.ins, self.out_shapes, self.aliases, self.n_sems = list(ins), list(out_shapes), dict(aliases), n_sems

    def mid_steps(self):
        return []

    def mid(self, ins, outs, send_sems, recv_sems, step=None):
        pass


class _SiblingScatter(_Exchange):
    def __init__(self, views):
        super().__init__(views, [jax.ShapeDtypeStruct((4,) + v.shape[2:], v.dtype) for v in views], {}, 4 * len(views))

    def _copies(self, ins, outs, send_sems, recv_sems):
        x, y, c, _ = _place()
        return [
            pltpu.make_async_remote_copy(src_ref=ins[a].at[k, 1 - c], dst_ref=outs[a].at[k], send_sem=send_sems.at[4 * a + k], recv_sem=recv_sems.at[4 * a + k], device_id=(x, y, 1 - c), device_id_type=MESH)
            for a in range(len(ins))
            for k in range(4)
        ]

    def start(self, *refs):
        for cp in self._copies(*refs):
            cp.start()

    def finish(self, *refs):
        cps = self._copies(*refs)
        for cp in cps:
            cp.wait_recv()
        for cp in cps:
            cp.wait_send()


class _ChipScatter(_Exchange):
    def __init__(self, parts):
        super().__init__(parts, [jax.ShapeDtypeStruct((3,) + p.shape[1:], p.dtype) for p in parts], {}, 3 * len(parts))

    def _copies(self, ins, outs, send_sems, recv_sems):
        x, y, c, others = _place()
        return [
            pltpu.make_async_remote_copy(src_ref=ins[a].at[2 * chip[0] + chip[1]], dst_ref=outs[a].at[j], send_sem=send_sems.at[3 * a + j], recv_sem=recv_sems.at[3 * a + j], device_id=(*chip, c), device_id_type=MESH)
            for a in range(len(ins))
            for j, chip in enumerate(others)
        ]

    start = _SiblingScatter.start
    finish = _SiblingScatter.finish


class _SiblingShare(_Exchange):
    def __init__(self, bufs):
        super().__init__(bufs, [jax.ShapeDtypeStruct(b.shape, b.dtype) for b in bufs], {a: a for a in range(len(bufs))}, len(bufs))

    def _copy(self, outs, send_sems, recv_sems, a, half):
        x, y, c, _ = _place()
        mine = outs[a].at[c if half == "mine" else 1 - c]
        return pltpu.make_async_remote_copy(src_ref=mine, dst_ref=mine, send_sem=send_sems.at[a], recv_sem=recv_sems.at[a], device_id=(x, y, 1 - c), device_id_type=MESH)

    def start(self, ins, outs, send_sems, recv_sems):
        for a in range(len(outs)):
            self._copy(outs, send_sems, recv_sems, a, "mine").start()

    def finish(self, ins, outs, send_sems, recv_sems):
        for a in range(len(outs)):
            self._copy(outs, send_sems, recv_sems, a, "theirs").wait_recv()
        for a in range(len(outs)):
            self._copy(outs, send_sems, recv_sems, a, "mine").wait_send()


class _Gather(_Exchange):
    def __init__(self, bufs, steps=None):
        super().__init__(bufs, [jax.ShapeDtypeStruct(b.shape, b.dtype) for b in bufs], {a: a for a in range(len(bufs))}, 6 * len(bufs))
        self.steps = steps

    def mid_steps(self):
        return sorted(set(self.steps or []))

    def _copy(self, outs, send_sems, recv_sems, a, j, chip, half, to):
        blk = outs[a].at[2 * chip[0] + chip[1], half]
        return pltpu.make_async_remote_copy(src_ref=blk, dst_ref=blk, send_sem=send_sems.at[6 * a + j], recv_sem=recv_sems.at[6 * a + j], device_id=to, device_id_type=MESH)

    def start(self, ins, outs, send_sems, recv_sems):
        x, y, c, others = _place()
        for a in range(len(outs)):
            for j, chip in enumerate(others):
                self._copy(outs, send_sems, recv_sems, a, j, (x, y), c, (*chip, c)).start()

    def mid(self, ins, outs, send_sems, recv_sems, step=None):
        x, y, c, others = _place()
        for a in range(len(outs)):
            if step is not None and self.steps[a] != step:
                continue
            for j, chip in enumerate(others):
                self._copy(outs, send_sems, recv_sems, a, j, chip, c, (x, y, c)).wait_recv()
                self._copy(outs, send_sems, recv_sems, a, 3 + j, chip, c, (x, y, 1 - c)).start()

    def finish(self, ins, outs, send_sems, recv_sems):
        x, y, c, others = _place()
        for a in range(len(outs)):
            for j, chip in enumerate(others):
                self._copy(outs, send_sems, recv_sems, a, 3 + j, chip, 1 - c, (x, y, c)).wait_recv()
        for a in range(len(outs)):
            for j, chip in enumerate(others):
                self._copy(outs, send_sems, recv_sems, a, j, (x, y), c, (*chip, c)).wait_send()
                self._copy(outs, send_sems, recv_sems, a, 3 + j, chip, c, (x, y, 1 - c)).wait_send()


class _SemSlice:
    def __init__(self, sems, off):
        self.sems, self.off = sems, off

    @property
    def at(self):
        return self

    def __getitem__(self, i):
        return self.sems.at[self.off + i]


class _Group(_Exchange):
    def __init__(self, exchanges):
        ins, outs, aliases, n_sems, self.spans = [], [], {}, 0, []
        for ex in exchanges:
            self.spans.append((len(ins), len(outs), n_sems))
            aliases.update({len(ins) + i: len(outs) + o for i, o in ex.aliases.items()})
            ins, outs, n_sems = ins + ex.ins, outs + ex.out_shapes, n_sems + ex.n_sems
        super().__init__(ins, outs, aliases, n_sems)
        self.exchanges = exchanges

    def mid_steps(self):
        return sorted({s for ex in self.exchanges for s in ex.mid_steps()})

    def _each(self, method, ins, outs, send_sems, recv_sems, **kw):
        for ex, (i0, o0, s0) in zip(self.exchanges, self.spans):
            getattr(ex, method)(ins[i0 : i0 + len(ex.ins)], outs[o0 : o0 + len(ex.out_shapes)], _SemSlice(send_sems, s0), _SemSlice(recv_sems, s0), **kw)

    def start(self, *refs):
        self._each("start", *refs)

    def mid(self, *refs, step=None):
        self._each("mid", *refs, step=step)

    def finish(self, *refs):
        self._each("finish", *refs)

    def split(self, outs):
        return [outs[o0 : o0 + len(ex.out_shapes)] for ex, (_, o0, _) in zip(self.exchanges, self.spans)]


def _exchange_call(ex, name):
    n, m = len(ex.ins), len(ex.out_shapes)

    def body(*refs):
        parts = (refs[:n], refs[n : n + m], refs[n + m], refs[n + m + 1])
        ex.start(*parts)
        ex.mid(*parts)
        ex.finish(*parts)

    return pl.pallas_call(
        body,
        name=name,
        out_shape=ex.out_shapes,
        in_specs=[ANY] * n,
        out_specs=[ANY] * m,
        scratch_shapes=[pltpu.SemaphoreType.DMA((ex.n_sems,)), pltpu.SemaphoreType.DMA((ex.n_sems,))],
        input_output_aliases=ex.aliases,
    )(*ex.ins)


def _pallas(body, *, name, grid, in_specs, out_specs, out_shape, operands, scratch_shapes=(), exchange=None):
    params = pltpu.CompilerParams(dimension_semantics=("arbitrary",) * len(grid), vmem_limit_bytes=VMEM_LIMIT)
    if exchange is None:
        outs = pl.pallas_call(body, name=name, grid=grid, in_specs=in_specs, out_specs=out_specs, out_shape=out_shape, scratch_shapes=list(scratch_shapes), compiler_params=params)(*operands)
        return outs, []
    ex = exchange
    n_in, n_out, n_scr = len(in_specs), len(out_specs), len(scratch_shapes)
    x_in, x_out = len(ex.ins), len(ex.out_shapes)

    def hosted(*refs):
        ins, x_ins = refs[:n_in], refs[n_in : n_in + x_in]
        outs, x_outs = refs[n_in + x_in : n_in + x_in + n_out], refs[n_in + x_in + n_out : n_in + x_in + n_out + x_out]
        rest = refs[n_in + x_in + n_out + x_out :]
        parts = (x_ins, x_outs, rest[n_scr], rest[n_scr + 1])
        ids = [pl.program_id(d) for d in range(len(grid))]
        first = _all([i == 0 for i in ids])
        last = _all([i == g - 1 for i, g in zip(ids, grid)])

        @pl.when(first)
        def _():
            ex.start(*parts)

        body(*ins, *outs, *rest[:n_scr])

        for step in ex.mid_steps():

            @pl.when(ids[0] == step)
            def _(step=step):
                ex.mid(*parts, step=step)

        @pl.when(last)
        def _():
            ex.finish(*parts)

    outs = pl.pallas_call(
        hosted,
        name=name,
        grid=grid,
        in_specs=list(in_specs) + [ANY] * x_in,
        out_specs=list(out_specs) + [ANY] * x_out,
        out_shape=list(out_shape) + ex.out_shapes,
        scratch_shapes=list(scratch_shapes) + [pltpu.SemaphoreType.DMA((ex.n_sems,)), pltpu.SemaphoreType.DMA((ex.n_sems,))],
        input_output_aliases={n_in + i: n_out + o for i, o in ex.aliases.items()},
        compiler_params=params,
    )(*operands, *ex.ins)
    return outs[:n_out], outs[n_out:]


def _all(conds):
    out = conds[0]
    for c in conds[1:]:
        out = jnp.logical_and(out, c)
    return out


def _row_tile(rows):
    if rows <= 512:
        return rows
    for t in (512, 352, 256, 176, 128, 112, 64, 32, 16, 8):
        if rows % t == 0:
            return t
    return rows


def _add_own_half(view, got, place, name, wire=WIRE_DTYPE):
    _, _, r, c = view.shape
    tr = _row_tile(r)

    def body(place_ref, v_ref, g_ref, o_ref, w_ref):
        s = v_ref[...] + g_ref[...]
        w_ref[...] = s.astype(w_ref.dtype)

        @pl.when(pl.program_id(1) == place_ref[0])
        def _():
            o_ref[...] = s

    blk = pl.BlockSpec((None, tr, c), lambda i, k, pr: (k, i, 0))
    return pl.pallas_call(
        body,
        name=name,
        out_shape=[jax.ShapeDtypeStruct((r, c), F32), jax.ShapeDtypeStruct((4, r, c), wire)],
        grid_spec=pltpu.PrefetchScalarGridSpec(
            num_scalar_prefetch=1,
            grid=(r // tr, 4),
            in_specs=[pl.BlockSpec((None, None, tr, c), lambda i, k, pr: (k, pr[1], i, 0)), blk],
            out_specs=[pl.BlockSpec((tr, c), lambda i, k, pr: (i, 0)), blk],
        ),
        compiler_params=pltpu.CompilerParams(dimension_semantics=("arbitrary", "arbitrary"), vmem_limit_bytes=VMEM_LIMIT),
    )(place, view, got)


def _add_chips(part, got, place, name):
    r, c = part.shape
    tr = _row_tile(r)

    def body(place_ref, p_ref, g_ref, o_ref):
        o_ref[...] = ((p_ref[...] + g_ref[0].astype(F32)) + g_ref[1].astype(F32)) + g_ref[2].astype(F32)

    return pl.pallas_call(
        body,
        name=name,
        out_shape=jax.ShapeDtypeStruct((2, r, c), F32),
        grid_spec=pltpu.PrefetchScalarGridSpec(
            num_scalar_prefetch=1,
            grid=(r // tr,),
            in_specs=[pl.BlockSpec((tr, c), lambda i, pr: (i, 0)), pl.BlockSpec((3, tr, c), lambda i, pr: (0, i, 0))],
            out_specs=pl.BlockSpec((None, tr, c), lambda i, pr: (pr[1], i, 0)),
        ),
        compiler_params=pltpu.CompilerParams(dimension_semantics=("arbitrary",), vmem_limit_bytes=VMEM_LIMIT),
    )(place, part, got)


def _adamw_update(w_ref, g_ref, m_ref, v_ref, d_ref, nm_ref, nv_ref):
    g = g_ref[...]
    nm = ADAM_B1 * m_ref[...] + (1.0 - ADAM_B1) * g
    nv = ADAM_B2 * v_ref[...] + (1.0 - ADAM_B2) * (g * g)
    m_hat = nm / (1.0 - ADAM_B1**ADAM_STEP)
    v_hat = nv / (1.0 - ADAM_B2**ADAM_STEP)
    d_ref[...] = -ADAM_LR * (m_hat / (jnp.sqrt(v_hat) + ADAM_EPS) + ADAM_WD * w_ref[...])
    nm_ref[...] = nm
    nv_ref[...] = nv


def _adamw(w, g, m, v, name):
    r, c = w.shape
    tr = _row_tile(r)
    spec = pl.BlockSpec((tr, c), lambda i: (i, 0))
    outs, _ = _pallas(_adamw_update, name=name, grid=(r // tr,), in_specs=[spec] * 4, out_specs=[spec] * 3, out_shape=[jax.ShapeDtypeStruct((r, c), F32)] * 3, operands=(w, g, m, v))
    return outs


def _adamw_many(ws, gs, ms, vs, name):
    n = len(ws)

    def body(*refs):
        for k in range(n):
            _adamw_update(*(refs[j * n + k] for j in range(7)))

    outs = pl.pallas_call(
        body,
        name=name,
        out_shape=[jax.ShapeDtypeStruct(w.shape, F32) for w in ws] * 3,
        in_specs=[pl.BlockSpec(memory_space=pltpu.VMEM)] * (4 * n),
        out_specs=[pl.BlockSpec(memory_space=pltpu.VMEM)] * (3 * n),
    )(*ws, *gs, *ms, *vs)
    return outs[:n], outs[n : 2 * n], outs[2 * n :]


def _load_weights(pairs, sems):
    @pl.when(pl.program_id(0) == 0)
    def _():
        cps = [pltpu.make_async_copy(src, dst, sems.at[k]) for k, (src, dst) in enumerate(pairs)]
        for cp in cps:
            cp.start()
        for cp in cps:
            cp.wait()


def _ffn_chunks(F):
    bounds = list(range(0, F, FFN_CHUNK)) + [F]
    return list(zip(bounds[:-1], bounds[1:]))


def _shifted_specs(tm, cols):
    per = tm // N_META
    return [_row_spec(tm, cols), pl.BlockSpec((N_META, cols), lambda i: (jnp.maximum(i * per - 1, 0), 0))]


def _shifted_tile(cur_ref, before_ref, tm):
    return jnp.concatenate([before_ref[...], cur_ref[0 : tm - N_META, :]], axis=0)


def _tokens_tile(cur_ref, before_ref, meta_ref, tm, tile_0):
    first = jnp.where(tile_0, meta_ref[...], before_ref[...])
    return jnp.concatenate([first, cur_ref[0 : tm - N_META, :]], axis=0)


def _ffn_fwd_loss(h, target, g_pre, g_post, wg, wu, wd, name):
    T, D = h.shape
    F = wg.shape[0]
    tm = FFN_TILE

    def body(h_ref, t_ref, tb_ref, gpre_ref, gpost_ref, wg_hbm, wu_hbm, wd_hbm, dy_ref, sq_ref, f_ref, ga_ref, si_ref, s_ref, n_ref, wg_v, wu_v, wd_v, sems):
        i = pl.program_id(0)
        _load_weights([(wg_hbm, wg_v), (wu_hbm, wu_v), (wd_hbm, wd_v)], sems)
        hh = h_ref[...]
        n = (hh * _rms_stat(hh) * gpre_ref[...]).astype(MXU_DTYPE)
        n_ref[...] = n.astype(n_ref.dtype)
        f = jnp.zeros((tm, D), F32)
        for lo, hi in _ffn_chunks(F):
            a = _mm_nt(n, wg_v[lo:hi, :])
            b = _mm_nt(n, wu_v[lo:hi, :])
            sg = _sigmoid(a)
            si = a * sg
            s = (si * b).astype(MXU_DTYPE)
            ga_ref[:, lo:hi] = (b * (sg * (1.0 + a * (1.0 - sg)))).astype(ga_ref.dtype)
            si_ref[:, lo:hi] = si.astype(si_ref.dtype)
            s_ref[:, lo:hi] = s.astype(s_ref.dtype)
            f = f + _mm(s, wd_v[lo:hi, :])
        f_ref[...] = f
        y = hh + 0.5 * (f * _rms_stat(f) * gpost_ref[...])
        rows = i * tm + lax.broadcasted_iota(jnp.int32, (tm, D), 0)
        err = jnp.where(rows >= N_META, y - _shifted_tile(t_ref, tb_ref, tm), 0.0)
        dy_ref[...] = err * (1.0 / D)
        _acc(sq_ref, jnp.sum(err * err, axis=0, keepdims=True), i == 0)

    tok = jax.ShapeDtypeStruct((T, D), F32)
    act = jax.ShapeDtypeStruct((T, F), MXU_DTYPE)
    outs, _ = _pallas(
        body,
        name=name,
        grid=(T // tm,),
        out_shape=[tok, jax.ShapeDtypeStruct((1, D), F32), tok, act, act, act, jax.ShapeDtypeStruct((T, D), MXU_DTYPE)],
        in_specs=[_row_spec(tm, D)] + _shifted_specs(tm, D) + [_full_spec((1, D)), _full_spec((1, D)), ANY, ANY, ANY],
        out_specs=[_row_spec(tm, D), _full_spec((1, D)), _row_spec(tm, D), _row_spec(tm, F), _row_spec(tm, F), _row_spec(tm, F), _row_spec(tm, D)],
        scratch_shapes=[pltpu.VMEM(wg.shape, wg.dtype), pltpu.VMEM(wu.shape, wu.dtype), pltpu.VMEM(wd.shape, wd.dtype), pltpu.SemaphoreType.DMA((3,))],
        operands=(h, target, target, g_pre, g_post, wg, wu, wd),
    )
    return outs


def _ffn_gate_up(x, meta, g_pre, wg, wu, name, exchange=None):
    D = x.shape[1]
    T = x.shape[0] + N_META
    F = wg.shape[0]
    tm = FFN_TILE

    def body(x_ref, xb_ref, meta_ref, gpre_ref, wg_hbm, wu_hbm, ga_ref, si_ref, s_ref, n_ref, wg_v, wu_v, sems):
        _load_weights([(wg_hbm, wg_v), (wu_hbm, wu_v)], sems)
        hh = _tokens_tile(x_ref, xb_ref, meta_ref, tm, pl.program_id(0) == 0)
        n = (hh * _rms_stat(hh) * gpre_ref[...]).astype(MXU_DTYPE)
        n_ref[...] = n.astype(n_ref.dtype)
        for lo, hi in _ffn_chunks(F):
            a = _mm_nt(n, wg_v[lo:hi, :])
            b = _mm_nt(n, wu_v[lo:hi, :])
            sg = _sigmoid(a)
            si = a * sg
            ga_ref[:, lo:hi] = (b * (sg * (1.0 + a * (1.0 - sg)))).astype(ga_ref.dtype)
            si_ref[:, lo:hi] = si.astype(si_ref.dtype)
            s_ref[:, lo:hi] = (si * b).astype(s_ref.dtype)

    act = jax.ShapeDtypeStruct((T, F), MXU_DTYPE)
    return _pallas(
        body,
        name=name,
        grid=(T // tm,),
        out_shape=[act, act, act, jax.ShapeDtypeStruct((T, D), MXU_DTYPE)],
        in_specs=_shifted_specs(tm, D) + [_full_spec((N_META, D)), _full_spec((1, D)), ANY, ANY],
        out_specs=[_row_spec(tm, F), _row_spec(tm, F), _row_spec(tm, F), _row_spec(tm, D)],
        scratch_shapes=[pltpu.VMEM(wg.shape, wg.dtype), pltpu.VMEM(wu.shape, wu.dtype), pltpu.SemaphoreType.DMA((2,))],
        operands=(x, x, meta, g_pre, wg, wu),
        exchange=exchange,
    )


def _ffn_down(x, meta, s, g_post, wd, name, exchange=None):
    D = x.shape[1]
    T = x.shape[0] + N_META
    F = wd.shape[0]
    tm = _token_tile(T)

    def body(x_ref, xb_ref, meta_ref, s_ref, gpost_ref, wd_hbm, hout_ref, f_ref, wd_v, sems):
        _load_weights([(wd_hbm, wd_v)], sems)
        f = _mm(s_ref[...], wd_v[...])
        f_ref[...] = f
        hout_ref[...] = _tokens_tile(x_ref, xb_ref, meta_ref, tm, pl.program_id(0) == 0) + 0.5 * (f * _rms_stat(f) * gpost_ref[...])

    tok = jax.ShapeDtypeStruct((T, D), F32)
    return _pallas(
        body,
        name=name,
        grid=(T // tm,),
        out_shape=[tok, tok],
        in_specs=_shifted_specs(tm, D) + [_full_spec((N_META, D)), _row_spec(tm, F), _full_spec((1, D)), ANY],
        out_specs=[_row_spec(tm, D), _row_spec(tm, D)],
        scratch_shapes=[pltpu.VMEM(wd.shape, wd.dtype), pltpu.SemaphoreType.DMA((1,))],
        operands=(x, x, meta, s, g_post, wd),
        exchange=exchange,
    )


def _ffn_bwd(dh, f, ga, si, h, g_post, g_pre, wg, wu, wd, name):
    T, D = dh.shape
    F = wd.shape[0]
    tm = FFN_TILE

    def body(dh_ref, f_ref, ga_ref, si_ref, h_ref, gpost_ref, gpre_ref, wg_hbm, wu_hbm, wd_hbm, da_ref, db_ref, df_ref, dhin_ref, dgpost_ref, dgpre_ref, wg_v, wu_v, wd_v, sems):
        first = pl.program_id(0) == 0
        _load_weights([(wg_hbm, wg_v), (wu_hbm, wu_v), (wd_hbm, wd_v)], sems)
        dh = dh_ref[...]
        df, dg = _rms_bwd(f_ref[...], gpost_ref[...], 0.5 * dh)
        _acc(dgpost_ref, dg, first)
        dfb = df.astype(MXU_DTYPE)
        df_ref[...] = dfb.astype(df_ref.dtype)
        dn = jnp.zeros((tm, D), F32)
        for lo, hi in _ffn_chunks(F):
            ds = _mm_nt(dfb, wd_v[lo:hi, :])
            da = (ds * ga_ref[:, lo:hi].astype(F32)).astype(MXU_DTYPE)
            db = (ds * si_ref[:, lo:hi].astype(F32)).astype(MXU_DTYPE)
            da_ref[:, lo:hi] = da.astype(da_ref.dtype)
            db_ref[:, lo:hi] = db.astype(db_ref.dtype)
            dn = dn + _mm(da, wg_v[lo:hi, :]) + _mm(db, wu_v[lo:hi, :])
        dx, dg = _rms_bwd(h_ref[...], gpre_ref[...], dn)
        _acc(dgpre_ref, dg, first)
        dhin_ref[...] = dh + dx

    act = jax.ShapeDtypeStruct((T, F), MXU_DTYPE)
    vec = jax.ShapeDtypeStruct((1, D), F32)
    outs, _ = _pallas(
        body,
        name=name,
        grid=(T // tm,),
        out_shape=[act, act, jax.ShapeDtypeStruct((T, D), MXU_DTYPE), jax.ShapeDtypeStruct((T, D), F32), vec, vec],
        in_specs=[_row_spec(tm, D), _row_spec(tm, D), _row_spec(tm, F), _row_spec(tm, F), _row_spec(tm, D), _full_spec((1, D)), _full_spec((1, D)), ANY, ANY, ANY],
        out_specs=[_row_spec(tm, F), _row_spec(tm, F), _row_spec(tm, D), _row_spec(tm, D), _full_spec((1, D)), _full_spec((1, D))],
        scratch_shapes=[pltpu.VMEM(wg.shape, wg.dtype), pltpu.VMEM(wu.shape, wu.dtype), pltpu.VMEM(wd.shape, wd.dtype), pltpu.SemaphoreType.DMA((3,))],
        operands=(dh, f, ga, si, h, g_post, g_pre, wg, wu, wd),
    )
    return outs


def _ffn_bwd_first(dh, f, ga, si, x, meta, g_post, g_pre, wg, wu, wd, name):
    D = x.shape[1]
    T = x.shape[0] + N_META
    F = wd.shape[0]
    tm = FFN_TILE
    nt = T // tm
    per = tm // N_META
    tile = lambda i: jnp.minimum(i, nt - 1)

    def body(dh_ref, f_ref, ga_ref, si_ref, x_ref, xb_ref, meta_ref, gpost_ref, gpre_ref, wg_hbm, wu_hbm, wd_hbm,
             da_ref, db_ref, df_ref, dx_ref, dmeta_ref, dgpost_ref, dgpre_ref, wg_v, wu_v, wd_v, sems, held):
        i = pl.program_id(0)
        _load_weights([(wg_hbm, wg_v), (wu_hbm, wu_v), (wd_hbm, wd_v)], sems)

        @pl.when(i < nt)
        def _():
            dh = dh_ref[...]
            df, dg = _rms_bwd(f_ref[...], gpost_ref[...], 0.5 * dh)
            _acc(dgpost_ref, dg, i == 0)
            dfb = df.astype(MXU_DTYPE)
            df_ref[...] = dfb.astype(df_ref.dtype)
            dn = jnp.zeros((tm, D), F32)
            for lo, hi in _ffn_chunks(F):
                ds = _mm_nt(dfb, wd_v[lo:hi, :])
                da = (ds * ga_ref[:, lo:hi].astype(F32)).astype(MXU_DTYPE)
                db = (ds * si_ref[:, lo:hi].astype(F32)).astype(MXU_DTYPE)
                da_ref[:, lo:hi] = da.astype(da_ref.dtype)
                db_ref[:, lo:hi] = db.astype(db_ref.dtype)
                dn = dn + _mm(da, wg_v[lo:hi, :]) + _mm(db, wu_v[lo:hi, :])
            dx, dg = _rms_bwd(_tokens_tile(x_ref, xb_ref, meta_ref, tm, i == 0), gpre_ref[...], dn)
            _acc(dgpre_ref, dg, i == 0)
            dh_in = dh + dx

            @pl.when(i == 0)
            def _():
                dmeta_ref[...] = dh_in[0:N_META, :]

            @pl.when(i > 0)
            def _():
                dx_ref[...] = jnp.concatenate([held[...], dh_in[0:N_META, :]], axis=0)

            held[...] = dh_in[N_META:, :]

        @pl.when(i == nt)
        def _():
            dx_ref[0 : tm - N_META, :] = held[...]

    rows = lambda cols: pl.BlockSpec((tm, cols), lambda i: (tile(i), 0))
    act = jax.ShapeDtypeStruct((T, F), MXU_DTYPE)
    vec = jax.ShapeDtypeStruct((1, D), F32)
    outs, _ = _pallas(
        body,
        name=name,
        grid=(nt + 1,),
        out_shape=[act, act, jax.ShapeDtypeStruct((T, D), MXU_DTYPE), jax.ShapeDtypeStruct((T - N_META, D), F32), jax.ShapeDtypeStruct((N_META, D), F32), vec, vec],
        in_specs=[rows(D), rows(D), rows(F), rows(F), rows(D), pl.BlockSpec((N_META, D), lambda i: (jnp.maximum(tile(i) * per - 1, 0), 0)), _full_spec((N_META, D)),
                  _full_spec((1, D)), _full_spec((1, D)), ANY, ANY, ANY],
        out_specs=[rows(F), rows(F), rows(D), pl.BlockSpec((tm, D), lambda i: (jnp.maximum(i - 1, 0), 0)), _full_spec((N_META, D)), _full_spec((1, D)), _full_spec((1, D))],
        scratch_shapes=[pltpu.VMEM(wg.shape, wg.dtype), pltpu.VMEM(wu.shape, wu.dtype), pltpu.VMEM(wd.shape, wd.dtype), pltpu.SemaphoreType.DMA((3,)), pltpu.VMEM((tm - N_META, D), F32)],
        operands=(dh, f, ga, si, x, x, meta, g_post, g_pre, wg, wu, wd),
    )
    return outs


def _token_tile(T):
    for t in (912, 864, 432):
        if T % t == 0:
            return t
    raise ValueError(f"no token tile for {T} rows")


def _tn_matmul(xm, ym, name, exchange=None):
    T, M = xm.shape
    N = ym.shape[1]
    if (T - N_META) % TN_TILE:
        tk = _token_tile(T)

        def body(x_ref, y_ref, o_ref):
            _acc(o_ref, _mm_tn(x_ref[...], y_ref[...]), pl.program_id(0) == 0)

        grid, operands = (T // tk,), (xm, ym)
        in_specs = [pl.BlockSpec((tk, M), lambda k: (k, 0)), pl.BlockSpec((tk, N), lambda k: (k, 0))]
    else:
        tk = TN_TILE

        def body(x_ref, y_ref, xh_ref, yh_ref, o_ref):
            prod = _mm_tn(x_ref[...], y_ref[...])

            @pl.when(pl.program_id(0) == 0)
            def _():
                o_ref[...] = prod + _mm_tn(xh_ref[...], yh_ref[...])

            @pl.when(pl.program_id(0) > 0)
            def _():
                o_ref[...] += prod

        grid, operands = ((T - N_META) // tk,), (xm, ym, xm, ym)
        start = lambda k: (pl.multiple_of(N_META + k * tk, N_META), 0)
        in_specs = [pl.BlockSpec((pl.Element(tk), pl.Element(M)), start), pl.BlockSpec((pl.Element(tk), pl.Element(N)), start),
                    pl.BlockSpec((N_META, M), lambda k: (0, 0)), pl.BlockSpec((N_META, N), lambda k: (0, 0))]

    (out,), x_outs = _pallas(
        body,
        name=name,
        grid=grid,
        out_shape=[jax.ShapeDtypeStruct((M, N), F32)],
        in_specs=in_specs,
        out_specs=[_full_spec((M, N))],
        operands=operands,
        exchange=exchange,
    )
    return out, x_outs


TAB_A, TAB_AS1, TAB_AS2, TAB_AS4, TAB_JF, TAB_JB = 0, 2, 4, 6, 8, 10


def _scan_inplace(zr, zi, tabs, pows, car_r, car_i, seg, reverse, base=0):
    n_slabs = zr.shape[0]
    sgn = -1.0 if reverse else 1.0
    row = lax.broadcasted_iota(jnp.int32, (SUBLANES, LANES), 0)

    def cmul(pr, pi, xr, xi):
        return pr * xr - pi * xi, pr * xi + pi * xr

    for k0 in range(0, n_slabs, SLAB_GROUP):
        slabs = range(k0, min(k0 + SLAB_GROUP, n_slabs))
        ar = [tabs[TAB_A, k] for k in slabs]
        ai = [sgn * tabs[TAB_A + 1, k] for k in slabs]

        def first_pass(t, carry):
            r = (seg - 1 - t) if reverse else t
            out = []
            for q, k in enumerate(slabs):
                xr, xi = carry[2 * q], carry[2 * q + 1]
                pr, pi = cmul(ar[q], ai[q], xr, xi)
                nr = pr + zr[k, pl.ds(base + r, SUBLANES, stride=seg), :]
                ni = pi + zi[k, pl.ds(base + r, SUBLANES, stride=seg), :]
                zr[k, pl.ds(base + r, SUBLANES, stride=seg), :] = nr
                zi[k, pl.ds(base + r, SUBLANES, stride=seg), :] = ni
                out += [nr, ni]
            return tuple(out)

        ends = lax.fori_loop(0, seg, first_pass, tuple(jnp.zeros((SUBLANES, LANES), F32) for _ in range(2 * len(slabs))))

        incoming = []
        for q, k in enumerate(slabs):
            fr, fi = ends[2 * q], ends[2 * q + 1]
            for d, tab in ((1, TAB_AS1), (2, TAB_AS2), (4, TAB_AS4)):
                shift, keep = (SUBLANES - d, row < SUBLANES - d) if reverse else (d, row >= d)
                sr = jnp.where(keep, pltpu.roll(fr, shift, 0), 0.0)
                si = jnp.where(keep, pltpu.roll(fi, shift, 0), 0.0)
                pr, pi = cmul(tabs[tab, k], sgn * tabs[tab + 1, k], sr, si)
                fr, fi = fr + pr, fi + pi
            cr, ci = car_r[k], car_i[k]
            jtab = TAB_JB if reverse else TAB_JF
            pr, pi = cmul(tabs[jtab, k], sgn * tabs[jtab + 1, k], cr, ci)
            er, ei = fr + pr, fi + pi
            if reverse:
                inr = jnp.where(row < SUBLANES - 1, pltpu.roll(er, SUBLANES - 1, 0), cr)
                ini = jnp.where(row < SUBLANES - 1, pltpu.roll(ei, SUBLANES - 1, 0), ci)
                car_r[k] = jnp.broadcast_to(er[0:1, :], (SUBLANES, LANES))
                car_i[k] = jnp.broadcast_to(ei[0:1, :], (SUBLANES, LANES))
            else:
                inr = jnp.where(row >= 1, pltpu.roll(er, 1, 0), cr)
                ini = jnp.where(row >= 1, pltpu.roll(ei, 1, 0), ci)
                car_r[k] = jnp.broadcast_to(er[SUBLANES - 1 : SUBLANES, :], (SUBLANES, LANES))
                car_i[k] = jnp.broadcast_to(ei[SUBLANES - 1 : SUBLANES, :], (SUBLANES, LANES))
            incoming += [inr, ini]

        def second_pass(r, _):
            p = (seg - 1 - r) if reverse else r
            for q, k in enumerate(slabs):
                pr, pi = cmul(pows[0, k, p], sgn * pows[1, k, p], incoming[2 * q], incoming[2 * q + 1])
                zr[k, pl.ds(base + r, SUBLANES, stride=seg), :] = zr[k, pl.ds(base + r, SUBLANES, stride=seg), :] + pr
                zi[k, pl.ds(base + r, SUBLANES, stride=seg), :] = zi[k, pl.ds(base + r, SUBLANES, stride=seg), :] + pi
            return 0

        lax.fori_loop(0, seg, second_pass, 0)


def _slabs_to_cols(ref, k0, n):
    return jnp.concatenate([ref[k0 + q] for q in range(n)], axis=1)


def _window_sum(ext, doublings, forward):
    rows = ext.shape[0]
    s = ext
    for k in range(doublings):
        s = s + pltpu.roll(s, (1 << k) if forward else rows - (1 << k), 0)
    return s


def _mix_fwd(h1, g_pre, g_so, g_po, g_post, dskip, pscale, win, wout, bbre, bbim, ccre, ccim, wgv, wgg, pw, tabs, pows, name, exchange=None):
    T, D = h1.shape
    W = D // 2
    tm = MIX_SUBTILES * MIX_TILE
    seg = MIX_TILE // SUBLANES
    n_slabs = tabs.shape[1]
    nch, cch, sch = bbre.shape
    spc = sch // LANES
    ngc, gch, _ = wgv.shape
    pg = W // len(POOL_WINDOWS)

    def body(h_ref, gpre_ref, gso_ref, gpo_ref, gpost_ref, dskip_ref, pscale_ref, win_ref, wout_ref, bbre_ref, bbim_ref, ccre_ref, ccim_ref, wgv_ref, wgg_ref, pw_ref, tabs_ref, pows_ref,
             proj_ref, xr_ref, xi_ref, y_ref, pooled_ref, mixed_ref, h2_ref, n2_ref, cat_ref, car_r, car_i, halo):
        i = pl.program_id(0)

        @pl.when(i == 0)
        def _():
            car_r[...] = jnp.zeros_like(car_r)
            car_i[...] = jnp.zeros_like(car_i)
            halo[...] = jnp.zeros_like(halo)

        hh = h_ref[...]
        n2 = (hh * _rms_stat(hh) * gpre_ref[...]).astype(MXU_DTYPE)
        n2_ref[...] = n2.astype(n2_ref.dtype)
        proj = _mm(n2, win_ref[...])
        proj_ref[...] = proj
        us, up = proj[:, :W], proj[:, W:]

        for c in range(nch):
            uc = us[:, c * cch : (c + 1) * cch].astype(MXU_DTYPE)
            bur, bui = _mm(uc, bbre_ref[c]), _mm(uc, bbim_ref[c])
            for q in range(spc):
                xr_ref[c * spc + q] = bur[:, q * LANES : (q + 1) * LANES]
                xi_ref[c * spc + q] = bui[:, q * LANES : (q + 1) * LANES]
        for sub in range(MIX_SUBTILES):
            _scan_inplace(xr_ref, xi_ref, tabs_ref, pows_ref, car_r, car_i, seg, reverse=False, base=sub * MIX_TILE)
        ys = []
        for c in range(nch):
            ys.append(_mm(_slabs_to_cols(xr_ref, c * spc, spc), ccre_ref[c]) - _mm(_slabs_to_cols(xi_ref, c * spc, spc), ccim_ref[c]))
        y = jnp.concatenate(ys, axis=1) + dskip_ref[...] * us
        y_ref[...] = y
        ge = _gelu(y).astype(MXU_DTYPE)
        zv = jnp.concatenate([_mm(ge[:, c * gch : (c + 1) * gch], wgv_ref[c]) for c in range(ngc)], axis=1)
        zg = jnp.concatenate([_mm(ge[:, c * gch : (c + 1) * gch], wgg_ref[c]) for c in range(ngc)], axis=1)
        out = zv * _sigmoid(zg)
        cat_s = out * _rms_stat(out) * gso_ref[...]

        ext = jnp.concatenate([halo[...], up], axis=0)
        halo[...] = up[tm - POOL_HALO :, :]
        t1 = (i * tm + 1 + lax.broadcasted_iota(jnp.int32, (tm, pg), 0)).astype(F32)
        pooled, pms = [], []
        for g, w in enumerate(POOL_WINDOWS):
            col = ext[:, g * pg : (g + 1) * pg]
            win_sum = _window_sum(col, g + 1, True)[POOL_HALO:, :]
            pooled_g = win_sum / jnp.minimum(t1, float(w)) - up[:, g * pg : (g + 1) * pg]
            pooled.append(pooled_g)
            pms.append(_mm(pooled_g, pw_ref[g]))
        pooled_ref[...] = jnp.concatenate(pooled, axis=1)
        yp = jnp.concatenate(pms, axis=1) * pscale_ref[...]
        cat_p = yp * _rms_stat(yp) * gpo_ref[...]

        cat = jnp.concatenate([cat_s, cat_p], axis=1).astype(MXU_DTYPE)
        cat_ref[...] = cat.astype(cat_ref.dtype)
        mixed = _mm(cat, wout_ref[...])
        mixed_ref[...] = mixed
        h2_ref[...] = hh + mixed * _rms_stat(mixed) * gpost_ref[...]

    tok = lambda cols, dt=F32: jax.ShapeDtypeStruct((T, cols), dt)
    slab_spec = pl.BlockSpec((n_slabs, tm, LANES), lambda i: (0, i, 0))
    operands = (h1, g_pre, g_so, g_po, g_post, dskip, pscale, win, wout, bbre, bbim, ccre, ccim, wgv, wgg, pw, tabs, pows)
    return _pallas(
        body,
        name=name,
        grid=(T // tm,),
        out_shape=[tok(D), jax.ShapeDtypeStruct((n_slabs, T, LANES), F32), jax.ShapeDtypeStruct((n_slabs, T, LANES), F32), tok(W), tok(W), tok(D), tok(D), tok(D, MXU_DTYPE), tok(D, MXU_DTYPE)],
        in_specs=[_row_spec(tm, D)] + [_full_spec(o.shape, single=True) for o in operands[1:]],
        out_specs=[_row_spec(tm, D), slab_spec, slab_spec, _row_spec(tm, W), _row_spec(tm, W), _row_spec(tm, D), _row_spec(tm, D), _row_spec(tm, D), _row_spec(tm, D)],
        scratch_shapes=[pltpu.VMEM((n_slabs, SUBLANES, LANES), F32), pltpu.VMEM((n_slabs, SUBLANES, LANES), F32), pltpu.VMEM((POOL_HALO, W), F32)],
        operands=operands,
        exchange=exchange,
    )


def _mix_bwd_heads(dh2, mixed, y, pooled, proj, g_so, g_po, g_post, pscale, wout, wgv, wgg, pw, name, exchange=None):
    T, D = dh2.shape
    W = D // 2
    tm = _token_tile(T)
    nch, cch, _ = wgv.shape
    ng, pg, _ = pw.shape

    def body(dh2_ref, mixed_ref, y_ref, pooled_ref, us_ref, gso_ref, gpo_ref, gpost_ref, pscale_ref, wout_ref, wgv_ref, wgg_ref, pw_ref,
             dy_ref, dpooled_ref, dmixed_ref, dgpost_ref, dgso_ref, dgpo_ref, dd_ref, dscale_ref, dwgv_ref, dwgg_ref, dpw_ref):
        first = pl.program_id(0) == 0
        dmixed, dgpost = _rms_bwd(mixed_ref[...], gpost_ref[...], dh2_ref[...])
        _acc(dgpost_ref, dgpost, first)
        dmb = dmixed.astype(MXU_DTYPE)
        dmixed_ref[...] = dmb.astype(dmixed_ref.dtype)
        dcat = _mm_nt(dmb, wout_ref[...])
        dcs, dcp = dcat[:, :W], dcat[:, W:]

        y = y_ref[...]
        ge = _gelu(y).astype(MXU_DTYPE)
        zv = jnp.concatenate([_mm(ge[:, c * cch : (c + 1) * cch], wgv_ref[c]) for c in range(nch)], axis=1)
        zg = jnp.concatenate([_mm(ge[:, c * cch : (c + 1) * cch], wgg_ref[c]) for c in range(nch)], axis=1)
        sg = _sigmoid(zg)
        dout, dgso = _rms_bwd(zv * sg, gso_ref[...], dcs)
        _acc(dgso_ref, dgso, first)
        dzv = (dout * sg).astype(MXU_DTYPE)
        dzg = (dout * zv * sg * (1.0 - sg)).astype(MXU_DTYPE)
        dges = []
        for c in range(nch):
            cs = slice(c * cch, (c + 1) * cch)
            dges.append(_mm_nt(dzv[:, cs], wgv_ref[c]) + _mm_nt(dzg[:, cs], wgg_ref[c]))
            _acc(dwgv_ref.at[c], _mm_tn(ge[:, cs], dzv[:, cs]), first)
            _acc(dwgg_ref.at[c], _mm_tn(ge[:, cs], dzg[:, cs]), first)
        dy = jnp.concatenate(dges, axis=1) * _gelu_grad(y)
        dy_ref[...] = dy
        _acc(dd_ref, jnp.sum(dy * us_ref[...], axis=0, keepdims=True), first)

        pooled_b = pooled_ref[...].astype(MXU_DTYPE)
        pm = jnp.concatenate([_mm(pooled_b[:, g * pg : (g + 1) * pg], pw_ref[g]) for g in range(ng)], axis=1)
        dyp, dgpo = _rms_bwd(pm * pscale_ref[...], gpo_ref[...], dcp)
        _acc(dgpo_ref, dgpo, first)
        _acc(dscale_ref, jnp.sum(dyp * pm, axis=0, keepdims=True), first)
        dpm = (dyp * pscale_ref[...]).astype(MXU_DTYPE)
        dps = []
        for g in range(ng):
            gs = slice(g * pg, (g + 1) * pg)
            dps.append(_mm_nt(dpm[:, gs], pw_ref[g]))
            _acc(dpw_ref.at[g], _mm_tn(pooled_b[:, gs], dpm[:, gs]), first)
        dpooled_ref[...] = jnp.concatenate(dps, axis=1)

    vec = lambda n: jax.ShapeDtypeStruct((1, n), F32)
    operands = (dh2, mixed, y, pooled, proj, g_so, g_po, g_post, pscale, wout, wgv, wgg, pw)
    return _pallas(
        body,
        name=name,
        grid=(T // tm,),
        out_shape=[jax.ShapeDtypeStruct((T, W), F32), jax.ShapeDtypeStruct((T, W), F32), jax.ShapeDtypeStruct((T, D), MXU_DTYPE), vec(D), vec(W), vec(W), vec(W), vec(W),
                   jax.ShapeDtypeStruct(wgv.shape, F32), jax.ShapeDtypeStruct(wgg.shape, F32), jax.ShapeDtypeStruct(pw.shape, F32)],
        in_specs=[_row_spec(tm, D), _row_spec(tm, D), _row_spec(tm, W), _row_spec(tm, W), _row_spec(tm, W)] + [_full_spec(o.shape) for o in operands[5:]],
        out_specs=[_row_spec(tm, W), _row_spec(tm, W), _row_spec(tm, D), _full_spec((1, D)), _full_spec((1, W)), _full_spec((1, W)), _full_spec((1, W)), _full_spec((1, W)),
                   _full_spec(wgv.shape), _full_spec(wgg.shape), _full_spec(pw.shape)],
        operands=operands,
        exchange=exchange,
    )


def _mix_bwd_scan(dy, dpooled, xr, xi, proj, dskip, bbre, bbim, ccre, ccim, tabs, pows, name, exchange=None):
    T, W = dy.shape
    D = 2 * W
    tm = MIX_SUBTILES * MIX_TILE
    seg = MIX_TILE // SUBLANES
    nt = T // tm
    n_slabs = tabs.shape[1]
    nch, cch, sch = bbre.shape
    spc = sch // LANES
    pg = W // len(POOL_WINDOWS)
    blocks_per_tile = tm // SUBLANES

    def body(dy_ref, dp_ref, xr_ref, xi_ref, xpr_ref, xpi_ref, us_ref, dskip_ref, bbre_ref, bbim_ref, ccre_ref, ccim_ref, tabs_ref, pows_ref,
             dproj_ref, dccre_ref, dccim_ref, dbbre_ref, dbbim_ref, dar_ref, dai_ref, lr, li, car_r, car_i, halo):
        i = pl.program_id(0)
        first = i == 0
        tile = nt - 1 - i
        row = lax.broadcasted_iota(jnp.int32, (SUBLANES, LANES), 0)

        @pl.when(first)
        def _():
            car_r[...] = jnp.zeros_like(car_r)
            car_i[...] = jnp.zeros_like(car_i)
            halo[...] = jnp.zeros_like(halo)
            dar_ref[...] = jnp.zeros_like(dar_ref)
            dai_ref[...] = jnp.zeros_like(dai_ref)

        dy = dy_ref[...]
        for c in range(nch):
            dyc = dy[:, c * cch : (c + 1) * cch]
            gr, gi = _mm_nt(dyc, ccre_ref[c]), _mm_nt(dyc, ccim_ref[c])
            for q in range(spc):
                lr[c * spc + q] = gr[:, q * LANES : (q + 1) * LANES]
                li[c * spc + q] = -gi[:, q * LANES : (q + 1) * LANES]
            _acc(dccre_ref.at[c], _mm_tn(_slabs_to_cols(xr_ref, c * spc, spc), dyc), first)
            _acc(dccim_ref.at[c], -_mm_tn(_slabs_to_cols(xi_ref, c * spc, spc), dyc), first)
        for sub in reversed(range(MIX_SUBTILES)):
            _scan_inplace(lr, li, tabs_ref, pows_ref, car_r, car_i, seg, reverse=True, base=sub * MIX_TILE)

        for sub in range(MIX_SUBTILES):
            base = sub * MIX_TILE
            for k0 in range(0, n_slabs, SLAB_GROUP):
                slabs = range(k0, min(k0 + SLAB_GROUP, n_slabs))
                init = []
                for k in slabs:
                    if sub == 0:
                        prev_r = jnp.where(tile > 0, jnp.broadcast_to(xpr_ref[k, SUBLANES - 1 : SUBLANES, :], (SUBLANES, LANES)), 0.0)
                        prev_i = jnp.where(tile > 0, jnp.broadcast_to(xpi_ref[k, SUBLANES - 1 : SUBLANES, :], (SUBLANES, LANES)), 0.0)
                    else:
                        prev_r = jnp.broadcast_to(xr_ref[k, base - 1 : base, :], (SUBLANES, LANES))
                        prev_i = jnp.broadcast_to(xi_ref[k, base - 1 : base, :], (SUBLANES, LANES))
                    x0r = jnp.where(row >= 1, pltpu.roll(xr_ref[k, pl.ds(base + seg - 1, SUBLANES, stride=seg), :], 1, 0), prev_r)
                    x0i = jnp.where(row >= 1, pltpu.roll(xi_ref[k, pl.ds(base + seg - 1, SUBLANES, stride=seg), :], 1, 0), prev_i)
                    l0r, l0i = lr[k, pl.ds(base, SUBLANES, stride=seg), :], li[k, pl.ds(base, SUBLANES, stride=seg), :]
                    init += [l0r * x0r + l0i * x0i, l0i * x0r - l0r * x0i]

                def step(r, acc, slabs=slabs, base=base):
                    out = []
                    for q, k in enumerate(slabs):
                        pr_, pi_ = xr_ref[k, pl.ds(base + r - 1, SUBLANES, stride=seg), :], xi_ref[k, pl.ds(base + r - 1, SUBLANES, stride=seg), :]
                        lr_, li_ = lr[k, pl.ds(base + r, SUBLANES, stride=seg), :], li[k, pl.ds(base + r, SUBLANES, stride=seg), :]
                        out += [acc[2 * q] + lr_ * pr_ + li_ * pi_, acc[2 * q + 1] + li_ * pr_ - lr_ * pi_]
                    return tuple(out)

                sums = lax.fori_loop(1, seg, step, tuple(init))
                for q, k in enumerate(slabs):
                    dar_ref[k] += sums[2 * q]
                    dai_ref[k] += sums[2 * q + 1]

        us = us_ref[...]
        dus = []
        for c in range(nch):
            lrc, lic = _slabs_to_cols(lr, c * spc, spc).astype(MXU_DTYPE), _slabs_to_cols(li, c * spc, spc).astype(MXU_DTYPE)
            uc = us[:, c * cch : (c + 1) * cch]
            _acc(dbbre_ref.at[c], _mm_tn(uc, lrc), first)
            _acc(dbbim_ref.at[c], _mm_tn(uc, lic), first)
            dus.append(_mm_nt(lrc, bbre_ref[c]) + _mm_nt(lic, bbim_ref[c]))
        du_s = jnp.concatenate(dus, axis=1) + dskip_ref[...] * dy

        dp = dp_ref[...]
        t1 = (tile * tm + 1 + lax.broadcasted_iota(jnp.int32, (tm, pg), 0)).astype(F32)
        dups, heads = [], []
        for g, w in enumerate(POOL_WINDOWS):
            dpg = dp[:, g * pg : (g + 1) * pg]
            qg = dpg / jnp.minimum(t1, float(w))
            ext = jnp.concatenate([qg, halo[:, g * pg : (g + 1) * pg]], axis=0)
            dups.append(_window_sum(ext, g + 1, False)[:tm, :] - dpg)
            heads.append(qg[:POOL_HALO, :])
        halo[...] = jnp.concatenate(heads, axis=1)
        dproj_ref[...] = jnp.concatenate([du_s] + dups, axis=1).astype(dproj_ref.dtype)

    rev = lambda cols: _row_spec(tm, cols, rev_n=nt)
    slab_spec = pl.BlockSpec((n_slabs, tm, LANES), lambda i: (0, nt - 1 - i, 0))
    prev_spec = pl.BlockSpec((n_slabs, SUBLANES, LANES), lambda i: (0, jnp.maximum((nt - 1 - i) * blocks_per_tile - 1, 0), 0))
    consts = (dskip, bbre, bbim, ccre, ccim, tabs, pows)
    return _pallas(
        body,
        name=name,
        grid=(nt,),
        out_shape=[jax.ShapeDtypeStruct((T, D), MXU_DTYPE), jax.ShapeDtypeStruct(ccre.shape, F32), jax.ShapeDtypeStruct(ccim.shape, F32), jax.ShapeDtypeStruct(bbre.shape, F32),
                   jax.ShapeDtypeStruct(bbim.shape, F32), jax.ShapeDtypeStruct((n_slabs, SUBLANES, LANES), F32), jax.ShapeDtypeStruct((n_slabs, SUBLANES, LANES), F32)],
        in_specs=[rev(W), rev(W), slab_spec, slab_spec, prev_spec, prev_spec, rev(W)] + [_full_spec(o.shape, single=True) for o in consts],
        out_specs=[rev(D), _full_spec(ccre.shape), _full_spec(ccim.shape), _full_spec(bbre.shape), _full_spec(bbim.shape),
                   _full_spec((n_slabs, SUBLANES, LANES)), _full_spec((n_slabs, SUBLANES, LANES))],
        scratch_shapes=[pltpu.VMEM((n_slabs, tm, LANES), F32), pltpu.VMEM((n_slabs, tm, LANES), F32), pltpu.VMEM((n_slabs, SUBLANES, LANES), F32), pltpu.VMEM((n_slabs, SUBLANES, LANES), F32),
                        pltpu.VMEM((POOL_HALO, W), F32)],
        operands=(dy, dpooled, xr, xi, xr, xi, proj, *consts),
        exchange=exchange,
    )


def _mix_bwd_in(dproj, h1, dh2, g_pre, win, name, exchange=None):
    T, D = h1.shape
    tm = _token_tile(T)

    def body(dproj_ref, h_ref, dh2_ref, gpre_ref, win_ref, dh1_ref, dg_ref):
        dx, dg = _rms_bwd(h_ref[...], gpre_ref[...], _mm_nt(dproj_ref[...], win_ref[...]))
        _acc(dg_ref, dg, pl.program_id(0) == 0)
        dh1_ref[...] = dh2_ref[...] + dx

    return _pallas(
        body,
        name=name,
        grid=(T // tm,),
        out_shape=[jax.ShapeDtypeStruct((T, D), F32), jax.ShapeDtypeStruct((1, D), F32)],
        in_specs=[_row_spec(tm, D), _row_spec(tm, D), _row_spec(tm, D), _full_spec((1, D)), _full_spec(win.shape)],
        out_specs=[_row_spec(tm, D), _full_spec((1, D))],
        operands=(dproj, h1, dh2, g_pre, win),
        exchange=exchange,
    )


def _discretize(lam_re, lam_im, log_dt, b_re, b_im):
    dt = jnp.exp(log_dt)[:, None]
    decay = jnp.exp(lam_re * dt)
    ang = lam_im * dt
    a_re, a_im = decay * jnp.cos(ang), decay * jnp.sin(ang)
    nr = a_re - 1.0
    den = lam_re * lam_re + lam_im * lam_im
    q_re = (nr * lam_re + a_im * lam_im) / den
    q_im = (a_im * lam_re - nr * lam_im) / den
    bb_re = q_re[..., None] * b_re - q_im[..., None] * b_im
    bb_im = q_re[..., None] * b_im + q_im[..., None] * b_re
    return a_re, a_im, bb_re, bb_im


STATE_CHUNK_GROUPS = 8
GLU_CHUNK_GROUPS = 16


def _block_diag(w, rows_first, n):
    G = w.shape[0]
    nch = G // n
    if not rows_first:
        w = jnp.swapaxes(w, 1, 2)
    p, q = w.shape[1], w.shape[2]
    out = jnp.einsum("cgpq,gk->cgpkq", w.reshape(nch, n, p, q), jnp.eye(n, dtype=w.dtype))
    return out.reshape(nch, n * p, n * q)


def _block_diag_extract(m, p, q, rows_first):
    nch, n = m.shape[0], m.shape[1] // p
    out = jnp.einsum("cgpkq,gk->cgpq", m.reshape(nch, n, p, n, q), jnp.eye(n, dtype=m.dtype)).reshape(nch * n, p, q)
    return out if rows_first else jnp.swapaxes(out, 1, 2)


def _cmul(ar, ai, br, bi):
    return ar * br - ai * bi, ar * bi + ai * br


def _powers(ar, ai, count):
    pr, pi = ar[None], ai[None]
    while pr.shape[0] < count:
        nr, ni = _cmul(pr, pi, pr[-1][None], pi[-1][None])
        pr, pi = jnp.concatenate([pr, nr]), jnp.concatenate([pi, ni])
    return pr[:count], pi[:count]


def _scan_tables(a_re, a_im, seg):
    n = a_re.size
    ns = n // LANES
    ar, ai = a_re.reshape(n), a_im.reshape(n)
    pr, pi = _powers(ar, ai, seg)
    jr, ji = _powers(pr[-1], pi[-1], SUBLANES)

    def bcast(v):
        return jnp.broadcast_to(v.reshape(ns, 1, LANES), (ns, SUBLANES, LANES))

    def per_sublane(vs):
        return jnp.transpose(vs.reshape(SUBLANES, ns, LANES), (1, 0, 2))

    tabs = jnp.stack([bcast(ar), bcast(ai), bcast(jr[0]), bcast(ji[0]), bcast(jr[1]), bcast(ji[1]), bcast(jr[3]), bcast(ji[3]),
                      per_sublane(jr), per_sublane(ji), per_sublane(jr[::-1]), per_sublane(ji[::-1])])

    def rows(vs):
        return jnp.broadcast_to(jnp.transpose(vs.reshape(seg, ns, 1, LANES), (1, 0, 2, 3)), (ns, seg, SUBLANES, LANES))

    return tabs, jnp.stack([rows(pr), rows(pi)])


SMALL = ("ffn1_pre_norm", "ffn1_post_norm", "mix_pre_norm", "mix_post_norm", "ssm_lambda_re", "ssm_lambda_im", "ssm_log_dt", "ssm_b_re", "ssm_b_im", "ssm_c_re", "ssm_c_im",
         "ssm_d", "ssm_w_glu", "pool_w", "pool_scale", "ssm_out_norm", "pool_out_norm", "ffn2_pre_norm", "ffn2_post_norm")
BIG = ("ffn1_w_gate", "ffn1_w_up", "ffn1_w_down", "w_in", "w_out", "ffn2_w_gate", "ffn2_w_up", "ffn2_w_down")
ORDER = ("meta_tokens", "ffn1_pre_norm", "ffn1_post_norm", "ffn1_w_gate", "ffn1_w_up", "ffn1_w_down", "mix_pre_norm", "mix_post_norm", "w_in", "ssm_lambda_re", "ssm_lambda_im",
         "ssm_log_dt", "ssm_b_re", "ssm_b_im", "ssm_c_re", "ssm_c_im", "ssm_d", "ssm_w_glu", "pool_w", "pool_scale", "ssm_out_norm", "pool_out_norm", "w_out", "ffn2_pre_norm",
         "ffn2_post_norm", "ffn2_w_gate", "ffn2_w_up", "ffn2_w_down")
PACK_ROWS = SUBLANES * 8
def _pack(arrays, rows):
    flat = jnp.concatenate([a.reshape(-1) for a in arrays])
    return jnp.pad(flat, (0, rows * LANES - flat.size)).reshape(rows, LANES)


def _unpack(packed, shapes):
    flat = packed.reshape(-1)
    out, off = [], 0
    for s in shapes:
        n = math.prod(s)
        out.append(flat[off : off + n].reshape(s))
        off += n
    return out


def _step(p, x, loss_target, m, v):
    D = x.shape[-1]
    chip = (2 * lax.axis_index("x") + lax.axis_index("y")).astype(jnp.int32)
    place = jnp.stack([chip, lax.axis_index("c").astype(jnp.int32)])

    def gather_buffer(w):
        own = w.reshape(1, 2, w.shape[0] // 2, w.shape[1])
        return lax.dynamic_update_slice(lax.empty((4,) + own.shape[1:], own.dtype), own, (chip, 0, 0, 0))

    def rows_of(n, a):
        return jnp.swapaxes(a[0], 0, 1) if n.endswith(("w_gate", "w_up")) else a[0]

    def rows_back(n, a):
        return (jnp.swapaxes(a, 0, 1) if n.endswith(("w_gate", "w_up")) else a)[None]

    def grad_view(g):
        return g.reshape(4, 2, g.shape[0] // 8, g.shape[1])

    def reduce_sum(got_sibling, views, tag, wires=None):
        wires = wires or [WIRE_DTYPE] * len(views)
        sums = [_add_own_half(v_, g_, place, f"{tag}_add_sibling_{k}", w_) for k, (v_, g_, w_) in enumerate(zip(views, got_sibling, wires))]
        return [s[0] for s in sums], [s[1] for s in sums]

    def reduce_halves(parts, got_chips, tag):
        return [_add_chips(p_, g_, place, f"{tag}_add_chips_{k}") for k, (p_, g_) in enumerate(zip(parts, got_chips))]

    bufs = {n: gather_buffer(rows_of(n, p[n]).astype(MXU_DTYPE)) for n in BIG}
    full = {}

    def gathered(names, got):
        full.update({n: g_.reshape(-1, g_.shape[-1]) for n, g_ in zip(names, got)})

    def gather_of(names, n_steps, tenths):
        sizes = [bufs[n].size * bufs[n].dtype.itemsize for n in names]
        steps = [min(n_steps - 1, max(1, -(-tenths * n_steps * sum(sizes[: a + 1]) // (10 * sum(sizes))))) for a in range(len(names))]
        return _Gather([bufs[n] for n in names], steps=steps)

    first_names = ("ffn1_w_gate", "ffn1_w_up")
    got = _exchange_call(_Gather([bufs[n] for n in first_names] + [gather_buffer(p["meta_tokens"])]), "gather_first")
    gathered(first_names, got)
    meta = jnp.transpose(got[-1].reshape(4, N_META, -1), (1, 0, 2)).reshape(N_META, D)

    vec = lambda n: p[n].reshape(1, -1)
    G, N, H = p["ssm_b_re"].shape[1:]
    a_re, a_im, bb_re, bb_im = _discretize(p["ssm_lambda_re"][0], p["ssm_lambda_im"][0], p["ssm_log_dt"][0], p["ssm_b_re"][0], p["ssm_b_im"][0])
    tabs, pows = _scan_tables(a_re, a_im, MIX_TILE // SUBLANES)
    bf = lambda a: a.astype(MXU_DTYPE)
    sg_, gg_ = STATE_CHUNK_GROUPS, GLU_CHUNK_GROUPS
    bbre, bbim = bf(_block_diag(bb_re, False, sg_)), bf(_block_diag(bb_im, False, sg_))
    ccre, ccim = bf(_block_diag(p["ssm_c_re"][0], False, sg_)), bf(_block_diag(p["ssm_c_im"][0], False, sg_))
    wgv, wgg = bf(_block_diag(p["ssm_w_glu"][0][:, :, :H], True, gg_)), bf(_block_diag(p["ssm_w_glu"][0][:, :, H:], True, gg_))
    pw = bf(p["pool_w"][0])

    T = x.shape[1] + N_META
    names = ("ffn1_w_down", "w_in", "w_out", "ffn2_w_gate", "ffn2_w_down")
    (ga1, si1, s1, n1), got = _ffn_gate_up(
        x[0], meta, vec("ffn1_pre_norm"), full["ffn1_w_gate"], full["ffn1_w_up"], "ffn1_gate_up", exchange=gather_of(names, T // FFN_TILE, 9)
    )
    gathered(names, got)
    (h1, f1), got = _ffn_down(x[0], meta, s1, vec("ffn1_post_norm"), full["ffn1_w_down"], "ffn1_down", exchange=gather_of(("ffn2_w_up",), T // _token_tile(T), 6))
    gathered(("ffn2_w_up",), got)
    (proj, xr, xim, y, pooled, mixed, h2, n2, cat), _ = _mix_fwd(
        h1, vec("mix_pre_norm"), vec("ssm_out_norm"), vec("pool_out_norm"), vec("mix_post_norm"), vec("ssm_d"), vec("pool_scale"), full["w_in"], full["w_out"],
        bbre, bbim, ccre, ccim, wgv, wgg, pw, tabs, pows, "mix_fwd",
    )
    dh3, sq, f2, ga2, si2, s2, n3 = _ffn_fwd_loss(
        h2, loss_target[0], vec("ffn2_pre_norm"), vec("ffn2_post_norm"), full["ffn2_w_gate"], full["ffn2_w_up"], full["ffn2_w_down"], "ffn2_fwd"
    )

    g, shared = {}, {}
    ffn_names = lambda tag: (tag + "_w_gate", tag + "_w_up", tag + "_w_down")

    da, db, df, dh2, g["ffn2_post_norm"], g["ffn2_pre_norm"] = _ffn_bwd(
        dh3, f2, ga2, si2, h2, vec("ffn2_post_norm"), vec("ffn2_pre_norm"), full["ffn2_w_gate"], full["ffn2_w_up"], full["ffn2_w_down"], "ffn2_bwd"
    )
    views2 = [
        grad_view(_tn_matmul(da, n3, "ffn2_dw_gate")[0]),
        grad_view(_tn_matmul(db, n3, "ffn2_dw_up")[0]),
        grad_view(_tn_matmul(s2, df, "ffn2_dw_down")[0]),
    ]
    (dy, dpooled, dmixed, g["mix_post_norm"], g["ssm_out_norm"], g["pool_out_norm"], g["ssm_d"], g["pool_scale"], dwgv, dwgg, g["pool_w"]), got = _mix_bwd_heads(
        dh2, mixed, y, pooled, proj, vec("ssm_out_norm"), vec("pool_out_norm"), vec("mix_post_norm"), vec("pool_scale"), full["w_out"], wgv, wgg, pw, "mix_bwd_heads",
        exchange=_SiblingScatter(views2),
    )
    parts2, wire2 = reduce_sum(got, views2, "ffn2")
    (dproj, dccre, dccim, dbbre, dbbim, dar, dai), got = _mix_bwd_scan(
        dy, dpooled, xr, xim, proj, vec("ssm_d"), bbre, bbim, ccre, ccim, tabs, pows, "mix_bwd_scan", exchange=_ChipScatter(wire2)
    )
    halves2 = reduce_halves(parts2, got, "ffn2")
    (dh1, g["mix_pre_norm"]), got = _mix_bwd_in(dproj, h1, dh2, vec("mix_pre_norm"), full["w_in"], "mix_bwd_in", exchange=_SiblingShare(halves2))
    shared.update(zip(ffn_names("ffn2"), got))

    da, db, df, grad_x, d_meta, g["ffn1_post_norm"], g["ffn1_pre_norm"] = _ffn_bwd_first(
        dh1, f1, ga1, si1, x[0], meta, vec("ffn1_post_norm"), vec("ffn1_pre_norm"), full["ffn1_w_gate"], full["ffn1_w_up"], full["ffn1_w_down"], "ffn1_bwd"
    )
    grad_x = grad_x[None]

    g["ssm_c_re"] = _block_diag_extract(dccre, N, H, False)
    g["ssm_c_im"] = _block_diag_extract(dccim, N, H, False)
    g["ssm_w_glu"] = jnp.concatenate([_block_diag_extract(dwgv, H, H, True), _block_diag_extract(dwgg, H, H, True)], axis=-1)
    d_a_re, d_a_im = jnp.sum(dar, axis=1).reshape(G, N), jnp.sum(dai, axis=1).reshape(G, N)
    _, pull = jax.vjp(_discretize, p["ssm_lambda_re"][0], p["ssm_lambda_im"][0], p["ssm_log_dt"][0], p["ssm_b_re"][0], p["ssm_b_im"][0])
    g["ssm_lambda_re"], g["ssm_lambda_im"], g["ssm_log_dt"], g["ssm_b_re"], g["ssm_b_im"] = pull(
        (d_a_re, d_a_im, _block_diag_extract(dbbre, H, N, False), _block_diag_extract(dbbim, H, N, False))
    )
    small_shapes = [p[n].shape for n in SMALL] + [(N_META, D), (1,)]
    small_size = sum(math.prod(s) for s in small_shapes)
    rows = -(-small_size // (LANES * PACK_ROWS)) * PACK_ROWS
    views_s = [_pack([g[n] for n in SMALL] + [d_meta, jnp.sum(sq).reshape(1)], rows).reshape(4, 2, rows // 8, LANES)]

    dw_down, got = _tn_matmul(s1, df, "ffn1_dw_down", exchange=_SiblingScatter(views_s))
    parts_s, wire_s = reduce_sum(got, views_s, "small", wires=[F32])
    views_d = [grad_view(dw_down)]
    ex = _Group([_SiblingScatter(views_d), _ChipScatter(wire_s)])
    dw_gate, got = _tn_matmul(da, n1, "ffn1_dw_gate", exchange=ex)
    got_d, got_s = ex.split(got)
    parts_d, wire_d = reduce_sum(got_d, views_d, "ffn1_down")
    halves_s = reduce_halves(parts_s, got_s, "small")
    views_g = [grad_view(dw_gate)]
    ex = _Group([_ChipScatter(wire_d), _SiblingScatter(views_g), _SiblingShare(halves_s)])
    dw_up, got = _tn_matmul(db, n1, "ffn1_dw_up", exchange=ex)
    got_d, got_g, got_s = ex.split(got)
    small_buf = lax.dynamic_update_slice(lax.empty((4,) + got_s[0].shape, F32), got_s[0][None], (chip, 0, 0, 0))
    halves_d = reduce_halves(parts_d, got_d, "ffn1_down")
    parts_g, wire_g = reduce_sum(got_g, views_g, "ffn1_gate")
    views_u = [grad_view(dw_up)]
    tn_steps = (T - N_META) // TN_TILE if (T - N_META) % TN_TILE == 0 else T // _token_tile(T)
    ex = _Group([_SiblingShare(halves_d), _ChipScatter(wire_g), _SiblingScatter(views_u), _Gather([small_buf], steps=[(3 * tn_steps) // 4])])
    dw_in, got = _tn_matmul(n2, dproj, "dw_in", exchange=ex)
    got_d, got_g, got_u, got_s = ex.split(got)
    grads = dict(zip(SMALL + ("meta_full", "sq_sum"), _unpack(got_s[0].reshape(rows, LANES), small_shapes)))
    grads["meta_tokens"] = lax.dynamic_slice_in_dim(grads.pop("meta_full"), chip * (D // 4), D // 4, axis=1)
    loss = (0.5 / D) * grads.pop("sq_sum")[0]
    shared["ffn1_w_down"] = got_d[0]
    halves_g = reduce_halves(parts_g, got_g, "ffn1_gate")
    parts_u, wire_u = reduce_sum(got_u, views_u, "ffn1_up")
    views_i = [grad_view(dw_in)]
    ex = _Group([_SiblingShare(halves_g), _ChipScatter(wire_u), _SiblingScatter(views_i)])
    dw_out, got = _tn_matmul(cat, dmixed, "dw_out", exchange=ex)
    got_g, got_u, got_i = ex.split(got)
    shared["ffn1_w_gate"] = got_g[0]
    halves_u = reduce_halves(parts_u, got_u, "ffn1_up")
    parts_i, wire_i = reduce_sum(got_i, views_i, "w_in")

    views_t = [grad_view(dw_out)]
    ex = _Group([_SiblingScatter(views_t), _SiblingShare(halves_u)])
    got_t, got_u = ex.split(_exchange_call(ex, "tail_reduce_sibling"))
    shared["ffn1_w_up"] = got_u[0]
    parts_t, wire_t = reduce_sum(got_t, views_t, "tail")
    got = _exchange_call(_ChipScatter(wire_i + wire_t), "tail_reduce_chips")
    got = _exchange_call(_SiblingShare(reduce_halves(parts_i + parts_t, got, "tail")), "tail_reduce_share")
    shared["w_in"], shared["w_out"] = got[0], got[1]
    delta, new_m, new_v = {}, {}, {}
    for n in BIG:
        g_rows = shared[n].reshape(-1, shared[n].shape[-1])
        outs = _adamw(rows_of(n, p[n]), g_rows, rows_of(n, m[n]), rows_of(n, v[n]), "adamw_" + n)
        grads[n], delta[n], new_m[n], new_v[n] = (rows_back(n, a) for a in (g_rows, *outs))
    delta["meta_tokens"], new_m["meta_tokens"], new_v["meta_tokens"] = _adamw(p["meta_tokens"], grads["meta_tokens"], m["meta_tokens"], v["meta_tokens"], "adamw_meta_tokens")

    def as_2d(n, a):
        a = a.reshape(p[n].shape)[0]
        if n in ("ssm_b_re", "ssm_b_im"):
            a = jnp.swapaxes(a, 1, 2)
        return a.reshape(-1, a.shape[-1])

    def from_2d(n, a):
        if n in ("ssm_b_re", "ssm_b_im"):
            g_, n_, h_ = p[n].shape[1:]
            return jnp.swapaxes(a.reshape(g_, h_, n_), 1, 2)[None]
        return a.reshape(p[n].shape)

    outs = _adamw_many(*[[as_2d(n, t[n]) for n in SMALL] for t in (p, grads, m, v)], "adamw_small")
    for out, arrays in zip((delta, new_m, new_v), outs):
        out.update({n: from_2d(n, a) for n, a in zip(SMALL, arrays)})

    return (loss, grad_x, *[grads[n] for n in ORDER], *[delta[n] for n in ORDER], *[new_m[n] for n in ORDER], *[new_v[n] for n in ORDER])


def kernel(x, meta_tokens, ffn1_pre_norm, ffn1_post_norm, ffn1_w_gate, ffn1_w_up, ffn1_w_down, mix_pre_norm, mix_post_norm, w_in, ssm_lambda_re, ssm_lambda_im, ssm_log_dt, ssm_b_re, ssm_b_im, ssm_c_re, ssm_c_im, ssm_d, ssm_w_glu, pool_w, pool_scale, ssm_out_norm, pool_out_norm, w_out, ffn2_pre_norm, ffn2_post_norm, ffn2_w_gate, ffn2_w_up, ffn2_w_down, loss_target, m_meta_tokens, m_ffn1_pre_norm, m_ffn1_post_norm, m_ffn1_w_gate, m_ffn1_w_up, m_ffn1_w_down, m_mix_pre_norm, m_mix_post_norm, m_w_in, m_ssm_lambda_re, m_ssm_lambda_im, m_ssm_log_dt, m_ssm_b_re, m_ssm_b_im, m_ssm_c_re, m_ssm_c_im, m_ssm_d, m_ssm_w_glu, m_pool_w, m_pool_scale, m_ssm_out_norm, m_pool_out_norm, m_w_out, m_ffn2_pre_norm, m_ffn2_post_norm, m_ffn2_w_gate, m_ffn2_w_up, m_ffn2_w_down, v_meta_tokens, v_ffn1_pre_norm, v_ffn1_post_norm, v_ffn1_w_gate, v_ffn1_w_up, v_ffn1_w_down, v_mix_pre_norm, v_mix_post_norm, v_w_in, v_ssm_lambda_re, v_ssm_lambda_im, v_ssm_log_dt, v_ssm_b_re, v_ssm_b_im, v_ssm_c_re, v_ssm_c_im, v_ssm_d, v_ssm_w_glu, v_pool_w, v_pool_scale, v_ssm_out_norm, v_pool_out_norm, v_w_out, v_ffn2_pre_norm, v_ffn2_post_norm, v_ffn2_w_gate, v_ffn2_w_up, v_ffn2_w_down):
    args = locals()
    p = {n: args[n] for n in ORDER}
    m = {n: args["m_" + n] for n in ORDER}
    v = {n: args["v_" + n] for n in ORDER}
    return _step(p, x, loss_target, m, v)
```

```python
import math

import jax
import jax.numpy as jnp
from jax import lax
from jax.experimental import pallas as pl
from jax.experimental.pallas import tpu as pltpu

F32 = jnp.float32
MXU_DTYPE = jnp.bfloat16
WIRE_DTYPE = jnp.bfloat16

RMS_EPS = 1e-6
N_META = 16
POOL_WINDOWS = (2, 4, 8, 16)
POOL_HALO = 16
ADAM_LR, ADAM_B1, ADAM_B2, ADAM_EPS, ADAM_WD, ADAM_STEP = 0.001, 0.9, 0.999, 1e-08, 0.01, 10

LANES = 128
SUBLANES = 8
VMEM_LIMIT = 60 * 1024 * 1024
FFN_TILE = 432
FFN_CHUNK = 1024
TN_TILE = 1024
MIX_TILE = 216
MIX_SUBTILES = 2
SLAB_GROUP = 8
MESH = pl.DeviceIdType.MESH
ANY = pl.BlockSpec(memory_space=pl.ANY)


def _mm(a, b):
    return jnp.dot(a.astype(MXU_DTYPE), b.astype(MXU_DTYPE), preferred_element_type=F32)


def _mm_nt(a, b):
    return lax.dot_general(a.astype(MXU_DTYPE), b.astype(MXU_DTYPE), (((1,), (1,)), ((), ())), preferred_element_type=F32)


def _mm_tn(a, b):
    return lax.dot_general(a.astype(MXU_DTYPE), b.astype(MXU_DTYPE), (((0,), (0,)), ((), ())), preferred_element_type=F32)


def _rms_stat(x):
    return lax.rsqrt(jnp.mean(x * x, axis=-1, keepdims=True) + RMS_EPS)


def _rms_bwd(x, g, dy):
    r = _rms_stat(x)
    xh = x * r
    dg = jnp.sum(dy * xh, axis=0, keepdims=True)
    dxh = dy * g
    dx = r * (dxh - xh * jnp.mean(dxh * xh, axis=-1, keepdims=True))
    return dx, dg


def _sigmoid(x):
    return 1.0 / (1.0 + jnp.exp(-x))


GELU_C = math.sqrt(2.0 / math.pi)
GELU_K = 0.044715


def _gelu(y):
    return 0.5 * y * (1.0 + jnp.tanh(GELU_C * (y + GELU_K * y * y * y)))


def _gelu_grad(y):
    th = jnp.tanh(GELU_C * (y + GELU_K * y * y * y))
    return 0.5 * (1.0 + th) + 0.5 * y * (1.0 - th * th) * GELU_C * (1.0 + 3.0 * GELU_K * y * y)


def _row_spec(tile, cols, rev_n=None):
    if rev_n is None:
        return pl.BlockSpec((tile, cols), lambda i: (i, 0))
    return pl.BlockSpec((tile, cols), lambda i: (rev_n - 1 - i, 0))


def _full_spec(shape, single=False):
    zeros = (0,) * len(shape)
    if single:
        return pl.BlockSpec(shape, lambda *_: zeros, pipeline_mode=pl.Buffered(1))
    return pl.BlockSpec(shape, lambda *_: zeros)


def _acc(ref, val, first):
    @pl.when(first)
    def _():
        ref[...] = val

    @pl.when(jnp.logical_not(first))
    def _():
        ref[...] += val


def _place():
    x, y, c = lax.axis_index("x"), lax.axis_index("y"), lax.axis_index("c")
    others = [(1 - x, y), (x, 1 - y), (1 - x, 1 - y)]
    return x, y, c, others


class _Exchange:
    def __init__(self, ins, out_shapes, aliases, n_sems):
        self.ins, self.out_shapes, self.aliases, self.n_sems = list(ins), list(out_shapes), dict(aliases), n_sems

    def mid_steps(self):
        return []

    def mid(self, ins, outs, send_sems, recv_sems, step=None):
        pass


class _SiblingScatter(_Exchange):
    def __init__(self, views):
        super().__init__(views, [jax.ShapeDtypeStruct((4,) + v.shape[2:], v.dtype) for v in views], {}, 4 * len(views))

    def _copies(self, ins, outs, send_sems, recv_sems):
        x, y, c, _ = _place()
        return [
            pltpu.make_async_remote_copy(src_ref=ins[a].at[k, 1 - c], dst_ref=outs[a].at[k], send_sem=send_sems.at[4 * a + k], recv_sem=recv_sems.at[4 * a + k], device_id=(x, y, 1 - c), device_id_type=MESH)
            for a in range(len(ins))
            for k in range(4)
        ]

    def start(self, *refs):
        for cp in self._copies(*refs):
            cp.start()

    def finish(self, *refs):
        cps = self._copies(*refs)
        for cp in cps:
            cp.wait_recv()
        for cp in cps:
            cp.wait_send()


class _ChipScatter(_Exchange):
    def __init__(self, parts):
        super().__init__(parts, [jax.ShapeDtypeStruct((3,) + p.shape[1:], p.dtype) for p in parts], {}, 3 * len(parts))

    def _copies(self, ins, outs, send_sems, recv_sems):
        x, y, c, others = _place()
        return [
            pltpu.make_async_remote_copy(src_ref=ins[a].at[2 * chip[0] + chip[1]], dst_ref=outs[a].at[j], send_sem=send_sems.at[3 * a + j], recv_sem=recv_sems.at[3 * a + j], device_id=(*chip, c), device_id_type=MESH)
            for a in range(len(ins))
            for j, chip in enumerate(others)
        ]

    start = _SiblingScatter.start
    finish = _SiblingScatter.finish


class _SiblingShare(_Exchange):
    def __init__(self, bufs):
        super().__init__(bufs, [jax.ShapeDtypeStruct(b.shape, b.dtype) for b in bufs], {a: a for a in range(len(bufs))}, len(bufs))

    def _copy(self, outs, send_sems, recv_sems, a, half):
        x, y, c, _ = _place()
        mine = outs[a].at[c if half == "mine" else 1 - c]
        return pltpu.make_async_remote_copy(src_ref=mine, dst_ref=mine, send_sem=send_sems.at[a], recv_sem=recv_sems.at[a], device_id=(x, y, 1 - c), device_id_type=MESH)

    def start(self, ins, outs, send_sems, recv_sems):
        for a in range(len(outs)):
            self._copy(outs, send_sems, recv_sems, a, "mine").start()

    def finish(self, ins, outs, send_sems, recv_sems):
        for a in range(len(outs)):
            self._copy(outs, send_sems, recv_sems, a, "theirs").wait_recv()
        for a in range(len(outs)):
            self._copy(outs, send_sems, recv_sems, a, "mine").wait_send()


class _Gather(_Exchange):
    def __init__(self, bufs, steps=None):
        super().__init__(bufs, [jax.ShapeDtypeStruct(b.shape, b.dtype) for b in bufs], {a: a for a in range(len(bufs))}, 6 * len(bufs))
        self.steps = steps

    def mid_steps(self):
        return sorted(set(self.steps or []))

    def _copy(self, outs, send_sems, recv_sems, a, j, chip, half, to):
        blk = outs[a].at[2 * chip[0] + chip[1], half]
        return pltpu.make_async_remote_copy(src_ref=blk, dst_ref=blk, send_sem=send_sems.at[6 * a + j], recv_sem=recv_sems.at[6 * a + j], device_id=to, device_id_type=MESH)

    def start(self, ins, outs, send_sems, recv_sems):
        x, y, c, others = _place()
        for a in range(len(outs)):
            for j, chip in enumerate(others):
                self._copy(outs, send_sems, recv_sems, a, j, (x, y), c, (*chip, c)).start()

    def mid(self, ins, outs, send_sems, recv_sems, step=None):
        x, y, c, others = _place()
        for a in range(len(outs)):
            if step is not None and self.steps[a] != step:
                continue
            for j, chip in enumerate(others):
                self._copy(outs, send_sems, recv_sems, a, j, chip, c, (x, y, c)).wait_recv()
                self._copy(outs, send_sems, recv_sems, a, 3 + j, chip, c, (x, y, 1 - c)).start()

    def finish(self, ins, outs, send_sems, recv_sems):
        x, y, c, others = _place()
        for a in range(len(outs)):
            for j, chip in enumerate(others):
                self._copy(outs, send_sems, recv_sems, a, 3 + j, chip, 1 - c, (x, y, c)).wait_recv()
        for a in range(len(outs)):
            for j, chip in enumerate(others):
                self._copy(outs, send_sems, recv_sems, a, j, (x, y), c, (*chip, c)).wait_send()
                self._copy(outs, send_sems, recv_sems, a, 3 + j, chip, c, (x, y, 1 - c)).wait_send()


class _SemSlice:
    def __init__(self, sems, off):
        self.sems, self.off = sems, off

    @property
    def at(self):
        return self

    def __getitem__(self, i):
        return self.sems.at[self.off + i]


class _Group(_Exchange):
    def __init__(self, exchanges):
        ins, outs, aliases, n_sems, self.spans = [], [], {}, 0, []
        for ex in exchanges:
            self.spans.append((len(ins), len(outs), n_sems))
            aliases.update({len(ins) + i: len(outs) + o for i, o in ex.aliases.items()})
            ins, outs, n_sems = ins + ex.ins, outs + ex.out_shapes, n_sems + ex.n_sems
        super().__init__(ins, outs, aliases, n_sems)
        self.exchanges = exchanges

    def mid_steps(self):
        return sorted({s for ex in self.exchanges for s in ex.mid_steps()})

    def _each(self, method, ins, outs, send_sems, recv_sems, **kw):
        for ex, (i0, o0, s0) in zip(self.exchanges, self.spans):
            getattr(ex, method)(ins[i0 : i0 + len(ex.ins)], outs[o0 : o0 + len(ex.out_shapes)], _SemSlice(send_sems, s0), _SemSlice(recv_sems, s0), **kw)

    def start(self, *refs):
        self._each("start", *refs)

    def mid(self, *refs, step=None):
        self._each("mid", *refs, step=step)

    def finish(self, *refs):
        self._each("finish", *refs)

    def split(self, outs):
        return [outs[o0 : o0 + len(ex.out_shapes)] for ex, (_, o0, _) in zip(self.exchanges, self.spans)]


def _exchange_call(ex, name):
    n, m = len(ex.ins), len(ex.out_shapes)

    def body(*refs):
        parts = (refs[:n], refs[n : n + m], refs[n + m], refs[n + m + 1])
        ex.start(*parts)
        ex.mid(*parts)
        ex.finish(*parts)

    return pl.pallas_call(
        body,
        name=name,
        out_shape=ex.out_shapes,
        in_specs=[ANY] * n,
        out_specs=[ANY] * m,
        scratch_shapes=[pltpu.SemaphoreType.DMA((ex.n_sems,)), pltpu.SemaphoreType.DMA((ex.n_sems,))],
        input_output_aliases=ex.aliases,
    )(*ex.ins)


def _pallas(body, *, name, grid, in_specs, out_specs, out_shape, operands, scratch_shapes=(), exchange=None):
    params = pltpu.CompilerParams(dimension_semantics=("arbitrary",) * len(grid), vmem_limit_bytes=VMEM_LIMIT)
    if exchange is None:
        outs = pl.pallas_call(body, name=name, grid=grid, in_specs=in_specs, out_specs=out_specs, out_shape=out_shape, scratch_shapes=list(scratch_shapes), compiler_params=params)(*operands)
        return outs, []
    ex = exchange
    n_in, n_out, n_scr = len(in_specs), len(out_specs), len(scratch_shapes)
    x_in, x_out = len(ex.ins), len(ex.out_shapes)

    def hosted(*refs):
        ins, x_ins = refs[:n_in], refs[n_in : n_in + x_in]
        outs, x_outs = refs[n_in + x_in : n_in + x_in + n_out], refs[n_in + x_in + n_out : n_in + x_in + n_out + x_out]
        rest = refs[n_in + x_in + n_out + x_out :]
        parts = (x_ins, x_outs, rest[n_scr], rest[n_scr + 1])
        ids = [pl.program_id(d) for d in range(len(grid))]
        first = _all([i == 0 for i in ids])
        last = _all([i == g - 1 for i, g in zip(ids, grid)])

        @pl.when(first)
        def _():
            ex.start(*parts)

        body(*ins, *outs, *rest[:n_scr])

        for step in ex.mid_steps():

            @pl.when(ids[0] == step)
            def _(step=step):
                ex.mid(*parts, step=step)

        @pl.when(last)
        def _():
            ex.finish(*parts)

    outs = pl.pallas_call(
        hosted,
        name=name,
        grid=grid,
        in_specs=list(in_specs) + [ANY] * x_in,
        out_specs=list(out_specs) + [ANY] * x_out,
        out_shape=list(out_shape) + ex.out_shapes,
        scratch_shapes=list(scratch_shapes) + [pltpu.SemaphoreType.DMA((ex.n_sems,)), pltpu.SemaphoreType.DMA((ex.n_sems,))],
        input_output_aliases={n_in + i: n_out + o for i, o in ex.aliases.items()},
        compiler_params=params,
    )(*operands, *ex.ins)
    return outs[:n_out], outs[n_out:]


def _all(conds):
    out = conds[0]
    for c in conds[1:]:
        out = jnp.logical_and(out, c)
    return out


def _row_tile(rows):
    if rows <= 512:
        return rows
    for t in (512, 352, 256, 176, 128, 112, 64, 32, 16, 8):
        if rows % t == 0:
            return t
    return rows


def _add_own_half(view, got, place, name, wire=WIRE_DTYPE):
    _, _, r, c = view.shape
    tr = _row_tile(r)

    def body(place_ref, v_ref, g_ref, o_ref, w_ref):
        s = v_ref[...] + g_ref[...]
        w_ref[...] = s.astype(w_ref.dtype)

        @pl.when(pl.program_id(1) == place_ref[0])
        def _():
            o_ref[...] = s

    blk = pl.BlockSpec((None, tr, c), lambda i, k, pr: (k, i, 0))
    return pl.pallas_call(
        body,
        name=name,
        out_shape=[jax.ShapeDtypeStruct((r, c), F32), jax.ShapeDtypeStruct((4, r, c), wire)],
        grid_spec=pltpu.PrefetchScalarGridSpec(
            num_scalar_prefetch=1,
            grid=(r // tr, 4),
            in_specs=[pl.BlockSpec((None, None, tr, c), lambda i, k, pr: (k, pr[1], i, 0)), blk],
            out_specs=[pl.BlockSpec((tr, c), lambda i, k, pr: (i, 0)), blk],
        ),
        compiler_params=pltpu.CompilerParams(dimension_semantics=("arbitrary", "arbitrary"), vmem_limit_bytes=VMEM_LIMIT),
    )(place, view, got)


def _add_chips(part, got, place, name):
    r, c = part.shape
    tr = _row_tile(r)

    def body(place_ref, p_ref, g_ref, o_ref):
        o_ref[...] = ((p_ref[...] + g_ref[0].astype(F32)) + g_ref[1].astype(F32)) + g_ref[2].astype(F32)

    return pl.pallas_call(
        body,
        name=name,
        out_shape=jax.ShapeDtypeStruct((2, r, c), F32),
        grid_spec=pltpu.PrefetchScalarGridSpec(
            num_scalar_prefetch=1,
            grid=(r // tr,),
            in_specs=[pl.BlockSpec((tr, c), lambda i, pr: (i, 0)), pl.BlockSpec((3, tr, c), lambda i, pr: (0, i, 0))],
            out_specs=pl.BlockSpec((None, tr, c), lambda i, pr: (pr[1], i, 0)),
        ),
        compiler_params=pltpu.CompilerParams(dimension_semantics=("arbitrary",), vmem_limit_bytes=VMEM_LIMIT),
    )(place, part, got)


def _adamw_update(w_ref, g_ref, m_ref, v_ref, d_ref, nm_ref, nv_ref):
    g = g_ref[...]
    nm = ADAM_B1 * m_ref[...] + (1.0 - ADAM_B1) * g
    nv = ADAM_B2 * v_ref[...] + (1.0 - ADAM_B2) * (g * g)
    m_hat = nm / (1.0 - ADAM_B1**ADAM_STEP)
    v_hat = nv / (1.0 - ADAM_B2**ADAM_STEP)
    d_ref[...] = -ADAM_LR * (m_hat / (jnp.sqrt(v_hat) + ADAM_EPS) + ADAM_WD * w_ref[...])
    nm_ref[...] = nm
    nv_ref[...] = nv


def _adamw(w, g, m, v, name):
    r, c = w.shape
    tr = _row_tile(r)
    spec = pl.BlockSpec((tr, c), lambda i: (i, 0))
    outs, _ = _pallas(_adamw_update, name=name, grid=(r // tr,), in_specs=[spec] * 4, out_specs=[spec] * 3, out_shape=[jax.ShapeDtypeStruct((r, c), F32)] * 3, operands=(w, g, m, v))
    return outs


def _adamw_many(ws, gs, ms, vs, name):
    n = len(ws)

    def body(*refs):
        for k in range(n):
            _adamw_update(*(refs[j * n + k] for j in range(7)))

    outs = pl.pallas_call(
        body,
        name=name,
        out_shape=[jax.ShapeDtypeStruct(w.shape, F32) for w in ws] * 3,
        in_specs=[pl.BlockSpec(memory_space=pltpu.VMEM)] * (4 * n),
        out_specs=[pl.BlockSpec(memory_space=pltpu.VMEM)] * (3 * n),
    )(*ws, *gs, *ms, *vs)
    return outs[:n], outs[n : 2 * n], outs[2 * n :]


def _load_weights(pairs, sems):
    @pl.when(pl.program_id(0) == 0)
    def _():
        cps = [pltpu.make_async_copy(src, dst, sems.at[k]) for k, (src, dst) in enumerate(pairs)]
        for cp in cps:
            cp.start()
        for cp in cps:
            cp.wait()


def _ffn_chunks(F):
    bounds = list(range(0, F, FFN_CHUNK)) + [F]
    return list(zip(bounds[:-1], bounds[1:]))


def _shifted_specs(tm, cols):
    per = tm // N_META
    return [_row_spec(tm, cols), pl.BlockSpec((N_META, cols), lambda i: (jnp.maximum(i * per - 1, 0), 0))]


def _shifted_tile(cur_ref, before_ref, tm):
    return jnp.concatenate([before_ref[...], cur_ref[0 : tm - N_META, :]], axis=0)


def _tokens_tile(cur_ref, before_ref, meta_ref, tm, tile_0):
    first = jnp.where(tile_0, meta_ref[...], before_ref[...])
    return jnp.concatenate([first, cur_ref[0 : tm - N_META, :]], axis=0)


def _ffn_fwd_loss(h, target, g_pre, g_post, wg, wu, wd, name):
    T, D = h.shape
    F = wg.shape[0]
    tm = FFN_TILE

    def body(h_ref, t_ref, tb_ref, gpre_ref, gpost_ref, wg_hbm, wu_hbm, wd_hbm, dy_ref, sq_ref, f_ref, ga_ref, si_ref, s_ref, n_ref, wg_v, wu_v, wd_v, sems):
        i = pl.program_id(0)
        _load_weights([(wg_hbm, wg_v), (wu_hbm, wu_v), (wd_hbm, wd_v)], sems)
        hh = h_ref[...]
        n = (hh * _rms_stat(hh) * gpre_ref[...]).astype(MXU_DTYPE)
        n_ref[...] = n.astype(n_ref.dtype)
        f = jnp.zeros((tm, D), F32)
        for lo, hi in _ffn_chunks(F):
            a = _mm_nt(n, wg_v[lo:hi, :])
            b = _mm_nt(n, wu_v[lo:hi, :])
            sg = _sigmoid(a)
            si = a * sg
            s = (si * b).astype(MXU_DTYPE)
            ga_ref[:, lo:hi] = (b * (sg * (1.0 + a * (1.0 - sg)))).astype(ga_ref.dtype)
            si_ref[:, lo:hi] = si.astype(si_ref.dtype)
            s_ref[:, lo:hi] = s.astype(s_ref.dtype)
            f = f + _mm(s, wd_v[lo:hi, :])
        f_ref[...] = f
        y = hh + 0.5 * (f * _rms_stat(f) * gpost_ref[...])
        rows = i * tm + lax.broadcasted_iota(jnp.int32, (tm, D), 0)
        err = jnp.where(rows >= N_META, y - _shifted_tile(t_ref, tb_ref, tm), 0.0)
        dy_ref[...] = err * (1.0 / D)
        _acc(sq_ref, jnp.sum(err * err, axis=0, keepdims=True), i == 0)

    tok = jax.ShapeDtypeStruct((T, D), F32)
    act = jax.ShapeDtypeStruct((T, F), MXU_DTYPE)
    outs, _ = _pallas(
        body,
        name=name,
        grid=(T // tm,),
        out_shape=[tok, jax.ShapeDtypeStruct((1, D), F32), tok, act, act, act, jax.ShapeDtypeStruct((T, D), MXU_DTYPE)],
        in_specs=[_row_spec(tm, D)] + _shifted_specs(tm, D) + [_full_spec((1, D)), _full_spec((1, D)), ANY, ANY, ANY],
        out_specs=[_row_spec(tm, D), _full_spec((1, D)), _row_spec(tm, D), _row_spec(tm, F), _row_spec(tm, F), _row_spec(tm, F), _row_spec(tm, D)],
        scratch_shapes=[pltpu.VMEM(wg.shape, wg.dtype), pltpu.VMEM(wu.shape, wu.dtype), pltpu.VMEM(wd.shape, wd.dtype), pltpu.SemaphoreType.DMA((3,))],
        operands=(h, target, target, g_pre, g_post, wg, wu, wd),
    )
    return outs


def _ffn_gate_up(x, meta, g_pre, wg, wu, name, exchange=None):
    D = x.shape[1]
    T = x.shape[0] + N_META
    F = wg.shape[0]
    tm = FFN_TILE

    def body(x_ref, xb_ref, meta_ref, gpre_ref, wg_hbm, wu_hbm, ga_ref, si_ref, s_ref, n_ref, wg_v, wu_v, sems):
        _load_weights([(wg_hbm, wg_v), (wu_hbm, wu_v)], sems)
        hh = _tokens_tile(x_ref, xb_ref, meta_ref, tm, pl.program_id(0) == 0)
        n = (hh * _rms_stat(hh) * gpre_ref[...]).astype(MXU_DTYPE)
        n_ref[...] = n.astype(n_ref.dtype)
        for lo, hi in _ffn_chunks(F):
            a = _mm_nt(n, wg_v[lo:hi, :])
            b = _mm_nt(n, wu_v[lo:hi, :])
            sg = _sigmoid(a)
            si = a * sg
            ga_ref[:, lo:hi] = (b * (sg * (1.0 + a * (1.0 - sg)))).astype(ga_ref.dtype)
            si_ref[:, lo:hi] = si.astype(si_ref.dtype)
            s_ref[:, lo:hi] = (si * b).astype(s_ref.dtype)

    act = jax.ShapeDtypeStruct((T, F), MXU_DTYPE)
    return _pallas(
        body,
        name=name,
        grid=(T // tm,),
        out_shape=[act, act, act, jax.ShapeDtypeStruct((T, D), MXU_DTYPE)],
        in_specs=_shifted_specs(tm, D) + [_full_spec((N_META, D)), _full_spec((1, D)), ANY, ANY],
        out_specs=[_row_spec(tm, F), _row_spec(tm, F), _row_spec(tm, F), _row_spec(tm, D)],
        scratch_shapes=[pltpu.VMEM(wg.shape, wg.dtype), pltpu.VMEM(wu.shape, wu.dtype), pltpu.SemaphoreType.DMA((2,))],
        operands=(x, x, meta, g_pre, wg, wu),
        exchange=exchange,
    )


def _ffn_down(x, meta, s, g_post, wd, name, exchange=None):
    D = x.shape[1]
    T = x.shape[0] + N_META
    F = wd.shape[0]
    tm = _token_tile(T)

    def body(x_ref, xb_ref, meta_ref, s_ref, gpost_ref, wd_hbm, hout_ref, f_ref, wd_v, sems):
        _load_weights([(wd_hbm, wd_v)], sems)
        f = _mm(s_ref[...], wd_v[...])
        f_ref[...] = f
        hout_ref[...] = _tokens_tile(x_ref, xb_ref, meta_ref, tm, pl.program_id(0) == 0) + 0.5 * (f * _rms_stat(f) * gpost_ref[...])

    tok = jax.ShapeDtypeStruct((T, D), F32)
    return _pallas(
        body,
        name=name,
        grid=(T // tm,),
        out_shape=[tok, tok],
        in_specs=_shifted_specs(tm, D) + [_full_spec((N_META, D)), _row_spec(tm, F), _full_spec((1, D)), ANY],
        out_specs=[_row_spec(tm, D), _row_spec(tm, D)],
        scratch_shapes=[pltpu.VMEM(wd.shape, wd.dtype), pltpu.SemaphoreType.DMA((1,))],
        operands=(x, x, meta, s, g_post, wd),
        exchange=exchange,
    )


def _ffn_bwd(dh, f, ga, si, h, g_post, g_pre, wg, wu, wd, name):
    T, D = dh.shape
    F = wd.shape[0]
    tm = FFN_TILE

    def body(dh_ref, f_ref, ga_ref, si_ref, h_ref, gpost_ref, gpre_ref, wg_hbm, wu_hbm, wd_hbm, da_ref, db_ref, df_ref, dhin_ref, dgpost_ref, dgpre_ref, wg_v, wu_v, wd_v, sems):
        first = pl.program_id(0) == 0
        _load_weights([(wg_hbm, wg_v), (wu_hbm, wu_v), (wd_hbm, wd_v)], sems)
        dh = dh_ref[...]
        df, dg = _rms_bwd(f_ref[...], gpost_ref[...], 0.5 * dh)
        _acc(dgpost_ref, dg, first)
        dfb = df.astype(MXU_DTYPE)
        df_ref[...] = dfb.astype(df_ref.dtype)
        dn = jnp.zeros((tm, D), F32)
        for lo, hi in _ffn_chunks(F):
            ds = _mm_nt(dfb, wd_v[lo:hi, :])
            da = (ds * ga_ref[:, lo:hi].astype(F32)).astype(MXU_DTYPE)
            db = (ds * si_ref[:, lo:hi].astype(F32)).astype(MXU_DTYPE)
            da_ref[:, lo:hi] = da.astype(da_ref.dtype)
            db_ref[:, lo:hi] = db.astype(db_ref.dtype)
            dn = dn + _mm(da, wg_v[lo:hi, :]) + _mm(db, wu_v[lo:hi, :])
        dx, dg = _rms_bwd(h_ref[...], gpre_ref[...], dn)
        _acc(dgpre_ref, dg, first)
        dhin_ref[...] = dh + dx

    act = jax.ShapeDtypeStruct((T, F), MXU_DTYPE)
    vec = jax.ShapeDtypeStruct((1, D), F32)
    outs, _ = _pallas(
        body,
        name=name,
        grid=(T // tm,),
        out_shape=[act, act, jax.ShapeDtypeStruct((T, D), MXU_DTYPE), jax.ShapeDtypeStruct((T, D), F32), vec, vec],
        in_specs=[_row_spec(tm, D), _row_spec(tm, D), _row_spec(tm, F), _row_spec(tm, F), _row_spec(tm, D), _full_spec((1, D)), _full_spec((1, D)), ANY, ANY, ANY],
        out_specs=[_row_spec(tm, F), _row_spec(tm, F), _row_spec(tm, D), _row_spec(tm, D), _full_spec((1, D)), _full_spec((1, D))],
        scratch_shapes=[pltpu.VMEM(wg.shape, wg.dtype), pltpu.VMEM(wu.shape, wu.dtype), pltpu.VMEM(wd.shape, wd.dtype), pltpu.SemaphoreType.DMA((3,))],
        operands=(dh, f, ga, si, h, g_post, g_pre, wg, wu, wd),
    )
    return outs


def _ffn_bwd_first(dh, f, ga, si, x, meta, g_post, g_pre, wg, wu, wd, name):
    D = x.shape[1]
    T = x.shape[0] + N_META
    F = wd.shape[0]
    tm = FFN_TILE
    nt = T // tm
    per = tm // N_META
    tile = lambda i: jnp.minimum(i, nt - 1)

    def body(dh_ref, f_ref, ga_ref, si_ref, x_ref, xb_ref, meta_ref, gpost_ref, gpre_ref, wg_hbm, wu_hbm, wd_hbm,
             da_ref, db_ref, df_ref, dx_ref, dmeta_ref, dgpost_ref, dgpre_ref, wg_v, wu_v, wd_v, sems, held):
        i = pl.program_id(0)
        _load_weights([(wg_hbm, wg_v), (wu_hbm, wu_v), (wd_hbm, wd_v)], sems)

        @pl.when(i < nt)
        def _():
            dh = dh_ref[...]
            df, dg = _rms_bwd(f_ref[...], gpost_ref[...], 0.5 * dh)
            _acc(dgpost_ref, dg, i == 0)
            dfb = df.astype(MXU_DTYPE)
            df_ref[...] = dfb.astype(df_ref.dtype)
            dn = jnp.zeros((tm, D), F32)
            for lo, hi in _ffn_chunks(F):
                ds = _mm_nt(dfb, wd_v[lo:hi, :])
                da = (ds * ga_ref[:, lo:hi].astype(F32)).astype(MXU_DTYPE)
                db = (ds * si_ref[:, lo:hi].astype(F32)).astype(MXU_DTYPE)
                da_ref[:, lo:hi] = da.astype(da_ref.dtype)
                db_ref[:, lo:hi] = db.astype(db_ref.dtype)
                dn = dn + _mm(da, wg_v[lo:hi, :]) + _mm(db, wu_v[lo:hi, :])
            dx, dg = _rms_bwd(_tokens_tile(x_ref, xb_ref, meta_ref, tm, i == 0), gpre_ref[...], dn)
            _acc(dgpre_ref, dg, i == 0)
            dh_in = dh + dx

            @pl.when(i == 0)
            def _():
                dmeta_ref[...] = dh_in[0:N_META, :]

            @pl.when(i > 0)
            def _():
                dx_ref[...] = jnp.concatenate([held[...], dh_in[0:N_META, :]], axis=0)

            held[...] = dh_in[N_META:, :]

        @pl.when(i == nt)
        def _():
            dx_ref[0 : tm - N_META, :] = held[...]

    rows = lambda cols: pl.BlockSpec((tm, cols), lambda i: (tile(i), 0))
    act = jax.ShapeDtypeStruct((T, F), MXU_DTYPE)
    vec = jax.ShapeDtypeStruct((1, D), F32)
    outs, _ = _pallas(
        body,
        name=name,
        grid=(nt + 1,),
        out_shape=[act, act, jax.ShapeDtypeStruct((T, D), MXU_DTYPE), jax.ShapeDtypeStruct((T - N_META, D), F32), jax.ShapeDtypeStruct((N_META, D), F32), vec, vec],
        in_specs=[rows(D), rows(D), rows(F), rows(F), rows(D), pl.BlockSpec((N_META, D), lambda i: (jnp.maximum(tile(i) * per - 1, 0), 0)), _full_spec((N_META, D)),
                  _full_spec((1, D)), _full_spec((1, D)), ANY, ANY, ANY],
        out_specs=[rows(F), rows(F), rows(D), pl.BlockSpec((tm, D), lambda i: (jnp.maximum(i - 1, 0), 0)), _full_spec((N_META, D)), _full_spec((1, D)), _full_spec((1, D))],
        scratch_shapes=[pltpu.VMEM(wg.shape, wg.dtype), pltpu.VMEM(wu.shape, wu.dtype), pltpu.VMEM(wd.shape, wd.dtype), pltpu.SemaphoreType.DMA((3,)), pltpu.VMEM((tm - N_META, D), F32)],
        operands=(dh, f, ga, si, x, x, meta, g_post, g_pre, wg, wu, wd),
    )
    return outs


def _token_tile(T):
    for t in (912, 864, 432):
        if T % t == 0:
            return t
    raise ValueError(f"no token tile for {T} rows")


def _tn_matmul(xm, ym, name, exchange=None):
    T, M = xm.shape
    N = ym.shape[1]
    if (T - N_META) % TN_TILE:
        tk = _token_tile(T)

        def body(x_ref, y_ref, o_ref):
            _acc(o_ref, _mm_tn(x_ref[...], y_ref[...]), pl.program_id(0) == 0)

        grid, operands = (T // tk,), (xm, ym)
        in_specs = [pl.BlockSpec((tk, M), lambda k: (k, 0)), pl.BlockSpec((tk, N), lambda k: (k, 0))]
    else:
        tk = TN_TILE

        def body(x_ref, y_ref, xh_ref, yh_ref, o_ref):
            prod = _mm_tn(x_ref[...], y_ref[...])

            @pl.when(pl.program_id(0) == 0)
            def _():
                o_ref[...] = prod + _mm_tn(xh_ref[...], yh_ref[...])

            @pl.when(pl.program_id(0) > 0)
            def _():
                o_ref[...] += prod

        grid, operands = ((T - N_META) // tk,), (xm, ym, xm, ym)
        start = lambda k: (pl.multiple_of(N_META + k * tk, N_META), 0)
        in_specs = [pl.BlockSpec((pl.Element(tk), pl.Element(M)), start), pl.BlockSpec((pl.Element(tk), pl.Element(N)), start),
                    pl.BlockSpec((N_META, M), lambda k: (0, 0)), pl.BlockSpec((N_META, N), lambda k: (0, 0))]

    (out,), x_outs = _pallas(
        body,
        name=name,
        grid=grid,
        out_shape=[jax.ShapeDtypeStruct((M, N), F32)],
        in_specs=in_specs,
        out_specs=[_full_spec((M, N))],
        operands=operands,
        exchange=exchange,
    )
    return out, x_outs


TAB_A, TAB_AS1, TAB_AS2, TAB_AS4, TAB_JF, TAB_JB = 0, 2, 4, 6, 8, 10


def _scan_inplace(zr, zi, tabs, pows, car_r, car_i, seg, reverse, base=0):
    n_slabs = zr.shape[0]
    sgn = -1.0 if reverse else 1.0
    row = lax.broadcasted_iota(jnp.int32, (SUBLANES, LANES), 0)

    def cmul(pr, pi, xr, xi):
        return pr * xr - pi * xi, pr * xi + pi * xr

    for k0 in range(0, n_slabs, SLAB_GROUP):
        slabs = range(k0, min(k0 + SLAB_GROUP, n_slabs))
        ar = [tabs[TAB_A, k] for k in slabs]
        ai = [sgn * tabs[TAB_A + 1, k] for k in slabs]

        def first_pass(t, carry):
            r = (seg - 1 - t) if reverse else t
            out = []
            for q, k in enumerate(slabs):
                xr, xi = carry[2 * q], carry[2 * q + 1]
                pr, pi = cmul(ar[q], ai[q], xr, xi)
                nr = pr + zr[k, pl.ds(base + r, SUBLANES, stride=seg), :]
                ni = pi + zi[k, pl.ds(base + r, SUBLANES, stride=seg), :]
                zr[k, pl.ds(base + r, SUBLANES, stride=seg), :] = nr
                zi[k, pl.ds(base + r, SUBLANES, stride=seg), :] = ni
                out += [nr, ni]
            return tuple(out)

        ends = lax.fori_loop(0, seg, first_pass, tuple(jnp.zeros((SUBLANES, LANES), F32) for _ in range(2 * len(slabs))))

        incoming = []
        for q, k in enumerate(slabs):
            fr, fi = ends[2 * q], ends[2 * q + 1]
            for d, tab in ((1, TAB_AS1), (2, TAB_AS2), (4, TAB_AS4)):
                shift, keep = (SUBLANES - d, row < SUBLANES - d) if reverse else (d, row >= d)
                sr = jnp.where(keep, pltpu.roll(fr, shift, 0), 0.0)
                si = jnp.where(keep, pltpu.roll(fi, shift, 0), 0.0)
                pr, pi = cmul(tabs[tab, k], sgn * tabs[tab + 1, k], sr, si)
                fr, fi = fr + pr, fi + pi
            cr, ci = car_r[k], car_i[k]
            jtab = TAB_JB if reverse else TAB_JF
            pr, pi = cmul(tabs[jtab, k], sgn * tabs[jtab + 1, k], cr, ci)
            er, ei = fr + pr, fi + pi
            if reverse:
                inr = jnp.where(row < SUBLANES - 1, pltpu.roll(er, SUBLANES - 1, 0), cr)
                ini = jnp.where(row < SUBLANES - 1, pltpu.roll(ei, SUBLANES - 1, 0), ci)
                car_r[k] = jnp.broadcast_to(er[0:1, :], (SUBLANES, LANES))
                car_i[k] = jnp.broadcast_to(ei[0:1, :], (SUBLANES, LANES))
            else:
                inr = jnp.where(row >= 1, pltpu.roll(er, 1, 0), cr)
                ini = jnp.where(row >= 1, pltpu.roll(ei, 1, 0), ci)
                car_r[k] = jnp.broadcast_to(er[SUBLANES - 1 : SUBLANES, :], (SUBLANES, LANES))
                car_i[k] = jnp.broadcast_to(ei[SUBLANES - 1 : SUBLANES, :], (SUBLANES, LANES))
            incoming += [inr, ini]

        def second_pass(r, _):
            p = (seg - 1 - r) if reverse else r
            for q, k in enumerate(slabs):
                pr, pi = cmul(pows[0, k, p], sgn * pows[1, k, p], incoming[2 * q], incoming[2 * q + 1])
                zr[k, pl.ds(base + r, SUBLANES, stride=seg), :] = zr[k, pl.ds(base + r, SUBLANES, stride=seg), :] + pr
                zi[k, pl.ds(base + r, SUBLANES, stride=seg), :] = zi[k, pl.ds(base + r, SUBLANES, stride=seg), :] + pi
            return 0

        lax.fori_loop(0, seg, second_pass, 0)


def _slabs_to_cols(ref, k0, n):
    return jnp.concatenate([ref[k0 + q] for q in range(n)], axis=1)


def _window_sum(ext, doublings, forward):
    rows = ext.shape[0]
    s = ext
    for k in range(doublings):
        s = s + pltpu.roll(s, (1 << k) if forward else rows - (1 << k), 0)
    return s


def _mix_fwd(h1, g_pre, g_so, g_po, g_post, dskip, pscale, win, wout, bbre, bbim, ccre, ccim, wgv, wgg, pw, tabs, pows, name, exchange=None):
    T, D = h1.shape
    W = D // 2
    tm = MIX_SUBTILES * MIX_TILE
    seg = MIX_TILE // SUBLANES
    n_slabs = tabs.shape[1]
    nch, cch, sch = bbre.shape
    spc = sch // LANES
    ngc, gch, _ = wgv.shape
    pg = W // len(POOL_WINDOWS)

    def body(h_ref, gpre_ref, gso_ref, gpo_ref, gpost_ref, dskip_ref, pscale_ref, win_ref, wout_ref, bbre_ref, bbim_ref, ccre_ref, ccim_ref, wgv_ref, wgg_ref, pw_ref, tabs_ref, pows_ref,
             proj_ref, xr_ref, xi_ref, y_ref, pooled_ref, mixed_ref, h2_ref, n2_ref, cat_ref, car_r, car_i, halo):
        i = pl.program_id(0)

        @pl.when(i == 0)
        def _():
            car_r[...] = jnp.zeros_like(car_r)
            car_i[...] = jnp.zeros_like(car_i)
            halo[...] = jnp.zeros_like(halo)

        hh = h_ref[...]
        n2 = (hh * _rms_stat(hh) * gpre_ref[...]).astype(MXU_DTYPE)
        n2_ref[...] = n2.astype(n2_ref.dtype)
        proj = _mm(n2, win_ref[...])
        proj_ref[...] = proj
        us, up = proj[:, :W], proj[:, W:]

        for c in range(nch):
            uc = us[:, c * cch : (c + 1) * cch].astype(MXU_DTYPE)
            bur, bui = _mm(uc, bbre_ref[c]), _mm(uc, bbim_ref[c])
            for q in range(spc):
                xr_ref[c * spc + q] = bur[:, q * LANES : (q + 1) * LANES]
                xi_ref[c * spc + q] = bui[:, q * LANES : (q + 1) * LANES]
        for sub in range(MIX_SUBTILES):
            _scan_inplace(xr_ref, xi_ref, tabs_ref, pows_ref, car_r, car_i, seg, reverse=False, base=sub * MIX_TILE)
        ys = []
        for c in range(nch):
            ys.append(_mm(_slabs_to_cols(xr_ref, c * spc, spc), ccre_ref[c]) - _mm(_slabs_to_cols(xi_ref, c * spc, spc), ccim_ref[c]))
        y = jnp.concatenate(ys, axis=1) + dskip_ref[...] * us
        y_ref[...] = y
        ge = _gelu(y).astype(MXU_DTYPE)
        zv = jnp.concatenate([_mm(ge[:, c * gch : (c + 1) * gch], wgv_ref[c]) for c in range(ngc)], axis=1)
        zg = jnp.concatenate([_mm(ge[:, c * gch : (c + 1) * gch], wgg_ref[c]) for c in range(ngc)], axis=1)
        out = zv * _sigmoid(zg)
        cat_s = out * _rms_stat(out) * gso_ref[...]

        ext = jnp.concatenate([halo[...], up], axis=0)
        halo[...] = up[tm - POOL_HALO :, :]
        t1 = (i * tm + 1 + lax.broadcasted_iota(jnp.int32, (tm, pg), 0)).astype(F32)
        pooled, pms = [], []
        for g, w in enumerate(POOL_WINDOWS):
            col = ext[:, g * pg : (g + 1) * pg]
            win_sum = _window_sum(col, g + 1, True)[POOL_HALO:, :]
            pooled_g = win_sum / jnp.minimum(t1, float(w)) - up[:, g * pg : (g + 1) * pg]
            pooled.append(pooled_g)
            pms.append(_mm(pooled_g, pw_ref[g]))
        pooled_ref[...] = jnp.concatenate(pooled, axis=1)
        yp = jnp.concatenate(pms, axis=1) * pscale_ref[...]
        cat_p = yp * _rms_stat(yp) * gpo_ref[...]

        cat = jnp.concatenate([cat_s, cat_p], axis=1).astype(MXU_DTYPE)
        cat_ref[...] = cat.astype(cat_ref.dtype)
        mixed = _mm(cat, wout_ref[...])
        mixed_ref[...] = mixed
        h2_ref[...] = hh + mixed * _rms_stat(mixed) * gpost_ref[...]

    tok = lambda cols, dt=F32: jax.ShapeDtypeStruct((T, cols), dt)
    slab_spec = pl.BlockSpec((n_slabs, tm, LANES), lambda i: (0, i, 0))
    operands = (h1, g_pre, g_so, g_po, g_post, dskip, pscale, win, wout, bbre, bbim, ccre, ccim, wgv, wgg, pw, tabs, pows)
    return _pallas(
        body,
        name=name,
        grid=(T // tm,),
        out_shape=[tok(D), jax.ShapeDtypeStruct((n_slabs, T, LANES), F32), jax.ShapeDtypeStruct((n_slabs, T, LANES), F32), tok(W), tok(W), tok(D), tok(D), tok(D, MXU_DTYPE), tok(D, MXU_DTYPE)],
        in_specs=[_row_spec(tm, D)] + [_full_spec(o.shape, single=True) for o in operands[1:]],
        out_specs=[_row_spec(tm, D), slab_spec, slab_spec, _row_spec(tm, W), _row_spec(tm, W), _row_spec(tm, D), _row_spec(tm, D), _row_spec(tm, D), _row_spec(tm, D)],
        scratch_shapes=[pltpu.VMEM((n_slabs, SUBLANES, LANES), F32), pltpu.VMEM((n_slabs, SUBLANES, LANES), F32), pltpu.VMEM((POOL_HALO, W), F32)],
        operands=operands,
        exchange=exchange,
    )


def _mix_bwd_heads(dh2, mixed, y, pooled, proj, g_so, g_po, g_post, pscale, wout, wgv, wgg, pw, name, exchange=None):
    T, D = dh2.shape
    W = D // 2
    tm = _token_tile(T)
    nch, cch, _ = wgv.shape
    ng, pg, _ = pw.shape

    def body(dh2_ref, mixed_ref, y_ref, pooled_ref, us_ref, gso_ref, gpo_ref, gpost_ref, pscale_ref, wout_ref, wgv_ref, wgg_ref, pw_ref,
             dy_ref, dpooled_ref, dmixed_ref, dgpost_ref, dgso_ref, dgpo_ref, dd_ref, dscale_ref, dwgv_ref, dwgg_ref, dpw_ref):
        first = pl.program_id(0) == 0
        dmixed, dgpost = _rms_bwd(mixed_ref[...], gpost_ref[...], dh2_ref[...])
        _acc(dgpost_ref, dgpost, first)
        dmb = dmixed.astype(MXU_DTYPE)
        dmixed_ref[...] = dmb.astype(dmixed_ref.dtype)
        dcat = _mm_nt(dmb, wout_ref[...])
        dcs, dcp = dcat[:, :W], dcat[:, W:]

        y = y_ref[...]
        ge = _gelu(y).astype(MXU_DTYPE)
        zv = jnp.concatenate([_mm(ge[:, c * cch : (c + 1) * cch], wgv_ref[c]) for c in range(nch)], axis=1)
        zg = jnp.concatenate([_mm(ge[:, c * cch : (c + 1) * cch], wgg_ref[c]) for c in range(nch)], axis=1)
        sg = _sigmoid(zg)
        dout, dgso = _rms_bwd(zv * sg, gso_ref[...], dcs)
        _acc(dgso_ref, dgso, first)
        dzv = (dout * sg).astype(MXU_DTYPE)
        dzg = (dout * zv * sg * (1.0 - sg)).astype(MXU_DTYPE)
        dges = []
        for c in range(nch):
            cs = slice(c * cch, (c + 1) * cch)
            dges.append(_mm_nt(dzv[:, cs], wgv_ref[c]) + _mm_nt(dzg[:, cs], wgg_ref[c]))
            _acc(dwgv_ref.at[c], _mm_tn(ge[:, cs], dzv[:, cs]), first)
            _acc(dwgg_ref.at[c], _mm_tn(ge[:, cs], dzg[:, cs]), first)
        dy = jnp.concatenate(dges, axis=1) * _gelu_grad(y)
        dy_ref[...] = dy
        _acc(dd_ref, jnp.sum(dy * us_ref[...], axis=0, keepdims=True), first)

        pooled_b = pooled_ref[...].astype(MXU_DTYPE)
        pm = jnp.concatenate([_mm(pooled_b[:, g * pg : (g + 1) * pg], pw_ref[g]) for g in range(ng)], axis=1)
        dyp, dgpo = _rms_bwd(pm * pscale_ref[...], gpo_ref[...], dcp)
        _acc(dgpo_ref, dgpo, first)
        _acc(dscale_ref, jnp.sum(dyp * pm, axis=0, keepdims=True), first)
        dpm = (dyp * pscale_ref[...]).astype(MXU_DTYPE)
        dps = []
        for g in range(ng):
            gs = slice(g * pg, (g + 1) * pg)
            dps.append(_mm_nt(dpm[:, gs], pw_ref[g]))
            _acc(dpw_ref.at[g], _mm_tn(pooled_b[:, gs], dpm[:, gs]), first)
        dpooled_ref[...] = jnp.concatenate(dps, axis=1)

    vec = lambda n: jax.ShapeDtypeStruct((1, n), F32)
    operands = (dh2, mixed, y, pooled, proj, g_so, g_po, g_post, pscale, wout, wgv, wgg, pw)
    return _pallas(
        body,
        name=name,
        grid=(T // tm,),
        out_shape=[jax.ShapeDtypeStruct((T, W), F32), jax.ShapeDtypeStruct((T, W), F32), jax.ShapeDtypeStruct((T, D), MXU_DTYPE), vec(D), vec(W), vec(W), vec(W), vec(W),
                   jax.ShapeDtypeStruct(wgv.shape, F32), jax.ShapeDtypeStruct(wgg.shape, F32), jax.ShapeDtypeStruct(pw.shape, F32)],
        in_specs=[_row_spec(tm, D), _row_spec(tm, D), _row_spec(tm, W), _row_spec(tm, W), _row_spec(tm, W)] + [_full_spec(o.shape) for o in operands[5:]],
        out_specs=[_row_spec(tm, W), _row_spec(tm, W), _row_spec(tm, D), _full_spec((1, D)), _full_spec((1, W)), _full_spec((1, W)), _full_spec((1, W)), _full_spec((1, W)),
                   _full_spec(wgv.shape), _full_spec(wgg.shape), _full_spec(pw.shape)],
        operands=operands,
        exchange=exchange,
    )


def _mix_bwd_scan(dy, dpooled, xr, xi, proj, dskip, bbre, bbim, ccre, ccim, tabs, pows, name, exchange=None):
    T, W = dy.shape
    D = 2 * W
    tm = MIX_SUBTILES * MIX_TILE
    seg = MIX_TILE // SUBLANES
    nt = T // tm
    n_slabs = tabs.shape[1]
    nch, cch, sch = bbre.shape
    spc = sch // LANES
    pg = W // len(POOL_WINDOWS)
    blocks_per_tile = tm // SUBLANES

    def body(dy_ref, dp_ref, xr_ref, xi_ref, xpr_ref, xpi_ref, us_ref, dskip_ref, bbre_ref, bbim_ref, ccre_ref, ccim_ref, tabs_ref, pows_ref,
             dproj_ref, dccre_ref, dccim_ref, dbbre_ref, dbbim_ref, dar_ref, dai_ref, lr, li, car_r, car_i, halo):
        i = pl.program_id(0)
        first = i == 0
        tile = nt - 1 - i
        row = lax.broadcasted_iota(jnp.int32, (SUBLANES, LANES), 0)

        @pl.when(first)
        def _():
            car_r[...] = jnp.zeros_like(car_r)
            car_i[...] = jnp.zeros_like(car_i)
            halo[...] = jnp.zeros_like(halo)
            dar_ref[...] = jnp.zeros_like(dar_ref)
            dai_ref[...] = jnp.zeros_like(dai_ref)

        dy = dy_ref[...]
        for c in range(nch):
            dyc = dy[:, c * cch : (c + 1) * cch]
            gr, gi = _mm_nt(dyc, ccre_ref[c]), _mm_nt(dyc, ccim_ref[c])
            for q in range(spc):
                lr[c * spc + q] = gr[:, q * LANES : (q + 1) * LANES]
                li[c * spc + q] = -gi[:, q * LANES : (q + 1) * LANES]
            _acc(dccre_ref.at[c], _mm_tn(_slabs_to_cols(xr_ref, c * spc, spc), dyc), first)
            _acc(dccim_ref.at[c], -_mm_tn(_slabs_to_cols(xi_ref, c * spc, spc), dyc), first)
        for sub in reversed(range(MIX_SUBTILES)):
            _scan_inplace(lr, li, tabs_ref, pows_ref, car_r, car_i, seg, reverse=True, base=sub * MIX_TILE)

        for sub in range(MIX_SUBTILES):
            base = sub * MIX_TILE
            for k0 in range(0, n_slabs, SLAB_GROUP):
                slabs = range(k0, min(k0 + SLAB_GROUP, n_slabs))
                init = []
                for k in slabs:
                    if sub == 0:
                        prev_r = jnp.where(tile > 0, jnp.broadcast_to(xpr_ref[k, SUBLANES - 1 : SUBLANES, :], (SUBLANES, LANES)), 0.0)
                        prev_i = jnp.where(tile > 0, jnp.broadcast_to(xpi_ref[k, SUBLANES - 1 : SUBLANES, :], (SUBLANES, LANES)), 0.0)
                    else:
                        prev_r = jnp.broadcast_to(xr_ref[k, base - 1 : base, :], (SUBLANES, LANES))
                        prev_i = jnp.broadcast_to(xi_ref[k, base - 1 : base, :], (SUBLANES, LANES))
                    x0r = jnp.where(row >= 1, pltpu.roll(xr_ref[k, pl.ds(base + seg - 1, SUBLANES, stride=seg), :], 1, 0), prev_r)
                    x0i = jnp.where(row >= 1, pltpu.roll(xi_ref[k, pl.ds(base + seg - 1, SUBLANES, stride=seg), :], 1, 0), prev_i)
                    l0r, l0i = lr[k, pl.ds(base, SUBLANES, stride=seg), :], li[k, pl.ds(base, SUBLANES, stride=seg), :]
                    init += [l0r * x0r + l0i * x0i, l0i * x0r - l0r * x0i]

                def step(r, acc, slabs=slabs, base=base):
                    out = []
                    for q, k in enumerate(slabs):
                        pr_, pi_ = xr_ref[k, pl.ds(base + r - 1, SUBLANES, stride=seg), :], xi_ref[k, pl.ds(base + r - 1, SUBLANES, stride=seg), :]
                        lr_, li_ = lr[k, pl.ds(base + r, SUBLANES, stride=seg), :], li[k, pl.ds(base + r, SUBLANES, stride=seg), :]
                        out += [acc[2 * q] + lr_ * pr_ + li_ * pi_, acc[2 * q + 1] + li_ * pr_ - lr_ * pi_]
                    return tuple(out)

                sums = lax.fori_loop(1, seg, step, tuple(init))
                for q, k in enumerate(slabs):
                    dar_ref[k] += sums[2 * q]
                    dai_ref[k] += sums[2 * q + 1]

        us = us_ref[...]
        dus = []
        for c in range(nch):
            lrc, lic = _slabs_to_cols(lr, c * spc, spc).astype(MXU_DTYPE), _slabs_to_cols(li, c * spc, spc).astype(MXU_DTYPE)
            uc = us[:, c * cch : (c + 1) * cch]
            _acc(dbbre_ref.at[c], _mm_tn(uc, lrc), first)
            _acc(dbbim_ref.at[c], _mm_tn(uc, lic), first)
            dus.append(_mm_nt(lrc, bbre_ref[c]) + _mm_nt(lic, bbim_ref[c]))
        du_s = jnp.concatenate(dus, axis=1) + dskip_ref[...] * dy

        dp = dp_ref[...]
        t1 = (tile * tm + 1 + lax.broadcasted_iota(jnp.int32, (tm, pg), 0)).astype(F32)
        dups, heads = [], []
        for g, w in enumerate(POOL_WINDOWS):
            dpg = dp[:, g * pg : (g + 1) * pg]
            qg = dpg / jnp.minimum(t1, float(w))
            ext = jnp.concatenate([qg, halo[:, g * pg : (g + 1) * pg]], axis=0)
            dups.append(_window_sum(ext, g + 1, False)[:tm, :] - dpg)
            heads.append(qg[:POOL_HALO, :])
        halo[...] = jnp.concatenate(heads, axis=1)
        dproj_ref[...] = jnp.concatenate([du_s] + dups, axis=1).astype(dproj_ref.dtype)

    rev = lambda cols: _row_spec(tm, cols, rev_n=nt)
    slab_spec = pl.BlockSpec((n_slabs, tm, LANES), lambda i: (0, nt - 1 - i, 0))
    prev_spec = pl.BlockSpec((n_slabs, SUBLANES, LANES), lambda i: (0, jnp.maximum((nt - 1 - i) * blocks_per_tile - 1, 0), 0))
    consts = (dskip, bbre, bbim, ccre, ccim, tabs, pows)
    return _pallas(
        body,
        name=name,
        grid=(nt,),
        out_shape=[jax.ShapeDtypeStruct((T, D), MXU_DTYPE), jax.ShapeDtypeStruct(ccre.shape, F32), jax.ShapeDtypeStruct(ccim.shape, F32), jax.ShapeDtypeStruct(bbre.shape, F32),
                   jax.ShapeDtypeStruct(bbim.shape, F32), jax.ShapeDtypeStruct((n_slabs, SUBLANES, LANES), F32), jax.ShapeDtypeStruct((n_slabs, SUBLANES, LANES), F32)],
        in_specs=[rev(W), rev(W), slab_spec, slab_spec, prev_spec, prev_spec, rev(W)] + [_full_spec(o.shape, single=True) for o in consts],
        out_specs=[rev(D), _full_spec(ccre.shape), _full_spec(ccim.shape), _full_spec(bbre.shape), _full_spec(bbim.shape),
                   _full_spec((n_slabs, SUBLANES, LANES)), _full_spec((n_slabs, SUBLANES, LANES))],
        scratch_shapes=[pltpu.VMEM((n_slabs, tm, LANES), F32), pltpu.VMEM((n_slabs, tm, LANES), F32), pltpu.VMEM((n_slabs, SUBLANES, LANES), F32), pltpu.VMEM((n_slabs, SUBLANES, LANES), F32),
                        pltpu.VMEM((POOL_HALO, W), F32)],
        operands=(dy, dpooled, xr, xi, xr, xi, proj, *consts),
        exchange=exchange,
    )


def _mix_bwd_in(dproj, h1, dh2, g_pre, win, name, exchange=None):
    T, D = h1.shape
    tm = _token_tile(T)

    def body(dproj_ref, h_ref, dh2_ref, gpre_ref, win_ref, dh1_ref, dg_ref):
        dx, dg = _rms_bwd(h_ref[...], gpre_ref[...], _mm_nt(dproj_ref[...], win_ref[...]))
        _acc(dg_ref, dg, pl.program_id(0) == 0)
        dh1_ref[...] = dh2_ref[...] + dx

    return _pallas(
        body,
        name=name,
        grid=(T // tm,),
        out_shape=[jax.ShapeDtypeStruct((T, D), F32), jax.ShapeDtypeStruct((1, D), F32)],
        in_specs=[_row_spec(tm, D), _row_spec(tm, D), _row_spec(tm, D), _full_spec((1, D)), _full_spec(win.shape)],
        out_specs=[_row_spec(tm, D), _full_spec((1, D))],
        operands=(dproj, h1, dh2, g_pre, win),
        exchange=exchange,
    )


def _discretize(lam_re, lam_im, log_dt, b_re, b_im):
    dt = jnp.exp(log_dt)[:, None]
    decay = jnp.exp(lam_re * dt)
    ang = lam_im * dt
    a_re, a_im = decay * jnp.cos(ang), decay * jnp.sin(ang)
    nr = a_re - 1.0
    den = lam_re * lam_re + lam_im * lam_im
    q_re = (nr * lam_re + a_im * lam_im) / den
    q_im = (a_im * lam_re - nr * lam_im) / den
    bb_re = q_re[..., None] * b_re - q_im[..., None] * b_im
    bb_im = q_re[..., None] * b_im + q_im[..., None] * b_re
    return a_re, a_im, bb_re, bb_im


STATE_CHUNK_GROUPS = 8
GLU_CHUNK_GROUPS = 16


def _block_diag(w, rows_first, n):
    G = w.shape[0]
    nch = G // n
    if not rows_first:
        w = jnp.swapaxes(w, 1, 2)
    p, q = w.shape[1], w.shape[2]
    out = jnp.einsum("cgpq,gk->cgpkq", w.reshape(nch, n, p, q), jnp.eye(n, dtype=w.dtype))
    return out.reshape(nch, n * p, n * q)


def _block_diag_extract(m, p, q, rows_first):
    nch, n = m.shape[0], m.shape[1] // p
    out = jnp.einsum("cgpkq,gk->cgpq", m.reshape(nch, n, p, n, q), jnp.eye(n, dtype=m.dtype)).reshape(nch * n, p, q)
    return out if rows_first else jnp.swapaxes(out, 1, 2)


def _cmul(ar, ai, br, bi):
    return ar * br - ai * bi, ar * bi + ai * br


def _powers(ar, ai, count):
    pr, pi = ar[None], ai[None]
    while pr.shape[0] < count:
        nr, ni = _cmul(pr, pi, pr[-1][None], pi[-1][None])
        pr, pi = jnp.concatenate([pr, nr]), jnp.concatenate([pi, ni])
    return pr[:count], pi[:count]


def _scan_tables(a_re, a_im, seg):
    n = a_re.size
    ns = n // LANES
    ar, ai = a_re.reshape(n), a_im.reshape(n)
    pr, pi = _powers(ar, ai, seg)
    jr, ji = _powers(pr[-1], pi[-1], SUBLANES)

    def bcast(v):
        return jnp.broadcast_to(v.reshape(ns, 1, LANES), (ns, SUBLANES, LANES))

    def per_sublane(vs):
        return jnp.transpose(vs.reshape(SUBLANES, ns, LANES), (1, 0, 2))

    tabs = jnp.stack([bcast(ar), bcast(ai), bcast(jr[0]), bcast(ji[0]), bcast(jr[1]), bcast(ji[1]), bcast(jr[3]), bcast(ji[3]),
                      per_sublane(jr), per_sublane(ji), per_sublane(jr[::-1]), per_sublane(ji[::-1])])

    def rows(vs):
        return jnp.broadcast_to(jnp.transpose(vs.reshape(seg, ns, 1, LANES), (1, 0, 2, 3)), (ns, seg, SUBLANES, LANES))

    return tabs, jnp.stack([rows(pr), rows(pi)])


SMALL = ("ffn1_pre_norm", "ffn1_post_norm", "mix_pre_norm", "mix_post_norm", "ssm_lambda_re", "ssm_lambda_im", "ssm_log_dt", "ssm_b_re", "ssm_b_im", "ssm_c_re", "ssm_c_im",
         "ssm_d", "ssm_w_glu", "pool_w", "pool_scale", "ssm_out_norm", "pool_out_norm", "ffn2_pre_norm", "ffn2_post_norm")
BIG = ("ffn1_w_gate", "ffn1_w_up", "ffn1_w_down", "w_in", "w_out", "ffn2_w_gate", "ffn2_w_up", "ffn2_w_down")
ORDER = ("meta_tokens", "ffn1_pre_norm", "ffn1_post_norm", "ffn1_w_gate", "ffn1_w_up", "ffn1_w_down", "mix_pre_norm", "mix_post_norm", "w_in", "ssm_lambda_re", "ssm_lambda_im",
         "ssm_log_dt", "ssm_b_re", "ssm_b_im", "ssm_c_re", "ssm_c_im", "ssm_d", "ssm_w_glu", "pool_w", "pool_scale", "ssm_out_norm", "pool_out_norm", "w_out", "ffn2_pre_norm",
         "ffn2_post_norm", "ffn2_w_gate", "ffn2_w_up", "ffn2_w_down")
PACK_ROWS = SUBLANES * 8
def _pack(arrays, rows):
    flat = jnp.concatenate([a.reshape(-1) for a in arrays])
    return jnp.pad(flat, (0, rows * LANES - flat.size)).reshape(rows, LANES)


def _unpack(packed, shapes):
    flat = packed.reshape(-1)
    out, off = [], 0
    for s in shapes:
        n = math.prod(s)
        out.append(flat[off : off + n].reshape(s))
        off += n
    return out


def _step(p, x, loss_target, m, v):
    D = x.shape[-1]
    chip = (2 * lax.axis_index("x") + lax.axis_index("y")).astype(jnp.int32)
    place = jnp.stack([chip, lax.axis_index("c").astype(jnp.int32)])

    def gather_buffer(w):
        own = w.reshape(1, 2, w.shape[0] // 2, w.shape[1])
        return lax.dynamic_update_slice(lax.empty((4,) + own.shape[1:], own.dtype), own, (chip, 0, 0, 0))

    def rows_of(n, a):
        return jnp.swapaxes(a[0], 0, 1) if n.endswith(("w_gate", "w_up")) else a[0]

    def rows_back(n, a):
        return (jnp.swapaxes(a, 0, 1) if n.endswith(("w_gate", "w_up")) else a)[None]

    def grad_view(g):
        return g.reshape(4, 2, g.shape[0] // 8, g.shape[1])

    def reduce_sum(got_sibling, views, tag, wires=None):
        wires = wires or [WIRE_DTYPE] * len(views)
        sums = [_add_own_half(v_, g_, place, f"{tag}_add_sibling_{k}", w_) for k, (v_, g_, w_) in enumerate(zip(views, got_sibling, wires))]
        return [s[0] for s in sums], [s[1] for s in sums]

    def reduce_halves(parts, got_chips, tag):
        return [_add_chips(p_, g_, place, f"{tag}_add_chips_{k}") for k, (p_, g_) in enumerate(zip(parts, got_chips))]

    bufs = {n: gather_buffer(rows_of(n, p[n]).astype(MXU_DTYPE)) for n in BIG}
    full = {}

    def gathered(names, got):
        full.update({n: g_.reshape(-1, g_.shape[-1]) for n, g_ in zip(names, got)})

    def gather_of(names, n_steps, tenths):
        sizes = [bufs[n].size * bufs[n].dtype.itemsize for n in names]
        steps = [min(n_steps - 1, max(1, -(-tenths * n_steps * sum(sizes[: a + 1]) // (10 * sum(sizes))))) for a in range(len(names))]
        return _Gather([bufs[n] for n in names], steps=steps)

    first_names = ("ffn1_w_gate", "ffn1_w_up")
    got = _exchange_call(_Gather([bufs[n] for n in first_names] + [gather_buffer(p["meta_tokens"])]), "gather_first")
    gathered(first_names, got)
    meta = jnp.transpose(got[-1].reshape(4, N_META, -1), (1, 0, 2)).reshape(N_META, D)

    vec = lambda n: p[n].reshape(1, -1)
    G, N, H = p["ssm_b_re"].shape[1:]
    a_re, a_im, bb_re, bb_im = _discretize(p["ssm_lambda_re"][0], p["ssm_lambda_im"][0], p["ssm_log_dt"][0], p["ssm_b_re"][0], p["ssm_b_im"][0])
    tabs, pows = _scan_tables(a_re, a_im, MIX_TILE // SUBLANES)
    bf = lambda a: a.astype(MXU_DTYPE)
    sg_, gg_ = STATE_CHUNK_GROUPS, GLU_CHUNK_GROUPS
    bbre, bbim = bf(_block_diag(bb_re, False, sg_)), bf(_block_diag(bb_im, False, sg_))
    ccre, ccim = bf(_block_diag(p["ssm_c_re"][0], False, sg_)), bf(_block_diag(p["ssm_c_im"][0], False, sg_))
    wgv, wgg = bf(_block_diag(p["ssm_w_glu"][0][:, :, :H], True, gg_)), bf(_block_diag(p["ssm_w_glu"][0][:, :, H:], True, gg_))
    pw = bf(p["pool_w"][0])

    T = x.shape[1] + N_META
    names = ("ffn1_w_down", "w_in", "w_out", "ffn2_w_gate", "ffn2_w_down")
    (ga1, si1, s1, n1), got = _ffn_gate_up(
        x[0], meta, vec("ffn1_pre_norm"), full["ffn1_w_gate"], full["ffn1_w_up"], "ffn1_gate_up", exchange=gather_of(names, T // FFN_TILE, 8)
    )
    gathered(names, got)
    (h1, f1), got = _ffn_down(x[0], meta, s1, vec("ffn1_post_norm"), full["ffn1_w_down"], "ffn1_down", exchange=gather_of(("ffn2_w_up",), T // _token_tile(T), 6))
    gathered(("ffn2_w_up",), got)
    (proj, xr, xim, y, pooled, mixed, h2, n2, cat), _ = _mix_fwd(
        h1, vec("mix_pre_norm"), vec("ssm_out_norm"), vec("pool_out_norm"), vec("mix_post_norm"), vec("ssm_d"), vec("pool_scale"), full["w_in"], full["w_out"],
        bbre, bbim, ccre, ccim, wgv, wgg, pw, tabs, pows, "mix_fwd",
    )
    dh3, sq, f2, ga2, si2, s2, n3 = _ffn_fwd_loss(
        h2, loss_target[0], vec("ffn2_pre_norm"), vec("ffn2_post_norm"), full["ffn2_w_gate"], full["ffn2_w_up"], full["ffn2_w_down"], "ffn2_fwd"
    )

    g, shared = {}, {}
    ffn_names = lambda tag: (tag + "_w_gate", tag + "_w_up", tag + "_w_down")

    da, db, df, dh2, g["ffn2_post_norm"], g["ffn2_pre_norm"] = _ffn_bwd(
        dh3, f2, ga2, si2, h2, vec("ffn2_post_norm"), vec("ffn2_pre_norm"), full["ffn2_w_gate"], full["ffn2_w_up"], full["ffn2_w_down"], "ffn2_bwd"
    )
    views2 = [
        grad_view(_tn_matmul(da, n3, "ffn2_dw_gate")[0]),
        grad_view(_tn_matmul(db, n3, "ffn2_dw_up")[0]),
        grad_view(_tn_matmul(s2, df, "ffn2_dw_down")[0]),
    ]
    (dy, dpooled, dmixed, g["mix_post_norm"], g["ssm_out_norm"], g["pool_out_norm"], g["ssm_d"], g["pool_scale"], dwgv, dwgg, g["pool_w"]), got = _mix_bwd_heads(
        dh2, mixed, y, pooled, proj, vec("ssm_out_norm"), vec("pool_out_norm"), vec("mix_post_norm"), vec("pool_scale"), full["w_out"], wgv, wgg, pw, "mix_bwd_heads",
        exchange=_SiblingScatter(views2),
    )
    parts2, wire2 = reduce_sum(got, views2, "ffn2")
    (dproj, dccre, dccim, dbbre, dbbim, dar, dai), got = _mix_bwd_scan(
        dy, dpooled, xr, xim, proj, vec("ssm_d"), bbre, bbim, ccre, ccim, tabs, pows, "mix_bwd_scan", exchange=_ChipScatter(wire2)
    )
    halves2 = reduce_halves(parts2, got, "ffn2")
    (dh1, g["mix_pre_norm"]), got = _mix_bwd_in(dproj, h1, dh2, vec("mix_pre_norm"), full["w_in"], "mix_bwd_in", exchange=_SiblingShare(halves2))
    shared.update(zip(ffn_names("ffn2"), got))

    da, db, df, grad_x, d_meta, g["ffn1_post_norm"], g["ffn1_pre_norm"] = _ffn_bwd_first(
        dh1, f1, ga1, si1, x[0], meta, vec("ffn1_post_norm"), vec("ffn1_pre_norm"), full["ffn1_w_gate"], full["ffn1_w_up"], full["ffn1_w_down"], "ffn1_bwd"
    )
    grad_x = grad_x[None]

    g["ssm_c_re"] = _block_diag_extract(dccre, N, H, False)
    g["ssm_c_im"] = _block_diag_extract(dccim, N, H, False)
    g["ssm_w_glu"] = jnp.concatenate([_block_diag_extract(dwgv, H, H, True), _block_diag_extract(dwgg, H, H, True)], axis=-1)
    d_a_re, d_a_im = jnp.sum(dar, axis=1).reshape(G, N), jnp.sum(dai, axis=1).reshape(G, N)
    _, pull = jax.vjp(_discretize, p["ssm_lambda_re"][0], p["ssm_lambda_im"][0], p["ssm_log_dt"][0], p["ssm_b_re"][0], p["ssm_b_im"][0])
    g["ssm_lambda_re"], g["ssm_lambda_im"], g["ssm_log_dt"], g["ssm_b_re"], g["ssm_b_im"] = pull(
        (d_a_re, d_a_im, _block_diag_extract(dbbre, H, N, False), _block_diag_extract(dbbim, H, N, False))
    )
    small_shapes = [p[n].shape for n in SMALL] + [(N_META, D), (1,)]
    small_size = sum(math.prod(s) for s in small_shapes)
    rows = -(-small_size // (LANES * PACK_ROWS)) * PACK_ROWS
    views_s = [_pack([g[n] for n in SMALL] + [d_meta, jnp.sum(sq).reshape(1)], rows).reshape(4, 2, rows // 8, LANES)]

    dw_down, got = _tn_matmul(s1, df, "ffn1_dw_down", exchange=_SiblingScatter(views_s))
    parts_s, wire_s = reduce_sum(got, views_s, "small", wires=[F32])
    views_d = [grad_view(dw_down)]
    ex = _Group([_SiblingScatter(views_d), _ChipScatter(wire_s)])
    dw_gate, got = _tn_matmul(da, n1, "ffn1_dw_gate", exchange=ex)
    got_d, got_s = ex.split(got)
    parts_d, wire_d = reduce_sum(got_d, views_d, "ffn1_down")
    halves_s = reduce_halves(parts_s, got_s, "small")
    views_g = [grad_view(dw_gate)]
    ex = _Group([_ChipScatter(wire_d), _SiblingScatter(views_g), _SiblingShare(halves_s)])
    dw_up, got = _tn_matmul(db, n1, "ffn1_dw_up", exchange=ex)
    got_d, got_g, got_s = ex.split(got)
    small_buf = lax.dynamic_update_slice(lax.empty((4,) + got_s[0].shape, F32), got_s[0][None], (chip, 0, 0, 0))
    halves_d = reduce_halves(parts_d, got_d, "ffn1_down")
    parts_g, wire_g = reduce_sum(got_g, views_g, "ffn1_gate")
    views_u = [grad_view(dw_up)]
    tn_steps = (T - N_META) // TN_TILE if (T - N_META) % TN_TILE == 0 else T // _token_tile(T)
    ex = _Group([_SiblingShare(halves_d), _ChipScatter(wire_g), _SiblingScatter(views_u), _Gather([small_buf], steps=[(3 * tn_steps) // 4])])
    dw_in, got = _tn_matmul(n2, dproj, "dw_in", exchange=ex)
    got_d, got_g, got_u, got_s = ex.split(got)
    grads = dict(zip(SMALL + ("meta_full", "sq_sum"), _unpack(got_s[0].reshape(rows, LANES), small_shapes)))
    grads["meta_tokens"] = lax.dynamic_slice_in_dim(grads.pop("meta_full"), chip * (D // 4), D // 4, axis=1)
    loss = (0.5 / D) * grads.pop("sq_sum")[0]
    shared["ffn1_w_down"] = got_d[0]
    halves_g = reduce_halves(parts_g, got_g, "ffn1_gate")
    parts_u, wire_u = reduce_sum(got_u, views_u, "ffn1_up")
    views_i = [grad_view(dw_in)]
    ex = _Group([_SiblingShare(halves_g), _ChipScatter(wire_u), _SiblingScatter(views_i)])
    dw_out, got = _tn_matmul(cat, dmixed, "dw_out", exchange=ex)
    got_g, got_u, got_i = ex.split(got)
    shared["ffn1_w_gate"] = got_g[0]
    halves_u = reduce_halves(parts_u, got_u, "ffn1_up")
    parts_i, wire_i = reduce_sum(got_i, views_i, "w_in")

    views_t = [grad_view(dw_out)]
    ex = _Group([_SiblingScatter(views_t), _SiblingShare(halves_u)])
    got_t, got_u = ex.split(_exchange_call(ex, "tail_reduce_sibling"))
    shared["ffn1_w_up"] = got_u[0]
    parts_t, wire_t = reduce_sum(got_t, views_t, "tail")
    got = _exchange_call(_ChipScatter(wire_i + wire_t), "tail_reduce_chips")
    got = _exchange_call(_SiblingShare(reduce_halves(parts_i + parts_t, got, "tail")), "tail_reduce_share")
    shared["w_in"], shared["w_out"] = got[0], got[1]
    delta, new_m, new_v = {}, {}, {}
    for n in BIG:
        g_rows = shared[n].reshape(-1, shared[n].shape[-1])
        outs = _adamw(rows_of(n, p[n]), g_rows, rows_of(n, m[n]), rows_of(n, v[n]), "adamw_" + n)
        grads[n], delta[n], new_m[n], new_v[n] = (rows_back(n, a) for a in (g_rows, *outs))
    delta["meta_tokens"], new_m["meta_tokens"], new_v["meta_tokens"] = _adamw(p["meta_tokens"], grads["meta_tokens"], m["meta_tokens"], v["meta_tokens"], "adamw_meta_tokens")

    def as_2d(n, a):
        a = a.reshape(p[n].shape)[0]
        if n in ("ssm_b_re", "ssm_b_im"):
            a = jnp.swapaxes(a, 1, 2)
        return a.reshape(-1, a.shape[-1])

    def from_2d(n, a):
        if n in ("ssm_b_re", "ssm_b_im"):
            g_, n_, h_ = p[n].shape[1:]
            return jnp.swapaxes(a.reshape(g_, h_, n_), 1, 2)[None]
        return a.reshape(p[n].shape)

    outs = _adamw_many(*[[as_2d(n, t[n]) for n in SMALL] for t in (p, grads, m, v)], "adamw_small")
    for out, arrays in zip((delta, new_m, new_v), outs):
        out.update({n: from_2d(n, a) for n, a in zip(SMALL, arrays)})

    return (loss, grad_x, *[grads[n] for n in ORDER], *[delta[n] for n in ORDER], *[new_m[n] for n in ORDER], *[new_v[n] for n in ORDER])


def kernel(x, meta_tokens, ffn1_pre_norm, ffn1_post_norm, ffn1_w_gate, ffn1_w_up, ffn1_w_down, mix_pre_norm, mix_post_norm, w_in, ssm_lambda_re, ssm_lambda_im, ssm_log_dt, ssm_b_re, ssm_b_im, ssm_c_re, ssm_c_im, ssm_d, ssm_w_glu, pool_w, pool_scale, ssm_out_norm, pool_out_norm, w_out, ffn2_pre_norm, ffn2_post_norm, ffn2_w_gate, ffn2_w_up, ffn2_w_down, loss_target, m_meta_tokens, m_ffn1_pre_norm, m_ffn1_post_norm, m_ffn1_w_gate, m_ffn1_w_up, m_ffn1_w_down, m_mix_pre_norm, m_mix_post_norm, m_w_in, m_ssm_lambda_re, m_ssm_lambda_im, m_ssm_log_dt, m_ssm_b_re, m_ssm_b_im, m_ssm_c_re, m_ssm_c_im, m_ssm_d, m_ssm_w_glu, m_pool_w, m_pool_scale, m_ssm_out_norm, m_pool_out_norm, m_w_out, m_ffn2_pre_norm, m_ffn2_post_norm, m_ffn2_w_gate, m_ffn2_w_up, m_ffn2_w_down, v_meta_tokens, v_ffn1_pre_norm, v_ffn1_post_norm, v_ffn1_w_gate, v_ffn1_w_up, v_ffn1_w_down, v_mix_pre_norm, v_mix_post_norm, v_w_in, v_ssm_lambda_re, v_ssm_lambda_im, v_ssm_log_dt, v_ssm_b_re, v_ssm_b_im, v_ssm_c_re, v_ssm_c_im, v_ssm_d, v_ssm_w_glu, v_pool_w, v_pool_scale, v_ssm_out_norm, v_pool_out_norm, v_w_out, v_ffn2_pre_norm, v_ffn2_post_norm, v_ffn2_w_gate, v_ffn2_w_up, v_ffn2_w_down):
    args = locals()
    p = {n: args[n] for n in ORDER}
    m = {n: args["m_" + n] for n in ORDER}
    v = {n: args["v_" + n] for n in ORDER}
    return _step(p, x, loss_target, m, v)
```

```python
import math

import jax
import jax.numpy as jnp
from jax import lax
from jax.experimental import pallas as pl
from jax.experimental.pallas import tpu as pltpu

F32 = jnp.float32
MXU_DTYPE = jnp.bfloat16
WIRE_DTYPE = jnp.bfloat16

RMS_EPS = 1e-6
N_META = 16
POOL_WINDOWS = (2, 4, 8, 16)
POOL_HALO = 16
ADAM_LR, ADAM_B1, ADAM_B2, ADAM_EPS, ADAM_WD, ADAM_STEP = 0.001, 0.9, 0.999, 1e-08, 0.01, 10

LANES = 128
SUBLANES = 8
VMEM_LIMIT = 60 * 1024 * 1024
FFN_TILE = 432
FFN_CHUNK = 1024
TN_TILE = 1024
MIX_TILE = 216
MIX_SUBTILES = 2
SLAB_GROUP = 8
MESH = pl.DeviceIdType.MESH
ANY = pl.BlockSpec(memory_space=pl.ANY)


def _mm(a, b):
    return jnp.dot(a.astype(MXU_DTYPE), b.astype(MXU_DTYPE), preferred_element_type=F32)


def _mm_nt(a, b):
    return lax.dot_general(a.astype(MXU_DTYPE), b.astype(MXU_DTYPE), (((1,), (1,)), ((), ())), preferred_element_type=F32)


def _mm_tn(a, b):
    return lax.dot_general(a.astype(MXU_DTYPE), b.astype(MXU_DTYPE), (((0,), (0,)), ((), ())), preferred_element_type=F32)


def _rms_stat(x):
    return lax.rsqrt(jnp.mean(x * x, axis=-1, keepdims=True) + RMS_EPS)


def _rms_bwd(x, g, dy):
    r = _rms_stat(x)
    xh = x * r
    dg = jnp.sum(dy * xh, axis=0, keepdims=True)
    dxh = dy * g
    dx = r * (dxh - xh * jnp.mean(dxh * xh, axis=-1, keepdims=True))
    return dx, dg


def _sigmoid(x):
    return 1.0 / (1.0 + jnp.exp(-x))


GELU_C = math.sqrt(2.0 / math.pi)
GELU_K = 0.044715


def _gelu(y):
    return 0.5 * y * (1.0 + jnp.tanh(GELU_C * (y + GELU_K * y * y * y)))


def _gelu_grad(y):
    th = jnp.tanh(GELU_C * (y + GELU_K * y * y * y))
    return 0.5 * (1.0 + th) + 0.5 * y * (1.0 - th * th) * GELU_C * (1.0 + 3.0 * GELU_K * y * y)


def _row_spec(tile, cols, rev_n=None):
    if rev_n is None:
        return pl.BlockSpec((tile, cols), lambda i: (i, 0))
    return pl.BlockSpec((tile, cols), lambda i: (rev_n - 1 - i, 0))


def _full_spec(shape, single=False):
    zeros = (0,) * len(shape)
    if single:
        return pl.BlockSpec(shape, lambda *_: zeros, pipeline_mode=pl.Buffered(1))
    return pl.BlockSpec(shape, lambda *_: zeros)


def _acc(ref, val, first):
    @pl.when(first)
    def _():
        ref[...] = val

    @pl.when(jnp.logical_not(first))
    def _():
        ref[...] += val


def _place():
    x, y, c = lax.axis_index("x"), lax.axis_index("y"), lax.axis_index("c")
    others = [(1 - x, y), (x, 1 - y), (1 - x, 1 - y)]
    return x, y, c, others


class _Exchange:
    def __init__(self, ins, out_shapes, aliases, n_sems):
        self.ins, self.out_shapes, self.aliases, self.n_sems = list(ins), list(out_shapes), dict(aliases), n_sems

    def mid_steps(self):
        return []

    def mid(self, ins, outs, send_sems, recv_sems, step=None):
        pass


class _SiblingScatter(_Exchange):
    def __init__(self, views):
        super().__init__(views, [jax.ShapeDtypeStruct((4,) + v.shape[2:], v.dtype) for v in views], {}, 4 * len(views))

    def _copies(self, ins, outs, send_sems, recv_sems):
        x, y, c, _ = _place()
        return [
            pltpu.make_async_remote_copy(src_ref=ins[a].at[k, 1 - c], dst_ref=outs[a].at[k], send_sem=send_sems.at[4 * a + k], recv_sem=recv_sems.at[4 * a + k], device_id=(x, y, 1 - c), device_id_type=MESH)
            for a in range(len(ins))
            for k in range(4)
        ]

    def start(self, *refs):
        for cp in self._copies(*refs):
            cp.start()

    def finish(self, *refs):
        cps = self._copies(*refs)
        for cp in cps:
            cp.wait_recv()
        for cp in cps:
            cp.wait_send()


class _ChipScatter(_Exchange):
    def __init__(self, parts):
        super().__init__(parts, [jax.ShapeDtypeStruct((3,) + p.shape[1:], p.dtype) for p in parts], {}, 3 * len(parts))

    def _copies(self, ins, outs, send_sems, recv_sems):
        x, y, c, others = _place()
        return [
            pltpu.make_async_remote_copy(src_ref=ins[a].at[2 * chip[0] + chip[1]], dst_ref=outs[a].at[j], send_sem=send_sems.at[3 * a + j], recv_sem=recv_sems.at[3 * a + j], device_id=(*chip, c), device_id_type=MESH)
            for a in range(len(ins))
            for j, chip in enumerate(others)
        ]

    start = _SiblingScatter.start
    finish = _SiblingScatter.finish


class _SiblingShare(_Exchange):
    def __init__(self, bufs):
        super().__init__(bufs, [jax.ShapeDtypeStruct(b.shape, b.dtype) for b in bufs], {a: a for a in range(len(bufs))}, len(bufs))

    def _copy(self, outs, send_sems, recv_sems, a, half):
        x, y, c, _ = _place()
        mine = outs[a].at[c if half == "mine" else 1 - c]
        return pltpu.make_async_remote_copy(src_ref=mine, dst_ref=mine, send_sem=send_sems.at[a], recv_sem=recv_sems.at[a], device_id=(x, y, 1 - c), device_id_type=MESH)

    def start(self, ins, outs, send_sems, recv_sems):
        for a in range(len(outs)):
            self._copy(outs, send_sems, recv_sems, a, "mine").start()

    def finish(self, ins, outs, send_sems, recv_sems):
        for a in range(len(outs)):
            self._copy(outs, send_sems, recv_sems, a, "theirs").wait_recv()
        for a in range(len(outs)):
            self._copy(outs, send_sems, recv_sems, a, "mine").wait_send()


class _Gather(_Exchange):
    def __init__(self, bufs, steps=None):
        super().__init__(bufs, [jax.ShapeDtypeStruct(b.shape, b.dtype) for b in bufs], {a: a for a in range(len(bufs))}, 6 * len(bufs))
        self.steps = steps

    def mid_steps(self):
        return sorted(set(self.steps or []))

    def _copy(self, outs, send_sems, recv_sems, a, j, chip, half, to):
        blk = outs[a].at[2 * chip[0] + chip[1], half]
        return pltpu.make_async_remote_copy(src_ref=blk, dst_ref=blk, send_sem=send_sems.at[6 * a + j], recv_sem=recv_sems.at[6 * a + j], device_id=to, device_id_type=MESH)

    def start(self, ins, outs, send_sems, recv_sems):
        x, y, c, others = _place()
        for a in range(len(outs)):
            for j, chip in enumerate(others):
                self._copy(outs, send_sems, recv_sems, a, j, (x, y), c, (*chip, c)).start()

    def mid(self, ins, outs, send_sems, recv_sems, step=None):
        x, y, c, others = _place()
        for a in range(len(outs)):
            if step is not None and self.steps[a] != step:
                continue
            for j, chip in enumerate(others):
                self._copy(outs, send_sems, recv_sems, a, j, chip, c, (x, y, c)).wait_recv()
                self._copy(outs, send_sems, recv_sems, a, 3 + j, chip, c, (x, y, 1 - c)).start()

    def finish(self, ins, outs, send_sems, recv_sems):
        x, y, c, others = _place()
        for a in range(len(outs)):
            for j, chip in enumerate(others):
                self._copy(outs, send_sems, recv_sems, a, 3 + j, chip, 1 - c, (x, y, c)).wait_recv()
        for a in range(len(outs)):
            for j, chip in enumerate(others):
                self._copy(outs, send_sems, recv_sems, a, j, (x, y), c, (*chip, c)).wait_send()
                self._copy(outs, send_sems, recv_sems, a, 3 + j, chip, c, (x, y, 1 - c)).wait_send()


class _SemSlice:
    def __init__(self, sems, off):
        self.sems, self.off = sems, off

    @property
    def at(self):
        return self

    def __getitem__(self, i):
        return self.sems.at[self.off + i]


class _Group(_Exchange):
    def __init__(self, exchanges):
        ins, outs, aliases, n_sems, self.spans = [], [], {}, 0, []
        for ex in exchanges:
            self.spans.append((len(ins), len(outs), n_sems))
            aliases.update({len(ins) + i: len(outs) + o for i, o in ex.aliases.items()})
            ins, outs, n_sems = ins + ex.ins, outs + ex.out_shapes, n_sems + ex.n_sems
        super().__init__(ins, outs, aliases, n_sems)
        self.exchanges = exchanges

    def mid_steps(self):
        return sorted({s for ex in self.exchanges for s in ex.mid_steps()})

    def _each(self, method, ins, outs, send_sems, recv_sems, **kw):
        for ex, (i0, o0, s0) in zip(self.exchanges, self.spans):
            getattr(ex, method)(ins[i0 : i0 + len(ex.ins)], outs[o0 : o0 + len(ex.out_shapes)], _SemSlice(send_sems, s0), _SemSlice(recv_sems, s0), **kw)

    def start(self, *refs):
        self._each("start", *refs)

    def mid(self, *refs, step=None):
        self._each("mid", *refs, step=step)

    def finish(self, *refs):
        self._each("finish", *refs)

    def split(self, outs):
        return [outs[o0 : o0 + len(ex.out_shapes)] for ex, (_, o0, _) in zip(self.exchanges, self.spans)]


def _exchange_call(ex, name):
    n, m = len(ex.ins), len(ex.out_shapes)

    def body(*refs):
        parts = (refs[:n], refs[n : n + m], refs[n + m], refs[n + m + 1])
        ex.start(*parts)
        ex.mid(*parts)
        ex.finish(*parts)

    return pl.pallas_call(
        body,
        name=name,
        out_shape=ex.out_shapes,
        in_specs=[ANY] * n,
        out_specs=[ANY] * m,
        scratch_shapes=[pltpu.SemaphoreType.DMA((ex.n_sems,)), pltpu.SemaphoreType.DMA((ex.n_sems,))],
        input_output_aliases=ex.aliases,
    )(*ex.ins)


def _pallas(body, *, name, grid, in_specs, out_specs, out_shape, operands, scratch_shapes=(), exchange=None):
    params = pltpu.CompilerParams(dimension_semantics=("arbitrary",) * len(grid), vmem_limit_bytes=VMEM_LIMIT)
    if exchange is None:
        outs = pl.pallas_call(body, name=name, grid=grid, in_specs=in_specs, out_specs=out_specs, out_shape=out_shape, scratch_shapes=list(scratch_shapes), compiler_params=params)(*operands)
        return outs, []
    ex = exchange
    n_in, n_out, n_scr = len(in_specs), len(out_specs), len(scratch_shapes)
    x_in, x_out = len(ex.ins), len(ex.out_shapes)

    def hosted(*refs):
        ins, x_ins = refs[:n_in], refs[n_in : n_in + x_in]
        outs, x_outs = refs[n_in + x_in : n_in + x_in + n_out], refs[n_in + x_in + n_out : n_in + x_in + n_out + x_out]
        rest = refs[n_in + x_in + n_out + x_out :]
        parts = (x_ins, x_outs, rest[n_scr], rest[n_scr + 1])
        ids = [pl.program_id(d) for d in range(len(grid))]
        first = _all([i == 0 for i in ids])
        last = _all([i == g - 1 for i, g in zip(ids, grid)])

        @pl.when(first)
        def _():
            ex.start(*parts)

        body(*ins, *outs, *rest[:n_scr])

        for step in ex.mid_steps():

            @pl.when(ids[0] == step)
            def _(step=step):
                ex.mid(*parts, step=step)

        @pl.when(last)
        def _():
            ex.finish(*parts)

    outs = pl.pallas_call(
        hosted,
        name=name,
        grid=grid,
        in_specs=list(in_specs) + [ANY] * x_in,
        out_specs=list(out_specs) + [ANY] * x_out,
        out_shape=list(out_shape) + ex.out_shapes,
        scratch_shapes=list(scratch_shapes) + [pltpu.SemaphoreType.DMA((ex.n_sems,)), pltpu.SemaphoreType.DMA((ex.n_sems,))],
        input_output_aliases={n_in + i: n_out + o for i, o in ex.aliases.items()},
        compiler_params=params,
    )(*operands, *ex.ins)
    return outs[:n_out], outs[n_out:]


def _all(conds):
    out = conds[0]
    for c in conds[1:]:
        out = jnp.logical_and(out, c)
    return out


def _row_tile(rows):
    if rows <= 512:
        return rows
    for t in (512, 352, 256, 176, 128, 112, 64, 32, 16, 8):
        if rows % t == 0:
            return t
    return rows


def _add_own_half(view, got, place, name, wire=WIRE_DTYPE):
    _, _, r, c = view.shape
    tr = _row_tile(r)

    def body(place_ref, v_ref, g_ref, o_ref, w_ref):
        s = v_ref[...] + g_ref[...]
        w_ref[...] = s.astype(w_ref.dtype)

        @pl.when(pl.program_id(1) == place_ref[0])
        def _():
            o_ref[...] = s

    blk = pl.BlockSpec((None, tr, c), lambda i, k, pr: (k, i, 0))
    return pl.pallas_call(
        body,
        name=name,
        out_shape=[jax.ShapeDtypeStruct((r, c), F32), jax.ShapeDtypeStruct((4, r, c), wire)],
        grid_spec=pltpu.PrefetchScalarGridSpec(
            num_scalar_prefetch=1,
            grid=(r // tr, 4),
            in_specs=[pl.BlockSpec((None, None, tr, c), lambda i, k, pr: (k, pr[1], i, 0)), blk],
            out_specs=[pl.BlockSpec((tr, c), lambda i, k, pr: (i, 0)), blk],
        ),
        compiler_params=pltpu.CompilerParams(dimension_semantics=("arbitrary", "arbitrary"), vmem_limit_bytes=VMEM_LIMIT),
    )(place, view, got)


def _add_chips(part, got, place, name):
    r, c = part.shape
    tr = _row_tile(r)

    def body(place_ref, p_ref, g_ref, o_ref):
        o_ref[...] = ((p_ref[...] + g_ref[0].astype(F32)) + g_ref[1].astype(F32)) + g_ref[2].astype(F32)

    return pl.pallas_call(
        body,
        name=name,
        out_shape=jax.ShapeDtypeStruct((2, r, c), F32),
        grid_spec=pltpu.PrefetchScalarGridSpec(
            num_scalar_prefetch=1,
            grid=(r // tr,),
            in_specs=[pl.BlockSpec((tr, c), lambda i, pr: (i, 0)), pl.BlockSpec((3, tr, c), lambda i, pr: (0, i, 0))],
            out_specs=pl.BlockSpec((None, tr, c), lambda i, pr: (pr[1], i, 0)),
        ),
        compiler_params=pltpu.CompilerParams(dimension_semantics=("arbitrary",), vmem_limit_bytes=VMEM_LIMIT),
    )(place, part, got)


def _adamw_update(w_ref, g_ref, m_ref, v_ref, d_ref, nm_ref, nv_ref):
    g = g_ref[...]
    nm = ADAM_B1 * m_ref[...] + (1.0 - ADAM_B1) * g
    nv = ADAM_B2 * v_ref[...] + (1.0 - ADAM_B2) * (g * g)
    m_hat = nm / (1.0 - ADAM_B1**ADAM_STEP)
    v_hat = nv / (1.0 - ADAM_B2**ADAM_STEP)
    d_ref[...] = -ADAM_LR * (m_hat / (jnp.sqrt(v_hat) + ADAM_EPS) + ADAM_WD * w_ref[...])
    nm_ref[...] = nm
    nv_ref[...] = nv


def _adamw(w, g, m, v, name):
    r, c = w.shape
    tr = _row_tile(r)
    spec = pl.BlockSpec((tr, c), lambda i: (i, 0))
    outs, _ = _pallas(_adamw_update, name=name, grid=(r // tr,), in_specs=[spec] * 4, out_specs=[spec] * 3, out_shape=[jax.ShapeDtypeStruct((r, c), F32)] * 3, operands=(w, g, m, v))
    return outs


def _adamw_many(ws, gs, ms, vs, name):
    n = len(ws)

    def body(*refs):
        for k in range(n):
            _adamw_update(*(refs[j * n + k] for j in range(7)))

    outs = pl.pallas_call(
        body,
        name=name,
        out_shape=[jax.ShapeDtypeStruct(w.shape, F32) for w in ws] * 3,
        in_specs=[pl.BlockSpec(memory_space=pltpu.VMEM)] * (4 * n),
        out_specs=[pl.BlockSpec(memory_space=pltpu.VMEM)] * (3 * n),
    )(*ws, *gs, *ms, *vs)
    return outs[:n], outs[n : 2 * n], outs[2 * n :]


def _load_weights(pairs, sems):
    @pl.when(pl.program_id(0) == 0)
    def _():
        cps = [pltpu.make_async_copy(src, dst, sems.at[k]) for k, (src, dst) in enumerate(pairs)]
        for cp in cps:
            cp.start()
        for cp in cps:
            cp.wait()


def _ffn_chunks(F):
    bounds = list(range(0, F, FFN_CHUNK)) + [F]
    return list(zip(bounds[:-1], bounds[1:]))


def _shifted_specs(tm, cols):
    per = tm // N_META
    return [_row_spec(tm, cols), pl.BlockSpec((N_META, cols), lambda i: (jnp.maximum(i * per - 1, 0), 0))]


def _shifted_tile(cur_ref, before_ref, tm):
    return jnp.concatenate([before_ref[...], cur_ref[0 : tm - N_META, :]], axis=0)


def _tokens_tile(cur_ref, before_ref, meta_ref, tm, tile_0):
    first = jnp.where(tile_0, meta_ref[...], before_ref[...])
    return jnp.concatenate([first, cur_ref[0 : tm - N_META, :]], axis=0)


def _ffn_fwd_loss(h, target, g_pre, g_post, wg, wu, wd, name):
    T, D = h.shape
    F = wg.shape[0]
    tm = FFN_TILE

    def body(h_ref, t_ref, tb_ref, gpre_ref, gpost_ref, wg_hbm, wu_hbm, wd_hbm, dy_ref, sq_ref, f_ref, ga_ref, si_ref, s_ref, n_ref, wg_v, wu_v, wd_v, sems):
        i = pl.program_id(0)
        _load_weights([(wg_hbm, wg_v), (wu_hbm, wu_v), (wd_hbm, wd_v)], sems)
        hh = h_ref[...]
        n = (hh * _rms_stat(hh) * gpre_ref[...]).astype(MXU_DTYPE)
        n_ref[...] = n.astype(n_ref.dtype)
        f = jnp.zeros((tm, D), F32)
        for lo, hi in _ffn_chunks(F):
            a = _mm_nt(n, wg_v[lo:hi, :])
            b = _mm_nt(n, wu_v[lo:hi, :])
            sg = _sigmoid(a)
            si = a * sg
            s = (si * b).astype(MXU_DTYPE)
            ga_ref[:, lo:hi] = (b * (sg * (1.0 + a * (1.0 - sg)))).astype(ga_ref.dtype)
            si_ref[:, lo:hi] = si.astype(si_ref.dtype)
            s_ref[:, lo:hi] = s.astype(s_ref.dtype)
            f = f + _mm(s, wd_v[lo:hi, :])
        f_ref[...] = f
        y = hh + 0.5 * (f * _rms_stat(f) * gpost_ref[...])
        rows = i * tm + lax.broadcasted_iota(jnp.int32, (tm, D), 0)
        err = jnp.where(rows >= N_META, y - _shifted_tile(t_ref, tb_ref, tm), 0.0)
        dy_ref[...] = err * (1.0 / D)
        _acc(sq_ref, jnp.sum(err * err, axis=0, keepdims=True), i == 0)

    tok = jax.ShapeDtypeStruct((T, D), F32)
    act = jax.ShapeDtypeStruct((T, F), MXU_DTYPE)
    outs, _ = _pallas(
        body,
        name=name,
        grid=(T // tm,),
        out_shape=[tok, jax.ShapeDtypeStruct((1, D), F32), tok, act, act, act, jax.ShapeDtypeStruct((T, D), MXU_DTYPE)],
        in_specs=[_row_spec(tm, D)] + _shifted_specs(tm, D) + [_full_spec((1, D)), _full_spec((1, D)), ANY, ANY, ANY],
        out_specs=[_row_spec(tm, D), _full_spec((1, D)), _row_spec(tm, D), _row_spec(tm, F), _row_spec(tm, F), _row_spec(tm, F), _row_spec(tm, D)],
        scratch_shapes=[pltpu.VMEM(wg.shape, wg.dtype), pltpu.VMEM(wu.shape, wu.dtype), pltpu.VMEM(wd.shape, wd.dtype), pltpu.SemaphoreType.DMA((3,))],
        operands=(h, target, target, g_pre, g_post, wg, wu, wd),
    )
    return outs


def _ffn_gate_up(x, meta, g_pre, wg, wu, name, exchange=None):
    D = x.shape[1]
    T = x.shape[0] + N_META
    F = wg.shape[0]
    tm = FFN_TILE

    def body(x_ref, xb_ref, meta_ref, gpre_ref, wg_hbm, wu_hbm, ga_ref, si_ref, s_ref, n_ref, wg_v, wu_v, sems):
        _load_weights([(wg_hbm, wg_v), (wu_hbm, wu_v)], sems)
        hh = _tokens_tile(x_ref, xb_ref, meta_ref, tm, pl.program_id(0) == 0)
        n = (hh * _rms_stat(hh) * gpre_ref[...]).astype(MXU_DTYPE)
        n_ref[...] = n.astype(n_ref.dtype)
        for lo, hi in _ffn_chunks(F):
            a = _mm_nt(n, wg_v[lo:hi, :])
            b = _mm_nt(n, wu_v[lo:hi, :])
            sg = _sigmoid(a)
            si = a * sg
            ga_ref[:, lo:hi] = (b * (sg * (1.0 + a * (1.0 - sg)))).astype(ga_ref.dtype)
            si_ref[:, lo:hi] = si.astype(si_ref.dtype)
            s_ref[:, lo:hi] = (si * b).astype(s_ref.dtype)

    act = jax.ShapeDtypeStruct((T, F), MXU_DTYPE)
    return _pallas(
        body,
        name=name,
        grid=(T // tm,),
        out_shape=[act, act, act, jax.ShapeDtypeStruct((T, D), MXU_DTYPE)],
        in_specs=_shifted_specs(tm, D) + [_full_spec((N_META, D)), _full_spec((1, D)), ANY, ANY],
        out_specs=[_row_spec(tm, F), _row_spec(tm, F), _row_spec(tm, F), _row_spec(tm, D)],
        scratch_shapes=[pltpu.VMEM(wg.shape, wg.dtype), pltpu.VMEM(wu.shape, wu.dtype), pltpu.SemaphoreType.DMA((2,))],
        operands=(x, x, meta, g_pre, wg, wu),
        exchange=exchange,
    )


def _ffn_down(x, meta, s, g_post, wd, name, exchange=None):
    D = x.shape[1]
    T = x.shape[0] + N_META
    F = wd.shape[0]
    tm = _token_tile(T)

    def body(x_ref, xb_ref, meta_ref, s_ref, gpost_ref, wd_hbm, hout_ref, f_ref, wd_v, sems):
        _load_weights([(wd_hbm, wd_v)], sems)
        f = _mm(s_ref[...], wd_v[...])
        f_ref[...] = f
        hout_ref[...] = _tokens_tile(x_ref, xb_ref, meta_ref, tm, pl.program_id(0) == 0) + 0.5 * (f * _rms_stat(f) * gpost_ref[...])

    tok = jax.ShapeDtypeStruct((T, D), F32)
    return _pallas(
        body,
        name=name,
        grid=(T // tm,),
        out_shape=[tok, tok],
        in_specs=_shifted_specs(tm, D) + [_full_spec((N_META, D)), _row_spec(tm, F), _full_spec((1, D)), ANY],
        out_specs=[_row_spec(tm, D), _row_spec(tm, D)],
        scratch_shapes=[pltpu.VMEM(wd.shape, wd.dtype), pltpu.SemaphoreType.DMA((1,))],
        operands=(x, x, meta, s, g_post, wd),
        exchange=exchange,
    )


def _ffn_bwd(dh, f, ga, si, h, g_post, g_pre, wg, wu, wd, name):
    T, D = dh.shape
    F = wd.shape[0]
    tm = FFN_TILE

    def body(dh_ref, f_ref, ga_ref, si_ref, h_ref, gpost_ref, gpre_ref, wg_hbm, wu_hbm, wd_hbm, da_ref, db_ref, df_ref, dhin_ref, dgpost_ref, dgpre_ref, wg_v, wu_v, wd_v, sems):
        first = pl.program_id(0) == 0
        _load_weights([(wg_hbm, wg_v), (wu_hbm, wu_v), (wd_hbm, wd_v)], sems)
        dh = dh_ref[...]
        df, dg = _rms_bwd(f_ref[...], gpost_ref[...], 0.5 * dh)
        _acc(dgpost_ref, dg, first)
        dfb = df.astype(MXU_DTYPE)
        df_ref[...] = dfb.astype(df_ref.dtype)
        dn = jnp.zeros((tm, D), F32)
        for lo, hi in _ffn_chunks(F):
            ds = _mm_nt(dfb, wd_v[lo:hi, :])
            da = (ds * ga_ref[:, lo:hi].astype(F32)).astype(MXU_DTYPE)
            db = (ds * si_ref[:, lo:hi].astype(F32)).astype(MXU_DTYPE)
            da_ref[:, lo:hi] = da.astype(da_ref.dtype)
            db_ref[:, lo:hi] = db.astype(db_ref.dtype)
            dn = dn + _mm(da, wg_v[lo:hi, :]) + _mm(db, wu_v[lo:hi, :])
        dx, dg = _rms_bwd(h_ref[...], gpre_ref[...], dn)
        _acc(dgpre_ref, dg, first)
        dhin_ref[...] = dh + dx

    act = jax.ShapeDtypeStruct((T, F), MXU_DTYPE)
    vec = jax.ShapeDtypeStruct((1, D), F32)
    outs, _ = _pallas(
        body,
        name=name,
        grid=(T // tm,),
        out_shape=[act, act, jax.ShapeDtypeStruct((T, D), MXU_DTYPE), jax.ShapeDtypeStruct((T, D), F32), vec, vec],
        in_specs=[_row_spec(tm, D), _row_spec(tm, D), _row_spec(tm, F), _row_spec(tm, F), _row_spec(tm, D), _full_spec((1, D)), _full_spec((1, D)), ANY, ANY, ANY],
        out_specs=[_row_spec(tm, F), _row_spec(tm, F), _row_spec(tm, D), _row_spec(tm, D), _full_spec((1, D)), _full_spec((1, D))],
        scratch_shapes=[pltpu.VMEM(wg.shape, wg.dtype), pltpu.VMEM(wu.shape, wu.dtype), pltpu.VMEM(wd.shape, wd.dtype), pltpu.SemaphoreType.DMA((3,))],
        operands=(dh, f, ga, si, h, g_post, g_pre, wg, wu, wd),
    )
    return outs


def _ffn_bwd_first(dh, f, ga, si, x, meta, g_post, g_pre, wg, wu, wd, name):
    D = x.shape[1]
    T = x.shape[0] + N_META
    F = wd.shape[0]
    tm = FFN_TILE
    nt = T // tm
    per = tm // N_META
    tile = lambda i: jnp.minimum(i, nt - 1)

    def body(dh_ref, f_ref, ga_ref, si_ref, x_ref, xb_ref, meta_ref, gpost_ref, gpre_ref, wg_hbm, wu_hbm, wd_hbm,
             da_ref, db_ref, df_ref, dx_ref, dmeta_ref, dgpost_ref, dgpre_ref, wg_v, wu_v, wd_v, sems, held):
        i = pl.program_id(0)
        _load_weights([(wg_hbm, wg_v), (wu_hbm, wu_v), (wd_hbm, wd_v)], sems)

        @pl.when(i < nt)
        def _():
            dh = dh_ref[...]
            df, dg = _rms_bwd(f_ref[...], gpost_ref[...], 0.5 * dh)
            _acc(dgpost_ref, dg, i == 0)
            dfb = df.astype(MXU_DTYPE)
            df_ref[...] = dfb.astype(df_ref.dtype)
            dn = jnp.zeros((tm, D), F32)
            for lo, hi in _ffn_chunks(F):
                ds = _mm_nt(dfb, wd_v[lo:hi, :])
                da = (ds * ga_ref[:, lo:hi].astype(F32)).astype(MXU_DTYPE)
                db = (ds * si_ref[:, lo:hi].astype(F32)).astype(MXU_DTYPE)
                da_ref[:, lo:hi] = da.astype(da_ref.dtype)
                db_ref[:, lo:hi] = db.astype(db_ref.dtype)
                dn = dn + _mm(da, wg_v[lo:hi, :]) + _mm(db, wu_v[lo:hi, :])
            dx, dg = _rms_bwd(_tokens_tile(x_ref, xb_ref, meta_ref, tm, i == 0), gpre_ref[...], dn)
            _acc(dgpre_ref, dg, i == 0)
            dh_in = dh + dx

            @pl.when(i == 0)
            def _():
                dmeta_ref[...] = dh_in[0:N_META, :]

            @pl.when(i > 0)
            def _():
                dx_ref[...] = jnp.concatenate([held[...], dh_in[0:N_META, :]], axis=0)

            held[...] = dh_in[N_META:, :]

        @pl.when(i == nt)
        def _():
            dx_ref[0 : tm - N_META, :] = held[...]

    rows = lambda cols: pl.BlockSpec((tm, cols), lambda i: (tile(i), 0))
    act = jax.ShapeDtypeStruct((T, F), MXU_DTYPE)
    vec = jax.ShapeDtypeStruct((1, D), F32)
    outs, _ = _pallas(
        body,
        name=name,
        grid=(nt + 1,),
        out_shape=[act, act, jax.ShapeDtypeStruct((T, D), MXU_DTYPE), jax.ShapeDtypeStruct((T - N_META, D), F32), jax.ShapeDtypeStruct((N_META, D), F32), vec, vec],
        in_specs=[rows(D), rows(D), rows(F), rows(F), rows(D), pl.BlockSpec((N_META, D), lambda i: (jnp.maximum(tile(i) * per - 1, 0), 0)), _full_spec((N_META, D)),
                  _full_spec((1, D)), _full_spec((1, D)), ANY, ANY, ANY],
        out_specs=[rows(F), rows(F), rows(D), pl.BlockSpec((tm, D), lambda i: (jnp.maximum(i - 1, 0), 0)), _full_spec((N_META, D)), _full_spec((1, D)), _full_spec((1, D))],
        scratch_shapes=[pltpu.VMEM(wg.shape, wg.dtype), pltpu.VMEM(wu.shape, wu.dtype), pltpu.VMEM(wd.shape, wd.dtype), pltpu.SemaphoreType.DMA((3,)), pltpu.VMEM((tm - N_META, D), F32)],
        operands=(dh, f, ga, si, x, x, meta, g_post, g_pre, wg, wu, wd),
    )
    return outs


def _token_tile(T):
    for t in (912, 864, 432):
        if T % t == 0:
            return t
    raise ValueError(f"no token tile for {T} rows")


def _tn_matmul(xm, ym, name, exchange=None):
    T, M = xm.shape
    N = ym.shape[1]
    if (T - N_META) % TN_TILE:
        tk = _token_tile(T)

        def body(x_ref, y_ref, o_ref):
            _acc(o_ref, _mm_tn(x_ref[...], y_ref[...]), pl.program_id(0) == 0)

        grid, operands = (T // tk,), (xm, ym)
        in_specs = [pl.BlockSpec((tk, M), lambda k: (k, 0)), pl.BlockSpec((tk, N), lambda k: (k, 0))]
    else:
        tk = TN_TILE
        mc = next(c for c in (1408, 1024, M) if M % c == 0)

        def body(x_ref, y_ref, xh_ref, yh_ref, o_ref):
            y = y_ref[...]
            for lo in range(0, M, mc):
                prod = _mm_tn(x_ref[:, lo : lo + mc], y)

                @pl.when(pl.program_id(0) == 0)
                def _(lo=lo, prod=prod):
                    o_ref[lo : lo + mc, :] = prod + _mm_tn(xh_ref[:, lo : lo + mc], yh_ref[...])

                @pl.when(pl.program_id(0) > 0)
                def _(lo=lo, prod=prod):
                    o_ref[lo : lo + mc, :] += prod

        grid, operands = ((T - N_META) // tk,), (xm, ym, xm, ym)
        start = lambda k: (pl.multiple_of(N_META + k * tk, N_META), 0)
        in_specs = [pl.BlockSpec((pl.Element(tk), pl.Element(M)), start), pl.BlockSpec((pl.Element(tk), pl.Element(N)), start),
                    pl.BlockSpec((N_META, M), lambda k: (0, 0)), pl.BlockSpec((N_META, N), lambda k: (0, 0))]

    (out,), x_outs = _pallas(
        body,
        name=name,
        grid=grid,
        out_shape=[jax.ShapeDtypeStruct((M, N), F32)],
        in_specs=in_specs,
        out_specs=[_full_spec((M, N))],
        operands=operands,
        exchange=exchange,
    )
    return out, x_outs


TAB_A, TAB_AS1, TAB_AS2, TAB_AS4, TAB_JF, TAB_JB = 0, 2, 4, 6, 8, 10


def _scan_inplace(zr, zi, tabs, pows, car_r, car_i, seg, reverse, base=0):
    n_slabs = zr.shape[0]
    sgn = -1.0 if reverse else 1.0
    row = lax.broadcasted_iota(jnp.int32, (SUBLANES, LANES), 0)

    def cmul(pr, pi, xr, xi):
        return pr * xr - pi * xi, pr * xi + pi * xr

    for k0 in range(0, n_slabs, SLAB_GROUP):
        slabs = range(k0, min(k0 + SLAB_GROUP, n_slabs))
        ar = [tabs[TAB_A, k] for k in slabs]
        ai = [sgn * tabs[TAB_A + 1, k] for k in slabs]

        def first_pass(t, carry):
            r = (seg - 1 - t) if reverse else t
            out = []
            for q, k in enumerate(slabs):
                xr, xi = carry[2 * q], carry[2 * q + 1]
                pr, pi = cmul(ar[q], ai[q], xr, xi)
                nr = pr + zr[k, pl.ds(base + r, SUBLANES, stride=seg), :]
                ni = pi + zi[k, pl.ds(base + r, SUBLANES, stride=seg), :]
                zr[k, pl.ds(base + r, SUBLANES, stride=seg), :] = nr
                zi[k, pl.ds(base + r, SUBLANES, stride=seg), :] = ni
                out += [nr, ni]
            return tuple(out)

        ends = lax.fori_loop(0, seg, first_pass, tuple(jnp.zeros((SUBLANES, LANES), F32) for _ in range(2 * len(slabs))))

        incoming = []
        for q, k in enumerate(slabs):
            fr, fi = ends[2 * q], ends[2 * q + 1]
            for d, tab in ((1, TAB_AS1), (2, TAB_AS2), (4, TAB_AS4)):
                shift, keep = (SUBLANES - d, row < SUBLANES - d) if reverse else (d, row >= d)
                sr = jnp.where(keep, pltpu.roll(fr, shift, 0), 0.0)
                si = jnp.where(keep, pltpu.roll(fi, shift, 0), 0.0)
                pr, pi = cmul(tabs[tab, k], sgn * tabs[tab + 1, k], sr, si)
                fr, fi = fr + pr, fi + pi
            cr, ci = car_r[k], car_i[k]
            jtab = TAB_JB if reverse else TAB_JF
            pr, pi = cmul(tabs[jtab, k], sgn * tabs[jtab + 1, k], cr, ci)
            er, ei = fr + pr, fi + pi
            if reverse:
                inr = jnp.where(row < SUBLANES - 1, pltpu.roll(er, SUBLANES - 1, 0), cr)
                ini = jnp.where(row < SUBLANES - 1, pltpu.roll(ei, SUBLANES - 1, 0), ci)
                car_r[k] = jnp.broadcast_to(er[0:1, :], (SUBLANES, LANES))
                car_i[k] = jnp.broadcast_to(ei[0:1, :], (SUBLANES, LANES))
            else:
                inr = jnp.where(row >= 1, pltpu.roll(er, 1, 0), cr)
                ini = jnp.where(row >= 1, pltpu.roll(ei, 1, 0), ci)
                car_r[k] = jnp.broadcast_to(er[SUBLANES - 1 : SUBLANES, :], (SUBLANES, LANES))
                car_i[k] = jnp.broadcast_to(ei[SUBLANES - 1 : SUBLANES, :], (SUBLANES, LANES))
            incoming += [inr, ini]

        def second_pass(r, _):
            p = (seg - 1 - r) if reverse else r
            for q, k in enumerate(slabs):
                pr, pi = cmul(pows[0, k, p], sgn * pows[1, k, p], incoming[2 * q], incoming[2 * q + 1])
                zr[k, pl.ds(base + r, SUBLANES, stride=seg), :] = zr[k, pl.ds(base + r, SUBLANES, stride=seg), :] + pr
                zi[k, pl.ds(base + r, SUBLANES, stride=seg), :] = zi[k, pl.ds(base + r, SUBLANES, stride=seg), :] + pi
            return 0

        lax.fori_loop(0, seg, second_pass, 0)


def _slabs_to_cols(ref, k0, n):
    return jnp.concatenate([ref[k0 + q] for q in range(n)], axis=1)


def _window_sum(ext, doublings, forward):
    rows = ext.shape[0]
    s = ext
    for k in range(doublings):
        s = s + pltpu.roll(s, (1 << k) if forward else rows - (1 << k), 0)
    return s


def _mix_fwd(h1, g_pre, g_so, g_po, g_post, dskip, pscale, win, wout, bbre, bbim, ccre, ccim, wgv, wgg, pw, tabs, pows, name, exchange=None):
    T, D = h1.shape
    W = D // 2
    tm = MIX_SUBTILES * MIX_TILE
    seg = MIX_TILE // SUBLANES
    n_slabs = tabs.shape[1]
    nch, cch, sch = bbre.shape
    spc = sch // LANES
    ngc, gch, _ = wgv.shape
    pg = W // len(POOL_WINDOWS)

    def body(h_ref, gpre_ref, gso_ref, gpo_ref, gpost_ref, dskip_ref, pscale_ref, win_ref, wout_ref, bbre_ref, bbim_ref, ccre_ref, ccim_ref, wgv_ref, wgg_ref, pw_ref, tabs_ref, pows_ref,
             proj_ref, xr_ref, xi_ref, y_ref, pooled_ref, mixed_ref, h2_ref, n2_ref, cat_ref, car_r, car_i, halo):
        i = pl.program_id(0)

        @pl.when(i == 0)
        def _():
            car_r[...] = jnp.zeros_like(car_r)
            car_i[...] = jnp.zeros_like(car_i)
            halo[...] = jnp.zeros_like(halo)

        hh = h_ref[...]
        n2 = (hh * _rms_stat(hh) * gpre_ref[...]).astype(MXU_DTYPE)
        n2_ref[...] = n2.astype(n2_ref.dtype)
        proj = _mm(n2, win_ref[...])
        proj_ref[...] = proj
        us, up = proj[:, :W], proj[:, W:]

        for c in range(nch):
            uc = us[:, c * cch : (c + 1) * cch].astype(MXU_DTYPE)
            bur, bui = _mm(uc, bbre_ref[c]), _mm(uc, bbim_ref[c])
            for q in range(spc):
                xr_ref[c * spc + q] = bur[:, q * LANES : (q + 1) * LANES]
                xi_ref[c * spc + q] = bui[:, q * LANES : (q + 1) * LANES]
        for sub in range(MIX_SUBTILES):
            _scan_inplace(xr_ref, xi_ref, tabs_ref, pows_ref, car_r, car_i, seg, reverse=False, base=sub * MIX_TILE)
        ys = []
        for c in range(nch):
            ys.append(_mm(_slabs_to_cols(xr_ref, c * spc, spc), ccre_ref[c]) - _mm(_slabs_to_cols(xi_ref, c * spc, spc), ccim_ref[c]))
        y = jnp.concatenate(ys, axis=1) + dskip_ref[...] * us
        y_ref[...] = y
        ge = _gelu(y).astype(MXU_DTYPE)
        zv = jnp.concatenate([_mm(ge[:, c * gch : (c + 1) * gch], wgv_ref[c]) for c in range(ngc)], axis=1)
        zg = jnp.concatenate([_mm(ge[:, c * gch : (c + 1) * gch], wgg_ref[c]) for c in range(ngc)], axis=1)
        out = zv * _sigmoid(zg)
        cat_s = out * _rms_stat(out) * gso_ref[...]

        ext = jnp.concatenate([halo[...], up], axis=0)
        halo[...] = up[tm - POOL_HALO :, :]
        t1 = (i * tm + 1 + lax.broadcasted_iota(jnp.int32, (tm, pg), 0)).astype(F32)
        pooled, pms = [], []
        for g, w in enumerate(POOL_WINDOWS):
            col = ext[:, g * pg : (g + 1) * pg]
            win_sum = _window_sum(col, g + 1, True)[POOL_HALO:, :]
            pooled_g = win_sum / jnp.minimum(t1, float(w)) - up[:, g * pg : (g + 1) * pg]
            pooled.append(pooled_g)
            pms.append(_mm(pooled_g, pw_ref[g]))
        pooled_ref[...] = jnp.concatenate(pooled, axis=1)
        yp = jnp.concatenate(pms, axis=1) * pscale_ref[...]
        cat_p = yp * _rms_stat(yp) * gpo_ref[...]

        cat = jnp.concatenate([cat_s, cat_p], axis=1).astype(MXU_DTYPE)
        cat_ref[...] = cat.astype(cat_ref.dtype)
        mixed = _mm(cat, wout_ref[...])
        mixed_ref[...] = mixed
        h2_ref[...] = hh + mixed * _rms_stat(mixed) * gpost_ref[...]

    tok = lambda cols, dt=F32: jax.ShapeDtypeStruct((T, cols), dt)
    slab_spec = pl.BlockSpec((n_slabs, tm, LANES), lambda i: (0, i, 0))
    operands = (h1, g_pre, g_so, g_po, g_post, dskip, pscale, win, wout, bbre, bbim, ccre, ccim, wgv, wgg, pw, tabs, pows)
    return _pallas(
        body,
        name=name,
        grid=(T // tm,),
        out_shape=[tok(D), jax.ShapeDtypeStruct((n_slabs, T, LANES), F32), jax.ShapeDtypeStruct((n_slabs, T, LANES), F32), tok(W), tok(W), tok(D), tok(D), tok(D, MXU_DTYPE), tok(D, MXU_DTYPE)],
        in_specs=[_row_spec(tm, D)] + [_full_spec(o.shape, single=True) for o in operands[1:]],
        out_specs=[_row_spec(tm, D), slab_spec, slab_spec, _row_spec(tm, W), _row_spec(tm, W), _row_spec(tm, D), _row_spec(tm, D), _row_spec(tm, D), _row_spec(tm, D)],
        scratch_shapes=[pltpu.VMEM((n_slabs, SUBLANES, LANES), F32), pltpu.VMEM((n_slabs, SUBLANES, LANES), F32), pltpu.VMEM((POOL_HALO, W), F32)],
        operands=operands,
        exchange=exchange,
    )


def _mix_bwd_heads(dh2, mixed, y, pooled, proj, g_so, g_po, g_post, pscale, wout, wgv, wgg, pw, name, exchange=None):
    T, D = dh2.shape
    W = D // 2
    tm = _token_tile(T)
    nch, cch, _ = wgv.shape
    ng, pg, _ = pw.shape

    def body(dh2_ref, mixed_ref, y_ref, pooled_ref, us_ref, gso_ref, gpo_ref, gpost_ref, pscale_ref, wout_ref, wgv_ref, wgg_ref, pw_ref,
             dy_ref, dpooled_ref, dmixed_ref, dgpost_ref, dgso_ref, dgpo_ref, dd_ref, dscale_ref, dwgv_ref, dwgg_ref, dpw_ref):
        first = pl.program_id(0) == 0
        dmixed, dgpost = _rms_bwd(mixed_ref[...], gpost_ref[...], dh2_ref[...])
        _acc(dgpost_ref, dgpost, first)
        dmb = dmixed.astype(MXU_DTYPE)
        dmixed_ref[...] = dmb.astype(dmixed_ref.dtype)
        dcat = _mm_nt(dmb, wout_ref[...])
        dcs, dcp = dcat[:, :W], dcat[:, W:]

        y = y_ref[...]
        ge = _gelu(y).astype(MXU_DTYPE)
        zv = jnp.concatenate([_mm(ge[:, c * cch : (c + 1) * cch], wgv_ref[c]) for c in range(nch)], axis=1)
        zg = jnp.concatenate([_mm(ge[:, c * cch : (c + 1) * cch], wgg_ref[c]) for c in range(nch)], axis=1)
        sg = _sigmoid(zg)
        dout, dgso = _rms_bwd(zv * sg, gso_ref[...], dcs)
        _acc(dgso_ref, dgso, first)
        dzv = (dout * sg).astype(MXU_DTYPE)
        dzg = (dout * zv * sg * (1.0 - sg)).astype(MXU_DTYPE)
        dges = []
        for c in range(nch):
            cs = slice(c * cch, (c + 1) * cch)
            dges.append(_mm_nt(dzv[:, cs], wgv_ref[c]) + _mm_nt(dzg[:, cs], wgg_ref[c]))
            _acc(dwgv_ref.at[c], _mm_tn(ge[:, cs], dzv[:, cs]), first)
            _acc(dwgg_ref.at[c], _mm_tn(ge[:, cs], dzg[:, cs]), first)
        dy = jnp.concatenate(dges, axis=1) * _gelu_grad(y)
        dy_ref[...] = dy
        _acc(dd_ref, jnp.sum(dy * us_ref[...], axis=0, keepdims=True), first)

        pooled_b = pooled_ref[...].astype(MXU_DTYPE)
        pm = jnp.concatenate([_mm(pooled_b[:, g * pg : (g + 1) * pg], pw_ref[g]) for g in range(ng)], axis=1)
        dyp, dgpo = _rms_bwd(pm * pscale_ref[...], gpo_ref[...], dcp)
        _acc(dgpo_ref, dgpo, first)
        _acc(dscale_ref, jnp.sum(dyp * pm, axis=0, keepdims=True), first)
        dpm = (dyp * pscale_ref[...]).astype(MXU_DTYPE)
        dps = []
        for g in range(ng):
            gs = slice(g * pg, (g + 1) * pg)
            dps.append(_mm_nt(dpm[:, gs], pw_ref[g]))
            _acc(dpw_ref.at[g], _mm_tn(pooled_b[:, gs], dpm[:, gs]), first)
        dpooled_ref[...] = jnp.concatenate(dps, axis=1)

    vec = lambda n: jax.ShapeDtypeStruct((1, n), F32)
    operands = (dh2, mixed, y, pooled, proj, g_so, g_po, g_post, pscale, wout, wgv, wgg, pw)
    return _pallas(
        body,
        name=name,
        grid=(T // tm,),
        out_shape=[jax.ShapeDtypeStruct((T, W), F32), jax.ShapeDtypeStruct((T, W), F32), jax.ShapeDtypeStruct((T, D), MXU_DTYPE), vec(D), vec(W), vec(W), vec(W), vec(W),
                   jax.ShapeDtypeStruct(wgv.shape, F32), jax.ShapeDtypeStruct(wgg.shape, F32), jax.ShapeDtypeStruct(pw.shape, F32)],
        in_specs=[_row_spec(tm, D), _row_spec(tm, D), _row_spec(tm, W), _row_spec(tm, W), _row_spec(tm, W)] + [_full_spec(o.shape) for o in operands[5:]],
        out_specs=[_row_spec(tm, W), _row_spec(tm, W), _row_spec(tm, D), _full_spec((1, D)), _full_spec((1, W)), _full_spec((1, W)), _full_spec((1, W)), _full_spec((1, W)),
                   _full_spec(wgv.shape), _full_spec(wgg.shape), _full_spec(pw.shape)],
        operands=operands,
        exchange=exchange,
    )


def _mix_bwd_scan(dy, dpooled, xr, xi, proj, dskip, bbre, bbim, ccre, ccim, tabs, pows, name, exchange=None):
    T, W = dy.shape
    D = 2 * W
    tm = MIX_SUBTILES * MIX_TILE
    seg = MIX_TILE // SUBLANES
    nt = T // tm
    n_slabs = tabs.shape[1]
    nch, cch, sch = bbre.shape
    spc = sch // LANES
    pg = W // len(POOL_WINDOWS)
    blocks_per_tile = tm // SUBLANES

    def body(dy_ref, dp_ref, xr_ref, xi_ref, xpr_ref, xpi_ref, us_ref, dskip_ref, bbre_ref, bbim_ref, ccre_ref, ccim_ref, tabs_ref, pows_ref,
             dproj_ref, dccre_ref, dccim_ref, dbbre_ref, dbbim_ref, dar_ref, dai_ref, lr, li, car_r, car_i, halo):
        i = pl.program_id(0)
        first = i == 0
        tile = nt - 1 - i
        row = lax.broadcasted_iota(jnp.int32, (SUBLANES, LANES), 0)

        @pl.when(first)
        def _():
            car_r[...] = jnp.zeros_like(car_r)
            car_i[...] = jnp.zeros_like(car_i)
            halo[...] = jnp.zeros_like(halo)
            dar_ref[...] = jnp.zeros_like(dar_ref)
            dai_ref[...] = jnp.zeros_like(dai_ref)

        dy = dy_ref[...]
        for c in range(nch):
            dyc = dy[:, c * cch : (c + 1) * cch]
            gr, gi = _mm_nt(dyc, ccre_ref[c]), _mm_nt(dyc, ccim_ref[c])
            for q in range(spc):
                lr[c * spc + q] = gr[:, q * LANES : (q + 1) * LANES]
                li[c * spc + q] = -gi[:, q * LANES : (q + 1) * LANES]
            _acc(dccre_ref.at[c], _mm_tn(_slabs_to_cols(xr_ref, c * spc, spc), dyc), first)
            _acc(dccim_ref.at[c], -_mm_tn(_slabs_to_cols(xi_ref, c * spc, spc), dyc), first)
        for sub in reversed(range(MIX_SUBTILES)):
            _scan_inplace(lr, li, tabs_ref, pows_ref, car_r, car_i, seg, reverse=True, base=sub * MIX_TILE)

        for sub in range(MIX_SUBTILES):
            base = sub * MIX_TILE
            for k0 in range(0, n_slabs, SLAB_GROUP):
                slabs = range(k0, min(k0 + SLAB_GROUP, n_slabs))
                init = []
                for k in slabs:
                    if sub == 0:
                        prev_r = jnp.where(tile > 0, jnp.broadcast_to(xpr_ref[k, SUBLANES - 1 : SUBLANES, :], (SUBLANES, LANES)), 0.0)
                        prev_i = jnp.where(tile > 0, jnp.broadcast_to(xpi_ref[k, SUBLANES - 1 : SUBLANES, :], (SUBLANES, LANES)), 0.0)
                    else:
                        prev_r = jnp.broadcast_to(xr_ref[k, base - 1 : base, :], (SUBLANES, LANES))
                        prev_i = jnp.broadcast_to(xi_ref[k, base - 1 : base, :], (SUBLANES, LANES))
                    x0r = jnp.where(row >= 1, pltpu.roll(xr_ref[k, pl.ds(base + seg - 1, SUBLANES, stride=seg), :], 1, 0), prev_r)
                    x0i = jnp.where(row >= 1, pltpu.roll(xi_ref[k, pl.ds(base + seg - 1, SUBLANES, stride=seg), :], 1, 0), prev_i)
                    l0r, l0i = lr[k, pl.ds(base, SUBLANES, stride=seg), :], li[k, pl.ds(base, SUBLANES, stride=seg), :]
                    init += [l0r * x0r + l0i * x0i, l0i * x0r - l0r * x0i]

                def step(r, acc, slabs=slabs, base=base):
                    out = []
                    for q, k in enumerate(slabs):
                        pr_, pi_ = xr_ref[k, pl.ds(base + r - 1, SUBLANES, stride=seg), :], xi_ref[k, pl.ds(base + r - 1, SUBLANES, stride=seg), :]
                        lr_, li_ = lr[k, pl.ds(base + r, SUBLANES, stride=seg), :], li[k, pl.ds(base + r, SUBLANES, stride=seg), :]
                        out += [acc[2 * q] + lr_ * pr_ + li_ * pi_, acc[2 * q + 1] + li_ * pr_ - lr_ * pi_]
                    return tuple(out)

                sums = lax.fori_loop(1, seg, step, tuple(init))
                for q, k in enumerate(slabs):
                    dar_ref[k] += sums[2 * q]
                    dai_ref[k] += sums[2 * q + 1]

        us = us_ref[...]
        dus = []
        for c in range(nch):
            lrc, lic = _slabs_to_cols(lr, c * spc, spc).astype(MXU_DTYPE), _slabs_to_cols(li, c * spc, spc).astype(MXU_DTYPE)
            uc = us[:, c * cch : (c + 1) * cch]
            _acc(dbbre_ref.at[c], _mm_tn(uc, lrc), first)
            _acc(dbbim_ref.at[c], _mm_tn(uc, lic), first)
            dus.append(_mm_nt(lrc, bbre_ref[c]) + _mm_nt(lic, bbim_ref[c]))
        du_s = jnp.concatenate(dus, axis=1) + dskip_ref[...] * dy

        dp = dp_ref[...]
        t1 = (tile * tm + 1 + lax.broadcasted_iota(jnp.int32, (tm, pg), 0)).astype(F32)
        dups, heads = [], []
        for g, w in enumerate(POOL_WINDOWS):
            dpg = dp[:, g * pg : (g + 1) * pg]
            qg = dpg / jnp.minimum(t1, float(w))
            ext = jnp.concatenate([qg, halo[:, g * pg : (g + 1) * pg]], axis=0)
            dups.append(_window_sum(ext, g + 1, False)[:tm, :] - dpg)
            heads.append(qg[:POOL_HALO, :])
        halo[...] = jnp.concatenate(heads, axis=1)
        dproj_ref[...] = jnp.concatenate([du_s] + dups, axis=1).astype(dproj_ref.dtype)

    rev = lambda cols: _row_spec(tm, cols, rev_n=nt)
    slab_spec = pl.BlockSpec((n_slabs, tm, LANES), lambda i: (0, nt - 1 - i, 0))
    prev_spec = pl.BlockSpec((n_slabs, SUBLANES, LANES), lambda i: (0, jnp.maximum((nt - 1 - i) * blocks_per_tile - 1, 0), 0))
    consts = (dskip, bbre, bbim, ccre, ccim, tabs, pows)
    return _pallas(
        body,
        name=name,
        grid=(nt,),
        out_shape=[jax.ShapeDtypeStruct((T, D), MXU_DTYPE), jax.ShapeDtypeStruct(ccre.shape, F32), jax.ShapeDtypeStruct(ccim.shape, F32), jax.ShapeDtypeStruct(bbre.shape, F32),
                   jax.ShapeDtypeStruct(bbim.shape, F32), jax.ShapeDtypeStruct((n_slabs, SUBLANES, LANES), F32), jax.ShapeDtypeStruct((n_slabs, SUBLANES, LANES), F32)],
        in_specs=[rev(W), rev(W), slab_spec, slab_spec, prev_spec, prev_spec, rev(W)] + [_full_spec(o.shape, single=True) for o in consts],
        out_specs=[rev(D), _full_spec(ccre.shape), _full_spec(ccim.shape), _full_spec(bbre.shape), _full_spec(bbim.shape),
                   _full_spec((n_slabs, SUBLANES, LANES)), _full_spec((n_slabs, SUBLANES, LANES))],
        scratch_shapes=[pltpu.VMEM((n_slabs, tm, LANES), F32), pltpu.VMEM((n_slabs, tm, LANES), F32), pltpu.VMEM((n_slabs, SUBLANES, LANES), F32), pltpu.VMEM((n_slabs, SUBLANES, LANES), F32),
                        pltpu.VMEM((POOL_HALO, W), F32)],
        operands=(dy, dpooled, xr, xi, xr, xi, proj, *consts),
        exchange=exchange,
    )


def _mix_bwd_in(dproj, h1, dh2, g_pre, win, name, exchange=None):
    T, D = h1.shape
    tm = _token_tile(T)

    def body(dproj_ref, h_ref, dh2_ref, gpre_ref, win_ref, dh1_ref, dg_ref):
        dx, dg = _rms_bwd(h_ref[...], gpre_ref[...], _mm_nt(dproj_ref[...], win_ref[...]))
        _acc(dg_ref, dg, pl.program_id(0) == 0)
        dh1_ref[...] = dh2_ref[...] + dx

    return _pallas(
        body,
        name=name,
        grid=(T // tm,),
        out_shape=[jax.ShapeDtypeStruct((T, D), F32), jax.ShapeDtypeStruct((1, D), F32)],
        in_specs=[_row_spec(tm, D), _row_spec(tm, D), _row_spec(tm, D), _full_spec((1, D)), _full_spec(win.shape)],
        out_specs=[_row_spec(tm, D), _full_spec((1, D))],
        operands=(dproj, h1, dh2, g_pre, win),
        exchange=exchange,
    )


def _discretize(lam_re, lam_im, log_dt, b_re, b_im):
    dt = jnp.exp(log_dt)[:, None]
    decay = jnp.exp(lam_re * dt)
    ang = lam_im * dt
    a_re, a_im = decay * jnp.cos(ang), decay * jnp.sin(ang)
    nr = a_re - 1.0
    den = lam_re * lam_re + lam_im * lam_im
    q_re = (nr * lam_re + a_im * lam_im) / den
    q_im = (a_im * lam_re - nr * lam_im) / den
    bb_re = q_re[..., None] * b_re - q_im[..., None] * b_im
    bb_im = q_re[..., None] * b_im + q_im[..., None] * b_re
    return a_re, a_im, bb_re, bb_im


STATE_CHUNK_GROUPS = 8
GLU_CHUNK_GROUPS = 16


def _block_diag(w, rows_first, n):
    G = w.shape[0]
    nch = G // n
    if not rows_first:
        w = jnp.swapaxes(w, 1, 2)
    p, q = w.shape[1], w.shape[2]
    out = jnp.einsum("cgpq,gk->cgpkq", w.reshape(nch, n, p, q), jnp.eye(n, dtype=w.dtype))
    return out.reshape(nch, n * p, n * q)


def _block_diag_extract(m, p, q, rows_first):
    nch, n = m.shape[0], m.shape[1] // p
    out = jnp.einsum("cgpkq,gk->cgpq", m.reshape(nch, n, p, n, q), jnp.eye(n, dtype=m.dtype)).reshape(nch * n, p, q)
    return out if rows_first else jnp.swapaxes(out, 1, 2)


def _cmul(ar, ai, br, bi):
    return ar * br - ai * bi, ar * bi + ai * br


def _powers(ar, ai, count):
    pr, pi = ar[None], ai[None]
    while pr.shape[0] < count:
        nr, ni = _cmul(pr, pi, pr[-1][None], pi[-1][None])
        pr, pi = jnp.concatenate([pr, nr]), jnp.concatenate([pi, ni])
    return pr[:count], pi[:count]


def _scan_tables(a_re, a_im, seg):
    n = a_re.size
    ns = n // LANES
    ar, ai = a_re.reshape(n), a_im.reshape(n)
    pr, pi = _powers(ar, ai, seg)
    jr, ji = _powers(pr[-1], pi[-1], SUBLANES)

    def bcast(v):
        return jnp.broadcast_to(v.reshape(ns, 1, LANES), (ns, SUBLANES, LANES))

    def per_sublane(vs):
        return jnp.transpose(vs.reshape(SUBLANES, ns, LANES), (1, 0, 2))

    tabs = jnp.stack([bcast(ar), bcast(ai), bcast(jr[0]), bcast(ji[0]), bcast(jr[1]), bcast(ji[1]), bcast(jr[3]), bcast(ji[3]),
                      per_sublane(jr), per_sublane(ji), per_sublane(jr[::-1]), per_sublane(ji[::-1])])

    def rows(vs):
        return jnp.broadcast_to(jnp.transpose(vs.reshape(seg, ns, 1, LANES), (1, 0, 2, 3)), (ns, seg, SUBLANES, LANES))

    return tabs, jnp.stack([rows(pr), rows(pi)])


SMALL = ("ffn1_pre_norm", "ffn1_post_norm", "mix_pre_norm", "mix_post_norm", "ssm_lambda_re", "ssm_lambda_im", "ssm_log_dt", "ssm_b_re", "ssm_b_im", "ssm_c_re", "ssm_c_im",
         "ssm_d", "ssm_w_glu", "pool_w", "pool_scale", "ssm_out_norm", "pool_out_norm", "ffn2_pre_norm", "ffn2_post_norm")
BIG = ("ffn1_w_gate", "ffn1_w_up", "ffn1_w_down", "w_in", "w_out", "ffn2_w_gate", "ffn2_w_up", "ffn2_w_down")
ORDER = ("meta_tokens", "ffn1_pre_norm", "ffn1_post_norm", "ffn1_w_gate", "ffn1_w_up", "ffn1_w_down", "mix_pre_norm", "mix_post_norm", "w_in", "ssm_lambda_re", "ssm_lambda_im",
         "ssm_log_dt", "ssm_b_re", "ssm_b_im", "ssm_c_re", "ssm_c_im", "ssm_d", "ssm_w_glu", "pool_w", "pool_scale", "ssm_out_norm", "pool_out_norm", "w_out", "ffn2_pre_norm",
         "ffn2_post_norm", "ffn2_w_gate", "ffn2_w_up", "ffn2_w_down")
PACK_ROWS = SUBLANES * 8
def _pack(arrays, rows):
    flat = jnp.concatenate([a.reshape(-1) for a in arrays])
    return jnp.pad(flat, (0, rows * LANES - flat.size)).reshape(rows, LANES)


def _unpack(packed, shapes):
    flat = packed.reshape(-1)
    out, off = [], 0
    for s in shapes:
        n = math.prod(s)
        out.append(flat[off : off + n].reshape(s))
        off += n
    return out


def _step(p, x, loss_target, m, v):
    D = x.shape[-1]
    chip = (2 * lax.axis_index("x") + lax.axis_index("y")).astype(jnp.int32)
    place = jnp.stack([chip, lax.axis_index("c").astype(jnp.int32)])

    def gather_buffer(w):
        own = w.reshape(1, 2, w.shape[0] // 2, w.shape[1])
        return lax.dynamic_update_slice(lax.empty((4,) + own.shape[1:], own.dtype), own, (chip, 0, 0, 0))

    def rows_of(n, a):
        return jnp.swapaxes(a[0], 0, 1) if n.endswith(("w_gate", "w_up")) else a[0]

    def rows_back(n, a):
        return (jnp.swapaxes(a, 0, 1) if n.endswith(("w_gate", "w_up")) else a)[None]

    def grad_view(g):
        return g.reshape(4, 2, g.shape[0] // 8, g.shape[1])

    def reduce_sum(got_sibling, views, tag, wires=None):
        wires = wires or [WIRE_DTYPE] * len(views)
        sums = [_add_own_half(v_, g_, place, f"{tag}_add_sibling_{k}", w_) for k, (v_, g_, w_) in enumerate(zip(views, got_sibling, wires))]
        return [s[0] for s in sums], [s[1] for s in sums]

    def reduce_halves(parts, got_chips, tag):
        return [_add_chips(p_, g_, place, f"{tag}_add_chips_{k}") for k, (p_, g_) in enumerate(zip(parts, got_chips))]

    bufs = {n: gather_buffer(rows_of(n, p[n]).astype(MXU_DTYPE)) for n in BIG}
    full = {}

    def gathered(names, got):
        full.update({n: g_.reshape(-1, g_.shape[-1]) for n, g_ in zip(names, got)})

    def gather_of(names, n_steps, tenths):
        sizes = [bufs[n].size * bufs[n].dtype.itemsize for n in names]
        steps = [min(n_steps - 1, max(1, -(-tenths * n_steps * sum(sizes[: a + 1]) // (10 * sum(sizes))))) for a in range(len(names))]
        return _Gather([bufs[n] for n in names], steps=steps)

    first_names = ("ffn1_w_gate", "ffn1_w_up")
    got = _exchange_call(_Gather([bufs[n] for n in first_names] + [gather_buffer(p["meta_tokens"])]), "gather_first")
    gathered(first_names, got)
    meta = jnp.transpose(got[-1].reshape(4, N_META, -1), (1, 0, 2)).reshape(N_META, D)

    vec = lambda n: p[n].reshape(1, -1)
    G, N, H = p["ssm_b_re"].shape[1:]
    a_re, a_im, bb_re, bb_im = _discretize(p["ssm_lambda_re"][0], p["ssm_lambda_im"][0], p["ssm_log_dt"][0], p["ssm_b_re"][0], p["ssm_b_im"][0])
    tabs, pows = _scan_tables(a_re, a_im, MIX_TILE // SUBLANES)
    bf = lambda a: a.astype(MXU_DTYPE)
    sg_, gg_ = STATE_CHUNK_GROUPS, GLU_CHUNK_GROUPS
    bbre, bbim = bf(_block_diag(bb_re, False, sg_)), bf(_block_diag(bb_im, False, sg_))
    ccre, ccim = bf(_block_diag(p["ssm_c_re"][0], False, sg_)), bf(_block_diag(p["ssm_c_im"][0], False, sg_))
    wgv, wgg = bf(_block_diag(p["ssm_w_glu"][0][:, :, :H], True, gg_)), bf(_block_diag(p["ssm_w_glu"][0][:, :, H:], True, gg_))
    pw = bf(p["pool_w"][0])

    T = x.shape[1] + N_META
    names = ("ffn1_w_down", "w_in", "w_out", "ffn2_w_gate", "ffn2_w_down")
    (ga1, si1, s1, n1), got = _ffn_gate_up(
        x[0], meta, vec("ffn1_pre_norm"), full["ffn1_w_gate"], full["ffn1_w_up"], "ffn1_gate_up", exchange=gather_of(names, T // FFN_TILE, 8)
    )
    gathered(names, got)
    (h1, f1), got = _ffn_down(x[0], meta, s1, vec("ffn1_post_norm"), full["ffn1_w_down"], "ffn1_down", exchange=gather_of(("ffn2_w_up",), T // _token_tile(T), 6))
    gathered(("ffn2_w_up",), got)
    (proj, xr, xim, y, pooled, mixed, h2, n2, cat), _ = _mix_fwd(
        h1, vec("mix_pre_norm"), vec("ssm_out_norm"), vec("pool_out_norm"), vec("mix_post_norm"), vec("ssm_d"), vec("pool_scale"), full["w_in"], full["w_out"],
        bbre, bbim, ccre, ccim, wgv, wgg, pw, tabs, pows, "mix_fwd",
    )
    dh3, sq, f2, ga2, si2, s2, n3 = _ffn_fwd_loss(
        h2, loss_target[0], vec("ffn2_pre_norm"), vec("ffn2_post_norm"), full["ffn2_w_gate"], full["ffn2_w_up"], full["ffn2_w_down"], "ffn2_fwd"
    )

    g, shared = {}, {}
    ffn_names = lambda tag: (tag + "_w_gate", tag + "_w_up", tag + "_w_down")

    da, db, df, dh2, g["ffn2_post_norm"], g["ffn2_pre_norm"] = _ffn_bwd(
        dh3, f2, ga2, si2, h2, vec("ffn2_post_norm"), vec("ffn2_pre_norm"), full["ffn2_w_gate"], full["ffn2_w_up"], full["ffn2_w_down"], "ffn2_bwd"
    )
    views2 = [
        grad_view(_tn_matmul(da, n3, "ffn2_dw_gate")[0]),
        grad_view(_tn_matmul(db, n3, "ffn2_dw_up")[0]),
        grad_view(_tn_matmul(s2, df, "ffn2_dw_down")[0]),
    ]
    (dy, dpooled, dmixed, g["mix_post_norm"], g["ssm_out_norm"], g["pool_out_norm"], g["ssm_d"], g["pool_scale"], dwgv, dwgg, g["pool_w"]), got = _mix_bwd_heads(
        dh2, mixed, y, pooled, proj, vec("ssm_out_norm"), vec("pool_out_norm"), vec("mix_post_norm"), vec("pool_scale"), full["w_out"], wgv, wgg, pw, "mix_bwd_heads",
        exchange=_SiblingScatter(views2),
    )
    parts2, wire2 = reduce_sum(got, views2, "ffn2")
    (dproj, dccre, dccim, dbbre, dbbim, dar, dai), got = _mix_bwd_scan(
        dy, dpooled, xr, xim, proj, vec("ssm_d"), bbre, bbim, ccre, ccim, tabs, pows, "mix_bwd_scan", exchange=_ChipScatter(wire2)
    )
    halves2 = reduce_halves(parts2, got, "ffn2")
    (dh1, g["mix_pre_norm"]), got = _mix_bwd_in(dproj, h1, dh2, vec("mix_pre_norm"), full["w_in"], "mix_bwd_in", exchange=_SiblingShare(halves2))
    shared.update(zip(ffn_names("ffn2"), got))

    da, db, df, grad_x, d_meta, g["ffn1_post_norm"], g["ffn1_pre_norm"] = _ffn_bwd_first(
        dh1, f1, ga1, si1, x[0], meta, vec("ffn1_post_norm"), vec("ffn1_pre_norm"), full["ffn1_w_gate"], full["ffn1_w_up"], full["ffn1_w_down"], "ffn1_bwd"
    )
    grad_x = grad_x[None]

    g["ssm_c_re"] = _block_diag_extract(dccre, N, H, False)
    g["ssm_c_im"] = _block_diag_extract(dccim, N, H, False)
    g["ssm_w_glu"] = jnp.concatenate([_block_diag_extract(dwgv, H, H, True), _block_diag_extract(dwgg, H, H, True)], axis=-1)
    d_a_re, d_a_im = jnp.sum(dar, axis=1).reshape(G, N), jnp.sum(dai, axis=1).reshape(G, N)
    _, pull = jax.vjp(_discretize, p["ssm_lambda_re"][0], p["ssm_lambda_im"][0], p["ssm_log_dt"][0], p["ssm_b_re"][0], p["ssm_b_im"][0])
    g["ssm_lambda_re"], g["ssm_lambda_im"], g["ssm_log_dt"], g["ssm_b_re"], g["ssm_b_im"] = pull(
        (d_a_re, d_a_im, _block_diag_extract(dbbre, H, N, False), _block_diag_extract(dbbim, H, N, False))
    )
    small_shapes = [p[n].shape for n in SMALL] + [(N_META, D), (1,)]
    small_size = sum(math.prod(s) for s in small_shapes)
    rows = -(-small_size // (LANES * PACK_ROWS)) * PACK_ROWS
    views_s = [_pack([g[n] for n in SMALL] + [d_meta, jnp.sum(sq).reshape(1)], rows).reshape(4, 2, rows // 8, LANES)]

    dw_down, got = _tn_matmul(s1, df, "ffn1_dw_down", exchange=_SiblingScatter(views_s))
    parts_s, wire_s = reduce_sum(got, views_s, "small", wires=[F32])
    views_d = [grad_view(dw_down)]
    ex = _Group([_SiblingScatter(views_d), _ChipScatter(wire_s)])
    dw_gate, got = _tn_matmul(da, n1, "ffn1_dw_gate", exchange=ex)
    got_d, got_s = ex.split(got)
    parts_d, wire_d = reduce_sum(got_d, views_d, "ffn1_down")
    halves_s = reduce_halves(parts_s, got_s, "small")
    views_g = [grad_view(dw_gate)]
    ex = _Group([_ChipScatter(wire_d), _SiblingScatter(views_g), _SiblingShare(halves_s)])
    dw_up, got = _tn_matmul(db, n1, "ffn1_dw_up", exchange=ex)
    got_d, got_g, got_s = ex.split(got)
    small_buf = lax.dynamic_update_slice(lax.empty((4,) + got_s[0].shape, F32), got_s[0][None], (chip, 0, 0, 0))
    halves_d = reduce_halves(parts_d, got_d, "ffn1_down")
    parts_g, wire_g = reduce_sum(got_g, views_g, "ffn1_gate")
    views_u = [grad_view(dw_up)]
    tn_steps = (T - N_META) // TN_TILE if (T - N_META) % TN_TILE == 0 else T // _token_tile(T)
    ex = _Group([_SiblingShare(halves_d), _ChipScatter(wire_g), _SiblingScatter(views_u), _Gather([small_buf], steps=[(3 * tn_steps) // 4])])
    dw_in, got = _tn_matmul(n2, dproj, "dw_in", exchange=ex)
    got_d, got_g, got_u, got_s = ex.split(got)
    grads = dict(zip(SMALL + ("meta_full", "sq_sum"), _unpack(got_s[0].reshape(rows, LANES), small_shapes)))
    grads["meta_tokens"] = lax.dynamic_slice_in_dim(grads.pop("meta_full"), chip * (D // 4), D // 4, axis=1)
    loss = (0.5 / D) * grads.pop("sq_sum")[0]
    shared["ffn1_w_down"] = got_d[0]
    halves_g = reduce_halves(parts_g, got_g, "ffn1_gate")
    parts_u, wire_u = reduce_sum(got_u, views_u, "ffn1_up")
    views_i = [grad_view(dw_in)]
    ex = _Group([_SiblingShare(halves_g), _ChipScatter(wire_u), _SiblingScatter(views_i)])
    dw_out, got = _tn_matmul(cat, dmixed, "dw_out", exchange=ex)
    got_g, got_u, got_i = ex.split(got)
    shared["ffn1_w_gate"] = got_g[0]
    halves_u = reduce_halves(parts_u, got_u, "ffn1_up")
    parts_i, wire_i = reduce_sum(got_i, views_i, "w_in")

    views_t = [grad_view(dw_out)]
    ex = _Group([_SiblingScatter(views_t), _SiblingShare(halves_u)])
    got_t, got_u = ex.split(_exchange_call(ex, "tail_reduce_sibling"))
    shared["ffn1_w_up"] = got_u[0]
    parts_t, wire_t = reduce_sum(got_t, views_t, "tail")
    got = _exchange_call(_ChipScatter(wire_i + wire_t), "tail_reduce_chips")
    got = _exchange_call(_SiblingShare(reduce_halves(parts_i + parts_t, got, "tail")), "tail_reduce_share")
    shared["w_in"], shared["w_out"] = got[0], got[1]
    delta, new_m, new_v = {}, {}, {}
    for n in BIG:
        g_rows = shared[n].reshape(-1, shared[n].shape[-1])
        outs = _adamw(rows_of(n, p[n]), g_rows, rows_of(n, m[n]), rows_of(n, v[n]), "adamw_" + n)
        grads[n], delta[n], new_m[n], new_v[n] = (rows_back(n, a) for a in (g_rows, *outs))
    delta["meta_tokens"], new_m["meta_tokens"], new_v["meta_tokens"] = _adamw(p["meta_tokens"], grads["meta_tokens"], m["meta_tokens"], v["meta_tokens"], "adamw_meta_tokens")

    def as_2d(n, a):
        a = a.reshape(p[n].shape)[0]
        if n in ("ssm_b_re", "ssm_b_im"):
            a = jnp.swapaxes(a, 1, 2)
        return a.reshape(-1, a.shape[-1])

    def from_2d(n, a):
        if n in ("ssm_b_re", "ssm_b_im"):
            g_, n_, h_ = p[n].shape[1:]
            return jnp.swapaxes(a.reshape(g_, h_, n_), 1, 2)[None]
        return a.reshape(p[n].shape)

    outs = _adamw_many(*[[as_2d(n, t[n]) for n in SMALL] for t in (p, grads, m, v)], "adamw_small")
    for out, arrays in zip((delta, new_m, new_v), outs):
        out.update({n: from_2d(n, a) for n, a in zip(SMALL, arrays)})

    return (loss, grad_x, *[grads[n] for n in ORDER], *[delta[n] for n in ORDER], *[new_m[n] for n in ORDER], *[new_v[n] for n in ORDER])


def kernel(x, meta_tokens, ffn1_pre_norm, ffn1_post_norm, ffn1_w_gate, ffn1_w_up, ffn1_w_down, mix_pre_norm, mix_post_norm, w_in, ssm_lambda_re, ssm_lambda_im, ssm_log_dt, ssm_b_re, ssm_b_im, ssm_c_re, ssm_c_im, ssm_d, ssm_w_glu, pool_w, pool_scale, ssm_out_norm, pool_out_norm, w_out, ffn2_pre_norm, ffn2_post_norm, ffn2_w_gate, ffn2_w_up, ffn2_w_down, loss_target, m_meta_tokens, m_ffn1_pre_norm, m_ffn1_post_norm, m_ffn1_w_gate, m_ffn1_w_up, m_ffn1_w_down, m_mix_pre_norm, m_mix_post_norm, m_w_in, m_ssm_lambda_re, m_ssm_lambda_im, m_ssm_log_dt, m_ssm_b_re, m_ssm_b_im, m_ssm_c_re, m_ssm_c_im, m_ssm_d, m_ssm_w_glu, m_pool_w, m_pool_scale, m_ssm_out_norm, m_pool_out_norm, m_w_out, m_ffn2_pre_norm, m_ffn2_post_norm, m_ffn2_w_gate, m_ffn2_w_up, m_ffn2_w_down, v_meta_tokens, v_ffn1_pre_norm, v_ffn1_post_norm, v_ffn1_w_gate, v_ffn1_w_up, v_ffn1_w_down, v_mix_pre_norm, v_mix_post_norm, v_w_in, v_ssm_lambda_re, v_ssm_lambda_im, v_ssm_log_dt, v_ssm_b_re, v_ssm_b_im, v_ssm_c_re, v_ssm_c_im, v_ssm_d, v_ssm_w_glu, v_pool_w, v_pool_scale, v_ssm_out_norm, v_pool_out_norm, v_w_out, v_ffn2_pre_norm, v_ffn2_post_norm, v_ffn2_w_gate, v_ffn2_w_up, v_ffn2_w_down):
    args = locals()
    p = {n: args[n] for n in ORDER}
    m = {n: args["m_" + n] for n in ORDER}
    v = {n: args["v_" + n] for n in ORDER}
    return _step(p, x, loss_target, m, v)
```
